```python
import math
import jax, jax.numpy as jnp
from jax import lax
import numpy as np


D_MODEL = 1024
BATCH = 32
SEQ = 2048
DEPTH = 1

ATT_GROUPS = ((128, 1), (512, 4), (2048, 16))
N_ATT_GROUPS = 3
ATT_HEADS_PER_GROUP = 4
ATT_HEAD_DIM = 128
N_ATT_HEADS = N_ATT_GROUPS * ATT_HEADS_PER_GROUP
ATT_GROUP_WIDTH = ATT_HEADS_PER_GROUP * ATT_HEAD_DIM
ATT_BLOCK = 128
ML_HEADS = 8
ML_QK_DIM = 64
ML_V_DIM = 128
ML_QK_WIDTH = ML_HEADS * ML_QK_DIM
ML_V_WIDTH = ML_HEADS * ML_V_DIM
ML_CHUNK = 64
CONV_WIDTH = 4
D_FF = 4 * D_MODEL
N_BUCKETS = 32
MAX_DISTANCE = 2048
N_BRANCHES = 2
EPS = 1e-6

IN_SPLITS = (
    N_ATT_GROUPS * ATT_GROUP_WIDTH,
    N_ATT_GROUPS * ATT_GROUP_WIDTH,
    N_ATT_GROUPS * ATT_GROUP_WIDTH,
    ML_QK_WIDTH,
    ML_QK_WIDTH,
    ML_V_WIDTH,
    ML_V_WIDTH,
    ML_HEADS,
    ML_HEADS,
    N_BRANCHES * D_MODEL,
)
D_IN = sum(IN_SPLITS)

kernel_name = 'hybrid_dilated_attn_mlstm_block'


def _rms_norm(x, gain):
    x32 = x.astype(jnp.float32)
    y = x32 * lax.rsqrt(jnp.mean(x32 * x32, axis=-1, keepdims=True) + EPS)
    return (y * gain.astype(jnp.float32)).astype(x.dtype)


def _t5_bucket(dist):
    max_exact = N_BUCKETS // 2
    d = jnp.maximum(dist, max_exact).astype(jnp.float32)
    large = max_exact + (jnp.log(d / max_exact) / math.log(MAX_DISTANCE / max_exact)
                         * (N_BUCKETS - max_exact)).astype(jnp.int32)
    large = jnp.minimum(large, N_BUCKETS - 1)
    return jnp.where(dist < max_exact, dist, large)


def _causal_conv(x, w, b):
    S = x.shape[1]
    xp = jnp.pad(x, ((0, 0), (CONV_WIDTH - 1, 0), (0, 0)))
    y = b
    for j in range(CONV_WIDTH):
        y = y + w[j] * xp[:, j:j + S]
    return y


def _dilated_attention(q, k, v, bias_table, window, dilation):
    B, S, H, dh = q.shape
    n_look = window // dilation
    n_sub = S // dilation
    nb = -(-n_sub // ATT_BLOCK)
    n_pad = nb * ATT_BLOCK

    def to_blocks(t):
        t = t.reshape(B, n_sub, dilation, H, dh).transpose(0, 2, 3, 1, 4)
        t = jnp.pad(t, ((0, 0), (0, 0), (0, 0), (0, n_pad - n_sub), (0, 0)))
        return t.astype(jnp.float32).reshape(B, dilation, H, nb, ATT_BLOCK, dh)

    def with_prev(t):
        prev = jnp.pad(t, ((0, 0), (0, 0), (0, 0), (1, 0), (0, 0), (0, 0)))[:, :, :, :-1]
        return jnp.concatenate([prev, t], axis=4)

    qb = to_blocks(q)
    kb = with_prev(to_blocks(k))
    vb = with_prev(to_blocks(v))

    i = jnp.arange(ATT_BLOCK)[:, None]
    j = jnp.arange(2 * ATT_BLOCK)[None, :]
    delta = ATT_BLOCK + i - j
    key_idx = jnp.arange(nb)[:, None, None] * ATT_BLOCK - ATT_BLOCK + j
    valid = (delta >= 0) & (delta <= n_look) & (key_idx >= 0)
    bucket = _t5_bucket(jnp.maximum(delta, 0) * dilation)
    bias = bias_table[bucket].astype(jnp.float32).transpose(2, 0, 1)

    s = jnp.einsum('brhnqd,brhnkd->brhnqk', qb, kb) * (dh ** -0.5) + bias[None, None, :, None]
    s = jnp.where(valid[None, None, None], s, -jnp.inf)
    m = jnp.max(s, axis=-1, keepdims=True)
    p = jnp.exp(s - m)
    l = jnp.sum(p, axis=-1, keepdims=True)
    o = jnp.einsum('brhnqk,brhnkd->brhnqd', p, vb) / l
    lse = (m + jnp.log(l))[..., 0]

    o = o.reshape(B, dilation, H, n_pad, dh)[:, :, :, :n_sub]
    o = o.transpose(0, 3, 1, 2, 4).reshape(B, S, H, dh)
    lse = lse.reshape(B, dilation, H, n_pad)[..., :n_sub]
    lse = lse.transpose(0, 3, 1, 2).reshape(B, S, H)
    return o, lse


def _mlstm_chunkwise(q, k, v, log_i, log_f):
    B, S, H, dk = q.shape
    dv = v.shape[-1]
    L = ML_CHUNK
    nc = S // L

    def chunks(t):
        rest = t.shape[3:]
        t = t.reshape((B, nc, L, H) + rest)
        return t.transpose((1, 0, 3, 2) + tuple(range(4, t.ndim)))

    causal = jnp.tril(jnp.ones((L, L), dtype=bool))

    def step(carry, xs):
        C, n, m = carry
        qc, kc, vc, li, lf = xs
        b = jnp.cumsum(lf, axis=-1)
        a = b + m[..., None]
        Dm = b[..., :, None] - b[..., None, :] + li[..., None, :]
        Dm = jnp.where(causal, Dm, -jnp.inf)
        m_t = jnp.maximum(a, jnp.max(Dm, axis=-1))
        w = jnp.einsum('bhtd,bhsd->bhts', qc, kc) * jnp.exp(Dm - m_t[..., None])
        inter = jnp.exp(a - m_t)
        num = jnp.einsum('bhts,bhsv->bhtv', w, vc) + inter[..., None] * jnp.einsum('bhtd,bhdv->bhtv', qc, C)
        nq = jnp.sum(w, axis=-1) + inter * jnp.einsum('bhtd,bhd->bht', qc, n)
        h = num / jnp.maximum(jnp.abs(nq), jnp.exp(-m_t))[..., None]
        b_end = b[..., -1]
        g = b_end[..., None] - b + li
        m_new = jnp.maximum(b_end + m, jnp.max(g, axis=-1))
        decay = jnp.exp(b_end + m - m_new)
        wk = jnp.exp(g - m_new[..., None])
        C = decay[..., None, None] * C + jnp.einsum('bhs,bhsd,bhsv->bhdv', wk, kc, vc)
        n = decay[..., None] * n + jnp.einsum('bhs,bhsd->bhd', wk, kc)
        return (C, n, m_new), h

    init = (jnp.zeros((B, H, dk, dv), jnp.float32),
            jnp.zeros((B, H, dk), jnp.float32),
            jnp.zeros((B, H), jnp.float32))
    _, h = lax.scan(step, init, (chunks(q), chunks(k), chunks(v), chunks(log_i), chunks(log_f)))
    return h.transpose(1, 0, 3, 2, 4).reshape(B, S, H, dv)


def _fwd_setup_inputs(seed: int = 0) -> dict:
    key = jax.random.key(seed)
    ks = jax.random.split(key, 20)
    f32 = jnp.float32

    def w(k, shape, fan_in, gain=1.0):
        return jax.random.normal(k, shape, f32) * (gain * fan_in ** -0.5)

    def gain_vec(k, shape):
        return 1.0 + 0.02 * jax.random.normal(k, shape, f32)

    gate_bias_offset = jnp.stack([jnp.zeros((ML_HEADS,), f32),
                                  jnp.linspace(3.0, 6.0, ML_HEADS, dtype=f32)])
    return {
        'x': jax.random.normal(ks[0], (BATCH, SEQ, D_MODEL), f32),
        'c': jax.random.normal(ks[1], (BATCH, D_MODEL), f32),
        'w_ada': w(ks[2], (DEPTH, D_MODEL, 6 * D_MODEL), D_MODEL, 0.5),
        'b_ada': 0.02 * jax.random.normal(ks[3], (DEPTH, 6 * D_MODEL), f32),
        'norm1_g': gain_vec(ks[4], (DEPTH, D_MODEL)),
        'norm2_g': gain_vec(ks[5], (DEPTH, D_MODEL)),
        'w_in': w(ks[6], (DEPTH, D_MODEL, D_IN), D_MODEL),
        'b_if': gate_bias_offset + 0.1 * jax.random.normal(ks[7], (DEPTH, 2, ML_HEADS), f32),
        'conv_w': w(ks[8], (DEPTH, CONV_WIDTH, 2 * ML_QK_WIDTH), CONV_WIDTH),
        'conv_b': 0.02 * jax.random.normal(ks[9], (DEPTH, 2 * ML_QK_WIDTH), f32),
        'q_norm_g': gain_vec(ks[10], (DEPTH, ATT_HEAD_DIM)),
        'k_norm_g': gain_vec(ks[11], (DEPTH, ATT_HEAD_DIM)),
        'rel_bias': 0.5 * jax.random.normal(ks[12], (N_BUCKETS, N_ATT_HEADS), f32),
        'mlstm_norm_g': gain_vec(ks[13], (DEPTH, ML_V_WIDTH)),
        'w_att_out': w(ks[14], (DEPTH, ATT_GROUP_WIDTH, D_MODEL), ATT_GROUP_WIDTH),
        'w_ml_out': w(ks[15], (DEPTH, ML_V_WIDTH, D_MODEL), ML_V_WIDTH),
        'w_out': w(ks[16], (DEPTH, D_MODEL, D_MODEL), D_MODEL),
        'w_ff1': w(ks[17], (DEPTH, D_MODEL, D_FF), D_MODEL),
        'w_ff2': w(ks[18], (DEPTH, D_FF, D_MODEL), D_FF),
    }


def _fwd_reference(x, c, w_ada, b_ada, norm1_g, norm2_g, w_in, b_if, conv_w, conv_b,
              q_norm_g, k_norm_g, rel_bias, mlstm_norm_g, w_att_out, w_ml_out,
              w_out, w_ff1, w_ff2):
    B, S, _ = x.shape
    split_points = [int(p) for p in np.cumsum(IN_SPLITS)[:-1]]
    for l in range(DEPTH):
        ada = jax.nn.silu(c) @ w_ada[l] + b_ada[l]
        shift1, scale1, gate1, shift2, scale2, gate2 = [t[:, None, :] for t in jnp.split(ada, 6, axis=-1)]

        u = _rms_norm(x, norm1_g[l]) * (1 + scale1) + shift1
        proj = u @ w_in[l]
        aq, ak, av, mq, mk, mv, mo, mi, mf, gates = jnp.split(proj, split_points, axis=-1)

        aq = _rms_norm(aq.reshape(B, S, N_ATT_HEADS, ATT_HEAD_DIM), q_norm_g[l])
        ak = _rms_norm(ak.reshape(B, S, N_ATT_HEADS, ATT_HEAD_DIM), k_norm_g[l])
        av = av.reshape(B, S, N_ATT_HEADS, ATT_HEAD_DIM)
        outs, lses = [], []
        for g, (window, dilation) in enumerate(ATT_GROUPS):
            sl = slice(g * ATT_HEADS_PER_GROUP, (g + 1) * ATT_HEADS_PER_GROUP)
            o, lse = _dilated_attention(aq[:, :, sl], ak[:, :, sl], av[:, :, sl],
                                        rel_bias[:, sl], window, dilation)
            outs.append(o)
            lses.append(lse)
        wts = jax.nn.softmax(jnp.stack(lses, axis=0), axis=0)
        att = jnp.sum(wts[..., None] * jnp.stack(outs, axis=0), axis=0)
        y_att = att.reshape(B, S, ATT_GROUP_WIDTH).astype(x.dtype) @ w_att_out[l]

        qk = jax.nn.silu(_causal_conv(jnp.concatenate([mq, mk], axis=-1), conv_w[l], conv_b[l]))
        mq, mk = jnp.split(qk.astype(jnp.float32), 2, axis=-1)
        mq = mq.reshape(B, S, ML_HEADS, ML_QK_DIM)
        mk = mk.reshape(B, S, ML_HEADS, ML_QK_DIM) * (ML_QK_DIM ** -0.5)
        mv = mv.astype(jnp.float32).reshape(B, S, ML_HEADS, ML_V_DIM)
        log_i = mi.astype(jnp.float32) + b_if[l, 0].astype(jnp.float32)
        log_f = jax.nn.log_sigmoid(mf.astype(jnp.float32) + b_if[l, 1].astype(jnp.float32))
        h = _mlstm_chunkwise(mq, mk, mv, log_i, log_f)
        h = _rms_norm(h, mlstm_norm_g[l].reshape(ML_HEADS, ML_V_DIM)).reshape(B, S, ML_V_WIDTH)
        h = h * jax.nn.sigmoid(mo.astype(jnp.float32))
        y_ml = h.astype(x.dtype) @ w_ml_out[l]

        g_att, g_ml = jnp.split(jax.nn.sigmoid(gates), N_BRANCHES, axis=-1)
        y = (g_att * y_att + g_ml * y_ml) @ w_out[l]
        x = x + gate1 * y

        u2 = _rms_norm(x, norm2_g[l]) * (1 + scale2) + shift2
        hdn = jnp.square(jax.nn.relu(u2 @ w_ff1[l]))
        x = x + gate2 * (hdn @ w_ff2[l])
    return x


import jax as _jax
import jax.numpy as _jnp

TWIN_FORMAT = 'train_step'
FWD_PARAMS = ['x', 'c', 'w_ada', 'b_ada', 'norm1_g', 'norm2_g', 'w_in', 'b_if', 'conv_w', 'conv_b', 'q_norm_g', 'k_norm_g', 'rel_bias', 'mlstm_norm_g', 'w_att_out', 'w_ml_out', 'w_out', 'w_ff1', 'w_ff2']
TWIN_WEIGHTS = ['w_ada', 'b_ada', 'norm1_g', 'norm2_g', 'w_in', 'b_if', 'conv_w', 'conv_b', 'q_norm_g', 'k_norm_g', 'rel_bias', 'mlstm_norm_g', 'w_att_out', 'w_ml_out', 'w_out', 'w_ff1', 'w_ff2']
TWIN_DIFF_INPUT = 'x'
TWIN_INPUTS = ['x', 'c', 'w_ada', 'b_ada', 'norm1_g', 'norm2_g', 'w_in', 'b_if', 'conv_w', 'conv_b', 'q_norm_g', 'k_norm_g', 'rel_bias', 'mlstm_norm_g', 'w_att_out', 'w_ml_out', 'w_out', 'w_ff1', 'w_ff2', 'loss_target', 'm_w_ada', 'm_b_ada', 'm_norm1_g', 'm_norm2_g', 'm_w_in', 'm_b_if', 'm_conv_w', 'm_conv_b', 'm_q_norm_g', 'm_k_norm_g', 'm_rel_bias', 'm_mlstm_norm_g', 'm_w_att_out', 'm_w_ml_out', 'm_w_out', 'm_w_ff1', 'm_w_ff2', 'v_w_ada', 'v_b_ada', 'v_norm1_g', 'v_norm2_g', 'v_w_in', 'v_b_if', 'v_conv_w', 'v_conv_b', 'v_q_norm_g', 'v_k_norm_g', 'v_rel_bias', 'v_mlstm_norm_g', 'v_w_att_out', 'v_w_ml_out', 'v_w_out', 'v_w_ff1', 'v_w_ff2']
TWIN_OUTPUTS = ['loss', 'grad_x', 'grad_w_ada', 'grad_b_ada', 'grad_norm1_g', 'grad_norm2_g', 'grad_w_in', 'grad_b_if', 'grad_conv_w', 'grad_conv_b', 'grad_q_norm_g', 'grad_k_norm_g', 'grad_rel_bias', 'grad_mlstm_norm_g', 'grad_w_att_out', 'grad_w_ml_out', 'grad_w_out', 'grad_w_ff1', 'grad_w_ff2', 'delta_w_ada', 'delta_b_ada', 'delta_norm1_g', 'delta_norm2_g', 'delta_w_in', 'delta_b_if', 'delta_conv_w', 'delta_conv_b', 'delta_q_norm_g', 'delta_k_norm_g', 'delta_rel_bias', 'delta_mlstm_norm_g', 'delta_w_att_out', 'delta_w_ml_out', 'delta_w_out', 'delta_w_ff1', 'delta_w_ff2', 'new_m_w_ada', 'new_m_b_ada', 'new_m_norm1_g', 'new_m_norm2_g', 'new_m_w_in', 'new_m_b_if', 'new_m_conv_w', 'new_m_conv_b', 'new_m_q_norm_g', 'new_m_k_norm_g', 'new_m_rel_bias', 'new_m_mlstm_norm_g', 'new_m_w_att_out', 'new_m_w_ml_out', 'new_m_w_out', 'new_m_w_ff1', 'new_m_w_ff2', 'new_v_w_ada', 'new_v_b_ada', 'new_v_norm1_g', 'new_v_norm2_g', 'new_v_w_in', 'new_v_b_if', 'new_v_conv_w', 'new_v_conv_b', 'new_v_q_norm_g', 'new_v_k_norm_g', 'new_v_rel_bias', 'new_v_mlstm_norm_g', 'new_v_w_att_out', 'new_v_w_ml_out', 'new_v_w_out', 'new_v_w_ff1', 'new_v_w_ff2']
TWIN_LEAF_KINDS = {'loss': 'loss', 'grad_x': 'grad_x', 'grad_w_ada': 'grad_w', 'grad_b_ada': 'grad_w', 'grad_norm1_g': 'grad_w', 'grad_norm2_g': 'grad_w', 'grad_w_in': 'grad_w', 'grad_b_if': 'grad_w', 'grad_conv_w': 'grad_w', 'grad_conv_b': 'grad_w', 'grad_q_norm_g': 'grad_w', 'grad_k_norm_g': 'grad_w', 'grad_rel_bias': 'grad_w', 'grad_mlstm_norm_g': 'grad_w', 'grad_w_att_out': 'grad_w', 'grad_w_ml_out': 'grad_w', 'grad_w_out': 'grad_w', 'grad_w_ff1': 'grad_w', 'grad_w_ff2': 'grad_w', 'delta_w_ada': 'delta_w', 'delta_b_ada': 'delta_w', 'delta_norm1_g': 'delta_w', 'delta_norm2_g': 'delta_w', 'delta_w_in': 'delta_w', 'delta_b_if': 'delta_w', 'delta_conv_w': 'delta_w', 'delta_conv_b': 'delta_w', 'delta_q_norm_g': 'delta_w', 'delta_k_norm_g': 'delta_w', 'delta_rel_bias': 'delta_w', 'delta_mlstm_norm_g': 'delta_w', 'delta_w_att_out': 'delta_w', 'delta_w_ml_out': 'delta_w', 'delta_w_out': 'delta_w', 'delta_w_ff1': 'delta_w', 'delta_w_ff2': 'delta_w', 'new_m_w_ada': 'new_m', 'new_m_b_ada': 'new_m', 'new_m_norm1_g': 'new_m', 'new_m_norm2_g': 'new_m', 'new_m_w_in': 'new_m', 'new_m_b_if': 'new_m', 'new_m_conv_w': 'new_m', 'new_m_conv_b': 'new_m', 'new_m_q_norm_g': 'new_m', 'new_m_k_norm_g': 'new_m', 'new_m_rel_bias': 'new_m', 'new_m_mlstm_norm_g': 'new_m', 'new_m_w_att_out': 'new_m', 'new_m_w_ml_out': 'new_m', 'new_m_w_out': 'new_m', 'new_m_w_ff1': 'new_m', 'new_m_w_ff2': 'new_m', 'new_v_w_ada': 'new_v', 'new_v_b_ada': 'new_v', 'new_v_norm1_g': 'new_v', 'new_v_norm2_g': 'new_v', 'new_v_w_in': 'new_v', 'new_v_b_if': 'new_v', 'new_v_conv_w': 'new_v', 'new_v_conv_b': 'new_v', 'new_v_q_norm_g': 'new_v', 'new_v_k_norm_g': 'new_v', 'new_v_rel_bias': 'new_v', 'new_v_mlstm_norm_g': 'new_v', 'new_v_w_att_out': 'new_v', 'new_v_w_ml_out': 'new_v', 'new_v_w_out': 'new_v', 'new_v_w_ff1': 'new_v', 'new_v_w_ff2': 'new_v'}


def _forward(args):
    return _fwd_reference(*[args[k] for k in FWD_PARAMS])


def _output_shape():
    out = _jax.eval_shape(lambda: _forward(_fwd_setup_inputs(0)))
    return out.shape, out.dtype

N_MICROBATCH = 1
ADAM_LR = 0.001
ADAM_B1 = 0.9
ADAM_B2 = 0.999
ADAM_EPS = 1e-08
ADAM_WD = 0.01
ADAM_STEP = 10
PER_EXAMPLE_BATCH_AXIS = {'x': 0, 'c': 0, 'loss_target': 0}
SHARED_INPUTS = []
_WEIGHT_DTYPES = {'w_ada': _jnp.float32, 'b_ada': _jnp.float32, 'norm1_g': _jnp.float32, 'norm2_g': _jnp.float32, 'w_in': _jnp.float32, 'b_if': _jnp.float32, 'conv_w': _jnp.float32, 'conv_b': _jnp.float32, 'q_norm_g': _jnp.float32, 'k_norm_g': _jnp.float32, 'rel_bias': _jnp.float32, 'mlstm_norm_g': _jnp.float32, 'w_att_out': _jnp.float32, 'w_ml_out': _jnp.float32, 'w_out': _jnp.float32, 'w_ff1': _jnp.float32, 'w_ff2': _jnp.float32}
MOMENT_SCALE = {'w_ada': 6.540666e+00, 'b_ada': 1.423702e+01, 'norm1_g': 1.160616e-01, 'norm2_g': 2.464625e+01, 'w_in': 1.525581e-01, 'b_if': 1.769081e-01, 'conv_w': 4.384865e-02, 'conv_b': 6.010173e-02, 'q_norm_g': 9.046676e-02, 'k_norm_g': 9.064305e-02, 'rel_bias': 9.764624e-02, 'mlstm_norm_g': 7.857004e-01, 'w_att_out': 1.828139e-01, 'w_ml_out': 5.231850e-01, 'w_out': 5.109898e-01, 'w_ff1': 6.495228e-01, 'w_ff2': 2.637013e+00}


def _to_microbatches(a, axis):
    t = _jnp.moveaxis(a, axis, 0)
    t = t.reshape((N_MICROBATCH, t.shape[0] // N_MICROBATCH) + t.shape[1:])
    return _jnp.moveaxis(t, 1, axis + 1)


def setup_inputs(seed: int = 0) -> dict:
    inp = _fwd_setup_inputs(seed)
    key = _jax.random.fold_in(_jax.random.key(seed), 7919)
    shape, _ = _output_shape()
    out = dict(inp)
    out["loss_target"] = _jax.random.normal(_jax.random.fold_in(key, 0), shape, _jnp.float32)
    for i, name in enumerate(TWIN_WEIGHTS):
        w = inp[name].astype(_jnp.float32)
        if MOMENT_SCALE is None:
            s = _jnp.sqrt(_jnp.mean(_jnp.square(w)) + 1e-30)
        else:
            s = MOMENT_SCALE[name]
        km, kv = _jax.random.split(_jax.random.fold_in(key, i + 1))
        out[name] = w
        out["m_" + name] = s * _jax.random.normal(km, w.shape, _jnp.float32)
        out["v_" + name] = (s * s) * _jax.random.uniform(kv, w.shape, _jnp.float32, 0.5, 1.5)
    if N_MICROBATCH > 1:
        for name, axis in PER_EXAMPLE_BATCH_AXIS.items():
            out[name] = _to_microbatches(out[name], axis)
    return {'x': out['x'], 'c': out['c'], 'w_ada': out['w_ada'], 'b_ada': out['b_ada'], 'norm1_g': out['norm1_g'], 'norm2_g': out['norm2_g'], 'w_in': out['w_in'], 'b_if': out['b_if'], 'conv_w': out['conv_w'], 'conv_b': out['conv_b'], 'q_norm_g': out['q_norm_g'], 'k_norm_g': out['k_norm_g'], 'rel_bias': out['rel_bias'], 'mlstm_norm_g': out['mlstm_norm_g'], 'w_att_out': out['w_att_out'], 'w_ml_out': out['w_ml_out'], 'w_out': out['w_out'], 'w_ff1': out['w_ff1'], 'w_ff2': out['w_ff2'], 'loss_target': out['loss_target'], 'm_w_ada': out['m_w_ada'], 'm_b_ada': out['m_b_ada'], 'm_norm1_g': out['m_norm1_g'], 'm_norm2_g': out['m_norm2_g'], 'm_w_in': out['m_w_in'], 'm_b_if': out['m_b_if'], 'm_conv_w': out['m_conv_w'], 'm_conv_b': out['m_conv_b'], 'm_q_norm_g': out['m_q_norm_g'], 'm_k_norm_g': out['m_k_norm_g'], 'm_rel_bias': out['m_rel_bias'], 'm_mlstm_norm_g': out['m_mlstm_norm_g'], 'm_w_att_out': out['m_w_att_out'], 'm_w_ml_out': out['m_w_ml_out'], 'm_w_out': out['m_w_out'], 'm_w_ff1': out['m_w_ff1'], 'm_w_ff2': out['m_w_ff2'], 'v_w_ada': out['v_w_ada'], 'v_b_ada': out['v_b_ada'], 'v_norm1_g': out['v_norm1_g'], 'v_norm2_g': out['v_norm2_g'], 'v_w_in': out['v_w_in'], 'v_b_if': out['v_b_if'], 'v_conv_w': out['v_conv_w'], 'v_conv_b': out['v_conv_b'], 'v_q_norm_g': out['v_q_norm_g'], 'v_k_norm_g': out['v_k_norm_g'], 'v_rel_bias': out['v_rel_bias'], 'v_mlstm_norm_g': out['v_mlstm_norm_g'], 'v_w_att_out': out['v_w_att_out'], 'v_w_ml_out': out['v_w_ml_out'], 'v_w_out': out['v_w_out'], 'v_w_ff1': out['v_w_ff1'], 'v_w_ff2': out['v_w_ff2']}


def _loss(weights, diff, rest, loss_target):
    with _jax.named_scope("forward"):
        args = {**rest, TWIN_DIFF_INPUT: diff, **{k: w.astype(_WEIGHT_DTYPES[k]) for k, w in weights.items()}}
        y = _forward(args)
    with _jax.named_scope("loss_head"):
        err = _jnp.square(y.astype(_jnp.float32) - loss_target)
        return 0.5 * _jnp.sum(_jnp.mean(err, axis=-1)) if err.ndim else 0.5 * err


def _adamw(w, g, m, v):
    m = ADAM_B1 * m + (1.0 - ADAM_B1) * g
    v = ADAM_B2 * v + (1.0 - ADAM_B2) * _jnp.square(g)
    m_hat = m / (1.0 - ADAM_B1 ** ADAM_STEP)
    v_hat = v / (1.0 - ADAM_B2 ** ADAM_STEP)
    delta = -ADAM_LR * (m_hat / (_jnp.sqrt(v_hat) + ADAM_EPS) + ADAM_WD * w)
    return delta, m, v


def reference(x, c, w_ada, b_ada, norm1_g, norm2_g, w_in, b_if, conv_w, conv_b, q_norm_g, k_norm_g, rel_bias, mlstm_norm_g, w_att_out, w_ml_out, w_out, w_ff1, w_ff2, loss_target, m_w_ada, m_b_ada, m_norm1_g, m_norm2_g, m_w_in, m_b_if, m_conv_w, m_conv_b, m_q_norm_g, m_k_norm_g, m_rel_bias, m_mlstm_norm_g, m_w_att_out, m_w_ml_out, m_w_out, m_w_ff1, m_w_ff2, v_w_ada, v_b_ada, v_norm1_g, v_norm2_g, v_w_in, v_b_if, v_conv_w, v_conv_b, v_q_norm_g, v_k_norm_g, v_rel_bias, v_mlstm_norm_g, v_w_att_out, v_w_ml_out, v_w_out, v_w_ff1, v_w_ff2):
    given = dict(x=x, c=c, w_ada=w_ada, b_ada=b_ada, norm1_g=norm1_g, norm2_g=norm2_g, w_in=w_in, b_if=b_if, conv_w=conv_w, conv_b=conv_b, q_norm_g=q_norm_g, k_norm_g=k_norm_g, rel_bias=rel_bias, mlstm_norm_g=mlstm_norm_g, w_att_out=w_att_out, w_ml_out=w_ml_out, w_out=w_out, w_ff1=w_ff1, w_ff2=w_ff2, loss_target=loss_target, m_w_ada=m_w_ada, m_b_ada=m_b_ada, m_norm1_g=m_norm1_g, m_norm2_g=m_norm2_g, m_w_in=m_w_in, m_b_if=m_b_if, m_conv_w=m_conv_w, m_conv_b=m_conv_b, m_q_norm_g=m_q_norm_g, m_k_norm_g=m_k_norm_g, m_rel_bias=m_rel_bias, m_mlstm_norm_g=m_mlstm_norm_g, m_w_att_out=m_w_att_out, m_w_ml_out=m_w_ml_out, m_w_out=m_w_out, m_w_ff1=m_w_ff1, m_w_ff2=m_w_ff2, v_w_ada=v_w_ada, v_b_ada=v_b_ada, v_norm1_g=v_norm1_g, v_norm2_g=v_norm2_g, v_w_in=v_w_in, v_b_if=v_b_if, v_conv_w=v_conv_w, v_conv_b=v_conv_b, v_q_norm_g=v_q_norm_g, v_k_norm_g=v_k_norm_g, v_rel_bias=v_rel_bias, v_mlstm_norm_g=v_mlstm_norm_g, v_w_att_out=v_w_att_out, v_w_ml_out=v_w_ml_out, v_w_out=v_w_out, v_w_ff1=v_w_ff1, v_w_ff2=v_w_ff2)
    weights = {n: given[n] for n in TWIN_WEIGHTS}
    shared = {n: given[n] for n in SHARED_INPUTS}
    per_example = {n: given[n] for n in ['x', 'c']}
    grad_fn = _jax.value_and_grad(_loss, argnums=(0, 1))

    def one_microbatch(ex, loss_target):
        ex = dict(ex)
        diff = ex.pop(TWIN_DIFF_INPUT)
        return grad_fn(weights, diff, {**shared, **ex}, loss_target)

    if N_MICROBATCH == 1:
        loss, (grad_w, grad_x) = one_microbatch(per_example, given["loss_target"])
    else:
        def body(carry, xs):
            loss_sum, grad_sum = carry
            l_k, (gw_k, gx_k) = one_microbatch(xs[0], xs[1])
            with _jax.named_scope("update"):
                return (loss_sum + l_k, _jax.tree.map(_jnp.add, grad_sum, gw_k)), gx_k

        init = (_jnp.zeros((), _jnp.float32), _jax.tree.map(_jnp.zeros_like, weights))
        (loss, grad_w), grad_x = _jax.lax.scan(body, init, (per_example, given["loss_target"]))
    with _jax.named_scope("update"):
        delta_w, new_m, new_v = {}, {}, {}
        for n in TWIN_WEIGHTS:
            delta_w[n], new_m[n], new_v[n] = _adamw(weights[n], grad_w[n], given["m_" + n], given["v_" + n])
    return (loss, grad_x, *[grad_w[n] for n in TWIN_WEIGHTS], *[delta_w[n] for n in TWIN_WEIGHTS],
            *[new_m[n] for n in TWIN_WEIGHTS], *[new_v[n] for n in TWIN_WEIGHTS])
```

```python
import functools
import math

import numpy as np
import jax
import jax.numpy as jnp
from jax import lax
from jax.experimental import pallas as pl
from jax.experimental.pallas import tpu as pltpu

F32 = jnp.float32
BF16 = jnp.bfloat16

N_DEV = 8
D_MODEL = 1024
SEQ = 2048
ATT_GROUPS = ((128, 1), (512, 4), (2048, 16))
N_ATT_HEADS = 12
ATT_BLOCK = 128
HEAD_DIM = 128
ML_HEADS = 8
ML_PAIRS = 4
ML_CHUNK = 64
N_CHUNKS = SEQ // ML_CHUNK
N_BUCKETS = 32
MAX_DISTANCE = 2048
D_FF = 4096
D_IN = 9744
EPS = 1e-6

ADAM_LR = 0.001
ADAM_B1 = 0.9
ADAM_B2 = 0.999
ADAM_EPS = 1e-08
ADAM_WD = 0.01
ADAM_STEP = 10

ATT_HEAD_COLS = 3 * HEAD_DIM
ATT_COLS = N_ATT_HEADS * ATT_HEAD_COLS
ML_PAIR_COLS = 896
ML_COLS = ML_PAIRS * ML_PAIR_COLS
GATE_COLS = 2 * D_MODEL
W_IN_SHARD = D_IN // N_DEV

VMEM_LIMIT = 56 * 1024 * 1024


def _cparams(**kw):
    return pltpu.CompilerParams(vmem_limit_bytes=VMEM_LIMIT, **kw)


_NN = ((1,), (0,))
_NT = ((1,), (1,))
_TN = ((0,), (0,))


def _mxu(a, b, dims):
    return lax.dot_general(a.astype(BF16), b.astype(BF16), (dims, ((), ())), preferred_element_type=F32)


@jax.custom_vjp
def bdot_nn(a, b):
    return _mxu(a, b, _NN)


def _nn_fwd(a, b):
    return _mxu(a, b, _NN), (a, b)


def _nn_bwd(res, g):
    a, b = res
    return _mxu(g, b, _NT), _mxu(a, g, _TN)


bdot_nn.defvjp(_nn_fwd, _nn_bwd)


@jax.custom_vjp
def bdot_nt(a, b):
    return _mxu(a, b, _NT)


def _nt_fwd(a, b):
    return _mxu(a, b, _NT), (a, b)


def _nt_bwd(res, g):
    a, b = res
    return _mxu(g, b, _NN), _mxu(g, a, _TN)


bdot_nt.defvjp(_nt_fwd, _nt_bwd)


@jax.custom_vjp
def bdot_tn(a, b):
    return _mxu(a, b, _TN)


def _tn_fwd(a, b):
    return _mxu(a, b, _TN), (a, b)


def _tn_bwd(res, g):
    a, b = res
    return _mxu(b, g, _NT), _mxu(a, g, _NN)


bdot_tn.defvjp(_tn_fwd, _tn_bwd)


def _doth(a, b, dims=_NN):
    return lax.dot_general(a, b, (dims, ((), ())), precision=lax.Precision.HIGHEST, preferred_element_type=F32)


def _rms(x):
    return x * lax.rsqrt(jnp.mean(x * x, axis=-1, keepdims=True) + EPS)


def _pick(n, cands):
    for t in cands:
        if n % t == 0:
            return t
    raise ValueError(f"no tile for {n}")


def matmul(a, b, *, mode, name, out_dtypes=(F32,), epi=None, extras=()):
    if mode == "nn":
        (M, K), (K2, N) = a.shape, b.shape
    elif mode == "nt":
        (M, K), (N, K2) = a.shape, b.shape
    else:
        (K, M), (K2, N) = a.shape, b.shape
    assert K == K2, (a.shape, b.shape, mode)
    tm = _pick(M, (512, 256, 128, 64, 32, 16, 8))
    tn = _pick(N, (1024, 768, 512, 256, 128))
    tk = _pick(K, (512, 256, 128, 64, 32))
    nk = K // tk
    n_ex = len(extras)
    n_out = len(out_dtypes)
    dims = {"nn": _NN, "nt": _NT, "tn": _TN}[mode]

    def body(*refs):
        a_ref, b_ref = refs[0], refs[1]
        ex_refs = refs[2:2 + n_ex]
        out_refs = refs[2 + n_ex:2 + n_ex + n_out]
        acc = refs[2 + n_ex + n_out]
        k = pl.program_id(2)

        @pl.when(k == 0)
        def _():
            acc[...] = jnp.zeros_like(acc)

        acc[...] += _mxu(a_ref[...], b_ref[...], dims)

        @pl.when(k == nk - 1)
        def _():
            r = acc[...]
            outs = epi(r, *[e[...] for e in ex_refs]) if epi is not None else (r,)
            for o_ref, o in zip(out_refs, outs):
                o_ref[...] = o.astype(o_ref.dtype)

    if mode == "nn":
        a_spec = pl.BlockSpec((tm, tk), lambda i, j, k: (i, k))
        b_spec = pl.BlockSpec((tk, tn), lambda i, j, k: (k, j))
    elif mode == "nt":
        a_spec = pl.BlockSpec((tm, tk), lambda i, j, k: (i, k))
        b_spec = pl.BlockSpec((tn, tk), lambda i, j, k: (j, k))
    else:
        a_spec = pl.BlockSpec((tk, tm), lambda i, j, k: (k, i))
        b_spec = pl.BlockSpec((tk, tn), lambda i, j, k: (k, j))
    o_spec = pl.BlockSpec((tm, tn), lambda i, j, k: (i, j))
    res = pl.pallas_call(
        body,
        name=name,
        grid=(M // tm, N // tn, nk),
        in_specs=[a_spec, b_spec] + [o_spec] * n_ex,
        out_specs=[o_spec] * n_out,
        out_shape=[jax.ShapeDtypeStruct((M, N), dt) for dt in out_dtypes],
        scratch_shapes=[pltpu.VMEM((tm, tn), F32)],
        compiler_params=_cparams(),
    )(a, b, *extras)
    return res[0] if n_out == 1 else tuple(res)


def small_call(fn, inputs, out_shapes, name):
    n_in = len(inputs)

    def body(*refs):
        outs = fn(*[r[...] for r in refs[:n_in]])
        for o_ref, o in zip(refs[n_in:], outs):
            o_ref[...] = o.astype(o_ref.dtype)

    res = pl.pallas_call(body, name=name, out_shape=list(out_shapes), compiler_params=_cparams())(*inputs)
    return tuple(res)


ROW_TILE = 512


def _modnorm(x, g, scale, shift):
    return _rms(x) * g * (1.0 + scale) + shift


def _row_spec(width):
    return pl.BlockSpec((1, ROW_TILE, width), lambda b, i: (b, i, 0))


def _mod_spec():
    return pl.BlockSpec((1, 1, D_MODEL), lambda b, i: (b, 0, 0))


def _vec_spec():
    return pl.BlockSpec((1, D_MODEL), lambda b, i: (0, 0))


def modnorm_fwd(x, g, scale, shift, name):
    B, S, D = x.shape

    def body(x_ref, g_ref, sc_ref, sh_ref, u_ref):
        u_ref[0] = _modnorm(x_ref[0], g_ref[...], sc_ref[0], sh_ref[0]).astype(BF16)

    return pl.pallas_call(
        body, name=name, grid=(B, S // ROW_TILE),
        in_specs=[_row_spec(D), _vec_spec(), _mod_spec(), _mod_spec()],
        out_specs=_row_spec(D),
        out_shape=jax.ShapeDtypeStruct((B, S, D), BF16),
        compiler_params=_cparams(),
    )(x, g, scale, shift)


def resid_modnorm_fwd(x, y, gate, g, scale, shift, name):
    B, S, D = x.shape

    def body(x_ref, y_ref, gt_ref, g_ref, sc_ref, sh_ref, x1_ref, u_ref):
        x1 = x_ref[0] + gt_ref[0] * y_ref[0]
        x1_ref[0] = x1
        u_ref[0] = _modnorm(x1, g_ref[...], sc_ref[0], sh_ref[0]).astype(BF16)

    return pl.pallas_call(
        body, name=name, grid=(B, S // ROW_TILE),
        in_specs=[_row_spec(D), _row_spec(D), _mod_spec(), _vec_spec(), _mod_spec(), _mod_spec()],
        out_specs=[_row_spec(D), _row_spec(D)],
        out_shape=[jax.ShapeDtypeStruct((B, S, D), F32), jax.ShapeDtypeStruct((B, S, D), BF16)],
        compiler_params=_cparams(),
    )(x, y, gate, g, scale, shift)


def resid_loss(x1, ffo, gate, target, name):
    B, S, D = x1.shape

    def body(x_ref, f_ref, gt_ref, t_ref, dx_ref, loss_ref):
        first = jnp.logical_and(pl.program_id(0) == 0, pl.program_id(1) == 0)

        @pl.when(first)
        def _():
            loss_ref[...] = jnp.zeros_like(loss_ref)

        err = x_ref[0] + gt_ref[0] * f_ref[0] - t_ref[0]
        dx_ref[0] = err * (1.0 / D)
        loss_ref[...] += 0.5 * jnp.sum(jnp.mean(err * err, axis=-1, keepdims=True), axis=0, keepdims=True)

    return pl.pallas_call(
        body, name=name, grid=(B, S // ROW_TILE),
        in_specs=[_row_spec(D), _row_spec(D), _mod_spec(), _row_spec(D)],
        out_specs=[_row_spec(D), pl.BlockSpec((1, 1), lambda b, i: (0, 0))],
        out_shape=[jax.ShapeDtypeStruct((B, S, D), F32), jax.ShapeDtypeStruct((1, 1), F32)],
        compiler_params=_cparams(),
    )(x1, ffo, gate, target)


def resid_bwd(dx, y, gate, name):
    B, S, D = dx.shape

    def body(dx_ref, y_ref, gt_ref, dy_ref, dg_ref):
        @pl.when(pl.program_id(1) == 0)
        def _():
            dg_ref[...] = jnp.zeros_like(dg_ref)

        d = dx_ref[0]
        dy_ref[0] = (gt_ref[0] * d).astype(BF16)
        dg_ref[0] += jnp.sum(d * y_ref[0], axis=0, keepdims=True)

    return pl.pallas_call(
        body, name=name, grid=(B, S // ROW_TILE),
        in_specs=[_row_spec(D), _row_spec(D), _mod_spec()],
        out_specs=[_row_spec(D), _mod_spec()],
        out_shape=[jax.ShapeDtypeStruct((B, S, D), BF16), jax.ShapeDtypeStruct((B, 1, D), F32)],
        compiler_params=_cparams(),
    )(dx, y, gate)


def modnorm_bwd(x, g, scale, shift, du, dx_res, name):
    B, S, D = x.shape

    def body(x_ref, g_ref, sc_ref, sh_ref, du_ref, dr_ref, dx_ref, dg_ref, dsc_ref, dsh_ref):
        first = jnp.logical_and(pl.program_id(0) == 0, pl.program_id(1) == 0)

        @pl.when(first)
        def _():
            dg_ref[...] = jnp.zeros_like(dg_ref)

        @pl.when(pl.program_id(1) == 0)
        def _():
            dsc_ref[...] = jnp.zeros_like(dsc_ref)
            dsh_ref[...] = jnp.zeros_like(dsh_ref)

        _, vjp = jax.vjp(_modnorm, x_ref[0], g_ref[...], sc_ref[0], sh_ref[0])
        dx, dg, dsc, dsh = vjp(du_ref[0].astype(F32))
        dx_ref[0] = dx + dr_ref[0]
        dg_ref[...] += dg
        dsc_ref[0] += dsc
        dsh_ref[0] += dsh

    return pl.pallas_call(
        body, name=name, grid=(B, S // ROW_TILE),
        in_specs=[_row_spec(D), _vec_spec(), _mod_spec(), _mod_spec(), _row_spec(D), _row_spec(D)],
        out_specs=[_row_spec(D), _vec_spec(), _mod_spec(), _mod_spec()],
        out_shape=[jax.ShapeDtypeStruct((B, S, D), F32), jax.ShapeDtypeStruct((1, D), F32),
                   jax.ShapeDtypeStruct((B, 1, D), F32), jax.ShapeDtypeStruct((B, 1, D), F32)],
        compiler_params=_cparams(),
    )(x, g, scale, shift, du, dx_res)


def _bucket_table(dilation):
    i = np.arange(ATT_BLOCK)[:, None]
    j = np.arange(2 * ATT_BLOCK)[None, :]
    delta = ATT_BLOCK + i - j
    dist = np.maximum(delta, 0) * dilation
    max_exact = N_BUCKETS // 2
    d = np.maximum(dist, max_exact).astype(np.float32)
    large = max_exact + (np.log(d / np.float32(max_exact)) / np.float32(math.log(MAX_DISTANCE / max_exact))
                         * np.float32(N_BUCKETS - max_exact)).astype(np.int32)
    large = np.minimum(large, N_BUCKETS - 1)
    return np.where(dist < max_exact, dist, large).astype(np.int32)


def _bucket_onehot(dilation):
    bt = jnp.asarray(_bucket_table(dilation).reshape(1, -1))
    return (bt == jnp.arange(N_BUCKETS, dtype=jnp.int32)[:, None]).astype(F32)


def bias_expand(rel_t, onehot, name):
    def fn(r, oh):
        return (_doth(r, oh),)
    return small_call(fn, [rel_t, onehot], [jax.ShapeDtypeStruct((rel_t.shape[0], onehot.shape[1]), F32)], name)[0]


def bias_reduce(dbias_flat, onehot, name):
    def fn(db, oh):
        return (_doth(db, oh, _NT),)
    return small_call(fn, [dbias_flat, onehot], [jax.ShapeDtypeStruct((dbias_flat.shape[0], N_BUCKETS), F32)], name)[0]


def _attn_tile(q, k, v, bias, qg, kg):
    qn = _rms(q) * qg
    kn = _rms(k) * kg
    s = bdot_nt(qn, kn) * (HEAD_DIM ** -0.5) + bias
    kl = k.shape[0]
    i = lax.broadcasted_iota(jnp.int32, (ATT_BLOCK, kl), 0)
    j = lax.broadcasted_iota(jnp.int32, (ATT_BLOCK, kl), 1) + (2 * ATT_BLOCK - kl)
    valid = jnp.logical_and(j >= i, j <= i + ATT_BLOCK)
    s = jnp.where(valid, s, -jnp.inf)
    m = jnp.max(s, axis=-1, keepdims=True)
    p = jnp.exp(s - m)
    l = jnp.sum(p, axis=-1, keepdims=True)
    o = bdot_nn(p, v) / l
    lse = jnp.broadcast_to(m + jnp.log(l), (ATT_BLOCK, HEAD_DIM))
    return o, lse


def _attn_tiles(dilation):
    nb = SEQ // dilation // ATT_BLOCK
    return [(r, n) for r in range(dilation) for n in range(nb)]


def _attn_rows(r, n, dilation, nblk=1):
    if dilation == 1:
        return pl.ds(r + n * ATT_BLOCK, nblk * ATT_BLOCK)
    return pl.ds(r + n * ATT_BLOCK * dilation, nblk * ATT_BLOCK, stride=dilation)


_QL, _KL, _VL = slice(0, 128), slice(128, 256), slice(256, 384)


def _qkv_specs(hb):
    return [pl.BlockSpec((None, SEQ, HEAD_DIM), functools.partial(lambda b, h, j: (b, 0, 3 * (hb + h) + j), j=j))
            for j in range(3)]


def attn_fwd(pa, bias, qg, kg, group, name):
    B = pa.shape[0]
    dilation = ATT_GROUPS[group][1]
    hb = group * 4

    def body(q_ref, k_ref, v_ref, b_ref, qg_ref, kg_ref, o_ref, l_ref):
        qg_, kg_ = qg_ref[...], kg_ref[...]
        for (r, n) in _attn_tiles(dilation):
            rows = _attn_rows(r, n, dilation)
            q = q_ref[rows, :]
            if n == 0:
                krows, bias_t = rows, b_ref[0, :, ATT_BLOCK:]
            else:
                krows, bias_t = _attn_rows(r, n - 1, dilation, 2), b_ref[0]
            o, lse = _attn_tile(q, k_ref[krows, :], v_ref[krows, :], bias_t, qg_, kg_)
            o_ref[rows, :] = o
            l_ref[rows, :] = lse

    head_out = pl.BlockSpec((None, SEQ, HEAD_DIM), lambda b, h: (b, 0, h))
    return pl.pallas_call(
        body, name=name, grid=(B, 4),
        in_specs=_qkv_specs(hb) + [
                  pl.BlockSpec((1, ATT_BLOCK, 2 * ATT_BLOCK), lambda b, h: (hb + h, 0, 0)),
                  pl.BlockSpec((1, HEAD_DIM), lambda b, h: (0, 0)),
                  pl.BlockSpec((1, HEAD_DIM), lambda b, h: (0, 0))],
        out_specs=[head_out, head_out],
        out_shape=[jax.ShapeDtypeStruct((B, SEQ, 512), F32), jax.ShapeDtypeStruct((B, SEQ, 512), F32)],
        compiler_params=_cparams(),
    )(pa, pa, pa, bias, qg, kg)


def attn_bwd(pa, bias, qg, kg, do, dlse, dpa, group, name):
    B = pa.shape[0]
    dilation = ATT_GROUPS[group][1]
    hb = group * 4

    def body(q_ref, k_ref, v_ref, b_ref, qg_ref, kg_ref, do_ref, dl_ref, dpa_in,
             dp_ref, db_ref, dqg_ref, dkg_ref, dq_s, dk_s, dv_s):
        del dpa_in
        h_id = pl.program_id(1)

        @pl.when(jnp.logical_and(pl.program_id(0) == 0, h_id == 0))
        def _():
            db_ref[...] = jnp.zeros_like(db_ref)
            dqg_ref[...] = jnp.zeros_like(dqg_ref)
            dkg_ref[...] = jnp.zeros_like(dkg_ref)

        dk_s[...] = jnp.zeros_like(dk_s)
        dv_s[...] = jnp.zeros_like(dv_s)
        qg_, kg_ = qg_ref[...], kg_ref[...]
        for (r, n) in _attn_tiles(dilation):
            rows = _attn_rows(r, n, dilation)
            q = q_ref[rows, :]
            if n == 0:
                krows, bias_t = rows, b_ref[0, :, ATT_BLOCK:]
            else:
                krows, bias_t = _attn_rows(r, n - 1, dilation, 2), b_ref[0]
            _, vjp = jax.vjp(_attn_tile, q, k_ref[krows, :], v_ref[krows, :], bias_t, qg_, kg_)
            dq, dk, dv, dbias, dqg, dkg = vjp((do_ref[rows, :], dl_ref[rows, :]))
            dq_s[rows, :] = dq
            dk_s[krows, :] += dk
            dv_s[krows, :] += dv
            if n == 0:
                db_ref[h_id, :, ATT_BLOCK:] += dbias
            else:
                db_ref[h_id] += dbias
            dqg_ref[...] += dqg
            dkg_ref[...] += dkg
        dp_ref[0, :, _QL] = dq_s[...].astype(BF16)
        dp_ref[0, :, _KL] = dk_s[...].astype(BF16)
        dp_ref[0, :, _VL] = dv_s[...].astype(BF16)

    const2 = lambda b, h: (0, 0)
    head_in = pl.BlockSpec((None, SEQ, HEAD_DIM), lambda b, h: (b, 0, h))
    head_blk = pl.BlockSpec((1, SEQ, ATT_HEAD_COLS), lambda b, h: (b, 0, hb + h))
    return pl.pallas_call(
        body, name=name, grid=(B, 4),
        in_specs=_qkv_specs(hb) + [
                  pl.BlockSpec((1, ATT_BLOCK, 2 * ATT_BLOCK), lambda b, h: (hb + h, 0, 0)),
                  pl.BlockSpec((1, HEAD_DIM), const2), pl.BlockSpec((1, HEAD_DIM), const2),
                  head_in, head_in,
                  pl.BlockSpec(memory_space=pl.ANY)],
        out_specs=[head_blk,
                   pl.BlockSpec((4, ATT_BLOCK, 2 * ATT_BLOCK), lambda b, h: (0, 0, 0)),
                   pl.BlockSpec((1, HEAD_DIM), const2), pl.BlockSpec((1, HEAD_DIM), const2)],
        out_shape=[jax.ShapeDtypeStruct(dpa.shape, BF16),
                   jax.ShapeDtypeStruct((4, ATT_BLOCK, 2 * ATT_BLOCK), F32),
                   jax.ShapeDtypeStruct((1, HEAD_DIM), F32), jax.ShapeDtypeStruct((1, HEAD_DIM), F32)],
        scratch_shapes=[pltpu.VMEM((SEQ, HEAD_DIM), F32)] * 3,
        input_output_aliases={8: 0},
        compiler_params=_cparams(),
    )(pa, pa, pa, bias, qg, kg, do, dlse, dpa)


def _merge(o0, o1, o2, l0, l1, l2):
    mx = jnp.maximum(jnp.maximum(l0, l1), l2)
    e0, e1, e2 = jnp.exp(l0 - mx), jnp.exp(l1 - mx), jnp.exp(l2 - mx)
    den = e0 + e1 + e2
    return (e0 / den) * o0 + (e1 / den) * o1 + (e2 / den) * o2


def merge_fwd(os_, ls_, name):
    B = os_[0].shape[0]

    def body(o0, o1, o2, l0, l1, l2, a_ref):
        a_ref[0] = _merge(o0[0], o1[0], o2[0], l0[0], l1[0], l2[0]).astype(BF16)

    return pl.pallas_call(
        body, name=name, grid=(B, SEQ // ROW_TILE),
        in_specs=[_row_spec(512)] * 6, out_specs=_row_spec(512),
        out_shape=jax.ShapeDtypeStruct((B, SEQ, 512), BF16),
        compiler_params=_cparams(),
    )(*os_, *ls_)


def merge_bwd(os_, ls_, datt, name):
    B = os_[0].shape[0]

    def body(o0, o1, o2, l0, l1, l2, da_ref, *outs):
        _, vjp = jax.vjp(_merge, o0[0], o1[0], o2[0], l0[0], l1[0], l2[0])
        for o_ref, g in zip(outs, vjp(da_ref[0])):
            o_ref[0] = g

    return pl.pallas_call(
        body, name=name, grid=(B, SEQ // ROW_TILE),
        in_specs=[_row_spec(512)] * 7, out_specs=[_row_spec(512)] * 6,
        out_shape=[jax.ShapeDtypeStruct((B, SEQ, 512), F32)] * 6,
        compiler_params=_cparams(),
    )(*os_, *ls_, datt)


def _gate_mix(ga, gm, ya, ym):
    return jax.nn.sigmoid(ga) * ya + jax.nn.sigmoid(gm) * ym


def gate_fwd(pg, ya, ym, name):
    B = pg.shape[0]

    def body(ga, gm, ya_ref, ym_ref, z_ref):
        z_ref[0] = _gate_mix(ga[0], gm[0], ya_ref[0], ym_ref[0]).astype(BF16)

    return pl.pallas_call(
        body, name=name, grid=(B, SEQ // ROW_TILE),
        in_specs=[pl.BlockSpec((1, ROW_TILE, D_MODEL), lambda b, i: (b, i, 0)),
                  pl.BlockSpec((1, ROW_TILE, D_MODEL), lambda b, i: (b, i, 1)),
                  _row_spec(D_MODEL), _row_spec(D_MODEL)],
        out_specs=_row_spec(D_MODEL),
        out_shape=jax.ShapeDtypeStruct((B, SEQ, D_MODEL), BF16),
        compiler_params=_cparams(),
    )(pg, pg, ya, ym)


def gate_bwd(pg, ya, ym, dz, name):
    B = pg.shape[0]

    def body(ga, gm, ya_ref, ym_ref, dz_ref, dpg_ref, dya_ref, dym_ref):
        _, vjp = jax.vjp(_gate_mix, ga[0], gm[0], ya_ref[0], ym_ref[0])
        dga, dgm, dya, dym = vjp(dz_ref[0])
        dpg_ref[0, :, :D_MODEL] = dga.astype(BF16)
        dpg_ref[0, :, D_MODEL:] = dgm.astype(BF16)
        dya_ref[0] = dya.astype(BF16)
        dym_ref[0] = dym.astype(BF16)

    return pl.pallas_call(
        body, name=name, grid=(B, SEQ // ROW_TILE),
        in_specs=[pl.BlockSpec((1, ROW_TILE, D_MODEL), lambda b, i: (b, i, 0)),
                  pl.BlockSpec((1, ROW_TILE, D_MODEL), lambda b, i: (b, i, 1)),
                  _row_spec(D_MODEL), _row_spec(D_MODEL), _row_spec(D_MODEL)],
        out_specs=[_row_spec(GATE_COLS), _row_spec(D_MODEL), _row_spec(D_MODEL)],
        out_shape=[jax.ShapeDtypeStruct((B, SEQ, GATE_COLS), BF16),
                   jax.ShapeDtypeStruct((B, SEQ, D_MODEL), BF16), jax.ShapeDtypeStruct((B, SEQ, D_MODEL), BF16)],
        compiler_params=_cparams(),
    )(pg, pg, ya, ym, dz)


def _log_sigmoid(x):
    return jnp.minimum(x, 0.0) - jnp.log(1.0 + jnp.exp(-jnp.abs(x)))


def _lane(x, idx):
    lanes = lax.broadcasted_iota(jnp.int32, x.shape, 1)
    return jnp.sum(jnp.where(lanes == idx, x, 0.0), axis=1, keepdims=True)


def _head_mask(e):
    lane = lax.broadcasted_iota(jnp.int32, (1, 128), 1)
    return jnp.logical_and(lane >= e * 64, lane < (e + 1) * 64).astype(F32)


def _mlstm_gates(ifb, m, e):
    L = ML_CHUNK
    li = _lane(ifb, e)
    ri = lax.broadcasted_iota(jnp.int32, (L, L), 0)
    ci = lax.broadcasted_iota(jnp.int32, (L, L), 1)
    causal = ri >= ci
    cs = _doth(causal.astype(F32), _log_sigmoid(ifb))
    b = _lane(cs, 2 + e)
    last = lax.broadcasted_iota(jnp.int32, (L, 1), 0) == L - 1
    b_end = jnp.sum(jnp.where(last, b, 0.0), axis=0, keepdims=True)
    g = b_end - b + li
    m_new = jnp.maximum(b_end + m, jnp.max(g, axis=0, keepdims=True))
    decay = jnp.exp(b_end + m - m_new)
    wk = jnp.exp(g - m_new)
    return li, b, causal, (ri == ci).astype(F32), m_new, decay, wk


def _mlstm_state(k2, v, ifb, C, n, m, *, e):
    _, _, _, _, m_new, decay, wk = _mlstm_gates(ifb, m, e)
    kw = wk * (k2 * _head_mask(e))
    return decay * C + bdot_tn(kw, v), decay * n + jnp.sum(kw, axis=0, keepdims=True), m_new


def _mlstm_step(q2, k2, v, mo, ifb, gn, C, n, m, *, e):
    L = ML_CHUNK
    hm = _head_mask(e)
    q, k = q2 * hm, k2 * hm
    li, b, causal, eye, m_new, decay, wk = _mlstm_gates(ifb, m, e)
    a = b + m
    rrow = _doth(jnp.ones((L, L), F32), eye * (li - b))
    Dm = jnp.where(causal, b + rrow, -jnp.inf)
    m_t = jnp.maximum(a, jnp.max(Dm, axis=1, keepdims=True))
    w = bdot_nt(q, k) * jnp.exp(Dm - m_t)
    inter = jnp.exp(a - m_t)
    num = bdot_nn(w, v) + inter * bdot_nn(q, C)
    nq = jnp.sum(w, axis=1, keepdims=True) + inter * jnp.sum(q * n, axis=1, keepdims=True)
    h = num / jnp.maximum(jnp.abs(nq), jnp.exp(-m_t))
    kw = wk * k
    C_new = decay * C + bdot_tn(kw, v)
    n_new = decay * n + jnp.sum(kw, axis=0, keepdims=True)
    hg = _rms(h) * gn * jax.nn.sigmoid(mo)
    return hg, C_new, n_new, m_new


def _shift_down(x, s):
    if s == 0:
        return x
    rows = lax.broadcasted_iota(jnp.int32, x.shape, 0)
    return jnp.where(rows >= s, pltpu.roll(x, s, 0), 0.0)


def _shift_up(x, s):
    if s == 0:
        return x
    S = x.shape[0]
    rows = lax.broadcasted_iota(jnp.int32, x.shape, 0)
    return jnp.where(rows < S - s, pltpu.roll(x, S - s, 0), 0.0)


def _conv_pre(x, cw, cb):
    y = cb + cw[3:4, :] * x
    for j in range(3):
        y = y + cw[j:j + 1, :] * _shift_down(x, 3 - j)
    return y


def _conv_bwd(x, cw, dpre):
    dx = cw[3:4, :] * dpre
    dcw = [None] * 4
    dcw[3] = jnp.sum(dpre * x, axis=0, keepdims=True)
    for j in range(3):
        dx = dx + cw[j:j + 1, :] * _shift_up(dpre, 3 - j)
        dcw[j] = jnp.sum(dpre * _shift_down(x, 3 - j), axis=0, keepdims=True)
    return dx, dcw, jnp.sum(dpre, axis=0, keepdims=True)


def _silu(z):
    return z * jax.nn.sigmoid(z)


def _dsilu(z):
    s = jax.nn.sigmoid(z)
    return s * (1.0 + z * (1.0 - s))


_ML_Q, _ML_K = slice(0, 128), slice(128, 256)
_ML_IF = slice(768, 896)


def _ml_v(e):
    return slice(256 + e * 128, 384 + e * 128)


def _ml_o(e):
    return slice(512 + e * 128, 640 + e * 128)


def _ml_specs():
    pair = lambda b, p: (b, 0, p)
    return [pl.BlockSpec((1, SEQ, ML_PAIR_COLS), pair),
            pl.BlockSpec((1, 4, 128), lambda b, p: (p, 0, 0)),
            pl.BlockSpec((1, 4, 128), lambda b, p: (4 + p, 0, 0)),
            pl.BlockSpec((1, 1, 128), lambda b, p: (p, 0, 0)),
            pl.BlockSpec((1, 1, 128), lambda b, p: (4 + p, 0, 0)),
            pl.BlockSpec((1, 1, 128), lambda b, p: (p, 0, 0)),
            pl.BlockSpec((1, 1, 256), lambda b, p: (p, 0, 0))]


def mlstm_fwd(pm, cw8, cb8, bifp, gn4, name):
    B = pm.shape[0]

    def body(p_ref, cwq, cwk, cbq, cbk, bif_ref, gn_ref, hg_ref, qc_s, kc_s, C_s, n_s, m_s):
        qc_s[...] = _silu(_conv_pre(p_ref[0, :, _ML_Q], cwq[0], cbq[0]))
        kc_s[...] = _silu(_conv_pre(p_ref[0, :, _ML_K], cwk[0], cbk[0])) * (64 ** -0.5)
        C_s[...] = jnp.zeros_like(C_s)
        n_s[...] = jnp.zeros_like(n_s)
        m_s[...] = jnp.zeros_like(m_s)

        def chunk(ci, carry):
            rows = pl.ds(pl.multiple_of(ci * ML_CHUNK, ML_CHUNK), ML_CHUNK)
            q2, k2 = qc_s[rows, :], kc_s[rows, :]
            ifb = p_ref[0, rows, _ML_IF] + bif_ref[0]
            for e in range(2):
                hg, C, n, m = _mlstm_step(q2, k2, p_ref[0, rows, _ml_v(e)], p_ref[0, rows, _ml_o(e)], ifb,
                                          gn_ref[0, :, e * 128:(e + 1) * 128], C_s[e], n_s[e], m_s[e], e=e)
                hg_ref[0, rows, e * 128:(e + 1) * 128] = hg.astype(BF16)
                C_s[e], n_s[e], m_s[e] = C, n, m
            return carry

        lax.fori_loop(0, N_CHUNKS, chunk, 0)

    return pl.pallas_call(
        body, name=name, grid=(B, ML_PAIRS),
        in_specs=_ml_specs(),
        out_specs=pl.BlockSpec((1, SEQ, 256), lambda b, p: (b, 0, p)),
        out_shape=jax.ShapeDtypeStruct((B, SEQ, D_MODEL), BF16),
        scratch_shapes=[pltpu.VMEM((SEQ, 128), F32), pltpu.VMEM((SEQ, 128), F32),
                        pltpu.VMEM((2, 128, 128), F32), pltpu.VMEM((2, 1, 128), F32), pltpu.VMEM((2, 1, 1), F32)],
        compiler_params=_cparams(),
    )(pm, cw8, cw8, cb8, cb8, bifp, gn4)


def mlstm_bwd(pm, cw8, cb8, bifp, gn4, dhg, name):
    B = pm.shape[0]

    def body(p_ref, cwq, cwk, cbq, cbk, bif_ref, gn_ref, dh_ref,
             dp_ref, dcw_ref, dcb_ref, dbif_ref, dgn_ref,
             qc_s, kc_s, dq_s, dk_s, Cs, ns, ms, dC_s, dn_s, dm_s):
        p_id = pl.program_id(1)

        @pl.when(jnp.logical_and(pl.program_id(0) == 0, p_id == 0))
        def _():
            dcw_ref[...] = jnp.zeros_like(dcw_ref)
            dcb_ref[...] = jnp.zeros_like(dcb_ref)
            dbif_ref[...] = jnp.zeros_like(dbif_ref)
            dgn_ref[...] = jnp.zeros_like(dgn_ref)

        qc_s[...] = _silu(_conv_pre(p_ref[0, :, _ML_Q], cwq[0], cbq[0]))
        kc_s[...] = _silu(_conv_pre(p_ref[0, :, _ML_K], cwk[0], cbk[0])) * (64 ** -0.5)

        Cs[0] = jnp.zeros_like(Cs[0])
        ns[0] = jnp.zeros_like(ns[0])
        ms[0] = jnp.zeros_like(ms[0])

        def fwd_chunk(ci, carry):
            rows = pl.ds(pl.multiple_of(ci * ML_CHUNK, ML_CHUNK), ML_CHUNK)
            k2 = kc_s[rows, :]
            ifb = p_ref[0, rows, _ML_IF] + bif_ref[0]
            for e in range(2):
                C, n, m = _mlstm_state(k2, p_ref[0, rows, _ml_v(e)], ifb, Cs[ci, e], ns[ci, e], ms[ci, e], e=e)
                Cs[ci + 1, e], ns[ci + 1, e], ms[ci + 1, e] = C, n, m
            return carry

        lax.fori_loop(0, N_CHUNKS - 1, fwd_chunk, 0)

        dC_s[...] = jnp.zeros_like(dC_s)
        dn_s[...] = jnp.zeros_like(dn_s)
        dm_s[...] = jnp.zeros_like(dm_s)

        def bwd_chunk(t, carry):
            ci = N_CHUNKS - 1 - t
            rows = pl.ds(pl.multiple_of(ci * ML_CHUNK, ML_CHUNK), ML_CHUNK)
            q2, k2 = qc_s[rows, :], kc_s[rows, :]
            ifb = p_ref[0, rows, _ML_IF] + bif_ref[0]
            dq2 = jnp.zeros((ML_CHUNK, 128), F32)
            dk2 = jnp.zeros((ML_CHUNK, 128), F32)
            difb = jnp.zeros((ML_CHUNK, 128), F32)
            for e in range(2):
                lanes = slice(e * 128, (e + 1) * 128)
                _, vjp = jax.vjp(functools.partial(_mlstm_step, e=e), q2, k2, p_ref[0, rows, _ml_v(e)],
                                 p_ref[0, rows, _ml_o(e)], ifb, gn_ref[0, :, lanes], Cs[ci, e], ns[ci, e], ms[ci, e])
                g = vjp((dh_ref[0, rows, lanes], dC_s[e], dn_s[e], dm_s[e]))
                dq2, dk2, difb = dq2 + g[0], dk2 + g[1], difb + g[4]
                dp_ref[0, rows, _ml_v(e)] = g[2].astype(BF16)
                dp_ref[0, rows, _ml_o(e)] = g[3].astype(BF16)
                dgn_ref[p_id, :, lanes] += g[5]
                dC_s[e], dn_s[e], dm_s[e] = g[6], g[7], g[8]
            dq_s[rows, :] = dq2
            dk_s[rows, :] = dk2
            dp_ref[0, rows, _ML_IF] = difb.astype(BF16)
            dbif_ref[p_id] += jnp.sum(difb, axis=0, keepdims=True)
            return carry

        lax.fori_loop(0, N_CHUNKS, bwd_chunk, 0)

        for (sl, cw, cb, d_s, blk, scale) in ((_ML_Q, cwq, cbq, dq_s, p_id, 1.0), (_ML_K, cwk, cbk, dk_s, 4 + p_id, 64 ** -0.5)):
            xr = p_ref[0, :, sl]
            dpre = d_s[...] * scale * _dsilu(_conv_pre(xr, cw[0], cb[0]))
            dx, dcw, dcb = _conv_bwd(xr, cw[0], dpre)
            dp_ref[0, :, sl] = dx.astype(BF16)
            for j in range(4):
                dcw_ref[blk, j:j + 1, :] += dcw[j]
            dcb_ref[blk] += dcb

    full3 = lambda b, p: (0, 0, 0)
    return pl.pallas_call(
        body, name=name, grid=(B, ML_PAIRS),
        in_specs=_ml_specs() + [pl.BlockSpec((1, SEQ, 256), lambda b, p: (b, 0, p))],
        out_specs=[pl.BlockSpec((1, SEQ, ML_PAIR_COLS), lambda b, p: (b, 0, p)),
                   pl.BlockSpec((8, 4, 128), full3), pl.BlockSpec((8, 1, 128), full3),
                   pl.BlockSpec((4, 1, 128), full3), pl.BlockSpec((4, 1, 256), full3)],
        out_shape=[jax.ShapeDtypeStruct((B, SEQ, ML_COLS), BF16),
                   jax.ShapeDtypeStruct((8, 4, 128), F32), jax.ShapeDtypeStruct((8, 1, 128), F32),
                   jax.ShapeDtypeStruct((4, 1, 128), F32), jax.ShapeDtypeStruct((4, 1, 256), F32)],
        scratch_shapes=[pltpu.VMEM((SEQ, 128), F32)] * 4 + [
            pltpu.VMEM((N_CHUNKS, 2, 128, 128), F32), pltpu.VMEM((N_CHUNKS, 2, 1, 128), F32),
            pltpu.VMEM((N_CHUNKS, 2, 1, 1), F32),
            pltpu.VMEM((2, 128, 128), F32), pltpu.VMEM((2, 1, 128), F32), pltpu.VMEM((2, 1, 1), F32)],
        compiler_params=_cparams(),
    )(pm, cw8, cw8, cb8, cb8, bifp, gn4, dhg)


def _adamw(w, g, m, v):
    m = ADAM_B1 * m + (1.0 - ADAM_B1) * g
    v = ADAM_B2 * v + (1.0 - ADAM_B2) * (g * g)
    m_hat = m / (1.0 - ADAM_B1 ** ADAM_STEP)
    v_hat = v / (1.0 - ADAM_B2 ** ADAM_STEP)
    delta = -ADAM_LR * (m_hat / (jnp.sqrt(v_hat) + ADAM_EPS) + ADAM_WD * w)
    return delta, m, v


def adamw(w, g, m, v, name, parts=False):
    R, C = w.shape
    tr = _pick(R, (256, 128, 64, 32, 16, 8, 4, 2, 1)) if R * C * 4 > (1 << 20) else R
    spec = pl.BlockSpec((tr, C), lambda i: (i, 0))
    g_spec = pl.BlockSpec((N_DEV, tr, C), lambda i: (0, i, 0)) if parts else spec

    def body(w_ref, g_ref, m_ref, v_ref, go_ref, d_ref, mo_ref, vo_ref):
        if parts:
            g = g_ref[0].astype(F32)
            for k in range(1, N_DEV):
                g = g + g_ref[k].astype(F32)
        else:
            g = g_ref[...]
        d, mn, vn = _adamw(w_ref[...], g, m_ref[...], v_ref[...])
        go_ref[...], d_ref[...], mo_ref[...], vo_ref[...] = g, d, mn, vn

    return pl.pallas_call(
        body, name=name, grid=(R // tr,),
        in_specs=[spec, g_spec, spec, spec], out_specs=[spec] * 4,
        out_shape=[jax.ShapeDtypeStruct((R, C), F32)] * 4,
        compiler_params=_cparams(),
    )(w, g, m, v)


def _mesh_pos():
    return lax.axis_index("x"), lax.axis_index("y"), lax.axis_index("c")


def _flip(pos, f):
    x, y, c = pos
    return (1 - x if f & 4 else x, 1 - y if f & 2 else y, 1 - c if f & 1 else c)


def _index(pos):
    return 4 * pos[0] + 2 * pos[1] + pos[2]


def _exchange(arrs, name, scatter):
    n = len(arrs)

    def body(*refs):
        ins, outs = refs[:n], refs[n:2 * n]
        send, recv, lsem = refs[2 * n:]
        me = _mesh_pos()
        mine = _index(me)
        copies = []
        for i in range(n):
            src = ins[i].at[mine] if scatter else ins[i]
            loc = pltpu.make_async_copy(src, outs[i].at[mine], lsem.at[i])
            loc.start()
            copies.append(loc)
            for f in range(1, N_DEV):
                peer = _flip(me, f)
                src = ins[i].at[_index(peer)] if scatter else ins[i]
                cp = pltpu.make_async_remote_copy(
                    src_ref=src, dst_ref=outs[i].at[mine],
                    send_sem=send.at[i * 7 + f - 1], recv_sem=recv.at[i * 7 + f - 1],
                    device_id=peer, device_id_type=pl.DeviceIdType.MESH)
                cp.start()
                copies.append(cp)
        for cp in copies:
            cp.wait()

    any_spec = pl.BlockSpec(memory_space=pl.ANY)
    out_shape = [jax.ShapeDtypeStruct(a.shape if scatter else (N_DEV,) + a.shape, a.dtype) for a in arrs]
    res = pl.pallas_call(
        body, name=name,
        in_specs=[any_spec] * n, out_specs=[any_spec] * n, out_shape=out_shape,
        scratch_shapes=[pltpu.SemaphoreType.DMA((7 * n,)), pltpu.SemaphoreType.DMA((7 * n,)),
                        pltpu.SemaphoreType.DMA((n,))],
        compiler_params=_cparams(),
    )(*arrs)
    return list(res)


def all_gather(arrs, name):
    return _exchange(arrs, name, False)


def all_to_all(arrs, name):
    return _exchange(arrs, name, True)


def cast_bf16(arrs, name):
    outs = []
    for i, a in enumerate(arrs):
        R, C = a.shape
        tr = _pick(R, (256, 128, 64, 32, 16, 8)) if R * C * 4 > (1 << 21) else R
        spec = pl.BlockSpec((tr, C), lambda i: (i, 0))

        def body(a_ref, o_ref):
            o_ref[...] = a_ref[...].astype(BF16)

        outs.append(pl.pallas_call(body, name=f"{name}_{i}", grid=(R // tr,), in_specs=[spec], out_specs=spec,
                                   out_shape=jax.ShapeDtypeStruct((R, C), BF16), compiler_params=_cparams())(a))
    return outs


def sum_parts(parts, name):
    def fn(p):
        g = p[0]
        for k in range(1, N_DEV):
            g = g + p[k]
        return (g,)
    return small_call(fn, [parts], [jax.ShapeDtypeStruct(parts.shape[1:], F32)], name)[0]


_SPLITS = np.cumsum([1536, 1536, 1536, 512, 512, 1024, 1024, 8, 8, 2048])[:-1].tolist()


def split_w_in(w):
    aq, ak, av, mq, mk, mv, mo, mi, mf, gates = jnp.split(w, _SPLITS, axis=1)
    R = w.shape[0]
    w_att = jnp.stack([aq.reshape(R, 12, 128), ak.reshape(R, 12, 128), av.reshape(R, 12, 128)], axis=2)
    gif = jnp.concatenate([mi.reshape(R, 4, 2), mf.reshape(R, 4, 2), jnp.zeros((R, 4, 124), w.dtype)], axis=2)
    w_ml = jnp.concatenate([mq.reshape(R, 4, 128), mk.reshape(R, 4, 128), mv.reshape(R, 4, 256),
                            mo.reshape(R, 4, 256), gif], axis=2)
    return w_att.reshape(R, ATT_COLS), w_ml.reshape(R, ML_COLS), gates


def merge_w_in(g_att, g_ml, g_gate):
    R = g_att.shape[0]
    a = g_att.reshape(R, 12, 3, 128)
    m = g_ml.reshape(R, 4, ML_PAIR_COLS)
    gif = m[:, :, 768:772]
    return jnp.concatenate([
        a[:, :, 0].reshape(R, 1536), a[:, :, 1].reshape(R, 1536), a[:, :, 2].reshape(R, 1536),
        m[:, :, 0:128].reshape(R, 512), m[:, :, 128:256].reshape(R, 512),
        m[:, :, 256:512].reshape(R, 1024), m[:, :, 512:768].reshape(R, 1024),
        gif[:, :, 0:2].reshape(R, 8), gif[:, :, 2:4].reshape(R, 8), g_gate], axis=1)


def _blk8(v, width=128):
    r = v.shape[0]
    return v.reshape(r, 1024 // width, width).transpose(1, 0, 2)


def _unblk8(v):
    nb, r, w = v.shape
    return v.transpose(1, 0, 2).reshape(r, nb * w)


def local_step(x, target, mods, w, small):
    B = x.shape[0]
    T = B * SEQ
    shift1, scale1, gate1, shift2, scale2, gate2 = mods
    f2 = lambda a: a.reshape(T, a.shape[-1])
    f3 = lambda a: a.reshape(B, SEQ, a.shape[-1])

    rel_t = jnp.pad(small["rel_bias"].T, ((0, 4), (0, 0)))
    onehots = [_bucket_onehot(d) for _, d in ATT_GROUPS]
    biases = [bias_expand(rel_t, oh, f"bias_expand{g}").reshape(16, ATT_BLOCK, 2 * ATT_BLOCK)
              for g, oh in enumerate(onehots)]
    qg, kg = small["q_norm_g"], small["k_norm_g"]
    cw8 = _blk8(small["conv_w"])
    cb8 = _blk8(small["conv_b"])
    b_if = small["b_if"].reshape(2, 4, 2)
    bifp = jnp.concatenate([b_if[0], b_if[1], jnp.zeros((4, 124), F32)], axis=1).reshape(4, 1, 128)
    gn4 = small["mlstm_norm_g"].reshape(4, 1, 256)

    u = modnorm_fwd(x, small["norm1_g"], scale1, shift1, "modnorm1")
    u2d = f2(u)
    pa = f3(matmul(u2d, w["w_att"], mode="nn", name="proj_att"))
    pm = f3(matmul(u2d, w["w_ml"], mode="nn", name="proj_ml"))
    pg = f3(matmul(u2d, w["w_gate"], mode="nn", name="proj_gate"))
    os_, ls_ = [], []
    for g in range(3):
        o, l = attn_fwd(pa, biases[g], qg, kg, g, f"attn_fwd{g}")
        os_.append(o)
        ls_.append(l)
    att = merge_fwd(os_, ls_, "merge_fwd")
    y_att = matmul(f2(att), w["w_att_out"], mode="nn", name="att_out")
    hg = mlstm_fwd(pm, cw8, cb8, bifp, gn4, "mlstm_fwd")
    y_ml = matmul(f2(hg), w["w_ml_out"], mode="nn", name="ml_out")
    z = gate_fwd(pg, f3(y_att), f3(y_ml), "gate_fwd")
    y = matmul(f2(z), w["w_out"], mode="nn", name="out_proj")
    x1, u2 = resid_modnorm_fwd(x, f3(y), gate1, small["norm2_g"], scale2, shift2, "resid_modnorm2")
    pre, hdn = matmul(f2(u2), w["w_ff1"], mode="nn", name="ff1", out_dtypes=(F32, BF16),
                      epi=lambda acc: (acc, jnp.square(jnp.maximum(acc, 0.0))))
    ffo = matmul(hdn, w["w_ff2"], mode="nn", name="ff2")
    dx2, loss = resid_loss(x1, f3(ffo), gate2, target, "resid_loss")

    d_ffo, d_gate2 = resid_bwd(dx2, f3(ffo), gate2, "resid_bwd2")
    g_ff2 = matmul(hdn, f2(d_ffo), mode="tn", name="g_ff2", out_dtypes=(BF16,))
    d_pre = matmul(f2(d_ffo), w["w_ff2"], mode="nt", name="d_hdn", out_dtypes=(BF16,), extras=(pre,),
                   epi=lambda acc, p: (acc * (2.0 * jnp.maximum(p, 0.0)),))
    g_ff1 = matmul(f2(u2), d_pre, mode="tn", name="g_ff1", out_dtypes=(BF16,))
    du2 = matmul(d_pre, w["w_ff1"], mode="nt", name="d_u2")
    dx1, d_norm2, d_scale2, d_shift2 = modnorm_bwd(x1, small["norm2_g"], scale2, shift2, f3(du2), dx2, "modnorm_bwd2")
    dy, d_gate1 = resid_bwd(dx1, f3(y), gate1, "resid_bwd1")
    g_out = matmul(f2(z), f2(dy), mode="tn", name="g_out", out_dtypes=(BF16,))
    dz = matmul(f2(dy), w["w_out"], mode="nt", name="d_z")
    dpg, d_ya, d_ym = gate_bwd(pg, f3(y_att), f3(y_ml), f3(dz), "gate_bwd")
    g_att_out = matmul(f2(att), f2(d_ya), mode="tn", name="g_att_out", out_dtypes=(BF16,))
    d_att = matmul(f2(d_ya), w["w_att_out"], mode="nt", name="d_att")
    g_ml_out = matmul(f2(hg), f2(d_ym), mode="tn", name="g_ml_out", out_dtypes=(BF16,))
    d_hg = matmul(f2(d_ym), w["w_ml_out"], mode="nt", name="d_hg")
    dmerge = merge_bwd(os_, ls_, f3(d_att), "merge_bwd")
    dpa = jnp.zeros((B, SEQ, ATT_COLS), BF16)
    d_rel = []
    d_qg = d_kg = None
    for g in range(3):
        dpa, dbias, dq_g, dk_g = attn_bwd(pa, biases[g], qg, kg, dmerge[g], dmerge[3 + g], dpa, g, f"attn_bwd{g}")
        db8 = jnp.pad(dbias.reshape(4, -1), ((0, 4), (0, 0)))
        d_rel.append(bias_reduce(db8, onehots[g], f"bias_reduce{g}")[:4])
        d_qg = dq_g if d_qg is None else d_qg + dq_g
        d_kg = dk_g if d_kg is None else d_kg + dk_g
    dpm, dcw8, dcb8, dbifp, dgn4 = mlstm_bwd(pm, cw8, cb8, bifp, gn4, f3(d_hg), "mlstm_bwd")
    g_w_att = matmul(u2d, f2(dpa), mode="tn", name="g_w_att", out_dtypes=(BF16,))
    g_w_ml = matmul(u2d, f2(dpm), mode="tn", name="g_w_ml", out_dtypes=(BF16,))
    g_w_gate = matmul(u2d, f2(dpg), mode="tn", name="g_w_gate", out_dtypes=(BF16,))
    du = matmul(f2(dpa), w["w_att"], mode="nt", name="d_u_att")
    du = matmul(f2(dpm), w["w_ml"], mode="nt", name="d_u_ml", extras=(du,), epi=lambda acc, e: (acc + e,))
    du = matmul(f2(dpg), w["w_gate"], mode="nt", name="d_u_gate", extras=(du,), epi=lambda acc, e: (acc + e,))
    grad_x, d_norm1, d_scale1, d_shift1 = modnorm_bwd(x, small["norm1_g"], scale1, shift1, f3(du), dx1, "modnorm_bwd1")

    d_mods = (d_shift1, d_scale1, d_gate1, d_shift2, d_scale2, d_gate2)
    big = dict(w_in=merge_w_in(g_w_att, g_w_ml, g_w_gate), w_att_out=g_att_out, w_ml_out=g_ml_out, w_out=g_out,
               w_ff1=g_ff1, w_ff2=g_ff2)
    dbif = dbifp.reshape(4, 128)
    small_g = dict(
        norm1_g=d_norm1, norm2_g=d_norm2,
        b_if=jnp.stack([dbif[:, 0:2].reshape(8), dbif[:, 2:4].reshape(8)]),
        conv_w=_unblk8(dcw8), conv_b=_unblk8(dcb8), q_norm_g=d_qg, k_norm_g=d_kg,
        rel_bias=jnp.concatenate(d_rel, axis=0).T,
        mlstm_norm_g=dgn4.reshape(1, 1024))
    return loss, grad_x, d_mods, big, small_g


_SMALL = (("b_ada", 6144), ("norm1_g", 1024), ("norm2_g", 1024), ("b_if", 16), ("conv_b", 1024),
          ("q_norm_g", 128), ("k_norm_g", 128), ("rel_bias", 384), ("mlstm_norm_g", 1024), ("conv_w", 4096))
_SMALL_ROWS = 120
_REPL = _SMALL[:-1]
_REPL_ROWS = 86


def _pack(d, names, rows):
    flat = jnp.concatenate([d[k].reshape(-1) for k, _ in names])
    return jnp.pad(flat, (0, rows * 128 - flat.shape[0])).reshape(rows, 128)


def _unpack(slab, names, shapes):
    flat = slab.reshape(-1)
    out, off = {}, 0
    for k, nel in names:
        out[k] = flat[off:off + nel].reshape(shapes[k])
        off += nel
    return out


def kernel(x, c, w_ada, b_ada, norm1_g, norm2_g, w_in, b_if, conv_w, conv_b, q_norm_g, k_norm_g, rel_bias, mlstm_norm_g, w_att_out, w_ml_out, w_out, w_ff1, w_ff2, loss_target, m_w_ada, m_b_ada, m_norm1_g, m_norm2_g, m_w_in, m_b_if, m_conv_w, m_conv_b, m_q_norm_g, m_k_norm_g, m_rel_bias, m_mlstm_norm_g, m_w_att_out, m_w_ml_out, m_w_out, m_w_ff1, m_w_ff2, v_w_ada, v_b_ada, v_norm1_g, v_norm2_g, v_w_in, v_b_if, v_conv_w, v_conv_b, v_q_norm_g, v_k_norm_g, v_rel_bias, v_mlstm_norm_g, v_w_att_out, v_w_ml_out, v_w_out, v_w_ff1, v_w_ff2):
    P = dict(w_ada=w_ada, b_ada=b_ada, norm1_g=norm1_g, norm2_g=norm2_g, w_in=w_in, b_if=b_if, conv_w=conv_w,
             conv_b=conv_b, q_norm_g=q_norm_g, k_norm_g=k_norm_g, rel_bias=rel_bias, mlstm_norm_g=mlstm_norm_g,
             w_att_out=w_att_out, w_ml_out=w_ml_out, w_out=w_out, w_ff1=w_ff1, w_ff2=w_ff2)
    M = dict(w_ada=m_w_ada, b_ada=m_b_ada, norm1_g=m_norm1_g, norm2_g=m_norm2_g, w_in=m_w_in, b_if=m_b_if,
             conv_w=m_conv_w, conv_b=m_conv_b, q_norm_g=m_q_norm_g, k_norm_g=m_k_norm_g, rel_bias=m_rel_bias,
             mlstm_norm_g=m_mlstm_norm_g, w_att_out=m_w_att_out, w_ml_out=m_w_ml_out, w_out=m_w_out,
             w_ff1=m_w_ff1, w_ff2=m_w_ff2)
    V = dict(w_ada=v_w_ada, b_ada=v_b_ada, norm1_g=v_norm1_g, norm2_g=v_norm2_g, w_in=v_w_in, b_if=v_b_if,
             conv_w=v_conv_w, conv_b=v_conv_b, q_norm_g=v_q_norm_g, k_norm_g=v_k_norm_g, rel_bias=v_rel_bias,
             mlstm_norm_g=v_mlstm_norm_g, w_att_out=v_w_att_out, w_ml_out=v_w_ml_out, w_out=v_w_out,
             w_ff1=v_w_ff1, w_ff2=v_w_ff2)
    names = list(P)
    shapes = {k: P[k].shape for k in names}
    B = x.shape[0]
    me = _index(_mesh_pos())

    big_names = ("w_in", "w_att_out", "w_ml_out", "w_out", "w_ff1", "w_ff2")
    shards = cast_bf16([P[k][0] for k in big_names], "cast_w")
    gathered = all_gather(shards + [c, conv_w[0]], "gather_weights")
    gw = dict(zip(big_names, gathered[:6]))
    c_all = gathered[6].reshape(N_DEV * B, D_MODEL)
    conv_w_full = gathered[7].transpose(1, 0, 2).reshape(4, 1024)
    w_in_full = gw["w_in"].transpose(1, 0, 2).reshape(D_MODEL, D_IN)
    w_att, w_ml, w_gate = split_w_in(w_in_full)
    w = dict(w_att=w_att, w_ml=w_ml, w_gate=w_gate,
             w_att_out=gw["w_att_out"].transpose(1, 0, 2).reshape(512, D_MODEL),
             w_ml_out=gw["w_ml_out"].reshape(D_MODEL, D_MODEL), w_out=gw["w_out"].reshape(D_MODEL, D_MODEL),
             w_ff1=gw["w_ff1"].transpose(1, 0, 2).reshape(D_MODEL, D_FF), w_ff2=gw["w_ff2"].reshape(D_FF, D_MODEL))

    (silu_c,) = small_call(lambda a: (_silu(a),), [c_all], [jax.ShapeDtypeStruct(c_all.shape, F32)], "silu_c")
    b_ada_cols = lax.dynamic_slice(b_ada, (0, me * 768), (1, 768))
    ada_cols = matmul(silu_c, w_ada[0], mode="nn", name="ada", extras=(jnp.broadcast_to(b_ada_cols, (N_DEV * B, 768)),),
                      epi=lambda acc, bb: (acc + bb,))
    (ada_t,) = all_to_all([ada_cols.reshape(N_DEV, B, 768)], "ada_exchange")
    ada = ada_t.transpose(1, 0, 2).reshape(B, 6 * D_MODEL)
    mods = tuple(ada[:, i * D_MODEL:(i + 1) * D_MODEL].reshape(B, 1, D_MODEL) for i in range(6))

    small = dict(norm1_g=norm1_g, norm2_g=norm2_g, b_if=b_if[0], conv_w=conv_w_full, conv_b=conv_b,
                 q_norm_g=q_norm_g, k_norm_g=k_norm_g, rel_bias=rel_bias, mlstm_norm_g=mlstm_norm_g)
    loss, grad_x, d_mods, big, small_g = local_step(x, loss_target, mods, w, small)
    loss = lax.psum(loss[0, 0], ("x", "y", "c"))

    d_ada = jnp.concatenate([d.reshape(B, D_MODEL) for d in d_mods], axis=1)
    (d_ada_t,) = all_to_all([d_ada.reshape(B, N_DEV, 768).transpose(1, 0, 2)], "d_ada_exchange")
    d_ada_cols = d_ada_t.reshape(N_DEV * B, 768)
    g_w_ada = matmul(silu_c, d_ada_cols, mode="tn", name="g_w_ada")
    (g_b_cols,) = small_call(lambda a: (jnp.sum(a, axis=0, keepdims=True),), [d_ada_cols],
                             [jax.ShapeDtypeStruct((1, 768), F32)], "g_b_ada_cols")
    small_g["b_ada"] = lax.dynamic_update_slice(jnp.zeros((1, 6144), F32), g_b_cols, (0, me * 768))

    send = [
        big["w_in"].reshape(D_MODEL, N_DEV, W_IN_SHARD).transpose(1, 0, 2),
        big["w_att_out"].reshape(512, N_DEV, 128).transpose(1, 0, 2),
        big["w_ml_out"].reshape(N_DEV, 128, D_MODEL),
        big["w_out"].reshape(N_DEV, 128, D_MODEL),
        big["w_ff1"].reshape(D_MODEL, N_DEV, 512).transpose(1, 0, 2),
        big["w_ff2"].reshape(N_DEV, 512, D_MODEL),
    ]
    recv = all_to_all(send, "grad_exchange")
    (small_parts,) = all_gather([_pack(small_g, _SMALL, _SMALL_ROWS)], "small_grad_gather")
    small_sum = sum_parts(small_parts, "small_grad_sum")
    sg = _unpack(small_sum, _SMALL, {**{k: shapes[k] for k, _ in _REPL}, "conv_w": (4, 1024)})

    G, Dl, NM, NV = {}, {}, {}, {}
    for k, parts in zip(big_names, recv):
        g, d, nm, nv = adamw(P[k][0], parts, M[k][0], V[k][0], f"adamw_{k}", parts=True)
        G[k], Dl[k], NM[k], NV[k] = g[None], d[None], nm[None], nv[None]
    g, d, nm, nv = adamw(w_ada[0], g_w_ada, m_w_ada[0], v_w_ada[0], "adamw_w_ada")
    G["w_ada"], Dl["w_ada"], NM["w_ada"], NV["w_ada"] = g[None], d[None], nm[None], nv[None]
    g_conv = lax.dynamic_slice(sg["conv_w"], (0, me * 128), (4, 128))
    g, d, nm, nv = adamw(conv_w[0], g_conv, m_conv_w[0], v_conv_w[0], "adamw_conv_w")
    G["conv_w"], Dl["conv_w"], NM["conv_w"], NV["conv_w"] = g[None], d[None], nm[None], nv[None]
    gslab = _pack(sg, _REPL, _REPL_ROWS)
    _, d, nm, nv = adamw(_pack(P, _REPL, _REPL_ROWS), gslab, _pack(M, _REPL, _REPL_ROWS), _pack(V, _REPL, _REPL_ROWS),
                         "adamw_small")
    rs = {k: shapes[k] for k, _ in _REPL}
    d, nm, nv = _unpack(d, _REPL, rs), _unpack(nm, _REPL, rs), _unpack(nv, _REPL, rs)
    for k, _ in _REPL:
        G[k], Dl[k], NM[k], NV[k] = sg[k], d[k], nm[k], nv[k]

    return (loss, grad_x, *[G[k] for k in names], *[Dl[k] for k in names], *[NM[k] for k in names],
            *[NV[k] for k in names])
```

```python
import functools
import math

import numpy as np
import jax
import jax.numpy as jnp
from jax import lax
from jax.experimental import pallas as pl
from jax.experimental.pallas import tpu as pltpu

F32 = jnp.float32
BF16 = jnp.bfloat16

N_DEV = 8
D_MODEL = 1024
SEQ = 2048
ATT_GROUPS = ((128, 1), (512, 4), (2048, 16))
N_ATT_HEADS = 12
ATT_BLOCK = 128
HEAD_DIM = 128
ML_HEADS = 8
ML_PAIRS = 4
ML_CHUNK = 64
N_CHUNKS = SEQ // ML_CHUNK
ML_UNROLL = 4
N_BUCKETS = 32
MAX_DISTANCE = 2048
D_FF = 4096
D_IN = 9744
EPS = 1e-6

ADAM_LR = 0.001
ADAM_B1 = 0.9
ADAM_B2 = 0.999
ADAM_EPS = 1e-08
ADAM_WD = 0.01
ADAM_STEP = 10

ATT_HEAD_COLS = 3 * HEAD_DIM
ATT_COLS = N_ATT_HEADS * ATT_HEAD_COLS
ML_PAIR_COLS = 896
ML_COLS = ML_PAIRS * ML_PAIR_COLS
GATE_COLS = 2 * D_MODEL
W_IN_SHARD = D_IN // N_DEV

VMEM_LIMIT = 56 * 1024 * 1024


def _cparams(**kw):
    return pltpu.CompilerParams(vmem_limit_bytes=VMEM_LIMIT, **kw)


_NN = ((1,), (0,))
_NT = ((1,), (1,))
_TN = ((0,), (0,))


def _mxu(a, b, dims):
    return lax.dot_general(a.astype(BF16), b.astype(BF16), (dims, ((), ())), preferred_element_type=F32)


@jax.custom_vjp
def bdot_nn(a, b):
    return _mxu(a, b, _NN)


def _nn_fwd(a, b):
    return _mxu(a, b, _NN), (a, b)


def _nn_bwd(res, g):
    a, b = res
    return _mxu(g, b, _NT), _mxu(a, g, _TN)


bdot_nn.defvjp(_nn_fwd, _nn_bwd)


@jax.custom_vjp
def bdot_nt(a, b):
    return _mxu(a, b, _NT)


def _nt_fwd(a, b):
    return _mxu(a, b, _NT), (a, b)


def _nt_bwd(res, g):
    a, b = res
    return _mxu(g, b, _NN), _mxu(g, a, _TN)


bdot_nt.defvjp(_nt_fwd, _nt_bwd)


@jax.custom_vjp
def bdot_tn(a, b):
    return _mxu(a, b, _TN)


def _tn_fwd(a, b):
    return _mxu(a, b, _TN), (a, b)


def _tn_bwd(res, g):
    a, b = res
    return _mxu(b, g, _NT), _mxu(a, g, _NN)


bdot_tn.defvjp(_tn_fwd, _tn_bwd)


def _doth(a, b, dims=_NN):
    return lax.dot_general(a, b, (dims, ((), ())), precision=lax.Precision.HIGHEST, preferred_element_type=F32)


def _rms(x):
    return x * lax.rsqrt(jnp.mean(x * x, axis=-1, keepdims=True) + EPS)


def _pick(n, cands):
    for t in cands:
        if n % t == 0:
            return t
    raise ValueError(f"no tile for {n}")


MM_TILE_M = (1024, 512, 256, 128, 64, 32, 16, 8)
MM_TILE_N = (2048, 1792, 1536, 1024, 768, 512, 256, 128)
MM_TILE_K = (1024, 512, 256, 128, 64, 32)

def matmul(a, b, *, mode, name, out_dtypes=(F32,), epi=None, extras=()):
    if mode == "nn":
        (M, K), (K2, N) = a.shape, b.shape
    elif mode == "nt":
        (M, K), (N, K2) = a.shape, b.shape
    else:
        (K, M), (K2, N) = a.shape, b.shape
    assert K == K2, (a.shape, b.shape, mode)
    tm = _pick(M, MM_TILE_M)
    tn = _pick(N, MM_TILE_N)
    tk = _pick(K, MM_TILE_K)
    nk = K // tk
    n_ex = len(extras)
    n_out = len(out_dtypes)
    dims = {"nn": _NN, "nt": _NT, "tn": _TN}[mode]

    def finish(r, ex_refs, out_refs):
        outs = epi(r, *[e[...] for e in ex_refs]) if epi is not None else (r,)
        for o_ref, o in zip(out_refs, outs):
            o_ref[...] = o.astype(o_ref.dtype)

    def body(*refs):
        a_ref, b_ref = refs[0], refs[1]
        ex_refs = refs[2:2 + n_ex]
        out_refs = refs[2 + n_ex:2 + n_ex + n_out]
        if nk == 1:
            finish(_mxu(a_ref[...], b_ref[...], dims), ex_refs, out_refs)
            return
        acc = refs[2 + n_ex + n_out]
        k = pl.program_id(2)

        @pl.when(k == 0)
        def _():
            acc[...] = jnp.zeros_like(acc)

        acc[...] += _mxu(a_ref[...], b_ref[...], dims)

        @pl.when(k == nk - 1)
        def _():
            finish(acc[...], ex_refs, out_refs)

    if mode == "nn":
        a_spec = pl.BlockSpec((tm, tk), lambda i, j, k: (i, k))
        b_spec = pl.BlockSpec((tk, tn), lambda i, j, k: (k, j))
    elif mode == "nt":
        a_spec = pl.BlockSpec((tm, tk), lambda i, j, k: (i, k))
        b_spec = pl.BlockSpec((tn, tk), lambda i, j, k: (j, k))
    else:
        a_spec = pl.BlockSpec((tk, tm), lambda i, j, k: (k, i))
        b_spec = pl.BlockSpec((tk, tn), lambda i, j, k: (k, j))
    o_spec = pl.BlockSpec((tm, tn), lambda i, j, k: (i, j))
    res = pl.pallas_call(
        body,
        name=name,
        grid=(M // tm, N // tn, nk),
        in_specs=[a_spec, b_spec] + [o_spec] * n_ex,
        out_specs=[o_spec] * n_out,
        out_shape=[jax.ShapeDtypeStruct((M, N), dt) for dt in out_dtypes],
        scratch_shapes=[pltpu.VMEM((tm, tn), F32)] if nk > 1 else [],
        compiler_params=_cparams(),
    )(a, b, *extras)
    return res[0] if n_out == 1 else tuple(res)


def small_call(fn, inputs, out_shapes, name):
    n_in = len(inputs)

    def body(*refs):
        outs = fn(*[r[...] for r in refs[:n_in]])
        for o_ref, o in zip(refs[n_in:], outs):
            o_ref[...] = o.astype(o_ref.dtype)

    res = pl.pallas_call(body, name=name, out_shape=list(out_shapes), compiler_params=_cparams())(*inputs)
    return tuple(res)


ROW_TILE = 512


def _modnorm(x, g, scale, shift):
    return _rms(x) * g * (1.0 + scale) + shift


def _row_spec(width):
    return pl.BlockSpec((1, ROW_TILE, width), lambda b, i: (b, i, 0))


def _mod_spec():
    return pl.BlockSpec((1, 1, D_MODEL), lambda b, i: (b, 0, 0))


def _vec_spec():
    return pl.BlockSpec((1, D_MODEL), lambda b, i: (0, 0))


def modnorm_fwd(x, g, scale, shift, name):
    B, S, D = x.shape

    def body(x_ref, g_ref, sc_ref, sh_ref, u_ref):
        u_ref[0] = _modnorm(x_ref[0], g_ref[...], sc_ref[0], sh_ref[0]).astype(BF16)

    return pl.pallas_call(
        body, name=name, grid=(B, S // ROW_TILE),
        in_specs=[_row_spec(D), _vec_spec(), _mod_spec(), _mod_spec()],
        out_specs=_row_spec(D),
        out_shape=jax.ShapeDtypeStruct((B, S, D), BF16),
        compiler_params=_cparams(),
    )(x, g, scale, shift)


def resid_modnorm_fwd(x, y, gate, g, scale, shift, name):
    B, S, D = x.shape

    def body(x_ref, y_ref, gt_ref, g_ref, sc_ref, sh_ref, x1_ref, u_ref):
        x1 = x_ref[0] + gt_ref[0] * y_ref[0]
        x1_ref[0] = x1
        u_ref[0] = _modnorm(x1, g_ref[...], sc_ref[0], sh_ref[0]).astype(BF16)

    return pl.pallas_call(
        body, name=name, grid=(B, S // ROW_TILE),
        in_specs=[_row_spec(D), _row_spec(D), _mod_spec(), _vec_spec(), _mod_spec(), _mod_spec()],
        out_specs=[_row_spec(D), _row_spec(D)],
        out_shape=[jax.ShapeDtypeStruct((B, S, D), F32), jax.ShapeDtypeStruct((B, S, D), BF16)],
        compiler_params=_cparams(),
    )(x, y, gate, g, scale, shift)


def resid_loss(x1, ffo, gate, target, name):
    B, S, D = x1.shape

    def body(x_ref, f_ref, gt_ref, t_ref, dx_ref, loss_ref):
        first = jnp.logical_and(pl.program_id(0) == 0, pl.program_id(1) == 0)

        @pl.when(first)
        def _():
            loss_ref[...] = jnp.zeros_like(loss_ref)

        err = x_ref[0] + gt_ref[0] * f_ref[0] - t_ref[0]
        dx_ref[0] = err * (1.0 / D)
        loss_ref[...] += 0.5 * jnp.sum(jnp.mean(err * err, axis=-1, keepdims=True), axis=0, keepdims=True)

    return pl.pallas_call(
        body, name=name, grid=(B, S // ROW_TILE),
        in_specs=[_row_spec(D), _row_spec(D), _mod_spec(), _row_spec(D)],
        out_specs=[_row_spec(D), pl.BlockSpec((1, 1), lambda b, i: (0, 0))],
        out_shape=[jax.ShapeDtypeStruct((B, S, D), F32), jax.ShapeDtypeStruct((1, 1), F32)],
        compiler_params=_cparams(),
    )(x1, ffo, gate, target)


def resid_bwd(dx, y, gate, name):
    B, S, D = dx.shape

    def body(dx_ref, y_ref, gt_ref, dy_ref, dg_ref):
        @pl.when(pl.program_id(1) == 0)
        def _():
            dg_ref[...] = jnp.zeros_like(dg_ref)

        d = dx_ref[0]
        dy_ref[0] = (gt_ref[0] * d).astype(BF16)
        dg_ref[0] += jnp.sum(d * y_ref[0], axis=0, keepdims=True)

    return pl.pallas_call(
        body, name=name, grid=(B, S // ROW_TILE),
        in_specs=[_row_spec(D), _row_spec(D), _mod_spec()],
        out_specs=[_row_spec(D), _mod_spec()],
        out_shape=[jax.ShapeDtypeStruct((B, S, D), BF16), jax.ShapeDtypeStruct((B, 1, D), F32)],
        compiler_params=_cparams(),
    )(dx, y, gate)


def modnorm_bwd(x, g, scale, shift, du, dx_res, name):
    B, S, D = x.shape

    def body(x_ref, g_ref, sc_ref, sh_ref, du_ref, dr_ref, dx_ref, dg_ref, dsc_ref, dsh_ref):
        first = jnp.logical_and(pl.program_id(0) == 0, pl.program_id(1) == 0)

        @pl.when(first)
        def _():
            dg_ref[...] = jnp.zeros_like(dg_ref)

        @pl.when(pl.program_id(1) == 0)
        def _():
            dsc_ref[...] = jnp.zeros_like(dsc_ref)
            dsh_ref[...] = jnp.zeros_like(dsh_ref)

        _, vjp = jax.vjp(_modnorm, x_ref[0], g_ref[...], sc_ref[0], sh_ref[0])
        dx, dg, dsc, dsh = vjp(du_ref[0].astype(F32))
        dx_ref[0] = dx + dr_ref[0]
        dg_ref[...] += dg
        dsc_ref[0] += dsc
        dsh_ref[0] += dsh

    return pl.pallas_call(
        body, name=name, grid=(B, S // ROW_TILE),
        in_specs=[_row_spec(D), _vec_spec(), _mod_spec(), _mod_spec(), _row_spec(D), _row_spec(D)],
        out_specs=[_row_spec(D), _vec_spec(), _mod_spec(), _mod_spec()],
        out_shape=[jax.ShapeDtypeStruct((B, S, D), F32), jax.ShapeDtypeStruct((1, D), F32),
                   jax.ShapeDtypeStruct((B, 1, D), F32), jax.ShapeDtypeStruct((B, 1, D), F32)],
        compiler_params=_cparams(),
    )(x, g, scale, shift, du, dx_res)


def _bucket_table(dilation):
    i = np.arange(ATT_BLOCK)[:, None]
    j = np.arange(2 * ATT_BLOCK)[None, :]
    delta = ATT_BLOCK + i - j
    dist = np.maximum(delta, 0) * dilation
    max_exact = N_BUCKETS // 2
    d = np.maximum(dist, max_exact).astype(np.float32)
    large = max_exact + (np.log(d / np.float32(max_exact)) / np.float32(math.log(MAX_DISTANCE / max_exact))
                         * np.float32(N_BUCKETS - max_exact)).astype(np.int32)
    large = np.minimum(large, N_BUCKETS - 1)
    return np.where(dist < max_exact, dist, large).astype(np.int32)


def _bucket_onehot(dilation):
    bt = jnp.asarray(_bucket_table(dilation).reshape(1, -1))
    return (bt == jnp.arange(N_BUCKETS, dtype=jnp.int32)[:, None]).astype(F32)


def bias_expand(rel_t, onehot, name):
    def fn(r, oh):
        return (_doth(r, oh),)
    return small_call(fn, [rel_t, onehot], [jax.ShapeDtypeStruct((rel_t.shape[0], onehot.shape[1]), F32)], name)[0]


def bias_reduce(dbias_flat, onehot, name):
    def fn(db, oh):
        return (_doth(db, oh, _NT),)
    return small_call(fn, [dbias_flat, onehot], [jax.ShapeDtypeStruct((dbias_flat.shape[0], N_BUCKETS), F32)], name)[0]


def _attn_tile(q, k, v, bias, qg, kg):
    qn = _rms(q) * qg
    kn = _rms(k) * kg
    s = bdot_nt(qn, kn) * (HEAD_DIM ** -0.5) + bias
    kl = k.shape[0]
    i = lax.broadcasted_iota(jnp.int32, (ATT_BLOCK, kl), 0)
    j = lax.broadcasted_iota(jnp.int32, (ATT_BLOCK, kl), 1) + (2 * ATT_BLOCK - kl)
    valid = jnp.logical_and(j >= i, j <= i + ATT_BLOCK)
    s = jnp.where(valid, s, -jnp.inf)
    m = jnp.max(s, axis=-1, keepdims=True)
    p = jnp.exp(s - m)
    l = jnp.sum(p, axis=-1, keepdims=True)
    o = bdot_nn(p, v) / l
    lse = jnp.broadcast_to(m + jnp.log(l), (ATT_BLOCK, HEAD_DIM))
    return o, lse


def _attn_tiles(dilation):
    nb = SEQ // dilation // ATT_BLOCK
    return [(r, n) for r in range(dilation) for n in range(nb)]


def _attn_rows(r, n, dilation, nblk=1):
    if dilation == 1:
        return pl.ds(r + n * ATT_BLOCK, nblk * ATT_BLOCK)
    return pl.ds(r + n * ATT_BLOCK * dilation, nblk * ATT_BLOCK, stride=dilation)


_QL, _KL, _VL = slice(0, 128), slice(128, 256), slice(256, 384)


def _qkv_specs(hb):
    return [pl.BlockSpec((None, SEQ, HEAD_DIM), functools.partial(lambda b, h, j: (b, 0, 3 * (hb + h) + j), j=j))
            for j in range(3)]


def attn_fwd(pa, bias, qg, kg, group, name):
    B = pa.shape[0]
    dilation = ATT_GROUPS[group][1]
    hb = group * 4

    def body(q_ref, k_ref, v_ref, b_ref, qg_ref, kg_ref, o_ref, l_ref):
        qg_, kg_ = qg_ref[...], kg_ref[...]
        for (r, n) in _attn_tiles(dilation):
            rows = _attn_rows(r, n, dilation)
            q = q_ref[rows, :]
            if n == 0:
                krows, bias_t = rows, b_ref[0, :, ATT_BLOCK:]
            else:
                krows, bias_t = _attn_rows(r, n - 1, dilation, 2), b_ref[0]
            o, lse = _attn_tile(q, k_ref[krows, :], v_ref[krows, :], bias_t, qg_, kg_)
            o_ref[rows, :] = o
            l_ref[rows, :] = lse

    head_out = pl.BlockSpec((None, SEQ, HEAD_DIM), lambda b, h: (b, 0, h))
    return pl.pallas_call(
        body, name=name, grid=(B, 4),
        in_specs=_qkv_specs(hb) + [
                  pl.BlockSpec((1, ATT_BLOCK, 2 * ATT_BLOCK), lambda b, h: (hb + h, 0, 0)),
                  pl.BlockSpec((1, HEAD_DIM), lambda b, h: (0, 0)),
                  pl.BlockSpec((1, HEAD_DIM), lambda b, h: (0, 0))],
        out_specs=[head_out, head_out],
        out_shape=[jax.ShapeDtypeStruct((B, SEQ, 512), F32), jax.ShapeDtypeStruct((B, SEQ, 512), F32)],
        compiler_params=_cparams(),
    )(pa, pa, pa, bias, qg, kg)


def attn_bwd(pa, bias, qg, kg, do, dlse, dpa, group, name):
    B = pa.shape[0]
    dilation = ATT_GROUPS[group][1]
    hb = group * 4

    def body(q_ref, k_ref, v_ref, b_ref, qg_ref, kg_ref, do_ref, dl_ref, dpa_in,
             dp_ref, db_ref, dqg_ref, dkg_ref, dq_s, dk_s, dv_s):
        del dpa_in
        h_id = pl.program_id(1)

        @pl.when(jnp.logical_and(pl.program_id(0) == 0, h_id == 0))
        def _():
            db_ref[...] = jnp.zeros_like(db_ref)
            dqg_ref[...] = jnp.zeros_like(dqg_ref)
            dkg_ref[...] = jnp.zeros_like(dkg_ref)

        dk_s[...] = jnp.zeros_like(dk_s)
        dv_s[...] = jnp.zeros_like(dv_s)
        qg_, kg_ = qg_ref[...], kg_ref[...]
        for (r, n) in _attn_tiles(dilation):
            rows = _attn_rows(r, n, dilation)
            q = q_ref[rows, :]
            if n == 0:
                krows, bias_t = rows, b_ref[0, :, ATT_BLOCK:]
            else:
                krows, bias_t = _attn_rows(r, n - 1, dilation, 2), b_ref[0]
            _, vjp = jax.vjp(_attn_tile, q, k_ref[krows, :], v_ref[krows, :], bias_t, qg_, kg_)
            dq, dk, dv, dbias, dqg, dkg = vjp((do_ref[rows, :], dl_ref[rows, :]))
            dq_s[rows, :] = dq
            dk_s[krows, :] += dk
            dv_s[krows, :] += dv
            if n == 0:
                db_ref[h_id, :, ATT_BLOCK:] += dbias
            else:
                db_ref[h_id] += dbias
            dqg_ref[...] += dqg
            dkg_ref[...] += dkg
        dp_ref[0, :, _QL] = dq_s[...].astype(BF16)
        dp_ref[0, :, _KL] = dk_s[...].astype(BF16)
        dp_ref[0, :, _VL] = dv_s[...].astype(BF16)

    const2 = lambda b, h: (0, 0)
    head_in = pl.BlockSpec((None, SEQ, HEAD_DIM), lambda b, h: (b, 0, h))
    head_blk = pl.BlockSpec((1, SEQ, ATT_HEAD_COLS), lambda b, h: (b, 0, hb + h))
    return pl.pallas_call(
        body, name=name, grid=(B, 4),
        in_specs=_qkv_specs(hb) + [
                  pl.BlockSpec((1, ATT_BLOCK, 2 * ATT_BLOCK), lambda b, h: (hb + h, 0, 0)),
                  pl.BlockSpec((1, HEAD_DIM), const2), pl.BlockSpec((1, HEAD_DIM), const2),
                  head_in, head_in,
                  pl.BlockSpec(memory_space=pl.ANY)],
        out_specs=[head_blk,
                   pl.BlockSpec((4, ATT_BLOCK, 2 * ATT_BLOCK), lambda b, h: (0, 0, 0)),
                   pl.BlockSpec((1, HEAD_DIM), const2), pl.BlockSpec((1, HEAD_DIM), const2)],
        out_shape=[jax.ShapeDtypeStruct(dpa.shape, BF16),
                   jax.ShapeDtypeStruct((4, ATT_BLOCK, 2 * ATT_BLOCK), F32),
                   jax.ShapeDtypeStruct((1, HEAD_DIM), F32), jax.ShapeDtypeStruct((1, HEAD_DIM), F32)],
        scratch_shapes=[pltpu.VMEM((SEQ, HEAD_DIM), F32)] * 3,
        input_output_aliases={8: 0},
        compiler_params=_cparams(),
    )(pa, pa, pa, bias, qg, kg, do, dlse, dpa)


def _merge(o0, o1, o2, l0, l1, l2):
    mx = jnp.maximum(jnp.maximum(l0, l1), l2)
    e0, e1, e2 = jnp.exp(l0 - mx), jnp.exp(l1 - mx), jnp.exp(l2 - mx)
    den = e0 + e1 + e2
    return (e0 / den) * o0 + (e1 / den) * o1 + (e2 / den) * o2


def merge_fwd(os_, ls_, name):
    B = os_[0].shape[0]

    def body(o0, o1, o2, l0, l1, l2, a_ref):
        a_ref[0] = _merge(o0[0], o1[0], o2[0], l0[0], l1[0], l2[0]).astype(BF16)

    return pl.pallas_call(
        body, name=name, grid=(B, SEQ // ROW_TILE),
        in_specs=[_row_spec(512)] * 6, out_specs=_row_spec(512),
        out_shape=jax.ShapeDtypeStruct((B, SEQ, 512), BF16),
        compiler_params=_cparams(),
    )(*os_, *ls_)


def merge_bwd(os_, ls_, datt, name):
    B = os_[0].shape[0]

    def body(o0, o1, o2, l0, l1, l2, da_ref, *outs):
        _, vjp = jax.vjp(_merge, o0[0], o1[0], o2[0], l0[0], l1[0], l2[0])
        for o_ref, g in zip(outs, vjp(da_ref[0])):
            o_ref[0] = g

    return pl.pallas_call(
        body, name=name, grid=(B, SEQ // ROW_TILE),
        in_specs=[_row_spec(512)] * 7, out_specs=[_row_spec(512)] * 6,
        out_shape=[jax.ShapeDtypeStruct((B, SEQ, 512), F32)] * 6,
        compiler_params=_cparams(),
    )(*os_, *ls_, datt)


def _gate_mix(ga, gm, ya, ym):
    return jax.nn.sigmoid(ga) * ya + jax.nn.sigmoid(gm) * ym


def gate_fwd(pg, ya, ym, name):
    B = pg.shape[0]

    def body(ga, gm, ya_ref, ym_ref, z_ref):
        z_ref[0] = _gate_mix(ga[0], gm[0], ya_ref[0], ym_ref[0]).astype(BF16)

    return pl.pallas_call(
        body, name=name, grid=(B, SEQ // ROW_TILE),
        in_specs=[pl.BlockSpec((1, ROW_TILE, D_MODEL), lambda b, i: (b, i, 0)),
                  pl.BlockSpec((1, ROW_TILE, D_MODEL), lambda b, i: (b, i, 1)),
                  _row_spec(D_MODEL), _row_spec(D_MODEL)],
        out_specs=_row_spec(D_MODEL),
        out_shape=jax.ShapeDtypeStruct((B, SEQ, D_MODEL), BF16),
        compiler_params=_cparams(),
    )(pg, pg, ya, ym)


def gate_bwd(pg, ya, ym, dz, name):
    B = pg.shape[0]

    def body(ga, gm, ya_ref, ym_ref, dz_ref, dpg_ref, dya_ref, dym_ref):
        _, vjp = jax.vjp(_gate_mix, ga[0], gm[0], ya_ref[0], ym_ref[0])
        dga, dgm, dya, dym = vjp(dz_ref[0])
        dpg_ref[0, :, :D_MODEL] = dga.astype(BF16)
        dpg_ref[0, :, D_MODEL:] = dgm.astype(BF16)
        dya_ref[0] = dya.astype(BF16)
        dym_ref[0] = dym.astype(BF16)

    return pl.pallas_call(
        body, name=name, grid=(B, SEQ // ROW_TILE),
        in_specs=[pl.BlockSpec((1, ROW_TILE, D_MODEL), lambda b, i: (b, i, 0)),
                  pl.BlockSpec((1, ROW_TILE, D_MODEL), lambda b, i: (b, i, 1)),
                  _row_spec(D_MODEL), _row_spec(D_MODEL), _row_spec(D_MODEL)],
        out_specs=[_row_spec(GATE_COLS), _row_spec(D_MODEL), _row_spec(D_MODEL)],
        out_shape=[jax.ShapeDtypeStruct((B, SEQ, GATE_COLS), BF16),
                   jax.ShapeDtypeStruct((B, SEQ, D_MODEL), BF16), jax.ShapeDtypeStruct((B, SEQ, D_MODEL), BF16)],
        compiler_params=_cparams(),
    )(pg, pg, ya, ym, dz)


def _log_sigmoid(x):
    return jnp.minimum(x, 0.0) - jnp.log(1.0 + jnp.exp(-jnp.abs(x)))


def _lane(x, idx):
    lanes = lax.broadcasted_iota(jnp.int32, x.shape, 1)
    return jnp.sum(jnp.where(lanes == idx, x, 0.0), axis=1, keepdims=True)


def _head_mask(e):
    lane = lax.broadcasted_iota(jnp.int32, (1, 128), 1)
    return jnp.logical_and(lane >= e * 64, lane < (e + 1) * 64).astype(F32)


def _mlstm_gates(ifb, m, e):
    L = ML_CHUNK
    li = _lane(ifb, e)
    ri = lax.broadcasted_iota(jnp.int32, (L, L), 0)
    ci = lax.broadcasted_iota(jnp.int32, (L, L), 1)
    causal = ri >= ci
    cs = _doth(causal.astype(F32), _log_sigmoid(ifb))
    b = _lane(cs, 2 + e)
    last = lax.broadcasted_iota(jnp.int32, (L, 1), 0) == L - 1
    b_end = jnp.sum(jnp.where(last, b, 0.0), axis=0, keepdims=True)
    g = b_end - b + li
    m_new = jnp.maximum(b_end + m, jnp.max(g, axis=0, keepdims=True))
    decay = jnp.exp(b_end + m - m_new)
    wk = jnp.exp(g - m_new)
    return li, b, causal, (ri == ci).astype(F32), m_new, decay, wk


def _mlstm_state(k2, v, ifb, C, n, m, *, e):
    _, _, _, _, m_new, decay, wk = _mlstm_gates(ifb, m, e)
    kw = wk * (k2 * _head_mask(e))
    return decay * C + bdot_tn(kw, v), decay * n + jnp.sum(kw, axis=0, keepdims=True), m_new


def _mlstm_step(q2, k2, v, mo, ifb, gn, C, n, m, *, e):
    L = ML_CHUNK
    hm = _head_mask(e)
    q, k = q2 * hm, k2 * hm
    li, b, causal, eye, m_new, decay, wk = _mlstm_gates(ifb, m, e)
    a = b + m
    rrow = _doth(jnp.ones((L, L), F32), eye * (li - b))
    Dm = jnp.where(causal, b + rrow, -jnp.inf)
    m_t = jnp.maximum(a, jnp.max(Dm, axis=1, keepdims=True))
    w = bdot_nt(q, k) * jnp.exp(Dm - m_t)
    inter = jnp.exp(a - m_t)
    num = bdot_nn(w, v) + inter * bdot_nn(q, C)
    nq = jnp.sum(w, axis=1, keepdims=True) + inter * jnp.sum(q * n, axis=1, keepdims=True)
    h = num / jnp.maximum(jnp.abs(nq), jnp.exp(-m_t))
    kw = wk * k
    C_new = decay * C + bdot_tn(kw, v)
    n_new = decay * n + jnp.sum(kw, axis=0, keepdims=True)
    hg = _rms(h) * gn * jax.nn.sigmoid(mo)
    return hg, C_new, n_new, m_new


def _shift_down(x, s):
    if s == 0:
        return x
    rows = lax.broadcasted_iota(jnp.int32, x.shape, 0)
    return jnp.where(rows >= s, pltpu.roll(x, s, 0), 0.0)


def _shift_up(x, s):
    if s == 0:
        return x
    S = x.shape[0]
    rows = lax.broadcasted_iota(jnp.int32, x.shape, 0)
    return jnp.where(rows < S - s, pltpu.roll(x, S - s, 0), 0.0)


def _conv_pre(x, cw, cb):
    y = cb + cw[3:4, :] * x
    for j in range(3):
        y = y + cw[j:j + 1, :] * _shift_down(x, 3 - j)
    return y


def _conv_bwd(x, cw, dpre):
    dx = cw[3:4, :] * dpre
    dcw = [None] * 4
    dcw[3] = jnp.sum(dpre * x, axis=0, keepdims=True)
    for j in range(3):
        dx = dx + cw[j:j + 1, :] * _shift_up(dpre, 3 - j)
        dcw[j] = jnp.sum(dpre * _shift_down(x, 3 - j), axis=0, keepdims=True)
    return dx, dcw, jnp.sum(dpre, axis=0, keepdims=True)


def _silu(z):
    return z * jax.nn.sigmoid(z)


def _dsilu(z):
    s = jax.nn.sigmoid(z)
    return s * (1.0 + z * (1.0 - s))


_ML_Q, _ML_K = slice(0, 128), slice(128, 256)
_ML_IF = slice(768, 896)


def _ml_v(e):
    return slice(256 + e * 128, 384 + e * 128)


def _ml_o(e):
    return slice(512 + e * 128, 640 + e * 128)


def _ml_specs():
    pair = lambda b, p: (b, 0, p)
    return [pl.BlockSpec((1, SEQ, ML_PAIR_COLS), pair),
            pl.BlockSpec((1, 4, 128), lambda b, p: (p, 0, 0)),
            pl.BlockSpec((1, 4, 128), lambda b, p: (4 + p, 0, 0)),
            pl.BlockSpec((1, 1, 128), lambda b, p: (p, 0, 0)),
            pl.BlockSpec((1, 1, 128), lambda b, p: (4 + p, 0, 0)),
            pl.BlockSpec((1, 1, 128), lambda b, p: (p, 0, 0)),
            pl.BlockSpec((1, 1, 256), lambda b, p: (p, 0, 0))]


def mlstm_fwd(pm, cw8, cb8, bifp, gn4, name):
    B = pm.shape[0]

    def body(p_ref, cwq, cwk, cbq, cbk, bif_ref, gn_ref, hg_ref, qc_s, kc_s, C_s, n_s, m_s):
        qc_s[...] = _silu(_conv_pre(p_ref[0, :, _ML_Q], cwq[0], cbq[0]))
        kc_s[...] = _silu(_conv_pre(p_ref[0, :, _ML_K], cwk[0], cbk[0])) * (64 ** -0.5)
        C_s[...] = jnp.zeros_like(C_s)
        n_s[...] = jnp.zeros_like(n_s)
        m_s[...] = jnp.zeros_like(m_s)

        def chunk(ci, carry):
            rows = pl.ds(pl.multiple_of(ci * ML_CHUNK, ML_CHUNK), ML_CHUNK)
            q2, k2 = qc_s[rows, :], kc_s[rows, :]
            ifb = p_ref[0, rows, _ML_IF] + bif_ref[0]
            for e in range(2):
                hg, C, n, m = _mlstm_step(q2, k2, p_ref[0, rows, _ml_v(e)], p_ref[0, rows, _ml_o(e)], ifb,
                                          gn_ref[0, :, e * 128:(e + 1) * 128], C_s[e], n_s[e], m_s[e], e=e)
                hg_ref[0, rows, e * 128:(e + 1) * 128] = hg.astype(BF16)
                C_s[e], n_s[e], m_s[e] = C, n, m
            return carry

        lax.fori_loop(0, N_CHUNKS, chunk, 0, unroll=ML_UNROLL)

    return pl.pallas_call(
        body, name=name, grid=(B, ML_PAIRS),
        in_specs=_ml_specs(),
        out_specs=pl.BlockSpec((1, SEQ, 256), lambda b, p: (b, 0, p)),
        out_shape=jax.ShapeDtypeStruct((B, SEQ, D_MODEL), BF16),
        scratch_shapes=[pltpu.VMEM((SEQ, 128), F32), pltpu.VMEM((SEQ, 128), F32),
                        pltpu.VMEM((2, 128, 128), F32), pltpu.VMEM((2, 1, 128), F32), pltpu.VMEM((2, 1, 1), F32)],
        compiler_params=_cparams(),
    )(pm, cw8, cw8, cb8, cb8, bifp, gn4)


def mlstm_bwd(pm, cw8, cb8, bifp, gn4, dhg, name):
    B = pm.shape[0]

    def body(p_ref, cwq, cwk, cbq, cbk, bif_ref, gn_ref, dh_ref,
             dp_ref, dcw_ref, dcb_ref, dbif_ref, dgn_ref,
             qc_s, kc_s, dq_s, dk_s, Cs, ns, ms, dC_s, dn_s, dm_s):
        p_id = pl.program_id(1)

        @pl.when(jnp.logical_and(pl.program_id(0) == 0, p_id == 0))
        def _():
            dcw_ref[...] = jnp.zeros_like(dcw_ref)
            dcb_ref[...] = jnp.zeros_like(dcb_ref)
            dbif_ref[...] = jnp.zeros_like(dbif_ref)
            dgn_ref[...] = jnp.zeros_like(dgn_ref)

        qc_s[...] = _silu(_conv_pre(p_ref[0, :, _ML_Q], cwq[0], cbq[0]))
        kc_s[...] = _silu(_conv_pre(p_ref[0, :, _ML_K], cwk[0], cbk[0])) * (64 ** -0.5)

        Cs[0] = jnp.zeros_like(Cs[0])
        ns[0] = jnp.zeros_like(ns[0])
        ms[0] = jnp.zeros_like(ms[0])

        def fwd_chunk(ci, carry):
            rows = pl.ds(pl.multiple_of(ci * ML_CHUNK, ML_CHUNK), ML_CHUNK)
            k2 = kc_s[rows, :]
            ifb = p_ref[0, rows, _ML_IF] + bif_ref[0]
            for e in range(2):
                C, n, m = _mlstm_state(k2, p_ref[0, rows, _ml_v(e)], ifb, Cs[ci, e], ns[ci, e], ms[ci, e], e=e)
                Cs[ci + 1, e], ns[ci + 1, e], ms[ci + 1, e] = C, n, m
            return carry

        lax.fori_loop(0, N_CHUNKS, fwd_chunk, 0, unroll=ML_UNROLL)

        dC_s[...] = jnp.zeros_like(dC_s)
        dn_s[...] = jnp.zeros_like(dn_s)
        dm_s[...] = jnp.zeros_like(dm_s)

        def bwd_chunk(t, carry):
            ci = N_CHUNKS - 1 - t
            rows = pl.ds(pl.multiple_of(ci * ML_CHUNK, ML_CHUNK), ML_CHUNK)
            q2, k2 = qc_s[rows, :], kc_s[rows, :]
            ifb = p_ref[0, rows, _ML_IF] + bif_ref[0]
            dq2 = jnp.zeros((ML_CHUNK, 128), F32)
            dk2 = jnp.zeros((ML_CHUNK, 128), F32)
            difb = jnp.zeros((ML_CHUNK, 128), F32)
            for e in range(2):
                lanes = slice(e * 128, (e + 1) * 128)
                _, vjp = jax.vjp(functools.partial(_mlstm_step, e=e), q2, k2, p_ref[0, rows, _ml_v(e)],
                                 p_ref[0, rows, _ml_o(e)], ifb, gn_ref[0, :, lanes], Cs[ci, e], ns[ci, e], ms[ci, e])
                g = vjp((dh_ref[0, rows, lanes], dC_s[e], dn_s[e], dm_s[e]))
                dq2, dk2, difb = dq2 + g[0], dk2 + g[1], difb + g[4]
                dp_ref[0, rows, _ml_v(e)] = g[2].astype(BF16)
                dp_ref[0, rows, _ml_o(e)] = g[3].astype(BF16)
                dgn_ref[p_id, :, lanes] += g[5]
                dC_s[e], dn_s[e], dm_s[e] = g[6], g[7], g[8]
            dq_s[rows, :] = dq2
            dk_s[rows, :] = dk2
            dp_ref[0, rows, _ML_IF] = difb.astype(BF16)
            dbif_ref[p_id] += jnp.sum(difb, axis=0, keepdims=True)
            return carry

        lax.fori_loop(0, N_CHUNKS, bwd_chunk, 0, unroll=ML_UNROLL)

        for (sl, cw, cb, d_s, blk, scale) in ((_ML_Q, cwq, cbq, dq_s, p_id, 1.0), (_ML_K, cwk, cbk, dk_s, 4 + p_id, 64 ** -0.5)):
            xr = p_ref[0, :, sl]
            dpre = d_s[...] * scale * _dsilu(_conv_pre(xr, cw[0], cb[0]))
            dx, dcw, dcb = _conv_bwd(xr, cw[0], dpre)
            dp_ref[0, :, sl] = dx.astype(BF16)
            for j in range(4):
                dcw_ref[blk, j:j + 1, :] += dcw[j]
            dcb_ref[blk] += dcb

    full3 = lambda b, p: (0, 0, 0)
    return pl.pallas_call(
        body, name=name, grid=(B, ML_PAIRS),
        in_specs=_ml_specs() + [pl.BlockSpec((1, SEQ, 256), lambda b, p: (b, 0, p))],
        out_specs=[pl.BlockSpec((1, SEQ, ML_PAIR_COLS), lambda b, p: (b, 0, p)),
                   pl.BlockSpec((8, 4, 128), full3), pl.BlockSpec((8, 1, 128), full3),
                   pl.BlockSpec((4, 1, 128), full3), pl.BlockSpec((4, 1, 256), full3)],
        out_shape=[jax.ShapeDtypeStruct((B, SEQ, ML_COLS), BF16),
                   jax.ShapeDtypeStruct((8, 4, 128), F32), jax.ShapeDtypeStruct((8, 1, 128), F32),
                   jax.ShapeDtypeStruct((4, 1, 128), F32), jax.ShapeDtypeStruct((4, 1, 256), F32)],
        scratch_shapes=[pltpu.VMEM((SEQ, 128), F32)] * 4 + [
            pltpu.VMEM((N_CHUNKS + 1, 2, 128, 128), F32), pltpu.VMEM((N_CHUNKS + 1, 2, 1, 128), F32),
            pltpu.VMEM((N_CHUNKS + 1, 2, 1, 1), F32),
            pltpu.VMEM((2, 128, 128), F32), pltpu.VMEM((2, 1, 128), F32), pltpu.VMEM((2, 1, 1), F32)],
        compiler_params=_cparams(),
    )(pm, cw8, cw8, cb8, cb8, bifp, gn4, dhg)


def _adamw(w, g, m, v):
    m = ADAM_B1 * m + (1.0 - ADAM_B1) * g
    v = ADAM_B2 * v + (1.0 - ADAM_B2) * (g * g)
    m_hat = m / (1.0 - ADAM_B1 ** ADAM_STEP)
    v_hat = v / (1.0 - ADAM_B2 ** ADAM_STEP)
    delta = -ADAM_LR * (m_hat / (jnp.sqrt(v_hat) + ADAM_EPS) + ADAM_WD * w)
    return delta, m, v


def adamw(w, g, m, v, name, parts=False):
    R, C = w.shape
    tr = _pick(R, (256, 128, 64, 32, 16, 8, 4, 2, 1)) if R * C * 4 > (1 << 20) else R
    spec = pl.BlockSpec((tr, C), lambda i: (i, 0))
    g_spec = pl.BlockSpec((N_DEV, tr, C), lambda i: (0, i, 0)) if parts else spec

    def body(w_ref, g_ref, m_ref, v_ref, go_ref, d_ref, mo_ref, vo_ref):
        if parts:
            g = g_ref[0].astype(F32)
            for k in range(1, N_DEV):
                g = g + g_ref[k].astype(F32)
        else:
            g = g_ref[...]
        d, mn, vn = _adamw(w_ref[...], g, m_ref[...], v_ref[...])
        go_ref[...], d_ref[...], mo_ref[...], vo_ref[...] = g, d, mn, vn

    return pl.pallas_call(
        body, name=name, grid=(R // tr,),
        in_specs=[spec, g_spec, spec, spec], out_specs=[spec] * 4,
        out_shape=[jax.ShapeDtypeStruct((R, C), F32)] * 4,
        compiler_params=_cparams(),
    )(w, g, m, v)


def _mesh_pos():
    return lax.axis_index("x"), lax.axis_index("y"), lax.axis_index("c")


def _flip(pos, f):
    x, y, c = pos
    return (1 - x if f & 4 else x, 1 - y if f & 2 else y, 1 - c if f & 1 else c)


def _index(pos):
    return 4 * pos[0] + 2 * pos[1] + pos[2]


def _exchange(arrs, name, scatter):
    n = len(arrs)

    def body(*refs):
        ins, outs = refs[:n], refs[n:2 * n]
        send, recv, lsem = refs[2 * n:]
        me = _mesh_pos()
        mine = _index(me)
        copies = []
        for i in range(n):
            src = ins[i].at[mine] if scatter else ins[i]
            loc = pltpu.make_async_copy(src, outs[i].at[mine], lsem.at[i])
            loc.start()
            copies.append(loc)
            for f in range(1, N_DEV):
                peer = _flip(me, f)
                src = ins[i].at[_index(peer)] if scatter else ins[i]
                cp = pltpu.make_async_remote_copy(
                    src_ref=src, dst_ref=outs[i].at[mine],
                    send_sem=send.at[i * 7 + f - 1], recv_sem=recv.at[i * 7 + f - 1],
                    device_id=peer, device_id_type=pl.DeviceIdType.MESH)
                cp.start()
                copies.append(cp)
        for cp in copies:
            cp.wait()

    any_spec = pl.BlockSpec(memory_space=pl.ANY)
    out_shape = [jax.ShapeDtypeStruct(a.shape if scatter else (N_DEV,) + a.shape, a.dtype) for a in arrs]
    res = pl.pallas_call(
        body, name=name,
        in_specs=[any_spec] * n, out_specs=[any_spec] * n, out_shape=out_shape,
        scratch_shapes=[pltpu.SemaphoreType.DMA((7 * n,)), pltpu.SemaphoreType.DMA((7 * n,)),
                        pltpu.SemaphoreType.DMA((n,))],
        compiler_params=_cparams(),
    )(*arrs)
    return list(res)


def all_gather(arrs, name):
    return _exchange(arrs, name, False)


def all_to_all(arrs, name):
    return _exchange(arrs, name, True)


def cast_bf16(arrs, name):
    outs = []
    for i, a in enumerate(arrs):
        R, C = a.shape
        tr = _pick(R, (256, 128, 64, 32, 16, 8)) if R * C * 4 > (1 << 21) else R
        spec = pl.BlockSpec((tr, C), lambda i: (i, 0))

        def body(a_ref, o_ref):
            o_ref[...] = a_ref[...].astype(BF16)

        outs.append(pl.pallas_call(body, name=f"{name}_{i}", grid=(R // tr,), in_specs=[spec], out_specs=spec,
                                   out_shape=jax.ShapeDtypeStruct((R, C), BF16), compiler_params=_cparams())(a))
    return outs


def sum_parts(parts, name):
    def fn(p):
        g = p[0]
        for k in range(1, N_DEV):
            g = g + p[k]
        return (g,)
    return small_call(fn, [parts], [jax.ShapeDtypeStruct(parts.shape[1:], F32)], name)[0]


_SPLITS = np.cumsum([1536, 1536, 1536, 512, 512, 1024, 1024, 8, 8, 2048])[:-1].tolist()


def split_w_in(w):
    aq, ak, av, mq, mk, mv, mo, mi, mf, gates = jnp.split(w, _SPLITS, axis=1)
    R = w.shape[0]
    w_att = jnp.stack([aq.reshape(R, 12, 128), ak.reshape(R, 12, 128), av.reshape(R, 12, 128)], axis=2)
    gif = jnp.concatenate([mi.reshape(R, 4, 2), mf.reshape(R, 4, 2), jnp.zeros((R, 4, 124), w.dtype)], axis=2)
    w_ml = jnp.concatenate([mq.reshape(R, 4, 128), mk.reshape(R, 4, 128), mv.reshape(R, 4, 256),
                            mo.reshape(R, 4, 256), gif], axis=2)
    return w_att.reshape(R, ATT_COLS), w_ml.reshape(R, ML_COLS), gates


def merge_w_in(g_att, g_ml, g_gate):
    R = g_att.shape[0]
    a = g_att.reshape(R, 12, 3, 128)
    m = g_ml.reshape(R, 4, ML_PAIR_COLS)
    gif = m[:, :, 768:772]
    return jnp.concatenate([
        a[:, :, 0].reshape(R, 1536), a[:, :, 1].reshape(R, 1536), a[:, :, 2].reshape(R, 1536),
        m[:, :, 0:128].reshape(R, 512), m[:, :, 128:256].reshape(R, 512),
        m[:, :, 256:512].reshape(R, 1024), m[:, :, 512:768].reshape(R, 1024),
        gif[:, :, 0:2].reshape(R, 8), gif[:, :, 2:4].reshape(R, 8), g_gate], axis=1)


def _blk8(v, width=128):
    r = v.shape[0]
    return v.reshape(r, 1024 // width, width).transpose(1, 0, 2)


def _unblk8(v):
    nb, r, w = v.shape
    return v.transpose(1, 0, 2).reshape(r, nb * w)


def local_step(x, target, mods, w, small):
    B = x.shape[0]
    T = B * SEQ
    shift1, scale1, gate1, shift2, scale2, gate2 = mods
    f2 = lambda a: a.reshape(T, a.shape[-1])
    f3 = lambda a: a.reshape(B, SEQ, a.shape[-1])

    rel_t = jnp.pad(small["rel_bias"].T, ((0, 4), (0, 0)))
    onehots = [_bucket_onehot(d) for _, d in ATT_GROUPS]
    biases = [bias_expand(rel_t, oh, f"bias_expand{g}").reshape(16, ATT_BLOCK, 2 * ATT_BLOCK)
              for g, oh in enumerate(onehots)]
    qg, kg = small["q_norm_g"], small["k_norm_g"]
    cw8 = _blk8(small["conv_w"])
    cb8 = _blk8(small["conv_b"])
    b_if = small["b_if"].reshape(2, 4, 2)
    bifp = jnp.concatenate([b_if[0], b_if[1], jnp.zeros((4, 124), F32)], axis=1).reshape(4, 1, 128)
    gn4 = small["mlstm_norm_g"].reshape(4, 1, 256)

    u = modnorm_fwd(x, small["norm1_g"], scale1, shift1, "modnorm1")
    u2d = f2(u)
    pa = f3(matmul(u2d, w["w_att"], mode="nn", name="proj_att"))
    pm = f3(matmul(u2d, w["w_ml"], mode="nn", name="proj_ml"))
    pg = f3(matmul(u2d, w["w_gate"], mode="nn", name="proj_gate"))
    os_, ls_ = [], []
    for g in range(3):
        o, l = attn_fwd(pa, biases[g], qg, kg, g, f"attn_fwd{g}")
        os_.append(o)
        ls_.append(l)
    att = merge_fwd(os_, ls_, "merge_fwd")
    y_att = matmul(f2(att), w["w_att_out"], mode="nn", name="att_out")
    hg = mlstm_fwd(pm, cw8, cb8, bifp, gn4, "mlstm_fwd")
    y_ml = matmul(f2(hg), w["w_ml_out"], mode="nn", name="ml_out")
    z = gate_fwd(pg, f3(y_att), f3(y_ml), "gate_fwd")
    y = matmul(f2(z), w["w_out"], mode="nn", name="out_proj")
    x1, u2 = resid_modnorm_fwd(x, f3(y), gate1, small["norm2_g"], scale2, shift2, "resid_modnorm2")
    pre, hdn = matmul(f2(u2), w["w_ff1"], mode="nn", name="ff1", out_dtypes=(F32, BF16),
                      epi=lambda acc: (acc, jnp.square(jnp.maximum(acc, 0.0))))
    ffo = matmul(hdn, w["w_ff2"], mode="nn", name="ff2")
    dx2, loss = resid_loss(x1, f3(ffo), gate2, target, "resid_loss")

    d_ffo, d_gate2 = resid_bwd(dx2, f3(ffo), gate2, "resid_bwd2")
    g_ff2 = matmul(hdn, f2(d_ffo), mode="tn", name="g_ff2", out_dtypes=(BF16,))
    d_pre = matmul(f2(d_ffo), w["w_ff2"], mode="nt", name="d_hdn", out_dtypes=(BF16,), extras=(pre,),
                   epi=lambda acc, p: (acc * (2.0 * jnp.maximum(p, 0.0)),))
    g_ff1 = matmul(f2(u2), d_pre, mode="tn", name="g_ff1", out_dtypes=(BF16,))
    du2 = matmul(d_pre, w["w_ff1"], mode="nt", name="d_u2")
    dx1, d_norm2, d_scale2, d_shift2 = modnorm_bwd(x1, small["norm2_g"], scale2, shift2, f3(du2), dx2, "modnorm_bwd2")
    dy, d_gate1 = resid_bwd(dx1, f3(y), gate1, "resid_bwd1")
    g_out = matmul(f2(z), f2(dy), mode="tn", name="g_out", out_dtypes=(BF16,))
    dz = matmul(f2(dy), w["w_out"], mode="nt", name="d_z")
    dpg, d_ya, d_ym = gate_bwd(pg, f3(y_att), f3(y_ml), f3(dz), "gate_bwd")
    g_att_out = matmul(f2(att), f2(d_ya), mode="tn", name="g_att_out", out_dtypes=(BF16,))
    d_att = matmul(f2(d_ya), w["w_att_out"], mode="nt", name="d_att")
    g_ml_out = matmul(f2(hg), f2(d_ym), mode="tn", name="g_ml_out", out_dtypes=(BF16,))
    d_hg = matmul(f2(d_ym), w["w_ml_out"], mode="nt", name="d_hg")
    dmerge = merge_bwd(os_, ls_, f3(d_att), "merge_bwd")
    dpa = jnp.zeros((B, SEQ, ATT_COLS), BF16)
    d_rel = []
    d_qg = d_kg = None
    for g in range(3):
        dpa, dbias, dq_g, dk_g = attn_bwd(pa, biases[g], qg, kg, dmerge[g], dmerge[3 + g], dpa, g, f"attn_bwd{g}")
        db8 = jnp.pad(dbias.reshape(4, -1), ((0, 4), (0, 0)))
        d_rel.append(bias_reduce(db8, onehots[g], f"bias_reduce{g}")[:4])
        d_qg = dq_g if d_qg is None else d_qg + dq_g
        d_kg = dk_g if d_kg is None else d_kg + dk_g
    dpm, dcw8, dcb8, dbifp, dgn4 = mlstm_bwd(pm, cw8, cb8, bifp, gn4, f3(d_hg), "mlstm_bwd")
    g_w_att = matmul(u2d, f2(dpa), mode="tn", name="g_w_att", out_dtypes=(BF16,))
    g_w_ml = matmul(u2d, f2(dpm), mode="tn", name="g_w_ml", out_dtypes=(BF16,))
    g_w_gate = matmul(u2d, f2(dpg), mode="tn", name="g_w_gate", out_dtypes=(BF16,))
    du = matmul(f2(dpa), w["w_att"], mode="nt", name="d_u_att")
    du = matmul(f2(dpm), w["w_ml"], mode="nt", name="d_u_ml", extras=(du,), epi=lambda acc, e: (acc + e,))
    du = matmul(f2(dpg), w["w_gate"], mode="nt", name="d_u_gate", extras=(du,), epi=lambda acc, e: (acc + e,))
    grad_x, d_norm1, d_scale1, d_shift1 = modnorm_bwd(x, small["norm1_g"], scale1, shift1, f3(du), dx1, "modnorm_bwd1")

    d_mods = (d_shift1, d_scale1, d_gate1, d_shift2, d_scale2, d_gate2)
    big = dict(w_in=merge_w_in(g_w_att, g_w_ml, g_w_gate), w_att_out=g_att_out, w_ml_out=g_ml_out, w_out=g_out,
               w_ff1=g_ff1, w_ff2=g_ff2)
    dbif = dbifp.reshape(4, 128)
    small_g = dict(
        norm1_g=d_norm1, norm2_g=d_norm2,
        b_if=jnp.stack([dbif[:, 0:2].reshape(8), dbif[:, 2:4].reshape(8)]),
        conv_w=_unblk8(dcw8), conv_b=_unblk8(dcb8), q_norm_g=d_qg, k_norm_g=d_kg,
        rel_bias=jnp.concatenate(d_rel, axis=0).T,
        mlstm_norm_g=dgn4.reshape(1, 1024))
    return loss, grad_x, d_mods, big, small_g


_SMALL = (("b_ada", 6144), ("norm1_g", 1024), ("norm2_g", 1024), ("b_if", 16), ("conv_b", 1024),
          ("q_norm_g", 128), ("k_norm_g", 128), ("rel_bias", 384), ("mlstm_norm_g", 1024), ("conv_w", 4096))
_SMALL_ROWS = 120
_REPL = _SMALL[:-1]
_REPL_ROWS = 86


def _pack(d, names, rows):
    flat = jnp.concatenate([d[k].reshape(-1) for k, _ in names])
    return jnp.pad(flat, (0, rows * 128 - flat.shape[0])).reshape(rows, 128)


def _unpack(slab, names, shapes):
    flat = slab.reshape(-1)
    out, off = {}, 0
    for k, nel in names:
        out[k] = flat[off:off + nel].reshape(shapes[k])
        off += nel
    return out


def kernel(x, c, w_ada, b_ada, norm1_g, norm2_g, w_in, b_if, conv_w, conv_b, q_norm_g, k_norm_g, rel_bias, mlstm_norm_g, w_att_out, w_ml_out, w_out, w_ff1, w_ff2, loss_target, m_w_ada, m_b_ada, m_norm1_g, m_norm2_g, m_w_in, m_b_if, m_conv_w, m_conv_b, m_q_norm_g, m_k_norm_g, m_rel_bias, m_mlstm_norm_g, m_w_att_out, m_w_ml_out, m_w_out, m_w_ff1, m_w_ff2, v_w_ada, v_b_ada, v_norm1_g, v_norm2_g, v_w_in, v_b_if, v_conv_w, v_conv_b, v_q_norm_g, v_k_norm_g, v_rel_bias, v_mlstm_norm_g, v_w_att_out, v_w_ml_out, v_w_out, v_w_ff1, v_w_ff2):
    P = dict(w_ada=w_ada, b_ada=b_ada, norm1_g=norm1_g, norm2_g=norm2_g, w_in=w_in, b_if=b_if, conv_w=conv_w,
             conv_b=conv_b, q_norm_g=q_norm_g, k_norm_g=k_norm_g, rel_bias=rel_bias, mlstm_norm_g=mlstm_norm_g,
             w_att_out=w_att_out, w_ml_out=w_ml_out, w_out=w_out, w_ff1=w_ff1, w_ff2=w_ff2)
    M = dict(w_ada=m_w_ada, b_ada=m_b_ada, norm1_g=m_norm1_g, norm2_g=m_norm2_g, w_in=m_w_in, b_if=m_b_if,
             conv_w=m_conv_w, conv_b=m_conv_b, q_norm_g=m_q_norm_g, k_norm_g=m_k_norm_g, rel_bias=m_rel_bias,
             mlstm_norm_g=m_mlstm_norm_g, w_att_out=m_w_att_out, w_ml_out=m_w_ml_out, w_out=m_w_out,
             w_ff1=m_w_ff1, w_ff2=m_w_ff2)
    V = dict(w_ada=v_w_ada, b_ada=v_b_ada, norm1_g=v_norm1_g, norm2_g=v_norm2_g, w_in=v_w_in, b_if=v_b_if,
             conv_w=v_conv_w, conv_b=v_conv_b, q_norm_g=v_q_norm_g, k_norm_g=v_k_norm_g, rel_bias=v_rel_bias,
             mlstm_norm_g=v_mlstm_norm_g, w_att_out=v_w_att_out, w_ml_out=v_w_ml_out, w_out=v_w_out,
             w_ff1=v_w_ff1, w_ff2=v_w_ff2)
    names = list(P)
    shapes = {k: P[k].shape for k in names}
    B = x.shape[0]
    me = _index(_mesh_pos())

    big_names = ("w_in", "w_att_out", "w_ml_out", "w_out", "w_ff1", "w_ff2")
    shards = cast_bf16([P[k][0] for k in big_names], "cast_w")
    gathered = all_gather(shards + [c, conv_w[0]], "gather_weights")
    gw = dict(zip(big_names, gathered[:6]))
    c_all = gathered[6].reshape(N_DEV * B, D_MODEL)
    conv_w_full = gathered[7].transpose(1, 0, 2).reshape(4, 1024)
    w_in_full = gw["w_in"].transpose(1, 0, 2).reshape(D_MODEL, D_IN)
    w_att, w_ml, w_gate = split_w_in(w_in_full)
    w = dict(w_att=w_att, w_ml=w_ml, w_gate=w_gate,
             w_att_out=gw["w_att_out"].transpose(1, 0, 2).reshape(512, D_MODEL),
             w_ml_out=gw["w_ml_out"].reshape(D_MODEL, D_MODEL), w_out=gw["w_out"].reshape(D_MODEL, D_MODEL),
             w_ff1=gw["w_ff1"].transpose(1, 0, 2).reshape(D_MODEL, D_FF), w_ff2=gw["w_ff2"].reshape(D_FF, D_MODEL))

    (silu_c,) = small_call(lambda a: (_silu(a),), [c_all], [jax.ShapeDtypeStruct(c_all.shape, F32)], "silu_c")
    b_ada_cols = lax.dynamic_slice(b_ada, (0, me * 768), (1, 768))
    ada_cols = matmul(silu_c, w_ada[0], mode="nn", name="ada", extras=(jnp.broadcast_to(b_ada_cols, (N_DEV * B, 768)),),
                      epi=lambda acc, bb: (acc + bb,))
    (ada_t,) = all_to_all([ada_cols.reshape(N_DEV, B, 768)], "ada_exchange")
    ada = ada_t.transpose(1, 0, 2).reshape(B, 6 * D_MODEL)
    mods = tuple(ada[:, i * D_MODEL:(i + 1) * D_MODEL].reshape(B, 1, D_MODEL) for i in range(6))

    small = dict(norm1_g=norm1_g, norm2_g=norm2_g, b_if=b_if[0], conv_w=conv_w_full, conv_b=conv_b,
                 q_norm_g=q_norm_g, k_norm_g=k_norm_g, rel_bias=rel_bias, mlstm_norm_g=mlstm_norm_g)
    loss, grad_x, d_mods, big, small_g = local_step(x, loss_target, mods, w, small)
    loss = lax.psum(loss[0, 0], ("x", "y", "c"))

    d_ada = jnp.concatenate([d.reshape(B, D_MODEL) for d in d_mods], axis=1)
    (d_ada_t,) = all_to_all([d_ada.reshape(B, N_DEV, 768).transpose(1, 0, 2)], "d_ada_exchange")
    d_ada_cols = d_ada_t.reshape(N_DEV * B, 768)
    g_w_ada = matmul(silu_c, d_ada_cols, mode="tn", name="g_w_ada")
    (g_b_cols,) = small_call(lambda a: (jnp.sum(a, axis=0, keepdims=True),), [d_ada_cols],
                             [jax.ShapeDtypeStruct((1, 768), F32)], "g_b_ada_cols")
    small_g["b_ada"] = lax.dynamic_update_slice(jnp.zeros((1, 6144), F32), g_b_cols, (0, me * 768))

    send = [
        big["w_in"].reshape(D_MODEL, N_DEV, W_IN_SHARD).transpose(1, 0, 2),
        big["w_att_out"].reshape(512, N_DEV, 128).transpose(1, 0, 2),
        big["w_ml_out"].reshape(N_DEV, 128, D_MODEL),
        big["w_out"].reshape(N_DEV, 128, D_MODEL),
        big["w_ff1"].reshape(D_MODEL, N_DEV, 512).transpose(1, 0, 2),
        big["w_ff2"].reshape(N_DEV, 512, D_MODEL),
    ]
    recv = all_to_all(send, "grad_exchange")
    (small_parts,) = all_gather([_pack(small_g, _SMALL, _SMALL_ROWS)], "small_grad_gather")
    small_sum = sum_parts(small_parts, "small_grad_sum")
    sg = _unpack(small_sum, _SMALL, {**{k: shapes[k] for k, _ in _REPL}, "conv_w": (4, 1024)})

    G, Dl, NM, NV = {}, {}, {}, {}
    for k, parts in zip(big_names, recv):
        g, d, nm, nv = adamw(P[k][0], parts, M[k][0], V[k][0], f"adamw_{k}", parts=True)
        G[k], Dl[k], NM[k], NV[k] = g[None], d[None], nm[None], nv[None]
    g, d, nm, nv = adamw(w_ada[0], g_w_ada, m_w_ada[0], v_w_ada[0], "adamw_w_ada")
    G["w_ada"], Dl["w_ada"], NM["w_ada"], NV["w_ada"] = g[None], d[None], nm[None], nv[None]
    g_conv = lax.dynamic_slice(sg["conv_w"], (0, me * 128), (4, 128))
    g, d, nm, nv = adamw(conv_w[0], g_conv, m_conv_w[0], v_conv_w[0], "adamw_conv_w")
    G["conv_w"], Dl["conv_w"], NM["conv_w"], NV["conv_w"] = g[None], d[None], nm[None], nv[None]
    gslab = _pack(sg, _REPL, _REPL_ROWS)
    _, d, nm, nv = adamw(_pack(P, _REPL, _REPL_ROWS), gslab, _pack(M, _REPL, _REPL_ROWS), _pack(V, _REPL, _REPL_ROWS),
                         "adamw_small")
    rs = {k: shapes[k] for k, _ in _REPL}
    d, nm, nv = _unpack(d, _REPL, rs), _unpack(nm, _REPL, rs), _unpack(nv, _REPL, rs)
    for k, _ in _REPL:
        G[k], Dl[k], NM[k], NV[k] = sg[k], d[k], nm[k], nv[k]

    return (loss, grad_x, *[G[k] for k in names], *[Dl[k] for k in names], *[NM[k] for k in names],
            *[NV[k] for k in names])
```

```python
import functools
import math

import numpy as np
import jax
import jax.numpy as jnp
from jax import lax
from jax.experimental import pallas as pl
from jax.experimental.pallas import tpu as pltpu

F32 = jnp.float32
BF16 = jnp.bfloat16

N_DEV = 8
D_MODEL = 1024
SEQ = 2048
ATT_GROUPS = ((128, 1), (512, 4), (2048, 16))
N_ATT_HEADS = 12
ATT_BLOCK = 128
HEAD_DIM = 128
ML_HEADS = 8
ML_PAIRS = 4
ML_CHUNK = 64
N_CHUNKS = SEQ // ML_CHUNK
N_BUCKETS = 32
MAX_DISTANCE = 2048
D_FF = 4096
D_IN = 9744
EPS = 1e-6

ADAM_LR = 0.001
ADAM_B1 = 0.9
ADAM_B2 = 0.999
ADAM_EPS = 1e-08
ADAM_WD = 0.01
ADAM_STEP = 10

ATT_HEAD_COLS = 3 * HEAD_DIM
ATT_COLS = N_ATT_HEADS * ATT_HEAD_COLS
ML_PAIR_COLS = 896
ML_COLS = ML_PAIRS * ML_PAIR_COLS
GATE_COLS = 2 * D_MODEL
W_IN_SHARD = D_IN // N_DEV

VMEM_LIMIT = 56 * 1024 * 1024


def _cparams(**kw):
    return pltpu.CompilerParams(vmem_limit_bytes=VMEM_LIMIT, **kw)


_NN = ((1,), (0,))
_NT = ((1,), (1,))
_TN = ((0,), (0,))


def _mxu(a, b, dims):
    return lax.dot_general(a.astype(BF16), b.astype(BF16), (dims, ((), ())), preferred_element_type=F32)


@jax.custom_vjp
def bdot_nn(a, b):
    return _mxu(a, b, _NN)


def _nn_fwd(a, b):
    return _mxu(a, b, _NN), (a, b)


def _nn_bwd(res, g):
    a, b = res
    return _mxu(g, b, _NT), _mxu(a, g, _TN)


bdot_nn.defvjp(_nn_fwd, _nn_bwd)


@jax.custom_vjp
def bdot_nt(a, b):
    return _mxu(a, b, _NT)


def _nt_fwd(a, b):
    return _mxu(a, b, _NT), (a, b)


def _nt_bwd(res, g):
    a, b = res
    return _mxu(g, b, _NN), _mxu(g, a, _TN)


bdot_nt.defvjp(_nt_fwd, _nt_bwd)


@jax.custom_vjp
def bdot_tn(a, b):
    return _mxu(a, b, _TN)


def _tn_fwd(a, b):
    return _mxu(a, b, _TN), (a, b)


def _tn_bwd(res, g):
    a, b = res
    return _mxu(b, g, _NT), _mxu(a, g, _NN)


bdot_tn.defvjp(_tn_fwd, _tn_bwd)


def _doth(a, b, dims=_NN):
    return lax.dot_general(a, b, (dims, ((), ())), precision=lax.Precision.HIGHEST, preferred_element_type=F32)


def _rms(x):
    return x * lax.rsqrt(jnp.mean(x * x, axis=-1, keepdims=True) + EPS)


def _pick(n, cands):
    for t in cands:
        if n % t == 0:
            return t
    raise ValueError(f"no tile for {n}")


MM_TILE_M = (1024, 512, 256, 128, 64, 32, 16, 8)
MM_TILE_N = (2048, 1792, 1536, 1024, 768, 512, 256, 128)
MM_TILE_K = (1024, 512, 256, 128, 64, 32)

def matmul(a, b, *, mode, name, out_dtypes=(F32,), epi=None, extras=()):
    if mode == "nn":
        (M, K), (K2, N) = a.shape, b.shape
    elif mode == "nt":
        (M, K), (N, K2) = a.shape, b.shape
    else:
        (K, M), (K2, N) = a.shape, b.shape
    assert K == K2, (a.shape, b.shape, mode)
    tm = _pick(M, MM_TILE_M)
    tn = _pick(N, MM_TILE_N)
    tk = _pick(K, MM_TILE_K)
    nk = K // tk
    n_ex = len(extras)
    n_out = len(out_dtypes)
    dims = {"nn": _NN, "nt": _NT, "tn": _TN}[mode]

    def finish(r, ex_refs, out_refs):
        outs = epi(r, *[e[...] for e in ex_refs]) if epi is not None else (r,)
        for o_ref, o in zip(out_refs, outs):
            o_ref[...] = o.astype(o_ref.dtype)

    def body(*refs):
        a_ref, b_ref = refs[0], refs[1]
        ex_refs = refs[2:2 + n_ex]
        out_refs = refs[2 + n_ex:2 + n_ex + n_out]
        if nk == 1:
            finish(_mxu(a_ref[...], b_ref[...], dims), ex_refs, out_refs)
            return
        acc = refs[2 + n_ex + n_out]
        k = pl.program_id(2)

        @pl.when(k == 0)
        def _():
            acc[...] = jnp.zeros_like(acc)

        acc[...] += _mxu(a_ref[...], b_ref[...], dims)

        @pl.when(k == nk - 1)
        def _():
            finish(acc[...], ex_refs, out_refs)

    if mode == "nn":
        a_spec = pl.BlockSpec((tm, tk), lambda i, j, k: (i, k))
        b_spec = pl.BlockSpec((tk, tn), lambda i, j, k: (k, j))
    elif mode == "nt":
        a_spec = pl.BlockSpec((tm, tk), lambda i, j, k: (i, k))
        b_spec = pl.BlockSpec((tn, tk), lambda i, j, k: (j, k))
    else:
        a_spec = pl.BlockSpec((tk, tm), lambda i, j, k: (k, i))
        b_spec = pl.BlockSpec((tk, tn), lambda i, j, k: (k, j))
    o_spec = pl.BlockSpec((tm, tn), lambda i, j, k: (i, j))
    res = pl.pallas_call(
        body,
        name=name,
        grid=(M // tm, N // tn, nk),
        in_specs=[a_spec, b_spec] + [o_spec] * n_ex,
        out_specs=[o_spec] * n_out,
        out_shape=[jax.ShapeDtypeStruct((M, N), dt) for dt in out_dtypes],
        scratch_shapes=[pltpu.VMEM((tm, tn), F32)] if nk > 1 else [],
        compiler_params=_cparams(),
    )(a, b, *extras)
    return res[0] if n_out == 1 else tuple(res)


def small_call(fn, inputs, out_shapes, name):
    n_in = len(inputs)

    def body(*refs):
        outs = fn(*[r[...] for r in refs[:n_in]])
        for o_ref, o in zip(refs[n_in:], outs):
            o_ref[...] = o.astype(o_ref.dtype)

    res = pl.pallas_call(body, name=name, out_shape=list(out_shapes), compiler_params=_cparams())(*inputs)
    return tuple(res)


ROW_TILE = 512


def _modnorm(x, g, scale, shift):
    return _rms(x) * g * (1.0 + scale) + shift


def _row_spec(width):
    return pl.BlockSpec((1, ROW_TILE, width), lambda b, i: (b, i, 0))


def _mod_spec():
    return pl.BlockSpec((1, 1, D_MODEL), lambda b, i: (b, 0, 0))


def _vec_spec():
    return pl.BlockSpec((1, D_MODEL), lambda b, i: (0, 0))


def modnorm_fwd(x, g, scale, shift, name):
    B, S, D = x.shape

    def body(x_ref, g_ref, sc_ref, sh_ref, u_ref):
        u_ref[0] = _modnorm(x_ref[0], g_ref[...], sc_ref[0], sh_ref[0]).astype(BF16)

    return pl.pallas_call(
        body, name=name, grid=(B, S // ROW_TILE),
        in_specs=[_row_spec(D), _vec_spec(), _mod_spec(), _mod_spec()],
        out_specs=_row_spec(D),
        out_shape=jax.ShapeDtypeStruct((B, S, D), BF16),
        compiler_params=_cparams(),
    )(x, g, scale, shift)


def resid_modnorm_fwd(x, y, gate, g, scale, shift, name):
    B, S, D = x.shape

    def body(x_ref, y_ref, gt_ref, g_ref, sc_ref, sh_ref, x1_ref, u_ref):
        x1 = x_ref[0] + gt_ref[0] * y_ref[0]
        x1_ref[0] = x1
        u_ref[0] = _modnorm(x1, g_ref[...], sc_ref[0], sh_ref[0]).astype(BF16)

    return pl.pallas_call(
        body, name=name, grid=(B, S // ROW_TILE),
        in_specs=[_row_spec(D), _row_spec(D), _mod_spec(), _vec_spec(), _mod_spec(), _mod_spec()],
        out_specs=[_row_spec(D), _row_spec(D)],
        out_shape=[jax.ShapeDtypeStruct((B, S, D), F32), jax.ShapeDtypeStruct((B, S, D), BF16)],
        compiler_params=_cparams(),
    )(x, y, gate, g, scale, shift)


def resid_loss(x1, ffo, gate, target, name):
    B, S, D = x1.shape

    def body(x_ref, f_ref, gt_ref, t_ref, dx_ref, loss_ref):
        first = jnp.logical_and(pl.program_id(0) == 0, pl.program_id(1) == 0)

        @pl.when(first)
        def _():
            loss_ref[...] = jnp.zeros_like(loss_ref)

        err = x_ref[0] + gt_ref[0] * f_ref[0] - t_ref[0]
        dx_ref[0] = err * (1.0 / D)
        loss_ref[...] += 0.5 * jnp.sum(jnp.mean(err * err, axis=-1, keepdims=True), axis=0, keepdims=True)

    return pl.pallas_call(
        body, name=name, grid=(B, S // ROW_TILE),
        in_specs=[_row_spec(D), _row_spec(D), _mod_spec(), _row_spec(D)],
        out_specs=[_row_spec(D), pl.BlockSpec((1, 1), lambda b, i: (0, 0))],
        out_shape=[jax.ShapeDtypeStruct((B, S, D), F32), jax.ShapeDtypeStruct((1, 1), F32)],
        compiler_params=_cparams(),
    )(x1, ffo, gate, target)


def resid_bwd(dx, y, gate, name):
    B, S, D = dx.shape

    def body(dx_ref, y_ref, gt_ref, dy_ref, dg_ref):
        @pl.when(pl.program_id(1) == 0)
        def _():
            dg_ref[...] = jnp.zeros_like(dg_ref)

        d = dx_ref[0]
        dy_ref[0] = (gt_ref[0] * d).astype(BF16)
        dg_ref[0] += jnp.sum(d * y_ref[0], axis=0, keepdims=True)

    return pl.pallas_call(
        body, name=name, grid=(B, S // ROW_TILE),
        in_specs=[_row_spec(D), _row_spec(D), _mod_spec()],
        out_specs=[_row_spec(D), _mod_spec()],
        out_shape=[jax.ShapeDtypeStruct((B, S, D), BF16), jax.ShapeDtypeStruct((B, 1, D), F32)],
        compiler_params=_cparams(),
    )(dx, y, gate)


def modnorm_bwd(x, g, scale, shift, du, dx_res, name):
    B, S, D = x.shape

    def body(x_ref, g_ref, sc_ref, sh_ref, du_ref, dr_ref, dx_ref, dg_ref, dsc_ref, dsh_ref):
        first = jnp.logical_and(pl.program_id(0) == 0, pl.program_id(1) == 0)

        @pl.when(first)
        def _():
            dg_ref[...] = jnp.zeros_like(dg_ref)

        @pl.when(pl.program_id(1) == 0)
        def _():
            dsc_ref[...] = jnp.zeros_like(dsc_ref)
            dsh_ref[...] = jnp.zeros_like(dsh_ref)

        _, vjp = jax.vjp(_modnorm, x_ref[0], g_ref[...], sc_ref[0], sh_ref[0])
        dx, dg, dsc, dsh = vjp(du_ref[0].astype(F32))
        dx_ref[0] = dx + dr_ref[0]
        dg_ref[...] += dg
        dsc_ref[0] += dsc
        dsh_ref[0] += dsh

    return pl.pallas_call(
        body, name=name, grid=(B, S // ROW_TILE),
        in_specs=[_row_spec(D), _vec_spec(), _mod_spec(), _mod_spec(), _row_spec(D), _row_spec(D)],
        out_specs=[_row_spec(D), _vec_spec(), _mod_spec(), _mod_spec()],
        out_shape=[jax.ShapeDtypeStruct((B, S, D), F32), jax.ShapeDtypeStruct((1, D), F32),
                   jax.ShapeDtypeStruct((B, 1, D), F32), jax.ShapeDtypeStruct((B, 1, D), F32)],
        compiler_params=_cparams(),
    )(x, g, scale, shift, du, dx_res)


def _bucket_table(dilation):
    i = np.arange(ATT_BLOCK)[:, None]
    j = np.arange(2 * ATT_BLOCK)[None, :]
    delta = ATT_BLOCK + i - j
    dist = np.maximum(delta, 0) * dilation
    max_exact = N_BUCKETS // 2
    d = np.maximum(dist, max_exact).astype(np.float32)
    large = max_exact + (np.log(d / np.float32(max_exact)) / np.float32(math.log(MAX_DISTANCE / max_exact))
                         * np.float32(N_BUCKETS - max_exact)).astype(np.int32)
    large = np.minimum(large, N_BUCKETS - 1)
    return np.where(dist < max_exact, dist, large).astype(np.int32)


def _bucket_onehot(dilation):
    bt = jnp.asarray(_bucket_table(dilation).reshape(1, -1))
    return (bt == jnp.arange(N_BUCKETS, dtype=jnp.int32)[:, None]).astype(F32)


def bias_expand(rel_t, onehot, name):
    def fn(r, oh):
        return (_doth(r, oh),)
    return small_call(fn, [rel_t, onehot], [jax.ShapeDtypeStruct((rel_t.shape[0], onehot.shape[1]), F32)], name)[0]


def bias_reduce(dbias_flat, onehot, name):
    def fn(db, oh):
        return (_doth(db, oh, _NT),)
    return small_call(fn, [dbias_flat, onehot], [jax.ShapeDtypeStruct((dbias_flat.shape[0], N_BUCKETS), F32)], name)[0]


def _attn_tile(q, k, v, bias, qg, kg):
    qn = _rms(q) * qg
    kn = _rms(k) * kg
    s = bdot_nt(qn, kn) * (HEAD_DIM ** -0.5) + bias
    kl = k.shape[0]
    i = lax.broadcasted_iota(jnp.int32, (ATT_BLOCK, kl), 0)
    j = lax.broadcasted_iota(jnp.int32, (ATT_BLOCK, kl), 1) + (2 * ATT_BLOCK - kl)
    valid = jnp.logical_and(j >= i, j <= i + ATT_BLOCK)
    s = jnp.where(valid, s, -jnp.inf)
    m = jnp.max(s, axis=-1, keepdims=True)
    p = jnp.exp(s - m)
    l = jnp.sum(p, axis=-1, keepdims=True)
    o = bdot_nn(p, v) / l
    lse = jnp.broadcast_to(m + jnp.log(l), (ATT_BLOCK, HEAD_DIM))
    return o, lse


def _attn_tiles(dilation):
    nb = SEQ // dilation // ATT_BLOCK
    return [(r, n) for r in range(dilation) for n in range(nb)]


def _attn_rows(r, n, dilation, nblk=1):
    if dilation == 1:
        return pl.ds(r + n * ATT_BLOCK, nblk * ATT_BLOCK)
    return pl.ds(r + n * ATT_BLOCK * dilation, nblk * ATT_BLOCK, stride=dilation)


_QL, _KL, _VL = slice(0, 128), slice(128, 256), slice(256, 384)


def _qkv_specs(hb):
    return [pl.BlockSpec((None, SEQ, HEAD_DIM), functools.partial(lambda b, h, j: (b, 0, 3 * (hb + h) + j), j=j))
            for j in range(3)]


def attn_fwd(pa, bias, qg, kg, group, name):
    B = pa.shape[0]
    dilation = ATT_GROUPS[group][1]
    hb = group * 4

    def body(q_ref, k_ref, v_ref, b_ref, qg_ref, kg_ref, o_ref, l_ref):
        qg_, kg_ = qg_ref[...], kg_ref[...]
        for (r, n) in _attn_tiles(dilation):
            rows = _attn_rows(r, n, dilation)
            q = q_ref[rows, :]
            if n == 0:
                krows, bias_t = rows, b_ref[0, :, ATT_BLOCK:]
            else:
                krows, bias_t = _attn_rows(r, n - 1, dilation, 2), b_ref[0]
            o, lse = _attn_tile(q, k_ref[krows, :], v_ref[krows, :], bias_t, qg_, kg_)
            o_ref[rows, :] = o
            l_ref[rows, :] = lse

    head_out = pl.BlockSpec((None, SEQ, HEAD_DIM), lambda b, h: (b, 0, h))
    return pl.pallas_call(
        body, name=name, grid=(B, 4),
        in_specs=_qkv_specs(hb) + [
                  pl.BlockSpec((1, ATT_BLOCK, 2 * ATT_BLOCK), lambda b, h: (hb + h, 0, 0)),
                  pl.BlockSpec((1, HEAD_DIM), lambda b, h: (0, 0)),
                  pl.BlockSpec((1, HEAD_DIM), lambda b, h: (0, 0))],
        out_specs=[head_out, head_out],
        out_shape=[jax.ShapeDtypeStruct((B, SEQ, 512), F32), jax.ShapeDtypeStruct((B, SEQ, 512), F32)],
        compiler_params=_cparams(),
    )(pa, pa, pa, bias, qg, kg)


def attn_bwd(pa, bias, qg, kg, do, dlse, dpa, group, name):
    B = pa.shape[0]
    dilation = ATT_GROUPS[group][1]
    hb = group * 4

    def body(q_ref, k_ref, v_ref, b_ref, qg_ref, kg_ref, do_ref, dl_ref, dpa_in,
             dp_ref, db_ref, dqg_ref, dkg_ref, dq_s, dk_s, dv_s):
        del dpa_in
        h_id = pl.program_id(1)

        @pl.when(jnp.logical_and(pl.program_id(0) == 0, h_id == 0))
        def _():
            db_ref[...] = jnp.zeros_like(db_ref)
            dqg_ref[...] = jnp.zeros_like(dqg_ref)
            dkg_ref[...] = jnp.zeros_like(dkg_ref)

        dk_s[...] = jnp.zeros_like(dk_s)
        dv_s[...] = jnp.zeros_like(dv_s)
        qg_, kg_ = qg_ref[...], kg_ref[...]
        for (r, n) in _attn_tiles(dilation):
            rows = _attn_rows(r, n, dilation)
            q = q_ref[rows, :]
            if n == 0:
                krows, bias_t = rows, b_ref[0, :, ATT_BLOCK:]
            else:
                krows, bias_t = _attn_rows(r, n - 1, dilation, 2), b_ref[0]
            _, vjp = jax.vjp(_attn_tile, q, k_ref[krows, :], v_ref[krows, :], bias_t, qg_, kg_)
            dq, dk, dv, dbias, dqg, dkg = vjp((do_ref[rows, :], dl_ref[rows, :]))
            dq_s[rows, :] = dq
            dk_s[krows, :] += dk
            dv_s[krows, :] += dv
            if n == 0:
                db_ref[h_id, :, ATT_BLOCK:] += dbias
            else:
                db_ref[h_id] += dbias
            dqg_ref[...] += dqg
            dkg_ref[...] += dkg
        dp_ref[0, :, _QL] = dq_s[...].astype(BF16)
        dp_ref[0, :, _KL] = dk_s[...].astype(BF16)
        dp_ref[0, :, _VL] = dv_s[...].astype(BF16)

    const2 = lambda b, h: (0, 0)
    head_in = pl.BlockSpec((None, SEQ, HEAD_DIM), lambda b, h: (b, 0, h))
    head_blk = pl.BlockSpec((1, SEQ, ATT_HEAD_COLS), lambda b, h: (b, 0, hb + h))
    return pl.pallas_call(
        body, name=name, grid=(B, 4),
        in_specs=_qkv_specs(hb) + [
                  pl.BlockSpec((1, ATT_BLOCK, 2 * ATT_BLOCK), lambda b, h: (hb + h, 0, 0)),
                  pl.BlockSpec((1, HEAD_DIM), const2), pl.BlockSpec((1, HEAD_DIM), const2),
                  head_in, head_in,
                  pl.BlockSpec(memory_space=pl.ANY)],
        out_specs=[head_blk,
                   pl.BlockSpec((4, ATT_BLOCK, 2 * ATT_BLOCK), lambda b, h: (0, 0, 0)),
                   pl.BlockSpec((1, HEAD_DIM), const2), pl.BlockSpec((1, HEAD_DIM), const2)],
        out_shape=[jax.ShapeDtypeStruct(dpa.shape, BF16),
                   jax.ShapeDtypeStruct((4, ATT_BLOCK, 2 * ATT_BLOCK), F32),
                   jax.ShapeDtypeStruct((1, HEAD_DIM), F32), jax.ShapeDtypeStruct((1, HEAD_DIM), F32)],
        scratch_shapes=[pltpu.VMEM((SEQ, HEAD_DIM), F32)] * 3,
        input_output_aliases={8: 0},
        compiler_params=_cparams(),
    )(pa, pa, pa, bias, qg, kg, do, dlse, dpa)


def _merge(o0, o1, o2, l0, l1, l2):
    mx = jnp.maximum(jnp.maximum(l0, l1), l2)
    e0, e1, e2 = jnp.exp(l0 - mx), jnp.exp(l1 - mx), jnp.exp(l2 - mx)
    den = e0 + e1 + e2
    return (e0 / den) * o0 + (e1 / den) * o1 + (e2 / den) * o2


def merge_fwd(os_, ls_, name):
    B = os_[0].shape[0]

    def body(o0, o1, o2, l0, l1, l2, a_ref):
        a_ref[0] = _merge(o0[0], o1[0], o2[0], l0[0], l1[0], l2[0]).astype(BF16)

    return pl.pallas_call(
        body, name=name, grid=(B, SEQ // ROW_TILE),
        in_specs=[_row_spec(512)] * 6, out_specs=_row_spec(512),
        out_shape=jax.ShapeDtypeStruct((B, SEQ, 512), BF16),
        compiler_params=_cparams(),
    )(*os_, *ls_)


def merge_bwd(os_, ls_, datt, name):
    B = os_[0].shape[0]

    def body(o0, o1, o2, l0, l1, l2, da_ref, *outs):
        _, vjp = jax.vjp(_merge, o0[0], o1[0], o2[0], l0[0], l1[0], l2[0])
        for o_ref, g in zip(outs, vjp(da_ref[0])):
            o_ref[0] = g

    return pl.pallas_call(
        body, name=name, grid=(B, SEQ // ROW_TILE),
        in_specs=[_row_spec(512)] * 7, out_specs=[_row_spec(512)] * 6,
        out_shape=[jax.ShapeDtypeStruct((B, SEQ, 512), F32)] * 6,
        compiler_params=_cparams(),
    )(*os_, *ls_, datt)


def _gate_mix(ga, gm, ya, ym):
    return jax.nn.sigmoid(ga) * ya + jax.nn.sigmoid(gm) * ym


def gate_fwd(pg, ya, ym, name):
    B = pg.shape[0]

    def body(ga, gm, ya_ref, ym_ref, z_ref):
        z_ref[0] = _gate_mix(ga[0], gm[0], ya_ref[0], ym_ref[0]).astype(BF16)

    return pl.pallas_call(
        body, name=name, grid=(B, SEQ // ROW_TILE),
        in_specs=[pl.BlockSpec((1, ROW_TILE, D_MODEL), lambda b, i: (b, i, 0)),
                  pl.BlockSpec((1, ROW_TILE, D_MODEL), lambda b, i: (b, i, 1)),
                  _row_spec(D_MODEL), _row_spec(D_MODEL)],
        out_specs=_row_spec(D_MODEL),
        out_shape=jax.ShapeDtypeStruct((B, SEQ, D_MODEL), BF16),
        compiler_params=_cparams(),
    )(pg, pg, ya, ym)


def gate_bwd(pg, ya, ym, dz, name):
    B = pg.shape[0]

    def body(ga, gm, ya_ref, ym_ref, dz_ref, dpg_ref, dya_ref, dym_ref):
        _, vjp = jax.vjp(_gate_mix, ga[0], gm[0], ya_ref[0], ym_ref[0])
        dga, dgm, dya, dym = vjp(dz_ref[0])
        dpg_ref[0, :, :D_MODEL] = dga.astype(BF16)
        dpg_ref[0, :, D_MODEL:] = dgm.astype(BF16)
        dya_ref[0] = dya.astype(BF16)
        dym_ref[0] = dym.astype(BF16)

    return pl.pallas_call(
        body, name=name, grid=(B, SEQ // ROW_TILE),
        in_specs=[pl.BlockSpec((1, ROW_TILE, D_MODEL), lambda b, i: (b, i, 0)),
                  pl.BlockSpec((1, ROW_TILE, D_MODEL), lambda b, i: (b, i, 1)),
                  _row_spec(D_MODEL), _row_spec(D_MODEL), _row_spec(D_MODEL)],
        out_specs=[_row_spec(GATE_COLS), _row_spec(D_MODEL), _row_spec(D_MODEL)],
        out_shape=[jax.ShapeDtypeStruct((B, SEQ, GATE_COLS), BF16),
                   jax.ShapeDtypeStruct((B, SEQ, D_MODEL), BF16), jax.ShapeDtypeStruct((B, SEQ, D_MODEL), BF16)],
        compiler_params=_cparams(),
    )(pg, pg, ya, ym, dz)


def _log_sigmoid(x):
    return jnp.minimum(x, 0.0) - jnp.log(1.0 + jnp.exp(-jnp.abs(x)))


def _head_mask(e):
    lane = lax.broadcasted_iota(jnp.int32, (1, 128), 1)
    return jnp.logical_and(lane >= e * 64, lane < (e + 1) * 64).astype(F32)


def _bmxu(a, b, ca, cb):
    return lax.dot_general(a.astype(BF16), b.astype(BF16), (((ca,), (cb,)), ((0,), (0,))), preferred_element_type=F32)


@jax.custom_vjp
def cdot_nt(a, b):
    return _bmxu(a, b, 2, 2)


cdot_nt.defvjp(lambda a, b: (_bmxu(a, b, 2, 2), (a, b)),
               lambda res, g: (_bmxu(g, res[1], 2, 1), _bmxu(g, res[0], 1, 1)))


@jax.custom_vjp
def cdot_nn(a, b):
    return _bmxu(a, b, 2, 1)


cdot_nn.defvjp(lambda a, b: (_bmxu(a, b, 2, 1), (a, b)),
               lambda res, g: (_bmxu(g, res[1], 2, 2), _bmxu(res[0], g, 1, 1)))


@jax.custom_vjp
def cdot_tn(a, b):
    return _bmxu(a, b, 1, 1)


cdot_tn.defvjp(lambda a, b: (_bmxu(a, b, 1, 1), (a, b)),
               lambda res, g: (_bmxu(res[1], g, 2, 2), _bmxu(res[0], g, 2, 1)))


def _cdoth(a, b):
    return lax.dot_general(a, b, (((2,), (1,)), ((0,), (0,))), precision=lax.Precision.HIGHEST,
                           preferred_element_type=F32)


def _ml_intra(q2, k2, v, ifb, *, e):
    c, L = N_CHUNKS, ML_CHUNK
    hm = _head_mask(e)
    q3 = (q2 * hm).reshape(c, L, 128)
    k3 = (k2 * hm).reshape(c, L, 128)
    v3 = v.reshape(c, L, 128)
    if3 = ifb.reshape(c, L, 128)
    lanes = lax.broadcasted_iota(jnp.int32, (c, L, 128), 2)
    li = jnp.sum(jnp.where(lanes == e, if3, 0.0), axis=-1, keepdims=True)
    ri = lax.broadcasted_iota(jnp.int32, (L, L), 0)
    ci = lax.broadcasted_iota(jnp.int32, (L, L), 1)
    causal = ri >= ci
    tri = jnp.broadcast_to(causal.astype(F32), (c, L, L))
    cs = _cdoth(tri, _log_sigmoid(if3))
    b = jnp.sum(jnp.where(lanes == 2 + e, cs, 0.0), axis=-1, keepdims=True)
    last = lax.broadcasted_iota(jnp.int32, (1, L, 1), 1) == L - 1
    b_end = jnp.sum(jnp.where(last, b, 0.0), axis=1, keepdims=True)
    rrow = _cdoth(jnp.ones((c, L, L), F32), (ri == ci).astype(F32) * (li - b))
    Dm = jnp.where(causal, b + rrow, -jnp.inf)
    mD = lax.stop_gradient(jnp.max(Dm, axis=-1, keepdims=True))
    P0 = cdot_nt(q3, k3) * jnp.exp(Dm - mD)
    H0 = cdot_nn(P0, v3)
    r0 = jnp.sum(P0, axis=-1, keepdims=True)
    g = b_end - b + li
    mg = lax.stop_gradient(jnp.max(g, axis=1, keepdims=True))
    kw = jnp.exp(g - mg) * k3
    return H0, r0, cdot_tn(kw, v3), jnp.sum(kw, axis=1, keepdims=True), b, b_end, mD, mg


def _ml_inter(q2, mo, gn, H0, r0, b, C_in, n_in, *, mD, m_in, e):
    c, L = N_CHUNKS, ML_CHUNK
    q3 = (q2 * _head_mask(e)).reshape(c, L, 128)
    a = b + m_in
    m_t = lax.stop_gradient(jnp.maximum(a, mD))
    c1 = jnp.exp(mD - m_t)
    c2 = jnp.exp(a - m_t)
    num = c1 * H0 + c2 * cdot_nn(q3, C_in)
    nq = c1 * r0 + c2 * jnp.sum(q3 * n_in, axis=-1, keepdims=True)
    h = num / jnp.maximum(jnp.abs(nq), jnp.exp(-m_t))
    hg = _rms(h) * gn * jax.nn.sigmoid(mo.reshape(c, L, 128))
    return hg.reshape(c * L, 128)


def _state_sweep(U_s, un_s, be_s, mg_s, Cin_s, nin_s, min_s, al_s, bt_s):
    def step(j, carry):
        C, n, m = carry
        Cin_s[j], nin_s[j], min_s[j] = C, n, m
        m_out = jnp.maximum(be_s[j] + m, mg_s[j])
        al = jnp.exp(be_s[j] + m - m_out)
        bt = jnp.exp(mg_s[j] - m_out)
        al_s[j], bt_s[j] = al, bt
        return al * C + bt * U_s[j], al * n + bt * un_s[j], m_out

    lax.fori_loop(0, N_CHUNKS, step, (jnp.zeros((128, 128), F32), jnp.zeros((1, 128), F32), jnp.zeros((1, 1), F32)))


def _state_sweep_bwd(U_s, un_s, dbe_s, Cin_s, nin_s, dCp_s, dnp_s, al_s, bt_s):
    def step(t, carry):
        j = N_CHUNKS - 1 - t
        dC, dn = carry
        al, bt = al_s[j], bt_s[j]
        U_s[j] = bt * dC
        un_s[j] = bt * dn
        dal = jnp.sum(jnp.sum(dC * Cin_s[j], axis=1, keepdims=True), axis=0, keepdims=True) \
            + jnp.sum(dn * nin_s[j], axis=1, keepdims=True)
        dbe_s[j] = dal * al
        return dCp_s[j] + al * dC, dnp_s[j] + al * dn

    lax.fori_loop(0, N_CHUNKS, step, (jnp.zeros((128, 128), F32), jnp.zeros((1, 128), F32)))


def _state_scratch():
    c = N_CHUNKS
    return [pltpu.VMEM((c, 128, 128), F32), pltpu.VMEM((c, 1, 128), F32), pltpu.VMEM((c, 1, 1), F32),
            pltpu.VMEM((c, 1, 1), F32),
            pltpu.VMEM((c, 128, 128), F32), pltpu.VMEM((c, 1, 128), F32), pltpu.VMEM((c, 1, 1), F32),
            pltpu.VMEM((c, 1, 1), F32), pltpu.VMEM((c, 1, 1), F32)]


def _shift_down(x, s):
    if s == 0:
        return x
    rows = lax.broadcasted_iota(jnp.int32, x.shape, 0)
    return jnp.where(rows >= s, pltpu.roll(x, s, 0), 0.0)


def _shift_up(x, s):
    if s == 0:
        return x
    S = x.shape[0]
    rows = lax.broadcasted_iota(jnp.int32, x.shape, 0)
    return jnp.where(rows < S - s, pltpu.roll(x, S - s, 0), 0.0)


def _conv_pre(x, cw, cb):
    y = cb + cw[3:4, :] * x
    for j in range(3):
        y = y + cw[j:j + 1, :] * _shift_down(x, 3 - j)
    return y


def _conv_bwd(x, cw, dpre):
    dx = cw[3:4, :] * dpre
    dcw = [None] * 4
    dcw[3] = jnp.sum(dpre * x, axis=0, keepdims=True)
    for j in range(3):
        dx = dx + cw[j:j + 1, :] * _shift_up(dpre, 3 - j)
        dcw[j] = jnp.sum(dpre * _shift_down(x, 3 - j), axis=0, keepdims=True)
    return dx, dcw, jnp.sum(dpre, axis=0, keepdims=True)


def _silu(z):
    return z * jax.nn.sigmoid(z)


def _dsilu(z):
    s = jax.nn.sigmoid(z)
    return s * (1.0 + z * (1.0 - s))


_ML_Q, _ML_K = slice(0, 128), slice(128, 256)
_ML_IF = slice(768, 896)


def _ml_v(e):
    return slice(256 + e * 128, 384 + e * 128)


def _ml_o(e):
    return slice(512 + e * 128, 640 + e * 128)


def _ml_specs():
    pair = lambda b, p: (b, 0, p)
    return [pl.BlockSpec((1, SEQ, ML_PAIR_COLS), pair),
            pl.BlockSpec((1, 4, 128), lambda b, p: (p, 0, 0)),
            pl.BlockSpec((1, 4, 128), lambda b, p: (4 + p, 0, 0)),
            pl.BlockSpec((1, 1, 128), lambda b, p: (p, 0, 0)),
            pl.BlockSpec((1, 1, 128), lambda b, p: (4 + p, 0, 0)),
            pl.BlockSpec((1, 1, 128), lambda b, p: (p, 0, 0)),
            pl.BlockSpec((1, 1, 256), lambda b, p: (p, 0, 0))]


def mlstm_fwd(pm, cw8, cb8, bifp, gn4, name):
    B = pm.shape[0]

    def body(p_ref, cwq, cwk, cbq, cbk, bif_ref, gn_ref, hg_ref, *st):
        U_s, un_s, be_s, mg_s, Cin_s, nin_s, min_s, al_s, bt_s = st
        qc = _silu(_conv_pre(p_ref[0, :, _ML_Q], cwq[0], cbq[0]))
        kc = _silu(_conv_pre(p_ref[0, :, _ML_K], cwk[0], cbk[0])) * (64 ** -0.5)
        ifb = p_ref[0, :, _ML_IF] + bif_ref[0]
        for e in range(2):
            lanes = slice(e * 128, (e + 1) * 128)
            H0, r0, U, un, b, b_end, mD, mg = _ml_intra(qc, kc, p_ref[0, :, _ml_v(e)], ifb, e=e)
            U_s[...], un_s[...], be_s[...], mg_s[...] = U, un, b_end, mg
            _state_sweep(*st)
            hg = _ml_inter(qc, p_ref[0, :, _ml_o(e)], gn_ref[0, :, lanes], H0, r0, b, Cin_s[...], nin_s[...],
                           mD=mD, m_in=min_s[...], e=e)
            hg_ref[0, :, lanes] = hg.astype(BF16)

    return pl.pallas_call(
        body, name=name, grid=(B, ML_PAIRS),
        in_specs=_ml_specs(),
        out_specs=pl.BlockSpec((1, SEQ, 256), lambda b, p: (b, 0, p)),
        out_shape=jax.ShapeDtypeStruct((B, SEQ, D_MODEL), BF16),
        scratch_shapes=_state_scratch(),
        compiler_params=_cparams(),
    )(pm, cw8, cw8, cb8, cb8, bifp, gn4)


def mlstm_bwd(pm, cw8, cb8, bifp, gn4, dhg, name):
    B = pm.shape[0]

    def body(p_ref, cwq, cwk, cbq, cbk, bif_ref, gn_ref, dh_ref,
             dp_ref, dcw_ref, dcb_ref, dbif_ref, dgn_ref, *scr):
        st = scr[:9]
        U_s, un_s, be_s, mg_s, Cin_s, nin_s, min_s, al_s, bt_s = st
        dCp_s, dnp_s, dbe_s = scr[9:]
        p_id = pl.program_id(1)

        @pl.when(jnp.logical_and(pl.program_id(0) == 0, p_id == 0))
        def _():
            dcw_ref[...] = jnp.zeros_like(dcw_ref)
            dcb_ref[...] = jnp.zeros_like(dcb_ref)
            dbif_ref[...] = jnp.zeros_like(dbif_ref)
            dgn_ref[...] = jnp.zeros_like(dgn_ref)

        qc = _silu(_conv_pre(p_ref[0, :, _ML_Q], cwq[0], cbq[0]))
        kc = _silu(_conv_pre(p_ref[0, :, _ML_K], cwk[0], cbk[0])) * (64 ** -0.5)
        ifb = p_ref[0, :, _ML_IF] + bif_ref[0]
        dq = jnp.zeros((SEQ, 128), F32)
        dk = jnp.zeros((SEQ, 128), F32)
        difb = jnp.zeros((SEQ, 128), F32)
        for e in range(2):
            lanes = slice(e * 128, (e + 1) * 128)
            (H0, r0, U, un, b, b_end, mD, mg), vjp1 = jax.vjp(functools.partial(_ml_intra, e=e), qc, kc,
                                                              p_ref[0, :, _ml_v(e)], ifb)
            U_s[...], un_s[...], be_s[...], mg_s[...] = U, un, b_end, mg
            _state_sweep(*st)
            _, vjp3 = jax.vjp(functools.partial(_ml_inter, mD=mD, m_in=min_s[...], e=e), qc, p_ref[0, :, _ml_o(e)],
                              gn_ref[0, :, lanes], H0, r0, b, Cin_s[...], nin_s[...])
            dq_a, dmo, dgn, dH0, dr0, db_a, dCp, dnp = vjp3(dh_ref[0, :, lanes])
            dCp_s[...], dnp_s[...] = dCp, dnp
            _state_sweep_bwd(U_s, un_s, dbe_s, Cin_s, nin_s, dCp_s, dnp_s, al_s, bt_s)
            dq_b, dk_b, dv, difb_e = vjp1((dH0, dr0, U_s[...], un_s[...], db_a, dbe_s[...],
                                           jnp.zeros_like(mD), jnp.zeros_like(mg)))
            dq, dk, difb = dq + dq_a + dq_b, dk + dk_b, difb + difb_e
            dp_ref[0, :, _ml_v(e)] = dv.astype(BF16)
            dp_ref[0, :, _ml_o(e)] = dmo.astype(BF16)
            dgn_ref[p_id, :, lanes] += dgn
        dp_ref[0, :, _ML_IF] = difb.astype(BF16)
        dbif_ref[p_id] += jnp.sum(difb, axis=0, keepdims=True)

        for (sl, cw, cb, d, blk, scale) in ((_ML_Q, cwq, cbq, dq, p_id, 1.0), (_ML_K, cwk, cbk, dk, 4 + p_id, 64 ** -0.5)):
            xr = p_ref[0, :, sl]
            dpre = d * scale * _dsilu(_conv_pre(xr, cw[0], cb[0]))
            dx, dcw, dcb = _conv_bwd(xr, cw[0], dpre)
            dp_ref[0, :, sl] = dx.astype(BF16)
            for j in range(4):
                dcw_ref[blk, j:j + 1, :] += dcw[j]
            dcb_ref[blk] += dcb

    full3 = lambda b, p: (0, 0, 0)
    return pl.pallas_call(
        body, name=name, grid=(B, ML_PAIRS),
        in_specs=[pl.BlockSpec((1, SEQ, ML_PAIR_COLS), lambda b, p: (b, 0, p), pipeline_mode=pl.Buffered(1))]
        + _ml_specs()[1:] + [pl.BlockSpec((1, SEQ, 256), lambda b, p: (b, 0, p), pipeline_mode=pl.Buffered(1))],
        out_specs=[pl.BlockSpec((1, SEQ, ML_PAIR_COLS), lambda b, p: (b, 0, p)),
                   pl.BlockSpec((8, 4, 128), full3), pl.BlockSpec((8, 1, 128), full3),
                   pl.BlockSpec((4, 1, 128), full3), pl.BlockSpec((4, 1, 256), full3)],
        out_shape=[jax.ShapeDtypeStruct((B, SEQ, ML_COLS), BF16),
                   jax.ShapeDtypeStruct((8, 4, 128), F32), jax.ShapeDtypeStruct((8, 1, 128), F32),
                   jax.ShapeDtypeStruct((4, 1, 128), F32), jax.ShapeDtypeStruct((4, 1, 256), F32)],
        scratch_shapes=_state_scratch() + [pltpu.VMEM((N_CHUNKS, 128, 128), F32), pltpu.VMEM((N_CHUNKS, 1, 128), F32),
                                           pltpu.VMEM((N_CHUNKS, 1, 1), F32)],
        compiler_params=_cparams(),
    )(pm, cw8, cw8, cb8, cb8, bifp, gn4, dhg)


def _adamw(w, g, m, v):
    m = ADAM_B1 * m + (1.0 - ADAM_B1) * g
    v = ADAM_B2 * v + (1.0 - ADAM_B2) * (g * g)
    m_hat = m / (1.0 - ADAM_B1 ** ADAM_STEP)
    v_hat = v / (1.0 - ADAM_B2 ** ADAM_STEP)
    delta = -ADAM_LR * (m_hat / (jnp.sqrt(v_hat) + ADAM_EPS) + ADAM_WD * w)
    return delta, m, v


def adamw(w, g, m, v, name, parts=False):
    R, C = w.shape
    tr = _pick(R, (256, 128, 64, 32, 16, 8, 4, 2, 1)) if R * C * 4 > (1 << 20) else R
    spec = pl.BlockSpec((tr, C), lambda i: (i, 0))
    g_spec = pl.BlockSpec((N_DEV, tr, C), lambda i: (0, i, 0)) if parts else spec

    def body(w_ref, g_ref, m_ref, v_ref, go_ref, d_ref, mo_ref, vo_ref):
        if parts:
            g = g_ref[0].astype(F32)
            for k in range(1, N_DEV):
                g = g + g_ref[k].astype(F32)
        else:
            g = g_ref[...]
        d, mn, vn = _adamw(w_ref[...], g, m_ref[...], v_ref[...])
        go_ref[...], d_ref[...], mo_ref[...], vo_ref[...] = g, d, mn, vn

    return pl.pallas_call(
        body, name=name, grid=(R // tr,),
        in_specs=[spec, g_spec, spec, spec], out_specs=[spec] * 4,
        out_shape=[jax.ShapeDtypeStruct((R, C), F32)] * 4,
        compiler_params=_cparams(),
    )(w, g, m, v)


def _mesh_pos():
    return lax.axis_index("x"), lax.axis_index("y"), lax.axis_index("c")


def _flip(pos, f):
    x, y, c = pos
    return (1 - x if f & 4 else x, 1 - y if f & 2 else y, 1 - c if f & 1 else c)


def _index(pos):
    return 4 * pos[0] + 2 * pos[1] + pos[2]


def _exchange(arrs, name, scatter):
    n = len(arrs)

    def body(*refs):
        ins, outs = refs[:n], refs[n:2 * n]
        send, recv, lsem = refs[2 * n:]
        me = _mesh_pos()
        mine = _index(me)
        copies = []
        for i in range(n):
            src = ins[i].at[mine] if scatter else ins[i]
            loc = pltpu.make_async_copy(src, outs[i].at[mine], lsem.at[i])
            loc.start()
            copies.append(loc)
            for f in range(1, N_DEV):
                peer = _flip(me, f)
                src = ins[i].at[_index(peer)] if scatter else ins[i]
                cp = pltpu.make_async_remote_copy(
                    src_ref=src, dst_ref=outs[i].at[mine],
                    send_sem=send.at[i * 7 + f - 1], recv_sem=recv.at[i * 7 + f - 1],
                    device_id=peer, device_id_type=pl.DeviceIdType.MESH)
                cp.start()
                copies.append(cp)
        for cp in copies:
            cp.wait()

    any_spec = pl.BlockSpec(memory_space=pl.ANY)
    out_shape = [jax.ShapeDtypeStruct(a.shape if scatter else (N_DEV,) + a.shape, a.dtype) for a in arrs]
    res = pl.pallas_call(
        body, name=name,
        in_specs=[any_spec] * n, out_specs=[any_spec] * n, out_shape=out_shape,
        scratch_shapes=[pltpu.SemaphoreType.DMA((7 * n,)), pltpu.SemaphoreType.DMA((7 * n,)),
                        pltpu.SemaphoreType.DMA((n,))],
        compiler_params=_cparams(),
    )(*arrs)
    return list(res)


def all_gather(arrs, name):
    return _exchange(arrs, name, False)


def all_to_all(arrs, name):
    return _exchange(arrs, name, True)


def cast_bf16(arrs, name):
    outs = []
    for i, a in enumerate(arrs):
        R, C = a.shape
        tr = _pick(R, (256, 128, 64, 32, 16, 8)) if R * C * 4 > (1 << 21) else R
        spec = pl.BlockSpec((tr, C), lambda i: (i, 0))

        def body(a_ref, o_ref):
            o_ref[...] = a_ref[...].astype(BF16)

        outs.append(pl.pallas_call(body, name=f"{name}_{i}", grid=(R // tr,), in_specs=[spec], out_specs=spec,
                                   out_shape=jax.ShapeDtypeStruct((R, C), BF16), compiler_params=_cparams())(a))
    return outs


def sum_parts(parts, name):
    def fn(p):
        g = p[0]
        for k in range(1, N_DEV):
            g = g + p[k]
        return (g,)
    return small_call(fn, [parts], [jax.ShapeDtypeStruct(parts.shape[1:], F32)], name)[0]


_SPLITS = np.cumsum([1536, 1536, 1536, 512, 512, 1024, 1024, 8, 8, 2048])[:-1].tolist()


def split_w_in(w):
    aq, ak, av, mq, mk, mv, mo, mi, mf, gates = jnp.split(w, _SPLITS, axis=1)
    R = w.shape[0]
    w_att = jnp.stack([aq.reshape(R, 12, 128), ak.reshape(R, 12, 128), av.reshape(R, 12, 128)], axis=2)
    gif = jnp.concatenate([mi.reshape(R, 4, 2), mf.reshape(R, 4, 2), jnp.zeros((R, 4, 124), w.dtype)], axis=2)
    w_ml = jnp.concatenate([mq.reshape(R, 4, 128), mk.reshape(R, 4, 128), mv.reshape(R, 4, 256),
                            mo.reshape(R, 4, 256), gif], axis=2)
    return w_att.reshape(R, ATT_COLS), w_ml.reshape(R, ML_COLS), gates


def merge_w_in(g_att, g_ml, g_gate):
    R = g_att.shape[0]
    a = g_att.reshape(R, 12, 3, 128)
    m = g_ml.reshape(R, 4, ML_PAIR_COLS)
    gif = m[:, :, 768:772]
    return jnp.concatenate([
        a[:, :, 0].reshape(R, 1536), a[:, :, 1].reshape(R, 1536), a[:, :, 2].reshape(R, 1536),
        m[:, :, 0:128].reshape(R, 512), m[:, :, 128:256].reshape(R, 512),
        m[:, :, 256:512].reshape(R, 1024), m[:, :, 512:768].reshape(R, 1024),
        gif[:, :, 0:2].reshape(R, 8), gif[:, :, 2:4].reshape(R, 8), g_gate], axis=1)


def _blk8(v, width=128):
    r = v.shape[0]
    return v.reshape(r, 1024 // width, width).transpose(1, 0, 2)


def _unblk8(v):
    nb, r, w = v.shape
    return v.transpose(1, 0, 2).reshape(r, nb * w)


def local_step(x, target, mods, w, small):
    B = x.shape[0]
    T = B * SEQ
    shift1, scale1, gate1, shift2, scale2, gate2 = mods
    f2 = lambda a: a.reshape(T, a.shape[-1])
    f3 = lambda a: a.reshape(B, SEQ, a.shape[-1])

    rel_t = jnp.pad(small["rel_bias"].T, ((0, 4), (0, 0)))
    onehots = [_bucket_onehot(d) for _, d in ATT_GROUPS]
    biases = [bias_expand(rel_t, oh, f"bias_expand{g}").reshape(16, ATT_BLOCK, 2 * ATT_BLOCK)
              for g, oh in enumerate(onehots)]
    qg, kg = small["q_norm_g"], small["k_norm_g"]
    cw8 = _blk8(small["conv_w"])
    cb8 = _blk8(small["conv_b"])
    b_if = small["b_if"].reshape(2, 4, 2)
    bifp = jnp.concatenate([b_if[0], b_if[1], jnp.zeros((4, 124), F32)], axis=1).reshape(4, 1, 128)
    gn4 = small["mlstm_norm_g"].reshape(4, 1, 256)

    u = modnorm_fwd(x, small["norm1_g"], scale1, shift1, "modnorm1")
    u2d = f2(u)
    pa = f3(matmul(u2d, w["w_att"], mode="nn", name="proj_att"))
    pm = f3(matmul(u2d, w["w_ml"], mode="nn", name="proj_ml"))
    pg = f3(matmul(u2d, w["w_gate"], mode="nn", name="proj_gate"))
    os_, ls_ = [], []
    for g in range(3):
        o, l = attn_fwd(pa, biases[g], qg, kg, g, f"attn_fwd{g}")
        os_.append(o)
        ls_.append(l)
    att = merge_fwd(os_, ls_, "merge_fwd")
    y_att = matmul(f2(att), w["w_att_out"], mode="nn", name="att_out")
    hg = mlstm_fwd(pm, cw8, cb8, bifp, gn4, "mlstm_fwd")
    y_ml = matmul(f2(hg), w["w_ml_out"], mode="nn", name="ml_out")
    z = gate_fwd(pg, f3(y_att), f3(y_ml), "gate_fwd")
    y = matmul(f2(z), w["w_out"], mode="nn", name="out_proj")
    x1, u2 = resid_modnorm_fwd(x, f3(y), gate1, small["norm2_g"], scale2, shift2, "resid_modnorm2")
    pre, hdn = matmul(f2(u2), w["w_ff1"], mode="nn", name="ff1", out_dtypes=(F32, BF16),
                      epi=lambda acc: (acc, jnp.square(jnp.maximum(acc, 0.0))))
    ffo = matmul(hdn, w["w_ff2"], mode="nn", name="ff2")
    dx2, loss = resid_loss(x1, f3(ffo), gate2, target, "resid_loss")

    d_ffo, d_gate2 = resid_bwd(dx2, f3(ffo), gate2, "resid_bwd2")
    g_ff2 = matmul(hdn, f2(d_ffo), mode="tn", name="g_ff2", out_dtypes=(BF16,))
    d_pre = matmul(f2(d_ffo), w["w_ff2"], mode="nt", name="d_hdn", out_dtypes=(BF16,), extras=(pre,),
                   epi=lambda acc, p: (acc * (2.0 * jnp.maximum(p, 0.0)),))
    g_ff1 = matmul(f2(u2), d_pre, mode="tn", name="g_ff1", out_dtypes=(BF16,))
    du2 = matmul(d_pre, w["w_ff1"], mode="nt", name="d_u2")
    dx1, d_norm2, d_scale2, d_shift2 = modnorm_bwd(x1, small["norm2_g"], scale2, shift2, f3(du2), dx2, "modnorm_bwd2")
    dy, d_gate1 = resid_bwd(dx1, f3(y), gate1, "resid_bwd1")
    g_out = matmul(f2(z), f2(dy), mode="tn", name="g_out", out_dtypes=(BF16,))
    dz = matmul(f2(dy), w["w_out"], mode="nt", name="d_z")
    dpg, d_ya, d_ym = gate_bwd(pg, f3(y_att), f3(y_ml), f3(dz), "gate_bwd")
    g_att_out = matmul(f2(att), f2(d_ya), mode="tn", name="g_att_out", out_dtypes=(BF16,))
    d_att = matmul(f2(d_ya), w["w_att_out"], mode="nt", name="d_att")
    g_ml_out = matmul(f2(hg), f2(d_ym), mode="tn", name="g_ml_out", out_dtypes=(BF16,))
    d_hg = matmul(f2(d_ym), w["w_ml_out"], mode="nt", name="d_hg")
    dmerge = merge_bwd(os_, ls_, f3(d_att), "merge_bwd")
    dpa = jnp.zeros((B, SEQ, ATT_COLS), BF16)
    d_rel = []
    d_qg = d_kg = None
    for g in range(3):
        dpa, dbias, dq_g, dk_g = attn_bwd(pa, biases[g], qg, kg, dmerge[g], dmerge[3 + g], dpa, g, f"attn_bwd{g}")
        db8 = jnp.pad(dbias.reshape(4, -1), ((0, 4), (0, 0)))
        d_rel.append(bias_reduce(db8, onehots[g], f"bias_reduce{g}")[:4])
        d_qg = dq_g if d_qg is None else d_qg + dq_g
        d_kg = dk_g if d_kg is None else d_kg + dk_g
    dpm, dcw8, dcb8, dbifp, dgn4 = mlstm_bwd(pm, cw8, cb8, bifp, gn4, f3(d_hg), "mlstm_bwd")
    g_w_att = matmul(u2d, f2(dpa), mode="tn", name="g_w_att", out_dtypes=(BF16,))
    g_w_ml = matmul(u2d, f2(dpm), mode="tn", name="g_w_ml", out_dtypes=(BF16,))
    g_w_gate = matmul(u2d, f2(dpg), mode="tn", name="g_w_gate", out_dtypes=(BF16,))
    du = matmul(f2(dpa), w["w_att"], mode="nt", name="d_u_att")
    du = matmul(f2(dpm), w["w_ml"], mode="nt", name="d_u_ml", extras=(du,), epi=lambda acc, e: (acc + e,))
    du = matmul(f2(dpg), w["w_gate"], mode="nt", name="d_u_gate", extras=(du,), epi=lambda acc, e: (acc + e,))
    grad_x, d_norm1, d_scale1, d_shift1 = modnorm_bwd(x, small["norm1_g"], scale1, shift1, f3(du), dx1, "modnorm_bwd1")

    d_mods = (d_shift1, d_scale1, d_gate1, d_shift2, d_scale2, d_gate2)
    big = dict(w_in=merge_w_in(g_w_att, g_w_ml, g_w_gate), w_att_out=g_att_out, w_ml_out=g_ml_out, w_out=g_out,
               w_ff1=g_ff1, w_ff2=g_ff2)
    dbif = dbifp.reshape(4, 128)
    small_g = dict(
        norm1_g=d_norm1, norm2_g=d_norm2,
        b_if=jnp.stack([dbif[:, 0:2].reshape(8), dbif[:, 2:4].reshape(8)]),
        conv_w=_unblk8(dcw8), conv_b=_unblk8(dcb8), q_norm_g=d_qg, k_norm_g=d_kg,
        rel_bias=jnp.concatenate(d_rel, axis=0).T,
        mlstm_norm_g=dgn4.reshape(1, 1024))
    return loss, grad_x, d_mods, big, small_g


_SMALL = (("b_ada", 6144), ("norm1_g", 1024), ("norm2_g", 1024), ("b_if", 16), ("conv_b", 1024),
          ("q_norm_g", 128), ("k_norm_g", 128), ("rel_bias", 384), ("mlstm_norm_g", 1024), ("conv_w", 4096))
_SMALL_ROWS = 120
_REPL = _SMALL[:-1]
_REPL_ROWS = 86


def _pack(d, names, rows):
    flat = jnp.concatenate([d[k].reshape(-1) for k, _ in names])
    return jnp.pad(flat, (0, rows * 128 - flat.shape[0])).reshape(rows, 128)


def _unpack(slab, names, shapes):
    flat = slab.reshape(-1)
    out, off = {}, 0
    for k, nel in names:
        out[k] = flat[off:off + nel].reshape(shapes[k])
        off += nel
    return out


def kernel(x, c, w_ada, b_ada, norm1_g, norm2_g, w_in, b_if, conv_w, conv_b, q_norm_g, k_norm_g, rel_bias, mlstm_norm_g, w_att_out, w_ml_out, w_out, w_ff1, w_ff2, loss_target, m_w_ada, m_b_ada, m_norm1_g, m_norm2_g, m_w_in, m_b_if, m_conv_w, m_conv_b, m_q_norm_g, m_k_norm_g, m_rel_bias, m_mlstm_norm_g, m_w_att_out, m_w_ml_out, m_w_out, m_w_ff1, m_w_ff2, v_w_ada, v_b_ada, v_norm1_g, v_norm2_g, v_w_in, v_b_if, v_conv_w, v_conv_b, v_q_norm_g, v_k_norm_g, v_rel_bias, v_mlstm_norm_g, v_w_att_out, v_w_ml_out, v_w_out, v_w_ff1, v_w_ff2):
    P = dict(w_ada=w_ada, b_ada=b_ada, norm1_g=norm1_g, norm2_g=norm2_g, w_in=w_in, b_if=b_if, conv_w=conv_w,
             conv_b=conv_b, q_norm_g=q_norm_g, k_norm_g=k_norm_g, rel_bias=rel_bias, mlstm_norm_g=mlstm_norm_g,
             w_att_out=w_att_out, w_ml_out=w_ml_out, w_out=w_out, w_ff1=w_ff1, w_ff2=w_ff2)
    M = dict(w_ada=m_w_ada, b_ada=m_b_ada, norm1_g=m_norm1_g, norm2_g=m_norm2_g, w_in=m_w_in, b_if=m_b_if,
             conv_w=m_conv_w, conv_b=m_conv_b, q_norm_g=m_q_norm_g, k_norm_g=m_k_norm_g, rel_bias=m_rel_bias,
             mlstm_norm_g=m_mlstm_norm_g, w_att_out=m_w_att_out, w_ml_out=m_w_ml_out, w_out=m_w_out,
             w_ff1=m_w_ff1, w_ff2=m_w_ff2)
    V = dict(w_ada=v_w_ada, b_ada=v_b_ada, norm1_g=v_norm1_g, norm2_g=v_norm2_g, w_in=v_w_in, b_if=v_b_if,
             conv_w=v_conv_w, conv_b=v_conv_b, q_norm_g=v_q_norm_g, k_norm_g=v_k_norm_g, rel_bias=v_rel_bias,
             mlstm_norm_g=v_mlstm_norm_g, w_att_out=v_w_att_out, w_ml_out=v_w_ml_out, w_out=v_w_out,
             w_ff1=v_w_ff1, w_ff2=v_w_ff2)
    names = list(P)
    shapes = {k: P[k].shape for k in names}
    B = x.shape[0]
    me = _index(_mesh_pos())

    big_names = ("w_in", "w_att_out", "w_ml_out", "w_out", "w_ff1", "w_ff2")
    shards = cast_bf16([P[k][0] for k in big_names], "cast_w")
    gathered = all_gather(shards + [c, conv_w[0]], "gather_weights")
    gw = dict(zip(big_names, gathered[:6]))
    c_all = gathered[6].reshape(N_DEV * B, D_MODEL)
    conv_w_full = gathered[7].transpose(1, 0, 2).reshape(4, 1024)
    w_in_full = gw["w_in"].transpose(1, 0, 2).reshape(D_MODEL, D_IN)
    w_att, w_ml, w_gate = split_w_in(w_in_full)
    w = dict(w_att=w_att, w_ml=w_ml, w_gate=w_gate,
             w_att_out=gw["w_att_out"].transpose(1, 0, 2).reshape(512, D_MODEL),
             w_ml_out=gw["w_ml_out"].reshape(D_MODEL, D_MODEL), w_out=gw["w_out"].reshape(D_MODEL, D_MODEL),
             w_ff1=gw["w_ff1"].transpose(1, 0, 2).reshape(D_MODEL, D_FF), w_ff2=gw["w_ff2"].reshape(D_FF, D_MODEL))

    (silu_c,) = small_call(lambda a: (_silu(a),), [c_all], [jax.ShapeDtypeStruct(c_all.shape, F32)], "silu_c")
    b_ada_cols = lax.dynamic_slice(b_ada, (0, me * 768), (1, 768))
    ada_cols = matmul(silu_c, w_ada[0], mode="nn", name="ada", extras=(jnp.broadcast_to(b_ada_cols, (N_DEV * B, 768)),),
                      epi=lambda acc, bb: (acc + bb,))
    (ada_t,) = all_to_all([ada_cols.reshape(N_DEV, B, 768)], "ada_exchange")
    ada = ada_t.transpose(1, 0, 2).reshape(B, 6 * D_MODEL)
    mods = tuple(ada[:, i * D_MODEL:(i + 1) * D_MODEL].reshape(B, 1, D_MODEL) for i in range(6))

    small = dict(norm1_g=norm1_g, norm2_g=norm2_g, b_if=b_if[0], conv_w=conv_w_full, conv_b=conv_b,
                 q_norm_g=q_norm_g, k_norm_g=k_norm_g, rel_bias=rel_bias, mlstm_norm_g=mlstm_norm_g)
    loss, grad_x, d_mods, big, small_g = local_step(x, loss_target, mods, w, small)
    loss = lax.psum(loss[0, 0], ("x", "y", "c"))

    d_ada = jnp.concatenate([d.reshape(B, D_MODEL) for d in d_mods], axis=1)
    (d_ada_t,) = all_to_all([d_ada.reshape(B, N_DEV, 768).transpose(1, 0, 2)], "d_ada_exchange")
    d_ada_cols = d_ada_t.reshape(N_DEV * B, 768)
    g_w_ada = matmul(silu_c, d_ada_cols, mode="tn", name="g_w_ada")
    (g_b_cols,) = small_call(lambda a: (jnp.sum(a, axis=0, keepdims=True),), [d_ada_cols],
                             [jax.ShapeDtypeStruct((1, 768), F32)], "g_b_ada_cols")
    small_g["b_ada"] = lax.dynamic_update_slice(jnp.zeros((1, 6144), F32), g_b_cols, (0, me * 768))

    send = [
        big["w_in"].reshape(D_MODEL, N_DEV, W_IN_SHARD).transpose(1, 0, 2),
        big["w_att_out"].reshape(512, N_DEV, 128).transpose(1, 0, 2),
        big["w_ml_out"].reshape(N_DEV, 128, D_MODEL),
        big["w_out"].reshape(N_DEV, 128, D_MODEL),
        big["w_ff1"].reshape(D_MODEL, N_DEV, 512).transpose(1, 0, 2),
        big["w_ff2"].reshape(N_DEV, 512, D_MODEL),
    ]
    recv = all_to_all(send, "grad_exchange")
    (small_parts,) = all_gather([_pack(small_g, _SMALL, _SMALL_ROWS)], "small_grad_gather")
    small_sum = sum_parts(small_parts, "small_grad_sum")
    sg = _unpack(small_sum, _SMALL, {**{k: shapes[k] for k, _ in _REPL}, "conv_w": (4, 1024)})

    G, Dl, NM, NV = {}, {}, {}, {}
    for k, parts in zip(big_names, recv):
        g, d, nm, nv = adamw(P[k][0], parts, M[k][0], V[k][0], f"adamw_{k}", parts=True)
        G[k], Dl[k], NM[k], NV[k] = g[None], d[None], nm[None], nv[None]
    g, d, nm, nv = adamw(w_ada[0], g_w_ada, m_w_ada[0], v_w_ada[0], "adamw_w_ada")
    G["w_ada"], Dl["w_ada"], NM["w_ada"], NV["w_ada"] = g[None], d[None], nm[None], nv[None]
    g_conv = lax.dynamic_slice(sg["conv_w"], (0, me * 128), (4, 128))
    g, d, nm, nv = adamw(conv_w[0], g_conv, m_conv_w[0], v_conv_w[0], "adamw_conv_w")
    G["conv_w"], Dl["conv_w"], NM["conv_w"], NV["conv_w"] = g[None], d[None], nm[None], nv[None]
    gslab = _pack(sg, _REPL, _REPL_ROWS)
    _, d, nm, nv = adamw(_pack(P, _REPL, _REPL_ROWS), gslab, _pack(M, _REPL, _REPL_ROWS), _pack(V, _REPL, _REPL_ROWS),
                         "adamw_small")
    rs = {k: shapes[k] for k, _ in _REPL}
    d, nm, nv = _unpack(d, _REPL, rs), _unpack(nm, _REPL, rs), _unpack(nv, _REPL, rs)
    for k, _ in _REPL:
        G[k], Dl[k], NM[k], NV[k] = sg[k], d[k], nm[k], nv[k]

    return (loss, grad_x, *[G[k] for k in names], *[Dl[k] for k in names], *[NM[k] for k in names],
            *[NV[k] for k in names])
```

```python
import functools
import math

import numpy as np
import jax
import jax.numpy as jnp
from jax import lax
from jax.experimental import pallas as pl
from jax.experimental.pallas import tpu as pltpu

F32 = jnp.float32
BF16 = jnp.bfloat16

N_DEV = 8
D_MODEL = 1024
SEQ = 2048
ATT_GROUPS = ((128, 1), (512, 4), (2048, 16))
N_ATT_HEADS = 12
ATT_BLOCK = 128
HEAD_DIM = 128
ML_HEADS = 8
ML_PAIRS = 4
ML_CHUNK = 64
N_CHUNKS = SEQ // ML_CHUNK
N_BUCKETS = 32
MAX_DISTANCE = 2048
D_FF = 4096
D_IN = 9744
EPS = 1e-6

ADAM_LR = 0.001
ADAM_B1 = 0.9
ADAM_B2 = 0.999
ADAM_EPS = 1e-08
ADAM_WD = 0.01
ADAM_STEP = 10

ATT_HEAD_COLS = 3 * HEAD_DIM
ATT_COLS = N_ATT_HEADS * ATT_HEAD_COLS
ML_PAIR_COLS = 896
ML_COLS = ML_PAIRS * ML_PAIR_COLS
GATE_COLS = 2 * D_MODEL
W_IN_SHARD = D_IN // N_DEV

VMEM_LIMIT = 56 * 1024 * 1024


def _cparams(**kw):
    return pltpu.CompilerParams(vmem_limit_bytes=VMEM_LIMIT, **kw)


_NN = ((1,), (0,))
_NT = ((1,), (1,))
_TN = ((0,), (0,))


def _mxu(a, b, dims):
    return lax.dot_general(a.astype(BF16), b.astype(BF16), (dims, ((), ())), preferred_element_type=F32)


@jax.custom_vjp
def bdot_nn(a, b):
    return _mxu(a, b, _NN)


def _nn_fwd(a, b):
    return _mxu(a, b, _NN), (a, b)


def _nn_bwd(res, g):
    a, b = res
    return _mxu(g, b, _NT), _mxu(a, g, _TN)


bdot_nn.defvjp(_nn_fwd, _nn_bwd)


@jax.custom_vjp
def bdot_nt(a, b):
    return _mxu(a, b, _NT)


def _nt_fwd(a, b):
    return _mxu(a, b, _NT), (a, b)


def _nt_bwd(res, g):
    a, b = res
    return _mxu(g, b, _NN), _mxu(g, a, _TN)


bdot_nt.defvjp(_nt_fwd, _nt_bwd)


@jax.custom_vjp
def bdot_tn(a, b):
    return _mxu(a, b, _TN)


def _tn_fwd(a, b):
    return _mxu(a, b, _TN), (a, b)


def _tn_bwd(res, g):
    a, b = res
    return _mxu(b, g, _NT), _mxu(a, g, _NN)


bdot_tn.defvjp(_tn_fwd, _tn_bwd)


def _doth(a, b, dims=_NN):
    return lax.dot_general(a, b, (dims, ((), ())), precision=lax.Precision.HIGHEST, preferred_element_type=F32)


def _rms(x):
    return x * lax.rsqrt(jnp.mean(x * x, axis=-1, keepdims=True) + EPS)


def _pick(n, cands):
    for t in cands:
        if n % t == 0:
            return t
    raise ValueError(f"no tile for {n}")


MM_TILE_M = (1024, 512, 256, 128, 64, 32, 16, 8)
MM_TILE_N = (2048, 1792, 1536, 1024, 768, 512, 256, 128)
MM_TILE_K = (1024, 512, 256, 128, 64, 32)

def matmul(a, b, *, mode, name, out_dtypes=(F32,), epi=None, extras=()):
    if mode == "nn":
        (M, K), (K2, N) = a.shape, b.shape
    elif mode == "nt":
        (M, K), (N, K2) = a.shape, b.shape
    else:
        (K, M), (K2, N) = a.shape, b.shape
    assert K == K2, (a.shape, b.shape, mode)
    tm = _pick(M, MM_TILE_M)
    tn = _pick(N, MM_TILE_N)
    tk = _pick(K, MM_TILE_K)
    nk = K // tk
    n_ex = len(extras)
    n_out = len(out_dtypes)
    dims = {"nn": _NN, "nt": _NT, "tn": _TN}[mode]

    def finish(r, ex_refs, out_refs):
        outs = epi(r, *[e[...] for e in ex_refs]) if epi is not None else (r,)
        for o_ref, o in zip(out_refs, outs):
            o_ref[...] = o.astype(o_ref.dtype)

    def body(*refs):
        a_ref, b_ref = refs[0], refs[1]
        ex_refs = refs[2:2 + n_ex]
        out_refs = refs[2 + n_ex:2 + n_ex + n_out]
        if nk == 1:
            finish(_mxu(a_ref[...], b_ref[...], dims), ex_refs, out_refs)
            return
        acc = refs[2 + n_ex + n_out]
        k = pl.program_id(2)

        @pl.when(k == 0)
        def _():
            acc[...] = jnp.zeros_like(acc)

        acc[...] += _mxu(a_ref[...], b_ref[...], dims)

        @pl.when(k == nk - 1)
        def _():
            finish(acc[...], ex_refs, out_refs)

    if mode == "nn":
        a_spec = pl.BlockSpec((tm, tk), lambda i, j, k: (i, k))
        b_spec = pl.BlockSpec((tk, tn), lambda i, j, k: (k, j))
    elif mode == "nt":
        a_spec = pl.BlockSpec((tm, tk), lambda i, j, k: (i, k))
        b_spec = pl.BlockSpec((tn, tk), lambda i, j, k: (j, k))
    else:
        a_spec = pl.BlockSpec((tk, tm), lambda i, j, k: (k, i))
        b_spec = pl.BlockSpec((tk, tn), lambda i, j, k: (k, j))
    o_spec = pl.BlockSpec((tm, tn), lambda i, j, k: (i, j))
    res = pl.pallas_call(
        body,
        name=name,
        grid=(M // tm, N // tn, nk),
        in_specs=[a_spec, b_spec] + [o_spec] * n_ex,
        out_specs=[o_spec] * n_out,
        out_shape=[jax.ShapeDtypeStruct((M, N), dt) for dt in out_dtypes],
        scratch_shapes=[pltpu.VMEM((tm, tn), F32)] if nk > 1 else [],
        compiler_params=_cparams(),
    )(a, b, *extras)
    return res[0] if n_out == 1 else tuple(res)


def small_call(fn, inputs, out_shapes, name):
    n_in = len(inputs)

    def body(*refs):
        outs = fn(*[r[...] for r in refs[:n_in]])
        for o_ref, o in zip(refs[n_in:], outs):
            o_ref[...] = o.astype(o_ref.dtype)

    res = pl.pallas_call(body, name=name, out_shape=list(out_shapes), compiler_params=_cparams())(*inputs)
    return tuple(res)


ROW_TILE = 512


def _modnorm(x, g, scale, shift):
    return _rms(x) * g * (1.0 + scale) + shift


def _row_spec(width):
    return pl.BlockSpec((1, ROW_TILE, width), lambda b, i: (b, i, 0))


def _mod_spec():
    return pl.BlockSpec((1, 1, D_MODEL), lambda b, i: (b, 0, 0))


def _vec_spec():
    return pl.BlockSpec((1, D_MODEL), lambda b, i: (0, 0))


def modnorm_fwd(x, g, scale, shift, name):
    B, S, D = x.shape

    def body(x_ref, g_ref, sc_ref, sh_ref, u_ref):
        u_ref[0] = _modnorm(x_ref[0], g_ref[...], sc_ref[0], sh_ref[0]).astype(BF16)

    return pl.pallas_call(
        body, name=name, grid=(B, S // ROW_TILE),
        in_specs=[_row_spec(D), _vec_spec(), _mod_spec(), _mod_spec()],
        out_specs=_row_spec(D),
        out_shape=jax.ShapeDtypeStruct((B, S, D), BF16),
        compiler_params=_cparams(),
    )(x, g, scale, shift)


def resid_modnorm_fwd(x, y, gate, g, scale, shift, name):
    B, S, D = x.shape

    def body(x_ref, y_ref, gt_ref, g_ref, sc_ref, sh_ref, x1_ref, u_ref):
        x1 = x_ref[0] + gt_ref[0] * y_ref[0]
        x1_ref[0] = x1
        u_ref[0] = _modnorm(x1, g_ref[...], sc_ref[0], sh_ref[0]).astype(BF16)

    return pl.pallas_call(
        body, name=name, grid=(B, S // ROW_TILE),
        in_specs=[_row_spec(D), _row_spec(D), _mod_spec(), _vec_spec(), _mod_spec(), _mod_spec()],
        out_specs=[_row_spec(D), _row_spec(D)],
        out_shape=[jax.ShapeDtypeStruct((B, S, D), F32), jax.ShapeDtypeStruct((B, S, D), BF16)],
        compiler_params=_cparams(),
    )(x, y, gate, g, scale, shift)


def resid_loss(x1, ffo, gate, target, name):
    B, S, D = x1.shape

    def body(x_ref, f_ref, gt_ref, t_ref, dx_ref, loss_ref):
        first = jnp.logical_and(pl.program_id(0) == 0, pl.program_id(1) == 0)

        @pl.when(first)
        def _():
            loss_ref[...] = jnp.zeros_like(loss_ref)

        err = x_ref[0] + gt_ref[0] * f_ref[0] - t_ref[0]
        dx_ref[0] = err * (1.0 / D)
        loss_ref[...] += 0.5 * jnp.sum(jnp.mean(err * err, axis=-1, keepdims=True), axis=0, keepdims=True)

    return pl.pallas_call(
        body, name=name, grid=(B, S // ROW_TILE),
        in_specs=[_row_spec(D), _row_spec(D), _mod_spec(), _row_spec(D)],
        out_specs=[_row_spec(D), pl.BlockSpec((1, 1), lambda b, i: (0, 0))],
        out_shape=[jax.ShapeDtypeStruct((B, S, D), F32), jax.ShapeDtypeStruct((1, 1), F32)],
        compiler_params=_cparams(),
    )(x1, ffo, gate, target)


def resid_bwd(dx, y, gate, name):
    B, S, D = dx.shape

    def body(dx_ref, y_ref, gt_ref, dy_ref, dg_ref):
        @pl.when(pl.program_id(1) == 0)
        def _():
            dg_ref[...] = jnp.zeros_like(dg_ref)

        d = dx_ref[0]
        dy_ref[0] = (gt_ref[0] * d).astype(BF16)
        dg_ref[0] += jnp.sum(d * y_ref[0], axis=0, keepdims=True)

    return pl.pallas_call(
        body, name=name, grid=(B, S // ROW_TILE),
        in_specs=[_row_spec(D), _row_spec(D), _mod_spec()],
        out_specs=[_row_spec(D), _mod_spec()],
        out_shape=[jax.ShapeDtypeStruct((B, S, D), BF16), jax.ShapeDtypeStruct((B, 1, D), F32)],
        compiler_params=_cparams(),
    )(dx, y, gate)


def modnorm_bwd(x, g, scale, shift, du, dx_res, name):
    B, S, D = x.shape

    def body(x_ref, g_ref, sc_ref, sh_ref, du_ref, dr_ref, dx_ref, dg_ref, dsc_ref, dsh_ref):
        first = jnp.logical_and(pl.program_id(0) == 0, pl.program_id(1) == 0)

        @pl.when(first)
        def _():
            dg_ref[...] = jnp.zeros_like(dg_ref)

        @pl.when(pl.program_id(1) == 0)
        def _():
            dsc_ref[...] = jnp.zeros_like(dsc_ref)
            dsh_ref[...] = jnp.zeros_like(dsh_ref)

        _, vjp = jax.vjp(_modnorm, x_ref[0], g_ref[...], sc_ref[0], sh_ref[0])
        dx, dg, dsc, dsh = vjp(du_ref[0].astype(F32))
        dx_ref[0] = dx + dr_ref[0]
        dg_ref[...] += dg
        dsc_ref[0] += dsc
        dsh_ref[0] += dsh

    return pl.pallas_call(
        body, name=name, grid=(B, S // ROW_TILE),
        in_specs=[_row_spec(D), _vec_spec(), _mod_spec(), _mod_spec(), _row_spec(D), _row_spec(D)],
        out_specs=[_row_spec(D), _vec_spec(), _mod_spec(), _mod_spec()],
        out_shape=[jax.ShapeDtypeStruct((B, S, D), F32), jax.ShapeDtypeStruct((1, D), F32),
                   jax.ShapeDtypeStruct((B, 1, D), F32), jax.ShapeDtypeStruct((B, 1, D), F32)],
        compiler_params=_cparams(),
    )(x, g, scale, shift, du, dx_res)


def _bucket_table(dilation):
    i = np.arange(ATT_BLOCK)[:, None]
    j = np.arange(2 * ATT_BLOCK)[None, :]
    delta = ATT_BLOCK + i - j
    dist = np.maximum(delta, 0) * dilation
    max_exact = N_BUCKETS // 2
    d = np.maximum(dist, max_exact).astype(np.float32)
    large = max_exact + (np.log(d / np.float32(max_exact)) / np.float32(math.log(MAX_DISTANCE / max_exact))
                         * np.float32(N_BUCKETS - max_exact)).astype(np.int32)
    large = np.minimum(large, N_BUCKETS - 1)
    return np.where(dist < max_exact, dist, large).astype(np.int32)


def _bucket_onehot(dilation):
    bt = jnp.asarray(_bucket_table(dilation).reshape(1, -1))
    return (bt == jnp.arange(N_BUCKETS, dtype=jnp.int32)[:, None]).astype(F32)


def bias_expand(rel_t, onehot, name):
    def fn(r, oh):
        return (_doth(r, oh),)
    return small_call(fn, [rel_t, onehot], [jax.ShapeDtypeStruct((rel_t.shape[0], onehot.shape[1]), F32)], name)[0]


def bias_reduce(dbias_flat, onehot, name):
    def fn(db, oh):
        return (_doth(db, oh, _NT),)
    return small_call(fn, [dbias_flat, onehot], [jax.ShapeDtypeStruct((dbias_flat.shape[0], N_BUCKETS), F32)], name)[0]


def _attn_tile(q, k, v, bias, qg, kg):
    qn = _rms(q) * qg
    kn = _rms(k) * kg
    s = bdot_nt(qn, kn) * (HEAD_DIM ** -0.5) + bias
    kl = k.shape[0]
    i = lax.broadcasted_iota(jnp.int32, (ATT_BLOCK, kl), 0)
    j = lax.broadcasted_iota(jnp.int32, (ATT_BLOCK, kl), 1) + (2 * ATT_BLOCK - kl)
    valid = jnp.logical_and(j >= i, j <= i + ATT_BLOCK)
    s = jnp.where(valid, s, -jnp.inf)
    m = jnp.max(s, axis=-1, keepdims=True)
    p = jnp.exp(s - m)
    l = jnp.sum(p, axis=-1, keepdims=True)
    o = bdot_nn(p, v) / l
    lse = jnp.broadcast_to(m + jnp.log(l), (ATT_BLOCK, HEAD_DIM))
    return o, lse


def _attn_tiles(dilation):
    nb = SEQ // dilation // ATT_BLOCK
    return [(r, n) for r in range(dilation) for n in range(nb)]


def _attn_rows(r, n, dilation, nblk=1):
    if dilation == 1:
        return pl.ds(r + n * ATT_BLOCK, nblk * ATT_BLOCK)
    return pl.ds(r + n * ATT_BLOCK * dilation, nblk * ATT_BLOCK, stride=dilation)


_QL, _KL, _VL = slice(0, 128), slice(128, 256), slice(256, 384)


def _qkv_specs(hb):
    return [pl.BlockSpec((None, SEQ, HEAD_DIM), functools.partial(lambda b, h, j: (b, 0, 3 * (hb + h) + j), j=j))
            for j in range(3)]


def attn_fwd(pa, bias, qg, kg, group, name):
    B = pa.shape[0]
    dilation = ATT_GROUPS[group][1]
    hb = group * 4

    def body(q_ref, k_ref, v_ref, b_ref, qg_ref, kg_ref, o_ref, l_ref):
        qg_, kg_ = qg_ref[...], kg_ref[...]
        for (r, n) in _attn_tiles(dilation):
            rows = _attn_rows(r, n, dilation)
            q = q_ref[rows, :]
            if n == 0:
                krows, bias_t = rows, b_ref[0, :, ATT_BLOCK:]
            else:
                krows, bias_t = _attn_rows(r, n - 1, dilation, 2), b_ref[0]
            o, lse = _attn_tile(q, k_ref[krows, :], v_ref[krows, :], bias_t, qg_, kg_)
            o_ref[rows, :] = o
            l_ref[rows, :] = lse

    head_out = pl.BlockSpec((None, SEQ, HEAD_DIM), lambda b, h: (b, 0, h))
    return pl.pallas_call(
        body, name=name, grid=(B, 4),
        in_specs=_qkv_specs(hb) + [
                  pl.BlockSpec((1, ATT_BLOCK, 2 * ATT_BLOCK), lambda b, h: (hb + h, 0, 0)),
                  pl.BlockSpec((1, HEAD_DIM), lambda b, h: (0, 0)),
                  pl.BlockSpec((1, HEAD_DIM), lambda b, h: (0, 0))],
        out_specs=[head_out, head_out],
        out_shape=[jax.ShapeDtypeStruct((B, SEQ, 512), F32), jax.ShapeDtypeStruct((B, SEQ, 512), F32)],
        compiler_params=_cparams(),
    )(pa, pa, pa, bias, qg, kg)


def attn_bwd(pa, bias, qg, kg, do, dlse, dpa, group, name):
    B = pa.shape[0]
    dilation = ATT_GROUPS[group][1]
    hb = group * 4

    def body(q_ref, k_ref, v_ref, b_ref, qg_ref, kg_ref, do_ref, dl_ref, dpa_in,
             dp_ref, db_ref, dqg_ref, dkg_ref, dq_s, dk_s, dv_s):
        del dpa_in
        h_id = pl.program_id(1)

        @pl.when(jnp.logical_and(pl.program_id(0) == 0, h_id == 0))
        def _():
            db_ref[...] = jnp.zeros_like(db_ref)
            dqg_ref[...] = jnp.zeros_like(dqg_ref)
            dkg_ref[...] = jnp.zeros_like(dkg_ref)

        dk_s[...] = jnp.zeros_like(dk_s)
        dv_s[...] = jnp.zeros_like(dv_s)
        qg_, kg_ = qg_ref[...], kg_ref[...]
        for (r, n) in _attn_tiles(dilation):
            rows = _attn_rows(r, n, dilation)
            q = q_ref[rows, :]
            if n == 0:
                krows, bias_t = rows, b_ref[0, :, ATT_BLOCK:]
            else:
                krows, bias_t = _attn_rows(r, n - 1, dilation, 2), b_ref[0]
            _, vjp = jax.vjp(_attn_tile, q, k_ref[krows, :], v_ref[krows, :], bias_t, qg_, kg_)
            dq, dk, dv, dbias, dqg, dkg = vjp((do_ref[rows, :], dl_ref[rows, :]))
            dq_s[rows, :] = dq
            dk_s[krows, :] += dk
            dv_s[krows, :] += dv
            if n == 0:
                db_ref[h_id, :, ATT_BLOCK:] += dbias
            else:
                db_ref[h_id] += dbias
            dqg_ref[...] += dqg
            dkg_ref[...] += dkg
        dp_ref[0, :, _QL] = dq_s[...].astype(BF16)
        dp_ref[0, :, _KL] = dk_s[...].astype(BF16)
        dp_ref[0, :, _VL] = dv_s[...].astype(BF16)

    const2 = lambda b, h: (0, 0)
    head_in = pl.BlockSpec((None, SEQ, HEAD_DIM), lambda b, h: (b, 0, h))
    head_blk = pl.BlockSpec((1, SEQ, ATT_HEAD_COLS), lambda b, h: (b, 0, hb + h))
    return pl.pallas_call(
        body, name=name, grid=(B, 4),
        in_specs=_qkv_specs(hb) + [
                  pl.BlockSpec((1, ATT_BLOCK, 2 * ATT_BLOCK), lambda b, h: (hb + h, 0, 0)),
                  pl.BlockSpec((1, HEAD_DIM), const2), pl.BlockSpec((1, HEAD_DIM), const2),
                  head_in, head_in,
                  pl.BlockSpec(memory_space=pl.ANY)],
        out_specs=[head_blk,
                   pl.BlockSpec((4, ATT_BLOCK, 2 * ATT_BLOCK), lambda b, h: (0, 0, 0)),
                   pl.BlockSpec((1, HEAD_DIM), const2), pl.BlockSpec((1, HEAD_DIM), const2)],
        out_shape=[jax.ShapeDtypeStruct(dpa.shape, BF16),
                   jax.ShapeDtypeStruct((4, ATT_BLOCK, 2 * ATT_BLOCK), F32),
                   jax.ShapeDtypeStruct((1, HEAD_DIM), F32), jax.ShapeDtypeStruct((1, HEAD_DIM), F32)],
        scratch_shapes=[pltpu.VMEM((SEQ, HEAD_DIM), F32)] * 3,
        input_output_aliases={8: 0},
        compiler_params=_cparams(),
    )(pa, pa, pa, bias, qg, kg, do, dlse, dpa)


def _merge(o0, o1, o2, l0, l1, l2):
    mx = jnp.maximum(jnp.maximum(l0, l1), l2)
    e0, e1, e2 = jnp.exp(l0 - mx), jnp.exp(l1 - mx), jnp.exp(l2 - mx)
    den = e0 + e1 + e2
    return (e0 / den) * o0 + (e1 / den) * o1 + (e2 / den) * o2


def merge_fwd(os_, ls_, name):
    B = os_[0].shape[0]

    def body(o0, o1, o2, l0, l1, l2, a_ref):
        a_ref[0] = _merge(o0[0], o1[0], o2[0], l0[0], l1[0], l2[0]).astype(BF16)

    return pl.pallas_call(
        body, name=name, grid=(B, SEQ // ROW_TILE),
        in_specs=[_row_spec(512)] * 6, out_specs=_row_spec(512),
        out_shape=jax.ShapeDtypeStruct((B, SEQ, 512), BF16),
        compiler_params=_cparams(),
    )(*os_, *ls_)


def merge_bwd(os_, ls_, datt, name):
    B = os_[0].shape[0]

    def body(o0, o1, o2, l0, l1, l2, da_ref, *outs):
        _, vjp = jax.vjp(_merge, o0[0], o1[0], o2[0], l0[0], l1[0], l2[0])
        for o_ref, g in zip(outs, vjp(da_ref[0])):
            o_ref[0] = g

    return pl.pallas_call(
        body, name=name, grid=(B, SEQ // ROW_TILE),
        in_specs=[_row_spec(512)] * 7, out_specs=[_row_spec(512)] * 6,
        out_shape=[jax.ShapeDtypeStruct((B, SEQ, 512), F32)] * 6,
        compiler_params=_cparams(),
    )(*os_, *ls_, datt)


def _gate_mix(ga, gm, ya, ym):
    return jax.nn.sigmoid(ga) * ya + jax.nn.sigmoid(gm) * ym


def gate_fwd(pg, ya, ym, name):
    B = pg.shape[0]

    def body(ga, gm, ya_ref, ym_ref, z_ref):
        z_ref[0] = _gate_mix(ga[0], gm[0], ya_ref[0], ym_ref[0]).astype(BF16)

    return pl.pallas_call(
        body, name=name, grid=(B, SEQ // ROW_TILE),
        in_specs=[pl.BlockSpec((1, ROW_TILE, D_MODEL), lambda b, i: (b, i, 0)),
                  pl.BlockSpec((1, ROW_TILE, D_MODEL), lambda b, i: (b, i, 1)),
                  _row_spec(D_MODEL), _row_spec(D_MODEL)],
        out_specs=_row_spec(D_MODEL),
        out_shape=jax.ShapeDtypeStruct((B, SEQ, D_MODEL), BF16),
        compiler_params=_cparams(),
    )(pg, pg, ya, ym)


def gate_bwd(pg, ya, ym, dz, name):
    B = pg.shape[0]

    def body(ga, gm, ya_ref, ym_ref, dz_ref, dpg_ref, dya_ref, dym_ref):
        _, vjp = jax.vjp(_gate_mix, ga[0], gm[0], ya_ref[0], ym_ref[0])
        dga, dgm, dya, dym = vjp(dz_ref[0])
        dpg_ref[0, :, :D_MODEL] = dga.astype(BF16)
        dpg_ref[0, :, D_MODEL:] = dgm.astype(BF16)
        dya_ref[0] = dya.astype(BF16)
        dym_ref[0] = dym.astype(BF16)

    return pl.pallas_call(
        body, name=name, grid=(B, SEQ // ROW_TILE),
        in_specs=[pl.BlockSpec((1, ROW_TILE, D_MODEL), lambda b, i: (b, i, 0)),
                  pl.BlockSpec((1, ROW_TILE, D_MODEL), lambda b, i: (b, i, 1)),
                  _row_spec(D_MODEL), _row_spec(D_MODEL), _row_spec(D_MODEL)],
        out_specs=[_row_spec(GATE_COLS), _row_spec(D_MODEL), _row_spec(D_MODEL)],
        out_shape=[jax.ShapeDtypeStruct((B, SEQ, GATE_COLS), BF16),
                   jax.ShapeDtypeStruct((B, SEQ, D_MODEL), BF16), jax.ShapeDtypeStruct((B, SEQ, D_MODEL), BF16)],
        compiler_params=_cparams(),
    )(pg, pg, ya, ym, dz)


def _log_sigmoid(x):
    return jnp.minimum(x, 0.0) - jnp.log(1.0 + jnp.exp(-jnp.abs(x)))


def _head_mask(e):
    lane = lax.broadcasted_iota(jnp.int32, (1, 128), 1)
    return jnp.logical_and(lane >= e * 64, lane < (e + 1) * 64).astype(F32)


def _bmxu(a, b, ca, cb):
    return lax.dot_general(a.astype(BF16), b.astype(BF16), (((ca,), (cb,)), ((0,), (0,))), preferred_element_type=F32)


@jax.custom_vjp
def cdot_nt(a, b):
    return _bmxu(a, b, 2, 2)


cdot_nt.defvjp(lambda a, b: (_bmxu(a, b, 2, 2), (a, b)),
               lambda res, g: (_bmxu(g, res[1], 2, 1), _bmxu(g, res[0], 1, 1)))


@jax.custom_vjp
def cdot_nn(a, b):
    return _bmxu(a, b, 2, 1)


cdot_nn.defvjp(lambda a, b: (_bmxu(a, b, 2, 1), (a, b)),
               lambda res, g: (_bmxu(g, res[1], 2, 2), _bmxu(res[0], g, 1, 1)))


@jax.custom_vjp
def cdot_tn(a, b):
    return _bmxu(a, b, 1, 1)


cdot_tn.defvjp(lambda a, b: (_bmxu(a, b, 1, 1), (a, b)),
               lambda res, g: (_bmxu(res[1], g, 2, 2), _bmxu(res[0], g, 2, 1)))


def _cdoth(a, b):
    return lax.dot_general(a, b, (((2,), (1,)), ((0,), (0,))), precision=lax.Precision.HIGHEST,
                           preferred_element_type=F32)


def _ml_intra(q2, k2, v, ifb, *, e):
    c, L = N_CHUNKS, ML_CHUNK
    hm = _head_mask(e)
    q3 = (q2 * hm).reshape(c, L, 128)
    k3 = (k2 * hm).reshape(c, L, 128)
    v3 = v.reshape(c, L, 128)
    if3 = ifb.reshape(c, L, 128)
    lanes = lax.broadcasted_iota(jnp.int32, (c, L, 128), 2)
    li = jnp.sum(jnp.where(lanes == e, if3, 0.0), axis=-1, keepdims=True)
    ri = lax.broadcasted_iota(jnp.int32, (L, L), 0)
    ci = lax.broadcasted_iota(jnp.int32, (L, L), 1)
    causal = ri >= ci
    tri = jnp.broadcast_to(causal.astype(F32), (c, L, L))
    cs = _cdoth(tri, _log_sigmoid(if3))
    b = jnp.sum(jnp.where(lanes == 2 + e, cs, 0.0), axis=-1, keepdims=True)
    last = lax.broadcasted_iota(jnp.int32, (1, L, 1), 1) == L - 1
    b_end = jnp.sum(jnp.where(last, b, 0.0), axis=1, keepdims=True)
    rrow = _cdoth(jnp.ones((c, L, L), F32), (ri == ci).astype(F32) * (li - b))
    Dm = jnp.where(causal, b + rrow, -jnp.inf)
    mD = lax.stop_gradient(jnp.max(Dm, axis=-1, keepdims=True))
    P0 = cdot_nt(q3, k3) * jnp.exp(Dm - mD)
    H0 = cdot_nn(P0, v3)
    r0 = jnp.sum(P0, axis=-1, keepdims=True)
    g = b_end - b + li
    mg = lax.stop_gradient(jnp.max(g, axis=1, keepdims=True))
    kw = jnp.exp(g - mg) * k3
    return H0, r0, cdot_tn(kw, v3), jnp.sum(kw, axis=1, keepdims=True), b, b_end, mD, mg


def _ml_inter(q2, mo, gn, H0, r0, b, C_in, n_in, *, mD, m_in, e):
    c, L = N_CHUNKS, ML_CHUNK
    q3 = (q2 * _head_mask(e)).reshape(c, L, 128)
    a = b + m_in
    m_t = lax.stop_gradient(jnp.maximum(a, mD))
    c1 = jnp.exp(mD - m_t)
    c2 = jnp.exp(a - m_t)
    num = c1 * H0 + c2 * cdot_nn(q3, C_in)
    nq = c1 * r0 + c2 * jnp.sum(q3 * n_in, axis=-1, keepdims=True)
    h = num / jnp.maximum(jnp.abs(nq), jnp.exp(-m_t))
    hg = _rms(h) * gn * jax.nn.sigmoid(mo.reshape(c, L, 128))
    return hg.reshape(c * L, 128)


def _state_sweep(U_s, un_s, be_s, mg_s, Cin_s, nin_s, min_s, al_s, bt_s):
    def step(j, carry):
        C, n, m = carry
        Cin_s[j], nin_s[j], min_s[j] = C, n, m
        m_out = jnp.maximum(be_s[j] + m, mg_s[j])
        al = jnp.exp(be_s[j] + m - m_out)
        bt = jnp.exp(mg_s[j] - m_out)
        al_s[j], bt_s[j] = al, bt
        return al * C + bt * U_s[j], al * n + bt * un_s[j], m_out

    lax.fori_loop(0, N_CHUNKS, step, (jnp.zeros((128, 128), F32), jnp.zeros((1, 128), F32), jnp.zeros((1, 1), F32)))


def _state_sweep_bwd(U_s, un_s, dbe_s, Cin_s, nin_s, dCp_s, dnp_s, al_s, bt_s):
    def step(t, carry):
        j = N_CHUNKS - 1 - t
        dC, dn = carry
        al, bt = al_s[j], bt_s[j]
        U_s[j] = bt * dC
        un_s[j] = bt * dn
        dal = jnp.sum(jnp.sum(dC * Cin_s[j], axis=1, keepdims=True), axis=0, keepdims=True) \
            + jnp.sum(dn * nin_s[j], axis=1, keepdims=True)
        dbe_s[j] = dal * al
        return dCp_s[j] + al * dC, dnp_s[j] + al * dn

    lax.fori_loop(0, N_CHUNKS, step, (jnp.zeros((128, 128), F32), jnp.zeros((1, 128), F32)))


def _state_scratch():
    c = N_CHUNKS
    return [pltpu.VMEM((c, 128, 128), F32), pltpu.VMEM((c, 1, 128), F32), pltpu.VMEM((c, 1, 1), F32),
            pltpu.VMEM((c, 1, 1), F32),
            pltpu.VMEM((c, 128, 128), F32), pltpu.VMEM((c, 1, 128), F32), pltpu.VMEM((c, 1, 1), F32),
            pltpu.VMEM((c, 1, 1), F32), pltpu.VMEM((c, 1, 1), F32)]


def _shift_down(x, s):
    if s == 0:
        return x
    rows = lax.broadcasted_iota(jnp.int32, x.shape, 0)
    return jnp.where(rows >= s, pltpu.roll(x, s, 0), 0.0)


def _shift_up(x, s):
    if s == 0:
        return x
    S = x.shape[0]
    rows = lax.broadcasted_iota(jnp.int32, x.shape, 0)
    return jnp.where(rows < S - s, pltpu.roll(x, S - s, 0), 0.0)


def _conv_pre(x, cw, cb):
    y = cb + cw[3:4, :] * x
    for j in range(3):
        y = y + cw[j:j + 1, :] * _shift_down(x, 3 - j)
    return y


def _conv_bwd(x, cw, dpre):
    dx = cw[3:4, :] * dpre
    dcw = [None] * 4
    dcw[3] = jnp.sum(dpre * x, axis=0, keepdims=True)
    for j in range(3):
        dx = dx + cw[j:j + 1, :] * _shift_up(dpre, 3 - j)
        dcw[j] = jnp.sum(dpre * _shift_down(x, 3 - j), axis=0, keepdims=True)
    return dx, dcw, jnp.sum(dpre, axis=0, keepdims=True)


def _silu(z):
    return z * jax.nn.sigmoid(z)


def _dsilu(z):
    s = jax.nn.sigmoid(z)
    return s * (1.0 + z * (1.0 - s))


_ML_Q, _ML_K = slice(0, 128), slice(128, 256)
_ML_IF = slice(768, 896)


def _ml_v(e):
    return slice(256 + e * 128, 384 + e * 128)


def _ml_o(e):
    return slice(512 + e * 128, 640 + e * 128)


def _ml_specs():
    pair = lambda b, p: (b, 0, p)
    return [pl.BlockSpec((1, SEQ, ML_PAIR_COLS), pair),
            pl.BlockSpec((1, 4, 128), lambda b, p: (p, 0, 0)),
            pl.BlockSpec((1, 4, 128), lambda b, p: (4 + p, 0, 0)),
            pl.BlockSpec((1, 1, 128), lambda b, p: (p, 0, 0)),
            pl.BlockSpec((1, 1, 128), lambda b, p: (4 + p, 0, 0)),
            pl.BlockSpec((1, 1, 128), lambda b, p: (p, 0, 0)),
            pl.BlockSpec((1, 1, 256), lambda b, p: (p, 0, 0))]


def mlstm_fwd(pm, cw8, cb8, bifp, gn4, name):
    B = pm.shape[0]

    def body(p_ref, cwq, cwk, cbq, cbk, bif_ref, gn_ref, hg_ref, *st):
        U_s, un_s, be_s, mg_s, Cin_s, nin_s, min_s, al_s, bt_s = st
        qc = _silu(_conv_pre(p_ref[0, :, _ML_Q], cwq[0], cbq[0]))
        kc = _silu(_conv_pre(p_ref[0, :, _ML_K], cwk[0], cbk[0])) * (64 ** -0.5)
        ifb = p_ref[0, :, _ML_IF] + bif_ref[0]
        for e in range(2):
            lanes = slice(e * 128, (e + 1) * 128)
            H0, r0, U, un, b, b_end, mD, mg = _ml_intra(qc, kc, p_ref[0, :, _ml_v(e)], ifb, e=e)
            U_s[...], un_s[...], be_s[...], mg_s[...] = U, un, b_end, mg
            _state_sweep(*st)
            hg = _ml_inter(qc, p_ref[0, :, _ml_o(e)], gn_ref[0, :, lanes], H0, r0, b, Cin_s[...], nin_s[...],
                           mD=mD, m_in=min_s[...], e=e)
            hg_ref[0, :, lanes] = hg.astype(BF16)

    return pl.pallas_call(
        body, name=name, grid=(B, ML_PAIRS),
        in_specs=_ml_specs(),
        out_specs=pl.BlockSpec((1, SEQ, 256), lambda b, p: (b, 0, p)),
        out_shape=jax.ShapeDtypeStruct((B, SEQ, D_MODEL), BF16),
        scratch_shapes=_state_scratch(),
        compiler_params=_cparams(),
    )(pm, cw8, cw8, cb8, cb8, bifp, gn4)


def mlstm_bwd(pm, cw8, cb8, bifp, gn4, dhg, name):
    B = pm.shape[0]

    def body(p_ref, cwq, cwk, cbq, cbk, bif_ref, gn_ref, dh_ref,
             dp_ref, dcw_ref, dcb_ref, dbif_ref, dgn_ref, *scr):
        st = scr[:9]
        U_s, un_s, be_s, mg_s, Cin_s, nin_s, min_s, al_s, bt_s = st
        dCp_s, dnp_s, dbe_s = scr[9:]
        p_id = pl.program_id(1)

        @pl.when(jnp.logical_and(pl.program_id(0) == 0, p_id == 0))
        def _():
            dcw_ref[...] = jnp.zeros_like(dcw_ref)
            dcb_ref[...] = jnp.zeros_like(dcb_ref)
            dbif_ref[...] = jnp.zeros_like(dbif_ref)
            dgn_ref[...] = jnp.zeros_like(dgn_ref)

        qc = _silu(_conv_pre(p_ref[0, :, _ML_Q], cwq[0], cbq[0]))
        kc = _silu(_conv_pre(p_ref[0, :, _ML_K], cwk[0], cbk[0])) * (64 ** -0.5)
        ifb = p_ref[0, :, _ML_IF] + bif_ref[0]
        dq = jnp.zeros((SEQ, 128), F32)
        dk = jnp.zeros((SEQ, 128), F32)
        difb = jnp.zeros((SEQ, 128), F32)
        for e in range(2):
            lanes = slice(e * 128, (e + 1) * 128)
            (H0, r0, U, un, b, b_end, mD, mg), vjp1 = jax.vjp(functools.partial(_ml_intra, e=e), qc, kc,
                                                              p_ref[0, :, _ml_v(e)], ifb)
            U_s[...], un_s[...], be_s[...], mg_s[...] = U, un, b_end, mg
            _state_sweep(*st)
            _, vjp3 = jax.vjp(functools.partial(_ml_inter, mD=mD, m_in=min_s[...], e=e), qc, p_ref[0, :, _ml_o(e)],
                              gn_ref[0, :, lanes], H0, r0, b, Cin_s[...], nin_s[...])
            dq_a, dmo, dgn, dH0, dr0, db_a, dCp, dnp = vjp3(dh_ref[0, :, lanes])
            dCp_s[...], dnp_s[...] = dCp, dnp
            _state_sweep_bwd(U_s, un_s, dbe_s, Cin_s, nin_s, dCp_s, dnp_s, al_s, bt_s)
            dq_b, dk_b, dv, difb_e = vjp1((dH0, dr0, U_s[...], un_s[...], db_a, dbe_s[...],
                                           jnp.zeros_like(mD), jnp.zeros_like(mg)))
            dq, dk, difb = dq + dq_a + dq_b, dk + dk_b, difb + difb_e
            dp_ref[0, :, _ml_v(e)] = dv.astype(BF16)
            dp_ref[0, :, _ml_o(e)] = dmo.astype(BF16)
            dgn_ref[p_id, :, lanes] += dgn
        dp_ref[0, :, _ML_IF] = difb.astype(BF16)
        dbif_ref[p_id] += jnp.sum(difb, axis=0, keepdims=True)

        for (sl, cw, cb, d, blk, scale) in ((_ML_Q, cwq, cbq, dq, p_id, 1.0), (_ML_K, cwk, cbk, dk, 4 + p_id, 64 ** -0.5)):
            xr = p_ref[0, :, sl]
            dpre = d * scale * _dsilu(_conv_pre(xr, cw[0], cb[0]))
            dx, dcw, dcb = _conv_bwd(xr, cw[0], dpre)
            dp_ref[0, :, sl] = dx.astype(BF16)
            for j in range(4):
                dcw_ref[blk, j:j + 1, :] += dcw[j]
            dcb_ref[blk] += dcb

    full3 = lambda b, p: (0, 0, 0)
    return pl.pallas_call(
        body, name=name, grid=(B, ML_PAIRS),
        in_specs=[pl.BlockSpec((1, SEQ, ML_PAIR_COLS), lambda b, p: (b, 0, p), pipeline_mode=pl.Buffered(1))]
        + _ml_specs()[1:] + [pl.BlockSpec((1, SEQ, 256), lambda b, p: (b, 0, p), pipeline_mode=pl.Buffered(1))],
        out_specs=[pl.BlockSpec((1, SEQ, ML_PAIR_COLS), lambda b, p: (b, 0, p)),
                   pl.BlockSpec((8, 4, 128), full3), pl.BlockSpec((8, 1, 128), full3),
                   pl.BlockSpec((4, 1, 128), full3), pl.BlockSpec((4, 1, 256), full3)],
        out_shape=[jax.ShapeDtypeStruct((B, SEQ, ML_COLS), BF16),
                   jax.ShapeDtypeStruct((8, 4, 128), F32), jax.ShapeDtypeStruct((8, 1, 128), F32),
                   jax.ShapeDtypeStruct((4, 1, 128), F32), jax.ShapeDtypeStruct((4, 1, 256), F32)],
        scratch_shapes=_state_scratch() + [pltpu.VMEM((N_CHUNKS, 128, 128), F32), pltpu.VMEM((N_CHUNKS, 1, 128), F32),
                                           pltpu.VMEM((N_CHUNKS, 1, 1), F32)],
        compiler_params=_cparams(),
    )(pm, cw8, cw8, cb8, cb8, bifp, gn4, dhg)


def _adamw(w, g, m, v):
    m = ADAM_B1 * m + (1.0 - ADAM_B1) * g
    v = ADAM_B2 * v + (1.0 - ADAM_B2) * (g * g)
    m_hat = m / (1.0 - ADAM_B1 ** ADAM_STEP)
    v_hat = v / (1.0 - ADAM_B2 ** ADAM_STEP)
    delta = -ADAM_LR * (m_hat / (jnp.sqrt(v_hat) + ADAM_EPS) + ADAM_WD * w)
    return delta, m, v


def adamw(w, g, m, v, name, parts=False):
    R, C = w.shape
    tr = _pick(R, (256, 128, 64, 32, 16, 8, 4, 2, 1)) if R * C * 4 > (1 << 20) else R
    spec = pl.BlockSpec((tr, C), lambda i: (i, 0))
    g_spec = pl.BlockSpec((N_DEV, tr, C), lambda i: (0, i, 0)) if parts else spec

    def body(w_ref, g_ref, m_ref, v_ref, go_ref, d_ref, mo_ref, vo_ref):
        if parts:
            g = g_ref[0].astype(F32)
            for k in range(1, N_DEV):
                g = g + g_ref[k].astype(F32)
        else:
            g = g_ref[...]
        d, mn, vn = _adamw(w_ref[...], g, m_ref[...], v_ref[...])
        go_ref[...], d_ref[...], mo_ref[...], vo_ref[...] = g, d, mn, vn

    return pl.pallas_call(
        body, name=name, grid=(R // tr,),
        in_specs=[spec, g_spec, spec, spec], out_specs=[spec] * 4,
        out_shape=[jax.ShapeDtypeStruct((R, C), F32)] * 4,
        compiler_params=_cparams(),
    )(w, g, m, v)


def _mesh_pos():
    return lax.axis_index("x"), lax.axis_index("y"), lax.axis_index("c")


def _flip(pos, f):
    x, y, c = pos
    return (1 - x if f & 4 else x, 1 - y if f & 2 else y, 1 - c if f & 1 else c)


def _index(pos):
    return 4 * pos[0] + 2 * pos[1] + pos[2]


def _exchange(arrs, name, scatter):
    n = len(arrs)

    def body(*refs):
        ins, outs = refs[:n], refs[n:2 * n]
        send, recv, lsem = refs[2 * n:]
        me = _mesh_pos()
        mine = _index(me)
        copies = []
        for i in range(n):
            src = ins[i].at[mine] if scatter else ins[i]
            loc = pltpu.make_async_copy(src, outs[i].at[mine], lsem.at[i])
            loc.start()
            copies.append(loc)
            for f in range(1, N_DEV):
                peer = _flip(me, f)
                src = ins[i].at[_index(peer)] if scatter else ins[i]
                cp = pltpu.make_async_remote_copy(
                    src_ref=src, dst_ref=outs[i].at[mine],
                    send_sem=send.at[i * 7 + f - 1], recv_sem=recv.at[i * 7 + f - 1],
                    device_id=peer, device_id_type=pl.DeviceIdType.MESH)
                cp.start()
                copies.append(cp)
        for cp in copies:
            cp.wait()

    any_spec = pl.BlockSpec(memory_space=pl.ANY)
    out_shape = [jax.ShapeDtypeStruct(a.shape if scatter else (N_DEV,) + a.shape, a.dtype) for a in arrs]
    res = pl.pallas_call(
        body, name=name,
        in_specs=[any_spec] * n, out_specs=[any_spec] * n, out_shape=out_shape,
        scratch_shapes=[pltpu.SemaphoreType.DMA((7 * n,)), pltpu.SemaphoreType.DMA((7 * n,)),
                        pltpu.SemaphoreType.DMA((n,))],
        compiler_params=_cparams(),
    )(*arrs)
    return list(res)


def all_gather(arrs, name):
    return _exchange(arrs, name, False)


def all_to_all(arrs, name):
    return _exchange(arrs, name, True)


_HBM = pl.BlockSpec(memory_space=pltpu.HBM)
_SEM = pl.BlockSpec(memory_space=pltpu.SEMAPHORE)
_EFFECT = pltpu.SideEffectType.DATAFLOW_SIDE_EFFECTING


def _split_copies(ins, lands, send, recv, scatter, waiting):
    me = _mesh_pos()
    mine = _index(me)
    copies = []
    for i in range(len(ins)):
        for f in range(1, N_DEV):
            peer = _flip(me, f)
            src = ins[i].at[_index(peer)] if scatter else ins[i]
            copies.append(pltpu.make_async_remote_copy(
                src_ref=src, dst_ref=lands[i].at[_index(peer) if waiting else mine],
                send_sem=send.at[i * 7 + f - 1], recv_sem=recv.at[i * 7 + f - 1],
                device_id=peer, device_id_type=pl.DeviceIdType.MESH))
    return copies


def exchange_start(arrs, name, scatter):
    n = len(arrs)
    land_shapes = [a.shape if scatter else (N_DEV,) + a.shape for a in arrs]

    def body(*refs):
        ins, lands = refs[:n], refs[n:2 * n]
        send, recv = refs[2 * n], refs[2 * n + 1]
        token = refs[-1]
        for cp in _split_copies(ins, lands, send, recv, scatter, False):
            cp.start()
        token[...] = jnp.zeros_like(token)

    res = pl.pallas_call(
        body, name=name,
        out_shape=(pltpu.SemaphoreType.DMA((7 * n,)), pltpu.SemaphoreType.DMA((7 * n,)),
                   *[pltpu.HBM(a.shape, a.dtype) for a in arrs],
                   *[pltpu.HBM(s, a.dtype) for s, a in zip(land_shapes, arrs)],
                   jax.ShapeDtypeStruct((8, 128), F32)),
        in_specs=[_HBM] * (2 * n),
        out_specs=(_SEM, _SEM, *[_HBM] * (2 * n), pl.BlockSpec(memory_space=pltpu.VMEM)),
        input_output_aliases={i: 2 + i for i in range(2 * n)},
        compiler_params=pltpu.CompilerParams(has_side_effects=_EFFECT),
    )(*[pltpu.with_memory_space_constraint(a, pltpu.HBM) for a in arrs],
      *[pltpu.with_memory_space_constraint(lax.empty(s, a.dtype), pltpu.HBM) for s, a in zip(land_shapes, arrs)])
    return (res[0], res[1], list(res[2:2 + n]), list(res[2 + n:2 + 2 * n])), res[-1]


def exchange_wait(handle, after, name, scatter):
    send, recv, srcs, lands = handle
    n = len(srcs)

    def body(*refs):
        ins, lnd = refs[:n], refs[n:2 * n]
        send_, recv_ = refs[2 * n], refs[2 * n + 1]
        for cp in _split_copies(ins, lnd, send_, recv_, scatter, True):
            cp.wait_send()
            cp.wait_recv()

    res = pl.pallas_call(
        body, name=name,
        out_shape=(*[pltpu.HBM(a.shape, a.dtype) for a in srcs], *[pltpu.HBM(a.shape, a.dtype) for a in lands]),
        in_specs=[_HBM] * (2 * n) + [_SEM, _SEM, pl.BlockSpec(memory_space=pl.ANY)],
        out_specs=tuple([_HBM] * (2 * n)),
        input_output_aliases={i: i for i in range(2 * n)},
        compiler_params=pltpu.CompilerParams(has_side_effects=_EFFECT),
    )(*srcs, *lands, send, recv, after)
    return list(res[n:])


def _own_slot(land, own):
    return lax.dynamic_update_slice(land, own[None], (_index(_mesh_pos()),) + (0,) * own.ndim)


def cast_bf16(arrs, name):
    outs = []
    for i, a in enumerate(arrs):
        R, C = a.shape
        tr = _pick(R, (256, 128, 64, 32, 16, 8)) if R * C * 4 > (1 << 21) else R
        spec = pl.BlockSpec((tr, C), lambda i: (i, 0))

        def body(a_ref, o_ref):
            o_ref[...] = a_ref[...].astype(BF16)

        outs.append(pl.pallas_call(body, name=f"{name}_{i}", grid=(R // tr,), in_specs=[spec], out_specs=spec,
                                   out_shape=jax.ShapeDtypeStruct((R, C), BF16), compiler_params=_cparams())(a))
    return outs


def sum_parts(parts, name):
    def fn(p):
        g = p[0]
        for k in range(1, N_DEV):
            g = g + p[k]
        return (g,)
    return small_call(fn, [parts], [jax.ShapeDtypeStruct(parts.shape[1:], F32)], name)[0]


_SPLITS = np.cumsum([1536, 1536, 1536, 512, 512, 1024, 1024, 8, 8, 2048])[:-1].tolist()


def split_w_in(w):
    aq, ak, av, mq, mk, mv, mo, mi, mf, gates = jnp.split(w, _SPLITS, axis=1)
    R = w.shape[0]
    w_att = jnp.stack([aq.reshape(R, 12, 128), ak.reshape(R, 12, 128), av.reshape(R, 12, 128)], axis=2)
    gif = jnp.concatenate([mi.reshape(R, 4, 2), mf.reshape(R, 4, 2), jnp.zeros((R, 4, 124), w.dtype)], axis=2)
    w_ml = jnp.concatenate([mq.reshape(R, 4, 128), mk.reshape(R, 4, 128), mv.reshape(R, 4, 256),
                            mo.reshape(R, 4, 256), gif], axis=2)
    return w_att.reshape(R, ATT_COLS), w_ml.reshape(R, ML_COLS), gates


def merge_w_in(g_att, g_ml, g_gate):
    R = g_att.shape[0]
    a = g_att.reshape(R, 12, 3, 128)
    m = g_ml.reshape(R, 4, ML_PAIR_COLS)
    gif = m[:, :, 768:772]
    return jnp.concatenate([
        a[:, :, 0].reshape(R, 1536), a[:, :, 1].reshape(R, 1536), a[:, :, 2].reshape(R, 1536),
        m[:, :, 0:128].reshape(R, 512), m[:, :, 128:256].reshape(R, 512),
        m[:, :, 256:512].reshape(R, 1024), m[:, :, 512:768].reshape(R, 1024),
        gif[:, :, 0:2].reshape(R, 8), gif[:, :, 2:4].reshape(R, 8), g_gate], axis=1)


def _blk8(v, width=128):
    r = v.shape[0]
    return v.reshape(r, 1024 // width, width).transpose(1, 0, 2)


def _unblk8(v):
    nb, r, w = v.shape
    return v.transpose(1, 0, 2).reshape(r, nb * w)


def local_step(x, target, mods, w, small, late_w=None, early_g=None, w_in_g=None):
    late_w = late_w or (lambda after: w)
    big = {}
    early_g = early_g or (lambda g: big.update(g) or 0.0)
    w_in_g = w_in_g or (lambda g: big.update(w_in=g) or 0.0)
    B = x.shape[0]
    T = B * SEQ
    shift1, scale1, gate1, shift2, scale2, gate2 = mods
    f2 = lambda a: a.reshape(T, a.shape[-1])
    f3 = lambda a: a.reshape(B, SEQ, a.shape[-1])

    rel_t = jnp.pad(small["rel_bias"].T, ((0, 4), (0, 0)))
    onehots = [_bucket_onehot(d) for _, d in ATT_GROUPS]
    biases = [bias_expand(rel_t, oh, f"bias_expand{g}").reshape(16, ATT_BLOCK, 2 * ATT_BLOCK)
              for g, oh in enumerate(onehots)]
    qg, kg = small["q_norm_g"], small["k_norm_g"]
    cw8 = _blk8(small["conv_w"])
    cb8 = _blk8(small["conv_b"])
    b_if = small["b_if"].reshape(2, 4, 2)
    bifp = jnp.concatenate([b_if[0], b_if[1], jnp.zeros((4, 124), F32)], axis=1).reshape(4, 1, 128)
    gn4 = small["mlstm_norm_g"].reshape(4, 1, 256)

    u = modnorm_fwd(x, small["norm1_g"], scale1, shift1, "modnorm1")
    u2d = f2(u)
    pa = f3(matmul(u2d, w["w_att"], mode="nn", name="proj_att"))
    pm = f3(matmul(u2d, w["w_ml"], mode="nn", name="proj_ml"))
    pg = f3(matmul(u2d, w["w_gate"], mode="nn", name="proj_gate"))
    os_, ls_ = [], []
    for g in range(3):
        o, l = attn_fwd(pa, biases[g], qg, kg, g, f"attn_fwd{g}")
        os_.append(o)
        ls_.append(l)
    att = merge_fwd(os_, ls_, "merge_fwd")
    hg = mlstm_fwd(pm, cw8, cb8, bifp, gn4, "mlstm_fwd")
    w = {**w, **late_w(hg)}
    y_att = matmul(f2(att), w["w_att_out"], mode="nn", name="att_out")
    y_ml = matmul(f2(hg), w["w_ml_out"], mode="nn", name="ml_out")
    z = gate_fwd(pg, f3(y_att), f3(y_ml), "gate_fwd")
    y = matmul(f2(z), w["w_out"], mode="nn", name="out_proj")
    x1, u2 = resid_modnorm_fwd(x, f3(y), gate1, small["norm2_g"], scale2, shift2, "resid_modnorm2")
    pre, hdn = matmul(f2(u2), w["w_ff1"], mode="nn", name="ff1", out_dtypes=(F32, BF16),
                      epi=lambda acc: (acc, jnp.square(jnp.maximum(acc, 0.0))))
    ffo = matmul(hdn, w["w_ff2"], mode="nn", name="ff2")
    dx2, loss = resid_loss(x1, f3(ffo), gate2, target, "resid_loss")

    d_ffo, d_gate2 = resid_bwd(dx2, f3(ffo), gate2, "resid_bwd2")
    g_ff2 = matmul(hdn, f2(d_ffo), mode="tn", name="g_ff2", out_dtypes=(BF16,))
    d_pre = matmul(f2(d_ffo), w["w_ff2"], mode="nt", name="d_hdn", out_dtypes=(BF16,), extras=(pre,),
                   epi=lambda acc, p: (acc * (2.0 * jnp.maximum(p, 0.0)),))
    g_ff1 = matmul(f2(u2), d_pre, mode="tn", name="g_ff1", out_dtypes=(BF16,))
    du2 = matmul(d_pre, w["w_ff1"], mode="nt", name="d_u2")
    dx1, d_norm2, d_scale2, d_shift2 = modnorm_bwd(x1, small["norm2_g"], scale2, shift2, f3(du2), dx2, "modnorm_bwd2")
    dy, d_gate1 = resid_bwd(dx1, f3(y), gate1, "resid_bwd1")
    g_out = matmul(f2(z), f2(dy), mode="tn", name="g_out", out_dtypes=(BF16,))
    dz = matmul(f2(dy), w["w_out"], mode="nt", name="d_z")
    dpg, d_ya, d_ym = gate_bwd(pg, f3(y_att), f3(y_ml), f3(dz), "gate_bwd")
    g_att_out = matmul(f2(att), f2(d_ya), mode="tn", name="g_att_out", out_dtypes=(BF16,))
    d_att = matmul(f2(d_ya), w["w_att_out"], mode="nt", name="d_att")
    g_ml_out = matmul(f2(hg), f2(d_ym), mode="tn", name="g_ml_out", out_dtypes=(BF16,))
    d_hg = matmul(f2(d_ym), w["w_ml_out"], mode="nt", name="d_hg")
    order = early_g(dict(w_att_out=g_att_out, w_ml_out=g_ml_out, w_out=g_out, w_ff1=g_ff1, w_ff2=g_ff2))
    dmerge = merge_bwd(os_, ls_, f3(d_att), "merge_bwd")
    dpa = jnp.zeros((B, SEQ, ATT_COLS), BF16)
    d_rel = []
    d_qg = d_kg = None
    for g in range(3):
        dpa, dbias, dq_g, dk_g = attn_bwd(pa, biases[g], qg + order, kg, dmerge[g], dmerge[3 + g], dpa, g,
                                          f"attn_bwd{g}")
        db8 = jnp.pad(dbias.reshape(4, -1), ((0, 4), (0, 0)))
        d_rel.append(bias_reduce(db8, onehots[g], f"bias_reduce{g}")[:4])
        d_qg = dq_g if d_qg is None else d_qg + dq_g
        d_kg = dk_g if d_kg is None else d_kg + dk_g
    dpm, dcw8, dcb8, dbifp, dgn4 = mlstm_bwd(pm, cw8, cb8, bifp, gn4, f3(d_hg), "mlstm_bwd")
    g_w_att = matmul(u2d, f2(dpa), mode="tn", name="g_w_att", out_dtypes=(BF16,))
    g_w_ml = matmul(u2d, f2(dpm), mode="tn", name="g_w_ml", out_dtypes=(BF16,))
    g_w_gate = matmul(u2d, f2(dpg), mode="tn", name="g_w_gate", out_dtypes=(BF16,))
    order = w_in_g(merge_w_in(g_w_att, g_w_ml, g_w_gate))
    du = matmul(f2(dpa), w["w_att"], mode="nt", name="d_u_att")
    du = matmul(f2(dpm), w["w_ml"], mode="nt", name="d_u_ml", extras=(du,), epi=lambda acc, e: (acc + e,))
    du = matmul(f2(dpg), w["w_gate"], mode="nt", name="d_u_gate", extras=(du,), epi=lambda acc, e: (acc + e,))
    grad_x, d_norm1, d_scale1, d_shift1 = modnorm_bwd(x, small["norm1_g"] + order, scale1, shift1, f3(du), dx1,
                                                      "modnorm_bwd1")

    d_mods = (d_shift1, d_scale1, d_gate1, d_shift2, d_scale2, d_gate2)
    dbif = dbifp.reshape(4, 128)
    small_g = dict(
        norm1_g=d_norm1, norm2_g=d_norm2,
        b_if=jnp.stack([dbif[:, 0:2].reshape(8), dbif[:, 2:4].reshape(8)]),
        conv_w=_unblk8(dcw8), conv_b=_unblk8(dcb8), q_norm_g=d_qg, k_norm_g=d_kg,
        rel_bias=jnp.concatenate(d_rel, axis=0).T,
        mlstm_norm_g=dgn4.reshape(1, 1024))
    return loss, grad_x, d_mods, big, small_g


_SMALL = (("b_ada", 6144), ("norm1_g", 1024), ("norm2_g", 1024), ("b_if", 16), ("conv_b", 1024),
          ("q_norm_g", 128), ("k_norm_g", 128), ("rel_bias", 384), ("mlstm_norm_g", 1024), ("conv_w", 4096))
_SMALL_ROWS = 120
_REPL = _SMALL[:-1]
_REPL_ROWS = 86


def _pack(d, names, rows):
    flat = jnp.concatenate([d[k].reshape(-1) for k, _ in names])
    return jnp.pad(flat, (0, rows * 128 - flat.shape[0])).reshape(rows, 128)


def _unpack(slab, names, shapes):
    flat = slab.reshape(-1)
    out, off = {}, 0
    for k, nel in names:
        out[k] = flat[off:off + nel].reshape(shapes[k])
        off += nel
    return out


def kernel(x, c, w_ada, b_ada, norm1_g, norm2_g, w_in, b_if, conv_w, conv_b, q_norm_g, k_norm_g, rel_bias, mlstm_norm_g, w_att_out, w_ml_out, w_out, w_ff1, w_ff2, loss_target, m_w_ada, m_b_ada, m_norm1_g, m_norm2_g, m_w_in, m_b_if, m_conv_w, m_conv_b, m_q_norm_g, m_k_norm_g, m_rel_bias, m_mlstm_norm_g, m_w_att_out, m_w_ml_out, m_w_out, m_w_ff1, m_w_ff2, v_w_ada, v_b_ada, v_norm1_g, v_norm2_g, v_w_in, v_b_if, v_conv_w, v_conv_b, v_q_norm_g, v_k_norm_g, v_rel_bias, v_mlstm_norm_g, v_w_att_out, v_w_ml_out, v_w_out, v_w_ff1, v_w_ff2):
    P = dict(w_ada=w_ada, b_ada=b_ada, norm1_g=norm1_g, norm2_g=norm2_g, w_in=w_in, b_if=b_if, conv_w=conv_w,
             conv_b=conv_b, q_norm_g=q_norm_g, k_norm_g=k_norm_g, rel_bias=rel_bias, mlstm_norm_g=mlstm_norm_g,
             w_att_out=w_att_out, w_ml_out=w_ml_out, w_out=w_out, w_ff1=w_ff1, w_ff2=w_ff2)
    M = dict(w_ada=m_w_ada, b_ada=m_b_ada, norm1_g=m_norm1_g, norm2_g=m_norm2_g, w_in=m_w_in, b_if=m_b_if,
             conv_w=m_conv_w, conv_b=m_conv_b, q_norm_g=m_q_norm_g, k_norm_g=m_k_norm_g, rel_bias=m_rel_bias,
             mlstm_norm_g=m_mlstm_norm_g, w_att_out=m_w_att_out, w_ml_out=m_w_ml_out, w_out=m_w_out,
             w_ff1=m_w_ff1, w_ff2=m_w_ff2)
    V = dict(w_ada=v_w_ada, b_ada=v_b_ada, norm1_g=v_norm1_g, norm2_g=v_norm2_g, w_in=v_w_in, b_if=v_b_if,
             conv_w=v_conv_w, conv_b=v_conv_b, q_norm_g=v_q_norm_g, k_norm_g=v_k_norm_g, rel_bias=v_rel_bias,
             mlstm_norm_g=v_mlstm_norm_g, w_att_out=v_w_att_out, w_ml_out=v_w_ml_out, w_out=v_w_out,
             w_ff1=v_w_ff1, w_ff2=v_w_ff2)
    names = list(P)
    shapes = {k: P[k].shape for k in names}
    B = x.shape[0]
    me = _index(_mesh_pos())

    big_names = ("w_in", "w_att_out", "w_ml_out", "w_out", "w_ff1", "w_ff2")
    shards = cast_bf16([P[k][0] for k in big_names], "cast_w")
    w_in_g8, c8, conv_w8 = all_gather([shards[0], c, conv_w[0]], "gather_w_in")
    late_handle, late_order = exchange_start(shards[1:], "gather_late_start", False)
    c_all = c8.reshape(N_DEV * B, D_MODEL)
    conv_w_full = conv_w8.transpose(1, 0, 2).reshape(4, 1024)
    w_att, w_ml, w_gate = split_w_in(w_in_g8.transpose(1, 0, 2).reshape(D_MODEL, D_IN))
    w = dict(w_att=w_att, w_ml=w_ml, w_gate=w_gate)

    def late_w(after):
        lands = exchange_wait(late_handle, after, "gather_late_wait", False)
        gw = dict(zip(big_names[1:], [_own_slot(l, s) for l, s in zip(lands, shards[1:])]))
        return dict(w_att_out=gw["w_att_out"].transpose(1, 0, 2).reshape(512, D_MODEL),
                    w_ml_out=gw["w_ml_out"].reshape(D_MODEL, D_MODEL), w_out=gw["w_out"].reshape(D_MODEL, D_MODEL),
                    w_ff1=gw["w_ff1"].transpose(1, 0, 2).reshape(D_MODEL, D_FF),
                    w_ff2=gw["w_ff2"].reshape(D_FF, D_MODEL))

    pending = {}

    def send_grads(key, blocks, name):
        handle, order = exchange_start(blocks, name, True)
        pending[key] = (handle, [lax.dynamic_index_in_dim(b, me, 0, keepdims=False) for b in blocks])
        return order[0, 0]

    def early_g(g):
        return send_grads("late", [g["w_att_out"].reshape(512, N_DEV, 128).transpose(1, 0, 2),
                                   g["w_ml_out"].reshape(N_DEV, 128, D_MODEL), g["w_out"].reshape(N_DEV, 128, D_MODEL),
                                   g["w_ff1"].reshape(D_MODEL, N_DEV, 512).transpose(1, 0, 2),
                                   g["w_ff2"].reshape(N_DEV, 512, D_MODEL)], "grad_late_start")

    def w_in_g(g):
        return send_grads("w_in", [g.reshape(D_MODEL, N_DEV, W_IN_SHARD).transpose(1, 0, 2)], "grad_w_in_start")

    def recv_grads(key, after, name):
        handle, own = pending[key]
        return [_own_slot(l, o) for l, o in zip(exchange_wait(handle, after, name, True), own)]

    (silu_c,) = small_call(lambda a: (_silu(a),), [c_all], [jax.ShapeDtypeStruct(c_all.shape, F32)], "silu_c")
    b_ada_cols = lax.dynamic_slice(b_ada, (0, me * 768), (1, 768))
    ada_cols = matmul(silu_c, w_ada[0], mode="nn", name="ada", extras=(jnp.broadcast_to(b_ada_cols, (N_DEV * B, 768)),),
                      epi=lambda acc, bb: (acc + bb,))
    (ada_t,) = all_to_all([ada_cols.reshape(N_DEV, B, 768)], "ada_exchange")
    ada = ada_t.transpose(1, 0, 2).reshape(B, 6 * D_MODEL)
    mods = tuple(ada[:, i * D_MODEL:(i + 1) * D_MODEL].reshape(B, 1, D_MODEL) for i in range(6))

    small = dict(norm1_g=norm1_g + late_order[0, 0], norm2_g=norm2_g, b_if=b_if[0], conv_w=conv_w_full, conv_b=conv_b,
                 q_norm_g=q_norm_g, k_norm_g=k_norm_g, rel_bias=rel_bias, mlstm_norm_g=mlstm_norm_g)
    loss, grad_x, d_mods, _, small_g = local_step(x, loss_target, mods, w, small, late_w, early_g, w_in_g)
    loss = lax.psum(loss[0, 0], ("x", "y", "c"))

    d_ada = jnp.concatenate([d.reshape(B, D_MODEL) for d in d_mods], axis=1)
    (d_ada_t,) = all_to_all([d_ada.reshape(B, N_DEV, 768).transpose(1, 0, 2)], "d_ada_exchange")
    d_ada_cols = d_ada_t.reshape(N_DEV * B, 768)
    g_w_ada = matmul(silu_c, d_ada_cols, mode="tn", name="g_w_ada")
    (g_b_cols,) = small_call(lambda a: (jnp.sum(a, axis=0, keepdims=True),), [d_ada_cols],
                             [jax.ShapeDtypeStruct((1, 768), F32)], "g_b_ada_cols")
    small_g["b_ada"] = lax.dynamic_update_slice(jnp.zeros((1, 6144), F32), g_b_cols, (0, me * 768))

    recv = recv_grads("w_in", grad_x, "grad_w_in_wait") + recv_grads("late", grad_x, "grad_late_wait")
    (small_parts,) = all_gather([_pack(small_g, _SMALL, _SMALL_ROWS)], "small_grad_gather")
    small_sum = sum_parts(small_parts, "small_grad_sum")
    sg = _unpack(small_sum, _SMALL, {**{k: shapes[k] for k, _ in _REPL}, "conv_w": (4, 1024)})

    G, Dl, NM, NV = {}, {}, {}, {}
    for k, parts in zip(big_names, recv):
        g, d, nm, nv = adamw(P[k][0], parts, M[k][0], V[k][0], f"adamw_{k}", parts=True)
        G[k], Dl[k], NM[k], NV[k] = g[None], d[None], nm[None], nv[None]
    g, d, nm, nv = adamw(w_ada[0], g_w_ada, m_w_ada[0], v_w_ada[0], "adamw_w_ada")
    G["w_ada"], Dl["w_ada"], NM["w_ada"], NV["w_ada"] = g[None], d[None], nm[None], nv[None]
    g_conv = lax.dynamic_slice(sg["conv_w"], (0, me * 128), (4, 128))
    g, d, nm, nv = adamw(conv_w[0], g_conv, m_conv_w[0], v_conv_w[0], "adamw_conv_w")
    G["conv_w"], Dl["conv_w"], NM["conv_w"], NV["conv_w"] = g[None], d[None], nm[None], nv[None]
    gslab = _pack(sg, _REPL, _REPL_ROWS)
    _, d, nm, nv = adamw(_pack(P, _REPL, _REPL_ROWS), gslab, _pack(M, _REPL, _REPL_ROWS), _pack(V, _REPL, _REPL_ROWS),
                         "adamw_small")
    rs = {k: shapes[k] for k, _ in _REPL}
    d, nm, nv = _unpack(d, _REPL, rs), _unpack(nm, _REPL, rs), _unpack(nv, _REPL, rs)
    for k, _ in _REPL:
        G[k], Dl[k], NM[k], NV[k] = sg[k], d[k], nm[k], nv[k]

    return (loss, grad_x, *[G[k] for k in names], *[Dl[k] for k in names], *[NM[k] for k in names],
            *[NV[k] for k in names])
```

```python
import functools
import math

import numpy as np
import jax
import jax.numpy as jnp
from jax import lax
from jax.experimental import pallas as pl
from jax.experimental.pallas import tpu as pltpu

F32 = jnp.float32
BF16 = jnp.bfloat16

N_DEV = 8
D_MODEL = 1024
SEQ = 2048
ATT_GROUPS = ((128, 1), (512, 4), (2048, 16))
N_ATT_HEADS = 12
ATT_BLOCK = 128
HEAD_DIM = 128
ML_HEADS = 8
ML_PAIRS = 4
ML_CHUNK = 64
N_CHUNKS = SEQ // ML_CHUNK
N_BUCKETS = 32
MAX_DISTANCE = 2048
D_FF = 4096
D_IN = 9744
EPS = 1e-6

ADAM_LR = 0.001
ADAM_B1 = 0.9
ADAM_B2 = 0.999
ADAM_EPS = 1e-08
ADAM_WD = 0.01
ADAM_STEP = 10

ATT_HEAD_COLS = 3 * HEAD_DIM
ATT_COLS = N_ATT_HEADS * ATT_HEAD_COLS
ML_PAIR_COLS = 896
ML_COLS = ML_PAIRS * ML_PAIR_COLS
GATE_COLS = 2 * D_MODEL
W_IN_SHARD = D_IN // N_DEV

VMEM_LIMIT = 56 * 1024 * 1024


def _cparams(**kw):
    return pltpu.CompilerParams(vmem_limit_bytes=VMEM_LIMIT, **kw)


_NN = ((1,), (0,))
_NT = ((1,), (1,))
_TN = ((0,), (0,))


def _mxu(a, b, dims):
    return lax.dot_general(a.astype(BF16), b.astype(BF16), (dims, ((), ())), preferred_element_type=F32)


@jax.custom_vjp
def bdot_nn(a, b):
    return _mxu(a, b, _NN)


def _nn_fwd(a, b):
    return _mxu(a, b, _NN), (a, b)


def _nn_bwd(res, g):
    a, b = res
    return _mxu(g, b, _NT), _mxu(a, g, _TN)


bdot_nn.defvjp(_nn_fwd, _nn_bwd)


@jax.custom_vjp
def bdot_nt(a, b):
    return _mxu(a, b, _NT)


def _nt_fwd(a, b):
    return _mxu(a, b, _NT), (a, b)


def _nt_bwd(res, g):
    a, b = res
    return _mxu(g, b, _NN), _mxu(g, a, _TN)


bdot_nt.defvjp(_nt_fwd, _nt_bwd)


@jax.custom_vjp
def bdot_tn(a, b):
    return _mxu(a, b, _TN)


def _tn_fwd(a, b):
    return _mxu(a, b, _TN), (a, b)


def _tn_bwd(res, g):
    a, b = res
    return _mxu(b, g, _NT), _mxu(a, g, _NN)


bdot_tn.defvjp(_tn_fwd, _tn_bwd)


def _doth(a, b, dims=_NN):
    return lax.dot_general(a, b, (dims, ((), ())), precision=lax.Precision.HIGHEST, preferred_element_type=F32)


def _rms(x):
    return x * lax.rsqrt(jnp.mean(x * x, axis=-1, keepdims=True) + EPS)


def _pick(n, cands):
    for t in cands:
        if n % t == 0:
            return t
    raise ValueError(f"no tile for {n}")


MM_TILE_M = (1024, 512, 256, 128, 64, 32, 16, 8)
MM_TILE_N = (2048, 1792, 1536, 1024, 768, 512, 256, 128)
MM_TILE_K = (1024, 512, 256, 128, 64, 32)

def matmul(a, b, *, mode, name, out_dtypes=(F32,), epi=None, extras=()):
    if mode == "nn":
        (M, K), (K2, N) = a.shape, b.shape
    elif mode == "nt":
        (M, K), (N, K2) = a.shape, b.shape
    else:
        (K, M), (K2, N) = a.shape, b.shape
    assert K == K2, (a.shape, b.shape, mode)
    tm = _pick(M, MM_TILE_M)
    tn = _pick(N, MM_TILE_N)
    tk = _pick(K, MM_TILE_K)
    nk = K // tk
    n_ex = len(extras)
    n_out = len(out_dtypes)
    dims = {"nn": _NN, "nt": _NT, "tn": _TN}[mode]

    def finish(r, ex_refs, out_refs):
        outs = epi(r, *[e[...] for e in ex_refs]) if epi is not None else (r,)
        for o_ref, o in zip(out_refs, outs):
            o_ref[...] = o.astype(o_ref.dtype)

    def body(*refs):
        a_ref, b_ref = refs[0], refs[1]
        ex_refs = refs[2:2 + n_ex]
        out_refs = refs[2 + n_ex:2 + n_ex + n_out]
        if nk == 1:
            finish(_mxu(a_ref[...], b_ref[...], dims), ex_refs, out_refs)
            return
        acc = refs[2 + n_ex + n_out]
        k = pl.program_id(2)

        @pl.when(k == 0)
        def _():
            acc[...] = jnp.zeros_like(acc)

        acc[...] += _mxu(a_ref[...], b_ref[...], dims)

        @pl.when(k == nk - 1)
        def _():
            finish(acc[...], ex_refs, out_refs)

    if mode == "nn":
        a_spec = pl.BlockSpec((tm, tk), lambda i, j, k: (i, k))
        b_spec = pl.BlockSpec((tk, tn), lambda i, j, k: (k, j))
    elif mode == "nt":
        a_spec = pl.BlockSpec((tm, tk), lambda i, j, k: (i, k))
        b_spec = pl.BlockSpec((tn, tk), lambda i, j, k: (j, k))
    else:
        a_spec = pl.BlockSpec((tk, tm), lambda i, j, k: (k, i))
        b_spec = pl.BlockSpec((tk, tn), lambda i, j, k: (k, j))
    o_spec = pl.BlockSpec((tm, tn), lambda i, j, k: (i, j))
    res = pl.pallas_call(
        body,
        name=name,
        grid=(M // tm, N // tn, nk),
        in_specs=[a_spec, b_spec] + [o_spec] * n_ex,
        out_specs=[o_spec] * n_out,
        out_shape=[jax.ShapeDtypeStruct((M, N), dt) for dt in out_dtypes],
        scratch_shapes=[pltpu.VMEM((tm, tn), F32)] if nk > 1 else [],
        compiler_params=_cparams(),
    )(a, b, *extras)
    return res[0] if n_out == 1 else tuple(res)


def small_call(fn, inputs, out_shapes, name):
    n_in = len(inputs)

    def body(*refs):
        outs = fn(*[r[...] for r in refs[:n_in]])
        for o_ref, o in zip(refs[n_in:], outs):
            o_ref[...] = o.astype(o_ref.dtype)

    res = pl.pallas_call(body, name=name, out_shape=list(out_shapes), compiler_params=_cparams())(*inputs)
    return tuple(res)


ROW_TILE = 512


def _modnorm(x, g, scale, shift):
    return _rms(x) * g * (1.0 + scale) + shift


def _row_spec(width):
    return pl.BlockSpec((1, ROW_TILE, width), lambda b, i: (b, i, 0))


def _mod_spec():
    return pl.BlockSpec((1, 1, D_MODEL), lambda b, i: (b, 0, 0))


def _vec_spec():
    return pl.BlockSpec((1, D_MODEL), lambda b, i: (0, 0))


def modnorm_fwd(x, g, scale, shift, name):
    B, S, D = x.shape

    def body(x_ref, g_ref, sc_ref, sh_ref, u_ref):
        u_ref[0] = _modnorm(x_ref[0], g_ref[...], sc_ref[0], sh_ref[0]).astype(BF16)

    return pl.pallas_call(
        body, name=name, grid=(B, S // ROW_TILE),
        in_specs=[_row_spec(D), _vec_spec(), _mod_spec(), _mod_spec()],
        out_specs=_row_spec(D),
        out_shape=jax.ShapeDtypeStruct((B, S, D), BF16),
        compiler_params=_cparams(),
    )(x, g, scale, shift)


def resid_modnorm_fwd(x, y, gate, g, scale, shift, name):
    B, S, D = x.shape

    def body(x_ref, y_ref, gt_ref, g_ref, sc_ref, sh_ref, x1_ref, u_ref):
        x1 = x_ref[0] + gt_ref[0] * y_ref[0]
        x1_ref[0] = x1
        u_ref[0] = _modnorm(x1, g_ref[...], sc_ref[0], sh_ref[0]).astype(BF16)

    return pl.pallas_call(
        body, name=name, grid=(B, S // ROW_TILE),
        in_specs=[_row_spec(D), _row_spec(D), _mod_spec(), _vec_spec(), _mod_spec(), _mod_spec()],
        out_specs=[_row_spec(D), _row_spec(D)],
        out_shape=[jax.ShapeDtypeStruct((B, S, D), F32), jax.ShapeDtypeStruct((B, S, D), BF16)],
        compiler_params=_cparams(),
    )(x, y, gate, g, scale, shift)


def resid_loss(x1, ffo, gate, target, name):
    B, S, D = x1.shape

    def body(x_ref, f_ref, gt_ref, t_ref, dx_ref, loss_ref):
        first = jnp.logical_and(pl.program_id(0) == 0, pl.program_id(1) == 0)

        @pl.when(first)
        def _():
            loss_ref[...] = jnp.zeros_like(loss_ref)

        err = x_ref[0] + gt_ref[0] * f_ref[0] - t_ref[0]
        dx_ref[0] = err * (1.0 / D)
        loss_ref[...] += 0.5 * jnp.sum(jnp.mean(err * err, axis=-1, keepdims=True), axis=0, keepdims=True)

    return pl.pallas_call(
        body, name=name, grid=(B, S // ROW_TILE),
        in_specs=[_row_spec(D), _row_spec(D), _mod_spec(), _row_spec(D)],
        out_specs=[_row_spec(D), pl.BlockSpec((1, 1), lambda b, i: (0, 0))],
        out_shape=[jax.ShapeDtypeStruct((B, S, D), F32), jax.ShapeDtypeStruct((1, 1), F32)],
        compiler_params=_cparams(),
    )(x1, ffo, gate, target)


def resid_bwd(dx, y, gate, name):
    B, S, D = dx.shape

    def body(dx_ref, y_ref, gt_ref, dy_ref, dg_ref):
        @pl.when(pl.program_id(1) == 0)
        def _():
            dg_ref[...] = jnp.zeros_like(dg_ref)

        d = dx_ref[0]
        dy_ref[0] = (gt_ref[0] * d).astype(BF16)
        dg_ref[0] += jnp.sum(d * y_ref[0], axis=0, keepdims=True)

    return pl.pallas_call(
        body, name=name, grid=(B, S // ROW_TILE),
        in_specs=[_row_spec(D), _row_spec(D), _mod_spec()],
        out_specs=[_row_spec(D), _mod_spec()],
        out_shape=[jax.ShapeDtypeStruct((B, S, D), BF16), jax.ShapeDtypeStruct((B, 1, D), F32)],
        compiler_params=_cparams(),
    )(dx, y, gate)


def modnorm_bwd(x, g, scale, shift, du, dx_res, name):
    B, S, D = x.shape

    def body(x_ref, g_ref, sc_ref, sh_ref, du_ref, dr_ref, dx_ref, dg_ref, dsc_ref, dsh_ref):
        first = jnp.logical_and(pl.program_id(0) == 0, pl.program_id(1) == 0)

        @pl.when(first)
        def _():
            dg_ref[...] = jnp.zeros_like(dg_ref)

        @pl.when(pl.program_id(1) == 0)
        def _():
            dsc_ref[...] = jnp.zeros_like(dsc_ref)
            dsh_ref[...] = jnp.zeros_like(dsh_ref)

        _, vjp = jax.vjp(_modnorm, x_ref[0], g_ref[...], sc_ref[0], sh_ref[0])
        dx, dg, dsc, dsh = vjp(du_ref[0].astype(F32))
        dx_ref[0] = dx + dr_ref[0]
        dg_ref[...] += dg
        dsc_ref[0] += dsc
        dsh_ref[0] += dsh

    return pl.pallas_call(
        body, name=name, grid=(B, S // ROW_TILE),
        in_specs=[_row_spec(D), _vec_spec(), _mod_spec(), _mod_spec(), _row_spec(D), _row_spec(D)],
        out_specs=[_row_spec(D), _vec_spec(), _mod_spec(), _mod_spec()],
        out_shape=[jax.ShapeDtypeStruct((B, S, D), F32), jax.ShapeDtypeStruct((1, D), F32),
                   jax.ShapeDtypeStruct((B, 1, D), F32), jax.ShapeDtypeStruct((B, 1, D), F32)],
        compiler_params=_cparams(),
    )(x, g, scale, shift, du, dx_res)


def _bucket_table(dilation):
    i = np.arange(ATT_BLOCK)[:, None]
    j = np.arange(2 * ATT_BLOCK)[None, :]
    delta = ATT_BLOCK + i - j
    dist = np.maximum(delta, 0) * dilation
    max_exact = N_BUCKETS // 2
    d = np.maximum(dist, max_exact).astype(np.float32)
    large = max_exact + (np.log(d / np.float32(max_exact)) / np.float32(math.log(MAX_DISTANCE / max_exact))
                         * np.float32(N_BUCKETS - max_exact)).astype(np.int32)
    large = np.minimum(large, N_BUCKETS - 1)
    return np.where(dist < max_exact, dist, large).astype(np.int32)


def _bucket_onehot(dilation):
    bt = jnp.asarray(_bucket_table(dilation).reshape(1, -1))
    return (bt == jnp.arange(N_BUCKETS, dtype=jnp.int32)[:, None]).astype(F32)


def bias_expand(rel_t, onehot, name):
    def fn(r, oh):
        return (_doth(r, oh),)
    return small_call(fn, [rel_t, onehot], [jax.ShapeDtypeStruct((rel_t.shape[0], onehot.shape[1]), F32)], name)[0]


def bias_reduce(dbias_flat, onehot, name):
    def fn(db, oh):
        return (_doth(db, oh, _NT),)
    return small_call(fn, [dbias_flat, onehot], [jax.ShapeDtypeStruct((dbias_flat.shape[0], N_BUCKETS), F32)], name)[0]


def _attn_tile(q, k, v, bias, qg, kg):
    qn = _rms(q) * qg
    kn = _rms(k) * kg
    s = bdot_nt(qn, kn) * (HEAD_DIM ** -0.5) + bias
    kl = k.shape[0]
    i = lax.broadcasted_iota(jnp.int32, (ATT_BLOCK, kl), 0)
    j = lax.broadcasted_iota(jnp.int32, (ATT_BLOCK, kl), 1) + (2 * ATT_BLOCK - kl)
    valid = jnp.logical_and(j >= i, j <= i + ATT_BLOCK)
    s = jnp.where(valid, s, -jnp.inf)
    m = jnp.max(s, axis=-1, keepdims=True)
    p = jnp.exp(s - m)
    l = jnp.sum(p, axis=-1, keepdims=True)
    o = bdot_nn(p, v) / l
    lse = jnp.broadcast_to(m + jnp.log(l), (ATT_BLOCK, HEAD_DIM))
    return o, lse


def _attn_tiles(dilation):
    nb = SEQ // dilation // ATT_BLOCK
    return [(r, n) for r in range(dilation) for n in range(nb)]


def _attn_rows(r, n, dilation, nblk=1):
    if dilation == 1:
        return pl.ds(r + n * ATT_BLOCK, nblk * ATT_BLOCK)
    return pl.ds(r + n * ATT_BLOCK * dilation, nblk * ATT_BLOCK, stride=dilation)


_QL, _KL, _VL = slice(0, 128), slice(128, 256), slice(256, 384)


def _qkv_specs(hb):
    return [pl.BlockSpec((None, SEQ, HEAD_DIM), functools.partial(lambda b, h, j: (b, 0, 3 * (hb + h) + j), j=j))
            for j in range(3)]


def attn_fwd(pa, bias, qg, kg, group, name):
    B = pa.shape[0]
    dilation = ATT_GROUPS[group][1]
    hb = group * 4

    def body(q_ref, k_ref, v_ref, b_ref, qg_ref, kg_ref, o_ref, l_ref):
        qg_, kg_ = qg_ref[...], kg_ref[...]
        for (r, n) in _attn_tiles(dilation):
            rows = _attn_rows(r, n, dilation)
            q = q_ref[rows, :]
            if n == 0:
                krows, bias_t = rows, b_ref[0, :, ATT_BLOCK:]
            else:
                krows, bias_t = _attn_rows(r, n - 1, dilation, 2), b_ref[0]
            o, lse = _attn_tile(q, k_ref[krows, :], v_ref[krows, :], bias_t, qg_, kg_)
            o_ref[rows, :] = o
            l_ref[rows, :] = lse

    head_out = pl.BlockSpec((None, SEQ, HEAD_DIM), lambda b, h: (b, 0, h))
    return pl.pallas_call(
        body, name=name, grid=(B, 4),
        in_specs=_qkv_specs(hb) + [
                  pl.BlockSpec((1, ATT_BLOCK, 2 * ATT_BLOCK), lambda b, h: (hb + h, 0, 0)),
                  pl.BlockSpec((1, HEAD_DIM), lambda b, h: (0, 0)),
                  pl.BlockSpec((1, HEAD_DIM), lambda b, h: (0, 0))],
        out_specs=[head_out, head_out],
        out_shape=[jax.ShapeDtypeStruct((B, SEQ, 512), F32), jax.ShapeDtypeStruct((B, SEQ, 512), F32)],
        compiler_params=_cparams(),
    )(pa, pa, pa, bias, qg, kg)


def attn_bwd(pa, bias, qg, kg, do, dlse, dpa, group, name):
    B = pa.shape[0]
    dilation = ATT_GROUPS[group][1]
    hb = group * 4

    def body(q_ref, k_ref, v_ref, b_ref, qg_ref, kg_ref, do_ref, dl_ref, dpa_in,
             dp_ref, db_ref, dqg_ref, dkg_ref, dq_s, dk_s, dv_s):
        del dpa_in
        h_id = pl.program_id(1)

        @pl.when(jnp.logical_and(pl.program_id(0) == 0, h_id == 0))
        def _():
            db_ref[...] = jnp.zeros_like(db_ref)
            dqg_ref[...] = jnp.zeros_like(dqg_ref)
            dkg_ref[...] = jnp.zeros_like(dkg_ref)

        dk_s[...] = jnp.zeros_like(dk_s)
        dv_s[...] = jnp.zeros_like(dv_s)
        qg_, kg_ = qg_ref[...], kg_ref[...]
        for (r, n) in _attn_tiles(dilation):
            rows = _attn_rows(r, n, dilation)
            q = q_ref[rows, :]
            if n == 0:
                krows, bias_t = rows, b_ref[0, :, ATT_BLOCK:]
            else:
                krows, bias_t = _attn_rows(r, n - 1, dilation, 2), b_ref[0]
            _, vjp = jax.vjp(_attn_tile, q, k_ref[krows, :], v_ref[krows, :], bias_t, qg_, kg_)
            dq, dk, dv, dbias, dqg, dkg = vjp((do_ref[rows, :], dl_ref[rows, :]))
            dq_s[rows, :] = dq
            dk_s[krows, :] += dk
            dv_s[krows, :] += dv
            if n == 0:
                db_ref[h_id, :, ATT_BLOCK:] += dbias
            else:
                db_ref[h_id] += dbias
            dqg_ref[...] += dqg
            dkg_ref[...] += dkg
        dp_ref[0, :, _QL] = dq_s[...].astype(BF16)
        dp_ref[0, :, _KL] = dk_s[...].astype(BF16)
        dp_ref[0, :, _VL] = dv_s[...].astype(BF16)

    const2 = lambda b, h: (0, 0)
    head_in = pl.BlockSpec((None, SEQ, HEAD_DIM), lambda b, h: (b, 0, h))
    head_blk = pl.BlockSpec((1, SEQ, ATT_HEAD_COLS), lambda b, h: (b, 0, hb + h))
    return pl.pallas_call(
        body, name=name, grid=(B, 4),
        in_specs=_qkv_specs(hb) + [
                  pl.BlockSpec((1, ATT_BLOCK, 2 * ATT_BLOCK), lambda b, h: (hb + h, 0, 0)),
                  pl.BlockSpec((1, HEAD_DIM), const2), pl.BlockSpec((1, HEAD_DIM), const2),
                  head_in, head_in,
                  pl.BlockSpec(memory_space=pl.ANY)],
        out_specs=[head_blk,
                   pl.BlockSpec((4, ATT_BLOCK, 2 * ATT_BLOCK), lambda b, h: (0, 0, 0)),
                   pl.BlockSpec((1, HEAD_DIM), const2), pl.BlockSpec((1, HEAD_DIM), const2)],
        out_shape=[jax.ShapeDtypeStruct(dpa.shape, BF16),
                   jax.ShapeDtypeStruct((4, ATT_BLOCK, 2 * ATT_BLOCK), F32),
                   jax.ShapeDtypeStruct((1, HEAD_DIM), F32), jax.ShapeDtypeStruct((1, HEAD_DIM), F32)],
        scratch_shapes=[pltpu.VMEM((SEQ, HEAD_DIM), F32)] * 3,
        input_output_aliases={8: 0},
        compiler_params=_cparams(),
    )(pa, pa, pa, bias, qg, kg, do, dlse, dpa)


def _merge(o0, o1, o2, l0, l1, l2):
    mx = jnp.maximum(jnp.maximum(l0, l1), l2)
    e0, e1, e2 = jnp.exp(l0 - mx), jnp.exp(l1 - mx), jnp.exp(l2 - mx)
    den = e0 + e1 + e2
    return (e0 / den) * o0 + (e1 / den) * o1 + (e2 / den) * o2


def merge_fwd(os_, ls_, name):
    B = os_[0].shape[0]

    def body(o0, o1, o2, l0, l1, l2, a_ref):
        a_ref[0] = _merge(o0[0], o1[0], o2[0], l0[0], l1[0], l2[0]).astype(BF16)

    return pl.pallas_call(
        body, name=name, grid=(B, SEQ // ROW_TILE),
        in_specs=[_row_spec(512)] * 6, out_specs=_row_spec(512),
        out_shape=jax.ShapeDtypeStruct((B, SEQ, 512), BF16),
        compiler_params=_cparams(),
    )(*os_, *ls_)


def merge_bwd(os_, ls_, datt, name):
    B = os_[0].shape[0]

    def body(o0, o1, o2, l0, l1, l2, da_ref, *outs):
        _, vjp = jax.vjp(_merge, o0[0], o1[0], o2[0], l0[0], l1[0], l2[0])
        for o_ref, g in zip(outs, vjp(da_ref[0])):
            o_ref[0] = g

    return pl.pallas_call(
        body, name=name, grid=(B, SEQ // ROW_TILE),
        in_specs=[_row_spec(512)] * 7, out_specs=[_row_spec(512)] * 6,
        out_shape=[jax.ShapeDtypeStruct((B, SEQ, 512), F32)] * 6,
        compiler_params=_cparams(),
    )(*os_, *ls_, datt)


def _gate_mix(ga, gm, ya, ym):
    return jax.nn.sigmoid(ga) * ya + jax.nn.sigmoid(gm) * ym


def gate_fwd(pg, ya, ym, name):
    B = pg.shape[0]

    def body(ga, gm, ya_ref, ym_ref, z_ref):
        z_ref[0] = _gate_mix(ga[0], gm[0], ya_ref[0], ym_ref[0]).astype(BF16)

    return pl.pallas_call(
        body, name=name, grid=(B, SEQ // ROW_TILE),
        in_specs=[pl.BlockSpec((1, ROW_TILE, D_MODEL), lambda b, i: (b, i, 0)),
                  pl.BlockSpec((1, ROW_TILE, D_MODEL), lambda b, i: (b, i, 1)),
                  _row_spec(D_MODEL), _row_spec(D_MODEL)],
        out_specs=_row_spec(D_MODEL),
        out_shape=jax.ShapeDtypeStruct((B, SEQ, D_MODEL), BF16),
        compiler_params=_cparams(),
    )(pg, pg, ya, ym)


def gate_bwd(pg, ya, ym, dz, name):
    B = pg.shape[0]

    def body(ga, gm, ya_ref, ym_ref, dz_ref, dpg_ref, dya_ref, dym_ref):
        _, vjp = jax.vjp(_gate_mix, ga[0], gm[0], ya_ref[0], ym_ref[0])
        dga, dgm, dya, dym = vjp(dz_ref[0])
        dpg_ref[0, :, :D_MODEL] = dga.astype(BF16)
        dpg_ref[0, :, D_MODEL:] = dgm.astype(BF16)
        dya_ref[0] = dya.astype(BF16)
        dym_ref[0] = dym.astype(BF16)

    return pl.pallas_call(
        body, name=name, grid=(B, SEQ // ROW_TILE),
        in_specs=[pl.BlockSpec((1, ROW_TILE, D_MODEL), lambda b, i: (b, i, 0)),
                  pl.BlockSpec((1, ROW_TILE, D_MODEL), lambda b, i: (b, i, 1)),
                  _row_spec(D_MODEL), _row_spec(D_MODEL), _row_spec(D_MODEL)],
        out_specs=[_row_spec(GATE_COLS), _row_spec(D_MODEL), _row_spec(D_MODEL)],
        out_shape=[jax.ShapeDtypeStruct((B, SEQ, GATE_COLS), BF16),
                   jax.ShapeDtypeStruct((B, SEQ, D_MODEL), BF16), jax.ShapeDtypeStruct((B, SEQ, D_MODEL), BF16)],
        compiler_params=_cparams(),
    )(pg, pg, ya, ym, dz)


def _log_sigmoid(x):
    return jnp.minimum(x, 0.0) - jnp.log(1.0 + jnp.exp(-jnp.abs(x)))


def _head_mask(e):
    lane = lax.broadcasted_iota(jnp.int32, (1, 128), 1)
    return jnp.logical_and(lane >= e * 64, lane < (e + 1) * 64).astype(F32)


def _bmxu(a, b, ca, cb):
    return lax.dot_general(a.astype(BF16), b.astype(BF16), (((ca,), (cb,)), ((0,), (0,))), preferred_element_type=F32)


@jax.custom_vjp
def cdot_nt(a, b):
    return _bmxu(a, b, 2, 2)


cdot_nt.defvjp(lambda a, b: (_bmxu(a, b, 2, 2), (a, b)),
               lambda res, g: (_bmxu(g, res[1], 2, 1), _bmxu(g, res[0], 1, 1)))


@jax.custom_vjp
def cdot_nn(a, b):
    return _bmxu(a, b, 2, 1)


cdot_nn.defvjp(lambda a, b: (_bmxu(a, b, 2, 1), (a, b)),
               lambda res, g: (_bmxu(g, res[1], 2, 2), _bmxu(res[0], g, 1, 1)))


@jax.custom_vjp
def cdot_tn(a, b):
    return _bmxu(a, b, 1, 1)


cdot_tn.defvjp(lambda a, b: (_bmxu(a, b, 1, 1), (a, b)),
               lambda res, g: (_bmxu(res[1], g, 2, 2), _bmxu(res[0], g, 2, 1)))


def _cdoth(a, b):
    return lax.dot_general(a, b, (((2,), (1,)), ((0,), (0,))), precision=lax.Precision.HIGHEST,
                           preferred_element_type=F32)


def _ml_intra(q2, k2, v, ifb, *, e):
    c, L = N_CHUNKS, ML_CHUNK
    hm = _head_mask(e)
    q3 = (q2 * hm).reshape(c, L, 128)
    k3 = (k2 * hm).reshape(c, L, 128)
    v3 = v.reshape(c, L, 128)
    if3 = ifb.reshape(c, L, 128)
    lanes = lax.broadcasted_iota(jnp.int32, (c, L, 128), 2)
    li = jnp.sum(jnp.where(lanes == e, if3, 0.0), axis=-1, keepdims=True)
    ri = lax.broadcasted_iota(jnp.int32, (L, L), 0)
    ci = lax.broadcasted_iota(jnp.int32, (L, L), 1)
    causal = ri >= ci
    tri = jnp.broadcast_to(causal.astype(F32), (c, L, L))
    cs = _cdoth(tri, _log_sigmoid(if3))
    b = jnp.sum(jnp.where(lanes == 2 + e, cs, 0.0), axis=-1, keepdims=True)
    last = lax.broadcasted_iota(jnp.int32, (1, L, 1), 1) == L - 1
    b_end = jnp.sum(jnp.where(last, b, 0.0), axis=1, keepdims=True)
    rrow = _cdoth(jnp.ones((c, L, L), F32), (ri == ci).astype(F32) * (li - b))
    Dm = jnp.where(causal, b + rrow, -jnp.inf)
    mD = lax.stop_gradient(jnp.max(Dm, axis=-1, keepdims=True))
    P0 = cdot_nt(q3, k3) * jnp.exp(Dm - mD)
    H0 = cdot_nn(P0, v3)
    r0 = jnp.sum(P0, axis=-1, keepdims=True)
    g = b_end - b + li
    mg = lax.stop_gradient(jnp.max(g, axis=1, keepdims=True))
    kw = jnp.exp(g - mg) * k3
    return H0, r0, cdot_tn(kw, v3), jnp.sum(kw, axis=1, keepdims=True), b, b_end, mD, mg


def _ml_inter(q2, mo, gn, H0, r0, b, C_in, n_in, *, mD, m_in, e):
    c, L = N_CHUNKS, ML_CHUNK
    q3 = (q2 * _head_mask(e)).reshape(c, L, 128)
    a = b + m_in
    m_t = lax.stop_gradient(jnp.maximum(a, mD))
    c1 = jnp.exp(mD - m_t)
    c2 = jnp.exp(a - m_t)
    num = c1 * H0 + c2 * cdot_nn(q3, C_in)
    nq = c1 * r0 + c2 * jnp.sum(q3 * n_in, axis=-1, keepdims=True)
    h = num / jnp.maximum(jnp.abs(nq), jnp.exp(-m_t))
    hg = _rms(h) * gn * jax.nn.sigmoid(mo.reshape(c, L, 128))
    return hg.reshape(c * L, 128)


def _state_sweep(U_s, un_s, be_s, mg_s, Cin_s, nin_s, min_s, al_s, bt_s):
    def step(j, carry):
        C, n, m = carry
        Cin_s[j], nin_s[j], min_s[j] = C, n, m
        m_out = jnp.maximum(be_s[j] + m, mg_s[j])
        al = jnp.exp(be_s[j] + m - m_out)
        bt = jnp.exp(mg_s[j] - m_out)
        al_s[j], bt_s[j] = al, bt
        return al * C + bt * U_s[j], al * n + bt * un_s[j], m_out

    lax.fori_loop(0, N_CHUNKS, step, (jnp.zeros((128, 128), F32), jnp.zeros((1, 128), F32), jnp.zeros((1, 1), F32)))


def _state_sweep_bwd(U_s, un_s, dbe_s, Cin_s, nin_s, dCp_s, dnp_s, al_s, bt_s):
    def step(t, carry):
        j = N_CHUNKS - 1 - t
        dC, dn = carry
        al, bt = al_s[j], bt_s[j]
        U_s[j] = bt * dC
        un_s[j] = bt * dn
        dal = jnp.sum(jnp.sum(dC * Cin_s[j], axis=1, keepdims=True), axis=0, keepdims=True) \
            + jnp.sum(dn * nin_s[j], axis=1, keepdims=True)
        dbe_s[j] = dal * al
        return dCp_s[j] + al * dC, dnp_s[j] + al * dn

    lax.fori_loop(0, N_CHUNKS, step, (jnp.zeros((128, 128), F32), jnp.zeros((1, 128), F32)))


def _state_scratch():
    c = N_CHUNKS
    return [pltpu.VMEM((c, 128, 128), F32), pltpu.VMEM((c, 1, 128), F32), pltpu.VMEM((c, 1, 1), F32),
            pltpu.VMEM((c, 1, 1), F32),
            pltpu.VMEM((c, 128, 128), F32), pltpu.VMEM((c, 1, 128), F32), pltpu.VMEM((c, 1, 1), F32),
            pltpu.VMEM((c, 1, 1), F32), pltpu.VMEM((c, 1, 1), F32)]


def _shift_down(x, s):
    if s == 0:
        return x
    rows = lax.broadcasted_iota(jnp.int32, x.shape, 0)
    return jnp.where(rows >= s, pltpu.roll(x, s, 0), 0.0)


def _shift_up(x, s):
    if s == 0:
        return x
    S = x.shape[0]
    rows = lax.broadcasted_iota(jnp.int32, x.shape, 0)
    return jnp.where(rows < S - s, pltpu.roll(x, S - s, 0), 0.0)


def _conv_pre(x, cw, cb):
    y = cb + cw[3:4, :] * x
    for j in range(3):
        y = y + cw[j:j + 1, :] * _shift_down(x, 3 - j)
    return y


def _conv_bwd(x, cw, dpre):
    dx = cw[3:4, :] * dpre
    dcw = [None] * 4
    dcw[3] = jnp.sum(dpre * x, axis=0, keepdims=True)
    for j in range(3):
        dx = dx + cw[j:j + 1, :] * _shift_up(dpre, 3 - j)
        dcw[j] = jnp.sum(dpre * _shift_down(x, 3 - j), axis=0, keepdims=True)
    return dx, dcw, jnp.sum(dpre, axis=0, keepdims=True)


def _silu(z):
    return z * jax.nn.sigmoid(z)


def _dsilu(z):
    s = jax.nn.sigmoid(z)
    return s * (1.0 + z * (1.0 - s))


_ML_Q, _ML_K = slice(0, 128), slice(128, 256)
_ML_IF = slice(768, 896)


def _ml_v(e):
    return slice(256 + e * 128, 384 + e * 128)


def _ml_o(e):
    return slice(512 + e * 128, 640 + e * 128)


def _ml_specs():
    pair = lambda b, p: (b, 0, p)
    return [pl.BlockSpec((1, SEQ, ML_PAIR_COLS), pair),
            pl.BlockSpec((1, 4, 128), lambda b, p: (p, 0, 0)),
            pl.BlockSpec((1, 4, 128), lambda b, p: (4 + p, 0, 0)),
            pl.BlockSpec((1, 1, 128), lambda b, p: (p, 0, 0)),
            pl.BlockSpec((1, 1, 128), lambda b, p: (4 + p, 0, 0)),
            pl.BlockSpec((1, 1, 128), lambda b, p: (p, 0, 0)),
            pl.BlockSpec((1, 1, 256), lambda b, p: (p, 0, 0))]


def mlstm_fwd(pm, cw8, cb8, bifp, gn4, name):
    B = pm.shape[0]

    def body(p_ref, cwq, cwk, cbq, cbk, bif_ref, gn_ref, hg_ref, *st):
        U_s, un_s, be_s, mg_s, Cin_s, nin_s, min_s, al_s, bt_s = st
        qc = _silu(_conv_pre(p_ref[0, :, _ML_Q], cwq[0], cbq[0]))
        kc = _silu(_conv_pre(p_ref[0, :, _ML_K], cwk[0], cbk[0])) * (64 ** -0.5)
        ifb = p_ref[0, :, _ML_IF] + bif_ref[0]
        for e in range(2):
            lanes = slice(e * 128, (e + 1) * 128)
            H0, r0, U, un, b, b_end, mD, mg = _ml_intra(qc, kc, p_ref[0, :, _ml_v(e)], ifb, e=e)
            U_s[...], un_s[...], be_s[...], mg_s[...] = U, un, b_end, mg
            _state_sweep(*st)
            hg = _ml_inter(qc, p_ref[0, :, _ml_o(e)], gn_ref[0, :, lanes], H0, r0, b, Cin_s[...], nin_s[...],
                           mD=mD, m_in=min_s[...], e=e)
            hg_ref[0, :, lanes] = hg.astype(BF16)

    return pl.pallas_call(
        body, name=name, grid=(B, ML_PAIRS),
        in_specs=_ml_specs(),
        out_specs=pl.BlockSpec((1, SEQ, 256), lambda b, p: (b, 0, p)),
        out_shape=jax.ShapeDtypeStruct((B, SEQ, D_MODEL), BF16),
        scratch_shapes=_state_scratch(),
        compiler_params=_cparams(),
    )(pm, cw8, cw8, cb8, cb8, bifp, gn4)


def mlstm_bwd(pm, cw8, cb8, bifp, gn4, dhg, name):
    B = pm.shape[0]

    def body(p_ref, cwq, cwk, cbq, cbk, bif_ref, gn_ref, dh_ref,
             dp_ref, dcw_ref, dcb_ref, dbif_ref, dgn_ref, *scr):
        st = scr[:9]
        U_s, un_s, be_s, mg_s, Cin_s, nin_s, min_s, al_s, bt_s = st
        dCp_s, dnp_s, dbe_s = scr[9:]
        p_id = pl.program_id(1)

        @pl.when(jnp.logical_and(pl.program_id(0) == 0, p_id == 0))
        def _():
            dcw_ref[...] = jnp.zeros_like(dcw_ref)
            dcb_ref[...] = jnp.zeros_like(dcb_ref)
            dbif_ref[...] = jnp.zeros_like(dbif_ref)
            dgn_ref[...] = jnp.zeros_like(dgn_ref)

        qc = _silu(_conv_pre(p_ref[0, :, _ML_Q], cwq[0], cbq[0]))
        kc = _silu(_conv_pre(p_ref[0, :, _ML_K], cwk[0], cbk[0])) * (64 ** -0.5)
        ifb = p_ref[0, :, _ML_IF] + bif_ref[0]
        dq = jnp.zeros((SEQ, 128), F32)
        dk = jnp.zeros((SEQ, 128), F32)
        difb = jnp.zeros((SEQ, 128), F32)
        for e in range(2):
            lanes = slice(e * 128, (e + 1) * 128)
            (H0, r0, U, un, b, b_end, mD, mg), vjp1 = jax.vjp(functools.partial(_ml_intra, e=e), qc, kc,
                                                              p_ref[0, :, _ml_v(e)], ifb)
            U_s[...], un_s[...], be_s[...], mg_s[...] = U, un, b_end, mg
            _state_sweep(*st)
            _, vjp3 = jax.vjp(functools.partial(_ml_inter, mD=mD, m_in=min_s[...], e=e), qc, p_ref[0, :, _ml_o(e)],
                              gn_ref[0, :, lanes], H0, r0, b, Cin_s[...], nin_s[...])
            dq_a, dmo, dgn, dH0, dr0, db_a, dCp, dnp = vjp3(dh_ref[0, :, lanes])
            dCp_s[...], dnp_s[...] = dCp, dnp
            _state_sweep_bwd(U_s, un_s, dbe_s, Cin_s, nin_s, dCp_s, dnp_s, al_s, bt_s)
            dq_b, dk_b, dv, difb_e = vjp1((dH0, dr0, U_s[...], un_s[...], db_a, dbe_s[...],
                                           jnp.zeros_like(mD), jnp.zeros_like(mg)))
            dq, dk, difb = dq + dq_a + dq_b, dk + dk_b, difb + difb_e
            dp_ref[0, :, _ml_v(e)] = dv.astype(BF16)
            dp_ref[0, :, _ml_o(e)] = dmo.astype(BF16)
            dgn_ref[p_id, :, lanes] += dgn
        dp_ref[0, :, _ML_IF] = difb.astype(BF16)
        dbif_ref[p_id] += jnp.sum(difb, axis=0, keepdims=True)

        for (sl, cw, cb, d, blk, scale) in ((_ML_Q, cwq, cbq, dq, p_id, 1.0), (_ML_K, cwk, cbk, dk, 4 + p_id, 64 ** -0.5)):
            xr = p_ref[0, :, sl]
            dpre = d * scale * _dsilu(_conv_pre(xr, cw[0], cb[0]))
            dx, dcw, dcb = _conv_bwd(xr, cw[0], dpre)
            dp_ref[0, :, sl] = dx.astype(BF16)
            for j in range(4):
                dcw_ref[blk, j:j + 1, :] += dcw[j]
            dcb_ref[blk] += dcb

    full3 = lambda b, p: (0, 0, 0)
    return pl.pallas_call(
        body, name=name, grid=(B, ML_PAIRS),
        in_specs=[pl.BlockSpec((1, SEQ, ML_PAIR_COLS), lambda b, p: (b, 0, p), pipeline_mode=pl.Buffered(1))]
        + _ml_specs()[1:] + [pl.BlockSpec((1, SEQ, 256), lambda b, p: (b, 0, p), pipeline_mode=pl.Buffered(1))],
        out_specs=[pl.BlockSpec((1, SEQ, ML_PAIR_COLS), lambda b, p: (b, 0, p)),
                   pl.BlockSpec((8, 4, 128), full3), pl.BlockSpec((8, 1, 128), full3),
                   pl.BlockSpec((4, 1, 128), full3), pl.BlockSpec((4, 1, 256), full3)],
        out_shape=[jax.ShapeDtypeStruct((B, SEQ, ML_COLS), BF16),
                   jax.ShapeDtypeStruct((8, 4, 128), F32), jax.ShapeDtypeStruct((8, 1, 128), F32),
                   jax.ShapeDtypeStruct((4, 1, 128), F32), jax.ShapeDtypeStruct((4, 1, 256), F32)],
        scratch_shapes=_state_scratch() + [pltpu.VMEM((N_CHUNKS, 128, 128), F32), pltpu.VMEM((N_CHUNKS, 1, 128), F32),
                                           pltpu.VMEM((N_CHUNKS, 1, 1), F32)],
        compiler_params=_cparams(),
    )(pm, cw8, cw8, cb8, cb8, bifp, gn4, dhg)


def _adamw(w, g, m, v):
    m = ADAM_B1 * m + (1.0 - ADAM_B1) * g
    v = ADAM_B2 * v + (1.0 - ADAM_B2) * (g * g)
    m_hat = m / (1.0 - ADAM_B1 ** ADAM_STEP)
    v_hat = v / (1.0 - ADAM_B2 ** ADAM_STEP)
    delta = -ADAM_LR * (m_hat / (jnp.sqrt(v_hat) + ADAM_EPS) + ADAM_WD * w)
    return delta, m, v


def adamw(w, g, m, v, name, parts=False):
    R, C = w.shape
    tr = _pick(R, (256, 128, 64, 32, 16, 8, 4, 2, 1)) if R * C * 4 > (1 << 20) else R
    spec = pl.BlockSpec((tr, C), lambda i: (i, 0))
    g_spec = pl.BlockSpec((N_DEV, tr, C), lambda i: (0, i, 0)) if parts else spec

    def body(w_ref, g_ref, m_ref, v_ref, go_ref, d_ref, mo_ref, vo_ref):
        if parts:
            g = g_ref[0].astype(F32)
            for k in range(1, N_DEV):
                g = g + g_ref[k].astype(F32)
        else:
            g = g_ref[...]
        d, mn, vn = _adamw(w_ref[...], g, m_ref[...], v_ref[...])
        go_ref[...], d_ref[...], mo_ref[...], vo_ref[...] = g, d, mn, vn

    return pl.pallas_call(
        body, name=name, grid=(R // tr,),
        in_specs=[spec, g_spec, spec, spec], out_specs=[spec] * 4,
        out_shape=[jax.ShapeDtypeStruct((R, C), F32)] * 4,
        compiler_params=_cparams(),
    )(w, g, m, v)


def _mesh_pos():
    return lax.axis_index("x"), lax.axis_index("y"), lax.axis_index("c")


def _flip(pos, f):
    x, y, c = pos
    return (1 - x if f & 4 else x, 1 - y if f & 2 else y, 1 - c if f & 1 else c)


def _index(pos):
    return 4 * pos[0] + 2 * pos[1] + pos[2]


def _exchange(arrs, name, scatter):
    n = len(arrs)

    def body(*refs):
        ins, outs = refs[:n], refs[n:2 * n]
        send, recv, lsem = refs[2 * n:]
        me = _mesh_pos()
        mine = _index(me)
        copies = []
        for i in range(n):
            src = ins[i].at[mine] if scatter else ins[i]
            loc = pltpu.make_async_copy(src, outs[i].at[mine], lsem.at[i])
            loc.start()
            copies.append(loc)
            for f in range(1, N_DEV):
                peer = _flip(me, f)
                src = ins[i].at[_index(peer)] if scatter else ins[i]
                cp = pltpu.make_async_remote_copy(
                    src_ref=src, dst_ref=outs[i].at[mine],
                    send_sem=send.at[i * 7 + f - 1], recv_sem=recv.at[i * 7 + f - 1],
                    device_id=peer, device_id_type=pl.DeviceIdType.MESH)
                cp.start()
                copies.append(cp)
        for cp in copies:
            cp.wait()

    any_spec = pl.BlockSpec(memory_space=pl.ANY)
    out_shape = [jax.ShapeDtypeStruct(a.shape if scatter else (N_DEV,) + a.shape, a.dtype) for a in arrs]
    res = pl.pallas_call(
        body, name=name,
        in_specs=[any_spec] * n, out_specs=[any_spec] * n, out_shape=out_shape,
        scratch_shapes=[pltpu.SemaphoreType.DMA((7 * n,)), pltpu.SemaphoreType.DMA((7 * n,)),
                        pltpu.SemaphoreType.DMA((n,))],
        compiler_params=_cparams(),
    )(*arrs)
    return list(res)


def all_gather(arrs, name):
    return _exchange(arrs, name, False)


def all_gather_two_level(arrs, name):
    n = len(arrs)

    def body(*refs):
        ins, outs = refs[:n], refs[n:2 * n]
        send, recv, lsem = refs[2 * n:]
        x, y, c = _mesh_pos()
        me, sibling = (x, y, c), (x, y, 1 - c)
        chips = [(1 - x, y), (x, 1 - y), (1 - x, 1 - y)]

        def copy(i, k, block, to, src=None):
            rows = outs[i].at[_index(block)]
            return pltpu.make_async_remote_copy(
                src_ref=rows if src is None else src, dst_ref=rows,
                send_sem=send.at[i * 7 + k], recv_sem=recv.at[i * 7 + k],
                device_id=to, device_id_type=pl.DeviceIdType.MESH)

        local = [pltpu.make_async_copy(ins[i], outs[i].at[_index(me)], lsem.at[i]) for i in range(n)]
        first = [copy(i, 0, me, sibling, src=ins[i]) for i in range(n)]
        first += [copy(i, 1 + j, me, (*chip, c), src=ins[i]) for i in range(n) for j, chip in enumerate(chips)]
        for cp in local + first:
            cp.start()
        passed = []
        for j, chip in enumerate(chips):
            for i in range(n):
                copy(i, 1 + j, (*chip, c), me).wait_recv()
                cp = copy(i, 4 + j, (*chip, c), sibling)
                cp.start()
                passed.append(cp)
        for i in range(n):
            copy(i, 0, sibling, me).wait_recv()
            for j, chip in enumerate(chips):
                copy(i, 4 + j, (*chip, 1 - c), me).wait_recv()
        for cp in first + passed:
            cp.wait_send()
        for cp in local:
            cp.wait()

    any_spec = pl.BlockSpec(memory_space=pl.ANY)
    res = pl.pallas_call(
        body, name=name,
        in_specs=[any_spec] * n, out_specs=[any_spec] * n,
        out_shape=[jax.ShapeDtypeStruct((N_DEV,) + a.shape, a.dtype) for a in arrs],
        scratch_shapes=[pltpu.SemaphoreType.DMA((7 * n,)), pltpu.SemaphoreType.DMA((7 * n,)),
                        pltpu.SemaphoreType.DMA((n,))],
        compiler_params=_cparams(),
    )(*arrs)
    return list(res)


def all_to_all(arrs, name):
    return _exchange(arrs, name, True)


_HBM = pl.BlockSpec(memory_space=pltpu.HBM)
_SEM = pl.BlockSpec(memory_space=pltpu.SEMAPHORE)
_EFFECT = pltpu.SideEffectType.DATAFLOW_SIDE_EFFECTING


def _split_copies(ins, lands, send, recv, scatter, waiting):
    me = _mesh_pos()
    mine = _index(me)
    copies = []
    for i in range(len(ins)):
        for f in range(1, N_DEV):
            peer = _flip(me, f)
            src = ins[i].at[_index(peer)] if scatter else ins[i]
            copies.append(pltpu.make_async_remote_copy(
                src_ref=src, dst_ref=lands[i].at[_index(peer) if waiting else mine],
                send_sem=send.at[i * 7 + f - 1], recv_sem=recv.at[i * 7 + f - 1],
                device_id=peer, device_id_type=pl.DeviceIdType.MESH))
    return copies


def exchange_start(arrs, name, scatter):
    n = len(arrs)
    land_shapes = [a.shape if scatter else (N_DEV,) + a.shape for a in arrs]

    def body(*refs):
        ins, lands = refs[:n], refs[n:2 * n]
        send, recv = refs[2 * n], refs[2 * n + 1]
        token = refs[-1]
        for cp in _split_copies(ins, lands, send, recv, scatter, False):
            cp.start()
        token[...] = jnp.zeros_like(token)

    res = pl.pallas_call(
        body, name=name,
        out_shape=(pltpu.SemaphoreType.DMA((7 * n,)), pltpu.SemaphoreType.DMA((7 * n,)),
                   *[pltpu.HBM(a.shape, a.dtype) for a in arrs],
                   *[pltpu.HBM(s, a.dtype) for s, a in zip(land_shapes, arrs)],
                   jax.ShapeDtypeStruct((8, 128), F32)),
        in_specs=[_HBM] * (2 * n),
        out_specs=(_SEM, _SEM, *[_HBM] * (2 * n), pl.BlockSpec(memory_space=pltpu.VMEM)),
        input_output_aliases={i: 2 + i for i in range(2 * n)},
        compiler_params=pltpu.CompilerParams(has_side_effects=_EFFECT),
    )(*[pltpu.with_memory_space_constraint(a, pltpu.HBM) for a in arrs],
      *[pltpu.with_memory_space_constraint(lax.empty(s, a.dtype), pltpu.HBM) for s, a in zip(land_shapes, arrs)])
    return (res[0], res[1], list(res[2:2 + n]), list(res[2 + n:2 + 2 * n])), res[-1]


def exchange_wait(handle, after, name, scatter):
    send, recv, srcs, lands = handle
    n = len(srcs)

    def body(*refs):
        ins, lnd = refs[:n], refs[n:2 * n]
        send_, recv_ = refs[2 * n], refs[2 * n + 1]
        for cp in _split_copies(ins, lnd, send_, recv_, scatter, True):
            cp.wait_send()
            cp.wait_recv()

    res = pl.pallas_call(
        body, name=name,
        out_shape=(*[pltpu.HBM(a.shape, a.dtype) for a in srcs], *[pltpu.HBM(a.shape, a.dtype) for a in lands]),
        in_specs=[_HBM] * (2 * n) + [_SEM, _SEM, pl.BlockSpec(memory_space=pl.ANY)],
        out_specs=tuple([_HBM] * (2 * n)),
        input_output_aliases={i: i for i in range(2 * n)},
        compiler_params=pltpu.CompilerParams(has_side_effects=_EFFECT),
    )(*srcs, *lands, send, recv, after)
    return list(res[n:])


def _own_slot(land, own):
    return lax.dynamic_update_slice(land, own[None], (_index(_mesh_pos()),) + (0,) * own.ndim)


def cast_bf16(arrs, name):
    outs = []
    for i, a in enumerate(arrs):
        R, C = a.shape
        tr = _pick(R, (256, 128, 64, 32, 16, 8)) if R * C * 4 > (1 << 21) else R
        spec = pl.BlockSpec((tr, C), lambda i: (i, 0))

        def body(a_ref, o_ref):
            o_ref[...] = a_ref[...].astype(BF16)

        outs.append(pl.pallas_call(body, name=f"{name}_{i}", grid=(R // tr,), in_specs=[spec], out_specs=spec,
                                   out_shape=jax.ShapeDtypeStruct((R, C), BF16), compiler_params=_cparams())(a))
    return outs


def sum_parts(parts, name):
    def fn(p):
        g = p[0]
        for k in range(1, N_DEV):
            g = g + p[k]
        return (g,)
    return small_call(fn, [parts], [jax.ShapeDtypeStruct(parts.shape[1:], F32)], name)[0]


_SPLITS = np.cumsum([1536, 1536, 1536, 512, 512, 1024, 1024, 8, 8, 2048])[:-1].tolist()


def split_w_in(w):
    aq, ak, av, mq, mk, mv, mo, mi, mf, gates = jnp.split(w, _SPLITS, axis=1)
    R = w.shape[0]
    w_att = jnp.stack([aq.reshape(R, 12, 128), ak.reshape(R, 12, 128), av.reshape(R, 12, 128)], axis=2)
    gif = jnp.concatenate([mi.reshape(R, 4, 2), mf.reshape(R, 4, 2), jnp.zeros((R, 4, 124), w.dtype)], axis=2)
    w_ml = jnp.concatenate([mq.reshape(R, 4, 128), mk.reshape(R, 4, 128), mv.reshape(R, 4, 256),
                            mo.reshape(R, 4, 256), gif], axis=2)
    return w_att.reshape(R, ATT_COLS), w_ml.reshape(R, ML_COLS), gates


def merge_w_in(g_att, g_ml, g_gate):
    R = g_att.shape[0]
    a = g_att.reshape(R, 12, 3, 128)
    m = g_ml.reshape(R, 4, ML_PAIR_COLS)
    gif = m[:, :, 768:772]
    return jnp.concatenate([
        a[:, :, 0].reshape(R, 1536), a[:, :, 1].reshape(R, 1536), a[:, :, 2].reshape(R, 1536),
        m[:, :, 0:128].reshape(R, 512), m[:, :, 128:256].reshape(R, 512),
        m[:, :, 256:512].reshape(R, 1024), m[:, :, 512:768].reshape(R, 1024),
        gif[:, :, 0:2].reshape(R, 8), gif[:, :, 2:4].reshape(R, 8), g_gate], axis=1)


def _blk8(v, width=128):
    r = v.shape[0]
    return v.reshape(r, 1024 // width, width).transpose(1, 0, 2)


def _unblk8(v):
    nb, r, w = v.shape
    return v.transpose(1, 0, 2).reshape(r, nb * w)


def local_step(x, target, mods, w, small, late_w=None, early_g=None, w_in_g=None):
    late_w = late_w or (lambda after: w)
    big = {}
    early_g = early_g or (lambda g: big.update(g) or 0.0)
    w_in_g = w_in_g or (lambda g: big.update(w_in=g) or 0.0)
    B = x.shape[0]
    T = B * SEQ
    shift1, scale1, gate1, shift2, scale2, gate2 = mods
    f2 = lambda a: a.reshape(T, a.shape[-1])
    f3 = lambda a: a.reshape(B, SEQ, a.shape[-1])

    rel_t = jnp.pad(small["rel_bias"].T, ((0, 4), (0, 0)))
    onehots = [_bucket_onehot(d) for _, d in ATT_GROUPS]
    biases = [bias_expand(rel_t, oh, f"bias_expand{g}").reshape(16, ATT_BLOCK, 2 * ATT_BLOCK)
              for g, oh in enumerate(onehots)]
    qg, kg = small["q_norm_g"], small["k_norm_g"]
    cw8 = _blk8(small["conv_w"])
    cb8 = _blk8(small["conv_b"])
    b_if = small["b_if"].reshape(2, 4, 2)
    bifp = jnp.concatenate([b_if[0], b_if[1], jnp.zeros((4, 124), F32)], axis=1).reshape(4, 1, 128)
    gn4 = small["mlstm_norm_g"].reshape(4, 1, 256)

    u = modnorm_fwd(x, small["norm1_g"], scale1, shift1, "modnorm1")
    u2d = f2(u)
    pa = f3(matmul(u2d, w["w_att"], mode="nn", name="proj_att"))
    pm = f3(matmul(u2d, w["w_ml"], mode="nn", name="proj_ml"))
    pg = f3(matmul(u2d, w["w_gate"], mode="nn", name="proj_gate"))
    os_, ls_ = [], []
    for g in range(3):
        o, l = attn_fwd(pa, biases[g], qg, kg, g, f"attn_fwd{g}")
        os_.append(o)
        ls_.append(l)
    att = merge_fwd(os_, ls_, "merge_fwd")
    hg = mlstm_fwd(pm, cw8, cb8, bifp, gn4, "mlstm_fwd")
    w = {**w, **late_w(hg)}
    y_att = matmul(f2(att), w["w_att_out"], mode="nn", name="att_out")
    y_ml = matmul(f2(hg), w["w_ml_out"], mode="nn", name="ml_out")
    z = gate_fwd(pg, f3(y_att), f3(y_ml), "gate_fwd")
    y = matmul(f2(z), w["w_out"], mode="nn", name="out_proj")
    x1, u2 = resid_modnorm_fwd(x, f3(y), gate1, small["norm2_g"], scale2, shift2, "resid_modnorm2")
    pre, hdn = matmul(f2(u2), w["w_ff1"], mode="nn", name="ff1", out_dtypes=(F32, BF16),
                      epi=lambda acc: (acc, jnp.square(jnp.maximum(acc, 0.0))))
    ffo = matmul(hdn, w["w_ff2"], mode="nn", name="ff2")
    dx2, loss = resid_loss(x1, f3(ffo), gate2, target, "resid_loss")

    d_ffo, d_gate2 = resid_bwd(dx2, f3(ffo), gate2, "resid_bwd2")
    g_ff2 = matmul(hdn, f2(d_ffo), mode="tn", name="g_ff2", out_dtypes=(BF16,))
    d_pre = matmul(f2(d_ffo), w["w_ff2"], mode="nt", name="d_hdn", out_dtypes=(BF16,), extras=(pre,),
                   epi=lambda acc, p: (acc * (2.0 * jnp.maximum(p, 0.0)),))
    g_ff1 = matmul(f2(u2), d_pre, mode="tn", name="g_ff1", out_dtypes=(BF16,))
    du2 = matmul(d_pre, w["w_ff1"], mode="nt", name="d_u2")
    dx1, d_norm2, d_scale2, d_shift2 = modnorm_bwd(x1, small["norm2_g"], scale2, shift2, f3(du2), dx2, "modnorm_bwd2")
    dy, d_gate1 = resid_bwd(dx1, f3(y), gate1, "resid_bwd1")
    g_out = matmul(f2(z), f2(dy), mode="tn", name="g_out", out_dtypes=(BF16,))
    dz = matmul(f2(dy), w["w_out"], mode="nt", name="d_z")
    dpg, d_ya, d_ym = gate_bwd(pg, f3(y_att), f3(y_ml), f3(dz), "gate_bwd")
    g_att_out = matmul(f2(att), f2(d_ya), mode="tn", name="g_att_out", out_dtypes=(BF16,))
    d_att = matmul(f2(d_ya), w["w_att_out"], mode="nt", name="d_att")
    g_ml_out = matmul(f2(hg), f2(d_ym), mode="tn", name="g_ml_out", out_dtypes=(BF16,))
    d_hg = matmul(f2(d_ym), w["w_ml_out"], mode="nt", name="d_hg")
    order = early_g(dict(w_att_out=g_att_out, w_ml_out=g_ml_out, w_out=g_out, w_ff1=g_ff1, w_ff2=g_ff2))
    dmerge = merge_bwd(os_, ls_, f3(d_att), "merge_bwd")
    dpa = jnp.zeros((B, SEQ, ATT_COLS), BF16)
    d_rel = []
    d_qg = d_kg = None
    for g in range(3):
        dpa, dbias, dq_g, dk_g = attn_bwd(pa, biases[g], qg + order, kg, dmerge[g], dmerge[3 + g], dpa, g,
                                          f"attn_bwd{g}")
        db8 = jnp.pad(dbias.reshape(4, -1), ((0, 4), (0, 0)))
        d_rel.append(bias_reduce(db8, onehots[g], f"bias_reduce{g}")[:4])
        d_qg = dq_g if d_qg is None else d_qg + dq_g
        d_kg = dk_g if d_kg is None else d_kg + dk_g
    dpm, dcw8, dcb8, dbifp, dgn4 = mlstm_bwd(pm, cw8, cb8, bifp, gn4 + order, f3(d_hg), "mlstm_bwd")
    g_w_att = matmul(u2d, f2(dpa), mode="tn", name="g_w_att", out_dtypes=(BF16,))
    g_w_ml = matmul(u2d, f2(dpm), mode="tn", name="g_w_ml", out_dtypes=(BF16,))
    g_w_gate = matmul(u2d, f2(dpg), mode="tn", name="g_w_gate", out_dtypes=(BF16,))
    order = w_in_g(merge_w_in(g_w_att, g_w_ml, g_w_gate))
    du = matmul(f2(dpa), w["w_att"], mode="nt", name="d_u_att")
    du = matmul(f2(dpm), w["w_ml"], mode="nt", name="d_u_ml", extras=(du,), epi=lambda acc, e: (acc + e,))
    du = matmul(f2(dpg), w["w_gate"], mode="nt", name="d_u_gate", extras=(du,), epi=lambda acc, e: (acc + e,))
    grad_x, d_norm1, d_scale1, d_shift1 = modnorm_bwd(x, small["norm1_g"] + order, scale1, shift1, f3(du), dx1,
                                                      "modnorm_bwd1")

    d_mods = (d_shift1, d_scale1, d_gate1, d_shift2, d_scale2, d_gate2)
    dbif = dbifp.reshape(4, 128)
    small_g = dict(
        norm1_g=d_norm1, norm2_g=d_norm2,
        b_if=jnp.stack([dbif[:, 0:2].reshape(8), dbif[:, 2:4].reshape(8)]),
        conv_w=_unblk8(dcw8), conv_b=_unblk8(dcb8), q_norm_g=d_qg, k_norm_g=d_kg,
        rel_bias=jnp.concatenate(d_rel, axis=0).T,
        mlstm_norm_g=dgn4.reshape(1, 1024))
    return loss, grad_x, d_mods, big, small_g


_SMALL = (("b_ada", 6144), ("norm1_g", 1024), ("norm2_g", 1024), ("b_if", 16), ("conv_b", 1024),
          ("q_norm_g", 128), ("k_norm_g", 128), ("rel_bias", 384), ("mlstm_norm_g", 1024), ("conv_w", 4096))
_SMALL_ROWS = 120
_REPL = _SMALL[:-1]
_REPL_ROWS = 86


def _pack(d, names, rows):
    flat = jnp.concatenate([d[k].reshape(-1) for k, _ in names])
    return jnp.pad(flat, (0, rows * 128 - flat.shape[0])).reshape(rows, 128)


def _unpack(slab, names, shapes):
    flat = slab.reshape(-1)
    out, off = {}, 0
    for k, nel in names:
        out[k] = flat[off:off + nel].reshape(shapes[k])
        off += nel
    return out


def kernel(x, c, w_ada, b_ada, norm1_g, norm2_g, w_in, b_if, conv_w, conv_b, q_norm_g, k_norm_g, rel_bias, mlstm_norm_g, w_att_out, w_ml_out, w_out, w_ff1, w_ff2, loss_target, m_w_ada, m_b_ada, m_norm1_g, m_norm2_g, m_w_in, m_b_if, m_conv_w, m_conv_b, m_q_norm_g, m_k_norm_g, m_rel_bias, m_mlstm_norm_g, m_w_att_out, m_w_ml_out, m_w_out, m_w_ff1, m_w_ff2, v_w_ada, v_b_ada, v_norm1_g, v_norm2_g, v_w_in, v_b_if, v_conv_w, v_conv_b, v_q_norm_g, v_k_norm_g, v_rel_bias, v_mlstm_norm_g, v_w_att_out, v_w_ml_out, v_w_out, v_w_ff1, v_w_ff2):
    P = dict(w_ada=w_ada, b_ada=b_ada, norm1_g=norm1_g, norm2_g=norm2_g, w_in=w_in, b_if=b_if, conv_w=conv_w,
             conv_b=conv_b, q_norm_g=q_norm_g, k_norm_g=k_norm_g, rel_bias=rel_bias, mlstm_norm_g=mlstm_norm_g,
             w_att_out=w_att_out, w_ml_out=w_ml_out, w_out=w_out, w_ff1=w_ff1, w_ff2=w_ff2)
    M = dict(w_ada=m_w_ada, b_ada=m_b_ada, norm1_g=m_norm1_g, norm2_g=m_norm2_g, w_in=m_w_in, b_if=m_b_if,
             conv_w=m_conv_w, conv_b=m_conv_b, q_norm_g=m_q_norm_g, k_norm_g=m_k_norm_g, rel_bias=m_rel_bias,
             mlstm_norm_g=m_mlstm_norm_g, w_att_out=m_w_att_out, w_ml_out=m_w_ml_out, w_out=m_w_out,
             w_ff1=m_w_ff1, w_ff2=m_w_ff2)
    V = dict(w_ada=v_w_ada, b_ada=v_b_ada, norm1_g=v_norm1_g, norm2_g=v_norm2_g, w_in=v_w_in, b_if=v_b_if,
             conv_w=v_conv_w, conv_b=v_conv_b, q_norm_g=v_q_norm_g, k_norm_g=v_k_norm_g, rel_bias=v_rel_bias,
             mlstm_norm_g=v_mlstm_norm_g, w_att_out=v_w_att_out, w_ml_out=v_w_ml_out, w_out=v_w_out,
             w_ff1=v_w_ff1, w_ff2=v_w_ff2)
    names = list(P)
    shapes = {k: P[k].shape for k in names}
    B = x.shape[0]
    me = _index(_mesh_pos())

    big_names = ("w_in", "w_att_out", "w_ml_out", "w_out", "w_ff1", "w_ff2")
    shards = cast_bf16([P[k][0] for k in big_names], "cast_w")
    w_in_g8, c8, conv_w8 = all_gather_two_level([shards[0], c, conv_w[0]], "gather_w_in")
    late_handle, late_order = exchange_start(shards[1:], "gather_late_start", False)
    c_all = c8.reshape(N_DEV * B, D_MODEL)
    conv_w_full = conv_w8.transpose(1, 0, 2).reshape(4, 1024)
    w_att, w_ml, w_gate = split_w_in(w_in_g8.transpose(1, 0, 2).reshape(D_MODEL, D_IN))
    w = dict(w_att=w_att, w_ml=w_ml, w_gate=w_gate)

    def late_w(after):
        lands = exchange_wait(late_handle, after, "gather_late_wait", False)
        gw = dict(zip(big_names[1:], [_own_slot(l, s) for l, s in zip(lands, shards[1:])]))
        return dict(w_att_out=gw["w_att_out"].transpose(1, 0, 2).reshape(512, D_MODEL),
                    w_ml_out=gw["w_ml_out"].reshape(D_MODEL, D_MODEL), w_out=gw["w_out"].reshape(D_MODEL, D_MODEL),
                    w_ff1=gw["w_ff1"].transpose(1, 0, 2).reshape(D_MODEL, D_FF),
                    w_ff2=gw["w_ff2"].reshape(D_FF, D_MODEL))

    pending = {}

    def send_grads(key, blocks, name):
        handle, order = exchange_start(blocks, name, True)
        pending[key] = (handle, [lax.dynamic_index_in_dim(b, me, 0, keepdims=False) for b in blocks])
        return order[0, 0]

    def early_g(g):
        return send_grads("late", [g["w_att_out"].reshape(512, N_DEV, 128).transpose(1, 0, 2),
                                   g["w_ml_out"].reshape(N_DEV, 128, D_MODEL), g["w_out"].reshape(N_DEV, 128, D_MODEL),
                                   g["w_ff1"].reshape(D_MODEL, N_DEV, 512).transpose(1, 0, 2),
                                   g["w_ff2"].reshape(N_DEV, 512, D_MODEL)], "grad_late_start")

    def w_in_g(g):
        return send_grads("w_in", [g.reshape(D_MODEL, N_DEV, W_IN_SHARD).transpose(1, 0, 2)], "grad_w_in_start")

    def recv_grads(key, after, name):
        handle, own = pending[key]
        return [_own_slot(l, o) for l, o in zip(exchange_wait(handle, after, name, True), own)]

    (silu_c,) = small_call(lambda a: (_silu(a),), [c_all], [jax.ShapeDtypeStruct(c_all.shape, F32)], "silu_c")
    b_ada_cols = lax.dynamic_slice(b_ada, (0, me * 768), (1, 768))
    ada_cols = matmul(silu_c, w_ada[0], mode="nn", name="ada", extras=(jnp.broadcast_to(b_ada_cols, (N_DEV * B, 768)),),
                      epi=lambda acc, bb: (acc + bb,))
    (ada_t,) = all_to_all([ada_cols.reshape(N_DEV, B, 768)], "ada_exchange")
    ada = ada_t.transpose(1, 0, 2).reshape(B, 6 * D_MODEL)
    mods = tuple(ada[:, i * D_MODEL:(i + 1) * D_MODEL].reshape(B, 1, D_MODEL) for i in range(6))

    small = dict(norm1_g=norm1_g + late_order[0, 0], norm2_g=norm2_g, b_if=b_if[0], conv_w=conv_w_full, conv_b=conv_b,
                 q_norm_g=q_norm_g, k_norm_g=k_norm_g, rel_bias=rel_bias, mlstm_norm_g=mlstm_norm_g)
    loss, grad_x, d_mods, _, small_g = local_step(x, loss_target, mods, w, small, late_w, early_g, w_in_g)
    loss = lax.psum(loss[0, 0], ("x", "y", "c"))

    d_ada = jnp.concatenate([d.reshape(B, D_MODEL) for d in d_mods], axis=1)
    (d_ada_t,) = all_to_all([d_ada.reshape(B, N_DEV, 768).transpose(1, 0, 2)], "d_ada_exchange")
    d_ada_cols = d_ada_t.reshape(N_DEV * B, 768)
    g_w_ada = matmul(silu_c, d_ada_cols, mode="tn", name="g_w_ada")
    (g_b_cols,) = small_call(lambda a: (jnp.sum(a, axis=0, keepdims=True),), [d_ada_cols],
                             [jax.ShapeDtypeStruct((1, 768), F32)], "g_b_ada_cols")
    small_g["b_ada"] = lax.dynamic_update_slice(jnp.zeros((1, 6144), F32), g_b_cols, (0, me * 768))

    recv = recv_grads("w_in", grad_x, "grad_w_in_wait") + recv_grads("late", grad_x, "grad_late_wait")
    (small_parts,) = all_gather([_pack(small_g, _SMALL, _SMALL_ROWS)], "small_grad_gather")
    small_sum = sum_parts(small_parts, "small_grad_sum")
    sg = _unpack(small_sum, _SMALL, {**{k: shapes[k] for k, _ in _REPL}, "conv_w": (4, 1024)})

    G, Dl, NM, NV = {}, {}, {}, {}
    for k, parts in zip(big_names, recv):
        g, d, nm, nv = adamw(P[k][0], parts, M[k][0], V[k][0], f"adamw_{k}", parts=True)
        G[k], Dl[k], NM[k], NV[k] = g[None], d[None], nm[None], nv[None]
    g, d, nm, nv = adamw(w_ada[0], g_w_ada, m_w_ada[0], v_w_ada[0], "adamw_w_ada")
    G["w_ada"], Dl["w_ada"], NM["w_ada"], NV["w_ada"] = g[None], d[None], nm[None], nv[None]
    g_conv = lax.dynamic_slice(sg["conv_w"], (0, me * 128), (4, 128))
    g, d, nm, nv = adamw(conv_w[0], g_conv, m_conv_w[0], v_conv_w[0], "adamw_conv_w")
    G["conv_w"], Dl["conv_w"], NM["conv_w"], NV["conv_w"] = g[None], d[None], nm[None], nv[None]
    gslab = _pack(sg, _REPL, _REPL_ROWS)
    _, d, nm, nv = adamw(_pack(P, _REPL, _REPL_ROWS), gslab, _pack(M, _REPL, _REPL_ROWS), _pack(V, _REPL, _REPL_ROWS),
                         "adamw_small")
    rs = {k: shapes[k] for k, _ in _REPL}
    d, nm, nv = _unpack(d, _REPL, rs), _unpack(nm, _REPL, rs), _unpack(nv, _REPL, rs)
    for k, _ in _REPL:
        G[k], Dl[k], NM[k], NV[k] = sg[k], d[k], nm[k], nv[k]

    return (loss, grad_x, *[G[k] for k in names], *[Dl[k] for k in names], *[NM[k] for k in names],
            *[NV[k] for k in names])
```

```python
import functools
import math

import numpy as np
import jax
import jax.numpy as jnp
from jax import lax
from jax.experimental import pallas as pl
from jax.experimental.pallas import tpu as pltpu

F32 = jnp.float32
BF16 = jnp.bfloat16

N_DEV = 8
D_MODEL = 1024
SEQ = 2048
ATT_GROUPS = ((128, 1), (512, 4), (2048, 16))
N_ATT_HEADS = 12
ATT_BLOCK = 128
HEAD_DIM = 128
ML_HEADS = 8
ML_PAIRS = 4
ML_CHUNK = 64
N_CHUNKS = SEQ // ML_CHUNK
N_BUCKETS = 32
MAX_DISTANCE = 2048
D_FF = 4096
D_IN = 9744
EPS = 1e-6

ADAM_LR = 0.001
ADAM_B1 = 0.9
ADAM_B2 = 0.999
ADAM_EPS = 1e-08
ADAM_WD = 0.01
ADAM_STEP = 10

ATT_HEAD_COLS = 3 * HEAD_DIM
ATT_COLS = N_ATT_HEADS * ATT_HEAD_COLS
ML_PAIR_COLS = 896
ML_COLS = ML_PAIRS * ML_PAIR_COLS
GATE_COLS = 2 * D_MODEL
W_IN_SHARD = D_IN // N_DEV

VMEM_LIMIT = 56 * 1024 * 1024


def _cparams(**kw):
    return pltpu.CompilerParams(vmem_limit_bytes=VMEM_LIMIT, **kw)


_NN = ((1,), (0,))
_NT = ((1,), (1,))
_TN = ((0,), (0,))


def _mxu(a, b, dims):
    return lax.dot_general(a.astype(BF16), b.astype(BF16), (dims, ((), ())), preferred_element_type=F32)


@jax.custom_vjp
def bdot_nn(a, b):
    return _mxu(a, b, _NN)


def _nn_fwd(a, b):
    return _mxu(a, b, _NN), (a, b)


def _nn_bwd(res, g):
    a, b = res
    return _mxu(g, b, _NT), _mxu(a, g, _TN)


bdot_nn.defvjp(_nn_fwd, _nn_bwd)


@jax.custom_vjp
def bdot_nt(a, b):
    return _mxu(a, b, _NT)


def _nt_fwd(a, b):
    return _mxu(a, b, _NT), (a, b)


def _nt_bwd(res, g):
    a, b = res
    return _mxu(g, b, _NN), _mxu(g, a, _TN)


bdot_nt.defvjp(_nt_fwd, _nt_bwd)


@jax.custom_vjp
def bdot_tn(a, b):
    return _mxu(a, b, _TN)


def _tn_fwd(a, b):
    return _mxu(a, b, _TN), (a, b)


def _tn_bwd(res, g):
    a, b = res
    return _mxu(b, g, _NT), _mxu(a, g, _NN)


bdot_tn.defvjp(_tn_fwd, _tn_bwd)


def _doth(a, b, dims=_NN):
    return lax.dot_general(a, b, (dims, ((), ())), precision=lax.Precision.HIGHEST, preferred_element_type=F32)


def _rms(x):
    return x * lax.rsqrt(jnp.mean(x * x, axis=-1, keepdims=True) + EPS)


def _pick(n, cands):
    for t in cands:
        if n % t == 0:
            return t
    raise ValueError(f"no tile for {n}")


MM_TILE_M = (1024, 512, 256, 128, 64, 32, 16, 8)
MM_TILE_N = (2048, 1792, 1536, 1024, 768, 512, 256, 128)
MM_TILE_K = (1024, 512, 256, 128, 64, 32)

def matmul(a, b, *, mode, name, out_dtypes=(F32,), epi=None, extras=(), after=()):
    if mode == "nn":
        (M, K), (K2, N) = a.shape, b.shape
    elif mode == "nt":
        (M, K), (N, K2) = a.shape, b.shape
    else:
        (K, M), (K2, N) = a.shape, b.shape
    assert K == K2, (a.shape, b.shape, mode)
    tm = _pick(M, MM_TILE_M)
    tn = _pick(N, MM_TILE_N)
    tk = _pick(K, MM_TILE_K)
    nk = K // tk
    n_ex = len(extras)
    n_out = len(out_dtypes)
    dims = {"nn": _NN, "nt": _NT, "tn": _TN}[mode]

    def finish(r, ex_refs, out_refs):
        outs = epi(r, *[e[...] for e in ex_refs]) if epi is not None else (r,)
        for o_ref, o in zip(out_refs, outs):
            o_ref[...] = o.astype(o_ref.dtype)

    def body(*refs):
        a_ref, b_ref = refs[0], refs[1]
        ex_refs = refs[2:2 + n_ex]
        out_refs = refs[2 + n_ex + len(after):2 + n_ex + len(after) + n_out]
        if nk == 1:
            finish(_mxu(a_ref[...], b_ref[...], dims), ex_refs, out_refs)
            return
        acc = refs[2 + n_ex + len(after) + n_out]
        k = pl.program_id(2)

        @pl.when(k == 0)
        def _():
            acc[...] = jnp.zeros_like(acc)

        acc[...] += _mxu(a_ref[...], b_ref[...], dims)

        @pl.when(k == nk - 1)
        def _():
            finish(acc[...], ex_refs, out_refs)

    if mode == "nn":
        a_spec = pl.BlockSpec((tm, tk), lambda i, j, k: (i, k))
        b_spec = pl.BlockSpec((tk, tn), lambda i, j, k: (k, j))
    elif mode == "nt":
        a_spec = pl.BlockSpec((tm, tk), lambda i, j, k: (i, k))
        b_spec = pl.BlockSpec((tn, tk), lambda i, j, k: (j, k))
    else:
        a_spec = pl.BlockSpec((tk, tm), lambda i, j, k: (k, i))
        b_spec = pl.BlockSpec((tk, tn), lambda i, j, k: (k, j))
    o_spec = pl.BlockSpec((tm, tn), lambda i, j, k: (i, j))
    res = pl.pallas_call(
        body,
        name=name,
        grid=(M // tm, N // tn, nk),
        in_specs=[a_spec, b_spec] + [o_spec] * n_ex + [pl.BlockSpec(memory_space=pl.ANY)] * len(after),
        out_specs=[o_spec] * n_out,
        out_shape=[jax.ShapeDtypeStruct((M, N), dt) for dt in out_dtypes],
        scratch_shapes=[pltpu.VMEM((tm, tn), F32)] if nk > 1 else [],
        compiler_params=_cparams(),
    )(a, b, *extras, *after)
    return res[0] if n_out == 1 else tuple(res)


def small_call(fn, inputs, out_shapes, name):
    n_in = len(inputs)

    def body(*refs):
        outs = fn(*[r[...] for r in refs[:n_in]])
        for o_ref, o in zip(refs[n_in:], outs):
            o_ref[...] = o.astype(o_ref.dtype)

    res = pl.pallas_call(body, name=name, out_shape=list(out_shapes), compiler_params=_cparams())(*inputs)
    return tuple(res)


ROW_TILE = 512


def _modnorm(x, g, scale, shift):
    return _rms(x) * g * (1.0 + scale) + shift


def _row_spec(width):
    return pl.BlockSpec((1, ROW_TILE, width), lambda b, i: (b, i, 0))


def _mod_spec():
    return pl.BlockSpec((1, 1, D_MODEL), lambda b, i: (b, 0, 0))


def _vec_spec():
    return pl.BlockSpec((1, D_MODEL), lambda b, i: (0, 0))


def modnorm_fwd(x, g, scale, shift, name):
    B, S, D = x.shape

    def body(x_ref, g_ref, sc_ref, sh_ref, u_ref):
        u_ref[0] = _modnorm(x_ref[0], g_ref[...], sc_ref[0], sh_ref[0]).astype(BF16)

    return pl.pallas_call(
        body, name=name, grid=(B, S // ROW_TILE),
        in_specs=[_row_spec(D), _vec_spec(), _mod_spec(), _mod_spec()],
        out_specs=_row_spec(D),
        out_shape=jax.ShapeDtypeStruct((B, S, D), BF16),
        compiler_params=_cparams(),
    )(x, g, scale, shift)


def resid_modnorm_fwd(x, y, gate, g, scale, shift, name):
    B, S, D = x.shape

    def body(x_ref, y_ref, gt_ref, g_ref, sc_ref, sh_ref, x1_ref, u_ref):
        x1 = x_ref[0] + gt_ref[0] * y_ref[0]
        x1_ref[0] = x1
        u_ref[0] = _modnorm(x1, g_ref[...], sc_ref[0], sh_ref[0]).astype(BF16)

    return pl.pallas_call(
        body, name=name, grid=(B, S // ROW_TILE),
        in_specs=[_row_spec(D), _row_spec(D), _mod_spec(), _vec_spec(), _mod_spec(), _mod_spec()],
        out_specs=[_row_spec(D), _row_spec(D)],
        out_shape=[jax.ShapeDtypeStruct((B, S, D), F32), jax.ShapeDtypeStruct((B, S, D), BF16)],
        compiler_params=_cparams(),
    )(x, y, gate, g, scale, shift)


def resid_loss(x1, ffo, gate, target, name):
    B, S, D = x1.shape

    def body(x_ref, f_ref, gt_ref, t_ref, dx_ref, loss_ref):
        first = jnp.logical_and(pl.program_id(0) == 0, pl.program_id(1) == 0)

        @pl.when(first)
        def _():
            loss_ref[...] = jnp.zeros_like(loss_ref)

        err = x_ref[0] + gt_ref[0] * f_ref[0] - t_ref[0]
        dx_ref[0] = err * (1.0 / D)
        loss_ref[...] += 0.5 * jnp.sum(jnp.mean(err * err, axis=-1, keepdims=True), axis=0, keepdims=True)

    return pl.pallas_call(
        body, name=name, grid=(B, S // ROW_TILE),
        in_specs=[_row_spec(D), _row_spec(D), _mod_spec(), _row_spec(D)],
        out_specs=[_row_spec(D), pl.BlockSpec((1, 1), lambda b, i: (0, 0))],
        out_shape=[jax.ShapeDtypeStruct((B, S, D), F32), jax.ShapeDtypeStruct((1, 1), F32)],
        compiler_params=_cparams(),
    )(x1, ffo, gate, target)


def resid_bwd(dx, y, gate, name):
    B, S, D = dx.shape

    def body(dx_ref, y_ref, gt_ref, dy_ref, dg_ref):
        @pl.when(pl.program_id(1) == 0)
        def _():
            dg_ref[...] = jnp.zeros_like(dg_ref)

        d = dx_ref[0]
        dy_ref[0] = (gt_ref[0] * d).astype(BF16)
        dg_ref[0] += jnp.sum(d * y_ref[0], axis=0, keepdims=True)

    return pl.pallas_call(
        body, name=name, grid=(B, S // ROW_TILE),
        in_specs=[_row_spec(D), _row_spec(D), _mod_spec()],
        out_specs=[_row_spec(D), _mod_spec()],
        out_shape=[jax.ShapeDtypeStruct((B, S, D), BF16), jax.ShapeDtypeStruct((B, 1, D), F32)],
        compiler_params=_cparams(),
    )(dx, y, gate)


def modnorm_bwd(x, g, scale, shift, du, dx_res, name):
    B, S, D = x.shape

    def body(x_ref, g_ref, sc_ref, sh_ref, du_ref, dr_ref, dx_ref, dg_ref, dsc_ref, dsh_ref):
        first = jnp.logical_and(pl.program_id(0) == 0, pl.program_id(1) == 0)

        @pl.when(first)
        def _():
            dg_ref[...] = jnp.zeros_like(dg_ref)

        @pl.when(pl.program_id(1) == 0)
        def _():
            dsc_ref[...] = jnp.zeros_like(dsc_ref)
            dsh_ref[...] = jnp.zeros_like(dsh_ref)

        _, vjp = jax.vjp(_modnorm, x_ref[0], g_ref[...], sc_ref[0], sh_ref[0])
        dx, dg, dsc, dsh = vjp(du_ref[0].astype(F32))
        dx_ref[0] = dx + dr_ref[0]
        dg_ref[...] += dg
        dsc_ref[0] += dsc
        dsh_ref[0] += dsh

    return pl.pallas_call(
        body, name=name, grid=(B, S // ROW_TILE),
        in_specs=[_row_spec(D), _vec_spec(), _mod_spec(), _mod_spec(), _row_spec(D), _row_spec(D)],
        out_specs=[_row_spec(D), _vec_spec(), _mod_spec(), _mod_spec()],
        out_shape=[jax.ShapeDtypeStruct((B, S, D), F32), jax.ShapeDtypeStruct((1, D), F32),
                   jax.ShapeDtypeStruct((B, 1, D), F32), jax.ShapeDtypeStruct((B, 1, D), F32)],
        compiler_params=_cparams(),
    )(x, g, scale, shift, du, dx_res)


def _bucket_table(dilation):
    i = np.arange(ATT_BLOCK)[:, None]
    j = np.arange(2 * ATT_BLOCK)[None, :]
    delta = ATT_BLOCK + i - j
    dist = np.maximum(delta, 0) * dilation
    max_exact = N_BUCKETS // 2
    d = np.maximum(dist, max_exact).astype(np.float32)
    large = max_exact + (np.log(d / np.float32(max_exact)) / np.float32(math.log(MAX_DISTANCE / max_exact))
                         * np.float32(N_BUCKETS - max_exact)).astype(np.int32)
    large = np.minimum(large, N_BUCKETS - 1)
    return np.where(dist < max_exact, dist, large).astype(np.int32)


def _bucket_onehot(dilation):
    bt = jnp.asarray(_bucket_table(dilation).reshape(1, -1))
    return (bt == jnp.arange(N_BUCKETS, dtype=jnp.int32)[:, None]).astype(F32)


def bias_expand(rel_t, onehot, name):
    def fn(r, oh):
        return (_doth(r, oh),)
    return small_call(fn, [rel_t, onehot], [jax.ShapeDtypeStruct((rel_t.shape[0], onehot.shape[1]), F32)], name)[0]


def bias_reduce(dbias_flat, onehot, name):
    def fn(db, oh):
        return (_doth(db, oh, _NT),)
    return small_call(fn, [dbias_flat, onehot], [jax.ShapeDtypeStruct((dbias_flat.shape[0], N_BUCKETS), F32)], name)[0]


def _attn_tile(q, k, v, bias, qg, kg):
    qn = _rms(q) * qg
    kn = _rms(k) * kg
    s = bdot_nt(qn, kn) * (HEAD_DIM ** -0.5) + bias
    kl = k.shape[0]
    i = lax.broadcasted_iota(jnp.int32, (ATT_BLOCK, kl), 0)
    j = lax.broadcasted_iota(jnp.int32, (ATT_BLOCK, kl), 1) + (2 * ATT_BLOCK - kl)
    valid = jnp.logical_and(j >= i, j <= i + ATT_BLOCK)
    s = jnp.where(valid, s, -jnp.inf)
    m = jnp.max(s, axis=-1, keepdims=True)
    p = jnp.exp(s - m)
    l = jnp.sum(p, axis=-1, keepdims=True)
    o = bdot_nn(p, v) / l
    lse = jnp.broadcast_to(m + jnp.log(l), (ATT_BLOCK, HEAD_DIM))
    return o, lse


def _attn_tiles(dilation):
    nb = SEQ // dilation // ATT_BLOCK
    return [(r, n) for r in range(dilation) for n in range(nb)]


def _attn_rows(r, n, dilation, nblk=1):
    if dilation == 1:
        return pl.ds(r + n * ATT_BLOCK, nblk * ATT_BLOCK)
    return pl.ds(r + n * ATT_BLOCK * dilation, nblk * ATT_BLOCK, stride=dilation)


_QL, _KL, _VL = slice(0, 128), slice(128, 256), slice(256, 384)


def _qkv_specs(hb):
    return [pl.BlockSpec((None, SEQ, HEAD_DIM), functools.partial(lambda b, h, j: (b, 0, 3 * (hb + h) + j), j=j))
            for j in range(3)]


def attn_fwd(pa, bias, qg, kg, group, name):
    B = pa.shape[0]
    dilation = ATT_GROUPS[group][1]
    hb = group * 4

    def body(q_ref, k_ref, v_ref, b_ref, qg_ref, kg_ref, o_ref, l_ref):
        qg_, kg_ = qg_ref[...], kg_ref[...]
        for (r, n) in _attn_tiles(dilation):
            rows = _attn_rows(r, n, dilation)
            q = q_ref[rows, :]
            if n == 0:
                krows, bias_t = rows, b_ref[0, :, ATT_BLOCK:]
            else:
                krows, bias_t = _attn_rows(r, n - 1, dilation, 2), b_ref[0]
            o, lse = _attn_tile(q, k_ref[krows, :], v_ref[krows, :], bias_t, qg_, kg_)
            o_ref[rows, :] = o
            l_ref[rows, :] = lse

    head_out = pl.BlockSpec((None, SEQ, HEAD_DIM), lambda b, h: (b, 0, h))
    return pl.pallas_call(
        body, name=name, grid=(B, 4),
        in_specs=_qkv_specs(hb) + [
                  pl.BlockSpec((1, ATT_BLOCK, 2 * ATT_BLOCK), lambda b, h: (hb + h, 0, 0)),
                  pl.BlockSpec((1, HEAD_DIM), lambda b, h: (0, 0)),
                  pl.BlockSpec((1, HEAD_DIM), lambda b, h: (0, 0))],
        out_specs=[head_out, head_out],
        out_shape=[jax.ShapeDtypeStruct((B, SEQ, 512), F32), jax.ShapeDtypeStruct((B, SEQ, 512), F32)],
        compiler_params=_cparams(),
    )(pa, pa, pa, bias, qg, kg)


def attn_bwd(pa, bias, qg, kg, do, dlse, dpa, group, name):
    B = pa.shape[0]
    dilation = ATT_GROUPS[group][1]
    hb = group * 4

    def body(q_ref, k_ref, v_ref, b_ref, qg_ref, kg_ref, do_ref, dl_ref, dpa_in,
             dp_ref, db_ref, dqg_ref, dkg_ref, dq_s, dk_s, dv_s):
        del dpa_in
        h_id = pl.program_id(1)

        @pl.when(jnp.logical_and(pl.program_id(0) == 0, h_id == 0))
        def _():
            db_ref[...] = jnp.zeros_like(db_ref)
            dqg_ref[...] = jnp.zeros_like(dqg_ref)
            dkg_ref[...] = jnp.zeros_like(dkg_ref)

        dk_s[...] = jnp.zeros_like(dk_s)
        dv_s[...] = jnp.zeros_like(dv_s)
        qg_, kg_ = qg_ref[...], kg_ref[...]
        for (r, n) in _attn_tiles(dilation):
            rows = _attn_rows(r, n, dilation)
            q = q_ref[rows, :]
            if n == 0:
                krows, bias_t = rows, b_ref[0, :, ATT_BLOCK:]
            else:
                krows, bias_t = _attn_rows(r, n - 1, dilation, 2), b_ref[0]
            _, vjp = jax.vjp(_attn_tile, q, k_ref[krows, :], v_ref[krows, :], bias_t, qg_, kg_)
            dq, dk, dv, dbias, dqg, dkg = vjp((do_ref[rows, :], dl_ref[rows, :]))
            dq_s[rows, :] = dq
            dk_s[krows, :] += dk
            dv_s[krows, :] += dv
            if n == 0:
                db_ref[h_id, :, ATT_BLOCK:] += dbias
            else:
                db_ref[h_id] += dbias
            dqg_ref[...] += dqg
            dkg_ref[...] += dkg
        dp_ref[0, :, _QL] = dq_s[...].astype(BF16)
        dp_ref[0, :, _KL] = dk_s[...].astype(BF16)
        dp_ref[0, :, _VL] = dv_s[...].astype(BF16)

    const2 = lambda b, h: (0, 0)
    head_in = pl.BlockSpec((None, SEQ, HEAD_DIM), lambda b, h: (b, 0, h))
    head_blk = pl.BlockSpec((1, SEQ, ATT_HEAD_COLS), lambda b, h: (b, 0, hb + h))
    return pl.pallas_call(
        body, name=name, grid=(B, 4),
        in_specs=_qkv_specs(hb) + [
                  pl.BlockSpec((1, ATT_BLOCK, 2 * ATT_BLOCK), lambda b, h: (hb + h, 0, 0)),
                  pl.BlockSpec((1, HEAD_DIM), const2), pl.BlockSpec((1, HEAD_DIM), const2),
                  head_in, head_in,
                  pl.BlockSpec(memory_space=pl.ANY)],
        out_specs=[head_blk,
                   pl.BlockSpec((4, ATT_BLOCK, 2 * ATT_BLOCK), lambda b, h: (0, 0, 0)),
                   pl.BlockSpec((1, HEAD_DIM), const2), pl.BlockSpec((1, HEAD_DIM), const2)],
        out_shape=[jax.ShapeDtypeStruct(dpa.shape, BF16),
                   jax.ShapeDtypeStruct((4, ATT_BLOCK, 2 * ATT_BLOCK), F32),
                   jax.ShapeDtypeStruct((1, HEAD_DIM), F32), jax.ShapeDtypeStruct((1, HEAD_DIM), F32)],
        scratch_shapes=[pltpu.VMEM((SEQ, HEAD_DIM), F32)] * 3,
        input_output_aliases={8: 0},
        compiler_params=_cparams(),
    )(pa, pa, pa, bias, qg, kg, do, dlse, dpa)


def _merge(o0, o1, o2, l0, l1, l2):
    mx = jnp.maximum(jnp.maximum(l0, l1), l2)
    e0, e1, e2 = jnp.exp(l0 - mx), jnp.exp(l1 - mx), jnp.exp(l2 - mx)
    den = e0 + e1 + e2
    return (e0 / den) * o0 + (e1 / den) * o1 + (e2 / den) * o2


def merge_fwd(os_, ls_, name):
    B = os_[0].shape[0]

    def body(o0, o1, o2, l0, l1, l2, a_ref):
        a_ref[0] = _merge(o0[0], o1[0], o2[0], l0[0], l1[0], l2[0]).astype(BF16)

    return pl.pallas_call(
        body, name=name, grid=(B, SEQ // ROW_TILE),
        in_specs=[_row_spec(512)] * 6, out_specs=_row_spec(512),
        out_shape=jax.ShapeDtypeStruct((B, SEQ, 512), BF16),
        compiler_params=_cparams(),
    )(*os_, *ls_)


def merge_bwd(os_, ls_, datt, name):
    B = os_[0].shape[0]

    def body(o0, o1, o2, l0, l1, l2, da_ref, *outs):
        _, vjp = jax.vjp(_merge, o0[0], o1[0], o2[0], l0[0], l1[0], l2[0])
        for o_ref, g in zip(outs, vjp(da_ref[0])):
            o_ref[0] = g

    return pl.pallas_call(
        body, name=name, grid=(B, SEQ // ROW_TILE),
        in_specs=[_row_spec(512)] * 7, out_specs=[_row_spec(512)] * 6,
        out_shape=[jax.ShapeDtypeStruct((B, SEQ, 512), F32)] * 6,
        compiler_params=_cparams(),
    )(*os_, *ls_, datt)


def _gate_mix(ga, gm, ya, ym):
    return jax.nn.sigmoid(ga) * ya + jax.nn.sigmoid(gm) * ym


def gate_fwd(pg, ya, ym, name):
    B = pg.shape[0]

    def body(ga, gm, ya_ref, ym_ref, z_ref):
        z_ref[0] = _gate_mix(ga[0], gm[0], ya_ref[0], ym_ref[0]).astype(BF16)

    return pl.pallas_call(
        body, name=name, grid=(B, SEQ // ROW_TILE),
        in_specs=[pl.BlockSpec((1, ROW_TILE, D_MODEL), lambda b, i: (b, i, 0)),
                  pl.BlockSpec((1, ROW_TILE, D_MODEL), lambda b, i: (b, i, 1)),
                  _row_spec(D_MODEL), _row_spec(D_MODEL)],
        out_specs=_row_spec(D_MODEL),
        out_shape=jax.ShapeDtypeStruct((B, SEQ, D_MODEL), BF16),
        compiler_params=_cparams(),
    )(pg, pg, ya, ym)


def gate_bwd(pg, ya, ym, dz, name):
    B = pg.shape[0]

    def body(ga, gm, ya_ref, ym_ref, dz_ref, dpg_ref, dya_ref, dym_ref):
        _, vjp = jax.vjp(_gate_mix, ga[0], gm[0], ya_ref[0], ym_ref[0])
        dga, dgm, dya, dym = vjp(dz_ref[0])
        dpg_ref[0, :, :D_MODEL] = dga.astype(BF16)
        dpg_ref[0, :, D_MODEL:] = dgm.astype(BF16)
        dya_ref[0] = dya.astype(BF16)
        dym_ref[0] = dym.astype(BF16)

    return pl.pallas_call(
        body, name=name, grid=(B, SEQ // ROW_TILE),
        in_specs=[pl.BlockSpec((1, ROW_TILE, D_MODEL), lambda b, i: (b, i, 0)),
                  pl.BlockSpec((1, ROW_TILE, D_MODEL), lambda b, i: (b, i, 1)),
                  _row_spec(D_MODEL), _row_spec(D_MODEL), _row_spec(D_MODEL)],
        out_specs=[_row_spec(GATE_COLS), _row_spec(D_MODEL), _row_spec(D_MODEL)],
        out_shape=[jax.ShapeDtypeStruct((B, SEQ, GATE_COLS), BF16),
                   jax.ShapeDtypeStruct((B, SEQ, D_MODEL), BF16), jax.ShapeDtypeStruct((B, SEQ, D_MODEL), BF16)],
        compiler_params=_cparams(),
    )(pg, pg, ya, ym, dz)


def _log_sigmoid(x):
    return jnp.minimum(x, 0.0) - jnp.log(1.0 + jnp.exp(-jnp.abs(x)))


def _head_mask(e):
    lane = lax.broadcasted_iota(jnp.int32, (1, 128), 1)
    return jnp.logical_and(lane >= e * 64, lane < (e + 1) * 64).astype(F32)


def _bmxu(a, b, ca, cb):
    return lax.dot_general(a.astype(BF16), b.astype(BF16), (((ca,), (cb,)), ((0,), (0,))), preferred_element_type=F32)


@jax.custom_vjp
def cdot_nt(a, b):
    return _bmxu(a, b, 2, 2)


cdot_nt.defvjp(lambda a, b: (_bmxu(a, b, 2, 2), (a, b)),
               lambda res, g: (_bmxu(g, res[1], 2, 1), _bmxu(g, res[0], 1, 1)))


@jax.custom_vjp
def cdot_nn(a, b):
    return _bmxu(a, b, 2, 1)


cdot_nn.defvjp(lambda a, b: (_bmxu(a, b, 2, 1), (a, b)),
               lambda res, g: (_bmxu(g, res[1], 2, 2), _bmxu(res[0], g, 1, 1)))


@jax.custom_vjp
def cdot_tn(a, b):
    return _bmxu(a, b, 1, 1)


cdot_tn.defvjp(lambda a, b: (_bmxu(a, b, 1, 1), (a, b)),
               lambda res, g: (_bmxu(res[1], g, 2, 2), _bmxu(res[0], g, 2, 1)))


def _cdoth(a, b):
    return lax.dot_general(a, b, (((2,), (1,)), ((0,), (0,))), precision=lax.Precision.HIGHEST,
                           preferred_element_type=F32)


def _ml_intra(q2, k2, v, ifb, *, e):
    c, L = N_CHUNKS, ML_CHUNK
    hm = _head_mask(e)
    q3 = (q2 * hm).reshape(c, L, 128)
    k3 = (k2 * hm).reshape(c, L, 128)
    v3 = v.reshape(c, L, 128)
    if3 = ifb.reshape(c, L, 128)
    lanes = lax.broadcasted_iota(jnp.int32, (c, L, 128), 2)
    li = jnp.sum(jnp.where(lanes == e, if3, 0.0), axis=-1, keepdims=True)
    ri = lax.broadcasted_iota(jnp.int32, (L, L), 0)
    ci = lax.broadcasted_iota(jnp.int32, (L, L), 1)
    causal = ri >= ci
    tri = jnp.broadcast_to(causal.astype(F32), (c, L, L))
    cs = _cdoth(tri, _log_sigmoid(if3))
    b = jnp.sum(jnp.where(lanes == 2 + e, cs, 0.0), axis=-1, keepdims=True)
    last = lax.broadcasted_iota(jnp.int32, (1, L, 1), 1) == L - 1
    b_end = jnp.sum(jnp.where(last, b, 0.0), axis=1, keepdims=True)
    rrow = _cdoth(jnp.ones((c, L, L), F32), (ri == ci).astype(F32) * (li - b))
    Dm = jnp.where(causal, b + rrow, -jnp.inf)
    mD = lax.stop_gradient(jnp.max(Dm, axis=-1, keepdims=True))
    P0 = cdot_nt(q3, k3) * jnp.exp(Dm - mD)
    H0 = cdot_nn(P0, v3)
    r0 = jnp.sum(P0, axis=-1, keepdims=True)
    g = b_end - b + li
    mg = lax.stop_gradient(jnp.max(g, axis=1, keepdims=True))
    kw = jnp.exp(g - mg) * k3
    return H0, r0, cdot_tn(kw, v3), jnp.sum(kw, axis=1, keepdims=True), b, b_end, mD, mg


def _ml_inter(q2, mo, gn, H0, r0, b, C_in, n_in, *, mD, m_in, e):
    c, L = N_CHUNKS, ML_CHUNK
    q3 = (q2 * _head_mask(e)).reshape(c, L, 128)
    a = b + m_in
    m_t = lax.stop_gradient(jnp.maximum(a, mD))
    c1 = jnp.exp(mD - m_t)
    c2 = jnp.exp(a - m_t)
    num = c1 * H0 + c2 * cdot_nn(q3, C_in)
    nq = c1 * r0 + c2 * jnp.sum(q3 * n_in, axis=-1, keepdims=True)
    h = num / jnp.maximum(jnp.abs(nq), jnp.exp(-m_t))
    hg = _rms(h) * gn * jax.nn.sigmoid(mo.reshape(c, L, 128))
    return hg.reshape(c * L, 128)


def _state_sweep(U_s, un_s, be_s, mg_s, Cin_s, nin_s, min_s, al_s, bt_s):
    def step(j, carry):
        C, n, m = carry
        Cin_s[j], nin_s[j], min_s[j] = C, n, m
        m_out = jnp.maximum(be_s[j] + m, mg_s[j])
        al = jnp.exp(be_s[j] + m - m_out)
        bt = jnp.exp(mg_s[j] - m_out)
        al_s[j], bt_s[j] = al, bt
        return al * C + bt * U_s[j], al * n + bt * un_s[j], m_out

    lax.fori_loop(0, N_CHUNKS, step, (jnp.zeros((128, 128), F32), jnp.zeros((1, 128), F32), jnp.zeros((1, 1), F32)))


def _state_sweep_bwd(U_s, un_s, dbe_s, Cin_s, nin_s, dCp_s, dnp_s, al_s, bt_s):
    def step(t, carry):
        j = N_CHUNKS - 1 - t
        dC, dn = carry
        al, bt = al_s[j], bt_s[j]
        U_s[j] = bt * dC
        un_s[j] = bt * dn
        dal = jnp.sum(jnp.sum(dC * Cin_s[j], axis=1, keepdims=True), axis=0, keepdims=True) \
            + jnp.sum(dn * nin_s[j], axis=1, keepdims=True)
        dbe_s[j] = dal * al
        return dCp_s[j] + al * dC, dnp_s[j] + al * dn

    lax.fori_loop(0, N_CHUNKS, step, (jnp.zeros((128, 128), F32), jnp.zeros((1, 128), F32)))


def _state_scratch():
    c = N_CHUNKS
    return [pltpu.VMEM((c, 128, 128), F32), pltpu.VMEM((c, 1, 128), F32), pltpu.VMEM((c, 1, 1), F32),
            pltpu.VMEM((c, 1, 1), F32),
            pltpu.VMEM((c, 128, 128), F32), pltpu.VMEM((c, 1, 128), F32), pltpu.VMEM((c, 1, 1), F32),
            pltpu.VMEM((c, 1, 1), F32), pltpu.VMEM((c, 1, 1), F32)]


def _shift_down(x, s):
    if s == 0:
        return x
    rows = lax.broadcasted_iota(jnp.int32, x.shape, 0)
    return jnp.where(rows >= s, pltpu.roll(x, s, 0), 0.0)


def _shift_up(x, s):
    if s == 0:
        return x
    S = x.shape[0]
    rows = lax.broadcasted_iota(jnp.int32, x.shape, 0)
    return jnp.where(rows < S - s, pltpu.roll(x, S - s, 0), 0.0)


def _conv_pre(x, cw, cb):
    y = cb + cw[3:4, :] * x
    for j in range(3):
        y = y + cw[j:j + 1, :] * _shift_down(x, 3 - j)
    return y


def _conv_bwd(x, cw, dpre):
    dx = cw[3:4, :] * dpre
    dcw = [None] * 4
    dcw[3] = jnp.sum(dpre * x, axis=0, keepdims=True)
    for j in range(3):
        dx = dx + cw[j:j + 1, :] * _shift_up(dpre, 3 - j)
        dcw[j] = jnp.sum(dpre * _shift_down(x, 3 - j), axis=0, keepdims=True)
    return dx, dcw, jnp.sum(dpre, axis=0, keepdims=True)


def _silu(z):
    return z * jax.nn.sigmoid(z)


def _dsilu(z):
    s = jax.nn.sigmoid(z)
    return s * (1.0 + z * (1.0 - s))


_ML_Q, _ML_K = slice(0, 128), slice(128, 256)
_ML_IF = slice(768, 896)


def _ml_v(e):
    return slice(256 + e * 128, 384 + e * 128)


def _ml_o(e):
    return slice(512 + e * 128, 640 + e * 128)


def _ml_specs():
    pair = lambda b, p: (b, 0, p)
    return [pl.BlockSpec((1, SEQ, ML_PAIR_COLS), pair),
            pl.BlockSpec((1, 4, 128), lambda b, p: (p, 0, 0)),
            pl.BlockSpec((1, 4, 128), lambda b, p: (4 + p, 0, 0)),
            pl.BlockSpec((1, 1, 128), lambda b, p: (p, 0, 0)),
            pl.BlockSpec((1, 1, 128), lambda b, p: (4 + p, 0, 0)),
            pl.BlockSpec((1, 1, 128), lambda b, p: (p, 0, 0)),
            pl.BlockSpec((1, 1, 256), lambda b, p: (p, 0, 0))]


def mlstm_fwd(pm, cw8, cb8, bifp, gn4, name):
    B = pm.shape[0]

    def body(p_ref, cwq, cwk, cbq, cbk, bif_ref, gn_ref, hg_ref, *st):
        U_s, un_s, be_s, mg_s, Cin_s, nin_s, min_s, al_s, bt_s = st
        qc = _silu(_conv_pre(p_ref[0, :, _ML_Q], cwq[0], cbq[0]))
        kc = _silu(_conv_pre(p_ref[0, :, _ML_K], cwk[0], cbk[0])) * (64 ** -0.5)
        ifb = p_ref[0, :, _ML_IF] + bif_ref[0]
        for e in range(2):
            lanes = slice(e * 128, (e + 1) * 128)
            H0, r0, U, un, b, b_end, mD, mg = _ml_intra(qc, kc, p_ref[0, :, _ml_v(e)], ifb, e=e)
            U_s[...], un_s[...], be_s[...], mg_s[...] = U, un, b_end, mg
            _state_sweep(*st)
            hg = _ml_inter(qc, p_ref[0, :, _ml_o(e)], gn_ref[0, :, lanes], H0, r0, b, Cin_s[...], nin_s[...],
                           mD=mD, m_in=min_s[...], e=e)
            hg_ref[0, :, lanes] = hg.astype(BF16)

    return pl.pallas_call(
        body, name=name, grid=(B, ML_PAIRS),
        in_specs=_ml_specs(),
        out_specs=pl.BlockSpec((1, SEQ, 256), lambda b, p: (b, 0, p)),
        out_shape=jax.ShapeDtypeStruct((B, SEQ, D_MODEL), BF16),
        scratch_shapes=_state_scratch(),
        compiler_params=_cparams(),
    )(pm, cw8, cw8, cb8, cb8, bifp, gn4)


def mlstm_bwd(pm, cw8, cb8, bifp, gn4, dhg, name):
    B = pm.shape[0]

    def body(p_ref, cwq, cwk, cbq, cbk, bif_ref, gn_ref, dh_ref,
             dp_ref, dcw_ref, dcb_ref, dbif_ref, dgn_ref, *scr):
        st = scr[:9]
        U_s, un_s, be_s, mg_s, Cin_s, nin_s, min_s, al_s, bt_s = st
        dCp_s, dnp_s, dbe_s = scr[9:]
        p_id = pl.program_id(1)

        @pl.when(jnp.logical_and(pl.program_id(0) == 0, p_id == 0))
        def _():
            dcw_ref[...] = jnp.zeros_like(dcw_ref)
            dcb_ref[...] = jnp.zeros_like(dcb_ref)
            dbif_ref[...] = jnp.zeros_like(dbif_ref)
            dgn_ref[...] = jnp.zeros_like(dgn_ref)

        qc = _silu(_conv_pre(p_ref[0, :, _ML_Q], cwq[0], cbq[0]))
        kc = _silu(_conv_pre(p_ref[0, :, _ML_K], cwk[0], cbk[0])) * (64 ** -0.5)
        ifb = p_ref[0, :, _ML_IF] + bif_ref[0]
        dq = jnp.zeros((SEQ, 128), F32)
        dk = jnp.zeros((SEQ, 128), F32)
        difb = jnp.zeros((SEQ, 128), F32)
        for e in range(2):
            lanes = slice(e * 128, (e + 1) * 128)
            (H0, r0, U, un, b, b_end, mD, mg), vjp1 = jax.vjp(functools.partial(_ml_intra, e=e), qc, kc,
                                                              p_ref[0, :, _ml_v(e)], ifb)
            U_s[...], un_s[...], be_s[...], mg_s[...] = U, un, b_end, mg
            _state_sweep(*st)
            _, vjp3 = jax.vjp(functools.partial(_ml_inter, mD=mD, m_in=min_s[...], e=e), qc, p_ref[0, :, _ml_o(e)],
                              gn_ref[0, :, lanes], H0, r0, b, Cin_s[...], nin_s[...])
            dq_a, dmo, dgn, dH0, dr0, db_a, dCp, dnp = vjp3(dh_ref[0, :, lanes])
            dCp_s[...], dnp_s[...] = dCp, dnp
            _state_sweep_bwd(U_s, un_s, dbe_s, Cin_s, nin_s, dCp_s, dnp_s, al_s, bt_s)
            dq_b, dk_b, dv, difb_e = vjp1((dH0, dr0, U_s[...], un_s[...], db_a, dbe_s[...],
                                           jnp.zeros_like(mD), jnp.zeros_like(mg)))
            dq, dk, difb = dq + dq_a + dq_b, dk + dk_b, difb + difb_e
            dp_ref[0, :, _ml_v(e)] = dv.astype(BF16)
            dp_ref[0, :, _ml_o(e)] = dmo.astype(BF16)
            dgn_ref[p_id, :, lanes] += dgn
        dp_ref[0, :, _ML_IF] = difb.astype(BF16)
        dbif_ref[p_id] += jnp.sum(difb, axis=0, keepdims=True)

        for (sl, cw, cb, d, blk, scale) in ((_ML_Q, cwq, cbq, dq, p_id, 1.0), (_ML_K, cwk, cbk, dk, 4 + p_id, 64 ** -0.5)):
            xr = p_ref[0, :, sl]
            dpre = d * scale * _dsilu(_conv_pre(xr, cw[0], cb[0]))
            dx, dcw, dcb = _conv_bwd(xr, cw[0], dpre)
            dp_ref[0, :, sl] = dx.astype(BF16)
            for j in range(4):
                dcw_ref[blk, j:j + 1, :] += dcw[j]
            dcb_ref[blk] += dcb

    full3 = lambda b, p: (0, 0, 0)
    return pl.pallas_call(
        body, name=name, grid=(B, ML_PAIRS),
        in_specs=[pl.BlockSpec((1, SEQ, ML_PAIR_COLS), lambda b, p: (b, 0, p), pipeline_mode=pl.Buffered(1))]
        + _ml_specs()[1:] + [pl.BlockSpec((1, SEQ, 256), lambda b, p: (b, 0, p), pipeline_mode=pl.Buffered(1))],
        out_specs=[pl.BlockSpec((1, SEQ, ML_PAIR_COLS), lambda b, p: (b, 0, p)),
                   pl.BlockSpec((8, 4, 128), full3), pl.BlockSpec((8, 1, 128), full3),
                   pl.BlockSpec((4, 1, 128), full3), pl.BlockSpec((4, 1, 256), full3)],
        out_shape=[jax.ShapeDtypeStruct((B, SEQ, ML_COLS), BF16),
                   jax.ShapeDtypeStruct((8, 4, 128), F32), jax.ShapeDtypeStruct((8, 1, 128), F32),
                   jax.ShapeDtypeStruct((4, 1, 128), F32), jax.ShapeDtypeStruct((4, 1, 256), F32)],
        scratch_shapes=_state_scratch() + [pltpu.VMEM((N_CHUNKS, 128, 128), F32), pltpu.VMEM((N_CHUNKS, 1, 128), F32),
                                           pltpu.VMEM((N_CHUNKS, 1, 1), F32)],
        compiler_params=_cparams(),
    )(pm, cw8, cw8, cb8, cb8, bifp, gn4, dhg)


def _adamw(w, g, m, v):
    m = ADAM_B1 * m + (1.0 - ADAM_B1) * g
    v = ADAM_B2 * v + (1.0 - ADAM_B2) * (g * g)
    m_hat = m / (1.0 - ADAM_B1 ** ADAM_STEP)
    v_hat = v / (1.0 - ADAM_B2 ** ADAM_STEP)
    delta = -ADAM_LR * (m_hat / (jnp.sqrt(v_hat) + ADAM_EPS) + ADAM_WD * w)
    return delta, m, v


def adamw(w, g, m, v, name, parts=False):
    R, C = w.shape
    tr = _pick(R, (256, 128, 64, 32, 16, 8, 4, 2, 1)) if R * C * 4 > (1 << 20) else R
    spec = pl.BlockSpec((tr, C), lambda i: (i, 0))
    g_spec = pl.BlockSpec((N_DEV, tr, C), lambda i: (0, i, 0)) if parts else spec

    def body(w_ref, g_ref, m_ref, v_ref, go_ref, d_ref, mo_ref, vo_ref):
        if parts:
            g = g_ref[0].astype(F32)
            for k in range(1, N_DEV):
                g = g + g_ref[k].astype(F32)
        else:
            g = g_ref[...]
        d, mn, vn = _adamw(w_ref[...], g, m_ref[...], v_ref[...])
        go_ref[...], d_ref[...], mo_ref[...], vo_ref[...] = g, d, mn, vn

    return pl.pallas_call(
        body, name=name, grid=(R // tr,),
        in_specs=[spec, g_spec, spec, spec], out_specs=[spec] * 4,
        out_shape=[jax.ShapeDtypeStruct((R, C), F32)] * 4,
        compiler_params=_cparams(),
    )(w, g, m, v)


def _mesh_pos():
    return lax.axis_index("x"), lax.axis_index("y"), lax.axis_index("c")


def _flip(pos, f):
    x, y, c = pos
    return (1 - x if f & 4 else x, 1 - y if f & 2 else y, 1 - c if f & 1 else c)


def _index(pos):
    return 4 * pos[0] + 2 * pos[1] + pos[2]


def _exchange(arrs, name, scatter):
    n = len(arrs)

    def body(*refs):
        ins, outs = refs[:n], refs[n:2 * n]
        send, recv, lsem = refs[2 * n:]
        me = _mesh_pos()
        mine = _index(me)
        copies = []
        for i in range(n):
            src = ins[i].at[mine] if scatter else ins[i]
            loc = pltpu.make_async_copy(src, outs[i].at[mine], lsem.at[i])
            loc.start()
            copies.append(loc)
            for f in range(1, N_DEV):
                peer = _flip(me, f)
                src = ins[i].at[_index(peer)] if scatter else ins[i]
                cp = pltpu.make_async_remote_copy(
                    src_ref=src, dst_ref=outs[i].at[mine],
                    send_sem=send.at[i * 7 + f - 1], recv_sem=recv.at[i * 7 + f - 1],
                    device_id=peer, device_id_type=pl.DeviceIdType.MESH)
                cp.start()
                copies.append(cp)
        for cp in copies:
            cp.wait()

    any_spec = pl.BlockSpec(memory_space=pl.ANY)
    out_shape = [jax.ShapeDtypeStruct(a.shape if scatter else (N_DEV,) + a.shape, a.dtype) for a in arrs]
    res = pl.pallas_call(
        body, name=name,
        in_specs=[any_spec] * n, out_specs=[any_spec] * n, out_shape=out_shape,
        scratch_shapes=[pltpu.SemaphoreType.DMA((7 * n,)), pltpu.SemaphoreType.DMA((7 * n,)),
                        pltpu.SemaphoreType.DMA((n,))],
        compiler_params=_cparams(),
    )(*arrs)
    return list(res)


def all_gather(arrs, name):
    return _exchange(arrs, name, False)


def all_gather_two_level(arrs, name):
    n = len(arrs)

    def body(*refs):
        ins, outs = refs[:n], refs[n:2 * n]
        send, recv, lsem = refs[2 * n:]
        x, y, c = _mesh_pos()
        me, sibling = (x, y, c), (x, y, 1 - c)
        chips = [(1 - x, y), (x, 1 - y), (1 - x, 1 - y)]

        def copy(i, k, block, to, src=None):
            rows = outs[i].at[_index(block)]
            return pltpu.make_async_remote_copy(
                src_ref=rows if src is None else src, dst_ref=rows,
                send_sem=send.at[i * 7 + k], recv_sem=recv.at[i * 7 + k],
                device_id=to, device_id_type=pl.DeviceIdType.MESH)

        local = [pltpu.make_async_copy(ins[i], outs[i].at[_index(me)], lsem.at[i]) for i in range(n)]
        first = [copy(i, 0, me, sibling, src=ins[i]) for i in range(n)]
        first += [copy(i, 1 + j, me, (*chip, c), src=ins[i]) for i in range(n) for j, chip in enumerate(chips)]
        for cp in local + first:
            cp.start()
        passed = []
        for j, chip in enumerate(chips):
            for i in range(n):
                copy(i, 1 + j, (*chip, c), me).wait_recv()
                cp = copy(i, 4 + j, (*chip, c), sibling)
                cp.start()
                passed.append(cp)
        for i in range(n):
            copy(i, 0, sibling, me).wait_recv()
            for j, chip in enumerate(chips):
                copy(i, 4 + j, (*chip, 1 - c), me).wait_recv()
        for cp in first + passed:
            cp.wait_send()
        for cp in local:
            cp.wait()

    any_spec = pl.BlockSpec(memory_space=pl.ANY)
    res = pl.pallas_call(
        body, name=name,
        in_specs=[any_spec] * n, out_specs=[any_spec] * n,
        out_shape=[jax.ShapeDtypeStruct((N_DEV,) + a.shape, a.dtype) for a in arrs],
        scratch_shapes=[pltpu.SemaphoreType.DMA((7 * n,)), pltpu.SemaphoreType.DMA((7 * n,)),
                        pltpu.SemaphoreType.DMA((n,))],
        compiler_params=_cparams(),
    )(*arrs)
    return list(res)


def all_to_all(arrs, name):
    return _exchange(arrs, name, True)


_HBM = pl.BlockSpec(memory_space=pltpu.HBM)
_SEM = pl.BlockSpec(memory_space=pltpu.SEMAPHORE)
_EFFECT = pltpu.SideEffectType.DATAFLOW_SIDE_EFFECTING


def _split_copies(ins, lands, send, recv, scatter, waiting):
    me = _mesh_pos()
    mine = _index(me)
    copies = []
    for i in range(len(ins)):
        for f in range(1, N_DEV):
            peer = _flip(me, f)
            src = ins[i].at[_index(peer)] if scatter else ins[i]
            copies.append(pltpu.make_async_remote_copy(
                src_ref=src, dst_ref=lands[i].at[_index(peer) if waiting else mine],
                send_sem=send.at[i * 7 + f - 1], recv_sem=recv.at[i * 7 + f - 1],
                device_id=peer, device_id_type=pl.DeviceIdType.MESH))
    return copies


def exchange_start(arrs, name, scatter, after=()):
    n = len(arrs)
    land_shapes = [a.shape if scatter else (N_DEV,) + a.shape for a in arrs]

    def body(*refs):
        ins, lands = refs[:n], refs[n:2 * n]
        send, recv = refs[2 * n + len(after)], refs[2 * n + len(after) + 1]
        token = refs[-1]
        for cp in _split_copies(ins, lands, send, recv, scatter, False):
            cp.start()
        token[...] = jnp.zeros_like(token)

    res = pl.pallas_call(
        body, name=name,
        out_shape=(pltpu.SemaphoreType.DMA((7 * n,)), pltpu.SemaphoreType.DMA((7 * n,)),
                   *[pltpu.HBM(a.shape, a.dtype) for a in arrs],
                   *[pltpu.HBM(s, a.dtype) for s, a in zip(land_shapes, arrs)],
                   jax.ShapeDtypeStruct((8, 128), F32)),
        in_specs=[_HBM] * (2 * n) + [pl.BlockSpec(memory_space=pl.ANY)] * len(after),
        out_specs=(_SEM, _SEM, *[_HBM] * (2 * n), pl.BlockSpec(memory_space=pltpu.VMEM)),
        input_output_aliases={i: 2 + i for i in range(2 * n)},
        compiler_params=pltpu.CompilerParams(has_side_effects=_EFFECT),
    )(*[pltpu.with_memory_space_constraint(a, pltpu.HBM) for a in arrs],
      *[pltpu.with_memory_space_constraint(lax.empty(s, a.dtype), pltpu.HBM) for s, a in zip(land_shapes, arrs)],
      *after)
    return (res[0], res[1], list(res[2:2 + n]), list(res[2 + n:2 + 2 * n])), res[-1]


def exchange_wait(handle, after, name, scatter):
    send, recv, srcs, lands = handle
    n = len(srcs)

    def body(*refs):
        ins, lnd = refs[:n], refs[n:2 * n]
        send_, recv_ = refs[2 * n], refs[2 * n + 1]
        for cp in _split_copies(ins, lnd, send_, recv_, scatter, True):
            cp.wait_send()
            cp.wait_recv()

    res = pl.pallas_call(
        body, name=name,
        out_shape=(*[pltpu.HBM(a.shape, a.dtype) for a in srcs], *[pltpu.HBM(a.shape, a.dtype) for a in lands]),
        in_specs=[_HBM] * (2 * n) + [_SEM, _SEM, pl.BlockSpec(memory_space=pl.ANY)],
        out_specs=tuple([_HBM] * (2 * n)),
        input_output_aliases={i: i for i in range(2 * n)},
        compiler_params=pltpu.CompilerParams(has_side_effects=_EFFECT),
    )(*srcs, *lands, send, recv, after)
    return list(res[n:])


def _own_slot(land, own):
    return lax.dynamic_update_slice(land, own[None], (_index(_mesh_pos()),) + (0,) * own.ndim)


def cast_bf16(arrs, name):
    outs = []
    for i, a in enumerate(arrs):
        R, C = a.shape
        tr = _pick(R, (256, 128, 64, 32, 16, 8)) if R * C * 4 > (1 << 21) else R
        spec = pl.BlockSpec((tr, C), lambda i: (i, 0))

        def body(a_ref, o_ref):
            o_ref[...] = a_ref[...].astype(BF16)

        outs.append(pl.pallas_call(body, name=f"{name}_{i}", grid=(R // tr,), in_specs=[spec], out_specs=spec,
                                   out_shape=jax.ShapeDtypeStruct((R, C), BF16), compiler_params=_cparams())(a))
    return outs


def sum_parts(parts, name):
    def fn(p):
        g = p[0]
        for k in range(1, N_DEV):
            g = g + p[k]
        return (g,)
    return small_call(fn, [parts], [jax.ShapeDtypeStruct(parts.shape[1:], F32)], name)[0]


_SPLITS = np.cumsum([1536, 1536, 1536, 512, 512, 1024, 1024, 8, 8, 2048])[:-1].tolist()


def split_w_in(w):
    aq, ak, av, mq, mk, mv, mo, mi, mf, gates = jnp.split(w, _SPLITS, axis=1)
    R = w.shape[0]
    w_att = jnp.stack([aq.reshape(R, 12, 128), ak.reshape(R, 12, 128), av.reshape(R, 12, 128)], axis=2)
    gif = jnp.concatenate([mi.reshape(R, 4, 2), mf.reshape(R, 4, 2), jnp.zeros((R, 4, 124), w.dtype)], axis=2)
    w_ml = jnp.concatenate([mq.reshape(R, 4, 128), mk.reshape(R, 4, 128), mv.reshape(R, 4, 256),
                            mo.reshape(R, 4, 256), gif], axis=2)
    return w_att.reshape(R, ATT_COLS), w_ml.reshape(R, ML_COLS), gates


def merge_w_in(g_att, g_ml, g_gate):
    R = g_att.shape[0]
    a = g_att.reshape(R, 12, 3, 128)
    m = g_ml.reshape(R, 4, ML_PAIR_COLS)
    gif = m[:, :, 768:772]
    return jnp.concatenate([
        a[:, :, 0].reshape(R, 1536), a[:, :, 1].reshape(R, 1536), a[:, :, 2].reshape(R, 1536),
        m[:, :, 0:128].reshape(R, 512), m[:, :, 128:256].reshape(R, 512),
        m[:, :, 256:512].reshape(R, 1024), m[:, :, 512:768].reshape(R, 1024),
        gif[:, :, 0:2].reshape(R, 8), gif[:, :, 2:4].reshape(R, 8), g_gate], axis=1)


def _blk8(v, width=128):
    r = v.shape[0]
    return v.reshape(r, 1024 // width, width).transpose(1, 0, 2)


def _unblk8(v):
    nb, r, w = v.shape
    return v.transpose(1, 0, 2).reshape(r, nb * w)


def local_step(x, target, mods, w, small, late_w=None, early_g=None, w_in_g=None):
    late_w = late_w or (lambda after: w)
    big = {}
    early_g = early_g or (lambda g: big.update(g))
    w_in_g = w_in_g or (lambda g: big.update(w_in=g))
    B = x.shape[0]
    T = B * SEQ
    shift1, scale1, gate1, shift2, scale2, gate2 = mods
    f2 = lambda a: a.reshape(T, a.shape[-1])
    f3 = lambda a: a.reshape(B, SEQ, a.shape[-1])

    rel_t = jnp.pad(small["rel_bias"].T, ((0, 4), (0, 0)))
    onehots = [_bucket_onehot(d) for _, d in ATT_GROUPS]
    biases = [bias_expand(rel_t, oh, f"bias_expand{g}").reshape(16, ATT_BLOCK, 2 * ATT_BLOCK)
              for g, oh in enumerate(onehots)]
    qg, kg = small["q_norm_g"], small["k_norm_g"]
    cw8 = _blk8(small["conv_w"])
    cb8 = _blk8(small["conv_b"])
    b_if = small["b_if"].reshape(2, 4, 2)
    bifp = jnp.concatenate([b_if[0], b_if[1], jnp.zeros((4, 124), F32)], axis=1).reshape(4, 1, 128)
    gn4 = small["mlstm_norm_g"].reshape(4, 1, 256)

    u = modnorm_fwd(x, small["norm1_g"], scale1, shift1, "modnorm1")
    u2d = f2(u)
    pa = f3(matmul(u2d, w["w_att"], mode="nn", name="proj_att"))
    pm = f3(matmul(u2d, w["w_ml"], mode="nn", name="proj_ml"))
    pg = f3(matmul(u2d, w["w_gate"], mode="nn", name="proj_gate"))
    os_, ls_ = [], []
    for g in range(3):
        o, l = attn_fwd(pa, biases[g], qg, kg, g, f"attn_fwd{g}")
        os_.append(o)
        ls_.append(l)
    att = merge_fwd(os_, ls_, "merge_fwd")
    hg = mlstm_fwd(pm, cw8, cb8, bifp, gn4, "mlstm_fwd")
    w = {**w, **late_w(hg)}
    y_att = matmul(f2(att), w["w_att_out"], mode="nn", name="att_out")
    y_ml = matmul(f2(hg), w["w_ml_out"], mode="nn", name="ml_out")
    z = gate_fwd(pg, f3(y_att), f3(y_ml), "gate_fwd")
    y = matmul(f2(z), w["w_out"], mode="nn", name="out_proj")
    x1, u2 = resid_modnorm_fwd(x, f3(y), gate1, small["norm2_g"], scale2, shift2, "resid_modnorm2")
    pre, hdn = matmul(f2(u2), w["w_ff1"], mode="nn", name="ff1", out_dtypes=(F32, BF16),
                      epi=lambda acc: (acc, jnp.square(jnp.maximum(acc, 0.0))))
    ffo = matmul(hdn, w["w_ff2"], mode="nn", name="ff2")
    dx2, loss = resid_loss(x1, f3(ffo), gate2, target, "resid_loss")

    d_ffo, d_gate2 = resid_bwd(dx2, f3(ffo), gate2, "resid_bwd2")
    g_ff2 = matmul(hdn, f2(d_ffo), mode="tn", name="g_ff2", out_dtypes=(BF16,))
    d_pre = matmul(f2(d_ffo), w["w_ff2"], mode="nt", name="d_hdn", out_dtypes=(BF16,), extras=(pre,),
                   epi=lambda acc, p: (acc * (2.0 * jnp.maximum(p, 0.0)),))
    g_ff1 = matmul(f2(u2), d_pre, mode="tn", name="g_ff1", out_dtypes=(BF16,))
    du2 = matmul(d_pre, w["w_ff1"], mode="nt", name="d_u2")
    dx1, d_norm2, d_scale2, d_shift2 = modnorm_bwd(x1, small["norm2_g"], scale2, shift2, f3(du2), dx2, "modnorm_bwd2")
    dy, d_gate1 = resid_bwd(dx1, f3(y), gate1, "resid_bwd1")
    g_out = matmul(f2(z), f2(dy), mode="tn", name="g_out", out_dtypes=(BF16,))
    dz = matmul(f2(dy), w["w_out"], mode="nt", name="d_z")
    dpg, d_ya, d_ym = gate_bwd(pg, f3(y_att), f3(y_ml), f3(dz), "gate_bwd")
    g_att_out = matmul(f2(att), f2(d_ya), mode="tn", name="g_att_out", out_dtypes=(BF16,))
    d_att = matmul(f2(d_ya), w["w_att_out"], mode="nt", name="d_att")
    g_ml_out = matmul(f2(hg), f2(d_ym), mode="tn", name="g_ml_out", out_dtypes=(BF16,))
    d_hg = matmul(f2(d_ym), w["w_ml_out"], mode="nt", name="d_hg")
    started = early_g(dict(w_att_out=g_att_out, w_ml_out=g_ml_out, w_out=g_out, w_ff1=g_ff1, w_ff2=g_ff2))
    order = 0.0 if started is None else started[0, 0]
    dmerge = merge_bwd(os_, ls_, f3(d_att), "merge_bwd")
    dpa = lax.empty((B, SEQ, ATT_COLS), BF16)
    d_rel = []
    d_qg = d_kg = None
    for g in range(3):
        dpa, dbias, dq_g, dk_g = attn_bwd(pa, biases[g], qg + order, kg, dmerge[g], dmerge[3 + g], dpa, g,
                                          f"attn_bwd{g}")
        db8 = jnp.pad(dbias.reshape(4, -1), ((0, 4), (0, 0)))
        d_rel.append(bias_reduce(db8, onehots[g], f"bias_reduce{g}")[:4])
        d_qg = dq_g if d_qg is None else d_qg + dq_g
        d_kg = dk_g if d_kg is None else d_kg + dk_g
    dpm, dcw8, dcb8, dbifp, dgn4 = mlstm_bwd(pm, cw8, cb8, bifp, gn4 + order, f3(d_hg), "mlstm_bwd")
    g_w_att = matmul(u2d, f2(dpa), mode="tn", name="g_w_att", out_dtypes=(BF16,))
    g_w_ml = matmul(u2d, f2(dpm), mode="tn", name="g_w_ml", out_dtypes=(BF16,))
    g_w_gate = matmul(u2d, f2(dpg), mode="tn", name="g_w_gate", out_dtypes=(BF16,))
    started = w_in_g(merge_w_in(g_w_att, g_w_ml, g_w_gate))
    du = matmul(f2(dpa), w["w_att"], mode="nt", name="d_u_att", after=() if started is None else (started,))
    du = matmul(f2(dpm), w["w_ml"], mode="nt", name="d_u_ml", extras=(du,), epi=lambda acc, e: (acc + e,))
    du = matmul(f2(dpg), w["w_gate"], mode="nt", name="d_u_gate", extras=(du,), epi=lambda acc, e: (acc + e,))
    grad_x, d_norm1, d_scale1, d_shift1 = modnorm_bwd(x, small["norm1_g"], scale1, shift1, f3(du), dx1, "modnorm_bwd1")

    d_mods = (d_shift1, d_scale1, d_gate1, d_shift2, d_scale2, d_gate2)
    dbif = dbifp.reshape(4, 128)
    small_g = dict(
        norm1_g=d_norm1, norm2_g=d_norm2,
        b_if=jnp.stack([dbif[:, 0:2].reshape(8), dbif[:, 2:4].reshape(8)]),
        conv_w=_unblk8(dcw8), conv_b=_unblk8(dcb8), q_norm_g=d_qg, k_norm_g=d_kg,
        rel_bias=jnp.concatenate(d_rel, axis=0).T,
        mlstm_norm_g=dgn4.reshape(1, 1024))
    return loss, grad_x, d_mods, big, small_g


_SMALL = (("b_ada", 6144), ("norm1_g", 1024), ("norm2_g", 1024), ("b_if", 16), ("conv_b", 1024),
          ("q_norm_g", 128), ("k_norm_g", 128), ("rel_bias", 384), ("mlstm_norm_g", 1024), ("conv_w", 4096))
_SMALL_ROWS = 120
_REPL = _SMALL[:-1]
_REPL_ROWS = 86


def _pack(d, names, rows):
    flat = jnp.concatenate([d[k].reshape(-1) for k, _ in names])
    return jnp.pad(flat, (0, rows * 128 - flat.shape[0])).reshape(rows, 128)


def _unpack(slab, names, shapes):
    flat = slab.reshape(-1)
    out, off = {}, 0
    for k, nel in names:
        out[k] = flat[off:off + nel].reshape(shapes[k])
        off += nel
    return out


def kernel(x, c, w_ada, b_ada, norm1_g, norm2_g, w_in, b_if, conv_w, conv_b, q_norm_g, k_norm_g, rel_bias, mlstm_norm_g, w_att_out, w_ml_out, w_out, w_ff1, w_ff2, loss_target, m_w_ada, m_b_ada, m_norm1_g, m_norm2_g, m_w_in, m_b_if, m_conv_w, m_conv_b, m_q_norm_g, m_k_norm_g, m_rel_bias, m_mlstm_norm_g, m_w_att_out, m_w_ml_out, m_w_out, m_w_ff1, m_w_ff2, v_w_ada, v_b_ada, v_norm1_g, v_norm2_g, v_w_in, v_b_if, v_conv_w, v_conv_b, v_q_norm_g, v_k_norm_g, v_rel_bias, v_mlstm_norm_g, v_w_att_out, v_w_ml_out, v_w_out, v_w_ff1, v_w_ff2):
    P = dict(w_ada=w_ada, b_ada=b_ada, norm1_g=norm1_g, norm2_g=norm2_g, w_in=w_in, b_if=b_if, conv_w=conv_w,
             conv_b=conv_b, q_norm_g=q_norm_g, k_norm_g=k_norm_g, rel_bias=rel_bias, mlstm_norm_g=mlstm_norm_g,
             w_att_out=w_att_out, w_ml_out=w_ml_out, w_out=w_out, w_ff1=w_ff1, w_ff2=w_ff2)
    M = dict(w_ada=m_w_ada, b_ada=m_b_ada, norm1_g=m_norm1_g, norm2_g=m_norm2_g, w_in=m_w_in, b_if=m_b_if,
             conv_w=m_conv_w, conv_b=m_conv_b, q_norm_g=m_q_norm_g, k_norm_g=m_k_norm_g, rel_bias=m_rel_bias,
             mlstm_norm_g=m_mlstm_norm_g, w_att_out=m_w_att_out, w_ml_out=m_w_ml_out, w_out=m_w_out,
             w_ff1=m_w_ff1, w_ff2=m_w_ff2)
    V = dict(w_ada=v_w_ada, b_ada=v_b_ada, norm1_g=v_norm1_g, norm2_g=v_norm2_g, w_in=v_w_in, b_if=v_b_if,
             conv_w=v_conv_w, conv_b=v_conv_b, q_norm_g=v_q_norm_g, k_norm_g=v_k_norm_g, rel_bias=v_rel_bias,
             mlstm_norm_g=v_mlstm_norm_g, w_att_out=v_w_att_out, w_ml_out=v_w_ml_out, w_out=v_w_out,
             w_ff1=v_w_ff1, w_ff2=v_w_ff2)
    names = list(P)
    shapes = {k: P[k].shape for k in names}
    B = x.shape[0]
    me = _index(_mesh_pos())

    big_names = ("w_in", "w_att_out", "w_ml_out", "w_out", "w_ff1", "w_ff2")
    shards = cast_bf16([P[k][0] for k in big_names], "cast_w")
    w_in_g8, c8, conv_w8 = all_gather_two_level([shards[0], c, conv_w[0]], "gather_w_in")
    c_all = c8.reshape(N_DEV * B, D_MODEL)
    conv_w_full = conv_w8.transpose(1, 0, 2).reshape(4, 1024)
    w_att, w_ml, w_gate = split_w_in(w_in_g8.transpose(1, 0, 2).reshape(D_MODEL, D_IN))
    w = dict(w_att=w_att, w_ml=w_ml, w_gate=w_gate)

    (silu_c,) = small_call(lambda a: (_silu(a),), [c_all], [jax.ShapeDtypeStruct(c_all.shape, F32)], "silu_c")
    b_ada_cols = lax.dynamic_slice(b_ada, (0, me * 768), (1, 768))
    ada_cols = matmul(silu_c, w_ada[0], mode="nn", name="ada", extras=(jnp.broadcast_to(b_ada_cols, (N_DEV * B, 768)),),
                      epi=lambda acc, bb: (acc + bb,))
    (ada_t,) = all_to_all([ada_cols.reshape(N_DEV, B, 768)], "ada_exchange")
    ada = ada_t.transpose(1, 0, 2).reshape(B, 6 * D_MODEL)
    mods = tuple(ada[:, i * D_MODEL:(i + 1) * D_MODEL].reshape(B, 1, D_MODEL) for i in range(6))

    late_handle, late_order = exchange_start(shards[1:], "gather_late_start", False, after=(ada_t,))

    def late_w(after):
        lands = exchange_wait(late_handle, after, "gather_late_wait", False)
        gw = dict(zip(big_names[1:], [_own_slot(l, s) for l, s in zip(lands, shards[1:])]))
        return dict(w_att_out=gw["w_att_out"].transpose(1, 0, 2).reshape(512, D_MODEL),
                    w_ml_out=gw["w_ml_out"].reshape(D_MODEL, D_MODEL), w_out=gw["w_out"].reshape(D_MODEL, D_MODEL),
                    w_ff1=gw["w_ff1"].transpose(1, 0, 2).reshape(D_MODEL, D_FF),
                    w_ff2=gw["w_ff2"].reshape(D_FF, D_MODEL))

    pending = {}

    def send_grads(key, blocks, name):
        handle, order = exchange_start(blocks, name, True)
        pending[key] = (handle, [lax.dynamic_index_in_dim(b, me, 0, keepdims=False) for b in blocks])
        return order

    def early_g(g):
        return send_grads("late", [g["w_att_out"].reshape(512, N_DEV, 128).transpose(1, 0, 2),
                                   g["w_ml_out"].reshape(N_DEV, 128, D_MODEL), g["w_out"].reshape(N_DEV, 128, D_MODEL),
                                   g["w_ff1"].reshape(D_MODEL, N_DEV, 512).transpose(1, 0, 2),
                                   g["w_ff2"].reshape(N_DEV, 512, D_MODEL)], "grad_late_start")

    def w_in_g(g):
        return send_grads("w_in", [g.reshape(D_MODEL, N_DEV, W_IN_SHARD).transpose(1, 0, 2)], "grad_w_in_start")

    def recv_grads(key, after, name):
        handle, own = pending[key]
        return [_own_slot(l, o) for l, o in zip(exchange_wait(handle, after, name, True), own)]

    small = dict(norm1_g=norm1_g + late_order[0, 0], norm2_g=norm2_g, b_if=b_if[0], conv_w=conv_w_full, conv_b=conv_b,
                 q_norm_g=q_norm_g, k_norm_g=k_norm_g, rel_bias=rel_bias, mlstm_norm_g=mlstm_norm_g)
    loss, grad_x, d_mods, _, small_g = local_step(x, loss_target, mods, w, small, late_w, early_g, w_in_g)
    loss = lax.psum(loss[0, 0], ("x", "y", "c"))

    d_ada = jnp.concatenate([d.reshape(B, D_MODEL) for d in d_mods], axis=1)
    (d_ada_t,) = all_to_all([d_ada.reshape(B, N_DEV, 768).transpose(1, 0, 2)], "d_ada_exchange")
    d_ada_cols = d_ada_t.reshape(N_DEV * B, 768)
    g_w_ada = matmul(silu_c, d_ada_cols, mode="tn", name="g_w_ada")
    (g_b_cols,) = small_call(lambda a: (jnp.sum(a, axis=0, keepdims=True),), [d_ada_cols],
                             [jax.ShapeDtypeStruct((1, 768), F32)], "g_b_ada_cols")
    small_g["b_ada"] = lax.dynamic_update_slice(jnp.zeros((1, 6144), F32), g_b_cols, (0, me * 768))

    recv = recv_grads("w_in", grad_x, "grad_w_in_wait") + recv_grads("late", grad_x, "grad_late_wait")
    (small_parts,) = all_gather([_pack(small_g, _SMALL, _SMALL_ROWS)], "small_grad_gather")
    small_sum = sum_parts(small_parts, "small_grad_sum")
    sg = _unpack(small_sum, _SMALL, {**{k: shapes[k] for k, _ in _REPL}, "conv_w": (4, 1024)})

    G, Dl, NM, NV = {}, {}, {}, {}
    for k, parts in zip(big_names, recv):
        g, d, nm, nv = adamw(P[k][0], parts, M[k][0], V[k][0], f"adamw_{k}", parts=True)
        G[k], Dl[k], NM[k], NV[k] = g[None], d[None], nm[None], nv[None]
    g, d, nm, nv = adamw(w_ada[0], g_w_ada, m_w_ada[0], v_w_ada[0], "adamw_w_ada")
    G["w_ada"], Dl["w_ada"], NM["w_ada"], NV["w_ada"] = g[None], d[None], nm[None], nv[None]
    g_conv = lax.dynamic_slice(sg["conv_w"], (0, me * 128), (4, 128))
    g, d, nm, nv = adamw(conv_w[0], g_conv, m_conv_w[0], v_conv_w[0], "adamw_conv_w")
    G["conv_w"], Dl["conv_w"], NM["conv_w"], NV["conv_w"] = g[None], d[None], nm[None], nv[None]
    gslab = _pack(sg, _REPL, _REPL_ROWS)
    _, d, nm, nv = adamw(_pack(P, _REPL, _REPL_ROWS), gslab, _pack(M, _REPL, _REPL_ROWS), _pack(V, _REPL, _REPL_ROWS),
                         "adamw_small")
    rs = {k: shapes[k] for k, _ in _REPL}
    d, nm, nv = _unpack(d, _REPL, rs), _unpack(nm, _REPL, rs), _unpack(nv, _REPL, rs)
    for k, _ in _REPL:
        G[k], Dl[k], NM[k], NV[k] = sg[k], d[k], nm[k], nv[k]

    return (loss, grad_x, *[G[k] for k in names], *[Dl[k] for k in names], *[NM[k] for k in names],
            *[NV[k] for k in names])
```

```python
import functools
import math

import numpy as np
import jax
import jax.numpy as jnp
from jax import lax
from jax.experimental import pallas as pl
from jax.experimental.pallas import tpu as pltpu

F32 = jnp.float32
BF16 = jnp.bfloat16

N_DEV = 8
D_MODEL = 1024
SEQ = 2048
ATT_GROUPS = ((128, 1), (512, 4), (2048, 16))
N_ATT_HEADS = 12
ATT_BLOCK = 128
HEAD_DIM = 128
ML_HEADS = 8
ML_PAIRS = 4
ML_CHUNK = 64
N_CHUNKS = SEQ // ML_CHUNK
N_BUCKETS = 32
MAX_DISTANCE = 2048
D_FF = 4096
D_IN = 9744
EPS = 1e-6

ADAM_LR = 0.001
ADAM_B1 = 0.9
ADAM_B2 = 0.999
ADAM_EPS = 1e-08
ADAM_WD = 0.01
ADAM_STEP = 10

ATT_HEAD_COLS = 3 * HEAD_DIM
ATT_COLS = N_ATT_HEADS * ATT_HEAD_COLS
ML_PAIR_COLS = 896
ML_COLS = ML_PAIRS * ML_PAIR_COLS
GATE_COLS = 2 * D_MODEL
W_IN_SHARD = D_IN // N_DEV

VMEM_LIMIT = 56 * 1024 * 1024


def _cparams(**kw):
    return pltpu.CompilerParams(vmem_limit_bytes=VMEM_LIMIT, **kw)


_NN = ((1,), (0,))
_NT = ((1,), (1,))
_TN = ((0,), (0,))


def _mxu(a, b, dims):
    return lax.dot_general(a.astype(BF16), b.astype(BF16), (dims, ((), ())), preferred_element_type=F32)


@jax.custom_vjp
def bdot_nn(a, b):
    return _mxu(a, b, _NN)


def _nn_fwd(a, b):
    return _mxu(a, b, _NN), (a, b)


def _nn_bwd(res, g):
    a, b = res
    return _mxu(g, b, _NT), _mxu(a, g, _TN)


bdot_nn.defvjp(_nn_fwd, _nn_bwd)


@jax.custom_vjp
def bdot_nt(a, b):
    return _mxu(a, b, _NT)


def _nt_fwd(a, b):
    return _mxu(a, b, _NT), (a, b)


def _nt_bwd(res, g):
    a, b = res
    return _mxu(g, b, _NN), _mxu(g, a, _TN)


bdot_nt.defvjp(_nt_fwd, _nt_bwd)


def _doth(a, b, dims=_NN):
    return lax.dot_general(a, b, (dims, ((), ())), precision=lax.Precision.HIGHEST, preferred_element_type=F32)


def _rms(x):
    return x * lax.rsqrt(jnp.mean(x * x, axis=-1, keepdims=True) + EPS)


def _pick(n, cands):
    for t in cands:
        if n % t == 0:
            return t
    raise ValueError(f"no tile for {n}")


MM_TILE_M = (1024, 512, 256, 128, 64, 32, 16, 8)
MM_TILE_N = (2048, 1792, 1536, 1024, 768, 512, 256, 128)
MM_TILE_K = (1024, 512, 256, 128, 64, 32)

def matmul(a, b, *, mode, name, out_dtypes=(F32,), epi=None, extras=(), after=()):
    if mode == "nn":
        (M, K), (K2, N) = a.shape, b.shape
    elif mode == "nt":
        (M, K), (N, K2) = a.shape, b.shape
    else:
        (K, M), (K2, N) = a.shape, b.shape
    assert K == K2, (a.shape, b.shape, mode)
    tm = _pick(M, MM_TILE_M)
    tn = _pick(N, MM_TILE_N)
    tk = _pick(K, MM_TILE_K)
    nk = K // tk
    n_ex = len(extras)
    n_out = len(out_dtypes)
    dims = {"nn": _NN, "nt": _NT, "tn": _TN}[mode]

    def finish(r, ex_refs, out_refs):
        outs = epi(r, *[e[...] for e in ex_refs]) if epi is not None else (r,)
        for o_ref, o in zip(out_refs, outs):
            o_ref[...] = o.astype(o_ref.dtype)

    def body(*refs):
        a_ref, b_ref = refs[0], refs[1]
        ex_refs = refs[2:2 + n_ex]
        out_refs = refs[2 + n_ex + len(after):2 + n_ex + len(after) + n_out]
        if nk == 1:
            finish(_mxu(a_ref[...], b_ref[...], dims), ex_refs, out_refs)
            return
        acc = refs[2 + n_ex + len(after) + n_out]
        k = pl.program_id(2)

        @pl.when(k == 0)
        def _():
            acc[...] = jnp.zeros_like(acc)

        acc[...] += _mxu(a_ref[...], b_ref[...], dims)

        @pl.when(k == nk - 1)
        def _():
            finish(acc[...], ex_refs, out_refs)

    if mode == "nn":
        a_spec = pl.BlockSpec((tm, tk), lambda i, j, k: (i, k))
        b_spec = pl.BlockSpec((tk, tn), lambda i, j, k: (k, j))
    elif mode == "nt":
        a_spec = pl.BlockSpec((tm, tk), lambda i, j, k: (i, k))
        b_spec = pl.BlockSpec((tn, tk), lambda i, j, k: (j, k))
    else:
        a_spec = pl.BlockSpec((tk, tm), lambda i, j, k: (k, i))
        b_spec = pl.BlockSpec((tk, tn), lambda i, j, k: (k, j))
    o_spec = pl.BlockSpec((tm, tn), lambda i, j, k: (i, j))
    res = pl.pallas_call(
        body,
        name=name,
        grid=(M // tm, N // tn, nk),
        in_specs=[a_spec, b_spec] + [o_spec] * n_ex + [pl.BlockSpec(memory_space=pl.ANY)] * len(after),
        out_specs=[o_spec] * n_out,
        out_shape=[jax.ShapeDtypeStruct((M, N), dt) for dt in out_dtypes],
        scratch_shapes=[pltpu.VMEM((tm, tn), F32)] if nk > 1 else [],
        compiler_params=_cparams(),
    )(a, b, *extras, *after)
    return res[0] if n_out == 1 else tuple(res)


def small_call(fn, inputs, out_shapes, name):
    n_in = len(inputs)

    def body(*refs):
        outs = fn(*[r[...] for r in refs[:n_in]])
        for o_ref, o in zip(refs[n_in:], outs):
            o_ref[...] = o.astype(o_ref.dtype)

    res = pl.pallas_call(body, name=name, out_shape=list(out_shapes), compiler_params=_cparams())(*inputs)
    return tuple(res)


ROW_TILE = 512


def _modnorm(x, g, scale, shift):
    return _rms(x) * g * (1.0 + scale) + shift


def _row_spec(width):
    return pl.BlockSpec((1, ROW_TILE, width), lambda b, i: (b, i, 0))


def _mod_spec():
    return pl.BlockSpec((1, 1, D_MODEL), lambda b, i: (b, 0, 0))


def _vec_spec():
    return pl.BlockSpec((1, D_MODEL), lambda b, i: (0, 0))


def modnorm_fwd(x, g, scale, shift, name):
    B, S, D = x.shape

    def body(x_ref, g_ref, sc_ref, sh_ref, u_ref):
        u_ref[0] = _modnorm(x_ref[0], g_ref[...], sc_ref[0], sh_ref[0]).astype(BF16)

    return pl.pallas_call(
        body, name=name, grid=(B, S // ROW_TILE),
        in_specs=[_row_spec(D), _vec_spec(), _mod_spec(), _mod_spec()],
        out_specs=_row_spec(D),
        out_shape=jax.ShapeDtypeStruct((B, S, D), BF16),
        compiler_params=_cparams(),
    )(x, g, scale, shift)


def resid_modnorm_fwd(x, y, gate, g, scale, shift, name):
    B, S, D = x.shape

    def body(x_ref, y_ref, gt_ref, g_ref, sc_ref, sh_ref, x1_ref, u_ref):
        x1 = x_ref[0] + gt_ref[0] * y_ref[0]
        x1_ref[0] = x1
        u_ref[0] = _modnorm(x1, g_ref[...], sc_ref[0], sh_ref[0]).astype(BF16)

    return pl.pallas_call(
        body, name=name, grid=(B, S // ROW_TILE),
        in_specs=[_row_spec(D), _row_spec(D), _mod_spec(), _vec_spec(), _mod_spec(), _mod_spec()],
        out_specs=[_row_spec(D), _row_spec(D)],
        out_shape=[jax.ShapeDtypeStruct((B, S, D), F32), jax.ShapeDtypeStruct((B, S, D), BF16)],
        compiler_params=_cparams(),
    )(x, y, gate, g, scale, shift)


def resid_loss(x1, ffo, gate, target, name):
    B, S, D = x1.shape

    def body(x_ref, f_ref, gt_ref, t_ref, dx_ref, loss_ref):
        first = jnp.logical_and(pl.program_id(0) == 0, pl.program_id(1) == 0)

        @pl.when(first)
        def _():
            loss_ref[...] = jnp.zeros_like(loss_ref)

        err = x_ref[0] + gt_ref[0] * f_ref[0] - t_ref[0]
        dx_ref[0] = err * (1.0 / D)
        loss_ref[...] += 0.5 * jnp.sum(jnp.mean(err * err, axis=-1, keepdims=True), axis=0, keepdims=True)

    return pl.pallas_call(
        body, name=name, grid=(B, S // ROW_TILE),
        in_specs=[_row_spec(D), _row_spec(D), _mod_spec(), _row_spec(D)],
        out_specs=[_row_spec(D), pl.BlockSpec((1, 1), lambda b, i: (0, 0))],
        out_shape=[jax.ShapeDtypeStruct((B, S, D), F32), jax.ShapeDtypeStruct((1, 1), F32)],
        compiler_params=_cparams(),
    )(x1, ffo, gate, target)


def resid_bwd(dx, y, gate, name):
    B, S, D = dx.shape

    def body(dx_ref, y_ref, gt_ref, dy_ref, dg_ref):
        @pl.when(pl.program_id(1) == 0)
        def _():
            dg_ref[...] = jnp.zeros_like(dg_ref)

        d = dx_ref[0]
        dy_ref[0] = (gt_ref[0] * d).astype(BF16)
        dg_ref[0] += jnp.sum(d * y_ref[0], axis=0, keepdims=True)

    return pl.pallas_call(
        body, name=name, grid=(B, S // ROW_TILE),
        in_specs=[_row_spec(D), _row_spec(D), _mod_spec()],
        out_specs=[_row_spec(D), _mod_spec()],
        out_shape=[jax.ShapeDtypeStruct((B, S, D), BF16), jax.ShapeDtypeStruct((B, 1, D), F32)],
        compiler_params=_cparams(),
    )(dx, y, gate)


def modnorm_bwd(x, g, scale, shift, du, dx_res, name):
    B, S, D = x.shape

    def body(x_ref, g_ref, sc_ref, sh_ref, du_ref, dr_ref, dx_ref, dg_ref, dsc_ref, dsh_ref):
        first = jnp.logical_and(pl.program_id(0) == 0, pl.program_id(1) == 0)

        @pl.when(first)
        def _():
            dg_ref[...] = jnp.zeros_like(dg_ref)

        @pl.when(pl.program_id(1) == 0)
        def _():
            dsc_ref[...] = jnp.zeros_like(dsc_ref)
            dsh_ref[...] = jnp.zeros_like(dsh_ref)

        _, vjp = jax.vjp(_modnorm, x_ref[0], g_ref[...], sc_ref[0], sh_ref[0])
        dx, dg, dsc, dsh = vjp(du_ref[0].astype(F32))
        dx_ref[0] = dx + dr_ref[0]
        dg_ref[...] += dg
        dsc_ref[0] += dsc
        dsh_ref[0] += dsh

    return pl.pallas_call(
        body, name=name, grid=(B, S // ROW_TILE),
        in_specs=[_row_spec(D), _vec_spec(), _mod_spec(), _mod_spec(), _row_spec(D), _row_spec(D)],
        out_specs=[_row_spec(D), _vec_spec(), _mod_spec(), _mod_spec()],
        out_shape=[jax.ShapeDtypeStruct((B, S, D), F32), jax.ShapeDtypeStruct((1, D), F32),
                   jax.ShapeDtypeStruct((B, 1, D), F32), jax.ShapeDtypeStruct((B, 1, D), F32)],
        compiler_params=_cparams(),
    )(x, g, scale, shift, du, dx_res)


def _bucket_table(dilation):
    i = np.arange(ATT_BLOCK)[:, None]
    j = np.arange(2 * ATT_BLOCK)[None, :]
    delta = ATT_BLOCK + i - j
    dist = np.maximum(delta, 0) * dilation
    max_exact = N_BUCKETS // 2
    d = np.maximum(dist, max_exact).astype(np.float32)
    large = max_exact + (np.log(d / np.float32(max_exact)) / np.float32(math.log(MAX_DISTANCE / max_exact))
                         * np.float32(N_BUCKETS - max_exact)).astype(np.int32)
    large = np.minimum(large, N_BUCKETS - 1)
    return np.where(dist < max_exact, dist, large).astype(np.int32)


def _bucket_onehot(dilation):
    bt = jnp.asarray(_bucket_table(dilation).reshape(1, -1))
    return (bt == jnp.arange(N_BUCKETS, dtype=jnp.int32)[:, None]).astype(F32)


def bias_expand(rel_t, onehot, name):
    def fn(r, oh):
        return (_doth(r, oh),)
    return small_call(fn, [rel_t, onehot], [jax.ShapeDtypeStruct((rel_t.shape[0], onehot.shape[1]), F32)], name)[0]


def bias_reduce(dbias_flat, onehot, name):
    def fn(db, oh):
        return (_doth(db, oh, _NT),)
    return small_call(fn, [dbias_flat, onehot], [jax.ShapeDtypeStruct((dbias_flat.shape[0], N_BUCKETS), F32)], name)[0]


def _qk_norm(x, g):
    return _rms(x) * g


def _masked_bias(bias):
    i = lax.broadcasted_iota(jnp.int32, (ATT_BLOCK, 2 * ATT_BLOCK), 0)
    j = lax.broadcasted_iota(jnp.int32, (ATT_BLOCK, 2 * ATT_BLOCK), 1)
    bm = jnp.where(jnp.logical_and(j >= i, j <= i + ATT_BLOCK), bias, -jnp.inf)
    return bm, bm[:, ATT_BLOCK:]


def _attn_tile(qn, kn, v, bias):
    s = bdot_nt(qn, kn) * (HEAD_DIM ** -0.5) + bias
    m = lax.stop_gradient(jnp.max(s, axis=-1, keepdims=True))
    p = jnp.exp(s - m)
    l = jnp.sum(p, axis=-1, keepdims=True)
    o = bdot_nn(p, v) / l
    lse = jnp.broadcast_to(m + jnp.log(l), (ATT_BLOCK, HEAD_DIM))
    return o, lse


def _attn_tiles(dilation):
    nb = SEQ // dilation // ATT_BLOCK
    return [(r, n) for r in range(dilation) for n in range(nb)]


def _attn_rows(r, n, dilation, nblk=1):
    if dilation == 1:
        return pl.ds(r + n * ATT_BLOCK, nblk * ATT_BLOCK)
    return pl.ds(r + n * ATT_BLOCK * dilation, nblk * ATT_BLOCK, stride=dilation)


_QL, _KL, _VL = slice(0, 128), slice(128, 256), slice(256, 384)


def _qkv_specs(hb):
    return [pl.BlockSpec((None, SEQ, HEAD_DIM), functools.partial(lambda b, h, j: (b, 0, 3 * (hb + h) + j), j=j))
            for j in range(3)]


def attn_fwd(pa, bias, qg, kg, group, name):
    B = pa.shape[0]
    dilation = ATT_GROUPS[group][1]
    hb = group * 4

    def body(q_ref, k_ref, v_ref, b_ref, qg_ref, kg_ref, o_ref, l_ref, qn_s, kn_s):
        qn_s[...] = _qk_norm(q_ref[...], qg_ref[...])
        kn_s[...] = _qk_norm(k_ref[...], kg_ref[...])
        bias_all, bias_first = _masked_bias(b_ref[0])
        for (r, n) in _attn_tiles(dilation):
            rows = _attn_rows(r, n, dilation)
            if n == 0:
                krows, bias_t = rows, bias_first
            else:
                krows, bias_t = _attn_rows(r, n - 1, dilation, 2), bias_all
            o, lse = _attn_tile(qn_s[rows, :], kn_s[krows, :], v_ref[krows, :], bias_t)
            o_ref[rows, :] = o
            l_ref[rows, :] = lse

    head_out = pl.BlockSpec((None, SEQ, HEAD_DIM), lambda b, h: (b, 0, h))
    return pl.pallas_call(
        body, name=name, grid=(B, 4),
        in_specs=_qkv_specs(hb) + [
                  pl.BlockSpec((1, ATT_BLOCK, 2 * ATT_BLOCK), lambda b, h: (hb + h, 0, 0)),
                  pl.BlockSpec((1, HEAD_DIM), lambda b, h: (0, 0)),
                  pl.BlockSpec((1, HEAD_DIM), lambda b, h: (0, 0))],
        out_specs=[head_out, head_out],
        out_shape=[jax.ShapeDtypeStruct((B, SEQ, 512), F32), jax.ShapeDtypeStruct((B, SEQ, 512), F32)],
        scratch_shapes=[pltpu.VMEM((SEQ, HEAD_DIM), F32)] * 2,
        compiler_params=_cparams(),
    )(pa, pa, pa, bias, qg, kg)


def attn_bwd(pa, bias, qg, kg, do, dlse, dpa, group, name):
    B = pa.shape[0]
    dilation = ATT_GROUPS[group][1]
    hb = group * 4

    def body(q_ref, k_ref, v_ref, b_ref, qg_ref, kg_ref, do_ref, dl_ref, dpa_in,
             dp_ref, db_ref, dqg_ref, dkg_ref, qn_s, kn_s, dq_s, dk_s, dv_s):
        del dpa_in
        h_id = pl.program_id(1)

        @pl.when(jnp.logical_and(pl.program_id(0) == 0, h_id == 0))
        def _():
            db_ref[...] = jnp.zeros_like(db_ref)
            dqg_ref[...] = jnp.zeros_like(dqg_ref)
            dkg_ref[...] = jnp.zeros_like(dkg_ref)

        dk_s[...] = jnp.zeros_like(dk_s)
        dv_s[...] = jnp.zeros_like(dv_s)
        qn_s[...] = _qk_norm(q_ref[...], qg_ref[...])
        kn_s[...] = _qk_norm(k_ref[...], kg_ref[...])
        bias_all, bias_first = _masked_bias(b_ref[0])
        for (r, n) in _attn_tiles(dilation):
            rows = _attn_rows(r, n, dilation)
            if n == 0:
                krows, bias_t = rows, bias_first
            else:
                krows, bias_t = _attn_rows(r, n - 1, dilation, 2), bias_all
            _, vjp = jax.vjp(_attn_tile, qn_s[rows, :], kn_s[krows, :], v_ref[krows, :], bias_t)
            dqn, dkn, dv, dbias = vjp((do_ref[rows, :], dl_ref[rows, :]))
            dq_s[rows, :] = dqn
            dk_s[krows, :] += dkn
            dv_s[krows, :] += dv
            if n == 0:
                db_ref[h_id, :, ATT_BLOCK:] += dbias
            else:
                db_ref[h_id] += dbias
        for x_ref, g_ref, d_s, dg_ref, lanes in ((q_ref, qg_ref, dq_s, dqg_ref, _QL), (k_ref, kg_ref, dk_s, dkg_ref, _KL)):
            _, vjp = jax.vjp(_qk_norm, x_ref[...], g_ref[...])
            dx, dg = vjp(d_s[...])
            dp_ref[0, :, lanes] = dx.astype(BF16)
            dg_ref[...] += dg
        dp_ref[0, :, _VL] = dv_s[...].astype(BF16)

    const2 = lambda b, h: (0, 0)
    head_in = pl.BlockSpec((None, SEQ, HEAD_DIM), lambda b, h: (b, 0, h))
    head_blk = pl.BlockSpec((1, SEQ, ATT_HEAD_COLS), lambda b, h: (b, 0, hb + h))
    return pl.pallas_call(
        body, name=name, grid=(B, 4),
        in_specs=_qkv_specs(hb) + [
                  pl.BlockSpec((1, ATT_BLOCK, 2 * ATT_BLOCK), lambda b, h: (hb + h, 0, 0)),
                  pl.BlockSpec((1, HEAD_DIM), const2), pl.BlockSpec((1, HEAD_DIM), const2),
                  head_in, head_in,
                  pl.BlockSpec(memory_space=pl.ANY)],
        out_specs=[head_blk,
                   pl.BlockSpec((4, ATT_BLOCK, 2 * ATT_BLOCK), lambda b, h: (0, 0, 0)),
                   pl.BlockSpec((1, HEAD_DIM), const2), pl.BlockSpec((1, HEAD_DIM), const2)],
        out_shape=[jax.ShapeDtypeStruct(dpa.shape, BF16),
                   jax.ShapeDtypeStruct((4, ATT_BLOCK, 2 * ATT_BLOCK), F32),
                   jax.ShapeDtypeStruct((1, HEAD_DIM), F32), jax.ShapeDtypeStruct((1, HEAD_DIM), F32)],
        scratch_shapes=[pltpu.VMEM((SEQ, HEAD_DIM), F32)] * 5,
        input_output_aliases={8: 0},
        compiler_params=_cparams(),
    )(pa, pa, pa, bias, qg, kg, do, dlse, dpa)


def _merge(o0, o1, o2, l0, l1, l2):
    mx = jnp.maximum(jnp.maximum(l0, l1), l2)
    e0, e1, e2 = jnp.exp(l0 - mx), jnp.exp(l1 - mx), jnp.exp(l2 - mx)
    den = e0 + e1 + e2
    return (e0 / den) * o0 + (e1 / den) * o1 + (e2 / den) * o2


def merge_fwd(os_, ls_, name):
    B = os_[0].shape[0]

    def body(o0, o1, o2, l0, l1, l2, a_ref):
        a_ref[0] = _merge(o0[0], o1[0], o2[0], l0[0], l1[0], l2[0]).astype(BF16)

    return pl.pallas_call(
        body, name=name, grid=(B, SEQ // ROW_TILE),
        in_specs=[_row_spec(512)] * 6, out_specs=_row_spec(512),
        out_shape=jax.ShapeDtypeStruct((B, SEQ, 512), BF16),
        compiler_params=_cparams(),
    )(*os_, *ls_)


def merge_bwd(os_, ls_, datt, name):
    B = os_[0].shape[0]

    def body(o0, o1, o2, l0, l1, l2, da_ref, *outs):
        _, vjp = jax.vjp(_merge, o0[0], o1[0], o2[0], l0[0], l1[0], l2[0])
        for o_ref, g in zip(outs, vjp(da_ref[0])):
            o_ref[0] = g

    return pl.pallas_call(
        body, name=name, grid=(B, SEQ // ROW_TILE),
        in_specs=[_row_spec(512)] * 7, out_specs=[_row_spec(512)] * 6,
        out_shape=[jax.ShapeDtypeStruct((B, SEQ, 512), F32)] * 6,
        compiler_params=_cparams(),
    )(*os_, *ls_, datt)


def _gate_mix(ga, gm, ya, ym):
    return jax.nn.sigmoid(ga) * ya + jax.nn.sigmoid(gm) * ym


def gate_fwd(pg, ya, ym, name):
    B = pg.shape[0]

    def body(ga, gm, ya_ref, ym_ref, z_ref):
        z_ref[0] = _gate_mix(ga[0], gm[0], ya_ref[0], ym_ref[0]).astype(BF16)

    return pl.pallas_call(
        body, name=name, grid=(B, SEQ // ROW_TILE),
        in_specs=[pl.BlockSpec((1, ROW_TILE, D_MODEL), lambda b, i: (b, i, 0)),
                  pl.BlockSpec((1, ROW_TILE, D_MODEL), lambda b, i: (b, i, 1)),
                  _row_spec(D_MODEL), _row_spec(D_MODEL)],
        out_specs=_row_spec(D_MODEL),
        out_shape=jax.ShapeDtypeStruct((B, SEQ, D_MODEL), BF16),
        compiler_params=_cparams(),
    )(pg, pg, ya, ym)


def gate_bwd(pg, ya, ym, dz, name):
    B = pg.shape[0]

    def body(ga, gm, ya_ref, ym_ref, dz_ref, dpg_ref, dya_ref, dym_ref):
        _, vjp = jax.vjp(_gate_mix, ga[0], gm[0], ya_ref[0], ym_ref[0])
        dga, dgm, dya, dym = vjp(dz_ref[0])
        dpg_ref[0, :, :D_MODEL] = dga.astype(BF16)
        dpg_ref[0, :, D_MODEL:] = dgm.astype(BF16)
        dya_ref[0] = dya.astype(BF16)
        dym_ref[0] = dym.astype(BF16)

    return pl.pallas_call(
        body, name=name, grid=(B, SEQ // ROW_TILE),
        in_specs=[pl.BlockSpec((1, ROW_TILE, D_MODEL), lambda b, i: (b, i, 0)),
                  pl.BlockSpec((1, ROW_TILE, D_MODEL), lambda b, i: (b, i, 1)),
                  _row_spec(D_MODEL), _row_spec(D_MODEL), _row_spec(D_MODEL)],
        out_specs=[_row_spec(GATE_COLS), _row_spec(D_MODEL), _row_spec(D_MODEL)],
        out_shape=[jax.ShapeDtypeStruct((B, SEQ, GATE_COLS), BF16),
                   jax.ShapeDtypeStruct((B, SEQ, D_MODEL), BF16), jax.ShapeDtypeStruct((B, SEQ, D_MODEL), BF16)],
        compiler_params=_cparams(),
    )(pg, pg, ya, ym, dz)


def _log_sigmoid(x):
    return jnp.minimum(x, 0.0) - jnp.log(1.0 + jnp.exp(-jnp.abs(x)))


def _head_mask(e):
    lane = lax.broadcasted_iota(jnp.int32, (1, 128), 1)
    return jnp.logical_and(lane >= e * 64, lane < (e + 1) * 64).astype(F32)


def _bmxu(a, b, ca, cb):
    return lax.dot_general(a.astype(BF16), b.astype(BF16), (((ca,), (cb,)), ((0,), (0,))), preferred_element_type=F32)


@jax.custom_vjp
def cdot_nt(a, b):
    return _bmxu(a, b, 2, 2)


cdot_nt.defvjp(lambda a, b: (_bmxu(a, b, 2, 2), (a, b)),
               lambda res, g: (_bmxu(g, res[1], 2, 1), _bmxu(g, res[0], 1, 1)))


@jax.custom_vjp
def cdot_nn(a, b):
    return _bmxu(a, b, 2, 1)


cdot_nn.defvjp(lambda a, b: (_bmxu(a, b, 2, 1), (a, b)),
               lambda res, g: (_bmxu(g, res[1], 2, 2), _bmxu(res[0], g, 1, 1)))


@jax.custom_vjp
def cdot_tn(a, b):
    return _bmxu(a, b, 1, 1)


cdot_tn.defvjp(lambda a, b: (_bmxu(a, b, 1, 1), (a, b)),
               lambda res, g: (_bmxu(res[1], g, 2, 2), _bmxu(res[0], g, 2, 1)))


def _top_bits(x):
    return lax.bitcast_convert_type(lax.bitcast_convert_type(x, jnp.uint32) & jnp.uint32(0xFFFF0000), F32)


def _split3(x):
    hi = _top_bits(x)
    r = x - hi
    mid = _top_bits(r)
    return hi, mid, r - mid


def _parts_in_lanes(col):
    hi, mid, lo = _split3(col)
    lane = lax.broadcasted_iota(jnp.int32, (1, 1, 8), 2)
    return jnp.where(lane == 0, hi, jnp.where(lane == 1, mid, jnp.where(lane == 2, lo, 0.0)))


def _parts_in_rows(row):
    hi, mid, lo = _split3(row)
    sub = lax.broadcasted_iota(jnp.int32, (1, 8, 1), 1)
    return jnp.where(sub == 0, hi, jnp.where(sub == 1, mid, jnp.where(sub == 2, lo, 0.0)))


def _chunk_matrix(kind, c):
    ri = lax.broadcasted_iota(jnp.int32, (c, ML_CHUNK, ML_CHUNK), 1)
    ci = lax.broadcasted_iota(jnp.int32, (c, ML_CHUNK, ML_CHUNK), 2)
    return {"eye": ri == ci, "lower": ri >= ci, "upper": ri <= ci}[kind].astype(F32)


def _col_col(kind, col):
    out = _bmxu(_chunk_matrix(kind, col.shape[0]), _parts_in_lanes(col), 2, 1)
    return jnp.sum(out, axis=-1, keepdims=True)


def _col_row(col):
    out = _bmxu(_parts_in_lanes(col), _chunk_matrix("eye", col.shape[0]), 1, 1)
    return jnp.sum(out, axis=1, keepdims=True)


def _row_col(row):
    out = _bmxu(_chunk_matrix("eye", row.shape[0]), _parts_in_rows(row), 2, 2)
    return jnp.sum(out, axis=-1, keepdims=True)


@jax.custom_vjp
def chunk_cumsum(col):
    return _col_col("lower", col)


chunk_cumsum.defvjp(lambda col: (_col_col("lower", col), None), lambda _, g: (_col_col("upper", g),))


@jax.custom_vjp
def col_to_row(col):
    return _col_row(col)


col_to_row.defvjp(lambda col: (_col_row(col), None), lambda _, g: (_row_col(g),))


def _ml_intra(q2, k2, v, ifb, *, e):
    c, L = N_CHUNKS, ML_CHUNK
    hm = _head_mask(e)
    q3 = (q2 * hm).reshape(c, L, 128)
    k3 = (k2 * hm).reshape(c, L, 128)
    v3 = v.reshape(c, L, 128)
    if3 = ifb.reshape(c, L, 128)
    lanes = lax.broadcasted_iota(jnp.int32, (c, L, 128), 2)
    li = jnp.sum(jnp.where(lanes == e, if3, 0.0), axis=-1, keepdims=True)
    lf = _log_sigmoid(jnp.sum(jnp.where(lanes == 2 + e, if3, 0.0), axis=-1, keepdims=True))
    b = chunk_cumsum(lf)
    last = lax.broadcasted_iota(jnp.int32, (1, L, 1), 1) == L - 1
    b_end = jnp.sum(jnp.where(last, b, 0.0), axis=1, keepdims=True)
    causal = lax.broadcasted_iota(jnp.int32, (L, L), 0) >= lax.broadcasted_iota(jnp.int32, (L, L), 1)
    Dm = jnp.where(causal, b + col_to_row(li - b), -jnp.inf)
    mD = lax.stop_gradient(jnp.max(Dm, axis=-1, keepdims=True))
    P0 = cdot_nt(q3, k3) * jnp.exp(Dm - mD)
    H0 = cdot_nn(P0, v3)
    r0 = jnp.sum(P0, axis=-1, keepdims=True)
    g = b_end - b + li
    mg = lax.stop_gradient(jnp.max(g, axis=1, keepdims=True))
    kw = jnp.exp(g - mg) * k3
    return H0, r0, cdot_tn(kw, v3), jnp.sum(kw, axis=1, keepdims=True), b, b_end, mD, mg


def _ml_inter(q2, mo, gn, H0, r0, b, C_in, n_in, *, mD, m_in, e):
    c, L = N_CHUNKS, ML_CHUNK
    q3 = (q2 * _head_mask(e)).reshape(c, L, 128)
    a = b + m_in
    m_t = lax.stop_gradient(jnp.maximum(a, mD))
    c1 = jnp.exp(mD - m_t)
    c2 = jnp.exp(a - m_t)
    num = c1 * H0 + c2 * cdot_nn(q3, C_in)
    nq = c1 * r0 + c2 * jnp.sum(q3 * n_in, axis=-1, keepdims=True)
    h = num / jnp.maximum(jnp.abs(nq), jnp.exp(-m_t))
    hg = _rms(h) * gn * jax.nn.sigmoid(mo.reshape(c, L, 128))
    return hg.reshape(c * L, 128)


def _state_sweep(U_s, un_s, be_s, mg_s, Cin_s, nin_s, min_s, al_s, bt_s):
    def step(j, carry):
        C, n, m = carry
        Cin_s[j], nin_s[j], min_s[j] = C, n, m
        m_out = jnp.maximum(be_s[j] + m, mg_s[j])
        al = jnp.exp(be_s[j] + m - m_out)
        bt = jnp.exp(mg_s[j] - m_out)
        al_s[j], bt_s[j] = al, bt
        return al * C + bt * U_s[j], al * n + bt * un_s[j], m_out

    lax.fori_loop(0, N_CHUNKS, step, (jnp.zeros((128, 128), F32), jnp.zeros((1, 128), F32), jnp.zeros((1, 1), F32)))


def _state_sweep_bwd(U_s, un_s, dbe_s, Cin_s, nin_s, dCp_s, dnp_s, al_s, bt_s):
    def step(t, carry):
        j = N_CHUNKS - 1 - t
        dC, dn = carry
        al, bt = al_s[j], bt_s[j]
        U_s[j] = bt * dC
        un_s[j] = bt * dn
        dal = jnp.sum(jnp.sum(dC * Cin_s[j], axis=1, keepdims=True), axis=0, keepdims=True) \
            + jnp.sum(dn * nin_s[j], axis=1, keepdims=True)
        dbe_s[j] = dal * al
        return dCp_s[j] + al * dC, dnp_s[j] + al * dn

    lax.fori_loop(0, N_CHUNKS, step, (jnp.zeros((128, 128), F32), jnp.zeros((1, 128), F32)))


def _state_scratch():
    c = N_CHUNKS
    return [pltpu.VMEM((c, 128, 128), F32), pltpu.VMEM((c, 1, 128), F32), pltpu.VMEM((c, 1, 1), F32),
            pltpu.VMEM((c, 1, 1), F32),
            pltpu.VMEM((c, 128, 128), F32), pltpu.VMEM((c, 1, 128), F32), pltpu.VMEM((c, 1, 1), F32),
            pltpu.VMEM((c, 1, 1), F32), pltpu.VMEM((c, 1, 1), F32)]


def _shift_down(x, s):
    if s == 0:
        return x
    rows = lax.broadcasted_iota(jnp.int32, x.shape, 0)
    return jnp.where(rows >= s, pltpu.roll(x, s, 0), 0.0)


def _shift_up(x, s):
    if s == 0:
        return x
    S = x.shape[0]
    rows = lax.broadcasted_iota(jnp.int32, x.shape, 0)
    return jnp.where(rows < S - s, pltpu.roll(x, S - s, 0), 0.0)


def _conv_pre(x, cw, cb):
    y = cb + cw[3:4, :] * x
    for j in range(3):
        y = y + cw[j:j + 1, :] * _shift_down(x, 3 - j)
    return y


def _conv_bwd(x, cw, dpre):
    dx = cw[3:4, :] * dpre
    dcw = [None] * 4
    dcw[3] = jnp.sum(dpre * x, axis=0, keepdims=True)
    for j in range(3):
        dx = dx + cw[j:j + 1, :] * _shift_up(dpre, 3 - j)
        dcw[j] = jnp.sum(dpre * _shift_down(x, 3 - j), axis=0, keepdims=True)
    return dx, dcw, jnp.sum(dpre, axis=0, keepdims=True)


def _silu(z):
    return z * jax.nn.sigmoid(z)


def _dsilu(z):
    s = jax.nn.sigmoid(z)
    return s * (1.0 + z * (1.0 - s))


_ML_Q, _ML_K = slice(0, 128), slice(128, 256)
_ML_IF = slice(768, 896)


def _ml_v(e):
    return slice(256 + e * 128, 384 + e * 128)


def _ml_o(e):
    return slice(512 + e * 128, 640 + e * 128)


def _ml_specs():
    pair = lambda b, p: (b, 0, p)
    return [pl.BlockSpec((1, SEQ, ML_PAIR_COLS), pair),
            pl.BlockSpec((1, 4, 128), lambda b, p: (p, 0, 0)),
            pl.BlockSpec((1, 4, 128), lambda b, p: (4 + p, 0, 0)),
            pl.BlockSpec((1, 1, 128), lambda b, p: (p, 0, 0)),
            pl.BlockSpec((1, 1, 128), lambda b, p: (4 + p, 0, 0)),
            pl.BlockSpec((1, 1, 128), lambda b, p: (p, 0, 0)),
            pl.BlockSpec((1, 1, 256), lambda b, p: (p, 0, 0))]


def mlstm_fwd(pm, cw8, cb8, bifp, gn4, name):
    B = pm.shape[0]

    def body(p_ref, cwq, cwk, cbq, cbk, bif_ref, gn_ref, hg_ref, *st):
        U_s, un_s, be_s, mg_s, Cin_s, nin_s, min_s, al_s, bt_s = st
        qc = _silu(_conv_pre(p_ref[0, :, _ML_Q], cwq[0], cbq[0]))
        kc = _silu(_conv_pre(p_ref[0, :, _ML_K], cwk[0], cbk[0])) * (64 ** -0.5)
        ifb = p_ref[0, :, _ML_IF] + bif_ref[0]
        for e in range(2):
            lanes = slice(e * 128, (e + 1) * 128)
            H0, r0, U, un, b, b_end, mD, mg = _ml_intra(qc, kc, p_ref[0, :, _ml_v(e)], ifb, e=e)
            U_s[...], un_s[...], be_s[...], mg_s[...] = U, un, b_end, mg
            _state_sweep(*st)
            hg = _ml_inter(qc, p_ref[0, :, _ml_o(e)], gn_ref[0, :, lanes], H0, r0, b, Cin_s[...], nin_s[...],
                           mD=mD, m_in=min_s[...], e=e)
            hg_ref[0, :, lanes] = hg.astype(BF16)

    return pl.pallas_call(
        body, name=name, grid=(B, ML_PAIRS),
        in_specs=_ml_specs(),
        out_specs=pl.BlockSpec((1, SEQ, 256), lambda b, p: (b, 0, p)),
        out_shape=jax.ShapeDtypeStruct((B, SEQ, D_MODEL), BF16),
        scratch_shapes=_state_scratch(),
        compiler_params=_cparams(),
    )(pm, cw8, cw8, cb8, cb8, bifp, gn4)


def mlstm_bwd(pm, cw8, cb8, bifp, gn4, dhg, name):
    B = pm.shape[0]

    def body(p_ref, cwq, cwk, cbq, cbk, bif_ref, gn_ref, dh_ref,
             dp_ref, dcw_ref, dcb_ref, dbif_ref, dgn_ref, *scr):
        st = scr[:9]
        U_s, un_s, be_s, mg_s, Cin_s, nin_s, min_s, al_s, bt_s = st
        dCp_s, dnp_s, dbe_s = scr[9:]
        p_id = pl.program_id(1)

        @pl.when(jnp.logical_and(pl.program_id(0) == 0, p_id == 0))
        def _():
            dcw_ref[...] = jnp.zeros_like(dcw_ref)
            dcb_ref[...] = jnp.zeros_like(dcb_ref)
            dbif_ref[...] = jnp.zeros_like(dbif_ref)
            dgn_ref[...] = jnp.zeros_like(dgn_ref)

        qc = _silu(_conv_pre(p_ref[0, :, _ML_Q], cwq[0], cbq[0]))
        kc = _silu(_conv_pre(p_ref[0, :, _ML_K], cwk[0], cbk[0])) * (64 ** -0.5)
        ifb = p_ref[0, :, _ML_IF] + bif_ref[0]
        dq = jnp.zeros((SEQ, 128), F32)
        dk = jnp.zeros((SEQ, 128), F32)
        difb = jnp.zeros((SEQ, 128), F32)
        for e in range(2):
            lanes = slice(e * 128, (e + 1) * 128)
            (H0, r0, U, un, b, b_end, mD, mg), vjp1 = jax.vjp(functools.partial(_ml_intra, e=e), qc, kc,
                                                              p_ref[0, :, _ml_v(e)], ifb)
            U_s[...], un_s[...], be_s[...], mg_s[...] = U, un, b_end, mg
            _state_sweep(*st)
            _, vjp3 = jax.vjp(functools.partial(_ml_inter, mD=mD, m_in=min_s[...], e=e), qc, p_ref[0, :, _ml_o(e)],
                              gn_ref[0, :, lanes], H0, r0, b, Cin_s[...], nin_s[...])
            dq_a, dmo, dgn, dH0, dr0, db_a, dCp, dnp = vjp3(dh_ref[0, :, lanes])
            dCp_s[...], dnp_s[...] = dCp, dnp
            _state_sweep_bwd(U_s, un_s, dbe_s, Cin_s, nin_s, dCp_s, dnp_s, al_s, bt_s)
            dq_b, dk_b, dv, difb_e = vjp1((dH0, dr0, U_s[...], un_s[...], db_a, dbe_s[...],
                                           jnp.zeros_like(mD), jnp.zeros_like(mg)))
            dq, dk, difb = dq + dq_a + dq_b, dk + dk_b, difb + difb_e
            dp_ref[0, :, _ml_v(e)] = dv.astype(BF16)
            dp_ref[0, :, _ml_o(e)] = dmo.astype(BF16)
            dgn_ref[p_id, :, lanes] += dgn
        dp_ref[0, :, _ML_IF] = difb.astype(BF16)
        dbif_ref[p_id] += jnp.sum(difb, axis=0, keepdims=True)

        for (sl, cw, cb, d, blk, scale) in ((_ML_Q, cwq, cbq, dq, p_id, 1.0), (_ML_K, cwk, cbk, dk, 4 + p_id, 64 ** -0.5)):
            xr = p_ref[0, :, sl]
            dpre = d * scale * _dsilu(_conv_pre(xr, cw[0], cb[0]))
            dx, dcw, dcb = _conv_bwd(xr, cw[0], dpre)
            dp_ref[0, :, sl] = dx.astype(BF16)
            for j in range(4):
                dcw_ref[blk, j:j + 1, :] += dcw[j]
            dcb_ref[blk] += dcb

    full3 = lambda b, p: (0, 0, 0)
    return pl.pallas_call(
        body, name=name, grid=(B, ML_PAIRS),
        in_specs=[pl.BlockSpec((1, SEQ, ML_PAIR_COLS), lambda b, p: (b, 0, p), pipeline_mode=pl.Buffered(1))]
        + _ml_specs()[1:] + [pl.BlockSpec((1, SEQ, 256), lambda b, p: (b, 0, p), pipeline_mode=pl.Buffered(1))],
        out_specs=[pl.BlockSpec((1, SEQ, ML_PAIR_COLS), lambda b, p: (b, 0, p)),
                   pl.BlockSpec((8, 4, 128), full3), pl.BlockSpec((8, 1, 128), full3),
                   pl.BlockSpec((4, 1, 128), full3), pl.BlockSpec((4, 1, 256), full3)],
        out_shape=[jax.ShapeDtypeStruct((B, SEQ, ML_COLS), BF16),
                   jax.ShapeDtypeStruct((8, 4, 128), F32), jax.ShapeDtypeStruct((8, 1, 128), F32),
                   jax.ShapeDtypeStruct((4, 1, 128), F32), jax.ShapeDtypeStruct((4, 1, 256), F32)],
        scratch_shapes=_state_scratch() + [pltpu.VMEM((N_CHUNKS, 128, 128), F32), pltpu.VMEM((N_CHUNKS, 1, 128), F32),
                                           pltpu.VMEM((N_CHUNKS, 1, 1), F32)],
        compiler_params=_cparams(),
    )(pm, cw8, cw8, cb8, cb8, bifp, gn4, dhg)


def _adamw(w, g, m, v):
    m = ADAM_B1 * m + (1.0 - ADAM_B1) * g
    v = ADAM_B2 * v + (1.0 - ADAM_B2) * (g * g)
    m_hat = m / (1.0 - ADAM_B1 ** ADAM_STEP)
    v_hat = v / (1.0 - ADAM_B2 ** ADAM_STEP)
    delta = -ADAM_LR * (m_hat / (jnp.sqrt(v_hat) + ADAM_EPS) + ADAM_WD * w)
    return delta, m, v


def adamw(w, g, m, v, name, parts=False):
    R, C = w.shape
    if R % 8 == 0 or R * C * 4 <= (1 << 20):
        tr = _pick(R, (256, 128, 64, 32, 16, 8)) if R * C * 4 > (1 << 20) else R
        steps = R // tr
        spec = pl.BlockSpec((tr, C), lambda i: (i, 0))
        g_spec = pl.BlockSpec((N_DEV, tr, C), lambda i: (0, i, 0)) if parts else spec
    else:
        tc = _pick(C, (256, 128))
        steps = C // tc
        spec = pl.BlockSpec((R, tc), lambda i: (0, i))
        g_spec = pl.BlockSpec((N_DEV, R, tc), lambda i: (0, 0, i)) if parts else spec

    def body(w_ref, g_ref, m_ref, v_ref, go_ref, d_ref, mo_ref, vo_ref):
        if parts:
            g = g_ref[0].astype(F32)
            for k in range(1, N_DEV):
                g = g + g_ref[k].astype(F32)
        else:
            g = g_ref[...]
        d, mn, vn = _adamw(w_ref[...], g, m_ref[...], v_ref[...])
        go_ref[...], d_ref[...], mo_ref[...], vo_ref[...] = g, d, mn, vn

    return pl.pallas_call(
        body, name=name, grid=(steps,),
        in_specs=[spec, g_spec, spec, spec], out_specs=[spec] * 4,
        out_shape=[jax.ShapeDtypeStruct((R, C), F32)] * 4,
        compiler_params=_cparams(),
    )(w, g, m, v)


def _mesh_pos():
    return lax.axis_index("x"), lax.axis_index("y"), lax.axis_index("c")


def _flip(pos, f):
    x, y, c = pos
    return (1 - x if f & 4 else x, 1 - y if f & 2 else y, 1 - c if f & 1 else c)


def _index(pos):
    return 4 * pos[0] + 2 * pos[1] + pos[2]


def _exchange(arrs, name, scatter):
    n = len(arrs)

    def body(*refs):
        ins, outs = refs[:n], refs[n:2 * n]
        send, recv, lsem = refs[2 * n:]
        me = _mesh_pos()
        mine = _index(me)
        copies = []
        for i in range(n):
            src = ins[i].at[mine] if scatter else ins[i]
            loc = pltpu.make_async_copy(src, outs[i].at[mine], lsem.at[i])
            loc.start()
            copies.append(loc)
            for f in range(1, N_DEV):
                peer = _flip(me, f)
                src = ins[i].at[_index(peer)] if scatter else ins[i]
                cp = pltpu.make_async_remote_copy(
                    src_ref=src, dst_ref=outs[i].at[mine],
                    send_sem=send.at[i * 7 + f - 1], recv_sem=recv.at[i * 7 + f - 1],
                    device_id=peer, device_id_type=pl.DeviceIdType.MESH)
                cp.start()
                copies.append(cp)
        for cp in copies:
            cp.wait()

    any_spec = pl.BlockSpec(memory_space=pl.ANY)
    out_shape = [jax.ShapeDtypeStruct(a.shape if scatter else (N_DEV,) + a.shape, a.dtype) for a in arrs]
    res = pl.pallas_call(
        body, name=name,
        in_specs=[any_spec] * n, out_specs=[any_spec] * n, out_shape=out_shape,
        scratch_shapes=[pltpu.SemaphoreType.DMA((7 * n,)), pltpu.SemaphoreType.DMA((7 * n,)),
                        pltpu.SemaphoreType.DMA((n,))],
        compiler_params=_cparams(),
    )(*arrs)
    return list(res)


def all_gather(arrs, name):
    return _exchange(arrs, name, False)


def all_gather_two_level(arrs, name):
    n = len(arrs)

    def body(*refs):
        ins, outs = refs[:n], refs[n:2 * n]
        send, recv, lsem = refs[2 * n:]
        x, y, c = _mesh_pos()
        me, sibling = (x, y, c), (x, y, 1 - c)
        chips = [(1 - x, y), (x, 1 - y), (1 - x, 1 - y)]

        def copy(i, k, block, to, src=None):
            rows = outs[i].at[_index(block)]
            return pltpu.make_async_remote_copy(
                src_ref=rows if src is None else src, dst_ref=rows,
                send_sem=send.at[i * 7 + k], recv_sem=recv.at[i * 7 + k],
                device_id=to, device_id_type=pl.DeviceIdType.MESH)

        local = [pltpu.make_async_copy(ins[i], outs[i].at[_index(me)], lsem.at[i]) for i in range(n)]
        first = [copy(i, 0, me, sibling, src=ins[i]) for i in range(n)]
        first += [copy(i, 1 + j, me, (*chip, c), src=ins[i]) for i in range(n) for j, chip in enumerate(chips)]
        for cp in local + first:
            cp.start()
        passed = []
        for j, chip in enumerate(chips):
            for i in range(n):
                copy(i, 1 + j, (*chip, c), me).wait_recv()
                cp = copy(i, 4 + j, (*chip, c), sibling)
                cp.start()
                passed.append(cp)
        for i in range(n):
            copy(i, 0, sibling, me).wait_recv()
            for j, chip in enumerate(chips):
                copy(i, 4 + j, (*chip, 1 - c), me).wait_recv()
        for cp in first + passed:
            cp.wait_send()
        for cp in local:
            cp.wait()

    any_spec = pl.BlockSpec(memory_space=pl.ANY)
    res = pl.pallas_call(
        body, name=name,
        in_specs=[any_spec] * n, out_specs=[any_spec] * n,
        out_shape=[jax.ShapeDtypeStruct((N_DEV,) + a.shape, a.dtype) for a in arrs],
        scratch_shapes=[pltpu.SemaphoreType.DMA((7 * n,)), pltpu.SemaphoreType.DMA((7 * n,)),
                        pltpu.SemaphoreType.DMA((n,))],
        compiler_params=_cparams(),
    )(*arrs)
    return list(res)


def all_to_all(arrs, name):
    return _exchange(arrs, name, True)


_HBM = pl.BlockSpec(memory_space=pltpu.HBM)
_SEM = pl.BlockSpec(memory_space=pltpu.SEMAPHORE)
_EFFECT = pltpu.SideEffectType.DATAFLOW_SIDE_EFFECTING


def _split_copies(ins, lands, send, recv, scatter, waiting):
    me = _mesh_pos()
    mine = _index(me)
    copies = []
    for i in range(len(ins)):
        for f in range(1, N_DEV):
            peer = _flip(me, f)
            src = ins[i].at[_index(peer)] if scatter else ins[i]
            copies.append(pltpu.make_async_remote_copy(
                src_ref=src, dst_ref=lands[i].at[_index(peer) if waiting else mine],
                send_sem=send.at[i * 7 + f - 1], recv_sem=recv.at[i * 7 + f - 1],
                device_id=peer, device_id_type=pl.DeviceIdType.MESH))
    return copies


def exchange_start(arrs, name, scatter, after=()):
    n = len(arrs)
    land_shapes = [a.shape if scatter else (N_DEV,) + a.shape for a in arrs]

    def body(*refs):
        ins, lands = refs[:n], refs[n:2 * n]
        send, recv = refs[2 * n + len(after)], refs[2 * n + len(after) + 1]
        token = refs[-1]
        for cp in _split_copies(ins, lands, send, recv, scatter, False):
            cp.start()
        token[...] = jnp.zeros_like(token)

    res = pl.pallas_call(
        body, name=name,
        out_shape=(pltpu.SemaphoreType.DMA((7 * n,)), pltpu.SemaphoreType.DMA((7 * n,)),
                   *[pltpu.HBM(a.shape, a.dtype) for a in arrs],
                   *[pltpu.HBM(s, a.dtype) for s, a in zip(land_shapes, arrs)],
                   jax.ShapeDtypeStruct((8, 128), F32)),
        in_specs=[_HBM] * (2 * n) + [pl.BlockSpec(memory_space=pl.ANY)] * len(after),
        out_specs=(_SEM, _SEM, *[_HBM] * (2 * n), pl.BlockSpec(memory_space=pltpu.VMEM)),
        input_output_aliases={i: 2 + i for i in range(2 * n)},
        compiler_params=pltpu.CompilerParams(has_side_effects=_EFFECT),
    )(*[pltpu.with_memory_space_constraint(a, pltpu.HBM) for a in arrs],
      *[pltpu.with_memory_space_constraint(lax.empty(s, a.dtype), pltpu.HBM) for s, a in zip(land_shapes, arrs)],
      *after)
    return (res[0], res[1], list(res[2:2 + n]), list(res[2 + n:2 + 2 * n])), res[-1]


def exchange_wait(handle, after, name, scatter):
    send, recv, srcs, lands = handle
    n = len(srcs)

    def body(*refs):
        ins, lnd = refs[:n], refs[n:2 * n]
        send_, recv_ = refs[2 * n], refs[2 * n + 1]
        for cp in _split_copies(ins, lnd, send_, recv_, scatter, True):
            cp.wait_send()
            cp.wait_recv()

    res = pl.pallas_call(
        body, name=name,
        out_shape=(*[pltpu.HBM(a.shape, a.dtype) for a in srcs], *[pltpu.HBM(a.shape, a.dtype) for a in lands]),
        in_specs=[_HBM] * (2 * n) + [_SEM, _SEM, pl.BlockSpec(memory_space=pl.ANY)],
        out_specs=tuple([_HBM] * (2 * n)),
        input_output_aliases={i: i for i in range(2 * n)},
        compiler_params=pltpu.CompilerParams(has_side_effects=_EFFECT),
    )(*srcs, *lands, send, recv, after)
    return list(res[n:])


def _own_slot(land, own):
    return lax.dynamic_update_slice(land, own[None], (_index(_mesh_pos()),) + (0,) * own.ndim)


def cast_bf16(arrs, name):
    outs = []
    for i, a in enumerate(arrs):
        R, C = a.shape
        if R % 8 == 0:
            tr = _pick(R, (256, 128, 64, 32, 16, 8)) if R * C * 4 > (1 << 21) else R
            steps, spec = R // tr, pl.BlockSpec((tr, C), lambda i: (i, 0))
        else:
            steps, spec = C // 256, pl.BlockSpec((R, 256), lambda i: (0, i))

        def body(a_ref, o_ref):
            o_ref[...] = a_ref[...].astype(BF16)

        outs.append(pl.pallas_call(body, name=f"{name}_{i}", grid=(steps,), in_specs=[spec], out_specs=spec,
                                   out_shape=jax.ShapeDtypeStruct((R, C), BF16), compiler_params=_cparams())(a))
    return outs


def sum_parts(parts, name):
    def fn(p):
        g = p[0]
        for k in range(1, N_DEV):
            g = g + p[k]
        return (g,)
    return small_call(fn, [parts], [jax.ShapeDtypeStruct(parts.shape[1:], F32)], name)[0]


_SPLITS = np.cumsum([1536, 1536, 1536, 512, 512, 1024, 1024, 8, 8, 2048])[:-1].tolist()


def split_w_in(w):
    aq, ak, av, mq, mk, mv, mo, mi, mf, gates = jnp.split(w, _SPLITS, axis=1)
    R = w.shape[0]
    w_att = jnp.stack([aq.reshape(R, 12, 128), ak.reshape(R, 12, 128), av.reshape(R, 12, 128)], axis=2)
    gif = jnp.concatenate([mi.reshape(R, 4, 2), mf.reshape(R, 4, 2), jnp.zeros((R, 4, 124), w.dtype)], axis=2)
    w_ml = jnp.concatenate([mq.reshape(R, 4, 128), mk.reshape(R, 4, 128), mv.reshape(R, 4, 256),
                            mo.reshape(R, 4, 256), gif], axis=2)
    return w_att.reshape(R, ATT_COLS), w_ml.reshape(R, ML_COLS), gates


def merge_w_in(g_att, g_ml, g_gate):
    R = g_att.shape[0]
    a = g_att.reshape(R, 12, 3, 128)
    m = g_ml.reshape(R, 4, ML_PAIR_COLS)
    gif = m[:, :, 768:772]
    return jnp.concatenate([
        a[:, :, 0].reshape(R, 1536), a[:, :, 1].reshape(R, 1536), a[:, :, 2].reshape(R, 1536),
        m[:, :, 0:128].reshape(R, 512), m[:, :, 128:256].reshape(R, 512),
        m[:, :, 256:512].reshape(R, 1024), m[:, :, 512:768].reshape(R, 1024),
        gif[:, :, 0:2].reshape(R, 8), gif[:, :, 2:4].reshape(R, 8), g_gate], axis=1)


def _blk8(v, width=128):
    r = v.shape[0]
    return v.reshape(r, 1024 // width, width).transpose(1, 0, 2)


def _unblk8(v):
    nb, r, w = v.shape
    return v.transpose(1, 0, 2).reshape(r, nb * w)


def local_step(x, target, mods, w, small, late_w=None, early_g=None, w_in_g=None):
    late_w = late_w or (lambda after: w)
    big = {}
    early_g = early_g or (lambda g: big.update(g))
    w_in_g = w_in_g or (lambda g: big.update(w_in=g))
    B = x.shape[0]
    T = B * SEQ
    shift1, scale1, gate1, shift2, scale2, gate2 = mods
    f2 = lambda a: a.reshape(T, a.shape[-1])
    f3 = lambda a: a.reshape(B, SEQ, a.shape[-1])

    rel_t = jnp.pad(small["rel_bias"].T, ((0, 4), (0, 0)))
    onehots = [_bucket_onehot(d) for _, d in ATT_GROUPS]
    biases = [bias_expand(rel_t, oh, f"bias_expand{g}").reshape(16, ATT_BLOCK, 2 * ATT_BLOCK)
              for g, oh in enumerate(onehots)]
    qg, kg = small["q_norm_g"], small["k_norm_g"]
    cw8 = _blk8(small["conv_w"])
    cb8 = _blk8(small["conv_b"])
    b_if = small["b_if"].reshape(2, 4, 2)
    bifp = jnp.concatenate([b_if[0], b_if[1], jnp.zeros((4, 124), F32)], axis=1).reshape(4, 1, 128)
    gn4 = small["mlstm_norm_g"].reshape(4, 1, 256)

    u = modnorm_fwd(x, small["norm1_g"], scale1, shift1, "modnorm1")
    u2d = f2(u)
    pa = f3(matmul(u2d, w["w_att"], mode="nn", name="proj_att"))
    pm = f3(matmul(u2d, w["w_ml"], mode="nn", name="proj_ml"))
    pg = f3(matmul(u2d, w["w_gate"], mode="nn", name="proj_gate"))
    os_, ls_ = [], []
    for g in range(3):
        o, l = attn_fwd(pa, biases[g], qg, kg, g, f"attn_fwd{g}")
        os_.append(o)
        ls_.append(l)
    att = merge_fwd(os_, ls_, "merge_fwd")
    hg = mlstm_fwd(pm, cw8, cb8, bifp, gn4, "mlstm_fwd")
    w = {**w, **late_w(hg)}
    y_att = matmul(f2(att), w["w_att_out"], mode="nn", name="att_out")
    y_ml = matmul(f2(hg), w["w_ml_out"], mode="nn", name="ml_out")
    z = gate_fwd(pg, f3(y_att), f3(y_ml), "gate_fwd")
    y = matmul(f2(z), w["w_out"], mode="nn", name="out_proj")
    x1, u2 = resid_modnorm_fwd(x, f3(y), gate1, small["norm2_g"], scale2, shift2, "resid_modnorm2")
    pre, hdn = matmul(f2(u2), w["w_ff1"], mode="nn", name="ff1", out_dtypes=(F32, BF16),
                      epi=lambda acc: (acc, jnp.square(jnp.maximum(acc, 0.0))))
    ffo = matmul(hdn, w["w_ff2"], mode="nn", name="ff2")
    dx2, loss = resid_loss(x1, f3(ffo), gate2, target, "resid_loss")

    d_ffo, d_gate2 = resid_bwd(dx2, f3(ffo), gate2, "resid_bwd2")
    g_ff2 = matmul(hdn, f2(d_ffo), mode="tn", name="g_ff2", out_dtypes=(BF16,))
    d_pre = matmul(f2(d_ffo), w["w_ff2"], mode="nt", name="d_hdn", out_dtypes=(BF16,), extras=(pre,),
                   epi=lambda acc, p: (acc * (2.0 * jnp.maximum(p, 0.0)),))
    g_ff1 = matmul(f2(u2), d_pre, mode="tn", name="g_ff1", out_dtypes=(BF16,))
    du2 = matmul(d_pre, w["w_ff1"], mode="nt", name="d_u2")
    dx1, d_norm2, d_scale2, d_shift2 = modnorm_bwd(x1, small["norm2_g"], scale2, shift2, f3(du2), dx2, "modnorm_bwd2")
    dy, d_gate1 = resid_bwd(dx1, f3(y), gate1, "resid_bwd1")
    g_out = matmul(f2(z), f2(dy), mode="tn", name="g_out", out_dtypes=(BF16,))
    dz = matmul(f2(dy), w["w_out"], mode="nt", name="d_z")
    dpg, d_ya, d_ym = gate_bwd(pg, f3(y_att), f3(y_ml), f3(dz), "gate_bwd")
    g_att_out = matmul(f2(att), f2(d_ya), mode="tn", name="g_att_out", out_dtypes=(BF16,))
    d_att = matmul(f2(d_ya), w["w_att_out"], mode="nt", name="d_att")
    g_ml_out = matmul(f2(hg), f2(d_ym), mode="tn", name="g_ml_out", out_dtypes=(BF16,))
    d_hg = matmul(f2(d_ym), w["w_ml_out"], mode="nt", name="d_hg")
    started = early_g(dict(w_att_out=g_att_out, w_ml_out=g_ml_out, w_out=g_out, w_ff1=g_ff1, w_ff2=g_ff2))
    order = 0.0 if started is None else started[0, 0]
    dmerge = merge_bwd(os_, ls_, f3(d_att), "merge_bwd")
    dpa = lax.empty((B, SEQ, ATT_COLS), BF16)
    d_rel = []
    d_qg = d_kg = None
    for g in range(3):
        dpa, dbias, dq_g, dk_g = attn_bwd(pa, biases[g], qg + order, kg, dmerge[g], dmerge[3 + g], dpa, g,
                                          f"attn_bwd{g}")
        db8 = jnp.pad(dbias.reshape(4, -1), ((0, 4), (0, 0)))
        d_rel.append(bias_reduce(db8, onehots[g], f"bias_reduce{g}")[:4])
        d_qg = dq_g if d_qg is None else d_qg + dq_g
        d_kg = dk_g if d_kg is None else d_kg + dk_g
    dpm, dcw8, dcb8, dbifp, dgn4 = mlstm_bwd(pm, cw8, cb8, bifp, gn4 + order, f3(d_hg), "mlstm_bwd")
    g_w_att = matmul(u2d, f2(dpa), mode="tn", name="g_w_att", out_dtypes=(BF16,))
    g_w_ml = matmul(u2d, f2(dpm), mode="tn", name="g_w_ml", out_dtypes=(BF16,))
    g_w_gate = matmul(u2d, f2(dpg), mode="tn", name="g_w_gate", out_dtypes=(BF16,))
    started = w_in_g(merge_w_in(g_w_att, g_w_ml, g_w_gate))
    du = matmul(f2(dpa), w["w_att"], mode="nt", name="d_u_att", after=() if started is None else (started,))
    du = matmul(f2(dpm), w["w_ml"], mode="nt", name="d_u_ml", extras=(du,), epi=lambda acc, e: (acc + e,))
    du = matmul(f2(dpg), w["w_gate"], mode="nt", name="d_u_gate", extras=(du,), epi=lambda acc, e: (acc + e,))
    grad_x, d_norm1, d_scale1, d_shift1 = modnorm_bwd(x, small["norm1_g"], scale1, shift1, f3(du), dx1, "modnorm_bwd1")

    d_mods = (d_shift1, d_scale1, d_gate1, d_shift2, d_scale2, d_gate2)
    dbif = dbifp.reshape(4, 128)
    small_g = dict(
        norm1_g=d_norm1, norm2_g=d_norm2,
        b_if=jnp.stack([dbif[:, 0:2].reshape(8), dbif[:, 2:4].reshape(8)]),
        conv_w=_unblk8(dcw8), conv_b=_unblk8(dcb8), q_norm_g=d_qg, k_norm_g=d_kg,
        rel_bias=jnp.concatenate(d_rel, axis=0).T,
        mlstm_norm_g=dgn4.reshape(1, 1024))
    return loss, grad_x, d_mods, big, small_g


_SMALL = (("b_ada", 6144), ("norm1_g", 1024), ("norm2_g", 1024), ("b_if", 16), ("conv_b", 1024),
          ("q_norm_g", 128), ("k_norm_g", 128), ("rel_bias", 384), ("mlstm_norm_g", 1024), ("conv_w", 4096))
_SMALL_ROWS = 120
_REPL = _SMALL[:-1]
_REPL_ROWS = 86


def _pack(d, names, rows):
    flat = jnp.concatenate([d[k].reshape(-1) for k, _ in names])
    return jnp.pad(flat, (0, rows * 128 - flat.shape[0])).reshape(rows, 128)


def _unpack(slab, names, shapes):
    flat = slab.reshape(-1)
    out, off = {}, 0
    for k, nel in names:
        out[k] = flat[off:off + nel].reshape(shapes[k])
        off += nel
    return out


def kernel(x, c, w_ada, b_ada, norm1_g, norm2_g, w_in, b_if, conv_w, conv_b, q_norm_g, k_norm_g, rel_bias, mlstm_norm_g, w_att_out, w_ml_out, w_out, w_ff1, w_ff2, loss_target, m_w_ada, m_b_ada, m_norm1_g, m_norm2_g, m_w_in, m_b_if, m_conv_w, m_conv_b, m_q_norm_g, m_k_norm_g, m_rel_bias, m_mlstm_norm_g, m_w_att_out, m_w_ml_out, m_w_out, m_w_ff1, m_w_ff2, v_w_ada, v_b_ada, v_norm1_g, v_norm2_g, v_w_in, v_b_if, v_conv_w, v_conv_b, v_q_norm_g, v_k_norm_g, v_rel_bias, v_mlstm_norm_g, v_w_att_out, v_w_ml_out, v_w_out, v_w_ff1, v_w_ff2):
    P = dict(w_ada=w_ada, b_ada=b_ada, norm1_g=norm1_g, norm2_g=norm2_g, w_in=w_in, b_if=b_if, conv_w=conv_w,
             conv_b=conv_b, q_norm_g=q_norm_g, k_norm_g=k_norm_g, rel_bias=rel_bias, mlstm_norm_g=mlstm_norm_g,
             w_att_out=w_att_out, w_ml_out=w_ml_out, w_out=w_out, w_ff1=w_ff1, w_ff2=w_ff2)
    M = dict(w_ada=m_w_ada, b_ada=m_b_ada, norm1_g=m_norm1_g, norm2_g=m_norm2_g, w_in=m_w_in, b_if=m_b_if,
             conv_w=m_conv_w, conv_b=m_conv_b, q_norm_g=m_q_norm_g, k_norm_g=m_k_norm_g, rel_bias=m_rel_bias,
             mlstm_norm_g=m_mlstm_norm_g, w_att_out=m_w_att_out, w_ml_out=m_w_ml_out, w_out=m_w_out,
             w_ff1=m_w_ff1, w_ff2=m_w_ff2)
    V = dict(w_ada=v_w_ada, b_ada=v_b_ada, norm1_g=v_norm1_g, norm2_g=v_norm2_g, w_in=v_w_in, b_if=v_b_if,
             conv_w=v_conv_w, conv_b=v_conv_b, q_norm_g=v_q_norm_g, k_norm_g=v_k_norm_g, rel_bias=v_rel_bias,
             mlstm_norm_g=v_mlstm_norm_g, w_att_out=v_w_att_out, w_ml_out=v_w_ml_out, w_out=v_w_out,
             w_ff1=v_w_ff1, w_ff2=v_w_ff2)
    names = list(P)
    shapes = {k: P[k].shape for k in names}
    B = x.shape[0]
    me = _index(_mesh_pos())

    big_names = ("w_in", "w_att_out", "w_ml_out", "w_out", "w_ff1", "w_ff2")
    shards = cast_bf16([P[k][0] for k in big_names], "cast_w")
    w_in_g8, c8, conv_w8 = all_gather_two_level([shards[0], c, conv_w[0]], "gather_w_in")
    c_all = c8.reshape(N_DEV * B, D_MODEL)
    conv_w_full = conv_w8.transpose(1, 0, 2).reshape(4, 1024)
    w_att, w_ml, w_gate = split_w_in(w_in_g8.transpose(1, 0, 2).reshape(D_MODEL, D_IN))
    w = dict(w_att=w_att, w_ml=w_ml, w_gate=w_gate)

    (silu_c,) = small_call(lambda a: (_silu(a),), [c_all], [jax.ShapeDtypeStruct(c_all.shape, F32)], "silu_c")
    b_ada_cols = lax.dynamic_slice(b_ada, (0, me * 768), (1, 768))
    ada_cols = matmul(silu_c, w_ada[0], mode="nn", name="ada", extras=(jnp.broadcast_to(b_ada_cols, (N_DEV * B, 768)),),
                      epi=lambda acc, bb: (acc + bb,))
    (ada_t,) = all_to_all([ada_cols.reshape(N_DEV, B, 768)], "ada_exchange")
    ada = ada_t.transpose(1, 0, 2).reshape(B, 6 * D_MODEL)
    mods = tuple(ada[:, i * D_MODEL:(i + 1) * D_MODEL].reshape(B, 1, D_MODEL) for i in range(6))

    late_handle, late_order = exchange_start(shards[1:], "gather_late_start", False, after=(ada_t,))

    def late_w(after):
        lands = exchange_wait(late_handle, after, "gather_late_wait", False)
        gw = dict(zip(big_names[1:], [_own_slot(l, s) for l, s in zip(lands, shards[1:])]))
        return dict(w_att_out=gw["w_att_out"].transpose(1, 0, 2).reshape(512, D_MODEL),
                    w_ml_out=gw["w_ml_out"].reshape(D_MODEL, D_MODEL), w_out=gw["w_out"].reshape(D_MODEL, D_MODEL),
                    w_ff1=gw["w_ff1"].transpose(1, 0, 2).reshape(D_MODEL, D_FF),
                    w_ff2=gw["w_ff2"].reshape(D_FF, D_MODEL))

    pending = {}

    def send_grads(key, blocks, name):
        handle, order = exchange_start(blocks, name, True)
        pending[key] = (handle, [lax.dynamic_index_in_dim(b, me, 0, keepdims=False) for b in blocks])
        return order

    def early_g(g):
        return send_grads("late", [g["w_att_out"].reshape(512, N_DEV, 128).transpose(1, 0, 2),
                                   g["w_ml_out"].reshape(N_DEV, 128, D_MODEL), g["w_out"].reshape(N_DEV, 128, D_MODEL),
                                   g["w_ff1"].reshape(D_MODEL, N_DEV, 512).transpose(1, 0, 2),
                                   g["w_ff2"].reshape(N_DEV, 512, D_MODEL)], "grad_late_start")

    def w_in_g(g):
        return send_grads("w_in", [g.reshape(D_MODEL, N_DEV, W_IN_SHARD).transpose(1, 0, 2)], "grad_w_in_start")

    def recv_grads(key, after, name):
        handle, own = pending[key]
        return [_own_slot(l, o) for l, o in zip(exchange_wait(handle, after, name, True), own)]

    small = dict(norm1_g=norm1_g + late_order[0, 0], norm2_g=norm2_g, b_if=b_if[0], conv_w=conv_w_full, conv_b=conv_b,
                 q_norm_g=q_norm_g, k_norm_g=k_norm_g, rel_bias=rel_bias, mlstm_norm_g=mlstm_norm_g)
    loss, grad_x, d_mods, _, small_g = local_step(x, loss_target, mods, w, small, late_w, early_g, w_in_g)
    loss = lax.psum(loss[0, 0], ("x", "y", "c"))

    d_ada = jnp.concatenate([d.reshape(B, D_MODEL) for d in d_mods], axis=1)
    (d_ada_t,) = all_to_all([d_ada.reshape(B, N_DEV, 768).transpose(1, 0, 2)], "d_ada_exchange")
    d_ada_cols = d_ada_t.reshape(N_DEV * B, 768)
    g_w_ada = matmul(silu_c, d_ada_cols, mode="tn", name="g_w_ada")
    (g_b_cols,) = small_call(lambda a: (jnp.sum(a, axis=0, keepdims=True),), [d_ada_cols],
                             [jax.ShapeDtypeStruct((1, 768), F32)], "g_b_ada_cols")
    small_g["b_ada"] = lax.dynamic_update_slice(jnp.zeros((1, 6144), F32), g_b_cols, (0, me * 768))

    recv = recv_grads("w_in", grad_x, "grad_w_in_wait") + recv_grads("late", grad_x, "grad_late_wait")
    (small_parts,) = all_gather([_pack(small_g, _SMALL, _SMALL_ROWS)], "small_grad_gather")
    small_sum = sum_parts(small_parts, "small_grad_sum")
    sg = _unpack(small_sum, _SMALL, {**{k: shapes[k] for k, _ in _REPL}, "conv_w": (4, 1024)})

    G, Dl, NM, NV = {}, {}, {}, {}
    for k, parts in zip(big_names, recv):
        g, d, nm, nv = adamw(P[k][0], parts, M[k][0], V[k][0], f"adamw_{k}", parts=True)
        G[k], Dl[k], NM[k], NV[k] = g[None], d[None], nm[None], nv[None]
    g, d, nm, nv = adamw(w_ada[0], g_w_ada, m_w_ada[0], v_w_ada[0], "adamw_w_ada")
    G["w_ada"], Dl["w_ada"], NM["w_ada"], NV["w_ada"] = g[None], d[None], nm[None], nv[None]
    g_conv = lax.dynamic_slice(sg["conv_w"], (0, me * 128), (4, 128))
    g, d, nm, nv = adamw(conv_w[0], g_conv, m_conv_w[0], v_conv_w[0], "adamw_conv_w")
    G["conv_w"], Dl["conv_w"], NM["conv_w"], NV["conv_w"] = g[None], d[None], nm[None], nv[None]
    gslab = _pack(sg, _REPL, _REPL_ROWS)
    _, d, nm, nv = adamw(_pack(P, _REPL, _REPL_ROWS), gslab, _pack(M, _REPL, _REPL_ROWS), _pack(V, _REPL, _REPL_ROWS),
                         "adamw_small")
    rs = {k: shapes[k] for k, _ in _REPL}
    d, nm, nv = _unpack(d, _REPL, rs), _unpack(nm, _REPL, rs), _unpack(nv, _REPL, rs)
    for k, _ in _REPL:
        G[k], Dl[k], NM[k], NV[k] = sg[k], d[k], nm[k], nv[k]

    return (loss, grad_x, *[G[k] for k in names], *[Dl[k] for k in names], *[NM[k] for k in names],
            *[NV[k] for k in names])
```

```python
import functools
import math

import numpy as np
import jax
import jax.numpy as jnp
from jax import lax
from jax.experimental import pallas as pl
from jax.experimental.pallas import tpu as pltpu

F32 = jnp.float32
BF16 = jnp.bfloat16

N_DEV = 8
D_MODEL = 1024
SEQ = 2048
ATT_GROUPS = ((128, 1), (512, 4), (2048, 16))
N_ATT_HEADS = 12
ATT_BLOCK = 128
HEAD_DIM = 128
ML_HEADS = 8
ML_PAIRS = 4
ML_CHUNK = 64
N_CHUNKS = SEQ // ML_CHUNK
N_BUCKETS = 32
MAX_DISTANCE = 2048
D_FF = 4096
D_IN = 9744
EPS = 1e-6

ADAM_LR = 0.001
ADAM_B1 = 0.9
ADAM_B2 = 0.999
ADAM_EPS = 1e-08
ADAM_WD = 0.01
ADAM_STEP = 10

ATT_HEAD_COLS = 3 * HEAD_DIM
ATT_COLS = N_ATT_HEADS * ATT_HEAD_COLS
ML_PAIR_COLS = 896
ML_COLS = ML_PAIRS * ML_PAIR_COLS
GATE_COLS = 2 * D_MODEL
W_IN_SHARD = D_IN // N_DEV

VMEM_LIMIT = 56 * 1024 * 1024


def _cparams(**kw):
    return pltpu.CompilerParams(vmem_limit_bytes=VMEM_LIMIT, **kw)


_NN = ((1,), (0,))
_NT = ((1,), (1,))
_TN = ((0,), (0,))


def _mxu(a, b, dims):
    return lax.dot_general(a.astype(BF16), b.astype(BF16), (dims, ((), ())), preferred_element_type=F32)


@jax.custom_vjp
def bdot_nn(a, b):
    return _mxu(a, b, _NN)


def _nn_fwd(a, b):
    return _mxu(a, b, _NN), (a, b)


def _nn_bwd(res, g):
    a, b = res
    return _mxu(g, b, _NT), _mxu(a, g, _TN)


bdot_nn.defvjp(_nn_fwd, _nn_bwd)


@jax.custom_vjp
def bdot_nt(a, b):
    return _mxu(a, b, _NT)


def _nt_fwd(a, b):
    return _mxu(a, b, _NT), (a, b)


def _nt_bwd(res, g):
    a, b = res
    return _mxu(g, b, _NN), _mxu(g, a, _TN)


bdot_nt.defvjp(_nt_fwd, _nt_bwd)


def _doth(a, b, dims=_NN):
    return lax.dot_general(a, b, (dims, ((), ())), precision=lax.Precision.HIGHEST, preferred_element_type=F32)


def _rms(x):
    return x * lax.rsqrt(jnp.mean(x * x, axis=-1, keepdims=True) + EPS)


def _pick(n, cands):
    for t in cands:
        if n % t == 0:
            return t
    raise ValueError(f"no tile for {n}")


MM_TILE_M = (1024, 512, 256, 128, 64, 32, 16, 8)
MM_TILE_N = (2048, 1792, 1536, 1024, 768, 512, 256, 128)
MM_TILE_K = (2048, 1792, 1536, 1024, 512, 256, 128, 64, 32)

def matmul(a, b, *, mode, name, out_dtypes=(F32,), epi=None, extras=(), after=()):
    if mode == "nn":
        (M, K), (K2, N) = a.shape, b.shape
    elif mode == "nt":
        (M, K), (N, K2) = a.shape, b.shape
    else:
        (K, M), (K2, N) = a.shape, b.shape
    assert K == K2, (a.shape, b.shape, mode)
    tm = _pick(M, MM_TILE_M)
    tn = _pick(N, MM_TILE_N)
    tk = _pick(K, MM_TILE_K)
    nk = K // tk
    n_ex = len(extras)
    n_out = len(out_dtypes)
    dims = {"nn": _NN, "nt": _NT, "tn": _TN}[mode]

    def finish(r, ex_refs, out_refs):
        outs = epi(r, *[e[...] for e in ex_refs]) if epi is not None else (r,)
        for o_ref, o in zip(out_refs, outs):
            o_ref[...] = o.astype(o_ref.dtype)

    def body(*refs):
        a_ref, b_ref = refs[0], refs[1]
        ex_refs = refs[2:2 + n_ex]
        out_refs = refs[2 + n_ex + len(after):2 + n_ex + len(after) + n_out]
        if nk == 1:
            finish(_mxu(a_ref[...], b_ref[...], dims), ex_refs, out_refs)
            return
        acc = refs[2 + n_ex + len(after) + n_out]
        k = pl.program_id(2)

        @pl.when(k == 0)
        def _():
            acc[...] = jnp.zeros_like(acc)

        acc[...] += _mxu(a_ref[...], b_ref[...], dims)

        @pl.when(k == nk - 1)
        def _():
            finish(acc[...], ex_refs, out_refs)

    if mode == "nn":
        a_spec = pl.BlockSpec((tm, tk), lambda i, j, k: (i, k))
        b_spec = pl.BlockSpec((tk, tn), lambda i, j, k: (k, j))
    elif mode == "nt":
        a_spec = pl.BlockSpec((tm, tk), lambda i, j, k: (i, k))
        b_spec = pl.BlockSpec((tn, tk), lambda i, j, k: (j, k))
    else:
        a_spec = pl.BlockSpec((tk, tm), lambda i, j, k: (k, i))
        b_spec = pl.BlockSpec((tk, tn), lambda i, j, k: (k, j))
    o_spec = pl.BlockSpec((tm, tn), lambda i, j, k: (i, j))
    res = pl.pallas_call(
        body,
        name=name,
        grid=(M // tm, N // tn, nk),
        in_specs=[a_spec, b_spec] + [o_spec] * n_ex + [pl.BlockSpec(memory_space=pl.ANY)] * len(after),
        out_specs=[o_spec] * n_out,
        out_shape=[jax.ShapeDtypeStruct((M, N), dt) for dt in out_dtypes],
        scratch_shapes=[pltpu.VMEM((tm, tn), F32)] if nk > 1 else [],
        compiler_params=_cparams(),
    )(a, b, *extras, *after)
    return res[0] if n_out == 1 else tuple(res)


ROW_MM_TILE = 512


def row_matmul(a, b, *, mode, name, extras, outs, epi):
    (M, K) = a.shape
    N = b.shape[1] if mode == "nn" else b.shape[0]
    tm = ROW_MM_TILE
    tk = _pick(K, MM_TILE_K)
    nk = K // tk
    n_ex, n_out = len(extras), len(outs)

    def body(*refs):
        a_ref, b_ref = refs[0], refs[1]
        ex_refs = refs[2:2 + n_ex]
        out_refs = refs[2 + n_ex:2 + n_ex + n_out]
        i = pl.program_id(0)
        dims = _NN if mode == "nn" else _NT
        if nk == 1:
            epi(_mxu(a_ref[...], b_ref[...], dims), i, ex_refs, out_refs)
            return
        acc = refs[2 + n_ex + n_out]
        k = pl.program_id(1)

        @pl.when(k == 0)
        def _():
            acc[...] = jnp.zeros_like(acc)

        acc[...] += _mxu(a_ref[...], b_ref[...], dims)

        @pl.when(k == nk - 1)
        def _():
            epi(acc[...], i, ex_refs, out_refs)

    def lift(index_map):
        return lambda i, k: index_map(i)

    b_spec = pl.BlockSpec((tk, N), lambda i, k: (k, 0)) if mode == "nn" else pl.BlockSpec((N, tk), lambda i, k: (0, k))
    res = pl.pallas_call(
        body, name=name, grid=(M // tm, nk),
        in_specs=[pl.BlockSpec((tm, tk), lambda i, k: (i, k)), b_spec]
        + [pl.BlockSpec(blk, lift(im)) for _, blk, im in extras],
        out_specs=[pl.BlockSpec(blk, lift(im)) for _, _, blk, im in outs],
        out_shape=[jax.ShapeDtypeStruct(shape, dt) for shape, dt, _, _ in outs],
        scratch_shapes=[pltpu.VMEM((tm, N), F32)] if nk > 1 else [],
        compiler_params=_cparams(),
    )(a, b, *[e[0] for e in extras])
    return tuple(res)


def _rows(arr):
    return (arr, (ROW_MM_TILE, arr.shape[1]), lambda i: (i, 0))


def _rows_out(T, dtype):
    return ((T, D_MODEL), dtype, (ROW_MM_TILE, D_MODEL), lambda i: (i, 0))


def _per_seq(arr):
    return (arr, (1, 1, D_MODEL), lambda i: (i // (SEQ // ROW_MM_TILE), 0, 0))


def _per_seq_out(B):
    return ((B, 1, D_MODEL), F32, (1, 1, D_MODEL), lambda i: (i // (SEQ // ROW_MM_TILE), 0, 0))


def _first_tile_of_seq(i):
    return i % (SEQ // ROW_MM_TILE) == 0


def small_call(fn, inputs, out_shapes, name):
    n_in = len(inputs)

    def body(*refs):
        outs = fn(*[r[...] for r in refs[:n_in]])
        for o_ref, o in zip(refs[n_in:], outs):
            o_ref[...] = o.astype(o_ref.dtype)

    res = pl.pallas_call(body, name=name, out_shape=list(out_shapes), compiler_params=_cparams())(*inputs)
    return tuple(res)


ROW_TILE = 512


def _modnorm(x, g, scale, shift):
    return _rms(x) * g * (1.0 + scale) + shift


def _row_spec(width):
    return pl.BlockSpec((1, ROW_TILE, width), lambda b, i: (b, i, 0))


def _mod_spec():
    return pl.BlockSpec((1, 1, D_MODEL), lambda b, i: (b, 0, 0))


def _vec_spec():
    return pl.BlockSpec((1, D_MODEL), lambda b, i: (0, 0))


def modnorm_fwd(x, g, scale, shift, name):
    B, S, D = x.shape

    def body(x_ref, g_ref, sc_ref, sh_ref, u_ref):
        u_ref[0] = _modnorm(x_ref[0], g_ref[...], sc_ref[0], sh_ref[0]).astype(BF16)

    return pl.pallas_call(
        body, name=name, grid=(B, S // ROW_TILE),
        in_specs=[_row_spec(D), _vec_spec(), _mod_spec(), _mod_spec()],
        out_specs=_row_spec(D),
        out_shape=jax.ShapeDtypeStruct((B, S, D), BF16),
        compiler_params=_cparams(),
    )(x, g, scale, shift)


def _gain_spec(g):
    return (g, (1, D_MODEL), lambda i: (0, 0))


def out_proj_resid_modnorm(z, w_out, x, gate, g, scale, shift, name):
    T = z.shape[0]

    def epi(acc, i, ex, out):
        x_ref, gt_ref, g_ref, sc_ref, sh_ref = ex
        y_ref, x1_ref, u_ref = out
        y_ref[...] = acc
        x1 = x_ref[...] + gt_ref[0] * acc
        x1_ref[...] = x1
        u_ref[...] = _modnorm(x1, g_ref[...], sc_ref[0], sh_ref[0]).astype(BF16)

    return row_matmul(z, w_out, mode="nn", name=name,
                      extras=[_rows(x), _per_seq(gate), _gain_spec(g), _per_seq(scale), _per_seq(shift)],
                      outs=[_rows_out(T, F32), _rows_out(T, F32), _rows_out(T, BF16)], epi=epi)


def ff2_loss(hdn, w_ff2, x1, gate, target, name):
    T = hdn.shape[0]

    def epi(acc, i, ex, out):
        x_ref, gt_ref, t_ref = ex
        dx_ref, dffo_ref, loss_ref, dg_ref = out

        @pl.when(i == 0)
        def _():
            loss_ref[...] = jnp.zeros_like(loss_ref)

        @pl.when(_first_tile_of_seq(i))
        def _():
            dg_ref[...] = jnp.zeros_like(dg_ref)

        err = x_ref[...] + gt_ref[0] * acc - t_ref[...]
        dx = err * (1.0 / D_MODEL)
        dx_ref[...] = dx
        dffo_ref[...] = (gt_ref[0] * dx).astype(BF16)
        loss_ref[...] += 0.5 * jnp.sum(jnp.mean(err * err, axis=-1, keepdims=True), axis=0, keepdims=True)
        dg_ref[0] += jnp.sum(dx * acc, axis=0, keepdims=True)

    return row_matmul(hdn, w_ff2, mode="nn", name=name,
                      extras=[_rows(x1), _per_seq(gate), _rows(target)],
                      outs=[_rows_out(T, F32), _rows_out(T, BF16), ((1, 1), F32, (1, 1), lambda i: (0, 0)),
                            _per_seq_out(T // SEQ)], epi=epi)


def d_u_modnorm_bwd(a, w, du_prev, x, g, scale, shift, dx_res, y, gate, name):
    T = a.shape[0]
    B = T // SEQ
    n_prev, resid = int(du_prev is not None), y is not None

    def epi(acc, i, ex, out):
        x_ref, g_ref, sc_ref, sh_ref, dr_ref = ex[n_prev:n_prev + 5]
        dx_ref, dg_ref, dsc_ref, dsh_ref = out[:4]

        @pl.when(i == 0)
        def _():
            dg_ref[...] = jnp.zeros_like(dg_ref)

        @pl.when(_first_tile_of_seq(i))
        def _():
            for r in out[2:4] + out[5:]:
                r[...] = jnp.zeros_like(r)

        du = acc + ex[0][...] if n_prev else acc
        _, vjp = jax.vjp(_modnorm, x_ref[...], g_ref[...], sc_ref[0], sh_ref[0])
        dxn, dg, dsc, dsh = vjp(du)
        dx = dxn + dr_ref[...]
        dx_ref[...] = dx
        dg_ref[...] += dg
        dsc_ref[0] += dsc
        dsh_ref[0] += dsh
        if resid:
            y_ref, gt_ref = ex[n_prev + 5:]
            out[4][...] = (gt_ref[0] * dx).astype(BF16)
            out[5][0] += jnp.sum(dx * y_ref[...], axis=0, keepdims=True)

    extras = ([_rows(du_prev)] if n_prev else []) + [_rows(x), _gain_spec(g), _per_seq(scale), _per_seq(shift),
                                                     _rows(dx_res)] + ([_rows(y), _per_seq(gate)] if resid else [])
    outs = [_rows_out(T, F32), ((1, D_MODEL), F32, (1, D_MODEL), lambda i: (0, 0)), _per_seq_out(B), _per_seq_out(B)]
    outs += [_rows_out(T, BF16), _per_seq_out(B)] if resid else []
    return row_matmul(a, w, mode="nt", name=name, extras=extras, outs=outs, epi=epi)


def _bucket_table(dilation):
    i = np.arange(ATT_BLOCK)[:, None]
    j = np.arange(2 * ATT_BLOCK)[None, :]
    delta = ATT_BLOCK + i - j
    dist = np.maximum(delta, 0) * dilation
    max_exact = N_BUCKETS // 2
    d = np.maximum(dist, max_exact).astype(np.float32)
    large = max_exact + (np.log(d / np.float32(max_exact)) / np.float32(math.log(MAX_DISTANCE / max_exact))
                         * np.float32(N_BUCKETS - max_exact)).astype(np.int32)
    large = np.minimum(large, N_BUCKETS - 1)
    return np.where(dist < max_exact, dist, large).astype(np.int32)


def _bucket_onehot(dilation):
    bt = jnp.asarray(_bucket_table(dilation).reshape(1, -1))
    return (bt == jnp.arange(N_BUCKETS, dtype=jnp.int32)[:, None]).astype(F32)


def bias_expand(rel_t, onehot, name):
    def fn(r, oh):
        return (_doth(r, oh),)
    return small_call(fn, [rel_t, onehot], [jax.ShapeDtypeStruct((rel_t.shape[0], onehot.shape[1]), F32)], name)[0]


def bias_reduce(dbias_flat, onehot, name):
    def fn(db, oh):
        return (_doth(db, oh, _NT),)
    return small_call(fn, [dbias_flat, onehot], [jax.ShapeDtypeStruct((dbias_flat.shape[0], N_BUCKETS), F32)], name)[0]


def _qk_norm(x, g):
    return _rms(x) * g


def _masked_bias(bias):
    i = lax.broadcasted_iota(jnp.int32, (ATT_BLOCK, 2 * ATT_BLOCK), 0)
    j = lax.broadcasted_iota(jnp.int32, (ATT_BLOCK, 2 * ATT_BLOCK), 1)
    bm = jnp.where(jnp.logical_and(j >= i, j <= i + ATT_BLOCK), bias, -jnp.inf)
    return bm, bm[:, ATT_BLOCK:]


def _attn_tile(qn, kn, v, bias):
    s = bdot_nt(qn, kn) * (HEAD_DIM ** -0.5) + bias
    m = lax.stop_gradient(jnp.max(s, axis=-1, keepdims=True))
    p = jnp.exp(s - m)
    l = jnp.sum(p, axis=-1, keepdims=True)
    o = bdot_nn(p, v) / l
    lse = jnp.broadcast_to(m + jnp.log(l), (ATT_BLOCK, HEAD_DIM))
    return o, lse


def _attn_tiles(dilation):
    nb = SEQ // dilation // ATT_BLOCK
    return [(r, n) for r in range(dilation) for n in range(nb)]


def _attn_rows(r, n, dilation, nblk=1):
    if dilation == 1:
        return pl.ds(r + n * ATT_BLOCK, nblk * ATT_BLOCK)
    return pl.ds(r + n * ATT_BLOCK * dilation, nblk * ATT_BLOCK, stride=dilation)


_QL, _KL, _VL = slice(0, 128), slice(128, 256), slice(256, 384)


def _qkv_specs(hb):
    return [pl.BlockSpec((None, SEQ, HEAD_DIM), functools.partial(lambda b, h, j: (b, 0, 3 * (hb + h) + j), j=j))
            for j in range(3)]


def attn_fwd(pa, bias, qg, kg, group, name):
    B = pa.shape[0]
    dilation = ATT_GROUPS[group][1]
    hb = group * 4

    def body(q_ref, k_ref, v_ref, b_ref, qg_ref, kg_ref, o_ref, l_ref, qn_s, kn_s):
        qn_s[...] = _qk_norm(q_ref[...], qg_ref[...])
        kn_s[...] = _qk_norm(k_ref[...], kg_ref[...])
        bias_all, bias_first = _masked_bias(b_ref[0])
        for (r, n) in _attn_tiles(dilation):
            rows = _attn_rows(r, n, dilation)
            if n == 0:
                krows, bias_t = rows, bias_first
            else:
                krows, bias_t = _attn_rows(r, n - 1, dilation, 2), bias_all
            o, lse = _attn_tile(qn_s[rows, :], kn_s[krows, :], v_ref[krows, :], bias_t)
            o_ref[rows, :] = o
            l_ref[rows, :] = lse

    head_out = pl.BlockSpec((None, SEQ, HEAD_DIM), lambda b, h: (b, 0, h))
    return pl.pallas_call(
        body, name=name, grid=(B, 4),
        in_specs=_qkv_specs(hb) + [
                  pl.BlockSpec((1, ATT_BLOCK, 2 * ATT_BLOCK), lambda b, h: (hb + h, 0, 0)),
                  pl.BlockSpec((1, HEAD_DIM), lambda b, h: (0, 0)),
                  pl.BlockSpec((1, HEAD_DIM), lambda b, h: (0, 0))],
        out_specs=[head_out, head_out],
        out_shape=[jax.ShapeDtypeStruct((B, SEQ, 512), F32), jax.ShapeDtypeStruct((B, SEQ, 512), F32)],
        scratch_shapes=[pltpu.VMEM((SEQ, HEAD_DIM), F32)] * 2,
        compiler_params=_cparams(),
    )(pa, pa, pa, bias, qg, kg)


def attn_bwd(pa, bias, qg, kg, do, dlse, dpa, group, name):
    B = pa.shape[0]
    dilation = ATT_GROUPS[group][1]
    hb = group * 4

    def body(q_ref, k_ref, v_ref, b_ref, qg_ref, kg_ref, do_ref, dl_ref, dpa_in,
             dp_ref, db_ref, dqg_ref, dkg_ref, qn_s, kn_s, dq_s, dk_s, dv_s):
        del dpa_in
        h_id = pl.program_id(1)

        @pl.when(jnp.logical_and(pl.program_id(0) == 0, h_id == 0))
        def _():
            db_ref[...] = jnp.zeros_like(db_ref)
            dqg_ref[...] = jnp.zeros_like(dqg_ref)
            dkg_ref[...] = jnp.zeros_like(dkg_ref)

        dk_s[...] = jnp.zeros_like(dk_s)
        dv_s[...] = jnp.zeros_like(dv_s)
        qn_s[...] = _qk_norm(q_ref[...], qg_ref[...])
        kn_s[...] = _qk_norm(k_ref[...], kg_ref[...])
        bias_all, bias_first = _masked_bias(b_ref[0])
        for (r, n) in _attn_tiles(dilation):
            rows = _attn_rows(r, n, dilation)
            if n == 0:
                krows, bias_t = rows, bias_first
            else:
                krows, bias_t = _attn_rows(r, n - 1, dilation, 2), bias_all
            _, vjp = jax.vjp(_attn_tile, qn_s[rows, :], kn_s[krows, :], v_ref[krows, :], bias_t)
            dqn, dkn, dv, dbias = vjp((do_ref[rows, :], dl_ref[rows, :]))
            dq_s[rows, :] = dqn
            dk_s[krows, :] += dkn
            dv_s[krows, :] += dv
            if n == 0:
                db_ref[h_id, :, ATT_BLOCK:] += dbias
            else:
                db_ref[h_id] += dbias
        for x_ref, g_ref, d_s, dg_ref, lanes in ((q_ref, qg_ref, dq_s, dqg_ref, _QL), (k_ref, kg_ref, dk_s, dkg_ref, _KL)):
            _, vjp = jax.vjp(_qk_norm, x_ref[...], g_ref[...])
            dx, dg = vjp(d_s[...])
            dp_ref[0, :, lanes] = dx.astype(BF16)
            dg_ref[...] += dg
        dp_ref[0, :, _VL] = dv_s[...].astype(BF16)

    const2 = lambda b, h: (0, 0)
    head_in = pl.BlockSpec((None, SEQ, HEAD_DIM), lambda b, h: (b, 0, h))
    head_blk = pl.BlockSpec((1, SEQ, ATT_HEAD_COLS), lambda b, h: (b, 0, hb + h))
    return pl.pallas_call(
        body, name=name, grid=(B, 4),
        in_specs=_qkv_specs(hb) + [
                  pl.BlockSpec((1, ATT_BLOCK, 2 * ATT_BLOCK), lambda b, h: (hb + h, 0, 0)),
                  pl.BlockSpec((1, HEAD_DIM), const2), pl.BlockSpec((1, HEAD_DIM), const2),
                  head_in, head_in,
                  pl.BlockSpec(memory_space=pl.ANY)],
        out_specs=[head_blk,
                   pl.BlockSpec((4, ATT_BLOCK, 2 * ATT_BLOCK), lambda b, h: (0, 0, 0)),
                   pl.BlockSpec((1, HEAD_DIM), const2), pl.BlockSpec((1, HEAD_DIM), const2)],
        out_shape=[jax.ShapeDtypeStruct(dpa.shape, BF16),
                   jax.ShapeDtypeStruct((4, ATT_BLOCK, 2 * ATT_BLOCK), F32),
                   jax.ShapeDtypeStruct((1, HEAD_DIM), F32), jax.ShapeDtypeStruct((1, HEAD_DIM), F32)],
        scratch_shapes=[pltpu.VMEM((SEQ, HEAD_DIM), F32)] * 5,
        input_output_aliases={8: 0},
        compiler_params=_cparams(),
    )(pa, pa, pa, bias, qg, kg, do, dlse, dpa)


def _merge(o0, o1, o2, l0, l1, l2):
    mx = jnp.maximum(jnp.maximum(l0, l1), l2)
    e0, e1, e2 = jnp.exp(l0 - mx), jnp.exp(l1 - mx), jnp.exp(l2 - mx)
    den = e0 + e1 + e2
    return (e0 / den) * o0 + (e1 / den) * o1 + (e2 / den) * o2


def merge_fwd(os_, ls_, name):
    B = os_[0].shape[0]

    def body(o0, o1, o2, l0, l1, l2, a_ref):
        a_ref[0] = _merge(o0[0], o1[0], o2[0], l0[0], l1[0], l2[0]).astype(BF16)

    return pl.pallas_call(
        body, name=name, grid=(B, SEQ // ROW_TILE),
        in_specs=[_row_spec(512)] * 6, out_specs=_row_spec(512),
        out_shape=jax.ShapeDtypeStruct((B, SEQ, 512), BF16),
        compiler_params=_cparams(),
    )(*os_, *ls_)


def merge_bwd(os_, ls_, datt, name):
    B = os_[0].shape[0]

    def body(o0, o1, o2, l0, l1, l2, da_ref, *outs):
        _, vjp = jax.vjp(_merge, o0[0], o1[0], o2[0], l0[0], l1[0], l2[0])
        for o_ref, g in zip(outs, vjp(da_ref[0])):
            o_ref[0] = g

    return pl.pallas_call(
        body, name=name, grid=(B, SEQ // ROW_TILE),
        in_specs=[_row_spec(512)] * 7, out_specs=[_row_spec(512)] * 6,
        out_shape=[jax.ShapeDtypeStruct((B, SEQ, 512), F32)] * 6,
        compiler_params=_cparams(),
    )(*os_, *ls_, datt)


def _gate_mix(ga, gm, ya, ym):
    return jax.nn.sigmoid(ga) * ya + jax.nn.sigmoid(gm) * ym


def gate_fwd(pg, ya, ym, name):
    B = pg.shape[0]

    def body(ga, gm, ya_ref, ym_ref, z_ref):
        z_ref[0] = _gate_mix(ga[0], gm[0], ya_ref[0], ym_ref[0]).astype(BF16)

    return pl.pallas_call(
        body, name=name, grid=(B, SEQ // ROW_TILE),
        in_specs=[pl.BlockSpec((1, ROW_TILE, D_MODEL), lambda b, i: (b, i, 0)),
                  pl.BlockSpec((1, ROW_TILE, D_MODEL), lambda b, i: (b, i, 1)),
                  _row_spec(D_MODEL), _row_spec(D_MODEL)],
        out_specs=_row_spec(D_MODEL),
        out_shape=jax.ShapeDtypeStruct((B, SEQ, D_MODEL), BF16),
        compiler_params=_cparams(),
    )(pg, pg, ya, ym)


def gate_bwd(pg, ya, ym, dz, name):
    B = pg.shape[0]

    def body(ga, gm, ya_ref, ym_ref, dz_ref, dpg_ref, dya_ref, dym_ref):
        _, vjp = jax.vjp(_gate_mix, ga[0], gm[0], ya_ref[0], ym_ref[0])
        dga, dgm, dya, dym = vjp(dz_ref[0])
        dpg_ref[0, :, :D_MODEL] = dga.astype(BF16)
        dpg_ref[0, :, D_MODEL:] = dgm.astype(BF16)
        dya_ref[0] = dya.astype(BF16)
        dym_ref[0] = dym.astype(BF16)

    return pl.pallas_call(
        body, name=name, grid=(B, SEQ // ROW_TILE),
        in_specs=[pl.BlockSpec((1, ROW_TILE, D_MODEL), lambda b, i: (b, i, 0)),
                  pl.BlockSpec((1, ROW_TILE, D_MODEL), lambda b, i: (b, i, 1)),
                  _row_spec(D_MODEL), _row_spec(D_MODEL), _row_spec(D_MODEL)],
        out_specs=[_row_spec(GATE_COLS), _row_spec(D_MODEL), _row_spec(D_MODEL)],
        out_shape=[jax.ShapeDtypeStruct((B, SEQ, GATE_COLS), BF16),
                   jax.ShapeDtypeStruct((B, SEQ, D_MODEL), BF16), jax.ShapeDtypeStruct((B, SEQ, D_MODEL), BF16)],
        compiler_params=_cparams(),
    )(pg, pg, ya, ym, dz)


def _log_sigmoid(x):
    return jnp.minimum(x, 0.0) - jnp.log(1.0 + jnp.exp(-jnp.abs(x)))


def _head_mask(e):
    lane = lax.broadcasted_iota(jnp.int32, (1, 128), 1)
    return jnp.logical_and(lane >= e * 64, lane < (e + 1) * 64).astype(F32)


def _bmxu(a, b, ca, cb):
    return lax.dot_general(a.astype(BF16), b.astype(BF16), (((ca,), (cb,)), ((0,), (0,))), preferred_element_type=F32)


@jax.custom_vjp
def cdot_nt(a, b):
    return _bmxu(a, b, 2, 2)


cdot_nt.defvjp(lambda a, b: (_bmxu(a, b, 2, 2), (a, b)),
               lambda res, g: (_bmxu(g, res[1], 2, 1), _bmxu(g, res[0], 1, 1)))


@jax.custom_vjp
def cdot_nn(a, b):
    return _bmxu(a, b, 2, 1)


cdot_nn.defvjp(lambda a, b: (_bmxu(a, b, 2, 1), (a, b)),
               lambda res, g: (_bmxu(g, res[1], 2, 2), _bmxu(res[0], g, 1, 1)))


@jax.custom_vjp
def cdot_tn(a, b):
    return _bmxu(a, b, 1, 1)


cdot_tn.defvjp(lambda a, b: (_bmxu(a, b, 1, 1), (a, b)),
               lambda res, g: (_bmxu(res[1], g, 2, 2), _bmxu(res[0], g, 2, 1)))


def _top_bits(x):
    return lax.bitcast_convert_type(lax.bitcast_convert_type(x, jnp.uint32) & jnp.uint32(0xFFFF0000), F32)


def _split3(x):
    hi = _top_bits(x)
    r = x - hi
    mid = _top_bits(r)
    return hi, mid, r - mid


def _parts_in_lanes(col):
    hi, mid, lo = _split3(col)
    lane = lax.broadcasted_iota(jnp.int32, (1, 1, 8), 2)
    return jnp.where(lane == 0, hi, jnp.where(lane == 1, mid, jnp.where(lane == 2, lo, 0.0)))


def _parts_in_rows(row):
    hi, mid, lo = _split3(row)
    sub = lax.broadcasted_iota(jnp.int32, (1, 8, 1), 1)
    return jnp.where(sub == 0, hi, jnp.where(sub == 1, mid, jnp.where(sub == 2, lo, 0.0)))


def _chunk_matrix(kind, c):
    ri = lax.broadcasted_iota(jnp.int32, (c, ML_CHUNK, ML_CHUNK), 1)
    ci = lax.broadcasted_iota(jnp.int32, (c, ML_CHUNK, ML_CHUNK), 2)
    return {"eye": ri == ci, "lower": ri >= ci, "upper": ri <= ci}[kind].astype(F32)


def _col_col(kind, col):
    out = _bmxu(_chunk_matrix(kind, col.shape[0]), _parts_in_lanes(col), 2, 1)
    return jnp.sum(out, axis=-1, keepdims=True)


def _col_row(col):
    out = _bmxu(_parts_in_lanes(col), _chunk_matrix("eye", col.shape[0]), 1, 1)
    return jnp.sum(out, axis=1, keepdims=True)


def _row_col(row):
    out = _bmxu(_chunk_matrix("eye", row.shape[0]), _parts_in_rows(row), 2, 2)
    return jnp.sum(out, axis=-1, keepdims=True)


@jax.custom_vjp
def chunk_cumsum(col):
    return _col_col("lower", col)


chunk_cumsum.defvjp(lambda col: (_col_col("lower", col), None), lambda _, g: (_col_col("upper", g),))


@jax.custom_vjp
def col_to_row(col):
    return _col_row(col)


col_to_row.defvjp(lambda col: (_col_row(col), None), lambda _, g: (_row_col(g),))


def _ml_intra(q2, k2, v, ifb, *, e):
    c, L = N_CHUNKS, ML_CHUNK
    hm = _head_mask(e)
    q3 = (q2 * hm).reshape(c, L, 128)
    k3 = (k2 * hm).reshape(c, L, 128)
    v3 = v.reshape(c, L, 128)
    if3 = ifb.reshape(c, L, 128)
    lanes = lax.broadcasted_iota(jnp.int32, (c, L, 128), 2)
    li = jnp.sum(jnp.where(lanes == e, if3, 0.0), axis=-1, keepdims=True)
    lf = _log_sigmoid(jnp.sum(jnp.where(lanes == 2 + e, if3, 0.0), axis=-1, keepdims=True))
    b = chunk_cumsum(lf)
    last = lax.broadcasted_iota(jnp.int32, (1, L, 1), 1) == L - 1
    b_end = jnp.sum(jnp.where(last, b, 0.0), axis=1, keepdims=True)
    causal = lax.broadcasted_iota(jnp.int32, (L, L), 0) >= lax.broadcasted_iota(jnp.int32, (L, L), 1)
    Dm = jnp.where(causal, b + col_to_row(li - b), -jnp.inf)
    mD = lax.stop_gradient(jnp.max(Dm, axis=-1, keepdims=True))
    P0 = cdot_nt(q3, k3) * jnp.exp(Dm - mD)
    H0 = cdot_nn(P0, v3)
    r0 = jnp.sum(P0, axis=-1, keepdims=True)
    g = b_end - b + li
    mg = lax.stop_gradient(jnp.max(g, axis=1, keepdims=True))
    kw = jnp.exp(g - mg) * k3
    return H0, r0, cdot_tn(kw, v3), jnp.sum(kw, axis=1, keepdims=True), b, b_end, mD, mg


def _ml_inter(q2, mo, gn, H0, r0, b, C_in, n_in, *, mD, m_in, e):
    c, L = N_CHUNKS, ML_CHUNK
    q3 = (q2 * _head_mask(e)).reshape(c, L, 128)
    a = b + m_in
    m_t = lax.stop_gradient(jnp.maximum(a, mD))
    c1 = jnp.exp(mD - m_t)
    c2 = jnp.exp(a - m_t)
    num = c1 * H0 + c2 * cdot_nn(q3, C_in)
    nq = c1 * r0 + c2 * jnp.sum(q3 * n_in, axis=-1, keepdims=True)
    h = num / jnp.maximum(jnp.abs(nq), jnp.exp(-m_t))
    hg = _rms(h) * gn * jax.nn.sigmoid(mo.reshape(c, L, 128))
    return hg.reshape(c * L, 128)


def _state_sweep(U_s, un_s, be_s, mg_s, Cin_s, nin_s, min_s, al_s, bt_s):
    def step(j, carry):
        C, n, m = carry
        Cin_s[j], nin_s[j], min_s[j] = C, n, m
        m_out = jnp.maximum(be_s[j] + m, mg_s[j])
        al = jnp.exp(be_s[j] + m - m_out)
        bt = jnp.exp(mg_s[j] - m_out)
        al_s[j], bt_s[j] = al, bt
        return al * C + bt * U_s[j], al * n + bt * un_s[j], m_out

    lax.fori_loop(0, N_CHUNKS, step, (jnp.zeros((128, 128), F32), jnp.zeros((1, 128), F32), jnp.zeros((1, 1), F32)))


def _state_sweep_bwd(U_s, un_s, dbe_s, Cin_s, nin_s, dCp_s, dnp_s, al_s, bt_s):
    def step(t, carry):
        j = N_CHUNKS - 1 - t
        dC, dn = carry
        al, bt = al_s[j], bt_s[j]
        U_s[j] = bt * dC
        un_s[j] = bt * dn
        dal = jnp.sum(jnp.sum(dC * Cin_s[j], axis=1, keepdims=True), axis=0, keepdims=True) \
            + jnp.sum(dn * nin_s[j], axis=1, keepdims=True)
        dbe_s[j] = dal * al
        return dCp_s[j] + al * dC, dnp_s[j] + al * dn

    lax.fori_loop(0, N_CHUNKS, step, (jnp.zeros((128, 128), F32), jnp.zeros((1, 128), F32)))


def _state_scratch():
    c = N_CHUNKS
    return [pltpu.VMEM((c, 128, 128), F32), pltpu.VMEM((c, 1, 128), F32), pltpu.VMEM((c, 1, 1), F32),
            pltpu.VMEM((c, 1, 1), F32),
            pltpu.VMEM((c, 128, 128), F32), pltpu.VMEM((c, 1, 128), F32), pltpu.VMEM((c, 1, 1), F32),
            pltpu.VMEM((c, 1, 1), F32), pltpu.VMEM((c, 1, 1), F32)]


def _shift_down(x, s):
    if s == 0:
        return x
    rows = lax.broadcasted_iota(jnp.int32, x.shape, 0)
    return jnp.where(rows >= s, pltpu.roll(x, s, 0), 0.0)


def _shift_up(x, s):
    if s == 0:
        return x
    S = x.shape[0]
    rows = lax.broadcasted_iota(jnp.int32, x.shape, 0)
    return jnp.where(rows < S - s, pltpu.roll(x, S - s, 0), 0.0)


def _conv_pre(x, cw, cb):
    y = cb + cw[3:4, :] * x
    for j in range(3):
        y = y + cw[j:j + 1, :] * _shift_down(x, 3 - j)
    return y


def _conv_bwd(x, cw, dpre):
    dx = cw[3:4, :] * dpre
    dcw = [None] * 4
    dcw[3] = jnp.sum(dpre * x, axis=0, keepdims=True)
    for j in range(3):
        dx = dx + cw[j:j + 1, :] * _shift_up(dpre, 3 - j)
        dcw[j] = jnp.sum(dpre * _shift_down(x, 3 - j), axis=0, keepdims=True)
    return dx, dcw, jnp.sum(dpre, axis=0, keepdims=True)


def _silu(z):
    return z * jax.nn.sigmoid(z)


def _dsilu(z):
    s = jax.nn.sigmoid(z)
    return s * (1.0 + z * (1.0 - s))


_ML_Q, _ML_K = slice(0, 128), slice(128, 256)
_ML_IF = slice(768, 896)


def _ml_v(e):
    return slice(256 + e * 128, 384 + e * 128)


def _ml_o(e):
    return slice(512 + e * 128, 640 + e * 128)


def _ml_specs():
    pair = lambda b, p: (b, 0, p)
    return [pl.BlockSpec((1, SEQ, ML_PAIR_COLS), pair),
            pl.BlockSpec((1, 4, 128), lambda b, p: (p, 0, 0)),
            pl.BlockSpec((1, 4, 128), lambda b, p: (4 + p, 0, 0)),
            pl.BlockSpec((1, 1, 128), lambda b, p: (p, 0, 0)),
            pl.BlockSpec((1, 1, 128), lambda b, p: (4 + p, 0, 0)),
            pl.BlockSpec((1, 1, 128), lambda b, p: (p, 0, 0)),
            pl.BlockSpec((1, 1, 256), lambda b, p: (p, 0, 0))]


def mlstm_fwd(pm, cw8, cb8, bifp, gn4, name):
    B = pm.shape[0]

    def body(p_ref, cwq, cwk, cbq, cbk, bif_ref, gn_ref, hg_ref, *st):
        U_s, un_s, be_s, mg_s, Cin_s, nin_s, min_s, al_s, bt_s = st
        qc = _silu(_conv_pre(p_ref[0, :, _ML_Q], cwq[0], cbq[0]))
        kc = _silu(_conv_pre(p_ref[0, :, _ML_K], cwk[0], cbk[0])) * (64 ** -0.5)
        ifb = p_ref[0, :, _ML_IF] + bif_ref[0]
        for e in range(2):
            lanes = slice(e * 128, (e + 1) * 128)
            H0, r0, U, un, b, b_end, mD, mg = _ml_intra(qc, kc, p_ref[0, :, _ml_v(e)], ifb, e=e)
            U_s[...], un_s[...], be_s[...], mg_s[...] = U, un, b_end, mg
            _state_sweep(*st)
            hg = _ml_inter(qc, p_ref[0, :, _ml_o(e)], gn_ref[0, :, lanes], H0, r0, b, Cin_s[...], nin_s[...],
                           mD=mD, m_in=min_s[...], e=e)
            hg_ref[0, :, lanes] = hg.astype(BF16)

    return pl.pallas_call(
        body, name=name, grid=(B, ML_PAIRS),
        in_specs=_ml_specs(),
        out_specs=pl.BlockSpec((1, SEQ, 256), lambda b, p: (b, 0, p)),
        out_shape=jax.ShapeDtypeStruct((B, SEQ, D_MODEL), BF16),
        scratch_shapes=_state_scratch(),
        compiler_params=_cparams(),
    )(pm, cw8, cw8, cb8, cb8, bifp, gn4)


def mlstm_bwd(pm, cw8, cb8, bifp, gn4, dhg, name):
    B = pm.shape[0]

    def body(p_ref, cwq, cwk, cbq, cbk, bif_ref, gn_ref, dh_ref,
             dp_ref, dcw_ref, dcb_ref, dbif_ref, dgn_ref, *scr):
        st = scr[:9]
        U_s, un_s, be_s, mg_s, Cin_s, nin_s, min_s, al_s, bt_s = st
        dCp_s, dnp_s, dbe_s = scr[9:]
        p_id = pl.program_id(1)

        @pl.when(jnp.logical_and(pl.program_id(0) == 0, p_id == 0))
        def _():
            dcw_ref[...] = jnp.zeros_like(dcw_ref)
            dcb_ref[...] = jnp.zeros_like(dcb_ref)
            dbif_ref[...] = jnp.zeros_like(dbif_ref)
            dgn_ref[...] = jnp.zeros_like(dgn_ref)

        qc = _silu(_conv_pre(p_ref[0, :, _ML_Q], cwq[0], cbq[0]))
        kc = _silu(_conv_pre(p_ref[0, :, _ML_K], cwk[0], cbk[0])) * (64 ** -0.5)
        ifb = p_ref[0, :, _ML_IF] + bif_ref[0]
        dq = jnp.zeros((SEQ, 128), F32)
        dk = jnp.zeros((SEQ, 128), F32)
        difb = jnp.zeros((SEQ, 128), F32)
        for e in range(2):
            lanes = slice(e * 128, (e + 1) * 128)
            (H0, r0, U, un, b, b_end, mD, mg), vjp1 = jax.vjp(functools.partial(_ml_intra, e=e), qc, kc,
                                                              p_ref[0, :, _ml_v(e)], ifb)
            U_s[...], un_s[...], be_s[...], mg_s[...] = U, un, b_end, mg
            _state_sweep(*st)
            _, vjp3 = jax.vjp(functools.partial(_ml_inter, mD=mD, m_in=min_s[...], e=e), qc, p_ref[0, :, _ml_o(e)],
                              gn_ref[0, :, lanes], H0, r0, b, Cin_s[...], nin_s[...])
            dq_a, dmo, dgn, dH0, dr0, db_a, dCp, dnp = vjp3(dh_ref[0, :, lanes])
            dCp_s[...], dnp_s[...] = dCp, dnp
            _state_sweep_bwd(U_s, un_s, dbe_s, Cin_s, nin_s, dCp_s, dnp_s, al_s, bt_s)
            dq_b, dk_b, dv, difb_e = vjp1((dH0, dr0, U_s[...], un_s[...], db_a, dbe_s[...],
                                           jnp.zeros_like(mD), jnp.zeros_like(mg)))
            dq, dk, difb = dq + dq_a + dq_b, dk + dk_b, difb + difb_e
            dp_ref[0, :, _ml_v(e)] = dv.astype(BF16)
            dp_ref[0, :, _ml_o(e)] = dmo.astype(BF16)
            dgn_ref[p_id, :, lanes] += dgn
        dp_ref[0, :, _ML_IF] = difb.astype(BF16)
        dbif_ref[p_id] += jnp.sum(difb, axis=0, keepdims=True)

        for (sl, cw, cb, d, blk, scale) in ((_ML_Q, cwq, cbq, dq, p_id, 1.0), (_ML_K, cwk, cbk, dk, 4 + p_id, 64 ** -0.5)):
            xr = p_ref[0, :, sl]
            dpre = d * scale * _dsilu(_conv_pre(xr, cw[0], cb[0]))
            dx, dcw, dcb = _conv_bwd(xr, cw[0], dpre)
            dp_ref[0, :, sl] = dx.astype(BF16)
            for j in range(4):
                dcw_ref[blk, j:j + 1, :] += dcw[j]
            dcb_ref[blk] += dcb

    full3 = lambda b, p: (0, 0, 0)
    return pl.pallas_call(
        body, name=name, grid=(B, ML_PAIRS),
        in_specs=[pl.BlockSpec((1, SEQ, ML_PAIR_COLS), lambda b, p: (b, 0, p), pipeline_mode=pl.Buffered(1))]
        + _ml_specs()[1:] + [pl.BlockSpec((1, SEQ, 256), lambda b, p: (b, 0, p), pipeline_mode=pl.Buffered(1))],
        out_specs=[pl.BlockSpec((1, SEQ, ML_PAIR_COLS), lambda b, p: (b, 0, p)),
                   pl.BlockSpec((8, 4, 128), full3), pl.BlockSpec((8, 1, 128), full3),
                   pl.BlockSpec((4, 1, 128), full3), pl.BlockSpec((4, 1, 256), full3)],
        out_shape=[jax.ShapeDtypeStruct((B, SEQ, ML_COLS), BF16),
                   jax.ShapeDtypeStruct((8, 4, 128), F32), jax.ShapeDtypeStruct((8, 1, 128), F32),
                   jax.ShapeDtypeStruct((4, 1, 128), F32), jax.ShapeDtypeStruct((4, 1, 256), F32)],
        scratch_shapes=_state_scratch() + [pltpu.VMEM((N_CHUNKS, 128, 128), F32), pltpu.VMEM((N_CHUNKS, 1, 128), F32),
                                           pltpu.VMEM((N_CHUNKS, 1, 1), F32)],
        compiler_params=_cparams(),
    )(pm, cw8, cw8, cb8, cb8, bifp, gn4, dhg)


def _adamw(w, g, m, v):
    m = ADAM_B1 * m + (1.0 - ADAM_B1) * g
    v = ADAM_B2 * v + (1.0 - ADAM_B2) * (g * g)
    m_hat = m / (1.0 - ADAM_B1 ** ADAM_STEP)
    v_hat = v / (1.0 - ADAM_B2 ** ADAM_STEP)
    delta = -ADAM_LR * (m_hat / (jnp.sqrt(v_hat) + ADAM_EPS) + ADAM_WD * w)
    return delta, m, v


def adamw(w, g, m, v, name, parts=False):
    R, C = w.shape
    if R % 8 == 0 or R * C * 4 <= (1 << 20):
        tr = _pick(R, (256, 128, 64, 32, 16, 8)) if R * C * 4 > (1 << 20) else R
        steps = R // tr
        spec = pl.BlockSpec((tr, C), lambda i: (i, 0))
        g_spec = pl.BlockSpec((N_DEV, tr, C), lambda i: (0, i, 0)) if parts else spec
    else:
        tc = _pick(C, (256, 128))
        steps = C // tc
        spec = pl.BlockSpec((R, tc), lambda i: (0, i))
        g_spec = pl.BlockSpec((N_DEV, R, tc), lambda i: (0, 0, i)) if parts else spec

    def body(w_ref, g_ref, m_ref, v_ref, go_ref, d_ref, mo_ref, vo_ref):
        if parts:
            g = g_ref[0].astype(F32)
            for k in range(1, N_DEV):
                g = g + g_ref[k].astype(F32)
        else:
            g = g_ref[...]
        d, mn, vn = _adamw(w_ref[...], g, m_ref[...], v_ref[...])
        go_ref[...], d_ref[...], mo_ref[...], vo_ref[...] = g, d, mn, vn

    return pl.pallas_call(
        body, name=name, grid=(steps,),
        in_specs=[spec, g_spec, spec, spec], out_specs=[spec] * 4,
        out_shape=[jax.ShapeDtypeStruct((R, C), F32)] * 4,
        compiler_params=_cparams(),
    )(w, g, m, v)


def _mesh_pos():
    return lax.axis_index("x"), lax.axis_index("y"), lax.axis_index("c")


def _flip(pos, f):
    x, y, c = pos
    return (1 - x if f & 4 else x, 1 - y if f & 2 else y, 1 - c if f & 1 else c)


def _index(pos):
    return 4 * pos[0] + 2 * pos[1] + pos[2]


def _exchange(arrs, name, scatter):
    n = len(arrs)

    def body(*refs):
        ins, outs = refs[:n], refs[n:2 * n]
        send, recv, lsem = refs[2 * n:]
        me = _mesh_pos()
        mine = _index(me)
        copies = []
        for i in range(n):
            src = ins[i].at[mine] if scatter else ins[i]
            loc = pltpu.make_async_copy(src, outs[i].at[mine], lsem.at[i])
            loc.start()
            copies.append(loc)
            for f in range(1, N_DEV):
                peer = _flip(me, f)
                src = ins[i].at[_index(peer)] if scatter else ins[i]
                cp = pltpu.make_async_remote_copy(
                    src_ref=src, dst_ref=outs[i].at[mine],
                    send_sem=send.at[i * 7 + f - 1], recv_sem=recv.at[i * 7 + f - 1],
                    device_id=peer, device_id_type=pl.DeviceIdType.MESH)
                cp.start()
                copies.append(cp)
        for cp in copies:
            cp.wait()

    any_spec = pl.BlockSpec(memory_space=pl.ANY)
    out_shape = [jax.ShapeDtypeStruct(a.shape if scatter else (N_DEV,) + a.shape, a.dtype) for a in arrs]
    res = pl.pallas_call(
        body, name=name,
        in_specs=[any_spec] * n, out_specs=[any_spec] * n, out_shape=out_shape,
        scratch_shapes=[pltpu.SemaphoreType.DMA((7 * n,)), pltpu.SemaphoreType.DMA((7 * n,)),
                        pltpu.SemaphoreType.DMA((n,))],
        compiler_params=_cparams(),
    )(*arrs)
    return list(res)


def all_gather(arrs, name):
    return _exchange(arrs, name, False)


def all_gather_two_level(arrs, name):
    n = len(arrs)

    def body(*refs):
        ins, outs = refs[:n], refs[n:2 * n]
        send, recv, lsem = refs[2 * n:]
        x, y, c = _mesh_pos()
        me, sibling = (x, y, c), (x, y, 1 - c)
        chips = [(1 - x, y), (x, 1 - y), (1 - x, 1 - y)]

        def copy(i, k, block, to, src=None):
            rows = outs[i].at[_index(block)]
            return pltpu.make_async_remote_copy(
                src_ref=rows if src is None else src, dst_ref=rows,
                send_sem=send.at[i * 7 + k], recv_sem=recv.at[i * 7 + k],
                device_id=to, device_id_type=pl.DeviceIdType.MESH)

        local = [pltpu.make_async_copy(ins[i], outs[i].at[_index(me)], lsem.at[i]) for i in range(n)]
        first = [copy(i, 0, me, sibling, src=ins[i]) for i in range(n)]
        first += [copy(i, 1 + j, me, (*chip, c), src=ins[i]) for i in range(n) for j, chip in enumerate(chips)]
        for cp in local + first:
            cp.start()
        passed = []
        for j, chip in enumerate(chips):
            for i in range(n):
                copy(i, 1 + j, (*chip, c), me).wait_recv()
                cp = copy(i, 4 + j, (*chip, c), sibling)
                cp.start()
                passed.append(cp)
        for i in range(n):
            copy(i, 0, sibling, me).wait_recv()
            for j, chip in enumerate(chips):
                copy(i, 4 + j, (*chip, 1 - c), me).wait_recv()
        for cp in first + passed:
            cp.wait_send()
        for cp in local:
            cp.wait()

    any_spec = pl.BlockSpec(memory_space=pl.ANY)
    res = pl.pallas_call(
        body, name=name,
        in_specs=[any_spec] * n, out_specs=[any_spec] * n,
        out_shape=[jax.ShapeDtypeStruct((N_DEV,) + a.shape, a.dtype) for a in arrs],
        scratch_shapes=[pltpu.SemaphoreType.DMA((7 * n,)), pltpu.SemaphoreType.DMA((7 * n,)),
                        pltpu.SemaphoreType.DMA((n,))],
        compiler_params=_cparams(),
    )(*arrs)
    return list(res)


def all_to_all(arrs, name):
    return _exchange(arrs, name, True)


_HBM = pl.BlockSpec(memory_space=pltpu.HBM)
_SEM = pl.BlockSpec(memory_space=pltpu.SEMAPHORE)
_EFFECT = pltpu.SideEffectType.DATAFLOW_SIDE_EFFECTING


def _split_copies(ins, lands, send, recv, scatter, waiting):
    me = _mesh_pos()
    mine = _index(me)
    copies = []
    for i in range(len(ins)):
        for f in range(1, N_DEV):
            peer = _flip(me, f)
            src = ins[i].at[_index(peer)] if scatter else ins[i]
            copies.append(pltpu.make_async_remote_copy(
                src_ref=src, dst_ref=lands[i].at[_index(peer) if waiting else mine],
                send_sem=send.at[i * 7 + f - 1], recv_sem=recv.at[i * 7 + f - 1],
                device_id=peer, device_id_type=pl.DeviceIdType.MESH))
    return copies


def exchange_start(arrs, name, scatter, after=()):
    n = len(arrs)
    land_shapes = [a.shape if scatter else (N_DEV,) + a.shape for a in arrs]

    def body(*refs):
        ins, lands = refs[:n], refs[n:2 * n]
        send, recv = refs[2 * n + len(after)], refs[2 * n + len(after) + 1]
        token = refs[-1]
        for cp in _split_copies(ins, lands, send, recv, scatter, False):
            cp.start()
        token[...] = jnp.zeros_like(token)

    res = pl.pallas_call(
        body, name=name,
        out_shape=(pltpu.SemaphoreType.DMA((7 * n,)), pltpu.SemaphoreType.DMA((7 * n,)),
                   *[pltpu.HBM(a.shape, a.dtype) for a in arrs],
                   *[pltpu.HBM(s, a.dtype) for s, a in zip(land_shapes, arrs)],
                   jax.ShapeDtypeStruct((8, 128), F32)),
        in_specs=[_HBM] * (2 * n) + [pl.BlockSpec(memory_space=pl.ANY)] * len(after),
        out_specs=(_SEM, _SEM, *[_HBM] * (2 * n), pl.BlockSpec(memory_space=pltpu.VMEM)),
        input_output_aliases={i: 2 + i for i in range(2 * n)},
        compiler_params=pltpu.CompilerParams(has_side_effects=_EFFECT),
    )(*[pltpu.with_memory_space_constraint(a, pltpu.HBM) for a in arrs],
      *[pltpu.with_memory_space_constraint(lax.empty(s, a.dtype), pltpu.HBM) for s, a in zip(land_shapes, arrs)],
      *after)
    return (res[0], res[1], list(res[2:2 + n]), list(res[2 + n:2 + 2 * n])), res[-1]


def exchange_wait(handle, after, name, scatter):
    send, recv, srcs, lands = handle
    n = len(srcs)

    def body(*refs):
        ins, lnd = refs[:n], refs[n:2 * n]
        send_, recv_ = refs[2 * n], refs[2 * n + 1]
        for cp in _split_copies(ins, lnd, send_, recv_, scatter, True):
            cp.wait_send()
            cp.wait_recv()

    res = pl.pallas_call(
        body, name=name,
        out_shape=(*[pltpu.HBM(a.shape, a.dtype) for a in srcs], *[pltpu.HBM(a.shape, a.dtype) for a in lands]),
        in_specs=[_HBM] * (2 * n) + [_SEM, _SEM, pl.BlockSpec(memory_space=pl.ANY)],
        out_specs=tuple([_HBM] * (2 * n)),
        input_output_aliases={i: i for i in range(2 * n)},
        compiler_params=pltpu.CompilerParams(has_side_effects=_EFFECT),
    )(*srcs, *lands, send, recv, after)
    return list(res[n:])


def _own_slot(land, own):
    return lax.dynamic_update_slice(land, own[None], (_index(_mesh_pos()),) + (0,) * own.ndim)


def cast_bf16(arrs, name):
    outs = []
    for i, a in enumerate(arrs):
        R, C = a.shape
        if R % 8 == 0:
            tr = _pick(R, (256, 128, 64, 32, 16, 8)) if R * C * 4 > (1 << 21) else R
            steps, spec = R // tr, pl.BlockSpec((tr, C), lambda i: (i, 0))
        else:
            steps, spec = C // 256, pl.BlockSpec((R, 256), lambda i: (0, i))

        def body(a_ref, o_ref):
            o_ref[...] = a_ref[...].astype(BF16)

        outs.append(pl.pallas_call(body, name=f"{name}_{i}", grid=(steps,), in_specs=[spec], out_specs=spec,
                                   out_shape=jax.ShapeDtypeStruct((R, C), BF16), compiler_params=_cparams())(a))
    return outs


def sum_parts(parts, name):
    def fn(p):
        g = p[0]
        for k in range(1, N_DEV):
            g = g + p[k]
        return (g,)
    return small_call(fn, [parts], [jax.ShapeDtypeStruct(parts.shape[1:], F32)], name)[0]


_SPLITS = np.cumsum([1536, 1536, 1536, 512, 512, 1024, 1024, 8, 8, 2048])[:-1].tolist()


def split_w_in(w):
    aq, ak, av, mq, mk, mv, mo, mi, mf, gates = jnp.split(w, _SPLITS, axis=1)
    R = w.shape[0]
    w_att = jnp.stack([aq.reshape(R, 12, 128), ak.reshape(R, 12, 128), av.reshape(R, 12, 128)], axis=2)
    gif = jnp.concatenate([mi.reshape(R, 4, 2), mf.reshape(R, 4, 2), jnp.zeros((R, 4, 124), w.dtype)], axis=2)
    w_ml = jnp.concatenate([mq.reshape(R, 4, 128), mk.reshape(R, 4, 128), mv.reshape(R, 4, 256),
                            mo.reshape(R, 4, 256), gif], axis=2)
    return w_att.reshape(R, ATT_COLS), w_ml.reshape(R, ML_COLS), gates


def merge_w_in(g_att, g_ml, g_gate):
    R = g_att.shape[0]
    a = g_att.reshape(R, 12, 3, 128)
    m = g_ml.reshape(R, 4, ML_PAIR_COLS)
    gif = m[:, :, 768:772]
    return jnp.concatenate([
        a[:, :, 0].reshape(R, 1536), a[:, :, 1].reshape(R, 1536), a[:, :, 2].reshape(R, 1536),
        m[:, :, 0:128].reshape(R, 512), m[:, :, 128:256].reshape(R, 512),
        m[:, :, 256:512].reshape(R, 1024), m[:, :, 512:768].reshape(R, 1024),
        gif[:, :, 0:2].reshape(R, 8), gif[:, :, 2:4].reshape(R, 8), g_gate], axis=1)


def _blk8(v, width=128):
    r = v.shape[0]
    return v.reshape(r, 1024 // width, width).transpose(1, 0, 2)


def _unblk8(v):
    nb, r, w = v.shape
    return v.transpose(1, 0, 2).reshape(r, nb * w)


def local_step(x, target, mods, w, small, late_w=None, early_g=None, w_in_g=None):
    late_w = late_w or (lambda after: w)
    big = {}
    early_g = early_g or (lambda g: big.update(g))
    w_in_g = w_in_g or (lambda g: big.update(w_in=g))
    B = x.shape[0]
    T = B * SEQ
    shift1, scale1, gate1, shift2, scale2, gate2 = mods
    f2 = lambda a: a.reshape(T, a.shape[-1])
    f3 = lambda a: a.reshape(B, SEQ, a.shape[-1])

    rel_t = jnp.pad(small["rel_bias"].T, ((0, 4), (0, 0)))
    onehots = [_bucket_onehot(d) for _, d in ATT_GROUPS]
    biases = [bias_expand(rel_t, oh, f"bias_expand{g}").reshape(16, ATT_BLOCK, 2 * ATT_BLOCK)
              for g, oh in enumerate(onehots)]
    qg, kg = small["q_norm_g"], small["k_norm_g"]
    cw8 = _blk8(small["conv_w"])
    cb8 = _blk8(small["conv_b"])
    b_if = small["b_if"].reshape(2, 4, 2)
    bifp = jnp.concatenate([b_if[0], b_if[1], jnp.zeros((4, 124), F32)], axis=1).reshape(4, 1, 128)
    gn4 = small["mlstm_norm_g"].reshape(4, 1, 256)

    u = modnorm_fwd(x, small["norm1_g"], scale1, shift1, "modnorm1")
    u2d = f2(u)
    pa = f3(matmul(u2d, w["w_att"], mode="nn", name="proj_att"))
    pm = f3(matmul(u2d, w["w_ml"], mode="nn", name="proj_ml"))
    pg = f3(matmul(u2d, w["w_gate"], mode="nn", name="proj_gate"))
    os_, ls_ = [], []
    for g in range(3):
        o, l = attn_fwd(pa, biases[g], qg, kg, g, f"attn_fwd{g}")
        os_.append(o)
        ls_.append(l)
    att = merge_fwd(os_, ls_, "merge_fwd")
    hg = mlstm_fwd(pm, cw8, cb8, bifp, gn4, "mlstm_fwd")
    w = {**w, **late_w(hg)}
    y_att = matmul(f2(att), w["w_att_out"], mode="nn", name="att_out")
    y_ml = matmul(f2(hg), w["w_ml_out"], mode="nn", name="ml_out")
    z = gate_fwd(pg, f3(y_att), f3(y_ml), "gate_fwd")
    y, x1, u2 = out_proj_resid_modnorm(f2(z), w["w_out"], f2(x), gate1, small["norm2_g"], scale2, shift2, "out_proj")
    pre, hdn = matmul(u2, w["w_ff1"], mode="nn", name="ff1", out_dtypes=(BF16, BF16),
                      epi=lambda acc: (acc, jnp.square(jnp.maximum(acc, 0.0))))
    dx2, d_ffo, loss, d_gate2 = ff2_loss(hdn, w["w_ff2"], x1, gate2, f2(target), "ff2_loss")

    g_ff2 = matmul(hdn, d_ffo, mode="tn", name="g_ff2", out_dtypes=(BF16,))
    d_pre = matmul(d_ffo, w["w_ff2"], mode="nt", name="d_hdn", out_dtypes=(BF16,), extras=(pre,),
                   epi=lambda acc, p: (acc * (2.0 * jnp.maximum(p.astype(F32), 0.0)),))
    g_ff1 = matmul(u2, d_pre, mode="tn", name="g_ff1", out_dtypes=(BF16,))
    dx1, d_norm2, d_scale2, d_shift2, dy, d_gate1 = d_u_modnorm_bwd(
        d_pre, w["w_ff1"], None, x1, small["norm2_g"], scale2, shift2, dx2, y, gate1, "d_u2")
    g_out = matmul(f2(z), dy, mode="tn", name="g_out", out_dtypes=(BF16,))
    dz = matmul(f2(dy), w["w_out"], mode="nt", name="d_z")
    dpg, d_ya, d_ym = gate_bwd(pg, f3(y_att), f3(y_ml), f3(dz), "gate_bwd")
    g_att_out = matmul(f2(att), f2(d_ya), mode="tn", name="g_att_out", out_dtypes=(BF16,))
    d_att = matmul(f2(d_ya), w["w_att_out"], mode="nt", name="d_att")
    g_ml_out = matmul(f2(hg), f2(d_ym), mode="tn", name="g_ml_out", out_dtypes=(BF16,))
    d_hg = matmul(f2(d_ym), w["w_ml_out"], mode="nt", name="d_hg")
    started = early_g(dict(w_att_out=g_att_out, w_ml_out=g_ml_out, w_out=g_out, w_ff1=g_ff1, w_ff2=g_ff2))
    order = 0.0 if started is None else started[0, 0]
    dmerge = merge_bwd(os_, ls_, f3(d_att), "merge_bwd")
    dpa = lax.empty((B, SEQ, ATT_COLS), BF16)
    d_rel = []
    d_qg = d_kg = None
    for g in range(3):
        dpa, dbias, dq_g, dk_g = attn_bwd(pa, biases[g], qg + order, kg, dmerge[g], dmerge[3 + g], dpa, g,
                                          f"attn_bwd{g}")
        db8 = jnp.pad(dbias.reshape(4, -1), ((0, 4), (0, 0)))
        d_rel.append(bias_reduce(db8, onehots[g], f"bias_reduce{g}")[:4])
        d_qg = dq_g if d_qg is None else d_qg + dq_g
        d_kg = dk_g if d_kg is None else d_kg + dk_g
    dpm, dcw8, dcb8, dbifp, dgn4 = mlstm_bwd(pm, cw8, cb8, bifp, gn4 + order, f3(d_hg), "mlstm_bwd")
    g_w_att = matmul(u2d, f2(dpa), mode="tn", name="g_w_att", out_dtypes=(BF16,))
    g_w_ml = matmul(u2d, f2(dpm), mode="tn", name="g_w_ml", out_dtypes=(BF16,))
    g_w_gate = matmul(u2d, f2(dpg), mode="tn", name="g_w_gate", out_dtypes=(BF16,))
    started = w_in_g(merge_w_in(g_w_att, g_w_ml, g_w_gate))
    du = matmul(f2(dpa), w["w_att"], mode="nt", name="d_u_att", after=() if started is None else (started,))
    du = matmul(f2(dpm), w["w_ml"], mode="nt", name="d_u_ml", extras=(du,), epi=lambda acc, e: (acc + e,))
    grad_x, d_norm1, d_scale1, d_shift1 = d_u_modnorm_bwd(
        f2(dpg), w["w_gate"], du, f2(x), small["norm1_g"], scale1, shift1, dx1, None, None, "d_u_gate")
    grad_x = f3(grad_x)

    d_mods = (d_shift1, d_scale1, d_gate1, d_shift2, d_scale2, d_gate2)
    dbif = dbifp.reshape(4, 128)
    small_g = dict(
        norm1_g=d_norm1, norm2_g=d_norm2,
        b_if=jnp.stack([dbif[:, 0:2].reshape(8), dbif[:, 2:4].reshape(8)]),
        conv_w=_unblk8(dcw8), conv_b=_unblk8(dcb8), q_norm_g=d_qg, k_norm_g=d_kg,
        rel_bias=jnp.concatenate(d_rel, axis=0).T,
        mlstm_norm_g=dgn4.reshape(1, 1024))
    return loss, grad_x, d_mods, big, small_g


_SMALL = (("b_ada", 6144), ("norm1_g", 1024), ("norm2_g", 1024), ("b_if", 16), ("conv_b", 1024),
          ("q_norm_g", 128), ("k_norm_g", 128), ("rel_bias", 384), ("mlstm_norm_g", 1024), ("conv_w", 4096))
_SMALL_ROWS = 120
_REPL = _SMALL[:-1]
_REPL_ROWS = 86


def _pack(d, names, rows):
    flat = jnp.concatenate([d[k].reshape(-1) for k, _ in names])
    return jnp.pad(flat, (0, rows * 128 - flat.shape[0])).reshape(rows, 128)


def _unpack(slab, names, shapes):
    flat = slab.reshape(-1)
    out, off = {}, 0
    for k, nel in names:
        out[k] = flat[off:off + nel].reshape(shapes[k])
        off += nel
    return out


def kernel(x, c, w_ada, b_ada, norm1_g, norm2_g, w_in, b_if, conv_w, conv_b, q_norm_g, k_norm_g, rel_bias, mlstm_norm_g, w_att_out, w_ml_out, w_out, w_ff1, w_ff2, loss_target, m_w_ada, m_b_ada, m_norm1_g, m_norm2_g, m_w_in, m_b_if, m_conv_w, m_conv_b, m_q_norm_g, m_k_norm_g, m_rel_bias, m_mlstm_norm_g, m_w_att_out, m_w_ml_out, m_w_out, m_w_ff1, m_w_ff2, v_w_ada, v_b_ada, v_norm1_g, v_norm2_g, v_w_in, v_b_if, v_conv_w, v_conv_b, v_q_norm_g, v_k_norm_g, v_rel_bias, v_mlstm_norm_g, v_w_att_out, v_w_ml_out, v_w_out, v_w_ff1, v_w_ff2):
    P = dict(w_ada=w_ada, b_ada=b_ada, norm1_g=norm1_g, norm2_g=norm2_g, w_in=w_in, b_if=b_if, conv_w=conv_w,
             conv_b=conv_b, q_norm_g=q_norm_g, k_norm_g=k_norm_g, rel_bias=rel_bias, mlstm_norm_g=mlstm_norm_g,
             w_att_out=w_att_out, w_ml_out=w_ml_out, w_out=w_out, w_ff1=w_ff1, w_ff2=w_ff2)
    M = dict(w_ada=m_w_ada, b_ada=m_b_ada, norm1_g=m_norm1_g, norm2_g=m_norm2_g, w_in=m_w_in, b_if=m_b_if,
             conv_w=m_conv_w, conv_b=m_conv_b, q_norm_g=m_q_norm_g, k_norm_g=m_k_norm_g, rel_bias=m_rel_bias,
             mlstm_norm_g=m_mlstm_norm_g, w_att_out=m_w_att_out, w_ml_out=m_w_ml_out, w_out=m_w_out,
             w_ff1=m_w_ff1, w_ff2=m_w_ff2)
    V = dict(w_ada=v_w_ada, b_ada=v_b_ada, norm1_g=v_norm1_g, norm2_g=v_norm2_g, w_in=v_w_in, b_if=v_b_if,
             conv_w=v_conv_w, conv_b=v_conv_b, q_norm_g=v_q_norm_g, k_norm_g=v_k_norm_g, rel_bias=v_rel_bias,
             mlstm_norm_g=v_mlstm_norm_g, w_att_out=v_w_att_out, w_ml_out=v_w_ml_out, w_out=v_w_out,
             w_ff1=v_w_ff1, w_ff2=v_w_ff2)
    names = list(P)
    shapes = {k: P[k].shape for k in names}
    B = x.shape[0]
    me = _index(_mesh_pos())

    big_names = ("w_in", "w_att_out", "w_ml_out", "w_out", "w_ff1", "w_ff2")
    shards = cast_bf16([P[k][0] for k in big_names], "cast_w")
    w_in_g8, c8, conv_w8 = all_gather_two_level([shards[0], c, conv_w[0]], "gather_w_in")
    c_all = c8.reshape(N_DEV * B, D_MODEL)
    conv_w_full = conv_w8.transpose(1, 0, 2).reshape(4, 1024)
    w_att, w_ml, w_gate = split_w_in(w_in_g8.transpose(1, 0, 2).reshape(D_MODEL, D_IN))
    w = dict(w_att=w_att, w_ml=w_ml, w_gate=w_gate)

    (silu_c,) = small_call(lambda a: (_silu(a),), [c_all], [jax.ShapeDtypeStruct(c_all.shape, F32)], "silu_c")
    b_ada_cols = lax.dynamic_slice(b_ada, (0, me * 768), (1, 768))
    ada_cols = matmul(silu_c, w_ada[0], mode="nn", name="ada", extras=(jnp.broadcast_to(b_ada_cols, (N_DEV * B, 768)),),
                      epi=lambda acc, bb: (acc + bb,))
    (ada_t,) = all_to_all([ada_cols.reshape(N_DEV, B, 768)], "ada_exchange")
    ada = ada_t.transpose(1, 0, 2).reshape(B, 6 * D_MODEL)
    mods = tuple(ada[:, i * D_MODEL:(i + 1) * D_MODEL].reshape(B, 1, D_MODEL) for i in range(6))

    late_handle, late_order = exchange_start(shards[1:], "gather_late_start", False, after=(ada_t,))

    def late_w(after):
        lands = exchange_wait(late_handle, after, "gather_late_wait", False)
        gw = dict(zip(big_names[1:], [_own_slot(l, s) for l, s in zip(lands, shards[1:])]))
        return dict(w_att_out=gw["w_att_out"].transpose(1, 0, 2).reshape(512, D_MODEL),
                    w_ml_out=gw["w_ml_out"].reshape(D_MODEL, D_MODEL), w_out=gw["w_out"].reshape(D_MODEL, D_MODEL),
                    w_ff1=gw["w_ff1"].transpose(1, 0, 2).reshape(D_MODEL, D_FF),
                    w_ff2=gw["w_ff2"].reshape(D_FF, D_MODEL))

    pending = {}

    def send_grads(key, blocks, name):
        handle, order = exchange_start(blocks, name, True)
        pending[key] = (handle, [lax.dynamic_index_in_dim(b, me, 0, keepdims=False) for b in blocks])
        return order

    def early_g(g):
        return send_grads("late", [g["w_att_out"].reshape(512, N_DEV, 128).transpose(1, 0, 2),
                                   g["w_ml_out"].reshape(N_DEV, 128, D_MODEL), g["w_out"].reshape(N_DEV, 128, D_MODEL),
                                   g["w_ff1"].reshape(D_MODEL, N_DEV, 512).transpose(1, 0, 2),
                                   g["w_ff2"].reshape(N_DEV, 512, D_MODEL)], "grad_late_start")

    def w_in_g(g):
        return send_grads("w_in", [g.reshape(D_MODEL, N_DEV, W_IN_SHARD).transpose(1, 0, 2)], "grad_w_in_start")

    def recv_grads(key, after, name):
        handle, own = pending[key]
        return [_own_slot(l, o) for l, o in zip(exchange_wait(handle, after, name, True), own)]

    small = dict(norm1_g=norm1_g + late_order[0, 0], norm2_g=norm2_g, b_if=b_if[0], conv_w=conv_w_full, conv_b=conv_b,
                 q_norm_g=q_norm_g, k_norm_g=k_norm_g, rel_bias=rel_bias, mlstm_norm_g=mlstm_norm_g)
    loss, grad_x, d_mods, _, small_g = local_step(x, loss_target, mods, w, small, late_w, early_g, w_in_g)
    loss = lax.psum(loss[0, 0], ("x", "y", "c"))

    d_ada = jnp.concatenate([d.reshape(B, D_MODEL) for d in d_mods], axis=1)
    (d_ada_t,) = all_to_all([d_ada.reshape(B, N_DEV, 768).transpose(1, 0, 2)], "d_ada_exchange")
    d_ada_cols = d_ada_t.reshape(N_DEV * B, 768)
    g_w_ada = matmul(silu_c, d_ada_cols, mode="tn", name="g_w_ada")
    (g_b_cols,) = small_call(lambda a: (jnp.sum(a, axis=0, keepdims=True),), [d_ada_cols],
                             [jax.ShapeDtypeStruct((1, 768), F32)], "g_b_ada_cols")
    small_g["b_ada"] = lax.dynamic_update_slice(jnp.zeros((1, 6144), F32), g_b_cols, (0, me * 768))

    recv = recv_grads("w_in", grad_x, "grad_w_in_wait") + recv_grads("late", grad_x, "grad_late_wait")
    (small_parts,) = all_gather([_pack(small_g, _SMALL, _SMALL_ROWS)], "small_grad_gather")
    small_sum = sum_parts(small_parts, "small_grad_sum")
    sg = _unpack(small_sum, _SMALL, {**{k: shapes[k] for k, _ in _REPL}, "conv_w": (4, 1024)})

    G, Dl, NM, NV = {}, {}, {}, {}
    for k, parts in zip(big_names, recv):
        g, d, nm, nv = adamw(P[k][0], parts, M[k][0], V[k][0], f"adamw_{k}", parts=True)
        G[k], Dl[k], NM[k], NV[k] = g[None], d[None], nm[None], nv[None]
    g, d, nm, nv = adamw(w_ada[0], g_w_ada, m_w_ada[0], v_w_ada[0], "adamw_w_ada")
    G["w_ada"], Dl["w_ada"], NM["w_ada"], NV["w_ada"] = g[None], d[None], nm[None], nv[None]
    g_conv = lax.dynamic_slice(sg["conv_w"], (0, me * 128), (4, 128))
    g, d, nm, nv = adamw(conv_w[0], g_conv, m_conv_w[0], v_conv_w[0], "adamw_conv_w")
    G["conv_w"], Dl["conv_w"], NM["conv_w"], NV["conv_w"] = g[None], d[None], nm[None], nv[None]
    gslab = _pack(sg, _REPL, _REPL_ROWS)
    _, d, nm, nv = adamw(_pack(P, _REPL, _REPL_ROWS), gslab, _pack(M, _REPL, _REPL_ROWS), _pack(V, _REPL, _REPL_ROWS),
                         "adamw_small")
    rs = {k: shapes[k] for k, _ in _REPL}
    d, nm, nv = _unpack(d, _REPL, rs), _unpack(nm, _REPL, rs), _unpack(nv, _REPL, rs)
    for k, _ in _REPL:
        G[k], Dl[k], NM[k], NV[k] = sg[k], d[k], nm[k], nv[k]

    return (loss, grad_x, *[G[k] for k in names], *[Dl[k] for k in names], *[NM[k] for k in names],
            *[NV[k] for k in names])
```

```python
import functools
import math

import numpy as np
import jax
import jax.numpy as jnp
from jax import lax
from jax.experimental import pallas as pl
from jax.experimental.pallas import tpu as pltpu

F32 = jnp.float32
BF16 = jnp.bfloat16

N_DEV = 8
D_MODEL = 1024
SEQ = 2048
ATT_GROUPS = ((128, 1), (512, 4), (2048, 16))
N_ATT_HEADS = 12
ATT_BLOCK = 128
HEAD_DIM = 128
ML_HEADS = 8
ML_PAIRS = 4
ML_CHUNK = 64
N_CHUNKS = SEQ // ML_CHUNK
N_BUCKETS = 32
MAX_DISTANCE = 2048
D_FF = 4096
D_IN = 9744
EPS = 1e-6

ADAM_LR = 0.001
ADAM_B1 = 0.9
ADAM_B2 = 0.999
ADAM_EPS = 1e-08
ADAM_WD = 0.01
ADAM_STEP = 10

ATT_HEAD_COLS = 3 * HEAD_DIM
ATT_COLS = N_ATT_HEADS * ATT_HEAD_COLS
ML_PAIR_COLS = 896
ML_COLS = ML_PAIRS * ML_PAIR_COLS
GATE_COLS = 2 * D_MODEL
W_IN_SHARD = D_IN // N_DEV

VMEM_LIMIT = 56 * 1024 * 1024


def _cparams(**kw):
    return pltpu.CompilerParams(vmem_limit_bytes=VMEM_LIMIT, **kw)


_NN = ((1,), (0,))
_NT = ((1,), (1,))
_TN = ((0,), (0,))


def _mxu(a, b, dims):
    return lax.dot_general(a.astype(BF16), b.astype(BF16), (dims, ((), ())), preferred_element_type=F32)


@jax.custom_vjp
def bdot_nn(a, b):
    return _mxu(a, b, _NN)


def _nn_fwd(a, b):
    return _mxu(a, b, _NN), (a, b)


def _nn_bwd(res, g):
    a, b = res
    return _mxu(g, b, _NT), _mxu(a, g, _TN)


bdot_nn.defvjp(_nn_fwd, _nn_bwd)


@jax.custom_vjp
def bdot_nt(a, b):
    return _mxu(a, b, _NT)


def _nt_fwd(a, b):
    return _mxu(a, b, _NT), (a, b)


def _nt_bwd(res, g):
    a, b = res
    return _mxu(g, b, _NN), _mxu(g, a, _TN)


bdot_nt.defvjp(_nt_fwd, _nt_bwd)


def _doth(a, b, dims=_NN):
    return lax.dot_general(a, b, (dims, ((), ())), precision=lax.Precision.HIGHEST, preferred_element_type=F32)


def _rms(x):
    return x * lax.rsqrt(jnp.mean(x * x, axis=-1, keepdims=True) + EPS)


def _pick(n, cands):
    for t in cands:
        if n % t == 0:
            return t
    raise ValueError(f"no tile for {n}")


MM_TILE_M = (1024, 512, 256, 128, 64, 32, 16, 8)
MM_TILE_N = (2048, 1792, 1536, 1024, 768, 512, 256, 128)
MM_TILE_K = (2048, 1792, 1536, 1024, 512, 256, 128, 64, 32)

def matmul(a, b, *, mode, name, out_dtypes=(F32,), epi=None, extras=(), after=()):
    if mode == "nn":
        (M, K), (K2, N) = a.shape, b.shape
    elif mode == "nt":
        (M, K), (N, K2) = a.shape, b.shape
    else:
        (K, M), (K2, N) = a.shape, b.shape
    assert K == K2, (a.shape, b.shape, mode)
    tm = _pick(M, MM_TILE_M)
    tn = _pick(N, MM_TILE_N)
    tk = _pick(K, MM_TILE_K)
    nk = K // tk
    n_ex = len(extras)
    n_out = len(out_dtypes)
    dims = {"nn": _NN, "nt": _NT, "tn": _TN}[mode]

    def finish(r, ex_refs, out_refs):
        outs = epi(r, *[e[...] for e in ex_refs]) if epi is not None else (r,)
        for o_ref, o in zip(out_refs, outs):
            o_ref[...] = o.astype(o_ref.dtype)

    def body(*refs):
        a_ref, b_ref = refs[0], refs[1]
        ex_refs = refs[2:2 + n_ex]
        out_refs = refs[2 + n_ex + len(after):2 + n_ex + len(after) + n_out]
        if nk == 1:
            finish(_mxu(a_ref[...], b_ref[...], dims), ex_refs, out_refs)
            return
        acc = refs[2 + n_ex + len(after) + n_out]
        k = pl.program_id(2)

        @pl.when(k == 0)
        def _():
            acc[...] = jnp.zeros_like(acc)

        acc[...] += _mxu(a_ref[...], b_ref[...], dims)

        @pl.when(k == nk - 1)
        def _():
            finish(acc[...], ex_refs, out_refs)

    if mode == "nn":
        a_spec = pl.BlockSpec((tm, tk), lambda i, j, k: (i, k))
        b_spec = pl.BlockSpec((tk, tn), lambda i, j, k: (k, j))
    elif mode == "nt":
        a_spec = pl.BlockSpec((tm, tk), lambda i, j, k: (i, k))
        b_spec = pl.BlockSpec((tn, tk), lambda i, j, k: (j, k))
    else:
        a_spec = pl.BlockSpec((tk, tm), lambda i, j, k: (k, i))
        b_spec = pl.BlockSpec((tk, tn), lambda i, j, k: (k, j))
    o_spec = pl.BlockSpec((tm, tn), lambda i, j, k: (i, j))
    res = pl.pallas_call(
        body,
        name=name,
        grid=(M // tm, N // tn, nk),
        in_specs=[a_spec, b_spec] + [o_spec] * n_ex + [pl.BlockSpec(memory_space=pl.ANY)] * len(after),
        out_specs=[o_spec] * n_out,
        out_shape=[jax.ShapeDtypeStruct((M, N), dt) for dt in out_dtypes],
        scratch_shapes=[pltpu.VMEM((tm, tn), F32)] if nk > 1 else [],
        compiler_params=_cparams(),
    )(a, b, *extras, *after)
    return res[0] if n_out == 1 else tuple(res)


ROW_MM_TILE = 512


def row_matmul(a, b, *, mode, name, extras, outs, epi):
    (M, K) = a.shape
    N = b.shape[1] if mode == "nn" else b.shape[0]
    tm = ROW_MM_TILE
    tk = _pick(K, MM_TILE_K)
    nk = K // tk
    n_ex, n_out = len(extras), len(outs)

    def body(*refs):
        a_ref, b_ref = refs[0], refs[1]
        ex_refs = refs[2:2 + n_ex]
        out_refs = refs[2 + n_ex:2 + n_ex + n_out]
        i = pl.program_id(0)
        dims = _NN if mode == "nn" else _NT
        if nk == 1:
            epi(_mxu(a_ref[...], b_ref[...], dims), i, ex_refs, out_refs)
            return
        acc = refs[2 + n_ex + n_out]
        k = pl.program_id(1)

        @pl.when(k == 0)
        def _():
            acc[...] = jnp.zeros_like(acc)

        acc[...] += _mxu(a_ref[...], b_ref[...], dims)

        @pl.when(k == nk - 1)
        def _():
            epi(acc[...], i, ex_refs, out_refs)

    def lift(index_map):
        return lambda i, k: index_map(i)

    b_spec = pl.BlockSpec((tk, N), lambda i, k: (k, 0)) if mode == "nn" else pl.BlockSpec((N, tk), lambda i, k: (0, k))
    res = pl.pallas_call(
        body, name=name, grid=(M // tm, nk),
        in_specs=[pl.BlockSpec((tm, tk), lambda i, k: (i, k)), b_spec]
        + [pl.BlockSpec(blk, lift(im)) for _, blk, im in extras],
        out_specs=[pl.BlockSpec(blk, lift(im)) for _, _, blk, im in outs],
        out_shape=[jax.ShapeDtypeStruct(shape, dt) for shape, dt, _, _ in outs],
        scratch_shapes=[pltpu.VMEM((tm, N), F32)] if nk > 1 else [],
        compiler_params=_cparams(),
    )(a, b, *[e[0] for e in extras])
    return tuple(res)


def _rows(arr):
    return (arr, (ROW_MM_TILE, arr.shape[1]), lambda i: (i, 0))


def _rows_out(T, dtype):
    return ((T, D_MODEL), dtype, (ROW_MM_TILE, D_MODEL), lambda i: (i, 0))


def _per_seq(arr):
    return (arr, (1, 1, D_MODEL), lambda i: (i // (SEQ // ROW_MM_TILE), 0, 0))


def _per_seq_out(B):
    return ((B, 1, D_MODEL), F32, (1, 1, D_MODEL), lambda i: (i // (SEQ // ROW_MM_TILE), 0, 0))


def _first_tile_of_seq(i):
    return i % (SEQ // ROW_MM_TILE) == 0


def small_call(fn, inputs, out_shapes, name):
    n_in = len(inputs)

    def body(*refs):
        outs = fn(*[r[...] for r in refs[:n_in]])
        for o_ref, o in zip(refs[n_in:], outs):
            o_ref[...] = o.astype(o_ref.dtype)

    res = pl.pallas_call(body, name=name, out_shape=list(out_shapes), compiler_params=_cparams())(*inputs)
    return tuple(res)


ROW_TILE = 512


def _modnorm(x, g, scale, shift):
    return _rms(x) * g * (1.0 + scale) + shift


def _row_spec(width):
    return pl.BlockSpec((1, ROW_TILE, width), lambda b, i: (b, i, 0))


def _mod_spec():
    return pl.BlockSpec((1, 1, D_MODEL), lambda b, i: (b, 0, 0))


def _vec_spec():
    return pl.BlockSpec((1, D_MODEL), lambda b, i: (0, 0))


def modnorm_fwd(x, g, scale, shift, name):
    B, S, D = x.shape

    def body(x_ref, g_ref, sc_ref, sh_ref, u_ref):
        u_ref[0] = _modnorm(x_ref[0], g_ref[...], sc_ref[0], sh_ref[0]).astype(BF16)

    return pl.pallas_call(
        body, name=name, grid=(B, S // ROW_TILE),
        in_specs=[_row_spec(D), _vec_spec(), _mod_spec(), _mod_spec()],
        out_specs=_row_spec(D),
        out_shape=jax.ShapeDtypeStruct((B, S, D), BF16),
        compiler_params=_cparams(),
    )(x, g, scale, shift)


def _gain_spec(g):
    return (g, (1, D_MODEL), lambda i: (0, 0))


def out_proj_resid_modnorm(z, w_out, x, gate, g, scale, shift, name):
    T = z.shape[0]

    def epi(acc, i, ex, out):
        x_ref, gt_ref, g_ref, sc_ref, sh_ref = ex
        y_ref, x1_ref, u_ref = out
        y_ref[...] = acc
        x1 = x_ref[...] + gt_ref[0] * acc
        x1_ref[...] = x1
        u_ref[...] = _modnorm(x1, g_ref[...], sc_ref[0], sh_ref[0]).astype(BF16)

    return row_matmul(z, w_out, mode="nn", name=name,
                      extras=[_rows(x), _per_seq(gate), _gain_spec(g), _per_seq(scale), _per_seq(shift)],
                      outs=[_rows_out(T, F32), _rows_out(T, F32), _rows_out(T, BF16)], epi=epi)


def ff2_loss(hdn, w_ff2, x1, gate, target, name):
    T = hdn.shape[0]

    def epi(acc, i, ex, out):
        x_ref, gt_ref, t_ref = ex
        dx_ref, dffo_ref, loss_ref, dg_ref = out

        @pl.when(i == 0)
        def _():
            loss_ref[...] = jnp.zeros_like(loss_ref)

        @pl.when(_first_tile_of_seq(i))
        def _():
            dg_ref[...] = jnp.zeros_like(dg_ref)

        err = x_ref[...] + gt_ref[0] * acc - t_ref[...]
        dx = err * (1.0 / D_MODEL)
        dx_ref[...] = dx
        dffo_ref[...] = (gt_ref[0] * dx).astype(BF16)
        loss_ref[...] += 0.5 * jnp.sum(jnp.mean(err * err, axis=-1, keepdims=True), axis=0, keepdims=True)
        dg_ref[0] += jnp.sum(dx * acc, axis=0, keepdims=True)

    return row_matmul(hdn, w_ff2, mode="nn", name=name,
                      extras=[_rows(x1), _per_seq(gate), _rows(target)],
                      outs=[_rows_out(T, F32), _rows_out(T, BF16), ((1, 1), F32, (1, 1), lambda i: (0, 0)),
                            _per_seq_out(T // SEQ)], epi=epi)


def d_u_modnorm_bwd(a, w, du_prev, x, g, scale, shift, dx_res, y, gate, name):
    T = a.shape[0]
    B = T // SEQ
    n_prev, resid = int(du_prev is not None), y is not None

    def epi(acc, i, ex, out):
        x_ref, g_ref, sc_ref, sh_ref, dr_ref = ex[n_prev:n_prev + 5]
        dx_ref, dg_ref, dsc_ref, dsh_ref = out[:4]

        @pl.when(i == 0)
        def _():
            dg_ref[...] = jnp.zeros_like(dg_ref)

        @pl.when(_first_tile_of_seq(i))
        def _():
            for r in out[2:4] + out[5:]:
                r[...] = jnp.zeros_like(r)

        du = acc + ex[0][...] if n_prev else acc
        _, vjp = jax.vjp(_modnorm, x_ref[...], g_ref[...], sc_ref[0], sh_ref[0])
        dxn, dg, dsc, dsh = vjp(du)
        dx = dxn + dr_ref[...]
        dx_ref[...] = dx
        dg_ref[...] += dg
        dsc_ref[0] += dsc
        dsh_ref[0] += dsh
        if resid:
            y_ref, gt_ref = ex[n_prev + 5:]
            out[4][...] = (gt_ref[0] * dx).astype(BF16)
            out[5][0] += jnp.sum(dx * y_ref[...], axis=0, keepdims=True)

    extras = ([_rows(du_prev)] if n_prev else []) + [_rows(x), _gain_spec(g), _per_seq(scale), _per_seq(shift),
                                                     _rows(dx_res)] + ([_rows(y), _per_seq(gate)] if resid else [])
    outs = [_rows_out(T, F32), ((1, D_MODEL), F32, (1, D_MODEL), lambda i: (0, 0)), _per_seq_out(B), _per_seq_out(B)]
    outs += [_rows_out(T, BF16), _per_seq_out(B)] if resid else []
    return row_matmul(a, w, mode="nt", name=name, extras=extras, outs=outs, epi=epi)


def _bucket_table(dilation):
    i = np.arange(ATT_BLOCK)[:, None]
    j = np.arange(2 * ATT_BLOCK)[None, :]
    delta = ATT_BLOCK + i - j
    dist = np.maximum(delta, 0) * dilation
    max_exact = N_BUCKETS // 2
    d = np.maximum(dist, max_exact).astype(np.float32)
    large = max_exact + (np.log(d / np.float32(max_exact)) / np.float32(math.log(MAX_DISTANCE / max_exact))
                         * np.float32(N_BUCKETS - max_exact)).astype(np.int32)
    large = np.minimum(large, N_BUCKETS - 1)
    return np.where(dist < max_exact, dist, large).astype(np.int32)


def _bucket_onehot(dilation):
    bt = jnp.asarray(_bucket_table(dilation).reshape(1, -1))
    return (bt == jnp.arange(N_BUCKETS, dtype=jnp.int32)[:, None]).astype(F32)


def bias_expand(rel_t, onehot, name):
    def fn(r, oh):
        return (_doth(r, oh),)
    return small_call(fn, [rel_t, onehot], [jax.ShapeDtypeStruct((rel_t.shape[0], onehot.shape[1]), F32)], name)[0]


def bias_reduce(dbias_flat, onehot, name):
    def fn(db, oh):
        return (_doth(db, oh, _NT),)
    return small_call(fn, [dbias_flat, onehot], [jax.ShapeDtypeStruct((dbias_flat.shape[0], N_BUCKETS), F32)], name)[0]


def _qk_norm(x, g):
    return _rms(x) * g


def _masked_bias(bias):
    i = lax.broadcasted_iota(jnp.int32, (ATT_BLOCK, 2 * ATT_BLOCK), 0)
    j = lax.broadcasted_iota(jnp.int32, (ATT_BLOCK, 2 * ATT_BLOCK), 1)
    bm = jnp.where(jnp.logical_and(j >= i, j <= i + ATT_BLOCK), bias, -jnp.inf)
    return bm, bm[:, ATT_BLOCK:]


def _attn_tile(qn, kn, v, bias):
    s = bdot_nt(qn, kn) * (HEAD_DIM ** -0.5) + bias
    m = lax.stop_gradient(jnp.max(s, axis=-1, keepdims=True))
    p = jnp.exp(s - m)
    l = jnp.sum(p, axis=-1, keepdims=True)
    o = bdot_nn(p, v) / l
    lse = jnp.broadcast_to(m + jnp.log(l), (ATT_BLOCK, HEAD_DIM))
    return o, lse


def _attn_tiles(dilation):
    nb = SEQ // dilation // ATT_BLOCK
    return [(r, n) for r in range(dilation) for n in range(nb)]


def _attn_rows(r, n, dilation, nblk=1):
    if dilation == 1:
        return pl.ds(r + n * ATT_BLOCK, nblk * ATT_BLOCK)
    return pl.ds(r + n * ATT_BLOCK * dilation, nblk * ATT_BLOCK, stride=dilation)


_QL, _KL, _VL = slice(0, 128), slice(128, 256), slice(256, 384)


def _qkv_specs(hb):
    return [pl.BlockSpec((None, SEQ, HEAD_DIM), functools.partial(lambda b, h, j: (b, 0, 3 * (hb + h) + j), j=j))
            for j in range(3)]


def attn_fwd(pa, bias, qg, kg, group, name):
    B = pa.shape[0]
    dilation = ATT_GROUPS[group][1]
    hb = group * 4

    def body(q_ref, k_ref, v_ref, b_ref, qg_ref, kg_ref, o_ref, l_ref, qn_s, kn_s):
        qn_s[...] = _qk_norm(q_ref[...], qg_ref[...])
        kn_s[...] = _qk_norm(k_ref[...], kg_ref[...])
        bias_all, bias_first = _masked_bias(b_ref[0])
        for (r, n) in _attn_tiles(dilation):
            rows = _attn_rows(r, n, dilation)
            if n == 0:
                krows, bias_t = rows, bias_first
            else:
                krows, bias_t = _attn_rows(r, n - 1, dilation, 2), bias_all
            o, lse = _attn_tile(qn_s[rows, :], kn_s[krows, :], v_ref[krows, :], bias_t)
            o_ref[rows, :] = o
            l_ref[rows, :] = lse

    head_out = pl.BlockSpec((None, SEQ, HEAD_DIM), lambda b, h: (b, 0, h))
    return pl.pallas_call(
        body, name=name, grid=(B, 4),
        in_specs=_qkv_specs(hb) + [
                  pl.BlockSpec((1, ATT_BLOCK, 2 * ATT_BLOCK), lambda b, h: (hb + h, 0, 0)),
                  pl.BlockSpec((1, HEAD_DIM), lambda b, h: (0, 0)),
                  pl.BlockSpec((1, HEAD_DIM), lambda b, h: (0, 0))],
        out_specs=[head_out, head_out],
        out_shape=[jax.ShapeDtypeStruct((B, SEQ, 512), F32), jax.ShapeDtypeStruct((B, SEQ, 512), F32)],
        scratch_shapes=[pltpu.VMEM((SEQ, HEAD_DIM), F32)] * 2,
        compiler_params=_cparams(),
    )(pa, pa, pa, bias, qg, kg)


def attn_bwd(pa, bias, qg, kg, do, dlse, dpa, group, name):
    B = pa.shape[0]
    dilation = ATT_GROUPS[group][1]
    hb = group * 4

    def body(q_ref, k_ref, v_ref, b_ref, qg_ref, kg_ref, do_ref, dl_ref, dpa_in,
             dp_ref, db_ref, dqg_ref, dkg_ref, qn_s, kn_s, dq_s, dk_s, dv_s):
        del dpa_in
        h_id = pl.program_id(1)

        @pl.when(jnp.logical_and(pl.program_id(0) == 0, h_id == 0))
        def _():
            db_ref[...] = jnp.zeros_like(db_ref)
            dqg_ref[...] = jnp.zeros_like(dqg_ref)
            dkg_ref[...] = jnp.zeros_like(dkg_ref)

        dk_s[...] = jnp.zeros_like(dk_s)
        dv_s[...] = jnp.zeros_like(dv_s)
        qn_s[...] = _qk_norm(q_ref[...], qg_ref[...])
        kn_s[...] = _qk_norm(k_ref[...], kg_ref[...])
        bias_all, bias_first = _masked_bias(b_ref[0])
        for (r, n) in _attn_tiles(dilation):
            rows = _attn_rows(r, n, dilation)
            if n == 0:
                krows, bias_t = rows, bias_first
            else:
                krows, bias_t = _attn_rows(r, n - 1, dilation, 2), bias_all
            _, vjp = jax.vjp(_attn_tile, qn_s[rows, :], kn_s[krows, :], v_ref[krows, :], bias_t)
            dqn, dkn, dv, dbias = vjp((do_ref[rows, :], dl_ref[rows, :]))
            dq_s[rows, :] = dqn
            dk_s[krows, :] += dkn
            dv_s[krows, :] += dv
            if n == 0:
                db_ref[h_id, :, ATT_BLOCK:] += dbias
            else:
                db_ref[h_id] += dbias
        for x_ref, g_ref, d_s, dg_ref, lanes in ((q_ref, qg_ref, dq_s, dqg_ref, _QL), (k_ref, kg_ref, dk_s, dkg_ref, _KL)):
            _, vjp = jax.vjp(_qk_norm, x_ref[...], g_ref[...])
            dx, dg = vjp(d_s[...])
            dp_ref[0, :, lanes] = dx.astype(BF16)
            dg_ref[...] += dg
        dp_ref[0, :, _VL] = dv_s[...].astype(BF16)

    const2 = lambda b, h: (0, 0)
    head_in = pl.BlockSpec((None, SEQ, HEAD_DIM), lambda b, h: (b, 0, h))
    head_blk = pl.BlockSpec((1, SEQ, ATT_HEAD_COLS), lambda b, h: (b, 0, hb + h))
    return pl.pallas_call(
        body, name=name, grid=(B, 4),
        in_specs=_qkv_specs(hb) + [
                  pl.BlockSpec((1, ATT_BLOCK, 2 * ATT_BLOCK), lambda b, h: (hb + h, 0, 0)),
                  pl.BlockSpec((1, HEAD_DIM), const2), pl.BlockSpec((1, HEAD_DIM), const2),
                  head_in, head_in,
                  pl.BlockSpec(memory_space=pl.ANY)],
        out_specs=[head_blk,
                   pl.BlockSpec((4, ATT_BLOCK, 2 * ATT_BLOCK), lambda b, h: (0, 0, 0)),
                   pl.BlockSpec((1, HEAD_DIM), const2), pl.BlockSpec((1, HEAD_DIM), const2)],
        out_shape=[jax.ShapeDtypeStruct(dpa.shape, BF16),
                   jax.ShapeDtypeStruct((4, ATT_BLOCK, 2 * ATT_BLOCK), F32),
                   jax.ShapeDtypeStruct((1, HEAD_DIM), F32), jax.ShapeDtypeStruct((1, HEAD_DIM), F32)],
        scratch_shapes=[pltpu.VMEM((SEQ, HEAD_DIM), F32)] * 5,
        input_output_aliases={8: 0},
        compiler_params=_cparams(),
    )(pa, pa, pa, bias, qg, kg, do, dlse, dpa)


def _merge(o0, o1, o2, l0, l1, l2):
    mx = jnp.maximum(jnp.maximum(l0, l1), l2)
    e0, e1, e2 = jnp.exp(l0 - mx), jnp.exp(l1 - mx), jnp.exp(l2 - mx)
    den = e0 + e1 + e2
    return (e0 / den) * o0 + (e1 / den) * o1 + (e2 / den) * o2


def merge_fwd(os_, ls_, name):
    B = os_[0].shape[0]

    def body(o0, o1, o2, l0, l1, l2, a_ref):
        a_ref[0] = _merge(o0[0], o1[0], o2[0], l0[0], l1[0], l2[0]).astype(BF16)

    return pl.pallas_call(
        body, name=name, grid=(B, SEQ // ROW_TILE),
        in_specs=[_row_spec(512)] * 6, out_specs=_row_spec(512),
        out_shape=jax.ShapeDtypeStruct((B, SEQ, 512), BF16),
        compiler_params=_cparams(),
    )(*os_, *ls_)


def merge_bwd(os_, ls_, datt, name):
    B = os_[0].shape[0]

    def body(o0, o1, o2, l0, l1, l2, da_ref, *outs):
        _, vjp = jax.vjp(_merge, o0[0], o1[0], o2[0], l0[0], l1[0], l2[0])
        for o_ref, g in zip(outs, vjp(da_ref[0])):
            o_ref[0] = g

    return pl.pallas_call(
        body, name=name, grid=(B, SEQ // ROW_TILE),
        in_specs=[_row_spec(512)] * 7, out_specs=[_row_spec(512)] * 6,
        out_shape=[jax.ShapeDtypeStruct((B, SEQ, 512), F32)] * 6,
        compiler_params=_cparams(),
    )(*os_, *ls_, datt)


def _gate_mix(ga, gm, ya, ym):
    return jax.nn.sigmoid(ga) * ya + jax.nn.sigmoid(gm) * ym


def _gate_halves(pg):
    return [(pg, (ROW_MM_TILE, D_MODEL), lambda i: (i, 0)), (pg, (ROW_MM_TILE, D_MODEL), lambda i: (i, 1))]


def ml_out_gate(hg, w_ml_out, pg, ya, name):
    T = hg.shape[0]

    def epi(acc, i, ex, out):
        ga, gm, ya_ref = ex
        out[0][...] = acc
        out[1][...] = _gate_mix(ga[...], gm[...], ya_ref[...], acc).astype(BF16)

    return row_matmul(hg, w_ml_out, mode="nn", name=name, extras=_gate_halves(pg) + [_rows(ya)],
                      outs=[_rows_out(T, F32), _rows_out(T, BF16)], epi=epi)


def d_z_gate_bwd(dy, w_out, pg, ya, ym, name):
    T = dy.shape[0]

    def epi(acc, i, ex, out):
        ga, gm, ya_ref, ym_ref = ex
        dpg_ref, dya_ref, dym_ref = out
        _, vjp = jax.vjp(_gate_mix, ga[...], gm[...], ya_ref[...], ym_ref[...])
        dga, dgm, dya, dym = vjp(acc)
        dpg_ref[:, :D_MODEL] = dga.astype(BF16)
        dpg_ref[:, D_MODEL:] = dgm.astype(BF16)
        dya_ref[...] = dya.astype(BF16)
        dym_ref[...] = dym.astype(BF16)

    return row_matmul(dy, w_out, mode="nt", name=name, extras=_gate_halves(pg) + [_rows(ya), _rows(ym)],
                      outs=[((T, GATE_COLS), BF16, (ROW_MM_TILE, GATE_COLS), lambda i: (i, 0)),
                            _rows_out(T, BF16), _rows_out(T, BF16)], epi=epi)


def _log_sigmoid(x):
    return jnp.minimum(x, 0.0) - jnp.log(1.0 + jnp.exp(-jnp.abs(x)))


def _head_mask(e):
    lane = lax.broadcasted_iota(jnp.int32, (1, 128), 1)
    return jnp.logical_and(lane >= e * 64, lane < (e + 1) * 64).astype(F32)


def _bmxu(a, b, ca, cb):
    return lax.dot_general(a.astype(BF16), b.astype(BF16), (((ca,), (cb,)), ((0,), (0,))), preferred_element_type=F32)


@jax.custom_vjp
def cdot_nt(a, b):
    return _bmxu(a, b, 2, 2)


cdot_nt.defvjp(lambda a, b: (_bmxu(a, b, 2, 2), (a, b)),
               lambda res, g: (_bmxu(g, res[1], 2, 1), _bmxu(g, res[0], 1, 1)))


@jax.custom_vjp
def cdot_nn(a, b):
    return _bmxu(a, b, 2, 1)


cdot_nn.defvjp(lambda a, b: (_bmxu(a, b, 2, 1), (a, b)),
               lambda res, g: (_bmxu(g, res[1], 2, 2), _bmxu(res[0], g, 1, 1)))


@jax.custom_vjp
def cdot_tn(a, b):
    return _bmxu(a, b, 1, 1)


cdot_tn.defvjp(lambda a, b: (_bmxu(a, b, 1, 1), (a, b)),
               lambda res, g: (_bmxu(res[1], g, 2, 2), _bmxu(res[0], g, 2, 1)))


def _top_bits(x):
    return lax.bitcast_convert_type(lax.bitcast_convert_type(x, jnp.uint32) & jnp.uint32(0xFFFF0000), F32)


def _split3(x):
    hi = _top_bits(x)
    r = x - hi
    mid = _top_bits(r)
    return hi, mid, r - mid


def _parts_in_lanes(col):
    hi, mid, lo = _split3(col)
    lane = lax.broadcasted_iota(jnp.int32, (1, 1, 8), 2)
    return jnp.where(lane == 0, hi, jnp.where(lane == 1, mid, jnp.where(lane == 2, lo, 0.0)))


def _parts_in_rows(row):
    hi, mid, lo = _split3(row)
    sub = lax.broadcasted_iota(jnp.int32, (1, 8, 1), 1)
    return jnp.where(sub == 0, hi, jnp.where(sub == 1, mid, jnp.where(sub == 2, lo, 0.0)))


def _chunk_matrix(kind, c):
    ri = lax.broadcasted_iota(jnp.int32, (c, ML_CHUNK, ML_CHUNK), 1)
    ci = lax.broadcasted_iota(jnp.int32, (c, ML_CHUNK, ML_CHUNK), 2)
    return {"eye": ri == ci, "lower": ri >= ci, "upper": ri <= ci}[kind].astype(F32)


def _col_col(kind, col):
    out = _bmxu(_chunk_matrix(kind, col.shape[0]), _parts_in_lanes(col), 2, 1)
    return jnp.sum(out, axis=-1, keepdims=True)


def _col_row(col):
    out = _bmxu(_parts_in_lanes(col), _chunk_matrix("eye", col.shape[0]), 1, 1)
    return jnp.sum(out, axis=1, keepdims=True)


def _row_col(row):
    out = _bmxu(_chunk_matrix("eye", row.shape[0]), _parts_in_rows(row), 2, 2)
    return jnp.sum(out, axis=-1, keepdims=True)


@jax.custom_vjp
def chunk_cumsum(col):
    return _col_col("lower", col)


chunk_cumsum.defvjp(lambda col: (_col_col("lower", col), None), lambda _, g: (_col_col("upper", g),))


@jax.custom_vjp
def col_to_row(col):
    return _col_row(col)


col_to_row.defvjp(lambda col: (_col_row(col), None), lambda _, g: (_row_col(g),))


def _ml_intra(q2, k2, v, ifb, *, e):
    c, L = N_CHUNKS, ML_CHUNK
    hm = _head_mask(e)
    q3 = (q2 * hm).reshape(c, L, 128)
    k3 = (k2 * hm).reshape(c, L, 128)
    v3 = v.reshape(c, L, 128)
    if3 = ifb.reshape(c, L, 128)
    lanes = lax.broadcasted_iota(jnp.int32, (c, L, 128), 2)
    li = jnp.sum(jnp.where(lanes == e, if3, 0.0), axis=-1, keepdims=True)
    lf = _log_sigmoid(jnp.sum(jnp.where(lanes == 2 + e, if3, 0.0), axis=-1, keepdims=True))
    b = chunk_cumsum(lf)
    last = lax.broadcasted_iota(jnp.int32, (1, L, 1), 1) == L - 1
    b_end = jnp.sum(jnp.where(last, b, 0.0), axis=1, keepdims=True)
    causal = lax.broadcasted_iota(jnp.int32, (L, L), 0) >= lax.broadcasted_iota(jnp.int32, (L, L), 1)
    Dm = jnp.where(causal, b + col_to_row(li - b), -jnp.inf)
    mD = lax.stop_gradient(jnp.max(Dm, axis=-1, keepdims=True))
    P0 = cdot_nt(q3, k3) * jnp.exp(Dm - mD)
    H0 = cdot_nn(P0, v3)
    r0 = jnp.sum(P0, axis=-1, keepdims=True)
    g = b_end - b + li
    mg = lax.stop_gradient(jnp.max(g, axis=1, keepdims=True))
    kw = jnp.exp(g - mg) * k3
    return H0, r0, cdot_tn(kw, v3), jnp.sum(kw, axis=1, keepdims=True), b, b_end, mD, mg


def _ml_inter(q2, mo, gn, H0, r0, b, C_in, n_in, *, mD, m_in, e):
    c, L = N_CHUNKS, ML_CHUNK
    q3 = (q2 * _head_mask(e)).reshape(c, L, 128)
    a = b + m_in
    m_t = lax.stop_gradient(jnp.maximum(a, mD))
    c1 = jnp.exp(mD - m_t)
    c2 = jnp.exp(a - m_t)
    num = c1 * H0 + c2 * cdot_nn(q3, C_in)
    nq = c1 * r0 + c2 * jnp.sum(q3 * n_in, axis=-1, keepdims=True)
    h = num / jnp.maximum(jnp.abs(nq), jnp.exp(-m_t))
    hg = _rms(h) * gn * jax.nn.sigmoid(mo.reshape(c, L, 128))
    return hg.reshape(c * L, 128)


def _state_sweep(U_s, un_s, be_s, mg_s, Cin_s, nin_s, min_s, al_s, bt_s):
    def step(j, carry):
        C, n, m = carry
        Cin_s[j], nin_s[j], min_s[j] = C, n, m
        m_out = jnp.maximum(be_s[j] + m, mg_s[j])
        al = jnp.exp(be_s[j] + m - m_out)
        bt = jnp.exp(mg_s[j] - m_out)
        al_s[j], bt_s[j] = al, bt
        return al * C + bt * U_s[j], al * n + bt * un_s[j], m_out

    lax.fori_loop(0, N_CHUNKS, step, (jnp.zeros((128, 128), F32), jnp.zeros((1, 128), F32), jnp.zeros((1, 1), F32)))


def _state_sweep_bwd(U_s, un_s, dbe_s, Cin_s, nin_s, dCp_s, dnp_s, al_s, bt_s):
    def step(t, carry):
        j = N_CHUNKS - 1 - t
        dC, dn = carry
        al, bt = al_s[j], bt_s[j]
        U_s[j] = bt * dC
        un_s[j] = bt * dn
        dal = jnp.sum(jnp.sum(dC * Cin_s[j], axis=1, keepdims=True), axis=0, keepdims=True) \
            + jnp.sum(dn * nin_s[j], axis=1, keepdims=True)
        dbe_s[j] = dal * al
        return dCp_s[j] + al * dC, dnp_s[j] + al * dn

    lax.fori_loop(0, N_CHUNKS, step, (jnp.zeros((128, 128), F32), jnp.zeros((1, 128), F32)))


def _state_scratch():
    c = N_CHUNKS
    return [pltpu.VMEM((c, 128, 128), F32), pltpu.VMEM((c, 1, 128), F32), pltpu.VMEM((c, 1, 1), F32),
            pltpu.VMEM((c, 1, 1), F32),
            pltpu.VMEM((c, 128, 128), F32), pltpu.VMEM((c, 1, 128), F32), pltpu.VMEM((c, 1, 1), F32),
            pltpu.VMEM((c, 1, 1), F32), pltpu.VMEM((c, 1, 1), F32)]


def _shift_down(x, s):
    if s == 0:
        return x
    rows = lax.broadcasted_iota(jnp.int32, x.shape, 0)
    return jnp.where(rows >= s, pltpu.roll(x, s, 0), 0.0)


def _shift_up(x, s):
    if s == 0:
        return x
    S = x.shape[0]
    rows = lax.broadcasted_iota(jnp.int32, x.shape, 0)
    return jnp.where(rows < S - s, pltpu.roll(x, S - s, 0), 0.0)


def _conv_pre(x, cw, cb):
    y = cb + cw[3:4, :] * x
    for j in range(3):
        y = y + cw[j:j + 1, :] * _shift_down(x, 3 - j)
    return y


def _conv_bwd(x, cw, dpre):
    dx = cw[3:4, :] * dpre
    dcw = [None] * 4
    dcw[3] = jnp.sum(dpre * x, axis=0, keepdims=True)
    for j in range(3):
        dx = dx + cw[j:j + 1, :] * _shift_up(dpre, 3 - j)
        dcw[j] = jnp.sum(dpre * _shift_down(x, 3 - j), axis=0, keepdims=True)
    return dx, dcw, jnp.sum(dpre, axis=0, keepdims=True)


def _silu(z):
    return z * jax.nn.sigmoid(z)


def _dsilu(z):
    s = jax.nn.sigmoid(z)
    return s * (1.0 + z * (1.0 - s))


_ML_Q, _ML_K = slice(0, 128), slice(128, 256)
_ML_IF = slice(768, 896)


def _ml_v(e):
    return slice(256 + e * 128, 384 + e * 128)


def _ml_o(e):
    return slice(512 + e * 128, 640 + e * 128)


def _ml_specs():
    pair = lambda b, p: (b, 0, p)
    return [pl.BlockSpec((1, SEQ, ML_PAIR_COLS), pair),
            pl.BlockSpec((1, 4, 128), lambda b, p: (p, 0, 0)),
            pl.BlockSpec((1, 4, 128), lambda b, p: (4 + p, 0, 0)),
            pl.BlockSpec((1, 1, 128), lambda b, p: (p, 0, 0)),
            pl.BlockSpec((1, 1, 128), lambda b, p: (4 + p, 0, 0)),
            pl.BlockSpec((1, 1, 128), lambda b, p: (p, 0, 0)),
            pl.BlockSpec((1, 1, 256), lambda b, p: (p, 0, 0))]


def mlstm_fwd(pm, cw8, cb8, bifp, gn4, name):
    B = pm.shape[0]

    def body(p_ref, cwq, cwk, cbq, cbk, bif_ref, gn_ref, hg_ref, *st):
        U_s, un_s, be_s, mg_s, Cin_s, nin_s, min_s, al_s, bt_s = st
        qc = _silu(_conv_pre(p_ref[0, :, _ML_Q], cwq[0], cbq[0]))
        kc = _silu(_conv_pre(p_ref[0, :, _ML_K], cwk[0], cbk[0])) * (64 ** -0.5)
        ifb = p_ref[0, :, _ML_IF] + bif_ref[0]
        for e in range(2):
            lanes = slice(e * 128, (e + 1) * 128)
            H0, r0, U, un, b, b_end, mD, mg = _ml_intra(qc, kc, p_ref[0, :, _ml_v(e)], ifb, e=e)
            U_s[...], un_s[...], be_s[...], mg_s[...] = U, un, b_end, mg
            _state_sweep(*st)
            hg = _ml_inter(qc, p_ref[0, :, _ml_o(e)], gn_ref[0, :, lanes], H0, r0, b, Cin_s[...], nin_s[...],
                           mD=mD, m_in=min_s[...], e=e)
            hg_ref[0, :, lanes] = hg.astype(BF16)

    return pl.pallas_call(
        body, name=name, grid=(B, ML_PAIRS),
        in_specs=_ml_specs(),
        out_specs=pl.BlockSpec((1, SEQ, 256), lambda b, p: (b, 0, p)),
        out_shape=jax.ShapeDtypeStruct((B, SEQ, D_MODEL), BF16),
        scratch_shapes=_state_scratch(),
        compiler_params=_cparams(),
    )(pm, cw8, cw8, cb8, cb8, bifp, gn4)


def mlstm_bwd(pm, cw8, cb8, bifp, gn4, dhg, name):
    B = pm.shape[0]

    def body(p_ref, cwq, cwk, cbq, cbk, bif_ref, gn_ref, dh_ref,
             dp_ref, dcw_ref, dcb_ref, dbif_ref, dgn_ref, *scr):
        st = scr[:9]
        U_s, un_s, be_s, mg_s, Cin_s, nin_s, min_s, al_s, bt_s = st
        dCp_s, dnp_s, dbe_s = scr[9:]
        p_id = pl.program_id(1)

        @pl.when(jnp.logical_and(pl.program_id(0) == 0, p_id == 0))
        def _():
            dcw_ref[...] = jnp.zeros_like(dcw_ref)
            dcb_ref[...] = jnp.zeros_like(dcb_ref)
            dbif_ref[...] = jnp.zeros_like(dbif_ref)
            dgn_ref[...] = jnp.zeros_like(dgn_ref)

        qc = _silu(_conv_pre(p_ref[0, :, _ML_Q], cwq[0], cbq[0]))
        kc = _silu(_conv_pre(p_ref[0, :, _ML_K], cwk[0], cbk[0])) * (64 ** -0.5)
        ifb = p_ref[0, :, _ML_IF] + bif_ref[0]
        dq = jnp.zeros((SEQ, 128), F32)
        dk = jnp.zeros((SEQ, 128), F32)
        difb = jnp.zeros((SEQ, 128), F32)
        for e in range(2):
            lanes = slice(e * 128, (e + 1) * 128)
            (H0, r0, U, un, b, b_end, mD, mg), vjp1 = jax.vjp(functools.partial(_ml_intra, e=e), qc, kc,
                                                              p_ref[0, :, _ml_v(e)], ifb)
            U_s[...], un_s[...], be_s[...], mg_s[...] = U, un, b_end, mg
            _state_sweep(*st)
            _, vjp3 = jax.vjp(functools.partial(_ml_inter, mD=mD, m_in=min_s[...], e=e), qc, p_ref[0, :, _ml_o(e)],
                              gn_ref[0, :, lanes], H0, r0, b, Cin_s[...], nin_s[...])
            dq_a, dmo, dgn, dH0, dr0, db_a, dCp, dnp = vjp3(dh_ref[0, :, lanes])
            dCp_s[...], dnp_s[...] = dCp, dnp
            _state_sweep_bwd(U_s, un_s, dbe_s, Cin_s, nin_s, dCp_s, dnp_s, al_s, bt_s)
            dq_b, dk_b, dv, difb_e = vjp1((dH0, dr0, U_s[...], un_s[...], db_a, dbe_s[...],
                                           jnp.zeros_like(mD), jnp.zeros_like(mg)))
            dq, dk, difb = dq + dq_a + dq_b, dk + dk_b, difb + difb_e
            dp_ref[0, :, _ml_v(e)] = dv.astype(BF16)
            dp_ref[0, :, _ml_o(e)] = dmo.astype(BF16)
            dgn_ref[p_id, :, lanes] += dgn
        dp_ref[0, :, _ML_IF] = difb.astype(BF16)
        dbif_ref[p_id] += jnp.sum(difb, axis=0, keepdims=True)

        for (sl, cw, cb, d, blk, scale) in ((_ML_Q, cwq, cbq, dq, p_id, 1.0), (_ML_K, cwk, cbk, dk, 4 + p_id, 64 ** -0.5)):
            xr = p_ref[0, :, sl]
            dpre = d * scale * _dsilu(_conv_pre(xr, cw[0], cb[0]))
            dx, dcw, dcb = _conv_bwd(xr, cw[0], dpre)
            dp_ref[0, :, sl] = dx.astype(BF16)
            for j in range(4):
                dcw_ref[blk, j:j + 1, :] += dcw[j]
            dcb_ref[blk] += dcb

    full3 = lambda b, p: (0, 0, 0)
    return pl.pallas_call(
        body, name=name, grid=(B, ML_PAIRS),
        in_specs=[pl.BlockSpec((1, SEQ, ML_PAIR_COLS), lambda b, p: (b, 0, p), pipeline_mode=pl.Buffered(1))]
        + _ml_specs()[1:] + [pl.BlockSpec((1, SEQ, 256), lambda b, p: (b, 0, p), pipeline_mode=pl.Buffered(1))],
        out_specs=[pl.BlockSpec((1, SEQ, ML_PAIR_COLS), lambda b, p: (b, 0, p)),
                   pl.BlockSpec((8, 4, 128), full3), pl.BlockSpec((8, 1, 128), full3),
                   pl.BlockSpec((4, 1, 128), full3), pl.BlockSpec((4, 1, 256), full3)],
        out_shape=[jax.ShapeDtypeStruct((B, SEQ, ML_COLS), BF16),
                   jax.ShapeDtypeStruct((8, 4, 128), F32), jax.ShapeDtypeStruct((8, 1, 128), F32),
                   jax.ShapeDtypeStruct((4, 1, 128), F32), jax.ShapeDtypeStruct((4, 1, 256), F32)],
        scratch_shapes=_state_scratch() + [pltpu.VMEM((N_CHUNKS, 128, 128), F32), pltpu.VMEM((N_CHUNKS, 1, 128), F32),
                                           pltpu.VMEM((N_CHUNKS, 1, 1), F32)],
        compiler_params=_cparams(),
    )(pm, cw8, cw8, cb8, cb8, bifp, gn4, dhg)


def _adamw(w, g, m, v):
    m = ADAM_B1 * m + (1.0 - ADAM_B1) * g
    v = ADAM_B2 * v + (1.0 - ADAM_B2) * (g * g)
    m_hat = m / (1.0 - ADAM_B1 ** ADAM_STEP)
    v_hat = v / (1.0 - ADAM_B2 ** ADAM_STEP)
    delta = -ADAM_LR * (m_hat / (jnp.sqrt(v_hat) + ADAM_EPS) + ADAM_WD * w)
    return delta, m, v


def adamw(w, g, m, v, name, parts=False):
    R, C = w.shape
    if R % 8 == 0 or R * C * 4 <= (1 << 20):
        tr = _pick(R, (256, 128, 64, 32, 16, 8)) if R * C * 4 > (1 << 20) else R
        steps = R // tr
        spec = pl.BlockSpec((tr, C), lambda i: (i, 0))
        g_spec = pl.BlockSpec((N_DEV, tr, C), lambda i: (0, i, 0)) if parts else spec
    else:
        tc = _pick(C, (256, 128))
        steps = C // tc
        spec = pl.BlockSpec((R, tc), lambda i: (0, i))
        g_spec = pl.BlockSpec((N_DEV, R, tc), lambda i: (0, 0, i)) if parts else spec

    def body(w_ref, g_ref, m_ref, v_ref, go_ref, d_ref, mo_ref, vo_ref):
        if parts:
            g = g_ref[0].astype(F32)
            for k in range(1, N_DEV):
                g = g + g_ref[k].astype(F32)
        else:
            g = g_ref[...]
        d, mn, vn = _adamw(w_ref[...], g, m_ref[...], v_ref[...])
        go_ref[...], d_ref[...], mo_ref[...], vo_ref[...] = g, d, mn, vn

    return pl.pallas_call(
        body, name=name, grid=(steps,),
        in_specs=[spec, g_spec, spec, spec], out_specs=[spec] * 4,
        out_shape=[jax.ShapeDtypeStruct((R, C), F32)] * 4,
        compiler_params=_cparams(),
    )(w, g, m, v)


def adamw_many(ws, gs, ms, vs, name):
    n = len(ws)

    def fn(*a):
        out = []
        for j in range(n):
            out += list(_adamw(a[j], a[n + j], a[2 * n + j], a[3 * n + j]))
        return tuple(out)

    shapes = [jax.ShapeDtypeStruct(w.shape, F32) for w in ws for _ in range(3)]
    return small_call(fn, list(ws) + list(gs) + list(ms) + list(vs), shapes, name)


def _mesh_pos():
    return lax.axis_index("x"), lax.axis_index("y"), lax.axis_index("c")


def _flip(pos, f):
    x, y, c = pos
    return (1 - x if f & 4 else x, 1 - y if f & 2 else y, 1 - c if f & 1 else c)


def _index(pos):
    return 4 * pos[0] + 2 * pos[1] + pos[2]


def _exchange(arrs, name, scatter):
    n = len(arrs)

    def body(*refs):
        ins, outs = refs[:n], refs[n:2 * n]
        send, recv, lsem = refs[2 * n:]
        me = _mesh_pos()
        mine = _index(me)
        copies = []
        for i in range(n):
            src = ins[i].at[mine] if scatter else ins[i]
            loc = pltpu.make_async_copy(src, outs[i].at[mine], lsem.at[i])
            loc.start()
            copies.append(loc)
            for f in range(1, N_DEV):
                peer = _flip(me, f)
                src = ins[i].at[_index(peer)] if scatter else ins[i]
                cp = pltpu.make_async_remote_copy(
                    src_ref=src, dst_ref=outs[i].at[mine],
                    send_sem=send.at[i * 7 + f - 1], recv_sem=recv.at[i * 7 + f - 1],
                    device_id=peer, device_id_type=pl.DeviceIdType.MESH)
                cp.start()
                copies.append(cp)
        for cp in copies:
            cp.wait()

    any_spec = pl.BlockSpec(memory_space=pl.ANY)
    out_shape = [jax.ShapeDtypeStruct(a.shape if scatter else (N_DEV,) + a.shape, a.dtype) for a in arrs]
    res = pl.pallas_call(
        body, name=name,
        in_specs=[any_spec] * n, out_specs=[any_spec] * n, out_shape=out_shape,
        scratch_shapes=[pltpu.SemaphoreType.DMA((7 * n,)), pltpu.SemaphoreType.DMA((7 * n,)),
                        pltpu.SemaphoreType.DMA((n,))],
        compiler_params=_cparams(),
    )(*arrs)
    return list(res)


def all_gather(arrs, name):
    return _exchange(arrs, name, False)


def all_gather_two_level(arrs, name):
    n = len(arrs)

    def body(*refs):
        ins, outs = refs[:n], refs[n:2 * n]
        send, recv, lsem = refs[2 * n:]
        x, y, c = _mesh_pos()
        me, sibling = (x, y, c), (x, y, 1 - c)
        chips = [(1 - x, y), (x, 1 - y), (1 - x, 1 - y)]

        def copy(i, k, block, to, src=None):
            rows = outs[i].at[_index(block)]
            return pltpu.make_async_remote_copy(
                src_ref=rows if src is None else src, dst_ref=rows,
                send_sem=send.at[i * 7 + k], recv_sem=recv.at[i * 7 + k],
                device_id=to, device_id_type=pl.DeviceIdType.MESH)

        local = [pltpu.make_async_copy(ins[i], outs[i].at[_index(me)], lsem.at[i]) for i in range(n)]
        first = [copy(i, 0, me, sibling, src=ins[i]) for i in range(n)]
        first += [copy(i, 1 + j, me, (*chip, c), src=ins[i]) for i in range(n) for j, chip in enumerate(chips)]
        for cp in local + first:
            cp.start()
        passed = []
        for j, chip in enumerate(chips):
            for i in range(n):
                copy(i, 1 + j, (*chip, c), me).wait_recv()
                cp = copy(i, 4 + j, (*chip, c), sibling)
                cp.start()
                passed.append(cp)
        for i in range(n):
            copy(i, 0, sibling, me).wait_recv()
            for j, chip in enumerate(chips):
                copy(i, 4 + j, (*chip, 1 - c), me).wait_recv()
        for cp in first + passed:
            cp.wait_send()
        for cp in local:
            cp.wait()

    any_spec = pl.BlockSpec(memory_space=pl.ANY)
    res = pl.pallas_call(
        body, name=name,
        in_specs=[any_spec] * n, out_specs=[any_spec] * n,
        out_shape=[jax.ShapeDtypeStruct((N_DEV,) + a.shape, a.dtype) for a in arrs],
        scratch_shapes=[pltpu.SemaphoreType.DMA((7 * n,)), pltpu.SemaphoreType.DMA((7 * n,)),
                        pltpu.SemaphoreType.DMA((n,))],
        compiler_params=_cparams(),
    )(*arrs)
    return list(res)


def all_to_all(arrs, name):
    return _exchange(arrs, name, True)


_HBM = pl.BlockSpec(memory_space=pltpu.HBM)
_SEM = pl.BlockSpec(memory_space=pltpu.SEMAPHORE)
_EFFECT = pltpu.SideEffectType.DATAFLOW_SIDE_EFFECTING


def _split_copies(ins, lands, send, recv, scatter, waiting):
    me = _mesh_pos()
    mine = _index(me)
    copies = []
    for i in range(len(ins)):
        for f in range(1, N_DEV):
            peer = _flip(me, f)
            src = ins[i].at[_index(peer)] if scatter else ins[i]
            copies.append(pltpu.make_async_remote_copy(
                src_ref=src, dst_ref=lands[i].at[_index(peer) if waiting else mine],
                send_sem=send.at[i * 7 + f - 1], recv_sem=recv.at[i * 7 + f - 1],
                device_id=peer, device_id_type=pl.DeviceIdType.MESH))
    return copies


def exchange_start(arrs, name, scatter, after=()):
    n = len(arrs)
    land_shapes = [a.shape if scatter else (N_DEV,) + a.shape for a in arrs]

    def body(*refs):
        ins, lands = refs[:n], refs[n:2 * n]
        send, recv = refs[2 * n + len(after)], refs[2 * n + len(after) + 1]
        token = refs[-1]
        for cp in _split_copies(ins, lands, send, recv, scatter, False):
            cp.start()
        token[...] = jnp.zeros_like(token)

    res = pl.pallas_call(
        body, name=name,
        out_shape=(pltpu.SemaphoreType.DMA((7 * n,)), pltpu.SemaphoreType.DMA((7 * n,)),
                   *[pltpu.HBM(a.shape, a.dtype) for a in arrs],
                   *[pltpu.HBM(s, a.dtype) for s, a in zip(land_shapes, arrs)],
                   jax.ShapeDtypeStruct((8, 128), F32)),
        in_specs=[_HBM] * (2 * n) + [pl.BlockSpec(memory_space=pl.ANY)] * len(after),
        out_specs=(_SEM, _SEM, *[_HBM] * (2 * n), pl.BlockSpec(memory_space=pltpu.VMEM)),
        input_output_aliases={i: 2 + i for i in range(2 * n)},
        compiler_params=pltpu.CompilerParams(has_side_effects=_EFFECT),
    )(*[pltpu.with_memory_space_constraint(a, pltpu.HBM) for a in arrs],
      *[pltpu.with_memory_space_constraint(lax.empty(s, a.dtype), pltpu.HBM) for s, a in zip(land_shapes, arrs)],
      *after)
    return (res[0], res[1], list(res[2:2 + n]), list(res[2 + n:2 + 2 * n])), res[-1]


def exchange_wait(handle, after, name, scatter):
    send, recv, srcs, lands = handle
    n = len(srcs)

    def body(*refs):
        ins, lnd = refs[:n], refs[n:2 * n]
        send_, recv_ = refs[2 * n], refs[2 * n + 1]
        for cp in _split_copies(ins, lnd, send_, recv_, scatter, True):
            cp.wait_send()
            cp.wait_recv()

    res = pl.pallas_call(
        body, name=name,
        out_shape=(*[pltpu.HBM(a.shape, a.dtype) for a in srcs], *[pltpu.HBM(a.shape, a.dtype) for a in lands]),
        in_specs=[_HBM] * (2 * n) + [_SEM, _SEM, pl.BlockSpec(memory_space=pl.ANY)],
        out_specs=tuple([_HBM] * (2 * n)),
        input_output_aliases={i: i for i in range(2 * n)},
        compiler_params=pltpu.CompilerParams(has_side_effects=_EFFECT),
    )(*srcs, *lands, send, recv, after)
    return list(res[n:])


def _own_slot(land, own):
    return lax.dynamic_update_slice(land, own[None], (_index(_mesh_pos()),) + (0,) * own.ndim)


def cast_bf16(arrs, name):
    outs = []
    for i, a in enumerate(arrs):
        R, C = a.shape
        if R % 8 == 0:
            tr = _pick(R, (256, 128, 64, 32, 16, 8)) if R * C * 4 > (1 << 21) else R
            steps, spec = R // tr, pl.BlockSpec((tr, C), lambda i: (i, 0))
        else:
            steps, spec = C // 256, pl.BlockSpec((R, 256), lambda i: (0, i))

        def body(a_ref, o_ref):
            o_ref[...] = a_ref[...].astype(BF16)

        outs.append(pl.pallas_call(body, name=f"{name}_{i}", grid=(steps,), in_specs=[spec], out_specs=spec,
                                   out_shape=jax.ShapeDtypeStruct((R, C), BF16), compiler_params=_cparams())(a))
    return outs


def sum_parts(parts, name):
    def fn(p):
        g = p[0]
        for k in range(1, N_DEV):
            g = g + p[k]
        return (g,)
    return small_call(fn, [parts], [jax.ShapeDtypeStruct(parts.shape[1:], F32)], name)[0]


_SPLITS = np.cumsum([1536, 1536, 1536, 512, 512, 1024, 1024, 8, 8, 2048])[:-1].tolist()


def split_w_in(w):
    aq, ak, av, mq, mk, mv, mo, mi, mf, gates = jnp.split(w, _SPLITS, axis=1)
    R = w.shape[0]
    w_att = jnp.stack([aq.reshape(R, 12, 128), ak.reshape(R, 12, 128), av.reshape(R, 12, 128)], axis=2)
    gif = jnp.concatenate([mi.reshape(R, 4, 2), mf.reshape(R, 4, 2), jnp.zeros((R, 4, 124), w.dtype)], axis=2)
    w_ml = jnp.concatenate([mq.reshape(R, 4, 128), mk.reshape(R, 4, 128), mv.reshape(R, 4, 256),
                            mo.reshape(R, 4, 256), gif], axis=2)
    return w_att.reshape(R, ATT_COLS), w_ml.reshape(R, ML_COLS), gates


def merge_w_in(g_att, g_ml, g_gate):
    R = g_att.shape[0]
    a = g_att.reshape(R, 12, 3, 128)
    m = g_ml.reshape(R, 4, ML_PAIR_COLS)
    gif = m[:, :, 768:772]
    return jnp.concatenate([
        a[:, :, 0].reshape(R, 1536), a[:, :, 1].reshape(R, 1536), a[:, :, 2].reshape(R, 1536),
        m[:, :, 0:128].reshape(R, 512), m[:, :, 128:256].reshape(R, 512),
        m[:, :, 256:512].reshape(R, 1024), m[:, :, 512:768].reshape(R, 1024),
        gif[:, :, 0:2].reshape(R, 8), gif[:, :, 2:4].reshape(R, 8), g_gate], axis=1)


def _blk8(v, width=128):
    r = v.shape[0]
    return v.reshape(r, 1024 // width, width).transpose(1, 0, 2)


def _unblk8(v):
    nb, r, w = v.shape
    return v.transpose(1, 0, 2).reshape(r, nb * w)


def local_step(x, target, mods, w, small, late_w=None, early_g=None, w_in_g=None):
    late_w = late_w or (lambda after: w)
    big = {}
    early_g = early_g or (lambda g: big.update(g))
    w_in_g = w_in_g or (lambda g: big.update(w_in=g))
    B = x.shape[0]
    T = B * SEQ
    shift1, scale1, gate1, shift2, scale2, gate2 = mods
    f2 = lambda a: a.reshape(T, a.shape[-1])
    f3 = lambda a: a.reshape(B, SEQ, a.shape[-1])

    rel_t = jnp.pad(small["rel_bias"].T, ((0, 4), (0, 0)))
    onehots = [_bucket_onehot(d) for _, d in ATT_GROUPS]
    biases = [bias_expand(rel_t, oh, f"bias_expand{g}").reshape(16, ATT_BLOCK, 2 * ATT_BLOCK)
              for g, oh in enumerate(onehots)]
    qg, kg = small["q_norm_g"], small["k_norm_g"]
    cw8 = _blk8(small["conv_w"])
    cb8 = _blk8(small["conv_b"])
    b_if = small["b_if"].reshape(2, 4, 2)
    bifp = jnp.concatenate([b_if[0], b_if[1], jnp.zeros((4, 124), F32)], axis=1).reshape(4, 1, 128)
    gn4 = small["mlstm_norm_g"].reshape(4, 1, 256)

    u = modnorm_fwd(x, small["norm1_g"], scale1, shift1, "modnorm1")
    u2d = f2(u)
    pa = f3(matmul(u2d, w["w_att"], mode="nn", name="proj_att"))
    pm = f3(matmul(u2d, w["w_ml"], mode="nn", name="proj_ml"))
    pg = matmul(u2d, w["w_gate"], mode="nn", name="proj_gate")
    os_, ls_ = [], []
    for g in range(3):
        o, l = attn_fwd(pa, biases[g], qg, kg, g, f"attn_fwd{g}")
        os_.append(o)
        ls_.append(l)
    att = merge_fwd(os_, ls_, "merge_fwd")
    hg = mlstm_fwd(pm, cw8, cb8, bifp, gn4, "mlstm_fwd")
    w = {**w, **late_w(hg)}
    y_att = matmul(f2(att), w["w_att_out"], mode="nn", name="att_out")
    y_ml, z = ml_out_gate(f2(hg), w["w_ml_out"], pg, y_att, "ml_out")
    y, x1, u2 = out_proj_resid_modnorm(f2(z), w["w_out"], f2(x), gate1, small["norm2_g"], scale2, shift2, "out_proj")
    pre, hdn = matmul(u2, w["w_ff1"], mode="nn", name="ff1", out_dtypes=(BF16, BF16),
                      epi=lambda acc: (acc, jnp.square(jnp.maximum(acc, 0.0))))
    dx2, d_ffo, loss, d_gate2 = ff2_loss(hdn, w["w_ff2"], x1, gate2, f2(target), "ff2_loss")

    g_ff2 = matmul(hdn, d_ffo, mode="tn", name="g_ff2", out_dtypes=(BF16,))
    d_pre = matmul(d_ffo, w["w_ff2"], mode="nt", name="d_hdn", out_dtypes=(BF16,), extras=(pre,),
                   epi=lambda acc, p: (acc * (2.0 * jnp.maximum(p.astype(F32), 0.0)),))
    g_ff1 = matmul(u2, d_pre, mode="tn", name="g_ff1", out_dtypes=(BF16,))
    dx1, d_norm2, d_scale2, d_shift2, dy, d_gate1 = d_u_modnorm_bwd(
        d_pre, w["w_ff1"], None, x1, small["norm2_g"], scale2, shift2, dx2, y, gate1, "d_u2")
    g_out = matmul(f2(z), dy, mode="tn", name="g_out", out_dtypes=(BF16,))
    dpg, d_ya, d_ym = d_z_gate_bwd(dy, w["w_out"], pg, y_att, y_ml, "d_z")
    g_att_out = matmul(f2(att), f2(d_ya), mode="tn", name="g_att_out", out_dtypes=(BF16,))
    d_att = matmul(f2(d_ya), w["w_att_out"], mode="nt", name="d_att")
    g_ml_out = matmul(f2(hg), f2(d_ym), mode="tn", name="g_ml_out", out_dtypes=(BF16,))
    d_hg = matmul(f2(d_ym), w["w_ml_out"], mode="nt", name="d_hg")
    started = early_g(dict(w_att_out=g_att_out, w_ml_out=g_ml_out, w_out=g_out, w_ff1=g_ff1, w_ff2=g_ff2))
    order = 0.0 if started is None else started[0, 0]
    dmerge = merge_bwd(os_, ls_, f3(d_att), "merge_bwd")
    dpa = lax.empty((B, SEQ, ATT_COLS), BF16)
    d_rel = []
    d_qg = d_kg = None
    for g in range(3):
        dpa, dbias, dq_g, dk_g = attn_bwd(pa, biases[g], qg + order, kg, dmerge[g], dmerge[3 + g], dpa, g,
                                          f"attn_bwd{g}")
        db8 = jnp.pad(dbias.reshape(4, -1), ((0, 4), (0, 0)))
        d_rel.append(bias_reduce(db8, onehots[g], f"bias_reduce{g}")[:4])
        d_qg = dq_g if d_qg is None else d_qg + dq_g
        d_kg = dk_g if d_kg is None else d_kg + dk_g
    dpm, dcw8, dcb8, dbifp, dgn4 = mlstm_bwd(pm, cw8, cb8, bifp, gn4 + order, f3(d_hg), "mlstm_bwd")
    g_w_att = matmul(u2d, f2(dpa), mode="tn", name="g_w_att", out_dtypes=(BF16,))
    g_w_ml = matmul(u2d, f2(dpm), mode="tn", name="g_w_ml", out_dtypes=(BF16,))
    g_w_gate = matmul(u2d, f2(dpg), mode="tn", name="g_w_gate", out_dtypes=(BF16,))
    started = w_in_g(merge_w_in(g_w_att, g_w_ml, g_w_gate))
    du = matmul(f2(dpa), w["w_att"], mode="nt", name="d_u_att", after=() if started is None else (started,))
    du = matmul(f2(dpm), w["w_ml"], mode="nt", name="d_u_ml", extras=(du,), epi=lambda acc, e: (acc + e,))
    grad_x, d_norm1, d_scale1, d_shift1 = d_u_modnorm_bwd(
        f2(dpg), w["w_gate"], du, f2(x), small["norm1_g"], scale1, shift1, dx1, None, None, "d_u_gate")
    grad_x = f3(grad_x)

    d_mods = (d_shift1, d_scale1, d_gate1, d_shift2, d_scale2, d_gate2)
    dbif = dbifp.reshape(4, 128)
    small_g = dict(
        norm1_g=d_norm1, norm2_g=d_norm2,
        b_if=jnp.stack([dbif[:, 0:2].reshape(8), dbif[:, 2:4].reshape(8)]),
        conv_w=_unblk8(dcw8), conv_b=_unblk8(dcb8), q_norm_g=d_qg, k_norm_g=d_kg,
        rel_bias=jnp.concatenate(d_rel, axis=0).T,
        mlstm_norm_g=dgn4.reshape(1, 1024))
    return loss, grad_x, d_mods, big, small_g


_SMALL = (("b_ada", 6144), ("norm1_g", 1024), ("norm2_g", 1024), ("b_if", 16), ("conv_b", 1024),
          ("q_norm_g", 128), ("k_norm_g", 128), ("rel_bias", 384), ("mlstm_norm_g", 1024), ("conv_w", 4096))
_SMALL_ROWS = 120
_REPL = _SMALL[:-1]


def _pack(d, names, rows):
    flat = jnp.concatenate([d[k].reshape(-1) for k, _ in names])
    return jnp.pad(flat, (0, rows * 128 - flat.shape[0])).reshape(rows, 128)


def _unpack(slab, names, shapes):
    flat = slab.reshape(-1)
    out, off = {}, 0
    for k, nel in names:
        out[k] = flat[off:off + nel].reshape(shapes[k])
        off += nel
    return out


def kernel(x, c, w_ada, b_ada, norm1_g, norm2_g, w_in, b_if, conv_w, conv_b, q_norm_g, k_norm_g, rel_bias, mlstm_norm_g, w_att_out, w_ml_out, w_out, w_ff1, w_ff2, loss_target, m_w_ada, m_b_ada, m_norm1_g, m_norm2_g, m_w_in, m_b_if, m_conv_w, m_conv_b, m_q_norm_g, m_k_norm_g, m_rel_bias, m_mlstm_norm_g, m_w_att_out, m_w_ml_out, m_w_out, m_w_ff1, m_w_ff2, v_w_ada, v_b_ada, v_norm1_g, v_norm2_g, v_w_in, v_b_if, v_conv_w, v_conv_b, v_q_norm_g, v_k_norm_g, v_rel_bias, v_mlstm_norm_g, v_w_att_out, v_w_ml_out, v_w_out, v_w_ff1, v_w_ff2):
    P = dict(w_ada=w_ada, b_ada=b_ada, norm1_g=norm1_g, norm2_g=norm2_g, w_in=w_in, b_if=b_if, conv_w=conv_w,
             conv_b=conv_b, q_norm_g=q_norm_g, k_norm_g=k_norm_g, rel_bias=rel_bias, mlstm_norm_g=mlstm_norm_g,
             w_att_out=w_att_out, w_ml_out=w_ml_out, w_out=w_out, w_ff1=w_ff1, w_ff2=w_ff2)
    M = dict(w_ada=m_w_ada, b_ada=m_b_ada, norm1_g=m_norm1_g, norm2_g=m_norm2_g, w_in=m_w_in, b_if=m_b_if,
             conv_w=m_conv_w, conv_b=m_conv_b, q_norm_g=m_q_norm_g, k_norm_g=m_k_norm_g, rel_bias=m_rel_bias,
             mlstm_norm_g=m_mlstm_norm_g, w_att_out=m_w_att_out, w_ml_out=m_w_ml_out, w_out=m_w_out,
             w_ff1=m_w_ff1, w_ff2=m_w_ff2)
    V = dict(w_ada=v_w_ada, b_ada=v_b_ada, norm1_g=v_norm1_g, norm2_g=v_norm2_g, w_in=v_w_in, b_if=v_b_if,
             conv_w=v_conv_w, conv_b=v_conv_b, q_norm_g=v_q_norm_g, k_norm_g=v_k_norm_g, rel_bias=v_rel_bias,
             mlstm_norm_g=v_mlstm_norm_g, w_att_out=v_w_att_out, w_ml_out=v_w_ml_out, w_out=v_w_out,
             w_ff1=v_w_ff1, w_ff2=v_w_ff2)
    names = list(P)
    shapes = {k: P[k].shape for k in names}
    B = x.shape[0]
    me = _index(_mesh_pos())

    big_names = ("w_in", "w_att_out", "w_ml_out", "w_out", "w_ff1", "w_ff2")
    shards = cast_bf16([P[k][0] for k in big_names], "cast_w")
    w_in_g8, c8, conv_w8 = all_gather_two_level([shards[0], c, conv_w[0]], "gather_w_in")
    c_all = c8.reshape(N_DEV * B, D_MODEL)
    conv_w_full = conv_w8.transpose(1, 0, 2).reshape(4, 1024)
    w_att, w_ml, w_gate = split_w_in(w_in_g8.transpose(1, 0, 2).reshape(D_MODEL, D_IN))
    w = dict(w_att=w_att, w_ml=w_ml, w_gate=w_gate)

    (silu_c,) = small_call(lambda a: (_silu(a),), [c_all], [jax.ShapeDtypeStruct(c_all.shape, F32)], "silu_c")
    b_ada_cols = lax.dynamic_slice(b_ada, (0, me * 768), (1, 768))
    ada_cols = matmul(silu_c, w_ada[0], mode="nn", name="ada", extras=(jnp.broadcast_to(b_ada_cols, (N_DEV * B, 768)),),
                      epi=lambda acc, bb: (acc + bb,))
    (ada_t,) = all_to_all([ada_cols.reshape(N_DEV, B, 768)], "ada_exchange")
    ada = ada_t.transpose(1, 0, 2).reshape(B, 6 * D_MODEL)
    mods = tuple(ada[:, i * D_MODEL:(i + 1) * D_MODEL].reshape(B, 1, D_MODEL) for i in range(6))

    late_handle, late_order = exchange_start(shards[1:], "gather_late_start", False, after=(ada_t,))

    def late_w(after):
        lands = exchange_wait(late_handle, after, "gather_late_wait", False)
        gw = dict(zip(big_names[1:], [_own_slot(l, s) for l, s in zip(lands, shards[1:])]))
        return dict(w_att_out=gw["w_att_out"].transpose(1, 0, 2).reshape(512, D_MODEL),
                    w_ml_out=gw["w_ml_out"].reshape(D_MODEL, D_MODEL), w_out=gw["w_out"].reshape(D_MODEL, D_MODEL),
                    w_ff1=gw["w_ff1"].transpose(1, 0, 2).reshape(D_MODEL, D_FF),
                    w_ff2=gw["w_ff2"].reshape(D_FF, D_MODEL))

    pending = {}

    def send_grads(key, blocks, name):
        handle, order = exchange_start(blocks, name, True)
        pending[key] = (handle, [lax.dynamic_index_in_dim(b, me, 0, keepdims=False) for b in blocks])
        return order

    def early_g(g):
        return send_grads("late", [g["w_att_out"].reshape(512, N_DEV, 128).transpose(1, 0, 2),
                                   g["w_ml_out"].reshape(N_DEV, 128, D_MODEL), g["w_out"].reshape(N_DEV, 128, D_MODEL),
                                   g["w_ff1"].reshape(D_MODEL, N_DEV, 512).transpose(1, 0, 2),
                                   g["w_ff2"].reshape(N_DEV, 512, D_MODEL)], "grad_late_start")

    def w_in_g(g):
        return send_grads("w_in", [g.reshape(D_MODEL, N_DEV, W_IN_SHARD).transpose(1, 0, 2)], "grad_w_in_start")

    def recv_grads(key, after, name):
        handle, own = pending[key]
        return [_own_slot(l, o) for l, o in zip(exchange_wait(handle, after, name, True), own)]

    small = dict(norm1_g=norm1_g + late_order[0, 0], norm2_g=norm2_g, b_if=b_if[0], conv_w=conv_w_full, conv_b=conv_b,
                 q_norm_g=q_norm_g, k_norm_g=k_norm_g, rel_bias=rel_bias, mlstm_norm_g=mlstm_norm_g)
    loss, grad_x, d_mods, _, small_g = local_step(x, loss_target, mods, w, small, late_w, early_g, w_in_g)
    loss = lax.psum(loss[0, 0], ("x", "y", "c"))

    d_ada = jnp.concatenate([d.reshape(B, D_MODEL) for d in d_mods], axis=1)
    (d_ada_t,) = all_to_all([d_ada.reshape(B, N_DEV, 768).transpose(1, 0, 2)], "d_ada_exchange")
    d_ada_cols = d_ada_t.reshape(N_DEV * B, 768)
    g_w_ada = matmul(silu_c, d_ada_cols, mode="tn", name="g_w_ada")
    (g_b_cols,) = small_call(lambda a: (jnp.sum(a, axis=0, keepdims=True),), [d_ada_cols],
                             [jax.ShapeDtypeStruct((1, 768), F32)], "g_b_ada_cols")
    small_g["b_ada"] = lax.dynamic_update_slice(jnp.zeros((1, 6144), F32), g_b_cols, (0, me * 768))

    recv = recv_grads("w_in", grad_x, "grad_w_in_wait") + recv_grads("late", grad_x, "grad_late_wait")
    (small_parts,) = all_gather([_pack(small_g, _SMALL, _SMALL_ROWS)], "small_grad_gather")
    small_sum = sum_parts(small_parts, "small_grad_sum")
    sg = _unpack(small_sum, _SMALL, {**{k: shapes[k] for k, _ in _REPL}, "conv_w": (4, 1024)})

    G, Dl, NM, NV = {}, {}, {}, {}
    for k, parts in zip(big_names, recv):
        g, d, nm, nv = adamw(P[k][0], parts, M[k][0], V[k][0], f"adamw_{k}", parts=True)
        G[k], Dl[k], NM[k], NV[k] = g[None], d[None], nm[None], nv[None]
    g, d, nm, nv = adamw(w_ada[0], g_w_ada, m_w_ada[0], v_w_ada[0], "adamw_w_ada")
    G["w_ada"], Dl["w_ada"], NM["w_ada"], NV["w_ada"] = g[None], d[None], nm[None], nv[None]
    g_conv = lax.dynamic_slice(sg["conv_w"], (0, me * 128), (4, 128))
    g, d, nm, nv = adamw(conv_w[0], g_conv, m_conv_w[0], v_conv_w[0], "adamw_conv_w")
    G["conv_w"], Dl["conv_w"], NM["conv_w"], NV["conv_w"] = g[None], d[None], nm[None], nv[None]
    flat2 = lambda a: a.reshape(-1, a.shape[-1])
    keys = [k for k, _ in _REPL]
    upd = adamw_many([flat2(P[k]) for k in keys], [flat2(sg[k]) for k in keys], [flat2(M[k]) for k in keys],
                     [flat2(V[k]) for k in keys], "adamw_small")
    for j, k in enumerate(keys):
        G[k] = sg[k]
        Dl[k], NM[k], NV[k] = [upd[3 * j + t].reshape(shapes[k]) for t in range(3)]

    return (loss, grad_x, *[G[k] for k in names], *[Dl[k] for k in names], *[NM[k] for k in names],
            *[NV[k] for k in names])
```

```python
import functools
import math

import numpy as np
import jax
import jax.numpy as jnp
from jax import lax
from jax.experimental import pallas as pl
from jax.experimental.pallas import tpu as pltpu

F32 = jnp.float32
BF16 = jnp.bfloat16

N_DEV = 8
D_MODEL = 1024
SEQ = 2048
ATT_GROUPS = ((128, 1), (512, 4), (2048, 16))
N_ATT_HEADS = 12
ATT_BLOCK = 128
HEAD_DIM = 128
ML_HEADS = 8
ML_PAIRS = 4
ML_CHUNK = 64
N_CHUNKS = SEQ // ML_CHUNK
N_BUCKETS = 32
MAX_DISTANCE = 2048
D_FF = 4096
D_IN = 9744
EPS = 1e-6

ADAM_LR = 0.001
ADAM_B1 = 0.9
ADAM_B2 = 0.999
ADAM_EPS = 1e-08
ADAM_WD = 0.01
ADAM_STEP = 10

ATT_HEAD_COLS = 3 * HEAD_DIM
ATT_COLS = N_ATT_HEADS * ATT_HEAD_COLS
ML_PAIR_COLS = 896
ML_COLS = ML_PAIRS * ML_PAIR_COLS
GATE_COLS = 2 * D_MODEL
W_IN_SHARD = D_IN // N_DEV

VMEM_LIMIT = 60 * 1024 * 1024


def _cparams(**kw):
    return pltpu.CompilerParams(vmem_limit_bytes=VMEM_LIMIT, **kw)


_NN = ((1,), (0,))
_NT = ((1,), (1,))
_TN = ((0,), (0,))


def _mxu(a, b, dims):
    return lax.dot_general(a.astype(BF16), b.astype(BF16), (dims, ((), ())), preferred_element_type=F32)


@jax.custom_vjp
def bdot_nn(a, b):
    return _mxu(a, b, _NN)


def _nn_fwd(a, b):
    return _mxu(a, b, _NN), (a, b)


def _nn_bwd(res, g):
    a, b = res
    return _mxu(g, b, _NT), _mxu(a, g, _TN)


bdot_nn.defvjp(_nn_fwd, _nn_bwd)


@jax.custom_vjp
def bdot_nt(a, b):
    return _mxu(a, b, _NT)


def _nt_fwd(a, b):
    return _mxu(a, b, _NT), (a, b)


def _nt_bwd(res, g):
    a, b = res
    return _mxu(g, b, _NN), _mxu(g, a, _TN)


bdot_nt.defvjp(_nt_fwd, _nt_bwd)


def _doth(a, b, dims=_NN):
    return lax.dot_general(a, b, (dims, ((), ())), precision=lax.Precision.HIGHEST, preferred_element_type=F32)


def _rms(x):
    return x * lax.rsqrt(jnp.mean(x * x, axis=-1, keepdims=True) + EPS)


def _pick(n, cands):
    for t in cands:
        if n % t == 0:
            return t
    raise ValueError(f"no tile for {n}")


MM_TILE_M = (1024, 512, 256, 128, 64, 32, 16, 8)
MM_TILE_N = (2048, 1792, 1536, 1024, 768, 512, 256, 128)
MM_TILE_K = (2048, 1792, 1536, 1024, 512, 256, 128, 64, 32)

def matmul(a, b, *, mode, name, out_dtypes=(F32,), epi=None, extras=(), after=()):
    if mode == "nn":
        (M, K), (K2, N) = a.shape, b.shape
    elif mode == "nt":
        (M, K), (N, K2) = a.shape, b.shape
    else:
        (K, M), (K2, N) = a.shape, b.shape
    assert K == K2, (a.shape, b.shape, mode)
    tm = _pick(M, MM_TILE_M)
    tn = _pick(N, MM_TILE_N)
    tk = _pick(K, MM_TILE_K)
    nk = K // tk
    n_ex = len(extras)
    n_out = len(out_dtypes)
    dims = {"nn": _NN, "nt": _NT, "tn": _TN}[mode]

    def finish(r, ex_refs, out_refs):
        outs = epi(r, *[e[...] for e in ex_refs]) if epi is not None else (r,)
        for o_ref, o in zip(out_refs, outs):
            o_ref[...] = o.astype(o_ref.dtype)

    def body(*refs):
        a_ref, b_ref = refs[0], refs[1]
        ex_refs = refs[2:2 + n_ex]
        out_refs = refs[2 + n_ex + len(after):2 + n_ex + len(after) + n_out]
        if nk == 1:
            finish(_mxu(a_ref[...], b_ref[...], dims), ex_refs, out_refs)
            return
        acc = refs[2 + n_ex + len(after) + n_out]
        k = pl.program_id(2)

        @pl.when(k == 0)
        def _():
            acc[...] = jnp.zeros_like(acc)

        acc[...] += _mxu(a_ref[...], b_ref[...], dims)

        @pl.when(k == nk - 1)
        def _():
            finish(acc[...], ex_refs, out_refs)

    if mode == "nn":
        a_spec = pl.BlockSpec((tm, tk), lambda i, j, k: (i, k))
        b_spec = pl.BlockSpec((tk, tn), lambda i, j, k: (k, j))
    elif mode == "nt":
        a_spec = pl.BlockSpec((tm, tk), lambda i, j, k: (i, k))
        b_spec = pl.BlockSpec((tn, tk), lambda i, j, k: (j, k))
    else:
        a_spec = pl.BlockSpec((tk, tm), lambda i, j, k: (k, i))
        b_spec = pl.BlockSpec((tk, tn), lambda i, j, k: (k, j))
    o_spec = pl.BlockSpec((tm, tn), lambda i, j, k: (i, j))
    res = pl.pallas_call(
        body,
        name=name,
        grid=(M // tm, N // tn, nk),
        in_specs=[a_spec, b_spec] + [o_spec] * n_ex + [pl.BlockSpec(memory_space=pl.ANY)] * len(after),
        out_specs=[o_spec] * n_out,
        out_shape=[jax.ShapeDtypeStruct((M, N), dt) for dt in out_dtypes],
        scratch_shapes=[pltpu.VMEM((tm, tn), F32)] if nk > 1 else [],
        compiler_params=_cparams(),
    )(a, b, *extras, *after)
    return res[0] if n_out == 1 else tuple(res)


ROW_MM_TILE = 512


def row_matmul(a, b, *, mode, name, extras, outs, epi):
    (M, K) = a.shape
    N = b.shape[1] if mode == "nn" else b.shape[0]
    tm = ROW_MM_TILE
    tk = _pick(K, MM_TILE_K)
    nk = K // tk
    n_ex, n_out = len(extras), len(outs)

    def body(*refs):
        a_ref, b_ref = refs[0], refs[1]
        ex_refs = refs[2:2 + n_ex]
        out_refs = refs[2 + n_ex:2 + n_ex + n_out]
        i = pl.program_id(0)
        dims = _NN if mode == "nn" else _NT
        if nk == 1:
            epi(_mxu(a_ref[...], b_ref[...], dims), i, ex_refs, out_refs)
            return
        acc = refs[2 + n_ex + n_out]
        k = pl.program_id(1)

        @pl.when(k == 0)
        def _():
            acc[...] = jnp.zeros_like(acc)

        acc[...] += _mxu(a_ref[...], b_ref[...], dims)

        @pl.when(k == nk - 1)
        def _():
            epi(acc[...], i, ex_refs, out_refs)

    def lift(index_map):
        return lambda i, k: index_map(i)

    b_spec = pl.BlockSpec((tk, N), lambda i, k: (k, 0)) if mode == "nn" else pl.BlockSpec((N, tk), lambda i, k: (0, k))
    res = pl.pallas_call(
        body, name=name, grid=(M // tm, nk),
        in_specs=[pl.BlockSpec((tm, tk), lambda i, k: (i, k)), b_spec]
        + [pl.BlockSpec(blk, lift(im)) for _, blk, im in extras],
        out_specs=[pl.BlockSpec(blk, lift(im)) for _, _, blk, im in outs],
        out_shape=[jax.ShapeDtypeStruct(shape, dt) for shape, dt, _, _ in outs],
        scratch_shapes=[pltpu.VMEM((tm, N), F32)] if nk > 1 else [],
        compiler_params=_cparams(),
    )(a, b, *[e[0] for e in extras])
    return tuple(res)


def _rows(arr):
    return (arr, (ROW_MM_TILE, arr.shape[1]), lambda i: (i, 0))


def _rows_out(T, dtype):
    return ((T, D_MODEL), dtype, (ROW_MM_TILE, D_MODEL), lambda i: (i, 0))


def _per_seq(arr):
    return (arr, (1, 1, D_MODEL), lambda i: (i // (SEQ // ROW_MM_TILE), 0, 0))


def _per_seq_out(B):
    return ((B, 1, D_MODEL), F32, (1, 1, D_MODEL), lambda i: (i // (SEQ // ROW_MM_TILE), 0, 0))


def _first_tile_of_seq(i):
    return i % (SEQ // ROW_MM_TILE) == 0


def small_call(fn, inputs, out_shapes, name):
    n_in = len(inputs)

    def body(*refs):
        outs = fn(*[r[...] for r in refs[:n_in]])
        for o_ref, o in zip(refs[n_in:], outs):
            o_ref[...] = o.astype(o_ref.dtype)

    res = pl.pallas_call(body, name=name, out_shape=list(out_shapes), compiler_params=_cparams())(*inputs)
    return tuple(res)


ROW_TILE = 512


def _modnorm(x, g, scale, shift):
    return _rms(x) * g * (1.0 + scale) + shift


def _row_spec(width):
    return pl.BlockSpec((1, ROW_TILE, width), lambda b, i: (b, i, 0))


def _mod_spec():
    return pl.BlockSpec((1, 1, D_MODEL), lambda b, i: (b, 0, 0))


def _vec_spec():
    return pl.BlockSpec((1, D_MODEL), lambda b, i: (0, 0))


def modnorm_fwd(x, g, scale, shift, name):
    B, S, D = x.shape

    def body(x_ref, g_ref, sc_ref, sh_ref, u_ref):
        u_ref[0] = _modnorm(x_ref[0], g_ref[...], sc_ref[0], sh_ref[0]).astype(BF16)

    return pl.pallas_call(
        body, name=name, grid=(B, S // ROW_TILE),
        in_specs=[_row_spec(D), _vec_spec(), _mod_spec(), _mod_spec()],
        out_specs=_row_spec(D),
        out_shape=jax.ShapeDtypeStruct((B, S, D), BF16),
        compiler_params=_cparams(),
    )(x, g, scale, shift)


def _gain_spec(g):
    return (g, (1, D_MODEL), lambda i: (0, 0))


def out_proj_resid_modnorm(z, w_out, x, gate, g, scale, shift, name):
    T = z.shape[0]

    def epi(acc, i, ex, out):
        x_ref, gt_ref, g_ref, sc_ref, sh_ref = ex
        y_ref, x1_ref, u_ref = out
        y_ref[...] = acc
        x1 = x_ref[...] + gt_ref[0] * acc
        x1_ref[...] = x1
        u_ref[...] = _modnorm(x1, g_ref[...], sc_ref[0], sh_ref[0]).astype(BF16)

    return row_matmul(z, w_out, mode="nn", name=name,
                      extras=[_rows(x), _per_seq(gate), _gain_spec(g), _per_seq(scale), _per_seq(shift)],
                      outs=[_rows_out(T, F32), _rows_out(T, F32), _rows_out(T, BF16)], epi=epi)


def ff2_loss(hdn, w_ff2, x1, gate, target, name):
    T = hdn.shape[0]

    def epi(acc, i, ex, out):
        x_ref, gt_ref, t_ref = ex
        dx_ref, dffo_ref, loss_ref, dg_ref = out

        @pl.when(i == 0)
        def _():
            loss_ref[...] = jnp.zeros_like(loss_ref)

        @pl.when(_first_tile_of_seq(i))
        def _():
            dg_ref[...] = jnp.zeros_like(dg_ref)

        err = x_ref[...] + gt_ref[0] * acc - t_ref[...]
        dx = err * (1.0 / D_MODEL)
        dx_ref[...] = dx
        dffo_ref[...] = (gt_ref[0] * dx).astype(BF16)
        loss_ref[...] += 0.5 * jnp.sum(jnp.mean(err * err, axis=-1, keepdims=True), axis=0, keepdims=True)
        dg_ref[0] += jnp.sum(dx * acc, axis=0, keepdims=True)

    return row_matmul(hdn, w_ff2, mode="nn", name=name,
                      extras=[_rows(x1), _per_seq(gate), _rows(target)],
                      outs=[_rows_out(T, F32), _rows_out(T, BF16), ((1, 1), F32, (1, 1), lambda i: (0, 0)),
                            _per_seq_out(T // SEQ)], epi=epi)


def d_u_modnorm_bwd(a, w, du_prev, x, g, scale, shift, dx_res, y, gate, name):
    T = a.shape[0]
    B = T // SEQ
    n_prev, resid = int(du_prev is not None), y is not None

    def epi(acc, i, ex, out):
        x_ref, g_ref, sc_ref, sh_ref, dr_ref = ex[n_prev:n_prev + 5]
        dx_ref, dg_ref, dsc_ref, dsh_ref = out[:4]

        @pl.when(i == 0)
        def _():
            dg_ref[...] = jnp.zeros_like(dg_ref)

        @pl.when(_first_tile_of_seq(i))
        def _():
            for r in out[2:4] + out[5:]:
                r[...] = jnp.zeros_like(r)

        du = acc + ex[0][...] if n_prev else acc
        _, vjp = jax.vjp(_modnorm, x_ref[...], g_ref[...], sc_ref[0], sh_ref[0])
        dxn, dg, dsc, dsh = vjp(du)
        dx = dxn + dr_ref[...]
        dx_ref[...] = dx
        dg_ref[...] += dg
        dsc_ref[0] += dsc
        dsh_ref[0] += dsh
        if resid:
            y_ref, gt_ref = ex[n_prev + 5:]
            out[4][...] = (gt_ref[0] * dx).astype(BF16)
            out[5][0] += jnp.sum(dx * y_ref[...], axis=0, keepdims=True)

    extras = ([_rows(du_prev)] if n_prev else []) + [_rows(x), _gain_spec(g), _per_seq(scale), _per_seq(shift),
                                                     _rows(dx_res)] + ([_rows(y), _per_seq(gate)] if resid else [])
    outs = [_rows_out(T, F32), ((1, D_MODEL), F32, (1, D_MODEL), lambda i: (0, 0)), _per_seq_out(B), _per_seq_out(B)]
    outs += [_rows_out(T, BF16), _per_seq_out(B)] if resid else []
    return row_matmul(a, w, mode="nt", name=name, extras=extras, outs=outs, epi=epi)


def _bucket_table(dilation):
    i = np.arange(ATT_BLOCK)[:, None]
    j = np.arange(2 * ATT_BLOCK)[None, :]
    delta = ATT_BLOCK + i - j
    dist = np.maximum(delta, 0) * dilation
    max_exact = N_BUCKETS // 2
    d = np.maximum(dist, max_exact).astype(np.float32)
    large = max_exact + (np.log(d / np.float32(max_exact)) / np.float32(math.log(MAX_DISTANCE / max_exact))
                         * np.float32(N_BUCKETS - max_exact)).astype(np.int32)
    large = np.minimum(large, N_BUCKETS - 1)
    return np.where(dist < max_exact, dist, large).astype(np.int32)


def _bucket_onehot(dilation):
    bt = jnp.asarray(_bucket_table(dilation).reshape(1, -1))
    return (bt == jnp.arange(N_BUCKETS, dtype=jnp.int32)[:, None]).astype(F32)


def bias_expand(rel_t, onehot, name):
    def fn(r, oh):
        return (_doth(r, oh),)
    return small_call(fn, [rel_t, onehot], [jax.ShapeDtypeStruct((rel_t.shape[0], onehot.shape[1]), F32)], name)[0]


def bias_reduce(dbias_flat, onehot, name):
    def fn(db, oh):
        return (_doth(db, oh, _NT),)
    return small_call(fn, [dbias_flat, onehot], [jax.ShapeDtypeStruct((dbias_flat.shape[0], N_BUCKETS), F32)], name)[0]


def _qk_norm(x, g):
    return _rms(x) * g


def _masked_bias(bias):
    i = lax.broadcasted_iota(jnp.int32, (ATT_BLOCK, 2 * ATT_BLOCK), 0)
    j = lax.broadcasted_iota(jnp.int32, (ATT_BLOCK, 2 * ATT_BLOCK), 1)
    bm = jnp.where(jnp.logical_and(j >= i, j <= i + ATT_BLOCK), bias, -jnp.inf)
    return bm, bm[:, ATT_BLOCK:]


def _attn_tile(qn, kn, v, bias):
    s = bdot_nt(qn, kn) * (HEAD_DIM ** -0.5) + bias
    m = lax.stop_gradient(jnp.max(s, axis=-1, keepdims=True))
    p = jnp.exp(s - m)
    l = jnp.sum(p, axis=-1, keepdims=True)
    o = bdot_nn(p, v) / l
    lse = jnp.broadcast_to(m + jnp.log(l), (ATT_BLOCK, HEAD_DIM))
    return o, lse


def _attn_tiles(dilation):
    nb = SEQ // dilation // ATT_BLOCK
    return [(r, n) for r in range(dilation) for n in range(nb)]


def _attn_rows(r, n, dilation, nblk=1):
    if dilation == 1:
        return pl.ds(r + n * ATT_BLOCK, nblk * ATT_BLOCK)
    return pl.ds(r + n * ATT_BLOCK * dilation, nblk * ATT_BLOCK, stride=dilation)


_QL, _KL, _VL = slice(0, 128), slice(128, 256), slice(256, 384)


def _qkv_specs(hb):
    return [pl.BlockSpec((None, SEQ, HEAD_DIM), functools.partial(lambda b, h, j: (b, 0, 3 * (hb + h) + j), j=j))
            for j in range(3)]


def attn_fwd(pa, bias, qg, kg, group, name):
    B = pa.shape[0]
    dilation = ATT_GROUPS[group][1]
    hb = group * 4

    def body(q_ref, k_ref, v_ref, b_ref, qg_ref, kg_ref, o_ref, l_ref, qn_s, kn_s):
        qn_s[...] = _qk_norm(q_ref[...], qg_ref[...])
        kn_s[...] = _qk_norm(k_ref[...], kg_ref[...])
        bias_all, bias_first = _masked_bias(b_ref[0])
        for (r, n) in _attn_tiles(dilation):
            rows = _attn_rows(r, n, dilation)
            if n == 0:
                krows, bias_t = rows, bias_first
            else:
                krows, bias_t = _attn_rows(r, n - 1, dilation, 2), bias_all
            o, lse = _attn_tile(qn_s[rows, :], kn_s[krows, :], v_ref[krows, :], bias_t)
            o_ref[rows, :] = o
            l_ref[rows, :] = lse

    head_out = pl.BlockSpec((None, SEQ, HEAD_DIM), lambda b, h: (b, 0, h))
    return pl.pallas_call(
        body, name=name, grid=(B, 4),
        in_specs=_qkv_specs(hb) + [
                  pl.BlockSpec((1, ATT_BLOCK, 2 * ATT_BLOCK), lambda b, h: (hb + h, 0, 0)),
                  pl.BlockSpec((1, HEAD_DIM), lambda b, h: (0, 0)),
                  pl.BlockSpec((1, HEAD_DIM), lambda b, h: (0, 0))],
        out_specs=[head_out, head_out],
        out_shape=[jax.ShapeDtypeStruct((B, SEQ, 512), F32), jax.ShapeDtypeStruct((B, SEQ, 512), F32)],
        scratch_shapes=[pltpu.VMEM((SEQ, HEAD_DIM), F32)] * 2,
        compiler_params=_cparams(),
    )(pa, pa, pa, bias, qg, kg)


def attn_bwd(pa, bias, qg, kg, do, dlse, dpa, group, name):
    B = pa.shape[0]
    dilation = ATT_GROUPS[group][1]
    hb = group * 4

    def body(q_ref, k_ref, v_ref, b_ref, qg_ref, kg_ref, do_ref, dl_ref, dpa_in,
             dp_ref, db_ref, dqg_ref, dkg_ref, qn_s, kn_s, dq_s, dk_s, dv_s):
        del dpa_in
        h_id = pl.program_id(1)

        @pl.when(jnp.logical_and(pl.program_id(0) == 0, h_id == 0))
        def _():
            db_ref[...] = jnp.zeros_like(db_ref)
            dqg_ref[...] = jnp.zeros_like(dqg_ref)
            dkg_ref[...] = jnp.zeros_like(dkg_ref)

        dk_s[...] = jnp.zeros_like(dk_s)
        dv_s[...] = jnp.zeros_like(dv_s)
        qn_s[...] = _qk_norm(q_ref[...], qg_ref[...])
        kn_s[...] = _qk_norm(k_ref[...], kg_ref[...])
        bias_all, bias_first = _masked_bias(b_ref[0])
        for (r, n) in _attn_tiles(dilation):
            rows = _attn_rows(r, n, dilation)
            if n == 0:
                krows, bias_t = rows, bias_first
            else:
                krows, bias_t = _attn_rows(r, n - 1, dilation, 2), bias_all
            _, vjp = jax.vjp(_attn_tile, qn_s[rows, :], kn_s[krows, :], v_ref[krows, :], bias_t)
            dqn, dkn, dv, dbias = vjp((do_ref[rows, :], dl_ref[rows, :]))
            dq_s[rows, :] = dqn
            dk_s[krows, :] += dkn
            dv_s[krows, :] += dv
            if n == 0:
                db_ref[h_id, :, ATT_BLOCK:] += dbias
            else:
                db_ref[h_id] += dbias
        for x_ref, g_ref, d_s, dg_ref, lanes in ((q_ref, qg_ref, dq_s, dqg_ref, _QL), (k_ref, kg_ref, dk_s, dkg_ref, _KL)):
            _, vjp = jax.vjp(_qk_norm, x_ref[...], g_ref[...])
            dx, dg = vjp(d_s[...])
            dp_ref[0, :, lanes] = dx.astype(BF16)
            dg_ref[...] += dg
        dp_ref[0, :, _VL] = dv_s[...].astype(BF16)

    const2 = lambda b, h: (0, 0)
    head_in = pl.BlockSpec((None, SEQ, HEAD_DIM), lambda b, h: (b, 0, h))
    head_blk = pl.BlockSpec((1, SEQ, ATT_HEAD_COLS), lambda b, h: (b, 0, hb + h))
    return pl.pallas_call(
        body, name=name, grid=(B, 4),
        in_specs=_qkv_specs(hb) + [
                  pl.BlockSpec((1, ATT_BLOCK, 2 * ATT_BLOCK), lambda b, h: (hb + h, 0, 0)),
                  pl.BlockSpec((1, HEAD_DIM), const2), pl.BlockSpec((1, HEAD_DIM), const2),
                  head_in, head_in,
                  pl.BlockSpec(memory_space=pl.ANY)],
        out_specs=[head_blk,
                   pl.BlockSpec((4, ATT_BLOCK, 2 * ATT_BLOCK), lambda b, h: (0, 0, 0)),
                   pl.BlockSpec((1, HEAD_DIM), const2), pl.BlockSpec((1, HEAD_DIM), const2)],
        out_shape=[jax.ShapeDtypeStruct(dpa.shape, BF16),
                   jax.ShapeDtypeStruct((4, ATT_BLOCK, 2 * ATT_BLOCK), F32),
                   jax.ShapeDtypeStruct((1, HEAD_DIM), F32), jax.ShapeDtypeStruct((1, HEAD_DIM), F32)],
        scratch_shapes=[pltpu.VMEM((SEQ, HEAD_DIM), F32)] * 5,
        input_output_aliases={8: 0},
        compiler_params=_cparams(),
    )(pa, pa, pa, bias, qg, kg, do, dlse, dpa)


def _merge(o0, o1, o2, l0, l1, l2):
    mx = jnp.maximum(jnp.maximum(l0, l1), l2)
    e0, e1, e2 = jnp.exp(l0 - mx), jnp.exp(l1 - mx), jnp.exp(l2 - mx)
    den = e0 + e1 + e2
    return (e0 / den) * o0 + (e1 / den) * o1 + (e2 / den) * o2


def merge_fwd(os_, ls_, name):
    B = os_[0].shape[0]

    def body(o0, o1, o2, l0, l1, l2, a_ref):
        a_ref[0] = _merge(o0[0], o1[0], o2[0], l0[0], l1[0], l2[0]).astype(BF16)

    return pl.pallas_call(
        body, name=name, grid=(B, SEQ // ROW_TILE),
        in_specs=[_row_spec(512)] * 6, out_specs=_row_spec(512),
        out_shape=jax.ShapeDtypeStruct((B, SEQ, 512), BF16),
        compiler_params=_cparams(),
    )(*os_, *ls_)


def merge_bwd(os_, ls_, datt, name):
    B = os_[0].shape[0]

    def body(o0, o1, o2, l0, l1, l2, da_ref, *outs):
        _, vjp = jax.vjp(_merge, o0[0], o1[0], o2[0], l0[0], l1[0], l2[0])
        for o_ref, g in zip(outs, vjp(da_ref[0])):
            o_ref[0] = g

    return pl.pallas_call(
        body, name=name, grid=(B, SEQ // ROW_TILE),
        in_specs=[_row_spec(512)] * 7, out_specs=[_row_spec(512)] * 6,
        out_shape=[jax.ShapeDtypeStruct((B, SEQ, 512), F32)] * 6,
        compiler_params=_cparams(),
    )(*os_, *ls_, datt)


def _gate_mix(ga, gm, ya, ym):
    return jax.nn.sigmoid(ga) * ya + jax.nn.sigmoid(gm) * ym


def _gate_halves(pg):
    return [(pg, (ROW_MM_TILE, D_MODEL), lambda i: (i, 0)), (pg, (ROW_MM_TILE, D_MODEL), lambda i: (i, 1))]


def ml_out_gate(hg, w_ml_out, pg, ya, name):
    T = hg.shape[0]

    def epi(acc, i, ex, out):
        ga, gm, ya_ref = ex
        out[0][...] = acc
        out[1][...] = _gate_mix(ga[...], gm[...], ya_ref[...], acc).astype(BF16)

    return row_matmul(hg, w_ml_out, mode="nn", name=name, extras=_gate_halves(pg) + [_rows(ya)],
                      outs=[_rows_out(T, F32), _rows_out(T, BF16)], epi=epi)


def d_z_gate_bwd(dy, w_out, pg, ya, ym, name):
    T = dy.shape[0]

    def epi(acc, i, ex, out):
        ga, gm, ya_ref, ym_ref = ex
        dpg_ref, dya_ref, dym_ref = out
        _, vjp = jax.vjp(_gate_mix, ga[...], gm[...], ya_ref[...], ym_ref[...])
        dga, dgm, dya, dym = vjp(acc)
        dpg_ref[:, :D_MODEL] = dga.astype(BF16)
        dpg_ref[:, D_MODEL:] = dgm.astype(BF16)
        dya_ref[...] = dya.astype(BF16)
        dym_ref[...] = dym.astype(BF16)

    return row_matmul(dy, w_out, mode="nt", name=name, extras=_gate_halves(pg) + [_rows(ya), _rows(ym)],
                      outs=[((T, GATE_COLS), BF16, (ROW_MM_TILE, GATE_COLS), lambda i: (i, 0)),
                            _rows_out(T, BF16), _rows_out(T, BF16)], epi=epi)


def _log_sigmoid(x):
    return jnp.minimum(x, 0.0) - jnp.log(1.0 + jnp.exp(-jnp.abs(x)))


def _head_mask(e):
    lane = lax.broadcasted_iota(jnp.int32, (1, 128), 1)
    return jnp.logical_and(lane >= e * 64, lane < (e + 1) * 64).astype(F32)


def _bmxu(a, b, ca, cb):
    return lax.dot_general(a.astype(BF16), b.astype(BF16), (((ca,), (cb,)), ((0,), (0,))), preferred_element_type=F32)


@jax.custom_vjp
def cdot_nt(a, b):
    return _bmxu(a, b, 2, 2)


cdot_nt.defvjp(lambda a, b: (_bmxu(a, b, 2, 2), (a, b)),
               lambda res, g: (_bmxu(g, res[1], 2, 1), _bmxu(g, res[0], 1, 1)))


@jax.custom_vjp
def cdot_nn(a, b):
    return _bmxu(a, b, 2, 1)


cdot_nn.defvjp(lambda a, b: (_bmxu(a, b, 2, 1), (a, b)),
               lambda res, g: (_bmxu(g, res[1], 2, 2), _bmxu(res[0], g, 1, 1)))


@jax.custom_vjp
def cdot_tn(a, b):
    return _bmxu(a, b, 1, 1)


cdot_tn.defvjp(lambda a, b: (_bmxu(a, b, 1, 1), (a, b)),
               lambda res, g: (_bmxu(res[1], g, 2, 2), _bmxu(res[0], g, 2, 1)))


def _top_bits(x):
    return lax.bitcast_convert_type(lax.bitcast_convert_type(x, jnp.uint32) & jnp.uint32(0xFFFF0000), F32)


def _split3(x):
    hi = _top_bits(x)
    r = x - hi
    mid = _top_bits(r)
    return hi, mid, r - mid


def _parts_in_lanes(col):
    hi, mid, lo = _split3(col)
    lane = lax.broadcasted_iota(jnp.int32, (1, 1, 8), 2)
    return jnp.where(lane == 0, hi, jnp.where(lane == 1, mid, jnp.where(lane == 2, lo, 0.0)))


def _parts_in_rows(row):
    hi, mid, lo = _split3(row)
    sub = lax.broadcasted_iota(jnp.int32, (1, 8, 1), 1)
    return jnp.where(sub == 0, hi, jnp.where(sub == 1, mid, jnp.where(sub == 2, lo, 0.0)))


def _chunk_matrix(kind, c):
    ri = lax.broadcasted_iota(jnp.int32, (c, ML_CHUNK, ML_CHUNK), 1)
    ci = lax.broadcasted_iota(jnp.int32, (c, ML_CHUNK, ML_CHUNK), 2)
    return {"eye": ri == ci, "lower": ri >= ci, "upper": ri <= ci}[kind].astype(F32)


def _col_col(kind, col):
    out = _bmxu(_chunk_matrix(kind, col.shape[0]), _parts_in_lanes(col), 2, 1)
    return jnp.sum(out, axis=-1, keepdims=True)


def _col_row(col):
    out = _bmxu(_parts_in_lanes(col), _chunk_matrix("eye", col.shape[0]), 1, 1)
    return jnp.sum(out, axis=1, keepdims=True)


def _row_col(row):
    out = _bmxu(_chunk_matrix("eye", row.shape[0]), _parts_in_rows(row), 2, 2)
    return jnp.sum(out, axis=-1, keepdims=True)


@jax.custom_vjp
def chunk_cumsum(col):
    return _col_col("lower", col)


chunk_cumsum.defvjp(lambda col: (_col_col("lower", col), None), lambda _, g: (_col_col("upper", g),))


@jax.custom_vjp
def col_to_row(col):
    return _col_row(col)


col_to_row.defvjp(lambda col: (_col_row(col), None), lambda _, g: (_row_col(g),))


def _gate_block(ifb):
    lane = lax.broadcasted_iota(jnp.int32, (1, 128), 1)
    return jnp.where(lane >= 2, _log_sigmoid(ifb), ifb)


def _ml_intra(q2, k2, v, ifb, *, e):
    c, L = N_CHUNKS, ML_CHUNK
    hm = _head_mask(e)
    q3 = (q2 * hm).reshape(c, L, 128)
    k3 = (k2 * hm).reshape(c, L, 128)
    v3 = v.reshape(c, L, 128)
    if3 = ifb.reshape(c, L, 128)
    lanes = lax.broadcasted_iota(jnp.int32, (c, L, 128), 2)
    li = jnp.sum(jnp.where(lanes == e, if3, 0.0), axis=-1, keepdims=True)
    lf = jnp.sum(jnp.where(lanes == 2 + e, if3, 0.0), axis=-1, keepdims=True)
    b = chunk_cumsum(lf)
    last = lax.broadcasted_iota(jnp.int32, (1, L, 1), 1) == L - 1
    b_end = jnp.sum(jnp.where(last, b, 0.0), axis=1, keepdims=True)
    causal = lax.broadcasted_iota(jnp.int32, (L, L), 0) >= lax.broadcasted_iota(jnp.int32, (L, L), 1)
    Dm = jnp.where(causal, b + col_to_row(li - b), -jnp.inf)
    mD = lax.stop_gradient(jnp.max(Dm, axis=-1, keepdims=True))
    P0 = cdot_nt(q3, k3) * jnp.exp(Dm - mD)
    H0 = cdot_nn(P0, v3)
    r0 = jnp.sum(P0, axis=-1, keepdims=True)
    g = b_end - b + li
    mg = lax.stop_gradient(jnp.max(g, axis=1, keepdims=True))
    kw = jnp.exp(g - mg) * k3
    return H0, r0, cdot_tn(kw, v3), jnp.sum(kw, axis=1, keepdims=True), b, b_end, mD, mg


def _ml_inter(q2, mo, gn, H0, r0, b, C_in, n_in, *, mD, m_in, e):
    c, L = N_CHUNKS, ML_CHUNK
    q3 = (q2 * _head_mask(e)).reshape(c, L, 128)
    a = b + m_in
    m_t = lax.stop_gradient(jnp.maximum(a, mD))
    c1 = jnp.exp(mD - m_t)
    c2 = jnp.exp(a - m_t)
    num = c1 * H0 + c2 * cdot_nn(q3, C_in)
    nq = c1 * r0 + c2 * jnp.sum(q3 * n_in, axis=-1, keepdims=True)
    h = num / jnp.maximum(jnp.abs(nq), jnp.exp(-m_t))
    hg = _rms(h) * gn * jax.nn.sigmoid(mo.reshape(c, L, 128))
    return hg.reshape(c * L, 128)


def _state_sweep(U_s, un_s, be_s, mg_s, Cin_s, nin_s, min_s, al_s, bt_s):
    def step(j, carry):
        C, n, m = carry
        Cin_s[j], nin_s[j], min_s[j] = C, n, m
        m_out = jnp.maximum(be_s[j] + m, mg_s[j])
        al = jnp.exp(be_s[j] + m - m_out)
        bt = jnp.exp(mg_s[j] - m_out)
        al_s[j], bt_s[j] = al, bt
        return al * C + bt * U_s[j], al * n + bt * un_s[j], m_out

    lax.fori_loop(0, N_CHUNKS, step, (jnp.zeros((128, 128), F32), jnp.zeros((1, 128), F32), jnp.zeros((1, 1), F32)))


def _state_sweep_bwd(U_s, un_s, dbe_s, Cin_s, nin_s, dCp_s, dnp_s, al_s, bt_s):
    def step(t, carry):
        j = N_CHUNKS - 1 - t
        dC, dn = carry
        al, bt = al_s[j], bt_s[j]
        U_s[j] = bt * dC
        un_s[j] = bt * dn
        dal = jnp.sum(jnp.sum(dC * Cin_s[j], axis=1, keepdims=True), axis=0, keepdims=True) \
            + jnp.sum(dn * nin_s[j], axis=1, keepdims=True)
        dbe_s[j] = dal * al
        return dCp_s[j] + al * dC, dnp_s[j] + al * dn

    lax.fori_loop(0, N_CHUNKS, step, (jnp.zeros((128, 128), F32), jnp.zeros((1, 128), F32)))


def _state_scratch():
    c = N_CHUNKS
    return [pltpu.VMEM((c, 128, 128), F32), pltpu.VMEM((c, 1, 128), F32), pltpu.VMEM((c, 1, 1), F32),
            pltpu.VMEM((c, 1, 1), F32),
            pltpu.VMEM((c, 128, 128), F32), pltpu.VMEM((c, 1, 128), F32), pltpu.VMEM((c, 1, 1), F32),
            pltpu.VMEM((c, 1, 1), F32), pltpu.VMEM((c, 1, 1), F32)]


def _shift_down(x, s):
    if s == 0:
        return x
    rows = lax.broadcasted_iota(jnp.int32, x.shape, 0)
    return jnp.where(rows >= s, pltpu.roll(x, s, 0), 0.0)


def _shift_up(x, s):
    if s == 0:
        return x
    S = x.shape[0]
    rows = lax.broadcasted_iota(jnp.int32, x.shape, 0)
    return jnp.where(rows < S - s, pltpu.roll(x, S - s, 0), 0.0)


def _conv_pre(x, cw, cb):
    y = cb + cw[3:4, :] * x
    for j in range(3):
        y = y + cw[j:j + 1, :] * _shift_down(x, 3 - j)
    return y


def _conv_bwd(x, cw, dpre):
    dx = cw[3:4, :] * dpre
    dcw = [None] * 4
    dcw[3] = jnp.sum(dpre * x, axis=0, keepdims=True)
    for j in range(3):
        dx = dx + cw[j:j + 1, :] * _shift_up(dpre, 3 - j)
        dcw[j] = jnp.sum(dpre * _shift_down(x, 3 - j), axis=0, keepdims=True)
    return dx, dcw, jnp.sum(dpre, axis=0, keepdims=True)


def _silu(z):
    return z * jax.nn.sigmoid(z)


def _dsilu(z):
    s = jax.nn.sigmoid(z)
    return s * (1.0 + z * (1.0 - s))


_ML_Q, _ML_K = slice(0, 128), slice(128, 256)
_ML_IF = slice(768, 896)


def _ml_v(e):
    return slice(256 + e * 128, 384 + e * 128)


def _ml_o(e):
    return slice(512 + e * 128, 640 + e * 128)


def _ml_specs():
    pair = lambda b, p: (b, 0, p)
    return [pl.BlockSpec((1, SEQ, ML_PAIR_COLS), pair),
            pl.BlockSpec((1, 4, 128), lambda b, p: (p, 0, 0)),
            pl.BlockSpec((1, 4, 128), lambda b, p: (4 + p, 0, 0)),
            pl.BlockSpec((1, 1, 128), lambda b, p: (p, 0, 0)),
            pl.BlockSpec((1, 1, 128), lambda b, p: (4 + p, 0, 0)),
            pl.BlockSpec((1, 1, 128), lambda b, p: (p, 0, 0)),
            pl.BlockSpec((1, 1, 256), lambda b, p: (p, 0, 0))]


def mlstm_fwd(pm, cw8, cb8, bifp, gn4, name):
    B = pm.shape[0]

    def body(p_ref, cwq, cwk, cbq, cbk, bif_ref, gn_ref, hg_ref, *st):
        U_s, un_s, be_s, mg_s, Cin_s, nin_s, min_s, al_s, bt_s = st
        qc = _silu(_conv_pre(p_ref[0, :, _ML_Q], cwq[0], cbq[0]))
        kc = _silu(_conv_pre(p_ref[0, :, _ML_K], cwk[0], cbk[0])) * (64 ** -0.5)
        ifb = _gate_block(p_ref[0, :, _ML_IF] + bif_ref[0])
        for e in range(2):
            lanes = slice(e * 128, (e + 1) * 128)
            H0, r0, U, un, b, b_end, mD, mg = _ml_intra(qc, kc, p_ref[0, :, _ml_v(e)], ifb, e=e)
            U_s[...], un_s[...], be_s[...], mg_s[...] = U, un, b_end, mg
            _state_sweep(*st)
            hg = _ml_inter(qc, p_ref[0, :, _ml_o(e)], gn_ref[0, :, lanes], H0, r0, b, Cin_s[...], nin_s[...],
                           mD=mD, m_in=min_s[...], e=e)
            hg_ref[0, :, lanes] = hg.astype(BF16)

    return pl.pallas_call(
        body, name=name, grid=(B, ML_PAIRS),
        in_specs=_ml_specs(),
        out_specs=pl.BlockSpec((1, SEQ, 256), lambda b, p: (b, 0, p)),
        out_shape=jax.ShapeDtypeStruct((B, SEQ, D_MODEL), BF16),
        scratch_shapes=_state_scratch(),
        compiler_params=_cparams(),
    )(pm, cw8, cw8, cb8, cb8, bifp, gn4)


def mlstm_bwd(pm, cw8, cb8, bifp, gn4, dhg, name):
    B = pm.shape[0]

    def body(p_ref, cwq, cwk, cbq, cbk, bif_ref, gn_ref, dh_ref,
             dp_ref, dcw_ref, dcb_ref, dbif_ref, dgn_ref, *scr):
        st = scr[:9]
        U_s, un_s, be_s, mg_s, Cin_s, nin_s, min_s, al_s, bt_s = st
        dCp_s, dnp_s, dbe_s = scr[9:]
        p_id = pl.program_id(1)

        @pl.when(jnp.logical_and(pl.program_id(0) == 0, p_id == 0))
        def _():
            dcw_ref[...] = jnp.zeros_like(dcw_ref)
            dcb_ref[...] = jnp.zeros_like(dcb_ref)
            dbif_ref[...] = jnp.zeros_like(dbif_ref)
            dgn_ref[...] = jnp.zeros_like(dgn_ref)

        pre_q = _conv_pre(p_ref[0, :, _ML_Q], cwq[0], cbq[0])
        pre_k = _conv_pre(p_ref[0, :, _ML_K], cwk[0], cbk[0])
        qc = _silu(pre_q)
        kc = _silu(pre_k) * (64 ** -0.5)
        ifb, gate_vjp = jax.vjp(_gate_block, p_ref[0, :, _ML_IF] + bif_ref[0])
        dq = jnp.zeros((SEQ, 128), F32)
        dk = jnp.zeros((SEQ, 128), F32)
        difb = jnp.zeros((SEQ, 128), F32)
        for e in range(2):
            lanes = slice(e * 128, (e + 1) * 128)
            (H0, r0, U, un, b, b_end, mD, mg), vjp1 = jax.vjp(functools.partial(_ml_intra, e=e), qc, kc,
                                                              p_ref[0, :, _ml_v(e)], ifb)
            U_s[...], un_s[...], be_s[...], mg_s[...] = U, un, b_end, mg
            _state_sweep(*st)
            _, vjp3 = jax.vjp(functools.partial(_ml_inter, mD=mD, m_in=min_s[...], e=e), qc, p_ref[0, :, _ml_o(e)],
                              gn_ref[0, :, lanes], H0, r0, b, Cin_s[...], nin_s[...])
            dq_a, dmo, dgn, dH0, dr0, db_a, dCp, dnp = vjp3(dh_ref[0, :, lanes])
            dCp_s[...], dnp_s[...] = dCp, dnp
            _state_sweep_bwd(U_s, un_s, dbe_s, Cin_s, nin_s, dCp_s, dnp_s, al_s, bt_s)
            dq_b, dk_b, dv, difb_e = vjp1((dH0, dr0, U_s[...], un_s[...], db_a, dbe_s[...],
                                           jnp.zeros_like(mD), jnp.zeros_like(mg)))
            dq, dk, difb = dq + dq_a + dq_b, dk + dk_b, difb + difb_e
            dp_ref[0, :, _ml_v(e)] = dv.astype(BF16)
            dp_ref[0, :, _ml_o(e)] = dmo.astype(BF16)
            dgn_ref[p_id, :, lanes] += dgn
        (difb,) = gate_vjp(difb)
        dp_ref[0, :, _ML_IF] = difb.astype(BF16)
        dbif_ref[p_id] += jnp.sum(difb, axis=0, keepdims=True)

        for (sl, cw, pre, d, blk, scale) in ((_ML_Q, cwq, pre_q, dq, p_id, 1.0), (_ML_K, cwk, pre_k, dk, 4 + p_id, 64 ** -0.5)):
            xr = p_ref[0, :, sl]
            dpre = d * scale * _dsilu(pre)
            dx, dcw, dcb = _conv_bwd(xr, cw[0], dpre)
            dp_ref[0, :, sl] = dx.astype(BF16)
            for j in range(4):
                dcw_ref[blk, j:j + 1, :] += dcw[j]
            dcb_ref[blk] += dcb

    full3 = lambda b, p: (0, 0, 0)
    return pl.pallas_call(
        body, name=name, grid=(B, ML_PAIRS),
        in_specs=[pl.BlockSpec((1, SEQ, ML_PAIR_COLS), lambda b, p: (b, 0, p), pipeline_mode=pl.Buffered(1))]
        + _ml_specs()[1:] + [pl.BlockSpec((1, SEQ, 256), lambda b, p: (b, 0, p), pipeline_mode=pl.Buffered(1))],
        out_specs=[pl.BlockSpec((1, SEQ, ML_PAIR_COLS), lambda b, p: (b, 0, p)),
                   pl.BlockSpec((8, 4, 128), full3), pl.BlockSpec((8, 1, 128), full3),
                   pl.BlockSpec((4, 1, 128), full3), pl.BlockSpec((4, 1, 256), full3)],
        out_shape=[jax.ShapeDtypeStruct((B, SEQ, ML_COLS), BF16),
                   jax.ShapeDtypeStruct((8, 4, 128), F32), jax.ShapeDtypeStruct((8, 1, 128), F32),
                   jax.ShapeDtypeStruct((4, 1, 128), F32), jax.ShapeDtypeStruct((4, 1, 256), F32)],
        scratch_shapes=_state_scratch() + [pltpu.VMEM((N_CHUNKS, 128, 128), F32), pltpu.VMEM((N_CHUNKS, 1, 128), F32),
                                           pltpu.VMEM((N_CHUNKS, 1, 1), F32)],
        compiler_params=_cparams(),
    )(pm, cw8, cw8, cb8, cb8, bifp, gn4, dhg)


def _adamw(w, g, m, v):
    m = ADAM_B1 * m + (1.0 - ADAM_B1) * g
    v = ADAM_B2 * v + (1.0 - ADAM_B2) * (g * g)
    m_hat = m / (1.0 - ADAM_B1 ** ADAM_STEP)
    v_hat = v / (1.0 - ADAM_B2 ** ADAM_STEP)
    delta = -ADAM_LR * (m_hat / (jnp.sqrt(v_hat) + ADAM_EPS) + ADAM_WD * w)
    return delta, m, v


def adamw(w, g, m, v, name, parts=False):
    R, C = w.shape
    if R % 8 == 0 or R * C * 4 <= (1 << 20):
        tr = _pick(R, (256, 128, 64, 32, 16, 8)) if R * C * 4 > (1 << 20) else R
        steps = R // tr
        spec = pl.BlockSpec((tr, C), lambda i: (i, 0))
        g_spec = pl.BlockSpec((N_DEV, tr, C), lambda i: (0, i, 0)) if parts else spec
    else:
        tc = _pick(C, (256, 128))
        steps = C // tc
        spec = pl.BlockSpec((R, tc), lambda i: (0, i))
        g_spec = pl.BlockSpec((N_DEV, R, tc), lambda i: (0, 0, i)) if parts else spec

    def body(w_ref, g_ref, m_ref, v_ref, go_ref, d_ref, mo_ref, vo_ref):
        if parts:
            g = g_ref[0].astype(F32)
            for k in range(1, N_DEV):
                g = g + g_ref[k].astype(F32)
        else:
            g = g_ref[...]
        d, mn, vn = _adamw(w_ref[...], g, m_ref[...], v_ref[...])
        go_ref[...], d_ref[...], mo_ref[...], vo_ref[...] = g, d, mn, vn

    return pl.pallas_call(
        body, name=name, grid=(steps,),
        in_specs=[spec, g_spec, spec, spec], out_specs=[spec] * 4,
        out_shape=[jax.ShapeDtypeStruct((R, C), F32)] * 4,
        compiler_params=_cparams(),
    )(w, g, m, v)


def adamw_many(ws, gs, ms, vs, name):
    n = len(ws)

    def fn(*a):
        out = []
        for j in range(n):
            out += list(_adamw(a[j], a[n + j], a[2 * n + j], a[3 * n + j]))
        return tuple(out)

    shapes = [jax.ShapeDtypeStruct(w.shape, F32) for w in ws for _ in range(3)]
    return small_call(fn, list(ws) + list(gs) + list(ms) + list(vs), shapes, name)


def _mesh_pos():
    return lax.axis_index("x"), lax.axis_index("y"), lax.axis_index("c")


def _flip(pos, f):
    x, y, c = pos
    return (1 - x if f & 4 else x, 1 - y if f & 2 else y, 1 - c if f & 1 else c)


def _index(pos):
    return 4 * pos[0] + 2 * pos[1] + pos[2]


def _exchange(arrs, name, scatter):
    n = len(arrs)

    def body(*refs):
        ins, outs = refs[:n], refs[n:2 * n]
        send, recv, lsem = refs[2 * n:]
        me = _mesh_pos()
        mine = _index(me)
        copies = []
        for i in range(n):
            src = ins[i].at[mine] if scatter else ins[i]
            loc = pltpu.make_async_copy(src, outs[i].at[mine], lsem.at[i])
            loc.start()
            copies.append(loc)
            for f in range(1, N_DEV):
                peer = _flip(me, f)
                src = ins[i].at[_index(peer)] if scatter else ins[i]
                cp = pltpu.make_async_remote_copy(
                    src_ref=src, dst_ref=outs[i].at[mine],
                    send_sem=send.at[i * 7 + f - 1], recv_sem=recv.at[i * 7 + f - 1],
                    device_id=peer, device_id_type=pl.DeviceIdType.MESH)
                cp.start()
                copies.append(cp)
        for cp in copies:
            cp.wait()

    any_spec = pl.BlockSpec(memory_space=pl.ANY)
    out_shape = [jax.ShapeDtypeStruct(a.shape if scatter else (N_DEV,) + a.shape, a.dtype) for a in arrs]
    res = pl.pallas_call(
        body, name=name,
        in_specs=[any_spec] * n, out_specs=[any_spec] * n, out_shape=out_shape,
        scratch_shapes=[pltpu.SemaphoreType.DMA((7 * n,)), pltpu.SemaphoreType.DMA((7 * n,)),
                        pltpu.SemaphoreType.DMA((n,))],
        compiler_params=_cparams(),
    )(*arrs)
    return list(res)


def all_gather(arrs, name):
    return _exchange(arrs, name, False)


def all_gather_two_level(arrs, name):
    n = len(arrs)

    def body(*refs):
        ins, outs = refs[:n], refs[n:2 * n]
        send, recv, lsem = refs[2 * n:]
        x, y, c = _mesh_pos()
        me, sibling = (x, y, c), (x, y, 1 - c)
        chips = [(1 - x, y), (x, 1 - y), (1 - x, 1 - y)]

        def copy(i, k, block, to, src=None):
            rows = outs[i].at[_index(block)]
            return pltpu.make_async_remote_copy(
                src_ref=rows if src is None else src, dst_ref=rows,
                send_sem=send.at[i * 7 + k], recv_sem=recv.at[i * 7 + k],
                device_id=to, device_id_type=pl.DeviceIdType.MESH)

        local = [pltpu.make_async_copy(ins[i], outs[i].at[_index(me)], lsem.at[i]) for i in range(n)]
        first = [copy(i, 0, me, sibling, src=ins[i]) for i in range(n)]
        first += [copy(i, 1 + j, me, (*chip, c), src=ins[i]) for i in range(n) for j, chip in enumerate(chips)]
        for cp in local + first:
            cp.start()
        passed = []
        for j, chip in enumerate(chips):
            for i in range(n):
                copy(i, 1 + j, (*chip, c), me).wait_recv()
                cp = copy(i, 4 + j, (*chip, c), sibling)
                cp.start()
                passed.append(cp)
        for i in range(n):
            copy(i, 0, sibling, me).wait_recv()
            for j, chip in enumerate(chips):
                copy(i, 4 + j, (*chip, 1 - c), me).wait_recv()
        for cp in first + passed:
            cp.wait_send()
        for cp in local:
            cp.wait()

    any_spec = pl.BlockSpec(memory_space=pl.ANY)
    res = pl.pallas_call(
        body, name=name,
        in_specs=[any_spec] * n, out_specs=[any_spec] * n,
        out_shape=[jax.ShapeDtypeStruct((N_DEV,) + a.shape, a.dtype) for a in arrs],
        scratch_shapes=[pltpu.SemaphoreType.DMA((7 * n,)), pltpu.SemaphoreType.DMA((7 * n,)),
                        pltpu.SemaphoreType.DMA((n,))],
        compiler_params=_cparams(),
    )(*arrs)
    return list(res)


def all_to_all(arrs, name):
    return _exchange(arrs, name, True)


_HBM = pl.BlockSpec(memory_space=pltpu.HBM)
_SEM = pl.BlockSpec(memory_space=pltpu.SEMAPHORE)
_EFFECT = pltpu.SideEffectType.DATAFLOW_SIDE_EFFECTING


def _split_copies(ins, lands, send, recv, scatter, waiting):
    me = _mesh_pos()
    mine = _index(me)
    copies = []
    for i in range(len(ins)):
        for f in range(1, N_DEV):
            peer = _flip(me, f)
            src = ins[i].at[_index(peer)] if scatter else ins[i]
            copies.append(pltpu.make_async_remote_copy(
                src_ref=src, dst_ref=lands[i].at[_index(peer) if waiting else mine],
                send_sem=send.at[i * 7 + f - 1], recv_sem=recv.at[i * 7 + f - 1],
                device_id=peer, device_id_type=pl.DeviceIdType.MESH))
    return copies


def exchange_start(arrs, name, scatter, after=()):
    n = len(arrs)
    land_shapes = [a.shape if scatter else (N_DEV,) + a.shape for a in arrs]

    def body(*refs):
        ins, lands = refs[:n], refs[n:2 * n]
        send, recv = refs[2 * n + len(after)], refs[2 * n + len(after) + 1]
        token = refs[-1]
        for cp in _split_copies(ins, lands, send, recv, scatter, False):
            cp.start()
        token[...] = jnp.zeros_like(token)

    res = pl.pallas_call(
        body, name=name,
        out_shape=(pltpu.SemaphoreType.DMA((7 * n,)), pltpu.SemaphoreType.DMA((7 * n,)),
                   *[pltpu.HBM(a.shape, a.dtype) for a in arrs],
                   *[pltpu.HBM(s, a.dtype) for s, a in zip(land_shapes, arrs)],
                   jax.ShapeDtypeStruct((8, 128), F32)),
        in_specs=[_HBM] * (2 * n) + [pl.BlockSpec(memory_space=pl.ANY)] * len(after),
        out_specs=(_SEM, _SEM, *[_HBM] * (2 * n), pl.BlockSpec(memory_space=pltpu.VMEM)),
        input_output_aliases={i: 2 + i for i in range(2 * n)},
        compiler_params=pltpu.CompilerParams(has_side_effects=_EFFECT),
    )(*[pltpu.with_memory_space_constraint(a, pltpu.HBM) for a in arrs],
      *[pltpu.with_memory_space_constraint(lax.empty(s, a.dtype), pltpu.HBM) for s, a in zip(land_shapes, arrs)],
      *after)
    return (res[0], res[1], list(res[2:2 + n]), list(res[2 + n:2 + 2 * n])), res[-1]


def exchange_wait(handle, after, name, scatter):
    send, recv, srcs, lands = handle
    n = len(srcs)

    def body(*refs):
        ins, lnd = refs[:n], refs[n:2 * n]
        send_, recv_ = refs[2 * n], refs[2 * n + 1]
        for cp in _split_copies(ins, lnd, send_, recv_, scatter, True):
            cp.wait_send()
            cp.wait_recv()

    res = pl.pallas_call(
        body, name=name,
        out_shape=(*[pltpu.HBM(a.shape, a.dtype) for a in srcs], *[pltpu.HBM(a.shape, a.dtype) for a in lands]),
        in_specs=[_HBM] * (2 * n) + [_SEM, _SEM, pl.BlockSpec(memory_space=pl.ANY)],
        out_specs=tuple([_HBM] * (2 * n)),
        input_output_aliases={i: i for i in range(2 * n)},
        compiler_params=pltpu.CompilerParams(has_side_effects=_EFFECT),
    )(*srcs, *lands, send, recv, after)
    return list(res[n:])


def _own_slot(land, own):
    return lax.dynamic_update_slice(land, own[None], (_index(_mesh_pos()),) + (0,) * own.ndim)


def cast_bf16(arrs, name):
    outs = []
    for i, a in enumerate(arrs):
        R, C = a.shape
        if R % 8 == 0:
            tr = _pick(R, (256, 128, 64, 32, 16, 8)) if R * C * 4 > (1 << 21) else R
            steps, spec = R // tr, pl.BlockSpec((tr, C), lambda i: (i, 0))
        else:
            steps, spec = C // 256, pl.BlockSpec((R, 256), lambda i: (0, i))

        def body(a_ref, o_ref):
            o_ref[...] = a_ref[...].astype(BF16)

        outs.append(pl.pallas_call(body, name=f"{name}_{i}", grid=(steps,), in_specs=[spec], out_specs=spec,
                                   out_shape=jax.ShapeDtypeStruct((R, C), BF16), compiler_params=_cparams())(a))
    return outs


def sum_parts(parts, name):
    def fn(p):
        g = p[0]
        for k in range(1, N_DEV):
            g = g + p[k]
        return (g,)
    return small_call(fn, [parts], [jax.ShapeDtypeStruct(parts.shape[1:], F32)], name)[0]


_SPLITS = np.cumsum([1536, 1536, 1536, 512, 512, 1024, 1024, 8, 8, 2048])[:-1].tolist()


def split_w_in(w):
    aq, ak, av, mq, mk, mv, mo, mi, mf, gates = jnp.split(w, _SPLITS, axis=1)
    R = w.shape[0]
    w_att = jnp.stack([aq.reshape(R, 12, 128), ak.reshape(R, 12, 128), av.reshape(R, 12, 128)], axis=2)
    gif = jnp.concatenate([mi.reshape(R, 4, 2), mf.reshape(R, 4, 2), jnp.zeros((R, 4, 124), w.dtype)], axis=2)
    w_ml = jnp.concatenate([mq.reshape(R, 4, 128), mk.reshape(R, 4, 128), mv.reshape(R, 4, 256),
                            mo.reshape(R, 4, 256), gif], axis=2)
    return w_att.reshape(R, ATT_COLS), w_ml.reshape(R, ML_COLS), gates


def merge_w_in(g_att, g_ml, g_gate):
    R = g_att.shape[0]
    a = g_att.reshape(R, 12, 3, 128)
    m = g_ml.reshape(R, 4, ML_PAIR_COLS)
    gif = m[:, :, 768:772]
    return jnp.concatenate([
        a[:, :, 0].reshape(R, 1536), a[:, :, 1].reshape(R, 1536), a[:, :, 2].reshape(R, 1536),
        m[:, :, 0:128].reshape(R, 512), m[:, :, 128:256].reshape(R, 512),
        m[:, :, 256:512].reshape(R, 1024), m[:, :, 512:768].reshape(R, 1024),
        gif[:, :, 0:2].reshape(R, 8), gif[:, :, 2:4].reshape(R, 8), g_gate], axis=1)


def _blk8(v, width=128):
    r = v.shape[0]
    return v.reshape(r, 1024 // width, width).transpose(1, 0, 2)


def _unblk8(v):
    nb, r, w = v.shape
    return v.transpose(1, 0, 2).reshape(r, nb * w)


def local_step(x, target, mods, w, small, late_w=None, early_g=None, w_in_g=None):
    late_w = late_w or (lambda after: w)
    big = {}
    early_g = early_g or (lambda g: big.update(g))
    w_in_g = w_in_g or (lambda g: big.update(w_in=g))
    B = x.shape[0]
    T = B * SEQ
    shift1, scale1, gate1, shift2, scale2, gate2 = mods
    f2 = lambda a: a.reshape(T, a.shape[-1])
    f3 = lambda a: a.reshape(B, SEQ, a.shape[-1])

    rel_t = jnp.pad(small["rel_bias"].T, ((0, 4), (0, 0)))
    onehots = [_bucket_onehot(d) for _, d in ATT_GROUPS]
    biases = [bias_expand(rel_t, oh, f"bias_expand{g}").reshape(16, ATT_BLOCK, 2 * ATT_BLOCK)
              for g, oh in enumerate(onehots)]
    qg, kg = small["q_norm_g"], small["k_norm_g"]
    cw8 = _blk8(small["conv_w"])
    cb8 = _blk8(small["conv_b"])
    b_if = small["b_if"].reshape(2, 4, 2)
    bifp = jnp.concatenate([b_if[0], b_if[1], jnp.zeros((4, 124), F32)], axis=1).reshape(4, 1, 128)
    gn4 = small["mlstm_norm_g"].reshape(4, 1, 256)

    u = modnorm_fwd(x, small["norm1_g"], scale1, shift1, "modnorm1")
    u2d = f2(u)
    pa = f3(matmul(u2d, w["w_att"], mode="nn", name="proj_att"))
    pm = f3(matmul(u2d, w["w_ml"], mode="nn", name="proj_ml"))
    pg = matmul(u2d, w["w_gate"], mode="nn", name="proj_gate")
    os_, ls_ = [], []
    for g in range(3):
        o, l = attn_fwd(pa, biases[g], qg, kg, g, f"attn_fwd{g}")
        os_.append(o)
        ls_.append(l)
    att = merge_fwd(os_, ls_, "merge_fwd")
    hg = mlstm_fwd(pm, cw8, cb8, bifp, gn4, "mlstm_fwd")
    w = {**w, **late_w(hg)}
    y_att = matmul(f2(att), w["w_att_out"], mode="nn", name="att_out")
    y_ml, z = ml_out_gate(f2(hg), w["w_ml_out"], pg, y_att, "ml_out")
    y, x1, u2 = out_proj_resid_modnorm(f2(z), w["w_out"], f2(x), gate1, small["norm2_g"], scale2, shift2, "out_proj")
    pre, hdn = matmul(u2, w["w_ff1"], mode="nn", name="ff1", out_dtypes=(BF16, BF16),
                      epi=lambda acc: (acc, jnp.square(jnp.maximum(acc, 0.0))))
    dx2, d_ffo, loss, d_gate2 = ff2_loss(hdn, w["w_ff2"], x1, gate2, f2(target), "ff2_loss")

    g_ff2 = matmul(hdn, d_ffo, mode="tn", name="g_ff2", out_dtypes=(BF16,))
    d_pre = matmul(d_ffo, w["w_ff2"], mode="nt", name="d_hdn", out_dtypes=(BF16,), extras=(pre,),
                   epi=lambda acc, p: (acc * (2.0 * jnp.maximum(p.astype(F32), 0.0)),))
    g_ff1 = matmul(u2, d_pre, mode="tn", name="g_ff1", out_dtypes=(BF16,))
    dx1, d_norm2, d_scale2, d_shift2, dy, d_gate1 = d_u_modnorm_bwd(
        d_pre, w["w_ff1"], None, x1, small["norm2_g"], scale2, shift2, dx2, y, gate1, "d_u2")
    g_out = matmul(f2(z), dy, mode="tn", name="g_out", out_dtypes=(BF16,))
    dpg, d_ya, d_ym = d_z_gate_bwd(dy, w["w_out"], pg, y_att, y_ml, "d_z")
    g_att_out = matmul(f2(att), f2(d_ya), mode="tn", name="g_att_out", out_dtypes=(BF16,))
    d_att = matmul(f2(d_ya), w["w_att_out"], mode="nt", name="d_att")
    g_ml_out = matmul(f2(hg), f2(d_ym), mode="tn", name="g_ml_out", out_dtypes=(BF16,))
    d_hg = matmul(f2(d_ym), w["w_ml_out"], mode="nt", name="d_hg")
    started = early_g(dict(w_att_out=g_att_out, w_ml_out=g_ml_out, w_out=g_out, w_ff1=g_ff1, w_ff2=g_ff2))
    order = 0.0 if started is None else started[0, 0]
    dmerge = merge_bwd(os_, ls_, f3(d_att), "merge_bwd")
    dpa = lax.empty((B, SEQ, ATT_COLS), BF16)
    d_rel = []
    d_qg = d_kg = None
    for g in range(3):
        dpa, dbias, dq_g, dk_g = attn_bwd(pa, biases[g], qg + order, kg, dmerge[g], dmerge[3 + g], dpa, g,
                                          f"attn_bwd{g}")
        db8 = jnp.pad(dbias.reshape(4, -1), ((0, 4), (0, 0)))
        d_rel.append(bias_reduce(db8, onehots[g], f"bias_reduce{g}")[:4])
        d_qg = dq_g if d_qg is None else d_qg + dq_g
        d_kg = dk_g if d_kg is None else d_kg + dk_g
    dpm, dcw8, dcb8, dbifp, dgn4 = mlstm_bwd(pm, cw8, cb8, bifp, gn4 + order, f3(d_hg), "mlstm_bwd")
    g_w_att = matmul(u2d, f2(dpa), mode="tn", name="g_w_att", out_dtypes=(BF16,))
    g_w_ml = matmul(u2d, f2(dpm), mode="tn", name="g_w_ml", out_dtypes=(BF16,))
    g_w_gate = matmul(u2d, f2(dpg), mode="tn", name="g_w_gate", out_dtypes=(BF16,))
    started = w_in_g(merge_w_in(g_w_att, g_w_ml, g_w_gate))
    du = matmul(f2(dpa), w["w_att"], mode="nt", name="d_u_att", after=() if started is None else (started,))
    du = matmul(f2(dpm), w["w_ml"], mode="nt", name="d_u_ml", extras=(du,), epi=lambda acc, e: (acc + e,))
    grad_x, d_norm1, d_scale1, d_shift1 = d_u_modnorm_bwd(
        f2(dpg), w["w_gate"], du, f2(x), small["norm1_g"], scale1, shift1, dx1, None, None, "d_u_gate")
    grad_x = f3(grad_x)

    d_mods = (d_shift1, d_scale1, d_gate1, d_shift2, d_scale2, d_gate2)
    dbif = dbifp.reshape(4, 128)
    small_g = dict(
        norm1_g=d_norm1, norm2_g=d_norm2,
        b_if=jnp.stack([dbif[:, 0:2].reshape(8), dbif[:, 2:4].reshape(8)]),
        conv_w=_unblk8(dcw8), conv_b=_unblk8(dcb8), q_norm_g=d_qg, k_norm_g=d_kg,
        rel_bias=jnp.concatenate(d_rel, axis=0).T,
        mlstm_norm_g=dgn4.reshape(1, 1024))
    return loss, grad_x, d_mods, big, small_g


_SMALL = (("b_ada", 6144), ("norm1_g", 1024), ("norm2_g", 1024), ("b_if", 16), ("conv_b", 1024),
          ("q_norm_g", 128), ("k_norm_g", 128), ("rel_bias", 384), ("mlstm_norm_g", 1024), ("conv_w", 4096))
_SMALL_ROWS = 120
_REPL = _SMALL[:-1]


def _pack(d, names, rows):
    flat = jnp.concatenate([d[k].reshape(-1) for k, _ in names])
    return jnp.pad(flat, (0, rows * 128 - flat.shape[0])).reshape(rows, 128)


def _unpack(slab, names, shapes):
    flat = slab.reshape(-1)
    out, off = {}, 0
    for k, nel in names:
        out[k] = flat[off:off + nel].reshape(shapes[k])
        off += nel
    return out


def kernel(x, c, w_ada, b_ada, norm1_g, norm2_g, w_in, b_if, conv_w, conv_b, q_norm_g, k_norm_g, rel_bias, mlstm_norm_g, w_att_out, w_ml_out, w_out, w_ff1, w_ff2, loss_target, m_w_ada, m_b_ada, m_norm1_g, m_norm2_g, m_w_in, m_b_if, m_conv_w, m_conv_b, m_q_norm_g, m_k_norm_g, m_rel_bias, m_mlstm_norm_g, m_w_att_out, m_w_ml_out, m_w_out, m_w_ff1, m_w_ff2, v_w_ada, v_b_ada, v_norm1_g, v_norm2_g, v_w_in, v_b_if, v_conv_w, v_conv_b, v_q_norm_g, v_k_norm_g, v_rel_bias, v_mlstm_norm_g, v_w_att_out, v_w_ml_out, v_w_out, v_w_ff1, v_w_ff2):
    P = dict(w_ada=w_ada, b_ada=b_ada, norm1_g=norm1_g, norm2_g=norm2_g, w_in=w_in, b_if=b_if, conv_w=conv_w,
             conv_b=conv_b, q_norm_g=q_norm_g, k_norm_g=k_norm_g, rel_bias=rel_bias, mlstm_norm_g=mlstm_norm_g,
             w_att_out=w_att_out, w_ml_out=w_ml_out, w_out=w_out, w_ff1=w_ff1, w_ff2=w_ff2)
    M = dict(w_ada=m_w_ada, b_ada=m_b_ada, norm1_g=m_norm1_g, norm2_g=m_norm2_g, w_in=m_w_in, b_if=m_b_if,
             conv_w=m_conv_w, conv_b=m_conv_b, q_norm_g=m_q_norm_g, k_norm_g=m_k_norm_g, rel_bias=m_rel_bias,
             mlstm_norm_g=m_mlstm_norm_g, w_att_out=m_w_att_out, w_ml_out=m_w_ml_out, w_out=m_w_out,
             w_ff1=m_w_ff1, w_ff2=m_w_ff2)
    V = dict(w_ada=v_w_ada, b_ada=v_b_ada, norm1_g=v_norm1_g, norm2_g=v_norm2_g, w_in=v_w_in, b_if=v_b_if,
             conv_w=v_conv_w, conv_b=v_conv_b, q_norm_g=v_q_norm_g, k_norm_g=v_k_norm_g, rel_bias=v_rel_bias,
             mlstm_norm_g=v_mlstm_norm_g, w_att_out=v_w_att_out, w_ml_out=v_w_ml_out, w_out=v_w_out,
             w_ff1=v_w_ff1, w_ff2=v_w_ff2)
    names = list(P)
    shapes = {k: P[k].shape for k in names}
    B = x.shape[0]
    me = _index(_mesh_pos())

    big_names = ("w_in", "w_att_out", "w_ml_out", "w_out", "w_ff1", "w_ff2")
    shards = cast_bf16([P[k][0] for k in big_names], "cast_w")
    w_in_g8, c8, conv_w8 = all_gather_two_level([shards[0], c, conv_w[0]], "gather_w_in")
    c_all = c8.reshape(N_DEV * B, D_MODEL)
    conv_w_full = conv_w8.transpose(1, 0, 2).reshape(4, 1024)
    w_att, w_ml, w_gate = split_w_in(w_in_g8.transpose(1, 0, 2).reshape(D_MODEL, D_IN))
    w = dict(w_att=w_att, w_ml=w_ml, w_gate=w_gate)

    (silu_c,) = small_call(lambda a: (_silu(a),), [c_all], [jax.ShapeDtypeStruct(c_all.shape, F32)], "silu_c")
    b_ada_cols = lax.dynamic_slice(b_ada, (0, me * 768), (1, 768))
    ada_cols = matmul(silu_c, w_ada[0], mode="nn", name="ada", extras=(jnp.broadcast_to(b_ada_cols, (N_DEV * B, 768)),),
                      epi=lambda acc, bb: (acc + bb,))
    (ada_t,) = all_to_all([ada_cols.reshape(N_DEV, B, 768)], "ada_exchange")
    ada = ada_t.transpose(1, 0, 2).reshape(B, 6 * D_MODEL)
    mods = tuple(ada[:, i * D_MODEL:(i + 1) * D_MODEL].reshape(B, 1, D_MODEL) for i in range(6))

    late_handle, late_order = exchange_start(shards[1:], "gather_late_start", False, after=(ada_t,))

    def late_w(after):
        lands = exchange_wait(late_handle, after, "gather_late_wait", False)
        gw = dict(zip(big_names[1:], [_own_slot(l, s) for l, s in zip(lands, shards[1:])]))
        return dict(w_att_out=gw["w_att_out"].transpose(1, 0, 2).reshape(512, D_MODEL),
                    w_ml_out=gw["w_ml_out"].reshape(D_MODEL, D_MODEL), w_out=gw["w_out"].reshape(D_MODEL, D_MODEL),
                    w_ff1=gw["w_ff1"].transpose(1, 0, 2).reshape(D_MODEL, D_FF),
                    w_ff2=gw["w_ff2"].reshape(D_FF, D_MODEL))

    pending = {}

    def send_grads(key, blocks, name):
        handle, order = exchange_start(blocks, name, True)
        pending[key] = (handle, [lax.dynamic_index_in_dim(b, me, 0, keepdims=False) for b in blocks])
        return order

    def early_g(g):
        return send_grads("late", [g["w_att_out"].reshape(512, N_DEV, 128).transpose(1, 0, 2),
                                   g["w_ml_out"].reshape(N_DEV, 128, D_MODEL), g["w_out"].reshape(N_DEV, 128, D_MODEL),
                                   g["w_ff1"].reshape(D_MODEL, N_DEV, 512).transpose(1, 0, 2),
                                   g["w_ff2"].reshape(N_DEV, 512, D_MODEL)], "grad_late_start")

    def w_in_g(g):
        return send_grads("w_in", [g.reshape(D_MODEL, N_DEV, W_IN_SHARD).transpose(1, 0, 2)], "grad_w_in_start")

    def recv_grads(key, after, name):
        handle, own = pending[key]
        return [_own_slot(l, o) for l, o in zip(exchange_wait(handle, after, name, True), own)]

    small = dict(norm1_g=norm1_g + late_order[0, 0], norm2_g=norm2_g, b_if=b_if[0], conv_w=conv_w_full, conv_b=conv_b,
                 q_norm_g=q_norm_g, k_norm_g=k_norm_g, rel_bias=rel_bias, mlstm_norm_g=mlstm_norm_g)
    loss, grad_x, d_mods, _, small_g = local_step(x, loss_target, mods, w, small, late_w, early_g, w_in_g)
    loss = lax.psum(loss[0, 0], ("x", "y", "c"))

    d_ada = jnp.concatenate([d.reshape(B, D_MODEL) for d in d_mods], axis=1)
    (d_ada_t,) = all_to_all([d_ada.reshape(B, N_DEV, 768).transpose(1, 0, 2)], "d_ada_exchange")
    d_ada_cols = d_ada_t.reshape(N_DEV * B, 768)
    g_w_ada = matmul(silu_c, d_ada_cols, mode="tn", name="g_w_ada")
    (g_b_cols,) = small_call(lambda a: (jnp.sum(a, axis=0, keepdims=True),), [d_ada_cols],
                             [jax.ShapeDtypeStruct((1, 768), F32)], "g_b_ada_cols")
    small_g["b_ada"] = lax.dynamic_update_slice(jnp.zeros((1, 6144), F32), g_b_cols, (0, me * 768))

    recv = recv_grads("w_in", grad_x, "grad_w_in_wait") + recv_grads("late", grad_x, "grad_late_wait")
    (small_parts,) = all_gather([_pack(small_g, _SMALL, _SMALL_ROWS)], "small_grad_gather")
    small_sum = sum_parts(small_parts, "small_grad_sum")
    sg = _unpack(small_sum, _SMALL, {**{k: shapes[k] for k, _ in _REPL}, "conv_w": (4, 1024)})

    G, Dl, NM, NV = {}, {}, {}, {}
    for k, parts in zip(big_names, recv):
        g, d, nm, nv = adamw(P[k][0], parts, M[k][0], V[k][0], f"adamw_{k}", parts=True)
        G[k], Dl[k], NM[k], NV[k] = g[None], d[None], nm[None], nv[None]
    g, d, nm, nv = adamw(w_ada[0], g_w_ada, m_w_ada[0], v_w_ada[0], "adamw_w_ada")
    G["w_ada"], Dl["w_ada"], NM["w_ada"], NV["w_ada"] = g[None], d[None], nm[None], nv[None]
    g_conv = lax.dynamic_slice(sg["conv_w"], (0, me * 128), (4, 128))
    g, d, nm, nv = adamw(conv_w[0], g_conv, m_conv_w[0], v_conv_w[0], "adamw_conv_w")
    G["conv_w"], Dl["conv_w"], NM["conv_w"], NV["conv_w"] = g[None], d[None], nm[None], nv[None]
    flat2 = lambda a: a.reshape(-1, a.shape[-1])
    keys = [k for k, _ in _REPL]
    upd = adamw_many([flat2(P[k]) for k in keys], [flat2(sg[k]) for k in keys], [flat2(M[k]) for k in keys],
                     [flat2(V[k]) for k in keys], "adamw_small")
    for j, k in enumerate(keys):
        G[k] = sg[k]
        Dl[k], NM[k], NV[k] = [upd[3 * j + t].reshape(shapes[k]) for t in range(3)]

    return (loss, grad_x, *[G[k] for k in names], *[Dl[k] for k in names], *[NM[k] for k in names],
            *[NV[k] for k in names])
```

```python
import functools
import math

import numpy as np
import jax
import jax.numpy as jnp
from jax import lax
from jax.experimental import pallas as pl
from jax.experimental.pallas import tpu as pltpu

F32 = jnp.float32
BF16 = jnp.bfloat16

N_DEV = 8
D_MODEL = 1024
SEQ = 2048
ATT_GROUPS = ((128, 1), (512, 4), (2048, 16))
N_ATT_HEADS = 12
ATT_BLOCK = 128
HEAD_DIM = 128
ML_HEADS = 8
ML_PAIRS = 4
ML_CHUNK = 64
N_CHUNKS = SEQ // ML_CHUNK
N_BUCKETS = 32
MAX_DISTANCE = 2048
D_FF = 4096
D_IN = 9744
EPS = 1e-6

ADAM_LR = 0.001
ADAM_B1 = 0.9
ADAM_B2 = 0.999
ADAM_EPS = 1e-08
ADAM_WD = 0.01
ADAM_STEP = 10

ATT_HEAD_COLS = 3 * HEAD_DIM
ATT_COLS = N_ATT_HEADS * ATT_HEAD_COLS
ML_PAIR_COLS = 896
ML_COLS = ML_PAIRS * ML_PAIR_COLS
GATE_COLS = 2 * D_MODEL
W_IN_SHARD = D_IN // N_DEV

VMEM_LIMIT = 60 * 1024 * 1024


def _cparams(**kw):
    return pltpu.CompilerParams(vmem_limit_bytes=VMEM_LIMIT, **kw)


_NN = ((1,), (0,))
_NT = ((1,), (1,))
_TN = ((0,), (0,))


def _mxu(a, b, dims):
    return lax.dot_general(a.astype(BF16), b.astype(BF16), (dims, ((), ())), preferred_element_type=F32)


@jax.custom_vjp
def bdot_nn(a, b):
    return _mxu(a, b, _NN)


def _nn_fwd(a, b):
    return _mxu(a, b, _NN), (a, b)


def _nn_bwd(res, g):
    a, b = res
    return _mxu(g, b, _NT), _mxu(a, g, _TN)


bdot_nn.defvjp(_nn_fwd, _nn_bwd)


@jax.custom_vjp
def bdot_nt(a, b):
    return _mxu(a, b, _NT)


def _nt_fwd(a, b):
    return _mxu(a, b, _NT), (a, b)


def _nt_bwd(res, g):
    a, b = res
    return _mxu(g, b, _NN), _mxu(g, a, _TN)


bdot_nt.defvjp(_nt_fwd, _nt_bwd)


def _doth(a, b, dims=_NN):
    return lax.dot_general(a, b, (dims, ((), ())), precision=lax.Precision.HIGHEST, preferred_element_type=F32)


def _rms(x):
    return x * lax.rsqrt(jnp.mean(x * x, axis=-1, keepdims=True) + EPS)


def _pick(n, cands):
    for t in cands:
        if n % t == 0:
            return t
    raise ValueError(f"no tile for {n}")


MM_TILE_M = (1024, 512, 256, 128, 64, 32, 16, 8)
MM_TILE_N = (2048, 1792, 1536, 1024, 768, 512, 256, 128)
MM_TILE_K = (2048, 1792, 1536, 1024, 512, 256, 128, 64, 32)

def matmul(a, b, *, mode, name, out_dtypes=(F32,), epi=None, extras=(), after=()):
    if mode == "nn":
        (M, K), (K2, N) = a.shape, b.shape
    elif mode == "nt":
        (M, K), (N, K2) = a.shape, b.shape
    else:
        (K, M), (K2, N) = a.shape, b.shape
    assert K == K2, (a.shape, b.shape, mode)
    tm = _pick(M, MM_TILE_M)
    tn = _pick(N, MM_TILE_N)
    tk = _pick(K, MM_TILE_K)
    nk = K // tk
    n_ex = len(extras)
    n_out = len(out_dtypes)
    dims = {"nn": _NN, "nt": _NT, "tn": _TN}[mode]

    def finish(r, ex_refs, out_refs):
        outs = epi(r, *[e[...] for e in ex_refs]) if epi is not None else (r,)
        for o_ref, o in zip(out_refs, outs):
            o_ref[...] = o.astype(o_ref.dtype)

    def body(*refs):
        a_ref, b_ref = refs[0], refs[1]
        ex_refs = refs[2:2 + n_ex]
        out_refs = refs[2 + n_ex + len(after):2 + n_ex + len(after) + n_out]
        if nk == 1:
            finish(_mxu(a_ref[...], b_ref[...], dims), ex_refs, out_refs)
            return
        acc = refs[2 + n_ex + len(after) + n_out]
        k = pl.program_id(2)

        @pl.when(k == 0)
        def _():
            acc[...] = jnp.zeros_like(acc)

        acc[...] += _mxu(a_ref[...], b_ref[...], dims)

        @pl.when(k == nk - 1)
        def _():
            finish(acc[...], ex_refs, out_refs)

    if mode == "nn":
        a_spec = pl.BlockSpec((tm, tk), lambda i, j, k: (i, k))
        b_spec = pl.BlockSpec((tk, tn), lambda i, j, k: (k, j))
    elif mode == "nt":
        a_spec = pl.BlockSpec((tm, tk), lambda i, j, k: (i, k))
        b_spec = pl.BlockSpec((tn, tk), lambda i, j, k: (j, k))
    else:
        a_spec = pl.BlockSpec((tk, tm), lambda i, j, k: (k, i))
        b_spec = pl.BlockSpec((tk, tn), lambda i, j, k: (k, j))
    o_spec = pl.BlockSpec((tm, tn), lambda i, j, k: (i, j))
    res = pl.pallas_call(
        body,
        name=name,
        grid=(M // tm, N // tn, nk),
        in_specs=[a_spec, b_spec] + [o_spec] * n_ex + [pl.BlockSpec(memory_space=pl.ANY)] * len(after),
        out_specs=[o_spec] * n_out,
        out_shape=[jax.ShapeDtypeStruct((M, N), dt) for dt in out_dtypes],
        scratch_shapes=[pltpu.VMEM((tm, tn), F32)] if nk > 1 else [],
        compiler_params=_cparams(),
    )(a, b, *extras, *after)
    return res[0] if n_out == 1 else tuple(res)


ROW_MM_TILE = 512


def row_matmul(a, b, *, mode, name, extras, outs, epi):
    (M, K) = a.shape
    N = b.shape[1] if mode == "nn" else b.shape[0]
    tm = ROW_MM_TILE
    tk = _pick(K, MM_TILE_K)
    nk = K // tk
    n_ex, n_out = len(extras), len(outs)

    def body(*refs):
        a_ref, b_ref = refs[0], refs[1]
        ex_refs = refs[2:2 + n_ex]
        out_refs = refs[2 + n_ex:2 + n_ex + n_out]
        i = pl.program_id(0)
        dims = _NN if mode == "nn" else _NT
        if nk == 1:
            epi(_mxu(a_ref[...], b_ref[...], dims), i, ex_refs, out_refs)
            return
        acc = refs[2 + n_ex + n_out]
        k = pl.program_id(1)

        @pl.when(k == 0)
        def _():
            acc[...] = jnp.zeros_like(acc)

        acc[...] += _mxu(a_ref[...], b_ref[...], dims)

        @pl.when(k == nk - 1)
        def _():
            epi(acc[...], i, ex_refs, out_refs)

    def lift(index_map):
        return lambda i, k: index_map(i)

    b_spec = pl.BlockSpec((tk, N), lambda i, k: (k, 0)) if mode == "nn" else pl.BlockSpec((N, tk), lambda i, k: (0, k))
    res = pl.pallas_call(
        body, name=name, grid=(M // tm, nk),
        in_specs=[pl.BlockSpec((tm, tk), lambda i, k: (i, k)), b_spec]
        + [pl.BlockSpec(blk, lift(im)) for _, blk, im in extras],
        out_specs=[pl.BlockSpec(blk, lift(im)) for _, _, blk, im in outs],
        out_shape=[jax.ShapeDtypeStruct(shape, dt) for shape, dt, _, _ in outs],
        scratch_shapes=[pltpu.VMEM((tm, N), F32)] if nk > 1 else [],
        compiler_params=_cparams(),
    )(a, b, *[e[0] for e in extras])
    return tuple(res)


def _rows(arr):
    return (arr, (ROW_MM_TILE, arr.shape[1]), lambda i: (i, 0))


def _rows_out(T, dtype):
    return ((T, D_MODEL), dtype, (ROW_MM_TILE, D_MODEL), lambda i: (i, 0))


def _per_seq(arr):
    return (arr, (1, 1, D_MODEL), lambda i: (i // (SEQ // ROW_MM_TILE), 0, 0))


def _per_seq_out(B):
    return ((B, 1, D_MODEL), F32, (1, 1, D_MODEL), lambda i: (i // (SEQ // ROW_MM_TILE), 0, 0))


def _first_tile_of_seq(i):
    return i % (SEQ // ROW_MM_TILE) == 0


def small_call(fn, inputs, out_shapes, name):
    n_in = len(inputs)

    def body(*refs):
        outs = fn(*[r[...] for r in refs[:n_in]])
        for o_ref, o in zip(refs[n_in:], outs):
            o_ref[...] = o.astype(o_ref.dtype)

    res = pl.pallas_call(body, name=name, out_shape=list(out_shapes), compiler_params=_cparams())(*inputs)
    return tuple(res)


ROW_TILE = 512


def _modnorm(x, g, scale, shift):
    return _rms(x) * g * (1.0 + scale) + shift


def _row_spec(width):
    return pl.BlockSpec((1, ROW_TILE, width), lambda b, i: (b, i, 0))


def _mod_spec():
    return pl.BlockSpec((1, 1, D_MODEL), lambda b, i: (b, 0, 0))


def _vec_spec():
    return pl.BlockSpec((1, D_MODEL), lambda b, i: (0, 0))


def modnorm_fwd(x, g, scale, shift, name):
    B, S, D = x.shape

    def body(x_ref, g_ref, sc_ref, sh_ref, u_ref):
        u_ref[0] = _modnorm(x_ref[0], g_ref[...], sc_ref[0], sh_ref[0]).astype(BF16)

    return pl.pallas_call(
        body, name=name, grid=(B, S // ROW_TILE),
        in_specs=[_row_spec(D), _vec_spec(), _mod_spec(), _mod_spec()],
        out_specs=_row_spec(D),
        out_shape=jax.ShapeDtypeStruct((B, S, D), BF16),
        compiler_params=_cparams(),
    )(x, g, scale, shift)


def _gain_spec(g):
    return (g, (1, D_MODEL), lambda i: (0, 0))


def out_proj_resid_modnorm(z, w_out, x, gate, g, scale, shift, name):
    T = z.shape[0]

    def epi(acc, i, ex, out):
        x_ref, gt_ref, g_ref, sc_ref, sh_ref = ex
        y_ref, x1_ref, u_ref = out
        y_ref[...] = acc
        x1 = x_ref[...] + gt_ref[0] * acc
        x1_ref[...] = x1
        u_ref[...] = _modnorm(x1, g_ref[...], sc_ref[0], sh_ref[0]).astype(BF16)

    return row_matmul(z, w_out, mode="nn", name=name,
                      extras=[_rows(x), _per_seq(gate), _gain_spec(g), _per_seq(scale), _per_seq(shift)],
                      outs=[_rows_out(T, F32), _rows_out(T, F32), _rows_out(T, BF16)], epi=epi)


def ff2_loss(hdn, w_ff2, x1, gate, target, name):
    T = hdn.shape[0]

    def epi(acc, i, ex, out):
        x_ref, gt_ref, t_ref = ex
        dx_ref, dffo_ref, loss_ref, dg_ref = out

        @pl.when(i == 0)
        def _():
            loss_ref[...] = jnp.zeros_like(loss_ref)

        @pl.when(_first_tile_of_seq(i))
        def _():
            dg_ref[...] = jnp.zeros_like(dg_ref)

        err = x_ref[...] + gt_ref[0] * acc - t_ref[...]
        dx = err * (1.0 / D_MODEL)
        dx_ref[...] = dx
        dffo_ref[...] = (gt_ref[0] * dx).astype(BF16)
        loss_ref[...] += 0.5 * jnp.sum(jnp.mean(err * err, axis=-1, keepdims=True), axis=0, keepdims=True)
        dg_ref[0] += jnp.sum(dx * acc, axis=0, keepdims=True)

    return row_matmul(hdn, w_ff2, mode="nn", name=name,
                      extras=[_rows(x1), _per_seq(gate), _rows(target)],
                      outs=[_rows_out(T, F32), _rows_out(T, BF16), ((1, 1), F32, (1, 1), lambda i: (0, 0)),
                            _per_seq_out(T // SEQ)], epi=epi)


def d_u_modnorm_bwd(a, w, du_prev, x, g, scale, shift, dx_res, y, gate, name):
    T = a.shape[0]
    B = T // SEQ
    n_prev, resid = int(du_prev is not None), y is not None

    def epi(acc, i, ex, out):
        x_ref, g_ref, sc_ref, sh_ref, dr_ref = ex[n_prev:n_prev + 5]
        dx_ref, dg_ref, dsc_ref, dsh_ref = out[:4]

        @pl.when(i == 0)
        def _():
            dg_ref[...] = jnp.zeros_like(dg_ref)

        @pl.when(_first_tile_of_seq(i))
        def _():
            for r in out[2:4] + out[5:]:
                r[...] = jnp.zeros_like(r)

        du = acc + ex[0][...] if n_prev else acc
        _, vjp = jax.vjp(_modnorm, x_ref[...], g_ref[...], sc_ref[0], sh_ref[0])
        dxn, dg, dsc, dsh = vjp(du)
        dx = dxn + dr_ref[...]
        dx_ref[...] = dx
        dg_ref[...] += dg
        dsc_ref[0] += dsc
        dsh_ref[0] += dsh
        if resid:
            y_ref, gt_ref = ex[n_prev + 5:]
            out[4][...] = (gt_ref[0] * dx).astype(BF16)
            out[5][0] += jnp.sum(dx * y_ref[...], axis=0, keepdims=True)

    extras = ([_rows(du_prev)] if n_prev else []) + [_rows(x), _gain_spec(g), _per_seq(scale), _per_seq(shift),
                                                     _rows(dx_res)] + ([_rows(y), _per_seq(gate)] if resid else [])
    outs = [_rows_out(T, F32), ((1, D_MODEL), F32, (1, D_MODEL), lambda i: (0, 0)), _per_seq_out(B), _per_seq_out(B)]
    outs += [_rows_out(T, BF16), _per_seq_out(B)] if resid else []
    return row_matmul(a, w, mode="nt", name=name, extras=extras, outs=outs, epi=epi)


def _bucket_table(dilation):
    i = np.arange(ATT_BLOCK)[:, None]
    j = np.arange(2 * ATT_BLOCK)[None, :]
    delta = ATT_BLOCK + i - j
    dist = np.maximum(delta, 0) * dilation
    max_exact = N_BUCKETS // 2
    d = np.maximum(dist, max_exact).astype(np.float32)
    large = max_exact + (np.log(d / np.float32(max_exact)) / np.float32(math.log(MAX_DISTANCE / max_exact))
                         * np.float32(N_BUCKETS - max_exact)).astype(np.int32)
    large = np.minimum(large, N_BUCKETS - 1)
    return np.where(dist < max_exact, dist, large).astype(np.int32)


def _bucket_onehot(dilation):
    bt = jnp.asarray(_bucket_table(dilation).reshape(1, -1))
    return (bt == jnp.arange(N_BUCKETS, dtype=jnp.int32)[:, None]).astype(F32)


def bias_expand(rel_t, onehot, name):
    def fn(r, oh):
        return (_doth(r, oh),)
    return small_call(fn, [rel_t, onehot], [jax.ShapeDtypeStruct((rel_t.shape[0], onehot.shape[1]), F32)], name)[0]


def bias_reduce(dbias_flat, onehot, name):
    def fn(db, oh):
        return (_doth(db, oh, _NT),)
    return small_call(fn, [dbias_flat, onehot], [jax.ShapeDtypeStruct((dbias_flat.shape[0], N_BUCKETS), F32)], name)[0]


def _qk_norm(x, g):
    return _rms(x) * g


def _masked_bias(bias):
    i = lax.broadcasted_iota(jnp.int32, (ATT_BLOCK, 2 * ATT_BLOCK), 0)
    j = lax.broadcasted_iota(jnp.int32, (ATT_BLOCK, 2 * ATT_BLOCK), 1)
    bm = jnp.where(jnp.logical_and(j >= i, j <= i + ATT_BLOCK), bias, -jnp.inf)
    return bm, bm[:, ATT_BLOCK:]


def _attn_tile(qn, kn, v, bias):
    s = bdot_nt(qn, kn) * (HEAD_DIM ** -0.5) + bias
    m = lax.stop_gradient(jnp.max(s, axis=-1, keepdims=True))
    p = jnp.exp(s - m)
    l = jnp.sum(p, axis=-1, keepdims=True)
    o = bdot_nn(p, v) / l
    lse = jnp.broadcast_to(m + jnp.log(l), (ATT_BLOCK, HEAD_DIM))
    return o, lse


def _attn_tiles(dilation, rows=SEQ):
    nb = rows // dilation // ATT_BLOCK
    return [(r, n) for r in range(dilation) for n in range(nb)]


def _attn_rows(r, n, dilation, nblk=1):
    if dilation == 1:
        return pl.ds(r + n * ATT_BLOCK, nblk * ATT_BLOCK)
    return pl.ds(r + n * ATT_BLOCK * dilation, nblk * ATT_BLOCK, stride=dilation)


_QL, _KL, _VL = slice(0, 128), slice(128, 256), slice(256, 384)


def _qkv_specs(hb):
    return [pl.BlockSpec((None, SEQ, HEAD_DIM), functools.partial(lambda b, h, j: (b, 0, 3 * (hb + h) + j), j=j))
            for j in range(3)]


def attn_fwd(pa, bias, qg, kg, group, name):
    B = pa.shape[0]
    dilation = ATT_GROUPS[group][1]
    hb = group * 4

    def body(q_ref, k_ref, v_ref, b_ref, qg_ref, kg_ref, o_ref, l_ref, qn_s, kn_s):
        qn_s[...] = _qk_norm(q_ref[...], qg_ref[...])
        kn_s[...] = _qk_norm(k_ref[...], kg_ref[...])
        bias_all, bias_first = _masked_bias(b_ref[0])
        for (r, n) in _attn_tiles(dilation):
            rows = _attn_rows(r, n, dilation)
            if n == 0:
                krows, bias_t = rows, bias_first
            else:
                krows, bias_t = _attn_rows(r, n - 1, dilation, 2), bias_all
            o, lse = _attn_tile(qn_s[rows, :], kn_s[krows, :], v_ref[krows, :], bias_t)
            o_ref[rows, :] = o
            l_ref[rows, :] = lse

    head_out = pl.BlockSpec((None, SEQ, HEAD_DIM), lambda b, h: (b, 0, h))
    return pl.pallas_call(
        body, name=name, grid=(B, 4),
        in_specs=_qkv_specs(hb) + [
                  pl.BlockSpec((1, ATT_BLOCK, 2 * ATT_BLOCK), lambda b, h: (hb + h, 0, 0)),
                  pl.BlockSpec((1, HEAD_DIM), lambda b, h: (0, 0)),
                  pl.BlockSpec((1, HEAD_DIM), lambda b, h: (0, 0))],
        out_specs=[head_out, head_out],
        out_shape=[jax.ShapeDtypeStruct((B, SEQ, 512), F32), jax.ShapeDtypeStruct((B, SEQ, 512), F32)],
        scratch_shapes=[pltpu.VMEM((SEQ, HEAD_DIM), F32)] * 2,
        compiler_params=_cparams(),
    )(pa, pa, pa, bias, qg, kg)


def attn_bwd(pa, bias, qg, kg, do, dlse, dpa, group, name):
    B = pa.shape[0]
    dilation = ATT_GROUPS[group][1]
    hb = group * 4

    def body(q_ref, k_ref, v_ref, b_ref, qg_ref, kg_ref, do_ref, dl_ref, dpa_in,
             dp_ref, db_ref, dqg_ref, dkg_ref, qn_s, kn_s, dq_s, dk_s, dv_s):
        del dpa_in
        h_id = pl.program_id(1)

        @pl.when(jnp.logical_and(pl.program_id(0) == 0, h_id == 0))
        def _():
            db_ref[...] = jnp.zeros_like(db_ref)
            dqg_ref[...] = jnp.zeros_like(dqg_ref)
            dkg_ref[...] = jnp.zeros_like(dkg_ref)

        dk_s[...] = jnp.zeros_like(dk_s)
        dv_s[...] = jnp.zeros_like(dv_s)
        qn_s[...] = _qk_norm(q_ref[...], qg_ref[...])
        kn_s[...] = _qk_norm(k_ref[...], kg_ref[...])
        bias_all, bias_first = _masked_bias(b_ref[0])
        for (r, n) in _attn_tiles(dilation):
            rows = _attn_rows(r, n, dilation)
            if n == 0:
                krows, bias_t = rows, bias_first
            else:
                krows, bias_t = _attn_rows(r, n - 1, dilation, 2), bias_all
            _, vjp = jax.vjp(_attn_tile, qn_s[rows, :], kn_s[krows, :], v_ref[krows, :], bias_t)
            dqn, dkn, dv, dbias = vjp((do_ref[rows, :], dl_ref[rows, :]))
            dq_s[rows, :] = dqn
            dk_s[krows, :] += dkn
            dv_s[krows, :] += dv
            if n == 0:
                db_ref[h_id, :, ATT_BLOCK:] += dbias
            else:
                db_ref[h_id] += dbias
        for x_ref, g_ref, d_s, dg_ref, lanes in ((q_ref, qg_ref, dq_s, dqg_ref, _QL), (k_ref, kg_ref, dk_s, dkg_ref, _KL)):
            _, vjp = jax.vjp(_qk_norm, x_ref[...], g_ref[...])
            dx, dg = vjp(d_s[...])
            dp_ref[0, :, lanes] = dx.astype(BF16)
            dg_ref[...] += dg
        dp_ref[0, :, _VL] = dv_s[...].astype(BF16)

    const2 = lambda b, h: (0, 0)
    head_in = pl.BlockSpec((None, SEQ, HEAD_DIM), lambda b, h: (b, 0, h))
    head_blk = pl.BlockSpec((1, SEQ, ATT_HEAD_COLS), lambda b, h: (b, 0, hb + h))
    return pl.pallas_call(
        body, name=name, grid=(B, 4),
        in_specs=_qkv_specs(hb) + [
                  pl.BlockSpec((1, ATT_BLOCK, 2 * ATT_BLOCK), lambda b, h: (hb + h, 0, 0)),
                  pl.BlockSpec((1, HEAD_DIM), const2), pl.BlockSpec((1, HEAD_DIM), const2),
                  head_in, head_in,
                  pl.BlockSpec(memory_space=pl.ANY)],
        out_specs=[head_blk,
                   pl.BlockSpec((4, ATT_BLOCK, 2 * ATT_BLOCK), lambda b, h: (0, 0, 0)),
                   pl.BlockSpec((1, HEAD_DIM), const2), pl.BlockSpec((1, HEAD_DIM), const2)],
        out_shape=[jax.ShapeDtypeStruct(dpa.shape, BF16),
                   jax.ShapeDtypeStruct((4, ATT_BLOCK, 2 * ATT_BLOCK), F32),
                   jax.ShapeDtypeStruct((1, HEAD_DIM), F32), jax.ShapeDtypeStruct((1, HEAD_DIM), F32)],
        scratch_shapes=[pltpu.VMEM((SEQ, HEAD_DIM), F32)] * 5,
        input_output_aliases={8: 0},
        compiler_params=_cparams(),
    )(pa, pa, pa, bias, qg, kg, do, dlse, dpa)


def _attn_classes(q, k, v, bias, qg, kg):
    s = cdot_nt(_qk_norm(q, qg), _qk_norm(k, kg)) * (HEAD_DIM ** -0.5) + bias
    m = lax.stop_gradient(jnp.max(s, axis=-1, keepdims=True))
    p = jnp.exp(s - m)
    l = jnp.sum(p, axis=-1, keepdims=True)
    o = cdot_nn(p, v) / l
    return o, jnp.broadcast_to(m + jnp.log(l), o.shape)


def _gather_classes(src_ref, dst_s, dilation):
    for r in range(dilation):
        dst_s[r] = src_ref[pl.ds(r, ATT_BLOCK, stride=dilation), :]


def _scatter_classes(src_s, dst_ref, dilation):
    for r in range(dilation):
        dst_ref[pl.ds(r, ATT_BLOCK, stride=dilation), :] = src_s[r]


def attn_fwd_classes(pa, bias, qg, kg, group, name):
    B = pa.shape[0]
    dilation = ATT_GROUPS[group][1]
    hb = group * 4

    def body(q_ref, k_ref, v_ref, b_ref, qg_ref, kg_ref, o_ref, l_ref, q_s, k_s, v_s):
        _gather_classes(q_ref, q_s, dilation)
        _gather_classes(k_ref, k_s, dilation)
        _gather_classes(v_ref, v_s, dilation)
        o, lse = _attn_classes(q_s[...], k_s[...], v_s[...], _masked_bias(b_ref[0])[1], qg_ref[...], kg_ref[...])
        q_s[...], k_s[...] = o, lse
        _scatter_classes(q_s, o_ref, dilation)
        _scatter_classes(k_s, l_ref, dilation)

    head_out = pl.BlockSpec((None, SEQ, HEAD_DIM), lambda b, h: (b, 0, h))
    return pl.pallas_call(
        body, name=name, grid=(B, 4),
        in_specs=_qkv_specs(hb) + [
                  pl.BlockSpec((1, ATT_BLOCK, 2 * ATT_BLOCK), lambda b, h: (hb + h, 0, 0)),
                  pl.BlockSpec((1, HEAD_DIM), lambda b, h: (0, 0)),
                  pl.BlockSpec((1, HEAD_DIM), lambda b, h: (0, 0))],
        out_specs=[head_out, head_out],
        out_shape=[jax.ShapeDtypeStruct((B, SEQ, 512), F32), jax.ShapeDtypeStruct((B, SEQ, 512), F32)],
        scratch_shapes=[pltpu.VMEM((dilation, ATT_BLOCK, HEAD_DIM), F32)] * 3,
        compiler_params=_cparams(),
    )(pa, pa, pa, bias, qg, kg)


def attn_bwd_classes(pa, bias, qg, kg, do, dlse, dpa, group, name):
    B = pa.shape[0]
    dilation = ATT_GROUPS[group][1]
    hb = group * 4

    def body(q_ref, k_ref, v_ref, b_ref, qg_ref, kg_ref, do_ref, dl_ref, dpa_in,
             dp_ref, db_ref, dqg_ref, dkg_ref, q_s, k_s, v_s, do_s, dl_s, rows_s):
        del dpa_in
        h_id = pl.program_id(1)

        @pl.when(jnp.logical_and(pl.program_id(0) == 0, h_id == 0))
        def _():
            db_ref[...] = jnp.zeros_like(db_ref)
            dqg_ref[...] = jnp.zeros_like(dqg_ref)
            dkg_ref[...] = jnp.zeros_like(dkg_ref)

        for src, dst in ((q_ref, q_s), (k_ref, k_s), (v_ref, v_s), (do_ref, do_s), (dl_ref, dl_s)):
            _gather_classes(src, dst, dilation)
        _, vjp = jax.vjp(_attn_classes, q_s[...], k_s[...], v_s[...], _masked_bias(b_ref[0])[1],
                         qg_ref[...], kg_ref[...])
        dq, dk, dv, dbias, dqg, dkg = vjp((do_s[...], dl_s[...]))
        db_ref[h_id, :, ATT_BLOCK:] += dbias
        dqg_ref[...] += dqg
        dkg_ref[...] += dkg
        for d, lanes in ((dq, _QL), (dk, _KL), (dv, _VL)):
            q_s[...] = d
            _scatter_classes(q_s, rows_s, dilation)
            dp_ref[0, :, lanes] = rows_s[...].astype(BF16)

    const2 = lambda b, h: (0, 0)
    head_in = pl.BlockSpec((None, SEQ, HEAD_DIM), lambda b, h: (b, 0, h))
    head_blk = pl.BlockSpec((1, SEQ, ATT_HEAD_COLS), lambda b, h: (b, 0, hb + h))
    return pl.pallas_call(
        body, name=name, grid=(B, 4),
        in_specs=_qkv_specs(hb) + [
                  pl.BlockSpec((1, ATT_BLOCK, 2 * ATT_BLOCK), lambda b, h: (hb + h, 0, 0)),
                  pl.BlockSpec((1, HEAD_DIM), const2), pl.BlockSpec((1, HEAD_DIM), const2),
                  head_in, head_in,
                  pl.BlockSpec(memory_space=pl.ANY)],
        out_specs=[head_blk,
                   pl.BlockSpec((4, ATT_BLOCK, 2 * ATT_BLOCK), lambda b, h: (0, 0, 0)),
                   pl.BlockSpec((1, HEAD_DIM), const2), pl.BlockSpec((1, HEAD_DIM), const2)],
        out_shape=[jax.ShapeDtypeStruct(dpa.shape, BF16),
                   jax.ShapeDtypeStruct((4, ATT_BLOCK, 2 * ATT_BLOCK), F32),
                   jax.ShapeDtypeStruct((1, HEAD_DIM), F32), jax.ShapeDtypeStruct((1, HEAD_DIM), F32)],
        scratch_shapes=[pltpu.VMEM((dilation, ATT_BLOCK, HEAD_DIM), F32)] * 5 + [pltpu.VMEM((SEQ, HEAD_DIM), F32)],
        input_output_aliases={8: 0},
        compiler_params=_cparams(),
    )(pa, pa, pa, bias, qg, kg, do, dlse, dpa)


def _merge(o0, o1, o2, l0, l1, l2):
    mx = jnp.maximum(jnp.maximum(l0, l1), l2)
    e0, e1, e2 = jnp.exp(l0 - mx), jnp.exp(l1 - mx), jnp.exp(l2 - mx)
    den = e0 + e1 + e2
    return (e0 / den) * o0 + (e1 / den) * o1 + (e2 / den) * o2


def merge_fwd(os_, ls_, name):
    B = os_[0].shape[0]

    def body(o0, o1, o2, l0, l1, l2, a_ref):
        a_ref[0] = _merge(o0[0], o1[0], o2[0], l0[0], l1[0], l2[0]).astype(BF16)

    return pl.pallas_call(
        body, name=name, grid=(B, SEQ // ROW_TILE),
        in_specs=[_row_spec(512)] * 6, out_specs=_row_spec(512),
        out_shape=jax.ShapeDtypeStruct((B, SEQ, 512), BF16),
        compiler_params=_cparams(),
    )(*os_, *ls_)


def merge_bwd(os_, ls_, datt, name):
    B = os_[0].shape[0]

    def body(o0, o1, o2, l0, l1, l2, da_ref, *outs):
        _, vjp = jax.vjp(_merge, o0[0], o1[0], o2[0], l0[0], l1[0], l2[0])
        for o_ref, g in zip(outs, vjp(da_ref[0])):
            o_ref[0] = g

    return pl.pallas_call(
        body, name=name, grid=(B, SEQ // ROW_TILE),
        in_specs=[_row_spec(512)] * 7, out_specs=[_row_spec(512)] * 6,
        out_shape=[jax.ShapeDtypeStruct((B, SEQ, 512), F32)] * 6,
        compiler_params=_cparams(),
    )(*os_, *ls_, datt)


def _gate_mix(ga, gm, ya, ym):
    return jax.nn.sigmoid(ga) * ya + jax.nn.sigmoid(gm) * ym


def _gate_halves(pg):
    return [(pg, (ROW_MM_TILE, D_MODEL), lambda i: (i, 0)), (pg, (ROW_MM_TILE, D_MODEL), lambda i: (i, 1))]


def ml_out_gate(hg, w_ml_out, pg, ya, name):
    T = hg.shape[0]

    def epi(acc, i, ex, out):
        ga, gm, ya_ref = ex
        out[0][...] = acc
        out[1][...] = _gate_mix(ga[...], gm[...], ya_ref[...], acc).astype(BF16)

    return row_matmul(hg, w_ml_out, mode="nn", name=name, extras=_gate_halves(pg) + [_rows(ya)],
                      outs=[_rows_out(T, F32), _rows_out(T, BF16)], epi=epi)


def d_z_gate_bwd(dy, w_out, pg, ya, ym, name):
    T = dy.shape[0]

    def epi(acc, i, ex, out):
        ga, gm, ya_ref, ym_ref = ex
        dpg_ref, dya_ref, dym_ref = out
        _, vjp = jax.vjp(_gate_mix, ga[...], gm[...], ya_ref[...], ym_ref[...])
        dga, dgm, dya, dym = vjp(acc)
        dpg_ref[:, :D_MODEL] = dga.astype(BF16)
        dpg_ref[:, D_MODEL:] = dgm.astype(BF16)
        dya_ref[...] = dya.astype(BF16)
        dym_ref[...] = dym.astype(BF16)

    return row_matmul(dy, w_out, mode="nt", name=name, extras=_gate_halves(pg) + [_rows(ya), _rows(ym)],
                      outs=[((T, GATE_COLS), BF16, (ROW_MM_TILE, GATE_COLS), lambda i: (i, 0)),
                            _rows_out(T, BF16), _rows_out(T, BF16)], epi=epi)


def _log_sigmoid(x):
    return jnp.minimum(x, 0.0) - jnp.log(1.0 + jnp.exp(-jnp.abs(x)))


def _head_mask(e):
    lane = lax.broadcasted_iota(jnp.int32, (1, 128), 1)
    return jnp.logical_and(lane >= e * 64, lane < (e + 1) * 64).astype(F32)


def _bmxu(a, b, ca, cb):
    return lax.dot_general(a.astype(BF16), b.astype(BF16), (((ca,), (cb,)), ((0,), (0,))), preferred_element_type=F32)


@jax.custom_vjp
def cdot_nt(a, b):
    return _bmxu(a, b, 2, 2)


cdot_nt.defvjp(lambda a, b: (_bmxu(a, b, 2, 2), (a, b)),
               lambda res, g: (_bmxu(g, res[1], 2, 1), _bmxu(g, res[0], 1, 1)))


@jax.custom_vjp
def cdot_nn(a, b):
    return _bmxu(a, b, 2, 1)


cdot_nn.defvjp(lambda a, b: (_bmxu(a, b, 2, 1), (a, b)),
               lambda res, g: (_bmxu(g, res[1], 2, 2), _bmxu(res[0], g, 1, 1)))


@jax.custom_vjp
def cdot_tn(a, b):
    return _bmxu(a, b, 1, 1)


cdot_tn.defvjp(lambda a, b: (_bmxu(a, b, 1, 1), (a, b)),
               lambda res, g: (_bmxu(res[1], g, 2, 2), _bmxu(res[0], g, 2, 1)))


def _top_bits(x):
    return lax.bitcast_convert_type(lax.bitcast_convert_type(x, jnp.uint32) & jnp.uint32(0xFFFF0000), F32)


def _split3(x):
    hi = _top_bits(x)
    r = x - hi
    mid = _top_bits(r)
    return hi, mid, r - mid


def _parts_in_lanes(col):
    hi, mid, lo = _split3(col)
    lane = lax.broadcasted_iota(jnp.int32, (1, 1, 8), 2)
    return jnp.where(lane == 0, hi, jnp.where(lane == 1, mid, jnp.where(lane == 2, lo, 0.0)))


def _parts_in_rows(row):
    hi, mid, lo = _split3(row)
    sub = lax.broadcasted_iota(jnp.int32, (1, 8, 1), 1)
    return jnp.where(sub == 0, hi, jnp.where(sub == 1, mid, jnp.where(sub == 2, lo, 0.0)))


def _chunk_matrix(kind, c):
    ri = lax.broadcasted_iota(jnp.int32, (c, ML_CHUNK, ML_CHUNK), 1)
    ci = lax.broadcasted_iota(jnp.int32, (c, ML_CHUNK, ML_CHUNK), 2)
    return {"eye": ri == ci, "lower": ri >= ci, "upper": ri <= ci}[kind].astype(F32)


def _col_col(kind, col):
    out = _bmxu(_chunk_matrix(kind, col.shape[0]), _parts_in_lanes(col), 2, 1)
    return jnp.sum(out, axis=-1, keepdims=True)


def _col_row(col):
    out = _bmxu(_parts_in_lanes(col), _chunk_matrix("eye", col.shape[0]), 1, 1)
    return jnp.sum(out, axis=1, keepdims=True)


def _row_col(row):
    out = _bmxu(_chunk_matrix("eye", row.shape[0]), _parts_in_rows(row), 2, 2)
    return jnp.sum(out, axis=-1, keepdims=True)


@jax.custom_vjp
def chunk_cumsum(col):
    return _col_col("lower", col)


chunk_cumsum.defvjp(lambda col: (_col_col("lower", col), None), lambda _, g: (_col_col("upper", g),))


@jax.custom_vjp
def col_to_row(col):
    return _col_row(col)


col_to_row.defvjp(lambda col: (_col_row(col), None), lambda _, g: (_row_col(g),))


def _gate_block(ifb):
    lane = lax.broadcasted_iota(jnp.int32, (1, 128), 1)
    return jnp.where(lane >= 2, _log_sigmoid(ifb), ifb)


def _ml_intra(q2, k2, v, ifb, *, e):
    c, L = N_CHUNKS, ML_CHUNK
    hm = _head_mask(e)
    q3 = (q2 * hm).reshape(c, L, 128)
    k3 = (k2 * hm).reshape(c, L, 128)
    v3 = v.reshape(c, L, 128)
    if3 = ifb.reshape(c, L, 128)
    lanes = lax.broadcasted_iota(jnp.int32, (c, L, 128), 2)
    li = jnp.sum(jnp.where(lanes == e, if3, 0.0), axis=-1, keepdims=True)
    lf = jnp.sum(jnp.where(lanes == 2 + e, if3, 0.0), axis=-1, keepdims=True)
    b = chunk_cumsum(lf)
    last = lax.broadcasted_iota(jnp.int32, (1, L, 1), 1) == L - 1
    b_end = jnp.sum(jnp.where(last, b, 0.0), axis=1, keepdims=True)
    causal = lax.broadcasted_iota(jnp.int32, (L, L), 0) >= lax.broadcasted_iota(jnp.int32, (L, L), 1)
    Dm = jnp.where(causal, b + col_to_row(li - b), -jnp.inf)
    mD = lax.stop_gradient(jnp.max(Dm, axis=-1, keepdims=True))
    P0 = cdot_nt(q3, k3) * jnp.exp(Dm - mD)
    H0 = cdot_nn(P0, v3)
    r0 = jnp.sum(P0, axis=-1, keepdims=True)
    g = b_end - b + li
    mg = lax.stop_gradient(jnp.max(g, axis=1, keepdims=True))
    kw = jnp.exp(g - mg) * k3
    return H0, r0, cdot_tn(kw, v3), jnp.sum(kw, axis=1, keepdims=True), b, b_end, mD, mg


def _ml_inter(q2, mo, gn, H0, r0, b, C_in, n_in, *, mD, m_in, e):
    c, L = N_CHUNKS, ML_CHUNK
    q3 = (q2 * _head_mask(e)).reshape(c, L, 128)
    a = b + m_in
    m_t = lax.stop_gradient(jnp.maximum(a, mD))
    c1 = jnp.exp(mD - m_t)
    c2 = jnp.exp(a - m_t)
    num = c1 * H0 + c2 * cdot_nn(q3, C_in)
    nq = c1 * r0 + c2 * jnp.sum(q3 * n_in, axis=-1, keepdims=True)
    h = num / jnp.maximum(jnp.abs(nq), jnp.exp(-m_t))
    hg = _rms(h) * gn * jax.nn.sigmoid(mo.reshape(c, L, 128))
    return hg.reshape(c * L, 128)


def _state_sweep(U_s, un_s, be_s, mg_s, Cin_s, nin_s, min_s, al_s, bt_s):
    def step(j, carry):
        C, n, m = carry
        Cin_s[j], nin_s[j], min_s[j] = C, n, m
        m_out = jnp.maximum(be_s[j] + m, mg_s[j])
        al = jnp.exp(be_s[j] + m - m_out)
        bt = jnp.exp(mg_s[j] - m_out)
        al_s[j], bt_s[j] = al, bt
        return al * C + bt * U_s[j], al * n + bt * un_s[j], m_out

    lax.fori_loop(0, N_CHUNKS, step, (jnp.zeros((128, 128), F32), jnp.zeros((1, 128), F32), jnp.zeros((1, 1), F32)))


def _state_sweep_bwd(U_s, un_s, dbe_s, Cin_s, nin_s, dCp_s, dnp_s, al_s, bt_s):
    def step(t, carry):
        j = N_CHUNKS - 1 - t
        dC, dn = carry
        al, bt = al_s[j], bt_s[j]
        U_s[j] = bt * dC
        un_s[j] = bt * dn
        dal = jnp.sum(jnp.sum(dC * Cin_s[j], axis=1, keepdims=True), axis=0, keepdims=True) \
            + jnp.sum(dn * nin_s[j], axis=1, keepdims=True)
        dbe_s[j] = dal * al
        return dCp_s[j] + al * dC, dnp_s[j] + al * dn

    lax.fori_loop(0, N_CHUNKS, step, (jnp.zeros((128, 128), F32), jnp.zeros((1, 128), F32)))


def _state_scratch():
    c = N_CHUNKS
    return [pltpu.VMEM((c, 128, 128), F32), pltpu.VMEM((c, 1, 128), F32), pltpu.VMEM((c, 1, 1), F32),
            pltpu.VMEM((c, 1, 1), F32),
            pltpu.VMEM((c, 128, 128), F32), pltpu.VMEM((c, 1, 128), F32), pltpu.VMEM((c, 1, 1), F32),
            pltpu.VMEM((c, 1, 1), F32), pltpu.VMEM((c, 1, 1), F32)]


def _shift_down(x, s):
    if s == 0:
        return x
    rows = lax.broadcasted_iota(jnp.int32, x.shape, 0)
    return jnp.where(rows >= s, pltpu.roll(x, s, 0), 0.0)


def _shift_up(x, s):
    if s == 0:
        return x
    S = x.shape[0]
    rows = lax.broadcasted_iota(jnp.int32, x.shape, 0)
    return jnp.where(rows < S - s, pltpu.roll(x, S - s, 0), 0.0)


def _conv_pre(x, cw, cb):
    y = cb + cw[3:4, :] * x
    for j in range(3):
        y = y + cw[j:j + 1, :] * _shift_down(x, 3 - j)
    return y


def _conv_bwd(x, cw, dpre):
    dx = cw[3:4, :] * dpre
    dcw = [None] * 4
    dcw[3] = jnp.sum(dpre * x, axis=0, keepdims=True)
    for j in range(3):
        dx = dx + cw[j:j + 1, :] * _shift_up(dpre, 3 - j)
        dcw[j] = jnp.sum(dpre * _shift_down(x, 3 - j), axis=0, keepdims=True)
    return dx, dcw, jnp.sum(dpre, axis=0, keepdims=True)


def _silu(z):
    return z * jax.nn.sigmoid(z)


def _dsilu(z):
    s = jax.nn.sigmoid(z)
    return s * (1.0 + z * (1.0 - s))


_ML_Q, _ML_K = slice(0, 128), slice(128, 256)
_ML_IF = slice(768, 896)


def _ml_v(e):
    return slice(256 + e * 128, 384 + e * 128)


def _ml_o(e):
    return slice(512 + e * 128, 640 + e * 128)


def _ml_specs():
    pair = lambda b, p: (b, 0, p)
    return [pl.BlockSpec((1, SEQ, ML_PAIR_COLS), pair),
            pl.BlockSpec((1, 4, 128), lambda b, p: (p, 0, 0)),
            pl.BlockSpec((1, 4, 128), lambda b, p: (4 + p, 0, 0)),
            pl.BlockSpec((1, 1, 128), lambda b, p: (p, 0, 0)),
            pl.BlockSpec((1, 1, 128), lambda b, p: (4 + p, 0, 0)),
            pl.BlockSpec((1, 1, 128), lambda b, p: (p, 0, 0)),
            pl.BlockSpec((1, 1, 256), lambda b, p: (p, 0, 0))]


def mlstm_fwd(pm, cw8, cb8, bifp, gn4, name):
    B = pm.shape[0]

    def body(p_ref, cwq, cwk, cbq, cbk, bif_ref, gn_ref, hg_ref, *st):
        U_s, un_s, be_s, mg_s, Cin_s, nin_s, min_s, al_s, bt_s = st
        qc = _silu(_conv_pre(p_ref[0, :, _ML_Q], cwq[0], cbq[0]))
        kc = _silu(_conv_pre(p_ref[0, :, _ML_K], cwk[0], cbk[0])) * (64 ** -0.5)
        ifb = _gate_block(p_ref[0, :, _ML_IF] + bif_ref[0])
        for e in range(2):
            lanes = slice(e * 128, (e + 1) * 128)
            H0, r0, U, un, b, b_end, mD, mg = _ml_intra(qc, kc, p_ref[0, :, _ml_v(e)], ifb, e=e)
            U_s[...], un_s[...], be_s[...], mg_s[...] = U, un, b_end, mg
            _state_sweep(*st)
            hg = _ml_inter(qc, p_ref[0, :, _ml_o(e)], gn_ref[0, :, lanes], H0, r0, b, Cin_s[...], nin_s[...],
                           mD=mD, m_in=min_s[...], e=e)
            hg_ref[0, :, lanes] = hg.astype(BF16)

    return pl.pallas_call(
        body, name=name, grid=(B, ML_PAIRS),
        in_specs=_ml_specs(),
        out_specs=pl.BlockSpec((1, SEQ, 256), lambda b, p: (b, 0, p)),
        out_shape=jax.ShapeDtypeStruct((B, SEQ, D_MODEL), BF16),
        scratch_shapes=_state_scratch(),
        compiler_params=_cparams(),
    )(pm, cw8, cw8, cb8, cb8, bifp, gn4)


def mlstm_bwd(pm, cw8, cb8, bifp, gn4, dhg, name):
    B = pm.shape[0]

    def body(p_ref, cwq, cwk, cbq, cbk, bif_ref, gn_ref, dh_ref,
             dp_ref, dcw_ref, dcb_ref, dbif_ref, dgn_ref, *scr):
        st = scr[:9]
        U_s, un_s, be_s, mg_s, Cin_s, nin_s, min_s, al_s, bt_s = st
        dCp_s, dnp_s, dbe_s = scr[9:]
        p_id = pl.program_id(1)

        @pl.when(jnp.logical_and(pl.program_id(0) == 0, p_id == 0))
        def _():
            dcw_ref[...] = jnp.zeros_like(dcw_ref)
            dcb_ref[...] = jnp.zeros_like(dcb_ref)
            dbif_ref[...] = jnp.zeros_like(dbif_ref)
            dgn_ref[...] = jnp.zeros_like(dgn_ref)

        pre_q = _conv_pre(p_ref[0, :, _ML_Q], cwq[0], cbq[0])
        pre_k = _conv_pre(p_ref[0, :, _ML_K], cwk[0], cbk[0])
        qc = _silu(pre_q)
        kc = _silu(pre_k) * (64 ** -0.5)
        ifb, gate_vjp = jax.vjp(_gate_block, p_ref[0, :, _ML_IF] + bif_ref[0])
        dq = jnp.zeros((SEQ, 128), F32)
        dk = jnp.zeros((SEQ, 128), F32)
        difb = jnp.zeros((SEQ, 128), F32)
        for e in range(2):
            lanes = slice(e * 128, (e + 1) * 128)
            (H0, r0, U, un, b, b_end, mD, mg), vjp1 = jax.vjp(functools.partial(_ml_intra, e=e), qc, kc,
                                                              p_ref[0, :, _ml_v(e)], ifb)
            U_s[...], un_s[...], be_s[...], mg_s[...] = U, un, b_end, mg
            _state_sweep(*st)
            _, vjp3 = jax.vjp(functools.partial(_ml_inter, mD=mD, m_in=min_s[...], e=e), qc, p_ref[0, :, _ml_o(e)],
                              gn_ref[0, :, lanes], H0, r0, b, Cin_s[...], nin_s[...])
            dq_a, dmo, dgn, dH0, dr0, db_a, dCp, dnp = vjp3(dh_ref[0, :, lanes])
            dCp_s[...], dnp_s[...] = dCp, dnp
            _state_sweep_bwd(U_s, un_s, dbe_s, Cin_s, nin_s, dCp_s, dnp_s, al_s, bt_s)
            dq_b, dk_b, dv, difb_e = vjp1((dH0, dr0, U_s[...], un_s[...], db_a, dbe_s[...],
                                           jnp.zeros_like(mD), jnp.zeros_like(mg)))
            dq, dk, difb = dq + dq_a + dq_b, dk + dk_b, difb + difb_e
            dp_ref[0, :, _ml_v(e)] = dv.astype(BF16)
            dp_ref[0, :, _ml_o(e)] = dmo.astype(BF16)
            dgn_ref[p_id, :, lanes] += dgn
        (difb,) = gate_vjp(difb)
        dp_ref[0, :, _ML_IF] = difb.astype(BF16)
        dbif_ref[p_id] += jnp.sum(difb, axis=0, keepdims=True)

        for (sl, cw, pre, d, blk, scale) in ((_ML_Q, cwq, pre_q, dq, p_id, 1.0), (_ML_K, cwk, pre_k, dk, 4 + p_id, 64 ** -0.5)):
            xr = p_ref[0, :, sl]
            dpre = d * scale * _dsilu(pre)
            dx, dcw, dcb = _conv_bwd(xr, cw[0], dpre)
            dp_ref[0, :, sl] = dx.astype(BF16)
            for j in range(4):
                dcw_ref[blk, j:j + 1, :] += dcw[j]
            dcb_ref[blk] += dcb

    full3 = lambda b, p: (0, 0, 0)
    return pl.pallas_call(
        body, name=name, grid=(B, ML_PAIRS),
        in_specs=[pl.BlockSpec((1, SEQ, ML_PAIR_COLS), lambda b, p: (b, 0, p), pipeline_mode=pl.Buffered(1))]
        + _ml_specs()[1:] + [pl.BlockSpec((1, SEQ, 256), lambda b, p: (b, 0, p), pipeline_mode=pl.Buffered(1))],
        out_specs=[pl.BlockSpec((1, SEQ, ML_PAIR_COLS), lambda b, p: (b, 0, p)),
                   pl.BlockSpec((8, 4, 128), full3), pl.BlockSpec((8, 1, 128), full3),
                   pl.BlockSpec((4, 1, 128), full3), pl.BlockSpec((4, 1, 256), full3)],
        out_shape=[jax.ShapeDtypeStruct((B, SEQ, ML_COLS), BF16),
                   jax.ShapeDtypeStruct((8, 4, 128), F32), jax.ShapeDtypeStruct((8, 1, 128), F32),
                   jax.ShapeDtypeStruct((4, 1, 128), F32), jax.ShapeDtypeStruct((4, 1, 256), F32)],
        scratch_shapes=_state_scratch() + [pltpu.VMEM((N_CHUNKS, 128, 128), F32), pltpu.VMEM((N_CHUNKS, 1, 128), F32),
                                           pltpu.VMEM((N_CHUNKS, 1, 1), F32)],
        compiler_params=_cparams(),
    )(pm, cw8, cw8, cb8, cb8, bifp, gn4, dhg)


def _adamw(w, g, m, v):
    m = ADAM_B1 * m + (1.0 - ADAM_B1) * g
    v = ADAM_B2 * v + (1.0 - ADAM_B2) * (g * g)
    m_hat = m / (1.0 - ADAM_B1 ** ADAM_STEP)
    v_hat = v / (1.0 - ADAM_B2 ** ADAM_STEP)
    delta = -ADAM_LR * (m_hat / (jnp.sqrt(v_hat) + ADAM_EPS) + ADAM_WD * w)
    return delta, m, v


def adamw(w, g, m, v, name, parts=False):
    R, C = w.shape
    if R % 8 == 0 or R * C * 4 <= (1 << 20):
        tr = _pick(R, (256, 128, 64, 32, 16, 8)) if R * C * 4 > (1 << 20) else R
        steps = R // tr
        spec = pl.BlockSpec((tr, C), lambda i: (i, 0))
        g_spec = pl.BlockSpec((N_DEV, tr, C), lambda i: (0, i, 0)) if parts else spec
    else:
        tc = _pick(C, (256, 128))
        steps = C // tc
        spec = pl.BlockSpec((R, tc), lambda i: (0, i))
        g_spec = pl.BlockSpec((N_DEV, R, tc), lambda i: (0, 0, i)) if parts else spec

    def body(w_ref, g_ref, m_ref, v_ref, go_ref, d_ref, mo_ref, vo_ref):
        if parts:
            g = g_ref[0].astype(F32)
            for k in range(1, N_DEV):
                g = g + g_ref[k].astype(F32)
        else:
            g = g_ref[...]
        d, mn, vn = _adamw(w_ref[...], g, m_ref[...], v_ref[...])
        go_ref[...], d_ref[...], mo_ref[...], vo_ref[...] = g, d, mn, vn

    return pl.pallas_call(
        body, name=name, grid=(steps,),
        in_specs=[spec, g_spec, spec, spec], out_specs=[spec] * 4,
        out_shape=[jax.ShapeDtypeStruct((R, C), F32)] * 4,
        compiler_params=_cparams(),
    )(w, g, m, v)


def adamw_many(ws, gs, ms, vs, name):
    n = len(ws)

    def fn(*a):
        out = []
        for j in range(n):
            out += list(_adamw(a[j], a[n + j], a[2 * n + j], a[3 * n + j]))
        return tuple(out)

    shapes = [jax.ShapeDtypeStruct(w.shape, F32) for w in ws for _ in range(3)]
    return small_call(fn, list(ws) + list(gs) + list(ms) + list(vs), shapes, name)


def _mesh_pos():
    return lax.axis_index("x"), lax.axis_index("y"), lax.axis_index("c")


def _flip(pos, f):
    x, y, c = pos
    return (1 - x if f & 4 else x, 1 - y if f & 2 else y, 1 - c if f & 1 else c)


def _index(pos):
    return 4 * pos[0] + 2 * pos[1] + pos[2]


def _exchange(arrs, name, scatter):
    n = len(arrs)

    def body(*refs):
        ins, outs = refs[:n], refs[n:2 * n]
        send, recv, lsem = refs[2 * n:]
        me = _mesh_pos()
        mine = _index(me)
        copies = []
        for i in range(n):
            src = ins[i].at[mine] if scatter else ins[i]
            loc = pltpu.make_async_copy(src, outs[i].at[mine], lsem.at[i])
            loc.start()
            copies.append(loc)
            for f in range(1, N_DEV):
                peer = _flip(me, f)
                src = ins[i].at[_index(peer)] if scatter else ins[i]
                cp = pltpu.make_async_remote_copy(
                    src_ref=src, dst_ref=outs[i].at[mine],
                    send_sem=send.at[i * 7 + f - 1], recv_sem=recv.at[i * 7 + f - 1],
                    device_id=peer, device_id_type=pl.DeviceIdType.MESH)
                cp.start()
                copies.append(cp)
        for cp in copies:
            cp.wait()

    any_spec = pl.BlockSpec(memory_space=pl.ANY)
    out_shape = [jax.ShapeDtypeStruct(a.shape if scatter else (N_DEV,) + a.shape, a.dtype) for a in arrs]
    res = pl.pallas_call(
        body, name=name,
        in_specs=[any_spec] * n, out_specs=[any_spec] * n, out_shape=out_shape,
        scratch_shapes=[pltpu.SemaphoreType.DMA((7 * n,)), pltpu.SemaphoreType.DMA((7 * n,)),
                        pltpu.SemaphoreType.DMA((n,))],
        compiler_params=_cparams(),
    )(*arrs)
    return list(res)


def all_gather(arrs, name):
    return _exchange(arrs, name, False)


def all_gather_two_level(arrs, name):
    n = len(arrs)

    def body(*refs):
        ins, outs = refs[:n], refs[n:2 * n]
        send, recv, lsem = refs[2 * n:]
        x, y, c = _mesh_pos()
        me, sibling = (x, y, c), (x, y, 1 - c)
        chips = [(1 - x, y), (x, 1 - y), (1 - x, 1 - y)]

        def copy(i, k, block, to, src=None):
            rows = outs[i].at[_index(block)]
            return pltpu.make_async_remote_copy(
                src_ref=rows if src is None else src, dst_ref=rows,
                send_sem=send.at[i * 7 + k], recv_sem=recv.at[i * 7 + k],
                device_id=to, device_id_type=pl.DeviceIdType.MESH)

        local = [pltpu.make_async_copy(ins[i], outs[i].at[_index(me)], lsem.at[i]) for i in range(n)]
        first = [copy(i, 0, me, sibling, src=ins[i]) for i in range(n)]
        first += [copy(i, 1 + j, me, (*chip, c), src=ins[i]) for i in range(n) for j, chip in enumerate(chips)]
        for cp in local + first:
            cp.start()
        passed = []
        for j, chip in enumerate(chips):
            for i in range(n):
                copy(i, 1 + j, (*chip, c), me).wait_recv()
                cp = copy(i, 4 + j, (*chip, c), sibling)
                cp.start()
                passed.append(cp)
        for i in range(n):
            copy(i, 0, sibling, me).wait_recv()
            for j, chip in enumerate(chips):
                copy(i, 4 + j, (*chip, 1 - c), me).wait_recv()
        for cp in first + passed:
            cp.wait_send()
        for cp in local:
            cp.wait()

    any_spec = pl.BlockSpec(memory_space=pl.ANY)
    res = pl.pallas_call(
        body, name=name,
        in_specs=[any_spec] * n, out_specs=[any_spec] * n,
        out_shape=[jax.ShapeDtypeStruct((N_DEV,) + a.shape, a.dtype) for a in arrs],
        scratch_shapes=[pltpu.SemaphoreType.DMA((7 * n,)), pltpu.SemaphoreType.DMA((7 * n,)),
                        pltpu.SemaphoreType.DMA((n,))],
        compiler_params=_cparams(),
    )(*arrs)
    return list(res)


def all_to_all(arrs, name):
    return _exchange(arrs, name, True)


_HBM = pl.BlockSpec(memory_space=pltpu.HBM)
_SEM = pl.BlockSpec(memory_space=pltpu.SEMAPHORE)
_EFFECT = pltpu.SideEffectType.DATAFLOW_SIDE_EFFECTING


def _split_copies(ins, lands, send, recv, scatter, waiting):
    me = _mesh_pos()
    mine = _index(me)
    copies = []
    for i in range(len(ins)):
        for f in range(1, N_DEV):
            peer = _flip(me, f)
            src = ins[i].at[_index(peer)] if scatter else ins[i]
            copies.append(pltpu.make_async_remote_copy(
                src_ref=src, dst_ref=lands[i].at[_index(peer) if waiting else mine],
                send_sem=send.at[i * 7 + f - 1], recv_sem=recv.at[i * 7 + f - 1],
                device_id=peer, device_id_type=pl.DeviceIdType.MESH))
    return copies


def exchange_start(arrs, name, scatter, after=()):
    n = len(arrs)
    land_shapes = [a.shape if scatter else (N_DEV,) + a.shape for a in arrs]

    def body(*refs):
        ins, lands = refs[:n], refs[n:2 * n]
        send, recv = refs[2 * n + len(after)], refs[2 * n + len(after) + 1]
        token = refs[-1]
        for cp in _split_copies(ins, lands, send, recv, scatter, False):
            cp.start()
        token[...] = jnp.zeros_like(token)

    res = pl.pallas_call(
        body, name=name,
        out_shape=(pltpu.SemaphoreType.DMA((7 * n,)), pltpu.SemaphoreType.DMA((7 * n,)),
                   *[pltpu.HBM(a.shape, a.dtype) for a in arrs],
                   *[pltpu.HBM(s, a.dtype) for s, a in zip(land_shapes, arrs)],
                   jax.ShapeDtypeStruct((8, 128), F32)),
        in_specs=[_HBM] * (2 * n) + [pl.BlockSpec(memory_space=pl.ANY)] * len(after),
        out_specs=(_SEM, _SEM, *[_HBM] * (2 * n), pl.BlockSpec(memory_space=pltpu.VMEM)),
        input_output_aliases={i: 2 + i for i in range(2 * n)},
        compiler_params=pltpu.CompilerParams(has_side_effects=_EFFECT),
    )(*[pltpu.with_memory_space_constraint(a, pltpu.HBM) for a in arrs],
      *[pltpu.with_memory_space_constraint(lax.empty(s, a.dtype), pltpu.HBM) for s, a in zip(land_shapes, arrs)],
      *after)
    return (res[0], res[1], list(res[2:2 + n]), list(res[2 + n:2 + 2 * n])), res[-1]


def exchange_wait(handle, after, name, scatter):
    send, recv, srcs, lands = handle
    n = len(srcs)

    def body(*refs):
        ins, lnd = refs[:n], refs[n:2 * n]
        send_, recv_ = refs[2 * n], refs[2 * n + 1]
        for cp in _split_copies(ins, lnd, send_, recv_, scatter, True):
            cp.wait_send()
            cp.wait_recv()

    res = pl.pallas_call(
        body, name=name,
        out_shape=(*[pltpu.HBM(a.shape, a.dtype) for a in srcs], *[pltpu.HBM(a.shape, a.dtype) for a in lands]),
        in_specs=[_HBM] * (2 * n) + [_SEM, _SEM, pl.BlockSpec(memory_space=pl.ANY)],
        out_specs=tuple([_HBM] * (2 * n)),
        input_output_aliases={i: i for i in range(2 * n)},
        compiler_params=pltpu.CompilerParams(has_side_effects=_EFFECT),
    )(*srcs, *lands, send, recv, after)
    return list(res[n:])


def _own_slot(land, own):
    return lax.dynamic_update_slice(land, own[None], (_index(_mesh_pos()),) + (0,) * own.ndim)


def cast_bf16(arrs, name):
    outs = []
    for i, a in enumerate(arrs):
        R, C = a.shape
        if R % 8 == 0:
            tr = _pick(R, (256, 128, 64, 32, 16, 8)) if R * C * 4 > (1 << 21) else R
            steps, spec = R // tr, pl.BlockSpec((tr, C), lambda i: (i, 0))
        else:
            steps, spec = C // 256, pl.BlockSpec((R, 256), lambda i: (0, i))

        def body(a_ref, o_ref):
            o_ref[...] = a_ref[...].astype(BF16)

        outs.append(pl.pallas_call(body, name=f"{name}_{i}", grid=(steps,), in_specs=[spec], out_specs=spec,
                                   out_shape=jax.ShapeDtypeStruct((R, C), BF16), compiler_params=_cparams())(a))
    return outs


def sum_parts(parts, name):
    def fn(p):
        g = p[0]
        for k in range(1, N_DEV):
            g = g + p[k]
        return (g,)
    return small_call(fn, [parts], [jax.ShapeDtypeStruct(parts.shape[1:], F32)], name)[0]


_SPLITS = np.cumsum([1536, 1536, 1536, 512, 512, 1024, 1024, 8, 8, 2048])[:-1].tolist()


def split_w_in(w):
    aq, ak, av, mq, mk, mv, mo, mi, mf, gates = jnp.split(w, _SPLITS, axis=1)
    R = w.shape[0]
    w_att = jnp.stack([aq.reshape(R, 12, 128), ak.reshape(R, 12, 128), av.reshape(R, 12, 128)], axis=2)
    gif = jnp.concatenate([mi.reshape(R, 4, 2), mf.reshape(R, 4, 2), jnp.zeros((R, 4, 124), w.dtype)], axis=2)
    w_ml = jnp.concatenate([mq.reshape(R, 4, 128), mk.reshape(R, 4, 128), mv.reshape(R, 4, 256),
                            mo.reshape(R, 4, 256), gif], axis=2)
    return w_att.reshape(R, ATT_COLS), w_ml.reshape(R, ML_COLS), gates


def merge_w_in(g_att, g_ml, g_gate):
    R = g_att.shape[0]
    a = g_att.reshape(R, 12, 3, 128)
    m = g_ml.reshape(R, 4, ML_PAIR_COLS)
    gif = m[:, :, 768:772]
    return jnp.concatenate([
        a[:, :, 0].reshape(R, 1536), a[:, :, 1].reshape(R, 1536), a[:, :, 2].reshape(R, 1536),
        m[:, :, 0:128].reshape(R, 512), m[:, :, 128:256].reshape(R, 512),
        m[:, :, 256:512].reshape(R, 1024), m[:, :, 512:768].reshape(R, 1024),
        gif[:, :, 0:2].reshape(R, 8), gif[:, :, 2:4].reshape(R, 8), g_gate], axis=1)


def _blk8(v, width=128):
    r = v.shape[0]
    return v.reshape(r, 1024 // width, width).transpose(1, 0, 2)


def _unblk8(v):
    nb, r, w = v.shape
    return v.transpose(1, 0, 2).reshape(r, nb * w)


def local_step(x, target, mods, w, small, late_w=None, early_g=None, w_in_g=None):
    late_w = late_w or (lambda after: w)
    big = {}
    early_g = early_g or (lambda g: big.update(g))
    w_in_g = w_in_g or (lambda g: big.update(w_in=g))
    B = x.shape[0]
    T = B * SEQ
    shift1, scale1, gate1, shift2, scale2, gate2 = mods
    f2 = lambda a: a.reshape(T, a.shape[-1])
    f3 = lambda a: a.reshape(B, SEQ, a.shape[-1])

    rel_t = jnp.pad(small["rel_bias"].T, ((0, 4), (0, 0)))
    onehots = [_bucket_onehot(d) for _, d in ATT_GROUPS]
    biases = [bias_expand(rel_t, oh, f"bias_expand{g}").reshape(16, ATT_BLOCK, 2 * ATT_BLOCK)
              for g, oh in enumerate(onehots)]
    qg, kg = small["q_norm_g"], small["k_norm_g"]
    cw8 = _blk8(small["conv_w"])
    cb8 = _blk8(small["conv_b"])
    b_if = small["b_if"].reshape(2, 4, 2)
    bifp = jnp.concatenate([b_if[0], b_if[1], jnp.zeros((4, 124), F32)], axis=1).reshape(4, 1, 128)
    gn4 = small["mlstm_norm_g"].reshape(4, 1, 256)

    u = modnorm_fwd(x, small["norm1_g"], scale1, shift1, "modnorm1")
    u2d = f2(u)
    pa = f3(matmul(u2d, w["w_att"], mode="nn", name="proj_att"))
    pm = f3(matmul(u2d, w["w_ml"], mode="nn", name="proj_ml"))
    pg = matmul(u2d, w["w_gate"], mode="nn", name="proj_gate")
    os_, ls_ = [], []
    one_block = [SEQ // d == ATT_BLOCK for _, d in ATT_GROUPS]
    for g in range(3):
        o, l = (attn_fwd_classes if one_block[g] else attn_fwd)(pa, biases[g], qg, kg, g, f"attn_fwd{g}")
        os_.append(o)
        ls_.append(l)
    att = merge_fwd(os_, ls_, "merge_fwd")
    hg = mlstm_fwd(pm, cw8, cb8, bifp, gn4, "mlstm_fwd")
    w = {**w, **late_w(hg)}
    y_att = matmul(f2(att), w["w_att_out"], mode="nn", name="att_out")
    y_ml, z = ml_out_gate(f2(hg), w["w_ml_out"], pg, y_att, "ml_out")
    y, x1, u2 = out_proj_resid_modnorm(f2(z), w["w_out"], f2(x), gate1, small["norm2_g"], scale2, shift2, "out_proj")
    pre, hdn = matmul(u2, w["w_ff1"], mode="nn", name="ff1", out_dtypes=(BF16, BF16),
                      epi=lambda acc: (acc, jnp.square(jnp.maximum(acc, 0.0))))
    dx2, d_ffo, loss, d_gate2 = ff2_loss(hdn, w["w_ff2"], x1, gate2, f2(target), "ff2_loss")

    g_ff2 = matmul(hdn, d_ffo, mode="tn", name="g_ff2", out_dtypes=(BF16,))
    d_pre = matmul(d_ffo, w["w_ff2"], mode="nt", name="d_hdn", out_dtypes=(BF16,), extras=(pre,),
                   epi=lambda acc, p: (acc * (2.0 * jnp.maximum(p.astype(F32), 0.0)),))
    g_ff1 = matmul(u2, d_pre, mode="tn", name="g_ff1", out_dtypes=(BF16,))
    dx1, d_norm2, d_scale2, d_shift2, dy, d_gate1 = d_u_modnorm_bwd(
        d_pre, w["w_ff1"], None, x1, small["norm2_g"], scale2, shift2, dx2, y, gate1, "d_u2")
    g_out = matmul(f2(z), dy, mode="tn", name="g_out", out_dtypes=(BF16,))
    dpg, d_ya, d_ym = d_z_gate_bwd(dy, w["w_out"], pg, y_att, y_ml, "d_z")
    g_att_out = matmul(f2(att), f2(d_ya), mode="tn", name="g_att_out", out_dtypes=(BF16,))
    d_att = matmul(f2(d_ya), w["w_att_out"], mode="nt", name="d_att")
    g_ml_out = matmul(f2(hg), f2(d_ym), mode="tn", name="g_ml_out", out_dtypes=(BF16,))
    d_hg = matmul(f2(d_ym), w["w_ml_out"], mode="nt", name="d_hg")
    started = early_g(dict(w_att_out=g_att_out, w_ml_out=g_ml_out, w_out=g_out, w_ff1=g_ff1, w_ff2=g_ff2))
    order = 0.0 if started is None else started[0, 0]
    dmerge = merge_bwd(os_, ls_, f3(d_att), "merge_bwd")
    dpa = lax.empty((B, SEQ, ATT_COLS), BF16)
    d_rel = []
    d_qg = d_kg = None
    for g in range(3):
        dpa, dbias, dq_g, dk_g = (attn_bwd_classes if one_block[g] else attn_bwd)(
            pa, biases[g], qg + order, kg, dmerge[g], dmerge[3 + g], dpa, g, f"attn_bwd{g}")
        db8 = jnp.pad(dbias.reshape(4, -1), ((0, 4), (0, 0)))
        d_rel.append(bias_reduce(db8, onehots[g], f"bias_reduce{g}")[:4])
        d_qg = dq_g if d_qg is None else d_qg + dq_g
        d_kg = dk_g if d_kg is None else d_kg + dk_g
    dpm, dcw8, dcb8, dbifp, dgn4 = mlstm_bwd(pm, cw8, cb8, bifp, gn4 + order, f3(d_hg), "mlstm_bwd")
    g_w_att = matmul(u2d, f2(dpa), mode="tn", name="g_w_att", out_dtypes=(BF16,))
    g_w_ml = matmul(u2d, f2(dpm), mode="tn", name="g_w_ml", out_dtypes=(BF16,))
    g_w_gate = matmul(u2d, f2(dpg), mode="tn", name="g_w_gate", out_dtypes=(BF16,))
    started = w_in_g(merge_w_in(g_w_att, g_w_ml, g_w_gate))
    du = matmul(f2(dpa), w["w_att"], mode="nt", name="d_u_att", after=() if started is None else (started,))
    du = matmul(f2(dpm), w["w_ml"], mode="nt", name="d_u_ml", extras=(du,), epi=lambda acc, e: (acc + e,))
    grad_x, d_norm1, d_scale1, d_shift1 = d_u_modnorm_bwd(
        f2(dpg), w["w_gate"], du, f2(x), small["norm1_g"], scale1, shift1, dx1, None, None, "d_u_gate")
    grad_x = f3(grad_x)

    d_mods = (d_shift1, d_scale1, d_gate1, d_shift2, d_scale2, d_gate2)
    dbif = dbifp.reshape(4, 128)
    small_g = dict(
        norm1_g=d_norm1, norm2_g=d_norm2,
        b_if=jnp.stack([dbif[:, 0:2].reshape(8), dbif[:, 2:4].reshape(8)]),
        conv_w=_unblk8(dcw8), conv_b=_unblk8(dcb8), q_norm_g=d_qg, k_norm_g=d_kg,
        rel_bias=jnp.concatenate(d_rel, axis=0).T,
        mlstm_norm_g=dgn4.reshape(1, 1024))
    return loss, grad_x, d_mods, big, small_g


_SMALL = (("b_ada", 6144), ("norm1_g", 1024), ("norm2_g", 1024), ("b_if", 16), ("conv_b", 1024),
          ("q_norm_g", 128), ("k_norm_g", 128), ("rel_bias", 384), ("mlstm_norm_g", 1024), ("conv_w", 4096))
_SMALL_ROWS = 120
_REPL = _SMALL[:-1]


def _pack(d, names, rows):
    flat = jnp.concatenate([d[k].reshape(-1) for k, _ in names])
    return jnp.pad(flat, (0, rows * 128 - flat.shape[0])).reshape(rows, 128)


def _unpack(slab, names, shapes):
    flat = slab.reshape(-1)
    out, off = {}, 0
    for k, nel in names:
        out[k] = flat[off:off + nel].reshape(shapes[k])
        off += nel
    return out


def kernel(x, c, w_ada, b_ada, norm1_g, norm2_g, w_in, b_if, conv_w, conv_b, q_norm_g, k_norm_g, rel_bias, mlstm_norm_g, w_att_out, w_ml_out, w_out, w_ff1, w_ff2, loss_target, m_w_ada, m_b_ada, m_norm1_g, m_norm2_g, m_w_in, m_b_if, m_conv_w, m_conv_b, m_q_norm_g, m_k_norm_g, m_rel_bias, m_mlstm_norm_g, m_w_att_out, m_w_ml_out, m_w_out, m_w_ff1, m_w_ff2, v_w_ada, v_b_ada, v_norm1_g, v_norm2_g, v_w_in, v_b_if, v_conv_w, v_conv_b, v_q_norm_g, v_k_norm_g, v_rel_bias, v_mlstm_norm_g, v_w_att_out, v_w_ml_out, v_w_out, v_w_ff1, v_w_ff2):
    P = dict(w_ada=w_ada, b_ada=b_ada, norm1_g=norm1_g, norm2_g=norm2_g, w_in=w_in, b_if=b_if, conv_w=conv_w,
             conv_b=conv_b, q_norm_g=q_norm_g, k_norm_g=k_norm_g, rel_bias=rel_bias, mlstm_norm_g=mlstm_norm_g,
             w_att_out=w_att_out, w_ml_out=w_ml_out, w_out=w_out, w_ff1=w_ff1, w_ff2=w_ff2)
    M = dict(w_ada=m_w_ada, b_ada=m_b_ada, norm1_g=m_norm1_g, norm2_g=m_norm2_g, w_in=m_w_in, b_if=m_b_if,
             conv_w=m_conv_w, conv_b=m_conv_b, q_norm_g=m_q_norm_g, k_norm_g=m_k_norm_g, rel_bias=m_rel_bias,
             mlstm_norm_g=m_mlstm_norm_g, w_att_out=m_w_att_out, w_ml_out=m_w_ml_out, w_out=m_w_out,
             w_ff1=m_w_ff1, w_ff2=m_w_ff2)
    V = dict(w_ada=v_w_ada, b_ada=v_b_ada, norm1_g=v_norm1_g, norm2_g=v_norm2_g, w_in=v_w_in, b_if=v_b_if,
             conv_w=v_conv_w, conv_b=v_conv_b, q_norm_g=v_q_norm_g, k_norm_g=v_k_norm_g, rel_bias=v_rel_bias,
             mlstm_norm_g=v_mlstm_norm_g, w_att_out=v_w_att_out, w_ml_out=v_w_ml_out, w_out=v_w_out,
             w_ff1=v_w_ff1, w_ff2=v_w_ff2)
    names = list(P)
    shapes = {k: P[k].shape for k in names}
    B = x.shape[0]
    me = _index(_mesh_pos())

    big_names = ("w_in", "w_att_out", "w_ml_out", "w_out", "w_ff1", "w_ff2")
    shards = cast_bf16([P[k][0] for k in big_names], "cast_w")
    w_in_g8, c8, conv_w8 = all_gather_two_level([shards[0], c, conv_w[0]], "gather_w_in")
    c_all = c8.reshape(N_DEV * B, D_MODEL)
    conv_w_full = conv_w8.transpose(1, 0, 2).reshape(4, 1024)
    w_att, w_ml, w_gate = split_w_in(w_in_g8.transpose(1, 0, 2).reshape(D_MODEL, D_IN))
    w = dict(w_att=w_att, w_ml=w_ml, w_gate=w_gate)

    (silu_c,) = small_call(lambda a: (_silu(a),), [c_all], [jax.ShapeDtypeStruct(c_all.shape, F32)], "silu_c")
    b_ada_cols = lax.dynamic_slice(b_ada, (0, me * 768), (1, 768))
    ada_cols = matmul(silu_c, w_ada[0], mode="nn", name="ada", extras=(jnp.broadcast_to(b_ada_cols, (N_DEV * B, 768)),),
                      epi=lambda acc, bb: (acc + bb,))
    (ada_t,) = all_to_all([ada_cols.reshape(N_DEV, B, 768)], "ada_exchange")
    ada = ada_t.transpose(1, 0, 2).reshape(B, 6 * D_MODEL)
    mods = tuple(ada[:, i * D_MODEL:(i + 1) * D_MODEL].reshape(B, 1, D_MODEL) for i in range(6))

    late_handle, late_order = exchange_start(shards[1:], "gather_late_start", False, after=(ada_t,))

    def late_w(after):
        lands = exchange_wait(late_handle, after, "gather_late_wait", False)
        gw = dict(zip(big_names[1:], [_own_slot(l, s) for l, s in zip(lands, shards[1:])]))
        return dict(w_att_out=gw["w_att_out"].transpose(1, 0, 2).reshape(512, D_MODEL),
                    w_ml_out=gw["w_ml_out"].reshape(D_MODEL, D_MODEL), w_out=gw["w_out"].reshape(D_MODEL, D_MODEL),
                    w_ff1=gw["w_ff1"].transpose(1, 0, 2).reshape(D_MODEL, D_FF),
                    w_ff2=gw["w_ff2"].reshape(D_FF, D_MODEL))

    pending = {}

    def send_grads(key, blocks, name):
        handle, order = exchange_start(blocks, name, True)
        pending[key] = (handle, [lax.dynamic_index_in_dim(b, me, 0, keepdims=False) for b in blocks])
        return order

    def early_g(g):
        return send_grads("late", [g["w_att_out"].reshape(512, N_DEV, 128).transpose(1, 0, 2),
                                   g["w_ml_out"].reshape(N_DEV, 128, D_MODEL), g["w_out"].reshape(N_DEV, 128, D_MODEL),
                                   g["w_ff1"].reshape(D_MODEL, N_DEV, 512).transpose(1, 0, 2),
                                   g["w_ff2"].reshape(N_DEV, 512, D_MODEL)], "grad_late_start")

    def w_in_g(g):
        return send_grads("w_in", [g.reshape(D_MODEL, N_DEV, W_IN_SHARD).transpose(1, 0, 2)], "grad_w_in_start")

    def recv_grads(key, after, name):
        handle, own = pending[key]
        return [_own_slot(l, o) for l, o in zip(exchange_wait(handle, after, name, True), own)]

    small = dict(norm1_g=norm1_g + late_order[0, 0], norm2_g=norm2_g, b_if=b_if[0], conv_w=conv_w_full, conv_b=conv_b,
                 q_norm_g=q_norm_g, k_norm_g=k_norm_g, rel_bias=rel_bias, mlstm_norm_g=mlstm_norm_g)
    loss, grad_x, d_mods, _, small_g = local_step(x, loss_target, mods, w, small, late_w, early_g, w_in_g)
    loss = lax.psum(loss[0, 0], ("x", "y", "c"))

    d_ada = jnp.concatenate([d.reshape(B, D_MODEL) for d in d_mods], axis=1)
    (d_ada_t,) = all_to_all([d_ada.reshape(B, N_DEV, 768).transpose(1, 0, 2)], "d_ada_exchange")
    d_ada_cols = d_ada_t.reshape(N_DEV * B, 768)
    g_w_ada = matmul(silu_c, d_ada_cols, mode="tn", name="g_w_ada")
    (g_b_cols,) = small_call(lambda a: (jnp.sum(a, axis=0, keepdims=True),), [d_ada_cols],
                             [jax.ShapeDtypeStruct((1, 768), F32)], "g_b_ada_cols")
    small_g["b_ada"] = lax.dynamic_update_slice(jnp.zeros((1, 6144), F32), g_b_cols, (0, me * 768))

    recv = recv_grads("w_in", grad_x, "grad_w_in_wait") + recv_grads("late", grad_x, "grad_late_wait")
    (small_parts,) = all_gather([_pack(small_g, _SMALL, _SMALL_ROWS)], "small_grad_gather")
    small_sum = sum_parts(small_parts, "small_grad_sum")
    sg = _unpack(small_sum, _SMALL, {**{k: shapes[k] for k, _ in _REPL}, "conv_w": (4, 1024)})

    G, Dl, NM, NV = {}, {}, {}, {}
    for k, parts in zip(big_names, recv):
        g, d, nm, nv = adamw(P[k][0], parts, M[k][0], V[k][0], f"adamw_{k}", parts=True)
        G[k], Dl[k], NM[k], NV[k] = g[None], d[None], nm[None], nv[None]
    g, d, nm, nv = adamw(w_ada[0], g_w_ada, m_w_ada[0], v_w_ada[0], "adamw_w_ada")
    G["w_ada"], Dl["w_ada"], NM["w_ada"], NV["w_ada"] = g[None], d[None], nm[None], nv[None]
    g_conv = lax.dynamic_slice(sg["conv_w"], (0, me * 128), (4, 128))
    g, d, nm, nv = adamw(conv_w[0], g_conv, m_conv_w[0], v_conv_w[0], "adamw_conv_w")
    G["conv_w"], Dl["conv_w"], NM["conv_w"], NV["conv_w"] = g[None], d[None], nm[None], nv[None]
    flat2 = lambda a: a.reshape(-1, a.shape[-1])
    keys = [k for k, _ in _REPL]
    upd = adamw_many([flat2(P[k]) for k in keys], [flat2(sg[k]) for k in keys], [flat2(M[k]) for k in keys],
                     [flat2(V[k]) for k in keys], "adamw_small")
    for j, k in enumerate(keys):
        G[k] = sg[k]
        Dl[k], NM[k], NV[k] = [upd[3 * j + t].reshape(shapes[k]) for t in range(3)]

    return (loss, grad_x, *[G[k] for k in names], *[Dl[k] for k in names], *[NM[k] for k in names],
            *[NV[k] for k in names])
```

```python
import functools
import math

import numpy as np
import jax
import jax.numpy as jnp
from jax import lax
from jax.experimental import pallas as pl
from jax.experimental.pallas import tpu as pltpu

F32 = jnp.float32
BF16 = jnp.bfloat16

N_DEV = 8
D_MODEL = 1024
SEQ = 2048
ATT_GROUPS = ((128, 1), (512, 4), (2048, 16))
N_ATT_HEADS = 12
ATT_BLOCK = 128
HEAD_DIM = 128
ML_HEADS = 8
ML_PAIRS = 4
ML_CHUNK = 64
N_CHUNKS = SEQ // ML_CHUNK
N_BUCKETS = 32
MAX_DISTANCE = 2048
D_FF = 4096
D_IN = 9744
EPS = 1e-6

ADAM_LR = 0.001
ADAM_B1 = 0.9
ADAM_B2 = 0.999
ADAM_EPS = 1e-08
ADAM_WD = 0.01
ADAM_STEP = 10

ATT_HEAD_COLS = 3 * HEAD_DIM
ATT_COLS = N_ATT_HEADS * ATT_HEAD_COLS
ML_PAIR_COLS = 896
ML_COLS = ML_PAIRS * ML_PAIR_COLS
GATE_COLS = 2 * D_MODEL
W_IN_SHARD = D_IN // N_DEV

VMEM_LIMIT = 60 * 1024 * 1024


def _cparams(**kw):
    return pltpu.CompilerParams(vmem_limit_bytes=VMEM_LIMIT, **kw)


_NN = ((1,), (0,))
_NT = ((1,), (1,))
_TN = ((0,), (0,))


def _mxu(a, b, dims):
    return lax.dot_general(a.astype(BF16), b.astype(BF16), (dims, ((), ())), preferred_element_type=F32)


@jax.custom_vjp
def bdot_nn(a, b):
    return _mxu(a, b, _NN)


def _nn_fwd(a, b):
    return _mxu(a, b, _NN), (a, b)


def _nn_bwd(res, g):
    a, b = res
    return _mxu(g, b, _NT), _mxu(a, g, _TN)


bdot_nn.defvjp(_nn_fwd, _nn_bwd)


@jax.custom_vjp
def bdot_nt(a, b):
    return _mxu(a, b, _NT)


def _nt_fwd(a, b):
    return _mxu(a, b, _NT), (a, b)


def _nt_bwd(res, g):
    a, b = res
    return _mxu(g, b, _NN), _mxu(g, a, _TN)


bdot_nt.defvjp(_nt_fwd, _nt_bwd)


def _doth(a, b, dims=_NN):
    return lax.dot_general(a, b, (dims, ((), ())), precision=lax.Precision.HIGHEST, preferred_element_type=F32)


def _rms(x):
    return x * lax.rsqrt(jnp.mean(x * x, axis=-1, keepdims=True) + EPS)


def _pick(n, cands):
    for t in cands:
        if n % t == 0:
            return t
    raise ValueError(f"no tile for {n}")


MM_TILE_M = (1024, 512, 256, 128, 64, 32, 16, 8)
MM_TILE_N = (2048, 1792, 1536, 1024, 768, 512, 256, 128)
MM_TILE_K = (2048, 1792, 1536, 1024, 512, 256, 128, 64, 32)

def matmul(a, b, *, mode, name, out_dtypes=(F32,), epi=None, extras=(), after=()):
    if mode == "nn":
        (M, K), (K2, N) = a.shape, b.shape
    elif mode == "nt":
        (M, K), (N, K2) = a.shape, b.shape
    else:
        (K, M), (K2, N) = a.shape, b.shape
    assert K == K2, (a.shape, b.shape, mode)
    tm = _pick(M, MM_TILE_M)
    tn = _pick(N, MM_TILE_N)
    tk = _pick(K, MM_TILE_K)
    nk = K // tk
    n_ex = len(extras)
    n_out = len(out_dtypes)
    dims = {"nn": _NN, "nt": _NT, "tn": _TN}[mode]

    def finish(r, ex_refs, out_refs):
        outs = epi(r, *[e[...] for e in ex_refs]) if epi is not None else (r,)
        for o_ref, o in zip(out_refs, outs):
            o_ref[...] = o.astype(o_ref.dtype)

    def body(*refs):
        a_ref, b_ref = refs[0], refs[1]
        ex_refs = refs[2:2 + n_ex]
        out_refs = refs[2 + n_ex + len(after):2 + n_ex + len(after) + n_out]
        if nk == 1:
            finish(_mxu(a_ref[...], b_ref[...], dims), ex_refs, out_refs)
            return
        acc = refs[2 + n_ex + len(after) + n_out]
        k = pl.program_id(2)

        @pl.when(k == 0)
        def _():
            acc[...] = jnp.zeros_like(acc)

        acc[...] += _mxu(a_ref[...], b_ref[...], dims)

        @pl.when(k == nk - 1)
        def _():
            finish(acc[...], ex_refs, out_refs)

    if mode == "nn":
        a_spec = pl.BlockSpec((tm, tk), lambda i, j, k: (i, k))
        b_spec = pl.BlockSpec((tk, tn), lambda i, j, k: (k, j))
    elif mode == "nt":
        a_spec = pl.BlockSpec((tm, tk), lambda i, j, k: (i, k))
        b_spec = pl.BlockSpec((tn, tk), lambda i, j, k: (j, k))
    else:
        a_spec = pl.BlockSpec((tk, tm), lambda i, j, k: (k, i))
        b_spec = pl.BlockSpec((tk, tn), lambda i, j, k: (k, j))
    o_spec = pl.BlockSpec((tm, tn), lambda i, j, k: (i, j))
    res = pl.pallas_call(
        body,
        name=name,
        grid=(M // tm, N // tn, nk),
        in_specs=[a_spec, b_spec] + [o_spec] * n_ex + [pl.BlockSpec(memory_space=pl.ANY)] * len(after),
        out_specs=[o_spec] * n_out,
        out_shape=[jax.ShapeDtypeStruct((M, N), dt) for dt in out_dtypes],
        scratch_shapes=[pltpu.VMEM((tm, tn), F32)] if nk > 1 else [],
        compiler_params=_cparams(),
    )(a, b, *extras, *after)
    return res[0] if n_out == 1 else tuple(res)


ROW_MM_TILE = 512


def row_matmul(a, b, *, mode, name, extras, outs, epi):
    (M, K) = a.shape
    N = b.shape[1] if mode == "nn" else b.shape[0]
    tm = ROW_MM_TILE
    tk = _pick(K, MM_TILE_K)
    nk = K // tk
    n_ex, n_out = len(extras), len(outs)

    def body(*refs):
        a_ref, b_ref = refs[0], refs[1]
        ex_refs = refs[2:2 + n_ex]
        out_refs = refs[2 + n_ex:2 + n_ex + n_out]
        i = pl.program_id(0)
        dims = _NN if mode == "nn" else _NT
        if nk == 1:
            epi(_mxu(a_ref[...], b_ref[...], dims), i, ex_refs, out_refs)
            return
        acc = refs[2 + n_ex + n_out]
        k = pl.program_id(1)

        @pl.when(k == 0)
        def _():
            acc[...] = jnp.zeros_like(acc)

        acc[...] += _mxu(a_ref[...], b_ref[...], dims)

        @pl.when(k == nk - 1)
        def _():
            epi(acc[...], i, ex_refs, out_refs)

    def lift(index_map):
        return lambda i, k: index_map(i)

    b_spec = pl.BlockSpec((tk, N), lambda i, k: (k, 0)) if mode == "nn" else pl.BlockSpec((N, tk), lambda i, k: (0, k))
    res = pl.pallas_call(
        body, name=name, grid=(M // tm, nk),
        in_specs=[pl.BlockSpec((tm, tk), lambda i, k: (i, k)), b_spec]
        + [pl.BlockSpec(blk, lift(im)) for _, blk, im in extras],
        out_specs=[pl.BlockSpec(blk, lift(im)) for _, _, blk, im in outs],
        out_shape=[jax.ShapeDtypeStruct(shape, dt) for shape, dt, _, _ in outs],
        scratch_shapes=[pltpu.VMEM((tm, N), F32)] if nk > 1 else [],
        compiler_params=_cparams(),
    )(a, b, *[e[0] for e in extras])
    return tuple(res)


def _rows(arr):
    return (arr, (ROW_MM_TILE, arr.shape[1]), lambda i: (i, 0))


def _rows_out(T, dtype):
    return ((T, D_MODEL), dtype, (ROW_MM_TILE, D_MODEL), lambda i: (i, 0))


def _per_seq(arr):
    return (arr, (1, 1, D_MODEL), lambda i: (i // (SEQ // ROW_MM_TILE), 0, 0))


def _per_seq_out(B):
    return ((B, 1, D_MODEL), F32, (1, 1, D_MODEL), lambda i: (i // (SEQ // ROW_MM_TILE), 0, 0))


def _first_tile_of_seq(i):
    return i % (SEQ // ROW_MM_TILE) == 0


def small_call(fn, inputs, out_shapes, name):
    n_in = len(inputs)

    def body(*refs):
        outs = fn(*[r[...] for r in refs[:n_in]])
        for o_ref, o in zip(refs[n_in:], outs):
            o_ref[...] = o.astype(o_ref.dtype)

    res = pl.pallas_call(body, name=name, out_shape=list(out_shapes), compiler_params=_cparams())(*inputs)
    return tuple(res)


ROW_TILE = 512


def _modnorm(x, g, scale, shift):
    return _rms(x) * g * (1.0 + scale) + shift


def _row_spec(width):
    return pl.BlockSpec((1, ROW_TILE, width), lambda b, i: (b, i, 0))


def _mod_spec():
    return pl.BlockSpec((1, 1, D_MODEL), lambda b, i: (b, 0, 0))


def _vec_spec():
    return pl.BlockSpec((1, D_MODEL), lambda b, i: (0, 0))


def modnorm_fwd(x, g, scale, shift, name):
    B, S, D = x.shape

    def body(x_ref, g_ref, sc_ref, sh_ref, u_ref):
        u_ref[0] = _modnorm(x_ref[0], g_ref[...], sc_ref[0], sh_ref[0]).astype(BF16)

    return pl.pallas_call(
        body, name=name, grid=(B, S // ROW_TILE),
        in_specs=[_row_spec(D), _vec_spec(), _mod_spec(), _mod_spec()],
        out_specs=_row_spec(D),
        out_shape=jax.ShapeDtypeStruct((B, S, D), BF16),
        compiler_params=_cparams(),
    )(x, g, scale, shift)


def _gain_spec(g):
    return (g, (1, D_MODEL), lambda i: (0, 0))


def out_proj_resid_modnorm(z, w_out, x, gate, g, scale, shift, name):
    T = z.shape[0]

    def epi(acc, i, ex, out):
        x_ref, gt_ref, g_ref, sc_ref, sh_ref = ex
        y_ref, x1_ref, u_ref = out
        y_ref[...] = acc
        x1 = x_ref[...] + gt_ref[0] * acc
        x1_ref[...] = x1
        u_ref[...] = _modnorm(x1, g_ref[...], sc_ref[0], sh_ref[0]).astype(BF16)

    return row_matmul(z, w_out, mode="nn", name=name,
                      extras=[_rows(x), _per_seq(gate), _gain_spec(g), _per_seq(scale), _per_seq(shift)],
                      outs=[_rows_out(T, F32), _rows_out(T, F32), _rows_out(T, BF16)], epi=epi)


def ff2_loss(hdn, w_ff2, x1, gate, target, name):
    T = hdn.shape[0]

    def epi(acc, i, ex, out):
        x_ref, gt_ref, t_ref = ex
        dx_ref, dffo_ref, loss_ref, dg_ref = out

        @pl.when(i == 0)
        def _():
            loss_ref[...] = jnp.zeros_like(loss_ref)

        @pl.when(_first_tile_of_seq(i))
        def _():
            dg_ref[...] = jnp.zeros_like(dg_ref)

        err = x_ref[...] + gt_ref[0] * acc - t_ref[...]
        dx = err * (1.0 / D_MODEL)
        dx_ref[...] = dx
        dffo_ref[...] = (gt_ref[0] * dx).astype(BF16)
        loss_ref[...] += 0.5 * jnp.sum(jnp.mean(err * err, axis=-1, keepdims=True), axis=0, keepdims=True)
        dg_ref[0] += jnp.sum(dx * acc, axis=0, keepdims=True)

    return row_matmul(hdn, w_ff2, mode="nn", name=name,
                      extras=[_rows(x1), _per_seq(gate), _rows(target)],
                      outs=[_rows_out(T, F32), _rows_out(T, BF16), ((1, 1), F32, (1, 1), lambda i: (0, 0)),
                            _per_seq_out(T // SEQ)], epi=epi)


def d_u_modnorm_bwd(a, w, du_prev, x, g, scale, shift, dx_res, y, gate, name):
    T = a.shape[0]
    B = T // SEQ
    n_prev, resid = int(du_prev is not None), y is not None

    def epi(acc, i, ex, out):
        x_ref, g_ref, sc_ref, sh_ref, dr_ref = ex[n_prev:n_prev + 5]
        dx_ref, dg_ref, dsc_ref, dsh_ref = out[:4]

        @pl.when(i == 0)
        def _():
            dg_ref[...] = jnp.zeros_like(dg_ref)

        @pl.when(_first_tile_of_seq(i))
        def _():
            for r in out[2:4] + out[5:]:
                r[...] = jnp.zeros_like(r)

        du = acc + ex[0][...] if n_prev else acc
        _, vjp = jax.vjp(_modnorm, x_ref[...], g_ref[...], sc_ref[0], sh_ref[0])
        dxn, dg, dsc, dsh = vjp(du)
        dx = dxn + dr_ref[...]
        dx_ref[...] = dx
        dg_ref[...] += dg
        dsc_ref[0] += dsc
        dsh_ref[0] += dsh
        if resid:
            y_ref, gt_ref = ex[n_prev + 5:]
            out[4][...] = (gt_ref[0] * dx).astype(BF16)
            out[5][0] += jnp.sum(dx * y_ref[...], axis=0, keepdims=True)

    extras = ([_rows(du_prev)] if n_prev else []) + [_rows(x), _gain_spec(g), _per_seq(scale), _per_seq(shift),
                                                     _rows(dx_res)] + ([_rows(y), _per_seq(gate)] if resid else [])
    outs = [_rows_out(T, F32), ((1, D_MODEL), F32, (1, D_MODEL), lambda i: (0, 0)), _per_seq_out(B), _per_seq_out(B)]
    outs += [_rows_out(T, BF16), _per_seq_out(B)] if resid else []
    return row_matmul(a, w, mode="nt", name=name, extras=extras, outs=outs, epi=epi)


def _bucket_table(dilation):
    i = np.arange(ATT_BLOCK)[:, None]
    j = np.arange(2 * ATT_BLOCK)[None, :]
    delta = ATT_BLOCK + i - j
    dist = np.maximum(delta, 0) * dilation
    max_exact = N_BUCKETS // 2
    d = np.maximum(dist, max_exact).astype(np.float32)
    large = max_exact + (np.log(d / np.float32(max_exact)) / np.float32(math.log(MAX_DISTANCE / max_exact))
                         * np.float32(N_BUCKETS - max_exact)).astype(np.int32)
    large = np.minimum(large, N_BUCKETS - 1)
    return np.where(dist < max_exact, dist, large).astype(np.int32)


def _bucket_onehot(dilation):
    bt = jnp.asarray(_bucket_table(dilation).reshape(1, -1))
    return (bt == jnp.arange(N_BUCKETS, dtype=jnp.int32)[:, None]).astype(F32)


def bias_expand(rel_t, onehot, name):
    def fn(r, oh):
        return (_doth(r, oh),)
    return small_call(fn, [rel_t, onehot], [jax.ShapeDtypeStruct((rel_t.shape[0], onehot.shape[1]), F32)], name)[0]


def bias_reduce(dbias_flat, onehot, name):
    def fn(db, oh):
        return (_doth(db, oh, _NT),)
    return small_call(fn, [dbias_flat, onehot], [jax.ShapeDtypeStruct((dbias_flat.shape[0], N_BUCKETS), F32)], name)[0]


def _qk_norm(x, g):
    return _rms(x) * g


def _masked_bias(bias):
    i = lax.broadcasted_iota(jnp.int32, (ATT_BLOCK, 2 * ATT_BLOCK), 0)
    j = lax.broadcasted_iota(jnp.int32, (ATT_BLOCK, 2 * ATT_BLOCK), 1)
    bm = jnp.where(jnp.logical_and(j >= i, j <= i + ATT_BLOCK), bias, -jnp.inf)
    return bm, bm[:, ATT_BLOCK:]


def _attn_tile(qn, kn, v, bias):
    s = bdot_nt(qn, kn) * (HEAD_DIM ** -0.5) + bias
    m = lax.stop_gradient(jnp.max(s, axis=-1, keepdims=True))
    p = jnp.exp(s - m)
    l = jnp.sum(p, axis=-1, keepdims=True)
    o = bdot_nn(p, v) / l
    lse = jnp.broadcast_to(m + jnp.log(l), (ATT_BLOCK, HEAD_DIM))
    return o, lse


def _attn_tiles(dilation, rows=SEQ):
    nb = rows // dilation // ATT_BLOCK
    return [(r, n) for r in range(dilation) for n in range(nb)]


def _attn_rows(r, n, dilation, nblk=1):
    if dilation == 1:
        return pl.ds(r + n * ATT_BLOCK, nblk * ATT_BLOCK)
    return pl.ds(r + n * ATT_BLOCK * dilation, nblk * ATT_BLOCK, stride=dilation)


_QL, _KL, _VL = slice(0, 128), slice(128, 256), slice(256, 384)


def _qkv_specs(hb):
    return [pl.BlockSpec((None, SEQ, HEAD_DIM), functools.partial(lambda b, h, j: (b, 0, 3 * (hb + h) + j), j=j))
            for j in range(3)]


def attn_fwd(pa, bias, qg, kg, group, name):
    B = pa.shape[0]
    dilation = ATT_GROUPS[group][1]
    hb = group * 4

    def body(q_ref, k_ref, v_ref, b_ref, qg_ref, kg_ref, o_ref, l_ref, qn_s, kn_s):
        qn_s[...] = _qk_norm(q_ref[...], qg_ref[...])
        kn_s[...] = _qk_norm(k_ref[...], kg_ref[...])
        bias_all, bias_first = _masked_bias(b_ref[0])
        for (r, n) in _attn_tiles(dilation):
            rows = _attn_rows(r, n, dilation)
            if n == 0:
                krows, bias_t = rows, bias_first
            else:
                krows, bias_t = _attn_rows(r, n - 1, dilation, 2), bias_all
            o, lse = _attn_tile(qn_s[rows, :], kn_s[krows, :], v_ref[krows, :], bias_t)
            o_ref[rows, :] = o
            l_ref[rows, :] = lse

    head_out = pl.BlockSpec((None, SEQ, HEAD_DIM), lambda b, h: (b, 0, h))
    return pl.pallas_call(
        body, name=name, grid=(B, 4),
        in_specs=_qkv_specs(hb) + [
                  pl.BlockSpec((1, ATT_BLOCK, 2 * ATT_BLOCK), lambda b, h: (hb + h, 0, 0)),
                  pl.BlockSpec((1, HEAD_DIM), lambda b, h: (0, 0)),
                  pl.BlockSpec((1, HEAD_DIM), lambda b, h: (0, 0))],
        out_specs=[head_out, head_out],
        out_shape=[jax.ShapeDtypeStruct((B, SEQ, 512), F32), jax.ShapeDtypeStruct((B, SEQ, 512), F32)],
        scratch_shapes=[pltpu.VMEM((SEQ, HEAD_DIM), F32)] * 2,
        compiler_params=_cparams(),
    )(pa, pa, pa, bias, qg, kg)


def attn_bwd(pa, bias, qg, kg, do, dlse, dpa, group, name):
    B = pa.shape[0]
    dilation = ATT_GROUPS[group][1]
    hb = group * 4

    def body(q_ref, k_ref, v_ref, b_ref, qg_ref, kg_ref, do_ref, dl_ref, dpa_in,
             dp_ref, db_ref, dqg_ref, dkg_ref, qn_s, kn_s, dq_s, dk_s, dv_s):
        del dpa_in
        h_id = pl.program_id(1)

        @pl.when(jnp.logical_and(pl.program_id(0) == 0, h_id == 0))
        def _():
            db_ref[...] = jnp.zeros_like(db_ref)
            dqg_ref[...] = jnp.zeros_like(dqg_ref)
            dkg_ref[...] = jnp.zeros_like(dkg_ref)

        dk_s[...] = jnp.zeros_like(dk_s)
        dv_s[...] = jnp.zeros_like(dv_s)
        qn_s[...] = _qk_norm(q_ref[...], qg_ref[...])
        kn_s[...] = _qk_norm(k_ref[...], kg_ref[...])
        bias_all, bias_first = _masked_bias(b_ref[0])
        for (r, n) in _attn_tiles(dilation):
            rows = _attn_rows(r, n, dilation)
            if n == 0:
                krows, bias_t = rows, bias_first
            else:
                krows, bias_t = _attn_rows(r, n - 1, dilation, 2), bias_all
            _, vjp = jax.vjp(_attn_tile, qn_s[rows, :], kn_s[krows, :], v_ref[krows, :], bias_t)
            dqn, dkn, dv, dbias = vjp((do_ref[rows, :], dl_ref[rows, :]))
            dq_s[rows, :] = dqn
            dk_s[krows, :] += dkn
            dv_s[krows, :] += dv
            if n == 0:
                db_ref[h_id, :, ATT_BLOCK:] += dbias
            else:
                db_ref[h_id] += dbias
        for x_ref, g_ref, d_s, dg_ref, lanes in ((q_ref, qg_ref, dq_s, dqg_ref, _QL), (k_ref, kg_ref, dk_s, dkg_ref, _KL)):
            _, vjp = jax.vjp(_qk_norm, x_ref[...], g_ref[...])
            dx, dg = vjp(d_s[...])
            dp_ref[0, :, lanes] = dx.astype(BF16)
            dg_ref[...] += dg
        dp_ref[0, :, _VL] = dv_s[...].astype(BF16)

    const2 = lambda b, h: (0, 0)
    head_in = pl.BlockSpec((None, SEQ, HEAD_DIM), lambda b, h: (b, 0, h))
    head_blk = pl.BlockSpec((1, SEQ, ATT_HEAD_COLS), lambda b, h: (b, 0, hb + h))
    return pl.pallas_call(
        body, name=name, grid=(B, 4),
        in_specs=_qkv_specs(hb) + [
                  pl.BlockSpec((1, ATT_BLOCK, 2 * ATT_BLOCK), lambda b, h: (hb + h, 0, 0)),
                  pl.BlockSpec((1, HEAD_DIM), const2), pl.BlockSpec((1, HEAD_DIM), const2),
                  head_in, head_in,
                  pl.BlockSpec(memory_space=pl.ANY)],
        out_specs=[head_blk,
                   pl.BlockSpec((4, ATT_BLOCK, 2 * ATT_BLOCK), lambda b, h: (0, 0, 0)),
                   pl.BlockSpec((1, HEAD_DIM), const2), pl.BlockSpec((1, HEAD_DIM), const2)],
        out_shape=[jax.ShapeDtypeStruct(dpa.shape, BF16),
                   jax.ShapeDtypeStruct((4, ATT_BLOCK, 2 * ATT_BLOCK), F32),
                   jax.ShapeDtypeStruct((1, HEAD_DIM), F32), jax.ShapeDtypeStruct((1, HEAD_DIM), F32)],
        scratch_shapes=[pltpu.VMEM((SEQ, HEAD_DIM), F32)] * 5,
        input_output_aliases={8: 0},
        compiler_params=_cparams(),
    )(pa, pa, pa, bias, qg, kg, do, dlse, dpa)


def _attn_classes(q, k, v, bias, qg, kg):
    s = cdot_nt(_qk_norm(q, qg), _qk_norm(k, kg)) * (HEAD_DIM ** -0.5) + bias
    m = lax.stop_gradient(jnp.max(s, axis=-1, keepdims=True))
    p = jnp.exp(s - m)
    l = jnp.sum(p, axis=-1, keepdims=True)
    o = cdot_nn(p, v) / l
    return o, jnp.broadcast_to(m + jnp.log(l), o.shape)


def _gather_classes(src_ref, dst_s, dilation):
    for r in range(dilation):
        dst_s[r] = src_ref[pl.ds(r, ATT_BLOCK, stride=dilation), :]


def _scatter_classes(src_s, dst_ref, dilation):
    for r in range(dilation):
        dst_ref[pl.ds(r, ATT_BLOCK, stride=dilation), :] = src_s[r]


def attn_fwd_classes(pa, bias, qg, kg, group, name):
    B = pa.shape[0]
    dilation = ATT_GROUPS[group][1]
    hb = group * 4

    def body(q_ref, k_ref, v_ref, b_ref, qg_ref, kg_ref, o_ref, l_ref, q_s, k_s, v_s):
        _gather_classes(q_ref, q_s, dilation)
        _gather_classes(k_ref, k_s, dilation)
        _gather_classes(v_ref, v_s, dilation)
        o, lse = _attn_classes(q_s[...], k_s[...], v_s[...], _masked_bias(b_ref[0])[1], qg_ref[...], kg_ref[...])
        q_s[...], k_s[...] = o, lse
        _scatter_classes(q_s, o_ref, dilation)
        _scatter_classes(k_s, l_ref, dilation)

    head_out = pl.BlockSpec((None, SEQ, HEAD_DIM), lambda b, h: (b, 0, h))
    return pl.pallas_call(
        body, name=name, grid=(B, 4),
        in_specs=_qkv_specs(hb) + [
                  pl.BlockSpec((1, ATT_BLOCK, 2 * ATT_BLOCK), lambda b, h: (hb + h, 0, 0)),
                  pl.BlockSpec((1, HEAD_DIM), lambda b, h: (0, 0)),
                  pl.BlockSpec((1, HEAD_DIM), lambda b, h: (0, 0))],
        out_specs=[head_out, head_out],
        out_shape=[jax.ShapeDtypeStruct((B, SEQ, 512), F32), jax.ShapeDtypeStruct((B, SEQ, 512), F32)],
        scratch_shapes=[pltpu.VMEM((dilation, ATT_BLOCK, HEAD_DIM), F32)] * 3,
        compiler_params=_cparams(),
    )(pa, pa, pa, bias, qg, kg)


def attn_bwd_classes(pa, bias, qg, kg, do, dlse, dpa, group, name):
    B = pa.shape[0]
    dilation = ATT_GROUPS[group][1]
    hb = group * 4

    def body(q_ref, k_ref, v_ref, b_ref, qg_ref, kg_ref, do_ref, dl_ref, dpa_in,
             dp_ref, db_ref, dqg_ref, dkg_ref, q_s, k_s, v_s, do_s, dl_s, rows_s):
        del dpa_in
        h_id = pl.program_id(1)

        @pl.when(jnp.logical_and(pl.program_id(0) == 0, h_id == 0))
        def _():
            db_ref[...] = jnp.zeros_like(db_ref)
            dqg_ref[...] = jnp.zeros_like(dqg_ref)
            dkg_ref[...] = jnp.zeros_like(dkg_ref)

        for src, dst in ((q_ref, q_s), (k_ref, k_s), (v_ref, v_s), (do_ref, do_s), (dl_ref, dl_s)):
            _gather_classes(src, dst, dilation)
        _, vjp = jax.vjp(_attn_classes, q_s[...], k_s[...], v_s[...], _masked_bias(b_ref[0])[1],
                         qg_ref[...], kg_ref[...])
        dq, dk, dv, dbias, dqg, dkg = vjp((do_s[...], dl_s[...]))
        db_ref[h_id, :, ATT_BLOCK:] += dbias
        dqg_ref[...] += dqg
        dkg_ref[...] += dkg
        for d, lanes in ((dq, _QL), (dk, _KL), (dv, _VL)):
            q_s[...] = d
            _scatter_classes(q_s, rows_s, dilation)
            dp_ref[0, :, lanes] = rows_s[...].astype(BF16)

    const2 = lambda b, h: (0, 0)
    head_in = pl.BlockSpec((None, SEQ, HEAD_DIM), lambda b, h: (b, 0, h))
    head_blk = pl.BlockSpec((1, SEQ, ATT_HEAD_COLS), lambda b, h: (b, 0, hb + h))
    return pl.pallas_call(
        body, name=name, grid=(B, 4),
        in_specs=_qkv_specs(hb) + [
                  pl.BlockSpec((1, ATT_BLOCK, 2 * ATT_BLOCK), lambda b, h: (hb + h, 0, 0)),
                  pl.BlockSpec((1, HEAD_DIM), const2), pl.BlockSpec((1, HEAD_DIM), const2),
                  head_in, head_in,
                  pl.BlockSpec(memory_space=pl.ANY)],
        out_specs=[head_blk,
                   pl.BlockSpec((4, ATT_BLOCK, 2 * ATT_BLOCK), lambda b, h: (0, 0, 0)),
                   pl.BlockSpec((1, HEAD_DIM), const2), pl.BlockSpec((1, HEAD_DIM), const2)],
        out_shape=[jax.ShapeDtypeStruct(dpa.shape, BF16),
                   jax.ShapeDtypeStruct((4, ATT_BLOCK, 2 * ATT_BLOCK), F32),
                   jax.ShapeDtypeStruct((1, HEAD_DIM), F32), jax.ShapeDtypeStruct((1, HEAD_DIM), F32)],
        scratch_shapes=[pltpu.VMEM((dilation, ATT_BLOCK, HEAD_DIM), F32)] * 5 + [pltpu.VMEM((SEQ, HEAD_DIM), F32)],
        input_output_aliases={8: 0},
        compiler_params=_cparams(),
    )(pa, pa, pa, bias, qg, kg, do, dlse, dpa)


def _merge(o0, o1, o2, l0, l1, l2):
    mx = jnp.maximum(jnp.maximum(l0, l1), l2)
    e0, e1, e2 = jnp.exp(l0 - mx), jnp.exp(l1 - mx), jnp.exp(l2 - mx)
    den = e0 + e1 + e2
    return (e0 / den) * o0 + (e1 / den) * o1 + (e2 / den) * o2


def merge_att_out(os_, ls_, w_att_out, name):
    T = os_[0].shape[0]
    tile = pl.BlockSpec((ROW_MM_TILE, 512), lambda i: (i, 0))

    def body(o0, o1, o2, l0, l1, l2, w_ref, a_ref, y_ref):
        att = _merge(o0[...], o1[...], o2[...], l0[...], l1[...], l2[...]).astype(BF16)
        a_ref[...] = att
        y_ref[...] = _mxu(att, w_ref[...], _NN)

    return pl.pallas_call(
        body, name=name, grid=(T // ROW_MM_TILE,),
        in_specs=[tile] * 6 + [pl.BlockSpec((512, D_MODEL), lambda i: (0, 0))],
        out_specs=[tile, pl.BlockSpec((ROW_MM_TILE, D_MODEL), lambda i: (i, 0))],
        out_shape=[jax.ShapeDtypeStruct((T, 512), BF16), jax.ShapeDtypeStruct((T, D_MODEL), F32)],
        compiler_params=_cparams(),
    )(*os_, *ls_, w_att_out)


def d_att_merge_bwd(d_ya, w_att_out, os_, ls_, name):
    T = d_ya.shape[0]
    tile = lambda a: (a, (ROW_MM_TILE, 512), lambda i: (i, 0))

    def epi(acc, i, ex, out):
        _, vjp = jax.vjp(_merge, *[e[...] for e in ex])
        for o_ref, g in zip(out, vjp(acc)):
            o_ref[...] = g

    return row_matmul(d_ya, w_att_out, mode="nt", name=name, extras=[tile(a) for a in list(os_) + list(ls_)],
                      outs=[((T, 512), F32, (ROW_MM_TILE, 512), lambda i: (i, 0))] * 6, epi=epi)


def _gate_mix(ga, gm, ya, ym):
    return jax.nn.sigmoid(ga) * ya + jax.nn.sigmoid(gm) * ym


def _gate_halves(pg):
    return [(pg, (ROW_MM_TILE, D_MODEL), lambda i: (i, 0)), (pg, (ROW_MM_TILE, D_MODEL), lambda i: (i, 1))]


def ml_out_gate(hg, w_ml_out, pg, ya, name):
    T = hg.shape[0]

    def epi(acc, i, ex, out):
        ga, gm, ya_ref = ex
        out[0][...] = acc
        out[1][...] = _gate_mix(ga[...], gm[...], ya_ref[...], acc).astype(BF16)

    return row_matmul(hg, w_ml_out, mode="nn", name=name, extras=_gate_halves(pg) + [_rows(ya)],
                      outs=[_rows_out(T, F32), _rows_out(T, BF16)], epi=epi)


def d_z_gate_bwd(dy, w_out, pg, ya, ym, name):
    T = dy.shape[0]

    def epi(acc, i, ex, out):
        ga, gm, ya_ref, ym_ref = ex
        dpg_ref, dya_ref, dym_ref = out
        _, vjp = jax.vjp(_gate_mix, ga[...], gm[...], ya_ref[...], ym_ref[...])
        dga, dgm, dya, dym = vjp(acc)
        dpg_ref[:, :D_MODEL] = dga.astype(BF16)
        dpg_ref[:, D_MODEL:] = dgm.astype(BF16)
        dya_ref[...] = dya.astype(BF16)
        dym_ref[...] = dym.astype(BF16)

    return row_matmul(dy, w_out, mode="nt", name=name, extras=_gate_halves(pg) + [_rows(ya), _rows(ym)],
                      outs=[((T, GATE_COLS), BF16, (ROW_MM_TILE, GATE_COLS), lambda i: (i, 0)),
                            _rows_out(T, BF16), _rows_out(T, BF16)], epi=epi)


def _log_sigmoid(x):
    return jnp.minimum(x, 0.0) - jnp.log(1.0 + jnp.exp(-jnp.abs(x)))


def _head_mask(e):
    lane = lax.broadcasted_iota(jnp.int32, (1, 128), 1)
    return jnp.logical_and(lane >= e * 64, lane < (e + 1) * 64).astype(F32)


def _bmxu(a, b, ca, cb):
    return lax.dot_general(a.astype(BF16), b.astype(BF16), (((ca,), (cb,)), ((0,), (0,))), preferred_element_type=F32)


@jax.custom_vjp
def cdot_nt(a, b):
    return _bmxu(a, b, 2, 2)


cdot_nt.defvjp(lambda a, b: (_bmxu(a, b, 2, 2), (a, b)),
               lambda res, g: (_bmxu(g, res[1], 2, 1), _bmxu(g, res[0], 1, 1)))


@jax.custom_vjp
def cdot_nn(a, b):
    return _bmxu(a, b, 2, 1)


cdot_nn.defvjp(lambda a, b: (_bmxu(a, b, 2, 1), (a, b)),
               lambda res, g: (_bmxu(g, res[1], 2, 2), _bmxu(res[0], g, 1, 1)))


@jax.custom_vjp
def cdot_tn(a, b):
    return _bmxu(a, b, 1, 1)


cdot_tn.defvjp(lambda a, b: (_bmxu(a, b, 1, 1), (a, b)),
               lambda res, g: (_bmxu(res[1], g, 2, 2), _bmxu(res[0], g, 2, 1)))


def _top_bits(x):
    return lax.bitcast_convert_type(lax.bitcast_convert_type(x, jnp.uint32) & jnp.uint32(0xFFFF0000), F32)


def _split3(x):
    hi = _top_bits(x)
    r = x - hi
    mid = _top_bits(r)
    return hi, mid, r - mid


def _parts_in_lanes(col):
    hi, mid, lo = _split3(col)
    lane = lax.broadcasted_iota(jnp.int32, (1, 1, 8), 2)
    return jnp.where(lane == 0, hi, jnp.where(lane == 1, mid, jnp.where(lane == 2, lo, 0.0)))


def _parts_in_rows(row):
    hi, mid, lo = _split3(row)
    sub = lax.broadcasted_iota(jnp.int32, (1, 8, 1), 1)
    return jnp.where(sub == 0, hi, jnp.where(sub == 1, mid, jnp.where(sub == 2, lo, 0.0)))


def _chunk_matrix(kind, c):
    ri = lax.broadcasted_iota(jnp.int32, (c, ML_CHUNK, ML_CHUNK), 1)
    ci = lax.broadcasted_iota(jnp.int32, (c, ML_CHUNK, ML_CHUNK), 2)
    return {"eye": ri == ci, "lower": ri >= ci, "upper": ri <= ci}[kind].astype(F32)


def _col_col(kind, col):
    out = _bmxu(_chunk_matrix(kind, col.shape[0]), _parts_in_lanes(col), 2, 1)
    return jnp.sum(out, axis=-1, keepdims=True)


def _col_row(col):
    out = _bmxu(_parts_in_lanes(col), _chunk_matrix("eye", col.shape[0]), 1, 1)
    return jnp.sum(out, axis=1, keepdims=True)


def _row_col(row):
    out = _bmxu(_chunk_matrix("eye", row.shape[0]), _parts_in_rows(row), 2, 2)
    return jnp.sum(out, axis=-1, keepdims=True)


@jax.custom_vjp
def chunk_cumsum(col):
    return _col_col("lower", col)


chunk_cumsum.defvjp(lambda col: (_col_col("lower", col), None), lambda _, g: (_col_col("upper", g),))


@jax.custom_vjp
def col_to_row(col):
    return _col_row(col)


col_to_row.defvjp(lambda col: (_col_row(col), None), lambda _, g: (_row_col(g),))


def _gate_block(ifb):
    lane = lax.broadcasted_iota(jnp.int32, (1, 128), 1)
    return jnp.where(lane >= 2, _log_sigmoid(ifb), ifb)


def _ml_intra(q2, k2, v, ifb, *, e):
    c, L = q2.shape[0] // ML_CHUNK, ML_CHUNK
    hm = _head_mask(e)
    q3 = (q2 * hm).reshape(c, L, 128)
    k3 = (k2 * hm).reshape(c, L, 128)
    v3 = v.reshape(c, L, 128)
    if3 = ifb.reshape(c, L, 128)
    lanes = lax.broadcasted_iota(jnp.int32, (c, L, 128), 2)
    li = jnp.sum(jnp.where(lanes == e, if3, 0.0), axis=-1, keepdims=True)
    lf = jnp.sum(jnp.where(lanes == 2 + e, if3, 0.0), axis=-1, keepdims=True)
    b = chunk_cumsum(lf)
    last = lax.broadcasted_iota(jnp.int32, (1, L, 1), 1) == L - 1
    b_end = jnp.sum(jnp.where(last, b, 0.0), axis=1, keepdims=True)
    causal = lax.broadcasted_iota(jnp.int32, (L, L), 0) >= lax.broadcasted_iota(jnp.int32, (L, L), 1)
    Dm = jnp.where(causal, b + col_to_row(li - b), -jnp.inf)
    mD = lax.stop_gradient(jnp.max(Dm, axis=-1, keepdims=True))
    P0 = cdot_nt(q3, k3) * jnp.exp(Dm - mD)
    H0 = cdot_nn(P0, v3)
    r0 = jnp.sum(P0, axis=-1, keepdims=True)
    g = b_end - b + li
    mg = lax.stop_gradient(jnp.max(g, axis=1, keepdims=True))
    kw = jnp.exp(g - mg) * k3
    return H0, r0, cdot_tn(kw, v3), jnp.sum(kw, axis=1, keepdims=True), b, b_end, mD, mg


def _ml_inter(q2, mo, gn, H0, r0, b, C_in, n_in, *, mD, m_in, e):
    c, L = q2.shape[0] // ML_CHUNK, ML_CHUNK
    q3 = (q2 * _head_mask(e)).reshape(c, L, 128)
    a = b + m_in
    m_t = lax.stop_gradient(jnp.maximum(a, mD))
    c1 = jnp.exp(mD - m_t)
    c2 = jnp.exp(a - m_t)
    num = c1 * H0 + c2 * cdot_nn(q3, C_in)
    nq = c1 * r0 + c2 * jnp.sum(q3 * n_in, axis=-1, keepdims=True)
    h = num / jnp.maximum(jnp.abs(nq), jnp.exp(-m_t))
    hg = _rms(h) * gn * jax.nn.sigmoid(mo.reshape(c, L, 128))
    return hg.reshape(c * L, 128)


def _state_sweep(U_s, un_s, be_s, mg_s, Cin_s, nin_s, min_s, al_s, bt_s):
    def step(j, carry):
        C, n, m = carry
        Cin_s[j], nin_s[j], min_s[j] = C, n, m
        m_out = jnp.maximum(be_s[j] + m, mg_s[j])
        al = jnp.exp(be_s[j] + m - m_out)
        bt = jnp.exp(mg_s[j] - m_out)
        al_s[j], bt_s[j] = al, bt
        return al * C + bt * U_s[j], al * n + bt * un_s[j], m_out

    lax.fori_loop(0, N_CHUNKS, step, (jnp.zeros((128, 128), F32), jnp.zeros((1, 128), F32), jnp.zeros((1, 1), F32)))


def _state_sweep_bwd(U_s, un_s, dbe_s, Cin_s, nin_s, dCp_s, dnp_s, al_s, bt_s):
    def step(t, carry):
        j = N_CHUNKS - 1 - t
        dC, dn = carry
        al, bt = al_s[j], bt_s[j]
        U_s[j] = bt * dC
        un_s[j] = bt * dn
        dal = jnp.sum(jnp.sum(dC * Cin_s[j], axis=1, keepdims=True), axis=0, keepdims=True) \
            + jnp.sum(dn * nin_s[j], axis=1, keepdims=True)
        dbe_s[j] = dal * al
        return dCp_s[j] + al * dC, dnp_s[j] + al * dn

    lax.fori_loop(0, N_CHUNKS, step, (jnp.zeros((128, 128), F32), jnp.zeros((1, 128), F32)))


def _state_scratch():
    c = N_CHUNKS
    return [pltpu.VMEM((c, 128, 128), F32), pltpu.VMEM((c, 1, 128), F32), pltpu.VMEM((c, 1, 1), F32),
            pltpu.VMEM((c, 1, 1), F32),
            pltpu.VMEM((c, 128, 128), F32), pltpu.VMEM((c, 1, 128), F32), pltpu.VMEM((c, 1, 1), F32),
            pltpu.VMEM((c, 1, 1), F32), pltpu.VMEM((c, 1, 1), F32)]


def _shift_down(x, s):
    if s == 0:
        return x
    rows = lax.broadcasted_iota(jnp.int32, x.shape, 0)
    return jnp.where(rows >= s, pltpu.roll(x, s, 0), 0.0)


def _shift_up(x, s):
    if s == 0:
        return x
    S = x.shape[0]
    rows = lax.broadcasted_iota(jnp.int32, x.shape, 0)
    return jnp.where(rows < S - s, pltpu.roll(x, S - s, 0), 0.0)


def _conv_pre(x, cw, cb):
    y = cb + cw[3:4, :] * x
    for j in range(3):
        y = y + cw[j:j + 1, :] * _shift_down(x, 3 - j)
    return y


def _conv_bwd(x, cw, dpre):
    dx = cw[3:4, :] * dpre
    dcw = [None] * 4
    dcw[3] = jnp.sum(dpre * x, axis=0, keepdims=True)
    for j in range(3):
        dx = dx + cw[j:j + 1, :] * _shift_up(dpre, 3 - j)
        dcw[j] = jnp.sum(dpre * _shift_down(x, 3 - j), axis=0, keepdims=True)
    return dx, dcw, jnp.sum(dpre, axis=0, keepdims=True)


def _silu(z):
    return z * jax.nn.sigmoid(z)


def _dsilu(z):
    s = jax.nn.sigmoid(z)
    return s * (1.0 + z * (1.0 - s))


_ML_Q, _ML_K = slice(0, 128), slice(128, 256)
_ML_IF = slice(768, 896)


def _ml_v(e):
    return slice(256 + e * 128, 384 + e * 128)


def _ml_o(e):
    return slice(512 + e * 128, 640 + e * 128)


def _ml_specs():
    pair = lambda b, p: (b, 0, p)
    return [pl.BlockSpec((1, SEQ, ML_PAIR_COLS), pair),
            pl.BlockSpec((1, 4, 128), lambda b, p: (p, 0, 0)),
            pl.BlockSpec((1, 4, 128), lambda b, p: (4 + p, 0, 0)),
            pl.BlockSpec((1, 1, 128), lambda b, p: (p, 0, 0)),
            pl.BlockSpec((1, 1, 128), lambda b, p: (4 + p, 0, 0)),
            pl.BlockSpec((1, 1, 128), lambda b, p: (p, 0, 0)),
            pl.BlockSpec((1, 1, 256), lambda b, p: (p, 0, 0))]


def mlstm_fwd(pm, cw8, cb8, bifp, gn4, name):
    B = pm.shape[0]

    def body(p_ref, cwq, cwk, cbq, cbk, bif_ref, gn_ref, hg_ref, *st):
        U_s, un_s, be_s, mg_s, Cin_s, nin_s, min_s, al_s, bt_s = st
        qc = _silu(_conv_pre(p_ref[0, :, _ML_Q], cwq[0], cbq[0]))
        kc = _silu(_conv_pre(p_ref[0, :, _ML_K], cwk[0], cbk[0])) * (64 ** -0.5)
        ifb = _gate_block(p_ref[0, :, _ML_IF] + bif_ref[0])
        for e in range(2):
            lanes = slice(e * 128, (e + 1) * 128)
            H0, r0, U, un, b, b_end, mD, mg = _ml_intra(qc, kc, p_ref[0, :, _ml_v(e)], ifb, e=e)
            U_s[...], un_s[...], be_s[...], mg_s[...] = U, un, b_end, mg
            _state_sweep(*st)
            hg = _ml_inter(qc, p_ref[0, :, _ml_o(e)], gn_ref[0, :, lanes], H0, r0, b, Cin_s[...], nin_s[...],
                           mD=mD, m_in=min_s[...], e=e)
            hg_ref[0, :, lanes] = hg.astype(BF16)

    return pl.pallas_call(
        body, name=name, grid=(B, ML_PAIRS),
        in_specs=_ml_specs(),
        out_specs=pl.BlockSpec((1, SEQ, 256), lambda b, p: (b, 0, p)),
        out_shape=jax.ShapeDtypeStruct((B, SEQ, D_MODEL), BF16),
        scratch_shapes=_state_scratch(),
        compiler_params=_cparams(),
    )(pm, cw8, cw8, cb8, cb8, bifp, gn4)


def mlstm_bwd(pm, cw8, cb8, bifp, gn4, dhg, name):
    B = pm.shape[0]

    def body(p_ref, cwq, cwk, cbq, cbk, bif_ref, gn_ref, dh_ref,
             dp_ref, dcw_ref, dcb_ref, dbif_ref, dgn_ref, *scr):
        st = scr[:9]
        U_s, un_s, be_s, mg_s, Cin_s, nin_s, min_s, al_s, bt_s = st
        dCp_s, dnp_s, dbe_s = scr[9:]
        p_id = pl.program_id(1)

        @pl.when(jnp.logical_and(pl.program_id(0) == 0, p_id == 0))
        def _():
            dcw_ref[...] = jnp.zeros_like(dcw_ref)
            dcb_ref[...] = jnp.zeros_like(dcb_ref)
            dbif_ref[...] = jnp.zeros_like(dbif_ref)
            dgn_ref[...] = jnp.zeros_like(dgn_ref)

        pre_q = _conv_pre(p_ref[0, :, _ML_Q], cwq[0], cbq[0])
        pre_k = _conv_pre(p_ref[0, :, _ML_K], cwk[0], cbk[0])
        qc = _silu(pre_q)
        kc = _silu(pre_k) * (64 ** -0.5)
        ifb, gate_vjp = jax.vjp(_gate_block, p_ref[0, :, _ML_IF] + bif_ref[0])
        dq = jnp.zeros((SEQ, 128), F32)
        dk = jnp.zeros((SEQ, 128), F32)
        difb = jnp.zeros((SEQ, 128), F32)
        for e in range(2):
            lanes = slice(e * 128, (e + 1) * 128)
            (H0, r0, U, un, b, b_end, mD, mg), vjp1 = jax.vjp(functools.partial(_ml_intra, e=e), qc, kc,
                                                              p_ref[0, :, _ml_v(e)], ifb)
            U_s[...], un_s[...], be_s[...], mg_s[...] = U, un, b_end, mg
            _state_sweep(*st)
            _, vjp3 = jax.vjp(functools.partial(_ml_inter, mD=mD, m_in=min_s[...], e=e), qc, p_ref[0, :, _ml_o(e)],
                              gn_ref[0, :, lanes], H0, r0, b, Cin_s[...], nin_s[...])
            dq_a, dmo, dgn, dH0, dr0, db_a, dCp, dnp = vjp3(dh_ref[0, :, lanes])
            dCp_s[...], dnp_s[...] = dCp, dnp
            _state_sweep_bwd(U_s, un_s, dbe_s, Cin_s, nin_s, dCp_s, dnp_s, al_s, bt_s)
            dq_b, dk_b, dv, difb_e = vjp1((dH0, dr0, U_s[...], un_s[...], db_a, dbe_s[...],
                                           jnp.zeros_like(mD), jnp.zeros_like(mg)))
            dq, dk, difb = dq + dq_a + dq_b, dk + dk_b, difb + difb_e
            dp_ref[0, :, _ml_v(e)] = dv.astype(BF16)
            dp_ref[0, :, _ml_o(e)] = dmo.astype(BF16)
            dgn_ref[p_id, :, lanes] += dgn
        (difb,) = gate_vjp(difb)
        dp_ref[0, :, _ML_IF] = difb.astype(BF16)
        dbif_ref[p_id] += jnp.sum(difb, axis=0, keepdims=True)

        for (sl, cw, pre, d, blk, scale) in ((_ML_Q, cwq, pre_q, dq, p_id, 1.0), (_ML_K, cwk, pre_k, dk, 4 + p_id, 64 ** -0.5)):
            xr = p_ref[0, :, sl]
            dpre = d * scale * _dsilu(pre)
            dx, dcw, dcb = _conv_bwd(xr, cw[0], dpre)
            dp_ref[0, :, sl] = dx.astype(BF16)
            for j in range(4):
                dcw_ref[blk, j:j + 1, :] += dcw[j]
            dcb_ref[blk] += dcb

    full3 = lambda b, p: (0, 0, 0)
    return pl.pallas_call(
        body, name=name, grid=(B, ML_PAIRS),
        in_specs=[pl.BlockSpec((1, SEQ, ML_PAIR_COLS), lambda b, p: (b, 0, p), pipeline_mode=pl.Buffered(1))]
        + _ml_specs()[1:] + [pl.BlockSpec((1, SEQ, 256), lambda b, p: (b, 0, p), pipeline_mode=pl.Buffered(1))],
        out_specs=[pl.BlockSpec((1, SEQ, ML_PAIR_COLS), lambda b, p: (b, 0, p)),
                   pl.BlockSpec((8, 4, 128), full3), pl.BlockSpec((8, 1, 128), full3),
                   pl.BlockSpec((4, 1, 128), full3), pl.BlockSpec((4, 1, 256), full3)],
        out_shape=[jax.ShapeDtypeStruct((B, SEQ, ML_COLS), BF16),
                   jax.ShapeDtypeStruct((8, 4, 128), F32), jax.ShapeDtypeStruct((8, 1, 128), F32),
                   jax.ShapeDtypeStruct((4, 1, 128), F32), jax.ShapeDtypeStruct((4, 1, 256), F32)],
        scratch_shapes=_state_scratch() + [pltpu.VMEM((N_CHUNKS, 128, 128), F32), pltpu.VMEM((N_CHUNKS, 1, 128), F32),
                                           pltpu.VMEM((N_CHUNKS, 1, 1), F32)],
        compiler_params=_cparams(),
    )(pm, cw8, cw8, cb8, cb8, bifp, gn4, dhg)


def _adamw(w, g, m, v):
    m = ADAM_B1 * m + (1.0 - ADAM_B1) * g
    v = ADAM_B2 * v + (1.0 - ADAM_B2) * (g * g)
    m_hat = m / (1.0 - ADAM_B1 ** ADAM_STEP)
    v_hat = v / (1.0 - ADAM_B2 ** ADAM_STEP)
    delta = -ADAM_LR * (m_hat / (jnp.sqrt(v_hat) + ADAM_EPS) + ADAM_WD * w)
    return delta, m, v


def adamw(w, g, m, v, name, parts=False):
    R, C = w.shape
    if R % 8 == 0 or R * C * 4 <= (1 << 20):
        tr = _pick(R, (256, 128, 64, 32, 16, 8)) if R * C * 4 > (1 << 20) else R
        steps = R // tr
        spec = pl.BlockSpec((tr, C), lambda i: (i, 0))
        g_spec = pl.BlockSpec((N_DEV, tr, C), lambda i: (0, i, 0)) if parts else spec
    else:
        tc = _pick(C, (256, 128))
        steps = C // tc
        spec = pl.BlockSpec((R, tc), lambda i: (0, i))
        g_spec = pl.BlockSpec((N_DEV, R, tc), lambda i: (0, 0, i)) if parts else spec

    def body(w_ref, g_ref, m_ref, v_ref, go_ref, d_ref, mo_ref, vo_ref):
        if parts:
            g = g_ref[0].astype(F32)
            for k in range(1, N_DEV):
                g = g + g_ref[k].astype(F32)
        else:
            g = g_ref[...]
        d, mn, vn = _adamw(w_ref[...], g, m_ref[...], v_ref[...])
        go_ref[...], d_ref[...], mo_ref[...], vo_ref[...] = g, d, mn, vn

    return pl.pallas_call(
        body, name=name, grid=(steps,),
        in_specs=[spec, g_spec, spec, spec], out_specs=[spec] * 4,
        out_shape=[jax.ShapeDtypeStruct((R, C), F32)] * 4,
        compiler_params=_cparams(),
    )(w, g, m, v)


def adamw_many(ws, gs, ms, vs, name):
    n = len(ws)

    def fn(*a):
        out = []
        for j in range(n):
            out += list(_adamw(a[j], a[n + j], a[2 * n + j], a[3 * n + j]))
        return tuple(out)

    shapes = [jax.ShapeDtypeStruct(w.shape, F32) for w in ws for _ in range(3)]
    return small_call(fn, list(ws) + list(gs) + list(ms) + list(vs), shapes, name)


def _mesh_pos():
    return lax.axis_index("x"), lax.axis_index("y"), lax.axis_index("c")


def _flip(pos, f):
    x, y, c = pos
    return (1 - x if f & 4 else x, 1 - y if f & 2 else y, 1 - c if f & 1 else c)


def _index(pos):
    return 4 * pos[0] + 2 * pos[1] + pos[2]


def _exchange(arrs, name, scatter):
    n = len(arrs)

    def body(*refs):
        ins, outs = refs[:n], refs[n:2 * n]
        send, recv, lsem = refs[2 * n:]
        me = _mesh_pos()
        mine = _index(me)
        copies = []
        for i in range(n):
            src = ins[i].at[mine] if scatter else ins[i]
            loc = pltpu.make_async_copy(src, outs[i].at[mine], lsem.at[i])
            loc.start()
            copies.append(loc)
            for f in range(1, N_DEV):
                peer = _flip(me, f)
                src = ins[i].at[_index(peer)] if scatter else ins[i]
                cp = pltpu.make_async_remote_copy(
                    src_ref=src, dst_ref=outs[i].at[mine],
                    send_sem=send.at[i * 7 + f - 1], recv_sem=recv.at[i * 7 + f - 1],
                    device_id=peer, device_id_type=pl.DeviceIdType.MESH)
                cp.start()
                copies.append(cp)
        for cp in copies:
            cp.wait()

    any_spec = pl.BlockSpec(memory_space=pl.ANY)
    out_shape = [jax.ShapeDtypeStruct(a.shape if scatter else (N_DEV,) + a.shape, a.dtype) for a in arrs]
    res = pl.pallas_call(
        body, name=name,
        in_specs=[any_spec] * n, out_specs=[any_spec] * n, out_shape=out_shape,
        scratch_shapes=[pltpu.SemaphoreType.DMA((7 * n,)), pltpu.SemaphoreType.DMA((7 * n,)),
                        pltpu.SemaphoreType.DMA((n,))],
        compiler_params=_cparams(),
    )(*arrs)
    return list(res)


def all_gather(arrs, name):
    return _exchange(arrs, name, False)


def all_gather_two_level(arrs, name):
    n = len(arrs)

    def body(*refs):
        ins, outs = refs[:n], refs[n:2 * n]
        send, recv, lsem = refs[2 * n:]
        x, y, c = _mesh_pos()
        me, sibling = (x, y, c), (x, y, 1 - c)
        chips = [(1 - x, y), (x, 1 - y), (1 - x, 1 - y)]

        def copy(i, k, block, to, src=None):
            rows = outs[i].at[_index(block)]
            return pltpu.make_async_remote_copy(
                src_ref=rows if src is None else src, dst_ref=rows,
                send_sem=send.at[i * 7 + k], recv_sem=recv.at[i * 7 + k],
                device_id=to, device_id_type=pl.DeviceIdType.MESH)

        local = [pltpu.make_async_copy(ins[i], outs[i].at[_index(me)], lsem.at[i]) for i in range(n)]
        first = [copy(i, 0, me, sibling, src=ins[i]) for i in range(n)]
        first += [copy(i, 1 + j, me, (*chip, c), src=ins[i]) for i in range(n) for j, chip in enumerate(chips)]
        for cp in local + first:
            cp.start()
        passed = []
        for j, chip in enumerate(chips):
            for i in range(n):
                copy(i, 1 + j, (*chip, c), me).wait_recv()
                cp = copy(i, 4 + j, (*chip, c), sibling)
                cp.start()
                passed.append(cp)
        for i in range(n):
            copy(i, 0, sibling, me).wait_recv()
            for j, chip in enumerate(chips):
                copy(i, 4 + j, (*chip, 1 - c), me).wait_recv()
        for cp in first + passed:
            cp.wait_send()
        for cp in local:
            cp.wait()

    any_spec = pl.BlockSpec(memory_space=pl.ANY)
    res = pl.pallas_call(
        body, name=name,
        in_specs=[any_spec] * n, out_specs=[any_spec] * n,
        out_shape=[jax.ShapeDtypeStruct((N_DEV,) + a.shape, a.dtype) for a in arrs],
        scratch_shapes=[pltpu.SemaphoreType.DMA((7 * n,)), pltpu.SemaphoreType.DMA((7 * n,)),
                        pltpu.SemaphoreType.DMA((n,))],
        compiler_params=_cparams(),
    )(*arrs)
    return list(res)


def all_to_all(arrs, name):
    return _exchange(arrs, name, True)


_HBM = pl.BlockSpec(memory_space=pltpu.HBM)
_SEM = pl.BlockSpec(memory_space=pltpu.SEMAPHORE)
_EFFECT = pltpu.SideEffectType.DATAFLOW_SIDE_EFFECTING


def _split_copies(ins, lands, send, recv, scatter, waiting):
    me = _mesh_pos()
    mine = _index(me)
    copies = []
    for i in range(len(ins)):
        for f in range(1, N_DEV):
            peer = _flip(me, f)
            src = ins[i].at[_index(peer)] if scatter else ins[i]
            copies.append(pltpu.make_async_remote_copy(
                src_ref=src, dst_ref=lands[i].at[_index(peer) if waiting else mine],
                send_sem=send.at[i * 7 + f - 1], recv_sem=recv.at[i * 7 + f - 1],
                device_id=peer, device_id_type=pl.DeviceIdType.MESH))
    return copies


def exchange_start(arrs, name, scatter, after=()):
    n = len(arrs)
    land_shapes = [a.shape if scatter else (N_DEV,) + a.shape for a in arrs]

    def body(*refs):
        ins, lands = refs[:n], refs[n:2 * n]
        send, recv = refs[2 * n + len(after)], refs[2 * n + len(after) + 1]
        token = refs[-1]
        for cp in _split_copies(ins, lands, send, recv, scatter, False):
            cp.start()
        token[...] = jnp.zeros_like(token)

    res = pl.pallas_call(
        body, name=name,
        out_shape=(pltpu.SemaphoreType.DMA((7 * n,)), pltpu.SemaphoreType.DMA((7 * n,)),
                   *[pltpu.HBM(a.shape, a.dtype) for a in arrs],
                   *[pltpu.HBM(s, a.dtype) for s, a in zip(land_shapes, arrs)],
                   jax.ShapeDtypeStruct((8, 128), F32)),
        in_specs=[_HBM] * (2 * n) + [pl.BlockSpec(memory_space=pl.ANY)] * len(after),
        out_specs=(_SEM, _SEM, *[_HBM] * (2 * n), pl.BlockSpec(memory_space=pltpu.VMEM)),
        input_output_aliases={i: 2 + i for i in range(2 * n)},
        compiler_params=pltpu.CompilerParams(has_side_effects=_EFFECT),
    )(*[pltpu.with_memory_space_constraint(a, pltpu.HBM) for a in arrs],
      *[pltpu.with_memory_space_constraint(lax.empty(s, a.dtype), pltpu.HBM) for s, a in zip(land_shapes, arrs)],
      *after)
    return (res[0], res[1], list(res[2:2 + n]), list(res[2 + n:2 + 2 * n])), res[-1]


def exchange_wait(handle, after, name, scatter):
    send, recv, srcs, lands = handle
    n = len(srcs)

    def body(*refs):
        ins, lnd = refs[:n], refs[n:2 * n]
        send_, recv_ = refs[2 * n], refs[2 * n + 1]
        for cp in _split_copies(ins, lnd, send_, recv_, scatter, True):
            cp.wait_send()
            cp.wait_recv()

    res = pl.pallas_call(
        body, name=name,
        out_shape=(*[pltpu.HBM(a.shape, a.dtype) for a in srcs], *[pltpu.HBM(a.shape, a.dtype) for a in lands]),
        in_specs=[_HBM] * (2 * n) + [_SEM, _SEM, pl.BlockSpec(memory_space=pl.ANY)],
        out_specs=tuple([_HBM] * (2 * n)),
        input_output_aliases={i: i for i in range(2 * n)},
        compiler_params=pltpu.CompilerParams(has_side_effects=_EFFECT),
    )(*srcs, *lands, send, recv, after)
    return list(res[n:])


def _own_slot(land, own):
    return lax.dynamic_update_slice(land, own[None], (_index(_mesh_pos()),) + (0,) * own.ndim)


def cast_bf16(arrs, name):
    outs = []
    for i, a in enumerate(arrs):
        R, C = a.shape
        if R % 8 == 0:
            tr = _pick(R, (256, 128, 64, 32, 16, 8)) if R * C * 4 > (1 << 21) else R
            steps, spec = R // tr, pl.BlockSpec((tr, C), lambda i: (i, 0))
        else:
            steps, spec = C // 256, pl.BlockSpec((R, 256), lambda i: (0, i))

        def body(a_ref, o_ref):
            o_ref[...] = a_ref[...].astype(BF16)

        outs.append(pl.pallas_call(body, name=f"{name}_{i}", grid=(steps,), in_specs=[spec], out_specs=spec,
                                   out_shape=jax.ShapeDtypeStruct((R, C), BF16), compiler_params=_cparams())(a))
    return outs


def sum_parts(parts, name):
    def fn(p):
        g = p[0]
        for k in range(1, N_DEV):
            g = g + p[k]
        return (g,)
    return small_call(fn, [parts], [jax.ShapeDtypeStruct(parts.shape[1:], F32)], name)[0]


_SPLITS = np.cumsum([1536, 1536, 1536, 512, 512, 1024, 1024, 8, 8, 2048])[:-1].tolist()


def split_w_in(w):
    aq, ak, av, mq, mk, mv, mo, mi, mf, gates = jnp.split(w, _SPLITS, axis=1)
    R = w.shape[0]
    w_att = jnp.stack([aq.reshape(R, 12, 128), ak.reshape(R, 12, 128), av.reshape(R, 12, 128)], axis=2)
    gif = jnp.concatenate([mi.reshape(R, 4, 2), mf.reshape(R, 4, 2), jnp.zeros((R, 4, 124), w.dtype)], axis=2)
    w_ml = jnp.concatenate([mq.reshape(R, 4, 128), mk.reshape(R, 4, 128), mv.reshape(R, 4, 256),
                            mo.reshape(R, 4, 256), gif], axis=2)
    return w_att.reshape(R, ATT_COLS), w_ml.reshape(R, ML_COLS), gates


def merge_w_in(g_att, g_ml, g_gate):
    R = g_att.shape[0]
    a = g_att.reshape(R, 12, 3, 128)
    m = g_ml.reshape(R, 4, ML_PAIR_COLS)
    gif = m[:, :, 768:772]
    return jnp.concatenate([
        a[:, :, 0].reshape(R, 1536), a[:, :, 1].reshape(R, 1536), a[:, :, 2].reshape(R, 1536),
        m[:, :, 0:128].reshape(R, 512), m[:, :, 128:256].reshape(R, 512),
        m[:, :, 256:512].reshape(R, 1024), m[:, :, 512:768].reshape(R, 1024),
        gif[:, :, 0:2].reshape(R, 8), gif[:, :, 2:4].reshape(R, 8), g_gate], axis=1)


def _blk8(v, width=128):
    r = v.shape[0]
    return v.reshape(r, 1024 // width, width).transpose(1, 0, 2)


def _unblk8(v):
    nb, r, w = v.shape
    return v.transpose(1, 0, 2).reshape(r, nb * w)


def local_step(x, target, mods, w, small, late_w=None, early_g=None, w_in_g=None):
    late_w = late_w or (lambda after: w)
    big = {}
    early_g = early_g or (lambda g: big.update(g))
    w_in_g = w_in_g or (lambda g: big.update(w_in=g))
    B = x.shape[0]
    T = B * SEQ
    shift1, scale1, gate1, shift2, scale2, gate2 = mods
    f2 = lambda a: a.reshape(T, a.shape[-1])
    f3 = lambda a: a.reshape(B, SEQ, a.shape[-1])

    rel_t = jnp.pad(small["rel_bias"].T, ((0, 4), (0, 0)))
    onehots = [_bucket_onehot(d) for _, d in ATT_GROUPS]
    biases = [bias_expand(rel_t, oh, f"bias_expand{g}").reshape(16, ATT_BLOCK, 2 * ATT_BLOCK)
              for g, oh in enumerate(onehots)]
    qg, kg = small["q_norm_g"], small["k_norm_g"]
    cw8 = _blk8(small["conv_w"])
    cb8 = _blk8(small["conv_b"])
    b_if = small["b_if"].reshape(2, 4, 2)
    bifp = jnp.concatenate([b_if[0], b_if[1], jnp.zeros((4, 124), F32)], axis=1).reshape(4, 1, 128)
    gn4 = small["mlstm_norm_g"].reshape(4, 1, 256)

    u = modnorm_fwd(x, small["norm1_g"], scale1, shift1, "modnorm1")
    u2d = f2(u)
    pa = f3(matmul(u2d, w["w_att"], mode="nn", name="proj_att"))
    pm = f3(matmul(u2d, w["w_ml"], mode="nn", name="proj_ml"))
    pg = matmul(u2d, w["w_gate"], mode="nn", name="proj_gate")
    os_, ls_ = [], []
    one_block = [SEQ // d == ATT_BLOCK for _, d in ATT_GROUPS]
    for g in range(3):
        o, l = (attn_fwd_classes if one_block[g] else attn_fwd)(pa, biases[g], qg, kg, g, f"attn_fwd{g}")
        os_.append(f2(o))
        ls_.append(f2(l))
    hg = mlstm_fwd(pm, cw8, cb8, bifp, gn4, "mlstm_fwd")
    w = {**w, **late_w(hg)}
    att, y_att = merge_att_out(os_, ls_, w["w_att_out"], "att_out")
    y_ml, z = ml_out_gate(f2(hg), w["w_ml_out"], pg, y_att, "ml_out")
    y, x1, u2 = out_proj_resid_modnorm(f2(z), w["w_out"], f2(x), gate1, small["norm2_g"], scale2, shift2, "out_proj")
    pre, hdn = matmul(u2, w["w_ff1"], mode="nn", name="ff1", out_dtypes=(BF16, BF16),
                      epi=lambda acc: (acc, jnp.square(jnp.maximum(acc, 0.0))))
    dx2, d_ffo, loss, d_gate2 = ff2_loss(hdn, w["w_ff2"], x1, gate2, f2(target), "ff2_loss")

    g_ff2 = matmul(hdn, d_ffo, mode="tn", name="g_ff2", out_dtypes=(BF16,))
    d_pre = matmul(d_ffo, w["w_ff2"], mode="nt", name="d_hdn", out_dtypes=(BF16,), extras=(pre,),
                   epi=lambda acc, p: (acc * (2.0 * jnp.maximum(p.astype(F32), 0.0)),))
    g_ff1 = matmul(u2, d_pre, mode="tn", name="g_ff1", out_dtypes=(BF16,))
    dx1, d_norm2, d_scale2, d_shift2, dy, d_gate1 = d_u_modnorm_bwd(
        d_pre, w["w_ff1"], None, x1, small["norm2_g"], scale2, shift2, dx2, y, gate1, "d_u2")
    g_out = matmul(f2(z), dy, mode="tn", name="g_out", out_dtypes=(BF16,))
    dpg, d_ya, d_ym = d_z_gate_bwd(dy, w["w_out"], pg, y_att, y_ml, "d_z")
    g_att_out = matmul(f2(att), f2(d_ya), mode="tn", name="g_att_out", out_dtypes=(BF16,))
    dmerge = d_att_merge_bwd(d_ya, w["w_att_out"], os_, ls_, "d_att")
    g_ml_out = matmul(f2(hg), f2(d_ym), mode="tn", name="g_ml_out", out_dtypes=(BF16,))
    d_hg = matmul(f2(d_ym), w["w_ml_out"], mode="nt", name="d_hg")
    started = early_g(dict(w_att_out=g_att_out, w_ml_out=g_ml_out, w_out=g_out, w_ff1=g_ff1, w_ff2=g_ff2))
    order = 0.0 if started is None else started[0, 0]
    dmerge = [f3(d) for d in dmerge]
    dpa = lax.empty((B, SEQ, ATT_COLS), BF16)
    d_rel = []
    d_qg = d_kg = None
    for g in range(3):
        dpa, dbias, dq_g, dk_g = (attn_bwd_classes if one_block[g] else attn_bwd)(
            pa, biases[g], qg + order, kg, dmerge[g], dmerge[3 + g], dpa, g, f"attn_bwd{g}")
        db8 = jnp.pad(dbias.reshape(4, -1), ((0, 4), (0, 0)))
        d_rel.append(bias_reduce(db8, onehots[g], f"bias_reduce{g}")[:4])
        d_qg = dq_g if d_qg is None else d_qg + dq_g
        d_kg = dk_g if d_kg is None else d_kg + dk_g
    dpm, dcw8, dcb8, dbifp, dgn4 = mlstm_bwd(pm, cw8, cb8, bifp, gn4 + order, f3(d_hg), "mlstm_bwd")
    g_w_att = matmul(u2d, f2(dpa), mode="tn", name="g_w_att", out_dtypes=(BF16,))
    g_w_ml = matmul(u2d, f2(dpm), mode="tn", name="g_w_ml", out_dtypes=(BF16,))
    g_w_gate = matmul(u2d, f2(dpg), mode="tn", name="g_w_gate", out_dtypes=(BF16,))
    started = w_in_g(merge_w_in(g_w_att, g_w_ml, g_w_gate))
    du = matmul(f2(dpa), w["w_att"], mode="nt", name="d_u_att", after=() if started is None else (started,))
    du = matmul(f2(dpm), w["w_ml"], mode="nt", name="d_u_ml", extras=(du,), epi=lambda acc, e: (acc + e,))
    grad_x, d_norm1, d_scale1, d_shift1 = d_u_modnorm_bwd(
        f2(dpg), w["w_gate"], du, f2(x), small["norm1_g"], scale1, shift1, dx1, None, None, "d_u_gate")
    grad_x = f3(grad_x)

    d_mods = (d_shift1, d_scale1, d_gate1, d_shift2, d_scale2, d_gate2)
    dbif = dbifp.reshape(4, 128)
    small_g = dict(
        norm1_g=d_norm1, norm2_g=d_norm2,
        b_if=jnp.stack([dbif[:, 0:2].reshape(8), dbif[:, 2:4].reshape(8)]),
        conv_w=_unblk8(dcw8), conv_b=_unblk8(dcb8), q_norm_g=d_qg, k_norm_g=d_kg,
        rel_bias=jnp.concatenate(d_rel, axis=0).T,
        mlstm_norm_g=dgn4.reshape(1, 1024))
    return loss, grad_x, d_mods, big, small_g


_SMALL = (("b_ada", 6144), ("norm1_g", 1024), ("norm2_g", 1024), ("b_if", 16), ("conv_b", 1024),
          ("q_norm_g", 128), ("k_norm_g", 128), ("rel_bias", 384), ("mlstm_norm_g", 1024), ("conv_w", 4096))
_SMALL_ROWS = 120
_REPL = _SMALL[:-1]


def _pack(d, names, rows):
    flat = jnp.concatenate([d[k].reshape(-1) for k, _ in names])
    return jnp.pad(flat, (0, rows * 128 - flat.shape[0])).reshape(rows, 128)


def _unpack(slab, names, shapes):
    flat = slab.reshape(-1)
    out, off = {}, 0
    for k, nel in names:
        out[k] = flat[off:off + nel].reshape(shapes[k])
        off += nel
    return out


def kernel(x, c, w_ada, b_ada, norm1_g, norm2_g, w_in, b_if, conv_w, conv_b, q_norm_g, k_norm_g, rel_bias, mlstm_norm_g, w_att_out, w_ml_out, w_out, w_ff1, w_ff2, loss_target, m_w_ada, m_b_ada, m_norm1_g, m_norm2_g, m_w_in, m_b_if, m_conv_w, m_conv_b, m_q_norm_g, m_k_norm_g, m_rel_bias, m_mlstm_norm_g, m_w_att_out, m_w_ml_out, m_w_out, m_w_ff1, m_w_ff2, v_w_ada, v_b_ada, v_norm1_g, v_norm2_g, v_w_in, v_b_if, v_conv_w, v_conv_b, v_q_norm_g, v_k_norm_g, v_rel_bias, v_mlstm_norm_g, v_w_att_out, v_w_ml_out, v_w_out, v_w_ff1, v_w_ff2):
    P = dict(w_ada=w_ada, b_ada=b_ada, norm1_g=norm1_g, norm2_g=norm2_g, w_in=w_in, b_if=b_if, conv_w=conv_w,
             conv_b=conv_b, q_norm_g=q_norm_g, k_norm_g=k_norm_g, rel_bias=rel_bias, mlstm_norm_g=mlstm_norm_g,
             w_att_out=w_att_out, w_ml_out=w_ml_out, w_out=w_out, w_ff1=w_ff1, w_ff2=w_ff2)
    M = dict(w_ada=m_w_ada, b_ada=m_b_ada, norm1_g=m_norm1_g, norm2_g=m_norm2_g, w_in=m_w_in, b_if=m_b_if,
             conv_w=m_conv_w, conv_b=m_conv_b, q_norm_g=m_q_norm_g, k_norm_g=m_k_norm_g, rel_bias=m_rel_bias,
             mlstm_norm_g=m_mlstm_norm_g, w_att_out=m_w_att_out, w_ml_out=m_w_ml_out, w_out=m_w_out,
             w_ff1=m_w_ff1, w_ff2=m_w_ff2)
    V = dict(w_ada=v_w_ada, b_ada=v_b_ada, norm1_g=v_norm1_g, norm2_g=v_norm2_g, w_in=v_w_in, b_if=v_b_if,
             conv_w=v_conv_w, conv_b=v_conv_b, q_norm_g=v_q_norm_g, k_norm_g=v_k_norm_g, rel_bias=v_rel_bias,
             mlstm_norm_g=v_mlstm_norm_g, w_att_out=v_w_att_out, w_ml_out=v_w_ml_out, w_out=v_w_out,
             w_ff1=v_w_ff1, w_ff2=v_w_ff2)
    names = list(P)
    shapes = {k: P[k].shape for k in names}
    B = x.shape[0]
    me = _index(_mesh_pos())

    big_names = ("w_in", "w_att_out", "w_ml_out", "w_out", "w_ff1", "w_ff2")
    shards = cast_bf16([P[k][0] for k in big_names], "cast_w")
    w_in_g8, c8, conv_w8 = all_gather_two_level([shards[0], c, conv_w[0]], "gather_w_in")
    c_all = c8.reshape(N_DEV * B, D_MODEL)
    conv_w_full = conv_w8.transpose(1, 0, 2).reshape(4, 1024)
    w_att, w_ml, w_gate = split_w_in(w_in_g8.transpose(1, 0, 2).reshape(D_MODEL, D_IN))
    w = dict(w_att=w_att, w_ml=w_ml, w_gate=w_gate)

    (silu_c,) = small_call(lambda a: (_silu(a),), [c_all], [jax.ShapeDtypeStruct(c_all.shape, F32)], "silu_c")
    b_ada_cols = lax.dynamic_slice(b_ada, (0, me * 768), (1, 768))
    ada_cols = matmul(silu_c, w_ada[0], mode="nn", name="ada", extras=(jnp.broadcast_to(b_ada_cols, (N_DEV * B, 768)),),
                      epi=lambda acc, bb: (acc + bb,))
    (ada_t,) = all_to_all([ada_cols.reshape(N_DEV, B, 768)], "ada_exchange")
    ada = ada_t.transpose(1, 0, 2).reshape(B, 6 * D_MODEL)
    mods = tuple(ada[:, i * D_MODEL:(i + 1) * D_MODEL].reshape(B, 1, D_MODEL) for i in range(6))

    late_handle, late_order = exchange_start(shards[1:], "gather_late_start", False, after=(ada_t,))

    def late_w(after):
        lands = exchange_wait(late_handle, after, "gather_late_wait", False)
        gw = dict(zip(big_names[1:], [_own_slot(l, s) for l, s in zip(lands, shards[1:])]))
        return dict(w_att_out=gw["w_att_out"].transpose(1, 0, 2).reshape(512, D_MODEL),
                    w_ml_out=gw["w_ml_out"].reshape(D_MODEL, D_MODEL), w_out=gw["w_out"].reshape(D_MODEL, D_MODEL),
                    w_ff1=gw["w_ff1"].transpose(1, 0, 2).reshape(D_MODEL, D_FF),
                    w_ff2=gw["w_ff2"].reshape(D_FF, D_MODEL))

    pending = {}

    def send_grads(key, blocks, name):
        handle, order = exchange_start(blocks, name, True)
        pending[key] = (handle, [lax.dynamic_index_in_dim(b, me, 0, keepdims=False) for b in blocks])
        return order

    def early_g(g):
        return send_grads("late", [g["w_att_out"].reshape(512, N_DEV, 128).transpose(1, 0, 2),
                                   g["w_ml_out"].reshape(N_DEV, 128, D_MODEL), g["w_out"].reshape(N_DEV, 128, D_MODEL),
                                   g["w_ff1"].reshape(D_MODEL, N_DEV, 512).transpose(1, 0, 2),
                                   g["w_ff2"].reshape(N_DEV, 512, D_MODEL)], "grad_late_start")

    def w_in_g(g):
        return send_grads("w_in", [g.reshape(D_MODEL, N_DEV, W_IN_SHARD).transpose(1, 0, 2)], "grad_w_in_start")

    def recv_grads(key, after, name):
        handle, own = pending[key]
        return [_own_slot(l, o) for l, o in zip(exchange_wait(handle, after, name, True), own)]

    small = dict(norm1_g=norm1_g + late_order[0, 0], norm2_g=norm2_g, b_if=b_if[0], conv_w=conv_w_full, conv_b=conv_b,
                 q_norm_g=q_norm_g, k_norm_g=k_norm_g, rel_bias=rel_bias, mlstm_norm_g=mlstm_norm_g)
    loss, grad_x, d_mods, _, small_g = local_step(x, loss_target, mods, w, small, late_w, early_g, w_in_g)
    loss = lax.psum(loss[0, 0], ("x", "y", "c"))

    d_ada = jnp.concatenate([d.reshape(B, D_MODEL) for d in d_mods], axis=1)
    (d_ada_t,) = all_to_all([d_ada.reshape(B, N_DEV, 768).transpose(1, 0, 2)], "d_ada_exchange")
    d_ada_cols = d_ada_t.reshape(N_DEV * B, 768)
    g_w_ada = matmul(silu_c, d_ada_cols, mode="tn", name="g_w_ada")
    (g_b_cols,) = small_call(lambda a: (jnp.sum(a, axis=0, keepdims=True),), [d_ada_cols],
                             [jax.ShapeDtypeStruct((1, 768), F32)], "g_b_ada_cols")
    small_g["b_ada"] = lax.dynamic_update_slice(jnp.zeros((1, 6144), F32), g_b_cols, (0, me * 768))

    recv = recv_grads("w_in", grad_x, "grad_w_in_wait") + recv_grads("late", grad_x, "grad_late_wait")
    (small_parts,) = all_gather([_pack(small_g, _SMALL, _SMALL_ROWS)], "small_grad_gather")
    small_sum = sum_parts(small_parts, "small_grad_sum")
    sg = _unpack(small_sum, _SMALL, {**{k: shapes[k] for k, _ in _REPL}, "conv_w": (4, 1024)})

    G, Dl, NM, NV = {}, {}, {}, {}
    for k, parts in zip(big_names, recv):
        g, d, nm, nv = adamw(P[k][0], parts, M[k][0], V[k][0], f"adamw_{k}", parts=True)
        G[k], Dl[k], NM[k], NV[k] = g[None], d[None], nm[None], nv[None]
    g, d, nm, nv = adamw(w_ada[0], g_w_ada, m_w_ada[0], v_w_ada[0], "adamw_w_ada")
    G["w_ada"], Dl["w_ada"], NM["w_ada"], NV["w_ada"] = g[None], d[None], nm[None], nv[None]
    g_conv = lax.dynamic_slice(sg["conv_w"], (0, me * 128), (4, 128))
    g, d, nm, nv = adamw(conv_w[0], g_conv, m_conv_w[0], v_conv_w[0], "adamw_conv_w")
    G["conv_w"], Dl["conv_w"], NM["conv_w"], NV["conv_w"] = g[None], d[None], nm[None], nv[None]
    flat2 = lambda a: a.reshape(-1, a.shape[-1])
    keys = [k for k, _ in _REPL]
    upd = adamw_many([flat2(P[k]) for k in keys], [flat2(sg[k]) for k in keys], [flat2(M[k]) for k in keys],
                     [flat2(V[k]) for k in keys], "adamw_small")
    for j, k in enumerate(keys):
        G[k] = sg[k]
        Dl[k], NM[k], NV[k] = [upd[3 * j + t].reshape(shapes[k]) for t in range(3)]

    return (loss, grad_x, *[G[k] for k in names], *[Dl[k] for k in names], *[NM[k] for k in names],
            *[NV[k] for k in names])
```

```python
import functools
import math

import numpy as np
import jax
import jax.numpy as jnp
from jax import lax
from jax.experimental import pallas as pl
from jax.experimental.pallas import tpu as pltpu

F32 = jnp.float32
BF16 = jnp.bfloat16

N_DEV = 8
D_MODEL = 1024
SEQ = 2048
ATT_GROUPS = ((128, 1), (512, 4), (2048, 16))
N_ATT_HEADS = 12
ATT_BLOCK = 128
HEAD_DIM = 128
ML_HEADS = 8
ML_PAIRS = 4
ML_CHUNK = 64
N_CHUNKS = SEQ // ML_CHUNK
N_BUCKETS = 32
MAX_DISTANCE = 2048
D_FF = 4096
D_IN = 9744
EPS = 1e-6

ADAM_LR = 0.001
ADAM_B1 = 0.9
ADAM_B2 = 0.999
ADAM_EPS = 1e-08
ADAM_WD = 0.01
ADAM_STEP = 10

ATT_HEAD_COLS = 3 * HEAD_DIM
ATT_COLS = N_ATT_HEADS * ATT_HEAD_COLS
ML_PAIR_COLS = 896
ML_COLS = ML_PAIRS * ML_PAIR_COLS
GATE_COLS = 2 * D_MODEL
W_IN_SHARD = D_IN // N_DEV

VMEM_LIMIT = 60 * 1024 * 1024


def _cparams(**kw):
    return pltpu.CompilerParams(vmem_limit_bytes=VMEM_LIMIT, **kw)


_NN = ((1,), (0,))
_NT = ((1,), (1,))
_TN = ((0,), (0,))


def _mxu(a, b, dims):
    return lax.dot_general(a.astype(BF16), b.astype(BF16), (dims, ((), ())), preferred_element_type=F32)


@jax.custom_vjp
def bdot_nn(a, b):
    return _mxu(a, b, _NN)


def _nn_fwd(a, b):
    return _mxu(a, b, _NN), (a, b)


def _nn_bwd(res, g):
    a, b = res
    return _mxu(g, b, _NT), _mxu(a, g, _TN)


bdot_nn.defvjp(_nn_fwd, _nn_bwd)


@jax.custom_vjp
def bdot_nt(a, b):
    return _mxu(a, b, _NT)


def _nt_fwd(a, b):
    return _mxu(a, b, _NT), (a, b)


def _nt_bwd(res, g):
    a, b = res
    return _mxu(g, b, _NN), _mxu(g, a, _TN)


bdot_nt.defvjp(_nt_fwd, _nt_bwd)


def _doth(a, b, dims=_NN):
    return lax.dot_general(a, b, (dims, ((), ())), precision=lax.Precision.HIGHEST, preferred_element_type=F32)


def _rms(x):
    return x * lax.rsqrt(jnp.mean(x * x, axis=-1, keepdims=True) + EPS)


def _pick(n, cands):
    for t in cands:
        if n % t == 0:
            return t
    raise ValueError(f"no tile for {n}")


MM_TILE_M = (1024, 512, 256, 128, 64, 32, 16, 8)
MM_TILE_N = (2048, 1792, 1536, 1024, 768, 512, 256, 128)
MM_TILE_K = (2048, 1792, 1536, 1024, 512, 256, 128, 64, 32)

def matmul(a, b, *, mode, name, out_dtypes=(F32,), epi=None, extras=(), after=()):
    if mode == "nn":
        (M, K), (K2, N) = a.shape, b.shape
    elif mode == "nt":
        (M, K), (N, K2) = a.shape, b.shape
    else:
        (K, M), (K2, N) = a.shape, b.shape
    assert K == K2, (a.shape, b.shape, mode)
    tm = _pick(M, MM_TILE_M)
    tn = _pick(N, MM_TILE_N)
    tk = _pick(K, MM_TILE_K)
    nk = K // tk
    n_ex = len(extras)
    n_out = len(out_dtypes)
    dims = {"nn": _NN, "nt": _NT, "tn": _TN}[mode]

    def finish(r, ex_refs, out_refs):
        outs = epi(r, *[e[...] for e in ex_refs]) if epi is not None else (r,)
        for o_ref, o in zip(out_refs, outs):
            o_ref[...] = o.astype(o_ref.dtype)

    def body(*refs):
        a_ref, b_ref = refs[0], refs[1]
        ex_refs = refs[2:2 + n_ex]
        out_refs = refs[2 + n_ex + len(after):2 + n_ex + len(after) + n_out]
        if nk == 1:
            finish(_mxu(a_ref[...], b_ref[...], dims), ex_refs, out_refs)
            return
        acc = refs[2 + n_ex + len(after) + n_out]
        k = pl.program_id(2)

        @pl.when(k == 0)
        def _():
            acc[...] = jnp.zeros_like(acc)

        acc[...] += _mxu(a_ref[...], b_ref[...], dims)

        @pl.when(k == nk - 1)
        def _():
            finish(acc[...], ex_refs, out_refs)

    if mode == "nn":
        a_spec = pl.BlockSpec((tm, tk), lambda i, j, k: (i, k))
        b_spec = pl.BlockSpec((tk, tn), lambda i, j, k: (k, j))
    elif mode == "nt":
        a_spec = pl.BlockSpec((tm, tk), lambda i, j, k: (i, k))
        b_spec = pl.BlockSpec((tn, tk), lambda i, j, k: (j, k))
    else:
        a_spec = pl.BlockSpec((tk, tm), lambda i, j, k: (k, i))
        b_spec = pl.BlockSpec((tk, tn), lambda i, j, k: (k, j))
    o_spec = pl.BlockSpec((tm, tn), lambda i, j, k: (i, j))
    res = pl.pallas_call(
        body,
        name=name,
        grid=(M // tm, N // tn, nk),
        in_specs=[a_spec, b_spec] + [o_spec] * n_ex + [pl.BlockSpec(memory_space=pl.ANY)] * len(after),
        out_specs=[o_spec] * n_out,
        out_shape=[jax.ShapeDtypeStruct((M, N), dt) for dt in out_dtypes],
        scratch_shapes=[pltpu.VMEM((tm, tn), F32)] if nk > 1 else [],
        compiler_params=_cparams(),
    )(a, b, *extras, *after)
    return res[0] if n_out == 1 else tuple(res)


ROW_MM_TILE = 512


def row_matmul(a, b, *, mode, name, extras, outs, epi):
    (M, K) = a.shape
    N = b.shape[1] if mode == "nn" else b.shape[0]
    tm = ROW_MM_TILE
    tk = _pick(K, MM_TILE_K)
    nk = K // tk
    n_ex, n_out = len(extras), len(outs)

    def body(*refs):
        a_ref, b_ref = refs[0], refs[1]
        ex_refs = refs[2:2 + n_ex]
        out_refs = refs[2 + n_ex:2 + n_ex + n_out]
        i = pl.program_id(0)
        dims = _NN if mode == "nn" else _NT
        if nk == 1:
            epi(_mxu(a_ref[...], b_ref[...], dims), i, ex_refs, out_refs)
            return
        acc = refs[2 + n_ex + n_out]
        k = pl.program_id(1)

        @pl.when(k == 0)
        def _():
            acc[...] = jnp.zeros_like(acc)

        acc[...] += _mxu(a_ref[...], b_ref[...], dims)

        @pl.when(k == nk - 1)
        def _():
            epi(acc[...], i, ex_refs, out_refs)

    def lift(index_map):
        return lambda i, k: index_map(i)

    b_spec = pl.BlockSpec((tk, N), lambda i, k: (k, 0)) if mode == "nn" else pl.BlockSpec((N, tk), lambda i, k: (0, k))
    res = pl.pallas_call(
        body, name=name, grid=(M // tm, nk),
        in_specs=[pl.BlockSpec((tm, tk), lambda i, k: (i, k)), b_spec]
        + [pl.BlockSpec(blk, lift(im)) for _, blk, im in extras],
        out_specs=[pl.BlockSpec(blk, lift(im)) for _, _, blk, im in outs],
        out_shape=[jax.ShapeDtypeStruct(shape, dt) for shape, dt, _, _ in outs],
        scratch_shapes=[pltpu.VMEM((tm, N), F32)] if nk > 1 else [],
        compiler_params=_cparams(),
    )(a, b, *[e[0] for e in extras])
    return tuple(res)


def _rows(arr):
    return (arr, (ROW_MM_TILE, arr.shape[1]), lambda i: (i, 0))


def _rows_out(T, dtype):
    return ((T, D_MODEL), dtype, (ROW_MM_TILE, D_MODEL), lambda i: (i, 0))


def _per_seq(arr):
    return (arr, (1, 1, D_MODEL), lambda i: (i // (SEQ // ROW_MM_TILE), 0, 0))


def _per_seq_out(B):
    return ((B, 1, D_MODEL), F32, (1, 1, D_MODEL), lambda i: (i // (SEQ // ROW_MM_TILE), 0, 0))


def _first_tile_of_seq(i):
    return i % (SEQ // ROW_MM_TILE) == 0


def small_call(fn, inputs, out_shapes, name):
    n_in = len(inputs)

    def body(*refs):
        outs = fn(*[r[...] for r in refs[:n_in]])
        for o_ref, o in zip(refs[n_in:], outs):
            o_ref[...] = o.astype(o_ref.dtype)

    res = pl.pallas_call(body, name=name, out_shape=list(out_shapes), compiler_params=_cparams())(*inputs)
    return tuple(res)


ROW_TILE = 512


def _modnorm(x, g, scale, shift):
    return _rms(x) * g * (1.0 + scale) + shift


def _row_spec(width):
    return pl.BlockSpec((1, ROW_TILE, width), lambda b, i: (b, i, 0))


def _mod_spec():
    return pl.BlockSpec((1, 1, D_MODEL), lambda b, i: (b, 0, 0))


def _vec_spec():
    return pl.BlockSpec((1, D_MODEL), lambda b, i: (0, 0))


def modnorm_fwd(x, g, scale, shift, name):
    B, S, D = x.shape

    def body(x_ref, g_ref, sc_ref, sh_ref, u_ref):
        u_ref[0] = _modnorm(x_ref[0], g_ref[...], sc_ref[0], sh_ref[0]).astype(BF16)

    return pl.pallas_call(
        body, name=name, grid=(B, S // ROW_TILE),
        in_specs=[_row_spec(D), _vec_spec(), _mod_spec(), _mod_spec()],
        out_specs=_row_spec(D),
        out_shape=jax.ShapeDtypeStruct((B, S, D), BF16),
        compiler_params=_cparams(),
    )(x, g, scale, shift)


def _gain_spec(g):
    return (g, (1, D_MODEL), lambda i: (0, 0))


def out_proj_resid_modnorm(z, w_out, x, gate, g, scale, shift, name):
    T = z.shape[0]

    def epi(acc, i, ex, out):
        x_ref, gt_ref, g_ref, sc_ref, sh_ref = ex
        y_ref, x1_ref, u_ref = out
        y_ref[...] = acc
        x1 = x_ref[...] + gt_ref[0] * acc
        x1_ref[...] = x1
        u_ref[...] = _modnorm(x1, g_ref[...], sc_ref[0], sh_ref[0]).astype(BF16)

    return row_matmul(z, w_out, mode="nn", name=name,
                      extras=[_rows(x), _per_seq(gate), _gain_spec(g), _per_seq(scale), _per_seq(shift)],
                      outs=[_rows_out(T, F32), _rows_out(T, F32), _rows_out(T, BF16)], epi=epi)


def ff2_loss(hdn, w_ff2, x1, gate, target, name):
    T = hdn.shape[0]

    def epi(acc, i, ex, out):
        x_ref, gt_ref, t_ref = ex
        dx_ref, dffo_ref, loss_ref, dg_ref = out

        @pl.when(i == 0)
        def _():
            loss_ref[...] = jnp.zeros_like(loss_ref)

        @pl.when(_first_tile_of_seq(i))
        def _():
            dg_ref[...] = jnp.zeros_like(dg_ref)

        err = x_ref[...] + gt_ref[0] * acc - t_ref[...]
        dx = err * (1.0 / D_MODEL)
        dx_ref[...] = dx
        dffo_ref[...] = (gt_ref[0] * dx).astype(BF16)
        loss_ref[...] += 0.5 * jnp.sum(jnp.mean(err * err, axis=-1, keepdims=True), axis=0, keepdims=True)
        dg_ref[0] += jnp.sum(dx * acc, axis=0, keepdims=True)

    return row_matmul(hdn, w_ff2, mode="nn", name=name,
                      extras=[_rows(x1), _per_seq(gate), _rows(target)],
                      outs=[_rows_out(T, F32), _rows_out(T, BF16), ((1, 1), F32, (1, 1), lambda i: (0, 0)),
                            _per_seq_out(T // SEQ)], epi=epi)


def d_u_modnorm_bwd(a, w, du_prev, x, g, scale, shift, dx_res, y, gate, name):
    T = a.shape[0]
    B = T // SEQ
    n_prev, resid = int(du_prev is not None), y is not None

    def epi(acc, i, ex, out):
        x_ref, g_ref, sc_ref, sh_ref, dr_ref = ex[n_prev:n_prev + 5]
        dx_ref, dg_ref, dsc_ref, dsh_ref = out[:4]

        @pl.when(i == 0)
        def _():
            dg_ref[...] = jnp.zeros_like(dg_ref)

        @pl.when(_first_tile_of_seq(i))
        def _():
            for r in out[2:4] + out[5:]:
                r[...] = jnp.zeros_like(r)

        du = acc + ex[0][...] if n_prev else acc
        _, vjp = jax.vjp(_modnorm, x_ref[...], g_ref[...], sc_ref[0], sh_ref[0])
        dxn, dg, dsc, dsh = vjp(du)
        dx = dxn + dr_ref[...]
        dx_ref[...] = dx
        dg_ref[...] += dg
        dsc_ref[0] += dsc
        dsh_ref[0] += dsh
        if resid:
            y_ref, gt_ref = ex[n_prev + 5:]
            out[4][...] = (gt_ref[0] * dx).astype(BF16)
            out[5][0] += jnp.sum(dx * y_ref[...], axis=0, keepdims=True)

    extras = ([_rows(du_prev)] if n_prev else []) + [_rows(x), _gain_spec(g), _per_seq(scale), _per_seq(shift),
                                                     _rows(dx_res)] + ([_rows(y), _per_seq(gate)] if resid else [])
    outs = [_rows_out(T, F32), ((1, D_MODEL), F32, (1, D_MODEL), lambda i: (0, 0)), _per_seq_out(B), _per_seq_out(B)]
    outs += [_rows_out(T, BF16), _per_seq_out(B)] if resid else []
    return row_matmul(a, w, mode="nt", name=name, extras=extras, outs=outs, epi=epi)


def _bucket_table(dilation):
    i = np.arange(ATT_BLOCK)[:, None]
    j = np.arange(2 * ATT_BLOCK)[None, :]
    delta = ATT_BLOCK + i - j
    dist = np.maximum(delta, 0) * dilation
    max_exact = N_BUCKETS // 2
    d = np.maximum(dist, max_exact).astype(np.float32)
    large = max_exact + (np.log(d / np.float32(max_exact)) / np.float32(math.log(MAX_DISTANCE / max_exact))
                         * np.float32(N_BUCKETS - max_exact)).astype(np.int32)
    large = np.minimum(large, N_BUCKETS - 1)
    return np.where(dist < max_exact, dist, large).astype(np.int32)


def _bucket_onehot(dilation):
    bt = jnp.asarray(_bucket_table(dilation).reshape(1, -1))
    return (bt == jnp.arange(N_BUCKETS, dtype=jnp.int32)[:, None]).astype(F32)


def bias_expand(rel_t, onehot, name):
    def fn(r, oh):
        return (_doth(r, oh),)
    return small_call(fn, [rel_t, onehot], [jax.ShapeDtypeStruct((rel_t.shape[0], onehot.shape[1]), F32)], name)[0]


def bias_reduce(dbias_flat, onehot, name):
    def fn(db, oh):
        return (_doth(db, oh, _NT),)
    return small_call(fn, [dbias_flat, onehot], [jax.ShapeDtypeStruct((dbias_flat.shape[0], N_BUCKETS), F32)], name)[0]


def _qk_norm(x, g):
    return _rms(x) * g


def _masked_bias(bias):
    i = lax.broadcasted_iota(jnp.int32, (ATT_BLOCK, 2 * ATT_BLOCK), 0)
    j = lax.broadcasted_iota(jnp.int32, (ATT_BLOCK, 2 * ATT_BLOCK), 1)
    bm = jnp.where(jnp.logical_and(j >= i, j <= i + ATT_BLOCK), bias, -jnp.inf)
    return bm, bm[:, ATT_BLOCK:]


def _attn_tile(qn, kn, v, bias):
    s = bdot_nt(qn, kn) * (HEAD_DIM ** -0.5) + bias
    m = lax.stop_gradient(jnp.max(s, axis=-1, keepdims=True))
    p = jnp.exp(s - m)
    l = jnp.sum(p, axis=-1, keepdims=True)
    o = bdot_nn(p, v) / l
    lse = jnp.broadcast_to(m + jnp.log(l), (ATT_BLOCK, HEAD_DIM))
    return o, lse


def _attn_tiles(dilation, rows=SEQ):
    nb = rows // dilation // ATT_BLOCK
    return [(r, n) for r in range(dilation) for n in range(nb)]


def _attn_rows(r, n, dilation, nblk=1):
    if dilation == 1:
        return pl.ds(r + n * ATT_BLOCK, nblk * ATT_BLOCK)
    return pl.ds(r + n * ATT_BLOCK * dilation, nblk * ATT_BLOCK, stride=dilation)


_QL, _KL, _VL = slice(0, 128), slice(128, 256), slice(256, 384)


def _qkv_specs(hb):
    return [pl.BlockSpec((None, SEQ, HEAD_DIM), functools.partial(lambda b, h, j: (b, 0, 3 * (hb + h) + j), j=j))
            for j in range(3)]


def attn_fwd(pa, bias, qg, kg, group, name):
    B = pa.shape[0]
    dilation = ATT_GROUPS[group][1]
    hb = group * 4

    def body(q_ref, k_ref, v_ref, b_ref, qg_ref, kg_ref, o_ref, l_ref, qn_s, kn_s):
        qn_s[...] = _qk_norm(q_ref[...], qg_ref[...])
        kn_s[...] = _qk_norm(k_ref[...], kg_ref[...])
        bias_all, bias_first = _masked_bias(b_ref[0])
        for (r, n) in _attn_tiles(dilation):
            rows = _attn_rows(r, n, dilation)
            if n == 0:
                krows, bias_t = rows, bias_first
            else:
                krows, bias_t = _attn_rows(r, n - 1, dilation, 2), bias_all
            o, lse = _attn_tile(qn_s[rows, :], kn_s[krows, :], v_ref[krows, :], bias_t)
            o_ref[rows, :] = o
            l_ref[rows, :] = lse

    head_out = pl.BlockSpec((None, SEQ, HEAD_DIM), lambda b, h: (b, 0, h))
    return pl.pallas_call(
        body, name=name, grid=(B, 4),
        in_specs=_qkv_specs(hb) + [
                  pl.BlockSpec((1, ATT_BLOCK, 2 * ATT_BLOCK), lambda b, h: (hb + h, 0, 0)),
                  pl.BlockSpec((1, HEAD_DIM), lambda b, h: (0, 0)),
                  pl.BlockSpec((1, HEAD_DIM), lambda b, h: (0, 0))],
        out_specs=[head_out, head_out],
        out_shape=[jax.ShapeDtypeStruct((B, SEQ, 512), F32), jax.ShapeDtypeStruct((B, SEQ, 512), F32)],
        scratch_shapes=[pltpu.VMEM((SEQ, HEAD_DIM), F32)] * 2,
        compiler_params=_cparams(),
    )(pa, pa, pa, bias, qg, kg)


def attn_bwd(pa, bias, qg, kg, do, dlse, dpa, group, name):
    B = pa.shape[0]
    dilation = ATT_GROUPS[group][1]
    hb = group * 4

    def body(q_ref, k_ref, v_ref, b_ref, qg_ref, kg_ref, do_ref, dl_ref, dpa_in,
             dp_ref, db_ref, dqg_ref, dkg_ref, qn_s, kn_s, dq_s, dk_s, dv_s):
        del dpa_in
        h_id = pl.program_id(1)

        @pl.when(jnp.logical_and(pl.program_id(0) == 0, h_id == 0))
        def _():
            db_ref[...] = jnp.zeros_like(db_ref)
            dqg_ref[...] = jnp.zeros_like(dqg_ref)
            dkg_ref[...] = jnp.zeros_like(dkg_ref)

        dk_s[...] = jnp.zeros_like(dk_s)
        dv_s[...] = jnp.zeros_like(dv_s)
        qn_s[...] = _qk_norm(q_ref[...], qg_ref[...])
        kn_s[...] = _qk_norm(k_ref[...], kg_ref[...])
        bias_all, bias_first = _masked_bias(b_ref[0])
        for (r, n) in _attn_tiles(dilation):
            rows = _attn_rows(r, n, dilation)
            if n == 0:
                krows, bias_t = rows, bias_first
            else:
                krows, bias_t = _attn_rows(r, n - 1, dilation, 2), bias_all
            _, vjp = jax.vjp(_attn_tile, qn_s[rows, :], kn_s[krows, :], v_ref[krows, :], bias_t)
            dqn, dkn, dv, dbias = vjp((do_ref[rows, :], dl_ref[rows, :]))
            dq_s[rows, :] = dqn
            dk_s[krows, :] += dkn
            dv_s[krows, :] += dv
            if n == 0:
                db_ref[h_id, :, ATT_BLOCK:] += dbias
            else:
                db_ref[h_id] += dbias
        for x_ref, g_ref, d_s, dg_ref, lanes in ((q_ref, qg_ref, dq_s, dqg_ref, _QL), (k_ref, kg_ref, dk_s, dkg_ref, _KL)):
            _, vjp = jax.vjp(_qk_norm, x_ref[...], g_ref[...])
            dx, dg = vjp(d_s[...])
            dp_ref[0, :, lanes] = dx.astype(BF16)
            dg_ref[...] += dg
        dp_ref[0, :, _VL] = dv_s[...].astype(BF16)

    const2 = lambda b, h: (0, 0)
    head_in = pl.BlockSpec((None, SEQ, HEAD_DIM), lambda b, h: (b, 0, h))
    head_blk = pl.BlockSpec((1, SEQ, ATT_HEAD_COLS), lambda b, h: (b, 0, hb + h))
    return pl.pallas_call(
        body, name=name, grid=(B, 4),
        in_specs=_qkv_specs(hb) + [
                  pl.BlockSpec((1, ATT_BLOCK, 2 * ATT_BLOCK), lambda b, h: (hb + h, 0, 0)),
                  pl.BlockSpec((1, HEAD_DIM), const2), pl.BlockSpec((1, HEAD_DIM), const2),
                  head_in, head_in,
                  pl.BlockSpec(memory_space=pl.ANY)],
        out_specs=[head_blk,
                   pl.BlockSpec((4, ATT_BLOCK, 2 * ATT_BLOCK), lambda b, h: (0, 0, 0)),
                   pl.BlockSpec((1, HEAD_DIM), const2), pl.BlockSpec((1, HEAD_DIM), const2)],
        out_shape=[jax.ShapeDtypeStruct(dpa.shape, BF16),
                   jax.ShapeDtypeStruct((4, ATT_BLOCK, 2 * ATT_BLOCK), F32),
                   jax.ShapeDtypeStruct((1, HEAD_DIM), F32), jax.ShapeDtypeStruct((1, HEAD_DIM), F32)],
        scratch_shapes=[pltpu.VMEM((SEQ, HEAD_DIM), F32)] * 5,
        input_output_aliases={8: 0},
        compiler_params=_cparams(),
    )(pa, pa, pa, bias, qg, kg, do, dlse, dpa)


def _attn_classes(q, k, v, bias, qg, kg):
    s = cdot_nt(_qk_norm(q, qg), _qk_norm(k, kg)) * (HEAD_DIM ** -0.5) + bias
    m = lax.stop_gradient(jnp.max(s, axis=-1, keepdims=True))
    p = jnp.exp(s - m)
    l = jnp.sum(p, axis=-1, keepdims=True)
    o = cdot_nn(p, v) / l
    return o, jnp.broadcast_to(m + jnp.log(l), o.shape)


def _gather_classes(src_ref, dst_s, dilation):
    for r in range(dilation):
        dst_s[r] = src_ref[pl.ds(r, ATT_BLOCK, stride=dilation), :]


def _scatter_classes(src_s, dst_ref, dilation):
    for r in range(dilation):
        dst_ref[pl.ds(r, ATT_BLOCK, stride=dilation), :] = src_s[r]


def attn_fwd_classes(pa, bias, qg, kg, group, name):
    B = pa.shape[0]
    dilation = ATT_GROUPS[group][1]
    hb = group * 4

    def body(q_ref, k_ref, v_ref, b_ref, qg_ref, kg_ref, o_ref, l_ref, q_s, k_s, v_s):
        _gather_classes(q_ref, q_s, dilation)
        _gather_classes(k_ref, k_s, dilation)
        _gather_classes(v_ref, v_s, dilation)
        o, lse = _attn_classes(q_s[...], k_s[...], v_s[...], _masked_bias(b_ref[0])[1], qg_ref[...], kg_ref[...])
        q_s[...], k_s[...] = o, lse
        _scatter_classes(q_s, o_ref, dilation)
        _scatter_classes(k_s, l_ref, dilation)

    head_out = pl.BlockSpec((None, SEQ, HEAD_DIM), lambda b, h: (b, 0, h))
    return pl.pallas_call(
        body, name=name, grid=(B, 4),
        in_specs=_qkv_specs(hb) + [
                  pl.BlockSpec((1, ATT_BLOCK, 2 * ATT_BLOCK), lambda b, h: (hb + h, 0, 0)),
                  pl.BlockSpec((1, HEAD_DIM), lambda b, h: (0, 0)),
                  pl.BlockSpec((1, HEAD_DIM), lambda b, h: (0, 0))],
        out_specs=[head_out, head_out],
        out_shape=[jax.ShapeDtypeStruct((B, SEQ, 512), F32), jax.ShapeDtypeStruct((B, SEQ, 512), F32)],
        scratch_shapes=[pltpu.VMEM((dilation, ATT_BLOCK, HEAD_DIM), F32)] * 3,
        compiler_params=_cparams(),
    )(pa, pa, pa, bias, qg, kg)


def attn_bwd_classes(pa, bias, qg, kg, do, dlse, dpa, group, name):
    B = pa.shape[0]
    dilation = ATT_GROUPS[group][1]
    hb = group * 4

    def body(q_ref, k_ref, v_ref, b_ref, qg_ref, kg_ref, do_ref, dl_ref, dpa_in,
             dp_ref, db_ref, dqg_ref, dkg_ref, q_s, k_s, v_s, do_s, dl_s, rows_s):
        del dpa_in
        h_id = pl.program_id(1)

        @pl.when(jnp.logical_and(pl.program_id(0) == 0, h_id == 0))
        def _():
            db_ref[...] = jnp.zeros_like(db_ref)
            dqg_ref[...] = jnp.zeros_like(dqg_ref)
            dkg_ref[...] = jnp.zeros_like(dkg_ref)

        for src, dst in ((q_ref, q_s), (k_ref, k_s), (v_ref, v_s), (do_ref, do_s), (dl_ref, dl_s)):
            _gather_classes(src, dst, dilation)
        _, vjp = jax.vjp(_attn_classes, q_s[...], k_s[...], v_s[...], _masked_bias(b_ref[0])[1],
                         qg_ref[...], kg_ref[...])
        dq, dk, dv, dbias, dqg, dkg = vjp((do_s[...], dl_s[...]))
        db_ref[h_id, :, ATT_BLOCK:] += dbias
        dqg_ref[...] += dqg
        dkg_ref[...] += dkg
        for d, lanes in ((dq, _QL), (dk, _KL), (dv, _VL)):
            q_s[...] = d
            _scatter_classes(q_s, rows_s, dilation)
            dp_ref[0, :, lanes] = rows_s[...].astype(BF16)

    const2 = lambda b, h: (0, 0)
    head_in = pl.BlockSpec((None, SEQ, HEAD_DIM), lambda b, h: (b, 0, h))
    head_blk = pl.BlockSpec((1, SEQ, ATT_HEAD_COLS), lambda b, h: (b, 0, hb + h))
    return pl.pallas_call(
        body, name=name, grid=(B, 4),
        in_specs=_qkv_specs(hb) + [
                  pl.BlockSpec((1, ATT_BLOCK, 2 * ATT_BLOCK), lambda b, h: (hb + h, 0, 0)),
                  pl.BlockSpec((1, HEAD_DIM), const2), pl.BlockSpec((1, HEAD_DIM), const2),
                  head_in, head_in,
                  pl.BlockSpec(memory_space=pl.ANY)],
        out_specs=[head_blk,
                   pl.BlockSpec((4, ATT_BLOCK, 2 * ATT_BLOCK), lambda b, h: (0, 0, 0)),
                   pl.BlockSpec((1, HEAD_DIM), const2), pl.BlockSpec((1, HEAD_DIM), const2)],
        out_shape=[jax.ShapeDtypeStruct(dpa.shape, BF16),
                   jax.ShapeDtypeStruct((4, ATT_BLOCK, 2 * ATT_BLOCK), F32),
                   jax.ShapeDtypeStruct((1, HEAD_DIM), F32), jax.ShapeDtypeStruct((1, HEAD_DIM), F32)],
        scratch_shapes=[pltpu.VMEM((dilation, ATT_BLOCK, HEAD_DIM), F32)] * 5 + [pltpu.VMEM((SEQ, HEAD_DIM), F32)],
        input_output_aliases={8: 0},
        compiler_params=_cparams(),
    )(pa, pa, pa, bias, qg, kg, do, dlse, dpa)


def _merge(o0, o1, o2, l0, l1, l2):
    mx = jnp.maximum(jnp.maximum(l0, l1), l2)
    e0, e1, e2 = jnp.exp(l0 - mx), jnp.exp(l1 - mx), jnp.exp(l2 - mx)
    den = e0 + e1 + e2
    return (e0 / den) * o0 + (e1 / den) * o1 + (e2 / den) * o2


def merge_att_out(os_, ls_, w_att_out, name):
    T = os_[0].shape[0]
    tile = pl.BlockSpec((ROW_MM_TILE, 512), lambda i: (i, 0))

    def body(o0, o1, o2, l0, l1, l2, w_ref, a_ref, y_ref):
        att = _merge(o0[...], o1[...], o2[...], l0[...], l1[...], l2[...]).astype(BF16)
        a_ref[...] = att
        y_ref[...] = _mxu(att, w_ref[...], _NN)

    return pl.pallas_call(
        body, name=name, grid=(T // ROW_MM_TILE,),
        in_specs=[tile] * 6 + [pl.BlockSpec((512, D_MODEL), lambda i: (0, 0))],
        out_specs=[tile, pl.BlockSpec((ROW_MM_TILE, D_MODEL), lambda i: (i, 0))],
        out_shape=[jax.ShapeDtypeStruct((T, 512), BF16), jax.ShapeDtypeStruct((T, D_MODEL), F32)],
        compiler_params=_cparams(),
    )(*os_, *ls_, w_att_out)


def d_att_merge_bwd(d_ya, w_att_out, os_, ls_, name):
    T = d_ya.shape[0]
    tile = lambda a: (a, (ROW_MM_TILE, 512), lambda i: (i, 0))

    def epi(acc, i, ex, out):
        _, vjp = jax.vjp(_merge, *[e[...] for e in ex])
        for o_ref, g in zip(out, vjp(acc)):
            o_ref[...] = g

    return row_matmul(d_ya, w_att_out, mode="nt", name=name, extras=[tile(a) for a in list(os_) + list(ls_)],
                      outs=[((T, 512), F32, (ROW_MM_TILE, 512), lambda i: (i, 0))] * 6, epi=epi)


def _gate_mix(ga, gm, ya, ym):
    return jax.nn.sigmoid(ga) * ya + jax.nn.sigmoid(gm) * ym


def _gate_halves(pg):
    return [(pg, (ROW_MM_TILE, D_MODEL), lambda i: (i, 0)), (pg, (ROW_MM_TILE, D_MODEL), lambda i: (i, 1))]


def ml_out_gate(hg, w_ml_out, pg, ya, name):
    T = hg.shape[0]

    def epi(acc, i, ex, out):
        ga, gm, ya_ref = ex
        out[0][...] = acc
        out[1][...] = _gate_mix(ga[...], gm[...], ya_ref[...], acc).astype(BF16)

    return row_matmul(hg, w_ml_out, mode="nn", name=name, extras=_gate_halves(pg) + [_rows(ya)],
                      outs=[_rows_out(T, F32), _rows_out(T, BF16)], epi=epi)


def d_z_gate_bwd(dy, w_out, pg, ya, ym, name):
    T = dy.shape[0]

    def epi(acc, i, ex, out):
        ga, gm, ya_ref, ym_ref = ex
        dpg_ref, dya_ref, dym_ref = out
        _, vjp = jax.vjp(_gate_mix, ga[...], gm[...], ya_ref[...], ym_ref[...])
        dga, dgm, dya, dym = vjp(acc)
        dpg_ref[:, :D_MODEL] = dga.astype(BF16)
        dpg_ref[:, D_MODEL:] = dgm.astype(BF16)
        dya_ref[...] = dya.astype(BF16)
        dym_ref[...] = dym.astype(BF16)

    return row_matmul(dy, w_out, mode="nt", name=name, extras=_gate_halves(pg) + [_rows(ya), _rows(ym)],
                      outs=[((T, GATE_COLS), BF16, (ROW_MM_TILE, GATE_COLS), lambda i: (i, 0)),
                            _rows_out(T, BF16), _rows_out(T, BF16)], epi=epi)


def _log_sigmoid(x):
    return jnp.minimum(x, 0.0) - jnp.log(1.0 + jnp.exp(-jnp.abs(x)))


def _head_mask(e):
    lane = lax.broadcasted_iota(jnp.int32, (1, 128), 1)
    return jnp.logical_and(lane >= e * 64, lane < (e + 1) * 64).astype(F32)


def _bmxu(a, b, ca, cb):
    return lax.dot_general(a.astype(BF16), b.astype(BF16), (((ca,), (cb,)), ((0,), (0,))), preferred_element_type=F32)


@jax.custom_vjp
def cdot_nt(a, b):
    return _bmxu(a, b, 2, 2)


cdot_nt.defvjp(lambda a, b: (_bmxu(a, b, 2, 2), (a, b)),
               lambda res, g: (_bmxu(g, res[1], 2, 1), _bmxu(g, res[0], 1, 1)))


@jax.custom_vjp
def cdot_nn(a, b):
    return _bmxu(a, b, 2, 1)


cdot_nn.defvjp(lambda a, b: (_bmxu(a, b, 2, 1), (a, b)),
               lambda res, g: (_bmxu(g, res[1], 2, 2), _bmxu(res[0], g, 1, 1)))


@jax.custom_vjp
def cdot_tn(a, b):
    return _bmxu(a, b, 1, 1)


cdot_tn.defvjp(lambda a, b: (_bmxu(a, b, 1, 1), (a, b)),
               lambda res, g: (_bmxu(res[1], g, 2, 2), _bmxu(res[0], g, 2, 1)))


def _top_bits(x):
    return lax.bitcast_convert_type(lax.bitcast_convert_type(x, jnp.uint32) & jnp.uint32(0xFFFF0000), F32)


def _split3(x):
    hi = _top_bits(x)
    r = x - hi
    mid = _top_bits(r)
    return hi, mid, r - mid


def _parts_in_lanes(col):
    hi, mid, lo = _split3(col)
    lane = lax.broadcasted_iota(jnp.int32, (1, 1, 8), 2)
    return jnp.where(lane == 0, hi, jnp.where(lane == 1, mid, jnp.where(lane == 2, lo, 0.0)))


def _parts_in_rows(row):
    hi, mid, lo = _split3(row)
    sub = lax.broadcasted_iota(jnp.int32, (1, 8, 1), 1)
    return jnp.where(sub == 0, hi, jnp.where(sub == 1, mid, jnp.where(sub == 2, lo, 0.0)))


def _chunk_matrix(kind, c):
    ri = lax.broadcasted_iota(jnp.int32, (c, ML_CHUNK, ML_CHUNK), 1)
    ci = lax.broadcasted_iota(jnp.int32, (c, ML_CHUNK, ML_CHUNK), 2)
    return {"eye": ri == ci, "lower": ri >= ci, "upper": ri <= ci}[kind].astype(F32)


def _col_col(kind, col):
    out = _bmxu(_chunk_matrix(kind, col.shape[0]), _parts_in_lanes(col), 2, 1)
    return jnp.sum(out, axis=-1, keepdims=True)


def _col_row(col):
    out = _bmxu(_parts_in_lanes(col), _chunk_matrix("eye", col.shape[0]), 1, 1)
    return jnp.sum(out, axis=1, keepdims=True)


def _row_col(row):
    out = _bmxu(_chunk_matrix("eye", row.shape[0]), _parts_in_rows(row), 2, 2)
    return jnp.sum(out, axis=-1, keepdims=True)


@jax.custom_vjp
def chunk_cumsum(col):
    return _col_col("lower", col)


chunk_cumsum.defvjp(lambda col: (_col_col("lower", col), None), lambda _, g: (_col_col("upper", g),))


@jax.custom_vjp
def col_to_row(col):
    return _col_row(col)


col_to_row.defvjp(lambda col: (_col_row(col), None), lambda _, g: (_row_col(g),))


def _gate_block(ifb):
    lane = lax.broadcasted_iota(jnp.int32, (1, 128), 1)
    return jnp.where(lane >= 2, _log_sigmoid(ifb), ifb)


def _ml_intra(q2, k2, v, ifb, *, e):
    c, L = q2.shape[0] // ML_CHUNK, ML_CHUNK
    hm = _head_mask(e)
    q3 = (q2 * hm).reshape(c, L, 128)
    k3 = (k2 * hm).reshape(c, L, 128)
    v3 = v.reshape(c, L, 128)
    if3 = ifb.reshape(c, L, 128)
    lanes = lax.broadcasted_iota(jnp.int32, (c, L, 128), 2)
    li = jnp.sum(jnp.where(lanes == e, if3, 0.0), axis=-1, keepdims=True)
    lf = jnp.sum(jnp.where(lanes == 2 + e, if3, 0.0), axis=-1, keepdims=True)
    b = chunk_cumsum(lf)
    last = lax.broadcasted_iota(jnp.int32, (1, L, 1), 1) == L - 1
    b_end = jnp.sum(jnp.where(last, b, 0.0), axis=1, keepdims=True)
    causal = lax.broadcasted_iota(jnp.int32, (L, L), 0) >= lax.broadcasted_iota(jnp.int32, (L, L), 1)
    Dm = jnp.where(causal, b + col_to_row(li - b), -jnp.inf)
    mD = lax.stop_gradient(jnp.max(Dm, axis=-1, keepdims=True))
    P0 = cdot_nt(q3, k3) * jnp.exp(Dm - mD)
    H0 = cdot_nn(P0, v3)
    r0 = jnp.sum(P0, axis=-1, keepdims=True)
    g = b_end - b + li
    mg = lax.stop_gradient(jnp.max(g, axis=1, keepdims=True))
    kw = jnp.exp(g - mg) * k3
    return H0, r0, cdot_tn(kw, v3), jnp.sum(kw, axis=1, keepdims=True), b, b_end, mD, mg


def _ml_inter(q2, mo, gn, H0, r0, b, C_in, n_in, *, mD, m_in, e):
    c, L = q2.shape[0] // ML_CHUNK, ML_CHUNK
    q3 = (q2 * _head_mask(e)).reshape(c, L, 128)
    a = b + m_in
    m_t = lax.stop_gradient(jnp.maximum(a, mD))
    c1 = jnp.exp(mD - m_t)
    c2 = jnp.exp(a - m_t)
    num = c1 * H0 + c2 * cdot_nn(q3, C_in)
    nq = c1 * r0 + c2 * jnp.sum(q3 * n_in, axis=-1, keepdims=True)
    h = num / jnp.maximum(jnp.abs(nq), jnp.exp(-m_t))
    hg = _rms(h) * gn * jax.nn.sigmoid(mo.reshape(c, L, 128))
    return hg.reshape(c * L, 128)


def _state_sweep(U_s, un_s, be_s, mg_s, Cin_s, nin_s, min_s, al_s, bt_s):
    def step(j, carry):
        C, n, m = carry
        Cin_s[j], nin_s[j], min_s[j] = C, n, m
        m_out = jnp.maximum(be_s[j] + m, mg_s[j])
        al = jnp.exp(be_s[j] + m - m_out)
        bt = jnp.exp(mg_s[j] - m_out)
        al_s[j], bt_s[j] = al, bt
        return al * C + bt * U_s[j], al * n + bt * un_s[j], m_out

    lax.fori_loop(0, N_CHUNKS, step, (jnp.zeros((128, 128), F32), jnp.zeros((1, 128), F32), jnp.zeros((1, 1), F32)))


def _state_sweep_bwd(U_s, un_s, dbe_s, Cin_s, nin_s, dCp_s, dnp_s, al_s, bt_s):
    def step(t, carry):
        j = N_CHUNKS - 1 - t
        dC, dn = carry
        al, bt = al_s[j], bt_s[j]
        U_s[j] = bt * dC
        un_s[j] = bt * dn
        dal = jnp.sum(jnp.sum(dC * Cin_s[j], axis=1, keepdims=True), axis=0, keepdims=True) \
            + jnp.sum(dn * nin_s[j], axis=1, keepdims=True)
        dbe_s[j] = dal * al
        return dCp_s[j] + al * dC, dnp_s[j] + al * dn

    lax.fori_loop(0, N_CHUNKS, step, (jnp.zeros((128, 128), F32), jnp.zeros((1, 128), F32)))


def _state_scratch():
    c = N_CHUNKS
    return [pltpu.VMEM((c, 128, 128), F32), pltpu.VMEM((c, 1, 128), F32), pltpu.VMEM((c, 1, 1), F32),
            pltpu.VMEM((c, 1, 1), F32),
            pltpu.VMEM((c, 128, 128), F32), pltpu.VMEM((c, 1, 128), F32), pltpu.VMEM((c, 1, 1), F32),
            pltpu.VMEM((c, 1, 1), F32), pltpu.VMEM((c, 1, 1), F32)]


def _shift_down(x, s):
    if s == 0:
        return x
    rows = lax.broadcasted_iota(jnp.int32, x.shape, 0)
    return jnp.where(rows >= s, pltpu.roll(x, s, 0), 0.0)


def _shift_up(x, s):
    if s == 0:
        return x
    S = x.shape[0]
    rows = lax.broadcasted_iota(jnp.int32, x.shape, 0)
    return jnp.where(rows < S - s, pltpu.roll(x, S - s, 0), 0.0)


def _conv_pre(x, cw, cb):
    y = cb + cw[3:4, :] * x
    for j in range(3):
        y = y + cw[j:j + 1, :] * _shift_down(x, 3 - j)
    return y


def _conv_bwd(x, cw, dpre):
    dx = cw[3:4, :] * dpre
    dcw = [None] * 4
    dcw[3] = jnp.sum(dpre * x, axis=0, keepdims=True)
    for j in range(3):
        dx = dx + cw[j:j + 1, :] * _shift_up(dpre, 3 - j)
        dcw[j] = jnp.sum(dpre * _shift_down(x, 3 - j), axis=0, keepdims=True)
    return dx, dcw, jnp.sum(dpre, axis=0, keepdims=True)


def _silu(z):
    return z * jax.nn.sigmoid(z)


def _dsilu(z):
    s = jax.nn.sigmoid(z)
    return s * (1.0 + z * (1.0 - s))


_ML_Q, _ML_K = slice(0, 128), slice(128, 256)
_ML_IF = slice(768, 896)


def _ml_v(e):
    return slice(256 + e * 128, 384 + e * 128)


def _ml_o(e):
    return slice(512 + e * 128, 640 + e * 128)


def _ml_specs():
    pair = lambda b, p: (b, 0, p)
    return [pl.BlockSpec((1, SEQ, ML_PAIR_COLS), pair),
            pl.BlockSpec((1, 4, 128), lambda b, p: (p, 0, 0)),
            pl.BlockSpec((1, 4, 128), lambda b, p: (4 + p, 0, 0)),
            pl.BlockSpec((1, 1, 128), lambda b, p: (p, 0, 0)),
            pl.BlockSpec((1, 1, 128), lambda b, p: (4 + p, 0, 0)),
            pl.BlockSpec((1, 1, 128), lambda b, p: (p, 0, 0)),
            pl.BlockSpec((1, 1, 256), lambda b, p: (p, 0, 0))]


def mlstm_fwd(pm, cw8, cb8, bifp, gn4, name):
    B = pm.shape[0]

    def body(p_ref, cwq, cwk, cbq, cbk, bif_ref, gn_ref, hg_ref, *st):
        U_s, un_s, be_s, mg_s, Cin_s, nin_s, min_s, al_s, bt_s = st
        qc = _silu(_conv_pre(p_ref[0, :, _ML_Q], cwq[0], cbq[0]))
        kc = _silu(_conv_pre(p_ref[0, :, _ML_K], cwk[0], cbk[0])) * (64 ** -0.5)
        ifb = _gate_block(p_ref[0, :, _ML_IF] + bif_ref[0])
        for e in range(2):
            lanes = slice(e * 128, (e + 1) * 128)
            H0, r0, U, un, b, b_end, mD, mg = _ml_intra(qc, kc, p_ref[0, :, _ml_v(e)], ifb, e=e)
            U_s[...], un_s[...], be_s[...], mg_s[...] = U, un, b_end, mg
            _state_sweep(*st)
            hg = _ml_inter(qc, p_ref[0, :, _ml_o(e)], gn_ref[0, :, lanes], H0, r0, b, Cin_s[...], nin_s[...],
                           mD=mD, m_in=min_s[...], e=e)
            hg_ref[0, :, lanes] = hg.astype(BF16)

    return pl.pallas_call(
        body, name=name, grid=(B, ML_PAIRS),
        in_specs=_ml_specs(),
        out_specs=pl.BlockSpec((1, SEQ, 256), lambda b, p: (b, 0, p)),
        out_shape=jax.ShapeDtypeStruct((B, SEQ, D_MODEL), BF16),
        scratch_shapes=_state_scratch(),
        compiler_params=_cparams(),
    )(pm, cw8, cw8, cb8, cb8, bifp, gn4)


def mlstm_bwd(pm, cw8, cb8, bifp, gn4, dhg, name):
    B = pm.shape[0]

    def body(p_ref, cwq, cwk, cbq, cbk, bif_ref, gn_ref, dh_ref,
             dp_ref, dcw_ref, dcb_ref, dbif_ref, dgn_ref, *scr):
        st = scr[:9]
        U_s, un_s, be_s, mg_s, Cin_s, nin_s, min_s, al_s, bt_s = st
        dCp_s, dnp_s, dbe_s = scr[9:]
        p_id = pl.program_id(1)

        @pl.when(jnp.logical_and(pl.program_id(0) == 0, p_id == 0))
        def _():
            dcw_ref[...] = jnp.zeros_like(dcw_ref)
            dcb_ref[...] = jnp.zeros_like(dcb_ref)
            dbif_ref[...] = jnp.zeros_like(dbif_ref)
            dgn_ref[...] = jnp.zeros_like(dgn_ref)

        pre_q = _conv_pre(p_ref[0, :, _ML_Q], cwq[0], cbq[0])
        pre_k = _conv_pre(p_ref[0, :, _ML_K], cwk[0], cbk[0])
        qc = _silu(pre_q)
        kc = _silu(pre_k) * (64 ** -0.5)
        ifb, gate_vjp = jax.vjp(_gate_block, p_ref[0, :, _ML_IF] + bif_ref[0])
        dq = jnp.zeros((SEQ, 128), F32)
        dk = jnp.zeros((SEQ, 128), F32)
        difb = jnp.zeros((SEQ, 128), F32)
        for e in range(2):
            lanes = slice(e * 128, (e + 1) * 128)
            (H0, r0, U, un, b, b_end, mD, mg), vjp1 = jax.vjp(functools.partial(_ml_intra, e=e), qc, kc,
                                                              p_ref[0, :, _ml_v(e)], ifb)
            U_s[...], un_s[...], be_s[...], mg_s[...] = U, un, b_end, mg
            _state_sweep(*st)
            _, vjp3 = jax.vjp(functools.partial(_ml_inter, mD=mD, m_in=min_s[...], e=e), qc, p_ref[0, :, _ml_o(e)],
                              gn_ref[0, :, lanes], H0, r0, b, Cin_s[...], nin_s[...])
            dq_a, dmo, dgn, dH0, dr0, db_a, dCp, dnp = vjp3(dh_ref[0, :, lanes])
            dCp_s[...], dnp_s[...] = dCp, dnp
            _state_sweep_bwd(U_s, un_s, dbe_s, Cin_s, nin_s, dCp_s, dnp_s, al_s, bt_s)
            dq_b, dk_b, dv, difb_e = vjp1((dH0, dr0, U_s[...], un_s[...], db_a, dbe_s[...],
                                           jnp.zeros_like(mD), jnp.zeros_like(mg)))
            dq, dk, difb = dq + dq_a + dq_b, dk + dk_b, difb + difb_e
            dp_ref[0, :, _ml_v(e)] = dv.astype(BF16)
            dp_ref[0, :, _ml_o(e)] = dmo.astype(BF16)
            dgn_ref[p_id, :, lanes] += dgn
        (difb,) = gate_vjp(difb)
        dp_ref[0, :, _ML_IF] = difb.astype(BF16)
        dbif_ref[p_id] += jnp.sum(difb, axis=0, keepdims=True)

        for (sl, cw, pre, d, blk, scale) in ((_ML_Q, cwq, pre_q, dq, p_id, 1.0), (_ML_K, cwk, pre_k, dk, 4 + p_id, 64 ** -0.5)):
            xr = p_ref[0, :, sl]
            dpre = d * scale * _dsilu(pre)
            dx, dcw, dcb = _conv_bwd(xr, cw[0], dpre)
            dp_ref[0, :, sl] = dx.astype(BF16)
            for j in range(4):
                dcw_ref[blk, j:j + 1, :] += dcw[j]
            dcb_ref[blk] += dcb

    full3 = lambda b, p: (0, 0, 0)
    return pl.pallas_call(
        body, name=name, grid=(B, ML_PAIRS),
        in_specs=[pl.BlockSpec((1, SEQ, ML_PAIR_COLS), lambda b, p: (b, 0, p), pipeline_mode=pl.Buffered(1))]
        + _ml_specs()[1:] + [pl.BlockSpec((1, SEQ, 256), lambda b, p: (b, 0, p), pipeline_mode=pl.Buffered(1))],
        out_specs=[pl.BlockSpec((1, SEQ, ML_PAIR_COLS), lambda b, p: (b, 0, p)),
                   pl.BlockSpec((8, 4, 128), full3), pl.BlockSpec((8, 1, 128), full3),
                   pl.BlockSpec((4, 1, 128), full3), pl.BlockSpec((4, 1, 256), full3)],
        out_shape=[jax.ShapeDtypeStruct((B, SEQ, ML_COLS), BF16),
                   jax.ShapeDtypeStruct((8, 4, 128), F32), jax.ShapeDtypeStruct((8, 1, 128), F32),
                   jax.ShapeDtypeStruct((4, 1, 128), F32), jax.ShapeDtypeStruct((4, 1, 256), F32)],
        scratch_shapes=_state_scratch() + [pltpu.VMEM((N_CHUNKS, 128, 128), F32), pltpu.VMEM((N_CHUNKS, 1, 128), F32),
                                           pltpu.VMEM((N_CHUNKS, 1, 1), F32)],
        compiler_params=_cparams(),
    )(pm, cw8, cw8, cb8, cb8, bifp, gn4, dhg)


def _adamw(w, g, m, v):
    m = ADAM_B1 * m + (1.0 - ADAM_B1) * g
    v = ADAM_B2 * v + (1.0 - ADAM_B2) * (g * g)
    m_hat = m / (1.0 - ADAM_B1 ** ADAM_STEP)
    v_hat = v / (1.0 - ADAM_B2 ** ADAM_STEP)
    delta = -ADAM_LR * (m_hat / (jnp.sqrt(v_hat) + ADAM_EPS) + ADAM_WD * w)
    return delta, m, v


def adamw(w, g, m, v, name, parts=False):
    R, C = w.shape
    if R % 8 == 0 or R * C * 4 <= (1 << 20):
        tr = _pick(R, (256, 128, 64, 32, 16, 8)) if R * C * 4 > (1 << 20) else R
        steps = R // tr
        spec = pl.BlockSpec((tr, C), lambda i: (i, 0))
        g_spec = pl.BlockSpec((N_DEV, tr, C), lambda i: (0, i, 0)) if parts else spec
    else:
        tc = _pick(C, (256, 128))
        steps = C // tc
        spec = pl.BlockSpec((R, tc), lambda i: (0, i))
        g_spec = pl.BlockSpec((N_DEV, R, tc), lambda i: (0, 0, i)) if parts else spec

    def body(w_ref, g_ref, m_ref, v_ref, go_ref, d_ref, mo_ref, vo_ref):
        if parts:
            g = g_ref[0].astype(F32)
            for k in range(1, N_DEV):
                g = g + g_ref[k].astype(F32)
        else:
            g = g_ref[...]
        d, mn, vn = _adamw(w_ref[...], g, m_ref[...], v_ref[...])
        go_ref[...], d_ref[...], mo_ref[...], vo_ref[...] = g, d, mn, vn

    return pl.pallas_call(
        body, name=name, grid=(steps,),
        in_specs=[spec, g_spec, spec, spec], out_specs=[spec] * 4,
        out_shape=[jax.ShapeDtypeStruct((R, C), F32)] * 4,
        compiler_params=_cparams(),
    )(w, g, m, v)


def adamw_many(ws, gs, ms, vs, name):
    n = len(ws)

    def fn(*a):
        out = []
        for j in range(n):
            out += list(_adamw(a[j], a[n + j], a[2 * n + j], a[3 * n + j]))
        return tuple(out)

    shapes = [jax.ShapeDtypeStruct(w.shape, F32) for w in ws for _ in range(3)]
    return small_call(fn, list(ws) + list(gs) + list(ms) + list(vs), shapes, name)


def _mesh_pos():
    return lax.axis_index("x"), lax.axis_index("y"), lax.axis_index("c")


def _flip(pos, f):
    x, y, c = pos
    return (1 - x if f & 4 else x, 1 - y if f & 2 else y, 1 - c if f & 1 else c)


def _index(pos):
    return 4 * pos[0] + 2 * pos[1] + pos[2]


def _exchange(arrs, name, scatter):
    n = len(arrs)
    scat = list(scatter) if isinstance(scatter, (list, tuple)) else [scatter] * n

    def body(*refs):
        ins, outs = refs[:n], refs[n:2 * n]
        send, recv, lsem = refs[2 * n:]
        me = _mesh_pos()
        mine = _index(me)
        copies = []
        for i in range(n):
            src = ins[i].at[mine] if scat[i] else ins[i]
            loc = pltpu.make_async_copy(src, outs[i].at[mine], lsem.at[i])
            loc.start()
            copies.append(loc)
            for f in range(1, N_DEV):
                peer = _flip(me, f)
                src = ins[i].at[_index(peer)] if scat[i] else ins[i]
                cp = pltpu.make_async_remote_copy(
                    src_ref=src, dst_ref=outs[i].at[mine],
                    send_sem=send.at[i * 7 + f - 1], recv_sem=recv.at[i * 7 + f - 1],
                    device_id=peer, device_id_type=pl.DeviceIdType.MESH)
                cp.start()
                copies.append(cp)
        for cp in copies:
            cp.wait()

    any_spec = pl.BlockSpec(memory_space=pl.ANY)
    out_shape = [jax.ShapeDtypeStruct(a.shape if s else (N_DEV,) + a.shape, a.dtype) for a, s in zip(arrs, scat)]
    res = pl.pallas_call(
        body, name=name,
        in_specs=[any_spec] * n, out_specs=[any_spec] * n, out_shape=out_shape,
        scratch_shapes=[pltpu.SemaphoreType.DMA((7 * n,)), pltpu.SemaphoreType.DMA((7 * n,)),
                        pltpu.SemaphoreType.DMA((n,))],
        compiler_params=_cparams(),
    )(*arrs)
    return list(res)


def all_gather(arrs, name):
    return _exchange(arrs, name, False)


def all_gather_two_level(arrs, name):
    n = len(arrs)

    def body(*refs):
        ins, outs = refs[:n], refs[n:2 * n]
        send, recv, lsem = refs[2 * n:]
        x, y, c = _mesh_pos()
        me, sibling = (x, y, c), (x, y, 1 - c)
        chips = [(1 - x, y), (x, 1 - y), (1 - x, 1 - y)]

        def copy(i, k, block, to, src=None):
            rows = outs[i].at[_index(block)]
            return pltpu.make_async_remote_copy(
                src_ref=rows if src is None else src, dst_ref=rows,
                send_sem=send.at[i * 7 + k], recv_sem=recv.at[i * 7 + k],
                device_id=to, device_id_type=pl.DeviceIdType.MESH)

        local = [pltpu.make_async_copy(ins[i], outs[i].at[_index(me)], lsem.at[i]) for i in range(n)]
        first = [copy(i, 0, me, sibling, src=ins[i]) for i in range(n)]
        first += [copy(i, 1 + j, me, (*chip, c), src=ins[i]) for i in range(n) for j, chip in enumerate(chips)]
        for cp in local + first:
            cp.start()
        passed = []
        for j, chip in enumerate(chips):
            for i in range(n):
                copy(i, 1 + j, (*chip, c), me).wait_recv()
                cp = copy(i, 4 + j, (*chip, c), sibling)
                cp.start()
                passed.append(cp)
        for i in range(n):
            copy(i, 0, sibling, me).wait_recv()
            for j, chip in enumerate(chips):
                copy(i, 4 + j, (*chip, 1 - c), me).wait_recv()
        for cp in first + passed:
            cp.wait_send()
        for cp in local:
            cp.wait()

    any_spec = pl.BlockSpec(memory_space=pl.ANY)
    res = pl.pallas_call(
        body, name=name,
        in_specs=[any_spec] * n, out_specs=[any_spec] * n,
        out_shape=[jax.ShapeDtypeStruct((N_DEV,) + a.shape, a.dtype) for a in arrs],
        scratch_shapes=[pltpu.SemaphoreType.DMA((7 * n,)), pltpu.SemaphoreType.DMA((7 * n,)),
                        pltpu.SemaphoreType.DMA((n,))],
        compiler_params=_cparams(),
    )(*arrs)
    return list(res)


def all_to_all(arrs, name):
    return _exchange(arrs, name, True)


_HBM = pl.BlockSpec(memory_space=pltpu.HBM)
_SEM = pl.BlockSpec(memory_space=pltpu.SEMAPHORE)
_EFFECT = pltpu.SideEffectType.DATAFLOW_SIDE_EFFECTING


def _split_copies(ins, lands, send, recv, scatter, waiting):
    me = _mesh_pos()
    mine = _index(me)
    copies = []
    for i in range(len(ins)):
        for f in range(1, N_DEV):
            peer = _flip(me, f)
            src = ins[i].at[_index(peer)] if scatter else ins[i]
            copies.append(pltpu.make_async_remote_copy(
                src_ref=src, dst_ref=lands[i].at[_index(peer) if waiting else mine],
                send_sem=send.at[i * 7 + f - 1], recv_sem=recv.at[i * 7 + f - 1],
                device_id=peer, device_id_type=pl.DeviceIdType.MESH))
    return copies


def exchange_start(arrs, name, scatter, after=()):
    n = len(arrs)
    land_shapes = [a.shape if scatter else (N_DEV,) + a.shape for a in arrs]

    def body(*refs):
        ins, lands = refs[:n], refs[n:2 * n]
        send, recv = refs[2 * n + len(after)], refs[2 * n + len(after) + 1]
        token = refs[-1]
        for cp in _split_copies(ins, lands, send, recv, scatter, False):
            cp.start()
        token[...] = jnp.zeros_like(token)

    res = pl.pallas_call(
        body, name=name,
        out_shape=(pltpu.SemaphoreType.DMA((7 * n,)), pltpu.SemaphoreType.DMA((7 * n,)),
                   *[pltpu.HBM(a.shape, a.dtype) for a in arrs],
                   *[pltpu.HBM(s, a.dtype) for s, a in zip(land_shapes, arrs)],
                   jax.ShapeDtypeStruct((8, 128), F32)),
        in_specs=[_HBM] * (2 * n) + [pl.BlockSpec(memory_space=pl.ANY)] * len(after),
        out_specs=(_SEM, _SEM, *[_HBM] * (2 * n), pl.BlockSpec(memory_space=pltpu.VMEM)),
        input_output_aliases={i: 2 + i for i in range(2 * n)},
        compiler_params=pltpu.CompilerParams(has_side_effects=_EFFECT),
    )(*[pltpu.with_memory_space_constraint(a, pltpu.HBM) for a in arrs],
      *[pltpu.with_memory_space_constraint(lax.empty(s, a.dtype), pltpu.HBM) for s, a in zip(land_shapes, arrs)],
      *after)
    return (res[0], res[1], list(res[2:2 + n]), list(res[2 + n:2 + 2 * n])), res[-1]


def exchange_wait(handle, after, name, scatter):
    send, recv, srcs, lands = handle
    n = len(srcs)

    def body(*refs):
        ins, lnd = refs[:n], refs[n:2 * n]
        send_, recv_ = refs[2 * n], refs[2 * n + 1]
        for cp in _split_copies(ins, lnd, send_, recv_, scatter, True):
            cp.wait_send()
            cp.wait_recv()

    res = pl.pallas_call(
        body, name=name,
        out_shape=(*[pltpu.HBM(a.shape, a.dtype) for a in srcs], *[pltpu.HBM(a.shape, a.dtype) for a in lands]),
        in_specs=[_HBM] * (2 * n) + [_SEM, _SEM, pl.BlockSpec(memory_space=pl.ANY)],
        out_specs=tuple([_HBM] * (2 * n)),
        input_output_aliases={i: i for i in range(2 * n)},
        compiler_params=pltpu.CompilerParams(has_side_effects=_EFFECT),
    )(*srcs, *lands, send, recv, after)
    return list(res[n:])


def _own_slot(land, own):
    return lax.dynamic_update_slice(land, own[None], (_index(_mesh_pos()),) + (0,) * own.ndim)


def cast_bf16(arrs, name):
    outs = []
    for i, a in enumerate(arrs):
        R, C = a.shape
        if R % 8 == 0:
            tr = _pick(R, (256, 128, 64, 32, 16, 8)) if R * C * 4 > (1 << 21) else R
            steps, spec = R // tr, pl.BlockSpec((tr, C), lambda i: (i, 0))
        else:
            steps, spec = C // 256, pl.BlockSpec((R, 256), lambda i: (0, i))

        def body(a_ref, o_ref):
            o_ref[...] = a_ref[...].astype(BF16)

        outs.append(pl.pallas_call(body, name=f"{name}_{i}", grid=(steps,), in_specs=[spec], out_specs=spec,
                                   out_shape=jax.ShapeDtypeStruct((R, C), BF16), compiler_params=_cparams())(a))
    return outs


def sum_parts(parts, name):
    def fn(p):
        g = p[0]
        for k in range(1, N_DEV):
            g = g + p[k]
        return (g,)
    return small_call(fn, [parts], [jax.ShapeDtypeStruct(parts.shape[1:], F32)], name)[0]


_SPLITS = np.cumsum([1536, 1536, 1536, 512, 512, 1024, 1024, 8, 8, 2048])[:-1].tolist()


def split_w_in(w):
    aq, ak, av, mq, mk, mv, mo, mi, mf, gates = jnp.split(w, _SPLITS, axis=1)
    R = w.shape[0]
    w_att = jnp.stack([aq.reshape(R, 12, 128), ak.reshape(R, 12, 128), av.reshape(R, 12, 128)], axis=2)
    gif = jnp.concatenate([mi.reshape(R, 4, 2), mf.reshape(R, 4, 2), jnp.zeros((R, 4, 124), w.dtype)], axis=2)
    w_ml = jnp.concatenate([mq.reshape(R, 4, 128), mk.reshape(R, 4, 128), mv.reshape(R, 4, 256),
                            mo.reshape(R, 4, 256), gif], axis=2)
    return w_att.reshape(R, ATT_COLS), w_ml.reshape(R, ML_COLS), gates


def merge_w_in(g_att, g_ml, g_gate):
    R = g_att.shape[0]
    a = g_att.reshape(R, 12, 3, 128)
    m = g_ml.reshape(R, 4, ML_PAIR_COLS)
    gif = m[:, :, 768:772]
    return jnp.concatenate([
        a[:, :, 0].reshape(R, 1536), a[:, :, 1].reshape(R, 1536), a[:, :, 2].reshape(R, 1536),
        m[:, :, 0:128].reshape(R, 512), m[:, :, 128:256].reshape(R, 512),
        m[:, :, 256:512].reshape(R, 1024), m[:, :, 512:768].reshape(R, 1024),
        gif[:, :, 0:2].reshape(R, 8), gif[:, :, 2:4].reshape(R, 8), g_gate], axis=1)


def _blk8(v, width=128):
    r = v.shape[0]
    return v.reshape(r, 1024 // width, width).transpose(1, 0, 2)


def _unblk8(v):
    nb, r, w = v.shape
    return v.transpose(1, 0, 2).reshape(r, nb * w)


def local_step(x, target, mods, w, small, late_w=None, early_g=None, w_in_g=None):
    late_w = late_w or (lambda after: w)
    big = {}
    early_g = early_g or (lambda g: big.update(g))
    w_in_g = w_in_g or (lambda g: big.update(w_in=g))
    B = x.shape[0]
    T = B * SEQ
    shift1, scale1, gate1, shift2, scale2, gate2 = mods
    f2 = lambda a: a.reshape(T, a.shape[-1])
    f3 = lambda a: a.reshape(B, SEQ, a.shape[-1])

    rel_t = jnp.pad(small["rel_bias"].T, ((0, 4), (0, 0)))
    onehots = [_bucket_onehot(d) for _, d in ATT_GROUPS]
    biases = [bias_expand(rel_t, oh, f"bias_expand{g}").reshape(16, ATT_BLOCK, 2 * ATT_BLOCK)
              for g, oh in enumerate(onehots)]
    qg, kg = small["q_norm_g"], small["k_norm_g"]
    cw8 = _blk8(small["conv_w"])
    cb8 = _blk8(small["conv_b"])
    b_if = small["b_if"].reshape(2, 4, 2)
    bifp = jnp.concatenate([b_if[0], b_if[1], jnp.zeros((4, 124), F32)], axis=1).reshape(4, 1, 128)
    gn4 = small["mlstm_norm_g"].reshape(4, 1, 256)

    u = modnorm_fwd(x, small["norm1_g"], scale1, shift1, "modnorm1")
    u2d = f2(u)
    pa = f3(matmul(u2d, w["w_att"], mode="nn", name="proj_att"))
    pm = f3(matmul(u2d, w["w_ml"], mode="nn", name="proj_ml"))
    pg = matmul(u2d, w["w_gate"], mode="nn", name="proj_gate")
    os_, ls_ = [], []
    one_block = [SEQ // d == ATT_BLOCK for _, d in ATT_GROUPS]
    for g in range(3):
        o, l = (attn_fwd_classes if one_block[g] else attn_fwd)(pa, biases[g], qg, kg, g, f"attn_fwd{g}")
        os_.append(f2(o))
        ls_.append(f2(l))
    hg = mlstm_fwd(pm, cw8, cb8, bifp, gn4, "mlstm_fwd")
    w = {**w, **late_w(hg)}
    att, y_att = merge_att_out(os_, ls_, w["w_att_out"], "att_out")
    y_ml, z = ml_out_gate(f2(hg), w["w_ml_out"], pg, y_att, "ml_out")
    y, x1, u2 = out_proj_resid_modnorm(f2(z), w["w_out"], f2(x), gate1, small["norm2_g"], scale2, shift2, "out_proj")
    pre, hdn = matmul(u2, w["w_ff1"], mode="nn", name="ff1", out_dtypes=(BF16, BF16),
                      epi=lambda acc: (acc, jnp.square(jnp.maximum(acc, 0.0))))
    dx2, d_ffo, loss, d_gate2 = ff2_loss(hdn, w["w_ff2"], x1, gate2, f2(target), "ff2_loss")

    g_ff2 = matmul(hdn, d_ffo, mode="tn", name="g_ff2", out_dtypes=(BF16,))
    d_pre = matmul(d_ffo, w["w_ff2"], mode="nt", name="d_hdn", out_dtypes=(BF16,), extras=(pre,),
                   epi=lambda acc, p: (acc * (2.0 * jnp.maximum(p.astype(F32), 0.0)),))
    g_ff1 = matmul(u2, d_pre, mode="tn", name="g_ff1", out_dtypes=(BF16,))
    dx1, d_norm2, d_scale2, d_shift2, dy, d_gate1 = d_u_modnorm_bwd(
        d_pre, w["w_ff1"], None, x1, small["norm2_g"], scale2, shift2, dx2, y, gate1, "d_u2")
    g_out = matmul(f2(z), dy, mode="tn", name="g_out", out_dtypes=(BF16,))
    dpg, d_ya, d_ym = d_z_gate_bwd(dy, w["w_out"], pg, y_att, y_ml, "d_z")
    g_att_out = matmul(f2(att), f2(d_ya), mode="tn", name="g_att_out", out_dtypes=(BF16,))
    dmerge = d_att_merge_bwd(d_ya, w["w_att_out"], os_, ls_, "d_att")
    g_ml_out = matmul(f2(hg), f2(d_ym), mode="tn", name="g_ml_out", out_dtypes=(BF16,))
    d_hg = matmul(f2(d_ym), w["w_ml_out"], mode="nt", name="d_hg")
    started = early_g(dict(w_att_out=g_att_out, w_ml_out=g_ml_out, w_out=g_out, w_ff1=g_ff1, w_ff2=g_ff2))
    order = 0.0 if started is None else started[0, 0]
    dmerge = [f3(d) for d in dmerge]
    dpa = lax.empty((B, SEQ, ATT_COLS), BF16)
    d_rel = []
    d_qg = d_kg = None
    for g in range(3):
        dpa, dbias, dq_g, dk_g = (attn_bwd_classes if one_block[g] else attn_bwd)(
            pa, biases[g], qg + order, kg, dmerge[g], dmerge[3 + g], dpa, g, f"attn_bwd{g}")
        db8 = jnp.pad(dbias.reshape(4, -1), ((0, 4), (0, 0)))
        d_rel.append(bias_reduce(db8, onehots[g], f"bias_reduce{g}")[:4])
        d_qg = dq_g if d_qg is None else d_qg + dq_g
        d_kg = dk_g if d_kg is None else d_kg + dk_g
    dpm, dcw8, dcb8, dbifp, dgn4 = mlstm_bwd(pm, cw8, cb8, bifp, gn4 + order, f3(d_hg), "mlstm_bwd")
    g_w_att = matmul(u2d, f2(dpa), mode="tn", name="g_w_att", out_dtypes=(BF16,))
    g_w_ml = matmul(u2d, f2(dpm), mode="tn", name="g_w_ml", out_dtypes=(BF16,))
    g_w_gate = matmul(u2d, f2(dpg), mode="tn", name="g_w_gate", out_dtypes=(BF16,))
    started = w_in_g(merge_w_in(g_w_att, g_w_ml, g_w_gate))
    du = matmul(f2(dpa), w["w_att"], mode="nt", name="d_u_att", after=() if started is None else (started,))
    du = matmul(f2(dpm), w["w_ml"], mode="nt", name="d_u_ml", extras=(du,), epi=lambda acc, e: (acc + e,))
    grad_x, d_norm1, d_scale1, d_shift1 = d_u_modnorm_bwd(
        f2(dpg), w["w_gate"], du, f2(x), small["norm1_g"], scale1, shift1, dx1, None, None, "d_u_gate")
    grad_x = f3(grad_x)

    d_mods = (d_shift1, d_scale1, d_gate1, d_shift2, d_scale2, d_gate2)
    dbif = dbifp.reshape(4, 128)
    small_g = dict(
        norm1_g=d_norm1, norm2_g=d_norm2,
        b_if=jnp.stack([dbif[:, 0:2].reshape(8), dbif[:, 2:4].reshape(8)]),
        conv_w=_unblk8(dcw8), conv_b=_unblk8(dcb8), q_norm_g=d_qg, k_norm_g=d_kg,
        rel_bias=jnp.concatenate(d_rel, axis=0).T,
        mlstm_norm_g=dgn4.reshape(1, 1024))
    return loss, grad_x, d_mods, big, small_g


_SMALL = (("b_ada", 6144), ("norm1_g", 1024), ("norm2_g", 1024), ("b_if", 16), ("conv_b", 1024),
          ("q_norm_g", 128), ("k_norm_g", 128), ("rel_bias", 384), ("mlstm_norm_g", 1024), ("conv_w", 4096))
_SMALL_ROWS = 120
_REPL = _SMALL[:-1]
_SMALL_SENT = _SMALL + (("loss", 1),)


def _pack(d, names, rows):
    flat = jnp.concatenate([d[k].reshape(-1) for k, _ in names])
    return jnp.pad(flat, (0, rows * 128 - flat.shape[0])).reshape(rows, 128)


def _unpack(slab, names, shapes):
    flat = slab.reshape(-1)
    out, off = {}, 0
    for k, nel in names:
        out[k] = flat[off:off + nel].reshape(shapes[k])
        off += nel
    return out


def kernel(x, c, w_ada, b_ada, norm1_g, norm2_g, w_in, b_if, conv_w, conv_b, q_norm_g, k_norm_g, rel_bias, mlstm_norm_g, w_att_out, w_ml_out, w_out, w_ff1, w_ff2, loss_target, m_w_ada, m_b_ada, m_norm1_g, m_norm2_g, m_w_in, m_b_if, m_conv_w, m_conv_b, m_q_norm_g, m_k_norm_g, m_rel_bias, m_mlstm_norm_g, m_w_att_out, m_w_ml_out, m_w_out, m_w_ff1, m_w_ff2, v_w_ada, v_b_ada, v_norm1_g, v_norm2_g, v_w_in, v_b_if, v_conv_w, v_conv_b, v_q_norm_g, v_k_norm_g, v_rel_bias, v_mlstm_norm_g, v_w_att_out, v_w_ml_out, v_w_out, v_w_ff1, v_w_ff2):
    P = dict(w_ada=w_ada, b_ada=b_ada, norm1_g=norm1_g, norm2_g=norm2_g, w_in=w_in, b_if=b_if, conv_w=conv_w,
             conv_b=conv_b, q_norm_g=q_norm_g, k_norm_g=k_norm_g, rel_bias=rel_bias, mlstm_norm_g=mlstm_norm_g,
             w_att_out=w_att_out, w_ml_out=w_ml_out, w_out=w_out, w_ff1=w_ff1, w_ff2=w_ff2)
    M = dict(w_ada=m_w_ada, b_ada=m_b_ada, norm1_g=m_norm1_g, norm2_g=m_norm2_g, w_in=m_w_in, b_if=m_b_if,
             conv_w=m_conv_w, conv_b=m_conv_b, q_norm_g=m_q_norm_g, k_norm_g=m_k_norm_g, rel_bias=m_rel_bias,
             mlstm_norm_g=m_mlstm_norm_g, w_att_out=m_w_att_out, w_ml_out=m_w_ml_out, w_out=m_w_out,
             w_ff1=m_w_ff1, w_ff2=m_w_ff2)
    V = dict(w_ada=v_w_ada, b_ada=v_b_ada, norm1_g=v_norm1_g, norm2_g=v_norm2_g, w_in=v_w_in, b_if=v_b_if,
             conv_w=v_conv_w, conv_b=v_conv_b, q_norm_g=v_q_norm_g, k_norm_g=v_k_norm_g, rel_bias=v_rel_bias,
             mlstm_norm_g=v_mlstm_norm_g, w_att_out=v_w_att_out, w_ml_out=v_w_ml_out, w_out=v_w_out,
             w_ff1=v_w_ff1, w_ff2=v_w_ff2)
    names = list(P)
    shapes = {k: P[k].shape for k in names}
    B = x.shape[0]
    me = _index(_mesh_pos())

    big_names = ("w_in", "w_att_out", "w_ml_out", "w_out", "w_ff1", "w_ff2")
    shards = cast_bf16([P[k][0] for k in big_names], "cast_w")
    w_in_g8, c8, conv_w8 = all_gather_two_level([shards[0], c, conv_w[0]], "gather_w_in")
    c_all = c8.reshape(N_DEV * B, D_MODEL)
    conv_w_full = conv_w8.transpose(1, 0, 2).reshape(4, 1024)
    w_att, w_ml, w_gate = split_w_in(w_in_g8.transpose(1, 0, 2).reshape(D_MODEL, D_IN))
    w = dict(w_att=w_att, w_ml=w_ml, w_gate=w_gate)

    (silu_c,) = small_call(lambda a: (_silu(a),), [c_all], [jax.ShapeDtypeStruct(c_all.shape, F32)], "silu_c")
    b_ada_cols = lax.dynamic_slice(b_ada, (0, me * 768), (1, 768))
    ada_cols = matmul(silu_c, w_ada[0], mode="nn", name="ada", extras=(jnp.broadcast_to(b_ada_cols, (N_DEV * B, 768)),),
                      epi=lambda acc, bb: (acc + bb,))
    (ada_t,) = all_to_all([ada_cols.reshape(N_DEV, B, 768)], "ada_exchange")
    ada = ada_t.transpose(1, 0, 2).reshape(B, 6 * D_MODEL)
    mods = tuple(ada[:, i * D_MODEL:(i + 1) * D_MODEL].reshape(B, 1, D_MODEL) for i in range(6))

    late_handle, late_order = exchange_start(shards[1:], "gather_late_start", False, after=(ada_t,))

    def late_w(after):
        lands = exchange_wait(late_handle, after, "gather_late_wait", False)
        gw = dict(zip(big_names[1:], [_own_slot(l, s) for l, s in zip(lands, shards[1:])]))
        return dict(w_att_out=gw["w_att_out"].transpose(1, 0, 2).reshape(512, D_MODEL),
                    w_ml_out=gw["w_ml_out"].reshape(D_MODEL, D_MODEL), w_out=gw["w_out"].reshape(D_MODEL, D_MODEL),
                    w_ff1=gw["w_ff1"].transpose(1, 0, 2).reshape(D_MODEL, D_FF),
                    w_ff2=gw["w_ff2"].reshape(D_FF, D_MODEL))

    pending = {}

    def send_grads(key, blocks, name):
        handle, order = exchange_start(blocks, name, True)
        pending[key] = (handle, [lax.dynamic_index_in_dim(b, me, 0, keepdims=False) for b in blocks])
        return order

    def early_g(g):
        return send_grads("late", [g["w_att_out"].reshape(512, N_DEV, 128).transpose(1, 0, 2),
                                   g["w_ml_out"].reshape(N_DEV, 128, D_MODEL), g["w_out"].reshape(N_DEV, 128, D_MODEL),
                                   g["w_ff1"].reshape(D_MODEL, N_DEV, 512).transpose(1, 0, 2),
                                   g["w_ff2"].reshape(N_DEV, 512, D_MODEL)], "grad_late_start")

    def w_in_g(g):
        return send_grads("w_in", [g.reshape(D_MODEL, N_DEV, W_IN_SHARD).transpose(1, 0, 2)], "grad_w_in_start")

    def recv_grads(key, after, name):
        handle, own = pending[key]
        return [_own_slot(l, o) for l, o in zip(exchange_wait(handle, after, name, True), own)]

    small = dict(norm1_g=norm1_g + late_order[0, 0], norm2_g=norm2_g, b_if=b_if[0], conv_w=conv_w_full, conv_b=conv_b,
                 q_norm_g=q_norm_g, k_norm_g=k_norm_g, rel_bias=rel_bias, mlstm_norm_g=mlstm_norm_g)
    loss, grad_x, d_mods, _, small_g = local_step(x, loss_target, mods, w, small, late_w, early_g, w_in_g)

    d_ada = jnp.concatenate([d.reshape(B, D_MODEL) for d in d_mods], axis=1)
    (small_g["b_ada"],) = small_call(lambda a: (jnp.sum(a, axis=0, keepdims=True),), [d_ada],
                                     [jax.ShapeDtypeStruct((1, 6144), F32)], "g_b_ada_local")
    small_g["loss"] = loss
    d_ada_t, small_parts = _exchange([d_ada.reshape(B, N_DEV, 768).transpose(1, 0, 2),
                                      _pack(small_g, _SMALL_SENT, _SMALL_ROWS)], "small_exchange", [True, False])
    d_ada_cols = d_ada_t.reshape(N_DEV * B, 768)
    g_w_ada = matmul(silu_c, d_ada_cols, mode="tn", name="g_w_ada")

    recv = recv_grads("w_in", grad_x, "grad_w_in_wait") + recv_grads("late", grad_x, "grad_late_wait")
    small_sum = sum_parts(small_parts, "small_grad_sum")
    sg = _unpack(small_sum, _SMALL_SENT, {**{k: shapes[k] for k, _ in _REPL}, "conv_w": (4, 1024), "loss": ()})
    loss = sg["loss"]

    G, Dl, NM, NV = {}, {}, {}, {}
    for k, parts in zip(big_names, recv):
        g, d, nm, nv = adamw(P[k][0], parts, M[k][0], V[k][0], f"adamw_{k}", parts=True)
        G[k], Dl[k], NM[k], NV[k] = g[None], d[None], nm[None], nv[None]
    g, d, nm, nv = adamw(w_ada[0], g_w_ada, m_w_ada[0], v_w_ada[0], "adamw_w_ada")
    G["w_ada"], Dl["w_ada"], NM["w_ada"], NV["w_ada"] = g[None], d[None], nm[None], nv[None]
    g_conv = lax.dynamic_slice(sg["conv_w"], (0, me * 128), (4, 128))
    g, d, nm, nv = adamw(conv_w[0], g_conv, m_conv_w[0], v_conv_w[0], "adamw_conv_w")
    G["conv_w"], Dl["conv_w"], NM["conv_w"], NV["conv_w"] = g[None], d[None], nm[None], nv[None]
    flat2 = lambda a: a.reshape(-1, a.shape[-1])
    keys = [k for k, _ in _REPL]
    upd = adamw_many([flat2(P[k]) for k in keys], [flat2(sg[k]) for k in keys], [flat2(M[k]) for k in keys],
                     [flat2(V[k]) for k in keys], "adamw_small")
    for j, k in enumerate(keys):
        G[k] = sg[k]
        Dl[k], NM[k], NV[k] = [upd[3 * j + t].reshape(shapes[k]) for t in range(3)]

    return (loss, grad_x, *[G[k] for k in names], *[Dl[k] for k in names], *[NM[k] for k in names],
            *[NV[k] for k in names])
```

```python
import functools
import math

import numpy as np
import jax
import jax.numpy as jnp
from jax import lax
from jax.experimental import pallas as pl
from jax.experimental.pallas import tpu as pltpu

F32 = jnp.float32
BF16 = jnp.bfloat16

N_DEV = 8
D_MODEL = 1024
SEQ = 2048
ATT_GROUPS = ((128, 1), (512, 4), (2048, 16))
N_ATT_HEADS = 12
ATT_BLOCK = 128
HEAD_DIM = 128
ML_HEADS = 8
ML_PAIRS = 4
ML_CHUNK = 64
N_CHUNKS = SEQ // ML_CHUNK
N_BUCKETS = 32
MAX_DISTANCE = 2048
D_FF = 4096
D_IN = 9744
EPS = 1e-6

ADAM_LR = 0.001
ADAM_B1 = 0.9
ADAM_B2 = 0.999
ADAM_EPS = 1e-08
ADAM_WD = 0.01
ADAM_STEP = 10

ATT_HEAD_COLS = 3 * HEAD_DIM
ATT_COLS = N_ATT_HEADS * ATT_HEAD_COLS
ML_PAIR_COLS = 896
ML_COLS = ML_PAIRS * ML_PAIR_COLS
GATE_COLS = 2 * D_MODEL
W_IN_SHARD = D_IN // N_DEV

VMEM_LIMIT = 60 * 1024 * 1024


def _cparams(**kw):
    return pltpu.CompilerParams(vmem_limit_bytes=VMEM_LIMIT, **kw)


_NN = ((1,), (0,))
_NT = ((1,), (1,))
_TN = ((0,), (0,))


def _mxu(a, b, dims):
    return lax.dot_general(a.astype(BF16), b.astype(BF16), (dims, ((), ())), preferred_element_type=F32)


@jax.custom_vjp
def bdot_nn(a, b):
    return _mxu(a, b, _NN)


def _nn_fwd(a, b):
    return _mxu(a, b, _NN), (a, b)


def _nn_bwd(res, g):
    a, b = res
    return _mxu(g, b, _NT), _mxu(a, g, _TN)


bdot_nn.defvjp(_nn_fwd, _nn_bwd)


@jax.custom_vjp
def bdot_nt(a, b):
    return _mxu(a, b, _NT)


def _nt_fwd(a, b):
    return _mxu(a, b, _NT), (a, b)


def _nt_bwd(res, g):
    a, b = res
    return _mxu(g, b, _NN), _mxu(g, a, _TN)


bdot_nt.defvjp(_nt_fwd, _nt_bwd)


def _doth(a, b, dims=_NN):
    return lax.dot_general(a, b, (dims, ((), ())), precision=lax.Precision.HIGHEST, preferred_element_type=F32)


def _rms(x):
    return x * lax.rsqrt(jnp.mean(x * x, axis=-1, keepdims=True) + EPS)


def _pick(n, cands):
    for t in cands:
        if n % t == 0:
            return t
    raise ValueError(f"no tile for {n}")


MM_TILE_M = (1024, 512, 256, 128, 64, 32, 16, 8)
MM_TILE_N = (2048, 1792, 1536, 1024, 768, 512, 256, 128)
MM_TILE_K = (2048, 1792, 1536, 1024, 512, 256, 128, 64, 32)

def matmul(a, b, *, mode, name, out_dtypes=(F32,), epi=None, extras=(), after=()):
    if mode == "nn":
        (M, K), (K2, N) = a.shape, b.shape
    elif mode == "nt":
        (M, K), (N, K2) = a.shape, b.shape
    else:
        (K, M), (K2, N) = a.shape, b.shape
    assert K == K2, (a.shape, b.shape, mode)
    tm = _pick(M, MM_TILE_M)
    tn = _pick(N, MM_TILE_N)
    tk = _pick(K, MM_TILE_K)
    nk = K // tk
    n_ex = len(extras)
    n_out = len(out_dtypes)
    dims = {"nn": _NN, "nt": _NT, "tn": _TN}[mode]

    def finish(r, ex_refs, out_refs):
        outs = epi(r, *[e[...] for e in ex_refs]) if epi is not None else (r,)
        for o_ref, o in zip(out_refs, outs):
            o_ref[...] = o.astype(o_ref.dtype)

    def body(*refs):
        a_ref, b_ref = refs[0], refs[1]
        ex_refs = refs[2:2 + n_ex]
        out_refs = refs[2 + n_ex + len(after):2 + n_ex + len(after) + n_out]
        if nk == 1:
            finish(_mxu(a_ref[...], b_ref[...], dims), ex_refs, out_refs)
            return
        acc = refs[2 + n_ex + len(after) + n_out]
        k = pl.program_id(2)

        @pl.when(k == 0)
        def _():
            acc[...] = jnp.zeros_like(acc)

        acc[...] += _mxu(a_ref[...], b_ref[...], dims)

        @pl.when(k == nk - 1)
        def _():
            finish(acc[...], ex_refs, out_refs)

    if mode == "nn":
        a_spec = pl.BlockSpec((tm, tk), lambda i, j, k: (i, k))
        b_spec = pl.BlockSpec((tk, tn), lambda i, j, k: (k, j))
    elif mode == "nt":
        a_spec = pl.BlockSpec((tm, tk), lambda i, j, k: (i, k))
        b_spec = pl.BlockSpec((tn, tk), lambda i, j, k: (j, k))
    else:
        a_spec = pl.BlockSpec((tk, tm), lambda i, j, k: (k, i))
        b_spec = pl.BlockSpec((tk, tn), lambda i, j, k: (k, j))
    o_spec = pl.BlockSpec((tm, tn), lambda i, j, k: (i, j))
    res = pl.pallas_call(
        body,
        name=name,
        grid=(M // tm, N // tn, nk),
        in_specs=[a_spec, b_spec] + [o_spec] * n_ex + [pl.BlockSpec(memory_space=pl.ANY)] * len(after),
        out_specs=[o_spec] * n_out,
        out_shape=[jax.ShapeDtypeStruct((M, N), dt) for dt in out_dtypes],
        scratch_shapes=[pltpu.VMEM((tm, tn), F32)] if nk > 1 else [],
        compiler_params=_cparams(),
    )(a, b, *extras, *after)
    return res[0] if n_out == 1 else tuple(res)


ROW_MM_TILE = 512


def row_matmul(a, b, *, mode, name, extras, outs, epi):
    (M, K) = a.shape
    N = b.shape[1] if mode == "nn" else b.shape[0]
    tm = ROW_MM_TILE
    tk = _pick(K, MM_TILE_K)
    nk = K // tk
    n_ex, n_out = len(extras), len(outs)

    def body(*refs):
        a_ref, b_ref = refs[0], refs[1]
        ex_refs = refs[2:2 + n_ex]
        out_refs = refs[2 + n_ex:2 + n_ex + n_out]
        i = pl.program_id(0)
        dims = _NN if mode == "nn" else _NT
        if nk == 1:
            epi(_mxu(a_ref[...], b_ref[...], dims), i, ex_refs, out_refs)
            return
        acc = refs[2 + n_ex + n_out]
        k = pl.program_id(1)

        @pl.when(k == 0)
        def _():
            acc[...] = jnp.zeros_like(acc)

        acc[...] += _mxu(a_ref[...], b_ref[...], dims)

        @pl.when(k == nk - 1)
        def _():
            epi(acc[...], i, ex_refs, out_refs)

    def lift(index_map):
        return lambda i, k: index_map(i)

    b_spec = pl.BlockSpec((tk, N), lambda i, k: (k, 0)) if mode == "nn" else pl.BlockSpec((N, tk), lambda i, k: (0, k))
    res = pl.pallas_call(
        body, name=name, grid=(M // tm, nk),
        in_specs=[pl.BlockSpec((tm, tk), lambda i, k: (i, k)), b_spec]
        + [pl.BlockSpec(blk, lift(im)) for _, blk, im in extras],
        out_specs=[pl.BlockSpec(blk, lift(im)) for _, _, blk, im in outs],
        out_shape=[jax.ShapeDtypeStruct(shape, dt) for shape, dt, _, _ in outs],
        scratch_shapes=[pltpu.VMEM((tm, N), F32)] if nk > 1 else [],
        compiler_params=_cparams(),
    )(a, b, *[e[0] for e in extras])
    return tuple(res)


def _rows(arr):
    return (arr, (ROW_MM_TILE, arr.shape[1]), lambda i: (i, 0))


def _rows_out(T, dtype):
    return ((T, D_MODEL), dtype, (ROW_MM_TILE, D_MODEL), lambda i: (i, 0))


def _per_seq(arr):
    return (arr, (1, 1, D_MODEL), lambda i: (i // (SEQ // ROW_MM_TILE), 0, 0))


def _per_seq_out(B):
    return ((B, 1, D_MODEL), F32, (1, 1, D_MODEL), lambda i: (i // (SEQ // ROW_MM_TILE), 0, 0))


def _first_tile_of_seq(i):
    return i % (SEQ // ROW_MM_TILE) == 0


def small_call(fn, inputs, out_shapes, name):
    n_in = len(inputs)

    def body(*refs):
        outs = fn(*[r[...] for r in refs[:n_in]])
        for o_ref, o in zip(refs[n_in:], outs):
            o_ref[...] = o.astype(o_ref.dtype)

    res = pl.pallas_call(body, name=name, out_shape=list(out_shapes), compiler_params=_cparams())(*inputs)
    return tuple(res)


ROW_TILE = 512


def _modnorm(x, g, scale, shift):
    return _rms(x) * g * (1.0 + scale) + shift


def _row_spec(width):
    return pl.BlockSpec((1, ROW_TILE, width), lambda b, i: (b, i, 0))


def _mod_spec():
    return pl.BlockSpec((1, 1, D_MODEL), lambda b, i: (b, 0, 0))


def _vec_spec():
    return pl.BlockSpec((1, D_MODEL), lambda b, i: (0, 0))


def modnorm_fwd(x, g, scale, shift, name):
    B, S, D = x.shape

    def body(x_ref, g_ref, sc_ref, sh_ref, u_ref):
        u_ref[0] = _modnorm(x_ref[0], g_ref[...], sc_ref[0], sh_ref[0]).astype(BF16)

    return pl.pallas_call(
        body, name=name, grid=(B, S // ROW_TILE),
        in_specs=[_row_spec(D), _vec_spec(), _mod_spec(), _mod_spec()],
        out_specs=_row_spec(D),
        out_shape=jax.ShapeDtypeStruct((B, S, D), BF16),
        compiler_params=_cparams(),
    )(x, g, scale, shift)


def _gain_spec(g):
    return (g, (1, D_MODEL), lambda i: (0, 0))


def out_proj_resid_modnorm(z, w_out, x, gate, g, scale, shift, name):
    T = z.shape[0]

    def epi(acc, i, ex, out):
        x_ref, gt_ref, g_ref, sc_ref, sh_ref = ex
        y_ref, x1_ref, u_ref = out
        y_ref[...] = acc
        x1 = x_ref[...] + gt_ref[0] * acc
        x1_ref[...] = x1
        u_ref[...] = _modnorm(x1, g_ref[...], sc_ref[0], sh_ref[0]).astype(BF16)

    return row_matmul(z, w_out, mode="nn", name=name,
                      extras=[_rows(x), _per_seq(gate), _gain_spec(g), _per_seq(scale), _per_seq(shift)],
                      outs=[_rows_out(T, F32), _rows_out(T, F32), _rows_out(T, BF16)], epi=epi)


def ff2_loss(hdn, w_ff2, x1, gate, target, name):
    T = hdn.shape[0]

    def epi(acc, i, ex, out):
        x_ref, gt_ref, t_ref = ex
        dx_ref, dffo_ref, loss_ref, dg_ref = out

        @pl.when(i == 0)
        def _():
            loss_ref[...] = jnp.zeros_like(loss_ref)

        @pl.when(_first_tile_of_seq(i))
        def _():
            dg_ref[...] = jnp.zeros_like(dg_ref)

        err = x_ref[...] + gt_ref[0] * acc - t_ref[...]
        dx = err * (1.0 / D_MODEL)
        dx_ref[...] = dx
        dffo_ref[...] = (gt_ref[0] * dx).astype(BF16)
        loss_ref[...] += 0.5 * jnp.sum(jnp.mean(err * err, axis=-1, keepdims=True), axis=0, keepdims=True)
        dg_ref[0] += jnp.sum(dx * acc, axis=0, keepdims=True)

    return row_matmul(hdn, w_ff2, mode="nn", name=name,
                      extras=[_rows(x1), _per_seq(gate), _rows(target)],
                      outs=[_rows_out(T, F32), _rows_out(T, BF16), ((1, 1), F32, (1, 1), lambda i: (0, 0)),
                            _per_seq_out(T // SEQ)], epi=epi)


def d_u_modnorm_bwd(a, w, du_prev, x, g, scale, shift, dx_res, y, gate, name):
    T = a.shape[0]
    B = T // SEQ
    n_prev, resid = int(du_prev is not None), y is not None

    def epi(acc, i, ex, out):
        x_ref, g_ref, sc_ref, sh_ref, dr_ref = ex[n_prev:n_prev + 5]
        dx_ref, dg_ref, dsc_ref, dsh_ref = out[:4]

        @pl.when(i == 0)
        def _():
            dg_ref[...] = jnp.zeros_like(dg_ref)

        @pl.when(_first_tile_of_seq(i))
        def _():
            for r in out[2:4] + out[5:]:
                r[...] = jnp.zeros_like(r)

        du = acc + ex[0][...] if n_prev else acc
        _, vjp = jax.vjp(_modnorm, x_ref[...], g_ref[...], sc_ref[0], sh_ref[0])
        dxn, dg, dsc, dsh = vjp(du)
        dx = dxn + dr_ref[...]
        dx_ref[...] = dx
        dg_ref[...] += dg
        dsc_ref[0] += dsc
        dsh_ref[0] += dsh
        if resid:
            y_ref, gt_ref = ex[n_prev + 5:]
            out[4][...] = (gt_ref[0] * dx).astype(BF16)
            out[5][0] += jnp.sum(dx * y_ref[...], axis=0, keepdims=True)

    extras = ([_rows(du_prev)] if n_prev else []) + [_rows(x), _gain_spec(g), _per_seq(scale), _per_seq(shift),
                                                     _rows(dx_res)] + ([_rows(y), _per_seq(gate)] if resid else [])
    outs = [_rows_out(T, F32), ((1, D_MODEL), F32, (1, D_MODEL), lambda i: (0, 0)), _per_seq_out(B), _per_seq_out(B)]
    outs += [_rows_out(T, BF16), _per_seq_out(B)] if resid else []
    return row_matmul(a, w, mode="nt", name=name, extras=extras, outs=outs, epi=epi)


def _bucket_table(dilation):
    i = np.arange(ATT_BLOCK)[:, None]
    j = np.arange(2 * ATT_BLOCK)[None, :]
    delta = ATT_BLOCK + i - j
    dist = np.maximum(delta, 0) * dilation
    max_exact = N_BUCKETS // 2
    d = np.maximum(dist, max_exact).astype(np.float32)
    large = max_exact + (np.log(d / np.float32(max_exact)) / np.float32(math.log(MAX_DISTANCE / max_exact))
                         * np.float32(N_BUCKETS - max_exact)).astype(np.int32)
    large = np.minimum(large, N_BUCKETS - 1)
    return np.where(dist < max_exact, dist, large).astype(np.int32)


def _bucket_onehot(dilation):
    bt = jnp.asarray(_bucket_table(dilation).reshape(1, -1))
    return (bt == jnp.arange(N_BUCKETS, dtype=jnp.int32)[:, None]).astype(F32)


def bias_expand(rel_t, onehot, name):
    def fn(r, oh):
        return (_doth(r, oh),)
    return small_call(fn, [rel_t, onehot], [jax.ShapeDtypeStruct((rel_t.shape[0], onehot.shape[1]), F32)], name)[0]


def bias_reduce(dbias_flat, onehot, name):
    def fn(db, oh):
        return (_doth(db, oh, _NT),)
    return small_call(fn, [dbias_flat, onehot], [jax.ShapeDtypeStruct((dbias_flat.shape[0], N_BUCKETS), F32)], name)[0]


def _qk_norm(x, g):
    return _rms(x) * g


def _masked_bias(bias):
    i = lax.broadcasted_iota(jnp.int32, (ATT_BLOCK, 2 * ATT_BLOCK), 0)
    j = lax.broadcasted_iota(jnp.int32, (ATT_BLOCK, 2 * ATT_BLOCK), 1)
    bm = jnp.where(jnp.logical_and(j >= i, j <= i + ATT_BLOCK), bias, -jnp.inf)
    return bm, bm[:, ATT_BLOCK:]


def _attn_tile(qn, kn, v, bias):
    s = bdot_nt(qn, kn) * (HEAD_DIM ** -0.5) + bias
    m = lax.stop_gradient(jnp.max(s, axis=-1, keepdims=True))
    p = jnp.exp(s - m)
    l = jnp.sum(p, axis=-1, keepdims=True)
    o = bdot_nn(p, v) / l
    lse = jnp.broadcast_to(m + jnp.log(l), (ATT_BLOCK, HEAD_DIM))
    return o, lse


def _attn_tiles(dilation, rows=SEQ):
    nb = rows // dilation // ATT_BLOCK
    return [(r, n) for r in range(dilation) for n in range(nb)]


def _attn_rows(r, n, dilation, nblk=1):
    if dilation == 1:
        return pl.ds(r + n * ATT_BLOCK, nblk * ATT_BLOCK)
    return pl.ds(r + n * ATT_BLOCK * dilation, nblk * ATT_BLOCK, stride=dilation)


_QL, _KL, _VL = slice(0, 128), slice(128, 256), slice(256, 384)


def _qkv_specs(hb):
    return [pl.BlockSpec((None, SEQ, HEAD_DIM), functools.partial(lambda b, h, j: (b, 0, 3 * (hb + h) + j), j=j))
            for j in range(3)]


def attn_fwd(pa, bias, qg, kg, group, name):
    B = pa.shape[0]
    dilation = ATT_GROUPS[group][1]
    hb = group * 4

    def body(q_ref, k_ref, v_ref, b_ref, qg_ref, kg_ref, o_ref, l_ref, qn_s, kn_s):
        qn_s[...] = _qk_norm(q_ref[...], qg_ref[...])
        kn_s[...] = _qk_norm(k_ref[...], kg_ref[...])
        bias_all, bias_first = _masked_bias(b_ref[0])
        for (r, n) in _attn_tiles(dilation):
            rows = _attn_rows(r, n, dilation)
            if n == 0:
                krows, bias_t = rows, bias_first
            else:
                krows, bias_t = _attn_rows(r, n - 1, dilation, 2), bias_all
            o, lse = _attn_tile(qn_s[rows, :], kn_s[krows, :], v_ref[krows, :], bias_t)
            o_ref[rows, :] = o
            l_ref[rows, :] = lse

    head_out = pl.BlockSpec((None, SEQ, HEAD_DIM), lambda b, h: (b, 0, h))
    return pl.pallas_call(
        body, name=name, grid=(B, 4),
        in_specs=_qkv_specs(hb) + [
                  pl.BlockSpec((1, ATT_BLOCK, 2 * ATT_BLOCK), lambda b, h: (hb + h, 0, 0)),
                  pl.BlockSpec((1, HEAD_DIM), lambda b, h: (0, 0)),
                  pl.BlockSpec((1, HEAD_DIM), lambda b, h: (0, 0))],
        out_specs=[head_out, head_out],
        out_shape=[jax.ShapeDtypeStruct((B, SEQ, 512), F32), jax.ShapeDtypeStruct((B, SEQ, 512), F32)],
        scratch_shapes=[pltpu.VMEM((SEQ, HEAD_DIM), F32)] * 2,
        compiler_params=_cparams(),
    )(pa, pa, pa, bias, qg, kg)


def attn_bwd(pa, bias, qg, kg, do, dlse, dpa, group, name):
    B = pa.shape[0]
    dilation = ATT_GROUPS[group][1]
    hb = group * 4

    def body(q_ref, k_ref, v_ref, b_ref, qg_ref, kg_ref, do_ref, dl_ref, dpa_in,
             dp_ref, db_ref, dqg_ref, dkg_ref, qn_s, kn_s, dq_s, dk_s, dv_s):
        del dpa_in
        h_id = pl.program_id(1)

        @pl.when(jnp.logical_and(pl.program_id(0) == 0, h_id == 0))
        def _():
            db_ref[...] = jnp.zeros_like(db_ref)
            dqg_ref[...] = jnp.zeros_like(dqg_ref)
            dkg_ref[...] = jnp.zeros_like(dkg_ref)

        dk_s[...] = jnp.zeros_like(dk_s)
        dv_s[...] = jnp.zeros_like(dv_s)
        qn_s[...] = _qk_norm(q_ref[...], qg_ref[...])
        kn_s[...] = _qk_norm(k_ref[...], kg_ref[...])
        bias_all, bias_first = _masked_bias(b_ref[0])
        for (r, n) in _attn_tiles(dilation):
            rows = _attn_rows(r, n, dilation)
            if n == 0:
                krows, bias_t = rows, bias_first
            else:
                krows, bias_t = _attn_rows(r, n - 1, dilation, 2), bias_all
            _, vjp = jax.vjp(_attn_tile, qn_s[rows, :], kn_s[krows, :], v_ref[krows, :], bias_t)
            dqn, dkn, dv, dbias = vjp((do_ref[rows, :], dl_ref[rows, :]))
            dq_s[rows, :] = dqn
            dk_s[krows, :] += dkn
            dv_s[krows, :] += dv
            if n == 0:
                db_ref[h_id, :, ATT_BLOCK:] += dbias
            else:
                db_ref[h_id] += dbias
        for x_ref, g_ref, d_s, dg_ref, lanes in ((q_ref, qg_ref, dq_s, dqg_ref, _QL), (k_ref, kg_ref, dk_s, dkg_ref, _KL)):
            _, vjp = jax.vjp(_qk_norm, x_ref[...], g_ref[...])
            dx, dg = vjp(d_s[...])
            dp_ref[0, :, lanes] = dx.astype(BF16)
            dg_ref[...] += dg
        dp_ref[0, :, _VL] = dv_s[...].astype(BF16)

    const2 = lambda b, h: (0, 0)
    head_in = pl.BlockSpec((None, SEQ, HEAD_DIM), lambda b, h: (b, 0, h))
    head_blk = pl.BlockSpec((1, SEQ, ATT_HEAD_COLS), lambda b, h: (b, 0, hb + h))
    return pl.pallas_call(
        body, name=name, grid=(B, 4),
        in_specs=_qkv_specs(hb) + [
                  pl.BlockSpec((1, ATT_BLOCK, 2 * ATT_BLOCK), lambda b, h: (hb + h, 0, 0)),
                  pl.BlockSpec((1, HEAD_DIM), const2), pl.BlockSpec((1, HEAD_DIM), const2),
                  head_in, head_in,
                  pl.BlockSpec(memory_space=pl.ANY)],
        out_specs=[head_blk,
                   pl.BlockSpec((4, ATT_BLOCK, 2 * ATT_BLOCK), lambda b, h: (0, 0, 0)),
                   pl.BlockSpec((1, HEAD_DIM), const2), pl.BlockSpec((1, HEAD_DIM), const2)],
        out_shape=[jax.ShapeDtypeStruct(dpa.shape, BF16),
                   jax.ShapeDtypeStruct((4, ATT_BLOCK, 2 * ATT_BLOCK), F32),
                   jax.ShapeDtypeStruct((1, HEAD_DIM), F32), jax.ShapeDtypeStruct((1, HEAD_DIM), F32)],
        scratch_shapes=[pltpu.VMEM((SEQ, HEAD_DIM), F32)] * 5,
        input_output_aliases={8: 0},
        compiler_params=_cparams(),
    )(pa, pa, pa, bias, qg, kg, do, dlse, dpa)


def _attn_classes(q, k, v, bias, qg, kg):
    s = cdot_nt(_qk_norm(q, qg), _qk_norm(k, kg)) * (HEAD_DIM ** -0.5) + bias
    m = lax.stop_gradient(jnp.max(s, axis=-1, keepdims=True))
    p = jnp.exp(s - m)
    l = jnp.sum(p, axis=-1, keepdims=True)
    o = cdot_nn(p, v) / l
    return o, jnp.broadcast_to(m + jnp.log(l), o.shape)


def _gather_classes(src_ref, dst_s, dilation):
    for r in range(dilation):
        dst_s[r] = src_ref[pl.ds(r, ATT_BLOCK, stride=dilation), :]


def _scatter_classes(src_s, dst_ref, dilation):
    for r in range(dilation):
        dst_ref[pl.ds(r, ATT_BLOCK, stride=dilation), :] = src_s[r]


def attn_fwd_classes(pa, bias, qg, kg, group, name):
    B = pa.shape[0]
    dilation = ATT_GROUPS[group][1]
    hb = group * 4

    def body(q_ref, k_ref, v_ref, b_ref, qg_ref, kg_ref, o_ref, l_ref, q_s, k_s, v_s):
        _gather_classes(q_ref, q_s, dilation)
        _gather_classes(k_ref, k_s, dilation)
        _gather_classes(v_ref, v_s, dilation)
        o, lse = _attn_classes(q_s[...], k_s[...], v_s[...], _masked_bias(b_ref[0])[1], qg_ref[...], kg_ref[...])
        q_s[...], k_s[...] = o, lse
        _scatter_classes(q_s, o_ref, dilation)
        _scatter_classes(k_s, l_ref, dilation)

    head_out = pl.BlockSpec((None, SEQ, HEAD_DIM), lambda b, h: (b, 0, h))
    return pl.pallas_call(
        body, name=name, grid=(B, 4),
        in_specs=_qkv_specs(hb) + [
                  pl.BlockSpec((1, ATT_BLOCK, 2 * ATT_BLOCK), lambda b, h: (hb + h, 0, 0)),
                  pl.BlockSpec((1, HEAD_DIM), lambda b, h: (0, 0)),
                  pl.BlockSpec((1, HEAD_DIM), lambda b, h: (0, 0))],
        out_specs=[head_out, head_out],
        out_shape=[jax.ShapeDtypeStruct((B, SEQ, 512), F32), jax.ShapeDtypeStruct((B, SEQ, 512), F32)],
        scratch_shapes=[pltpu.VMEM((dilation, ATT_BLOCK, HEAD_DIM), F32)] * 3,
        compiler_params=_cparams(),
    )(pa, pa, pa, bias, qg, kg)


def attn_bwd_classes(pa, bias, qg, kg, do, dlse, dpa, group, name):
    B = pa.shape[0]
    dilation = ATT_GROUPS[group][1]
    hb = group * 4

    def body(q_ref, k_ref, v_ref, b_ref, qg_ref, kg_ref, do_ref, dl_ref, dpa_in,
             dp_ref, db_ref, dqg_ref, dkg_ref, q_s, k_s, v_s, do_s, dl_s, rows_s):
        del dpa_in
        h_id = pl.program_id(1)

        @pl.when(jnp.logical_and(pl.program_id(0) == 0, h_id == 0))
        def _():
            db_ref[...] = jnp.zeros_like(db_ref)
            dqg_ref[...] = jnp.zeros_like(dqg_ref)
            dkg_ref[...] = jnp.zeros_like(dkg_ref)

        for src, dst in ((q_ref, q_s), (k_ref, k_s), (v_ref, v_s), (do_ref, do_s), (dl_ref, dl_s)):
            _gather_classes(src, dst, dilation)
        _, vjp = jax.vjp(_attn_classes, q_s[...], k_s[...], v_s[...], _masked_bias(b_ref[0])[1],
                         qg_ref[...], kg_ref[...])
        dq, dk, dv, dbias, dqg, dkg = vjp((do_s[...], dl_s[...]))
        db_ref[h_id, :, ATT_BLOCK:] += dbias
        dqg_ref[...] += dqg
        dkg_ref[...] += dkg
        for d, lanes in ((dq, _QL), (dk, _KL), (dv, _VL)):
            q_s[...] = d
            _scatter_classes(q_s, rows_s, dilation)
            dp_ref[0, :, lanes] = rows_s[...].astype(BF16)

    const2 = lambda b, h: (0, 0)
    head_in = pl.BlockSpec((None, SEQ, HEAD_DIM), lambda b, h: (b, 0, h))
    head_blk = pl.BlockSpec((1, SEQ, ATT_HEAD_COLS), lambda b, h: (b, 0, hb + h))
    return pl.pallas_call(
        body, name=name, grid=(B, 4),
        in_specs=_qkv_specs(hb) + [
                  pl.BlockSpec((1, ATT_BLOCK, 2 * ATT_BLOCK), lambda b, h: (hb + h, 0, 0)),
                  pl.BlockSpec((1, HEAD_DIM), const2), pl.BlockSpec((1, HEAD_DIM), const2),
                  head_in, head_in,
                  pl.BlockSpec(memory_space=pl.ANY)],
        out_specs=[head_blk,
                   pl.BlockSpec((4, ATT_BLOCK, 2 * ATT_BLOCK), lambda b, h: (0, 0, 0)),
                   pl.BlockSpec((1, HEAD_DIM), const2), pl.BlockSpec((1, HEAD_DIM), const2)],
        out_shape=[jax.ShapeDtypeStruct(dpa.shape, BF16),
                   jax.ShapeDtypeStruct((4, ATT_BLOCK, 2 * ATT_BLOCK), F32),
                   jax.ShapeDtypeStruct((1, HEAD_DIM), F32), jax.ShapeDtypeStruct((1, HEAD_DIM), F32)],
        scratch_shapes=[pltpu.VMEM((dilation, ATT_BLOCK, HEAD_DIM), F32)] * 5 + [pltpu.VMEM((SEQ, HEAD_DIM), F32)],
        input_output_aliases={8: 0},
        compiler_params=_cparams(),
    )(pa, pa, pa, bias, qg, kg, do, dlse, dpa)


def _merge(o0, o1, o2, l0, l1, l2):
    mx = jnp.maximum(jnp.maximum(l0, l1), l2)
    e0, e1, e2 = jnp.exp(l0 - mx), jnp.exp(l1 - mx), jnp.exp(l2 - mx)
    den = e0 + e1 + e2
    return (e0 / den) * o0 + (e1 / den) * o1 + (e2 / den) * o2


def merge_att_out(os_, ls_, w_att_out, name):
    T = os_[0].shape[0]
    tile = pl.BlockSpec((ROW_MM_TILE, 512), lambda i: (i, 0))

    def body(o0, o1, o2, l0, l1, l2, w_ref, a_ref, y_ref):
        att = _merge(o0[...], o1[...], o2[...], l0[...], l1[...], l2[...]).astype(BF16)
        a_ref[...] = att
        y_ref[...] = _mxu(att, w_ref[...], _NN)

    return pl.pallas_call(
        body, name=name, grid=(T // ROW_MM_TILE,),
        in_specs=[tile] * 6 + [pl.BlockSpec((512, D_MODEL), lambda i: (0, 0))],
        out_specs=[tile, pl.BlockSpec((ROW_MM_TILE, D_MODEL), lambda i: (i, 0))],
        out_shape=[jax.ShapeDtypeStruct((T, 512), BF16), jax.ShapeDtypeStruct((T, D_MODEL), F32)],
        compiler_params=_cparams(),
    )(*os_, *ls_, w_att_out)


def d_att_merge_bwd(d_ya, w_att_out, os_, ls_, name):
    T = d_ya.shape[0]
    tile = lambda a: (a, (ROW_MM_TILE, 512), lambda i: (i, 0))

    def epi(acc, i, ex, out):
        _, vjp = jax.vjp(_merge, *[e[...] for e in ex])
        for o_ref, g in zip(out, vjp(acc)):
            o_ref[...] = g

    return row_matmul(d_ya, w_att_out, mode="nt", name=name, extras=[tile(a) for a in list(os_) + list(ls_)],
                      outs=[((T, 512), F32, (ROW_MM_TILE, 512), lambda i: (i, 0))] * 6, epi=epi)


def _gate_mix(ga, gm, ya, ym):
    return jax.nn.sigmoid(ga) * ya + jax.nn.sigmoid(gm) * ym


def _gate_halves(pg):
    return [(pg, (ROW_MM_TILE, D_MODEL), lambda i: (i, 0)), (pg, (ROW_MM_TILE, D_MODEL), lambda i: (i, 1))]


def ml_out_gate(hg, w_ml_out, pg, ya, name):
    T = hg.shape[0]

    def epi(acc, i, ex, out):
        ga, gm, ya_ref = ex
        out[0][...] = acc
        out[1][...] = _gate_mix(ga[...], gm[...], ya_ref[...], acc).astype(BF16)

    return row_matmul(hg, w_ml_out, mode="nn", name=name, extras=_gate_halves(pg) + [_rows(ya)],
                      outs=[_rows_out(T, F32), _rows_out(T, BF16)], epi=epi)


def d_z_gate_bwd(dy, w_out, pg, ya, ym, name):
    T = dy.shape[0]

    def epi(acc, i, ex, out):
        ga, gm, ya_ref, ym_ref = ex
        dpg_ref, dya_ref, dym_ref = out
        _, vjp = jax.vjp(_gate_mix, ga[...], gm[...], ya_ref[...], ym_ref[...])
        dga, dgm, dya, dym = vjp(acc)
        dpg_ref[:, :D_MODEL] = dga.astype(BF16)
        dpg_ref[:, D_MODEL:] = dgm.astype(BF16)
        dya_ref[...] = dya.astype(BF16)
        dym_ref[...] = dym.astype(BF16)

    return row_matmul(dy, w_out, mode="nt", name=name, extras=_gate_halves(pg) + [_rows(ya), _rows(ym)],
                      outs=[((T, GATE_COLS), BF16, (ROW_MM_TILE, GATE_COLS), lambda i: (i, 0)),
                            _rows_out(T, BF16), _rows_out(T, BF16)], epi=epi)


def _log_sigmoid(x):
    return jnp.minimum(x, 0.0) - jnp.log(1.0 + jnp.exp(-jnp.abs(x)))


def _head_mask(e):
    lane = lax.broadcasted_iota(jnp.int32, (1, 128), 1)
    return jnp.logical_and(lane >= e * 64, lane < (e + 1) * 64).astype(F32)


def _bmxu(a, b, ca, cb):
    return lax.dot_general(a.astype(BF16), b.astype(BF16), (((ca,), (cb,)), ((0,), (0,))), preferred_element_type=F32)


@jax.custom_vjp
def cdot_nt(a, b):
    return _bmxu(a, b, 2, 2)


cdot_nt.defvjp(lambda a, b: (_bmxu(a, b, 2, 2), (a, b)),
               lambda res, g: (_bmxu(g, res[1], 2, 1), _bmxu(g, res[0], 1, 1)))


@jax.custom_vjp
def cdot_nn(a, b):
    return _bmxu(a, b, 2, 1)


cdot_nn.defvjp(lambda a, b: (_bmxu(a, b, 2, 1), (a, b)),
               lambda res, g: (_bmxu(g, res[1], 2, 2), _bmxu(res[0], g, 1, 1)))


@jax.custom_vjp
def cdot_tn(a, b):
    return _bmxu(a, b, 1, 1)


cdot_tn.defvjp(lambda a, b: (_bmxu(a, b, 1, 1), (a, b)),
               lambda res, g: (_bmxu(res[1], g, 2, 2), _bmxu(res[0], g, 2, 1)))


def _top_bits(x):
    return lax.bitcast_convert_type(lax.bitcast_convert_type(x, jnp.uint32) & jnp.uint32(0xFFFF0000), F32)


def _split3(x):
    hi = _top_bits(x)
    r = x - hi
    mid = _top_bits(r)
    return hi, mid, r - mid


def _parts_in_lanes(col):
    hi, mid, lo = _split3(col)
    lane = lax.broadcasted_iota(jnp.int32, (1, 1, 8), 2)
    return jnp.where(lane == 0, hi, jnp.where(lane == 1, mid, jnp.where(lane == 2, lo, 0.0)))


def _parts_in_rows(row):
    hi, mid, lo = _split3(row)
    sub = lax.broadcasted_iota(jnp.int32, (1, 8, 1), 1)
    return jnp.where(sub == 0, hi, jnp.where(sub == 1, mid, jnp.where(sub == 2, lo, 0.0)))


def _chunk_matrix(kind, c):
    ri = lax.broadcasted_iota(jnp.int32, (c, ML_CHUNK, ML_CHUNK), 1)
    ci = lax.broadcasted_iota(jnp.int32, (c, ML_CHUNK, ML_CHUNK), 2)
    return {"eye": ri == ci, "lower": ri >= ci, "upper": ri <= ci}[kind].astype(F32)


def _col_col(kind, col):
    out = _bmxu(_chunk_matrix(kind, col.shape[0]), _parts_in_lanes(col), 2, 1)
    return jnp.sum(out, axis=-1, keepdims=True)


def _col_row(col):
    out = _bmxu(_parts_in_lanes(col), _chunk_matrix("eye", col.shape[0]), 1, 1)
    return jnp.sum(out, axis=1, keepdims=True)


def _row_col(row):
    out = _bmxu(_chunk_matrix("eye", row.shape[0]), _parts_in_rows(row), 2, 2)
    return jnp.sum(out, axis=-1, keepdims=True)


@jax.custom_vjp
def chunk_cumsum(col):
    return _col_col("lower", col)


chunk_cumsum.defvjp(lambda col: (_col_col("lower", col), None), lambda _, g: (_col_col("upper", g),))


@jax.custom_vjp
def col_to_row(col):
    return _col_row(col)


col_to_row.defvjp(lambda col: (_col_row(col), None), lambda _, g: (_row_col(g),))


def _gate_block(ifb):
    lane = lax.broadcasted_iota(jnp.int32, (1, 128), 1)
    return jnp.where(lane >= 2, _log_sigmoid(ifb), ifb)


def _ml_intra(q2, k2, v, ifb, *, e):
    c, L = q2.shape[0] // ML_CHUNK, ML_CHUNK
    hm = _head_mask(e)
    q3 = (q2 * hm).reshape(c, L, 128)
    k3 = (k2 * hm).reshape(c, L, 128)
    v3 = v.reshape(c, L, 128)
    if3 = ifb.reshape(c, L, 128)
    lanes = lax.broadcasted_iota(jnp.int32, (c, L, 128), 2)
    li = jnp.sum(jnp.where(lanes == e, if3, 0.0), axis=-1, keepdims=True)
    lf = jnp.sum(jnp.where(lanes == 2 + e, if3, 0.0), axis=-1, keepdims=True)
    b = chunk_cumsum(lf)
    last = lax.broadcasted_iota(jnp.int32, (1, L, 1), 1) == L - 1
    b_end = jnp.sum(jnp.where(last, b, 0.0), axis=1, keepdims=True)
    causal = lax.broadcasted_iota(jnp.int32, (L, L), 0) >= lax.broadcasted_iota(jnp.int32, (L, L), 1)
    Dm = jnp.where(causal, b + col_to_row(li - b), -jnp.inf)
    mD = lax.stop_gradient(jnp.max(Dm, axis=-1, keepdims=True))
    P0 = cdot_nt(q3, k3) * jnp.exp(Dm - mD)
    H0 = cdot_nn(P0, v3)
    r0 = jnp.sum(P0, axis=-1, keepdims=True)
    g = b_end - b + li
    mg = lax.stop_gradient(jnp.max(g, axis=1, keepdims=True))
    kw = jnp.exp(g - mg) * k3
    return H0, r0, cdot_tn(kw, v3), jnp.sum(kw, axis=1, keepdims=True), b, b_end, mD, mg


def _ml_inter(q2, mo, gn, H0, r0, b, C_in, n_in, *, mD, m_in, e):
    c, L = q2.shape[0] // ML_CHUNK, ML_CHUNK
    q3 = (q2 * _head_mask(e)).reshape(c, L, 128)
    a = b + m_in
    m_t = lax.stop_gradient(jnp.maximum(a, mD))
    c1 = jnp.exp(mD - m_t)
    c2 = jnp.exp(a - m_t)
    num = c1 * H0 + c2 * cdot_nn(q3, C_in)
    nq = c1 * r0 + c2 * jnp.sum(q3 * n_in, axis=-1, keepdims=True)
    h = num / jnp.maximum(jnp.abs(nq), jnp.exp(-m_t))
    hg = _rms(h) * gn * jax.nn.sigmoid(mo.reshape(c, L, 128))
    return hg.reshape(c * L, 128)


def _state_sweep(U_s, un_s, be_s, mg_s, Cin_s, nin_s, min_s, al_s, bt_s):
    def step(j, carry):
        C, n, m = carry
        Cin_s[j], nin_s[j], min_s[j] = C, n, m
        m_out = jnp.maximum(be_s[j] + m, mg_s[j])
        al = jnp.exp(be_s[j] + m - m_out)
        bt = jnp.exp(mg_s[j] - m_out)
        al_s[j], bt_s[j] = al, bt
        return al * C + bt * U_s[j], al * n + bt * un_s[j], m_out

    lax.fori_loop(0, N_CHUNKS, step, (jnp.zeros((128, 128), F32), jnp.zeros((1, 128), F32), jnp.zeros((1, 1), F32)))


def _state_sweep_bwd(U_s, un_s, dbe_s, Cin_s, nin_s, dCp_s, dnp_s, al_s, bt_s):
    def step(t, carry):
        j = N_CHUNKS - 1 - t
        dC, dn = carry
        al, bt = al_s[j], bt_s[j]
        U_s[j] = bt * dC
        un_s[j] = bt * dn
        dal = jnp.sum(jnp.sum(dC * Cin_s[j], axis=1, keepdims=True), axis=0, keepdims=True) \
            + jnp.sum(dn * nin_s[j], axis=1, keepdims=True)
        dbe_s[j] = dal * al
        return dCp_s[j] + al * dC, dnp_s[j] + al * dn

    lax.fori_loop(0, N_CHUNKS, step, (jnp.zeros((128, 128), F32), jnp.zeros((1, 128), F32)))


def _state_scratch():
    c = N_CHUNKS
    return [pltpu.VMEM((c, 128, 128), F32), pltpu.VMEM((c, 1, 128), F32), pltpu.VMEM((c, 1, 1), F32),
            pltpu.VMEM((c, 1, 1), F32),
            pltpu.VMEM((c, 128, 128), F32), pltpu.VMEM((c, 1, 128), F32), pltpu.VMEM((c, 1, 1), F32),
            pltpu.VMEM((c, 1, 1), F32), pltpu.VMEM((c, 1, 1), F32)]


def _shift_down(x, s):
    if s == 0:
        return x
    rows = lax.broadcasted_iota(jnp.int32, x.shape, 0)
    return jnp.where(rows >= s, pltpu.roll(x, s, 0), 0.0)


def _shift_up(x, s):
    if s == 0:
        return x
    S = x.shape[0]
    rows = lax.broadcasted_iota(jnp.int32, x.shape, 0)
    return jnp.where(rows < S - s, pltpu.roll(x, S - s, 0), 0.0)


def _conv_pre(x, cw, cb):
    y = cb + cw[3:4, :] * x
    for j in range(3):
        y = y + cw[j:j + 1, :] * _shift_down(x, 3 - j)
    return y


def _conv_bwd(x, cw, dpre):
    dx = cw[3:4, :] * dpre
    dcw = [None] * 4
    dcw[3] = jnp.sum(dpre * x, axis=0, keepdims=True)
    for j in range(3):
        dx = dx + cw[j:j + 1, :] * _shift_up(dpre, 3 - j)
        dcw[j] = jnp.sum(dpre * _shift_down(x, 3 - j), axis=0, keepdims=True)
    return dx, dcw, jnp.sum(dpre, axis=0, keepdims=True)


def _silu(z):
    return z * jax.nn.sigmoid(z)


def _dsilu(z):
    s = jax.nn.sigmoid(z)
    return s * (1.0 + z * (1.0 - s))


_ML_Q, _ML_K = slice(0, 128), slice(128, 256)
_ML_IF = slice(768, 896)


def _ml_v(e):
    return slice(256 + e * 128, 384 + e * 128)


def _ml_o(e):
    return slice(512 + e * 128, 640 + e * 128)


def _ml_specs():
    pair = lambda b, p: (b, 0, p)
    return [pl.BlockSpec((1, SEQ, ML_PAIR_COLS), pair),
            pl.BlockSpec((1, 4, 128), lambda b, p: (p, 0, 0)),
            pl.BlockSpec((1, 4, 128), lambda b, p: (4 + p, 0, 0)),
            pl.BlockSpec((1, 1, 128), lambda b, p: (p, 0, 0)),
            pl.BlockSpec((1, 1, 128), lambda b, p: (4 + p, 0, 0)),
            pl.BlockSpec((1, 1, 128), lambda b, p: (p, 0, 0)),
            pl.BlockSpec((1, 1, 256), lambda b, p: (p, 0, 0))]


def mlstm_fwd(pm, cw8, cb8, bifp, gn4, name):
    B = pm.shape[0]

    def body(p_ref, cwq, cwk, cbq, cbk, bif_ref, gn_ref, hg_ref, *st):
        U_s, un_s, be_s, mg_s, Cin_s, nin_s, min_s, al_s, bt_s = st
        qc = _silu(_conv_pre(p_ref[0, :, _ML_Q], cwq[0], cbq[0]))
        kc = _silu(_conv_pre(p_ref[0, :, _ML_K], cwk[0], cbk[0])) * (64 ** -0.5)
        ifb = _gate_block(p_ref[0, :, _ML_IF] + bif_ref[0])
        for e in range(2):
            lanes = slice(e * 128, (e + 1) * 128)
            H0, r0, U, un, b, b_end, mD, mg = _ml_intra(qc, kc, p_ref[0, :, _ml_v(e)], ifb, e=e)
            U_s[...], un_s[...], be_s[...], mg_s[...] = U, un, b_end, mg
            _state_sweep(*st)
            hg = _ml_inter(qc, p_ref[0, :, _ml_o(e)], gn_ref[0, :, lanes], H0, r0, b, Cin_s[...], nin_s[...],
                           mD=mD, m_in=min_s[...], e=e)
            hg_ref[0, :, lanes] = hg.astype(BF16)

    return pl.pallas_call(
        body, name=name, grid=(B, ML_PAIRS),
        in_specs=_ml_specs(),
        out_specs=pl.BlockSpec((1, SEQ, 256), lambda b, p: (b, 0, p)),
        out_shape=jax.ShapeDtypeStruct((B, SEQ, D_MODEL), BF16),
        scratch_shapes=_state_scratch(),
        compiler_params=_cparams(),
    )(pm, cw8, cw8, cb8, cb8, bifp, gn4)


def mlstm_bwd(pm, cw8, cb8, bifp, gn4, dhg, name):
    B = pm.shape[0]

    def body(p_ref, cwq, cwk, cbq, cbk, bif_ref, gn_ref, dh_ref,
             dp_ref, dcw_ref, dcb_ref, dbif_ref, dgn_ref, *scr):
        st = scr[:9]
        U_s, un_s, be_s, mg_s, Cin_s, nin_s, min_s, al_s, bt_s = st
        dCp_s, dnp_s, dbe_s = scr[9:]
        p_id = pl.program_id(1)

        @pl.when(jnp.logical_and(pl.program_id(0) == 0, p_id == 0))
        def _():
            dcw_ref[...] = jnp.zeros_like(dcw_ref)
            dcb_ref[...] = jnp.zeros_like(dcb_ref)
            dbif_ref[...] = jnp.zeros_like(dbif_ref)
            dgn_ref[...] = jnp.zeros_like(dgn_ref)

        pre_q = _conv_pre(p_ref[0, :, _ML_Q], cwq[0], cbq[0])
        pre_k = _conv_pre(p_ref[0, :, _ML_K], cwk[0], cbk[0])
        qc = _silu(pre_q)
        kc = _silu(pre_k) * (64 ** -0.5)
        ifb, gate_vjp = jax.vjp(_gate_block, p_ref[0, :, _ML_IF] + bif_ref[0])
        dq = jnp.zeros((SEQ, 128), F32)
        dk = jnp.zeros((SEQ, 128), F32)
        difb = jnp.zeros((SEQ, 128), F32)
        for e in range(2):
            lanes = slice(e * 128, (e + 1) * 128)
            (H0, r0, U, un, b, b_end, mD, mg), vjp1 = jax.vjp(functools.partial(_ml_intra, e=e), qc, kc,
                                                              p_ref[0, :, _ml_v(e)], ifb)
            U_s[...], un_s[...], be_s[...], mg_s[...] = U, un, b_end, mg
            _state_sweep(*st)
            _, vjp3 = jax.vjp(functools.partial(_ml_inter, mD=mD, m_in=min_s[...], e=e), qc, p_ref[0, :, _ml_o(e)],
                              gn_ref[0, :, lanes], H0, r0, b, Cin_s[...], nin_s[...])
            dq_a, dmo, dgn, dH0, dr0, db_a, dCp, dnp = vjp3(dh_ref[0, :, lanes])
            dCp_s[...], dnp_s[...] = dCp, dnp
            _state_sweep_bwd(U_s, un_s, dbe_s, Cin_s, nin_s, dCp_s, dnp_s, al_s, bt_s)
            dq_b, dk_b, dv, difb_e = vjp1((dH0, dr0, U_s[...], un_s[...], db_a, dbe_s[...],
                                           jnp.zeros_like(mD), jnp.zeros_like(mg)))
            dq, dk, difb = dq + dq_a + dq_b, dk + dk_b, difb + difb_e
            dp_ref[0, :, _ml_v(e)] = dv.astype(BF16)
            dp_ref[0, :, _ml_o(e)] = dmo.astype(BF16)
            dgn_ref[p_id, :, lanes] += dgn
        (difb,) = gate_vjp(difb)
        dp_ref[0, :, _ML_IF] = difb.astype(BF16)
        dbif_ref[p_id] += jnp.sum(difb, axis=0, keepdims=True)

        for (sl, cw, pre, d, blk, scale) in ((_ML_Q, cwq, pre_q, dq, p_id, 1.0), (_ML_K, cwk, pre_k, dk, 4 + p_id, 64 ** -0.5)):
            xr = p_ref[0, :, sl]
            dpre = d * scale * _dsilu(pre)
            dx, dcw, dcb = _conv_bwd(xr, cw[0], dpre)
            dp_ref[0, :, sl] = dx.astype(BF16)
            for j in range(4):
                dcw_ref[blk, j:j + 1, :] += dcw[j]
            dcb_ref[blk] += dcb

    full3 = lambda b, p: (0, 0, 0)
    return pl.pallas_call(
        body, name=name, grid=(B, ML_PAIRS),
        in_specs=[pl.BlockSpec((1, SEQ, ML_PAIR_COLS), lambda b, p: (b, 0, p), pipeline_mode=pl.Buffered(1))]
        + _ml_specs()[1:] + [pl.BlockSpec((1, SEQ, 256), lambda b, p: (b, 0, p), pipeline_mode=pl.Buffered(1))],
        out_specs=[pl.BlockSpec((1, SEQ, ML_PAIR_COLS), lambda b, p: (b, 0, p)),
                   pl.BlockSpec((8, 4, 128), full3), pl.BlockSpec((8, 1, 128), full3),
                   pl.BlockSpec((4, 1, 128), full3), pl.BlockSpec((4, 1, 256), full3)],
        out_shape=[jax.ShapeDtypeStruct((B, SEQ, ML_COLS), BF16),
                   jax.ShapeDtypeStruct((8, 4, 128), F32), jax.ShapeDtypeStruct((8, 1, 128), F32),
                   jax.ShapeDtypeStruct((4, 1, 128), F32), jax.ShapeDtypeStruct((4, 1, 256), F32)],
        scratch_shapes=_state_scratch() + [pltpu.VMEM((N_CHUNKS, 128, 128), F32), pltpu.VMEM((N_CHUNKS, 1, 128), F32),
                                           pltpu.VMEM((N_CHUNKS, 1, 1), F32)],
        compiler_params=_cparams(),
    )(pm, cw8, cw8, cb8, cb8, bifp, gn4, dhg)


def _adamw(w, g, m, v):
    m = ADAM_B1 * m + (1.0 - ADAM_B1) * g
    v = ADAM_B2 * v + (1.0 - ADAM_B2) * (g * g)
    m_hat = m / (1.0 - ADAM_B1 ** ADAM_STEP)
    v_hat = v / (1.0 - ADAM_B2 ** ADAM_STEP)
    delta = -ADAM_LR * (m_hat / (jnp.sqrt(v_hat) + ADAM_EPS) + ADAM_WD * w)
    return delta, m, v


def adamw(w, g, m, v, name, parts=False):
    R, C = w.shape
    if R % 8 == 0 or R * C * 4 <= (1 << 20):
        tr = _pick(R, (256, 128, 64, 32, 16, 8)) if R * C * 4 > (1 << 20) else R
        steps = R // tr
        spec = pl.BlockSpec((tr, C), lambda i: (i, 0))
        g_spec = pl.BlockSpec((N_DEV, tr, C), lambda i: (0, i, 0)) if parts else spec
    else:
        tc = _pick(C, (256, 128))
        steps = C // tc
        spec = pl.BlockSpec((R, tc), lambda i: (0, i))
        g_spec = pl.BlockSpec((N_DEV, R, tc), lambda i: (0, 0, i)) if parts else spec

    def body(w_ref, g_ref, m_ref, v_ref, go_ref, d_ref, mo_ref, vo_ref):
        if parts:
            g = g_ref[0].astype(F32)
            for k in range(1, N_DEV):
                g = g + g_ref[k].astype(F32)
        else:
            g = g_ref[...]
        d, mn, vn = _adamw(w_ref[...], g, m_ref[...], v_ref[...])
        go_ref[...], d_ref[...], mo_ref[...], vo_ref[...] = g, d, mn, vn

    return pl.pallas_call(
        body, name=name, grid=(steps,),
        in_specs=[spec, g_spec, spec, spec], out_specs=[spec] * 4,
        out_shape=[jax.ShapeDtypeStruct((R, C), F32)] * 4,
        compiler_params=_cparams(),
    )(w, g, m, v)


def adamw_many(ws, gs, ms, vs, name):
    n = len(ws)

    def fn(*a):
        out = []
        for j in range(n):
            out += list(_adamw(a[j], a[n + j], a[2 * n + j], a[3 * n + j]))
        return tuple(out)

    shapes = [jax.ShapeDtypeStruct(w.shape, F32) for w in ws for _ in range(3)]
    return small_call(fn, list(ws) + list(gs) + list(ms) + list(vs), shapes, name)


def _mesh_pos():
    return lax.axis_index("x"), lax.axis_index("y"), lax.axis_index("c")


def _flip(pos, f):
    x, y, c = pos
    return (1 - x if f & 4 else x, 1 - y if f & 2 else y, 1 - c if f & 1 else c)


def _index(pos):
    return 4 * pos[0] + 2 * pos[1] + pos[2]


def _exchange(arrs, name, scatter):
    n = len(arrs)
    scat = list(scatter) if isinstance(scatter, (list, tuple)) else [scatter] * n

    def body(*refs):
        ins, outs = refs[:n], refs[n:2 * n]
        send, recv, lsem = refs[2 * n:]
        me = _mesh_pos()
        mine = _index(me)
        copies = []
        for i in range(n):
            src = ins[i].at[mine] if scat[i] else ins[i]
            loc = pltpu.make_async_copy(src, outs[i].at[mine], lsem.at[i])
            loc.start()
            copies.append(loc)
            for f in range(1, N_DEV):
                peer = _flip(me, f)
                src = ins[i].at[_index(peer)] if scat[i] else ins[i]
                cp = pltpu.make_async_remote_copy(
                    src_ref=src, dst_ref=outs[i].at[mine],
                    send_sem=send.at[i * 7 + f - 1], recv_sem=recv.at[i * 7 + f - 1],
                    device_id=peer, device_id_type=pl.DeviceIdType.MESH)
                cp.start()
                copies.append(cp)
        for cp in copies:
            cp.wait()

    any_spec = pl.BlockSpec(memory_space=pl.ANY)
    out_shape = [jax.ShapeDtypeStruct(a.shape if s else (N_DEV,) + a.shape, a.dtype) for a, s in zip(arrs, scat)]
    res = pl.pallas_call(
        body, name=name,
        in_specs=[any_spec] * n, out_specs=[any_spec] * n, out_shape=out_shape,
        scratch_shapes=[pltpu.SemaphoreType.DMA((7 * n,)), pltpu.SemaphoreType.DMA((7 * n,)),
                        pltpu.SemaphoreType.DMA((n,))],
        compiler_params=_cparams(),
    )(*arrs)
    return list(res)


def all_gather(arrs, name):
    return _exchange(arrs, name, False)


def all_gather_two_level(arrs, name):
    n = len(arrs)

    def body(*refs):
        ins, outs = refs[:n], refs[n:2 * n]
        send, recv, lsem = refs[2 * n:]
        x, y, c = _mesh_pos()
        me, sibling = (x, y, c), (x, y, 1 - c)
        chips = [(1 - x, y), (x, 1 - y), (1 - x, 1 - y)]

        def copy(i, k, block, to, src=None):
            rows = outs[i].at[_index(block)]
            return pltpu.make_async_remote_copy(
                src_ref=rows if src is None else src, dst_ref=rows,
                send_sem=send.at[i * 7 + k], recv_sem=recv.at[i * 7 + k],
                device_id=to, device_id_type=pl.DeviceIdType.MESH)

        local = [pltpu.make_async_copy(ins[i], outs[i].at[_index(me)], lsem.at[i]) for i in range(n)]
        first = [copy(i, 0, me, sibling, src=ins[i]) for i in range(n)]
        first += [copy(i, 1 + j, me, (*chip, c), src=ins[i]) for i in range(n) for j, chip in enumerate(chips)]
        for cp in local + first:
            cp.start()
        passed = []
        for j, chip in enumerate(chips):
            for i in range(n):
                copy(i, 1 + j, (*chip, c), me).wait_recv()
                cp = copy(i, 4 + j, (*chip, c), sibling)
                cp.start()
                passed.append(cp)
        for i in range(n):
            copy(i, 0, sibling, me).wait_recv()
            for j, chip in enumerate(chips):
                copy(i, 4 + j, (*chip, 1 - c), me).wait_recv()
        for cp in first + passed:
            cp.wait_send()
        for cp in local:
            cp.wait()

    any_spec = pl.BlockSpec(memory_space=pl.ANY)
    res = pl.pallas_call(
        body, name=name,
        in_specs=[any_spec] * n, out_specs=[any_spec] * n,
        out_shape=[jax.ShapeDtypeStruct((N_DEV,) + a.shape, a.dtype) for a in arrs],
        scratch_shapes=[pltpu.SemaphoreType.DMA((7 * n,)), pltpu.SemaphoreType.DMA((7 * n,)),
                        pltpu.SemaphoreType.DMA((n,))],
        compiler_params=_cparams(),
    )(*arrs)
    return list(res)


def all_to_all(arrs, name):
    return _exchange(arrs, name, True)


_HBM = pl.BlockSpec(memory_space=pltpu.HBM)
_SEM = pl.BlockSpec(memory_space=pltpu.SEMAPHORE)
_EFFECT = pltpu.SideEffectType.DATAFLOW_SIDE_EFFECTING


def _split_copies(ins, lands, send, recv, scatter, waiting):
    me = _mesh_pos()
    mine = _index(me)
    copies = []
    for i in range(len(ins)):
        for f in range(1, N_DEV):
            peer = _flip(me, f)
            src = ins[i].at[_index(peer)] if scatter else ins[i]
            copies.append(pltpu.make_async_remote_copy(
                src_ref=src, dst_ref=lands[i].at[_index(peer) if waiting else mine],
                send_sem=send.at[i * 7 + f - 1], recv_sem=recv.at[i * 7 + f - 1],
                device_id=peer, device_id_type=pl.DeviceIdType.MESH))
    return copies


def exchange_start(arrs, name, scatter, after=()):
    n = len(arrs)
    land_shapes = [a.shape if scatter else (N_DEV,) + a.shape for a in arrs]

    def body(*refs):
        ins, lands = refs[:n], refs[n:2 * n]
        send, recv = refs[2 * n + len(after)], refs[2 * n + len(after) + 1]
        token = refs[-1]
        for cp in _split_copies(ins, lands, send, recv, scatter, False):
            cp.start()
        token[...] = jnp.zeros_like(token)

    res = pl.pallas_call(
        body, name=name,
        out_shape=(pltpu.SemaphoreType.DMA((7 * n,)), pltpu.SemaphoreType.DMA((7 * n,)),
                   *[pltpu.HBM(a.shape, a.dtype) for a in arrs],
                   *[pltpu.HBM(s, a.dtype) for s, a in zip(land_shapes, arrs)],
                   jax.ShapeDtypeStruct((8, 128), F32)),
        in_specs=[_HBM] * (2 * n) + [pl.BlockSpec(memory_space=pl.ANY)] * len(after),
        out_specs=(_SEM, _SEM, *[_HBM] * (2 * n), pl.BlockSpec(memory_space=pltpu.VMEM)),
        input_output_aliases={i: 2 + i for i in range(2 * n)},
        compiler_params=pltpu.CompilerParams(has_side_effects=_EFFECT),
    )(*[pltpu.with_memory_space_constraint(a, pltpu.HBM) for a in arrs],
      *[pltpu.with_memory_space_constraint(lax.empty(s, a.dtype), pltpu.HBM) for s, a in zip(land_shapes, arrs)],
      *after)
    return (res[0], res[1], list(res[2:2 + n]), list(res[2 + n:2 + 2 * n])), res[-1]


def exchange_wait(handle, after, name, scatter):
    send, recv, srcs, lands = handle
    n = len(srcs)

    def body(*refs):
        ins, lnd = refs[:n], refs[n:2 * n]
        send_, recv_ = refs[2 * n], refs[2 * n + 1]
        for cp in _split_copies(ins, lnd, send_, recv_, scatter, True):
            cp.wait_send()
            cp.wait_recv()

    res = pl.pallas_call(
        body, name=name,
        out_shape=(*[pltpu.HBM(a.shape, a.dtype) for a in srcs], *[pltpu.HBM(a.shape, a.dtype) for a in lands]),
        in_specs=[_HBM] * (2 * n) + [_SEM, _SEM, pl.BlockSpec(memory_space=pl.ANY)],
        out_specs=tuple([_HBM] * (2 * n)),
        input_output_aliases={i: i for i in range(2 * n)},
        compiler_params=pltpu.CompilerParams(has_side_effects=_EFFECT),
    )(*srcs, *lands, send, recv, after)
    return list(res[n:])


def _own_slot(land, own):
    return lax.dynamic_update_slice(land, own[None], (_index(_mesh_pos()),) + (0,) * own.ndim)


def cast_bf16(arrs, name):
    outs = []
    for i, a in enumerate(arrs):
        R, C = a.shape
        if R % 8 == 0:
            tr = _pick(R, (256, 128, 64, 32, 16, 8)) if R * C * 4 > (1 << 21) else R
            steps, spec = R // tr, pl.BlockSpec((tr, C), lambda i: (i, 0))
        else:
            steps, spec = C // 256, pl.BlockSpec((R, 256), lambda i: (0, i))

        def body(a_ref, o_ref):
            o_ref[...] = a_ref[...].astype(BF16)

        outs.append(pl.pallas_call(body, name=f"{name}_{i}", grid=(steps,), in_specs=[spec], out_specs=spec,
                                   out_shape=jax.ShapeDtypeStruct((R, C), BF16), compiler_params=_cparams())(a))
    return outs


def sum_slabs(parts, name):
    def fn(*a):
        outs = []
        for p in a:
            s = p[0].astype(F32)
            for k in range(1, N_DEV):
                s = s + p[k].astype(F32)
            outs.append(s)
        return tuple(outs)
    return list(small_call(fn, list(parts), [jax.ShapeDtypeStruct(p.shape[1:], F32) for p in parts], name))


def sum_parts(parts, name):
    def fn(p):
        g = p[0]
        for k in range(1, N_DEV):
            g = g + p[k]
        return (g,)
    return small_call(fn, [parts], [jax.ShapeDtypeStruct(parts.shape[1:], F32)], name)[0]


_SPLITS = np.cumsum([1536, 1536, 1536, 512, 512, 1024, 1024, 8, 8, 2048])[:-1].tolist()


def split_w_in(w):
    aq, ak, av, mq, mk, mv, mo, mi, mf, gates = jnp.split(w, _SPLITS, axis=1)
    R = w.shape[0]
    w_att = jnp.stack([aq.reshape(R, 12, 128), ak.reshape(R, 12, 128), av.reshape(R, 12, 128)], axis=2)
    gif = jnp.concatenate([mi.reshape(R, 4, 2), mf.reshape(R, 4, 2), jnp.zeros((R, 4, 124), w.dtype)], axis=2)
    w_ml = jnp.concatenate([mq.reshape(R, 4, 128), mk.reshape(R, 4, 128), mv.reshape(R, 4, 256),
                            mo.reshape(R, 4, 256), gif], axis=2)
    return w_att.reshape(R, ATT_COLS), w_ml.reshape(R, ML_COLS), gates


def merge_w_in(g_att, g_ml, g_gate):
    R = g_att.shape[0]
    a = g_att.reshape(R, 12, 3, 128)
    m = g_ml.reshape(R, 4, ML_PAIR_COLS)
    gif = m[:, :, 768:772]
    return jnp.concatenate([
        a[:, :, 0].reshape(R, 1536), a[:, :, 1].reshape(R, 1536), a[:, :, 2].reshape(R, 1536),
        m[:, :, 0:128].reshape(R, 512), m[:, :, 128:256].reshape(R, 512),
        m[:, :, 256:512].reshape(R, 1024), m[:, :, 512:768].reshape(R, 1024),
        gif[:, :, 0:2].reshape(R, 8), gif[:, :, 2:4].reshape(R, 8), g_gate], axis=1)


def _blk8(v, width=128):
    r = v.shape[0]
    return v.reshape(r, 1024 // width, width).transpose(1, 0, 2)


def _unblk8(v):
    nb, r, w = v.shape
    return v.transpose(1, 0, 2).reshape(r, nb * w)


def local_step(x, target, mods, w, small, late_w=None, early_g=None, w_in_g=None):
    late_w = late_w or (lambda after: w)
    big = {}
    early_g = early_g or (lambda g: big.update(g))
    w_in_g = w_in_g or (lambda g: big.update(w_in=merge_w_in(*g)))
    B = x.shape[0]
    T = B * SEQ
    shift1, scale1, gate1, shift2, scale2, gate2 = mods
    f2 = lambda a: a.reshape(T, a.shape[-1])
    f3 = lambda a: a.reshape(B, SEQ, a.shape[-1])

    rel_t = jnp.pad(small["rel_bias"].T, ((0, 4), (0, 0)))
    onehots = [_bucket_onehot(d) for _, d in ATT_GROUPS]
    biases = [bias_expand(rel_t, oh, f"bias_expand{g}").reshape(16, ATT_BLOCK, 2 * ATT_BLOCK)
              for g, oh in enumerate(onehots)]
    qg, kg = small["q_norm_g"], small["k_norm_g"]
    cw8 = _blk8(small["conv_w"])
    cb8 = _blk8(small["conv_b"])
    b_if = small["b_if"].reshape(2, 4, 2)
    bifp = jnp.concatenate([b_if[0], b_if[1], jnp.zeros((4, 124), F32)], axis=1).reshape(4, 1, 128)
    gn4 = small["mlstm_norm_g"].reshape(4, 1, 256)

    u = modnorm_fwd(x, small["norm1_g"], scale1, shift1, "modnorm1")
    u2d = f2(u)
    pa = f3(matmul(u2d, w["w_att"], mode="nn", name="proj_att"))
    pm = f3(matmul(u2d, w["w_ml"], mode="nn", name="proj_ml"))
    pg = matmul(u2d, w["w_gate"], mode="nn", name="proj_gate")
    os_, ls_ = [], []
    one_block = [SEQ // d == ATT_BLOCK for _, d in ATT_GROUPS]
    for g in range(3):
        o, l = (attn_fwd_classes if one_block[g] else attn_fwd)(pa, biases[g], qg, kg, g, f"attn_fwd{g}")
        os_.append(f2(o))
        ls_.append(f2(l))
    hg = mlstm_fwd(pm, cw8, cb8, bifp, gn4, "mlstm_fwd")
    w = {**w, **late_w(hg)}
    att, y_att = merge_att_out(os_, ls_, w["w_att_out"], "att_out")
    y_ml, z = ml_out_gate(f2(hg), w["w_ml_out"], pg, y_att, "ml_out")
    y, x1, u2 = out_proj_resid_modnorm(f2(z), w["w_out"], f2(x), gate1, small["norm2_g"], scale2, shift2, "out_proj")
    pre, hdn = matmul(u2, w["w_ff1"], mode="nn", name="ff1", out_dtypes=(BF16, BF16),
                      epi=lambda acc: (acc, jnp.square(jnp.maximum(acc, 0.0))))
    dx2, d_ffo, loss, d_gate2 = ff2_loss(hdn, w["w_ff2"], x1, gate2, f2(target), "ff2_loss")

    g_ff2 = matmul(hdn, d_ffo, mode="tn", name="g_ff2", out_dtypes=(BF16,))
    d_pre = matmul(d_ffo, w["w_ff2"], mode="nt", name="d_hdn", out_dtypes=(BF16,), extras=(pre,),
                   epi=lambda acc, p: (acc * (2.0 * jnp.maximum(p.astype(F32), 0.0)),))
    g_ff1 = matmul(u2, d_pre, mode="tn", name="g_ff1", out_dtypes=(BF16,))
    dx1, d_norm2, d_scale2, d_shift2, dy, d_gate1 = d_u_modnorm_bwd(
        d_pre, w["w_ff1"], None, x1, small["norm2_g"], scale2, shift2, dx2, y, gate1, "d_u2")
    g_out = matmul(f2(z), dy, mode="tn", name="g_out", out_dtypes=(BF16,))
    dpg, d_ya, d_ym = d_z_gate_bwd(dy, w["w_out"], pg, y_att, y_ml, "d_z")
    g_att_out = matmul(f2(att), f2(d_ya), mode="tn", name="g_att_out", out_dtypes=(BF16,))
    dmerge = d_att_merge_bwd(d_ya, w["w_att_out"], os_, ls_, "d_att")
    g_ml_out = matmul(f2(hg), f2(d_ym), mode="tn", name="g_ml_out", out_dtypes=(BF16,))
    d_hg = matmul(f2(d_ym), w["w_ml_out"], mode="nt", name="d_hg")
    started = early_g(dict(w_att_out=g_att_out, w_ml_out=g_ml_out, w_out=g_out, w_ff1=g_ff1, w_ff2=g_ff2))
    order = 0.0 if started is None else started[0, 0]
    dmerge = [f3(d) for d in dmerge]
    dpa = lax.empty((B, SEQ, ATT_COLS), BF16)
    d_rel = []
    d_qg = d_kg = None
    for g in range(3):
        dpa, dbias, dq_g, dk_g = (attn_bwd_classes if one_block[g] else attn_bwd)(
            pa, biases[g], qg + order, kg, dmerge[g], dmerge[3 + g], dpa, g, f"attn_bwd{g}")
        db8 = jnp.pad(dbias.reshape(4, -1), ((0, 4), (0, 0)))
        d_rel.append(bias_reduce(db8, onehots[g], f"bias_reduce{g}")[:4])
        d_qg = dq_g if d_qg is None else d_qg + dq_g
        d_kg = dk_g if d_kg is None else d_kg + dk_g
    dpm, dcw8, dcb8, dbifp, dgn4 = mlstm_bwd(pm, cw8, cb8, bifp, gn4 + order, f3(d_hg), "mlstm_bwd")
    g_w_att = matmul(u2d, f2(dpa), mode="tn", name="g_w_att", out_dtypes=(BF16,))
    g_w_ml = matmul(u2d, f2(dpm), mode="tn", name="g_w_ml", out_dtypes=(BF16,))
    g_w_gate = matmul(u2d, f2(dpg), mode="tn", name="g_w_gate", out_dtypes=(BF16,))
    started = w_in_g((g_w_att, g_w_ml, g_w_gate))
    du = matmul(f2(dpa), w["w_att"], mode="nt", name="d_u_att", after=() if started is None else (started,))
    du = matmul(f2(dpm), w["w_ml"], mode="nt", name="d_u_ml", extras=(du,), epi=lambda acc, e: (acc + e,))
    grad_x, d_norm1, d_scale1, d_shift1 = d_u_modnorm_bwd(
        f2(dpg), w["w_gate"], du, f2(x), small["norm1_g"], scale1, shift1, dx1, None, None, "d_u_gate")
    grad_x = f3(grad_x)

    d_mods = (d_shift1, d_scale1, d_gate1, d_shift2, d_scale2, d_gate2)
    dbif = dbifp.reshape(4, 128)
    small_g = dict(
        norm1_g=d_norm1, norm2_g=d_norm2,
        b_if=jnp.stack([dbif[:, 0:2].reshape(8), dbif[:, 2:4].reshape(8)]),
        conv_w=_unblk8(dcw8), conv_b=_unblk8(dcb8), q_norm_g=d_qg, k_norm_g=d_kg,
        rel_bias=jnp.concatenate(d_rel, axis=0).T,
        mlstm_norm_g=dgn4.reshape(1, 1024))
    return loss, grad_x, d_mods, big, small_g


_SMALL = (("b_ada", 6144), ("norm1_g", 1024), ("norm2_g", 1024), ("b_if", 16), ("conv_b", 1024),
          ("q_norm_g", 128), ("k_norm_g", 128), ("rel_bias", 384), ("mlstm_norm_g", 1024), ("conv_w", 4096))
_SMALL_ROWS = 120
_REPL = _SMALL[:-1]
_SMALL_SENT = _SMALL + (("loss", 1),)


def _pack(d, names, rows):
    flat = jnp.concatenate([d[k].reshape(-1) for k, _ in names])
    return jnp.pad(flat, (0, rows * 128 - flat.shape[0])).reshape(rows, 128)


def _unpack(slab, names, shapes):
    flat = slab.reshape(-1)
    out, off = {}, 0
    for k, nel in names:
        out[k] = flat[off:off + nel].reshape(shapes[k])
        off += nel
    return out


def kernel(x, c, w_ada, b_ada, norm1_g, norm2_g, w_in, b_if, conv_w, conv_b, q_norm_g, k_norm_g, rel_bias, mlstm_norm_g, w_att_out, w_ml_out, w_out, w_ff1, w_ff2, loss_target, m_w_ada, m_b_ada, m_norm1_g, m_norm2_g, m_w_in, m_b_if, m_conv_w, m_conv_b, m_q_norm_g, m_k_norm_g, m_rel_bias, m_mlstm_norm_g, m_w_att_out, m_w_ml_out, m_w_out, m_w_ff1, m_w_ff2, v_w_ada, v_b_ada, v_norm1_g, v_norm2_g, v_w_in, v_b_if, v_conv_w, v_conv_b, v_q_norm_g, v_k_norm_g, v_rel_bias, v_mlstm_norm_g, v_w_att_out, v_w_ml_out, v_w_out, v_w_ff1, v_w_ff2):
    P = dict(w_ada=w_ada, b_ada=b_ada, norm1_g=norm1_g, norm2_g=norm2_g, w_in=w_in, b_if=b_if, conv_w=conv_w,
             conv_b=conv_b, q_norm_g=q_norm_g, k_norm_g=k_norm_g, rel_bias=rel_bias, mlstm_norm_g=mlstm_norm_g,
             w_att_out=w_att_out, w_ml_out=w_ml_out, w_out=w_out, w_ff1=w_ff1, w_ff2=w_ff2)
    M = dict(w_ada=m_w_ada, b_ada=m_b_ada, norm1_g=m_norm1_g, norm2_g=m_norm2_g, w_in=m_w_in, b_if=m_b_if,
             conv_w=m_conv_w, conv_b=m_conv_b, q_norm_g=m_q_norm_g, k_norm_g=m_k_norm_g, rel_bias=m_rel_bias,
             mlstm_norm_g=m_mlstm_norm_g, w_att_out=m_w_att_out, w_ml_out=m_w_ml_out, w_out=m_w_out,
             w_ff1=m_w_ff1, w_ff2=m_w_ff2)
    V = dict(w_ada=v_w_ada, b_ada=v_b_ada, norm1_g=v_norm1_g, norm2_g=v_norm2_g, w_in=v_w_in, b_if=v_b_if,
             conv_w=v_conv_w, conv_b=v_conv_b, q_norm_g=v_q_norm_g, k_norm_g=v_k_norm_g, rel_bias=v_rel_bias,
             mlstm_norm_g=v_mlstm_norm_g, w_att_out=v_w_att_out, w_ml_out=v_w_ml_out, w_out=v_w_out,
             w_ff1=v_w_ff1, w_ff2=v_w_ff2)
    names = list(P)
    shapes = {k: P[k].shape for k in names}
    B = x.shape[0]
    me = _index(_mesh_pos())

    big_names = ("w_in", "w_att_out", "w_ml_out", "w_out", "w_ff1", "w_ff2")
    shards = cast_bf16([P[k][0] for k in big_names], "cast_w")
    w_in_g8, c8, conv_w8 = all_gather_two_level([shards[0], c, conv_w[0]], "gather_w_in")
    c_all = c8.reshape(N_DEV * B, D_MODEL)
    conv_w_full = conv_w8.transpose(1, 0, 2).reshape(4, 1024)
    w_att, w_ml, w_gate = split_w_in(w_in_g8.transpose(1, 0, 2).reshape(D_MODEL, D_IN))
    w = dict(w_att=w_att, w_ml=w_ml, w_gate=w_gate)

    (silu_c,) = small_call(lambda a: (_silu(a),), [c_all], [jax.ShapeDtypeStruct(c_all.shape, F32)], "silu_c")
    b_ada_cols = lax.dynamic_slice(b_ada, (0, me * 768), (1, 768))
    ada_cols = matmul(silu_c, w_ada[0], mode="nn", name="ada", extras=(jnp.broadcast_to(b_ada_cols, (N_DEV * B, 768)),),
                      epi=lambda acc, bb: (acc + bb,))
    (ada_t,) = all_to_all([ada_cols.reshape(N_DEV, B, 768)], "ada_exchange")
    ada = ada_t.transpose(1, 0, 2).reshape(B, 6 * D_MODEL)
    mods = tuple(ada[:, i * D_MODEL:(i + 1) * D_MODEL].reshape(B, 1, D_MODEL) for i in range(6))

    late_handle, late_order = exchange_start(shards[1:], "gather_late_start", False, after=(ada_t,))

    def late_w(after):
        lands = exchange_wait(late_handle, after, "gather_late_wait", False)
        gw = dict(zip(big_names[1:], [_own_slot(l, s) for l, s in zip(lands, shards[1:])]))
        return dict(w_att_out=gw["w_att_out"].transpose(1, 0, 2).reshape(512, D_MODEL),
                    w_ml_out=gw["w_ml_out"].reshape(D_MODEL, D_MODEL), w_out=gw["w_out"].reshape(D_MODEL, D_MODEL),
                    w_ff1=gw["w_ff1"].transpose(1, 0, 2).reshape(D_MODEL, D_FF),
                    w_ff2=gw["w_ff2"].reshape(D_FF, D_MODEL))

    pending = {}

    def send_grads(key, blocks, name):
        handle, order = exchange_start(blocks, name, True)
        pending[key] = (handle, [lax.dynamic_index_in_dim(b, me, 0, keepdims=False) for b in blocks])
        return order

    def early_g(g):
        return send_grads("late", [g["w_att_out"].reshape(512, N_DEV, 128).transpose(1, 0, 2),
                                   g["w_ml_out"].reshape(N_DEV, 128, D_MODEL), g["w_out"].reshape(N_DEV, 128, D_MODEL),
                                   g["w_ff1"].reshape(D_MODEL, N_DEV, 512).transpose(1, 0, 2),
                                   g["w_ff2"].reshape(N_DEV, 512, D_MODEL)], "grad_late_start")

    def w_in_g(parts):
        return send_grads("w_in", [g.reshape(N_DEV, D_MODEL // N_DEV, g.shape[1]) for g in parts], "grad_w_in_start")

    def recv_grads(key, after, name):
        handle, own = pending[key]
        return [_own_slot(l, o) for l, o in zip(exchange_wait(handle, after, name, True), own)]

    small = dict(norm1_g=norm1_g + late_order[0, 0], norm2_g=norm2_g, b_if=b_if[0], conv_w=conv_w_full, conv_b=conv_b,
                 q_norm_g=q_norm_g, k_norm_g=k_norm_g, rel_bias=rel_bias, mlstm_norm_g=mlstm_norm_g)
    loss, grad_x, d_mods, _, small_g = local_step(x, loss_target, mods, w, small, late_w, early_g, w_in_g)

    d_ada = jnp.concatenate([d.reshape(B, D_MODEL) for d in d_mods], axis=1)
    (small_g["b_ada"],) = small_call(lambda a: (jnp.sum(a, axis=0, keepdims=True),), [d_ada],
                                     [jax.ShapeDtypeStruct((1, 6144), F32)], "g_b_ada_local")
    small_g["loss"] = loss
    slabs = sum_slabs(recv_grads("w_in", grad_x, "grad_w_in_wait"), "w_in_slab_sum")
    w_in_cols = merge_w_in(*slabs).reshape(D_MODEL // N_DEV, N_DEV, W_IN_SHARD).transpose(1, 0, 2)
    d_ada_t, small_parts, g_w_in = _exchange(
        [d_ada.reshape(B, N_DEV, 768).transpose(1, 0, 2), _pack(small_g, _SMALL_SENT, _SMALL_ROWS), w_in_cols],
        "small_exchange", [True, False, True])
    d_ada_cols = d_ada_t.reshape(N_DEV * B, 768)
    g_w_ada = matmul(silu_c, d_ada_cols, mode="tn", name="g_w_ada")

    recv = [g_w_in.reshape(D_MODEL, W_IN_SHARD)] + recv_grads("late", grad_x, "grad_late_wait")
    small_sum = sum_parts(small_parts, "small_grad_sum")
    sg = _unpack(small_sum, _SMALL_SENT, {**{k: shapes[k] for k, _ in _REPL}, "conv_w": (4, 1024), "loss": ()})
    loss = sg["loss"]

    G, Dl, NM, NV = {}, {}, {}, {}
    for k, parts in zip(big_names, recv):
        g, d, nm, nv = adamw(P[k][0], parts, M[k][0], V[k][0], f"adamw_{k}", parts=k != "w_in")
        G[k], Dl[k], NM[k], NV[k] = g[None], d[None], nm[None], nv[None]
    g, d, nm, nv = adamw(w_ada[0], g_w_ada, m_w_ada[0], v_w_ada[0], "adamw_w_ada")
    G["w_ada"], Dl["w_ada"], NM["w_ada"], NV["w_ada"] = g[None], d[None], nm[None], nv[None]
    g_conv = lax.dynamic_slice(sg["conv_w"], (0, me * 128), (4, 128))
    g, d, nm, nv = adamw(conv_w[0], g_conv, m_conv_w[0], v_conv_w[0], "adamw_conv_w")
    G["conv_w"], Dl["conv_w"], NM["conv_w"], NV["conv_w"] = g[None], d[None], nm[None], nv[None]
    flat2 = lambda a: a.reshape(-1, a.shape[-1])
    keys = [k for k, _ in _REPL]
    upd = adamw_many([flat2(P[k]) for k in keys], [flat2(sg[k]) for k in keys], [flat2(M[k]) for k in keys],
                     [flat2(V[k]) for k in keys], "adamw_small")
    for j, k in enumerate(keys):
        G[k] = sg[k]
        Dl[k], NM[k], NV[k] = [upd[3 * j + t].reshape(shapes[k]) for t in range(3)]

    return (loss, grad_x, *[G[k] for k in names], *[Dl[k] for k in names], *[NM[k] for k in names],
            *[NV[k] for k in names])
```

```python
import functools
import math

import numpy as np
import jax
import jax.numpy as jnp
from jax import lax
from jax.experimental import pallas as pl
from jax.experimental.pallas import tpu as pltpu

F32 = jnp.float32
BF16 = jnp.bfloat16

N_DEV = 8
D_MODEL = 1024
SEQ = 2048
ATT_GROUPS = ((128, 1), (512, 4), (2048, 16))
N_ATT_HEADS = 12
ATT_BLOCK = 128
HEAD_DIM = 128
ML_HEADS = 8
ML_PAIRS = 4
ML_CHUNK = 64
N_CHUNKS = SEQ // ML_CHUNK
N_BUCKETS = 32
MAX_DISTANCE = 2048
D_FF = 4096
D_IN = 9744
EPS = 1e-6

ADAM_LR = 0.001
ADAM_B1 = 0.9
ADAM_B2 = 0.999
ADAM_EPS = 1e-08
ADAM_WD = 0.01
ADAM_STEP = 10

ATT_HEAD_COLS = 3 * HEAD_DIM
ATT_COLS = N_ATT_HEADS * ATT_HEAD_COLS
ML_PAIR_COLS = 896
ML_COLS = ML_PAIRS * ML_PAIR_COLS
GATE_COLS = 2 * D_MODEL
W_IN_SHARD = D_IN // N_DEV

VMEM_LIMIT = 60 * 1024 * 1024


def _cparams(**kw):
    return pltpu.CompilerParams(vmem_limit_bytes=VMEM_LIMIT, **kw)


_NN = ((1,), (0,))
_NT = ((1,), (1,))
_TN = ((0,), (0,))


def _mxu(a, b, dims):
    return lax.dot_general(a.astype(BF16), b.astype(BF16), (dims, ((), ())), preferred_element_type=F32)


@jax.custom_vjp
def bdot_nn(a, b):
    return _mxu(a, b, _NN)


def _nn_fwd(a, b):
    return _mxu(a, b, _NN), (a, b)


def _nn_bwd(res, g):
    a, b = res
    return _mxu(g, b, _NT), _mxu(a, g, _TN)


bdot_nn.defvjp(_nn_fwd, _nn_bwd)


@jax.custom_vjp
def bdot_nt(a, b):
    return _mxu(a, b, _NT)


def _nt_fwd(a, b):
    return _mxu(a, b, _NT), (a, b)


def _nt_bwd(res, g):
    a, b = res
    return _mxu(g, b, _NN), _mxu(g, a, _TN)


bdot_nt.defvjp(_nt_fwd, _nt_bwd)


def _doth(a, b, dims=_NN):
    return lax.dot_general(a, b, (dims, ((), ())), precision=lax.Precision.HIGHEST, preferred_element_type=F32)


def _rms(x):
    return x * lax.rsqrt(jnp.mean(x * x, axis=-1, keepdims=True) + EPS)


def _pick(n, cands):
    for t in cands:
        if n % t == 0:
            return t
    raise ValueError(f"no tile for {n}")


MM_TILE_M = (1024, 512, 256, 128, 64, 32, 16, 8)
MM_TILE_N = (2048, 1792, 1536, 1024, 768, 512, 256, 128)
MM_TILE_K = (2048, 1792, 1536, 1024, 512, 256, 128, 64, 32)

def matmul(a, b, *, mode, name, out_dtypes=(F32,), epi=None, extras=(), after=()):
    if mode == "nn":
        (M, K), (K2, N) = a.shape, b.shape
    elif mode == "nt":
        (M, K), (N, K2) = a.shape, b.shape
    else:
        (K, M), (K2, N) = a.shape, b.shape
    assert K == K2, (a.shape, b.shape, mode)
    tm = _pick(M, MM_TILE_M)
    tn = _pick(N, MM_TILE_N)
    tk = _pick(K, MM_TILE_K)
    nk = K // tk
    n_ex = len(extras)
    n_out = len(out_dtypes)
    dims = {"nn": _NN, "nt": _NT, "tn": _TN}[mode]

    def finish(r, ex_refs, out_refs):
        outs = epi(r, *[e[...] for e in ex_refs]) if epi is not None else (r,)
        for o_ref, o in zip(out_refs, outs):
            o_ref[...] = o.astype(o_ref.dtype)

    def body(*refs):
        a_ref, b_ref = refs[0], refs[1]
        ex_refs = refs[2:2 + n_ex]
        out_refs = refs[2 + n_ex + len(after):2 + n_ex + len(after) + n_out]
        if nk == 1:
            finish(_mxu(a_ref[...], b_ref[...], dims), ex_refs, out_refs)
            return
        acc = refs[2 + n_ex + len(after) + n_out]
        k = pl.program_id(2)

        @pl.when(k == 0)
        def _():
            acc[...] = jnp.zeros_like(acc)

        acc[...] += _mxu(a_ref[...], b_ref[...], dims)

        @pl.when(k == nk - 1)
        def _():
            finish(acc[...], ex_refs, out_refs)

    if mode == "nn":
        a_spec = pl.BlockSpec((tm, tk), lambda i, j, k: (i, k))
        b_spec = pl.BlockSpec((tk, tn), lambda i, j, k: (k, j))
    elif mode == "nt":
        a_spec = pl.BlockSpec((tm, tk), lambda i, j, k: (i, k))
        b_spec = pl.BlockSpec((tn, tk), lambda i, j, k: (j, k))
    else:
        a_spec = pl.BlockSpec((tk, tm), lambda i, j, k: (k, i))
        b_spec = pl.BlockSpec((tk, tn), lambda i, j, k: (k, j))
    o_spec = pl.BlockSpec((tm, tn), lambda i, j, k: (i, j))
    res = pl.pallas_call(
        body,
        name=name,
        grid=(M // tm, N // tn, nk),
        in_specs=[a_spec, b_spec] + [o_spec] * n_ex + [pl.BlockSpec(memory_space=pl.ANY)] * len(after),
        out_specs=[o_spec] * n_out,
        out_shape=[jax.ShapeDtypeStruct((M, N), dt) for dt in out_dtypes],
        scratch_shapes=[pltpu.VMEM((tm, tn), F32)] if nk > 1 else [],
        compiler_params=_cparams(),
    )(a, b, *extras, *after)
    return res[0] if n_out == 1 else tuple(res)


ROW_MM_TILE = 512


def row_matmul(a, b, *, mode, name, extras, outs, epi):
    (M, K) = a.shape
    N = b.shape[1] if mode == "nn" else b.shape[0]
    tm = ROW_MM_TILE
    tk = _pick(K, MM_TILE_K)
    nk = K // tk
    n_ex, n_out = len(extras), len(outs)

    def body(*refs):
        a_ref, b_ref = refs[0], refs[1]
        ex_refs = refs[2:2 + n_ex]
        out_refs = refs[2 + n_ex:2 + n_ex + n_out]
        i = pl.program_id(0)
        dims = _NN if mode == "nn" else _NT
        if nk == 1:
            epi(_mxu(a_ref[...], b_ref[...], dims), i, ex_refs, out_refs)
            return
        acc = refs[2 + n_ex + n_out]
        k = pl.program_id(1)

        @pl.when(k == 0)
        def _():
            acc[...] = jnp.zeros_like(acc)

        acc[...] += _mxu(a_ref[...], b_ref[...], dims)

        @pl.when(k == nk - 1)
        def _():
            epi(acc[...], i, ex_refs, out_refs)

    def lift(index_map):
        return lambda i, k: index_map(i)

    b_spec = pl.BlockSpec((tk, N), lambda i, k: (k, 0)) if mode == "nn" else pl.BlockSpec((N, tk), lambda i, k: (0, k))
    res = pl.pallas_call(
        body, name=name, grid=(M // tm, nk),
        in_specs=[pl.BlockSpec((tm, tk), lambda i, k: (i, k)), b_spec]
        + [pl.BlockSpec(blk, lift(im)) for _, blk, im in extras],
        out_specs=[pl.BlockSpec(blk, lift(im)) for _, _, blk, im in outs],
        out_shape=[jax.ShapeDtypeStruct(shape, dt) for shape, dt, _, _ in outs],
        scratch_shapes=[pltpu.VMEM((tm, N), F32)] if nk > 1 else [],
        compiler_params=_cparams(),
    )(a, b, *[e[0] for e in extras])
    return tuple(res)


def _rows(arr):
    return (arr, (ROW_MM_TILE, arr.shape[1]), lambda i: (i, 0))


def _rows_out(T, dtype):
    return ((T, D_MODEL), dtype, (ROW_MM_TILE, D_MODEL), lambda i: (i, 0))


def _per_seq(arr):
    return (arr, (1, 1, D_MODEL), lambda i: (i // (SEQ // ROW_MM_TILE), 0, 0))


def _per_seq_out(B):
    return ((B, 1, D_MODEL), F32, (1, 1, D_MODEL), lambda i: (i // (SEQ // ROW_MM_TILE), 0, 0))


def _first_tile_of_seq(i):
    return i % (SEQ // ROW_MM_TILE) == 0


def small_call(fn, inputs, out_shapes, name):
    n_in = len(inputs)

    def body(*refs):
        outs = fn(*[r[...] for r in refs[:n_in]])
        for o_ref, o in zip(refs[n_in:], outs):
            o_ref[...] = o.astype(o_ref.dtype)

    res = pl.pallas_call(body, name=name, out_shape=list(out_shapes), compiler_params=_cparams())(*inputs)
    return tuple(res)


ROW_TILE = 512


def _modnorm(x, g, scale, shift):
    return _rms(x) * g * (1.0 + scale) + shift


def _row_spec(width):
    return pl.BlockSpec((1, ROW_TILE, width), lambda b, i: (b, i, 0))


def _mod_spec():
    return pl.BlockSpec((1, 1, D_MODEL), lambda b, i: (b, 0, 0))


def _vec_spec():
    return pl.BlockSpec((1, D_MODEL), lambda b, i: (0, 0))


def modnorm_fwd(x, g, scale, shift, name):
    B, S, D = x.shape

    def body(x_ref, g_ref, sc_ref, sh_ref, u_ref):
        u_ref[0] = _modnorm(x_ref[0], g_ref[...], sc_ref[0], sh_ref[0]).astype(BF16)

    return pl.pallas_call(
        body, name=name, grid=(B, S // ROW_TILE),
        in_specs=[_row_spec(D), _vec_spec(), _mod_spec(), _mod_spec()],
        out_specs=_row_spec(D),
        out_shape=jax.ShapeDtypeStruct((B, S, D), BF16),
        compiler_params=_cparams(),
    )(x, g, scale, shift)


def _gain_spec(g):
    return (g, (1, D_MODEL), lambda i: (0, 0))


def out_proj_resid_modnorm(z, w_out, x, gate, g, scale, shift, name):
    T = z.shape[0]

    def epi(acc, i, ex, out):
        x_ref, gt_ref, g_ref, sc_ref, sh_ref = ex
        y_ref, x1_ref, u_ref = out
        y_ref[...] = acc
        x1 = x_ref[...] + gt_ref[0] * acc
        x1_ref[...] = x1
        u_ref[...] = _modnorm(x1, g_ref[...], sc_ref[0], sh_ref[0]).astype(BF16)

    return row_matmul(z, w_out, mode="nn", name=name,
                      extras=[_rows(x), _per_seq(gate), _gain_spec(g), _per_seq(scale), _per_seq(shift)],
                      outs=[_rows_out(T, F32), _rows_out(T, F32), _rows_out(T, BF16)], epi=epi)


def ff2_loss(hdn, w_ff2, x1, gate, target, name):
    T = hdn.shape[0]

    def epi(acc, i, ex, out):
        x_ref, gt_ref, t_ref = ex
        dx_ref, dffo_ref, loss_ref, dg_ref = out

        @pl.when(i == 0)
        def _():
            loss_ref[...] = jnp.zeros_like(loss_ref)

        @pl.when(_first_tile_of_seq(i))
        def _():
            dg_ref[...] = jnp.zeros_like(dg_ref)

        err = x_ref[...] + gt_ref[0] * acc - t_ref[...]
        dx = err * (1.0 / D_MODEL)
        dx_ref[...] = dx
        dffo_ref[...] = (gt_ref[0] * dx).astype(BF16)
        loss_ref[...] += 0.5 * jnp.sum(jnp.mean(err * err, axis=-1, keepdims=True), axis=0, keepdims=True)
        dg_ref[0] += jnp.sum(dx * acc, axis=0, keepdims=True)

    return row_matmul(hdn, w_ff2, mode="nn", name=name,
                      extras=[_rows(x1), _per_seq(gate), _rows(target)],
                      outs=[_rows_out(T, F32), _rows_out(T, BF16), ((1, 1), F32, (1, 1), lambda i: (0, 0)),
                            _per_seq_out(T // SEQ)], epi=epi)


def d_u_modnorm_bwd(a, w, du_prev, x, g, scale, shift, dx_res, y, gate, name):
    T = a.shape[0]
    B = T // SEQ
    n_prev, resid = int(du_prev is not None), y is not None

    def epi(acc, i, ex, out):
        x_ref, g_ref, sc_ref, sh_ref, dr_ref = ex[n_prev:n_prev + 5]
        dx_ref, dg_ref, dsc_ref, dsh_ref = out[:4]

        @pl.when(i == 0)
        def _():
            dg_ref[...] = jnp.zeros_like(dg_ref)

        @pl.when(_first_tile_of_seq(i))
        def _():
            for r in out[2:4] + out[5:]:
                r[...] = jnp.zeros_like(r)

        du = acc + ex[0][...] if n_prev else acc
        _, vjp = jax.vjp(_modnorm, x_ref[...], g_ref[...], sc_ref[0], sh_ref[0])
        dxn, dg, dsc, dsh = vjp(du)
        dx = dxn + dr_ref[...]
        dx_ref[...] = dx
        dg_ref[...] += dg
        dsc_ref[0] += dsc
        dsh_ref[0] += dsh
        if resid:
            y_ref, gt_ref = ex[n_prev + 5:]
            out[4][...] = (gt_ref[0] * dx).astype(BF16)
            out[5][0] += jnp.sum(dx * y_ref[...], axis=0, keepdims=True)

    extras = ([_rows(du_prev)] if n_prev else []) + [_rows(x), _gain_spec(g), _per_seq(scale), _per_seq(shift),
                                                     _rows(dx_res)] + ([_rows(y), _per_seq(gate)] if resid else [])
    outs = [_rows_out(T, F32), ((1, D_MODEL), F32, (1, D_MODEL), lambda i: (0, 0)), _per_seq_out(B), _per_seq_out(B)]
    outs += [_rows_out(T, BF16), _per_seq_out(B)] if resid else []
    return row_matmul(a, w, mode="nt", name=name, extras=extras, outs=outs, epi=epi)


def _bucket_table(dilation):
    i = np.arange(ATT_BLOCK)[:, None]
    j = np.arange(2 * ATT_BLOCK)[None, :]
    delta = ATT_BLOCK + i - j
    dist = np.maximum(delta, 0) * dilation
    max_exact = N_BUCKETS // 2
    d = np.maximum(dist, max_exact).astype(np.float32)
    large = max_exact + (np.log(d / np.float32(max_exact)) / np.float32(math.log(MAX_DISTANCE / max_exact))
                         * np.float32(N_BUCKETS - max_exact)).astype(np.int32)
    large = np.minimum(large, N_BUCKETS - 1)
    return np.where(dist < max_exact, dist, large).astype(np.int32)


def _bucket_onehot(dilation):
    bt = jnp.asarray(_bucket_table(dilation).reshape(1, -1))
    return (bt == jnp.arange(N_BUCKETS, dtype=jnp.int32)[:, None]).astype(F32)


def bias_expand(rel_t, onehot, name):
    def fn(r, oh):
        return (_doth(r, oh),)
    return small_call(fn, [rel_t, onehot], [jax.ShapeDtypeStruct((rel_t.shape[0], onehot.shape[1]), F32)], name)[0]


def bias_reduce(dbias_flat, onehot, name):
    def fn(db, oh):
        return (_doth(db, oh, _NT),)
    return small_call(fn, [dbias_flat, onehot], [jax.ShapeDtypeStruct((dbias_flat.shape[0], N_BUCKETS), F32)], name)[0]


def _qk_norm(x, g):
    return _rms(x) * g


def _masked_bias(bias):
    i = lax.broadcasted_iota(jnp.int32, (ATT_BLOCK, 2 * ATT_BLOCK), 0)
    j = lax.broadcasted_iota(jnp.int32, (ATT_BLOCK, 2 * ATT_BLOCK), 1)
    bm = jnp.where(jnp.logical_and(j >= i, j <= i + ATT_BLOCK), bias, -jnp.inf)
    return bm, bm[:, ATT_BLOCK:]


def _attn_tile(qn, kn, v, bias):
    s = bdot_nt(qn, kn) * (HEAD_DIM ** -0.5) + bias
    m = lax.stop_gradient(jnp.max(s, axis=-1, keepdims=True))
    p = jnp.exp(s - m)
    l = jnp.sum(p, axis=-1, keepdims=True)
    o = bdot_nn(p, v) / l
    lse = jnp.broadcast_to(m + jnp.log(l), (ATT_BLOCK, HEAD_DIM))
    return o, lse


def _attn_tiles(dilation, rows=SEQ):
    nb = rows // dilation // ATT_BLOCK
    return [(r, n) for r in range(dilation) for n in range(nb)]


def _attn_rows(r, n, dilation, nblk=1):
    if dilation == 1:
        return pl.ds(r + n * ATT_BLOCK, nblk * ATT_BLOCK)
    return pl.ds(r + n * ATT_BLOCK * dilation, nblk * ATT_BLOCK, stride=dilation)


_QL, _KL, _VL = slice(0, 128), slice(128, 256), slice(256, 384)


def _qkv_specs(hb):
    return [pl.BlockSpec((None, SEQ, HEAD_DIM), functools.partial(lambda b, h, j: (b, 0, 3 * (hb + h) + j), j=j))
            for j in range(3)]


def attn_fwd(pa, bias, qg, kg, group, name):
    B = pa.shape[0]
    dilation = ATT_GROUPS[group][1]
    hb = group * 4

    def body(q_ref, k_ref, v_ref, b_ref, qg_ref, kg_ref, o_ref, l_ref, qn_s, kn_s):
        qn_s[...] = _qk_norm(q_ref[...], qg_ref[...])
        kn_s[...] = _qk_norm(k_ref[...], kg_ref[...])
        bias_all, bias_first = _masked_bias(b_ref[0])
        for (r, n) in _attn_tiles(dilation):
            rows = _attn_rows(r, n, dilation)
            if n == 0:
                krows, bias_t = rows, bias_first
            else:
                krows, bias_t = _attn_rows(r, n - 1, dilation, 2), bias_all
            o, lse = _attn_tile(qn_s[rows, :], kn_s[krows, :], v_ref[krows, :], bias_t)
            o_ref[rows, :] = o
            l_ref[rows, :] = lse

    head_out = pl.BlockSpec((None, SEQ, HEAD_DIM), lambda b, h: (b, 0, h))
    return pl.pallas_call(
        body, name=name, grid=(B, 4),
        in_specs=_qkv_specs(hb) + [
                  pl.BlockSpec((1, ATT_BLOCK, 2 * ATT_BLOCK), lambda b, h: (hb + h, 0, 0)),
                  pl.BlockSpec((1, HEAD_DIM), lambda b, h: (0, 0)),
                  pl.BlockSpec((1, HEAD_DIM), lambda b, h: (0, 0))],
        out_specs=[head_out, head_out],
        out_shape=[jax.ShapeDtypeStruct((B, SEQ, 512), F32), jax.ShapeDtypeStruct((B, SEQ, 512), F32)],
        scratch_shapes=[pltpu.VMEM((SEQ, HEAD_DIM), F32)] * 2,
        compiler_params=_cparams(),
    )(pa, pa, pa, bias, qg, kg)


def attn_bwd(pa, bias, qg, kg, do, dlse, dpa, group, name):
    B = pa.shape[0]
    dilation = ATT_GROUPS[group][1]
    hb = group * 4

    def body(q_ref, k_ref, v_ref, b_ref, qg_ref, kg_ref, do_ref, dl_ref, dpa_in,
             dp_ref, db_ref, dqg_ref, dkg_ref, qn_s, kn_s, dq_s, dk_s, dv_s):
        del dpa_in
        h_id = pl.program_id(1)

        @pl.when(jnp.logical_and(pl.program_id(0) == 0, h_id == 0))
        def _():
            db_ref[...] = jnp.zeros_like(db_ref)
            dqg_ref[...] = jnp.zeros_like(dqg_ref)
            dkg_ref[...] = jnp.zeros_like(dkg_ref)

        dk_s[...] = jnp.zeros_like(dk_s)
        dv_s[...] = jnp.zeros_like(dv_s)
        qn_s[...] = _qk_norm(q_ref[...], qg_ref[...])
        kn_s[...] = _qk_norm(k_ref[...], kg_ref[...])
        bias_all, bias_first = _masked_bias(b_ref[0])
        for (r, n) in _attn_tiles(dilation):
            rows = _attn_rows(r, n, dilation)
            if n == 0:
                krows, bias_t = rows, bias_first
            else:
                krows, bias_t = _attn_rows(r, n - 1, dilation, 2), bias_all
            _, vjp = jax.vjp(_attn_tile, qn_s[rows, :], kn_s[krows, :], v_ref[krows, :], bias_t)
            dqn, dkn, dv, dbias = vjp((do_ref[rows, :], dl_ref[rows, :]))
            dq_s[rows, :] = dqn
            dk_s[krows, :] += dkn
            dv_s[krows, :] += dv
            if n == 0:
                db_ref[h_id, :, ATT_BLOCK:] += dbias
            else:
                db_ref[h_id] += dbias
        for x_ref, g_ref, d_s, dg_ref, lanes in ((q_ref, qg_ref, dq_s, dqg_ref, _QL), (k_ref, kg_ref, dk_s, dkg_ref, _KL)):
            _, vjp = jax.vjp(_qk_norm, x_ref[...], g_ref[...])
            dx, dg = vjp(d_s[...])
            dp_ref[0, :, lanes] = dx.astype(BF16)
            dg_ref[...] += dg
        dp_ref[0, :, _VL] = dv_s[...].astype(BF16)

    const2 = lambda b, h: (0, 0)
    head_in = pl.BlockSpec((None, SEQ, HEAD_DIM), lambda b, h: (b, 0, h))
    head_blk = pl.BlockSpec((1, SEQ, ATT_HEAD_COLS), lambda b, h: (b, 0, hb + h))
    return pl.pallas_call(
        body, name=name, grid=(B, 4),
        in_specs=_qkv_specs(hb) + [
                  pl.BlockSpec((1, ATT_BLOCK, 2 * ATT_BLOCK), lambda b, h: (hb + h, 0, 0)),
                  pl.BlockSpec((1, HEAD_DIM), const2), pl.BlockSpec((1, HEAD_DIM), const2),
                  head_in, head_in,
                  pl.BlockSpec(memory_space=pl.ANY)],
        out_specs=[head_blk,
                   pl.BlockSpec((4, ATT_BLOCK, 2 * ATT_BLOCK), lambda b, h: (0, 0, 0)),
                   pl.BlockSpec((1, HEAD_DIM), const2), pl.BlockSpec((1, HEAD_DIM), const2)],
        out_shape=[jax.ShapeDtypeStruct(dpa.shape, BF16),
                   jax.ShapeDtypeStruct((4, ATT_BLOCK, 2 * ATT_BLOCK), F32),
                   jax.ShapeDtypeStruct((1, HEAD_DIM), F32), jax.ShapeDtypeStruct((1, HEAD_DIM), F32)],
        scratch_shapes=[pltpu.VMEM((SEQ, HEAD_DIM), F32)] * 5,
        input_output_aliases={8: 0},
        compiler_params=_cparams(),
    )(pa, pa, pa, bias, qg, kg, do, dlse, dpa)


def _attn_classes(q, k, v, bias, qg, kg):
    s = cdot_nt(_qk_norm(q, qg), _qk_norm(k, kg)) * (HEAD_DIM ** -0.5) + bias
    m = lax.stop_gradient(jnp.max(s, axis=-1, keepdims=True))
    p = jnp.exp(s - m)
    l = jnp.sum(p, axis=-1, keepdims=True)
    o = cdot_nn(p, v) / l
    return o, jnp.broadcast_to(m + jnp.log(l), o.shape)


def _gather_classes(src_ref, dst_s, dilation):
    for r in range(dilation):
        dst_s[r] = src_ref[pl.ds(r, ATT_BLOCK, stride=dilation), :]


def _scatter_classes(src_s, dst_ref, dilation):
    for r in range(dilation):
        dst_ref[pl.ds(r, ATT_BLOCK, stride=dilation), :] = src_s[r]


def attn_fwd_classes(pa, bias, qg, kg, group, name):
    B = pa.shape[0]
    dilation = ATT_GROUPS[group][1]
    hb = group * 4

    def body(q_ref, k_ref, v_ref, b_ref, qg_ref, kg_ref, o_ref, l_ref, q_s, k_s, v_s):
        _gather_classes(q_ref, q_s, dilation)
        _gather_classes(k_ref, k_s, dilation)
        _gather_classes(v_ref, v_s, dilation)
        o, lse = _attn_classes(q_s[...], k_s[...], v_s[...], _masked_bias(b_ref[0])[1], qg_ref[...], kg_ref[...])
        q_s[...], k_s[...] = o, lse
        _scatter_classes(q_s, o_ref, dilation)
        _scatter_classes(k_s, l_ref, dilation)

    head_out = pl.BlockSpec((None, SEQ, HEAD_DIM), lambda b, h: (b, 0, h))
    return pl.pallas_call(
        body, name=name, grid=(B, 4),
        in_specs=_qkv_specs(hb) + [
                  pl.BlockSpec((1, ATT_BLOCK, 2 * ATT_BLOCK), lambda b, h: (hb + h, 0, 0)),
                  pl.BlockSpec((1, HEAD_DIM), lambda b, h: (0, 0)),
                  pl.BlockSpec((1, HEAD_DIM), lambda b, h: (0, 0))],
        out_specs=[head_out, head_out],
        out_shape=[jax.ShapeDtypeStruct((B, SEQ, 512), F32), jax.ShapeDtypeStruct((B, SEQ, 512), F32)],
        scratch_shapes=[pltpu.VMEM((dilation, ATT_BLOCK, HEAD_DIM), F32)] * 3,
        compiler_params=_cparams(),
    )(pa, pa, pa, bias, qg, kg)


def attn_bwd_classes(pa, bias, qg, kg, do, dlse, dpa, group, name):
    B = pa.shape[0]
    dilation = ATT_GROUPS[group][1]
    hb = group * 4

    def body(q_ref, k_ref, v_ref, b_ref, qg_ref, kg_ref, do_ref, dl_ref, dpa_in,
             dp_ref, db_ref, dqg_ref, dkg_ref, q_s, k_s, v_s, do_s, dl_s, rows_s):
        del dpa_in
        h_id = pl.program_id(1)

        @pl.when(jnp.logical_and(pl.program_id(0) == 0, h_id == 0))
        def _():
            db_ref[...] = jnp.zeros_like(db_ref)
            dqg_ref[...] = jnp.zeros_like(dqg_ref)
            dkg_ref[...] = jnp.zeros_like(dkg_ref)

        for src, dst in ((q_ref, q_s), (k_ref, k_s), (v_ref, v_s), (do_ref, do_s), (dl_ref, dl_s)):
            _gather_classes(src, dst, dilation)
        _, vjp = jax.vjp(_attn_classes, q_s[...], k_s[...], v_s[...], _masked_bias(b_ref[0])[1],
                         qg_ref[...], kg_ref[...])
        dq, dk, dv, dbias, dqg, dkg = vjp((do_s[...], dl_s[...]))
        db_ref[h_id, :, ATT_BLOCK:] += dbias
        dqg_ref[...] += dqg
        dkg_ref[...] += dkg
        for d, lanes in ((dq, _QL), (dk, _KL), (dv, _VL)):
            q_s[...] = d
            _scatter_classes(q_s, rows_s, dilation)
            dp_ref[0, :, lanes] = rows_s[...].astype(BF16)

    const2 = lambda b, h: (0, 0)
    head_in = pl.BlockSpec((None, SEQ, HEAD_DIM), lambda b, h: (b, 0, h))
    head_blk = pl.BlockSpec((1, SEQ, ATT_HEAD_COLS), lambda b, h: (b, 0, hb + h))
    return pl.pallas_call(
        body, name=name, grid=(B, 4),
        in_specs=_qkv_specs(hb) + [
                  pl.BlockSpec((1, ATT_BLOCK, 2 * ATT_BLOCK), lambda b, h: (hb + h, 0, 0)),
                  pl.BlockSpec((1, HEAD_DIM), const2), pl.BlockSpec((1, HEAD_DIM), const2),
                  head_in, head_in,
                  pl.BlockSpec(memory_space=pl.ANY)],
        out_specs=[head_blk,
                   pl.BlockSpec((4, ATT_BLOCK, 2 * ATT_BLOCK), lambda b, h: (0, 0, 0)),
                   pl.BlockSpec((1, HEAD_DIM), const2), pl.BlockSpec((1, HEAD_DIM), const2)],
        out_shape=[jax.ShapeDtypeStruct(dpa.shape, BF16),
                   jax.ShapeDtypeStruct((4, ATT_BLOCK, 2 * ATT_BLOCK), F32),
                   jax.ShapeDtypeStruct((1, HEAD_DIM), F32), jax.ShapeDtypeStruct((1, HEAD_DIM), F32)],
        scratch_shapes=[pltpu.VMEM((dilation, ATT_BLOCK, HEAD_DIM), F32)] * 5 + [pltpu.VMEM((SEQ, HEAD_DIM), F32)],
        input_output_aliases={8: 0},
        compiler_params=_cparams(),
    )(pa, pa, pa, bias, qg, kg, do, dlse, dpa)


def _merge(o0, o1, o2, l0, l1, l2):
    mx = jnp.maximum(jnp.maximum(l0, l1), l2)
    e0, e1, e2 = jnp.exp(l0 - mx), jnp.exp(l1 - mx), jnp.exp(l2 - mx)
    den = e0 + e1 + e2
    return (e0 / den) * o0 + (e1 / den) * o1 + (e2 / den) * o2


def merge_att_out(os_, ls_, w_att_out, name):
    T = os_[0].shape[0]
    tile = pl.BlockSpec((ROW_MM_TILE, 512), lambda i: (i, 0))

    def body(o0, o1, o2, l0, l1, l2, w_ref, a_ref, y_ref):
        att = _merge(o0[...], o1[...], o2[...], l0[...], l1[...], l2[...]).astype(BF16)
        a_ref[...] = att
        y_ref[...] = _mxu(att, w_ref[...], _NN)

    return pl.pallas_call(
        body, name=name, grid=(T // ROW_MM_TILE,),
        in_specs=[tile] * 6 + [pl.BlockSpec((512, D_MODEL), lambda i: (0, 0))],
        out_specs=[tile, pl.BlockSpec((ROW_MM_TILE, D_MODEL), lambda i: (i, 0))],
        out_shape=[jax.ShapeDtypeStruct((T, 512), BF16), jax.ShapeDtypeStruct((T, D_MODEL), F32)],
        compiler_params=_cparams(),
    )(*os_, *ls_, w_att_out)


def d_att_merge_bwd(d_ya, w_att_out, os_, ls_, name):
    T = d_ya.shape[0]
    tile = lambda a: (a, (ROW_MM_TILE, 512), lambda i: (i, 0))

    def epi(acc, i, ex, out):
        _, vjp = jax.vjp(_merge, *[e[...] for e in ex])
        for o_ref, g in zip(out, vjp(acc)):
            o_ref[...] = g

    return row_matmul(d_ya, w_att_out, mode="nt", name=name, extras=[tile(a) for a in list(os_) + list(ls_)],
                      outs=[((T, 512), F32, (ROW_MM_TILE, 512), lambda i: (i, 0))] * 6, epi=epi)


def _gate_mix(ga, gm, ya, ym):
    return jax.nn.sigmoid(ga) * ya + jax.nn.sigmoid(gm) * ym


def _gate_halves(pg):
    return [(pg, (ROW_MM_TILE, D_MODEL), lambda i: (i, 0)), (pg, (ROW_MM_TILE, D_MODEL), lambda i: (i, 1))]


def ml_out_gate(hg, w_ml_out, pg, ya, name):
    T = hg.shape[0]

    def epi(acc, i, ex, out):
        ga, gm, ya_ref = ex
        out[0][...] = acc
        out[1][...] = _gate_mix(ga[...], gm[...], ya_ref[...], acc).astype(BF16)

    return row_matmul(hg, w_ml_out, mode="nn", name=name, extras=_gate_halves(pg) + [_rows(ya)],
                      outs=[_rows_out(T, F32), _rows_out(T, BF16)], epi=epi)


def d_z_gate_bwd(dy, w_out, pg, ya, ym, name):
    T = dy.shape[0]

    def epi(acc, i, ex, out):
        ga, gm, ya_ref, ym_ref = ex
        dpg_ref, dya_ref, dym_ref = out
        _, vjp = jax.vjp(_gate_mix, ga[...], gm[...], ya_ref[...], ym_ref[...])
        dga, dgm, dya, dym = vjp(acc)
        dpg_ref[:, :D_MODEL] = dga.astype(BF16)
        dpg_ref[:, D_MODEL:] = dgm.astype(BF16)
        dya_ref[...] = dya.astype(BF16)
        dym_ref[...] = dym.astype(BF16)

    return row_matmul(dy, w_out, mode="nt", name=name, extras=_gate_halves(pg) + [_rows(ya), _rows(ym)],
                      outs=[((T, GATE_COLS), BF16, (ROW_MM_TILE, GATE_COLS), lambda i: (i, 0)),
                            _rows_out(T, BF16), _rows_out(T, BF16)], epi=epi)


def _log_sigmoid(x):
    return jnp.minimum(x, 0.0) - jnp.log(1.0 + jnp.exp(-jnp.abs(x)))


def _head_mask(e):
    lane = lax.broadcasted_iota(jnp.int32, (1, 128), 1)
    return jnp.logical_and(lane >= e * 64, lane < (e + 1) * 64).astype(F32)


def _bmxu(a, b, ca, cb):
    return lax.dot_general(a.astype(BF16), b.astype(BF16), (((ca,), (cb,)), ((0,), (0,))), preferred_element_type=F32)


@jax.custom_vjp
def cdot_nt(a, b):
    return _bmxu(a, b, 2, 2)


cdot_nt.defvjp(lambda a, b: (_bmxu(a, b, 2, 2), (a, b)),
               lambda res, g: (_bmxu(g, res[1], 2, 1), _bmxu(g, res[0], 1, 1)))


@jax.custom_vjp
def cdot_nn(a, b):
    return _bmxu(a, b, 2, 1)


cdot_nn.defvjp(lambda a, b: (_bmxu(a, b, 2, 1), (a, b)),
               lambda res, g: (_bmxu(g, res[1], 2, 2), _bmxu(res[0], g, 1, 1)))


@jax.custom_vjp
def cdot_tn(a, b):
    return _bmxu(a, b, 1, 1)


cdot_tn.defvjp(lambda a, b: (_bmxu(a, b, 1, 1), (a, b)),
               lambda res, g: (_bmxu(res[1], g, 2, 2), _bmxu(res[0], g, 2, 1)))


def _top_bits(x):
    return lax.bitcast_convert_type(lax.bitcast_convert_type(x, jnp.uint32) & jnp.uint32(0xFFFF0000), F32)


def _split3(x):
    hi = _top_bits(x)
    r = x - hi
    mid = _top_bits(r)
    return hi, mid, r - mid


def _parts_in_lanes(col):
    hi, mid, lo = _split3(col)
    lane = lax.broadcasted_iota(jnp.int32, (1, 1, 8), 2)
    return jnp.where(lane == 0, hi, jnp.where(lane == 1, mid, jnp.where(lane == 2, lo, 0.0)))


def _parts_in_rows(row):
    hi, mid, lo = _split3(row)
    sub = lax.broadcasted_iota(jnp.int32, (1, 8, 1), 1)
    return jnp.where(sub == 0, hi, jnp.where(sub == 1, mid, jnp.where(sub == 2, lo, 0.0)))


def _chunk_matrix(kind, c):
    ri = lax.broadcasted_iota(jnp.int32, (c, ML_CHUNK, ML_CHUNK), 1)
    ci = lax.broadcasted_iota(jnp.int32, (c, ML_CHUNK, ML_CHUNK), 2)
    return {"eye": ri == ci, "lower": ri >= ci, "upper": ri <= ci}[kind].astype(F32)


def _col_col(kind, col):
    out = _bmxu(_chunk_matrix(kind, col.shape[0]), _parts_in_lanes(col), 2, 1)
    return jnp.sum(out, axis=-1, keepdims=True)


def _col_row(col):
    out = _bmxu(_parts_in_lanes(col), _chunk_matrix("eye", col.shape[0]), 1, 1)
    return jnp.sum(out, axis=1, keepdims=True)


def _row_col(row):
    out = _bmxu(_chunk_matrix("eye", row.shape[0]), _parts_in_rows(row), 2, 2)
    return jnp.sum(out, axis=-1, keepdims=True)


@jax.custom_vjp
def chunk_cumsum(col):
    return _col_col("lower", col)


chunk_cumsum.defvjp(lambda col: (_col_col("lower", col), None), lambda _, g: (_col_col("upper", g),))


@jax.custom_vjp
def col_to_row(col):
    return _col_row(col)


col_to_row.defvjp(lambda col: (_col_row(col), None), lambda _, g: (_row_col(g),))


def _gate_block(ifb):
    lane = lax.broadcasted_iota(jnp.int32, (1, 128), 1)
    return jnp.where(lane >= 2, _log_sigmoid(ifb), ifb)


def _ml_intra(q2, k2, v, ifb, *, e):
    c, L = q2.shape[0] // ML_CHUNK, ML_CHUNK
    hm = _head_mask(e)
    q3 = (q2 * hm).reshape(c, L, 128)
    k3 = (k2 * hm).reshape(c, L, 128)
    v3 = v.reshape(c, L, 128)
    if3 = ifb.reshape(c, L, 128)
    lanes = lax.broadcasted_iota(jnp.int32, (c, L, 128), 2)
    li = jnp.sum(jnp.where(lanes == e, if3, 0.0), axis=-1, keepdims=True)
    lf = jnp.sum(jnp.where(lanes == 2 + e, if3, 0.0), axis=-1, keepdims=True)
    b = chunk_cumsum(lf)
    last = lax.broadcasted_iota(jnp.int32, (1, L, 1), 1) == L - 1
    b_end = jnp.sum(jnp.where(last, b, 0.0), axis=1, keepdims=True)
    causal = lax.broadcasted_iota(jnp.int32, (L, L), 0) >= lax.broadcasted_iota(jnp.int32, (L, L), 1)
    Dm = jnp.where(causal, b + col_to_row(li - b), -jnp.inf)
    mD = lax.stop_gradient(jnp.max(Dm, axis=-1, keepdims=True))
    P0 = cdot_nt(q3, k3) * jnp.exp(Dm - mD)
    H0 = cdot_nn(P0, v3)
    r0 = jnp.sum(P0, axis=-1, keepdims=True)
    g = b_end - b + li
    mg = lax.stop_gradient(jnp.max(g, axis=1, keepdims=True))
    kw = jnp.exp(g - mg) * k3
    return H0, r0, cdot_tn(kw, v3), jnp.sum(kw, axis=1, keepdims=True), b, b_end, mD, mg


def _ml_inter(q2, mo, gn, H0, r0, b, C_in, n_in, *, mD, m_in, e):
    c, L = q2.shape[0] // ML_CHUNK, ML_CHUNK
    q3 = (q2 * _head_mask(e)).reshape(c, L, 128)
    a = b + m_in
    m_t = lax.stop_gradient(jnp.maximum(a, mD))
    c1 = jnp.exp(mD - m_t)
    c2 = jnp.exp(a - m_t)
    num = c1 * H0 + c2 * cdot_nn(q3, C_in)
    nq = c1 * r0 + c2 * jnp.sum(q3 * n_in, axis=-1, keepdims=True)
    h = num / jnp.maximum(jnp.abs(nq), jnp.exp(-m_t))
    hg = _rms(h) * gn * jax.nn.sigmoid(mo.reshape(c, L, 128))
    return hg.reshape(c * L, 128)


def _state_sweep(U_s, un_s, be_s, mg_s, Cin_s, nin_s, min_s, al_s, bt_s):
    def step(j, carry):
        C, n, m = carry
        Cin_s[j], nin_s[j], min_s[j] = C, n, m
        m_out = jnp.maximum(be_s[j] + m, mg_s[j])
        al = jnp.exp(be_s[j] + m - m_out)
        bt = jnp.exp(mg_s[j] - m_out)
        al_s[j], bt_s[j] = al, bt
        return al * C + bt * U_s[j], al * n + bt * un_s[j], m_out

    lax.fori_loop(0, N_CHUNKS, step, (jnp.zeros((128, 128), F32), jnp.zeros((1, 128), F32), jnp.zeros((1, 1), F32)))


def _state_sweep_bwd(U_s, un_s, dbe_s, Cin_s, nin_s, dCp_s, dnp_s, al_s, bt_s):
    def step(t, carry):
        j = N_CHUNKS - 1 - t
        dC, dn = carry
        al, bt = al_s[j], bt_s[j]
        U_s[j] = bt * dC
        un_s[j] = bt * dn
        dal = jnp.sum(jnp.sum(dC * Cin_s[j], axis=1, keepdims=True), axis=0, keepdims=True) \
            + jnp.sum(dn * nin_s[j], axis=1, keepdims=True)
        dbe_s[j] = dal * al
        return dCp_s[j] + al * dC, dnp_s[j] + al * dn

    lax.fori_loop(0, N_CHUNKS, step, (jnp.zeros((128, 128), F32), jnp.zeros((1, 128), F32)))


def _state_scratch():
    c = N_CHUNKS
    return [pltpu.VMEM((c, 128, 128), F32), pltpu.VMEM((c, 1, 128), F32), pltpu.VMEM((c, 1, 1), F32),
            pltpu.VMEM((c, 1, 1), F32),
            pltpu.VMEM((c, 128, 128), F32), pltpu.VMEM((c, 1, 128), F32), pltpu.VMEM((c, 1, 1), F32),
            pltpu.VMEM((c, 1, 1), F32), pltpu.VMEM((c, 1, 1), F32)]


def _shift_down(x, s):
    if s == 0:
        return x
    rows = lax.broadcasted_iota(jnp.int32, x.shape, 0)
    return jnp.where(rows >= s, pltpu.roll(x, s, 0), 0.0)


def _shift_up(x, s):
    if s == 0:
        return x
    S = x.shape[0]
    rows = lax.broadcasted_iota(jnp.int32, x.shape, 0)
    return jnp.where(rows < S - s, pltpu.roll(x, S - s, 0), 0.0)


def _conv_pre(x, cw, cb):
    y = cb + cw[3:4, :] * x
    for j in range(3):
        y = y + cw[j:j + 1, :] * _shift_down(x, 3 - j)
    return y


def _conv_bwd(x, cw, dpre):
    dx = cw[3:4, :] * dpre
    dcw = [None] * 4
    dcw[3] = jnp.sum(dpre * x, axis=0, keepdims=True)
    for j in range(3):
        dx = dx + cw[j:j + 1, :] * _shift_up(dpre, 3 - j)
        dcw[j] = jnp.sum(dpre * _shift_down(x, 3 - j), axis=0, keepdims=True)
    return dx, dcw, jnp.sum(dpre, axis=0, keepdims=True)


def _silu(z):
    return z * jax.nn.sigmoid(z)


def _dsilu(z):
    s = jax.nn.sigmoid(z)
    return s * (1.0 + z * (1.0 - s))


_ML_Q, _ML_K = slice(0, 128), slice(128, 256)
_ML_IF = slice(768, 896)


def _ml_v(e):
    return slice(256 + e * 128, 384 + e * 128)


def _ml_o(e):
    return slice(512 + e * 128, 640 + e * 128)


def _ml_specs():
    pair = lambda b, p: (b, 0, p)
    return [pl.BlockSpec((1, SEQ, ML_PAIR_COLS), pair),
            pl.BlockSpec((1, 4, 128), lambda b, p: (p, 0, 0)),
            pl.BlockSpec((1, 4, 128), lambda b, p: (4 + p, 0, 0)),
            pl.BlockSpec((1, 1, 128), lambda b, p: (p, 0, 0)),
            pl.BlockSpec((1, 1, 128), lambda b, p: (4 + p, 0, 0)),
            pl.BlockSpec((1, 1, 128), lambda b, p: (p, 0, 0)),
            pl.BlockSpec((1, 1, 256), lambda b, p: (p, 0, 0))]


def mlstm_fwd(pm, cw8, cb8, bifp, gn4, name):
    B = pm.shape[0]

    def body(p_ref, cwq, cwk, cbq, cbk, bif_ref, gn_ref, hg_ref, *st):
        U_s, un_s, be_s, mg_s, Cin_s, nin_s, min_s, al_s, bt_s = st
        qc = _silu(_conv_pre(p_ref[0, :, _ML_Q], cwq[0], cbq[0]))
        kc = _silu(_conv_pre(p_ref[0, :, _ML_K], cwk[0], cbk[0])) * (64 ** -0.5)
        ifb = _gate_block(p_ref[0, :, _ML_IF] + bif_ref[0])
        for e in range(2):
            lanes = slice(e * 128, (e + 1) * 128)
            H0, r0, U, un, b, b_end, mD, mg = _ml_intra(qc, kc, p_ref[0, :, _ml_v(e)], ifb, e=e)
            U_s[...], un_s[...], be_s[...], mg_s[...] = U, un, b_end, mg
            _state_sweep(*st)
            hg = _ml_inter(qc, p_ref[0, :, _ml_o(e)], gn_ref[0, :, lanes], H0, r0, b, Cin_s[...], nin_s[...],
                           mD=mD, m_in=min_s[...], e=e)
            hg_ref[0, :, lanes] = hg.astype(BF16)

    return pl.pallas_call(
        body, name=name, grid=(B, ML_PAIRS),
        in_specs=_ml_specs(),
        out_specs=pl.BlockSpec((1, SEQ, 256), lambda b, p: (b, 0, p)),
        out_shape=jax.ShapeDtypeStruct((B, SEQ, D_MODEL), BF16),
        scratch_shapes=_state_scratch(),
        compiler_params=_cparams(),
    )(pm, cw8, cw8, cb8, cb8, bifp, gn4)


def mlstm_bwd(pm, cw8, cb8, bifp, gn4, dhg, name):
    B = pm.shape[0]

    def body(p_ref, cwq, cwk, cbq, cbk, bif_ref, gn_ref, dh_ref,
             dp_ref, dcw_ref, dcb_ref, dbif_ref, dgn_ref, *scr):
        st = scr[:9]
        U_s, un_s, be_s, mg_s, Cin_s, nin_s, min_s, al_s, bt_s = st
        dCp_s, dnp_s, dbe_s = scr[9:]
        p_id = pl.program_id(1)

        @pl.when(jnp.logical_and(pl.program_id(0) == 0, p_id == 0))
        def _():
            dcw_ref[...] = jnp.zeros_like(dcw_ref)
            dcb_ref[...] = jnp.zeros_like(dcb_ref)
            dbif_ref[...] = jnp.zeros_like(dbif_ref)
            dgn_ref[...] = jnp.zeros_like(dgn_ref)

        pre_q = _conv_pre(p_ref[0, :, _ML_Q], cwq[0], cbq[0])
        pre_k = _conv_pre(p_ref[0, :, _ML_K], cwk[0], cbk[0])
        qc = _silu(pre_q)
        kc = _silu(pre_k) * (64 ** -0.5)
        ifb, gate_vjp = jax.vjp(_gate_block, p_ref[0, :, _ML_IF] + bif_ref[0])
        dq = jnp.zeros((SEQ, 128), F32)
        dk = jnp.zeros((SEQ, 128), F32)
        difb = jnp.zeros((SEQ, 128), F32)
        for e in range(2):
            lanes = slice(e * 128, (e + 1) * 128)
            (H0, r0, U, un, b, b_end, mD, mg), vjp1 = jax.vjp(functools.partial(_ml_intra, e=e), qc, kc,
                                                              p_ref[0, :, _ml_v(e)], ifb)
            U_s[...], un_s[...], be_s[...], mg_s[...] = U, un, b_end, mg
            _state_sweep(*st)
            _, vjp3 = jax.vjp(functools.partial(_ml_inter, mD=mD, m_in=min_s[...], e=e), qc, p_ref[0, :, _ml_o(e)],
                              gn_ref[0, :, lanes], H0, r0, b, Cin_s[...], nin_s[...])
            dq_a, dmo, dgn, dH0, dr0, db_a, dCp, dnp = vjp3(dh_ref[0, :, lanes])
            dCp_s[...], dnp_s[...] = dCp, dnp
            _state_sweep_bwd(U_s, un_s, dbe_s, Cin_s, nin_s, dCp_s, dnp_s, al_s, bt_s)
            dq_b, dk_b, dv, difb_e = vjp1((dH0, dr0, U_s[...], un_s[...], db_a, dbe_s[...],
                                           jnp.zeros_like(mD), jnp.zeros_like(mg)))
            dq, dk, difb = dq + dq_a + dq_b, dk + dk_b, difb + difb_e
            dp_ref[0, :, _ml_v(e)] = dv.astype(BF16)
            dp_ref[0, :, _ml_o(e)] = dmo.astype(BF16)
            dgn_ref[p_id, :, lanes] += dgn
        (difb,) = gate_vjp(difb)
        dp_ref[0, :, _ML_IF] = difb.astype(BF16)
        dbif_ref[p_id] += jnp.sum(difb, axis=0, keepdims=True)

        for (sl, cw, pre, d, blk, scale) in ((_ML_Q, cwq, pre_q, dq, p_id, 1.0), (_ML_K, cwk, pre_k, dk, 4 + p_id, 64 ** -0.5)):
            xr = p_ref[0, :, sl]
            dpre = d * scale * _dsilu(pre)
            dx, dcw, dcb = _conv_bwd(xr, cw[0], dpre)
            dp_ref[0, :, sl] = dx.astype(BF16)
            for j in range(4):
                dcw_ref[blk, j:j + 1, :] += dcw[j]
            dcb_ref[blk] += dcb

    full3 = lambda b, p: (0, 0, 0)
    return pl.pallas_call(
        body, name=name, grid=(B, ML_PAIRS),
        in_specs=[pl.BlockSpec((1, SEQ, ML_PAIR_COLS), lambda b, p: (b, 0, p), pipeline_mode=pl.Buffered(1))]
        + _ml_specs()[1:] + [pl.BlockSpec((1, SEQ, 256), lambda b, p: (b, 0, p), pipeline_mode=pl.Buffered(1))],
        out_specs=[pl.BlockSpec((1, SEQ, ML_PAIR_COLS), lambda b, p: (b, 0, p)),
                   pl.BlockSpec((8, 4, 128), full3), pl.BlockSpec((8, 1, 128), full3),
                   pl.BlockSpec((4, 1, 128), full3), pl.BlockSpec((4, 1, 256), full3)],
        out_shape=[jax.ShapeDtypeStruct((B, SEQ, ML_COLS), BF16),
                   jax.ShapeDtypeStruct((8, 4, 128), F32), jax.ShapeDtypeStruct((8, 1, 128), F32),
                   jax.ShapeDtypeStruct((4, 1, 128), F32), jax.ShapeDtypeStruct((4, 1, 256), F32)],
        scratch_shapes=_state_scratch() + [pltpu.VMEM((N_CHUNKS, 128, 128), F32), pltpu.VMEM((N_CHUNKS, 1, 128), F32),
                                           pltpu.VMEM((N_CHUNKS, 1, 1), F32)],
        compiler_params=_cparams(),
    )(pm, cw8, cw8, cb8, cb8, bifp, gn4, dhg)


def _adamw(w, g, m, v):
    m = ADAM_B1 * m + (1.0 - ADAM_B1) * g
    v = ADAM_B2 * v + (1.0 - ADAM_B2) * (g * g)
    m_hat = m / (1.0 - ADAM_B1 ** ADAM_STEP)
    v_hat = v / (1.0 - ADAM_B2 ** ADAM_STEP)
    delta = -ADAM_LR * (m_hat / (jnp.sqrt(v_hat) + ADAM_EPS) + ADAM_WD * w)
    return delta, m, v


def adamw(w, g, m, v, name, parts=False):
    R, C = w.shape
    if R % 8 == 0 or R * C * 4 <= (1 << 20):
        tr = _pick(R, (256, 128, 64, 32, 16, 8)) if R * C * 4 > (1 << 20) else R
        steps = R // tr
        spec = pl.BlockSpec((tr, C), lambda i: (i, 0))
        g_spec = pl.BlockSpec((N_DEV, tr, C), lambda i: (0, i, 0)) if parts else spec
    else:
        tc = _pick(C, (256, 128))
        steps = C // tc
        spec = pl.BlockSpec((R, tc), lambda i: (0, i))
        g_spec = pl.BlockSpec((N_DEV, R, tc), lambda i: (0, 0, i)) if parts else spec

    def body(w_ref, g_ref, m_ref, v_ref, go_ref, d_ref, mo_ref, vo_ref):
        if parts:
            g = g_ref[0].astype(F32)
            for k in range(1, N_DEV):
                g = g + g_ref[k].astype(F32)
        else:
            g = g_ref[...]
        d, mn, vn = _adamw(w_ref[...], g, m_ref[...], v_ref[...])
        go_ref[...], d_ref[...], mo_ref[...], vo_ref[...] = g, d, mn, vn

    return pl.pallas_call(
        body, name=name, grid=(steps,),
        in_specs=[spec, g_spec, spec, spec], out_specs=[spec] * 4,
        out_shape=[jax.ShapeDtypeStruct((R, C), F32)] * 4,
        compiler_params=_cparams(),
    )(w, g, m, v)


def adamw_many(ws, gs, ms, vs, name):
    n = len(ws)

    def fn(*a):
        out = []
        for j in range(n):
            out += list(_adamw(a[j], a[n + j], a[2 * n + j], a[3 * n + j]))
        return tuple(out)

    shapes = [jax.ShapeDtypeStruct(w.shape, F32) for w in ws for _ in range(3)]
    return small_call(fn, list(ws) + list(gs) + list(ms) + list(vs), shapes, name)


def _mesh_pos():
    return lax.axis_index("x"), lax.axis_index("y"), lax.axis_index("c")


def _flip(pos, f):
    x, y, c = pos
    return (1 - x if f & 4 else x, 1 - y if f & 2 else y, 1 - c if f & 1 else c)


def _index(pos):
    return 4 * pos[0] + 2 * pos[1] + pos[2]


def _exchange(arrs, name, scatter):
    n = len(arrs)
    scat = list(scatter) if isinstance(scatter, (list, tuple)) else [scatter] * n

    def body(*refs):
        ins, outs = refs[:n], refs[n:2 * n]
        send, recv, lsem = refs[2 * n:]
        me = _mesh_pos()
        mine = _index(me)
        copies = []
        for i in range(n):
            src = ins[i].at[mine] if scat[i] else ins[i]
            loc = pltpu.make_async_copy(src, outs[i].at[mine], lsem.at[i])
            loc.start()
            copies.append(loc)
            for f in range(1, N_DEV):
                peer = _flip(me, f)
                src = ins[i].at[_index(peer)] if scat[i] else ins[i]
                cp = pltpu.make_async_remote_copy(
                    src_ref=src, dst_ref=outs[i].at[mine],
                    send_sem=send.at[i * 7 + f - 1], recv_sem=recv.at[i * 7 + f - 1],
                    device_id=peer, device_id_type=pl.DeviceIdType.MESH)
                cp.start()
                copies.append(cp)
        for cp in copies:
            cp.wait()

    any_spec = pl.BlockSpec(memory_space=pl.ANY)
    out_shape = [jax.ShapeDtypeStruct(a.shape if s else (N_DEV,) + a.shape, a.dtype) for a, s in zip(arrs, scat)]
    res = pl.pallas_call(
        body, name=name,
        in_specs=[any_spec] * n, out_specs=[any_spec] * n, out_shape=out_shape,
        scratch_shapes=[pltpu.SemaphoreType.DMA((7 * n,)), pltpu.SemaphoreType.DMA((7 * n,)),
                        pltpu.SemaphoreType.DMA((n,))],
        compiler_params=_cparams(),
    )(*arrs)
    return list(res)


def all_gather(arrs, name):
    return _exchange(arrs, name, False)


def all_gather_two_level(arrs, name):
    n = len(arrs)

    def body(*refs):
        ins, outs = refs[:n], refs[n:2 * n]
        send, recv, lsem = refs[2 * n:]
        x, y, c = _mesh_pos()
        me, sibling = (x, y, c), (x, y, 1 - c)
        chips = [(1 - x, y), (x, 1 - y), (1 - x, 1 - y)]

        def copy(i, k, block, to, src=None):
            rows = outs[i].at[_index(block)]
            return pltpu.make_async_remote_copy(
                src_ref=rows if src is None else src, dst_ref=rows,
                send_sem=send.at[i * 7 + k], recv_sem=recv.at[i * 7 + k],
                device_id=to, device_id_type=pl.DeviceIdType.MESH)

        local = [pltpu.make_async_copy(ins[i], outs[i].at[_index(me)], lsem.at[i]) for i in range(n)]
        first = [copy(i, 0, me, sibling, src=ins[i]) for i in range(n)]
        first += [copy(i, 1 + j, me, (*chip, c), src=ins[i]) for i in range(n) for j, chip in enumerate(chips)]
        for cp in local + first:
            cp.start()
        passed = []
        for j, chip in enumerate(chips):
            for i in range(n):
                copy(i, 1 + j, (*chip, c), me).wait_recv()
                cp = copy(i, 4 + j, (*chip, c), sibling)
                cp.start()
                passed.append(cp)
        for i in range(n):
            copy(i, 0, sibling, me).wait_recv()
            for j, chip in enumerate(chips):
                copy(i, 4 + j, (*chip, 1 - c), me).wait_recv()
        for cp in first + passed:
            cp.wait_send()
        for cp in local:
            cp.wait()

    any_spec = pl.BlockSpec(memory_space=pl.ANY)
    res = pl.pallas_call(
        body, name=name,
        in_specs=[any_spec] * n, out_specs=[any_spec] * n,
        out_shape=[jax.ShapeDtypeStruct((N_DEV,) + a.shape, a.dtype) for a in arrs],
        scratch_shapes=[pltpu.SemaphoreType.DMA((7 * n,)), pltpu.SemaphoreType.DMA((7 * n,)),
                        pltpu.SemaphoreType.DMA((n,))],
        compiler_params=_cparams(),
    )(*arrs)
    return list(res)


def all_to_all(arrs, name):
    return _exchange(arrs, name, True)


_HBM = pl.BlockSpec(memory_space=pltpu.HBM)
_SEM = pl.BlockSpec(memory_space=pltpu.SEMAPHORE)
_EFFECT = pltpu.SideEffectType.DATAFLOW_SIDE_EFFECTING


def _split_copies(ins, lands, send, recv, scatter, waiting):
    me = _mesh_pos()
    mine = _index(me)
    copies = []
    for i in range(len(ins)):
        for f in range(1, N_DEV):
            peer = _flip(me, f)
            src = ins[i].at[_index(peer)] if scatter else ins[i]
            copies.append(pltpu.make_async_remote_copy(
                src_ref=src, dst_ref=lands[i].at[_index(peer) if waiting else mine],
                send_sem=send.at[i * 7 + f - 1], recv_sem=recv.at[i * 7 + f - 1],
                device_id=peer, device_id_type=pl.DeviceIdType.MESH))
    return copies


def exchange_start(arrs, name, scatter, after=()):
    n = len(arrs)
    land_shapes = [a.shape if scatter else (N_DEV,) + a.shape for a in arrs]

    def body(*refs):
        ins, lands = refs[:n], refs[n:2 * n]
        send, recv = refs[2 * n + len(after)], refs[2 * n + len(after) + 1]
        token = refs[-1]
        for cp in _split_copies(ins, lands, send, recv, scatter, False):
            cp.start()
        token[...] = jnp.zeros_like(token)

    res = pl.pallas_call(
        body, name=name,
        out_shape=(pltpu.SemaphoreType.DMA((7 * n,)), pltpu.SemaphoreType.DMA((7 * n,)),
                   *[pltpu.HBM(a.shape, a.dtype) for a in arrs],
                   *[pltpu.HBM(s, a.dtype) for s, a in zip(land_shapes, arrs)],
                   jax.ShapeDtypeStruct((8, 128), F32)),
        in_specs=[_HBM] * (2 * n) + [pl.BlockSpec(memory_space=pl.ANY)] * len(after),
        out_specs=(_SEM, _SEM, *[_HBM] * (2 * n), pl.BlockSpec(memory_space=pltpu.VMEM)),
        input_output_aliases={i: 2 + i for i in range(2 * n)},
        compiler_params=pltpu.CompilerParams(has_side_effects=_EFFECT),
    )(*[pltpu.with_memory_space_constraint(a, pltpu.HBM) for a in arrs],
      *[pltpu.with_memory_space_constraint(lax.empty(s, a.dtype), pltpu.HBM) for s, a in zip(land_shapes, arrs)],
      *after)
    return (res[0], res[1], list(res[2:2 + n]), list(res[2 + n:2 + 2 * n])), res[-1]


def exchange_wait(handle, after, name, scatter):
    send, recv, srcs, lands = handle
    n = len(srcs)

    def body(*refs):
        ins, lnd = refs[:n], refs[n:2 * n]
        send_, recv_ = refs[2 * n], refs[2 * n + 1]
        for cp in _split_copies(ins, lnd, send_, recv_, scatter, True):
            cp.wait_send()
            cp.wait_recv()

    res = pl.pallas_call(
        body, name=name,
        out_shape=(*[pltpu.HBM(a.shape, a.dtype) for a in srcs], *[pltpu.HBM(a.shape, a.dtype) for a in lands]),
        in_specs=[_HBM] * (2 * n) + [_SEM, _SEM, pl.BlockSpec(memory_space=pl.ANY)],
        out_specs=tuple([_HBM] * (2 * n)),
        input_output_aliases={i: i for i in range(2 * n)},
        compiler_params=pltpu.CompilerParams(has_side_effects=_EFFECT),
    )(*srcs, *lands, send, recv, after)
    return list(res[n:])


def _own_slot(land, own):
    return lax.dynamic_update_slice(land, own[None], (_index(_mesh_pos()),) + (0,) * own.ndim)


def cast_bf16(arrs, name):
    outs = []
    for i, a in enumerate(arrs):
        R, C = a.shape
        if R % 8 == 0:
            tr = _pick(R, (256, 128, 64, 32, 16, 8)) if R * C * 4 > (1 << 21) else R
            steps, spec = R // tr, pl.BlockSpec((tr, C), lambda i: (i, 0))
        else:
            steps, spec = C // 256, pl.BlockSpec((R, 256), lambda i: (0, i))

        def body(a_ref, o_ref):
            o_ref[...] = a_ref[...].astype(BF16)

        outs.append(pl.pallas_call(body, name=f"{name}_{i}", grid=(steps,), in_specs=[spec], out_specs=spec,
                                   out_shape=jax.ShapeDtypeStruct((R, C), BF16), compiler_params=_cparams())(a))
    return outs


def sum_slabs(parts, name):
    def fn(*a):
        outs = []
        for p in a:
            s = p[0].astype(F32)
            for k in range(1, N_DEV):
                s = s + p[k].astype(F32)
            outs.append(s)
        return tuple(outs)
    return list(small_call(fn, list(parts), [jax.ShapeDtypeStruct(p.shape[1:], F32) for p in parts], name))


def sum_parts(parts, name):
    def fn(p):
        g = p[0]
        for k in range(1, N_DEV):
            g = g + p[k]
        return (g,)
    return small_call(fn, [parts], [jax.ShapeDtypeStruct(parts.shape[1:], F32)], name)[0]


_SPLITS = np.cumsum([1536, 1536, 1536, 512, 512, 1024, 1024, 8, 8, 2048])[:-1].tolist()


def split_w_in(w):
    aq, ak, av, mq, mk, mv, mo, mi, mf, gates = jnp.split(w, _SPLITS, axis=1)
    R = w.shape[0]
    w_att = jnp.stack([aq.reshape(R, 12, 128), ak.reshape(R, 12, 128), av.reshape(R, 12, 128)], axis=2)
    gif = jnp.concatenate([mi.reshape(R, 4, 2), mf.reshape(R, 4, 2), jnp.zeros((R, 4, 124), w.dtype)], axis=2)
    w_ml = jnp.concatenate([mq.reshape(R, 4, 128), mk.reshape(R, 4, 128), mv.reshape(R, 4, 256),
                            mo.reshape(R, 4, 256), gif], axis=2)
    return w_att.reshape(R, ATT_COLS), w_ml.reshape(R, ML_COLS), gates


def merge_w_in(g_att, g_ml, g_gate):
    R = g_att.shape[0]
    a = g_att.reshape(R, 12, 3, 128)
    m = g_ml.reshape(R, 4, ML_PAIR_COLS)
    gif = m[:, :, 768:772]
    return jnp.concatenate([
        a[:, :, 0].reshape(R, 1536), a[:, :, 1].reshape(R, 1536), a[:, :, 2].reshape(R, 1536),
        m[:, :, 0:128].reshape(R, 512), m[:, :, 128:256].reshape(R, 512),
        m[:, :, 256:512].reshape(R, 1024), m[:, :, 512:768].reshape(R, 1024),
        gif[:, :, 0:2].reshape(R, 8), gif[:, :, 2:4].reshape(R, 8), g_gate], axis=1)


def _blk8(v, width=128):
    r = v.shape[0]
    return v.reshape(r, 1024 // width, width).transpose(1, 0, 2)


def _unblk8(v):
    nb, r, w = v.shape
    return v.transpose(1, 0, 2).reshape(r, nb * w)


def local_step(x, target, mods, w, small, late_w=None, early_g=None, w_in_g=None):
    late_w = late_w or (lambda after: w)
    big = {}
    early_g = early_g or (lambda g: big.update(g))
    w_in_g = w_in_g or (lambda g: big.update(w_in=merge_w_in(*g)))
    B = x.shape[0]
    T = B * SEQ
    shift1, scale1, gate1, shift2, scale2, gate2 = mods
    f2 = lambda a: a.reshape(T, a.shape[-1])
    f3 = lambda a: a.reshape(B, SEQ, a.shape[-1])

    rel_t = jnp.pad(small["rel_bias"].T, ((0, 4), (0, 0)))
    onehots = [_bucket_onehot(d) for _, d in ATT_GROUPS]
    biases = [bias_expand(rel_t, oh, f"bias_expand{g}").reshape(16, ATT_BLOCK, 2 * ATT_BLOCK)
              for g, oh in enumerate(onehots)]
    qg, kg = small["q_norm_g"], small["k_norm_g"]
    cw8 = _blk8(small["conv_w"])
    cb8 = _blk8(small["conv_b"])
    b_if = small["b_if"].reshape(2, 4, 2)
    bifp = jnp.concatenate([b_if[0], b_if[1], jnp.zeros((4, 124), F32)], axis=1).reshape(4, 1, 128)
    gn4 = small["mlstm_norm_g"].reshape(4, 1, 256)

    u = modnorm_fwd(x, small["norm1_g"], scale1, shift1, "modnorm1")
    u2d = f2(u)
    pa = f3(matmul(u2d, w["w_att"], mode="nn", name="proj_att"))
    pm = f3(matmul(u2d, w["w_ml"], mode="nn", name="proj_ml"))
    pg = matmul(u2d, w["w_gate"], mode="nn", name="proj_gate")
    os_, ls_ = [], []
    one_block = [SEQ // d == ATT_BLOCK for _, d in ATT_GROUPS]
    for g in range(3):
        o, l = (attn_fwd_classes if one_block[g] else attn_fwd)(pa, biases[g], qg, kg, g, f"attn_fwd{g}")
        os_.append(f2(o))
        ls_.append(f2(l))
    hg = mlstm_fwd(pm, cw8, cb8, bifp, gn4, "mlstm_fwd")
    w = {**w, **late_w(hg)}
    att, y_att = merge_att_out(os_, ls_, w["w_att_out"], "att_out")
    y_ml, z = ml_out_gate(f2(hg), w["w_ml_out"], pg, y_att, "ml_out")
    y, x1, u2 = out_proj_resid_modnorm(f2(z), w["w_out"], f2(x), gate1, small["norm2_g"], scale2, shift2, "out_proj")
    pre, hdn = matmul(u2, w["w_ff1"], mode="nn", name="ff1", out_dtypes=(BF16, BF16),
                      epi=lambda acc: (acc, jnp.square(jnp.maximum(acc, 0.0))))
    dx2, d_ffo, loss, d_gate2 = ff2_loss(hdn, w["w_ff2"], x1, gate2, f2(target), "ff2_loss")

    g_ff2 = matmul(hdn, d_ffo, mode="tn", name="g_ff2", out_dtypes=(BF16,))
    d_pre = matmul(d_ffo, w["w_ff2"], mode="nt", name="d_hdn", out_dtypes=(BF16,), extras=(pre,),
                   epi=lambda acc, p: (acc * (2.0 * jnp.maximum(p.astype(F32), 0.0)),))
    g_ff1 = matmul(u2, d_pre, mode="tn", name="g_ff1", out_dtypes=(BF16,))
    dx1, d_norm2, d_scale2, d_shift2, dy, d_gate1 = d_u_modnorm_bwd(
        d_pre, w["w_ff1"], None, x1, small["norm2_g"], scale2, shift2, dx2, y, gate1, "d_u2")
    g_out = matmul(f2(z), dy, mode="tn", name="g_out", out_dtypes=(BF16,))
    dpg, d_ya, d_ym = d_z_gate_bwd(dy, w["w_out"], pg, y_att, y_ml, "d_z")
    g_att_out = matmul(f2(att), f2(d_ya), mode="tn", name="g_att_out", out_dtypes=(BF16,))
    dmerge = d_att_merge_bwd(d_ya, w["w_att_out"], os_, ls_, "d_att")
    g_ml_out = matmul(f2(hg), f2(d_ym), mode="tn", name="g_ml_out", out_dtypes=(BF16,))
    d_hg = matmul(f2(d_ym), w["w_ml_out"], mode="nt", name="d_hg")
    started = early_g(dict(w_att_out=g_att_out, w_ml_out=g_ml_out, w_out=g_out, w_ff1=g_ff1, w_ff2=g_ff2))
    order = 0.0 if started is None else started[0, 0]
    dmerge = [f3(d) for d in dmerge]
    dpa = lax.empty((B, SEQ, ATT_COLS), BF16)
    d_rel = []
    d_qg = d_kg = None
    for g in range(3):
        dpa, dbias, dq_g, dk_g = (attn_bwd_classes if one_block[g] else attn_bwd)(
            pa, biases[g], qg + order, kg, dmerge[g], dmerge[3 + g], dpa, g, f"attn_bwd{g}")
        db8 = jnp.pad(dbias.reshape(4, -1), ((0, 4), (0, 0)))
        d_rel.append(bias_reduce(db8, onehots[g], f"bias_reduce{g}")[:4])
        d_qg = dq_g if d_qg is None else d_qg + dq_g
        d_kg = dk_g if d_kg is None else d_kg + dk_g
    dpm, dcw8, dcb8, dbifp, dgn4 = mlstm_bwd(pm, cw8, cb8, bifp, gn4 + order, f3(d_hg), "mlstm_bwd")
    g_w_att = matmul(u2d, f2(dpa), mode="tn", name="g_w_att", out_dtypes=(BF16,))
    g_w_ml = matmul(u2d, f2(dpm), mode="tn", name="g_w_ml", out_dtypes=(BF16,))
    g_w_gate = matmul(u2d, f2(dpg), mode="tn", name="g_w_gate", out_dtypes=(BF16,))
    started = w_in_g((g_w_att, g_w_ml, g_w_gate))
    du = matmul(f2(dpa), w["w_att"], mode="nt", name="d_u_att", after=() if started is None else (started,))
    du = matmul(f2(dpm), w["w_ml"], mode="nt", name="d_u_ml", extras=(du,), epi=lambda acc, e: (acc + e,))
    grad_x, d_norm1, d_scale1, d_shift1 = d_u_modnorm_bwd(
        f2(dpg), w["w_gate"], du, f2(x), small["norm1_g"], scale1, shift1, dx1, None, None, "d_u_gate")
    grad_x = f3(grad_x)

    d_mods = (d_shift1, d_scale1, d_gate1, d_shift2, d_scale2, d_gate2)
    dbif = dbifp.reshape(4, 128)
    small_g = dict(
        norm1_g=d_norm1, norm2_g=d_norm2,
        b_if=jnp.stack([dbif[:, 0:2].reshape(8), dbif[:, 2:4].reshape(8)]),
        conv_w=_unblk8(dcw8), conv_b=_unblk8(dcb8), q_norm_g=d_qg, k_norm_g=d_kg,
        rel_bias=jnp.concatenate(d_rel, axis=0).T,
        mlstm_norm_g=dgn4.reshape(1, 1024))
    return loss, grad_x, d_mods, big, small_g


_SMALL = (("b_ada", 6144), ("norm1_g", 1024), ("norm2_g", 1024), ("b_if", 16), ("conv_b", 1024),
          ("q_norm_g", 128), ("k_norm_g", 128), ("rel_bias", 384), ("mlstm_norm_g", 1024), ("conv_w", 4096))
_SMALL_ROWS = 120
_REPL = _SMALL[:-1]
_SMALL_SENT = _SMALL + (("loss", 1),)


def _pack(d, names, rows):
    flat = jnp.concatenate([d[k].reshape(-1) for k, _ in names])
    return jnp.pad(flat, (0, rows * 128 - flat.shape[0])).reshape(rows, 128)


def _unpack(slab, names, shapes):
    flat = slab.reshape(-1)
    out, off = {}, 0
    for k, nel in names:
        out[k] = flat[off:off + nel].reshape(shapes[k])
        off += nel
    return out


def kernel(x, c, w_ada, b_ada, norm1_g, norm2_g, w_in, b_if, conv_w, conv_b, q_norm_g, k_norm_g, rel_bias, mlstm_norm_g, w_att_out, w_ml_out, w_out, w_ff1, w_ff2, loss_target, m_w_ada, m_b_ada, m_norm1_g, m_norm2_g, m_w_in, m_b_if, m_conv_w, m_conv_b, m_q_norm_g, m_k_norm_g, m_rel_bias, m_mlstm_norm_g, m_w_att_out, m_w_ml_out, m_w_out, m_w_ff1, m_w_ff2, v_w_ada, v_b_ada, v_norm1_g, v_norm2_g, v_w_in, v_b_if, v_conv_w, v_conv_b, v_q_norm_g, v_k_norm_g, v_rel_bias, v_mlstm_norm_g, v_w_att_out, v_w_ml_out, v_w_out, v_w_ff1, v_w_ff2):
    P = dict(w_ada=w_ada, b_ada=b_ada, norm1_g=norm1_g, norm2_g=norm2_g, w_in=w_in, b_if=b_if, conv_w=conv_w,
             conv_b=conv_b, q_norm_g=q_norm_g, k_norm_g=k_norm_g, rel_bias=rel_bias, mlstm_norm_g=mlstm_norm_g,
             w_att_out=w_att_out, w_ml_out=w_ml_out, w_out=w_out, w_ff1=w_ff1, w_ff2=w_ff2)
    M = dict(w_ada=m_w_ada, b_ada=m_b_ada, norm1_g=m_norm1_g, norm2_g=m_norm2_g, w_in=m_w_in, b_if=m_b_if,
             conv_w=m_conv_w, conv_b=m_conv_b, q_norm_g=m_q_norm_g, k_norm_g=m_k_norm_g, rel_bias=m_rel_bias,
             mlstm_norm_g=m_mlstm_norm_g, w_att_out=m_w_att_out, w_ml_out=m_w_ml_out, w_out=m_w_out,
             w_ff1=m_w_ff1, w_ff2=m_w_ff2)
    V = dict(w_ada=v_w_ada, b_ada=v_b_ada, norm1_g=v_norm1_g, norm2_g=v_norm2_g, w_in=v_w_in, b_if=v_b_if,
             conv_w=v_conv_w, conv_b=v_conv_b, q_norm_g=v_q_norm_g, k_norm_g=v_k_norm_g, rel_bias=v_rel_bias,
             mlstm_norm_g=v_mlstm_norm_g, w_att_out=v_w_att_out, w_ml_out=v_w_ml_out, w_out=v_w_out,
             w_ff1=v_w_ff1, w_ff2=v_w_ff2)
    names = list(P)
    shapes = {k: P[k].shape for k in names}
    B = x.shape[0]
    me = _index(_mesh_pos())

    big_names = ("w_in", "w_att_out", "w_ml_out", "w_out", "w_ff1", "w_ff2")
    shards = cast_bf16([P[k][0] for k in big_names], "cast_w")
    w_in_g8, c8, conv_w8 = all_gather_two_level([shards[0], c, conv_w[0]], "gather_w_in")
    c_all = c8.reshape(N_DEV * B, D_MODEL)
    conv_w_full = conv_w8.transpose(1, 0, 2).reshape(4, 1024)
    w_att, w_ml, w_gate = split_w_in(w_in_g8.transpose(1, 0, 2).reshape(D_MODEL, D_IN))
    w = dict(w_att=w_att, w_ml=w_ml, w_gate=w_gate)

    (silu_c,) = small_call(lambda a: (_silu(a),), [c_all], [jax.ShapeDtypeStruct(c_all.shape, F32)], "silu_c")
    b_ada_cols = lax.dynamic_slice(b_ada, (0, me * 768), (1, 768))
    ada_cols = matmul(silu_c, w_ada[0], mode="nn", name="ada", extras=(jnp.broadcast_to(b_ada_cols, (N_DEV * B, 768)),),
                      epi=lambda acc, bb: (acc + bb,))
    (ada_t,) = all_to_all([ada_cols.reshape(N_DEV, B, 768)], "ada_exchange")
    ada = ada_t.transpose(1, 0, 2).reshape(B, 6 * D_MODEL)
    mods = tuple(ada[:, i * D_MODEL:(i + 1) * D_MODEL].reshape(B, 1, D_MODEL) for i in range(6))

    late_handle, late_order = exchange_start(shards[1:], "gather_late_start", False, after=(ada_t,))

    def late_w(after):
        lands = exchange_wait(late_handle, after, "gather_late_wait", False)
        gw = dict(zip(big_names[1:], [_own_slot(l, s) for l, s in zip(lands, shards[1:])]))
        return dict(w_att_out=gw["w_att_out"].transpose(1, 0, 2).reshape(512, D_MODEL),
                    w_ml_out=gw["w_ml_out"].reshape(D_MODEL, D_MODEL), w_out=gw["w_out"].reshape(D_MODEL, D_MODEL),
                    w_ff1=gw["w_ff1"].transpose(1, 0, 2).reshape(D_MODEL, D_FF),
                    w_ff2=gw["w_ff2"].reshape(D_FF, D_MODEL))

    pending = {}

    def send_grads(key, blocks, name):
        handle, order = exchange_start(blocks, name, True)
        pending[key] = (handle, [lax.dynamic_index_in_dim(b, me, 0, keepdims=False) for b in blocks])
        return order

    def early_g(g):
        return send_grads("late", [g["w_att_out"].reshape(512, N_DEV, 128).transpose(1, 0, 2),
                                   g["w_ml_out"].reshape(N_DEV, 128, D_MODEL), g["w_out"].reshape(N_DEV, 128, D_MODEL),
                                   g["w_ff1"].reshape(D_MODEL, N_DEV, 512).transpose(1, 0, 2),
                                   g["w_ff2"].reshape(N_DEV, 512, D_MODEL)], "grad_late_start")

    def w_in_g(parts):
        return send_grads("w_in", [g.reshape(N_DEV, D_MODEL // N_DEV, g.shape[1]) for g in parts], "grad_w_in_start")

    def recv_grads(key, after, name):
        handle, own = pending[key]
        return [_own_slot(l, o) for l, o in zip(exchange_wait(handle, after, name, True), own)]

    small = dict(norm1_g=norm1_g + late_order[0, 0], norm2_g=norm2_g, b_if=b_if[0], conv_w=conv_w_full, conv_b=conv_b,
                 q_norm_g=q_norm_g, k_norm_g=k_norm_g, rel_bias=rel_bias, mlstm_norm_g=mlstm_norm_g)
    loss, grad_x, d_mods, _, small_g = local_step(x, loss_target, mods, w, small, late_w, early_g, w_in_g)

    d_ada = jnp.concatenate([d.reshape(B, D_MODEL) for d in d_mods], axis=1)
    (small_g["b_ada"],) = small_call(lambda a: (jnp.sum(a, axis=0, keepdims=True),), [d_ada],
                                     [jax.ShapeDtypeStruct((1, 6144), F32)], "g_b_ada_local")
    small_g["loss"] = loss
    slabs = sum_slabs(recv_grads("w_in", grad_x, "grad_w_in_wait"), "w_in_slab_sum")
    w_in_cols = merge_w_in(*slabs).reshape(D_MODEL // N_DEV, N_DEV, W_IN_SHARD).transpose(1, 0, 2)
    d_ada_t, small_parts = _exchange(
        [d_ada.reshape(B, N_DEV, 768).transpose(1, 0, 2), _pack(small_g, _SMALL_SENT, _SMALL_ROWS)],
        "small_exchange", [True, False])
    cols_handle, _ = exchange_start([w_in_cols], "grad_w_in_cols_start", True, after=(small_parts,))
    d_ada_cols = d_ada_t.reshape(N_DEV * B, 768)
    g_w_ada = matmul(silu_c, d_ada_cols, mode="tn", name="g_w_ada")

    recv = recv_grads("late", grad_x, "grad_late_wait")
    small_sum = sum_parts(small_parts, "small_grad_sum")
    sg = _unpack(small_sum, _SMALL_SENT, {**{k: shapes[k] for k, _ in _REPL}, "conv_w": (4, 1024), "loss": ()})
    loss = sg["loss"]

    G, Dl, NM, NV = {}, {}, {}, {}
    for k, parts in zip(big_names[1:], recv):
        g, d, nm, nv = adamw(P[k][0], parts, M[k][0], V[k][0], f"adamw_{k}", parts=True)
        G[k], Dl[k], NM[k], NV[k] = g[None], d[None], nm[None], nv[None]
    g, d, nm, nv = adamw(w_ada[0], g_w_ada, m_w_ada[0], v_w_ada[0], "adamw_w_ada")
    G["w_ada"], Dl["w_ada"], NM["w_ada"], NV["w_ada"] = g[None], d[None], nm[None], nv[None]
    g_conv = lax.dynamic_slice(sg["conv_w"], (0, me * 128), (4, 128))
    g, d, nm, nv = adamw(conv_w[0], g_conv, m_conv_w[0], v_conv_w[0], "adamw_conv_w")
    G["conv_w"], Dl["conv_w"], NM["conv_w"], NV["conv_w"] = g[None], d[None], nm[None], nv[None]
    flat2 = lambda a: a.reshape(-1, a.shape[-1])
    keys = [k for k, _ in _REPL]
    upd = adamw_many([flat2(P[k]) for k in keys], [flat2(sg[k]) for k in keys], [flat2(M[k]) for k in keys],
                     [flat2(V[k]) for k in keys], "adamw_small")
    for j, k in enumerate(keys):
        G[k] = sg[k]
        Dl[k], NM[k], NV[k] = [upd[3 * j + t].reshape(shapes[k]) for t in range(3)]

    (g_w_in,) = exchange_wait(cols_handle, NV["w_ada"], "grad_w_in_cols_wait", True)
    g_w_in = _own_slot(g_w_in, lax.dynamic_index_in_dim(w_in_cols, me, 0, keepdims=False))
    g, d, nm, nv = adamw(w_in[0], g_w_in.reshape(D_MODEL, W_IN_SHARD), m_w_in[0], v_w_in[0], "adamw_w_in")
    G["w_in"], Dl["w_in"], NM["w_in"], NV["w_in"] = g[None], d[None], nm[None], nv[None]

    return (loss, grad_x, *[G[k] for k in names], *[Dl[k] for k in names], *[NM[k] for k in names],
            *[NV[k] for k in names])
```

```python
import functools
import math

import numpy as np
import jax
import jax.numpy as jnp
from jax import lax
from jax.experimental import pallas as pl
from jax.experimental.pallas import tpu as pltpu

F32 = jnp.float32
BF16 = jnp.bfloat16

N_DEV = 8
D_MODEL = 1024
SEQ = 2048
ATT_GROUPS = ((128, 1), (512, 4), (2048, 16))
N_ATT_HEADS = 12
ATT_BLOCK = 128
HEAD_DIM = 128
ML_HEADS = 8
ML_PAIRS = 4
ML_CHUNK = 64
N_CHUNKS = SEQ // ML_CHUNK
N_BUCKETS = 32
MAX_DISTANCE = 2048
D_FF = 4096
D_IN = 9744
EPS = 1e-6

ADAM_LR = 0.001
ADAM_B1 = 0.9
ADAM_B2 = 0.999
ADAM_EPS = 1e-08
ADAM_WD = 0.01
ADAM_STEP = 10

ATT_HEAD_COLS = 3 * HEAD_DIM
ATT_COLS = N_ATT_HEADS * ATT_HEAD_COLS
ML_PAIR_COLS = 896
ML_COLS = ML_PAIRS * ML_PAIR_COLS
GATE_COLS = 2 * D_MODEL
W_IN_SHARD = D_IN // N_DEV

VMEM_LIMIT = 60 * 1024 * 1024


def _cparams(**kw):
    return pltpu.CompilerParams(vmem_limit_bytes=VMEM_LIMIT, **kw)


_NN = ((1,), (0,))
_NT = ((1,), (1,))
_TN = ((0,), (0,))


def _mxu(a, b, dims):
    return lax.dot_general(a.astype(BF16), b.astype(BF16), (dims, ((), ())), preferred_element_type=F32)


@jax.custom_vjp
def bdot_nn(a, b):
    return _mxu(a, b, _NN)


def _nn_fwd(a, b):
    return _mxu(a, b, _NN), (a, b)


def _nn_bwd(res, g):
    a, b = res
    return _mxu(g, b, _NT), _mxu(a, g, _TN)


bdot_nn.defvjp(_nn_fwd, _nn_bwd)


@jax.custom_vjp
def bdot_nt(a, b):
    return _mxu(a, b, _NT)


def _nt_fwd(a, b):
    return _mxu(a, b, _NT), (a, b)


def _nt_bwd(res, g):
    a, b = res
    return _mxu(g, b, _NN), _mxu(g, a, _TN)


bdot_nt.defvjp(_nt_fwd, _nt_bwd)


def _doth(a, b, dims=_NN):
    return lax.dot_general(a, b, (dims, ((), ())), precision=lax.Precision.HIGHEST, preferred_element_type=F32)


def _rms(x):
    return x * lax.rsqrt(jnp.mean(x * x, axis=-1, keepdims=True) + EPS)


def _pick(n, cands):
    for t in cands:
        if n % t == 0:
            return t
    raise ValueError(f"no tile for {n}")


MM_TILE_M = (1024, 512, 256, 128, 64, 32, 16, 8)
MM_TILE_N = (2048, 1792, 1536, 1024, 768, 512, 256, 128)
MM_TILE_K = (2048, 1792, 1536, 1024, 512, 256, 128, 64, 32)

def matmul(a, b, *, mode, name, out_dtypes=(F32,), epi=None, extras=(), after=()):
    if mode == "nn":
        (M, K), (K2, N) = a.shape, b.shape
    elif mode == "nt":
        (M, K), (N, K2) = a.shape, b.shape
    else:
        (K, M), (K2, N) = a.shape, b.shape
    assert K == K2, (a.shape, b.shape, mode)
    tm = _pick(M, MM_TILE_M)
    tn = _pick(N, MM_TILE_N)
    tk = _pick(K, MM_TILE_K)
    nk = K // tk
    n_ex = len(extras)
    n_out = len(out_dtypes)
    dims = {"nn": _NN, "nt": _NT, "tn": _TN}[mode]

    def finish(r, ex_refs, out_refs):
        outs = epi(r, *[e[...] for e in ex_refs]) if epi is not None else (r,)
        for o_ref, o in zip(out_refs, outs):
            o_ref[...] = o.astype(o_ref.dtype)

    def body(*refs):
        a_ref, b_ref = refs[0], refs[1]
        ex_refs = refs[2:2 + n_ex]
        out_refs = refs[2 + n_ex + len(after):2 + n_ex + len(after) + n_out]
        if nk == 1:
            finish(_mxu(a_ref[...], b_ref[...], dims), ex_refs, out_refs)
            return
        acc = refs[2 + n_ex + len(after) + n_out]
        k = pl.program_id(2)

        @pl.when(k == 0)
        def _():
            acc[...] = jnp.zeros_like(acc)

        acc[...] += _mxu(a_ref[...], b_ref[...], dims)

        @pl.when(k == nk - 1)
        def _():
            finish(acc[...], ex_refs, out_refs)

    if mode == "nn":
        a_spec = pl.BlockSpec((tm, tk), lambda i, j, k: (i, k))
        b_spec = pl.BlockSpec((tk, tn), lambda i, j, k: (k, j))
    elif mode == "nt":
        a_spec = pl.BlockSpec((tm, tk), lambda i, j, k: (i, k))
        b_spec = pl.BlockSpec((tn, tk), lambda i, j, k: (j, k))
    else:
        a_spec = pl.BlockSpec((tk, tm), lambda i, j, k: (k, i))
        b_spec = pl.BlockSpec((tk, tn), lambda i, j, k: (k, j))
    o_spec = pl.BlockSpec((tm, tn), lambda i, j, k: (i, j))
    res = pl.pallas_call(
        body,
        name=name,
        grid=(M // tm, N // tn, nk),
        in_specs=[a_spec, b_spec] + [o_spec] * n_ex + [pl.BlockSpec(memory_space=pl.ANY)] * len(after),
        out_specs=[o_spec] * n_out,
        out_shape=[jax.ShapeDtypeStruct((M, N), dt) for dt in out_dtypes],
        scratch_shapes=[pltpu.VMEM((tm, tn), F32)] if nk > 1 else [],
        compiler_params=_cparams(),
    )(a, b, *extras, *after)
    return res[0] if n_out == 1 else tuple(res)


ROW_MM_TILE = 512


def row_matmul(a, b, *, mode, name, extras, outs, epi):
    (M, K) = a.shape
    N = b.shape[1] if mode == "nn" else b.shape[0]
    tm = ROW_MM_TILE
    tk = _pick(K, MM_TILE_K)
    nk = K // tk
    n_ex, n_out = len(extras), len(outs)

    def body(*refs):
        a_ref, b_ref = refs[0], refs[1]
        ex_refs = refs[2:2 + n_ex]
        out_refs = refs[2 + n_ex:2 + n_ex + n_out]
        i = pl.program_id(0)
        dims = _NN if mode == "nn" else _NT
        if nk == 1:
            epi(_mxu(a_ref[...], b_ref[...], dims), i, ex_refs, out_refs)
            return
        acc = refs[2 + n_ex + n_out]
        k = pl.program_id(1)

        @pl.when(k == 0)
        def _():
            acc[...] = jnp.zeros_like(acc)

        acc[...] += _mxu(a_ref[...], b_ref[...], dims)

        @pl.when(k == nk - 1)
        def _():
            epi(acc[...], i, ex_refs, out_refs)

    def lift(index_map):
        return lambda i, k: index_map(i)

    b_spec = pl.BlockSpec((tk, N), lambda i, k: (k, 0)) if mode == "nn" else pl.BlockSpec((N, tk), lambda i, k: (0, k))
    res = pl.pallas_call(
        body, name=name, grid=(M // tm, nk),
        in_specs=[pl.BlockSpec((tm, tk), lambda i, k: (i, k)), b_spec]
        + [pl.BlockSpec(blk, lift(im)) for _, blk, im in extras],
        out_specs=[pl.BlockSpec(blk, lift(im)) for _, _, blk, im in outs],
        out_shape=[jax.ShapeDtypeStruct(shape, dt) for shape, dt, _, _ in outs],
        scratch_shapes=[pltpu.VMEM((tm, N), F32)] if nk > 1 else [],
        compiler_params=_cparams(),
    )(a, b, *[e[0] for e in extras])
    return tuple(res)


def _rows(arr):
    return (arr, (ROW_MM_TILE, arr.shape[1]), lambda i: (i, 0))


def _rows_out(T, dtype):
    return ((T, D_MODEL), dtype, (ROW_MM_TILE, D_MODEL), lambda i: (i, 0))


def _per_seq(arr):
    return (arr, (1, 1, D_MODEL), lambda i: (i // (SEQ // ROW_MM_TILE), 0, 0))


def _per_seq_out(B):
    return ((B, 1, D_MODEL), F32, (1, 1, D_MODEL), lambda i: (i // (SEQ // ROW_MM_TILE), 0, 0))


def _first_tile_of_seq(i):
    return i % (SEQ // ROW_MM_TILE) == 0


def small_call(fn, inputs, out_shapes, name):
    n_in = len(inputs)

    def body(*refs):
        outs = fn(*[r[...] for r in refs[:n_in]])
        for o_ref, o in zip(refs[n_in:], outs):
            o_ref[...] = o.astype(o_ref.dtype)

    res = pl.pallas_call(body, name=name, out_shape=list(out_shapes), compiler_params=_cparams())(*inputs)
    return tuple(res)


ROW_TILE = 512


def _modnorm(x, g, scale, shift):
    return _rms(x) * g * (1.0 + scale) + shift


def _row_spec(width):
    return pl.BlockSpec((1, ROW_TILE, width), lambda b, i: (b, i, 0))


def _mod_spec():
    return pl.BlockSpec((1, 1, D_MODEL), lambda b, i: (b, 0, 0))


def _vec_spec():
    return pl.BlockSpec((1, D_MODEL), lambda b, i: (0, 0))


def modnorm_fwd(x, g, scale, shift, name):
    B, S, D = x.shape

    def body(x_ref, g_ref, sc_ref, sh_ref, u_ref):
        u_ref[0] = _modnorm(x_ref[0], g_ref[...], sc_ref[0], sh_ref[0]).astype(BF16)

    return pl.pallas_call(
        body, name=name, grid=(B, S // ROW_TILE),
        in_specs=[_row_spec(D), _vec_spec(), _mod_spec(), _mod_spec()],
        out_specs=_row_spec(D),
        out_shape=jax.ShapeDtypeStruct((B, S, D), BF16),
        compiler_params=_cparams(),
    )(x, g, scale, shift)


def _gain_spec(g):
    return (g, (1, D_MODEL), lambda i: (0, 0))


def out_proj_resid_modnorm(z, w_out, x, gate, g, scale, shift, name):
    T = z.shape[0]

    def epi(acc, i, ex, out):
        x_ref, gt_ref, g_ref, sc_ref, sh_ref = ex
        y_ref, x1_ref, u_ref = out
        y_ref[...] = acc
        x1 = x_ref[...] + gt_ref[0] * acc
        x1_ref[...] = x1
        u_ref[...] = _modnorm(x1, g_ref[...], sc_ref[0], sh_ref[0]).astype(BF16)

    return row_matmul(z, w_out, mode="nn", name=name,
                      extras=[_rows(x), _per_seq(gate), _gain_spec(g), _per_seq(scale), _per_seq(shift)],
                      outs=[_rows_out(T, F32), _rows_out(T, F32), _rows_out(T, BF16)], epi=epi)


def ff2_loss(hdn, w_ff2, x1, gate, target, name):
    T = hdn.shape[0]

    def epi(acc, i, ex, out):
        x_ref, gt_ref, t_ref = ex
        dx_ref, dffo_ref, loss_ref, dg_ref = out

        @pl.when(i == 0)
        def _():
            loss_ref[...] = jnp.zeros_like(loss_ref)

        @pl.when(_first_tile_of_seq(i))
        def _():
            dg_ref[...] = jnp.zeros_like(dg_ref)

        err = x_ref[...] + gt_ref[0] * acc - t_ref[...]
        dx = err * (1.0 / D_MODEL)
        dx_ref[...] = dx
        dffo_ref[...] = (gt_ref[0] * dx).astype(BF16)
        loss_ref[...] += 0.5 * jnp.sum(jnp.mean(err * err, axis=-1, keepdims=True), axis=0, keepdims=True)
        dg_ref[0] += jnp.sum(dx * acc, axis=0, keepdims=True)

    return row_matmul(hdn, w_ff2, mode="nn", name=name,
                      extras=[_rows(x1), _per_seq(gate), _rows(target)],
                      outs=[_rows_out(T, F32), _rows_out(T, BF16), ((1, 1), F32, (1, 1), lambda i: (0, 0)),
                            _per_seq_out(T // SEQ)], epi=epi)


def d_u_modnorm_bwd(a, w, du_prev, x, g, scale, shift, dx_res, y, gate, name):
    T = a.shape[0]
    B = T // SEQ
    n_prev, resid = int(du_prev is not None), y is not None

    def epi(acc, i, ex, out):
        x_ref, g_ref, sc_ref, sh_ref, dr_ref = ex[n_prev:n_prev + 5]
        dx_ref, dg_ref, dsc_ref, dsh_ref = out[:4]

        @pl.when(i == 0)
        def _():
            dg_ref[...] = jnp.zeros_like(dg_ref)

        @pl.when(_first_tile_of_seq(i))
        def _():
            for r in out[2:4] + out[5:]:
                r[...] = jnp.zeros_like(r)

        du = acc + ex[0][...] if n_prev else acc
        _, vjp = jax.vjp(_modnorm, x_ref[...], g_ref[...], sc_ref[0], sh_ref[0])
        dxn, dg, dsc, dsh = vjp(du)
        dx = dxn + dr_ref[...]
        dx_ref[...] = dx
        dg_ref[...] += dg
        dsc_ref[0] += dsc
        dsh_ref[0] += dsh
        if resid:
            y_ref, gt_ref = ex[n_prev + 5:]
            out[4][...] = (gt_ref[0] * dx).astype(BF16)
            out[5][0] += jnp.sum(dx * y_ref[...], axis=0, keepdims=True)

    extras = ([_rows(du_prev)] if n_prev else []) + [_rows(x), _gain_spec(g), _per_seq(scale), _per_seq(shift),
                                                     _rows(dx_res)] + ([_rows(y), _per_seq(gate)] if resid else [])
    outs = [_rows_out(T, F32), ((1, D_MODEL), F32, (1, D_MODEL), lambda i: (0, 0)), _per_seq_out(B), _per_seq_out(B)]
    outs += [_rows_out(T, BF16), _per_seq_out(B)] if resid else []
    return row_matmul(a, w, mode="nt", name=name, extras=extras, outs=outs, epi=epi)


def _bucket_table(dilation):
    i = np.arange(ATT_BLOCK)[:, None]
    j = np.arange(2 * ATT_BLOCK)[None, :]
    delta = ATT_BLOCK + i - j
    dist = np.maximum(delta, 0) * dilation
    max_exact = N_BUCKETS // 2
    d = np.maximum(dist, max_exact).astype(np.float32)
    large = max_exact + (np.log(d / np.float32(max_exact)) / np.float32(math.log(MAX_DISTANCE / max_exact))
                         * np.float32(N_BUCKETS - max_exact)).astype(np.int32)
    large = np.minimum(large, N_BUCKETS - 1)
    return np.where(dist < max_exact, dist, large).astype(np.int32)


def _bucket_onehot(dilation):
    bt = jnp.asarray(_bucket_table(dilation).reshape(1, -1))
    return (bt == jnp.arange(N_BUCKETS, dtype=jnp.int32)[:, None]).astype(F32)


def bias_expand(rel_t, onehot, name):
    def fn(r, oh):
        return (_doth(r, oh),)
    return small_call(fn, [rel_t, onehot], [jax.ShapeDtypeStruct((rel_t.shape[0], onehot.shape[1]), F32)], name)[0]


def bias_reduce(dbias_flat, onehot, name):
    def fn(db, oh):
        return (_doth(db, oh, _NT),)
    return small_call(fn, [dbias_flat, onehot], [jax.ShapeDtypeStruct((dbias_flat.shape[0], N_BUCKETS), F32)], name)[0]


def _qk_norm(x, g):
    return _rms(x) * g


def _masked_bias(bias):
    i = lax.broadcasted_iota(jnp.int32, (ATT_BLOCK, 2 * ATT_BLOCK), 0)
    j = lax.broadcasted_iota(jnp.int32, (ATT_BLOCK, 2 * ATT_BLOCK), 1)
    bm = jnp.where(jnp.logical_and(j >= i, j <= i + ATT_BLOCK), bias, -jnp.inf)
    return bm, bm[:, ATT_BLOCK:]


def _attn_tile(qn, kn, v, bias):
    s = bdot_nt(qn, kn) * (HEAD_DIM ** -0.5) + bias
    m = lax.stop_gradient(jnp.max(s, axis=-1, keepdims=True))
    p = jnp.exp(s - m)
    l = jnp.sum(p, axis=-1, keepdims=True)
    o = bdot_nn(p, v) / l
    lse = jnp.broadcast_to(m + jnp.log(l), (ATT_BLOCK, HEAD_DIM))
    return o, lse


def _attn_tiles(dilation, rows=SEQ):
    nb = rows // dilation // ATT_BLOCK
    return [(r, n) for r in range(dilation) for n in range(nb)]


def _attn_rows(r, n, dilation, nblk=1):
    if dilation == 1:
        return pl.ds(r + n * ATT_BLOCK, nblk * ATT_BLOCK)
    return pl.ds(r + n * ATT_BLOCK * dilation, nblk * ATT_BLOCK, stride=dilation)


_QL, _KL, _VL = slice(0, 128), slice(128, 256), slice(256, 384)


def _qkv_specs(hb):
    return [pl.BlockSpec((None, SEQ, HEAD_DIM), functools.partial(lambda b, h, j: (b, 0, 3 * (hb + h) + j), j=j))
            for j in range(3)]


def attn_fwd(pa, bias, qg, kg, group, name):
    B = pa.shape[0]
    dilation = ATT_GROUPS[group][1]
    hb = group * 4

    def body(q_ref, k_ref, v_ref, b_ref, qg_ref, kg_ref, o_ref, l_ref, qn_s, kn_s):
        qn_s[...] = _qk_norm(q_ref[...], qg_ref[...])
        kn_s[...] = _qk_norm(k_ref[...], kg_ref[...])
        bias_all, bias_first = _masked_bias(b_ref[0])
        for (r, n) in _attn_tiles(dilation):
            rows = _attn_rows(r, n, dilation)
            if n == 0:
                krows, bias_t = rows, bias_first
            else:
                krows, bias_t = _attn_rows(r, n - 1, dilation, 2), bias_all
            o, lse = _attn_tile(qn_s[rows, :], kn_s[krows, :], v_ref[krows, :], bias_t)
            o_ref[rows, :] = o
            l_ref[rows, :] = lse

    head_out = pl.BlockSpec((None, SEQ, HEAD_DIM), lambda b, h: (b, 0, h))
    return pl.pallas_call(
        body, name=name, grid=(B, 4),
        in_specs=_qkv_specs(hb) + [
                  pl.BlockSpec((1, ATT_BLOCK, 2 * ATT_BLOCK), lambda b, h: (hb + h, 0, 0)),
                  pl.BlockSpec((1, HEAD_DIM), lambda b, h: (0, 0)),
                  pl.BlockSpec((1, HEAD_DIM), lambda b, h: (0, 0))],
        out_specs=[head_out, head_out],
        out_shape=[jax.ShapeDtypeStruct((B, SEQ, 512), F32), jax.ShapeDtypeStruct((B, SEQ, 512), F32)],
        scratch_shapes=[pltpu.VMEM((SEQ, HEAD_DIM), F32)] * 2,
        compiler_params=_cparams(),
    )(pa, pa, pa, bias, qg, kg)


def attn_bwd(pa, bias, qg, kg, do, dlse, dpa, group, name):
    B = pa.shape[0]
    dilation = ATT_GROUPS[group][1]
    hb = group * 4

    def body(q_ref, k_ref, v_ref, b_ref, qg_ref, kg_ref, do_ref, dl_ref, dpa_in,
             dp_ref, db_ref, dqg_ref, dkg_ref, qn_s, kn_s, dq_s, dk_s, dv_s):
        del dpa_in
        h_id = pl.program_id(1)

        @pl.when(jnp.logical_and(pl.program_id(0) == 0, h_id == 0))
        def _():
            db_ref[...] = jnp.zeros_like(db_ref)
            dqg_ref[...] = jnp.zeros_like(dqg_ref)
            dkg_ref[...] = jnp.zeros_like(dkg_ref)

        dk_s[...] = jnp.zeros_like(dk_s)
        dv_s[...] = jnp.zeros_like(dv_s)
        qn_s[...] = _qk_norm(q_ref[...], qg_ref[...])
        kn_s[...] = _qk_norm(k_ref[...], kg_ref[...])
        bias_all, bias_first = _masked_bias(b_ref[0])
        for (r, n) in _attn_tiles(dilation):
            rows = _attn_rows(r, n, dilation)
            if n == 0:
                krows, bias_t = rows, bias_first
            else:
                krows, bias_t = _attn_rows(r, n - 1, dilation, 2), bias_all
            _, vjp = jax.vjp(_attn_tile, qn_s[rows, :], kn_s[krows, :], v_ref[krows, :], bias_t)
            dqn, dkn, dv, dbias = vjp((do_ref[rows, :], dl_ref[rows, :]))
            dq_s[rows, :] = dqn
            dk_s[krows, :] += dkn
            dv_s[krows, :] += dv
            if n == 0:
                db_ref[h_id, :, ATT_BLOCK:] += dbias
            else:
                db_ref[h_id] += dbias
        for x_ref, g_ref, d_s, dg_ref, lanes in ((q_ref, qg_ref, dq_s, dqg_ref, _QL), (k_ref, kg_ref, dk_s, dkg_ref, _KL)):
            _, vjp = jax.vjp(_qk_norm, x_ref[...], g_ref[...])
            dx, dg = vjp(d_s[...])
            dp_ref[0, :, lanes] = dx.astype(BF16)
            dg_ref[...] += dg
        dp_ref[0, :, _VL] = dv_s[...].astype(BF16)

    const2 = lambda b, h: (0, 0)
    head_in = pl.BlockSpec((None, SEQ, HEAD_DIM), lambda b, h: (b, 0, h))
    head_blk = pl.BlockSpec((1, SEQ, ATT_HEAD_COLS), lambda b, h: (b, 0, hb + h))
    return pl.pallas_call(
        body, name=name, grid=(B, 4),
        in_specs=_qkv_specs(hb) + [
                  pl.BlockSpec((1, ATT_BLOCK, 2 * ATT_BLOCK), lambda b, h: (hb + h, 0, 0)),
                  pl.BlockSpec((1, HEAD_DIM), const2), pl.BlockSpec((1, HEAD_DIM), const2),
                  head_in, head_in,
                  pl.BlockSpec(memory_space=pl.ANY)],
        out_specs=[head_blk,
                   pl.BlockSpec((4, ATT_BLOCK, 2 * ATT_BLOCK), lambda b, h: (0, 0, 0)),
                   pl.BlockSpec((1, HEAD_DIM), const2), pl.BlockSpec((1, HEAD_DIM), const2)],
        out_shape=[jax.ShapeDtypeStruct(dpa.shape, BF16),
                   jax.ShapeDtypeStruct((4, ATT_BLOCK, 2 * ATT_BLOCK), F32),
                   jax.ShapeDtypeStruct((1, HEAD_DIM), F32), jax.ShapeDtypeStruct((1, HEAD_DIM), F32)],
        scratch_shapes=[pltpu.VMEM((SEQ, HEAD_DIM), F32)] * 5,
        input_output_aliases={8: 0},
        compiler_params=_cparams(),
    )(pa, pa, pa, bias, qg, kg, do, dlse, dpa)


def _attn_classes(q, k, v, bias, qg, kg):
    s = cdot_nt(_qk_norm(q, qg), _qk_norm(k, kg)) * (HEAD_DIM ** -0.5) + bias
    m = lax.stop_gradient(jnp.max(s, axis=-1, keepdims=True))
    p = jnp.exp(s - m)
    l = jnp.sum(p, axis=-1, keepdims=True)
    o = cdot_nn(p, v) / l
    return o, jnp.broadcast_to(m + jnp.log(l), o.shape)


def _gather_classes(src_ref, dst_s, dilation):
    for r in range(dilation):
        dst_s[r] = src_ref[pl.ds(r, ATT_BLOCK, stride=dilation), :]


def _scatter_classes(src_s, dst_ref, dilation):
    for r in range(dilation):
        dst_ref[pl.ds(r, ATT_BLOCK, stride=dilation), :] = src_s[r]


def attn_fwd_classes(pa, bias, qg, kg, group, name):
    B = pa.shape[0]
    dilation = ATT_GROUPS[group][1]
    hb = group * 4

    def body(q_ref, k_ref, v_ref, b_ref, qg_ref, kg_ref, o_ref, l_ref, q_s, k_s, v_s):
        _gather_classes(q_ref, q_s, dilation)
        _gather_classes(k_ref, k_s, dilation)
        _gather_classes(v_ref, v_s, dilation)
        o, lse = _attn_classes(q_s[...], k_s[...], v_s[...], _masked_bias(b_ref[0])[1], qg_ref[...], kg_ref[...])
        q_s[...], k_s[...] = o, lse
        _scatter_classes(q_s, o_ref, dilation)
        _scatter_classes(k_s, l_ref, dilation)

    head_out = pl.BlockSpec((None, SEQ, HEAD_DIM), lambda b, h: (b, 0, h))
    return pl.pallas_call(
        body, name=name, grid=(B, 4),
        in_specs=_qkv_specs(hb) + [
                  pl.BlockSpec((1, ATT_BLOCK, 2 * ATT_BLOCK), lambda b, h: (hb + h, 0, 0)),
                  pl.BlockSpec((1, HEAD_DIM), lambda b, h: (0, 0)),
                  pl.BlockSpec((1, HEAD_DIM), lambda b, h: (0, 0))],
        out_specs=[head_out, head_out],
        out_shape=[jax.ShapeDtypeStruct((B, SEQ, 512), F32), jax.ShapeDtypeStruct((B, SEQ, 512), F32)],
        scratch_shapes=[pltpu.VMEM((dilation, ATT_BLOCK, HEAD_DIM), F32)] * 3,
        compiler_params=_cparams(),
    )(pa, pa, pa, bias, qg, kg)


def attn_bwd_classes(pa, bias, qg, kg, do, dlse, dpa, group, name):
    B = pa.shape[0]
    dilation = ATT_GROUPS[group][1]
    hb = group * 4

    def body(q_ref, k_ref, v_ref, b_ref, qg_ref, kg_ref, do_ref, dl_ref, dpa_in,
             dp_ref, db_ref, dqg_ref, dkg_ref, q_s, k_s, v_s, do_s, dl_s, rows_s):
        del dpa_in
        h_id = pl.program_id(1)

        @pl.when(jnp.logical_and(pl.program_id(0) == 0, h_id == 0))
        def _():
            db_ref[...] = jnp.zeros_like(db_ref)
            dqg_ref[...] = jnp.zeros_like(dqg_ref)
            dkg_ref[...] = jnp.zeros_like(dkg_ref)

        for src, dst in ((q_ref, q_s), (k_ref, k_s), (v_ref, v_s), (do_ref, do_s), (dl_ref, dl_s)):
            _gather_classes(src, dst, dilation)
        _, vjp = jax.vjp(_attn_classes, q_s[...], k_s[...], v_s[...], _masked_bias(b_ref[0])[1],
                         qg_ref[...], kg_ref[...])
        dq, dk, dv, dbias, dqg, dkg = vjp((do_s[...], dl_s[...]))
        db_ref[h_id, :, ATT_BLOCK:] += dbias
        dqg_ref[...] += dqg
        dkg_ref[...] += dkg
        for d, lanes in ((dq, _QL), (dk, _KL), (dv, _VL)):
            q_s[...] = d
            _scatter_classes(q_s, rows_s, dilation)
            dp_ref[0, :, lanes] = rows_s[...].astype(BF16)

    const2 = lambda b, h: (0, 0)
    head_in = pl.BlockSpec((None, SEQ, HEAD_DIM), lambda b, h: (b, 0, h))
    head_blk = pl.BlockSpec((1, SEQ, ATT_HEAD_COLS), lambda b, h: (b, 0, hb + h))
    return pl.pallas_call(
        body, name=name, grid=(B, 4),
        in_specs=_qkv_specs(hb) + [
                  pl.BlockSpec((1, ATT_BLOCK, 2 * ATT_BLOCK), lambda b, h: (hb + h, 0, 0)),
                  pl.BlockSpec((1, HEAD_DIM), const2), pl.BlockSpec((1, HEAD_DIM), const2),
                  head_in, head_in,
                  pl.BlockSpec(memory_space=pl.ANY)],
        out_specs=[head_blk,
                   pl.BlockSpec((4, ATT_BLOCK, 2 * ATT_BLOCK), lambda b, h: (0, 0, 0)),
                   pl.BlockSpec((1, HEAD_DIM), const2), pl.BlockSpec((1, HEAD_DIM), const2)],
        out_shape=[jax.ShapeDtypeStruct(dpa.shape, BF16),
                   jax.ShapeDtypeStruct((4, ATT_BLOCK, 2 * ATT_BLOCK), F32),
                   jax.ShapeDtypeStruct((1, HEAD_DIM), F32), jax.ShapeDtypeStruct((1, HEAD_DIM), F32)],
        scratch_shapes=[pltpu.VMEM((dilation, ATT_BLOCK, HEAD_DIM), F32)] * 5 + [pltpu.VMEM((SEQ, HEAD_DIM), F32)],
        input_output_aliases={8: 0},
        compiler_params=_cparams(),
    )(pa, pa, pa, bias, qg, kg, do, dlse, dpa)


def _merge(o0, o1, o2, l0, l1, l2):
    mx = jnp.maximum(jnp.maximum(l0, l1), l2)
    e0, e1, e2 = jnp.exp(l0 - mx), jnp.exp(l1 - mx), jnp.exp(l2 - mx)
    den = e0 + e1 + e2
    return (e0 / den) * o0 + (e1 / den) * o1 + (e2 / den) * o2


def merge_att_out(os_, ls_, w_att_out, name):
    T = os_[0].shape[0]
    tile = pl.BlockSpec((ROW_MM_TILE, 512), lambda i: (i, 0))

    def body(o0, o1, o2, l0, l1, l2, w_ref, a_ref, y_ref):
        att = _merge(o0[...], o1[...], o2[...], l0[...], l1[...], l2[...]).astype(BF16)
        a_ref[...] = att
        y_ref[...] = _mxu(att, w_ref[...], _NN)

    return pl.pallas_call(
        body, name=name, grid=(T // ROW_MM_TILE,),
        in_specs=[tile] * 6 + [pl.BlockSpec((512, D_MODEL), lambda i: (0, 0))],
        out_specs=[tile, pl.BlockSpec((ROW_MM_TILE, D_MODEL), lambda i: (i, 0))],
        out_shape=[jax.ShapeDtypeStruct((T, 512), BF16), jax.ShapeDtypeStruct((T, D_MODEL), F32)],
        compiler_params=_cparams(),
    )(*os_, *ls_, w_att_out)


def d_att_merge_bwd(d_ya, w_att_out, os_, ls_, name):
    T = d_ya.shape[0]
    tile = lambda a: (a, (ROW_MM_TILE, 512), lambda i: (i, 0))

    def epi(acc, i, ex, out):
        _, vjp = jax.vjp(_merge, *[e[...] for e in ex])
        for o_ref, g in zip(out, vjp(acc)):
            o_ref[...] = g

    return row_matmul(d_ya, w_att_out, mode="nt", name=name, extras=[tile(a) for a in list(os_) + list(ls_)],
                      outs=[((T, 512), F32, (ROW_MM_TILE, 512), lambda i: (i, 0))] * 6, epi=epi)


def _gate_mix(ga, gm, ya, ym):
    return jax.nn.sigmoid(ga) * ya + jax.nn.sigmoid(gm) * ym


def _gate_halves(pg):
    return [(pg, (ROW_MM_TILE, D_MODEL), lambda i: (i, 0)), (pg, (ROW_MM_TILE, D_MODEL), lambda i: (i, 1))]


def ml_out_gate(hg, w_ml_out, pg, ya, name):
    T = hg.shape[0]

    def epi(acc, i, ex, out):
        ga, gm, ya_ref = ex
        out[0][...] = acc
        out[1][...] = _gate_mix(ga[...], gm[...], ya_ref[...], acc).astype(BF16)

    return row_matmul(hg, w_ml_out, mode="nn", name=name, extras=_gate_halves(pg) + [_rows(ya)],
                      outs=[_rows_out(T, F32), _rows_out(T, BF16)], epi=epi)


def d_z_gate_bwd(dy, w_out, pg, ya, ym, name):
    T = dy.shape[0]

    def epi(acc, i, ex, out):
        ga, gm, ya_ref, ym_ref = ex
        dpg_ref, dya_ref, dym_ref = out
        _, vjp = jax.vjp(_gate_mix, ga[...], gm[...], ya_ref[...], ym_ref[...])
        dga, dgm, dya, dym = vjp(acc)
        dpg_ref[:, :D_MODEL] = dga.astype(BF16)
        dpg_ref[:, D_MODEL:] = dgm.astype(BF16)
        dya_ref[...] = dya.astype(BF16)
        dym_ref[...] = dym.astype(BF16)

    return row_matmul(dy, w_out, mode="nt", name=name, extras=_gate_halves(pg) + [_rows(ya), _rows(ym)],
                      outs=[((T, GATE_COLS), BF16, (ROW_MM_TILE, GATE_COLS), lambda i: (i, 0)),
                            _rows_out(T, BF16), _rows_out(T, BF16)], epi=epi)


def _log_sigmoid(x):
    return jnp.minimum(x, 0.0) - jnp.log(1.0 + jnp.exp(-jnp.abs(x)))


def _head_mask(e):
    lane = lax.broadcasted_iota(jnp.int32, (1, 128), 1)
    return jnp.logical_and(lane >= e * 64, lane < (e + 1) * 64).astype(F32)


def _bmxu(a, b, ca, cb):
    return lax.dot_general(a.astype(BF16), b.astype(BF16), (((ca,), (cb,)), ((0,), (0,))), preferred_element_type=F32)


@jax.custom_vjp
def cdot_nt(a, b):
    return _bmxu(a, b, 2, 2)


cdot_nt.defvjp(lambda a, b: (_bmxu(a, b, 2, 2), (a, b)),
               lambda res, g: (_bmxu(g, res[1], 2, 1), _bmxu(g, res[0], 1, 1)))


@jax.custom_vjp
def cdot_nn(a, b):
    return _bmxu(a, b, 2, 1)


cdot_nn.defvjp(lambda a, b: (_bmxu(a, b, 2, 1), (a, b)),
               lambda res, g: (_bmxu(g, res[1], 2, 2), _bmxu(res[0], g, 1, 1)))


@jax.custom_vjp
def cdot_tn(a, b):
    return _bmxu(a, b, 1, 1)


cdot_tn.defvjp(lambda a, b: (_bmxu(a, b, 1, 1), (a, b)),
               lambda res, g: (_bmxu(res[1], g, 2, 2), _bmxu(res[0], g, 2, 1)))


def _top_bits(x):
    return lax.bitcast_convert_type(lax.bitcast_convert_type(x, jnp.uint32) & jnp.uint32(0xFFFF0000), F32)


def _split3(x):
    hi = _top_bits(x)
    r = x - hi
    mid = _top_bits(r)
    return hi, mid, r - mid


def _parts_in_lanes(col):
    hi, mid, lo = _split3(col)
    lane = lax.broadcasted_iota(jnp.int32, (1, 1, 8), 2)
    return jnp.where(lane == 0, hi, jnp.where(lane == 1, mid, jnp.where(lane == 2, lo, 0.0)))


def _parts_in_rows(row):
    hi, mid, lo = _split3(row)
    sub = lax.broadcasted_iota(jnp.int32, (1, 8, 1), 1)
    return jnp.where(sub == 0, hi, jnp.where(sub == 1, mid, jnp.where(sub == 2, lo, 0.0)))


def _chunk_matrix(kind, c):
    ri = lax.broadcasted_iota(jnp.int32, (c, ML_CHUNK, ML_CHUNK), 1)
    ci = lax.broadcasted_iota(jnp.int32, (c, ML_CHUNK, ML_CHUNK), 2)
    return {"eye": ri == ci, "lower": ri >= ci, "upper": ri <= ci}[kind].astype(F32)


def _col_col(kind, col):
    out = _bmxu(_chunk_matrix(kind, col.shape[0]), _parts_in_lanes(col), 2, 1)
    return jnp.sum(out, axis=-1, keepdims=True)


def _col_row(col):
    out = _bmxu(_parts_in_lanes(col), _chunk_matrix("eye", col.shape[0]), 1, 1)
    return jnp.sum(out, axis=1, keepdims=True)


def _row_col(row):
    out = _bmxu(_chunk_matrix("eye", row.shape[0]), _parts_in_rows(row), 2, 2)
    return jnp.sum(out, axis=-1, keepdims=True)


@jax.custom_vjp
def chunk_cumsum(col):
    return _col_col("lower", col)


chunk_cumsum.defvjp(lambda col: (_col_col("lower", col), None), lambda _, g: (_col_col("upper", g),))


@jax.custom_vjp
def col_to_row(col):
    return _col_row(col)


col_to_row.defvjp(lambda col: (_col_row(col), None), lambda _, g: (_row_col(g),))


def _gate_block(ifb):
    lane = lax.broadcasted_iota(jnp.int32, (1, 128), 1)
    return jnp.where(lane >= 2, _log_sigmoid(ifb), ifb)


def _ml_intra(q2, k2, v, ifb, *, e):
    c, L = q2.shape[0] // ML_CHUNK, ML_CHUNK
    hm = _head_mask(e)
    q3 = (q2 * hm).reshape(c, L, 128)
    k3 = (k2 * hm).reshape(c, L, 128)
    v3 = v.reshape(c, L, 128)
    if3 = ifb.reshape(c, L, 128)
    lanes = lax.broadcasted_iota(jnp.int32, (c, L, 128), 2)
    li = jnp.sum(jnp.where(lanes == e, if3, 0.0), axis=-1, keepdims=True)
    lf = jnp.sum(jnp.where(lanes == 2 + e, if3, 0.0), axis=-1, keepdims=True)
    b = chunk_cumsum(lf)
    last = lax.broadcasted_iota(jnp.int32, (1, L, 1), 1) == L - 1
    b_end = jnp.sum(jnp.where(last, b, 0.0), axis=1, keepdims=True)
    causal = lax.broadcasted_iota(jnp.int32, (L, L), 0) >= lax.broadcasted_iota(jnp.int32, (L, L), 1)
    Dm = jnp.where(causal, b + col_to_row(li - b), -jnp.inf)
    mD = lax.stop_gradient(jnp.max(Dm, axis=-1, keepdims=True))
    P0 = cdot_nt(q3, k3) * jnp.exp(Dm - mD)
    H0 = cdot_nn(P0, v3)
    r0 = jnp.sum(P0, axis=-1, keepdims=True)
    g = b_end - b + li
    mg = lax.stop_gradient(jnp.max(g, axis=1, keepdims=True))
    kw = jnp.exp(g - mg) * k3
    return H0, r0, cdot_tn(kw, v3), jnp.sum(kw, axis=1, keepdims=True), b, b_end, mD, mg


def _ml_inter(q2, mo, gn, H0, r0, b, C_in, n_in, *, mD, m_in, e):
    c, L = q2.shape[0] // ML_CHUNK, ML_CHUNK
    q3 = (q2 * _head_mask(e)).reshape(c, L, 128)
    a = b + m_in
    m_t = lax.stop_gradient(jnp.maximum(a, mD))
    c1 = jnp.exp(mD - m_t)
    c2 = jnp.exp(a - m_t)
    num = c1 * H0 + c2 * cdot_nn(q3, C_in)
    nq = c1 * r0 + c2 * jnp.sum(q3 * n_in, axis=-1, keepdims=True)
    h = num / jnp.maximum(jnp.abs(nq), jnp.exp(-m_t))
    hg = _rms(h) * gn * jax.nn.sigmoid(mo.reshape(c, L, 128))
    return hg.reshape(c * L, 128)


def _state_sweep(U_s, un_s, be_s, mg_s, Cin_s, nin_s, min_s, al_s, bt_s):
    def step(j, carry):
        C, n, m = carry
        Cin_s[j], nin_s[j], min_s[j] = C, n, m
        m_out = jnp.maximum(be_s[j] + m, mg_s[j])
        al = jnp.exp(be_s[j] + m - m_out)
        bt = jnp.exp(mg_s[j] - m_out)
        al_s[j], bt_s[j] = al, bt
        return al * C + bt * U_s[j], al * n + bt * un_s[j], m_out

    lax.fori_loop(0, N_CHUNKS, step, (jnp.zeros((128, 128), F32), jnp.zeros((1, 128), F32), jnp.zeros((1, 1), F32)))


def _state_sweep_bwd(U_s, un_s, dbe_s, Cin_s, nin_s, dCp_s, dnp_s, al_s, bt_s):
    def step(t, carry):
        j = N_CHUNKS - 1 - t
        dC, dn = carry
        al, bt = al_s[j], bt_s[j]
        U_s[j] = bt * dC
        un_s[j] = bt * dn
        dal = jnp.sum(jnp.sum(dC * Cin_s[j], axis=1, keepdims=True), axis=0, keepdims=True) \
            + jnp.sum(dn * nin_s[j], axis=1, keepdims=True)
        dbe_s[j] = dal * al
        return dCp_s[j] + al * dC, dnp_s[j] + al * dn

    lax.fori_loop(0, N_CHUNKS, step, (jnp.zeros((128, 128), F32), jnp.zeros((1, 128), F32)))


def _state_scratch():
    c = N_CHUNKS
    return [pltpu.VMEM((c, 128, 128), F32), pltpu.VMEM((c, 1, 128), F32), pltpu.VMEM((c, 1, 1), F32),
            pltpu.VMEM((c, 1, 1), F32),
            pltpu.VMEM((c, 128, 128), F32), pltpu.VMEM((c, 1, 128), F32), pltpu.VMEM((c, 1, 1), F32),
            pltpu.VMEM((c, 1, 1), F32), pltpu.VMEM((c, 1, 1), F32)]


def _shift_down(x, s):
    if s == 0:
        return x
    rows = lax.broadcasted_iota(jnp.int32, x.shape, 0)
    return jnp.where(rows >= s, pltpu.roll(x, s, 0), 0.0)


def _shift_up(x, s):
    if s == 0:
        return x
    S = x.shape[0]
    rows = lax.broadcasted_iota(jnp.int32, x.shape, 0)
    return jnp.where(rows < S - s, pltpu.roll(x, S - s, 0), 0.0)


def _conv_pre(x, cw, cb):
    y = cb + cw[3:4, :] * x
    for j in range(3):
        y = y + cw[j:j + 1, :] * _shift_down(x, 3 - j)
    return y


def _conv_bwd(x, cw, dpre):
    dx = cw[3:4, :] * dpre
    dcw = [None] * 4
    dcw[3] = jnp.sum(dpre * x, axis=0, keepdims=True)
    for j in range(3):
        dx = dx + cw[j:j + 1, :] * _shift_up(dpre, 3 - j)
        dcw[j] = jnp.sum(dpre * _shift_down(x, 3 - j), axis=0, keepdims=True)
    return dx, dcw, jnp.sum(dpre, axis=0, keepdims=True)


def _silu(z):
    return z * jax.nn.sigmoid(z)


def _dsilu(z):
    s = jax.nn.sigmoid(z)
    return s * (1.0 + z * (1.0 - s))


_ML_Q, _ML_K = slice(0, 128), slice(128, 256)
_ML_IF = slice(768, 896)


def _ml_v(e):
    return slice(256 + e * 128, 384 + e * 128)


def _ml_o(e):
    return slice(512 + e * 128, 640 + e * 128)


def _ml_specs():
    pair = lambda b, p: (b, 0, p)
    return [pl.BlockSpec((1, SEQ, ML_PAIR_COLS), pair),
            pl.BlockSpec((1, 4, 128), lambda b, p: (p, 0, 0)),
            pl.BlockSpec((1, 4, 128), lambda b, p: (4 + p, 0, 0)),
            pl.BlockSpec((1, 1, 128), lambda b, p: (p, 0, 0)),
            pl.BlockSpec((1, 1, 128), lambda b, p: (4 + p, 0, 0)),
            pl.BlockSpec((1, 1, 128), lambda b, p: (p, 0, 0)),
            pl.BlockSpec((1, 1, 256), lambda b, p: (p, 0, 0))]


def mlstm_fwd(pm, cw8, cb8, bifp, gn4, name):
    B = pm.shape[0]

    def body(p_ref, cwq, cwk, cbq, cbk, bif_ref, gn_ref, hg_ref, *st):
        U_s, un_s, be_s, mg_s, Cin_s, nin_s, min_s, al_s, bt_s = st
        qc = _silu(_conv_pre(p_ref[0, :, _ML_Q], cwq[0], cbq[0]))
        kc = _silu(_conv_pre(p_ref[0, :, _ML_K], cwk[0], cbk[0])) * (64 ** -0.5)
        ifb = _gate_block(p_ref[0, :, _ML_IF] + bif_ref[0])
        for e in range(2):
            lanes = slice(e * 128, (e + 1) * 128)
            H0, r0, U, un, b, b_end, mD, mg = _ml_intra(qc, kc, p_ref[0, :, _ml_v(e)], ifb, e=e)
            U_s[...], un_s[...], be_s[...], mg_s[...] = U, un, b_end, mg
            _state_sweep(*st)
            hg = _ml_inter(qc, p_ref[0, :, _ml_o(e)], gn_ref[0, :, lanes], H0, r0, b, Cin_s[...], nin_s[...],
                           mD=mD, m_in=min_s[...], e=e)
            hg_ref[0, :, lanes] = hg.astype(BF16)

    return pl.pallas_call(
        body, name=name, grid=(B, ML_PAIRS),
        in_specs=_ml_specs(),
        out_specs=pl.BlockSpec((1, SEQ, 256), lambda b, p: (b, 0, p)),
        out_shape=jax.ShapeDtypeStruct((B, SEQ, D_MODEL), BF16),
        scratch_shapes=_state_scratch(),
        compiler_params=_cparams(),
    )(pm, cw8, cw8, cb8, cb8, bifp, gn4)


def mlstm_bwd(pm, cw8, cb8, bifp, gn4, dhg, name):
    B = pm.shape[0]

    def body(p_ref, cwq, cwk, cbq, cbk, bif_ref, gn_ref, dh_ref,
             dp_ref, dcw_ref, dcb_ref, dbif_ref, dgn_ref, *scr):
        st = scr[:9]
        U_s, un_s, be_s, mg_s, Cin_s, nin_s, min_s, al_s, bt_s = st
        dCp_s, dnp_s, dbe_s = scr[9:]
        p_id = pl.program_id(1)

        @pl.when(jnp.logical_and(pl.program_id(0) == 0, p_id == 0))
        def _():
            dcw_ref[...] = jnp.zeros_like(dcw_ref)
            dcb_ref[...] = jnp.zeros_like(dcb_ref)
            dbif_ref[...] = jnp.zeros_like(dbif_ref)
            dgn_ref[...] = jnp.zeros_like(dgn_ref)

        pre_q = _conv_pre(p_ref[0, :, _ML_Q], cwq[0], cbq[0])
        pre_k = _conv_pre(p_ref[0, :, _ML_K], cwk[0], cbk[0])
        qc = _silu(pre_q)
        kc = _silu(pre_k) * (64 ** -0.5)
        ifb, gate_vjp = jax.vjp(_gate_block, p_ref[0, :, _ML_IF] + bif_ref[0])
        dq = jnp.zeros((SEQ, 128), F32)
        dk = jnp.zeros((SEQ, 128), F32)
        difb = jnp.zeros((SEQ, 128), F32)
        for e in range(2):
            lanes = slice(e * 128, (e + 1) * 128)
            (H0, r0, U, un, b, b_end, mD, mg), vjp1 = jax.vjp(functools.partial(_ml_intra, e=e), qc, kc,
                                                              p_ref[0, :, _ml_v(e)], ifb)
            U_s[...], un_s[...], be_s[...], mg_s[...] = U, un, b_end, mg
            _state_sweep(*st)
            _, vjp3 = jax.vjp(functools.partial(_ml_inter, mD=mD, m_in=min_s[...], e=e), qc, p_ref[0, :, _ml_o(e)],
                              gn_ref[0, :, lanes], H0, r0, b, Cin_s[...], nin_s[...])
            dq_a, dmo, dgn, dH0, dr0, db_a, dCp, dnp = vjp3(dh_ref[0, :, lanes])
            dCp_s[...], dnp_s[...] = dCp, dnp
            _state_sweep_bwd(U_s, un_s, dbe_s, Cin_s, nin_s, dCp_s, dnp_s, al_s, bt_s)
            dq_b, dk_b, dv, difb_e = vjp1((dH0, dr0, U_s[...], un_s[...], db_a, dbe_s[...],
                                           jnp.zeros_like(mD), jnp.zeros_like(mg)))
            dq, dk, difb = dq + dq_a + dq_b, dk + dk_b, difb + difb_e
            dp_ref[0, :, _ml_v(e)] = dv.astype(BF16)
            dp_ref[0, :, _ml_o(e)] = dmo.astype(BF16)
            dgn_ref[p_id, :, lanes] += dgn
        (difb,) = gate_vjp(difb)
        dp_ref[0, :, _ML_IF] = difb.astype(BF16)
        dbif_ref[p_id] += jnp.sum(difb, axis=0, keepdims=True)

        for (sl, cw, pre, d, blk, scale) in ((_ML_Q, cwq, pre_q, dq, p_id, 1.0), (_ML_K, cwk, pre_k, dk, 4 + p_id, 64 ** -0.5)):
            xr = p_ref[0, :, sl]
            dpre = d * scale * _dsilu(pre)
            dx, dcw, dcb = _conv_bwd(xr, cw[0], dpre)
            dp_ref[0, :, sl] = dx.astype(BF16)
            for j in range(4):
                dcw_ref[blk, j:j + 1, :] += dcw[j]
            dcb_ref[blk] += dcb

    full3 = lambda b, p: (0, 0, 0)
    return pl.pallas_call(
        body, name=name, grid=(B, ML_PAIRS),
        in_specs=[pl.BlockSpec((1, SEQ, ML_PAIR_COLS), lambda b, p: (b, 0, p), pipeline_mode=pl.Buffered(1))]
        + _ml_specs()[1:] + [pl.BlockSpec((1, SEQ, 256), lambda b, p: (b, 0, p), pipeline_mode=pl.Buffered(1))],
        out_specs=[pl.BlockSpec((1, SEQ, ML_PAIR_COLS), lambda b, p: (b, 0, p)),
                   pl.BlockSpec((8, 4, 128), full3), pl.BlockSpec((8, 1, 128), full3),
                   pl.BlockSpec((4, 1, 128), full3), pl.BlockSpec((4, 1, 256), full3)],
        out_shape=[jax.ShapeDtypeStruct((B, SEQ, ML_COLS), BF16),
                   jax.ShapeDtypeStruct((8, 4, 128), F32), jax.ShapeDtypeStruct((8, 1, 128), F32),
                   jax.ShapeDtypeStruct((4, 1, 128), F32), jax.ShapeDtypeStruct((4, 1, 256), F32)],
        scratch_shapes=_state_scratch() + [pltpu.VMEM((N_CHUNKS, 128, 128), F32), pltpu.VMEM((N_CHUNKS, 1, 128), F32),
                                           pltpu.VMEM((N_CHUNKS, 1, 1), F32)],
        compiler_params=_cparams(),
    )(pm, cw8, cw8, cb8, cb8, bifp, gn4, dhg)


def _adamw(w, g, m, v):
    m = ADAM_B1 * m + (1.0 - ADAM_B1) * g
    v = ADAM_B2 * v + (1.0 - ADAM_B2) * (g * g)
    m_hat = m / (1.0 - ADAM_B1 ** ADAM_STEP)
    v_hat = v / (1.0 - ADAM_B2 ** ADAM_STEP)
    delta = -ADAM_LR * (m_hat / (jnp.sqrt(v_hat) + ADAM_EPS) + ADAM_WD * w)
    return delta, m, v


def adamw(w, g, m, v, name, parts=False):
    R, C = w.shape
    if R % 8 == 0 or R * C * 4 <= (1 << 20):
        tr = _pick(R, (256, 128, 64, 32, 16, 8)) if R * C * 4 > (1 << 20) else R
        steps = R // tr
        spec = pl.BlockSpec((tr, C), lambda i: (i, 0))
        g_spec = pl.BlockSpec((N_DEV, tr, C), lambda i: (0, i, 0)) if parts else spec
    else:
        tc = _pick(C, (256, 128))
        steps = C // tc
        spec = pl.BlockSpec((R, tc), lambda i: (0, i))
        g_spec = pl.BlockSpec((N_DEV, R, tc), lambda i: (0, 0, i)) if parts else spec

    def body(w_ref, g_ref, m_ref, v_ref, go_ref, d_ref, mo_ref, vo_ref):
        if parts:
            g = g_ref[0].astype(F32)
            for k in range(1, N_DEV):
                g = g + g_ref[k].astype(F32)
        else:
            g = g_ref[...]
        d, mn, vn = _adamw(w_ref[...], g, m_ref[...], v_ref[...])
        go_ref[...], d_ref[...], mo_ref[...], vo_ref[...] = g, d, mn, vn

    return pl.pallas_call(
        body, name=name, grid=(steps,),
        in_specs=[spec, g_spec, spec, spec], out_specs=[spec] * 4,
        out_shape=[jax.ShapeDtypeStruct((R, C), F32)] * 4,
        compiler_params=_cparams(),
    )(w, g, m, v)


def adamw_many(ws, gs, ms, vs, name):
    n = len(ws)

    def fn(*a):
        out = []
        for j in range(n):
            out += list(_adamw(a[j], a[n + j], a[2 * n + j], a[3 * n + j]))
        return tuple(out)

    shapes = [jax.ShapeDtypeStruct(w.shape, F32) for w in ws for _ in range(3)]
    return small_call(fn, list(ws) + list(gs) + list(ms) + list(vs), shapes, name)


def _mesh_pos():
    return lax.axis_index("x"), lax.axis_index("y"), lax.axis_index("c")


def _flip(pos, f):
    x, y, c = pos
    return (1 - x if f & 4 else x, 1 - y if f & 2 else y, 1 - c if f & 1 else c)


def _index(pos):
    return 4 * pos[0] + 2 * pos[1] + pos[2]


def _exchange(arrs, name, scatter):
    n = len(arrs)
    scat = list(scatter) if isinstance(scatter, (list, tuple)) else [scatter] * n

    def body(*refs):
        ins, outs = refs[:n], refs[n:2 * n]
        send, recv, lsem = refs[2 * n:]
        me = _mesh_pos()
        mine = _index(me)
        copies = []
        for i in range(n):
            src = ins[i].at[mine] if scat[i] else ins[i]
            loc = pltpu.make_async_copy(src, outs[i].at[mine], lsem.at[i])
            loc.start()
            copies.append(loc)
            for f in range(1, N_DEV):
                peer = _flip(me, f)
                src = ins[i].at[_index(peer)] if scat[i] else ins[i]
                cp = pltpu.make_async_remote_copy(
                    src_ref=src, dst_ref=outs[i].at[mine],
                    send_sem=send.at[i * 7 + f - 1], recv_sem=recv.at[i * 7 + f - 1],
                    device_id=peer, device_id_type=pl.DeviceIdType.MESH)
                cp.start()
                copies.append(cp)
        for cp in copies:
            cp.wait()

    any_spec = pl.BlockSpec(memory_space=pl.ANY)
    out_shape = [jax.ShapeDtypeStruct(a.shape if s else (N_DEV,) + a.shape, a.dtype) for a, s in zip(arrs, scat)]
    res = pl.pallas_call(
        body, name=name,
        in_specs=[any_spec] * n, out_specs=[any_spec] * n, out_shape=out_shape,
        scratch_shapes=[pltpu.SemaphoreType.DMA((7 * n,)), pltpu.SemaphoreType.DMA((7 * n,)),
                        pltpu.SemaphoreType.DMA((n,))],
        compiler_params=_cparams(),
    )(*arrs)
    return list(res)


def all_gather(arrs, name):
    return _exchange(arrs, name, False)


def all_gather_two_level(arrs, name):
    n = len(arrs)

    def body(*refs):
        ins, outs = refs[:n], refs[n:2 * n]
        send, recv, lsem = refs[2 * n:]
        x, y, c = _mesh_pos()
        me, sibling = (x, y, c), (x, y, 1 - c)
        chips = [(1 - x, y), (x, 1 - y), (1 - x, 1 - y)]

        def copy(i, k, block, to, src=None):
            rows = outs[i].at[_index(block)]
            return pltpu.make_async_remote_copy(
                src_ref=rows if src is None else src, dst_ref=rows,
                send_sem=send.at[i * 7 + k], recv_sem=recv.at[i * 7 + k],
                device_id=to, device_id_type=pl.DeviceIdType.MESH)

        local = [pltpu.make_async_copy(ins[i], outs[i].at[_index(me)], lsem.at[i]) for i in range(n)]
        first = [copy(i, 0, me, sibling, src=ins[i]) for i in range(n)]
        first += [copy(i, 1 + j, me, (*chip, c), src=ins[i]) for i in range(n) for j, chip in enumerate(chips)]
        for cp in local + first:
            cp.start()
        passed = []
        for j, chip in enumerate(chips):
            for i in range(n):
                copy(i, 1 + j, (*chip, c), me).wait_recv()
                cp = copy(i, 4 + j, (*chip, c), sibling)
                cp.start()
                passed.append(cp)
        for i in range(n):
            copy(i, 0, sibling, me).wait_recv()
            for j, chip in enumerate(chips):
                copy(i, 4 + j, (*chip, 1 - c), me).wait_recv()
        for cp in first + passed:
            cp.wait_send()
        for cp in local:
            cp.wait()

    any_spec = pl.BlockSpec(memory_space=pl.ANY)
    res = pl.pallas_call(
        body, name=name,
        in_specs=[any_spec] * n, out_specs=[any_spec] * n,
        out_shape=[jax.ShapeDtypeStruct((N_DEV,) + a.shape, a.dtype) for a in arrs],
        scratch_shapes=[pltpu.SemaphoreType.DMA((7 * n,)), pltpu.SemaphoreType.DMA((7 * n,)),
                        pltpu.SemaphoreType.DMA((n,))],
        compiler_params=_cparams(),
    )(*arrs)
    return list(res)


def all_to_all(arrs, name):
    return _exchange(arrs, name, True)


_HBM = pl.BlockSpec(memory_space=pltpu.HBM)
_SEM = pl.BlockSpec(memory_space=pltpu.SEMAPHORE)
_EFFECT = pltpu.SideEffectType.DATAFLOW_SIDE_EFFECTING


def _split_copies(ins, lands, send, recv, scatter, waiting):
    me = _mesh_pos()
    mine = _index(me)
    copies = []
    for i in range(len(ins)):
        for f in range(1, N_DEV):
            peer = _flip(me, f)
            src = ins[i].at[_index(peer)] if scatter else ins[i]
            copies.append(pltpu.make_async_remote_copy(
                src_ref=src, dst_ref=lands[i].at[_index(peer) if waiting else mine],
                send_sem=send.at[i * 7 + f - 1], recv_sem=recv.at[i * 7 + f - 1],
                device_id=peer, device_id_type=pl.DeviceIdType.MESH))
    return copies


def exchange_start(arrs, name, scatter, after=()):
    n = len(arrs)
    land_shapes = [a.shape if scatter else (N_DEV,) + a.shape for a in arrs]

    def body(*refs):
        ins, lands = refs[:n], refs[n:2 * n]
        send, recv = refs[2 * n + len(after)], refs[2 * n + len(after) + 1]
        token = refs[-1]
        for cp in _split_copies(ins, lands, send, recv, scatter, False):
            cp.start()
        token[...] = jnp.zeros_like(token)

    res = pl.pallas_call(
        body, name=name,
        out_shape=(pltpu.SemaphoreType.DMA((7 * n,)), pltpu.SemaphoreType.DMA((7 * n,)),
                   *[pltpu.HBM(a.shape, a.dtype) for a in arrs],
                   *[pltpu.HBM(s, a.dtype) for s, a in zip(land_shapes, arrs)],
                   jax.ShapeDtypeStruct((8, 128), F32)),
        in_specs=[_HBM] * (2 * n) + [pl.BlockSpec(memory_space=pl.ANY)] * len(after),
        out_specs=(_SEM, _SEM, *[_HBM] * (2 * n), pl.BlockSpec(memory_space=pltpu.VMEM)),
        input_output_aliases={i: 2 + i for i in range(2 * n)},
        compiler_params=pltpu.CompilerParams(has_side_effects=_EFFECT),
    )(*[pltpu.with_memory_space_constraint(a, pltpu.HBM) for a in arrs],
      *[pltpu.with_memory_space_constraint(lax.empty(s, a.dtype), pltpu.HBM) for s, a in zip(land_shapes, arrs)],
      *after)
    return (res[0], res[1], list(res[2:2 + n]), list(res[2 + n:2 + 2 * n])), res[-1]


def exchange_wait(handle, after, name, scatter):
    send, recv, srcs, lands = handle
    n = len(srcs)
    after = list(after) if isinstance(after, (list, tuple)) else [after]

    def body(*refs):
        ins, lnd = refs[:n], refs[n:2 * n]
        send_, recv_ = refs[2 * n], refs[2 * n + 1]
        for cp in _split_copies(ins, lnd, send_, recv_, scatter, True):
            cp.wait_send()
            cp.wait_recv()

    res = pl.pallas_call(
        body, name=name,
        out_shape=(*[pltpu.HBM(a.shape, a.dtype) for a in srcs], *[pltpu.HBM(a.shape, a.dtype) for a in lands]),
        in_specs=[_HBM] * (2 * n) + [_SEM, _SEM] + [pl.BlockSpec(memory_space=pl.ANY)] * len(after),
        out_specs=tuple([_HBM] * (2 * n)),
        input_output_aliases={i: i for i in range(2 * n)},
        compiler_params=pltpu.CompilerParams(has_side_effects=_EFFECT),
    )(*srcs, *lands, send, recv, *after)
    return list(res[n:])


def _own_slot(land, own):
    return lax.dynamic_update_slice(land, own[None], (_index(_mesh_pos()),) + (0,) * own.ndim)


def cast_bf16(arrs, name):
    outs = []
    for i, a in enumerate(arrs):
        R, C = a.shape
        if R % 8 == 0:
            tr = _pick(R, (256, 128, 64, 32, 16, 8)) if R * C * 4 > (1 << 21) else R
            steps, spec = R // tr, pl.BlockSpec((tr, C), lambda i: (i, 0))
        else:
            steps, spec = C // 256, pl.BlockSpec((R, 256), lambda i: (0, i))

        def body(a_ref, o_ref):
            o_ref[...] = a_ref[...].astype(BF16)

        outs.append(pl.pallas_call(body, name=f"{name}_{i}", grid=(steps,), in_specs=[spec], out_specs=spec,
                                   out_shape=jax.ShapeDtypeStruct((R, C), BF16), compiler_params=_cparams())(a))
    return outs


def sum_slabs(parts, name):
    def fn(*a):
        outs = []
        for p in a:
            s = p[0].astype(F32)
            for k in range(1, N_DEV):
                s = s + p[k].astype(F32)
            outs.append(s)
        return tuple(outs)
    return list(small_call(fn, list(parts), [jax.ShapeDtypeStruct(p.shape[1:], F32) for p in parts], name))


def sum_parts(parts, name):
    def fn(p):
        g = p[0]
        for k in range(1, N_DEV):
            g = g + p[k]
        return (g,)
    return small_call(fn, [parts], [jax.ShapeDtypeStruct(parts.shape[1:], F32)], name)[0]


_SPLITS = np.cumsum([1536, 1536, 1536, 512, 512, 1024, 1024, 8, 8, 2048])[:-1].tolist()


def split_w_in(w):
    aq, ak, av, mq, mk, mv, mo, mi, mf, gates = jnp.split(w, _SPLITS, axis=1)
    R = w.shape[0]
    w_att = jnp.stack([aq.reshape(R, 12, 128), ak.reshape(R, 12, 128), av.reshape(R, 12, 128)], axis=2)
    gif = jnp.concatenate([mi.reshape(R, 4, 2), mf.reshape(R, 4, 2), jnp.zeros((R, 4, 124), w.dtype)], axis=2)
    w_ml = jnp.concatenate([mq.reshape(R, 4, 128), mk.reshape(R, 4, 128), mv.reshape(R, 4, 256),
                            mo.reshape(R, 4, 256), gif], axis=2)
    return w_att.reshape(R, ATT_COLS), w_ml.reshape(R, ML_COLS), gates


def merge_w_in(g_att, g_ml, g_gate):
    R = g_att.shape[0]
    a = g_att.reshape(R, 12, 3, 128)
    m = g_ml.reshape(R, 4, ML_PAIR_COLS)
    gif = m[:, :, 768:772]
    return jnp.concatenate([
        a[:, :, 0].reshape(R, 1536), a[:, :, 1].reshape(R, 1536), a[:, :, 2].reshape(R, 1536),
        m[:, :, 0:128].reshape(R, 512), m[:, :, 128:256].reshape(R, 512),
        m[:, :, 256:512].reshape(R, 1024), m[:, :, 512:768].reshape(R, 1024),
        gif[:, :, 0:2].reshape(R, 8), gif[:, :, 2:4].reshape(R, 8), g_gate], axis=1)


def _blk8(v, width=128):
    r = v.shape[0]
    return v.reshape(r, 1024 // width, width).transpose(1, 0, 2)


def _unblk8(v):
    nb, r, w = v.shape
    return v.transpose(1, 0, 2).reshape(r, nb * w)


def local_step(x, target, mods, w, small, late_w=None, early_g=None, w_in_g=None):
    late_w = late_w or (lambda after: w)
    big = {}
    early_g = early_g or (lambda g: big.update(g))
    w_in_g = w_in_g or (lambda g: big.update(w_in=merge_w_in(*g)))
    B = x.shape[0]
    T = B * SEQ
    shift1, scale1, gate1, shift2, scale2, gate2 = mods
    f2 = lambda a: a.reshape(T, a.shape[-1])
    f3 = lambda a: a.reshape(B, SEQ, a.shape[-1])

    rel_t = jnp.pad(small["rel_bias"].T, ((0, 4), (0, 0)))
    onehots = [_bucket_onehot(d) for _, d in ATT_GROUPS]
    biases = [bias_expand(rel_t, oh, f"bias_expand{g}").reshape(16, ATT_BLOCK, 2 * ATT_BLOCK)
              for g, oh in enumerate(onehots)]
    qg, kg = small["q_norm_g"], small["k_norm_g"]
    cw8 = _blk8(small["conv_w"])
    cb8 = _blk8(small["conv_b"])
    b_if = small["b_if"].reshape(2, 4, 2)
    bifp = jnp.concatenate([b_if[0], b_if[1], jnp.zeros((4, 124), F32)], axis=1).reshape(4, 1, 128)
    gn4 = small["mlstm_norm_g"].reshape(4, 1, 256)

    u = modnorm_fwd(x, small["norm1_g"], scale1, shift1, "modnorm1")
    u2d = f2(u)
    pa = f3(matmul(u2d, w["w_att"], mode="nn", name="proj_att"))
    pm = f3(matmul(u2d, w["w_ml"], mode="nn", name="proj_ml"))
    pg = matmul(u2d, w["w_gate"], mode="nn", name="proj_gate")
    os_, ls_ = [], []
    one_block = [SEQ // d == ATT_BLOCK for _, d in ATT_GROUPS]
    for g in range(3):
        o, l = (attn_fwd_classes if one_block[g] else attn_fwd)(pa, biases[g], qg, kg, g, f"attn_fwd{g}")
        os_.append(f2(o))
        ls_.append(f2(l))
    hg = mlstm_fwd(pm, cw8, cb8, bifp, gn4, "mlstm_fwd")
    w = {**w, **late_w(hg)}
    att, y_att = merge_att_out(os_, ls_, w["w_att_out"], "att_out")
    y_ml, z = ml_out_gate(f2(hg), w["w_ml_out"], pg, y_att, "ml_out")
    y, x1, u2 = out_proj_resid_modnorm(f2(z), w["w_out"], f2(x), gate1, small["norm2_g"], scale2, shift2, "out_proj")
    pre, hdn = matmul(u2, w["w_ff1"], mode="nn", name="ff1", out_dtypes=(BF16, BF16),
                      epi=lambda acc: (acc, jnp.square(jnp.maximum(acc, 0.0))))
    dx2, d_ffo, loss, d_gate2 = ff2_loss(hdn, w["w_ff2"], x1, gate2, f2(target), "ff2_loss")

    g_ff2 = matmul(hdn, d_ffo, mode="tn", name="g_ff2", out_dtypes=(BF16,))
    d_pre = matmul(d_ffo, w["w_ff2"], mode="nt", name="d_hdn", out_dtypes=(BF16,), extras=(pre,),
                   epi=lambda acc, p: (acc * (2.0 * jnp.maximum(p.astype(F32), 0.0)),))
    g_ff1 = matmul(u2, d_pre, mode="tn", name="g_ff1", out_dtypes=(BF16,))
    dx1, d_norm2, d_scale2, d_shift2, dy, d_gate1 = d_u_modnorm_bwd(
        d_pre, w["w_ff1"], None, x1, small["norm2_g"], scale2, shift2, dx2, y, gate1, "d_u2")
    g_out = matmul(f2(z), dy, mode="tn", name="g_out", out_dtypes=(BF16,))
    dpg, d_ya, d_ym = d_z_gate_bwd(dy, w["w_out"], pg, y_att, y_ml, "d_z")
    g_att_out = matmul(f2(att), f2(d_ya), mode="tn", name="g_att_out", out_dtypes=(BF16,))
    dmerge = d_att_merge_bwd(d_ya, w["w_att_out"], os_, ls_, "d_att")
    g_ml_out = matmul(f2(hg), f2(d_ym), mode="tn", name="g_ml_out", out_dtypes=(BF16,))
    d_hg = matmul(f2(d_ym), w["w_ml_out"], mode="nt", name="d_hg")
    started = early_g(dict(w_att_out=g_att_out, w_ml_out=g_ml_out, w_out=g_out, w_ff1=g_ff1, w_ff2=g_ff2))
    order = 0.0 if started is None else started[0, 0]
    dmerge = [f3(d) for d in dmerge]
    dpa = lax.empty((B, SEQ, ATT_COLS), BF16)
    d_rel = []
    d_qg = d_kg = None
    for g in range(3):
        dpa, dbias, dq_g, dk_g = (attn_bwd_classes if one_block[g] else attn_bwd)(
            pa, biases[g], qg + order, kg, dmerge[g], dmerge[3 + g], dpa, g, f"attn_bwd{g}")
        db8 = jnp.pad(dbias.reshape(4, -1), ((0, 4), (0, 0)))
        d_rel.append(bias_reduce(db8, onehots[g], f"bias_reduce{g}")[:4])
        d_qg = dq_g if d_qg is None else d_qg + dq_g
        d_kg = dk_g if d_kg is None else d_kg + dk_g
    dpm, dcw8, dcb8, dbifp, dgn4 = mlstm_bwd(pm, cw8, cb8, bifp, gn4 + order, f3(d_hg), "mlstm_bwd")
    g_w_att = matmul(u2d, f2(dpa), mode="tn", name="g_w_att", out_dtypes=(BF16,))
    g_w_ml = matmul(u2d, f2(dpm), mode="tn", name="g_w_ml", out_dtypes=(BF16,))
    g_w_gate = matmul(u2d, f2(dpg), mode="tn", name="g_w_gate", out_dtypes=(BF16,))
    started = w_in_g((g_w_att, g_w_ml, g_w_gate))
    du = matmul(f2(dpa), w["w_att"], mode="nt", name="d_u_att", after=() if started is None else (started,))
    du = matmul(f2(dpm), w["w_ml"], mode="nt", name="d_u_ml", extras=(du,), epi=lambda acc, e: (acc + e,))
    grad_x, d_norm1, d_scale1, d_shift1 = d_u_modnorm_bwd(
        f2(dpg), w["w_gate"], du, f2(x), small["norm1_g"], scale1, shift1, dx1, None, None, "d_u_gate")
    grad_x = f3(grad_x)

    d_mods = (d_shift1, d_scale1, d_gate1, d_shift2, d_scale2, d_gate2)
    dbif = dbifp.reshape(4, 128)
    small_g = dict(
        norm1_g=d_norm1, norm2_g=d_norm2,
        b_if=jnp.stack([dbif[:, 0:2].reshape(8), dbif[:, 2:4].reshape(8)]),
        conv_w=_unblk8(dcw8), conv_b=_unblk8(dcb8), q_norm_g=d_qg, k_norm_g=d_kg,
        rel_bias=jnp.concatenate(d_rel, axis=0).T,
        mlstm_norm_g=dgn4.reshape(1, 1024))
    return loss, grad_x, d_mods, big, small_g


_SMALL = (("b_ada", 6144), ("norm1_g", 1024), ("norm2_g", 1024), ("b_if", 16), ("conv_b", 1024),
          ("q_norm_g", 128), ("k_norm_g", 128), ("rel_bias", 384), ("mlstm_norm_g", 1024), ("conv_w", 4096))
_SMALL_ROWS = 120
_REPL = _SMALL[:-1]
_SMALL_SENT = _SMALL + (("loss", 1),)


def _pack(d, names, rows):
    flat = jnp.concatenate([d[k].reshape(-1) for k, _ in names])
    return jnp.pad(flat, (0, rows * 128 - flat.shape[0])).reshape(rows, 128)


def _unpack(slab, names, shapes):
    flat = slab.reshape(-1)
    out, off = {}, 0
    for k, nel in names:
        out[k] = flat[off:off + nel].reshape(shapes[k])
        off += nel
    return out


def kernel(x, c, w_ada, b_ada, norm1_g, norm2_g, w_in, b_if, conv_w, conv_b, q_norm_g, k_norm_g, rel_bias, mlstm_norm_g, w_att_out, w_ml_out, w_out, w_ff1, w_ff2, loss_target, m_w_ada, m_b_ada, m_norm1_g, m_norm2_g, m_w_in, m_b_if, m_conv_w, m_conv_b, m_q_norm_g, m_k_norm_g, m_rel_bias, m_mlstm_norm_g, m_w_att_out, m_w_ml_out, m_w_out, m_w_ff1, m_w_ff2, v_w_ada, v_b_ada, v_norm1_g, v_norm2_g, v_w_in, v_b_if, v_conv_w, v_conv_b, v_q_norm_g, v_k_norm_g, v_rel_bias, v_mlstm_norm_g, v_w_att_out, v_w_ml_out, v_w_out, v_w_ff1, v_w_ff2):
    P = dict(w_ada=w_ada, b_ada=b_ada, norm1_g=norm1_g, norm2_g=norm2_g, w_in=w_in, b_if=b_if, conv_w=conv_w,
             conv_b=conv_b, q_norm_g=q_norm_g, k_norm_g=k_norm_g, rel_bias=rel_bias, mlstm_norm_g=mlstm_norm_g,
             w_att_out=w_att_out, w_ml_out=w_ml_out, w_out=w_out, w_ff1=w_ff1, w_ff2=w_ff2)
    M = dict(w_ada=m_w_ada, b_ada=m_b_ada, norm1_g=m_norm1_g, norm2_g=m_norm2_g, w_in=m_w_in, b_if=m_b_if,
             conv_w=m_conv_w, conv_b=m_conv_b, q_norm_g=m_q_norm_g, k_norm_g=m_k_norm_g, rel_bias=m_rel_bias,
             mlstm_norm_g=m_mlstm_norm_g, w_att_out=m_w_att_out, w_ml_out=m_w_ml_out, w_out=m_w_out,
             w_ff1=m_w_ff1, w_ff2=m_w_ff2)
    V = dict(w_ada=v_w_ada, b_ada=v_b_ada, norm1_g=v_norm1_g, norm2_g=v_norm2_g, w_in=v_w_in, b_if=v_b_if,
             conv_w=v_conv_w, conv_b=v_conv_b, q_norm_g=v_q_norm_g, k_norm_g=v_k_norm_g, rel_bias=v_rel_bias,
             mlstm_norm_g=v_mlstm_norm_g, w_att_out=v_w_att_out, w_ml_out=v_w_ml_out, w_out=v_w_out,
             w_ff1=v_w_ff1, w_ff2=v_w_ff2)
    names = list(P)
    shapes = {k: P[k].shape for k in names}
    B = x.shape[0]
    me = _index(_mesh_pos())

    big_names = ("w_in", "w_att_out", "w_ml_out", "w_out", "w_ff1", "w_ff2")
    shards = cast_bf16([P[k][0] for k in big_names], "cast_w")
    w_in_g8, c8, conv_w8 = all_gather_two_level([shards[0], c, conv_w[0]], "gather_w_in")
    c_all = c8.reshape(N_DEV * B, D_MODEL)
    conv_w_full = conv_w8.transpose(1, 0, 2).reshape(4, 1024)
    w_att, w_ml, w_gate = split_w_in(w_in_g8.transpose(1, 0, 2).reshape(D_MODEL, D_IN))
    w = dict(w_att=w_att, w_ml=w_ml, w_gate=w_gate)

    (silu_c,) = small_call(lambda a: (_silu(a),), [c_all], [jax.ShapeDtypeStruct(c_all.shape, F32)], "silu_c")
    b_ada_cols = lax.dynamic_slice(b_ada, (0, me * 768), (1, 768))
    ada_cols = matmul(silu_c, w_ada[0], mode="nn", name="ada", extras=(jnp.broadcast_to(b_ada_cols, (N_DEV * B, 768)),),
                      epi=lambda acc, bb: (acc + bb,))
    (ada_t,) = all_to_all([ada_cols.reshape(N_DEV, B, 768)], "ada_exchange")
    ada = ada_t.transpose(1, 0, 2).reshape(B, 6 * D_MODEL)
    mods = tuple(ada[:, i * D_MODEL:(i + 1) * D_MODEL].reshape(B, 1, D_MODEL) for i in range(6))

    late_handle, late_order = exchange_start(shards[1:], "gather_late_start", False, after=(ada_t,))

    def late_w(after):
        lands = exchange_wait(late_handle, after, "gather_late_wait", False)
        gw = dict(zip(big_names[1:], [_own_slot(l, s) for l, s in zip(lands, shards[1:])]))
        return dict(w_att_out=gw["w_att_out"].transpose(1, 0, 2).reshape(512, D_MODEL),
                    w_ml_out=gw["w_ml_out"].reshape(D_MODEL, D_MODEL), w_out=gw["w_out"].reshape(D_MODEL, D_MODEL),
                    w_ff1=gw["w_ff1"].transpose(1, 0, 2).reshape(D_MODEL, D_FF),
                    w_ff2=gw["w_ff2"].reshape(D_FF, D_MODEL))

    pending = {}

    def send_grads(key, blocks, name):
        handle, order = exchange_start(blocks, name, True)
        pending[key] = (handle, [lax.dynamic_index_in_dim(b, me, 0, keepdims=False) for b in blocks])
        return order

    def early_g(g):
        return send_grads("late", [g["w_att_out"].reshape(512, N_DEV, 128).transpose(1, 0, 2),
                                   g["w_ml_out"].reshape(N_DEV, 128, D_MODEL), g["w_out"].reshape(N_DEV, 128, D_MODEL),
                                   g["w_ff1"].reshape(D_MODEL, N_DEV, 512).transpose(1, 0, 2),
                                   g["w_ff2"].reshape(N_DEV, 512, D_MODEL)], "grad_late_start")

    def w_in_g(parts):
        return send_grads("w_in", [g.reshape(N_DEV, D_MODEL // N_DEV, g.shape[1]) for g in parts], "grad_w_in_start")

    def recv_grads(key, after, name):
        handle, own = pending[key]
        return [_own_slot(l, o) for l, o in zip(exchange_wait(handle, after, name, True), own)]

    small = dict(norm1_g=norm1_g + late_order[0, 0], norm2_g=norm2_g, b_if=b_if[0], conv_w=conv_w_full, conv_b=conv_b,
                 q_norm_g=q_norm_g, k_norm_g=k_norm_g, rel_bias=rel_bias, mlstm_norm_g=mlstm_norm_g)
    loss, grad_x, d_mods, _, small_g = local_step(x, loss_target, mods, w, small, late_w, early_g, w_in_g)

    d_ada = jnp.concatenate([d.reshape(B, D_MODEL) for d in d_mods], axis=1)
    (small_g["b_ada"],) = small_call(lambda a: (jnp.sum(a, axis=0, keepdims=True),), [d_ada],
                                     [jax.ShapeDtypeStruct((1, 6144), F32)], "g_b_ada_local")
    small_g["loss"] = loss
    slabs = sum_slabs(recv_grads("w_in", grad_x, "grad_w_in_wait"), "w_in_slab_sum")
    w_in_cols = merge_w_in(*slabs).reshape(D_MODEL // N_DEV, N_DEV, W_IN_SHARD).transpose(1, 0, 2)
    d_ada_t, small_parts = _exchange(
        [d_ada.reshape(B, N_DEV, 768).transpose(1, 0, 2), _pack(small_g, _SMALL_SENT, _SMALL_ROWS)],
        "small_exchange", [True, False])
    cols_handle, _ = exchange_start([w_in_cols], "grad_w_in_cols_start", True, after=(small_parts,))
    d_ada_cols = d_ada_t.reshape(N_DEV * B, 768)
    g_w_ada = matmul(silu_c, d_ada_cols, mode="tn", name="g_w_ada")

    recv = recv_grads("late", grad_x, "grad_late_wait")
    small_sum = sum_parts(small_parts, "small_grad_sum")
    sg = _unpack(small_sum, _SMALL_SENT, {**{k: shapes[k] for k, _ in _REPL}, "conv_w": (4, 1024), "loss": ()})
    loss = sg["loss"]

    G, Dl, NM, NV = {}, {}, {}, {}
    for k, parts in zip(big_names[1:], recv):
        g, d, nm, nv = adamw(P[k][0], parts, M[k][0], V[k][0], f"adamw_{k}", parts=True)
        G[k], Dl[k], NM[k], NV[k] = g[None], d[None], nm[None], nv[None]
    g, d, nm, nv = adamw(w_ada[0], g_w_ada, m_w_ada[0], v_w_ada[0], "adamw_w_ada")
    G["w_ada"], Dl["w_ada"], NM["w_ada"], NV["w_ada"] = g[None], d[None], nm[None], nv[None]
    g_conv = lax.dynamic_slice(sg["conv_w"], (0, me * 128), (4, 128))
    g, d, nm, nv = adamw(conv_w[0], g_conv, m_conv_w[0], v_conv_w[0], "adamw_conv_w")
    G["conv_w"], Dl["conv_w"], NM["conv_w"], NV["conv_w"] = g[None], d[None], nm[None], nv[None]
    flat2 = lambda a: a.reshape(-1, a.shape[-1])
    keys = [k for k, _ in _REPL]
    upd = adamw_many([flat2(P[k]) for k in keys], [flat2(sg[k]) for k in keys], [flat2(M[k]) for k in keys],
                     [flat2(V[k]) for k in keys], "adamw_small")
    for j, k in enumerate(keys):
        G[k] = sg[k]
        Dl[k], NM[k], NV[k] = [upd[3 * j + t].reshape(shapes[k]) for t in range(3)]

    others = [NV[k] for k in big_names[1:]] + [NV["w_ada"], NV["conv_w"], upd[-1]]
    (g_w_in,) = exchange_wait(cols_handle, others, "grad_w_in_cols_wait", True)
    g_w_in = _own_slot(g_w_in, lax.dynamic_index_in_dim(w_in_cols, me, 0, keepdims=False))
    g, d, nm, nv = adamw(w_in[0], g_w_in.reshape(D_MODEL, W_IN_SHARD), m_w_in[0], v_w_in[0], "adamw_w_in")
    G["w_in"], Dl["w_in"], NM["w_in"], NV["w_in"] = g[None], d[None], nm[None], nv[None]

    return (loss, grad_x, *[G[k] for k in names], *[Dl[k] for k in names], *[NM[k] for k in names],
            *[NV[k] for k in names])
```

```python
import functools
import math

import numpy as np
import jax
import jax.numpy as jnp
from jax import lax
from jax.experimental import pallas as pl
from jax.experimental.pallas import tpu as pltpu

F32 = jnp.float32
BF16 = jnp.bfloat16

N_DEV = 8
D_MODEL = 1024
SEQ = 2048
ATT_GROUPS = ((128, 1), (512, 4), (2048, 16))
N_ATT_HEADS = 12
ATT_BLOCK = 128
HEAD_DIM = 128
ML_HEADS = 8
ML_PAIRS = 4
ML_CHUNK = 64
N_CHUNKS = SEQ // ML_CHUNK
N_BUCKETS = 32
MAX_DISTANCE = 2048
D_FF = 4096
D_IN = 9744
EPS = 1e-6

ADAM_LR = 0.001
ADAM_B1 = 0.9
ADAM_B2 = 0.999
ADAM_EPS = 1e-08
ADAM_WD = 0.01
ADAM_STEP = 10

ATT_HEAD_COLS = 3 * HEAD_DIM
ATT_COLS = N_ATT_HEADS * ATT_HEAD_COLS
ML_PAIR_COLS = 896
ML_COLS = ML_PAIRS * ML_PAIR_COLS
GATE_COLS = 2 * D_MODEL
W_IN_SHARD = D_IN // N_DEV

VMEM_LIMIT = 60 * 1024 * 1024


def _cparams(**kw):
    return pltpu.CompilerParams(vmem_limit_bytes=VMEM_LIMIT, **kw)


_NN = ((1,), (0,))
_NT = ((1,), (1,))
_TN = ((0,), (0,))


def _mxu(a, b, dims):
    return lax.dot_general(a.astype(BF16), b.astype(BF16), (dims, ((), ())), preferred_element_type=F32)


@jax.custom_vjp
def bdot_nn(a, b):
    return _mxu(a, b, _NN)


def _nn_fwd(a, b):
    return _mxu(a, b, _NN), (a, b)


def _nn_bwd(res, g):
    a, b = res
    return _mxu(g, b, _NT), _mxu(a, g, _TN)


bdot_nn.defvjp(_nn_fwd, _nn_bwd)


@jax.custom_vjp
def bdot_nt(a, b):
    return _mxu(a, b, _NT)


def _nt_fwd(a, b):
    return _mxu(a, b, _NT), (a, b)


def _nt_bwd(res, g):
    a, b = res
    return _mxu(g, b, _NN), _mxu(g, a, _TN)


bdot_nt.defvjp(_nt_fwd, _nt_bwd)


def _doth(a, b, dims=_NN):
    return lax.dot_general(a, b, (dims, ((), ())), precision=lax.Precision.HIGHEST, preferred_element_type=F32)


def _rms(x):
    return x * lax.rsqrt(jnp.mean(x * x, axis=-1, keepdims=True) + EPS)


def _pick(n, cands):
    for t in cands:
        if n % t == 0:
            return t
    raise ValueError(f"no tile for {n}")


MM_TILE_M = (1024, 512, 256, 128, 64, 32, 16, 8)
MM_TILE_N = (2048, 1792, 1536, 1024, 768, 512, 256, 128)
MM_TILE_K = (2048, 1792, 1536, 1024, 512, 256, 128, 64, 32)

def matmul(a, b, *, mode, name, out_dtypes=(F32,), epi=None, extras=(), after=()):
    if mode == "nn":
        (M, K), (K2, N) = a.shape, b.shape
    elif mode == "nt":
        (M, K), (N, K2) = a.shape, b.shape
    else:
        (K, M), (K2, N) = a.shape, b.shape
    assert K == K2, (a.shape, b.shape, mode)
    tm = _pick(M, MM_TILE_M)
    tn = _pick(N, MM_TILE_N)
    tk = _pick(K, MM_TILE_K)
    nk = K // tk
    n_ex = len(extras)
    n_out = len(out_dtypes)
    dims = {"nn": _NN, "nt": _NT, "tn": _TN}[mode]

    def finish(r, ex_refs, out_refs):
        outs = epi(r, *[e[...] for e in ex_refs]) if epi is not None else (r,)
        for o_ref, o in zip(out_refs, outs):
            o_ref[...] = o.astype(o_ref.dtype)

    def body(*refs):
        a_ref, b_ref = refs[0], refs[1]
        ex_refs = refs[2:2 + n_ex]
        out_refs = refs[2 + n_ex + len(after):2 + n_ex + len(after) + n_out]
        if nk == 1:
            finish(_mxu(a_ref[...], b_ref[...], dims), ex_refs, out_refs)
            return
        acc = refs[2 + n_ex + len(after) + n_out]
        k = pl.program_id(2)

        @pl.when(k == 0)
        def _():
            acc[...] = jnp.zeros_like(acc)

        acc[...] += _mxu(a_ref[...], b_ref[...], dims)

        @pl.when(k == nk - 1)
        def _():
            finish(acc[...], ex_refs, out_refs)

    if mode == "nn":
        a_spec = pl.BlockSpec((tm, tk), lambda i, j, k: (i, k))
        b_spec = pl.BlockSpec((tk, tn), lambda i, j, k: (k, j))
    elif mode == "nt":
        a_spec = pl.BlockSpec((tm, tk), lambda i, j, k: (i, k))
        b_spec = pl.BlockSpec((tn, tk), lambda i, j, k: (j, k))
    else:
        a_spec = pl.BlockSpec((tk, tm), lambda i, j, k: (k, i))
        b_spec = pl.BlockSpec((tk, tn), lambda i, j, k: (k, j))
    o_spec = pl.BlockSpec((tm, tn), lambda i, j, k: (i, j))
    res = pl.pallas_call(
        body,
        name=name,
        grid=(M // tm, N // tn, nk),
        in_specs=[a_spec, b_spec] + [o_spec] * n_ex + [pl.BlockSpec(memory_space=pl.ANY)] * len(after),
        out_specs=[o_spec] * n_out,
        out_shape=[jax.ShapeDtypeStruct((M, N), dt) for dt in out_dtypes],
        scratch_shapes=[pltpu.VMEM((tm, tn), F32)] if nk > 1 else [],
        compiler_params=_cparams(),
    )(a, b, *extras, *after)
    return res[0] if n_out == 1 else tuple(res)


ROW_MM_TILE = 512


def row_matmul(a, b, *, mode, name, extras, outs, epi):
    (M, K) = a.shape
    N = b.shape[1] if mode == "nn" else b.shape[0]
    tm = ROW_MM_TILE
    tk = _pick(K, MM_TILE_K)
    nk = K // tk
    n_ex, n_out = len(extras), len(outs)

    def body(*refs):
        a_ref, b_ref = refs[0], refs[1]
        ex_refs = refs[2:2 + n_ex]
        out_refs = refs[2 + n_ex:2 + n_ex + n_out]
        i = pl.program_id(0)
        dims = _NN if mode == "nn" else _NT
        if nk == 1:
            epi(_mxu(a_ref[...], b_ref[...], dims), i, ex_refs, out_refs)
            return
        acc = refs[2 + n_ex + n_out]
        k = pl.program_id(1)

        @pl.when(k == 0)
        def _():
            acc[...] = jnp.zeros_like(acc)

        acc[...] += _mxu(a_ref[...], b_ref[...], dims)

        @pl.when(k == nk - 1)
        def _():
            epi(acc[...], i, ex_refs, out_refs)

    def lift(index_map):
        return lambda i, k: index_map(i)

    b_spec = pl.BlockSpec((tk, N), lambda i, k: (k, 0)) if mode == "nn" else pl.BlockSpec((N, tk), lambda i, k: (0, k))
    res = pl.pallas_call(
        body, name=name, grid=(M // tm, nk),
        in_specs=[pl.BlockSpec((tm, tk), lambda i, k: (i, k)), b_spec]
        + [pl.BlockSpec(blk, lift(im)) for _, blk, im in extras],
        out_specs=[pl.BlockSpec(blk, lift(im)) for _, _, blk, im in outs],
        out_shape=[jax.ShapeDtypeStruct(shape, dt) for shape, dt, _, _ in outs],
        scratch_shapes=[pltpu.VMEM((tm, N), F32)] if nk > 1 else [],
        compiler_params=_cparams(),
    )(a, b, *[e[0] for e in extras])
    return tuple(res)


def _rows(arr):
    return (arr, (ROW_MM_TILE, arr.shape[1]), lambda i: (i, 0))


def _rows_out(T, dtype):
    return ((T, D_MODEL), dtype, (ROW_MM_TILE, D_MODEL), lambda i: (i, 0))


def _per_seq(arr):
    return (arr, (1, 1, D_MODEL), lambda i: (i // (SEQ // ROW_MM_TILE), 0, 0))


def _per_seq_out(B):
    return ((B, 1, D_MODEL), F32, (1, 1, D_MODEL), lambda i: (i // (SEQ // ROW_MM_TILE), 0, 0))


def _first_tile_of_seq(i):
    return i % (SEQ // ROW_MM_TILE) == 0


def small_call(fn, inputs, out_shapes, name):
    n_in = len(inputs)

    def body(*refs):
        outs = fn(*[r[...] for r in refs[:n_in]])
        for o_ref, o in zip(refs[n_in:], outs):
            o_ref[...] = o.astype(o_ref.dtype)

    res = pl.pallas_call(body, name=name, out_shape=list(out_shapes), compiler_params=_cparams())(*inputs)
    return tuple(res)


ROW_TILE = 512


def _modnorm(x, g, scale, shift):
    return _rms(x) * g * (1.0 + scale) + shift


def _row_spec(width):
    return pl.BlockSpec((1, ROW_TILE, width), lambda b, i: (b, i, 0))


def _mod_spec():
    return pl.BlockSpec((1, 1, D_MODEL), lambda b, i: (b, 0, 0))


def _vec_spec():
    return pl.BlockSpec((1, D_MODEL), lambda b, i: (0, 0))


def modnorm_fwd(x, g, scale, shift, name):
    B, S, D = x.shape

    def body(x_ref, g_ref, sc_ref, sh_ref, u_ref):
        u_ref[0] = _modnorm(x_ref[0], g_ref[...], sc_ref[0], sh_ref[0]).astype(BF16)

    return pl.pallas_call(
        body, name=name, grid=(B, S // ROW_TILE),
        in_specs=[_row_spec(D), _vec_spec(), _mod_spec(), _mod_spec()],
        out_specs=_row_spec(D),
        out_shape=jax.ShapeDtypeStruct((B, S, D), BF16),
        compiler_params=_cparams(),
    )(x, g, scale, shift)


def _gain_spec(g):
    return (g, (1, D_MODEL), lambda i: (0, 0))


def out_proj_resid_modnorm(z, w_out, x, gate, g, scale, shift, name):
    T = z.shape[0]

    def epi(acc, i, ex, out):
        x_ref, gt_ref, g_ref, sc_ref, sh_ref = ex
        y_ref, x1_ref, u_ref = out
        y_ref[...] = acc
        x1 = x_ref[...] + gt_ref[0] * acc
        x1_ref[...] = x1
        u_ref[...] = _modnorm(x1, g_ref[...], sc_ref[0], sh_ref[0]).astype(BF16)

    return row_matmul(z, w_out, mode="nn", name=name,
                      extras=[_rows(x), _per_seq(gate), _gain_spec(g), _per_seq(scale), _per_seq(shift)],
                      outs=[_rows_out(T, F32), _rows_out(T, F32), _rows_out(T, BF16)], epi=epi)


def ff2_loss(hdn, w_ff2, x1, gate, target, name):
    T = hdn.shape[0]

    def epi(acc, i, ex, out):
        x_ref, gt_ref, t_ref = ex
        dx_ref, dffo_ref, loss_ref, dg_ref = out

        @pl.when(i == 0)
        def _():
            loss_ref[...] = jnp.zeros_like(loss_ref)

        @pl.when(_first_tile_of_seq(i))
        def _():
            dg_ref[...] = jnp.zeros_like(dg_ref)

        err = x_ref[...] + gt_ref[0] * acc - t_ref[...]
        dx = err * (1.0 / D_MODEL)
        dx_ref[...] = dx
        dffo_ref[...] = (gt_ref[0] * dx).astype(BF16)
        loss_ref[...] += 0.5 * jnp.sum(jnp.mean(err * err, axis=-1, keepdims=True), axis=0, keepdims=True)
        dg_ref[0] += jnp.sum(dx * acc, axis=0, keepdims=True)

    return row_matmul(hdn, w_ff2, mode="nn", name=name,
                      extras=[_rows(x1), _per_seq(gate), _rows(target)],
                      outs=[_rows_out(T, F32), _rows_out(T, BF16), ((1, 1), F32, (1, 1), lambda i: (0, 0)),
                            _per_seq_out(T // SEQ)], epi=epi)


def d_u_modnorm_bwd(a, w, du_prev, x, g, scale, shift, dx_res, y, gate, name):
    T = a.shape[0]
    B = T // SEQ
    n_prev, resid = int(du_prev is not None), y is not None

    def epi(acc, i, ex, out):
        x_ref, g_ref, sc_ref, sh_ref, dr_ref = ex[n_prev:n_prev + 5]
        dx_ref, dg_ref, dsc_ref, dsh_ref = out[:4]

        @pl.when(i == 0)
        def _():
            dg_ref[...] = jnp.zeros_like(dg_ref)

        @pl.when(_first_tile_of_seq(i))
        def _():
            for r in out[2:4] + out[5:]:
                r[...] = jnp.zeros_like(r)

        du = acc + ex[0][...] if n_prev else acc
        _, vjp = jax.vjp(_modnorm, x_ref[...], g_ref[...], sc_ref[0], sh_ref[0])
        dxn, dg, dsc, dsh = vjp(du)
        dx = dxn + dr_ref[...]
        dx_ref[...] = dx
        dg_ref[...] += dg
        dsc_ref[0] += dsc
        dsh_ref[0] += dsh
        if resid:
            y_ref, gt_ref = ex[n_prev + 5:]
            out[4][...] = (gt_ref[0] * dx).astype(BF16)
            out[5][0] += jnp.sum(dx * y_ref[...], axis=0, keepdims=True)

    extras = ([_rows(du_prev)] if n_prev else []) + [_rows(x), _gain_spec(g), _per_seq(scale), _per_seq(shift),
                                                     _rows(dx_res)] + ([_rows(y), _per_seq(gate)] if resid else [])
    outs = [_rows_out(T, F32), ((1, D_MODEL), F32, (1, D_MODEL), lambda i: (0, 0)), _per_seq_out(B), _per_seq_out(B)]
    outs += [_rows_out(T, BF16), _per_seq_out(B)] if resid else []
    return row_matmul(a, w, mode="nt", name=name, extras=extras, outs=outs, epi=epi)


def _bucket_table(dilation):
    i = np.arange(ATT_BLOCK)[:, None]
    j = np.arange(2 * ATT_BLOCK)[None, :]
    delta = ATT_BLOCK + i - j
    dist = np.maximum(delta, 0) * dilation
    max_exact = N_BUCKETS // 2
    d = np.maximum(dist, max_exact).astype(np.float32)
    large = max_exact + (np.log(d / np.float32(max_exact)) / np.float32(math.log(MAX_DISTANCE / max_exact))
                         * np.float32(N_BUCKETS - max_exact)).astype(np.int32)
    large = np.minimum(large, N_BUCKETS - 1)
    return np.where(dist < max_exact, dist, large).astype(np.int32)


def _bucket_onehot(dilation):
    bt = jnp.asarray(_bucket_table(dilation).reshape(1, -1))
    return (bt == jnp.arange(N_BUCKETS, dtype=jnp.int32)[:, None]).astype(F32)


def bias_expand(rel_t, onehot, name):
    def fn(r, oh):
        return (_doth(r, oh),)
    return small_call(fn, [rel_t, onehot], [jax.ShapeDtypeStruct((rel_t.shape[0], onehot.shape[1]), F32)], name)[0]


def bias_reduce(dbias_flat, onehot, name):
    def fn(db, oh):
        return (_doth(db, oh, _NT),)
    return small_call(fn, [dbias_flat, onehot], [jax.ShapeDtypeStruct((dbias_flat.shape[0], N_BUCKETS), F32)], name)[0]


def _qk_norm(x, g):
    return _rms(x) * g


def _masked_bias(bias):
    i = lax.broadcasted_iota(jnp.int32, (ATT_BLOCK, 2 * ATT_BLOCK), 0)
    j = lax.broadcasted_iota(jnp.int32, (ATT_BLOCK, 2 * ATT_BLOCK), 1)
    bm = jnp.where(jnp.logical_and(j >= i, j <= i + ATT_BLOCK), bias, -jnp.inf)
    return bm, bm[:, ATT_BLOCK:]


def _attn_tile(qn, kn, v, bias):
    s = bdot_nt(qn, kn) * (HEAD_DIM ** -0.5) + bias
    m = lax.stop_gradient(jnp.max(s, axis=-1, keepdims=True))
    p = jnp.exp(s - m)
    l = jnp.sum(p, axis=-1, keepdims=True)
    o = bdot_nn(p, v) / l
    lse = jnp.broadcast_to(m + jnp.log(l), (ATT_BLOCK, HEAD_DIM))
    return o, lse


def _attn_tiles(dilation, rows=SEQ):
    nb = rows // dilation // ATT_BLOCK
    return [(r, n) for r in range(dilation) for n in range(nb)]


def _attn_rows(r, n, dilation, nblk=1):
    if dilation == 1:
        return pl.ds(r + n * ATT_BLOCK, nblk * ATT_BLOCK)
    return pl.ds(r + n * ATT_BLOCK * dilation, nblk * ATT_BLOCK, stride=dilation)


_QL, _KL, _VL = slice(0, 128), slice(128, 256), slice(256, 384)


def _qkv_specs(hb):
    return [pl.BlockSpec((None, SEQ, HEAD_DIM), functools.partial(lambda b, h, j: (b, 0, 3 * (hb + h) + j), j=j))
            for j in range(3)]


def attn_fwd(pa, bias, qg, kg, group, name):
    B = pa.shape[0]
    dilation = ATT_GROUPS[group][1]
    hb = group * 4

    def body(q_ref, k_ref, v_ref, b_ref, qg_ref, kg_ref, o_ref, l_ref, qn_s, kn_s):
        qn_s[...] = _qk_norm(q_ref[...], qg_ref[...])
        kn_s[...] = _qk_norm(k_ref[...], kg_ref[...])
        bias_all, bias_first = _masked_bias(b_ref[0])
        for (r, n) in _attn_tiles(dilation):
            rows = _attn_rows(r, n, dilation)
            if n == 0:
                krows, bias_t = rows, bias_first
            else:
                krows, bias_t = _attn_rows(r, n - 1, dilation, 2), bias_all
            o, lse = _attn_tile(qn_s[rows, :], kn_s[krows, :], v_ref[krows, :], bias_t)
            o_ref[rows, :] = o
            l_ref[rows, :] = lse

    head_out = pl.BlockSpec((None, SEQ, HEAD_DIM), lambda b, h: (b, 0, h))
    return pl.pallas_call(
        body, name=name, grid=(B, 4),
        in_specs=_qkv_specs(hb) + [
                  pl.BlockSpec((1, ATT_BLOCK, 2 * ATT_BLOCK), lambda b, h: (hb + h, 0, 0)),
                  pl.BlockSpec((1, HEAD_DIM), lambda b, h: (0, 0)),
                  pl.BlockSpec((1, HEAD_DIM), lambda b, h: (0, 0))],
        out_specs=[head_out, head_out],
        out_shape=[jax.ShapeDtypeStruct((B, SEQ, 512), F32), jax.ShapeDtypeStruct((B, SEQ, 512), F32)],
        scratch_shapes=[pltpu.VMEM((SEQ, HEAD_DIM), F32)] * 2,
        compiler_params=_cparams(),
    )(pa, pa, pa, bias, qg, kg)


def attn_bwd(pa, bias, qg, kg, do, dlse, dpa, group, name):
    B = pa.shape[0]
    dilation = ATT_GROUPS[group][1]
    hb = group * 4

    def body(q_ref, k_ref, v_ref, b_ref, qg_ref, kg_ref, do_ref, dl_ref, dpa_in,
             dp_ref, db_ref, dqg_ref, dkg_ref, qn_s, kn_s, dq_s, dk_s, dv_s):
        del dpa_in
        h_id = pl.program_id(1)

        @pl.when(jnp.logical_and(pl.program_id(0) == 0, h_id == 0))
        def _():
            db_ref[...] = jnp.zeros_like(db_ref)
            dqg_ref[...] = jnp.zeros_like(dqg_ref)
            dkg_ref[...] = jnp.zeros_like(dkg_ref)

        dk_s[...] = jnp.zeros_like(dk_s)
        dv_s[...] = jnp.zeros_like(dv_s)
        qn_s[...] = _qk_norm(q_ref[...], qg_ref[...])
        kn_s[...] = _qk_norm(k_ref[...], kg_ref[...])
        bias_all, bias_first = _masked_bias(b_ref[0])
        for (r, n) in _attn_tiles(dilation):
            rows = _attn_rows(r, n, dilation)
            if n == 0:
                krows, bias_t = rows, bias_first
            else:
                krows, bias_t = _attn_rows(r, n - 1, dilation, 2), bias_all
            _, vjp = jax.vjp(_attn_tile, qn_s[rows, :], kn_s[krows, :], v_ref[krows, :], bias_t)
            dqn, dkn, dv, dbias = vjp((do_ref[rows, :], dl_ref[rows, :]))
            dq_s[rows, :] = dqn
            dk_s[krows, :] += dkn
            dv_s[krows, :] += dv
            if n == 0:
                db_ref[h_id, :, ATT_BLOCK:] += dbias
            else:
                db_ref[h_id] += dbias
        for x_ref, g_ref, d_s, dg_ref, lanes in ((q_ref, qg_ref, dq_s, dqg_ref, _QL), (k_ref, kg_ref, dk_s, dkg_ref, _KL)):
            _, vjp = jax.vjp(_qk_norm, x_ref[...], g_ref[...])
            dx, dg = vjp(d_s[...])
            dp_ref[0, :, lanes] = dx.astype(BF16)
            dg_ref[...] += dg
        dp_ref[0, :, _VL] = dv_s[...].astype(BF16)

    const2 = lambda b, h: (0, 0)
    head_in = pl.BlockSpec((None, SEQ, HEAD_DIM), lambda b, h: (b, 0, h))
    head_blk = pl.BlockSpec((1, SEQ, ATT_HEAD_COLS), lambda b, h: (b, 0, hb + h))
    return pl.pallas_call(
        body, name=name, grid=(B, 4),
        in_specs=_qkv_specs(hb) + [
                  pl.BlockSpec((1, ATT_BLOCK, 2 * ATT_BLOCK), lambda b, h: (hb + h, 0, 0)),
                  pl.BlockSpec((1, HEAD_DIM), const2), pl.BlockSpec((1, HEAD_DIM), const2),
                  head_in, head_in,
                  pl.BlockSpec(memory_space=pl.ANY)],
        out_specs=[head_blk,
                   pl.BlockSpec((4, ATT_BLOCK, 2 * ATT_BLOCK), lambda b, h: (0, 0, 0)),
                   pl.BlockSpec((1, HEAD_DIM), const2), pl.BlockSpec((1, HEAD_DIM), const2)],
        out_shape=[jax.ShapeDtypeStruct(dpa.shape, BF16),
                   jax.ShapeDtypeStruct((4, ATT_BLOCK, 2 * ATT_BLOCK), F32),
                   jax.ShapeDtypeStruct((1, HEAD_DIM), F32), jax.ShapeDtypeStruct((1, HEAD_DIM), F32)],
        scratch_shapes=[pltpu.VMEM((SEQ, HEAD_DIM), F32)] * 5,
        input_output_aliases={8: 0},
        compiler_params=_cparams(),
    )(pa, pa, pa, bias, qg, kg, do, dlse, dpa)


def _attn_classes(q, k, v, bias, qg, kg):
    s = cdot_nt(_qk_norm(q, qg), _qk_norm(k, kg)) * (HEAD_DIM ** -0.5) + bias
    m = lax.stop_gradient(jnp.max(s, axis=-1, keepdims=True))
    p = jnp.exp(s - m)
    l = jnp.sum(p, axis=-1, keepdims=True)
    o = cdot_nn(p, v) / l
    return o, jnp.broadcast_to(m + jnp.log(l), o.shape)


def _gather_classes(src_ref, dst_s, dilation):
    for r in range(dilation):
        dst_s[r] = src_ref[pl.ds(r, ATT_BLOCK, stride=dilation), :]


def _scatter_classes(src_s, dst_ref, dilation):
    for r in range(dilation):
        dst_ref[pl.ds(r, ATT_BLOCK, stride=dilation), :] = src_s[r]


def attn_fwd_classes(pa, bias, qg, kg, group, name):
    B = pa.shape[0]
    dilation = ATT_GROUPS[group][1]
    hb = group * 4

    def body(q_ref, k_ref, v_ref, b_ref, qg_ref, kg_ref, o_ref, l_ref, q_s, k_s, v_s):
        _gather_classes(q_ref, q_s, dilation)
        _gather_classes(k_ref, k_s, dilation)
        _gather_classes(v_ref, v_s, dilation)
        o, lse = _attn_classes(q_s[...], k_s[...], v_s[...], _masked_bias(b_ref[0])[1], qg_ref[...], kg_ref[...])
        q_s[...], k_s[...] = o, lse
        _scatter_classes(q_s, o_ref, dilation)
        _scatter_classes(k_s, l_ref, dilation)

    head_out = pl.BlockSpec((None, SEQ, HEAD_DIM), lambda b, h: (b, 0, h))
    return pl.pallas_call(
        body, name=name, grid=(B, 4),
        in_specs=_qkv_specs(hb) + [
                  pl.BlockSpec((1, ATT_BLOCK, 2 * ATT_BLOCK), lambda b, h: (hb + h, 0, 0)),
                  pl.BlockSpec((1, HEAD_DIM), lambda b, h: (0, 0)),
                  pl.BlockSpec((1, HEAD_DIM), lambda b, h: (0, 0))],
        out_specs=[head_out, head_out],
        out_shape=[jax.ShapeDtypeStruct((B, SEQ, 512), F32), jax.ShapeDtypeStruct((B, SEQ, 512), F32)],
        scratch_shapes=[pltpu.VMEM((dilation, ATT_BLOCK, HEAD_DIM), F32)] * 3,
        compiler_params=_cparams(),
    )(pa, pa, pa, bias, qg, kg)


def attn_bwd_classes(pa, bias, qg, kg, do, dlse, dpa, group, name):
    B = pa.shape[0]
    dilation = ATT_GROUPS[group][1]
    hb = group * 4

    def body(q_ref, k_ref, v_ref, b_ref, qg_ref, kg_ref, do_ref, dl_ref, dpa_in,
             dp_ref, db_ref, dqg_ref, dkg_ref, q_s, k_s, v_s, do_s, dl_s, rows_s):
        del dpa_in
        h_id = pl.program_id(1)

        @pl.when(jnp.logical_and(pl.program_id(0) == 0, h_id == 0))
        def _():
            db_ref[...] = jnp.zeros_like(db_ref)
            dqg_ref[...] = jnp.zeros_like(dqg_ref)
            dkg_ref[...] = jnp.zeros_like(dkg_ref)

        for src, dst in ((q_ref, q_s), (k_ref, k_s), (v_ref, v_s), (do_ref, do_s), (dl_ref, dl_s)):
            _gather_classes(src, dst, dilation)
        _, vjp = jax.vjp(_attn_classes, q_s[...], k_s[...], v_s[...], _masked_bias(b_ref[0])[1],
                         qg_ref[...], kg_ref[...])
        dq, dk, dv, dbias, dqg, dkg = vjp((do_s[...], dl_s[...]))
        db_ref[h_id, :, ATT_BLOCK:] += dbias
        dqg_ref[...] += dqg
        dkg_ref[...] += dkg
        for d, lanes in ((dq, _QL), (dk, _KL), (dv, _VL)):
            q_s[...] = d
            _scatter_classes(q_s, rows_s, dilation)
            dp_ref[0, :, lanes] = rows_s[...].astype(BF16)

    const2 = lambda b, h: (0, 0)
    head_in = pl.BlockSpec((None, SEQ, HEAD_DIM), lambda b, h: (b, 0, h))
    head_blk = pl.BlockSpec((1, SEQ, ATT_HEAD_COLS), lambda b, h: (b, 0, hb + h))
    return pl.pallas_call(
        body, name=name, grid=(B, 4),
        in_specs=_qkv_specs(hb) + [
                  pl.BlockSpec((1, ATT_BLOCK, 2 * ATT_BLOCK), lambda b, h: (hb + h, 0, 0)),
                  pl.BlockSpec((1, HEAD_DIM), const2), pl.BlockSpec((1, HEAD_DIM), const2),
                  head_in, head_in,
                  pl.BlockSpec(memory_space=pl.ANY)],
        out_specs=[head_blk,
                   pl.BlockSpec((4, ATT_BLOCK, 2 * ATT_BLOCK), lambda b, h: (0, 0, 0)),
                   pl.BlockSpec((1, HEAD_DIM), const2), pl.BlockSpec((1, HEAD_DIM), const2)],
        out_shape=[jax.ShapeDtypeStruct(dpa.shape, BF16),
                   jax.ShapeDtypeStruct((4, ATT_BLOCK, 2 * ATT_BLOCK), F32),
                   jax.ShapeDtypeStruct((1, HEAD_DIM), F32), jax.ShapeDtypeStruct((1, HEAD_DIM), F32)],
        scratch_shapes=[pltpu.VMEM((dilation, ATT_BLOCK, HEAD_DIM), F32)] * 5 + [pltpu.VMEM((SEQ, HEAD_DIM), F32)],
        input_output_aliases={8: 0},
        compiler_params=_cparams(),
    )(pa, pa, pa, bias, qg, kg, do, dlse, dpa)


def _merge(o0, o1, o2, l0, l1, l2):
    mx = jnp.maximum(jnp.maximum(l0, l1), l2)
    e0, e1, e2 = jnp.exp(l0 - mx), jnp.exp(l1 - mx), jnp.exp(l2 - mx)
    den = e0 + e1 + e2
    return (e0 / den) * o0 + (e1 / den) * o1 + (e2 / den) * o2


def merge_att_out(os_, ls_, w_att_out, name):
    T = os_[0].shape[0]
    tile = pl.BlockSpec((ROW_MM_TILE, 512), lambda i: (i, 0))

    def body(o0, o1, o2, l0, l1, l2, w_ref, a_ref, y_ref):
        att = _merge(o0[...], o1[...], o2[...], l0[...], l1[...], l2[...]).astype(BF16)
        a_ref[...] = att
        y_ref[...] = _mxu(att, w_ref[...], _NN)

    return pl.pallas_call(
        body, name=name, grid=(T // ROW_MM_TILE,),
        in_specs=[tile] * 6 + [pl.BlockSpec((512, D_MODEL), lambda i: (0, 0))],
        out_specs=[tile, pl.BlockSpec((ROW_MM_TILE, D_MODEL), lambda i: (i, 0))],
        out_shape=[jax.ShapeDtypeStruct((T, 512), BF16), jax.ShapeDtypeStruct((T, D_MODEL), F32)],
        compiler_params=_cparams(),
    )(*os_, *ls_, w_att_out)


def d_att_merge_bwd(d_ya, w_att_out, os_, ls_, name):
    T = d_ya.shape[0]
    tile = lambda a: (a, (ROW_MM_TILE, 512), lambda i: (i, 0))

    def epi(acc, i, ex, out):
        _, vjp = jax.vjp(_merge, *[e[...] for e in ex])
        for o_ref, g in zip(out, vjp(acc)):
            o_ref[...] = g

    return row_matmul(d_ya, w_att_out, mode="nt", name=name, extras=[tile(a) for a in list(os_) + list(ls_)],
                      outs=[((T, 512), F32, (ROW_MM_TILE, 512), lambda i: (i, 0))] * 6, epi=epi)


def _gate_mix(ga, gm, ya, ym):
    return jax.nn.sigmoid(ga) * ya + jax.nn.sigmoid(gm) * ym


def _gate_halves(pg):
    return [(pg, (ROW_MM_TILE, D_MODEL), lambda i: (i, 0)), (pg, (ROW_MM_TILE, D_MODEL), lambda i: (i, 1))]


def ml_out_gate(hg, w_ml_out, pg, ya, name):
    T = hg.shape[0]

    def epi(acc, i, ex, out):
        ga, gm, ya_ref = ex
        out[0][...] = acc
        out[1][...] = _gate_mix(ga[...], gm[...], ya_ref[...], acc).astype(BF16)

    return row_matmul(hg, w_ml_out, mode="nn", name=name, extras=_gate_halves(pg) + [_rows(ya)],
                      outs=[_rows_out(T, F32), _rows_out(T, BF16)], epi=epi)


def d_z_gate_bwd(dy, w_out, pg, ya, ym, name):
    T = dy.shape[0]

    def epi(acc, i, ex, out):
        ga, gm, ya_ref, ym_ref = ex
        dpg_ref, dya_ref, dym_ref = out
        _, vjp = jax.vjp(_gate_mix, ga[...], gm[...], ya_ref[...], ym_ref[...])
        dga, dgm, dya, dym = vjp(acc)
        dpg_ref[:, :D_MODEL] = dga.astype(BF16)
        dpg_ref[:, D_MODEL:] = dgm.astype(BF16)
        dya_ref[...] = dya.astype(BF16)
        dym_ref[...] = dym.astype(BF16)

    return row_matmul(dy, w_out, mode="nt", name=name, extras=_gate_halves(pg) + [_rows(ya), _rows(ym)],
                      outs=[((T, GATE_COLS), BF16, (ROW_MM_TILE, GATE_COLS), lambda i: (i, 0)),
                            _rows_out(T, BF16), _rows_out(T, BF16)], epi=epi)


def _log_sigmoid(x):
    return jnp.minimum(x, 0.0) - jnp.log(1.0 + jnp.exp(-jnp.abs(x)))


def _head_mask(e):
    lane = lax.broadcasted_iota(jnp.int32, (1, 128), 1)
    return jnp.logical_and(lane >= e * 64, lane < (e + 1) * 64).astype(F32)


def _bmxu(a, b, ca, cb):
    return lax.dot_general(a.astype(BF16), b.astype(BF16), (((ca,), (cb,)), ((0,), (0,))), preferred_element_type=F32)


@jax.custom_vjp
def cdot_nt(a, b):
    return _bmxu(a, b, 2, 2)


cdot_nt.defvjp(lambda a, b: (_bmxu(a, b, 2, 2), (a, b)),
               lambda res, g: (_bmxu(g, res[1], 2, 1), _bmxu(g, res[0], 1, 1)))


@jax.custom_vjp
def cdot_nn(a, b):
    return _bmxu(a, b, 2, 1)


cdot_nn.defvjp(lambda a, b: (_bmxu(a, b, 2, 1), (a, b)),
               lambda res, g: (_bmxu(g, res[1], 2, 2), _bmxu(res[0], g, 1, 1)))


@jax.custom_vjp
def cdot_tn(a, b):
    return _bmxu(a, b, 1, 1)


cdot_tn.defvjp(lambda a, b: (_bmxu(a, b, 1, 1), (a, b)),
               lambda res, g: (_bmxu(res[1], g, 2, 2), _bmxu(res[0], g, 2, 1)))


def _top_bits(x):
    return lax.bitcast_convert_type(lax.bitcast_convert_type(x, jnp.uint32) & jnp.uint32(0xFFFF0000), F32)


def _split3(x):
    hi = _top_bits(x)
    r = x - hi
    mid = _top_bits(r)
    return hi, mid, r - mid


def _parts_in_lanes(col):
    hi, mid, lo = _split3(col)
    lane = lax.broadcasted_iota(jnp.int32, (1, 1, 8), 2)
    return jnp.where(lane == 0, hi, jnp.where(lane == 1, mid, jnp.where(lane == 2, lo, 0.0)))


def _parts_in_rows(row):
    hi, mid, lo = _split3(row)
    sub = lax.broadcasted_iota(jnp.int32, (1, 8, 1), 1)
    return jnp.where(sub == 0, hi, jnp.where(sub == 1, mid, jnp.where(sub == 2, lo, 0.0)))


def _chunk_matrix(kind, c):
    ri = lax.broadcasted_iota(jnp.int32, (c, ML_CHUNK, ML_CHUNK), 1)
    ci = lax.broadcasted_iota(jnp.int32, (c, ML_CHUNK, ML_CHUNK), 2)
    return {"eye": ri == ci, "lower": ri >= ci, "upper": ri <= ci}[kind].astype(F32)


def _col_col(kind, col):
    out = _bmxu(_chunk_matrix(kind, col.shape[0]), _parts_in_lanes(col), 2, 1)
    return jnp.sum(out, axis=-1, keepdims=True)


def _col_row(col):
    out = _bmxu(_parts_in_lanes(col), _chunk_matrix("eye", col.shape[0]), 1, 1)
    return jnp.sum(out, axis=1, keepdims=True)


def _row_col(row):
    out = _bmxu(_chunk_matrix("eye", row.shape[0]), _parts_in_rows(row), 2, 2)
    return jnp.sum(out, axis=-1, keepdims=True)


@jax.custom_vjp
def chunk_cumsum(col):
    return _col_col("lower", col)


chunk_cumsum.defvjp(lambda col: (_col_col("lower", col), None), lambda _, g: (_col_col("upper", g),))


@jax.custom_vjp
def col_to_row(col):
    return _col_row(col)


col_to_row.defvjp(lambda col: (_col_row(col), None), lambda _, g: (_row_col(g),))


def _gate_block(ifb):
    lane = lax.broadcasted_iota(jnp.int32, (1, 128), 1)
    return jnp.where(lane >= 2, _log_sigmoid(ifb), ifb)


def _ml_intra(q2, k2, v, ifb, *, e):
    c, L = q2.shape[0] // ML_CHUNK, ML_CHUNK
    hm = _head_mask(e)
    q3 = (q2 * hm).reshape(c, L, 128)
    k3 = (k2 * hm).reshape(c, L, 128)
    v3 = v.reshape(c, L, 128)
    if3 = ifb.reshape(c, L, 128)
    lanes = lax.broadcasted_iota(jnp.int32, (c, L, 128), 2)
    li = jnp.sum(jnp.where(lanes == e, if3, 0.0), axis=-1, keepdims=True)
    lf = jnp.sum(jnp.where(lanes == 2 + e, if3, 0.0), axis=-1, keepdims=True)
    b = chunk_cumsum(lf)
    last = lax.broadcasted_iota(jnp.int32, (1, L, 1), 1) == L - 1
    b_end = jnp.sum(jnp.where(last, b, 0.0), axis=1, keepdims=True)
    causal = lax.broadcasted_iota(jnp.int32, (L, L), 0) >= lax.broadcasted_iota(jnp.int32, (L, L), 1)
    Dm = jnp.where(causal, b + col_to_row(li - b), -jnp.inf)
    mD = lax.stop_gradient(jnp.max(Dm, axis=-1, keepdims=True))
    P0 = cdot_nt(q3, k3) * jnp.exp(Dm - mD)
    H0 = cdot_nn(P0, v3)
    r0 = jnp.sum(P0, axis=-1, keepdims=True)
    g = b_end - b + li
    mg = lax.stop_gradient(jnp.max(g, axis=1, keepdims=True))
    kw = jnp.exp(g - mg) * k3
    return H0, r0, cdot_tn(kw, v3), jnp.sum(kw, axis=1, keepdims=True), b, b_end, mD, mg


def _ml_inter(q2, mo, gn, H0, r0, b, C_in, n_in, *, mD, m_in, e):
    c, L = q2.shape[0] // ML_CHUNK, ML_CHUNK
    q3 = (q2 * _head_mask(e)).reshape(c, L, 128)
    a = b + m_in
    m_t = lax.stop_gradient(jnp.maximum(a, mD))
    c1 = jnp.exp(mD - m_t)
    c2 = jnp.exp(a - m_t)
    num = c1 * H0 + c2 * cdot_nn(q3, C_in)
    nq = c1 * r0 + c2 * jnp.sum(q3 * n_in, axis=-1, keepdims=True)
    h = num / jnp.maximum(jnp.abs(nq), jnp.exp(-m_t))
    hg = _rms(h) * gn * jax.nn.sigmoid(mo.reshape(c, L, 128))
    return hg.reshape(c * L, 128)


def _state_sweep(U_s, un_s, be_s, mg_s, Cin_s, nin_s, min_s, al_s, bt_s):
    def step(j, carry):
        C, n, m = carry
        Cin_s[j], nin_s[j], min_s[j] = C, n, m
        m_out = jnp.maximum(be_s[j] + m, mg_s[j])
        al = jnp.exp(be_s[j] + m - m_out)
        bt = jnp.exp(mg_s[j] - m_out)
        al_s[j], bt_s[j] = al, bt
        return al * C + bt * U_s[j], al * n + bt * un_s[j], m_out

    lax.fori_loop(0, N_CHUNKS, step, (jnp.zeros((128, 128), F32), jnp.zeros((1, 128), F32), jnp.zeros((1, 1), F32)))


def _state_sweep_bwd(U_s, un_s, dbe_s, Cin_s, nin_s, dCp_s, dnp_s, al_s, bt_s):
    def step(t, carry):
        j = N_CHUNKS - 1 - t
        dC, dn = carry
        al, bt = al_s[j], bt_s[j]
        U_s[j] = bt * dC
        un_s[j] = bt * dn
        dal = jnp.sum(jnp.sum(dC * Cin_s[j], axis=1, keepdims=True), axis=0, keepdims=True) \
            + jnp.sum(dn * nin_s[j], axis=1, keepdims=True)
        dbe_s[j] = dal * al
        return dCp_s[j] + al * dC, dnp_s[j] + al * dn

    lax.fori_loop(0, N_CHUNKS, step, (jnp.zeros((128, 128), F32), jnp.zeros((1, 128), F32)))


def _state_scratch():
    c = N_CHUNKS
    return [pltpu.VMEM((c, 128, 128), F32), pltpu.VMEM((c, 1, 128), F32), pltpu.VMEM((c, 1, 1), F32),
            pltpu.VMEM((c, 1, 1), F32),
            pltpu.VMEM((c, 128, 128), F32), pltpu.VMEM((c, 1, 128), F32), pltpu.VMEM((c, 1, 1), F32),
            pltpu.VMEM((c, 1, 1), F32), pltpu.VMEM((c, 1, 1), F32)]


def _shift_down(x, s):
    if s == 0:
        return x
    rows = lax.broadcasted_iota(jnp.int32, x.shape, 0)
    return jnp.where(rows >= s, pltpu.roll(x, s, 0), 0.0)


def _shift_up(x, s):
    if s == 0:
        return x
    S = x.shape[0]
    rows = lax.broadcasted_iota(jnp.int32, x.shape, 0)
    return jnp.where(rows < S - s, pltpu.roll(x, S - s, 0), 0.0)


def _conv_pre(x, cw, cb):
    y = cb + cw[3:4, :] * x
    for j in range(3):
        y = y + cw[j:j + 1, :] * _shift_down(x, 3 - j)
    return y


def _conv_bwd(x, cw, dpre):
    dx = cw[3:4, :] * dpre
    dcw = [None] * 4
    dcw[3] = jnp.sum(dpre * x, axis=0, keepdims=True)
    for j in range(3):
        dx = dx + cw[j:j + 1, :] * _shift_up(dpre, 3 - j)
        dcw[j] = jnp.sum(dpre * _shift_down(x, 3 - j), axis=0, keepdims=True)
    return dx, dcw, jnp.sum(dpre, axis=0, keepdims=True)


def _silu(z):
    return z * jax.nn.sigmoid(z)


def _dsilu(z):
    s = jax.nn.sigmoid(z)
    return s * (1.0 + z * (1.0 - s))


_ML_Q, _ML_K = slice(0, 128), slice(128, 256)
_ML_IF = slice(768, 896)


def _ml_v(e):
    return slice(256 + e * 128, 384 + e * 128)


def _ml_o(e):
    return slice(512 + e * 128, 640 + e * 128)


def _ml_specs():
    pair = lambda b, p: (b, 0, p)
    return [pl.BlockSpec((1, SEQ, ML_PAIR_COLS), pair),
            pl.BlockSpec((1, 4, 128), lambda b, p: (p, 0, 0)),
            pl.BlockSpec((1, 4, 128), lambda b, p: (4 + p, 0, 0)),
            pl.BlockSpec((1, 1, 128), lambda b, p: (p, 0, 0)),
            pl.BlockSpec((1, 1, 128), lambda b, p: (4 + p, 0, 0)),
            pl.BlockSpec((1, 1, 128), lambda b, p: (p, 0, 0)),
            pl.BlockSpec((1, 1, 256), lambda b, p: (p, 0, 0))]


def mlstm_fwd(pm, cw8, cb8, bifp, gn4, name):
    B = pm.shape[0]

    def body(p_ref, cwq, cwk, cbq, cbk, bif_ref, gn_ref, hg_ref, *st):
        U_s, un_s, be_s, mg_s, Cin_s, nin_s, min_s, al_s, bt_s = st
        qc = _silu(_conv_pre(p_ref[0, :, _ML_Q], cwq[0], cbq[0]))
        kc = _silu(_conv_pre(p_ref[0, :, _ML_K], cwk[0], cbk[0])) * (64 ** -0.5)
        ifb = _gate_block(p_ref[0, :, _ML_IF] + bif_ref[0])
        for e in range(2):
            lanes = slice(e * 128, (e + 1) * 128)
            H0, r0, U, un, b, b_end, mD, mg = _ml_intra(qc, kc, p_ref[0, :, _ml_v(e)], ifb, e=e)
            U_s[...], un_s[...], be_s[...], mg_s[...] = U, un, b_end, mg
            _state_sweep(*st)
            hg = _ml_inter(qc, p_ref[0, :, _ml_o(e)], gn_ref[0, :, lanes], H0, r0, b, Cin_s[...], nin_s[...],
                           mD=mD, m_in=min_s[...], e=e)
            hg_ref[0, :, lanes] = hg.astype(BF16)

    return pl.pallas_call(
        body, name=name, grid=(B, ML_PAIRS),
        in_specs=_ml_specs(),
        out_specs=pl.BlockSpec((1, SEQ, 256), lambda b, p: (b, 0, p)),
        out_shape=jax.ShapeDtypeStruct((B, SEQ, D_MODEL), BF16),
        scratch_shapes=_state_scratch(),
        compiler_params=_cparams(),
    )(pm, cw8, cw8, cb8, cb8, bifp, gn4)


def mlstm_bwd(pm, cw8, cb8, bifp, gn4, dhg, name):
    B = pm.shape[0]

    def body(p_ref, cwq, cwk, cbq, cbk, bif_ref, gn_ref, dh_ref,
             dp_ref, dcw_ref, dcb_ref, dbif_ref, dgn_ref, *scr):
        st = scr[:9]
        U_s, un_s, be_s, mg_s, Cin_s, nin_s, min_s, al_s, bt_s = st
        dCp_s, dnp_s, dbe_s = scr[9:]
        p_id = pl.program_id(1)

        @pl.when(jnp.logical_and(pl.program_id(0) == 0, p_id == 0))
        def _():
            dcw_ref[...] = jnp.zeros_like(dcw_ref)
            dcb_ref[...] = jnp.zeros_like(dcb_ref)
            dbif_ref[...] = jnp.zeros_like(dbif_ref)
            dgn_ref[...] = jnp.zeros_like(dgn_ref)

        pre_q = _conv_pre(p_ref[0, :, _ML_Q], cwq[0], cbq[0])
        pre_k = _conv_pre(p_ref[0, :, _ML_K], cwk[0], cbk[0])
        qc = _silu(pre_q)
        kc = _silu(pre_k) * (64 ** -0.5)
        ifb, gate_vjp = jax.vjp(_gate_block, p_ref[0, :, _ML_IF] + bif_ref[0])
        dq = jnp.zeros((SEQ, 128), F32)
        dk = jnp.zeros((SEQ, 128), F32)
        difb = jnp.zeros((SEQ, 128), F32)
        for e in range(2):
            lanes = slice(e * 128, (e + 1) * 128)
            (H0, r0, U, un, b, b_end, mD, mg), vjp1 = jax.vjp(functools.partial(_ml_intra, e=e), qc, kc,
                                                              p_ref[0, :, _ml_v(e)], ifb)
            U_s[...], un_s[...], be_s[...], mg_s[...] = U, un, b_end, mg
            _state_sweep(*st)
            _, vjp3 = jax.vjp(functools.partial(_ml_inter, mD=mD, m_in=min_s[...], e=e), qc, p_ref[0, :, _ml_o(e)],
                              gn_ref[0, :, lanes], H0, r0, b, Cin_s[...], nin_s[...])
            dq_a, dmo, dgn, dH0, dr0, db_a, dCp, dnp = vjp3(dh_ref[0, :, lanes])
            dCp_s[...], dnp_s[...] = dCp, dnp
            _state_sweep_bwd(U_s, un_s, dbe_s, Cin_s, nin_s, dCp_s, dnp_s, al_s, bt_s)
            dq_b, dk_b, dv, difb_e = vjp1((dH0, dr0, U_s[...], un_s[...], db_a, dbe_s[...],
                                           jnp.zeros_like(mD), jnp.zeros_like(mg)))
            dq, dk, difb = dq + dq_a + dq_b, dk + dk_b, difb + difb_e
            dp_ref[0, :, _ml_v(e)] = dv.astype(BF16)
            dp_ref[0, :, _ml_o(e)] = dmo.astype(BF16)
            dgn_ref[p_id, :, lanes] += dgn
        (difb,) = gate_vjp(difb)
        dp_ref[0, :, _ML_IF] = difb.astype(BF16)
        dbif_ref[p_id] += jnp.sum(difb, axis=0, keepdims=True)

        for (sl, cw, pre, d, blk, scale) in ((_ML_Q, cwq, pre_q, dq, p_id, 1.0), (_ML_K, cwk, pre_k, dk, 4 + p_id, 64 ** -0.5)):
            xr = p_ref[0, :, sl]
            dpre = d * scale * _dsilu(pre)
            dx, dcw, dcb = _conv_bwd(xr, cw[0], dpre)
            dp_ref[0, :, sl] = dx.astype(BF16)
            for j in range(4):
                dcw_ref[blk, j:j + 1, :] += dcw[j]
            dcb_ref[blk] += dcb

    full3 = lambda b, p: (0, 0, 0)
    return pl.pallas_call(
        body, name=name, grid=(B, ML_PAIRS),
        in_specs=[pl.BlockSpec((1, SEQ, ML_PAIR_COLS), lambda b, p: (b, 0, p), pipeline_mode=pl.Buffered(1))]
        + _ml_specs()[1:] + [pl.BlockSpec((1, SEQ, 256), lambda b, p: (b, 0, p), pipeline_mode=pl.Buffered(1))],
        out_specs=[pl.BlockSpec((1, SEQ, ML_PAIR_COLS), lambda b, p: (b, 0, p)),
                   pl.BlockSpec((8, 4, 128), full3), pl.BlockSpec((8, 1, 128), full3),
                   pl.BlockSpec((4, 1, 128), full3), pl.BlockSpec((4, 1, 256), full3)],
        out_shape=[jax.ShapeDtypeStruct((B, SEQ, ML_COLS), BF16),
                   jax.ShapeDtypeStruct((8, 4, 128), F32), jax.ShapeDtypeStruct((8, 1, 128), F32),
                   jax.ShapeDtypeStruct((4, 1, 128), F32), jax.ShapeDtypeStruct((4, 1, 256), F32)],
        scratch_shapes=_state_scratch() + [pltpu.VMEM((N_CHUNKS, 128, 128), F32), pltpu.VMEM((N_CHUNKS, 1, 128), F32),
                                           pltpu.VMEM((N_CHUNKS, 1, 1), F32)],
        compiler_params=_cparams(),
    )(pm, cw8, cw8, cb8, cb8, bifp, gn4, dhg)


def _adamw(w, g, m, v):
    m = ADAM_B1 * m + (1.0 - ADAM_B1) * g
    v = ADAM_B2 * v + (1.0 - ADAM_B2) * (g * g)
    m_hat = m / (1.0 - ADAM_B1 ** ADAM_STEP)
    v_hat = v / (1.0 - ADAM_B2 ** ADAM_STEP)
    delta = -ADAM_LR * (m_hat / (jnp.sqrt(v_hat) + ADAM_EPS) + ADAM_WD * w)
    return delta, m, v


def adamw(w, g, m, v, name, parts=False):
    R, C = w.shape
    if R % 8 == 0 or R * C * 4 <= (1 << 20):
        tr = _pick(R, (256, 128, 64, 32, 16, 8)) if R * C * 4 > (1 << 20) else R
        steps = R // tr
        spec = pl.BlockSpec((tr, C), lambda i: (i, 0))
        g_spec = pl.BlockSpec((N_DEV, tr, C), lambda i: (0, i, 0)) if parts else spec
    else:
        tc = _pick(C, (256, 128))
        steps = C // tc
        spec = pl.BlockSpec((R, tc), lambda i: (0, i))
        g_spec = pl.BlockSpec((N_DEV, R, tc), lambda i: (0, 0, i)) if parts else spec

    def body(w_ref, g_ref, m_ref, v_ref, go_ref, d_ref, mo_ref, vo_ref):
        if parts:
            g = g_ref[0].astype(F32)
            for k in range(1, N_DEV):
                g = g + g_ref[k].astype(F32)
        else:
            g = g_ref[...]
        d, mn, vn = _adamw(w_ref[...], g, m_ref[...], v_ref[...])
        go_ref[...], d_ref[...], mo_ref[...], vo_ref[...] = g, d, mn, vn

    return pl.pallas_call(
        body, name=name, grid=(steps,),
        in_specs=[spec, g_spec, spec, spec], out_specs=[spec] * 4,
        out_shape=[jax.ShapeDtypeStruct((R, C), F32)] * 4,
        compiler_params=_cparams(),
    )(w, g, m, v)


def adamw_many(ws, gs, ms, vs, name):
    n = len(ws)

    def fn(*a):
        out = []
        for j in range(n):
            out += list(_adamw(a[j], a[n + j], a[2 * n + j], a[3 * n + j]))
        return tuple(out)

    shapes = [jax.ShapeDtypeStruct(w.shape, F32) for w in ws for _ in range(3)]
    return small_call(fn, list(ws) + list(gs) + list(ms) + list(vs), shapes, name)


def _mesh_pos():
    return lax.axis_index("x"), lax.axis_index("y"), lax.axis_index("c")


def _flip(pos, f):
    x, y, c = pos
    return (1 - x if f & 4 else x, 1 - y if f & 2 else y, 1 - c if f & 1 else c)


def _index(pos):
    return 4 * pos[0] + 2 * pos[1] + pos[2]


def _exchange(arrs, name, scatter):
    n = len(arrs)
    scat = list(scatter) if isinstance(scatter, (list, tuple)) else [scatter] * n

    def body(*refs):
        ins, outs = refs[:n], refs[n:2 * n]
        send, recv, lsem = refs[2 * n:]
        me = _mesh_pos()
        mine = _index(me)
        copies = []
        for i in range(n):
            src = ins[i].at[mine] if scat[i] else ins[i]
            loc = pltpu.make_async_copy(src, outs[i].at[mine], lsem.at[i])
            loc.start()
            copies.append(loc)
            for f in range(1, N_DEV):
                peer = _flip(me, f)
                src = ins[i].at[_index(peer)] if scat[i] else ins[i]
                cp = pltpu.make_async_remote_copy(
                    src_ref=src, dst_ref=outs[i].at[mine],
                    send_sem=send.at[i * 7 + f - 1], recv_sem=recv.at[i * 7 + f - 1],
                    device_id=peer, device_id_type=pl.DeviceIdType.MESH)
                cp.start()
                copies.append(cp)
        for cp in copies:
            cp.wait()

    any_spec = pl.BlockSpec(memory_space=pl.ANY)
    out_shape = [jax.ShapeDtypeStruct(a.shape if s else (N_DEV,) + a.shape, a.dtype) for a, s in zip(arrs, scat)]
    res = pl.pallas_call(
        body, name=name,
        in_specs=[any_spec] * n, out_specs=[any_spec] * n, out_shape=out_shape,
        scratch_shapes=[pltpu.SemaphoreType.DMA((7 * n,)), pltpu.SemaphoreType.DMA((7 * n,)),
                        pltpu.SemaphoreType.DMA((n,))],
        compiler_params=_cparams(),
    )(*arrs)
    return list(res)


def all_gather(arrs, name):
    return _exchange(arrs, name, False)


def all_gather_two_level(arrs, name):
    n = len(arrs)

    def body(*refs):
        ins, outs = refs[:n], refs[n:2 * n]
        send, recv, lsem = refs[2 * n:]
        x, y, c = _mesh_pos()
        me, sibling = (x, y, c), (x, y, 1 - c)
        chips = [(1 - x, y), (x, 1 - y), (1 - x, 1 - y)]

        def copy(i, k, block, to, src=None):
            rows = outs[i].at[_index(block)]
            return pltpu.make_async_remote_copy(
                src_ref=rows if src is None else src, dst_ref=rows,
                send_sem=send.at[i * 7 + k], recv_sem=recv.at[i * 7 + k],
                device_id=to, device_id_type=pl.DeviceIdType.MESH)

        local = [pltpu.make_async_copy(ins[i], outs[i].at[_index(me)], lsem.at[i]) for i in range(n)]
        first = [copy(i, 0, me, sibling, src=ins[i]) for i in range(n)]
        first += [copy(i, 1 + j, me, (*chip, c), src=ins[i]) for i in range(n) for j, chip in enumerate(chips)]
        for cp in local + first:
            cp.start()
        passed = []
        for j, chip in enumerate(chips):
            for i in range(n):
                copy(i, 1 + j, (*chip, c), me).wait_recv()
                cp = copy(i, 4 + j, (*chip, c), sibling)
                cp.start()
                passed.append(cp)
        for i in range(n):
            copy(i, 0, sibling, me).wait_recv()
            for j, chip in enumerate(chips):
                copy(i, 4 + j, (*chip, 1 - c), me).wait_recv()
        for cp in first + passed:
            cp.wait_send()
        for cp in local:
            cp.wait()

    any_spec = pl.BlockSpec(memory_space=pl.ANY)
    res = pl.pallas_call(
        body, name=name,
        in_specs=[any_spec] * n, out_specs=[any_spec] * n,
        out_shape=[jax.ShapeDtypeStruct((N_DEV,) + a.shape, a.dtype) for a in arrs],
        scratch_shapes=[pltpu.SemaphoreType.DMA((7 * n,)), pltpu.SemaphoreType.DMA((7 * n,)),
                        pltpu.SemaphoreType.DMA((n,))],
        compiler_params=_cparams(),
    )(*arrs)
    return list(res)


def all_to_all(arrs, name):
    return _exchange(arrs, name, True)


_HBM = pl.BlockSpec(memory_space=pltpu.HBM)
_SEM = pl.BlockSpec(memory_space=pltpu.SEMAPHORE)
_EFFECT = pltpu.SideEffectType.DATAFLOW_SIDE_EFFECTING


def _split_copies(ins, lands, send, recv, scatter, waiting):
    me = _mesh_pos()
    mine = _index(me)
    copies = []
    for i in range(len(ins)):
        for f in range(1, N_DEV):
            peer = _flip(me, f)
            src = ins[i].at[_index(peer)] if scatter else ins[i]
            copies.append(pltpu.make_async_remote_copy(
                src_ref=src, dst_ref=lands[i].at[_index(peer) if waiting else mine],
                send_sem=send.at[i * 7 + f - 1], recv_sem=recv.at[i * 7 + f - 1],
                device_id=peer, device_id_type=pl.DeviceIdType.MESH))
    return copies


def exchange_start(arrs, name, scatter, after=()):
    n = len(arrs)
    land_shapes = [a.shape if scatter else (N_DEV,) + a.shape for a in arrs]

    def body(*refs):
        ins, lands = refs[:n], refs[n:2 * n]
        send, recv = refs[2 * n + len(after)], refs[2 * n + len(after) + 1]
        token = refs[-1]
        for cp in _split_copies(ins, lands, send, recv, scatter, False):
            cp.start()
        token[...] = jnp.zeros_like(token)

    res = pl.pallas_call(
        body, name=name,
        out_shape=(pltpu.SemaphoreType.DMA((7 * n,)), pltpu.SemaphoreType.DMA((7 * n,)),
                   *[pltpu.HBM(a.shape, a.dtype) for a in arrs],
                   *[pltpu.HBM(s, a.dtype) for s, a in zip(land_shapes, arrs)],
                   jax.ShapeDtypeStruct((8, 128), F32)),
        in_specs=[_HBM] * (2 * n) + [pl.BlockSpec(memory_space=pl.ANY)] * len(after),
        out_specs=(_SEM, _SEM, *[_HBM] * (2 * n), pl.BlockSpec(memory_space=pltpu.VMEM)),
        input_output_aliases={i: 2 + i for i in range(2 * n)},
        compiler_params=pltpu.CompilerParams(has_side_effects=_EFFECT),
    )(*[pltpu.with_memory_space_constraint(a, pltpu.HBM) for a in arrs],
      *[pltpu.with_memory_space_constraint(lax.empty(s, a.dtype), pltpu.HBM) for s, a in zip(land_shapes, arrs)],
      *after)
    return (res[0], res[1], list(res[2:2 + n]), list(res[2 + n:2 + 2 * n])), res[-1]


def exchange_wait(handle, after, name, scatter):
    send, recv, srcs, lands = handle
    n = len(srcs)
    after = list(after) if isinstance(after, (list, tuple)) else [after]

    def body(*refs):
        ins, lnd = refs[:n], refs[n:2 * n]
        send_, recv_ = refs[2 * n], refs[2 * n + 1]
        for cp in _split_copies(ins, lnd, send_, recv_, scatter, True):
            cp.wait_send()
            cp.wait_recv()

    res = pl.pallas_call(
        body, name=name,
        out_shape=(*[pltpu.HBM(a.shape, a.dtype) for a in srcs], *[pltpu.HBM(a.shape, a.dtype) for a in lands]),
        in_specs=[_HBM] * (2 * n) + [_SEM, _SEM] + [pl.BlockSpec(memory_space=pl.ANY)] * len(after),
        out_specs=tuple([_HBM] * (2 * n)),
        input_output_aliases={i: i for i in range(2 * n)},
        compiler_params=pltpu.CompilerParams(has_side_effects=_EFFECT),
    )(*srcs, *lands, send, recv, *after)
    return list(res[n:])


def _own_slot(land, own):
    return lax.dynamic_update_slice(land, own[None], (_index(_mesh_pos()),) + (0,) * own.ndim)


def cast_bf16(arrs, name):
    outs = []
    for i, a in enumerate(arrs):
        R, C = a.shape
        if R % 8 == 0:
            tr = _pick(R, (256, 128, 64, 32, 16, 8)) if R * C * 4 > (1 << 21) else R
            steps, spec = R // tr, pl.BlockSpec((tr, C), lambda i: (i, 0))
        else:
            steps, spec = C // 256, pl.BlockSpec((R, 256), lambda i: (0, i))

        def body(a_ref, o_ref):
            o_ref[...] = a_ref[...].astype(BF16)

        outs.append(pl.pallas_call(body, name=f"{name}_{i}", grid=(steps,), in_specs=[spec], out_specs=spec,
                                   out_shape=jax.ShapeDtypeStruct((R, C), BF16), compiler_params=_cparams())(a))
    return outs


def sum_slabs(parts, name):
    def fn(*a):
        outs = []
        for p in a:
            s = p[0].astype(F32)
            for k in range(1, N_DEV):
                s = s + p[k].astype(F32)
            outs.append(s)
        return tuple(outs)
    return list(small_call(fn, list(parts), [jax.ShapeDtypeStruct(p.shape[1:], F32) for p in parts], name))


def sum_parts(parts, name):
    def fn(p):
        g = p[0]
        for k in range(1, N_DEV):
            g = g + p[k]
        return (g,)
    return small_call(fn, [parts], [jax.ShapeDtypeStruct(parts.shape[1:], F32)], name)[0]


_SPLITS = np.cumsum([1536, 1536, 1536, 512, 512, 1024, 1024, 8, 8, 2048])[:-1].tolist()


def split_w_in(w):
    aq, ak, av, mq, mk, mv, mo, mi, mf, gates = jnp.split(w, _SPLITS, axis=1)
    R = w.shape[0]
    w_att = jnp.stack([aq.reshape(R, 12, 128), ak.reshape(R, 12, 128), av.reshape(R, 12, 128)], axis=2)
    gif = jnp.concatenate([mi.reshape(R, 4, 2), mf.reshape(R, 4, 2), jnp.zeros((R, 4, 124), w.dtype)], axis=2)
    w_ml = jnp.concatenate([mq.reshape(R, 4, 128), mk.reshape(R, 4, 128), mv.reshape(R, 4, 256),
                            mo.reshape(R, 4, 256), gif], axis=2)
    return w_att.reshape(R, ATT_COLS), w_ml.reshape(R, ML_COLS), gates


def merge_w_in(g_att, g_ml, g_gate):
    R = g_att.shape[0]
    a = g_att.reshape(R, 12, 3, 128)
    m = g_ml.reshape(R, 4, ML_PAIR_COLS)
    gif = m[:, :, 768:772]
    return jnp.concatenate([
        a[:, :, 0].reshape(R, 1536), a[:, :, 1].reshape(R, 1536), a[:, :, 2].reshape(R, 1536),
        m[:, :, 0:128].reshape(R, 512), m[:, :, 128:256].reshape(R, 512),
        m[:, :, 256:512].reshape(R, 1024), m[:, :, 512:768].reshape(R, 1024),
        gif[:, :, 0:2].reshape(R, 8), gif[:, :, 2:4].reshape(R, 8), g_gate], axis=1)


def _blk8(v, width=128):
    r = v.shape[0]
    return v.reshape(r, 1024 // width, width).transpose(1, 0, 2)


def _unblk8(v):
    nb, r, w = v.shape
    return v.transpose(1, 0, 2).reshape(r, nb * w)


def local_step(x, target, mods, w, small, late_w=None, early_g=None, w_in_g=None):
    late_w = late_w or (lambda after: w)
    big = {}
    early_g = early_g or (lambda g: big.update(g))
    w_in_g = w_in_g or (lambda g: big.update(w_in=merge_w_in(*g)))
    B = x.shape[0]
    T = B * SEQ
    shift1, scale1, gate1, shift2, scale2, gate2 = mods
    f2 = lambda a: a.reshape(T, a.shape[-1])
    f3 = lambda a: a.reshape(B, SEQ, a.shape[-1])

    rel_t = jnp.pad(small["rel_bias"].T, ((0, 4), (0, 0)))
    onehots = [_bucket_onehot(d) for _, d in ATT_GROUPS]
    biases = [bias_expand(rel_t, oh, f"bias_expand{g}").reshape(16, ATT_BLOCK, 2 * ATT_BLOCK)
              for g, oh in enumerate(onehots)]
    qg, kg = small["q_norm_g"], small["k_norm_g"]
    cw8 = _blk8(small["conv_w"])
    cb8 = _blk8(small["conv_b"])
    b_if = small["b_if"].reshape(2, 4, 2)
    bifp = jnp.concatenate([b_if[0], b_if[1], jnp.zeros((4, 124), F32)], axis=1).reshape(4, 1, 128)
    gn4 = small["mlstm_norm_g"].reshape(4, 1, 256)

    u = modnorm_fwd(x, small["norm1_g"], scale1, shift1, "modnorm1")
    u2d = f2(u)
    pa = f3(matmul(u2d, w["w_att"], mode="nn", name="proj_att"))
    pm = f3(matmul(u2d, w["w_ml"], mode="nn", name="proj_ml"))
    pg = matmul(u2d, w["w_gate"], mode="nn", name="proj_gate")
    os_, ls_ = [], []
    one_block = [SEQ // d == ATT_BLOCK for _, d in ATT_GROUPS]
    for g in range(3):
        o, l = (attn_fwd_classes if one_block[g] else attn_fwd)(pa, biases[g], qg, kg, g, f"attn_fwd{g}")
        os_.append(f2(o))
        ls_.append(f2(l))
    hg = mlstm_fwd(pm, cw8, cb8, bifp, gn4, "mlstm_fwd")
    w = {**w, **late_w(hg)}
    att, y_att = merge_att_out(os_, ls_, w["w_att_out"], "att_out")
    y_ml, z = ml_out_gate(f2(hg), w["w_ml_out"], pg, y_att, "ml_out")
    y, x1, u2 = out_proj_resid_modnorm(f2(z), w["w_out"], f2(x), gate1, small["norm2_g"], scale2, shift2, "out_proj")
    pre, hdn = matmul(u2, w["w_ff1"], mode="nn", name="ff1", out_dtypes=(BF16, BF16),
                      epi=lambda acc: (acc, jnp.square(jnp.maximum(acc, 0.0))))
    dx2, d_ffo, loss, d_gate2 = ff2_loss(hdn, w["w_ff2"], x1, gate2, f2(target), "ff2_loss")

    g_ff2 = matmul(hdn, d_ffo, mode="tn", name="g_ff2", out_dtypes=(BF16,))
    d_pre = matmul(d_ffo, w["w_ff2"], mode="nt", name="d_hdn", out_dtypes=(BF16,), extras=(pre,),
                   epi=lambda acc, p: (acc * (2.0 * jnp.maximum(p.astype(F32), 0.0)),))
    g_ff1 = matmul(u2, d_pre, mode="tn", name="g_ff1", out_dtypes=(BF16,))
    dx1, d_norm2, d_scale2, d_shift2, dy, d_gate1 = d_u_modnorm_bwd(
        d_pre, w["w_ff1"], None, x1, small["norm2_g"], scale2, shift2, dx2, y, gate1, "d_u2")
    g_out = matmul(f2(z), dy, mode="tn", name="g_out", out_dtypes=(BF16,))
    dpg, d_ya, d_ym = d_z_gate_bwd(dy, w["w_out"], pg, y_att, y_ml, "d_z")
    g_att_out = matmul(f2(att), f2(d_ya), mode="tn", name="g_att_out", out_dtypes=(BF16,))
    dmerge = d_att_merge_bwd(d_ya, w["w_att_out"], os_, ls_, "d_att")
    g_ml_out = matmul(f2(hg), f2(d_ym), mode="tn", name="g_ml_out", out_dtypes=(BF16,))
    d_hg = matmul(f2(d_ym), w["w_ml_out"], mode="nt", name="d_hg")
    started = early_g(dict(w_att_out=g_att_out, w_ml_out=g_ml_out, w_out=g_out, w_ff1=g_ff1, w_ff2=g_ff2))
    order = 0.0 if started is None else started[0, 0]
    dmerge = [f3(d) for d in dmerge]
    dpa = lax.empty((B, SEQ, ATT_COLS), BF16)
    d_rel = []
    d_qg = d_kg = None
    for g in range(3):
        dpa, dbias, dq_g, dk_g = (attn_bwd_classes if one_block[g] else attn_bwd)(
            pa, biases[g], qg + order, kg, dmerge[g], dmerge[3 + g], dpa, g, f"attn_bwd{g}")
        db8 = jnp.pad(dbias.reshape(4, -1), ((0, 4), (0, 0)))
        d_rel.append(bias_reduce(db8, onehots[g], f"bias_reduce{g}")[:4])
        d_qg = dq_g if d_qg is None else d_qg + dq_g
        d_kg = dk_g if d_kg is None else d_kg + dk_g
    dpm, dcw8, dcb8, dbifp, dgn4 = mlstm_bwd(pm, cw8, cb8, bifp, gn4 + order, f3(d_hg), "mlstm_bwd")
    g_w_att = matmul(u2d, f2(dpa), mode="tn", name="g_w_att", out_dtypes=(BF16,))
    g_w_ml = matmul(u2d, f2(dpm), mode="tn", name="g_w_ml", out_dtypes=(BF16,))
    g_w_gate = matmul(u2d, f2(dpg), mode="tn", name="g_w_gate", out_dtypes=(BF16,))
    started = w_in_g((g_w_att, g_w_ml, g_w_gate))
    du = matmul(f2(dpa), w["w_att"], mode="nt", name="d_u_att", after=() if started is None else (started,))
    du = matmul(f2(dpm), w["w_ml"], mode="nt", name="d_u_ml", extras=(du,), epi=lambda acc, e: (acc + e,))
    grad_x, d_norm1, d_scale1, d_shift1 = d_u_modnorm_bwd(
        f2(dpg), w["w_gate"], du, f2(x), small["norm1_g"], scale1, shift1, dx1, None, None, "d_u_gate")
    grad_x = f3(grad_x)

    d_mods = (d_shift1, d_scale1, d_gate1, d_shift2, d_scale2, d_gate2)
    dbif = dbifp.reshape(4, 128)
    small_g = dict(
        norm1_g=d_norm1, norm2_g=d_norm2,
        b_if=jnp.stack([dbif[:, 0:2].reshape(8), dbif[:, 2:4].reshape(8)]),
        conv_w=_unblk8(dcw8), conv_b=_unblk8(dcb8), q_norm_g=d_qg, k_norm_g=d_kg,
        rel_bias=jnp.concatenate(d_rel, axis=0).T,
        mlstm_norm_g=dgn4.reshape(1, 1024))
    return loss, grad_x, d_mods, big, small_g


_SMALL = (("b_ada", 6144), ("norm1_g", 1024), ("norm2_g", 1024), ("b_if", 16), ("conv_b", 1024),
          ("q_norm_g", 128), ("k_norm_g", 128), ("rel_bias", 384), ("mlstm_norm_g", 1024), ("conv_w", 4096))
_SMALL_ROWS = 120
_REPL = _SMALL[:-1]
_SMALL_SENT = _SMALL + (("loss", 1),)


def _pack(d, names, rows):
    flat = jnp.concatenate([d[k].reshape(-1) for k, _ in names])
    return jnp.pad(flat, (0, rows * 128 - flat.shape[0])).reshape(rows, 128)


def _unpack(slab, names, shapes):
    flat = slab.reshape(-1)
    out, off = {}, 0
    for k, nel in names:
        out[k] = flat[off:off + nel].reshape(shapes[k])
        off += nel
    return out


def kernel(x, c, w_ada, b_ada, norm1_g, norm2_g, w_in, b_if, conv_w, conv_b, q_norm_g, k_norm_g, rel_bias, mlstm_norm_g, w_att_out, w_ml_out, w_out, w_ff1, w_ff2, loss_target, m_w_ada, m_b_ada, m_norm1_g, m_norm2_g, m_w_in, m_b_if, m_conv_w, m_conv_b, m_q_norm_g, m_k_norm_g, m_rel_bias, m_mlstm_norm_g, m_w_att_out, m_w_ml_out, m_w_out, m_w_ff1, m_w_ff2, v_w_ada, v_b_ada, v_norm1_g, v_norm2_g, v_w_in, v_b_if, v_conv_w, v_conv_b, v_q_norm_g, v_k_norm_g, v_rel_bias, v_mlstm_norm_g, v_w_att_out, v_w_ml_out, v_w_out, v_w_ff1, v_w_ff2):
    P = dict(w_ada=w_ada, b_ada=b_ada, norm1_g=norm1_g, norm2_g=norm2_g, w_in=w_in, b_if=b_if, conv_w=conv_w,
             conv_b=conv_b, q_norm_g=q_norm_g, k_norm_g=k_norm_g, rel_bias=rel_bias, mlstm_norm_g=mlstm_norm_g,
             w_att_out=w_att_out, w_ml_out=w_ml_out, w_out=w_out, w_ff1=w_ff1, w_ff2=w_ff2)
    M = dict(w_ada=m_w_ada, b_ada=m_b_ada, norm1_g=m_norm1_g, norm2_g=m_norm2_g, w_in=m_w_in, b_if=m_b_if,
             conv_w=m_conv_w, conv_b=m_conv_b, q_norm_g=m_q_norm_g, k_norm_g=m_k_norm_g, rel_bias=m_rel_bias,
             mlstm_norm_g=m_mlstm_norm_g, w_att_out=m_w_att_out, w_ml_out=m_w_ml_out, w_out=m_w_out,
             w_ff1=m_w_ff1, w_ff2=m_w_ff2)
    V = dict(w_ada=v_w_ada, b_ada=v_b_ada, norm1_g=v_norm1_g, norm2_g=v_norm2_g, w_in=v_w_in, b_if=v_b_if,
             conv_w=v_conv_w, conv_b=v_conv_b, q_norm_g=v_q_norm_g, k_norm_g=v_k_norm_g, rel_bias=v_rel_bias,
             mlstm_norm_g=v_mlstm_norm_g, w_att_out=v_w_att_out, w_ml_out=v_w_ml_out, w_out=v_w_out,
             w_ff1=v_w_ff1, w_ff2=v_w_ff2)
    names = list(P)
    shapes = {k: P[k].shape for k in names}
    B = x.shape[0]
    me = _index(_mesh_pos())

    big_names = ("w_in", "w_att_out", "w_ml_out", "w_out", "w_ff1", "w_ff2")
    shards = cast_bf16([P[k][0] for k in big_names], "cast_w")
    (rows8,) = all_to_all([shards[0].reshape(N_DEV, D_MODEL // N_DEV, W_IN_SHARD)], "w_in_rows_exchange")
    slab = split_w_in(rows8.transpose(1, 0, 2).reshape(D_MODEL // N_DEV, D_IN))
    a8, m8, g8, c8, conv_w8 = all_gather_two_level(list(slab) + [c, conv_w[0]], "gather_w_in")
    c_all = c8.reshape(N_DEV * B, D_MODEL)
    conv_w_full = conv_w8.transpose(1, 0, 2).reshape(4, 1024)
    w = dict(w_att=a8.reshape(D_MODEL, ATT_COLS), w_ml=m8.reshape(D_MODEL, ML_COLS),
             w_gate=g8.reshape(D_MODEL, GATE_COLS))

    (silu_c,) = small_call(lambda a: (_silu(a),), [c_all], [jax.ShapeDtypeStruct(c_all.shape, F32)], "silu_c")
    b_ada_cols = lax.dynamic_slice(b_ada, (0, me * 768), (1, 768))
    ada_cols = matmul(silu_c, w_ada[0], mode="nn", name="ada", extras=(jnp.broadcast_to(b_ada_cols, (N_DEV * B, 768)),),
                      epi=lambda acc, bb: (acc + bb,))
    (ada_t,) = all_to_all([ada_cols.reshape(N_DEV, B, 768)], "ada_exchange")
    ada = ada_t.transpose(1, 0, 2).reshape(B, 6 * D_MODEL)
    mods = tuple(ada[:, i * D_MODEL:(i + 1) * D_MODEL].reshape(B, 1, D_MODEL) for i in range(6))

    late_handle, late_order = exchange_start(shards[1:], "gather_late_start", False, after=(ada_t,))

    def late_w(after):
        lands = exchange_wait(late_handle, after, "gather_late_wait", False)
        gw = dict(zip(big_names[1:], [_own_slot(l, s) for l, s in zip(lands, shards[1:])]))
        return dict(w_att_out=gw["w_att_out"].transpose(1, 0, 2).reshape(512, D_MODEL),
                    w_ml_out=gw["w_ml_out"].reshape(D_MODEL, D_MODEL), w_out=gw["w_out"].reshape(D_MODEL, D_MODEL),
                    w_ff1=gw["w_ff1"].transpose(1, 0, 2).reshape(D_MODEL, D_FF),
                    w_ff2=gw["w_ff2"].reshape(D_FF, D_MODEL))

    pending = {}

    def send_grads(key, blocks, name):
        handle, order = exchange_start(blocks, name, True)
        pending[key] = (handle, [lax.dynamic_index_in_dim(b, me, 0, keepdims=False) for b in blocks])
        return order

    def early_g(g):
        return send_grads("late", [g["w_att_out"].reshape(512, N_DEV, 128).transpose(1, 0, 2),
                                   g["w_ml_out"].reshape(N_DEV, 128, D_MODEL), g["w_out"].reshape(N_DEV, 128, D_MODEL),
                                   g["w_ff1"].reshape(D_MODEL, N_DEV, 512).transpose(1, 0, 2),
                                   g["w_ff2"].reshape(N_DEV, 512, D_MODEL)], "grad_late_start")

    def w_in_g(parts):
        return send_grads("w_in", [g.reshape(N_DEV, D_MODEL // N_DEV, g.shape[1]) for g in parts], "grad_w_in_start")

    def recv_grads(key, after, name):
        handle, own = pending[key]
        return [_own_slot(l, o) for l, o in zip(exchange_wait(handle, after, name, True), own)]

    small = dict(norm1_g=norm1_g + late_order[0, 0], norm2_g=norm2_g, b_if=b_if[0], conv_w=conv_w_full, conv_b=conv_b,
                 q_norm_g=q_norm_g, k_norm_g=k_norm_g, rel_bias=rel_bias, mlstm_norm_g=mlstm_norm_g)
    loss, grad_x, d_mods, _, small_g = local_step(x, loss_target, mods, w, small, late_w, early_g, w_in_g)

    d_ada = jnp.concatenate([d.reshape(B, D_MODEL) for d in d_mods], axis=1)
    (small_g["b_ada"],) = small_call(lambda a: (jnp.sum(a, axis=0, keepdims=True),), [d_ada],
                                     [jax.ShapeDtypeStruct((1, 6144), F32)], "g_b_ada_local")
    small_g["loss"] = loss
    slabs = sum_slabs(recv_grads("w_in", grad_x, "grad_w_in_wait"), "w_in_slab_sum")
    w_in_cols = merge_w_in(*slabs).reshape(D_MODEL // N_DEV, N_DEV, W_IN_SHARD).transpose(1, 0, 2)
    d_ada_t, small_parts = _exchange(
        [d_ada.reshape(B, N_DEV, 768).transpose(1, 0, 2), _pack(small_g, _SMALL_SENT, _SMALL_ROWS)],
        "small_exchange", [True, False])
    cols_handle, _ = exchange_start([w_in_cols], "grad_w_in_cols_start", True, after=(small_parts,))
    d_ada_cols = d_ada_t.reshape(N_DEV * B, 768)
    g_w_ada = matmul(silu_c, d_ada_cols, mode="tn", name="g_w_ada")

    recv = recv_grads("late", grad_x, "grad_late_wait")
    small_sum = sum_parts(small_parts, "small_grad_sum")
    sg = _unpack(small_sum, _SMALL_SENT, {**{k: shapes[k] for k, _ in _REPL}, "conv_w": (4, 1024), "loss": ()})
    loss = sg["loss"]

    G, Dl, NM, NV = {}, {}, {}, {}
    for k, parts in zip(big_names[1:], recv):
        g, d, nm, nv = adamw(P[k][0], parts, M[k][0], V[k][0], f"adamw_{k}", parts=True)
        G[k], Dl[k], NM[k], NV[k] = g[None], d[None], nm[None], nv[None]
    g, d, nm, nv = adamw(w_ada[0], g_w_ada, m_w_ada[0], v_w_ada[0], "adamw_w_ada")
    G["w_ada"], Dl["w_ada"], NM["w_ada"], NV["w_ada"] = g[None], d[None], nm[None], nv[None]
    g_conv = lax.dynamic_slice(sg["conv_w"], (0, me * 128), (4, 128))
    g, d, nm, nv = adamw(conv_w[0], g_conv, m_conv_w[0], v_conv_w[0], "adamw_conv_w")
    G["conv_w"], Dl["conv_w"], NM["conv_w"], NV["conv_w"] = g[None], d[None], nm[None], nv[None]
    flat2 = lambda a: a.reshape(-1, a.shape[-1])
    keys = [k for k, _ in _REPL]
    upd = adamw_many([flat2(P[k]) for k in keys], [flat2(sg[k]) for k in keys], [flat2(M[k]) for k in keys],
                     [flat2(V[k]) for k in keys], "adamw_small")
    for j, k in enumerate(keys):
        G[k] = sg[k]
        Dl[k], NM[k], NV[k] = [upd[3 * j + t].reshape(shapes[k]) for t in range(3)]

    others = [NV[k] for k in big_names[1:]] + [NV["w_ada"], NV["conv_w"], upd[-1]]
    (g_w_in,) = exchange_wait(cols_handle, others, "grad_w_in_cols_wait", True)
    g_w_in = _own_slot(g_w_in, lax.dynamic_index_in_dim(w_in_cols, me, 0, keepdims=False))
    g, d, nm, nv = adamw(w_in[0], g_w_in.reshape(D_MODEL, W_IN_SHARD), m_w_in[0], v_w_in[0], "adamw_w_in")
    G["w_in"], Dl["w_in"], NM["w_in"], NV["w_in"] = g[None], d[None], nm[None], nv[None]

    return (loss, grad_x, *[G[k] for k in names], *[Dl[k] for k in names], *[NM[k] for k in names],
            *[NV[k] for k in names])
```

```python
import functools
import math

import numpy as np
import jax
import jax.numpy as jnp
from jax import lax
from jax.experimental import pallas as pl
from jax.experimental.pallas import tpu as pltpu

F32 = jnp.float32
BF16 = jnp.bfloat16

N_DEV = 8
D_MODEL = 1024
SEQ = 2048
ATT_GROUPS = ((128, 1), (512, 4), (2048, 16))
N_ATT_HEADS = 12
ATT_BLOCK = 128
HEAD_DIM = 128
ML_HEADS = 8
ML_PAIRS = 4
ML_CHUNK = 64
N_CHUNKS = SEQ // ML_CHUNK
N_BUCKETS = 32
MAX_DISTANCE = 2048
D_FF = 4096
D_IN = 9744
EPS = 1e-6

ADAM_LR = 0.001
ADAM_B1 = 0.9
ADAM_B2 = 0.999
ADAM_EPS = 1e-08
ADAM_WD = 0.01
ADAM_STEP = 10

ATT_HEAD_COLS = 3 * HEAD_DIM
ATT_COLS = N_ATT_HEADS * ATT_HEAD_COLS
ML_PAIR_COLS = 896
ML_COLS = ML_PAIRS * ML_PAIR_COLS
GATE_COLS = 2 * D_MODEL
W_IN_SHARD = D_IN // N_DEV

VMEM_LIMIT = 60 * 1024 * 1024


def _cparams(**kw):
    return pltpu.CompilerParams(vmem_limit_bytes=VMEM_LIMIT, **kw)


_NN = ((1,), (0,))
_NT = ((1,), (1,))
_TN = ((0,), (0,))


def _mxu(a, b, dims):
    return lax.dot_general(a.astype(BF16), b.astype(BF16), (dims, ((), ())), preferred_element_type=F32)


@jax.custom_vjp
def bdot_nn(a, b):
    return _mxu(a, b, _NN)


def _nn_fwd(a, b):
    return _mxu(a, b, _NN), (a, b)


def _nn_bwd(res, g):
    a, b = res
    return _mxu(g, b, _NT), _mxu(a, g, _TN)


bdot_nn.defvjp(_nn_fwd, _nn_bwd)


@jax.custom_vjp
def bdot_nt(a, b):
    return _mxu(a, b, _NT)


def _nt_fwd(a, b):
    return _mxu(a, b, _NT), (a, b)


def _nt_bwd(res, g):
    a, b = res
    return _mxu(g, b, _NN), _mxu(g, a, _TN)


bdot_nt.defvjp(_nt_fwd, _nt_bwd)


def _doth(a, b, dims=_NN):
    return lax.dot_general(a, b, (dims, ((), ())), precision=lax.Precision.HIGHEST, preferred_element_type=F32)


def _rms(x):
    return x * lax.rsqrt(jnp.mean(x * x, axis=-1, keepdims=True) + EPS)


def _pick(n, cands):
    for t in cands:
        if n % t == 0:
            return t
    raise ValueError(f"no tile for {n}")


MM_TILE_M = (1024, 512, 256, 128, 64, 32, 16, 8)
MM_TILE_N = (2048, 1792, 1536, 1024, 768, 512, 256, 128)
MM_TILE_K = (2048, 1792, 1536, 1024, 512, 256, 128, 64, 32)

def matmul(a, b, *, mode, name, out_dtypes=(F32,), epi=None, extras=(), after=()):
    if mode == "nn":
        (M, K), (K2, N) = a.shape, b.shape
    elif mode == "nt":
        (M, K), (N, K2) = a.shape, b.shape
    else:
        (K, M), (K2, N) = a.shape, b.shape
    assert K == K2, (a.shape, b.shape, mode)
    tm = _pick(M, MM_TILE_M)
    tn = _pick(N, MM_TILE_N)
    tk = _pick(K, MM_TILE_K)
    nk = K // tk
    n_ex = len(extras)
    n_out = len(out_dtypes)
    dims = {"nn": _NN, "nt": _NT, "tn": _TN}[mode]

    def finish(r, ex_refs, out_refs):
        outs = epi(r, *[e[...] for e in ex_refs]) if epi is not None else (r,)
        for o_ref, o in zip(out_refs, outs):
            o_ref[...] = o.astype(o_ref.dtype)

    def body(*refs):
        a_ref, b_ref = refs[0], refs[1]
        ex_refs = refs[2:2 + n_ex]
        out_refs = refs[2 + n_ex + len(after):2 + n_ex + len(after) + n_out]
        if nk == 1:
            finish(_mxu(a_ref[...], b_ref[...], dims), ex_refs, out_refs)
            return
        acc = refs[2 + n_ex + len(after) + n_out]
        k = pl.program_id(2)

        @pl.when(k == 0)
        def _():
            acc[...] = jnp.zeros_like(acc)

        acc[...] += _mxu(a_ref[...], b_ref[...], dims)

        @pl.when(k == nk - 1)
        def _():
            finish(acc[...], ex_refs, out_refs)

    if mode == "nn":
        a_spec = pl.BlockSpec((tm, tk), lambda i, j, k: (i, k))
        b_spec = pl.BlockSpec((tk, tn), lambda i, j, k: (k, j))
    elif mode == "nt":
        a_spec = pl.BlockSpec((tm, tk), lambda i, j, k: (i, k))
        b_spec = pl.BlockSpec((tn, tk), lambda i, j, k: (j, k))
    else:
        a_spec = pl.BlockSpec((tk, tm), lambda i, j, k: (k, i))
        b_spec = pl.BlockSpec((tk, tn), lambda i, j, k: (k, j))
    o_spec = pl.BlockSpec((tm, tn), lambda i, j, k: (i, j))
    res = pl.pallas_call(
        body,
        name=name,
        grid=(M // tm, N // tn, nk),
        in_specs=[a_spec, b_spec] + [o_spec] * n_ex + [pl.BlockSpec(memory_space=pl.ANY)] * len(after),
        out_specs=[o_spec] * n_out,
        out_shape=[jax.ShapeDtypeStruct((M, N), dt) for dt in out_dtypes],
        scratch_shapes=[pltpu.VMEM((tm, tn), F32)] if nk > 1 else [],
        compiler_params=_cparams(),
    )(a, b, *extras, *after)
    return res[0] if n_out == 1 else tuple(res)


ROW_MM_TILE = 512


def row_matmul(a, b, *, mode, name, extras, outs, epi):
    (M, K) = a.shape
    N = b.shape[1] if mode == "nn" else b.shape[0]
    tm = ROW_MM_TILE
    tk = _pick(K, MM_TILE_K)
    nk = K // tk
    n_ex, n_out = len(extras), len(outs)

    def body(*refs):
        a_ref, b_ref = refs[0], refs[1]
        ex_refs = refs[2:2 + n_ex]
        out_refs = refs[2 + n_ex:2 + n_ex + n_out]
        i = pl.program_id(0)
        dims = _NN if mode == "nn" else _NT
        if nk == 1:
            epi(_mxu(a_ref[...], b_ref[...], dims), i, ex_refs, out_refs)
            return
        acc = refs[2 + n_ex + n_out]
        k = pl.program_id(1)

        @pl.when(k == 0)
        def _():
            acc[...] = jnp.zeros_like(acc)

        acc[...] += _mxu(a_ref[...], b_ref[...], dims)

        @pl.when(k == nk - 1)
        def _():
            epi(acc[...], i, ex_refs, out_refs)

    def lift(index_map):
        return lambda i, k: index_map(i)

    b_spec = pl.BlockSpec((tk, N), lambda i, k: (k, 0)) if mode == "nn" else pl.BlockSpec((N, tk), lambda i, k: (0, k))
    res = pl.pallas_call(
        body, name=name, grid=(M // tm, nk),
        in_specs=[pl.BlockSpec((tm, tk), lambda i, k: (i, k)), b_spec]
        + [pl.BlockSpec(blk, lift(im)) for _, blk, im in extras],
        out_specs=[pl.BlockSpec(blk, lift(im)) for _, _, blk, im in outs],
        out_shape=[jax.ShapeDtypeStruct(shape, dt) for shape, dt, _, _ in outs],
        scratch_shapes=[pltpu.VMEM((tm, N), F32)] if nk > 1 else [],
        compiler_params=_cparams(),
    )(a, b, *[e[0] for e in extras])
    return tuple(res)


def _rows(arr):
    return (arr, (ROW_MM_TILE, arr.shape[1]), lambda i: (i, 0))


def _rows_out(T, dtype):
    return ((T, D_MODEL), dtype, (ROW_MM_TILE, D_MODEL), lambda i: (i, 0))


def _per_seq(arr):
    return (arr, (1, 1, D_MODEL), lambda i: (i // (SEQ // ROW_MM_TILE), 0, 0))


def _per_seq_out(B):
    return ((B, 1, D_MODEL), F32, (1, 1, D_MODEL), lambda i: (i // (SEQ // ROW_MM_TILE), 0, 0))


def _first_tile_of_seq(i):
    return i % (SEQ // ROW_MM_TILE) == 0


def small_call(fn, inputs, out_shapes, name):
    n_in = len(inputs)

    def body(*refs):
        outs = fn(*[r[...] for r in refs[:n_in]])
        for o_ref, o in zip(refs[n_in:], outs):
            o_ref[...] = o.astype(o_ref.dtype)

    res = pl.pallas_call(body, name=name, out_shape=list(out_shapes), compiler_params=_cparams())(*inputs)
    return tuple(res)


ROW_TILE = 512


def _modnorm(x, g, scale, shift):
    return _rms(x) * g * (1.0 + scale) + shift


def _row_spec(width):
    return pl.BlockSpec((1, ROW_TILE, width), lambda b, i: (b, i, 0))


def _mod_spec():
    return pl.BlockSpec((1, 1, D_MODEL), lambda b, i: (b, 0, 0))


def _vec_spec():
    return pl.BlockSpec((1, D_MODEL), lambda b, i: (0, 0))


def modnorm_fwd(x, g, scale, shift, name):
    B, S, D = x.shape

    def body(x_ref, g_ref, sc_ref, sh_ref, u_ref):
        u_ref[0] = _modnorm(x_ref[0], g_ref[...], sc_ref[0], sh_ref[0]).astype(BF16)

    return pl.pallas_call(
        body, name=name, grid=(B, S // ROW_TILE),
        in_specs=[_row_spec(D), _vec_spec(), _mod_spec(), _mod_spec()],
        out_specs=_row_spec(D),
        out_shape=jax.ShapeDtypeStruct((B, S, D), BF16),
        compiler_params=_cparams(),
    )(x, g, scale, shift)


def _gain_spec(g):
    return (g, (1, D_MODEL), lambda i: (0, 0))


def out_proj_resid_modnorm(z, w_out, x, gate, g, scale, shift, name):
    T = z.shape[0]

    def epi(acc, i, ex, out):
        x_ref, gt_ref, g_ref, sc_ref, sh_ref = ex
        y_ref, x1_ref, u_ref = out
        y_ref[...] = acc
        x1 = x_ref[...] + gt_ref[0] * acc
        x1_ref[...] = x1
        u_ref[...] = _modnorm(x1, g_ref[...], sc_ref[0], sh_ref[0]).astype(BF16)

    return row_matmul(z, w_out, mode="nn", name=name,
                      extras=[_rows(x), _per_seq(gate), _gain_spec(g), _per_seq(scale), _per_seq(shift)],
                      outs=[_rows_out(T, F32), _rows_out(T, F32), _rows_out(T, BF16)], epi=epi)


def ff2_loss(hdn, w_ff2, x1, gate, target, name):
    T = hdn.shape[0]

    def epi(acc, i, ex, out):
        x_ref, gt_ref, t_ref = ex
        dx_ref, dffo_ref, loss_ref, dg_ref = out

        @pl.when(i == 0)
        def _():
            loss_ref[...] = jnp.zeros_like(loss_ref)

        @pl.when(_first_tile_of_seq(i))
        def _():
            dg_ref[...] = jnp.zeros_like(dg_ref)

        err = x_ref[...] + gt_ref[0] * acc - t_ref[...]
        dx = err * (1.0 / D_MODEL)
        dx_ref[...] = dx
        dffo_ref[...] = (gt_ref[0] * dx).astype(BF16)
        loss_ref[...] += 0.5 * jnp.sum(jnp.mean(err * err, axis=-1, keepdims=True), axis=0, keepdims=True)
        dg_ref[0] += jnp.sum(dx * acc, axis=0, keepdims=True)

    return row_matmul(hdn, w_ff2, mode="nn", name=name,
                      extras=[_rows(x1), _per_seq(gate), _rows(target)],
                      outs=[_rows_out(T, F32), _rows_out(T, BF16), ((1, 1), F32, (1, 1), lambda i: (0, 0)),
                            _per_seq_out(T // SEQ)], epi=epi)


def d_u_modnorm_bwd(a, w, du_prev, x, g, scale, shift, dx_res, y, gate, name):
    T = a.shape[0]
    B = T // SEQ
    n_prev, resid = int(du_prev is not None), y is not None

    def epi(acc, i, ex, out):
        x_ref, g_ref, sc_ref, sh_ref, dr_ref = ex[n_prev:n_prev + 5]
        dx_ref, dg_ref, dsc_ref, dsh_ref = out[:4]

        @pl.when(i == 0)
        def _():
            dg_ref[...] = jnp.zeros_like(dg_ref)

        @pl.when(_first_tile_of_seq(i))
        def _():
            for r in out[2:4] + out[5:]:
                r[...] = jnp.zeros_like(r)

        du = acc + ex[0][...] if n_prev else acc
        _, vjp = jax.vjp(_modnorm, x_ref[...], g_ref[...], sc_ref[0], sh_ref[0])
        dxn, dg, dsc, dsh = vjp(du)
        dx = dxn + dr_ref[...]
        dx_ref[...] = dx
        dg_ref[...] += dg
        dsc_ref[0] += dsc
        dsh_ref[0] += dsh
        if resid:
            y_ref, gt_ref = ex[n_prev + 5:]
            out[4][...] = (gt_ref[0] * dx).astype(BF16)
            out[5][0] += jnp.sum(dx * y_ref[...], axis=0, keepdims=True)

    extras = ([_rows(du_prev)] if n_prev else []) + [_rows(x), _gain_spec(g), _per_seq(scale), _per_seq(shift),
                                                     _rows(dx_res)] + ([_rows(y), _per_seq(gate)] if resid else [])
    outs = [_rows_out(T, F32), ((1, D_MODEL), F32, (1, D_MODEL), lambda i: (0, 0)), _per_seq_out(B), _per_seq_out(B)]
    outs += [_rows_out(T, BF16), _per_seq_out(B)] if resid else []
    return row_matmul(a, w, mode="nt", name=name, extras=extras, outs=outs, epi=epi)


def _bucket_table(dilation):
    i = np.arange(ATT_BLOCK)[:, None]
    j = np.arange(2 * ATT_BLOCK)[None, :]
    delta = ATT_BLOCK + i - j
    dist = np.maximum(delta, 0) * dilation
    max_exact = N_BUCKETS // 2
    d = np.maximum(dist, max_exact).astype(np.float32)
    large = max_exact + (np.log(d / np.float32(max_exact)) / np.float32(math.log(MAX_DISTANCE / max_exact))
                         * np.float32(N_BUCKETS - max_exact)).astype(np.int32)
    large = np.minimum(large, N_BUCKETS - 1)
    return np.where(dist < max_exact, dist, large).astype(np.int32)


def _bucket_onehot(dilation):
    bt = jnp.asarray(_bucket_table(dilation).reshape(1, -1))
    return (bt == jnp.arange(N_BUCKETS, dtype=jnp.int32)[:, None]).astype(F32)


def bias_expand(rel_t, onehots, name):
    def fn(r, *ohs):
        return tuple(_doth(r, oh) for oh in ohs)
    return small_call(fn, [rel_t] + list(onehots),
                      [jax.ShapeDtypeStruct((rel_t.shape[0], oh.shape[1]), F32) for oh in onehots], name)


def bias_reduce(dbias_flat, onehot, name):
    def fn(db, oh):
        return (_doth(db, oh, _NT),)
    return small_call(fn, [dbias_flat, onehot], [jax.ShapeDtypeStruct((dbias_flat.shape[0], N_BUCKETS), F32)], name)[0]


def _qk_norm(x, g):
    return _rms(x) * g


def _masked_bias(bias):
    i = lax.broadcasted_iota(jnp.int32, (ATT_BLOCK, 2 * ATT_BLOCK), 0)
    j = lax.broadcasted_iota(jnp.int32, (ATT_BLOCK, 2 * ATT_BLOCK), 1)
    bm = jnp.where(jnp.logical_and(j >= i, j <= i + ATT_BLOCK), bias, -jnp.inf)
    return bm, bm[:, ATT_BLOCK:]


def _attn_tile(qn, kn, v, bias):
    s = bdot_nt(qn, kn) * (HEAD_DIM ** -0.5) + bias
    m = lax.stop_gradient(jnp.max(s, axis=-1, keepdims=True))
    p = jnp.exp(s - m)
    l = jnp.sum(p, axis=-1, keepdims=True)
    o = bdot_nn(p, v) / l
    lse = jnp.broadcast_to(m + jnp.log(l), (ATT_BLOCK, HEAD_DIM))
    return o, lse


def _attn_tiles(dilation, rows=SEQ):
    nb = rows // dilation // ATT_BLOCK
    return [(r, n) for r in range(dilation) for n in range(nb)]


def _attn_rows(r, n, dilation, nblk=1):
    if dilation == 1:
        return pl.ds(r + n * ATT_BLOCK, nblk * ATT_BLOCK)
    return pl.ds(r + n * ATT_BLOCK * dilation, nblk * ATT_BLOCK, stride=dilation)


_QL, _KL, _VL = slice(0, 128), slice(128, 256), slice(256, 384)


def _qkv_specs(hb):
    return [pl.BlockSpec((None, SEQ, HEAD_DIM), functools.partial(lambda b, h, j: (b, 0, 3 * (hb + h) + j), j=j))
            for j in range(3)]


def attn_fwd(pa, bias, qg, kg, group, name):
    B = pa.shape[0]
    dilation = ATT_GROUPS[group][1]
    hb = group * 4

    def body(q_ref, k_ref, v_ref, b_ref, qg_ref, kg_ref, o_ref, l_ref, qn_s, kn_s):
        qn_s[...] = _qk_norm(q_ref[...], qg_ref[...])
        kn_s[...] = _qk_norm(k_ref[...], kg_ref[...])
        bias_all, bias_first = _masked_bias(b_ref[0])
        for (r, n) in _attn_tiles(dilation):
            rows = _attn_rows(r, n, dilation)
            if n == 0:
                krows, bias_t = rows, bias_first
            else:
                krows, bias_t = _attn_rows(r, n - 1, dilation, 2), bias_all
            o, lse = _attn_tile(qn_s[rows, :], kn_s[krows, :], v_ref[krows, :], bias_t)
            o_ref[rows, :] = o
            l_ref[rows, :] = lse

    head_out = pl.BlockSpec((None, SEQ, HEAD_DIM), lambda b, h: (b, 0, h))
    return pl.pallas_call(
        body, name=name, grid=(B, 4),
        in_specs=_qkv_specs(hb) + [
                  pl.BlockSpec((1, ATT_BLOCK, 2 * ATT_BLOCK), lambda b, h: (hb + h, 0, 0)),
                  pl.BlockSpec((1, HEAD_DIM), lambda b, h: (0, 0)),
                  pl.BlockSpec((1, HEAD_DIM), lambda b, h: (0, 0))],
        out_specs=[head_out, head_out],
        out_shape=[jax.ShapeDtypeStruct((B, SEQ, 512), F32), jax.ShapeDtypeStruct((B, SEQ, 512), F32)],
        scratch_shapes=[pltpu.VMEM((SEQ, HEAD_DIM), F32)] * 2,
        compiler_params=_cparams(),
    )(pa, pa, pa, bias, qg, kg)


def attn_bwd(pa, bias, qg, kg, do, dlse, dpa, group, name):
    B = pa.shape[0]
    dilation = ATT_GROUPS[group][1]
    hb = group * 4

    def body(q_ref, k_ref, v_ref, b_ref, qg_ref, kg_ref, do_ref, dl_ref, dpa_in,
             dp_ref, db_ref, dqg_ref, dkg_ref, qn_s, kn_s, dq_s, dk_s, dv_s):
        del dpa_in
        h_id = pl.program_id(1)

        @pl.when(jnp.logical_and(pl.program_id(0) == 0, h_id == 0))
        def _():
            db_ref[...] = jnp.zeros_like(db_ref)
            dqg_ref[...] = jnp.zeros_like(dqg_ref)
            dkg_ref[...] = jnp.zeros_like(dkg_ref)

        dk_s[...] = jnp.zeros_like(dk_s)
        dv_s[...] = jnp.zeros_like(dv_s)
        qn_s[...] = _qk_norm(q_ref[...], qg_ref[...])
        kn_s[...] = _qk_norm(k_ref[...], kg_ref[...])
        bias_all, bias_first = _masked_bias(b_ref[0])
        for (r, n) in _attn_tiles(dilation):
            rows = _attn_rows(r, n, dilation)
            if n == 0:
                krows, bias_t = rows, bias_first
            else:
                krows, bias_t = _attn_rows(r, n - 1, dilation, 2), bias_all
            _, vjp = jax.vjp(_attn_tile, qn_s[rows, :], kn_s[krows, :], v_ref[krows, :], bias_t)
            dqn, dkn, dv, dbias = vjp((do_ref[rows, :], dl_ref[rows, :]))
            dq_s[rows, :] = dqn
            dk_s[krows, :] += dkn
            dv_s[krows, :] += dv
            if n == 0:
                db_ref[h_id, :, ATT_BLOCK:] += dbias
            else:
                db_ref[h_id] += dbias
        for x_ref, g_ref, d_s, dg_ref, lanes in ((q_ref, qg_ref, dq_s, dqg_ref, _QL), (k_ref, kg_ref, dk_s, dkg_ref, _KL)):
            _, vjp = jax.vjp(_qk_norm, x_ref[...], g_ref[...])
            dx, dg = vjp(d_s[...])
            dp_ref[0, :, lanes] = dx.astype(BF16)
            dg_ref[...] += dg
        dp_ref[0, :, _VL] = dv_s[...].astype(BF16)

    const2 = lambda b, h: (0, 0)
    head_in = pl.BlockSpec((None, SEQ, HEAD_DIM), lambda b, h: (b, 0, h))
    head_blk = pl.BlockSpec((1, SEQ, ATT_HEAD_COLS), lambda b, h: (b, 0, hb + h))
    return pl.pallas_call(
        body, name=name, grid=(B, 4),
        in_specs=_qkv_specs(hb) + [
                  pl.BlockSpec((1, ATT_BLOCK, 2 * ATT_BLOCK), lambda b, h: (hb + h, 0, 0)),
                  pl.BlockSpec((1, HEAD_DIM), const2), pl.BlockSpec((1, HEAD_DIM), const2),
                  head_in, head_in,
                  pl.BlockSpec(memory_space=pl.ANY)],
        out_specs=[head_blk,
                   pl.BlockSpec((4, ATT_BLOCK, 2 * ATT_BLOCK), lambda b, h: (0, 0, 0)),
                   pl.BlockSpec((1, HEAD_DIM), const2), pl.BlockSpec((1, HEAD_DIM), const2)],
        out_shape=[jax.ShapeDtypeStruct(dpa.shape, BF16),
                   jax.ShapeDtypeStruct((4, ATT_BLOCK, 2 * ATT_BLOCK), F32),
                   jax.ShapeDtypeStruct((1, HEAD_DIM), F32), jax.ShapeDtypeStruct((1, HEAD_DIM), F32)],
        scratch_shapes=[pltpu.VMEM((SEQ, HEAD_DIM), F32)] * 5,
        input_output_aliases={8: 0},
        compiler_params=_cparams(),
    )(pa, pa, pa, bias, qg, kg, do, dlse, dpa)


def _attn_classes(q, k, v, bias, qg, kg):
    s = cdot_nt(_qk_norm(q, qg), _qk_norm(k, kg)) * (HEAD_DIM ** -0.5) + bias
    m = lax.stop_gradient(jnp.max(s, axis=-1, keepdims=True))
    p = jnp.exp(s - m)
    l = jnp.sum(p, axis=-1, keepdims=True)
    o = cdot_nn(p, v) / l
    return o, jnp.broadcast_to(m + jnp.log(l), o.shape)


def _gather_classes(src_ref, dst_s, dilation):
    for r in range(dilation):
        dst_s[r] = src_ref[pl.ds(r, ATT_BLOCK, stride=dilation), :]


def _scatter_classes(src_s, dst_ref, dilation):
    for r in range(dilation):
        dst_ref[pl.ds(r, ATT_BLOCK, stride=dilation), :] = src_s[r]


def attn_fwd_classes(pa, bias, qg, kg, group, name):
    B = pa.shape[0]
    dilation = ATT_GROUPS[group][1]
    hb = group * 4

    def body(q_ref, k_ref, v_ref, b_ref, qg_ref, kg_ref, o_ref, l_ref, q_s, k_s, v_s):
        _gather_classes(q_ref, q_s, dilation)
        _gather_classes(k_ref, k_s, dilation)
        _gather_classes(v_ref, v_s, dilation)
        o, lse = _attn_classes(q_s[...], k_s[...], v_s[...], _masked_bias(b_ref[0])[1], qg_ref[...], kg_ref[...])
        q_s[...], k_s[...] = o, lse
        _scatter_classes(q_s, o_ref, dilation)
        _scatter_classes(k_s, l_ref, dilation)

    head_out = pl.BlockSpec((None, SEQ, HEAD_DIM), lambda b, h: (b, 0, h))
    return pl.pallas_call(
        body, name=name, grid=(B, 4),
        in_specs=_qkv_specs(hb) + [
                  pl.BlockSpec((1, ATT_BLOCK, 2 * ATT_BLOCK), lambda b, h: (hb + h, 0, 0)),
                  pl.BlockSpec((1, HEAD_DIM), lambda b, h: (0, 0)),
                  pl.BlockSpec((1, HEAD_DIM), lambda b, h: (0, 0))],
        out_specs=[head_out, head_out],
        out_shape=[jax.ShapeDtypeStruct((B, SEQ, 512), F32), jax.ShapeDtypeStruct((B, SEQ, 512), F32)],
        scratch_shapes=[pltpu.VMEM((dilation, ATT_BLOCK, HEAD_DIM), F32)] * 3,
        compiler_params=_cparams(),
    )(pa, pa, pa, bias, qg, kg)


def attn_bwd_classes(pa, bias, qg, kg, do, dlse, dpa, group, name):
    B = pa.shape[0]
    dilation = ATT_GROUPS[group][1]
    hb = group * 4

    def body(q_ref, k_ref, v_ref, b_ref, qg_ref, kg_ref, do_ref, dl_ref, dpa_in,
             dp_ref, db_ref, dqg_ref, dkg_ref, q_s, k_s, v_s, do_s, dl_s, rows_s):
        del dpa_in
        h_id = pl.program_id(1)

        @pl.when(jnp.logical_and(pl.program_id(0) == 0, h_id == 0))
        def _():
            db_ref[...] = jnp.zeros_like(db_ref)
            dqg_ref[...] = jnp.zeros_like(dqg_ref)
            dkg_ref[...] = jnp.zeros_like(dkg_ref)

        for src, dst in ((q_ref, q_s), (k_ref, k_s), (v_ref, v_s), (do_ref, do_s), (dl_ref, dl_s)):
            _gather_classes(src, dst, dilation)
        _, vjp = jax.vjp(_attn_classes, q_s[...], k_s[...], v_s[...], _masked_bias(b_ref[0])[1],
                         qg_ref[...], kg_ref[...])
        dq, dk, dv, dbias, dqg, dkg = vjp((do_s[...], dl_s[...]))
        db_ref[h_id, :, ATT_BLOCK:] += dbias
        dqg_ref[...] += dqg
        dkg_ref[...] += dkg
        for d, lanes in ((dq, _QL), (dk, _KL), (dv, _VL)):
            q_s[...] = d
            _scatter_classes(q_s, rows_s, dilation)
            dp_ref[0, :, lanes] = rows_s[...].astype(BF16)

    const2 = lambda b, h: (0, 0)
    head_in = pl.BlockSpec((None, SEQ, HEAD_DIM), lambda b, h: (b, 0, h))
    head_blk = pl.BlockSpec((1, SEQ, ATT_HEAD_COLS), lambda b, h: (b, 0, hb + h))
    return pl.pallas_call(
        body, name=name, grid=(B, 4),
        in_specs=_qkv_specs(hb) + [
                  pl.BlockSpec((1, ATT_BLOCK, 2 * ATT_BLOCK), lambda b, h: (hb + h, 0, 0)),
                  pl.BlockSpec((1, HEAD_DIM), const2), pl.BlockSpec((1, HEAD_DIM), const2),
                  head_in, head_in,
                  pl.BlockSpec(memory_space=pl.ANY)],
        out_specs=[head_blk,
                   pl.BlockSpec((4, ATT_BLOCK, 2 * ATT_BLOCK), lambda b, h: (0, 0, 0)),
                   pl.BlockSpec((1, HEAD_DIM), const2), pl.BlockSpec((1, HEAD_DIM), const2)],
        out_shape=[jax.ShapeDtypeStruct(dpa.shape, BF16),
                   jax.ShapeDtypeStruct((4, ATT_BLOCK, 2 * ATT_BLOCK), F32),
                   jax.ShapeDtypeStruct((1, HEAD_DIM), F32), jax.ShapeDtypeStruct((1, HEAD_DIM), F32)],
        scratch_shapes=[pltpu.VMEM((dilation, ATT_BLOCK, HEAD_DIM), F32)] * 5 + [pltpu.VMEM((SEQ, HEAD_DIM), F32)],
        input_output_aliases={8: 0},
        compiler_params=_cparams(),
    )(pa, pa, pa, bias, qg, kg, do, dlse, dpa)


def _merge(o0, o1, o2, l0, l1, l2):
    mx = jnp.maximum(jnp.maximum(l0, l1), l2)
    e0, e1, e2 = jnp.exp(l0 - mx), jnp.exp(l1 - mx), jnp.exp(l2 - mx)
    den = e0 + e1 + e2
    return (e0 / den) * o0 + (e1 / den) * o1 + (e2 / den) * o2


def merge_att_out(os_, ls_, w_att_out, name):
    T = os_[0].shape[0]
    tile = pl.BlockSpec((ROW_MM_TILE, 512), lambda i: (i, 0))

    def body(o0, o1, o2, l0, l1, l2, w_ref, a_ref, y_ref):
        att = _merge(o0[...], o1[...], o2[...], l0[...], l1[...], l2[...]).astype(BF16)
        a_ref[...] = att
        y_ref[...] = _mxu(att, w_ref[...], _NN)

    return pl.pallas_call(
        body, name=name, grid=(T // ROW_MM_TILE,),
        in_specs=[tile] * 6 + [pl.BlockSpec((512, D_MODEL), lambda i: (0, 0))],
        out_specs=[tile, pl.BlockSpec((ROW_MM_TILE, D_MODEL), lambda i: (i, 0))],
        out_shape=[jax.ShapeDtypeStruct((T, 512), BF16), jax.ShapeDtypeStruct((T, D_MODEL), F32)],
        compiler_params=_cparams(),
    )(*os_, *ls_, w_att_out)


def d_att_merge_bwd(d_ya, w_att_out, os_, ls_, name):
    T = d_ya.shape[0]
    tile = lambda a: (a, (ROW_MM_TILE, 512), lambda i: (i, 0))

    def epi(acc, i, ex, out):
        _, vjp = jax.vjp(_merge, *[e[...] for e in ex])
        for o_ref, g in zip(out, vjp(acc)):
            o_ref[...] = g

    return row_matmul(d_ya, w_att_out, mode="nt", name=name, extras=[tile(a) for a in list(os_) + list(ls_)],
                      outs=[((T, 512), F32, (ROW_MM_TILE, 512), lambda i: (i, 0))] * 6, epi=epi)


def _gate_mix(ga, gm, ya, ym):
    return jax.nn.sigmoid(ga) * ya + jax.nn.sigmoid(gm) * ym


def _gate_halves(pg):
    return [(pg, (ROW_MM_TILE, D_MODEL), lambda i: (i, 0)), (pg, (ROW_MM_TILE, D_MODEL), lambda i: (i, 1))]


def ml_out_gate(hg, w_ml_out, pg, ya, name):
    T = hg.shape[0]

    def epi(acc, i, ex, out):
        ga, gm, ya_ref = ex
        out[0][...] = acc
        out[1][...] = _gate_mix(ga[...], gm[...], ya_ref[...], acc).astype(BF16)

    return row_matmul(hg, w_ml_out, mode="nn", name=name, extras=_gate_halves(pg) + [_rows(ya)],
                      outs=[_rows_out(T, F32), _rows_out(T, BF16)], epi=epi)


def d_z_gate_bwd(dy, w_out, pg, ya, ym, name):
    T = dy.shape[0]

    def epi(acc, i, ex, out):
        ga, gm, ya_ref, ym_ref = ex
        dpg_ref, dya_ref, dym_ref = out
        _, vjp = jax.vjp(_gate_mix, ga[...], gm[...], ya_ref[...], ym_ref[...])
        dga, dgm, dya, dym = vjp(acc)
        dpg_ref[:, :D_MODEL] = dga.astype(BF16)
        dpg_ref[:, D_MODEL:] = dgm.astype(BF16)
        dya_ref[...] = dya.astype(BF16)
        dym_ref[...] = dym.astype(BF16)

    return row_matmul(dy, w_out, mode="nt", name=name, extras=_gate_halves(pg) + [_rows(ya), _rows(ym)],
                      outs=[((T, GATE_COLS), BF16, (ROW_MM_TILE, GATE_COLS), lambda i: (i, 0)),
                            _rows_out(T, BF16), _rows_out(T, BF16)], epi=epi)


def _log_sigmoid(x):
    return jnp.minimum(x, 0.0) - jnp.log(1.0 + jnp.exp(-jnp.abs(x)))


def _head_mask(e):
    lane = lax.broadcasted_iota(jnp.int32, (1, 128), 1)
    return jnp.logical_and(lane >= e * 64, lane < (e + 1) * 64).astype(F32)


def _bmxu(a, b, ca, cb):
    return lax.dot_general(a.astype(BF16), b.astype(BF16), (((ca,), (cb,)), ((0,), (0,))), preferred_element_type=F32)


@jax.custom_vjp
def cdot_nt(a, b):
    return _bmxu(a, b, 2, 2)


cdot_nt.defvjp(lambda a, b: (_bmxu(a, b, 2, 2), (a, b)),
               lambda res, g: (_bmxu(g, res[1], 2, 1), _bmxu(g, res[0], 1, 1)))


@jax.custom_vjp
def cdot_nn(a, b):
    return _bmxu(a, b, 2, 1)


cdot_nn.defvjp(lambda a, b: (_bmxu(a, b, 2, 1), (a, b)),
               lambda res, g: (_bmxu(g, res[1], 2, 2), _bmxu(res[0], g, 1, 1)))


@jax.custom_vjp
def cdot_tn(a, b):
    return _bmxu(a, b, 1, 1)


cdot_tn.defvjp(lambda a, b: (_bmxu(a, b, 1, 1), (a, b)),
               lambda res, g: (_bmxu(res[1], g, 2, 2), _bmxu(res[0], g, 2, 1)))


def _top_bits(x):
    return lax.bitcast_convert_type(lax.bitcast_convert_type(x, jnp.uint32) & jnp.uint32(0xFFFF0000), F32)


def _split3(x):
    hi = _top_bits(x)
    r = x - hi
    mid = _top_bits(r)
    return hi, mid, r - mid


def _parts_in_lanes(col):
    hi, mid, lo = _split3(col)
    lane = lax.broadcasted_iota(jnp.int32, (1, 1, 8), 2)
    return jnp.where(lane == 0, hi, jnp.where(lane == 1, mid, jnp.where(lane == 2, lo, 0.0)))


def _parts_in_rows(row):
    hi, mid, lo = _split3(row)
    sub = lax.broadcasted_iota(jnp.int32, (1, 8, 1), 1)
    return jnp.where(sub == 0, hi, jnp.where(sub == 1, mid, jnp.where(sub == 2, lo, 0.0)))


def _chunk_matrix(kind, c):
    ri = lax.broadcasted_iota(jnp.int32, (c, ML_CHUNK, ML_CHUNK), 1)
    ci = lax.broadcasted_iota(jnp.int32, (c, ML_CHUNK, ML_CHUNK), 2)
    return {"eye": ri == ci, "lower": ri >= ci, "upper": ri <= ci}[kind].astype(F32)


def _col_col(kind, col):
    out = _bmxu(_chunk_matrix(kind, col.shape[0]), _parts_in_lanes(col), 2, 1)
    return jnp.sum(out, axis=-1, keepdims=True)


def _col_row(col):
    out = _bmxu(_parts_in_lanes(col), _chunk_matrix("eye", col.shape[0]), 1, 1)
    return jnp.sum(out, axis=1, keepdims=True)


def _row_col(row):
    out = _bmxu(_chunk_matrix("eye", row.shape[0]), _parts_in_rows(row), 2, 2)
    return jnp.sum(out, axis=-1, keepdims=True)


@jax.custom_vjp
def chunk_cumsum(col):
    return _col_col("lower", col)


chunk_cumsum.defvjp(lambda col: (_col_col("lower", col), None), lambda _, g: (_col_col("upper", g),))


@jax.custom_vjp
def col_to_row(col):
    return _col_row(col)


col_to_row.defvjp(lambda col: (_col_row(col), None), lambda _, g: (_row_col(g),))


def _gate_block(ifb):
    lane = lax.broadcasted_iota(jnp.int32, (1, 128), 1)
    return jnp.where(lane >= 2, _log_sigmoid(ifb), ifb)


def _ml_intra(q2, k2, v, ifb, *, e):
    c, L = q2.shape[0] // ML_CHUNK, ML_CHUNK
    hm = _head_mask(e)
    q3 = (q2 * hm).reshape(c, L, 128)
    k3 = (k2 * hm).reshape(c, L, 128)
    v3 = v.reshape(c, L, 128)
    if3 = ifb.reshape(c, L, 128)
    lanes = lax.broadcasted_iota(jnp.int32, (c, L, 128), 2)
    li = jnp.sum(jnp.where(lanes == e, if3, 0.0), axis=-1, keepdims=True)
    lf = jnp.sum(jnp.where(lanes == 2 + e, if3, 0.0), axis=-1, keepdims=True)
    b = chunk_cumsum(lf)
    last = lax.broadcasted_iota(jnp.int32, (1, L, 1), 1) == L - 1
    b_end = jnp.sum(jnp.where(last, b, 0.0), axis=1, keepdims=True)
    causal = lax.broadcasted_iota(jnp.int32, (L, L), 0) >= lax.broadcasted_iota(jnp.int32, (L, L), 1)
    Dm = jnp.where(causal, b + col_to_row(li - b), -jnp.inf)
    mD = lax.stop_gradient(jnp.max(Dm, axis=-1, keepdims=True))
    P0 = cdot_nt(q3, k3) * jnp.exp(Dm - mD)
    H0 = cdot_nn(P0, v3)
    r0 = jnp.sum(P0, axis=-1, keepdims=True)
    g = b_end - b + li
    mg = lax.stop_gradient(jnp.max(g, axis=1, keepdims=True))
    kw = jnp.exp(g - mg) * k3
    return H0, r0, cdot_tn(kw, v3), jnp.sum(kw, axis=1, keepdims=True), b, b_end, mD, mg


def _ml_inter(q2, mo, gn, H0, r0, b, C_in, n_in, *, mD, m_in, e):
    c, L = q2.shape[0] // ML_CHUNK, ML_CHUNK
    q3 = (q2 * _head_mask(e)).reshape(c, L, 128)
    a = b + m_in
    m_t = lax.stop_gradient(jnp.maximum(a, mD))
    c1 = jnp.exp(mD - m_t)
    c2 = jnp.exp(a - m_t)
    num = c1 * H0 + c2 * cdot_nn(q3, C_in)
    nq = c1 * r0 + c2 * jnp.sum(q3 * n_in, axis=-1, keepdims=True)
    h = num / jnp.maximum(jnp.abs(nq), jnp.exp(-m_t))
    hg = _rms(h) * gn * jax.nn.sigmoid(mo.reshape(c, L, 128))
    return hg.reshape(c * L, 128)


def _state_sweep(U_s, un_s, be_s, mg_s, Cin_s, nin_s, min_s, al_s, bt_s):
    def step(j, carry):
        C, n, m = carry
        Cin_s[j], nin_s[j], min_s[j] = C, n, m
        m_out = jnp.maximum(be_s[j] + m, mg_s[j])
        al = jnp.exp(be_s[j] + m - m_out)
        bt = jnp.exp(mg_s[j] - m_out)
        al_s[j], bt_s[j] = al, bt
        return al * C + bt * U_s[j], al * n + bt * un_s[j], m_out

    lax.fori_loop(0, N_CHUNKS, step, (jnp.zeros((128, 128), F32), jnp.zeros((1, 128), F32), jnp.zeros((1, 1), F32)))


def _state_sweep_bwd(U_s, un_s, dbe_s, Cin_s, nin_s, dCp_s, dnp_s, al_s, bt_s):
    def step(t, carry):
        j = N_CHUNKS - 1 - t
        dC, dn = carry
        al, bt = al_s[j], bt_s[j]
        U_s[j] = bt * dC
        un_s[j] = bt * dn
        dal = jnp.sum(jnp.sum(dC * Cin_s[j], axis=1, keepdims=True), axis=0, keepdims=True) \
            + jnp.sum(dn * nin_s[j], axis=1, keepdims=True)
        dbe_s[j] = dal * al
        return dCp_s[j] + al * dC, dnp_s[j] + al * dn

    lax.fori_loop(0, N_CHUNKS, step, (jnp.zeros((128, 128), F32), jnp.zeros((1, 128), F32)))


def _state_scratch():
    c = N_CHUNKS
    return [pltpu.VMEM((c, 128, 128), F32), pltpu.VMEM((c, 1, 128), F32), pltpu.VMEM((c, 1, 1), F32),
            pltpu.VMEM((c, 1, 1), F32),
            pltpu.VMEM((c, 128, 128), F32), pltpu.VMEM((c, 1, 128), F32), pltpu.VMEM((c, 1, 1), F32),
            pltpu.VMEM((c, 1, 1), F32), pltpu.VMEM((c, 1, 1), F32)]


def _shift_down(x, s):
    if s == 0:
        return x
    rows = lax.broadcasted_iota(jnp.int32, x.shape, 0)
    return jnp.where(rows >= s, pltpu.roll(x, s, 0), 0.0)


def _shift_up(x, s):
    if s == 0:
        return x
    S = x.shape[0]
    rows = lax.broadcasted_iota(jnp.int32, x.shape, 0)
    return jnp.where(rows < S - s, pltpu.roll(x, S - s, 0), 0.0)


def _conv_pre(x, cw, cb):
    y = cb + cw[3:4, :] * x
    for j in range(3):
        y = y + cw[j:j + 1, :] * _shift_down(x, 3 - j)
    return y


def _conv_bwd(x, cw, dpre):
    dx = cw[3:4, :] * dpre
    dcw = [None] * 4
    dcw[3] = jnp.sum(dpre * x, axis=0, keepdims=True)
    for j in range(3):
        dx = dx + cw[j:j + 1, :] * _shift_up(dpre, 3 - j)
        dcw[j] = jnp.sum(dpre * _shift_down(x, 3 - j), axis=0, keepdims=True)
    return dx, dcw, jnp.sum(dpre, axis=0, keepdims=True)


def _silu(z):
    return z * jax.nn.sigmoid(z)


def _dsilu(z):
    s = jax.nn.sigmoid(z)
    return s * (1.0 + z * (1.0 - s))


_ML_Q, _ML_K = slice(0, 128), slice(128, 256)
_ML_IF = slice(768, 896)


def _ml_v(e):
    return slice(256 + e * 128, 384 + e * 128)


def _ml_o(e):
    return slice(512 + e * 128, 640 + e * 128)


def _ml_specs():
    pair = lambda b, p: (b, 0, p)
    return [pl.BlockSpec((1, SEQ, ML_PAIR_COLS), pair),
            pl.BlockSpec((1, 4, 128), lambda b, p: (p, 0, 0)),
            pl.BlockSpec((1, 4, 128), lambda b, p: (4 + p, 0, 0)),
            pl.BlockSpec((1, 1, 128), lambda b, p: (p, 0, 0)),
            pl.BlockSpec((1, 1, 128), lambda b, p: (4 + p, 0, 0)),
            pl.BlockSpec((1, 1, 128), lambda b, p: (p, 0, 0)),
            pl.BlockSpec((1, 1, 256), lambda b, p: (p, 0, 0))]


def mlstm_fwd(pm, cw8, cb8, bifp, gn4, name):
    B = pm.shape[0]

    def body(p_ref, cwq, cwk, cbq, cbk, bif_ref, gn_ref, hg_ref, *st):
        U_s, un_s, be_s, mg_s, Cin_s, nin_s, min_s, al_s, bt_s = st
        qc = _silu(_conv_pre(p_ref[0, :, _ML_Q], cwq[0], cbq[0]))
        kc = _silu(_conv_pre(p_ref[0, :, _ML_K], cwk[0], cbk[0])) * (64 ** -0.5)
        ifb = _gate_block(p_ref[0, :, _ML_IF] + bif_ref[0])
        for e in range(2):
            lanes = slice(e * 128, (e + 1) * 128)
            H0, r0, U, un, b, b_end, mD, mg = _ml_intra(qc, kc, p_ref[0, :, _ml_v(e)], ifb, e=e)
            U_s[...], un_s[...], be_s[...], mg_s[...] = U, un, b_end, mg
            _state_sweep(*st)
            hg = _ml_inter(qc, p_ref[0, :, _ml_o(e)], gn_ref[0, :, lanes], H0, r0, b, Cin_s[...], nin_s[...],
                           mD=mD, m_in=min_s[...], e=e)
            hg_ref[0, :, lanes] = hg.astype(BF16)

    return pl.pallas_call(
        body, name=name, grid=(B, ML_PAIRS),
        in_specs=_ml_specs(),
        out_specs=pl.BlockSpec((1, SEQ, 256), lambda b, p: (b, 0, p)),
        out_shape=jax.ShapeDtypeStruct((B, SEQ, D_MODEL), BF16),
        scratch_shapes=_state_scratch(),
        compiler_params=_cparams(),
    )(pm, cw8, cw8, cb8, cb8, bifp, gn4)


def mlstm_bwd(pm, cw8, cb8, bifp, gn4, dhg, name):
    B = pm.shape[0]

    def body(p_ref, cwq, cwk, cbq, cbk, bif_ref, gn_ref, dh_ref,
             dp_ref, dcw_ref, dcb_ref, dbif_ref, dgn_ref, *scr):
        st = scr[:9]
        U_s, un_s, be_s, mg_s, Cin_s, nin_s, min_s, al_s, bt_s = st
        dCp_s, dnp_s, dbe_s = scr[9:]
        p_id = pl.program_id(1)

        @pl.when(jnp.logical_and(pl.program_id(0) == 0, p_id == 0))
        def _():
            dcw_ref[...] = jnp.zeros_like(dcw_ref)
            dcb_ref[...] = jnp.zeros_like(dcb_ref)
            dbif_ref[...] = jnp.zeros_like(dbif_ref)
            dgn_ref[...] = jnp.zeros_like(dgn_ref)

        pre_q = _conv_pre(p_ref[0, :, _ML_Q], cwq[0], cbq[0])
        pre_k = _conv_pre(p_ref[0, :, _ML_K], cwk[0], cbk[0])
        qc = _silu(pre_q)
        kc = _silu(pre_k) * (64 ** -0.5)
        ifb, gate_vjp = jax.vjp(_gate_block, p_ref[0, :, _ML_IF] + bif_ref[0])
        dq = jnp.zeros((SEQ, 128), F32)
        dk = jnp.zeros((SEQ, 128), F32)
        difb = jnp.zeros((SEQ, 128), F32)
        for e in range(2):
            lanes = slice(e * 128, (e + 1) * 128)
            (H0, r0, U, un, b, b_end, mD, mg), vjp1 = jax.vjp(functools.partial(_ml_intra, e=e), qc, kc,
                                                              p_ref[0, :, _ml_v(e)], ifb)
            U_s[...], un_s[...], be_s[...], mg_s[...] = U, un, b_end, mg
            _state_sweep(*st)
            _, vjp3 = jax.vjp(functools.partial(_ml_inter, mD=mD, m_in=min_s[...], e=e), qc, p_ref[0, :, _ml_o(e)],
                              gn_ref[0, :, lanes], H0, r0, b, Cin_s[...], nin_s[...])
            dq_a, dmo, dgn, dH0, dr0, db_a, dCp, dnp = vjp3(dh_ref[0, :, lanes])
            dCp_s[...], dnp_s[...] = dCp, dnp
            _state_sweep_bwd(U_s, un_s, dbe_s, Cin_s, nin_s, dCp_s, dnp_s, al_s, bt_s)
            dq_b, dk_b, dv, difb_e = vjp1((dH0, dr0, U_s[...], un_s[...], db_a, dbe_s[...],
                                           jnp.zeros_like(mD), jnp.zeros_like(mg)))
            dq, dk, difb = dq + dq_a + dq_b, dk + dk_b, difb + difb_e
            dp_ref[0, :, _ml_v(e)] = dv.astype(BF16)
            dp_ref[0, :, _ml_o(e)] = dmo.astype(BF16)
            dgn_ref[p_id, :, lanes] += dgn
        (difb,) = gate_vjp(difb)
        dp_ref[0, :, _ML_IF] = difb.astype(BF16)
        dbif_ref[p_id] += jnp.sum(difb, axis=0, keepdims=True)

        for (sl, cw, pre, d, blk, scale) in ((_ML_Q, cwq, pre_q, dq, p_id, 1.0), (_ML_K, cwk, pre_k, dk, 4 + p_id, 64 ** -0.5)):
            xr = p_ref[0, :, sl]
            dpre = d * scale * _dsilu(pre)
            dx, dcw, dcb = _conv_bwd(xr, cw[0], dpre)
            dp_ref[0, :, sl] = dx.astype(BF16)
            for j in range(4):
                dcw_ref[blk, j:j + 1, :] += dcw[j]
            dcb_ref[blk] += dcb

    full3 = lambda b, p: (0, 0, 0)
    return pl.pallas_call(
        body, name=name, grid=(B, ML_PAIRS),
        in_specs=[pl.BlockSpec((1, SEQ, ML_PAIR_COLS), lambda b, p: (b, 0, p), pipeline_mode=pl.Buffered(1))]
        + _ml_specs()[1:] + [pl.BlockSpec((1, SEQ, 256), lambda b, p: (b, 0, p), pipeline_mode=pl.Buffered(1))],
        out_specs=[pl.BlockSpec((1, SEQ, ML_PAIR_COLS), lambda b, p: (b, 0, p)),
                   pl.BlockSpec((8, 4, 128), full3), pl.BlockSpec((8, 1, 128), full3),
                   pl.BlockSpec((4, 1, 128), full3), pl.BlockSpec((4, 1, 256), full3)],
        out_shape=[jax.ShapeDtypeStruct((B, SEQ, ML_COLS), BF16),
                   jax.ShapeDtypeStruct((8, 4, 128), F32), jax.ShapeDtypeStruct((8, 1, 128), F32),
                   jax.ShapeDtypeStruct((4, 1, 128), F32), jax.ShapeDtypeStruct((4, 1, 256), F32)],
        scratch_shapes=_state_scratch() + [pltpu.VMEM((N_CHUNKS, 128, 128), F32), pltpu.VMEM((N_CHUNKS, 1, 128), F32),
                                           pltpu.VMEM((N_CHUNKS, 1, 1), F32)],
        compiler_params=_cparams(),
    )(pm, cw8, cw8, cb8, cb8, bifp, gn4, dhg)


def _adamw(w, g, m, v):
    m = ADAM_B1 * m + (1.0 - ADAM_B1) * g
    v = ADAM_B2 * v + (1.0 - ADAM_B2) * (g * g)
    m_hat = m / (1.0 - ADAM_B1 ** ADAM_STEP)
    v_hat = v / (1.0 - ADAM_B2 ** ADAM_STEP)
    delta = -ADAM_LR * (m_hat / (jnp.sqrt(v_hat) + ADAM_EPS) + ADAM_WD * w)
    return delta, m, v


def adamw(w, g, m, v, name, parts=False):
    R, C = w.shape
    if R % 8 == 0 or R * C * 4 <= (1 << 20):
        tr = _pick(R, (256, 128, 64, 32, 16, 8)) if R * C * 4 > (1 << 20) else R
        steps = R // tr
        spec = pl.BlockSpec((tr, C), lambda i: (i, 0))
        g_spec = pl.BlockSpec((N_DEV, tr, C), lambda i: (0, i, 0)) if parts else spec
    else:
        tc = _pick(C, (256, 128))
        steps = C // tc
        spec = pl.BlockSpec((R, tc), lambda i: (0, i))
        g_spec = pl.BlockSpec((N_DEV, R, tc), lambda i: (0, 0, i)) if parts else spec

    def body(w_ref, g_ref, m_ref, v_ref, go_ref, d_ref, mo_ref, vo_ref):
        if parts:
            g = g_ref[0].astype(F32)
            for k in range(1, N_DEV):
                g = g + g_ref[k].astype(F32)
        else:
            g = g_ref[...]
        d, mn, vn = _adamw(w_ref[...], g, m_ref[...], v_ref[...])
        go_ref[...], d_ref[...], mo_ref[...], vo_ref[...] = g, d, mn, vn

    return pl.pallas_call(
        body, name=name, grid=(steps,),
        in_specs=[spec, g_spec, spec, spec], out_specs=[spec] * 4,
        out_shape=[jax.ShapeDtypeStruct((R, C), F32)] * 4,
        compiler_params=_cparams(),
    )(w, g, m, v)


def adamw_many(ws, gs, ms, vs, name):
    n = len(ws)

    def fn(*a):
        out = []
        for j in range(n):
            out += list(_adamw(a[j], a[n + j], a[2 * n + j], a[3 * n + j]))
        return tuple(out)

    shapes = [jax.ShapeDtypeStruct(w.shape, F32) for w in ws for _ in range(3)]
    return small_call(fn, list(ws) + list(gs) + list(ms) + list(vs), shapes, name)


def _mesh_pos():
    return lax.axis_index("x"), lax.axis_index("y"), lax.axis_index("c")


def _flip(pos, f):
    x, y, c = pos
    return (1 - x if f & 4 else x, 1 - y if f & 2 else y, 1 - c if f & 1 else c)


def _index(pos):
    return 4 * pos[0] + 2 * pos[1] + pos[2]


def _exchange(arrs, name, scatter):
    n = len(arrs)
    scat = list(scatter) if isinstance(scatter, (list, tuple)) else [scatter] * n

    def body(*refs):
        ins, outs = refs[:n], refs[n:2 * n]
        send, recv, lsem = refs[2 * n:]
        me = _mesh_pos()
        mine = _index(me)
        copies = []
        for i in range(n):
            src = ins[i].at[mine] if scat[i] else ins[i]
            loc = pltpu.make_async_copy(src, outs[i].at[mine], lsem.at[i])
            loc.start()
            copies.append(loc)
            for f in range(1, N_DEV):
                peer = _flip(me, f)
                src = ins[i].at[_index(peer)] if scat[i] else ins[i]
                cp = pltpu.make_async_remote_copy(
                    src_ref=src, dst_ref=outs[i].at[mine],
                    send_sem=send.at[i * 7 + f - 1], recv_sem=recv.at[i * 7 + f - 1],
                    device_id=peer, device_id_type=pl.DeviceIdType.MESH)
                cp.start()
                copies.append(cp)
        for cp in copies:
            cp.wait()

    any_spec = pl.BlockSpec(memory_space=pl.ANY)
    out_shape = [jax.ShapeDtypeStruct(a.shape if s else (N_DEV,) + a.shape, a.dtype) for a, s in zip(arrs, scat)]
    res = pl.pallas_call(
        body, name=name,
        in_specs=[any_spec] * n, out_specs=[any_spec] * n, out_shape=out_shape,
        scratch_shapes=[pltpu.SemaphoreType.DMA((7 * n,)), pltpu.SemaphoreType.DMA((7 * n,)),
                        pltpu.SemaphoreType.DMA((n,))],
        compiler_params=_cparams(),
    )(*arrs)
    return list(res)


def all_gather_two_level(arrs, name):
    n = len(arrs)

    def body(*refs):
        ins, outs = refs[:n], refs[n:2 * n]
        send, recv, lsem = refs[2 * n:]
        x, y, c = _mesh_pos()
        me, sibling = (x, y, c), (x, y, 1 - c)
        chips = [(1 - x, y), (x, 1 - y), (1 - x, 1 - y)]

        def copy(i, k, block, to, src=None):
            rows = outs[i].at[_index(block)]
            return pltpu.make_async_remote_copy(
                src_ref=rows if src is None else src, dst_ref=rows,
                send_sem=send.at[i * 7 + k], recv_sem=recv.at[i * 7 + k],
                device_id=to, device_id_type=pl.DeviceIdType.MESH)

        local = [pltpu.make_async_copy(ins[i], outs[i].at[_index(me)], lsem.at[i]) for i in range(n)]
        first = [copy(i, 0, me, sibling, src=ins[i]) for i in range(n)]
        first += [copy(i, 1 + j, me, (*chip, c), src=ins[i]) for i in range(n) for j, chip in enumerate(chips)]
        for cp in local + first:
            cp.start()
        passed = []
        for j, chip in enumerate(chips):
            for i in range(n):
                copy(i, 1 + j, (*chip, c), me).wait_recv()
                cp = copy(i, 4 + j, (*chip, c), sibling)
                cp.start()
                passed.append(cp)
        for i in range(n):
            copy(i, 0, sibling, me).wait_recv()
            for j, chip in enumerate(chips):
                copy(i, 4 + j, (*chip, 1 - c), me).wait_recv()
        for cp in first + passed:
            cp.wait_send()
        for cp in local:
            cp.wait()

    any_spec = pl.BlockSpec(memory_space=pl.ANY)
    res = pl.pallas_call(
        body, name=name,
        in_specs=[any_spec] * n, out_specs=[any_spec] * n,
        out_shape=[jax.ShapeDtypeStruct((N_DEV,) + a.shape, a.dtype) for a in arrs],
        scratch_shapes=[pltpu.SemaphoreType.DMA((7 * n,)), pltpu.SemaphoreType.DMA((7 * n,)),
                        pltpu.SemaphoreType.DMA((n,))],
        compiler_params=_cparams(),
    )(*arrs)
    return list(res)


def all_to_all(arrs, name):
    return _exchange(arrs, name, True)


_HBM = pl.BlockSpec(memory_space=pltpu.HBM)
_SEM = pl.BlockSpec(memory_space=pltpu.SEMAPHORE)
_EFFECT = pltpu.SideEffectType.DATAFLOW_SIDE_EFFECTING


def _split_copies(ins, lands, send, recv, scatter, waiting):
    me = _mesh_pos()
    mine = _index(me)
    copies = []
    for i in range(len(ins)):
        for f in range(1, N_DEV):
            peer = _flip(me, f)
            src = ins[i].at[_index(peer)] if scatter else ins[i]
            copies.append(pltpu.make_async_remote_copy(
                src_ref=src, dst_ref=lands[i].at[_index(peer) if waiting else mine],
                send_sem=send.at[i * 7 + f - 1], recv_sem=recv.at[i * 7 + f - 1],
                device_id=peer, device_id_type=pl.DeviceIdType.MESH))
    return copies


def exchange_start(arrs, name, scatter, after=()):
    n = len(arrs)
    land_shapes = [a.shape if scatter else (N_DEV,) + a.shape for a in arrs]

    def body(*refs):
        ins, lands = refs[:n], refs[n:2 * n]
        send, recv = refs[2 * n + len(after)], refs[2 * n + len(after) + 1]
        token = refs[-1]
        for cp in _split_copies(ins, lands, send, recv, scatter, False):
            cp.start()
        token[...] = jnp.zeros_like(token)

    res = pl.pallas_call(
        body, name=name,
        out_shape=(pltpu.SemaphoreType.DMA((7 * n,)), pltpu.SemaphoreType.DMA((7 * n,)),
                   *[pltpu.HBM(a.shape, a.dtype) for a in arrs],
                   *[pltpu.HBM(s, a.dtype) for s, a in zip(land_shapes, arrs)],
                   jax.ShapeDtypeStruct((8, 128), F32)),
        in_specs=[_HBM] * (2 * n) + [pl.BlockSpec(memory_space=pl.ANY)] * len(after),
        out_specs=(_SEM, _SEM, *[_HBM] * (2 * n), pl.BlockSpec(memory_space=pltpu.VMEM)),
        input_output_aliases={i: 2 + i for i in range(2 * n)},
        compiler_params=pltpu.CompilerParams(has_side_effects=_EFFECT),
    )(*[pltpu.with_memory_space_constraint(a, pltpu.HBM) for a in arrs],
      *[pltpu.with_memory_space_constraint(lax.empty(s, a.dtype), pltpu.HBM) for s, a in zip(land_shapes, arrs)],
      *after)
    return (res[0], res[1], list(res[2:2 + n]), list(res[2 + n:2 + 2 * n])), res[-1]


def exchange_wait(handle, after, name, scatter):
    send, recv, srcs, lands = handle
    n = len(srcs)
    after = list(after) if isinstance(after, (list, tuple)) else [after]

    def body(*refs):
        ins, lnd = refs[:n], refs[n:2 * n]
        send_, recv_ = refs[2 * n], refs[2 * n + 1]
        for cp in _split_copies(ins, lnd, send_, recv_, scatter, True):
            cp.wait_send()
            cp.wait_recv()

    res = pl.pallas_call(
        body, name=name,
        out_shape=(*[pltpu.HBM(a.shape, a.dtype) for a in srcs], *[pltpu.HBM(a.shape, a.dtype) for a in lands]),
        in_specs=[_HBM] * (2 * n) + [_SEM, _SEM] + [pl.BlockSpec(memory_space=pl.ANY)] * len(after),
        out_specs=tuple([_HBM] * (2 * n)),
        input_output_aliases={i: i for i in range(2 * n)},
        compiler_params=pltpu.CompilerParams(has_side_effects=_EFFECT),
    )(*srcs, *lands, send, recv, *after)
    return list(res[n:])


def _own_slot(land, own):
    return lax.dynamic_update_slice(land, own[None], (_index(_mesh_pos()),) + (0,) * own.ndim)


def cast_bf16(arrs, name):
    outs = []
    for i, a in enumerate(arrs):
        R, C = a.shape
        if R % 8 == 0:
            tr = _pick(R, (256, 128, 64, 32, 16, 8)) if R * C * 4 > (1 << 21) else R
            steps, spec = R // tr, pl.BlockSpec((tr, C), lambda i: (i, 0))
        else:
            steps, spec = C // 256, pl.BlockSpec((R, 256), lambda i: (0, i))

        def body(a_ref, o_ref):
            o_ref[...] = a_ref[...].astype(BF16)

        outs.append(pl.pallas_call(body, name=f"{name}_{i}", grid=(steps,), in_specs=[spec], out_specs=spec,
                                   out_shape=jax.ShapeDtypeStruct((R, C), BF16), compiler_params=_cparams())(a))
    return outs


def sum_slabs(parts, name):
    def fn(*a):
        outs = []
        for p in a:
            s = p[0].astype(F32)
            for k in range(1, N_DEV):
                s = s + p[k].astype(F32)
            outs.append(s)
        return tuple(outs)
    return list(small_call(fn, list(parts), [jax.ShapeDtypeStruct(p.shape[1:], F32) for p in parts], name))


def sum_parts(parts, name):
    def fn(p):
        g = p[0]
        for k in range(1, N_DEV):
            g = g + p[k]
        return (g,)
    return small_call(fn, [parts], [jax.ShapeDtypeStruct(parts.shape[1:], F32)], name)[0]


_SPLITS = np.cumsum([1536, 1536, 1536, 512, 512, 1024, 1024, 8, 8, 2048])[:-1].tolist()


def split_w_in(w):
    aq, ak, av, mq, mk, mv, mo, mi, mf, gates = jnp.split(w, _SPLITS, axis=1)
    R = w.shape[0]
    w_att = jnp.stack([aq.reshape(R, 12, 128), ak.reshape(R, 12, 128), av.reshape(R, 12, 128)], axis=2)
    gif = jnp.concatenate([mi.reshape(R, 4, 2), mf.reshape(R, 4, 2), jnp.zeros((R, 4, 124), w.dtype)], axis=2)
    w_ml = jnp.concatenate([mq.reshape(R, 4, 128), mk.reshape(R, 4, 128), mv.reshape(R, 4, 256),
                            mo.reshape(R, 4, 256), gif], axis=2)
    return w_att.reshape(R, ATT_COLS), w_ml.reshape(R, ML_COLS), gates


def merge_w_in(g_att, g_ml, g_gate):
    R = g_att.shape[0]
    a = g_att.reshape(R, 12, 3, 128)
    m = g_ml.reshape(R, 4, ML_PAIR_COLS)
    gif = m[:, :, 768:772]
    return jnp.concatenate([
        a[:, :, 0].reshape(R, 1536), a[:, :, 1].reshape(R, 1536), a[:, :, 2].reshape(R, 1536),
        m[:, :, 0:128].reshape(R, 512), m[:, :, 128:256].reshape(R, 512),
        m[:, :, 256:512].reshape(R, 1024), m[:, :, 512:768].reshape(R, 1024),
        gif[:, :, 0:2].reshape(R, 8), gif[:, :, 2:4].reshape(R, 8), g_gate], axis=1)


def _blk8(v, width=128):
    r = v.shape[0]
    return v.reshape(r, 1024 // width, width).transpose(1, 0, 2)


def _unblk8(v):
    nb, r, w = v.shape
    return v.transpose(1, 0, 2).reshape(r, nb * w)


def local_step(x, target, mods, w, small, late_w=None, early_g=None, w_in_g=None):
    late_w = late_w or (lambda after: w)
    big = {}
    early_g = early_g or (lambda g: big.update(g))
    w_in_g = w_in_g or (lambda g: big.update(w_in=merge_w_in(*g)))
    B = x.shape[0]
    T = B * SEQ
    shift1, scale1, gate1, shift2, scale2, gate2 = mods
    f2 = lambda a: a.reshape(T, a.shape[-1])
    f3 = lambda a: a.reshape(B, SEQ, a.shape[-1])

    rel_t = jnp.pad(small["rel_bias"].T, ((0, 4), (0, 0)))
    onehots = [_bucket_onehot(d) for _, d in ATT_GROUPS]
    biases = [b.reshape(16, ATT_BLOCK, 2 * ATT_BLOCK) for b in bias_expand(rel_t, onehots, "bias_expand")]
    qg, kg = small["q_norm_g"], small["k_norm_g"]
    cw8 = _blk8(small["conv_w"])
    cb8 = _blk8(small["conv_b"])
    b_if = small["b_if"].reshape(2, 4, 2)
    bifp = jnp.concatenate([b_if[0], b_if[1], jnp.zeros((4, 124), F32)], axis=1).reshape(4, 1, 128)
    gn4 = small["mlstm_norm_g"].reshape(4, 1, 256)

    u = modnorm_fwd(x, small["norm1_g"], scale1, shift1, "modnorm1")
    u2d = f2(u)
    pa = f3(matmul(u2d, w["w_att"], mode="nn", name="proj_att"))
    pm = f3(matmul(u2d, w["w_ml"], mode="nn", name="proj_ml"))
    pg = matmul(u2d, w["w_gate"], mode="nn", name="proj_gate")
    os_, ls_ = [], []
    one_block = [SEQ // d == ATT_BLOCK for _, d in ATT_GROUPS]
    for g in range(3):
        o, l = (attn_fwd_classes if one_block[g] else attn_fwd)(pa, biases[g], qg, kg, g, f"attn_fwd{g}")
        os_.append(f2(o))
        ls_.append(f2(l))
    hg = mlstm_fwd(pm, cw8, cb8, bifp, gn4, "mlstm_fwd")
    w = {**w, **late_w(hg)}
    att, y_att = merge_att_out(os_, ls_, w["w_att_out"], "att_out")
    y_ml, z = ml_out_gate(f2(hg), w["w_ml_out"], pg, y_att, "ml_out")
    y, x1, u2 = out_proj_resid_modnorm(f2(z), w["w_out"], f2(x), gate1, small["norm2_g"], scale2, shift2, "out_proj")
    pre, hdn = matmul(u2, w["w_ff1"], mode="nn", name="ff1", out_dtypes=(BF16, BF16),
                      epi=lambda acc: (acc, jnp.square(jnp.maximum(acc, 0.0))))
    dx2, d_ffo, loss, d_gate2 = ff2_loss(hdn, w["w_ff2"], x1, gate2, f2(target), "ff2_loss")

    g_ff2 = matmul(hdn, d_ffo, mode="tn", name="g_ff2", out_dtypes=(BF16,))
    d_pre = matmul(d_ffo, w["w_ff2"], mode="nt", name="d_hdn", out_dtypes=(BF16,), extras=(pre,),
                   epi=lambda acc, p: (acc * (2.0 * jnp.maximum(p.astype(F32), 0.0)),))
    g_ff1 = matmul(u2, d_pre, mode="tn", name="g_ff1", out_dtypes=(BF16,))
    dx1, d_norm2, d_scale2, d_shift2, dy, d_gate1 = d_u_modnorm_bwd(
        d_pre, w["w_ff1"], None, x1, small["norm2_g"], scale2, shift2, dx2, y, gate1, "d_u2")
    g_out = matmul(f2(z), dy, mode="tn", name="g_out", out_dtypes=(BF16,))
    dpg, d_ya, d_ym = d_z_gate_bwd(dy, w["w_out"], pg, y_att, y_ml, "d_z")
    g_att_out = matmul(f2(att), f2(d_ya), mode="tn", name="g_att_out", out_dtypes=(BF16,))
    dmerge = d_att_merge_bwd(d_ya, w["w_att_out"], os_, ls_, "d_att")
    g_ml_out = matmul(f2(hg), f2(d_ym), mode="tn", name="g_ml_out", out_dtypes=(BF16,))
    d_hg = matmul(f2(d_ym), w["w_ml_out"], mode="nt", name="d_hg")
    started = early_g(dict(w_att_out=g_att_out, w_ml_out=g_ml_out, w_out=g_out, w_ff1=g_ff1, w_ff2=g_ff2))
    order = 0.0 if started is None else started[0, 0]
    dmerge = [f3(d) for d in dmerge]
    dpa = lax.empty((B, SEQ, ATT_COLS), BF16)
    d_rel = []
    d_qg = d_kg = None
    for g in range(3):
        dpa, dbias, dq_g, dk_g = (attn_bwd_classes if one_block[g] else attn_bwd)(
            pa, biases[g], qg + order, kg, dmerge[g], dmerge[3 + g], dpa, g, f"attn_bwd{g}")
        db8 = jnp.pad(dbias.reshape(4, -1), ((0, 4), (0, 0)))
        d_rel.append(bias_reduce(db8, onehots[g], f"bias_reduce{g}")[:4])
        d_qg = dq_g if d_qg is None else d_qg + dq_g
        d_kg = dk_g if d_kg is None else d_kg + dk_g
    dpm, dcw8, dcb8, dbifp, dgn4 = mlstm_bwd(pm, cw8, cb8, bifp, gn4 + order, f3(d_hg), "mlstm_bwd")
    g_w_att = matmul(u2d, f2(dpa), mode="tn", name="g_w_att", out_dtypes=(BF16,))
    g_w_ml = matmul(u2d, f2(dpm), mode="tn", name="g_w_ml", out_dtypes=(BF16,))
    g_w_gate = matmul(u2d, f2(dpg), mode="tn", name="g_w_gate", out_dtypes=(BF16,))
    started = w_in_g((g_w_att, g_w_ml, g_w_gate))
    du = matmul(f2(dpa), w["w_att"], mode="nt", name="d_u_att", after=() if started is None else (started,))
    du = matmul(f2(dpm), w["w_ml"], mode="nt", name="d_u_ml", extras=(du,), epi=lambda acc, e: (acc + e,))
    grad_x, d_norm1, d_scale1, d_shift1 = d_u_modnorm_bwd(
        f2(dpg), w["w_gate"], du, f2(x), small["norm1_g"], scale1, shift1, dx1, None, None, "d_u_gate")
    grad_x = f3(grad_x)

    d_mods = (d_shift1, d_scale1, d_gate1, d_shift2, d_scale2, d_gate2)
    dbif = dbifp.reshape(4, 128)
    small_g = dict(
        norm1_g=d_norm1, norm2_g=d_norm2,
        b_if=jnp.stack([dbif[:, 0:2].reshape(8), dbif[:, 2:4].reshape(8)]),
        conv_w=_unblk8(dcw8), conv_b=_unblk8(dcb8), q_norm_g=d_qg, k_norm_g=d_kg,
        rel_bias=jnp.concatenate(d_rel, axis=0).T,
        mlstm_norm_g=dgn4.reshape(1, 1024))
    return loss, grad_x, d_mods, big, small_g


_SMALL = (("b_ada", 6144), ("norm1_g", 1024), ("norm2_g", 1024), ("b_if", 16), ("conv_b", 1024),
          ("q_norm_g", 128), ("k_norm_g", 128), ("rel_bias", 384), ("mlstm_norm_g", 1024), ("conv_w", 4096))
_SMALL_ROWS = 120
_REPL = _SMALL[:-1]
_SMALL_SENT = _SMALL + (("loss", 1),)


def _pack(d, names, rows):
    flat = jnp.concatenate([d[k].reshape(-1) for k, _ in names])
    return jnp.pad(flat, (0, rows * 128 - flat.shape[0])).reshape(rows, 128)


def _unpack(slab, names, shapes):
    flat = slab.reshape(-1)
    out, off = {}, 0
    for k, nel in names:
        out[k] = flat[off:off + nel].reshape(shapes[k])
        off += nel
    return out


def kernel(x, c, w_ada, b_ada, norm1_g, norm2_g, w_in, b_if, conv_w, conv_b, q_norm_g, k_norm_g, rel_bias, mlstm_norm_g, w_att_out, w_ml_out, w_out, w_ff1, w_ff2, loss_target, m_w_ada, m_b_ada, m_norm1_g, m_norm2_g, m_w_in, m_b_if, m_conv_w, m_conv_b, m_q_norm_g, m_k_norm_g, m_rel_bias, m_mlstm_norm_g, m_w_att_out, m_w_ml_out, m_w_out, m_w_ff1, m_w_ff2, v_w_ada, v_b_ada, v_norm1_g, v_norm2_g, v_w_in, v_b_if, v_conv_w, v_conv_b, v_q_norm_g, v_k_norm_g, v_rel_bias, v_mlstm_norm_g, v_w_att_out, v_w_ml_out, v_w_out, v_w_ff1, v_w_ff2):
    P = dict(w_ada=w_ada, b_ada=b_ada, norm1_g=norm1_g, norm2_g=norm2_g, w_in=w_in, b_if=b_if, conv_w=conv_w,
             conv_b=conv_b, q_norm_g=q_norm_g, k_norm_g=k_norm_g, rel_bias=rel_bias, mlstm_norm_g=mlstm_norm_g,
             w_att_out=w_att_out, w_ml_out=w_ml_out, w_out=w_out, w_ff1=w_ff1, w_ff2=w_ff2)
    M = dict(w_ada=m_w_ada, b_ada=m_b_ada, norm1_g=m_norm1_g, norm2_g=m_norm2_g, w_in=m_w_in, b_if=m_b_if,
             conv_w=m_conv_w, conv_b=m_conv_b, q_norm_g=m_q_norm_g, k_norm_g=m_k_norm_g, rel_bias=m_rel_bias,
             mlstm_norm_g=m_mlstm_norm_g, w_att_out=m_w_att_out, w_ml_out=m_w_ml_out, w_out=m_w_out,
             w_ff1=m_w_ff1, w_ff2=m_w_ff2)
    V = dict(w_ada=v_w_ada, b_ada=v_b_ada, norm1_g=v_norm1_g, norm2_g=v_norm2_g, w_in=v_w_in, b_if=v_b_if,
             conv_w=v_conv_w, conv_b=v_conv_b, q_norm_g=v_q_norm_g, k_norm_g=v_k_norm_g, rel_bias=v_rel_bias,
             mlstm_norm_g=v_mlstm_norm_g, w_att_out=v_w_att_out, w_ml_out=v_w_ml_out, w_out=v_w_out,
             w_ff1=v_w_ff1, w_ff2=v_w_ff2)
    names = list(P)
    shapes = {k: P[k].shape for k in names}
    B = x.shape[0]
    me = _index(_mesh_pos())

    big_names = ("w_in", "w_att_out", "w_ml_out", "w_out", "w_ff1", "w_ff2")
    shards = cast_bf16([P[k][0] for k in big_names], "cast_w")
    (rows8,) = all_to_all([shards[0].reshape(N_DEV, D_MODEL // N_DEV, W_IN_SHARD)], "w_in_rows_exchange")
    slab = split_w_in(rows8.transpose(1, 0, 2).reshape(D_MODEL // N_DEV, D_IN))
    a8, m8, g8, c8, conv_w8 = all_gather_two_level(list(slab) + [c, conv_w[0]], "gather_w_in")
    c_all = c8.reshape(N_DEV * B, D_MODEL)
    conv_w_full = conv_w8.transpose(1, 0, 2).reshape(4, 1024)
    w = dict(w_att=a8.reshape(D_MODEL, ATT_COLS), w_ml=m8.reshape(D_MODEL, ML_COLS),
             w_gate=g8.reshape(D_MODEL, GATE_COLS))

    (silu_c,) = small_call(lambda a: (_silu(a),), [c_all], [jax.ShapeDtypeStruct(c_all.shape, F32)], "silu_c")
    b_ada_cols = lax.dynamic_slice(b_ada, (0, me * 768), (1, 768))
    ada_cols = matmul(silu_c, w_ada[0], mode="nn", name="ada", extras=(jnp.broadcast_to(b_ada_cols, (N_DEV * B, 768)),),
                      epi=lambda acc, bb: (acc + bb,))
    (ada_t,) = all_to_all([ada_cols.reshape(N_DEV, B, 768)], "ada_exchange")
    ada = ada_t.transpose(1, 0, 2).reshape(B, 6 * D_MODEL)
    mods = tuple(ada[:, i * D_MODEL:(i + 1) * D_MODEL].reshape(B, 1, D_MODEL) for i in range(6))

    late_handle, late_order = exchange_start(shards[1:], "gather_late_start", False, after=(ada_t,))

    def late_w(after):
        lands = exchange_wait(late_handle, after, "gather_late_wait", False)
        gw = dict(zip(big_names[1:], [_own_slot(l, s) for l, s in zip(lands, shards[1:])]))
        return dict(w_att_out=gw["w_att_out"].transpose(1, 0, 2).reshape(512, D_MODEL),
                    w_ml_out=gw["w_ml_out"].reshape(D_MODEL, D_MODEL), w_out=gw["w_out"].reshape(D_MODEL, D_MODEL),
                    w_ff1=gw["w_ff1"].transpose(1, 0, 2).reshape(D_MODEL, D_FF),
                    w_ff2=gw["w_ff2"].reshape(D_FF, D_MODEL))

    pending = {}

    def send_grads(key, blocks, name):
        handle, order = exchange_start(blocks, name, True)
        pending[key] = (handle, [lax.dynamic_index_in_dim(b, me, 0, keepdims=False) for b in blocks])
        return order

    def early_g(g):
        return send_grads("late", [g["w_att_out"].reshape(512, N_DEV, 128).transpose(1, 0, 2),
                                   g["w_ml_out"].reshape(N_DEV, 128, D_MODEL), g["w_out"].reshape(N_DEV, 128, D_MODEL),
                                   g["w_ff1"].reshape(D_MODEL, N_DEV, 512).transpose(1, 0, 2),
                                   g["w_ff2"].reshape(N_DEV, 512, D_MODEL)], "grad_late_start")

    def w_in_g(parts):
        return send_grads("w_in", [g.reshape(N_DEV, D_MODEL // N_DEV, g.shape[1]) for g in parts], "grad_w_in_start")

    def recv_grads(key, after, name):
        handle, own = pending[key]
        return [_own_slot(l, o) for l, o in zip(exchange_wait(handle, after, name, True), own)]

    small = dict(norm1_g=norm1_g + late_order[0, 0], norm2_g=norm2_g, b_if=b_if[0], conv_w=conv_w_full, conv_b=conv_b,
                 q_norm_g=q_norm_g, k_norm_g=k_norm_g, rel_bias=rel_bias, mlstm_norm_g=mlstm_norm_g)
    loss, grad_x, d_mods, _, small_g = local_step(x, loss_target, mods, w, small, late_w, early_g, w_in_g)

    d_ada = jnp.concatenate([d.reshape(B, D_MODEL) for d in d_mods], axis=1)
    (small_g["b_ada"],) = small_call(lambda a: (jnp.sum(a, axis=0, keepdims=True),), [d_ada],
                                     [jax.ShapeDtypeStruct((1, 6144), F32)], "g_b_ada_local")
    small_g["loss"] = loss
    slabs = sum_slabs(recv_grads("w_in", grad_x, "grad_w_in_wait"), "w_in_slab_sum")
    w_in_cols = merge_w_in(*slabs).reshape(D_MODEL // N_DEV, N_DEV, W_IN_SHARD).transpose(1, 0, 2)
    d_ada_t, small_parts = _exchange(
        [d_ada.reshape(B, N_DEV, 768).transpose(1, 0, 2), _pack(small_g, _SMALL_SENT, _SMALL_ROWS)],
        "small_exchange", [True, False])
    cols_handle, _ = exchange_start([w_in_cols], "grad_w_in_cols_start", True, after=(small_parts,))
    d_ada_cols = d_ada_t.reshape(N_DEV * B, 768)
    g_w_ada = matmul(silu_c, d_ada_cols, mode="tn", name="g_w_ada")

    recv = recv_grads("late", grad_x, "grad_late_wait")
    small_sum = sum_parts(small_parts, "small_grad_sum")
    sg = _unpack(small_sum, _SMALL_SENT, {**{k: shapes[k] for k, _ in _REPL}, "conv_w": (4, 1024), "loss": ()})
    loss = sg["loss"]

    G, Dl, NM, NV = {}, {}, {}, {}
    for k, parts in zip(big_names[1:], recv):
        g, d, nm, nv = adamw(P[k][0], parts, M[k][0], V[k][0], f"adamw_{k}", parts=True)
        G[k], Dl[k], NM[k], NV[k] = g[None], d[None], nm[None], nv[None]
    g, d, nm, nv = adamw(w_ada[0], g_w_ada, m_w_ada[0], v_w_ada[0], "adamw_w_ada")
    G["w_ada"], Dl["w_ada"], NM["w_ada"], NV["w_ada"] = g[None], d[None], nm[None], nv[None]
    g_conv = lax.dynamic_slice(sg["conv_w"], (0, me * 128), (4, 128))
    g, d, nm, nv = adamw(conv_w[0], g_conv, m_conv_w[0], v_conv_w[0], "adamw_conv_w")
    G["conv_w"], Dl["conv_w"], NM["conv_w"], NV["conv_w"] = g[None], d[None], nm[None], nv[None]
    flat2 = lambda a: a.reshape(-1, a.shape[-1])
    keys = [k for k, _ in _REPL]
    upd = adamw_many([flat2(P[k]) for k in keys], [flat2(sg[k]) for k in keys], [flat2(M[k]) for k in keys],
                     [flat2(V[k]) for k in keys], "adamw_small")
    for j, k in enumerate(keys):
        G[k] = sg[k]
        Dl[k], NM[k], NV[k] = [upd[3 * j + t].reshape(shapes[k]) for t in range(3)]

    others = [NV[k] for k in big_names[1:]] + [NV["w_ada"], NV["conv_w"], upd[-1]]
    (g_w_in,) = exchange_wait(cols_handle, others, "grad_w_in_cols_wait", True)
    g_w_in = _own_slot(g_w_in, lax.dynamic_index_in_dim(w_in_cols, me, 0, keepdims=False))
    g, d, nm, nv = adamw(w_in[0], g_w_in.reshape(D_MODEL, W_IN_SHARD), m_w_in[0], v_w_in[0], "adamw_w_in")
    G["w_in"], Dl["w_in"], NM["w_in"], NV["w_in"] = g[None], d[None], nm[None], nv[None]

    return (loss, grad_x, *[G[k] for k in names], *[Dl[k] for k in names], *[NM[k] for k in names],
            *[NV[k] for k in names])
```

```python
import functools
import math

import numpy as np
import jax
import jax.numpy as jnp
from jax import lax
from jax.experimental import pallas as pl
from jax.experimental.pallas import tpu as pltpu

F32 = jnp.float32
BF16 = jnp.bfloat16

N_DEV = 8
D_MODEL = 1024
SEQ = 2048
ATT_GROUPS = ((128, 1), (512, 4), (2048, 16))
N_ATT_HEADS = 12
ATT_BLOCK = 128
HEAD_DIM = 128
ML_HEADS = 8
ML_PAIRS = 4
ML_CHUNK = 64
N_CHUNKS = SEQ // ML_CHUNK
N_BUCKETS = 32
MAX_DISTANCE = 2048
D_FF = 4096
D_IN = 9744
EPS = 1e-6

ADAM_LR = 0.001
ADAM_B1 = 0.9
ADAM_B2 = 0.999
ADAM_EPS = 1e-08
ADAM_WD = 0.01
ADAM_STEP = 10

ATT_HEAD_COLS = 3 * HEAD_DIM
ATT_COLS = N_ATT_HEADS * ATT_HEAD_COLS
ML_PAIR_COLS = 896
ML_COLS = ML_PAIRS * ML_PAIR_COLS
GATE_COLS = 2 * D_MODEL
W_IN_SHARD = D_IN // N_DEV

VMEM_LIMIT = 60 * 1024 * 1024


def _cparams(**kw):
    return pltpu.CompilerParams(vmem_limit_bytes=VMEM_LIMIT, **kw)


_NN = ((1,), (0,))
_NT = ((1,), (1,))
_TN = ((0,), (0,))


def _mxu(a, b, dims):
    return lax.dot_general(a.astype(BF16), b.astype(BF16), (dims, ((), ())), preferred_element_type=F32)


@jax.custom_vjp
def bdot_nn(a, b):
    return _mxu(a, b, _NN)


def _nn_fwd(a, b):
    return _mxu(a, b, _NN), (a, b)


def _nn_bwd(res, g):
    a, b = res
    return _mxu(g, b, _NT), _mxu(a, g, _TN)


bdot_nn.defvjp(_nn_fwd, _nn_bwd)


@jax.custom_vjp
def bdot_nt(a, b):
    return _mxu(a, b, _NT)


def _nt_fwd(a, b):
    return _mxu(a, b, _NT), (a, b)


def _nt_bwd(res, g):
    a, b = res
    return _mxu(g, b, _NN), _mxu(g, a, _TN)


bdot_nt.defvjp(_nt_fwd, _nt_bwd)


def _doth(a, b, dims=_NN):
    return lax.dot_general(a, b, (dims, ((), ())), precision=lax.Precision.HIGHEST, preferred_element_type=F32)


def _rms(x):
    return x * lax.rsqrt(jnp.mean(x * x, axis=-1, keepdims=True) + EPS)


def _pick(n, cands):
    for t in cands:
        if n % t == 0:
            return t
    raise ValueError(f"no tile for {n}")


MM_TILE_M = (1024, 512, 256, 128, 64, 32, 16, 8)
MM_TILE_N = (2048, 1792, 1536, 1024, 768, 512, 256, 128)
MM_TILE_K = (2048, 1792, 1536, 1024, 512, 256, 128, 64, 32)

def matmul(a, b, *, mode, name, out_dtypes=(F32,), epi=None, extras=(), after=()):
    if mode == "nn":
        (M, K), (K2, N) = a.shape, b.shape
    elif mode == "nt":
        (M, K), (N, K2) = a.shape, b.shape
    else:
        (K, M), (K2, N) = a.shape, b.shape
    assert K == K2, (a.shape, b.shape, mode)
    tm = _pick(M, MM_TILE_M)
    tn = _pick(N, MM_TILE_N)
    tk = _pick(K, MM_TILE_K)
    nk = K // tk
    n_ex = len(extras)
    n_out = len(out_dtypes)
    dims = {"nn": _NN, "nt": _NT, "tn": _TN}[mode]

    def finish(r, ex_refs, out_refs):
        outs = epi(r, *[e[...] for e in ex_refs]) if epi is not None else (r,)
        for o_ref, o in zip(out_refs, outs):
            o_ref[...] = o.astype(o_ref.dtype)

    def body(*refs):
        a_ref, b_ref = refs[0], refs[1]
        ex_refs = refs[2:2 + n_ex]
        out_refs = refs[2 + n_ex + len(after):2 + n_ex + len(after) + n_out]
        if nk == 1:
            finish(_mxu(a_ref[...], b_ref[...], dims), ex_refs, out_refs)
            return
        acc = refs[2 + n_ex + len(after) + n_out]
        k = pl.program_id(2)

        @pl.when(k == 0)
        def _():
            acc[...] = jnp.zeros_like(acc)

        acc[...] += _mxu(a_ref[...], b_ref[...], dims)

        @pl.when(k == nk - 1)
        def _():
            finish(acc[...], ex_refs, out_refs)

    if mode == "nn":
        a_spec = pl.BlockSpec((tm, tk), lambda i, j, k: (i, k))
        b_spec = pl.BlockSpec((tk, tn), lambda i, j, k: (k, j))
    elif mode == "nt":
        a_spec = pl.BlockSpec((tm, tk), lambda i, j, k: (i, k))
        b_spec = pl.BlockSpec((tn, tk), lambda i, j, k: (j, k))
    else:
        a_spec = pl.BlockSpec((tk, tm), lambda i, j, k: (k, i))
        b_spec = pl.BlockSpec((tk, tn), lambda i, j, k: (k, j))
    o_spec = pl.BlockSpec((tm, tn), lambda i, j, k: (i, j))
    res = pl.pallas_call(
        body,
        name=name,
        grid=(M // tm, N // tn, nk),
        in_specs=[a_spec, b_spec] + [o_spec] * n_ex + [pl.BlockSpec(memory_space=pl.ANY)] * len(after),
        out_specs=[o_spec] * n_out,
        out_shape=[jax.ShapeDtypeStruct((M, N), dt) for dt in out_dtypes],
        scratch_shapes=[pltpu.VMEM((tm, tn), F32)] if nk > 1 else [],
        compiler_params=_cparams(),
    )(a, b, *extras, *after)
    return res[0] if n_out == 1 else tuple(res)


ROW_MM_TILE = 512


def row_matmul(a, b, *, mode, name, extras, outs, epi):
    (M, K) = a.shape
    N = b.shape[1] if mode == "nn" else b.shape[0]
    tm = ROW_MM_TILE
    tk = _pick(K, MM_TILE_K)
    nk = K // tk
    n_ex, n_out = len(extras), len(outs)

    def body(*refs):
        a_ref, b_ref = refs[0], refs[1]
        ex_refs = refs[2:2 + n_ex]
        out_refs = refs[2 + n_ex:2 + n_ex + n_out]
        i = pl.program_id(0)
        dims = _NN if mode == "nn" else _NT
        if nk == 1:
            epi(_mxu(a_ref[...], b_ref[...], dims), i, ex_refs, out_refs)
            return
        acc = refs[2 + n_ex + n_out]
        k = pl.program_id(1)

        @pl.when(k == 0)
        def _():
            acc[...] = jnp.zeros_like(acc)

        acc[...] += _mxu(a_ref[...], b_ref[...], dims)

        @pl.when(k == nk - 1)
        def _():
            epi(acc[...], i, ex_refs, out_refs)

    def lift(index_map):
        return lambda i, k: index_map(i)

    b_spec = pl.BlockSpec((tk, N), lambda i, k: (k, 0)) if mode == "nn" else pl.BlockSpec((N, tk), lambda i, k: (0, k))
    res = pl.pallas_call(
        body, name=name, grid=(M // tm, nk),
        in_specs=[pl.BlockSpec((tm, tk), lambda i, k: (i, k)), b_spec]
        + [pl.BlockSpec(blk, lift(im)) for _, blk, im in extras],
        out_specs=[pl.BlockSpec(blk, lift(im)) for _, _, blk, im in outs],
        out_shape=[jax.ShapeDtypeStruct(shape, dt) for shape, dt, _, _ in outs],
        scratch_shapes=[pltpu.VMEM((tm, N), F32)] if nk > 1 else [],
        compiler_params=_cparams(),
    )(a, b, *[e[0] for e in extras])
    return tuple(res)


def _rows(arr):
    return (arr, (ROW_MM_TILE, arr.shape[1]), lambda i: (i, 0))


def _rows_out(T, dtype):
    return ((T, D_MODEL), dtype, (ROW_MM_TILE, D_MODEL), lambda i: (i, 0))


def _per_seq(arr):
    return (arr, (1, 1, D_MODEL), lambda i: (i // (SEQ // ROW_MM_TILE), 0, 0))


def _per_seq_out(B):
    return ((B, 1, D_MODEL), F32, (1, 1, D_MODEL), lambda i: (i // (SEQ // ROW_MM_TILE), 0, 0))


def _first_tile_of_seq(i):
    return i % (SEQ // ROW_MM_TILE) == 0


def small_call(fn, inputs, out_shapes, name):
    n_in = len(inputs)

    def body(*refs):
        outs = fn(*[r[...] for r in refs[:n_in]])
        for o_ref, o in zip(refs[n_in:], outs):
            o_ref[...] = o.astype(o_ref.dtype)

    res = pl.pallas_call(body, name=name, out_shape=list(out_shapes), compiler_params=_cparams())(*inputs)
    return tuple(res)


ROW_TILE = 512


def _modnorm(x, g, scale, shift):
    return _rms(x) * g * (1.0 + scale) + shift


def _row_spec(width):
    return pl.BlockSpec((1, ROW_TILE, width), lambda b, i: (b, i, 0))


def _mod_spec():
    return pl.BlockSpec((1, 1, D_MODEL), lambda b, i: (b, 0, 0))


def _vec_spec():
    return pl.BlockSpec((1, D_MODEL), lambda b, i: (0, 0))


def modnorm_fwd(x, g, scale, shift, name):
    B, S, D = x.shape

    def body(x_ref, g_ref, sc_ref, sh_ref, u_ref):
        u_ref[0] = _modnorm(x_ref[0], g_ref[...], sc_ref[0], sh_ref[0]).astype(BF16)

    return pl.pallas_call(
        body, name=name, grid=(B, S // ROW_TILE),
        in_specs=[_row_spec(D), _vec_spec(), _mod_spec(), _mod_spec()],
        out_specs=_row_spec(D),
        out_shape=jax.ShapeDtypeStruct((B, S, D), BF16),
        compiler_params=_cparams(),
    )(x, g, scale, shift)


def _gain_spec(g):
    return (g, (1, D_MODEL), lambda i: (0, 0))


def out_proj_resid_modnorm(z, w_out, x, gate, g, scale, shift, name):
    T = z.shape[0]

    def epi(acc, i, ex, out):
        x_ref, gt_ref, g_ref, sc_ref, sh_ref = ex
        y_ref, x1_ref, u_ref = out
        y_ref[...] = acc
        x1 = x_ref[...] + gt_ref[0] * acc
        x1_ref[...] = x1
        u_ref[...] = _modnorm(x1, g_ref[...], sc_ref[0], sh_ref[0]).astype(BF16)

    return row_matmul(z, w_out, mode="nn", name=name,
                      extras=[_rows(x), _per_seq(gate), _gain_spec(g), _per_seq(scale), _per_seq(shift)],
                      outs=[_rows_out(T, F32), _rows_out(T, F32), _rows_out(T, BF16)], epi=epi)


def ff2_loss(hdn, w_ff2, x1, gate, target, name):
    T = hdn.shape[0]

    def epi(acc, i, ex, out):
        x_ref, gt_ref, t_ref = ex
        dx_ref, dffo_ref, loss_ref, dg_ref = out

        @pl.when(i == 0)
        def _():
            loss_ref[...] = jnp.zeros_like(loss_ref)

        @pl.when(_first_tile_of_seq(i))
        def _():
            dg_ref[...] = jnp.zeros_like(dg_ref)

        err = x_ref[...] + gt_ref[0] * acc - t_ref[...]
        dx = err * (1.0 / D_MODEL)
        dx_ref[...] = dx
        dffo_ref[...] = (gt_ref[0] * dx).astype(BF16)
        loss_ref[...] += 0.5 * jnp.sum(jnp.mean(err * err, axis=-1, keepdims=True), axis=0, keepdims=True)
        dg_ref[0] += jnp.sum(dx * acc, axis=0, keepdims=True)

    return row_matmul(hdn, w_ff2, mode="nn", name=name,
                      extras=[_rows(x1), _per_seq(gate), _rows(target)],
                      outs=[_rows_out(T, F32), _rows_out(T, BF16), ((1, 1), F32, (1, 1), lambda i: (0, 0)),
                            _per_seq_out(T // SEQ)], epi=epi)


def d_u_modnorm_bwd(a, w, du_prev, x, g, scale, shift, dx_res, y, gate, name):
    T = a.shape[0]
    B = T // SEQ
    n_prev, resid = int(du_prev is not None), y is not None

    def epi(acc, i, ex, out):
        x_ref, g_ref, sc_ref, sh_ref, dr_ref = ex[n_prev:n_prev + 5]
        dx_ref, dg_ref, dsc_ref, dsh_ref = out[:4]

        @pl.when(i == 0)
        def _():
            dg_ref[...] = jnp.zeros_like(dg_ref)

        @pl.when(_first_tile_of_seq(i))
        def _():
            for r in out[2:4] + out[5:]:
                r[...] = jnp.zeros_like(r)

        du = acc + ex[0][...] if n_prev else acc
        _, vjp = jax.vjp(_modnorm, x_ref[...], g_ref[...], sc_ref[0], sh_ref[0])
        dxn, dg, dsc, dsh = vjp(du)
        dx = dxn + dr_ref[...]
        dx_ref[...] = dx
        dg_ref[...] += dg
        dsc_ref[0] += dsc
        dsh_ref[0] += dsh
        if resid:
            y_ref, gt_ref = ex[n_prev + 5:]
            out[4][...] = (gt_ref[0] * dx).astype(BF16)
            out[5][0] += jnp.sum(dx * y_ref[...], axis=0, keepdims=True)

    extras = ([_rows(du_prev)] if n_prev else []) + [_rows(x), _gain_spec(g), _per_seq(scale), _per_seq(shift),
                                                     _rows(dx_res)] + ([_rows(y), _per_seq(gate)] if resid else [])
    outs = [_rows_out(T, F32), ((1, D_MODEL), F32, (1, D_MODEL), lambda i: (0, 0)), _per_seq_out(B), _per_seq_out(B)]
    outs += [_rows_out(T, BF16), _per_seq_out(B)] if resid else []
    return row_matmul(a, w, mode="nt", name=name, extras=extras, outs=outs, epi=epi)


def _bucket_table(dilation):
    i = np.arange(ATT_BLOCK)[:, None]
    j = np.arange(2 * ATT_BLOCK)[None, :]
    delta = ATT_BLOCK + i - j
    dist = np.maximum(delta, 0) * dilation
    max_exact = N_BUCKETS // 2
    d = np.maximum(dist, max_exact).astype(np.float32)
    large = max_exact + (np.log(d / np.float32(max_exact)) / np.float32(math.log(MAX_DISTANCE / max_exact))
                         * np.float32(N_BUCKETS - max_exact)).astype(np.int32)
    large = np.minimum(large, N_BUCKETS - 1)
    return np.where(dist < max_exact, dist, large).astype(np.int32)


def _bucket_onehot(dilation):
    bt = jnp.asarray(_bucket_table(dilation).reshape(1, -1))
    return (bt == jnp.arange(N_BUCKETS, dtype=jnp.int32)[:, None]).astype(F32)


def bias_expand(rel_t, onehot, name):
    def fn(r, oh):
        return (_doth(r, oh),)
    return small_call(fn, [rel_t, onehot], [jax.ShapeDtypeStruct((rel_t.shape[0], onehot.shape[1]), F32)], name)[0]


def bias_reduce(dbias_flat, onehot, name):
    def fn(db, oh):
        return (_doth(db, oh, _NT),)
    return small_call(fn, [dbias_flat, onehot], [jax.ShapeDtypeStruct((dbias_flat.shape[0], N_BUCKETS), F32)], name)[0]


def _qk_norm(x, g):
    return _rms(x) * g


def _masked_bias(bias):
    i = lax.broadcasted_iota(jnp.int32, (ATT_BLOCK, 2 * ATT_BLOCK), 0)
    j = lax.broadcasted_iota(jnp.int32, (ATT_BLOCK, 2 * ATT_BLOCK), 1)
    bm = jnp.where(jnp.logical_and(j >= i, j <= i + ATT_BLOCK), bias, -jnp.inf)
    return bm, bm[:, ATT_BLOCK:]


def _attn_tile(qn, kn, v, bias):
    s = bdot_nt(qn, kn) * (HEAD_DIM ** -0.5) + bias
    m = lax.stop_gradient(jnp.max(s, axis=-1, keepdims=True))
    p = jnp.exp(s - m)
    l = jnp.sum(p, axis=-1, keepdims=True)
    o = bdot_nn(p, v) / l
    lse = jnp.broadcast_to(m + jnp.log(l), (ATT_BLOCK, HEAD_DIM))
    return o, lse


def _attn_tiles(dilation, rows=SEQ):
    nb = rows // dilation // ATT_BLOCK
    return [(r, n) for r in range(dilation) for n in range(nb)]


def _attn_rows(r, n, dilation, nblk=1):
    if dilation == 1:
        return pl.ds(r + n * ATT_BLOCK, nblk * ATT_BLOCK)
    return pl.ds(r + n * ATT_BLOCK * dilation, nblk * ATT_BLOCK, stride=dilation)


_QL, _KL, _VL = slice(0, 128), slice(128, 256), slice(256, 384)


def _qkv_specs(hb):
    return [pl.BlockSpec((None, SEQ, HEAD_DIM), functools.partial(lambda b, h, j: (b, 0, 3 * (hb + h) + j), j=j))
            for j in range(3)]


def attn_fwd(pa, bias, qg, kg, group, name):
    B = pa.shape[0]
    dilation = ATT_GROUPS[group][1]
    hb = group * 4

    def body(q_ref, k_ref, v_ref, b_ref, qg_ref, kg_ref, o_ref, l_ref, qn_s, kn_s):
        qn_s[...] = _qk_norm(q_ref[...], qg_ref[...])
        kn_s[...] = _qk_norm(k_ref[...], kg_ref[...])
        bias_all, bias_first = _masked_bias(b_ref[0])
        for (r, n) in _attn_tiles(dilation):
            rows = _attn_rows(r, n, dilation)
            if n == 0:
                krows, bias_t = rows, bias_first
            else:
                krows, bias_t = _attn_rows(r, n - 1, dilation, 2), bias_all
            o, lse = _attn_tile(qn_s[rows, :], kn_s[krows, :], v_ref[krows, :], bias_t)
            o_ref[rows, :] = o
            l_ref[rows, :] = lse

    head_out = pl.BlockSpec((None, SEQ, HEAD_DIM), lambda b, h: (b, 0, h))
    return pl.pallas_call(
        body, name=name, grid=(B, 4),
        in_specs=_qkv_specs(hb) + [
                  pl.BlockSpec((1, ATT_BLOCK, 2 * ATT_BLOCK), lambda b, h: (hb + h, 0, 0)),
                  pl.BlockSpec((1, HEAD_DIM), lambda b, h: (0, 0)),
                  pl.BlockSpec((1, HEAD_DIM), lambda b, h: (0, 0))],
        out_specs=[head_out, head_out],
        out_shape=[jax.ShapeDtypeStruct((B, SEQ, 512), F32), jax.ShapeDtypeStruct((B, SEQ, 512), F32)],
        scratch_shapes=[pltpu.VMEM((SEQ, HEAD_DIM), F32)] * 2,
        compiler_params=_cparams(),
    )(pa, pa, pa, bias, qg, kg)


def attn_bwd(pa, bias, qg, kg, do, dlse, dpa, group, name):
    B = pa.shape[0]
    dilation = ATT_GROUPS[group][1]
    hb = group * 4

    def body(q_ref, k_ref, v_ref, b_ref, qg_ref, kg_ref, do_ref, dl_ref, dpa_in,
             dp_ref, db_ref, dqg_ref, dkg_ref, qn_s, kn_s, dq_s, dk_s, dv_s):
        del dpa_in
        h_id = pl.program_id(1)

        @pl.when(jnp.logical_and(pl.program_id(0) == 0, h_id == 0))
        def _():
            db_ref[...] = jnp.zeros_like(db_ref)
            dqg_ref[...] = jnp.zeros_like(dqg_ref)
            dkg_ref[...] = jnp.zeros_like(dkg_ref)

        dk_s[...] = jnp.zeros_like(dk_s)
        dv_s[...] = jnp.zeros_like(dv_s)
        qn_s[...] = _qk_norm(q_ref[...], qg_ref[...])
        kn_s[...] = _qk_norm(k_ref[...], kg_ref[...])
        bias_all, bias_first = _masked_bias(b_ref[0])
        for (r, n) in _attn_tiles(dilation):
            rows = _attn_rows(r, n, dilation)
            if n == 0:
                krows, bias_t = rows, bias_first
            else:
                krows, bias_t = _attn_rows(r, n - 1, dilation, 2), bias_all
            _, vjp = jax.vjp(_attn_tile, qn_s[rows, :], kn_s[krows, :], v_ref[krows, :], bias_t)
            dqn, dkn, dv, dbias = vjp((do_ref[rows, :], dl_ref[rows, :]))
            dq_s[rows, :] = dqn
            dk_s[krows, :] += dkn
            dv_s[krows, :] += dv
            if n == 0:
                db_ref[h_id, :, ATT_BLOCK:] += dbias
            else:
                db_ref[h_id] += dbias
        for x_ref, g_ref, d_s, dg_ref, lanes in ((q_ref, qg_ref, dq_s, dqg_ref, _QL), (k_ref, kg_ref, dk_s, dkg_ref, _KL)):
            _, vjp = jax.vjp(_qk_norm, x_ref[...], g_ref[...])
            dx, dg = vjp(d_s[...])
            dp_ref[0, :, lanes] = dx.astype(BF16)
            dg_ref[...] += dg
        dp_ref[0, :, _VL] = dv_s[...].astype(BF16)

    const2 = lambda b, h: (0, 0)
    head_in = pl.BlockSpec((None, SEQ, HEAD_DIM), lambda b, h: (b, 0, h))
    head_blk = pl.BlockSpec((1, SEQ, ATT_HEAD_COLS), lambda b, h: (b, 0, hb + h))
    return pl.pallas_call(
        body, name=name, grid=(B, 4),
        in_specs=_qkv_specs(hb) + [
                  pl.BlockSpec((1, ATT_BLOCK, 2 * ATT_BLOCK), lambda b, h: (hb + h, 0, 0)),
                  pl.BlockSpec((1, HEAD_DIM), const2), pl.BlockSpec((1, HEAD_DIM), const2),
                  head_in, head_in,
                  pl.BlockSpec(memory_space=pl.ANY)],
        out_specs=[head_blk,
                   pl.BlockSpec((4, ATT_BLOCK, 2 * ATT_BLOCK), lambda b, h: (0, 0, 0)),
                   pl.BlockSpec((1, HEAD_DIM), const2), pl.BlockSpec((1, HEAD_DIM), const2)],
        out_shape=[jax.ShapeDtypeStruct(dpa.shape, BF16),
                   jax.ShapeDtypeStruct((4, ATT_BLOCK, 2 * ATT_BLOCK), F32),
                   jax.ShapeDtypeStruct((1, HEAD_DIM), F32), jax.ShapeDtypeStruct((1, HEAD_DIM), F32)],
        scratch_shapes=[pltpu.VMEM((SEQ, HEAD_DIM), F32)] * 5,
        input_output_aliases={8: 0},
        compiler_params=_cparams(),
    )(pa, pa, pa, bias, qg, kg, do, dlse, dpa)


def _attn_classes(q, k, v, bias, qg, kg):
    s = cdot_nt(_qk_norm(q, qg), _qk_norm(k, kg)) * (HEAD_DIM ** -0.5) + bias
    m = lax.stop_gradient(jnp.max(s, axis=-1, keepdims=True))
    p = jnp.exp(s - m)
    l = jnp.sum(p, axis=-1, keepdims=True)
    o = cdot_nn(p, v) / l
    return o, jnp.broadcast_to(m + jnp.log(l), o.shape)


def _gather_classes(src_ref, dst_s, dilation):
    for r in range(dilation):
        dst_s[r] = src_ref[pl.ds(r, ATT_BLOCK, stride=dilation), :]


def _scatter_classes(src_s, dst_ref, dilation):
    for r in range(dilation):
        dst_ref[pl.ds(r, ATT_BLOCK, stride=dilation), :] = src_s[r]


def attn_fwd_classes(pa, bias, qg, kg, group, name):
    B = pa.shape[0]
    dilation = ATT_GROUPS[group][1]
    hb = group * 4

    def body(q_ref, k_ref, v_ref, b_ref, qg_ref, kg_ref, o_ref, l_ref, q_s, k_s, v_s):
        _gather_classes(q_ref, q_s, dilation)
        _gather_classes(k_ref, k_s, dilation)
        _gather_classes(v_ref, v_s, dilation)
        o, lse = _attn_classes(q_s[...], k_s[...], v_s[...], _masked_bias(b_ref[0])[1], qg_ref[...], kg_ref[...])
        q_s[...], k_s[...] = o, lse
        _scatter_classes(q_s, o_ref, dilation)
        _scatter_classes(k_s, l_ref, dilation)

    head_out = pl.BlockSpec((None, SEQ, HEAD_DIM), lambda b, h: (b, 0, h))
    return pl.pallas_call(
        body, name=name, grid=(B, 4),
        in_specs=_qkv_specs(hb) + [
                  pl.BlockSpec((1, ATT_BLOCK, 2 * ATT_BLOCK), lambda b, h: (hb + h, 0, 0)),
                  pl.BlockSpec((1, HEAD_DIM), lambda b, h: (0, 0)),
                  pl.BlockSpec((1, HEAD_DIM), lambda b, h: (0, 0))],
        out_specs=[head_out, head_out],
        out_shape=[jax.ShapeDtypeStruct((B, SEQ, 512), F32), jax.ShapeDtypeStruct((B, SEQ, 512), F32)],
        scratch_shapes=[pltpu.VMEM((dilation, ATT_BLOCK, HEAD_DIM), F32)] * 3,
        compiler_params=_cparams(),
    )(pa, pa, pa, bias, qg, kg)


def attn_bwd_classes(pa, bias, qg, kg, do, dlse, dpa, group, name):
    B = pa.shape[0]
    dilation = ATT_GROUPS[group][1]
    hb = group * 4

    def body(q_ref, k_ref, v_ref, b_ref, qg_ref, kg_ref, do_ref, dl_ref, dpa_in,
             dp_ref, db_ref, dqg_ref, dkg_ref, q_s, k_s, v_s, do_s, dl_s, rows_s):
        del dpa_in
        h_id = pl.program_id(1)

        @pl.when(jnp.logical_and(pl.program_id(0) == 0, h_id == 0))
        def _():
            db_ref[...] = jnp.zeros_like(db_ref)
            dqg_ref[...] = jnp.zeros_like(dqg_ref)
            dkg_ref[...] = jnp.zeros_like(dkg_ref)

        for src, dst in ((q_ref, q_s), (k_ref, k_s), (v_ref, v_s), (do_ref, do_s), (dl_ref, dl_s)):
            _gather_classes(src, dst, dilation)
        _, vjp = jax.vjp(_attn_classes, q_s[...], k_s[...], v_s[...], _masked_bias(b_ref[0])[1],
                         qg_ref[...], kg_ref[...])
        dq, dk, dv, dbias, dqg, dkg = vjp((do_s[...], dl_s[...]))
        db_ref[h_id, :, ATT_BLOCK:] += dbias
        dqg_ref[...] += dqg
        dkg_ref[...] += dkg
        for d, lanes in ((dq, _QL), (dk, _KL), (dv, _VL)):
            q_s[...] = d
            _scatter_classes(q_s, rows_s, dilation)
            dp_ref[0, :, lanes] = rows_s[...].astype(BF16)

    const2 = lambda b, h: (0, 0)
    head_in = pl.BlockSpec((None, SEQ, HEAD_DIM), lambda b, h: (b, 0, h))
    head_blk = pl.BlockSpec((1, SEQ, ATT_HEAD_COLS), lambda b, h: (b, 0, hb + h))
    return pl.pallas_call(
        body, name=name, grid=(B, 4),
        in_specs=_qkv_specs(hb) + [
                  pl.BlockSpec((1, ATT_BLOCK, 2 * ATT_BLOCK), lambda b, h: (hb + h, 0, 0)),
                  pl.BlockSpec((1, HEAD_DIM), const2), pl.BlockSpec((1, HEAD_DIM), const2),
                  head_in, head_in,
                  pl.BlockSpec(memory_space=pl.ANY)],
        out_specs=[head_blk,
                   pl.BlockSpec((4, ATT_BLOCK, 2 * ATT_BLOCK), lambda b, h: (0, 0, 0)),
                   pl.BlockSpec((1, HEAD_DIM), const2), pl.BlockSpec((1, HEAD_DIM), const2)],
        out_shape=[jax.ShapeDtypeStruct(dpa.shape, BF16),
                   jax.ShapeDtypeStruct((4, ATT_BLOCK, 2 * ATT_BLOCK), F32),
                   jax.ShapeDtypeStruct((1, HEAD_DIM), F32), jax.ShapeDtypeStruct((1, HEAD_DIM), F32)],
        scratch_shapes=[pltpu.VMEM((dilation, ATT_BLOCK, HEAD_DIM), F32)] * 5 + [pltpu.VMEM((SEQ, HEAD_DIM), F32)],
        input_output_aliases={8: 0},
        compiler_params=_cparams(),
    )(pa, pa, pa, bias, qg, kg, do, dlse, dpa)


def _merge(o0, o1, o2, l0, l1, l2):
    mx = jnp.maximum(jnp.maximum(l0, l1), l2)
    e0, e1, e2 = jnp.exp(l0 - mx), jnp.exp(l1 - mx), jnp.exp(l2 - mx)
    den = e0 + e1 + e2
    return (e0 / den) * o0 + (e1 / den) * o1 + (e2 / den) * o2


def merge_att_out(os_, ls_, w_att_out, name):
    T = os_[0].shape[0]
    tile = pl.BlockSpec((ROW_MM_TILE, 512), lambda i: (i, 0))

    def body(o0, o1, o2, l0, l1, l2, w_ref, a_ref, y_ref):
        att = _merge(o0[...], o1[...], o2[...], l0[...], l1[...], l2[...]).astype(BF16)
        a_ref[...] = att
        y_ref[...] = _mxu(att, w_ref[...], _NN)

    return pl.pallas_call(
        body, name=name, grid=(T // ROW_MM_TILE,),
        in_specs=[tile] * 6 + [pl.BlockSpec((512, D_MODEL), lambda i: (0, 0))],
        out_specs=[tile, pl.BlockSpec((ROW_MM_TILE, D_MODEL), lambda i: (i, 0))],
        out_shape=[jax.ShapeDtypeStruct((T, 512), BF16), jax.ShapeDtypeStruct((T, D_MODEL), F32)],
        compiler_params=_cparams(),
    )(*os_, *ls_, w_att_out)


def d_att_merge_bwd(d_ya, w_att_out, os_, ls_, name):
    T = d_ya.shape[0]
    tile = lambda a: (a, (ROW_MM_TILE, 512), lambda i: (i, 0))

    def epi(acc, i, ex, out):
        _, vjp = jax.vjp(_merge, *[e[...] for e in ex])
        for o_ref, g in zip(out, vjp(acc)):
            o_ref[...] = g

    return row_matmul(d_ya, w_att_out, mode="nt", name=name, extras=[tile(a) for a in list(os_) + list(ls_)],
                      outs=[((T, 512), F32, (ROW_MM_TILE, 512), lambda i: (i, 0))] * 6, epi=epi)


def _gate_mix(ga, gm, ya, ym):
    return jax.nn.sigmoid(ga) * ya + jax.nn.sigmoid(gm) * ym


def _gate_halves(pg):
    return [(pg, (ROW_MM_TILE, D_MODEL), lambda i: (i, 0)), (pg, (ROW_MM_TILE, D_MODEL), lambda i: (i, 1))]


def ml_out_gate(hg, w_ml_out, pg, ya, name):
    T = hg.shape[0]

    def epi(acc, i, ex, out):
        ga, gm, ya_ref = ex
        out[0][...] = acc
        out[1][...] = _gate_mix(ga[...], gm[...], ya_ref[...], acc).astype(BF16)

    return row_matmul(hg, w_ml_out, mode="nn", name=name, extras=_gate_halves(pg) + [_rows(ya)],
                      outs=[_rows_out(T, F32), _rows_out(T, BF16)], epi=epi)


def d_z_gate_bwd(dy, w_out, pg, ya, ym, name):
    T = dy.shape[0]

    def epi(acc, i, ex, out):
        ga, gm, ya_ref, ym_ref = ex
        dpg_ref, dya_ref, dym_ref = out
        _, vjp = jax.vjp(_gate_mix, ga[...], gm[...], ya_ref[...], ym_ref[...])
        dga, dgm, dya, dym = vjp(acc)
        dpg_ref[:, :D_MODEL] = dga.astype(BF16)
        dpg_ref[:, D_MODEL:] = dgm.astype(BF16)
        dya_ref[...] = dya.astype(BF16)
        dym_ref[...] = dym.astype(BF16)

    return row_matmul(dy, w_out, mode="nt", name=name, extras=_gate_halves(pg) + [_rows(ya), _rows(ym)],
                      outs=[((T, GATE_COLS), BF16, (ROW_MM_TILE, GATE_COLS), lambda i: (i, 0)),
                            _rows_out(T, BF16), _rows_out(T, BF16)], epi=epi)


def _log_sigmoid(x):
    return jnp.minimum(x, 0.0) - jnp.log(1.0 + jnp.exp(-jnp.abs(x)))


def _head_mask(e):
    lane = lax.broadcasted_iota(jnp.int32, (1, 128), 1)
    return jnp.logical_and(lane >= e * 64, lane < (e + 1) * 64).astype(F32)


def _bmxu(a, b, ca, cb):
    return lax.dot_general(a.astype(BF16), b.astype(BF16), (((ca,), (cb,)), ((0,), (0,))), preferred_element_type=F32)


@jax.custom_vjp
def cdot_nt(a, b):
    return _bmxu(a, b, 2, 2)


cdot_nt.defvjp(lambda a, b: (_bmxu(a, b, 2, 2), (a, b)),
               lambda res, g: (_bmxu(g, res[1], 2, 1), _bmxu(g, res[0], 1, 1)))


@jax.custom_vjp
def cdot_nn(a, b):
    return _bmxu(a, b, 2, 1)


cdot_nn.defvjp(lambda a, b: (_bmxu(a, b, 2, 1), (a, b)),
               lambda res, g: (_bmxu(g, res[1], 2, 2), _bmxu(res[0], g, 1, 1)))


@jax.custom_vjp
def cdot_tn(a, b):
    return _bmxu(a, b, 1, 1)


cdot_tn.defvjp(lambda a, b: (_bmxu(a, b, 1, 1), (a, b)),
               lambda res, g: (_bmxu(res[1], g, 2, 2), _bmxu(res[0], g, 2, 1)))


def _top_bits(x):
    return lax.bitcast_convert_type(lax.bitcast_convert_type(x, jnp.uint32) & jnp.uint32(0xFFFF0000), F32)


def _split3(x):
    hi = _top_bits(x)
    r = x - hi
    mid = _top_bits(r)
    return hi, mid, r - mid


def _parts_in_lanes(col):
    hi, mid, lo = _split3(col)
    lane = lax.broadcasted_iota(jnp.int32, (1, 1, 8), 2)
    return jnp.where(lane == 0, hi, jnp.where(lane == 1, mid, jnp.where(lane == 2, lo, 0.0)))


def _parts_in_rows(row):
    hi, mid, lo = _split3(row)
    sub = lax.broadcasted_iota(jnp.int32, (1, 8, 1), 1)
    return jnp.where(sub == 0, hi, jnp.where(sub == 1, mid, jnp.where(sub == 2, lo, 0.0)))


def _chunk_matrix(kind, c):
    ri = lax.broadcasted_iota(jnp.int32, (c, ML_CHUNK, ML_CHUNK), 1)
    ci = lax.broadcasted_iota(jnp.int32, (c, ML_CHUNK, ML_CHUNK), 2)
    return {"eye": ri == ci, "lower": ri >= ci, "upper": ri <= ci}[kind].astype(F32)


def _col_col(kind, col):
    out = _bmxu(_chunk_matrix(kind, col.shape[0]), _parts_in_lanes(col), 2, 1)
    return jnp.sum(out, axis=-1, keepdims=True)


def _col_row(col):
    out = _bmxu(_parts_in_lanes(col), _chunk_matrix("eye", col.shape[0]), 1, 1)
    return jnp.sum(out, axis=1, keepdims=True)


def _row_col(row):
    out = _bmxu(_chunk_matrix("eye", row.shape[0]), _parts_in_rows(row), 2, 2)
    return jnp.sum(out, axis=-1, keepdims=True)


@jax.custom_vjp
def chunk_cumsum(col):
    return _col_col("lower", col)


chunk_cumsum.defvjp(lambda col: (_col_col("lower", col), None), lambda _, g: (_col_col("upper", g),))


@jax.custom_vjp
def col_to_row(col):
    return _col_row(col)


col_to_row.defvjp(lambda col: (_col_row(col), None), lambda _, g: (_row_col(g),))


def _gate_block(ifb):
    lane = lax.broadcasted_iota(jnp.int32, (1, 128), 1)
    return jnp.where(lane >= 2, _log_sigmoid(ifb), ifb)


def _ml_intra(q2, k2, v, ifb, *, e):
    c, L = q2.shape[0] // ML_CHUNK, ML_CHUNK
    hm = _head_mask(e)
    q3 = (q2 * hm).reshape(c, L, 128)
    k3 = (k2 * hm).reshape(c, L, 128)
    v3 = v.reshape(c, L, 128)
    if3 = ifb.reshape(c, L, 128)
    lanes = lax.broadcasted_iota(jnp.int32, (c, L, 128), 2)
    li = jnp.sum(jnp.where(lanes == e, if3, 0.0), axis=-1, keepdims=True)
    lf = jnp.sum(jnp.where(lanes == 2 + e, if3, 0.0), axis=-1, keepdims=True)
    b = chunk_cumsum(lf)
    last = lax.broadcasted_iota(jnp.int32, (1, L, 1), 1) == L - 1
    b_end = jnp.sum(jnp.where(last, b, 0.0), axis=1, keepdims=True)
    causal = lax.broadcasted_iota(jnp.int32, (L, L), 0) >= lax.broadcasted_iota(jnp.int32, (L, L), 1)
    Dm = jnp.where(causal, b + col_to_row(li - b), -jnp.inf)
    mD = lax.stop_gradient(jnp.max(Dm, axis=-1, keepdims=True))
    P0 = cdot_nt(q3, k3) * jnp.exp(Dm - mD)
    H0 = cdot_nn(P0, v3)
    r0 = jnp.sum(P0, axis=-1, keepdims=True)
    g = b_end - b + li
    mg = lax.stop_gradient(jnp.max(g, axis=1, keepdims=True))
    kw = jnp.exp(g - mg) * k3
    return H0, r0, cdot_tn(kw, v3), jnp.sum(kw, axis=1, keepdims=True), b, b_end, mD, mg


def _ml_inter(q2, mo, gn, H0, r0, b, C_in, n_in, *, mD, m_in, e):
    c, L = q2.shape[0] // ML_CHUNK, ML_CHUNK
    q3 = (q2 * _head_mask(e)).reshape(c, L, 128)
    a = b + m_in
    m_t = lax.stop_gradient(jnp.maximum(a, mD))
    c1 = jnp.exp(mD - m_t)
    c2 = jnp.exp(a - m_t)
    num = c1 * H0 + c2 * cdot_nn(q3, C_in)
    nq = c1 * r0 + c2 * jnp.sum(q3 * n_in, axis=-1, keepdims=True)
    h = num / jnp.maximum(jnp.abs(nq), jnp.exp(-m_t))
    hg = _rms(h) * gn * jax.nn.sigmoid(mo.reshape(c, L, 128))
    return hg.reshape(c * L, 128)


def _state_scalars(be, mg):
    c = be.shape[0]

    def shift(x, k, fill):
        return jnp.concatenate([jnp.full((k, 1, 1), fill, F32), x[:c - k]], axis=0)

    run = be
    k = 1
    while k < c:
        run = run + shift(run, k, 0.0)
        k *= 2
    top = mg - run
    k = 1
    while k < c:
        top = jnp.maximum(top, shift(top, k, -jnp.inf))
        k *= 2
    m_in = shift(run, 1, 0.0) + jnp.maximum(shift(top, 1, -jnp.inf), 0.0)
    m_next = jnp.maximum(be + m_in, mg)
    return m_in, jnp.exp(be + m_in - m_next), jnp.exp(mg - m_next)


def _state_sweep(U_s, un_s, be_s, mg_s, Cin_s, nin_s, min_s, al_s, bt_s):
    min_s[...], al_s[...], bt_s[...] = _state_scalars(be_s[...], mg_s[...])

    def step(j, carry):
        C, n = carry
        Cin_s[j], nin_s[j] = C, n
        return al_s[j] * C + bt_s[j] * U_s[j], al_s[j] * n + bt_s[j] * un_s[j]

    lax.fori_loop(0, N_CHUNKS, step, (jnp.zeros((128, 128), F32), jnp.zeros((1, 128), F32)), unroll=4)


def _state_sweep_bwd(U_s, un_s, dbe_s, Cin_s, nin_s, dCp_s, dnp_s, al_s, bt_s):
    def step(t, carry):
        j = N_CHUNKS - 1 - t
        dC, dn = carry
        al, bt = al_s[j], bt_s[j]
        U_s[j] = bt * dC
        un_s[j] = bt * dn
        dal = jnp.sum(jnp.sum(dC * Cin_s[j], axis=1, keepdims=True), axis=0, keepdims=True) \
            + jnp.sum(dn * nin_s[j], axis=1, keepdims=True)
        dbe_s[j] = dal * al
        return dCp_s[j] + al * dC, dnp_s[j] + al * dn

    lax.fori_loop(0, N_CHUNKS, step, (jnp.zeros((128, 128), F32), jnp.zeros((1, 128), F32)))


def _state_scratch():
    c = N_CHUNKS
    return [pltpu.VMEM((c, 128, 128), F32), pltpu.VMEM((c, 1, 128), F32), pltpu.VMEM((c, 1, 1), F32),
            pltpu.VMEM((c, 1, 1), F32),
            pltpu.VMEM((c, 128, 128), F32), pltpu.VMEM((c, 1, 128), F32), pltpu.VMEM((c, 1, 1), F32),
            pltpu.VMEM((c, 1, 1), F32), pltpu.VMEM((c, 1, 1), F32)]


def _shift_down(x, s):
    if s == 0:
        return x
    rows = lax.broadcasted_iota(jnp.int32, x.shape, 0)
    return jnp.where(rows >= s, pltpu.roll(x, s, 0), 0.0)


def _shift_up(x, s):
    if s == 0:
        return x
    S = x.shape[0]
    rows = lax.broadcasted_iota(jnp.int32, x.shape, 0)
    return jnp.where(rows < S - s, pltpu.roll(x, S - s, 0), 0.0)


def _conv_pre(x, cw, cb):
    y = cb + cw[3:4, :] * x
    for j in range(3):
        y = y + cw[j:j + 1, :] * _shift_down(x, 3 - j)
    return y


def _conv_bwd(x, cw, dpre):
    dx = cw[3:4, :] * dpre
    dcw = [None] * 4
    dcw[3] = jnp.sum(dpre * x, axis=0, keepdims=True)
    for j in range(3):
        dx = dx + cw[j:j + 1, :] * _shift_up(dpre, 3 - j)
        dcw[j] = jnp.sum(dpre * _shift_down(x, 3 - j), axis=0, keepdims=True)
    return dx, dcw, jnp.sum(dpre, axis=0, keepdims=True)


def _silu(z):
    return z * jax.nn.sigmoid(z)


def _dsilu(z):
    s = jax.nn.sigmoid(z)
    return s * (1.0 + z * (1.0 - s))


_ML_Q, _ML_K = slice(0, 128), slice(128, 256)
_ML_IF = slice(768, 896)


def _ml_v(e):
    return slice(256 + e * 128, 384 + e * 128)


def _ml_o(e):
    return slice(512 + e * 128, 640 + e * 128)


def _ml_specs():
    pair = lambda b, p: (b, 0, p)
    return [pl.BlockSpec((1, SEQ, ML_PAIR_COLS), pair),
            pl.BlockSpec((1, 4, 128), lambda b, p: (p, 0, 0)),
            pl.BlockSpec((1, 4, 128), lambda b, p: (4 + p, 0, 0)),
            pl.BlockSpec((1, 1, 128), lambda b, p: (p, 0, 0)),
            pl.BlockSpec((1, 1, 128), lambda b, p: (4 + p, 0, 0)),
            pl.BlockSpec((1, 1, 128), lambda b, p: (p, 0, 0)),
            pl.BlockSpec((1, 1, 256), lambda b, p: (p, 0, 0))]


def mlstm_fwd(pm, cw8, cb8, bifp, gn4, name):
    B = pm.shape[0]

    def body(p_ref, cwq, cwk, cbq, cbk, bif_ref, gn_ref, hg_ref, *st):
        U_s, un_s, be_s, mg_s, Cin_s, nin_s, min_s, al_s, bt_s = st
        qc = _silu(_conv_pre(p_ref[0, :, _ML_Q], cwq[0], cbq[0]))
        kc = _silu(_conv_pre(p_ref[0, :, _ML_K], cwk[0], cbk[0])) * (64 ** -0.5)
        ifb = _gate_block(p_ref[0, :, _ML_IF] + bif_ref[0])
        for e in range(2):
            lanes = slice(e * 128, (e + 1) * 128)
            H0, r0, U, un, b, b_end, mD, mg = _ml_intra(qc, kc, p_ref[0, :, _ml_v(e)], ifb, e=e)
            U_s[...], un_s[...], be_s[...], mg_s[...] = U, un, b_end, mg
            _state_sweep(*st)
            hg = _ml_inter(qc, p_ref[0, :, _ml_o(e)], gn_ref[0, :, lanes], H0, r0, b, Cin_s[...], nin_s[...],
                           mD=mD, m_in=min_s[...], e=e)
            hg_ref[0, :, lanes] = hg.astype(BF16)

    return pl.pallas_call(
        body, name=name, grid=(B, ML_PAIRS),
        in_specs=_ml_specs(),
        out_specs=pl.BlockSpec((1, SEQ, 256), lambda b, p: (b, 0, p)),
        out_shape=jax.ShapeDtypeStruct((B, SEQ, D_MODEL), BF16),
        scratch_shapes=_state_scratch(),
        compiler_params=_cparams(),
    )(pm, cw8, cw8, cb8, cb8, bifp, gn4)


def mlstm_bwd(pm, cw8, cb8, bifp, gn4, dhg, name):
    B = pm.shape[0]

    def body(p_ref, cwq, cwk, cbq, cbk, bif_ref, gn_ref, dh_ref,
             dp_ref, dcw_ref, dcb_ref, dbif_ref, dgn_ref, *scr):
        st = scr[:9]
        U_s, un_s, be_s, mg_s, Cin_s, nin_s, min_s, al_s, bt_s = st
        dCp_s, dnp_s, dbe_s = scr[9:]
        p_id = pl.program_id(1)

        @pl.when(jnp.logical_and(pl.program_id(0) == 0, p_id == 0))
        def _():
            dcw_ref[...] = jnp.zeros_like(dcw_ref)
            dcb_ref[...] = jnp.zeros_like(dcb_ref)
            dbif_ref[...] = jnp.zeros_like(dbif_ref)
            dgn_ref[...] = jnp.zeros_like(dgn_ref)

        pre_q = _conv_pre(p_ref[0, :, _ML_Q], cwq[0], cbq[0])
        pre_k = _conv_pre(p_ref[0, :, _ML_K], cwk[0], cbk[0])
        qc = _silu(pre_q)
        kc = _silu(pre_k) * (64 ** -0.5)
        ifb, gate_vjp = jax.vjp(_gate_block, p_ref[0, :, _ML_IF] + bif_ref[0])
        dq = jnp.zeros((SEQ, 128), F32)
        dk = jnp.zeros((SEQ, 128), F32)
        difb = jnp.zeros((SEQ, 128), F32)
        for e in range(2):
            lanes = slice(e * 128, (e + 1) * 128)
            (H0, r0, U, un, b, b_end, mD, mg), vjp1 = jax.vjp(functools.partial(_ml_intra, e=e), qc, kc,
                                                              p_ref[0, :, _ml_v(e)], ifb)
            U_s[...], un_s[...], be_s[...], mg_s[...] = U, un, b_end, mg
            _state_sweep(*st)
            _, vjp3 = jax.vjp(functools.partial(_ml_inter, mD=mD, m_in=min_s[...], e=e), qc, p_ref[0, :, _ml_o(e)],
                              gn_ref[0, :, lanes], H0, r0, b, Cin_s[...], nin_s[...])
            dq_a, dmo, dgn, dH0, dr0, db_a, dCp, dnp = vjp3(dh_ref[0, :, lanes])
            dCp_s[...], dnp_s[...] = dCp, dnp
            _state_sweep_bwd(U_s, un_s, dbe_s, Cin_s, nin_s, dCp_s, dnp_s, al_s, bt_s)
            dq_b, dk_b, dv, difb_e = vjp1((dH0, dr0, U_s[...], un_s[...], db_a, dbe_s[...],
                                           jnp.zeros_like(mD), jnp.zeros_like(mg)))
            dq, dk, difb = dq + dq_a + dq_b, dk + dk_b, difb + difb_e
            dp_ref[0, :, _ml_v(e)] = dv.astype(BF16)
            dp_ref[0, :, _ml_o(e)] = dmo.astype(BF16)
            dgn_ref[p_id, :, lanes] += dgn
        (difb,) = gate_vjp(difb)
        dp_ref[0, :, _ML_IF] = difb.astype(BF16)
        dbif_ref[p_id] += jnp.sum(difb, axis=0, keepdims=True)

        for (sl, cw, pre, d, blk, scale) in ((_ML_Q, cwq, pre_q, dq, p_id, 1.0), (_ML_K, cwk, pre_k, dk, 4 + p_id, 64 ** -0.5)):
            xr = p_ref[0, :, sl]
            dpre = d * scale * _dsilu(pre)
            dx, dcw, dcb = _conv_bwd(xr, cw[0], dpre)
            dp_ref[0, :, sl] = dx.astype(BF16)
            for j in range(4):
                dcw_ref[blk, j:j + 1, :] += dcw[j]
            dcb_ref[blk] += dcb

    full3 = lambda b, p: (0, 0, 0)
    return pl.pallas_call(
        body, name=name, grid=(B, ML_PAIRS),
        in_specs=[pl.BlockSpec((1, SEQ, ML_PAIR_COLS), lambda b, p: (b, 0, p), pipeline_mode=pl.Buffered(1))]
        + _ml_specs()[1:] + [pl.BlockSpec((1, SEQ, 256), lambda b, p: (b, 0, p), pipeline_mode=pl.Buffered(1))],
        out_specs=[pl.BlockSpec((1, SEQ, ML_PAIR_COLS), lambda b, p: (b, 0, p)),
                   pl.BlockSpec((8, 4, 128), full3), pl.BlockSpec((8, 1, 128), full3),
                   pl.BlockSpec((4, 1, 128), full3), pl.BlockSpec((4, 1, 256), full3)],
        out_shape=[jax.ShapeDtypeStruct((B, SEQ, ML_COLS), BF16),
                   jax.ShapeDtypeStruct((8, 4, 128), F32), jax.ShapeDtypeStruct((8, 1, 128), F32),
                   jax.ShapeDtypeStruct((4, 1, 128), F32), jax.ShapeDtypeStruct((4, 1, 256), F32)],
        scratch_shapes=_state_scratch() + [pltpu.VMEM((N_CHUNKS, 128, 128), F32), pltpu.VMEM((N_CHUNKS, 1, 128), F32),
                                           pltpu.VMEM((N_CHUNKS, 1, 1), F32)],
        compiler_params=_cparams(),
    )(pm, cw8, cw8, cb8, cb8, bifp, gn4, dhg)


def _adamw(w, g, m, v):
    m = ADAM_B1 * m + (1.0 - ADAM_B1) * g
    v = ADAM_B2 * v + (1.0 - ADAM_B2) * (g * g)
    m_hat = m / (1.0 - ADAM_B1 ** ADAM_STEP)
    v_hat = v / (1.0 - ADAM_B2 ** ADAM_STEP)
    delta = -ADAM_LR * (m_hat / (jnp.sqrt(v_hat) + ADAM_EPS) + ADAM_WD * w)
    return delta, m, v


def adamw(w, g, m, v, name, parts=False):
    R, C = w.shape
    if R % 8 == 0 or R * C * 4 <= (1 << 20):
        tr = _pick(R, (256, 128, 64, 32, 16, 8)) if R * C * 4 > (1 << 20) else R
        steps = R // tr
        spec = pl.BlockSpec((tr, C), lambda i: (i, 0))
        g_spec = pl.BlockSpec((N_DEV, tr, C), lambda i: (0, i, 0)) if parts else spec
    else:
        tc = _pick(C, (256, 128))
        steps = C // tc
        spec = pl.BlockSpec((R, tc), lambda i: (0, i))
        g_spec = pl.BlockSpec((N_DEV, R, tc), lambda i: (0, 0, i)) if parts else spec

    def body(w_ref, g_ref, m_ref, v_ref, go_ref, d_ref, mo_ref, vo_ref):
        if parts:
            g = g_ref[0].astype(F32)
            for k in range(1, N_DEV):
                g = g + g_ref[k].astype(F32)
        else:
            g = g_ref[...]
        d, mn, vn = _adamw(w_ref[...], g, m_ref[...], v_ref[...])
        go_ref[...], d_ref[...], mo_ref[...], vo_ref[...] = g, d, mn, vn

    return pl.pallas_call(
        body, name=name, grid=(steps,),
        in_specs=[spec, g_spec, spec, spec], out_specs=[spec] * 4,
        out_shape=[jax.ShapeDtypeStruct((R, C), F32)] * 4,
        compiler_params=_cparams(),
    )(w, g, m, v)


def adamw_many(ws, gs, ms, vs, name):
    n = len(ws)

    def fn(*a):
        out = []
        for j in range(n):
            out += list(_adamw(a[j], a[n + j], a[2 * n + j], a[3 * n + j]))
        return tuple(out)

    shapes = [jax.ShapeDtypeStruct(w.shape, F32) for w in ws for _ in range(3)]
    return small_call(fn, list(ws) + list(gs) + list(ms) + list(vs), shapes, name)


def _mesh_pos():
    return lax.axis_index("x"), lax.axis_index("y"), lax.axis_index("c")


def _flip(pos, f):
    x, y, c = pos
    return (1 - x if f & 4 else x, 1 - y if f & 2 else y, 1 - c if f & 1 else c)


def _index(pos):
    return 4 * pos[0] + 2 * pos[1] + pos[2]


def _exchange(arrs, name, scatter):
    n = len(arrs)
    scat = list(scatter) if isinstance(scatter, (list, tuple)) else [scatter] * n

    def body(*refs):
        ins, outs = refs[:n], refs[n:2 * n]
        send, recv, lsem = refs[2 * n:]
        me = _mesh_pos()
        mine = _index(me)
        copies = []
        for i in range(n):
            src = ins[i].at[mine] if scat[i] else ins[i]
            loc = pltpu.make_async_copy(src, outs[i].at[mine], lsem.at[i])
            loc.start()
            copies.append(loc)
            for f in range(1, N_DEV):
                peer = _flip(me, f)
                src = ins[i].at[_index(peer)] if scat[i] else ins[i]
                cp = pltpu.make_async_remote_copy(
                    src_ref=src, dst_ref=outs[i].at[mine],
                    send_sem=send.at[i * 7 + f - 1], recv_sem=recv.at[i * 7 + f - 1],
                    device_id=peer, device_id_type=pl.DeviceIdType.MESH)
                cp.start()
                copies.append(cp)
        for cp in copies:
            cp.wait()

    any_spec = pl.BlockSpec(memory_space=pl.ANY)
    out_shape = [jax.ShapeDtypeStruct(a.shape if s else (N_DEV,) + a.shape, a.dtype) for a, s in zip(arrs, scat)]
    res = pl.pallas_call(
        body, name=name,
        in_specs=[any_spec] * n, out_specs=[any_spec] * n, out_shape=out_shape,
        scratch_shapes=[pltpu.SemaphoreType.DMA((7 * n,)), pltpu.SemaphoreType.DMA((7 * n,)),
                        pltpu.SemaphoreType.DMA((n,))],
        compiler_params=_cparams(),
    )(*arrs)
    return list(res)


def all_gather(arrs, name):
    return _exchange(arrs, name, False)


def all_gather_two_level(arrs, name):
    n = len(arrs)

    def body(*refs):
        ins, outs = refs[:n], refs[n:2 * n]
        send, recv, lsem = refs[2 * n:]
        x, y, c = _mesh_pos()
        me, sibling = (x, y, c), (x, y, 1 - c)
        chips = [(1 - x, y), (x, 1 - y), (1 - x, 1 - y)]

        def copy(i, k, block, to, src=None):
            rows = outs[i].at[_index(block)]
            return pltpu.make_async_remote_copy(
                src_ref=rows if src is None else src, dst_ref=rows,
                send_sem=send.at[i * 7 + k], recv_sem=recv.at[i * 7 + k],
                device_id=to, device_id_type=pl.DeviceIdType.MESH)

        local = [pltpu.make_async_copy(ins[i], outs[i].at[_index(me)], lsem.at[i]) for i in range(n)]
        first = [copy(i, 0, me, sibling, src=ins[i]) for i in range(n)]
        first += [copy(i, 1 + j, me, (*chip, c), src=ins[i]) for i in range(n) for j, chip in enumerate(chips)]
        for cp in local + first:
            cp.start()
        passed = []
        for j, chip in enumerate(chips):
            for i in range(n):
                copy(i, 1 + j, (*chip, c), me).wait_recv()
                cp = copy(i, 4 + j, (*chip, c), sibling)
                cp.start()
                passed.append(cp)
        for i in range(n):
            copy(i, 0, sibling, me).wait_recv()
            for j, chip in enumerate(chips):
                copy(i, 4 + j, (*chip, 1 - c), me).wait_recv()
        for cp in first + passed:
            cp.wait_send()
        for cp in local:
            cp.wait()

    any_spec = pl.BlockSpec(memory_space=pl.ANY)
    res = pl.pallas_call(
        body, name=name,
        in_specs=[any_spec] * n, out_specs=[any_spec] * n,
        out_shape=[jax.ShapeDtypeStruct((N_DEV,) + a.shape, a.dtype) for a in arrs],
        scratch_shapes=[pltpu.SemaphoreType.DMA((7 * n,)), pltpu.SemaphoreType.DMA((7 * n,)),
                        pltpu.SemaphoreType.DMA((n,))],
        compiler_params=_cparams(),
    )(*arrs)
    return list(res)


def all_to_all(arrs, name):
    return _exchange(arrs, name, True)


_HBM = pl.BlockSpec(memory_space=pltpu.HBM)
_SEM = pl.BlockSpec(memory_space=pltpu.SEMAPHORE)
_EFFECT = pltpu.SideEffectType.DATAFLOW_SIDE_EFFECTING


def _split_copies(ins, lands, send, recv, scatter, waiting):
    me = _mesh_pos()
    mine = _index(me)
    copies = []
    for i in range(len(ins)):
        for f in range(1, N_DEV):
            peer = _flip(me, f)
            src = ins[i].at[_index(peer)] if scatter else ins[i]
            copies.append(pltpu.make_async_remote_copy(
                src_ref=src, dst_ref=lands[i].at[_index(peer) if waiting else mine],
                send_sem=send.at[i * 7 + f - 1], recv_sem=recv.at[i * 7 + f - 1],
                device_id=peer, device_id_type=pl.DeviceIdType.MESH))
    return copies


def exchange_start(arrs, name, scatter, after=()):
    n = len(arrs)
    land_shapes = [a.shape if scatter else (N_DEV,) + a.shape for a in arrs]

    def body(*refs):
        ins, lands = refs[:n], refs[n:2 * n]
        send, recv = refs[2 * n + len(after)], refs[2 * n + len(after) + 1]
        token = refs[-1]
        for cp in _split_copies(ins, lands, send, recv, scatter, False):
            cp.start()
        token[...] = jnp.zeros_like(token)

    res = pl.pallas_call(
        body, name=name,
        out_shape=(pltpu.SemaphoreType.DMA((7 * n,)), pltpu.SemaphoreType.DMA((7 * n,)),
                   *[pltpu.HBM(a.shape, a.dtype) for a in arrs],
                   *[pltpu.HBM(s, a.dtype) for s, a in zip(land_shapes, arrs)],
                   jax.ShapeDtypeStruct((8, 128), F32)),
        in_specs=[_HBM] * (2 * n) + [pl.BlockSpec(memory_space=pl.ANY)] * len(after),
        out_specs=(_SEM, _SEM, *[_HBM] * (2 * n), pl.BlockSpec(memory_space=pltpu.VMEM)),
        input_output_aliases={i: 2 + i for i in range(2 * n)},
        compiler_params=pltpu.CompilerParams(has_side_effects=_EFFECT),
    )(*[pltpu.with_memory_space_constraint(a, pltpu.HBM) for a in arrs],
      *[pltpu.with_memory_space_constraint(lax.empty(s, a.dtype), pltpu.HBM) for s, a in zip(land_shapes, arrs)],
      *after)
    return (res[0], res[1], list(res[2:2 + n]), list(res[2 + n:2 + 2 * n])), res[-1]


def exchange_wait(handle, after, name, scatter):
    send, recv, srcs, lands = handle
    n = len(srcs)
    after = list(after) if isinstance(after, (list, tuple)) else [after]

    def body(*refs):
        ins, lnd = refs[:n], refs[n:2 * n]
        send_, recv_ = refs[2 * n], refs[2 * n + 1]
        for cp in _split_copies(ins, lnd, send_, recv_, scatter, True):
            cp.wait_send()
            cp.wait_recv()

    res = pl.pallas_call(
        body, name=name,
        out_shape=(*[pltpu.HBM(a.shape, a.dtype) for a in srcs], *[pltpu.HBM(a.shape, a.dtype) for a in lands]),
        in_specs=[_HBM] * (2 * n) + [_SEM, _SEM] + [pl.BlockSpec(memory_space=pl.ANY)] * len(after),
        out_specs=tuple([_HBM] * (2 * n)),
        input_output_aliases={i: i for i in range(2 * n)},
        compiler_params=pltpu.CompilerParams(has_side_effects=_EFFECT),
    )(*srcs, *lands, send, recv, *after)
    return list(res[n:])


def _own_slot(land, own):
    return lax.dynamic_update_slice(land, own[None], (_index(_mesh_pos()),) + (0,) * own.ndim)


def cast_bf16(arrs, name):
    outs = []
    for i, a in enumerate(arrs):
        R, C = a.shape
        if R % 8 == 0:
            tr = _pick(R, (256, 128, 64, 32, 16, 8)) if R * C * 4 > (1 << 21) else R
            steps, spec = R // tr, pl.BlockSpec((tr, C), lambda i: (i, 0))
        else:
            steps, spec = C // 256, pl.BlockSpec((R, 256), lambda i: (0, i))

        def body(a_ref, o_ref):
            o_ref[...] = a_ref[...].astype(BF16)

        outs.append(pl.pallas_call(body, name=f"{name}_{i}", grid=(steps,), in_specs=[spec], out_specs=spec,
                                   out_shape=jax.ShapeDtypeStruct((R, C), BF16), compiler_params=_cparams())(a))
    return outs


def sum_slabs(parts, name):
    def fn(*a):
        outs = []
        for p in a:
            s = p[0].astype(F32)
            for k in range(1, N_DEV):
                s = s + p[k].astype(F32)
            outs.append(s)
        return tuple(outs)
    return list(small_call(fn, list(parts), [jax.ShapeDtypeStruct(p.shape[1:], F32) for p in parts], name))


def sum_parts(parts, name):
    def fn(p):
        g = p[0]
        for k in range(1, N_DEV):
            g = g + p[k]
        return (g,)
    return small_call(fn, [parts], [jax.ShapeDtypeStruct(parts.shape[1:], F32)], name)[0]


_SPLITS = np.cumsum([1536, 1536, 1536, 512, 512, 1024, 1024, 8, 8, 2048])[:-1].tolist()


def split_w_in(w):
    aq, ak, av, mq, mk, mv, mo, mi, mf, gates = jnp.split(w, _SPLITS, axis=1)
    R = w.shape[0]
    w_att = jnp.stack([aq.reshape(R, 12, 128), ak.reshape(R, 12, 128), av.reshape(R, 12, 128)], axis=2)
    gif = jnp.concatenate([mi.reshape(R, 4, 2), mf.reshape(R, 4, 2), jnp.zeros((R, 4, 124), w.dtype)], axis=2)
    w_ml = jnp.concatenate([mq.reshape(R, 4, 128), mk.reshape(R, 4, 128), mv.reshape(R, 4, 256),
                            mo.reshape(R, 4, 256), gif], axis=2)
    return w_att.reshape(R, ATT_COLS), w_ml.reshape(R, ML_COLS), gates


def merge_w_in(g_att, g_ml, g_gate):
    R = g_att.shape[0]
    a = g_att.reshape(R, 12, 3, 128)
    m = g_ml.reshape(R, 4, ML_PAIR_COLS)
    gif = m[:, :, 768:772]
    return jnp.concatenate([
        a[:, :, 0].reshape(R, 1536), a[:, :, 1].reshape(R, 1536), a[:, :, 2].reshape(R, 1536),
        m[:, :, 0:128].reshape(R, 512), m[:, :, 128:256].reshape(R, 512),
        m[:, :, 256:512].reshape(R, 1024), m[:, :, 512:768].reshape(R, 1024),
        gif[:, :, 0:2].reshape(R, 8), gif[:, :, 2:4].reshape(R, 8), g_gate], axis=1)


def _blk8(v, width=128):
    r = v.shape[0]
    return v.reshape(r, 1024 // width, width).transpose(1, 0, 2)


def _unblk8(v):
    nb, r, w = v.shape
    return v.transpose(1, 0, 2).reshape(r, nb * w)


def local_step(x, target, mods, w, small, late_w=None, early_g=None, w_in_g=None):
    late_w = late_w or (lambda after: w)
    big = {}
    early_g = early_g or (lambda g: big.update(g))
    w_in_g = w_in_g or (lambda g: big.update(w_in=merge_w_in(*g)))
    B = x.shape[0]
    T = B * SEQ
    shift1, scale1, gate1, shift2, scale2, gate2 = mods
    f2 = lambda a: a.reshape(T, a.shape[-1])
    f3 = lambda a: a.reshape(B, SEQ, a.shape[-1])

    rel_t = jnp.pad(small["rel_bias"].T, ((0, 4), (0, 0)))
    onehots = [_bucket_onehot(d) for _, d in ATT_GROUPS]
    biases = [bias_expand(rel_t, oh, f"bias_expand{g}").reshape(16, ATT_BLOCK, 2 * ATT_BLOCK)
              for g, oh in enumerate(onehots)]
    qg, kg = small["q_norm_g"], small["k_norm_g"]
    cw8 = _blk8(small["conv_w"])
    cb8 = _blk8(small["conv_b"])
    b_if = small["b_if"].reshape(2, 4, 2)
    bifp = jnp.concatenate([b_if[0], b_if[1], jnp.zeros((4, 124), F32)], axis=1).reshape(4, 1, 128)
    gn4 = small["mlstm_norm_g"].reshape(4, 1, 256)

    u = modnorm_fwd(x, small["norm1_g"], scale1, shift1, "modnorm1")
    u2d = f2(u)
    pa = f3(matmul(u2d, w["w_att"], mode="nn", name="proj_att"))
    pm = f3(matmul(u2d, w["w_ml"], mode="nn", name="proj_ml"))
    pg = matmul(u2d, w["w_gate"], mode="nn", name="proj_gate")
    os_, ls_ = [], []
    one_block = [SEQ // d == ATT_BLOCK for _, d in ATT_GROUPS]
    for g in range(3):
        o, l = (attn_fwd_classes if one_block[g] else attn_fwd)(pa, biases[g], qg, kg, g, f"attn_fwd{g}")
        os_.append(f2(o))
        ls_.append(f2(l))
    hg = mlstm_fwd(pm, cw8, cb8, bifp, gn4, "mlstm_fwd")
    w = {**w, **late_w(hg)}
    att, y_att = merge_att_out(os_, ls_, w["w_att_out"], "att_out")
    y_ml, z = ml_out_gate(f2(hg), w["w_ml_out"], pg, y_att, "ml_out")
    y, x1, u2 = out_proj_resid_modnorm(f2(z), w["w_out"], f2(x), gate1, small["norm2_g"], scale2, shift2, "out_proj")
    pre, hdn = matmul(u2, w["w_ff1"], mode="nn", name="ff1", out_dtypes=(BF16, BF16),
                      epi=lambda acc: (acc, jnp.square(jnp.maximum(acc, 0.0))))
    dx2, d_ffo, loss, d_gate2 = ff2_loss(hdn, w["w_ff2"], x1, gate2, f2(target), "ff2_loss")

    g_ff2 = matmul(hdn, d_ffo, mode="tn", name="g_ff2", out_dtypes=(BF16,))
    d_pre = matmul(d_ffo, w["w_ff2"], mode="nt", name="d_hdn", out_dtypes=(BF16,), extras=(pre,),
                   epi=lambda acc, p: (acc * (2.0 * jnp.maximum(p.astype(F32), 0.0)),))
    g_ff1 = matmul(u2, d_pre, mode="tn", name="g_ff1", out_dtypes=(BF16,))
    dx1, d_norm2, d_scale2, d_shift2, dy, d_gate1 = d_u_modnorm_bwd(
        d_pre, w["w_ff1"], None, x1, small["norm2_g"], scale2, shift2, dx2, y, gate1, "d_u2")
    g_out = matmul(f2(z), dy, mode="tn", name="g_out", out_dtypes=(BF16,))
    dpg, d_ya, d_ym = d_z_gate_bwd(dy, w["w_out"], pg, y_att, y_ml, "d_z")
    g_att_out = matmul(f2(att), f2(d_ya), mode="tn", name="g_att_out", out_dtypes=(BF16,))
    dmerge = d_att_merge_bwd(d_ya, w["w_att_out"], os_, ls_, "d_att")
    g_ml_out = matmul(f2(hg), f2(d_ym), mode="tn", name="g_ml_out", out_dtypes=(BF16,))
    d_hg = matmul(f2(d_ym), w["w_ml_out"], mode="nt", name="d_hg")
    started = early_g(dict(w_att_out=g_att_out, w_ml_out=g_ml_out, w_out=g_out, w_ff1=g_ff1, w_ff2=g_ff2))
    order = 0.0 if started is None else started[0, 0]
    dmerge = [f3(d) for d in dmerge]
    dpa = lax.empty((B, SEQ, ATT_COLS), BF16)
    d_rel = []
    d_qg = d_kg = None
    for g in range(3):
        dpa, dbias, dq_g, dk_g = (attn_bwd_classes if one_block[g] else attn_bwd)(
            pa, biases[g], qg + order, kg, dmerge[g], dmerge[3 + g], dpa, g, f"attn_bwd{g}")
        db8 = jnp.pad(dbias.reshape(4, -1), ((0, 4), (0, 0)))
        d_rel.append(bias_reduce(db8, onehots[g], f"bias_reduce{g}")[:4])
        d_qg = dq_g if d_qg is None else d_qg + dq_g
        d_kg = dk_g if d_kg is None else d_kg + dk_g
    dpm, dcw8, dcb8, dbifp, dgn4 = mlstm_bwd(pm, cw8, cb8, bifp, gn4 + order, f3(d_hg), "mlstm_bwd")
    g_w_att = matmul(u2d, f2(dpa), mode="tn", name="g_w_att", out_dtypes=(BF16,))
    g_w_ml = matmul(u2d, f2(dpm), mode="tn", name="g_w_ml", out_dtypes=(BF16,))
    g_w_gate = matmul(u2d, f2(dpg), mode="tn", name="g_w_gate", out_dtypes=(BF16,))
    started = w_in_g((g_w_att, g_w_ml, g_w_gate))
    du = matmul(f2(dpa), w["w_att"], mode="nt", name="d_u_att", after=() if started is None else (started,))
    du = matmul(f2(dpm), w["w_ml"], mode="nt", name="d_u_ml", extras=(du,), epi=lambda acc, e: (acc + e,))
    grad_x, d_norm1, d_scale1, d_shift1 = d_u_modnorm_bwd(
        f2(dpg), w["w_gate"], du, f2(x), small["norm1_g"], scale1, shift1, dx1, None, None, "d_u_gate")
    grad_x = f3(grad_x)

    d_mods = (d_shift1, d_scale1, d_gate1, d_shift2, d_scale2, d_gate2)
    dbif = dbifp.reshape(4, 128)
    small_g = dict(
        norm1_g=d_norm1, norm2_g=d_norm2,
        b_if=jnp.stack([dbif[:, 0:2].reshape(8), dbif[:, 2:4].reshape(8)]),
        conv_w=_unblk8(dcw8), conv_b=_unblk8(dcb8), q_norm_g=d_qg, k_norm_g=d_kg,
        rel_bias=jnp.concatenate(d_rel, axis=0).T,
        mlstm_norm_g=dgn4.reshape(1, 1024))
    return loss, grad_x, d_mods, big, small_g


_SMALL = (("b_ada", 6144), ("norm1_g", 1024), ("norm2_g", 1024), ("b_if", 16), ("conv_b", 1024),
          ("q_norm_g", 128), ("k_norm_g", 128), ("rel_bias", 384), ("mlstm_norm_g", 1024), ("conv_w", 4096))
_SMALL_ROWS = 120
_REPL = _SMALL[:-1]
_SMALL_SENT = _SMALL + (("loss", 1),)


def _pack(d, names, rows):
    flat = jnp.concatenate([d[k].reshape(-1) for k, _ in names])
    return jnp.pad(flat, (0, rows * 128 - flat.shape[0])).reshape(rows, 128)


def _unpack(slab, names, shapes):
    flat = slab.reshape(-1)
    out, off = {}, 0
    for k, nel in names:
        out[k] = flat[off:off + nel].reshape(shapes[k])
        off += nel
    return out


def kernel(x, c, w_ada, b_ada, norm1_g, norm2_g, w_in, b_if, conv_w, conv_b, q_norm_g, k_norm_g, rel_bias, mlstm_norm_g, w_att_out, w_ml_out, w_out, w_ff1, w_ff2, loss_target, m_w_ada, m_b_ada, m_norm1_g, m_norm2_g, m_w_in, m_b_if, m_conv_w, m_conv_b, m_q_norm_g, m_k_norm_g, m_rel_bias, m_mlstm_norm_g, m_w_att_out, m_w_ml_out, m_w_out, m_w_ff1, m_w_ff2, v_w_ada, v_b_ada, v_norm1_g, v_norm2_g, v_w_in, v_b_if, v_conv_w, v_conv_b, v_q_norm_g, v_k_norm_g, v_rel_bias, v_mlstm_norm_g, v_w_att_out, v_w_ml_out, v_w_out, v_w_ff1, v_w_ff2):
    P = dict(w_ada=w_ada, b_ada=b_ada, norm1_g=norm1_g, norm2_g=norm2_g, w_in=w_in, b_if=b_if, conv_w=conv_w,
             conv_b=conv_b, q_norm_g=q_norm_g, k_norm_g=k_norm_g, rel_bias=rel_bias, mlstm_norm_g=mlstm_norm_g,
             w_att_out=w_att_out, w_ml_out=w_ml_out, w_out=w_out, w_ff1=w_ff1, w_ff2=w_ff2)
    M = dict(w_ada=m_w_ada, b_ada=m_b_ada, norm1_g=m_norm1_g, norm2_g=m_norm2_g, w_in=m_w_in, b_if=m_b_if,
             conv_w=m_conv_w, conv_b=m_conv_b, q_norm_g=m_q_norm_g, k_norm_g=m_k_norm_g, rel_bias=m_rel_bias,
             mlstm_norm_g=m_mlstm_norm_g, w_att_out=m_w_att_out, w_ml_out=m_w_ml_out, w_out=m_w_out,
             w_ff1=m_w_ff1, w_ff2=m_w_ff2)
    V = dict(w_ada=v_w_ada, b_ada=v_b_ada, norm1_g=v_norm1_g, norm2_g=v_norm2_g, w_in=v_w_in, b_if=v_b_if,
             conv_w=v_conv_w, conv_b=v_conv_b, q_norm_g=v_q_norm_g, k_norm_g=v_k_norm_g, rel_bias=v_rel_bias,
             mlstm_norm_g=v_mlstm_norm_g, w_att_out=v_w_att_out, w_ml_out=v_w_ml_out, w_out=v_w_out,
             w_ff1=v_w_ff1, w_ff2=v_w_ff2)
    names = list(P)
    shapes = {k: P[k].shape for k in names}
    B = x.shape[0]
    me = _index(_mesh_pos())

    big_names = ("w_in", "w_att_out", "w_ml_out", "w_out", "w_ff1", "w_ff2")
    shards = cast_bf16([P[k][0] for k in big_names], "cast_w")
    (rows8,) = all_to_all([shards[0].reshape(N_DEV, D_MODEL // N_DEV, W_IN_SHARD)], "w_in_rows_exchange")
    slab = split_w_in(rows8.transpose(1, 0, 2).reshape(D_MODEL // N_DEV, D_IN))
    a8, m8, g8, c8, conv_w8 = all_gather_two_level(list(slab) + [c, conv_w[0]], "gather_w_in")
    c_all = c8.reshape(N_DEV * B, D_MODEL)
    conv_w_full = conv_w8.transpose(1, 0, 2).reshape(4, 1024)
    w = dict(w_att=a8.reshape(D_MODEL, ATT_COLS), w_ml=m8.reshape(D_MODEL, ML_COLS),
             w_gate=g8.reshape(D_MODEL, GATE_COLS))

    (silu_c,) = small_call(lambda a: (_silu(a),), [c_all], [jax.ShapeDtypeStruct(c_all.shape, F32)], "silu_c")
    b_ada_cols = lax.dynamic_slice(b_ada, (0, me * 768), (1, 768))
    ada_cols = matmul(silu_c, w_ada[0], mode="nn", name="ada", extras=(jnp.broadcast_to(b_ada_cols, (N_DEV * B, 768)),),
                      epi=lambda acc, bb: (acc + bb,))
    (ada_t,) = all_to_all([ada_cols.reshape(N_DEV, B, 768)], "ada_exchange")
    ada = ada_t.transpose(1, 0, 2).reshape(B, 6 * D_MODEL)
    mods = tuple(ada[:, i * D_MODEL:(i + 1) * D_MODEL].reshape(B, 1, D_MODEL) for i in range(6))

    late_handle, late_order = exchange_start(shards[1:], "gather_late_start", False, after=(ada_t,))

    def late_w(after):
        lands = exchange_wait(late_handle, after, "gather_late_wait", False)
        gw = dict(zip(big_names[1:], [_own_slot(l, s) for l, s in zip(lands, shards[1:])]))
        return dict(w_att_out=gw["w_att_out"].transpose(1, 0, 2).reshape(512, D_MODEL),
                    w_ml_out=gw["w_ml_out"].reshape(D_MODEL, D_MODEL), w_out=gw["w_out"].reshape(D_MODEL, D_MODEL),
                    w_ff1=gw["w_ff1"].transpose(1, 0, 2).reshape(D_MODEL, D_FF),
                    w_ff2=gw["w_ff2"].reshape(D_FF, D_MODEL))

    pending = {}

    def send_grads(key, blocks, name):
        handle, order = exchange_start(blocks, name, True)
        pending[key] = (handle, [lax.dynamic_index_in_dim(b, me, 0, keepdims=False) for b in blocks])
        return order

    def early_g(g):
        return send_grads("late", [g["w_att_out"].reshape(512, N_DEV, 128).transpose(1, 0, 2),
                                   g["w_ml_out"].reshape(N_DEV, 128, D_MODEL), g["w_out"].reshape(N_DEV, 128, D_MODEL),
                                   g["w_ff1"].reshape(D_MODEL, N_DEV, 512).transpose(1, 0, 2),
                                   g["w_ff2"].reshape(N_DEV, 512, D_MODEL)], "grad_late_start")

    def w_in_g(parts):
        return send_grads("w_in", [g.reshape(N_DEV, D_MODEL // N_DEV, g.shape[1]) for g in parts], "grad_w_in_start")

    def recv_grads(key, after, name):
        handle, own = pending[key]
        return [_own_slot(l, o) for l, o in zip(exchange_wait(handle, after, name, True), own)]

    small = dict(norm1_g=norm1_g + late_order[0, 0], norm2_g=norm2_g, b_if=b_if[0], conv_w=conv_w_full, conv_b=conv_b,
                 q_norm_g=q_norm_g, k_norm_g=k_norm_g, rel_bias=rel_bias, mlstm_norm_g=mlstm_norm_g)
    loss, grad_x, d_mods, _, small_g = local_step(x, loss_target, mods, w, small, late_w, early_g, w_in_g)

    d_ada = jnp.concatenate([d.reshape(B, D_MODEL) for d in d_mods], axis=1)
    (small_g["b_ada"],) = small_call(lambda a: (jnp.sum(a, axis=0, keepdims=True),), [d_ada],
                                     [jax.ShapeDtypeStruct((1, 6144), F32)], "g_b_ada_local")
    small_g["loss"] = loss
    slabs = sum_slabs(recv_grads("w_in", grad_x, "grad_w_in_wait"), "w_in_slab_sum")
    w_in_cols = merge_w_in(*slabs).reshape(D_MODEL // N_DEV, N_DEV, W_IN_SHARD).transpose(1, 0, 2)
    d_ada_t, small_parts = _exchange(
        [d_ada.reshape(B, N_DEV, 768).transpose(1, 0, 2), _pack(small_g, _SMALL_SENT, _SMALL_ROWS)],
        "small_exchange", [True, False])
    cols_handle, _ = exchange_start([w_in_cols], "grad_w_in_cols_start", True, after=(small_parts,))
    d_ada_cols = d_ada_t.reshape(N_DEV * B, 768)
    g_w_ada = matmul(silu_c, d_ada_cols, mode="tn", name="g_w_ada")

    recv = recv_grads("late", grad_x, "grad_late_wait")
    small_sum = sum_parts(small_parts, "small_grad_sum")
    sg = _unpack(small_sum, _SMALL_SENT, {**{k: shapes[k] for k, _ in _REPL}, "conv_w": (4, 1024), "loss": ()})
    loss = sg["loss"]

    G, Dl, NM, NV = {}, {}, {}, {}
    for k, parts in zip(big_names[1:], recv):
        g, d, nm, nv = adamw(P[k][0], parts, M[k][0], V[k][0], f"adamw_{k}", parts=True)
        G[k], Dl[k], NM[k], NV[k] = g[None], d[None], nm[None], nv[None]
    g, d, nm, nv = adamw(w_ada[0], g_w_ada, m_w_ada[0], v_w_ada[0], "adamw_w_ada")
    G["w_ada"], Dl["w_ada"], NM["w_ada"], NV["w_ada"] = g[None], d[None], nm[None], nv[None]
    g_conv = lax.dynamic_slice(sg["conv_w"], (0, me * 128), (4, 128))
    g, d, nm, nv = adamw(conv_w[0], g_conv, m_conv_w[0], v_conv_w[0], "adamw_conv_w")
    G["conv_w"], Dl["conv_w"], NM["conv_w"], NV["conv_w"] = g[None], d[None], nm[None], nv[None]
    flat2 = lambda a: a.reshape(-1, a.shape[-1])
    keys = [k for k, _ in _REPL]
    upd = adamw_many([flat2(P[k]) for k in keys], [flat2(sg[k]) for k in keys], [flat2(M[k]) for k in keys],
                     [flat2(V[k]) for k in keys], "adamw_small")
    for j, k in enumerate(keys):
        G[k] = sg[k]
        Dl[k], NM[k], NV[k] = [upd[3 * j + t].reshape(shapes[k]) for t in range(3)]

    others = [NV[k] for k in big_names[1:]] + [NV["w_ada"], NV["conv_w"], upd[-1]]
    (g_w_in,) = exchange_wait(cols_handle, others, "grad_w_in_cols_wait", True)
    g_w_in = _own_slot(g_w_in, lax.dynamic_index_in_dim(w_in_cols, me, 0, keepdims=False))
    g, d, nm, nv = adamw(w_in[0], g_w_in.reshape(D_MODEL, W_IN_SHARD), m_w_in[0], v_w_in[0], "adamw_w_in")
    G["w_in"], Dl["w_in"], NM["w_in"], NV["w_in"] = g[None], d[None], nm[None], nv[None]

    return (loss, grad_x, *[G[k] for k in names], *[Dl[k] for k in names], *[NM[k] for k in names],
            *[NV[k] for k in names])
```

```python
import functools
import math

import numpy as np
import jax
import jax.numpy as jnp
from jax import lax
from jax.experimental import pallas as pl
from jax.experimental.pallas import tpu as pltpu

F32 = jnp.float32
BF16 = jnp.bfloat16

N_DEV = 8
D_MODEL = 1024
SEQ = 2048
ATT_GROUPS = ((128, 1), (512, 4), (2048, 16))
N_ATT_HEADS = 12
ATT_BLOCK = 128
HEAD_DIM = 128
ML_HEADS = 8
ML_PAIRS = 4
ML_CHUNK = 64
N_CHUNKS = SEQ // ML_CHUNK
N_BUCKETS = 32
MAX_DISTANCE = 2048
D_FF = 4096
D_IN = 9744
EPS = 1e-6

ADAM_LR = 0.001
ADAM_B1 = 0.9
ADAM_B2 = 0.999
ADAM_EPS = 1e-08
ADAM_WD = 0.01
ADAM_STEP = 10

ATT_HEAD_COLS = 3 * HEAD_DIM
ATT_COLS = N_ATT_HEADS * ATT_HEAD_COLS
ML_PAIR_COLS = 896
ML_COLS = ML_PAIRS * ML_PAIR_COLS
GATE_COLS = 2 * D_MODEL
W_IN_SHARD = D_IN // N_DEV

VMEM_LIMIT = 60 * 1024 * 1024


def _cparams(**kw):
    return pltpu.CompilerParams(vmem_limit_bytes=VMEM_LIMIT, **kw)


_NN = ((1,), (0,))
_NT = ((1,), (1,))
_TN = ((0,), (0,))


def _mxu(a, b, dims):
    return lax.dot_general(a.astype(BF16), b.astype(BF16), (dims, ((), ())), preferred_element_type=F32)


@jax.custom_vjp
def bdot_nn(a, b):
    return _mxu(a, b, _NN)


def _nn_fwd(a, b):
    return _mxu(a, b, _NN), (a, b)


def _nn_bwd(res, g):
    a, b = res
    return _mxu(g, b, _NT), _mxu(a, g, _TN)


bdot_nn.defvjp(_nn_fwd, _nn_bwd)


@jax.custom_vjp
def bdot_nt(a, b):
    return _mxu(a, b, _NT)


def _nt_fwd(a, b):
    return _mxu(a, b, _NT), (a, b)


def _nt_bwd(res, g):
    a, b = res
    return _mxu(g, b, _NN), _mxu(g, a, _TN)


bdot_nt.defvjp(_nt_fwd, _nt_bwd)


def _doth(a, b, dims=_NN):
    return lax.dot_general(a, b, (dims, ((), ())), precision=lax.Precision.HIGHEST, preferred_element_type=F32)


def _rms(x):
    return x * lax.rsqrt(jnp.mean(x * x, axis=-1, keepdims=True) + EPS)


def _pick(n, cands):
    for t in cands:
        if n % t == 0:
            return t
    raise ValueError(f"no tile for {n}")


MM_TILE_M = (1024, 512, 256, 128, 64, 32, 16, 8)
MM_TILE_N = (2048, 1792, 1536, 1024, 768, 512, 256, 128)
MM_TILE_K = (2048, 1792, 1536, 1024, 512, 256, 128, 64, 32)

def matmul(a, b, *, mode, name, out_dtypes=(F32,), epi=None, extras=(), after=()):
    if mode == "nn":
        (M, K), (K2, N) = a.shape, b.shape
    elif mode == "nt":
        (M, K), (N, K2) = a.shape, b.shape
    else:
        (K, M), (K2, N) = a.shape, b.shape
    assert K == K2, (a.shape, b.shape, mode)
    tm = _pick(M, MM_TILE_M)
    tn = _pick(N, MM_TILE_N)
    tk = _pick(K, MM_TILE_K)
    nk = K // tk
    n_ex = len(extras)
    n_out = len(out_dtypes)
    dims = {"nn": _NN, "nt": _NT, "tn": _TN}[mode]

    def finish(r, ex_refs, out_refs):
        outs = epi(r, *[e[...] for e in ex_refs]) if epi is not None else (r,)
        for o_ref, o in zip(out_refs, outs):
            o_ref[...] = o.astype(o_ref.dtype)

    def body(*refs):
        a_ref, b_ref = refs[0], refs[1]
        ex_refs = refs[2:2 + n_ex]
        out_refs = refs[2 + n_ex + len(after):2 + n_ex + len(after) + n_out]
        if nk == 1:
            finish(_mxu(a_ref[...], b_ref[...], dims), ex_refs, out_refs)
            return
        acc = refs[2 + n_ex + len(after) + n_out]
        k = pl.program_id(2)

        @pl.when(k == 0)
        def _():
            acc[...] = jnp.zeros_like(acc)

        acc[...] += _mxu(a_ref[...], b_ref[...], dims)

        @pl.when(k == nk - 1)
        def _():
            finish(acc[...], ex_refs, out_refs)

    if mode == "nn":
        a_spec = pl.BlockSpec((tm, tk), lambda i, j, k: (i, k))
        b_spec = pl.BlockSpec((tk, tn), lambda i, j, k: (k, j))
    elif mode == "nt":
        a_spec = pl.BlockSpec((tm, tk), lambda i, j, k: (i, k))
        b_spec = pl.BlockSpec((tn, tk), lambda i, j, k: (j, k))
    else:
        a_spec = pl.BlockSpec((tk, tm), lambda i, j, k: (k, i))
        b_spec = pl.BlockSpec((tk, tn), lambda i, j, k: (k, j))
    o_spec = pl.BlockSpec((tm, tn), lambda i, j, k: (i, j))
    res = pl.pallas_call(
        body,
        name=name,
        grid=(M // tm, N // tn, nk),
        in_specs=[a_spec, b_spec] + [o_spec] * n_ex + [pl.BlockSpec(memory_space=pl.ANY)] * len(after),
        out_specs=[o_spec] * n_out,
        out_shape=[jax.ShapeDtypeStruct((M, N), dt) for dt in out_dtypes],
        scratch_shapes=[pltpu.VMEM((tm, tn), F32)] if nk > 1 else [],
        compiler_params=_cparams(),
    )(a, b, *extras, *after)
    return res[0] if n_out == 1 else tuple(res)


ROW_MM_TILE = 512


def row_matmul(a, b, *, mode, name, extras, outs, epi):
    (M, K) = a.shape
    N = b.shape[1] if mode == "nn" else b.shape[0]
    tm = ROW_MM_TILE
    tk = _pick(K, MM_TILE_K)
    nk = K // tk
    n_ex, n_out = len(extras), len(outs)

    def body(*refs):
        a_ref, b_ref = refs[0], refs[1]
        ex_refs = refs[2:2 + n_ex]
        out_refs = refs[2 + n_ex:2 + n_ex + n_out]
        i = pl.program_id(0)
        dims = _NN if mode == "nn" else _NT
        if nk == 1:
            epi(_mxu(a_ref[...], b_ref[...], dims), i, ex_refs, out_refs)
            return
        acc = refs[2 + n_ex + n_out]
        k = pl.program_id(1)

        @pl.when(k == 0)
        def _():
            acc[...] = jnp.zeros_like(acc)

        acc[...] += _mxu(a_ref[...], b_ref[...], dims)

        @pl.when(k == nk - 1)
        def _():
            epi(acc[...], i, ex_refs, out_refs)

    def lift(index_map):
        return lambda i, k: index_map(i)

    b_spec = pl.BlockSpec((tk, N), lambda i, k: (k, 0)) if mode == "nn" else pl.BlockSpec((N, tk), lambda i, k: (0, k))
    res = pl.pallas_call(
        body, name=name, grid=(M // tm, nk),
        in_specs=[pl.BlockSpec((tm, tk), lambda i, k: (i, k)), b_spec]
        + [pl.BlockSpec(blk, lift(im)) for _, blk, im in extras],
        out_specs=[pl.BlockSpec(blk, lift(im)) for _, _, blk, im in outs],
        out_shape=[jax.ShapeDtypeStruct(shape, dt) for shape, dt, _, _ in outs],
        scratch_shapes=[pltpu.VMEM((tm, N), F32)] if nk > 1 else [],
        compiler_params=_cparams(),
    )(a, b, *[e[0] for e in extras])
    return tuple(res)


def _rows(arr):
    return (arr, (ROW_MM_TILE, arr.shape[1]), lambda i: (i, 0))


def _rows_out(T, dtype):
    return ((T, D_MODEL), dtype, (ROW_MM_TILE, D_MODEL), lambda i: (i, 0))


def _per_seq(arr):
    return (arr, (1, 1, D_MODEL), lambda i: (i // (SEQ // ROW_MM_TILE), 0, 0))


def _per_seq_out(B):
    return ((B, 1, D_MODEL), F32, (1, 1, D_MODEL), lambda i: (i // (SEQ // ROW_MM_TILE), 0, 0))


def _first_tile_of_seq(i):
    return i % (SEQ // ROW_MM_TILE) == 0


def small_call(fn, inputs, out_shapes, name):
    n_in = len(inputs)

    def body(*refs):
        outs = fn(*[r[...] for r in refs[:n_in]])
        for o_ref, o in zip(refs[n_in:], outs):
            o_ref[...] = o.astype(o_ref.dtype)

    res = pl.pallas_call(body, name=name, out_shape=list(out_shapes), compiler_params=_cparams())(*inputs)
    return tuple(res)


ROW_TILE = 512


def _modnorm(x, g, scale, shift):
    return _rms(x) * g * (1.0 + scale) + shift


def _row_spec(width):
    return pl.BlockSpec((1, ROW_TILE, width), lambda b, i: (b, i, 0))


def _mod_spec():
    return pl.BlockSpec((1, 1, D_MODEL), lambda b, i: (b, 0, 0))


def _vec_spec():
    return pl.BlockSpec((1, D_MODEL), lambda b, i: (0, 0))


def modnorm_fwd(x, g, scale, shift, name):
    B, S, D = x.shape

    def body(x_ref, g_ref, sc_ref, sh_ref, u_ref):
        u_ref[0] = _modnorm(x_ref[0], g_ref[...], sc_ref[0], sh_ref[0]).astype(BF16)

    return pl.pallas_call(
        body, name=name, grid=(B, S // ROW_TILE),
        in_specs=[_row_spec(D), _vec_spec(), _mod_spec(), _mod_spec()],
        out_specs=_row_spec(D),
        out_shape=jax.ShapeDtypeStruct((B, S, D), BF16),
        compiler_params=_cparams(),
    )(x, g, scale, shift)


def _gain_spec(g):
    return (g, (1, D_MODEL), lambda i: (0, 0))


def out_proj_resid_modnorm(z, w_out, x, gate, g, scale, shift, name):
    T = z.shape[0]

    def epi(acc, i, ex, out):
        x_ref, gt_ref, g_ref, sc_ref, sh_ref = ex
        y_ref, x1_ref, u_ref = out
        y_ref[...] = acc
        x1 = x_ref[...] + gt_ref[0] * acc
        x1_ref[...] = x1
        u_ref[...] = _modnorm(x1, g_ref[...], sc_ref[0], sh_ref[0]).astype(BF16)

    return row_matmul(z, w_out, mode="nn", name=name,
                      extras=[_rows(x), _per_seq(gate), _gain_spec(g), _per_seq(scale), _per_seq(shift)],
                      outs=[_rows_out(T, F32), _rows_out(T, F32), _rows_out(T, BF16)], epi=epi)


def ff2_loss(hdn, w_ff2, x1, gate, target, name):
    T = hdn.shape[0]

    def epi(acc, i, ex, out):
        x_ref, gt_ref, t_ref = ex
        dx_ref, dffo_ref, loss_ref, dg_ref = out

        @pl.when(i == 0)
        def _():
            loss_ref[...] = jnp.zeros_like(loss_ref)

        @pl.when(_first_tile_of_seq(i))
        def _():
            dg_ref[...] = jnp.zeros_like(dg_ref)

        err = x_ref[...] + gt_ref[0] * acc - t_ref[...]
        dx = err * (1.0 / D_MODEL)
        dx_ref[...] = dx
        dffo_ref[...] = (gt_ref[0] * dx).astype(BF16)
        loss_ref[...] += 0.5 * jnp.sum(jnp.mean(err * err, axis=-1, keepdims=True), axis=0, keepdims=True)
        dg_ref[0] += jnp.sum(dx * acc, axis=0, keepdims=True)

    return row_matmul(hdn, w_ff2, mode="nn", name=name,
                      extras=[_rows(x1), _per_seq(gate), _rows(target)],
                      outs=[_rows_out(T, F32), _rows_out(T, BF16), ((1, 1), F32, (1, 1), lambda i: (0, 0)),
                            _per_seq_out(T // SEQ)], epi=epi)


def d_u_modnorm_bwd(a, w, du_prev, x, g, scale, shift, dx_res, y, gate, name):
    T = a.shape[0]
    B = T // SEQ
    n_prev, resid = int(du_prev is not None), y is not None

    def epi(acc, i, ex, out):
        x_ref, g_ref, sc_ref, sh_ref, dr_ref = ex[n_prev:n_prev + 5]
        dx_ref, dg_ref, dsc_ref, dsh_ref = out[:4]

        @pl.when(i == 0)
        def _():
            dg_ref[...] = jnp.zeros_like(dg_ref)

        @pl.when(_first_tile_of_seq(i))
        def _():
            for r in out[2:4] + out[5:]:
                r[...] = jnp.zeros_like(r)

        du = acc + ex[0][...] if n_prev else acc
        _, vjp = jax.vjp(_modnorm, x_ref[...], g_ref[...], sc_ref[0], sh_ref[0])
        dxn, dg, dsc, dsh = vjp(du)
        dx = dxn + dr_ref[...]
        dx_ref[...] = dx
        dg_ref[...] += dg
        dsc_ref[0] += dsc
        dsh_ref[0] += dsh
        if resid:
            y_ref, gt_ref = ex[n_prev + 5:]
            out[4][...] = (gt_ref[0] * dx).astype(BF16)
            out[5][0] += jnp.sum(dx * y_ref[...], axis=0, keepdims=True)

    extras = ([_rows(du_prev)] if n_prev else []) + [_rows(x), _gain_spec(g), _per_seq(scale), _per_seq(shift),
                                                     _rows(dx_res)] + ([_rows(y), _per_seq(gate)] if resid else [])
    outs = [_rows_out(T, F32), ((1, D_MODEL), F32, (1, D_MODEL), lambda i: (0, 0)), _per_seq_out(B), _per_seq_out(B)]
    outs += [_rows_out(T, BF16), _per_seq_out(B)] if resid else []
    return row_matmul(a, w, mode="nt", name=name, extras=extras, outs=outs, epi=epi)


def _bucket_table(dilation):
    i = np.arange(ATT_BLOCK)[:, None]
    j = np.arange(2 * ATT_BLOCK)[None, :]
    delta = ATT_BLOCK + i - j
    dist = np.maximum(delta, 0) * dilation
    max_exact = N_BUCKETS // 2
    d = np.maximum(dist, max_exact).astype(np.float32)
    large = max_exact + (np.log(d / np.float32(max_exact)) / np.float32(math.log(MAX_DISTANCE / max_exact))
                         * np.float32(N_BUCKETS - max_exact)).astype(np.int32)
    large = np.minimum(large, N_BUCKETS - 1)
    return np.where(dist < max_exact, dist, large).astype(np.int32)


def _bucket_onehot(dilation):
    bt = jnp.asarray(_bucket_table(dilation).reshape(1, -1))
    return (bt == jnp.arange(N_BUCKETS, dtype=jnp.int32)[:, None]).astype(F32)


def bias_expand(rel_t, onehot, name):
    def fn(r, oh):
        return (_doth(r, oh),)
    return small_call(fn, [rel_t, onehot], [jax.ShapeDtypeStruct((rel_t.shape[0], onehot.shape[1]), F32)], name)[0]


def bias_reduce(dbias_flat, onehot, name):
    def fn(db, oh):
        return (_doth(db, oh, _NT),)
    return small_call(fn, [dbias_flat, onehot], [jax.ShapeDtypeStruct((dbias_flat.shape[0], N_BUCKETS), F32)], name)[0]


def _qk_norm(x, g):
    return _rms(x) * g


def _masked_bias(bias):
    i = lax.broadcasted_iota(jnp.int32, (ATT_BLOCK, 2 * ATT_BLOCK), 0)
    j = lax.broadcasted_iota(jnp.int32, (ATT_BLOCK, 2 * ATT_BLOCK), 1)
    bm = jnp.where(jnp.logical_and(j >= i, j <= i + ATT_BLOCK), bias, -jnp.inf)
    return bm, bm[:, ATT_BLOCK:]


def _attn_tile(qn, kn, v, bias):
    s = bdot_nt(qn, kn) * (HEAD_DIM ** -0.5) + bias
    m = lax.stop_gradient(jnp.max(s, axis=-1, keepdims=True))
    p = jnp.exp(s - m)
    l = jnp.sum(p, axis=-1, keepdims=True)
    o = bdot_nn(p, v) / l
    lse = jnp.broadcast_to(m + jnp.log(l), (ATT_BLOCK, HEAD_DIM))
    return o, lse


def _attn_tiles(dilation, rows=SEQ):
    nb = rows // dilation // ATT_BLOCK
    return [(r, n) for r in range(dilation) for n in range(nb)]


def _attn_rows(r, n, dilation, nblk=1):
    if dilation == 1:
        return pl.ds(r + n * ATT_BLOCK, nblk * ATT_BLOCK)
    return pl.ds(r + n * ATT_BLOCK * dilation, nblk * ATT_BLOCK, stride=dilation)


_QL, _KL, _VL = slice(0, 128), slice(128, 256), slice(256, 384)


def _qkv_specs(hb):
    return [pl.BlockSpec((None, SEQ, HEAD_DIM), functools.partial(lambda b, h, j: (b, 0, 3 * (hb + h) + j), j=j))
            for j in range(3)]


def attn_fwd(pa, bias, qg, kg, group, name):
    B = pa.shape[0]
    dilation = ATT_GROUPS[group][1]
    hb = group * 4

    def body(q_ref, k_ref, v_ref, b_ref, qg_ref, kg_ref, o_ref, l_ref, qn_s, kn_s):
        qn_s[...] = _qk_norm(q_ref[...], qg_ref[...])
        kn_s[...] = _qk_norm(k_ref[...], kg_ref[...])
        bias_all, bias_first = _masked_bias(b_ref[0])
        for (r, n) in _attn_tiles(dilation):
            rows = _attn_rows(r, n, dilation)
            if n == 0:
                krows, bias_t = rows, bias_first
            else:
                krows, bias_t = _attn_rows(r, n - 1, dilation, 2), bias_all
            o, lse = _attn_tile(qn_s[rows, :], kn_s[krows, :], v_ref[krows, :], bias_t)
            o_ref[rows, :] = o
            l_ref[rows, :] = lse

    head_out = pl.BlockSpec((None, SEQ, HEAD_DIM), lambda b, h: (b, 0, h))
    return pl.pallas_call(
        body, name=name, grid=(B, 4),
        in_specs=_qkv_specs(hb) + [
                  pl.BlockSpec((1, ATT_BLOCK, 2 * ATT_BLOCK), lambda b, h: (hb + h, 0, 0)),
                  pl.BlockSpec((1, HEAD_DIM), lambda b, h: (0, 0)),
                  pl.BlockSpec((1, HEAD_DIM), lambda b, h: (0, 0))],
        out_specs=[head_out, head_out],
        out_shape=[jax.ShapeDtypeStruct((B, SEQ, 512), F32), jax.ShapeDtypeStruct((B, SEQ, 512), F32)],
        scratch_shapes=[pltpu.VMEM((SEQ, HEAD_DIM), F32)] * 2,
        compiler_params=_cparams(),
    )(pa, pa, pa, bias, qg, kg)


def attn_bwd(pa, bias, qg, kg, do, dlse, dpa, group, name):
    B = pa.shape[0]
    dilation = ATT_GROUPS[group][1]
    hb = group * 4

    def body(q_ref, k_ref, v_ref, b_ref, qg_ref, kg_ref, do_ref, dl_ref, dpa_in,
             dp_ref, db_ref, dqg_ref, dkg_ref, qn_s, kn_s, dq_s, dk_s, dv_s):
        del dpa_in
        h_id = pl.program_id(1)

        @pl.when(jnp.logical_and(pl.program_id(0) == 0, h_id == 0))
        def _():
            db_ref[...] = jnp.zeros_like(db_ref)
            dqg_ref[...] = jnp.zeros_like(dqg_ref)
            dkg_ref[...] = jnp.zeros_like(dkg_ref)

        dk_s[...] = jnp.zeros_like(dk_s)
        dv_s[...] = jnp.zeros_like(dv_s)
        qn_s[...] = _qk_norm(q_ref[...], qg_ref[...])
        kn_s[...] = _qk_norm(k_ref[...], kg_ref[...])
        bias_all, bias_first = _masked_bias(b_ref[0])
        for (r, n) in _attn_tiles(dilation):
            rows = _attn_rows(r, n, dilation)
            if n == 0:
                krows, bias_t = rows, bias_first
            else:
                krows, bias_t = _attn_rows(r, n - 1, dilation, 2), bias_all
            _, vjp = jax.vjp(_attn_tile, qn_s[rows, :], kn_s[krows, :], v_ref[krows, :], bias_t)
            dqn, dkn, dv, dbias = vjp((do_ref[rows, :], dl_ref[rows, :]))
            dq_s[rows, :] = dqn
            dk_s[krows, :] += dkn
            dv_s[krows, :] += dv
            if n == 0:
                db_ref[h_id, :, ATT_BLOCK:] += dbias
            else:
                db_ref[h_id] += dbias
        for x_ref, g_ref, d_s, dg_ref, lanes in ((q_ref, qg_ref, dq_s, dqg_ref, _QL), (k_ref, kg_ref, dk_s, dkg_ref, _KL)):
            _, vjp = jax.vjp(_qk_norm, x_ref[...], g_ref[...])
            dx, dg = vjp(d_s[...])
            dp_ref[0, :, lanes] = dx.astype(BF16)
            dg_ref[...] += dg
        dp_ref[0, :, _VL] = dv_s[...].astype(BF16)

    const2 = lambda b, h: (0, 0)
    head_in = pl.BlockSpec((None, SEQ, HEAD_DIM), lambda b, h: (b, 0, h))
    head_blk = pl.BlockSpec((1, SEQ, ATT_HEAD_COLS), lambda b, h: (b, 0, hb + h))
    return pl.pallas_call(
        body, name=name, grid=(B, 4),
        in_specs=_qkv_specs(hb) + [
                  pl.BlockSpec((1, ATT_BLOCK, 2 * ATT_BLOCK), lambda b, h: (hb + h, 0, 0)),
                  pl.BlockSpec((1, HEAD_DIM), const2), pl.BlockSpec((1, HEAD_DIM), const2),
                  head_in, head_in,
                  pl.BlockSpec(memory_space=pl.ANY)],
        out_specs=[head_blk,
                   pl.BlockSpec((4, ATT_BLOCK, 2 * ATT_BLOCK), lambda b, h: (0, 0, 0)),
                   pl.BlockSpec((1, HEAD_DIM), const2), pl.BlockSpec((1, HEAD_DIM), const2)],
        out_shape=[jax.ShapeDtypeStruct(dpa.shape, BF16),
                   jax.ShapeDtypeStruct((4, ATT_BLOCK, 2 * ATT_BLOCK), F32),
                   jax.ShapeDtypeStruct((1, HEAD_DIM), F32), jax.ShapeDtypeStruct((1, HEAD_DIM), F32)],
        scratch_shapes=[pltpu.VMEM((SEQ, HEAD_DIM), F32)] * 5,
        input_output_aliases={8: 0},
        compiler_params=_cparams(),
    )(pa, pa, pa, bias, qg, kg, do, dlse, dpa)


def _attn_classes(q, k, v, bias, qg, kg):
    s = cdot_nt(_qk_norm(q, qg), _qk_norm(k, kg)) * (HEAD_DIM ** -0.5) + bias
    m = lax.stop_gradient(jnp.max(s, axis=-1, keepdims=True))
    p = jnp.exp(s - m)
    l = jnp.sum(p, axis=-1, keepdims=True)
    o = cdot_nn(p, v) / l
    return o, jnp.broadcast_to(m + jnp.log(l), o.shape)


def _gather_classes(src_ref, dst_s, dilation):
    for r in range(dilation):
        dst_s[r] = src_ref[pl.ds(r, ATT_BLOCK, stride=dilation), :]


def _scatter_classes(src_s, dst_ref, dilation):
    for r in range(dilation):
        dst_ref[pl.ds(r, ATT_BLOCK, stride=dilation), :] = src_s[r]


def attn_fwd_classes(pa, bias, qg, kg, group, name):
    B = pa.shape[0]
    dilation = ATT_GROUPS[group][1]
    hb = group * 4

    def body(q_ref, k_ref, v_ref, b_ref, qg_ref, kg_ref, o_ref, l_ref, q_s, k_s, v_s):
        _gather_classes(q_ref, q_s, dilation)
        _gather_classes(k_ref, k_s, dilation)
        _gather_classes(v_ref, v_s, dilation)
        o, lse = _attn_classes(q_s[...], k_s[...], v_s[...], _masked_bias(b_ref[0])[1], qg_ref[...], kg_ref[...])
        q_s[...], k_s[...] = o, lse
        _scatter_classes(q_s, o_ref, dilation)
        _scatter_classes(k_s, l_ref, dilation)

    head_out = pl.BlockSpec((None, SEQ, HEAD_DIM), lambda b, h: (b, 0, h))
    return pl.pallas_call(
        body, name=name, grid=(B, 4),
        in_specs=_qkv_specs(hb) + [
                  pl.BlockSpec((1, ATT_BLOCK, 2 * ATT_BLOCK), lambda b, h: (hb + h, 0, 0)),
                  pl.BlockSpec((1, HEAD_DIM), lambda b, h: (0, 0)),
                  pl.BlockSpec((1, HEAD_DIM), lambda b, h: (0, 0))],
        out_specs=[head_out, head_out],
        out_shape=[jax.ShapeDtypeStruct((B, SEQ, 512), F32), jax.ShapeDtypeStruct((B, SEQ, 512), F32)],
        scratch_shapes=[pltpu.VMEM((dilation, ATT_BLOCK, HEAD_DIM), F32)] * 3,
        compiler_params=_cparams(),
    )(pa, pa, pa, bias, qg, kg)


def attn_bwd_classes(pa, bias, qg, kg, do, dlse, dpa, group, name):
    B = pa.shape[0]
    dilation = ATT_GROUPS[group][1]
    hb = group * 4

    def body(q_ref, k_ref, v_ref, b_ref, qg_ref, kg_ref, do_ref, dl_ref, dpa_in,
             dp_ref, db_ref, dqg_ref, dkg_ref, q_s, k_s, v_s, do_s, dl_s, rows_s):
        del dpa_in
        h_id = pl.program_id(1)

        @pl.when(jnp.logical_and(pl.program_id(0) == 0, h_id == 0))
        def _():
            db_ref[...] = jnp.zeros_like(db_ref)
            dqg_ref[...] = jnp.zeros_like(dqg_ref)
            dkg_ref[...] = jnp.zeros_like(dkg_ref)

        for src, dst in ((q_ref, q_s), (k_ref, k_s), (v_ref, v_s), (do_ref, do_s), (dl_ref, dl_s)):
            _gather_classes(src, dst, dilation)
        _, vjp = jax.vjp(_attn_classes, q_s[...], k_s[...], v_s[...], _masked_bias(b_ref[0])[1],
                         qg_ref[...], kg_ref[...])
        dq, dk, dv, dbias, dqg, dkg = vjp((do_s[...], dl_s[...]))
        db_ref[h_id, :, ATT_BLOCK:] += dbias
        dqg_ref[...] += dqg
        dkg_ref[...] += dkg
        for d, lanes in ((dq, _QL), (dk, _KL), (dv, _VL)):
            q_s[...] = d
            _scatter_classes(q_s, rows_s, dilation)
            dp_ref[0, :, lanes] = rows_s[...].astype(BF16)

    const2 = lambda b, h: (0, 0)
    head_in = pl.BlockSpec((None, SEQ, HEAD_DIM), lambda b, h: (b, 0, h))
    head_blk = pl.BlockSpec((1, SEQ, ATT_HEAD_COLS), lambda b, h: (b, 0, hb + h))
    return pl.pallas_call(
        body, name=name, grid=(B, 4),
        in_specs=_qkv_specs(hb) + [
                  pl.BlockSpec((1, ATT_BLOCK, 2 * ATT_BLOCK), lambda b, h: (hb + h, 0, 0)),
                  pl.BlockSpec((1, HEAD_DIM), const2), pl.BlockSpec((1, HEAD_DIM), const2),
                  head_in, head_in,
                  pl.BlockSpec(memory_space=pl.ANY)],
        out_specs=[head_blk,
                   pl.BlockSpec((4, ATT_BLOCK, 2 * ATT_BLOCK), lambda b, h: (0, 0, 0)),
                   pl.BlockSpec((1, HEAD_DIM), const2), pl.BlockSpec((1, HEAD_DIM), const2)],
        out_shape=[jax.ShapeDtypeStruct(dpa.shape, BF16),
                   jax.ShapeDtypeStruct((4, ATT_BLOCK, 2 * ATT_BLOCK), F32),
                   jax.ShapeDtypeStruct((1, HEAD_DIM), F32), jax.ShapeDtypeStruct((1, HEAD_DIM), F32)],
        scratch_shapes=[pltpu.VMEM((dilation, ATT_BLOCK, HEAD_DIM), F32)] * 5 + [pltpu.VMEM((SEQ, HEAD_DIM), F32)],
        input_output_aliases={8: 0},
        compiler_params=_cparams(),
    )(pa, pa, pa, bias, qg, kg, do, dlse, dpa)


def _merge(o0, o1, o2, l0, l1, l2):
    mx = jnp.maximum(jnp.maximum(l0, l1), l2)
    e0, e1, e2 = jnp.exp(l0 - mx), jnp.exp(l1 - mx), jnp.exp(l2 - mx)
    den = e0 + e1 + e2
    return (e0 / den) * o0 + (e1 / den) * o1 + (e2 / den) * o2


def merge_att_out(os_, ls_, w_att_out, name):
    T = os_[0].shape[0]
    tile = pl.BlockSpec((ROW_MM_TILE, 512), lambda i: (i, 0))

    def body(o0, o1, o2, l0, l1, l2, w_ref, a_ref, y_ref):
        att = _merge(o0[...], o1[...], o2[...], l0[...], l1[...], l2[...]).astype(BF16)
        a_ref[...] = att
        y_ref[...] = _mxu(att, w_ref[...], _NN)

    return pl.pallas_call(
        body, name=name, grid=(T // ROW_MM_TILE,),
        in_specs=[tile] * 6 + [pl.BlockSpec((512, D_MODEL), lambda i: (0, 0))],
        out_specs=[tile, pl.BlockSpec((ROW_MM_TILE, D_MODEL), lambda i: (i, 0))],
        out_shape=[jax.ShapeDtypeStruct((T, 512), BF16), jax.ShapeDtypeStruct((T, D_MODEL), F32)],
        compiler_params=_cparams(),
    )(*os_, *ls_, w_att_out)


def d_att_merge_bwd(d_ya, w_att_out, os_, ls_, name):
    T = d_ya.shape[0]
    tile = lambda a: (a, (ROW_MM_TILE, 512), lambda i: (i, 0))

    def epi(acc, i, ex, out):
        _, vjp = jax.vjp(_merge, *[e[...] for e in ex])
        for o_ref, g in zip(out, vjp(acc)):
            o_ref[...] = g

    return row_matmul(d_ya, w_att_out, mode="nt", name=name, extras=[tile(a) for a in list(os_) + list(ls_)],
                      outs=[((T, 512), F32, (ROW_MM_TILE, 512), lambda i: (i, 0))] * 6, epi=epi)


def _gate_mix(ga, gm, ya, ym):
    return jax.nn.sigmoid(ga) * ya + jax.nn.sigmoid(gm) * ym


def _gate_halves(pg):
    return [(pg, (ROW_MM_TILE, D_MODEL), lambda i: (i, 0)), (pg, (ROW_MM_TILE, D_MODEL), lambda i: (i, 1))]


def ml_out_gate(hg, w_ml_out, pg, ya, name):
    T = hg.shape[0]

    def epi(acc, i, ex, out):
        ga, gm, ya_ref = ex
        out[0][...] = acc
        out[1][...] = _gate_mix(ga[...], gm[...], ya_ref[...], acc).astype(BF16)

    return row_matmul(hg, w_ml_out, mode="nn", name=name, extras=_gate_halves(pg) + [_rows(ya)],
                      outs=[_rows_out(T, F32), _rows_out(T, BF16)], epi=epi)


def d_z_gate_bwd(dy, w_out, pg, ya, ym, name):
    T = dy.shape[0]

    def epi(acc, i, ex, out):
        ga, gm, ya_ref, ym_ref = ex
        dpg_ref, dya_ref, dym_ref = out
        _, vjp = jax.vjp(_gate_mix, ga[...], gm[...], ya_ref[...], ym_ref[...])
        dga, dgm, dya, dym = vjp(acc)
        dpg_ref[:, :D_MODEL] = dga.astype(BF16)
        dpg_ref[:, D_MODEL:] = dgm.astype(BF16)
        dya_ref[...] = dya.astype(BF16)
        dym_ref[...] = dym.astype(BF16)

    return row_matmul(dy, w_out, mode="nt", name=name, extras=_gate_halves(pg) + [_rows(ya), _rows(ym)],
                      outs=[((T, GATE_COLS), BF16, (ROW_MM_TILE, GATE_COLS), lambda i: (i, 0)),
                            _rows_out(T, BF16), _rows_out(T, BF16)], epi=epi)


def _log_sigmoid(x):
    return jnp.minimum(x, 0.0) - jnp.log(1.0 + jnp.exp(-jnp.abs(x)))


def _head_mask(e):
    lane = lax.broadcasted_iota(jnp.int32, (1, 128), 1)
    return jnp.logical_and(lane >= e * 64, lane < (e + 1) * 64).astype(F32)


def _bmxu(a, b, ca, cb):
    return lax.dot_general(a.astype(BF16), b.astype(BF16), (((ca,), (cb,)), ((0,), (0,))), preferred_element_type=F32)


@jax.custom_vjp
def cdot_nt(a, b):
    return _bmxu(a, b, 2, 2)


cdot_nt.defvjp(lambda a, b: (_bmxu(a, b, 2, 2), (a, b)),
               lambda res, g: (_bmxu(g, res[1], 2, 1), _bmxu(g, res[0], 1, 1)))


@jax.custom_vjp
def cdot_nn(a, b):
    return _bmxu(a, b, 2, 1)


cdot_nn.defvjp(lambda a, b: (_bmxu(a, b, 2, 1), (a, b)),
               lambda res, g: (_bmxu(g, res[1], 2, 2), _bmxu(res[0], g, 1, 1)))


@jax.custom_vjp
def cdot_tn(a, b):
    return _bmxu(a, b, 1, 1)


cdot_tn.defvjp(lambda a, b: (_bmxu(a, b, 1, 1), (a, b)),
               lambda res, g: (_bmxu(res[1], g, 2, 2), _bmxu(res[0], g, 2, 1)))


def _top_bits(x):
    return lax.bitcast_convert_type(lax.bitcast_convert_type(x, jnp.uint32) & jnp.uint32(0xFFFF0000), F32)


def _split3(x):
    hi = _top_bits(x)
    r = x - hi
    mid = _top_bits(r)
    return hi, mid, r - mid


def _parts_in_lanes(col):
    hi, mid, lo = _split3(col)
    lane = lax.broadcasted_iota(jnp.int32, (1, 1, 8), 2)
    return jnp.where(lane == 0, hi, jnp.where(lane == 1, mid, jnp.where(lane == 2, lo, 0.0)))


def _parts_in_rows(row):
    hi, mid, lo = _split3(row)
    sub = lax.broadcasted_iota(jnp.int32, (1, 8, 1), 1)
    return jnp.where(sub == 0, hi, jnp.where(sub == 1, mid, jnp.where(sub == 2, lo, 0.0)))


def _chunk_matrix(kind, c):
    ri = lax.broadcasted_iota(jnp.int32, (c, ML_CHUNK, ML_CHUNK), 1)
    ci = lax.broadcasted_iota(jnp.int32, (c, ML_CHUNK, ML_CHUNK), 2)
    return {"eye": ri == ci, "lower": ri >= ci, "upper": ri <= ci}[kind].astype(F32)


def _col_col(kind, col):
    out = _bmxu(_chunk_matrix(kind, col.shape[0]), _parts_in_lanes(col), 2, 1)
    return jnp.sum(out, axis=-1, keepdims=True)


def _col_row(col):
    out = _bmxu(_parts_in_lanes(col), _chunk_matrix("eye", col.shape[0]), 1, 1)
    return jnp.sum(out, axis=1, keepdims=True)


def _row_col(row):
    out = _bmxu(_chunk_matrix("eye", row.shape[0]), _parts_in_rows(row), 2, 2)
    return jnp.sum(out, axis=-1, keepdims=True)


@jax.custom_vjp
def chunk_cumsum(col):
    return _col_col("lower", col)


chunk_cumsum.defvjp(lambda col: (_col_col("lower", col), None), lambda _, g: (_col_col("upper", g),))


@jax.custom_vjp
def col_to_row(col):
    return _col_row(col)


col_to_row.defvjp(lambda col: (_col_row(col), None), lambda _, g: (_row_col(g),))


def _gate_block(ifb):
    lane = lax.broadcasted_iota(jnp.int32, (1, 128), 1)
    return jnp.where(lane >= 2, _log_sigmoid(ifb), ifb)


def _ml_intra(q2, k2, v, ifb, *, e):
    c, L = q2.shape[0] // ML_CHUNK, ML_CHUNK
    hm = _head_mask(e)
    q3 = (q2 * hm).reshape(c, L, 128)
    k3 = (k2 * hm).reshape(c, L, 128)
    v3 = v.reshape(c, L, 128)
    if3 = ifb.reshape(c, L, 128)
    lanes = lax.broadcasted_iota(jnp.int32, (c, L, 128), 2)
    li = jnp.sum(jnp.where(lanes == e, if3, 0.0), axis=-1, keepdims=True)
    lf = jnp.sum(jnp.where(lanes == 2 + e, if3, 0.0), axis=-1, keepdims=True)
    b = chunk_cumsum(lf)
    last = lax.broadcasted_iota(jnp.int32, (1, L, 1), 1) == L - 1
    b_end = jnp.sum(jnp.where(last, b, 0.0), axis=1, keepdims=True)
    causal = lax.broadcasted_iota(jnp.int32, (L, L), 0) >= lax.broadcasted_iota(jnp.int32, (L, L), 1)
    Dm = jnp.where(causal, b + col_to_row(li - b), -jnp.inf)
    mD = lax.stop_gradient(jnp.max(Dm, axis=-1, keepdims=True))
    P0 = cdot_nt(q3, k3) * jnp.exp(Dm - mD)
    H0 = cdot_nn(P0, v3)
    r0 = jnp.sum(P0, axis=-1, keepdims=True)
    g = b_end - b + li
    mg = lax.stop_gradient(jnp.max(g, axis=1, keepdims=True))
    kw = jnp.exp(g - mg) * k3
    return H0, r0, cdot_tn(kw, v3), jnp.sum(kw, axis=1, keepdims=True), b, b_end, mD, mg


def _ml_inter(q2, mo, gn, H0, r0, b, C_in, n_in, *, mD, m_in, e):
    c, L = q2.shape[0] // ML_CHUNK, ML_CHUNK
    q3 = (q2 * _head_mask(e)).reshape(c, L, 128)
    a = b + m_in
    m_t = lax.stop_gradient(jnp.maximum(a, mD))
    c1 = jnp.exp(mD - m_t)
    c2 = jnp.exp(a - m_t)
    num = c1 * H0 + c2 * cdot_nn(q3, C_in)
    nq = c1 * r0 + c2 * jnp.sum(q3 * n_in, axis=-1, keepdims=True)
    h = num / jnp.maximum(jnp.abs(nq), jnp.exp(-m_t))
    hg = _rms(h) * gn * jax.nn.sigmoid(mo.reshape(c, L, 128))
    return hg.reshape(c * L, 128)


def _state_scalars(be, mg):
    c = be.shape[0]

    def shift(x, k, fill):
        return jnp.concatenate([jnp.full((k, 1, 1), fill, F32), x[:c - k]], axis=0)

    run = be
    k = 1
    while k < c:
        run = run + shift(run, k, 0.0)
        k *= 2
    top = mg - run
    k = 1
    while k < c:
        top = jnp.maximum(top, shift(top, k, -jnp.inf))
        k *= 2
    m_in = shift(run, 1, 0.0) + jnp.maximum(shift(top, 1, -jnp.inf), 0.0)
    m_next = jnp.maximum(be + m_in, mg)
    return m_in, jnp.exp(be + m_in - m_next), jnp.exp(mg - m_next)


def _state_sweep(U_s, un_s, be_s, mg_s, Cin_s, nin_s, min_s, al_s, bt_s):
    min_s[...], al_s[...], bt_s[...] = _state_scalars(be_s[...], mg_s[...])

    def step(j, carry):
        C, n = carry
        Cin_s[j], nin_s[j] = C, n
        return al_s[j] * C + bt_s[j] * U_s[j], al_s[j] * n + bt_s[j] * un_s[j]

    carry = (jnp.zeros((128, 128), F32), jnp.zeros((1, 128), F32))
    for j in range(N_CHUNKS):
        carry = step(j, carry)


def _state_sweep_bwd(U_s, un_s, dbe_s, Cin_s, nin_s, dCp_s, dnp_s, al_s, bt_s):
    def step(t, carry):
        j = N_CHUNKS - 1 - t
        dC, dn = carry
        al, bt = al_s[j], bt_s[j]
        U_s[j] = bt * dC
        un_s[j] = bt * dn
        dal = jnp.sum(jnp.sum(dC * Cin_s[j], axis=1, keepdims=True), axis=0, keepdims=True) \
            + jnp.sum(dn * nin_s[j], axis=1, keepdims=True)
        dbe_s[j] = dal * al
        return dCp_s[j] + al * dC, dnp_s[j] + al * dn

    carry = (jnp.zeros((128, 128), F32), jnp.zeros((1, 128), F32))
    for t in range(N_CHUNKS):
        carry = step(t, carry)


def _state_scratch():
    c = N_CHUNKS
    return [pltpu.VMEM((c, 128, 128), F32), pltpu.VMEM((c, 1, 128), F32), pltpu.VMEM((c, 1, 1), F32),
            pltpu.VMEM((c, 1, 1), F32),
            pltpu.VMEM((c, 128, 128), F32), pltpu.VMEM((c, 1, 128), F32), pltpu.VMEM((c, 1, 1), F32),
            pltpu.VMEM((c, 1, 1), F32), pltpu.VMEM((c, 1, 1), F32)]


def _shift_down(x, s):
    if s == 0:
        return x
    rows = lax.broadcasted_iota(jnp.int32, x.shape, 0)
    return jnp.where(rows >= s, pltpu.roll(x, s, 0), 0.0)


def _shift_up(x, s):
    if s == 0:
        return x
    S = x.shape[0]
    rows = lax.broadcasted_iota(jnp.int32, x.shape, 0)
    return jnp.where(rows < S - s, pltpu.roll(x, S - s, 0), 0.0)


def _conv_pre(x, cw, cb):
    y = cb + cw[3:4, :] * x
    for j in range(3):
        y = y + cw[j:j + 1, :] * _shift_down(x, 3 - j)
    return y


def _conv_bwd(x, cw, dpre):
    dx = cw[3:4, :] * dpre
    dcw = [None] * 4
    dcw[3] = jnp.sum(dpre * x, axis=0, keepdims=True)
    for j in range(3):
        dx = dx + cw[j:j + 1, :] * _shift_up(dpre, 3 - j)
        dcw[j] = jnp.sum(dpre * _shift_down(x, 3 - j), axis=0, keepdims=True)
    return dx, dcw, jnp.sum(dpre, axis=0, keepdims=True)


def _silu(z):
    return z * jax.nn.sigmoid(z)


def _dsilu(z):
    s = jax.nn.sigmoid(z)
    return s * (1.0 + z * (1.0 - s))


_ML_Q, _ML_K = slice(0, 128), slice(128, 256)
_ML_IF = slice(768, 896)


def _ml_v(e):
    return slice(256 + e * 128, 384 + e * 128)


def _ml_o(e):
    return slice(512 + e * 128, 640 + e * 128)


def _ml_specs():
    pair = lambda b, p: (b, 0, p)
    return [pl.BlockSpec((1, SEQ, ML_PAIR_COLS), pair),
            pl.BlockSpec((1, 4, 128), lambda b, p: (p, 0, 0)),
            pl.BlockSpec((1, 4, 128), lambda b, p: (4 + p, 0, 0)),
            pl.BlockSpec((1, 1, 128), lambda b, p: (p, 0, 0)),
            pl.BlockSpec((1, 1, 128), lambda b, p: (4 + p, 0, 0)),
            pl.BlockSpec((1, 1, 128), lambda b, p: (p, 0, 0)),
            pl.BlockSpec((1, 1, 256), lambda b, p: (p, 0, 0))]


def mlstm_fwd(pm, cw8, cb8, bifp, gn4, name):
    B = pm.shape[0]

    def body(p_ref, cwq, cwk, cbq, cbk, bif_ref, gn_ref, hg_ref, *st):
        U_s, un_s, be_s, mg_s, Cin_s, nin_s, min_s, al_s, bt_s = st
        qc = _silu(_conv_pre(p_ref[0, :, _ML_Q], cwq[0], cbq[0]))
        kc = _silu(_conv_pre(p_ref[0, :, _ML_K], cwk[0], cbk[0])) * (64 ** -0.5)
        ifb = _gate_block(p_ref[0, :, _ML_IF] + bif_ref[0])
        for e in range(2):
            lanes = slice(e * 128, (e + 1) * 128)
            H0, r0, U, un, b, b_end, mD, mg = _ml_intra(qc, kc, p_ref[0, :, _ml_v(e)], ifb, e=e)
            U_s[...], un_s[...], be_s[...], mg_s[...] = U, un, b_end, mg
            _state_sweep(*st)
            hg = _ml_inter(qc, p_ref[0, :, _ml_o(e)], gn_ref[0, :, lanes], H0, r0, b, Cin_s[...], nin_s[...],
                           mD=mD, m_in=min_s[...], e=e)
            hg_ref[0, :, lanes] = hg.astype(BF16)

    return pl.pallas_call(
        body, name=name, grid=(B, ML_PAIRS),
        in_specs=_ml_specs(),
        out_specs=pl.BlockSpec((1, SEQ, 256), lambda b, p: (b, 0, p)),
        out_shape=jax.ShapeDtypeStruct((B, SEQ, D_MODEL), BF16),
        scratch_shapes=_state_scratch(),
        compiler_params=_cparams(),
    )(pm, cw8, cw8, cb8, cb8, bifp, gn4)


def mlstm_bwd(pm, cw8, cb8, bifp, gn4, dhg, name):
    B = pm.shape[0]

    def body(p_ref, cwq, cwk, cbq, cbk, bif_ref, gn_ref, dh_ref,
             dp_ref, dcw_ref, dcb_ref, dbif_ref, dgn_ref, *scr):
        st = scr[:9]
        U_s, un_s, be_s, mg_s, Cin_s, nin_s, min_s, al_s, bt_s = st
        dCp_s, dnp_s, dbe_s = scr[9:]
        p_id = pl.program_id(1)

        @pl.when(jnp.logical_and(pl.program_id(0) == 0, p_id == 0))
        def _():
            dcw_ref[...] = jnp.zeros_like(dcw_ref)
            dcb_ref[...] = jnp.zeros_like(dcb_ref)
            dbif_ref[...] = jnp.zeros_like(dbif_ref)
            dgn_ref[...] = jnp.zeros_like(dgn_ref)

        pre_q = _conv_pre(p_ref[0, :, _ML_Q], cwq[0], cbq[0])
        pre_k = _conv_pre(p_ref[0, :, _ML_K], cwk[0], cbk[0])
        qc = _silu(pre_q)
        kc = _silu(pre_k) * (64 ** -0.5)
        ifb, gate_vjp = jax.vjp(_gate_block, p_ref[0, :, _ML_IF] + bif_ref[0])
        dq = jnp.zeros((SEQ, 128), F32)
        dk = jnp.zeros((SEQ, 128), F32)
        difb = jnp.zeros((SEQ, 128), F32)
        for e in range(2):
            lanes = slice(e * 128, (e + 1) * 128)
            (H0, r0, U, un, b, b_end, mD, mg), vjp1 = jax.vjp(functools.partial(_ml_intra, e=e), qc, kc,
                                                              p_ref[0, :, _ml_v(e)], ifb)
            U_s[...], un_s[...], be_s[...], mg_s[...] = U, un, b_end, mg
            _state_sweep(*st)
            _, vjp3 = jax.vjp(functools.partial(_ml_inter, mD=mD, m_in=min_s[...], e=e), qc, p_ref[0, :, _ml_o(e)],
                              gn_ref[0, :, lanes], H0, r0, b, Cin_s[...], nin_s[...])
            dq_a, dmo, dgn, dH0, dr0, db_a, dCp, dnp = vjp3(dh_ref[0, :, lanes])
            dCp_s[...], dnp_s[...] = dCp, dnp
            _state_sweep_bwd(U_s, un_s, dbe_s, Cin_s, nin_s, dCp_s, dnp_s, al_s, bt_s)
            dq_b, dk_b, dv, difb_e = vjp1((dH0, dr0, U_s[...], un_s[...], db_a, dbe_s[...],
                                           jnp.zeros_like(mD), jnp.zeros_like(mg)))
            dq, dk, difb = dq + dq_a + dq_b, dk + dk_b, difb + difb_e
            dp_ref[0, :, _ml_v(e)] = dv.astype(BF16)
            dp_ref[0, :, _ml_o(e)] = dmo.astype(BF16)
            dgn_ref[p_id, :, lanes] += dgn
        (difb,) = gate_vjp(difb)
        dp_ref[0, :, _ML_IF] = difb.astype(BF16)
        dbif_ref[p_id] += jnp.sum(difb, axis=0, keepdims=True)

        for (sl, cw, pre, d, blk, scale) in ((_ML_Q, cwq, pre_q, dq, p_id, 1.0), (_ML_K, cwk, pre_k, dk, 4 + p_id, 64 ** -0.5)):
            xr = p_ref[0, :, sl]
            dpre = d * scale * _dsilu(pre)
            dx, dcw, dcb = _conv_bwd(xr, cw[0], dpre)
            dp_ref[0, :, sl] = dx.astype(BF16)
            for j in range(4):
                dcw_ref[blk, j:j + 1, :] += dcw[j]
            dcb_ref[blk] += dcb

    full3 = lambda b, p: (0, 0, 0)
    return pl.pallas_call(
        body, name=name, grid=(B, ML_PAIRS),
        in_specs=[pl.BlockSpec((1, SEQ, ML_PAIR_COLS), lambda b, p: (b, 0, p), pipeline_mode=pl.Buffered(1))]
        + _ml_specs()[1:] + [pl.BlockSpec((1, SEQ, 256), lambda b, p: (b, 0, p), pipeline_mode=pl.Buffered(1))],
        out_specs=[pl.BlockSpec((1, SEQ, ML_PAIR_COLS), lambda b, p: (b, 0, p)),
                   pl.BlockSpec((8, 4, 128), full3), pl.BlockSpec((8, 1, 128), full3),
                   pl.BlockSpec((4, 1, 128), full3), pl.BlockSpec((4, 1, 256), full3)],
        out_shape=[jax.ShapeDtypeStruct((B, SEQ, ML_COLS), BF16),
                   jax.ShapeDtypeStruct((8, 4, 128), F32), jax.ShapeDtypeStruct((8, 1, 128), F32),
                   jax.ShapeDtypeStruct((4, 1, 128), F32), jax.ShapeDtypeStruct((4, 1, 256), F32)],
        scratch_shapes=_state_scratch() + [pltpu.VMEM((N_CHUNKS, 128, 128), F32), pltpu.VMEM((N_CHUNKS, 1, 128), F32),
                                           pltpu.VMEM((N_CHUNKS, 1, 1), F32)],
        compiler_params=_cparams(),
    )(pm, cw8, cw8, cb8, cb8, bifp, gn4, dhg)


def _adamw(w, g, m, v):
    m = ADAM_B1 * m + (1.0 - ADAM_B1) * g
    v = ADAM_B2 * v + (1.0 - ADAM_B2) * (g * g)
    m_hat = m / (1.0 - ADAM_B1 ** ADAM_STEP)
    v_hat = v / (1.0 - ADAM_B2 ** ADAM_STEP)
    delta = -ADAM_LR * (m_hat / (jnp.sqrt(v_hat) + ADAM_EPS) + ADAM_WD * w)
    return delta, m, v


def adamw(w, g, m, v, name, parts=False):
    R, C = w.shape
    if R % 8 == 0 or R * C * 4 <= (1 << 20):
        tr = _pick(R, (256, 128, 64, 32, 16, 8)) if R * C * 4 > (1 << 20) else R
        steps = R // tr
        spec = pl.BlockSpec((tr, C), lambda i: (i, 0))
        g_spec = pl.BlockSpec((N_DEV, tr, C), lambda i: (0, i, 0)) if parts else spec
    else:
        tc = _pick(C, (256, 128))
        steps = C // tc
        spec = pl.BlockSpec((R, tc), lambda i: (0, i))
        g_spec = pl.BlockSpec((N_DEV, R, tc), lambda i: (0, 0, i)) if parts else spec

    def body(w_ref, g_ref, m_ref, v_ref, go_ref, d_ref, mo_ref, vo_ref):
        if parts:
            g = g_ref[0].astype(F32)
            for k in range(1, N_DEV):
                g = g + g_ref[k].astype(F32)
        else:
            g = g_ref[...]
        d, mn, vn = _adamw(w_ref[...], g, m_ref[...], v_ref[...])
        go_ref[...], d_ref[...], mo_ref[...], vo_ref[...] = g, d, mn, vn

    return pl.pallas_call(
        body, name=name, grid=(steps,),
        in_specs=[spec, g_spec, spec, spec], out_specs=[spec] * 4,
        out_shape=[jax.ShapeDtypeStruct((R, C), F32)] * 4,
        compiler_params=_cparams(),
    )(w, g, m, v)


def adamw_many(ws, gs, ms, vs, name):
    n = len(ws)

    def fn(*a):
        out = []
        for j in range(n):
            out += list(_adamw(a[j], a[n + j], a[2 * n + j], a[3 * n + j]))
        return tuple(out)

    shapes = [jax.ShapeDtypeStruct(w.shape, F32) for w in ws for _ in range(3)]
    return small_call(fn, list(ws) + list(gs) + list(ms) + list(vs), shapes, name)


def _mesh_pos():
    return lax.axis_index("x"), lax.axis_index("y"), lax.axis_index("c")


def _flip(pos, f):
    x, y, c = pos
    return (1 - x if f & 4 else x, 1 - y if f & 2 else y, 1 - c if f & 1 else c)


def _index(pos):
    return 4 * pos[0] + 2 * pos[1] + pos[2]


def _exchange(arrs, name, scatter):
    n = len(arrs)
    scat = list(scatter) if isinstance(scatter, (list, tuple)) else [scatter] * n

    def body(*refs):
        ins, outs = refs[:n], refs[n:2 * n]
        send, recv, lsem = refs[2 * n:]
        me = _mesh_pos()
        mine = _index(me)
        copies = []
        for i in range(n):
            src = ins[i].at[mine] if scat[i] else ins[i]
            loc = pltpu.make_async_copy(src, outs[i].at[mine], lsem.at[i])
            loc.start()
            copies.append(loc)
            for f in range(1, N_DEV):
                peer = _flip(me, f)
                src = ins[i].at[_index(peer)] if scat[i] else ins[i]
                cp = pltpu.make_async_remote_copy(
                    src_ref=src, dst_ref=outs[i].at[mine],
                    send_sem=send.at[i * 7 + f - 1], recv_sem=recv.at[i * 7 + f - 1],
                    device_id=peer, device_id_type=pl.DeviceIdType.MESH)
                cp.start()
                copies.append(cp)
        for cp in copies:
            cp.wait()

    any_spec = pl.BlockSpec(memory_space=pl.ANY)
    out_shape = [jax.ShapeDtypeStruct(a.shape if s else (N_DEV,) + a.shape, a.dtype) for a, s in zip(arrs, scat)]
    res = pl.pallas_call(
        body, name=name,
        in_specs=[any_spec] * n, out_specs=[any_spec] * n, out_shape=out_shape,
        scratch_shapes=[pltpu.SemaphoreType.DMA((7 * n,)), pltpu.SemaphoreType.DMA((7 * n,)),
                        pltpu.SemaphoreType.DMA((n,))],
        compiler_params=_cparams(),
    )(*arrs)
    return list(res)


def all_gather(arrs, name):
    return _exchange(arrs, name, False)


def all_gather_two_level(arrs, name):
    n = len(arrs)

    def body(*refs):
        ins, outs = refs[:n], refs[n:2 * n]
        send, recv, lsem = refs[2 * n:]
        x, y, c = _mesh_pos()
        me, sibling = (x, y, c), (x, y, 1 - c)
        chips = [(1 - x, y), (x, 1 - y), (1 - x, 1 - y)]

        def copy(i, k, block, to, src=None):
            rows = outs[i].at[_index(block)]
            return pltpu.make_async_remote_copy(
                src_ref=rows if src is None else src, dst_ref=rows,
                send_sem=send.at[i * 7 + k], recv_sem=recv.at[i * 7 + k],
                device_id=to, device_id_type=pl.DeviceIdType.MESH)

        local = [pltpu.make_async_copy(ins[i], outs[i].at[_index(me)], lsem.at[i]) for i in range(n)]
        first = [copy(i, 0, me, sibling, src=ins[i]) for i in range(n)]
        first += [copy(i, 1 + j, me, (*chip, c), src=ins[i]) for i in range(n) for j, chip in enumerate(chips)]
        for cp in local + first:
            cp.start()
        passed = []
        for j, chip in enumerate(chips):
            for i in range(n):
                copy(i, 1 + j, (*chip, c), me).wait_recv()
                cp = copy(i, 4 + j, (*chip, c), sibling)
                cp.start()
                passed.append(cp)
        for i in range(n):
            copy(i, 0, sibling, me).wait_recv()
            for j, chip in enumerate(chips):
                copy(i, 4 + j, (*chip, 1 - c), me).wait_recv()
        for cp in first + passed:
            cp.wait_send()
        for cp in local:
            cp.wait()

    any_spec = pl.BlockSpec(memory_space=pl.ANY)
    res = pl.pallas_call(
        body, name=name,
        in_specs=[any_spec] * n, out_specs=[any_spec] * n,
        out_shape=[jax.ShapeDtypeStruct((N_DEV,) + a.shape, a.dtype) for a in arrs],
        scratch_shapes=[pltpu.SemaphoreType.DMA((7 * n,)), pltpu.SemaphoreType.DMA((7 * n,)),
                        pltpu.SemaphoreType.DMA((n,))],
        compiler_params=_cparams(),
    )(*arrs)
    return list(res)


def all_to_all(arrs, name):
    return _exchange(arrs, name, True)


_HBM = pl.BlockSpec(memory_space=pltpu.HBM)
_SEM = pl.BlockSpec(memory_space=pltpu.SEMAPHORE)
_EFFECT = pltpu.SideEffectType.DATAFLOW_SIDE_EFFECTING


def _split_copies(ins, lands, send, recv, scatter, waiting):
    me = _mesh_pos()
    mine = _index(me)
    copies = []
    for i in range(len(ins)):
        for f in range(1, N_DEV):
            peer = _flip(me, f)
            src = ins[i].at[_index(peer)] if scatter else ins[i]
            copies.append(pltpu.make_async_remote_copy(
                src_ref=src, dst_ref=lands[i].at[_index(peer) if waiting else mine],
                send_sem=send.at[i * 7 + f - 1], recv_sem=recv.at[i * 7 + f - 1],
                device_id=peer, device_id_type=pl.DeviceIdType.MESH))
    return copies


def exchange_start(arrs, name, scatter, after=()):
    n = len(arrs)
    land_shapes = [a.shape if scatter else (N_DEV,) + a.shape for a in arrs]

    def body(*refs):
        ins, lands = refs[:n], refs[n:2 * n]
        send, recv = refs[2 * n + len(after)], refs[2 * n + len(after) + 1]
        token = refs[-1]
        for cp in _split_copies(ins, lands, send, recv, scatter, False):
            cp.start()
        token[...] = jnp.zeros_like(token)

    res = pl.pallas_call(
        body, name=name,
        out_shape=(pltpu.SemaphoreType.DMA((7 * n,)), pltpu.SemaphoreType.DMA((7 * n,)),
                   *[pltpu.HBM(a.shape, a.dtype) for a in arrs],
                   *[pltpu.HBM(s, a.dtype) for s, a in zip(land_shapes, arrs)],
                   jax.ShapeDtypeStruct((8, 128), F32)),
        in_specs=[_HBM] * (2 * n) + [pl.BlockSpec(memory_space=pl.ANY)] * len(after),
        out_specs=(_SEM, _SEM, *[_HBM] * (2 * n), pl.BlockSpec(memory_space=pltpu.VMEM)),
        input_output_aliases={i: 2 + i for i in range(2 * n)},
        compiler_params=pltpu.CompilerParams(has_side_effects=_EFFECT),
    )(*[pltpu.with_memory_space_constraint(a, pltpu.HBM) for a in arrs],
      *[pltpu.with_memory_space_constraint(lax.empty(s, a.dtype), pltpu.HBM) for s, a in zip(land_shapes, arrs)],
      *after)
    return (res[0], res[1], list(res[2:2 + n]), list(res[2 + n:2 + 2 * n])), res[-1]


def exchange_wait(handle, after, name, scatter):
    send, recv, srcs, lands = handle
    n = len(srcs)
    after = list(after) if isinstance(after, (list, tuple)) else [after]

    def body(*refs):
        ins, lnd = refs[:n], refs[n:2 * n]
        send_, recv_ = refs[2 * n], refs[2 * n + 1]
        for cp in _split_copies(ins, lnd, send_, recv_, scatter, True):
            cp.wait_send()
            cp.wait_recv()

    res = pl.pallas_call(
        body, name=name,
        out_shape=(*[pltpu.HBM(a.shape, a.dtype) for a in srcs], *[pltpu.HBM(a.shape, a.dtype) for a in lands]),
        in_specs=[_HBM] * (2 * n) + [_SEM, _SEM] + [pl.BlockSpec(memory_space=pl.ANY)] * len(after),
        out_specs=tuple([_HBM] * (2 * n)),
        input_output_aliases={i: i for i in range(2 * n)},
        compiler_params=pltpu.CompilerParams(has_side_effects=_EFFECT),
    )(*srcs, *lands, send, recv, *after)
    return list(res[n:])


def _own_slot(land, own):
    return lax.dynamic_update_slice(land, own[None], (_index(_mesh_pos()),) + (0,) * own.ndim)


def cast_bf16(arrs, name):
    outs = []
    for i, a in enumerate(arrs):
        R, C = a.shape
        if R % 8 == 0:
            tr = _pick(R, (256, 128, 64, 32, 16, 8)) if R * C * 4 > (1 << 21) else R
            steps, spec = R // tr, pl.BlockSpec((tr, C), lambda i: (i, 0))
        else:
            steps, spec = C // 256, pl.BlockSpec((R, 256), lambda i: (0, i))

        def body(a_ref, o_ref):
            o_ref[...] = a_ref[...].astype(BF16)

        outs.append(pl.pallas_call(body, name=f"{name}_{i}", grid=(steps,), in_specs=[spec], out_specs=spec,
                                   out_shape=jax.ShapeDtypeStruct((R, C), BF16), compiler_params=_cparams())(a))
    return outs


def sum_slabs(parts, name):
    def fn(*a):
        outs = []
        for p in a:
            s = p[0].astype(F32)
            for k in range(1, N_DEV):
                s = s + p[k].astype(F32)
            outs.append(s)
        return tuple(outs)
    return list(small_call(fn, list(parts), [jax.ShapeDtypeStruct(p.shape[1:], F32) for p in parts], name))


def sum_parts(parts, name):
    def fn(p):
        g = p[0]
        for k in range(1, N_DEV):
            g = g + p[k]
        return (g,)
    return small_call(fn, [parts], [jax.ShapeDtypeStruct(parts.shape[1:], F32)], name)[0]


_SPLITS = np.cumsum([1536, 1536, 1536, 512, 512, 1024, 1024, 8, 8, 2048])[:-1].tolist()


def split_w_in(w):
    aq, ak, av, mq, mk, mv, mo, mi, mf, gates = jnp.split(w, _SPLITS, axis=1)
    R = w.shape[0]
    w_att = jnp.stack([aq.reshape(R, 12, 128), ak.reshape(R, 12, 128), av.reshape(R, 12, 128)], axis=2)
    gif = jnp.concatenate([mi.reshape(R, 4, 2), mf.reshape(R, 4, 2), jnp.zeros((R, 4, 124), w.dtype)], axis=2)
    w_ml = jnp.concatenate([mq.reshape(R, 4, 128), mk.reshape(R, 4, 128), mv.reshape(R, 4, 256),
                            mo.reshape(R, 4, 256), gif], axis=2)
    return w_att.reshape(R, ATT_COLS), w_ml.reshape(R, ML_COLS), gates


def merge_w_in(g_att, g_ml, g_gate):
    R = g_att.shape[0]
    a = g_att.reshape(R, 12, 3, 128)
    m = g_ml.reshape(R, 4, ML_PAIR_COLS)
    gif = m[:, :, 768:772]
    return jnp.concatenate([
        a[:, :, 0].reshape(R, 1536), a[:, :, 1].reshape(R, 1536), a[:, :, 2].reshape(R, 1536),
        m[:, :, 0:128].reshape(R, 512), m[:, :, 128:256].reshape(R, 512),
        m[:, :, 256:512].reshape(R, 1024), m[:, :, 512:768].reshape(R, 1024),
        gif[:, :, 0:2].reshape(R, 8), gif[:, :, 2:4].reshape(R, 8), g_gate], axis=1)


def _blk8(v, width=128):
    r = v.shape[0]
    return v.reshape(r, 1024 // width, width).transpose(1, 0, 2)


def _unblk8(v):
    nb, r, w = v.shape
    return v.transpose(1, 0, 2).reshape(r, nb * w)


def local_step(x, target, mods, w, small, late_w=None, early_g=None, w_in_g=None):
    late_w = late_w or (lambda after: w)
    big = {}
    early_g = early_g or (lambda g: big.update(g))
    w_in_g = w_in_g or (lambda g: big.update(w_in=merge_w_in(*g)))
    B = x.shape[0]
    T = B * SEQ
    shift1, scale1, gate1, shift2, scale2, gate2 = mods
    f2 = lambda a: a.reshape(T, a.shape[-1])
    f3 = lambda a: a.reshape(B, SEQ, a.shape[-1])

    rel_t = jnp.pad(small["rel_bias"].T, ((0, 4), (0, 0)))
    onehots = [_bucket_onehot(d) for _, d in ATT_GROUPS]
    biases = [bias_expand(rel_t, oh, f"bias_expand{g}").reshape(16, ATT_BLOCK, 2 * ATT_BLOCK)
              for g, oh in enumerate(onehots)]
    qg, kg = small["q_norm_g"], small["k_norm_g"]
    cw8 = _blk8(small["conv_w"])
    cb8 = _blk8(small["conv_b"])
    b_if = small["b_if"].reshape(2, 4, 2)
    bifp = jnp.concatenate([b_if[0], b_if[1], jnp.zeros((4, 124), F32)], axis=1).reshape(4, 1, 128)
    gn4 = small["mlstm_norm_g"].reshape(4, 1, 256)

    u = modnorm_fwd(x, small["norm1_g"], scale1, shift1, "modnorm1")
    u2d = f2(u)
    pa = f3(matmul(u2d, w["w_att"], mode="nn", name="proj_att"))
    pm = f3(matmul(u2d, w["w_ml"], mode="nn", name="proj_ml"))
    pg = matmul(u2d, w["w_gate"], mode="nn", name="proj_gate")
    os_, ls_ = [], []
    one_block = [SEQ // d == ATT_BLOCK for _, d in ATT_GROUPS]
    for g in range(3):
        o, l = (attn_fwd_classes if one_block[g] else attn_fwd)(pa, biases[g], qg, kg, g, f"attn_fwd{g}")
        os_.append(f2(o))
        ls_.append(f2(l))
    hg = mlstm_fwd(pm, cw8, cb8, bifp, gn4, "mlstm_fwd")
    w = {**w, **late_w(hg)}
    att, y_att = merge_att_out(os_, ls_, w["w_att_out"], "att_out")
    y_ml, z = ml_out_gate(f2(hg), w["w_ml_out"], pg, y_att, "ml_out")
    y, x1, u2 = out_proj_resid_modnorm(f2(z), w["w_out"], f2(x), gate1, small["norm2_g"], scale2, shift2, "out_proj")
    pre, hdn = matmul(u2, w["w_ff1"], mode="nn", name="ff1", out_dtypes=(BF16, BF16),
                      epi=lambda acc: (acc, jnp.square(jnp.maximum(acc, 0.0))))
    dx2, d_ffo, loss, d_gate2 = ff2_loss(hdn, w["w_ff2"], x1, gate2, f2(target), "ff2_loss")

    g_ff2 = matmul(hdn, d_ffo, mode="tn", name="g_ff2", out_dtypes=(BF16,))
    d_pre = matmul(d_ffo, w["w_ff2"], mode="nt", name="d_hdn", out_dtypes=(BF16,), extras=(pre,),
                   epi=lambda acc, p: (acc * (2.0 * jnp.maximum(p.astype(F32), 0.0)),))
    g_ff1 = matmul(u2, d_pre, mode="tn", name="g_ff1", out_dtypes=(BF16,))
    dx1, d_norm2, d_scale2, d_shift2, dy, d_gate1 = d_u_modnorm_bwd(
        d_pre, w["w_ff1"], None, x1, small["norm2_g"], scale2, shift2, dx2, y, gate1, "d_u2")
    g_out = matmul(f2(z), dy, mode="tn", name="g_out", out_dtypes=(BF16,))
    dpg, d_ya, d_ym = d_z_gate_bwd(dy, w["w_out"], pg, y_att, y_ml, "d_z")
    g_att_out = matmul(f2(att), f2(d_ya), mode="tn", name="g_att_out", out_dtypes=(BF16,))
    dmerge = d_att_merge_bwd(d_ya, w["w_att_out"], os_, ls_, "d_att")
    g_ml_out = matmul(f2(hg), f2(d_ym), mode="tn", name="g_ml_out", out_dtypes=(BF16,))
    d_hg = matmul(f2(d_ym), w["w_ml_out"], mode="nt", name="d_hg")
    started = early_g(dict(w_att_out=g_att_out, w_ml_out=g_ml_out, w_out=g_out, w_ff1=g_ff1, w_ff2=g_ff2))
    order = 0.0 if started is None else started[0, 0]
    dmerge = [f3(d) for d in dmerge]
    dpa = lax.empty((B, SEQ, ATT_COLS), BF16)
    d_rel = []
    d_qg = d_kg = None
    for g in range(3):
        dpa, dbias, dq_g, dk_g = (attn_bwd_classes if one_block[g] else attn_bwd)(
            pa, biases[g], qg + order, kg, dmerge[g], dmerge[3 + g], dpa, g, f"attn_bwd{g}")
        db8 = jnp.pad(dbias.reshape(4, -1), ((0, 4), (0, 0)))
        d_rel.append(bias_reduce(db8, onehots[g], f"bias_reduce{g}")[:4])
        d_qg = dq_g if d_qg is None else d_qg + dq_g
        d_kg = dk_g if d_kg is None else d_kg + dk_g
    dpm, dcw8, dcb8, dbifp, dgn4 = mlstm_bwd(pm, cw8, cb8, bifp, gn4 + order, f3(d_hg), "mlstm_bwd")
    g_w_att = matmul(u2d, f2(dpa), mode="tn", name="g_w_att", out_dtypes=(BF16,))
    g_w_ml = matmul(u2d, f2(dpm), mode="tn", name="g_w_ml", out_dtypes=(BF16,))
    g_w_gate = matmul(u2d, f2(dpg), mode="tn", name="g_w_gate", out_dtypes=(BF16,))
    started = w_in_g((g_w_att, g_w_ml, g_w_gate))
    du = matmul(f2(dpa), w["w_att"], mode="nt", name="d_u_att", after=() if started is None else (started,))
    du = matmul(f2(dpm), w["w_ml"], mode="nt", name="d_u_ml", extras=(du,), epi=lambda acc, e: (acc + e,))
    grad_x, d_norm1, d_scale1, d_shift1 = d_u_modnorm_bwd(
        f2(dpg), w["w_gate"], du, f2(x), small["norm1_g"], scale1, shift1, dx1, None, None, "d_u_gate")
    grad_x = f3(grad_x)

    d_mods = (d_shift1, d_scale1, d_gate1, d_shift2, d_scale2, d_gate2)
    dbif = dbifp.reshape(4, 128)
    small_g = dict(
        norm1_g=d_norm1, norm2_g=d_norm2,
        b_if=jnp.stack([dbif[:, 0:2].reshape(8), dbif[:, 2:4].reshape(8)]),
        conv_w=_unblk8(dcw8), conv_b=_unblk8(dcb8), q_norm_g=d_qg, k_norm_g=d_kg,
        rel_bias=jnp.concatenate(d_rel, axis=0).T,
        mlstm_norm_g=dgn4.reshape(1, 1024))
    return loss, grad_x, d_mods, big, small_g


_SMALL = (("b_ada", 6144), ("norm1_g", 1024), ("norm2_g", 1024), ("b_if", 16), ("conv_b", 1024),
          ("q_norm_g", 128), ("k_norm_g", 128), ("rel_bias", 384), ("mlstm_norm_g", 1024), ("conv_w", 4096))
_SMALL_ROWS = 120
_REPL = _SMALL[:-1]
_SMALL_SENT = _SMALL + (("loss", 1),)


def _pack(d, names, rows):
    flat = jnp.concatenate([d[k].reshape(-1) for k, _ in names])
    return jnp.pad(flat, (0, rows * 128 - flat.shape[0])).reshape(rows, 128)


def _unpack(slab, names, shapes):
    flat = slab.reshape(-1)
    out, off = {}, 0
    for k, nel in names:
        out[k] = flat[off:off + nel].reshape(shapes[k])
        off += nel
    return out


def kernel(x, c, w_ada, b_ada, norm1_g, norm2_g, w_in, b_if, conv_w, conv_b, q_norm_g, k_norm_g, rel_bias, mlstm_norm_g, w_att_out, w_ml_out, w_out, w_ff1, w_ff2, loss_target, m_w_ada, m_b_ada, m_norm1_g, m_norm2_g, m_w_in, m_b_if, m_conv_w, m_conv_b, m_q_norm_g, m_k_norm_g, m_rel_bias, m_mlstm_norm_g, m_w_att_out, m_w_ml_out, m_w_out, m_w_ff1, m_w_ff2, v_w_ada, v_b_ada, v_norm1_g, v_norm2_g, v_w_in, v_b_if, v_conv_w, v_conv_b, v_q_norm_g, v_k_norm_g, v_rel_bias, v_mlstm_norm_g, v_w_att_out, v_w_ml_out, v_w_out, v_w_ff1, v_w_ff2):
    P = dict(w_ada=w_ada, b_ada=b_ada, norm1_g=norm1_g, norm2_g=norm2_g, w_in=w_in, b_if=b_if, conv_w=conv_w,
             conv_b=conv_b, q_norm_g=q_norm_g, k_norm_g=k_norm_g, rel_bias=rel_bias, mlstm_norm_g=mlstm_norm_g,
             w_att_out=w_att_out, w_ml_out=w_ml_out, w_out=w_out, w_ff1=w_ff1, w_ff2=w_ff2)
    M = dict(w_ada=m_w_ada, b_ada=m_b_ada, norm1_g=m_norm1_g, norm2_g=m_norm2_g, w_in=m_w_in, b_if=m_b_if,
             conv_w=m_conv_w, conv_b=m_conv_b, q_norm_g=m_q_norm_g, k_norm_g=m_k_norm_g, rel_bias=m_rel_bias,
             mlstm_norm_g=m_mlstm_norm_g, w_att_out=m_w_att_out, w_ml_out=m_w_ml_out, w_out=m_w_out,
             w_ff1=m_w_ff1, w_ff2=m_w_ff2)
    V = dict(w_ada=v_w_ada, b_ada=v_b_ada, norm1_g=v_norm1_g, norm2_g=v_norm2_g, w_in=v_w_in, b_if=v_b_if,
             conv_w=v_conv_w, conv_b=v_conv_b, q_norm_g=v_q_norm_g, k_norm_g=v_k_norm_g, rel_bias=v_rel_bias,
             mlstm_norm_g=v_mlstm_norm_g, w_att_out=v_w_att_out, w_ml_out=v_w_ml_out, w_out=v_w_out,
             w_ff1=v_w_ff1, w_ff2=v_w_ff2)
    names = list(P)
    shapes = {k: P[k].shape for k in names}
    B = x.shape[0]
    me = _index(_mesh_pos())

    big_names = ("w_in", "w_att_out", "w_ml_out", "w_out", "w_ff1", "w_ff2")
    shards = cast_bf16([P[k][0] for k in big_names], "cast_w")
    (rows8,) = all_to_all([shards[0].reshape(N_DEV, D_MODEL // N_DEV, W_IN_SHARD)], "w_in_rows_exchange")
    slab = split_w_in(rows8.transpose(1, 0, 2).reshape(D_MODEL // N_DEV, D_IN))
    a8, m8, g8, c8, conv_w8 = all_gather_two_level(list(slab) + [c, conv_w[0]], "gather_w_in")
    c_all = c8.reshape(N_DEV * B, D_MODEL)
    conv_w_full = conv_w8.transpose(1, 0, 2).reshape(4, 1024)
    w = dict(w_att=a8.reshape(D_MODEL, ATT_COLS), w_ml=m8.reshape(D_MODEL, ML_COLS),
             w_gate=g8.reshape(D_MODEL, GATE_COLS))

    (silu_c,) = small_call(lambda a: (_silu(a),), [c_all], [jax.ShapeDtypeStruct(c_all.shape, F32)], "silu_c")
    b_ada_cols = lax.dynamic_slice(b_ada, (0, me * 768), (1, 768))
    ada_cols = matmul(silu_c, w_ada[0], mode="nn", name="ada", extras=(jnp.broadcast_to(b_ada_cols, (N_DEV * B, 768)),),
                      epi=lambda acc, bb: (acc + bb,))
    (ada_t,) = all_to_all([ada_cols.reshape(N_DEV, B, 768)], "ada_exchange")
    ada = ada_t.transpose(1, 0, 2).reshape(B, 6 * D_MODEL)
    mods = tuple(ada[:, i * D_MODEL:(i + 1) * D_MODEL].reshape(B, 1, D_MODEL) for i in range(6))

    late_handle, late_order = exchange_start(shards[1:], "gather_late_start", False, after=(ada_t,))

    def late_w(after):
        lands = exchange_wait(late_handle, after, "gather_late_wait", False)
        gw = dict(zip(big_names[1:], [_own_slot(l, s) for l, s in zip(lands, shards[1:])]))
        return dict(w_att_out=gw["w_att_out"].transpose(1, 0, 2).reshape(512, D_MODEL),
                    w_ml_out=gw["w_ml_out"].reshape(D_MODEL, D_MODEL), w_out=gw["w_out"].reshape(D_MODEL, D_MODEL),
                    w_ff1=gw["w_ff1"].transpose(1, 0, 2).reshape(D_MODEL, D_FF),
                    w_ff2=gw["w_ff2"].reshape(D_FF, D_MODEL))

    pending = {}

    def send_grads(key, blocks, name):
        handle, order = exchange_start(blocks, name, True)
        pending[key] = (handle, [lax.dynamic_index_in_dim(b, me, 0, keepdims=False) for b in blocks])
        return order

    def early_g(g):
        return send_grads("late", [g["w_att_out"].reshape(512, N_DEV, 128).transpose(1, 0, 2),
                                   g["w_ml_out"].reshape(N_DEV, 128, D_MODEL), g["w_out"].reshape(N_DEV, 128, D_MODEL),
                                   g["w_ff1"].reshape(D_MODEL, N_DEV, 512).transpose(1, 0, 2),
                                   g["w_ff2"].reshape(N_DEV, 512, D_MODEL)], "grad_late_start")

    def w_in_g(parts):
        return send_grads("w_in", [g.reshape(N_DEV, D_MODEL // N_DEV, g.shape[1]) for g in parts], "grad_w_in_start")

    def recv_grads(key, after, name):
        handle, own = pending[key]
        return [_own_slot(l, o) for l, o in zip(exchange_wait(handle, after, name, True), own)]

    small = dict(norm1_g=norm1_g + late_order[0, 0], norm2_g=norm2_g, b_if=b_if[0], conv_w=conv_w_full, conv_b=conv_b,
                 q_norm_g=q_norm_g, k_norm_g=k_norm_g, rel_bias=rel_bias, mlstm_norm_g=mlstm_norm_g)
    loss, grad_x, d_mods, _, small_g = local_step(x, loss_target, mods, w, small, late_w, early_g, w_in_g)

    d_ada = jnp.concatenate([d.reshape(B, D_MODEL) for d in d_mods], axis=1)
    (small_g["b_ada"],) = small_call(lambda a: (jnp.sum(a, axis=0, keepdims=True),), [d_ada],
                                     [jax.ShapeDtypeStruct((1, 6144), F32)], "g_b_ada_local")
    small_g["loss"] = loss
    slabs = sum_slabs(recv_grads("w_in", grad_x, "grad_w_in_wait"), "w_in_slab_sum")
    w_in_cols = merge_w_in(*slabs).reshape(D_MODEL // N_DEV, N_DEV, W_IN_SHARD).transpose(1, 0, 2)
    d_ada_t, small_parts = _exchange(
        [d_ada.reshape(B, N_DEV, 768).transpose(1, 0, 2), _pack(small_g, _SMALL_SENT, _SMALL_ROWS)],
        "small_exchange", [True, False])
    cols_handle, _ = exchange_start([w_in_cols], "grad_w_in_cols_start", True, after=(small_parts,))
    d_ada_cols = d_ada_t.reshape(N_DEV * B, 768)
    g_w_ada = matmul(silu_c, d_ada_cols, mode="tn", name="g_w_ada")

    recv = recv_grads("late", grad_x, "grad_late_wait")
    small_sum = sum_parts(small_parts, "small_grad_sum")
    sg = _unpack(small_sum, _SMALL_SENT, {**{k: shapes[k] for k, _ in _REPL}, "conv_w": (4, 1024), "loss": ()})
    loss = sg["loss"]

    G, Dl, NM, NV = {}, {}, {}, {}
    for k, parts in zip(big_names[1:], recv):
        g, d, nm, nv = adamw(P[k][0], parts, M[k][0], V[k][0], f"adamw_{k}", parts=True)
        G[k], Dl[k], NM[k], NV[k] = g[None], d[None], nm[None], nv[None]
    g, d, nm, nv = adamw(w_ada[0], g_w_ada, m_w_ada[0], v_w_ada[0], "adamw_w_ada")
    G["w_ada"], Dl["w_ada"], NM["w_ada"], NV["w_ada"] = g[None], d[None], nm[None], nv[None]
    g_conv = lax.dynamic_slice(sg["conv_w"], (0, me * 128), (4, 128))
    g, d, nm, nv = adamw(conv_w[0], g_conv, m_conv_w[0], v_conv_w[0], "adamw_conv_w")
    G["conv_w"], Dl["conv_w"], NM["conv_w"], NV["conv_w"] = g[None], d[None], nm[None], nv[None]
    flat2 = lambda a: a.reshape(-1, a.shape[-1])
    keys = [k for k, _ in _REPL]
    upd = adamw_many([flat2(P[k]) for k in keys], [flat2(sg[k]) for k in keys], [flat2(M[k]) for k in keys],
                     [flat2(V[k]) for k in keys], "adamw_small")
    for j, k in enumerate(keys):
        G[k] = sg[k]
        Dl[k], NM[k], NV[k] = [upd[3 * j + t].reshape(shapes[k]) for t in range(3)]

    others = [NV[k] for k in big_names[1:]] + [NV["w_ada"], NV["conv_w"], upd[-1]]
    (g_w_in,) = exchange_wait(cols_handle, others, "grad_w_in_cols_wait", True)
    g_w_in = _own_slot(g_w_in, lax.dynamic_index_in_dim(w_in_cols, me, 0, keepdims=False))
    g, d, nm, nv = adamw(w_in[0], g_w_in.reshape(D_MODEL, W_IN_SHARD), m_w_in[0], v_w_in[0], "adamw_w_in")
    G["w_in"], Dl["w_in"], NM["w_in"], NV["w_in"] = g[None], d[None], nm[None], nv[None]

    return (loss, grad_x, *[G[k] for k in names], *[Dl[k] for k in names], *[NM[k] for k in names],
            *[NV[k] for k in names])
```

```python
import functools
import math

import numpy as np
import jax
import jax.numpy as jnp
from jax import lax
from jax.experimental import pallas as pl
from jax.experimental.pallas import tpu as pltpu

F32 = jnp.float32
BF16 = jnp.bfloat16

N_DEV = 8
D_MODEL = 1024
SEQ = 2048
ATT_GROUPS = ((128, 1), (512, 4), (2048, 16))
N_ATT_HEADS = 12
ATT_BLOCK = 128
HEAD_DIM = 128
ML_HEADS = 8
ML_PAIRS = 4
ML_CHUNK = 64
N_CHUNKS = SEQ // ML_CHUNK
N_BUCKETS = 32
MAX_DISTANCE = 2048
D_FF = 4096
D_IN = 9744
EPS = 1e-6

ADAM_LR = 0.001
ADAM_B1 = 0.9
ADAM_B2 = 0.999
ADAM_EPS = 1e-08
ADAM_WD = 0.01
ADAM_STEP = 10

ATT_HEAD_COLS = 3 * HEAD_DIM
ATT_COLS = N_ATT_HEADS * ATT_HEAD_COLS
ML_PAIR_COLS = 896
ML_COLS = ML_PAIRS * ML_PAIR_COLS
GATE_COLS = 2 * D_MODEL
W_IN_SHARD = D_IN // N_DEV

VMEM_LIMIT = 60 * 1024 * 1024


def _cparams(**kw):
    return pltpu.CompilerParams(vmem_limit_bytes=VMEM_LIMIT, **kw)


_NN = ((1,), (0,))
_NT = ((1,), (1,))
_TN = ((0,), (0,))


def _mxu(a, b, dims):
    return lax.dot_general(a.astype(BF16), b.astype(BF16), (dims, ((), ())), preferred_element_type=F32)


@jax.custom_vjp
def bdot_nn(a, b):
    return _mxu(a, b, _NN)


def _nn_fwd(a, b):
    return _mxu(a, b, _NN), (a, b)


def _nn_bwd(res, g):
    a, b = res
    return _mxu(g, b, _NT), _mxu(a, g, _TN)


bdot_nn.defvjp(_nn_fwd, _nn_bwd)


@jax.custom_vjp
def bdot_nt(a, b):
    return _mxu(a, b, _NT)


def _nt_fwd(a, b):
    return _mxu(a, b, _NT), (a, b)


def _nt_bwd(res, g):
    a, b = res
    return _mxu(g, b, _NN), _mxu(g, a, _TN)


bdot_nt.defvjp(_nt_fwd, _nt_bwd)


def _doth(a, b, dims=_NN):
    return lax.dot_general(a, b, (dims, ((), ())), precision=lax.Precision.HIGHEST, preferred_element_type=F32)


def _rms(x):
    return x * lax.rsqrt(jnp.mean(x * x, axis=-1, keepdims=True) + EPS)


def _pick(n, cands):
    for t in cands:
        if n % t == 0:
            return t
    raise ValueError(f"no tile for {n}")


MM_TILE_M = (1024, 512, 256, 128, 64, 32, 16, 8)
MM_TILE_N = (2048, 1792, 1536, 1024, 768, 512, 256, 128)
MM_TILE_K = (2048, 1792, 1536, 1024, 512, 256, 128, 64, 32)

def matmul(a, b, *, mode, name, out_dtypes=(F32,), epi=None, extras=(), after=()):
    if mode == "nn":
        (M, K), (K2, N) = a.shape, b.shape
    elif mode == "nt":
        (M, K), (N, K2) = a.shape, b.shape
    else:
        (K, M), (K2, N) = a.shape, b.shape
    assert K == K2, (a.shape, b.shape, mode)
    tm = _pick(M, MM_TILE_M)
    tn = _pick(N, MM_TILE_N)
    tk = _pick(K, MM_TILE_K)
    nk = K // tk
    n_ex = len(extras)
    n_out = len(out_dtypes)
    dims = {"nn": _NN, "nt": _NT, "tn": _TN}[mode]

    def finish(r, ex_refs, out_refs):
        outs = epi(r, *[e[...] for e in ex_refs]) if epi is not None else (r,)
        for o_ref, o in zip(out_refs, outs):
            o_ref[...] = o.astype(o_ref.dtype)

    def body(*refs):
        a_ref, b_ref = refs[0], refs[1]
        ex_refs = refs[2:2 + n_ex]
        out_refs = refs[2 + n_ex + len(after):2 + n_ex + len(after) + n_out]
        if nk == 1:
            finish(_mxu(a_ref[...], b_ref[...], dims), ex_refs, out_refs)
            return
        acc = refs[2 + n_ex + len(after) + n_out]
        k = pl.program_id(2)

        @pl.when(k == 0)
        def _():
            acc[...] = jnp.zeros_like(acc)

        acc[...] += _mxu(a_ref[...], b_ref[...], dims)

        @pl.when(k == nk - 1)
        def _():
            finish(acc[...], ex_refs, out_refs)

    if mode == "nn":
        a_spec = pl.BlockSpec((tm, tk), lambda i, j, k: (i, k))
        b_spec = pl.BlockSpec((tk, tn), lambda i, j, k: (k, j))
    elif mode == "nt":
        a_spec = pl.BlockSpec((tm, tk), lambda i, j, k: (i, k))
        b_spec = pl.BlockSpec((tn, tk), lambda i, j, k: (j, k))
    else:
        a_spec = pl.BlockSpec((tk, tm), lambda i, j, k: (k, i))
        b_spec = pl.BlockSpec((tk, tn), lambda i, j, k: (k, j))
    o_spec = pl.BlockSpec((tm, tn), lambda i, j, k: (i, j))
    res = pl.pallas_call(
        body,
        name=name,
        grid=(M // tm, N // tn, nk),
        in_specs=[a_spec, b_spec] + [o_spec] * n_ex + [pl.BlockSpec(memory_space=pl.ANY)] * len(after),
        out_specs=[o_spec] * n_out,
        out_shape=[jax.ShapeDtypeStruct((M, N), dt) for dt in out_dtypes],
        scratch_shapes=[pltpu.VMEM((tm, tn), F32)] if nk > 1 else [],
        compiler_params=_cparams(),
    )(a, b, *extras, *after)
    return res[0] if n_out == 1 else tuple(res)


ROW_MM_TILE = 512


def row_matmul(a, b, *, mode, name, extras, outs, epi):
    (M, K) = a.shape
    N = b.shape[1] if mode == "nn" else b.shape[0]
    tm = ROW_MM_TILE
    tk = _pick(K, MM_TILE_K)
    nk = K // tk
    n_ex, n_out = len(extras), len(outs)

    def body(*refs):
        a_ref, b_ref = refs[0], refs[1]
        ex_refs = refs[2:2 + n_ex]
        out_refs = refs[2 + n_ex:2 + n_ex + n_out]
        i = pl.program_id(0)
        dims = _NN if mode == "nn" else _NT
        if nk == 1:
            epi(_mxu(a_ref[...], b_ref[...], dims), i, ex_refs, out_refs)
            return
        acc = refs[2 + n_ex + n_out]
        k = pl.program_id(1)

        @pl.when(k == 0)
        def _():
            acc[...] = jnp.zeros_like(acc)

        acc[...] += _mxu(a_ref[...], b_ref[...], dims)

        @pl.when(k == nk - 1)
        def _():
            epi(acc[...], i, ex_refs, out_refs)

    def lift(index_map):
        return lambda i, k: index_map(i)

    b_spec = pl.BlockSpec((tk, N), lambda i, k: (k, 0)) if mode == "nn" else pl.BlockSpec((N, tk), lambda i, k: (0, k))
    res = pl.pallas_call(
        body, name=name, grid=(M // tm, nk),
        in_specs=[pl.BlockSpec((tm, tk), lambda i, k: (i, k)), b_spec]
        + [pl.BlockSpec(blk, lift(im)) for _, blk, im in extras],
        out_specs=[pl.BlockSpec(blk, lift(im)) for _, _, blk, im in outs],
        out_shape=[jax.ShapeDtypeStruct(shape, dt) for shape, dt, _, _ in outs],
        scratch_shapes=[pltpu.VMEM((tm, N), F32)] if nk > 1 else [],
        compiler_params=_cparams(),
    )(a, b, *[e[0] for e in extras])
    return tuple(res)


def _rows(arr):
    return (arr, (ROW_MM_TILE, arr.shape[1]), lambda i: (i, 0))


def _rows_out(T, dtype):
    return ((T, D_MODEL), dtype, (ROW_MM_TILE, D_MODEL), lambda i: (i, 0))


def _per_seq(arr):
    return (arr, (1, 1, D_MODEL), lambda i: (i // (SEQ // ROW_MM_TILE), 0, 0))


def _per_seq_out(B):
    return ((B, 1, D_MODEL), F32, (1, 1, D_MODEL), lambda i: (i // (SEQ // ROW_MM_TILE), 0, 0))


def _first_tile_of_seq(i):
    return i % (SEQ // ROW_MM_TILE) == 0


def small_call(fn, inputs, out_shapes, name):
    n_in = len(inputs)

    def body(*refs):
        outs = fn(*[r[...] for r in refs[:n_in]])
        for o_ref, o in zip(refs[n_in:], outs):
            o_ref[...] = o.astype(o_ref.dtype)

    res = pl.pallas_call(body, name=name, out_shape=list(out_shapes), compiler_params=_cparams())(*inputs)
    return tuple(res)


ROW_TILE = 512


def _modnorm(x, g, scale, shift):
    return _rms(x) * g * (1.0 + scale) + shift


def _row_spec(width):
    return pl.BlockSpec((1, ROW_TILE, width), lambda b, i: (b, i, 0))


def _mod_spec():
    return pl.BlockSpec((1, 1, D_MODEL), lambda b, i: (b, 0, 0))


def _vec_spec():
    return pl.BlockSpec((1, D_MODEL), lambda b, i: (0, 0))


def modnorm_fwd(x, g, scale, shift, name):
    B, S, D = x.shape

    def body(x_ref, g_ref, sc_ref, sh_ref, u_ref):
        u_ref[0] = _modnorm(x_ref[0], g_ref[...], sc_ref[0], sh_ref[0]).astype(BF16)

    return pl.pallas_call(
        body, name=name, grid=(B, S // ROW_TILE),
        in_specs=[_row_spec(D), _vec_spec(), _mod_spec(), _mod_spec()],
        out_specs=_row_spec(D),
        out_shape=jax.ShapeDtypeStruct((B, S, D), BF16),
        compiler_params=_cparams(),
    )(x, g, scale, shift)


def _gain_spec(g):
    return (g, (1, D_MODEL), lambda i: (0, 0))


def out_proj_resid_modnorm(z, w_out, x, gate, g, scale, shift, name):
    T = z.shape[0]

    def epi(acc, i, ex, out):
        x_ref, gt_ref, g_ref, sc_ref, sh_ref = ex
        y_ref, x1_ref, u_ref = out
        y_ref[...] = acc
        x1 = x_ref[...] + gt_ref[0] * acc
        x1_ref[...] = x1
        u_ref[...] = _modnorm(x1, g_ref[...], sc_ref[0], sh_ref[0]).astype(BF16)

    return row_matmul(z, w_out, mode="nn", name=name,
                      extras=[_rows(x), _per_seq(gate), _gain_spec(g), _per_seq(scale), _per_seq(shift)],
                      outs=[_rows_out(T, F32), _rows_out(T, F32), _rows_out(T, BF16)], epi=epi)


def ff2_loss(hdn, w_ff2, x1, gate, target, name):
    T = hdn.shape[0]

    def epi(acc, i, ex, out):
        x_ref, gt_ref, t_ref = ex
        dx_ref, dffo_ref, loss_ref, dg_ref = out

        @pl.when(i == 0)
        def _():
            loss_ref[...] = jnp.zeros_like(loss_ref)

        @pl.when(_first_tile_of_seq(i))
        def _():
            dg_ref[...] = jnp.zeros_like(dg_ref)

        err = x_ref[...] + gt_ref[0] * acc - t_ref[...]
        dx = err * (1.0 / D_MODEL)
        dx_ref[...] = dx
        dffo_ref[...] = (gt_ref[0] * dx).astype(BF16)
        loss_ref[...] += 0.5 * jnp.sum(jnp.mean(err * err, axis=-1, keepdims=True), axis=0, keepdims=True)
        dg_ref[0] += jnp.sum(dx * acc, axis=0, keepdims=True)

    return row_matmul(hdn, w_ff2, mode="nn", name=name,
                      extras=[_rows(x1), _per_seq(gate), _rows(target)],
                      outs=[_rows_out(T, F32), _rows_out(T, BF16), ((1, 1), F32, (1, 1), lambda i: (0, 0)),
                            _per_seq_out(T // SEQ)], epi=epi)


def d_u_modnorm_bwd(a, w, du_prev, x, g, scale, shift, dx_res, y, gate, name):
    T = a.shape[0]
    B = T // SEQ
    n_prev, resid = int(du_prev is not None), y is not None

    def epi(acc, i, ex, out):
        x_ref, g_ref, sc_ref, sh_ref, dr_ref = ex[n_prev:n_prev + 5]
        dx_ref, dg_ref, dsc_ref, dsh_ref = out[:4]

        @pl.when(i == 0)
        def _():
            dg_ref[...] = jnp.zeros_like(dg_ref)

        @pl.when(_first_tile_of_seq(i))
        def _():
            for r in out[2:4] + out[5:]:
                r[...] = jnp.zeros_like(r)

        du = acc + ex[0][...] if n_prev else acc
        _, vjp = jax.vjp(_modnorm, x_ref[...], g_ref[...], sc_ref[0], sh_ref[0])
        dxn, dg, dsc, dsh = vjp(du)
        dx = dxn + dr_ref[...]
        dx_ref[...] = dx
        dg_ref[...] += dg
        dsc_ref[0] += dsc
        dsh_ref[0] += dsh
        if resid:
            y_ref, gt_ref = ex[n_prev + 5:]
            out[4][...] = (gt_ref[0] * dx).astype(BF16)
            out[5][0] += jnp.sum(dx * y_ref[...], axis=0, keepdims=True)

    extras = ([_rows(du_prev)] if n_prev else []) + [_rows(x), _gain_spec(g), _per_seq(scale), _per_seq(shift),
                                                     _rows(dx_res)] + ([_rows(y), _per_seq(gate)] if resid else [])
    outs = [_rows_out(T, F32), ((1, D_MODEL), F32, (1, D_MODEL), lambda i: (0, 0)), _per_seq_out(B), _per_seq_out(B)]
    outs += [_rows_out(T, BF16), _per_seq_out(B)] if resid else []
    return row_matmul(a, w, mode="nt", name=name, extras=extras, outs=outs, epi=epi)


def _bucket_table(dilation):
    i = np.arange(ATT_BLOCK)[:, None]
    j = np.arange(2 * ATT_BLOCK)[None, :]
    delta = ATT_BLOCK + i - j
    dist = np.maximum(delta, 0) * dilation
    max_exact = N_BUCKETS // 2
    d = np.maximum(dist, max_exact).astype(np.float32)
    large = max_exact + (np.log(d / np.float32(max_exact)) / np.float32(math.log(MAX_DISTANCE / max_exact))
                         * np.float32(N_BUCKETS - max_exact)).astype(np.int32)
    large = np.minimum(large, N_BUCKETS - 1)
    return np.where(dist < max_exact, dist, large).astype(np.int32)


def _bucket_onehot(dilation):
    bt = jnp.asarray(_bucket_table(dilation).reshape(1, -1))
    return (bt == jnp.arange(N_BUCKETS, dtype=jnp.int32)[:, None]).astype(F32)


def bias_expand(rel_t, onehot, name):
    def fn(r, oh):
        return (_doth(r, oh),)
    return small_call(fn, [rel_t, onehot], [jax.ShapeDtypeStruct((rel_t.shape[0], onehot.shape[1]), F32)], name)[0]


def bias_reduce(dbias_flat, onehot, name):
    def fn(db, oh):
        return (_doth(db, oh, _NT),)
    return small_call(fn, [dbias_flat, onehot], [jax.ShapeDtypeStruct((dbias_flat.shape[0], N_BUCKETS), F32)], name)[0]


def _qk_norm(x, g):
    return _rms(x) * g


def _masked_bias(bias):
    i = lax.broadcasted_iota(jnp.int32, (ATT_BLOCK, 2 * ATT_BLOCK), 0)
    j = lax.broadcasted_iota(jnp.int32, (ATT_BLOCK, 2 * ATT_BLOCK), 1)
    bm = jnp.where(jnp.logical_and(j >= i, j <= i + ATT_BLOCK), bias, -jnp.inf)
    return bm, bm[:, ATT_BLOCK:]


def _attn_tile(qn, kn, v, bias):
    s = bdot_nt(qn, kn) * (HEAD_DIM ** -0.5) + bias
    m = lax.stop_gradient(jnp.max(s, axis=-1, keepdims=True))
    p = jnp.exp(s - m)
    l = jnp.sum(p, axis=-1, keepdims=True)
    o = bdot_nn(p, v) / l
    lse = jnp.broadcast_to(m + jnp.log(l), (ATT_BLOCK, HEAD_DIM))
    return o, lse


def _attn_tiles(dilation, rows=SEQ):
    nb = rows // dilation // ATT_BLOCK
    return [(r, n) for r in range(dilation) for n in range(nb)]


def _attn_rows(r, n, dilation, nblk=1):
    if dilation == 1:
        return pl.ds(r + n * ATT_BLOCK, nblk * ATT_BLOCK)
    return pl.ds(r + n * ATT_BLOCK * dilation, nblk * ATT_BLOCK, stride=dilation)


_QL, _KL, _VL = slice(0, 128), slice(128, 256), slice(256, 384)


def _qkv_specs(hb):
    return [pl.BlockSpec((None, SEQ, HEAD_DIM), functools.partial(lambda b, h, j: (b, 0, 3 * (hb + h) + j), j=j))
            for j in range(3)]


def attn_fwd(pa, bias, qg, kg, group, name):
    B = pa.shape[0]
    dilation = ATT_GROUPS[group][1]
    hb = group * 4

    def body(q_ref, k_ref, v_ref, b_ref, qg_ref, kg_ref, o_ref, l_ref, qn_s, kn_s):
        qn_s[...] = _qk_norm(q_ref[...], qg_ref[...])
        kn_s[...] = _qk_norm(k_ref[...], kg_ref[...])
        bias_all, bias_first = _masked_bias(b_ref[0])
        for (r, n) in _attn_tiles(dilation):
            rows = _attn_rows(r, n, dilation)
            if n == 0:
                krows, bias_t = rows, bias_first
            else:
                krows, bias_t = _attn_rows(r, n - 1, dilation, 2), bias_all
            o, lse = _attn_tile(qn_s[rows, :], kn_s[krows, :], v_ref[krows, :], bias_t)
            o_ref[rows, :] = o
            l_ref[rows, :] = lse

    head_out = pl.BlockSpec((None, SEQ, HEAD_DIM), lambda b, h: (b, 0, h))
    return pl.pallas_call(
        body, name=name, grid=(B, 4),
        in_specs=_qkv_specs(hb) + [
                  pl.BlockSpec((1, ATT_BLOCK, 2 * ATT_BLOCK), lambda b, h: (hb + h, 0, 0)),
                  pl.BlockSpec((1, HEAD_DIM), lambda b, h: (0, 0)),
                  pl.BlockSpec((1, HEAD_DIM), lambda b, h: (0, 0))],
        out_specs=[head_out, head_out],
        out_shape=[jax.ShapeDtypeStruct((B, SEQ, 512), F32), jax.ShapeDtypeStruct((B, SEQ, 512), F32)],
        scratch_shapes=[pltpu.VMEM((SEQ, HEAD_DIM), F32)] * 2,
        compiler_params=_cparams(),
    )(pa, pa, pa, bias, qg, kg)


def attn_bwd(pa, bias, qg, kg, do, dlse, dpa, group, name):
    B = pa.shape[0]
    dilation = ATT_GROUPS[group][1]
    hb = group * 4

    def body(q_ref, k_ref, v_ref, b_ref, qg_ref, kg_ref, do_ref, dl_ref, dpa_in,
             dp_ref, db_ref, dqg_ref, dkg_ref, qn_s, kn_s, dq_s, dk_s, dv_s):
        del dpa_in
        h_id = pl.program_id(1)

        @pl.when(jnp.logical_and(pl.program_id(0) == 0, h_id == 0))
        def _():
            db_ref[...] = jnp.zeros_like(db_ref)
            dqg_ref[...] = jnp.zeros_like(dqg_ref)
            dkg_ref[...] = jnp.zeros_like(dkg_ref)

        dk_s[...] = jnp.zeros_like(dk_s)
        dv_s[...] = jnp.zeros_like(dv_s)
        qn_s[...] = _qk_norm(q_ref[...], qg_ref[...])
        kn_s[...] = _qk_norm(k_ref[...], kg_ref[...])
        bias_all, bias_first = _masked_bias(b_ref[0])
        for (r, n) in _attn_tiles(dilation):
            rows = _attn_rows(r, n, dilation)
            if n == 0:
                krows, bias_t = rows, bias_first
            else:
                krows, bias_t = _attn_rows(r, n - 1, dilation, 2), bias_all
            _, vjp = jax.vjp(_attn_tile, qn_s[rows, :], kn_s[krows, :], v_ref[krows, :], bias_t)
            dqn, dkn, dv, dbias = vjp((do_ref[rows, :], dl_ref[rows, :]))
            dq_s[rows, :] = dqn
            dk_s[krows, :] += dkn
            dv_s[krows, :] += dv
            if n == 0:
                db_ref[h_id, :, ATT_BLOCK:] += dbias
            else:
                db_ref[h_id] += dbias
        for x_ref, g_ref, d_s, dg_ref, lanes in ((q_ref, qg_ref, dq_s, dqg_ref, _QL), (k_ref, kg_ref, dk_s, dkg_ref, _KL)):
            _, vjp = jax.vjp(_qk_norm, x_ref[...], g_ref[...])
            dx, dg = vjp(d_s[...])
            dp_ref[0, :, lanes] = dx.astype(BF16)
            dg_ref[...] += dg
        dp_ref[0, :, _VL] = dv_s[...].astype(BF16)

    const2 = lambda b, h: (0, 0)
    head_in = pl.BlockSpec((None, SEQ, HEAD_DIM), lambda b, h: (b, 0, h))
    head_blk = pl.BlockSpec((1, SEQ, ATT_HEAD_COLS), lambda b, h: (b, 0, hb + h))
    return pl.pallas_call(
        body, name=name, grid=(B, 4),
        in_specs=_qkv_specs(hb) + [
                  pl.BlockSpec((1, ATT_BLOCK, 2 * ATT_BLOCK), lambda b, h: (hb + h, 0, 0)),
                  pl.BlockSpec((1, HEAD_DIM), const2), pl.BlockSpec((1, HEAD_DIM), const2),
                  head_in, head_in,
                  pl.BlockSpec(memory_space=pl.ANY)],
        out_specs=[head_blk,
                   pl.BlockSpec((4, ATT_BLOCK, 2 * ATT_BLOCK), lambda b, h: (0, 0, 0)),
                   pl.BlockSpec((1, HEAD_DIM), const2), pl.BlockSpec((1, HEAD_DIM), const2)],
        out_shape=[jax.ShapeDtypeStruct(dpa.shape, BF16),
                   jax.ShapeDtypeStruct((4, ATT_BLOCK, 2 * ATT_BLOCK), F32),
                   jax.ShapeDtypeStruct((1, HEAD_DIM), F32), jax.ShapeDtypeStruct((1, HEAD_DIM), F32)],
        scratch_shapes=[pltpu.VMEM((SEQ, HEAD_DIM), F32)] * 5,
        input_output_aliases={8: 0},
        compiler_params=_cparams(),
    )(pa, pa, pa, bias, qg, kg, do, dlse, dpa)


def _attn_classes(q, k, v, bias, qg, kg):
    s = cdot_nt(_qk_norm(q, qg), _qk_norm(k, kg)) * (HEAD_DIM ** -0.5) + bias
    m = lax.stop_gradient(jnp.max(s, axis=-1, keepdims=True))
    p = jnp.exp(s - m)
    l = jnp.sum(p, axis=-1, keepdims=True)
    o = cdot_nn(p, v) / l
    return o, jnp.broadcast_to(m + jnp.log(l), o.shape)


def _gather_classes(src_ref, dst_s, dilation):
    for r in range(dilation):
        dst_s[r] = src_ref[pl.ds(r, ATT_BLOCK, stride=dilation), :]


def _scatter_classes(src_s, dst_ref, dilation):
    for r in range(dilation):
        dst_ref[pl.ds(r, ATT_BLOCK, stride=dilation), :] = src_s[r]


def attn_fwd_classes(pa, bias, qg, kg, group, name):
    B = pa.shape[0]
    dilation = ATT_GROUPS[group][1]
    hb = group * 4

    def body(q_ref, k_ref, v_ref, b_ref, qg_ref, kg_ref, o_ref, l_ref, q_s, k_s, v_s):
        _gather_classes(q_ref, q_s, dilation)
        _gather_classes(k_ref, k_s, dilation)
        _gather_classes(v_ref, v_s, dilation)
        o, lse = _attn_classes(q_s[...], k_s[...], v_s[...], _masked_bias(b_ref[0])[1], qg_ref[...], kg_ref[...])
        q_s[...], k_s[...] = o, lse
        _scatter_classes(q_s, o_ref, dilation)
        _scatter_classes(k_s, l_ref, dilation)

    head_out = pl.BlockSpec((None, SEQ, HEAD_DIM), lambda b, h: (b, 0, h))
    return pl.pallas_call(
        body, name=name, grid=(B, 4),
        in_specs=_qkv_specs(hb) + [
                  pl.BlockSpec((1, ATT_BLOCK, 2 * ATT_BLOCK), lambda b, h: (hb + h, 0, 0)),
                  pl.BlockSpec((1, HEAD_DIM), lambda b, h: (0, 0)),
                  pl.BlockSpec((1, HEAD_DIM), lambda b, h: (0, 0))],
        out_specs=[head_out, head_out],
        out_shape=[jax.ShapeDtypeStruct((B, SEQ, 512), F32), jax.ShapeDtypeStruct((B, SEQ, 512), F32)],
        scratch_shapes=[pltpu.VMEM((dilation, ATT_BLOCK, HEAD_DIM), F32)] * 3,
        compiler_params=_cparams(),
    )(pa, pa, pa, bias, qg, kg)


def attn_bwd_classes(pa, bias, qg, kg, do, dlse, dpa, group, name):
    B = pa.shape[0]
    dilation = ATT_GROUPS[group][1]
    hb = group * 4

    def body(q_ref, k_ref, v_ref, b_ref, qg_ref, kg_ref, do_ref, dl_ref, dpa_in,
             dp_ref, db_ref, dqg_ref, dkg_ref, q_s, k_s, v_s, do_s, dl_s, rows_s):
        del dpa_in
        h_id = pl.program_id(1)

        @pl.when(jnp.logical_and(pl.program_id(0) == 0, h_id == 0))
        def _():
            db_ref[...] = jnp.zeros_like(db_ref)
            dqg_ref[...] = jnp.zeros_like(dqg_ref)
            dkg_ref[...] = jnp.zeros_like(dkg_ref)

        for src, dst in ((q_ref, q_s), (k_ref, k_s), (v_ref, v_s), (do_ref, do_s), (dl_ref, dl_s)):
            _gather_classes(src, dst, dilation)
        _, vjp = jax.vjp(_attn_classes, q_s[...], k_s[...], v_s[...], _masked_bias(b_ref[0])[1],
                         qg_ref[...], kg_ref[...])
        dq, dk, dv, dbias, dqg, dkg = vjp((do_s[...], dl_s[...]))
        db_ref[h_id, :, ATT_BLOCK:] += dbias
        dqg_ref[...] += dqg
        dkg_ref[...] += dkg
        for d, lanes in ((dq, _QL), (dk, _KL), (dv, _VL)):
            q_s[...] = d
            _scatter_classes(q_s, rows_s, dilation)
            dp_ref[0, :, lanes] = rows_s[...].astype(BF16)

    const2 = lambda b, h: (0, 0)
    head_in = pl.BlockSpec((None, SEQ, HEAD_DIM), lambda b, h: (b, 0, h))
    head_blk = pl.BlockSpec((1, SEQ, ATT_HEAD_COLS), lambda b, h: (b, 0, hb + h))
    return pl.pallas_call(
        body, name=name, grid=(B, 4),
        in_specs=_qkv_specs(hb) + [
                  pl.BlockSpec((1, ATT_BLOCK, 2 * ATT_BLOCK), lambda b, h: (hb + h, 0, 0)),
                  pl.BlockSpec((1, HEAD_DIM), const2), pl.BlockSpec((1, HEAD_DIM), const2),
                  head_in, head_in,
                  pl.BlockSpec(memory_space=pl.ANY)],
        out_specs=[head_blk,
                   pl.BlockSpec((4, ATT_BLOCK, 2 * ATT_BLOCK), lambda b, h: (0, 0, 0)),
                   pl.BlockSpec((1, HEAD_DIM), const2), pl.BlockSpec((1, HEAD_DIM), const2)],
        out_shape=[jax.ShapeDtypeStruct(dpa.shape, BF16),
                   jax.ShapeDtypeStruct((4, ATT_BLOCK, 2 * ATT_BLOCK), F32),
                   jax.ShapeDtypeStruct((1, HEAD_DIM), F32), jax.ShapeDtypeStruct((1, HEAD_DIM), F32)],
        scratch_shapes=[pltpu.VMEM((dilation, ATT_BLOCK, HEAD_DIM), F32)] * 5 + [pltpu.VMEM((SEQ, HEAD_DIM), F32)],
        input_output_aliases={8: 0},
        compiler_params=_cparams(),
    )(pa, pa, pa, bias, qg, kg, do, dlse, dpa)


def _merge(o0, o1, o2, l0, l1, l2):
    mx = jnp.maximum(jnp.maximum(l0, l1), l2)
    e0, e1, e2 = jnp.exp(l0 - mx), jnp.exp(l1 - mx), jnp.exp(l2 - mx)
    den = e0 + e1 + e2
    return (e0 / den) * o0 + (e1 / den) * o1 + (e2 / den) * o2


def merge_att_out(os_, ls_, w_att_out, name):
    T = os_[0].shape[0]
    tile = pl.BlockSpec((ROW_MM_TILE, 512), lambda i: (i, 0))

    def body(o0, o1, o2, l0, l1, l2, w_ref, a_ref, y_ref):
        att = _merge(o0[...], o1[...], o2[...], l0[...], l1[...], l2[...]).astype(BF16)
        a_ref[...] = att
        y_ref[...] = _mxu(att, w_ref[...], _NN)

    return pl.pallas_call(
        body, name=name, grid=(T // ROW_MM_TILE,),
        in_specs=[tile] * 6 + [pl.BlockSpec((512, D_MODEL), lambda i: (0, 0))],
        out_specs=[tile, pl.BlockSpec((ROW_MM_TILE, D_MODEL), lambda i: (i, 0))],
        out_shape=[jax.ShapeDtypeStruct((T, 512), BF16), jax.ShapeDtypeStruct((T, D_MODEL), F32)],
        compiler_params=_cparams(),
    )(*os_, *ls_, w_att_out)


def d_att_merge_bwd(d_ya, w_att_out, os_, ls_, name):
    T = d_ya.shape[0]
    tile = lambda a: (a, (ROW_MM_TILE, 512), lambda i: (i, 0))

    def epi(acc, i, ex, out):
        _, vjp = jax.vjp(_merge, *[e[...] for e in ex])
        for o_ref, g in zip(out, vjp(acc)):
            o_ref[...] = g

    return row_matmul(d_ya, w_att_out, mode="nt", name=name, extras=[tile(a) for a in list(os_) + list(ls_)],
                      outs=[((T, 512), F32, (ROW_MM_TILE, 512), lambda i: (i, 0))] * 6, epi=epi)


def _gate_mix(ga, gm, ya, ym):
    return jax.nn.sigmoid(ga) * ya + jax.nn.sigmoid(gm) * ym


def _gate_halves(pg):
    return [(pg, (ROW_MM_TILE, D_MODEL), lambda i: (i, 0)), (pg, (ROW_MM_TILE, D_MODEL), lambda i: (i, 1))]


def ml_out_gate(hg, w_ml_out, pg, ya, name):
    T = hg.shape[0]

    def epi(acc, i, ex, out):
        ga, gm, ya_ref = ex
        out[0][...] = acc
        out[1][...] = _gate_mix(ga[...], gm[...], ya_ref[...], acc).astype(BF16)

    return row_matmul(hg, w_ml_out, mode="nn", name=name, extras=_gate_halves(pg) + [_rows(ya)],
                      outs=[_rows_out(T, F32), _rows_out(T, BF16)], epi=epi)


def d_z_gate_bwd(dy, w_out, pg, ya, ym, name):
    T = dy.shape[0]

    def epi(acc, i, ex, out):
        ga, gm, ya_ref, ym_ref = ex
        dpg_ref, dya_ref, dym_ref = out
        _, vjp = jax.vjp(_gate_mix, ga[...], gm[...], ya_ref[...], ym_ref[...])
        dga, dgm, dya, dym = vjp(acc)
        dpg_ref[:, :D_MODEL] = dga.astype(BF16)
        dpg_ref[:, D_MODEL:] = dgm.astype(BF16)
        dya_ref[...] = dya.astype(BF16)
        dym_ref[...] = dym.astype(BF16)

    return row_matmul(dy, w_out, mode="nt", name=name, extras=_gate_halves(pg) + [_rows(ya), _rows(ym)],
                      outs=[((T, GATE_COLS), BF16, (ROW_MM_TILE, GATE_COLS), lambda i: (i, 0)),
                            _rows_out(T, BF16), _rows_out(T, BF16)], epi=epi)


def _log_sigmoid(x):
    return jnp.minimum(x, 0.0) - jnp.log(1.0 + jnp.exp(-jnp.abs(x)))


def _head_mask(e):
    lane = lax.broadcasted_iota(jnp.int32, (1, 128), 1)
    return jnp.logical_and(lane >= e * 64, lane < (e + 1) * 64).astype(F32)


def _bmxu(a, b, ca, cb):
    return lax.dot_general(a.astype(BF16), b.astype(BF16), (((ca,), (cb,)), ((0,), (0,))), preferred_element_type=F32)


@jax.custom_vjp
def cdot_nt(a, b):
    return _bmxu(a, b, 2, 2)


cdot_nt.defvjp(lambda a, b: (_bmxu(a, b, 2, 2), (a, b)),
               lambda res, g: (_bmxu(g, res[1], 2, 1), _bmxu(g, res[0], 1, 1)))


@jax.custom_vjp
def cdot_nn(a, b):
    return _bmxu(a, b, 2, 1)


cdot_nn.defvjp(lambda a, b: (_bmxu(a, b, 2, 1), (a, b)),
               lambda res, g: (_bmxu(g, res[1], 2, 2), _bmxu(res[0], g, 1, 1)))


@jax.custom_vjp
def cdot_tn(a, b):
    return _bmxu(a, b, 1, 1)


cdot_tn.defvjp(lambda a, b: (_bmxu(a, b, 1, 1), (a, b)),
               lambda res, g: (_bmxu(res[1], g, 2, 2), _bmxu(res[0], g, 2, 1)))


def _top_bits(x):
    return lax.bitcast_convert_type(lax.bitcast_convert_type(x, jnp.uint32) & jnp.uint32(0xFFFF0000), F32)


def _split3(x):
    hi = _top_bits(x)
    r = x - hi
    mid = _top_bits(r)
    return hi, mid, r - mid


def _parts_in_lanes(col):
    hi, mid, lo = _split3(col)
    lane = lax.broadcasted_iota(jnp.int32, (1, 1, 8), 2)
    return jnp.where(lane == 0, hi, jnp.where(lane == 1, mid, jnp.where(lane == 2, lo, 0.0)))


def _parts_in_rows(row):
    hi, mid, lo = _split3(row)
    sub = lax.broadcasted_iota(jnp.int32, (1, 8, 1), 1)
    return jnp.where(sub == 0, hi, jnp.where(sub == 1, mid, jnp.where(sub == 2, lo, 0.0)))


def _chunk_matrix(kind, c):
    ri = lax.broadcasted_iota(jnp.int32, (c, ML_CHUNK, ML_CHUNK), 1)
    ci = lax.broadcasted_iota(jnp.int32, (c, ML_CHUNK, ML_CHUNK), 2)
    return {"eye": ri == ci, "lower": ri >= ci, "upper": ri <= ci}[kind].astype(F32)


def _col_col(kind, col):
    out = _bmxu(_chunk_matrix(kind, col.shape[0]), _parts_in_lanes(col), 2, 1)
    return jnp.sum(out, axis=-1, keepdims=True)


def _col_row(col):
    out = _bmxu(_parts_in_lanes(col), _chunk_matrix("eye", col.shape[0]), 1, 1)
    return jnp.sum(out, axis=1, keepdims=True)


def _row_col(row):
    out = _bmxu(_chunk_matrix("eye", row.shape[0]), _parts_in_rows(row), 2, 2)
    return jnp.sum(out, axis=-1, keepdims=True)


@jax.custom_vjp
def chunk_cumsum(col):
    return _col_col("lower", col)


chunk_cumsum.defvjp(lambda col: (_col_col("lower", col), None), lambda _, g: (_col_col("upper", g),))


@jax.custom_vjp
def col_to_row(col):
    return _col_row(col)


col_to_row.defvjp(lambda col: (_col_row(col), None), lambda _, g: (_row_col(g),))


def _gate_block(ifb):
    lane = lax.broadcasted_iota(jnp.int32, (1, 128), 1)
    return jnp.where(lane >= 2, _log_sigmoid(ifb), ifb)


def _ml_intra(q2, k2, v, ifb, *, e):
    c, L = q2.shape[0] // ML_CHUNK, ML_CHUNK
    hm = _head_mask(e)
    q3 = (q2 * hm).reshape(c, L, 128)
    k3 = (k2 * hm).reshape(c, L, 128)
    v3 = v.reshape(c, L, 128)
    if3 = ifb.reshape(c, L, 128)
    lanes = lax.broadcasted_iota(jnp.int32, (c, L, 128), 2)
    li = jnp.sum(jnp.where(lanes == e, if3, 0.0), axis=-1, keepdims=True)
    lf = jnp.sum(jnp.where(lanes == 2 + e, if3, 0.0), axis=-1, keepdims=True)
    b = chunk_cumsum(lf)
    last = lax.broadcasted_iota(jnp.int32, (1, L, 1), 1) == L - 1
    b_end = jnp.sum(jnp.where(last, b, 0.0), axis=1, keepdims=True)
    causal = lax.broadcasted_iota(jnp.int32, (L, L), 0) >= lax.broadcasted_iota(jnp.int32, (L, L), 1)
    Dm = jnp.where(causal, b + col_to_row(li - b), -jnp.inf)
    mD = lax.stop_gradient(jnp.max(Dm, axis=-1, keepdims=True))
    P0 = cdot_nt(q3, k3) * jnp.exp(Dm - mD)
    H0 = cdot_nn(P0, v3)
    r0 = jnp.sum(P0, axis=-1, keepdims=True)
    g = b_end - b + li
    mg = lax.stop_gradient(jnp.max(g, axis=1, keepdims=True))
    kw = jnp.exp(g - mg) * k3
    return H0, r0, cdot_tn(kw, v3), jnp.sum(kw, axis=1, keepdims=True), b, b_end, mD, mg


def _ml_inter(q2, mo, gn, H0, r0, b, C_in, n_in, *, mD, m_in, e):
    c, L = q2.shape[0] // ML_CHUNK, ML_CHUNK
    q3 = (q2 * _head_mask(e)).reshape(c, L, 128)
    a = b + m_in
    m_t = lax.stop_gradient(jnp.maximum(a, mD))
    c1 = jnp.exp(mD - m_t)
    c2 = jnp.exp(a - m_t)
    num = c1 * H0 + c2 * cdot_nn(q3, C_in)
    nq = c1 * r0 + c2 * jnp.sum(q3 * n_in, axis=-1, keepdims=True)
    h = num / jnp.maximum(jnp.abs(nq), jnp.exp(-m_t))
    hg = _rms(h) * gn * jax.nn.sigmoid(mo.reshape(c, L, 128))
    return hg.reshape(c * L, 128)


def _state_scalars(be, mg):
    c = be.shape[0]

    def shift(x, k, fill):
        return jnp.concatenate([jnp.full((k, 1, 1), fill, F32), x[:c - k]], axis=0)

    run = be
    k = 1
    while k < c:
        run = run + shift(run, k, 0.0)
        k *= 2
    top = mg - run
    k = 1
    while k < c:
        top = jnp.maximum(top, shift(top, k, -jnp.inf))
        k *= 2
    m_in = shift(run, 1, 0.0) + jnp.maximum(shift(top, 1, -jnp.inf), 0.0)
    m_next = jnp.maximum(be + m_in, mg)
    return m_in, jnp.exp(be + m_in - m_next), jnp.exp(mg - m_next)


def _state_sweep(U_s, un_s, be_s, mg_s, Cin_s, nin_s, min_s, al_s, bt_s):
    min_s[...], al_s[...], bt_s[...] = _state_scalars(be_s[...], mg_s[...])

    def step(j, carry):
        C, n = carry
        Cin_s[j], nin_s[j] = C, n
        return al_s[j] * C + bt_s[j] * U_s[j], al_s[j] * n + bt_s[j] * un_s[j]

    carry = (jnp.zeros((128, 128), F32), jnp.zeros((1, 128), F32))
    for j in range(N_CHUNKS):
        carry = step(j, carry)


def _state_sweep_pair(set0, set1):
    carries = []
    for st in (set0, set1):
        U_s, un_s, be_s, mg_s, Cin_s, nin_s, min_s, al_s, bt_s = st
        min_s[...], al_s[...], bt_s[...] = _state_scalars(be_s[...], mg_s[...])
        carries.append((jnp.zeros((128, 128), F32), jnp.zeros((1, 128), F32)))
    for j in range(N_CHUNKS):
        for e, st in enumerate((set0, set1)):
            U_s, un_s, _, _, Cin_s, nin_s, _, al_s, bt_s = st
            C, n = carries[e]
            Cin_s[j], nin_s[j] = C, n
            carries[e] = (al_s[j] * C + bt_s[j] * U_s[j], al_s[j] * n + bt_s[j] * un_s[j])


def _state_sweep_bwd(U_s, un_s, dbe_s, Cin_s, nin_s, dCp_s, dnp_s, al_s, bt_s):
    def step(t, carry):
        j = N_CHUNKS - 1 - t
        dC, dn = carry
        al, bt = al_s[j], bt_s[j]
        U_s[j] = bt * dC
        un_s[j] = bt * dn
        dal = jnp.sum(jnp.sum(dC * Cin_s[j], axis=1, keepdims=True), axis=0, keepdims=True) \
            + jnp.sum(dn * nin_s[j], axis=1, keepdims=True)
        dbe_s[j] = dal * al
        return dCp_s[j] + al * dC, dnp_s[j] + al * dn

    carry = (jnp.zeros((128, 128), F32), jnp.zeros((1, 128), F32))
    for t in range(N_CHUNKS):
        carry = step(t, carry)


def _state_scratch():
    c = N_CHUNKS
    return [pltpu.VMEM((c, 128, 128), F32), pltpu.VMEM((c, 1, 128), F32), pltpu.VMEM((c, 1, 1), F32),
            pltpu.VMEM((c, 1, 1), F32),
            pltpu.VMEM((c, 128, 128), F32), pltpu.VMEM((c, 1, 128), F32), pltpu.VMEM((c, 1, 1), F32),
            pltpu.VMEM((c, 1, 1), F32), pltpu.VMEM((c, 1, 1), F32)]


def _shift_down(x, s):
    if s == 0:
        return x
    rows = lax.broadcasted_iota(jnp.int32, x.shape, 0)
    return jnp.where(rows >= s, pltpu.roll(x, s, 0), 0.0)


def _shift_up(x, s):
    if s == 0:
        return x
    S = x.shape[0]
    rows = lax.broadcasted_iota(jnp.int32, x.shape, 0)
    return jnp.where(rows < S - s, pltpu.roll(x, S - s, 0), 0.0)


def _conv_pre(x, cw, cb):
    y = cb + cw[3:4, :] * x
    for j in range(3):
        y = y + cw[j:j + 1, :] * _shift_down(x, 3 - j)
    return y


def _conv_bwd(x, cw, dpre):
    dx = cw[3:4, :] * dpre
    dcw = [None] * 4
    dcw[3] = jnp.sum(dpre * x, axis=0, keepdims=True)
    for j in range(3):
        dx = dx + cw[j:j + 1, :] * _shift_up(dpre, 3 - j)
        dcw[j] = jnp.sum(dpre * _shift_down(x, 3 - j), axis=0, keepdims=True)
    return dx, dcw, jnp.sum(dpre, axis=0, keepdims=True)


def _silu(z):
    return z * jax.nn.sigmoid(z)


def _dsilu(z):
    s = jax.nn.sigmoid(z)
    return s * (1.0 + z * (1.0 - s))


_ML_Q, _ML_K = slice(0, 128), slice(128, 256)
_ML_IF = slice(768, 896)


def _ml_v(e):
    return slice(256 + e * 128, 384 + e * 128)


def _ml_o(e):
    return slice(512 + e * 128, 640 + e * 128)


def _ml_specs():
    pair = lambda b, p: (b, 0, p)
    return [pl.BlockSpec((1, SEQ, ML_PAIR_COLS), pair),
            pl.BlockSpec((1, 4, 128), lambda b, p: (p, 0, 0)),
            pl.BlockSpec((1, 4, 128), lambda b, p: (4 + p, 0, 0)),
            pl.BlockSpec((1, 1, 128), lambda b, p: (p, 0, 0)),
            pl.BlockSpec((1, 1, 128), lambda b, p: (4 + p, 0, 0)),
            pl.BlockSpec((1, 1, 128), lambda b, p: (p, 0, 0)),
            pl.BlockSpec((1, 1, 256), lambda b, p: (p, 0, 0))]


def mlstm_fwd(pm, cw8, cb8, bifp, gn4, name):
    B = pm.shape[0]

    def body(p_ref, cwq, cwk, cbq, cbk, bif_ref, gn_ref, hg_ref, *scr):
        sets = (scr[:9], scr[9:])
        qc = _silu(_conv_pre(p_ref[0, :, _ML_Q], cwq[0], cbq[0]))
        kc = _silu(_conv_pre(p_ref[0, :, _ML_K], cwk[0], cbk[0])) * (64 ** -0.5)
        ifb = _gate_block(p_ref[0, :, _ML_IF] + bif_ref[0])
        kept = []
        for e in range(2):
            U_s, un_s, be_s, mg_s = sets[e][:4]
            H0, r0, U, un, b, b_end, mD, mg = _ml_intra(qc, kc, p_ref[0, :, _ml_v(e)], ifb, e=e)
            U_s[...], un_s[...], be_s[...], mg_s[...] = U, un, b_end, mg
            kept.append((H0, r0, b, mD))
        _state_sweep_pair(*sets)
        for e in range(2):
            lanes = slice(e * 128, (e + 1) * 128)
            Cin_s, nin_s, min_s = sets[e][4:7]
            H0, r0, b, mD = kept[e]
            hg = _ml_inter(qc, p_ref[0, :, _ml_o(e)], gn_ref[0, :, lanes], H0, r0, b, Cin_s[...], nin_s[...],
                           mD=mD, m_in=min_s[...], e=e)
            hg_ref[0, :, lanes] = hg.astype(BF16)

    return pl.pallas_call(
        body, name=name, grid=(B, ML_PAIRS),
        in_specs=_ml_specs(),
        out_specs=pl.BlockSpec((1, SEQ, 256), lambda b, p: (b, 0, p)),
        out_shape=jax.ShapeDtypeStruct((B, SEQ, D_MODEL), BF16),
        scratch_shapes=_state_scratch() * 2,
        compiler_params=_cparams(),
    )(pm, cw8, cw8, cb8, cb8, bifp, gn4)


def mlstm_bwd(pm, cw8, cb8, bifp, gn4, dhg, name):
    B = pm.shape[0]

    def body(p_ref, cwq, cwk, cbq, cbk, bif_ref, gn_ref, dh_ref,
             dp_ref, dcw_ref, dcb_ref, dbif_ref, dgn_ref, *scr):
        st = scr[:9]
        U_s, un_s, be_s, mg_s, Cin_s, nin_s, min_s, al_s, bt_s = st
        dCp_s, dnp_s, dbe_s = scr[9:]
        p_id = pl.program_id(1)

        @pl.when(jnp.logical_and(pl.program_id(0) == 0, p_id == 0))
        def _():
            dcw_ref[...] = jnp.zeros_like(dcw_ref)
            dcb_ref[...] = jnp.zeros_like(dcb_ref)
            dbif_ref[...] = jnp.zeros_like(dbif_ref)
            dgn_ref[...] = jnp.zeros_like(dgn_ref)

        pre_q = _conv_pre(p_ref[0, :, _ML_Q], cwq[0], cbq[0])
        pre_k = _conv_pre(p_ref[0, :, _ML_K], cwk[0], cbk[0])
        qc = _silu(pre_q)
        kc = _silu(pre_k) * (64 ** -0.5)
        ifb, gate_vjp = jax.vjp(_gate_block, p_ref[0, :, _ML_IF] + bif_ref[0])
        dq = jnp.zeros((SEQ, 128), F32)
        dk = jnp.zeros((SEQ, 128), F32)
        difb = jnp.zeros((SEQ, 128), F32)
        for e in range(2):
            lanes = slice(e * 128, (e + 1) * 128)
            (H0, r0, U, un, b, b_end, mD, mg), vjp1 = jax.vjp(functools.partial(_ml_intra, e=e), qc, kc,
                                                              p_ref[0, :, _ml_v(e)], ifb)
            U_s[...], un_s[...], be_s[...], mg_s[...] = U, un, b_end, mg
            _state_sweep(*st)
            _, vjp3 = jax.vjp(functools.partial(_ml_inter, mD=mD, m_in=min_s[...], e=e), qc, p_ref[0, :, _ml_o(e)],
                              gn_ref[0, :, lanes], H0, r0, b, Cin_s[...], nin_s[...])
            dq_a, dmo, dgn, dH0, dr0, db_a, dCp, dnp = vjp3(dh_ref[0, :, lanes])
            dCp_s[...], dnp_s[...] = dCp, dnp
            _state_sweep_bwd(U_s, un_s, dbe_s, Cin_s, nin_s, dCp_s, dnp_s, al_s, bt_s)
            dq_b, dk_b, dv, difb_e = vjp1((dH0, dr0, U_s[...], un_s[...], db_a, dbe_s[...],
                                           jnp.zeros_like(mD), jnp.zeros_like(mg)))
            dq, dk, difb = dq + dq_a + dq_b, dk + dk_b, difb + difb_e
            dp_ref[0, :, _ml_v(e)] = dv.astype(BF16)
            dp_ref[0, :, _ml_o(e)] = dmo.astype(BF16)
            dgn_ref[p_id, :, lanes] += dgn
        (difb,) = gate_vjp(difb)
        dp_ref[0, :, _ML_IF] = difb.astype(BF16)
        dbif_ref[p_id] += jnp.sum(difb, axis=0, keepdims=True)

        for (sl, cw, pre, d, blk, scale) in ((_ML_Q, cwq, pre_q, dq, p_id, 1.0), (_ML_K, cwk, pre_k, dk, 4 + p_id, 64 ** -0.5)):
            xr = p_ref[0, :, sl]
            dpre = d * scale * _dsilu(pre)
            dx, dcw, dcb = _conv_bwd(xr, cw[0], dpre)
            dp_ref[0, :, sl] = dx.astype(BF16)
            for j in range(4):
                dcw_ref[blk, j:j + 1, :] += dcw[j]
            dcb_ref[blk] += dcb

    full3 = lambda b, p: (0, 0, 0)
    return pl.pallas_call(
        body, name=name, grid=(B, ML_PAIRS),
        in_specs=[pl.BlockSpec((1, SEQ, ML_PAIR_COLS), lambda b, p: (b, 0, p), pipeline_mode=pl.Buffered(1))]
        + _ml_specs()[1:] + [pl.BlockSpec((1, SEQ, 256), lambda b, p: (b, 0, p), pipeline_mode=pl.Buffered(1))],
        out_specs=[pl.BlockSpec((1, SEQ, ML_PAIR_COLS), lambda b, p: (b, 0, p)),
                   pl.BlockSpec((8, 4, 128), full3), pl.BlockSpec((8, 1, 128), full3),
                   pl.BlockSpec((4, 1, 128), full3), pl.BlockSpec((4, 1, 256), full3)],
        out_shape=[jax.ShapeDtypeStruct((B, SEQ, ML_COLS), BF16),
                   jax.ShapeDtypeStruct((8, 4, 128), F32), jax.ShapeDtypeStruct((8, 1, 128), F32),
                   jax.ShapeDtypeStruct((4, 1, 128), F32), jax.ShapeDtypeStruct((4, 1, 256), F32)],
        scratch_shapes=_state_scratch() + [pltpu.VMEM((N_CHUNKS, 128, 128), F32), pltpu.VMEM((N_CHUNKS, 1, 128), F32),
                                           pltpu.VMEM((N_CHUNKS, 1, 1), F32)],
        compiler_params=_cparams(),
    )(pm, cw8, cw8, cb8, cb8, bifp, gn4, dhg)


def _adamw(w, g, m, v):
    m = ADAM_B1 * m + (1.0 - ADAM_B1) * g
    v = ADAM_B2 * v + (1.0 - ADAM_B2) * (g * g)
    m_hat = m / (1.0 - ADAM_B1 ** ADAM_STEP)
    v_hat = v / (1.0 - ADAM_B2 ** ADAM_STEP)
    delta = -ADAM_LR * (m_hat / (jnp.sqrt(v_hat) + ADAM_EPS) + ADAM_WD * w)
    return delta, m, v


def adamw(w, g, m, v, name, parts=False):
    R, C = w.shape
    if R % 8 == 0 or R * C * 4 <= (1 << 20):
        tr = _pick(R, (256, 128, 64, 32, 16, 8)) if R * C * 4 > (1 << 20) else R
        steps = R // tr
        spec = pl.BlockSpec((tr, C), lambda i: (i, 0))
        g_spec = pl.BlockSpec((N_DEV, tr, C), lambda i: (0, i, 0)) if parts else spec
    else:
        tc = _pick(C, (256, 128))
        steps = C // tc
        spec = pl.BlockSpec((R, tc), lambda i: (0, i))
        g_spec = pl.BlockSpec((N_DEV, R, tc), lambda i: (0, 0, i)) if parts else spec

    def body(w_ref, g_ref, m_ref, v_ref, go_ref, d_ref, mo_ref, vo_ref):
        if parts:
            g = g_ref[0].astype(F32)
            for k in range(1, N_DEV):
                g = g + g_ref[k].astype(F32)
        else:
            g = g_ref[...]
        d, mn, vn = _adamw(w_ref[...], g, m_ref[...], v_ref[...])
        go_ref[...], d_ref[...], mo_ref[...], vo_ref[...] = g, d, mn, vn

    return pl.pallas_call(
        body, name=name, grid=(steps,),
        in_specs=[spec, g_spec, spec, spec], out_specs=[spec] * 4,
        out_shape=[jax.ShapeDtypeStruct((R, C), F32)] * 4,
        compiler_params=_cparams(),
    )(w, g, m, v)


def adamw_many(ws, gs, ms, vs, name):
    n = len(ws)

    def fn(*a):
        out = []
        for j in range(n):
            out += list(_adamw(a[j], a[n + j], a[2 * n + j], a[3 * n + j]))
        return tuple(out)

    shapes = [jax.ShapeDtypeStruct(w.shape, F32) for w in ws for _ in range(3)]
    return small_call(fn, list(ws) + list(gs) + list(ms) + list(vs), shapes, name)


def _mesh_pos():
    return lax.axis_index("x"), lax.axis_index("y"), lax.axis_index("c")


def _flip(pos, f):
    x, y, c = pos
    return (1 - x if f & 4 else x, 1 - y if f & 2 else y, 1 - c if f & 1 else c)


def _index(pos):
    return 4 * pos[0] + 2 * pos[1] + pos[2]


def _exchange(arrs, name, scatter):
    n = len(arrs)
    scat = list(scatter) if isinstance(scatter, (list, tuple)) else [scatter] * n

    def body(*refs):
        ins, outs = refs[:n], refs[n:2 * n]
        send, recv, lsem = refs[2 * n:]
        me = _mesh_pos()
        mine = _index(me)
        copies = []
        for i in range(n):
            src = ins[i].at[mine] if scat[i] else ins[i]
            loc = pltpu.make_async_copy(src, outs[i].at[mine], lsem.at[i])
            loc.start()
            copies.append(loc)
            for f in range(1, N_DEV):
                peer = _flip(me, f)
                src = ins[i].at[_index(peer)] if scat[i] else ins[i]
                cp = pltpu.make_async_remote_copy(
                    src_ref=src, dst_ref=outs[i].at[mine],
                    send_sem=send.at[i * 7 + f - 1], recv_sem=recv.at[i * 7 + f - 1],
                    device_id=peer, device_id_type=pl.DeviceIdType.MESH)
                cp.start()
                copies.append(cp)
        for cp in copies:
            cp.wait()

    any_spec = pl.BlockSpec(memory_space=pl.ANY)
    out_shape = [jax.ShapeDtypeStruct(a.shape if s else (N_DEV,) + a.shape, a.dtype) for a, s in zip(arrs, scat)]
    res = pl.pallas_call(
        body, name=name,
        in_specs=[any_spec] * n, out_specs=[any_spec] * n, out_shape=out_shape,
        scratch_shapes=[pltpu.SemaphoreType.DMA((7 * n,)), pltpu.SemaphoreType.DMA((7 * n,)),
                        pltpu.SemaphoreType.DMA((n,))],
        compiler_params=_cparams(),
    )(*arrs)
    return list(res)


def all_gather(arrs, name):
    return _exchange(arrs, name, False)


def all_gather_two_level(arrs, name):
    n = len(arrs)

    def body(*refs):
        ins, outs = refs[:n], refs[n:2 * n]
        send, recv, lsem = refs[2 * n:]
        x, y, c = _mesh_pos()
        me, sibling = (x, y, c), (x, y, 1 - c)
        chips = [(1 - x, y), (x, 1 - y), (1 - x, 1 - y)]

        def copy(i, k, block, to, src=None):
            rows = outs[i].at[_index(block)]
            return pltpu.make_async_remote_copy(
                src_ref=rows if src is None else src, dst_ref=rows,
                send_sem=send.at[i * 7 + k], recv_sem=recv.at[i * 7 + k],
                device_id=to, device_id_type=pl.DeviceIdType.MESH)

        local = [pltpu.make_async_copy(ins[i], outs[i].at[_index(me)], lsem.at[i]) for i in range(n)]
        first = [copy(i, 0, me, sibling, src=ins[i]) for i in range(n)]
        first += [copy(i, 1 + j, me, (*chip, c), src=ins[i]) for i in range(n) for j, chip in enumerate(chips)]
        for cp in local + first:
            cp.start()
        passed = []
        for j, chip in enumerate(chips):
            for i in range(n):
                copy(i, 1 + j, (*chip, c), me).wait_recv()
                cp = copy(i, 4 + j, (*chip, c), sibling)
                cp.start()
                passed.append(cp)
        for i in range(n):
            copy(i, 0, sibling, me).wait_recv()
            for j, chip in enumerate(chips):
                copy(i, 4 + j, (*chip, 1 - c), me).wait_recv()
        for cp in first + passed:
            cp.wait_send()
        for cp in local:
            cp.wait()

    any_spec = pl.BlockSpec(memory_space=pl.ANY)
    res = pl.pallas_call(
        body, name=name,
        in_specs=[any_spec] * n, out_specs=[any_spec] * n,
        out_shape=[jax.ShapeDtypeStruct((N_DEV,) + a.shape, a.dtype) for a in arrs],
        scratch_shapes=[pltpu.SemaphoreType.DMA((7 * n,)), pltpu.SemaphoreType.DMA((7 * n,)),
                        pltpu.SemaphoreType.DMA((n,))],
        compiler_params=_cparams(),
    )(*arrs)
    return list(res)


def all_to_all(arrs, name):
    return _exchange(arrs, name, True)


_HBM = pl.BlockSpec(memory_space=pltpu.HBM)
_SEM = pl.BlockSpec(memory_space=pltpu.SEMAPHORE)
_EFFECT = pltpu.SideEffectType.DATAFLOW_SIDE_EFFECTING


def _split_copies(ins, lands, send, recv, scatter, waiting):
    me = _mesh_pos()
    mine = _index(me)
    copies = []
    for i in range(len(ins)):
        for f in range(1, N_DEV):
            peer = _flip(me, f)
            src = ins[i].at[_index(peer)] if scatter else ins[i]
            copies.append(pltpu.make_async_remote_copy(
                src_ref=src, dst_ref=lands[i].at[_index(peer) if waiting else mine],
                send_sem=send.at[i * 7 + f - 1], recv_sem=recv.at[i * 7 + f - 1],
                device_id=peer, device_id_type=pl.DeviceIdType.MESH))
    return copies


def exchange_start(arrs, name, scatter, after=()):
    n = len(arrs)
    land_shapes = [a.shape if scatter else (N_DEV,) + a.shape for a in arrs]

    def body(*refs):
        ins, lands = refs[:n], refs[n:2 * n]
        send, recv = refs[2 * n + len(after)], refs[2 * n + len(after) + 1]
        token = refs[-1]
        for cp in _split_copies(ins, lands, send, recv, scatter, False):
            cp.start()
        token[...] = jnp.zeros_like(token)

    res = pl.pallas_call(
        body, name=name,
        out_shape=(pltpu.SemaphoreType.DMA((7 * n,)), pltpu.SemaphoreType.DMA((7 * n,)),
                   *[pltpu.HBM(a.shape, a.dtype) for a in arrs],
                   *[pltpu.HBM(s, a.dtype) for s, a in zip(land_shapes, arrs)],
                   jax.ShapeDtypeStruct((8, 128), F32)),
        in_specs=[_HBM] * (2 * n) + [pl.BlockSpec(memory_space=pl.ANY)] * len(after),
        out_specs=(_SEM, _SEM, *[_HBM] * (2 * n), pl.BlockSpec(memory_space=pltpu.VMEM)),
        input_output_aliases={i: 2 + i for i in range(2 * n)},
        compiler_params=pltpu.CompilerParams(has_side_effects=_EFFECT),
    )(*[pltpu.with_memory_space_constraint(a, pltpu.HBM) for a in arrs],
      *[pltpu.with_memory_space_constraint(lax.empty(s, a.dtype), pltpu.HBM) for s, a in zip(land_shapes, arrs)],
      *after)
    return (res[0], res[1], list(res[2:2 + n]), list(res[2 + n:2 + 2 * n])), res[-1]


def exchange_wait(handle, after, name, scatter):
    send, recv, srcs, lands = handle
    n = len(srcs)
    after = list(after) if isinstance(after, (list, tuple)) else [after]

    def body(*refs):
        ins, lnd = refs[:n], refs[n:2 * n]
        send_, recv_ = refs[2 * n], refs[2 * n + 1]
        for cp in _split_copies(ins, lnd, send_, recv_, scatter, True):
            cp.wait_send()
            cp.wait_recv()

    res = pl.pallas_call(
        body, name=name,
        out_shape=(*[pltpu.HBM(a.shape, a.dtype) for a in srcs], *[pltpu.HBM(a.shape, a.dtype) for a in lands]),
        in_specs=[_HBM] * (2 * n) + [_SEM, _SEM] + [pl.BlockSpec(memory_space=pl.ANY)] * len(after),
        out_specs=tuple([_HBM] * (2 * n)),
        input_output_aliases={i: i for i in range(2 * n)},
        compiler_params=pltpu.CompilerParams(has_side_effects=_EFFECT),
    )(*srcs, *lands, send, recv, *after)
    return list(res[n:])


def _own_slot(land, own):
    return lax.dynamic_update_slice(land, own[None], (_index(_mesh_pos()),) + (0,) * own.ndim)


def cast_bf16(arrs, name):
    outs = []
    for i, a in enumerate(arrs):
        R, C = a.shape
        if R % 8 == 0:
            tr = _pick(R, (256, 128, 64, 32, 16, 8)) if R * C * 4 > (1 << 21) else R
            steps, spec = R // tr, pl.BlockSpec((tr, C), lambda i: (i, 0))
        else:
            steps, spec = C // 256, pl.BlockSpec((R, 256), lambda i: (0, i))

        def body(a_ref, o_ref):
            o_ref[...] = a_ref[...].astype(BF16)

        outs.append(pl.pallas_call(body, name=f"{name}_{i}", grid=(steps,), in_specs=[spec], out_specs=spec,
                                   out_shape=jax.ShapeDtypeStruct((R, C), BF16), compiler_params=_cparams())(a))
    return outs


def sum_slabs(parts, name):
    def fn(*a):
        outs = []
        for p in a:
            s = p[0].astype(F32)
            for k in range(1, N_DEV):
                s = s + p[k].astype(F32)
            outs.append(s)
        return tuple(outs)
    return list(small_call(fn, list(parts), [jax.ShapeDtypeStruct(p.shape[1:], F32) for p in parts], name))


def sum_parts(parts, name):
    def fn(p):
        g = p[0]
        for k in range(1, N_DEV):
            g = g + p[k]
        return (g,)
    return small_call(fn, [parts], [jax.ShapeDtypeStruct(parts.shape[1:], F32)], name)[0]


_SPLITS = np.cumsum([1536, 1536, 1536, 512, 512, 1024, 1024, 8, 8, 2048])[:-1].tolist()


def split_w_in(w):
    aq, ak, av, mq, mk, mv, mo, mi, mf, gates = jnp.split(w, _SPLITS, axis=1)
    R = w.shape[0]
    w_att = jnp.stack([aq.reshape(R, 12, 128), ak.reshape(R, 12, 128), av.reshape(R, 12, 128)], axis=2)
    gif = jnp.concatenate([mi.reshape(R, 4, 2), mf.reshape(R, 4, 2), jnp.zeros((R, 4, 124), w.dtype)], axis=2)
    w_ml = jnp.concatenate([mq.reshape(R, 4, 128), mk.reshape(R, 4, 128), mv.reshape(R, 4, 256),
                            mo.reshape(R, 4, 256), gif], axis=2)
    return w_att.reshape(R, ATT_COLS), w_ml.reshape(R, ML_COLS), gates


def merge_w_in(g_att, g_ml, g_gate):
    R = g_att.shape[0]
    a = g_att.reshape(R, 12, 3, 128)
    m = g_ml.reshape(R, 4, ML_PAIR_COLS)
    gif = m[:, :, 768:772]
    return jnp.concatenate([
        a[:, :, 0].reshape(R, 1536), a[:, :, 1].reshape(R, 1536), a[:, :, 2].reshape(R, 1536),
        m[:, :, 0:128].reshape(R, 512), m[:, :, 128:256].reshape(R, 512),
        m[:, :, 256:512].reshape(R, 1024), m[:, :, 512:768].reshape(R, 1024),
        gif[:, :, 0:2].reshape(R, 8), gif[:, :, 2:4].reshape(R, 8), g_gate], axis=1)


def _blk8(v, width=128):
    r = v.shape[0]
    return v.reshape(r, 1024 // width, width).transpose(1, 0, 2)


def _unblk8(v):
    nb, r, w = v.shape
    return v.transpose(1, 0, 2).reshape(r, nb * w)


def local_step(x, target, mods, w, small, late_w=None, early_g=None, w_in_g=None):
    late_w = late_w or (lambda after: w)
    big = {}
    early_g = early_g or (lambda g: big.update(g))
    w_in_g = w_in_g or (lambda g: big.update(w_in=merge_w_in(*g)))
    B = x.shape[0]
    T = B * SEQ
    shift1, scale1, gate1, shift2, scale2, gate2 = mods
    f2 = lambda a: a.reshape(T, a.shape[-1])
    f3 = lambda a: a.reshape(B, SEQ, a.shape[-1])

    rel_t = jnp.pad(small["rel_bias"].T, ((0, 4), (0, 0)))
    onehots = [_bucket_onehot(d) for _, d in ATT_GROUPS]
    biases = [bias_expand(rel_t, oh, f"bias_expand{g}").reshape(16, ATT_BLOCK, 2 * ATT_BLOCK)
              for g, oh in enumerate(onehots)]
    qg, kg = small["q_norm_g"], small["k_norm_g"]
    cw8 = _blk8(small["conv_w"])
    cb8 = _blk8(small["conv_b"])
    b_if = small["b_if"].reshape(2, 4, 2)
    bifp = jnp.concatenate([b_if[0], b_if[1], jnp.zeros((4, 124), F32)], axis=1).reshape(4, 1, 128)
    gn4 = small["mlstm_norm_g"].reshape(4, 1, 256)

    u = modnorm_fwd(x, small["norm1_g"], scale1, shift1, "modnorm1")
    u2d = f2(u)
    pa = f3(matmul(u2d, w["w_att"], mode="nn", name="proj_att"))
    pm = f3(matmul(u2d, w["w_ml"], mode="nn", name="proj_ml"))
    pg = matmul(u2d, w["w_gate"], mode="nn", name="proj_gate")
    os_, ls_ = [], []
    one_block = [SEQ // d == ATT_BLOCK for _, d in ATT_GROUPS]
    for g in range(3):
        o, l = (attn_fwd_classes if one_block[g] else attn_fwd)(pa, biases[g], qg, kg, g, f"attn_fwd{g}")
        os_.append(f2(o))
        ls_.append(f2(l))
    hg = mlstm_fwd(pm, cw8, cb8, bifp, gn4, "mlstm_fwd")
    w = {**w, **late_w(hg)}
    att, y_att = merge_att_out(os_, ls_, w["w_att_out"], "att_out")
    y_ml, z = ml_out_gate(f2(hg), w["w_ml_out"], pg, y_att, "ml_out")
    y, x1, u2 = out_proj_resid_modnorm(f2(z), w["w_out"], f2(x), gate1, small["norm2_g"], scale2, shift2, "out_proj")
    pre, hdn = matmul(u2, w["w_ff1"], mode="nn", name="ff1", out_dtypes=(BF16, BF16),
                      epi=lambda acc: (acc, jnp.square(jnp.maximum(acc, 0.0))))
    dx2, d_ffo, loss, d_gate2 = ff2_loss(hdn, w["w_ff2"], x1, gate2, f2(target), "ff2_loss")

    g_ff2 = matmul(hdn, d_ffo, mode="tn", name="g_ff2", out_dtypes=(BF16,))
    d_pre = matmul(d_ffo, w["w_ff2"], mode="nt", name="d_hdn", out_dtypes=(BF16,), extras=(pre,),
                   epi=lambda acc, p: (acc * (2.0 * jnp.maximum(p.astype(F32), 0.0)),))
    g_ff1 = matmul(u2, d_pre, mode="tn", name="g_ff1", out_dtypes=(BF16,))
    dx1, d_norm2, d_scale2, d_shift2, dy, d_gate1 = d_u_modnorm_bwd(
        d_pre, w["w_ff1"], None, x1, small["norm2_g"], scale2, shift2, dx2, y, gate1, "d_u2")
    g_out = matmul(f2(z), dy, mode="tn", name="g_out", out_dtypes=(BF16,))
    dpg, d_ya, d_ym = d_z_gate_bwd(dy, w["w_out"], pg, y_att, y_ml, "d_z")
    g_att_out = matmul(f2(att), f2(d_ya), mode="tn", name="g_att_out", out_dtypes=(BF16,))
    dmerge = d_att_merge_bwd(d_ya, w["w_att_out"], os_, ls_, "d_att")
    g_ml_out = matmul(f2(hg), f2(d_ym), mode="tn", name="g_ml_out", out_dtypes=(BF16,))
    d_hg = matmul(f2(d_ym), w["w_ml_out"], mode="nt", name="d_hg")
    started = early_g(dict(w_att_out=g_att_out, w_ml_out=g_ml_out, w_out=g_out, w_ff1=g_ff1, w_ff2=g_ff2))
    order = 0.0 if started is None else started[0, 0]
    dmerge = [f3(d) for d in dmerge]
    dpa = lax.empty((B, SEQ, ATT_COLS), BF16)
    d_rel = []
    d_qg = d_kg = None
    for g in range(3):
        dpa, dbias, dq_g, dk_g = (attn_bwd_classes if one_block[g] else attn_bwd)(
            pa, biases[g], qg + order, kg, dmerge[g], dmerge[3 + g], dpa, g, f"attn_bwd{g}")
        db8 = jnp.pad(dbias.reshape(4, -1), ((0, 4), (0, 0)))
        d_rel.append(bias_reduce(db8, onehots[g], f"bias_reduce{g}")[:4])
        d_qg = dq_g if d_qg is None else d_qg + dq_g
        d_kg = dk_g if d_kg is None else d_kg + dk_g
    dpm, dcw8, dcb8, dbifp, dgn4 = mlstm_bwd(pm, cw8, cb8, bifp, gn4 + order, f3(d_hg), "mlstm_bwd")
    g_w_att = matmul(u2d, f2(dpa), mode="tn", name="g_w_att", out_dtypes=(BF16,))
    g_w_ml = matmul(u2d, f2(dpm), mode="tn", name="g_w_ml", out_dtypes=(BF16,))
    g_w_gate = matmul(u2d, f2(dpg), mode="tn", name="g_w_gate", out_dtypes=(BF16,))
    started = w_in_g((g_w_att, g_w_ml, g_w_gate))
    du = matmul(f2(dpa), w["w_att"], mode="nt", name="d_u_att", after=() if started is None else (started,))
    du = matmul(f2(dpm), w["w_ml"], mode="nt", name="d_u_ml", extras=(du,), epi=lambda acc, e: (acc + e,))
    grad_x, d_norm1, d_scale1, d_shift1 = d_u_modnorm_bwd(
        f2(dpg), w["w_gate"], du, f2(x), small["norm1_g"], scale1, shift1, dx1, None, None, "d_u_gate")
    grad_x = f3(grad_x)

    d_mods = (d_shift1, d_scale1, d_gate1, d_shift2, d_scale2, d_gate2)
    dbif = dbifp.reshape(4, 128)
    small_g = dict(
        norm1_g=d_norm1, norm2_g=d_norm2,
        b_if=jnp.stack([dbif[:, 0:2].reshape(8), dbif[:, 2:4].reshape(8)]),
        conv_w=_unblk8(dcw8), conv_b=_unblk8(dcb8), q_norm_g=d_qg, k_norm_g=d_kg,
        rel_bias=jnp.concatenate(d_rel, axis=0).T,
        mlstm_norm_g=dgn4.reshape(1, 1024))
    return loss, grad_x, d_mods, big, small_g


_SMALL = (("b_ada", 6144), ("norm1_g", 1024), ("norm2_g", 1024), ("b_if", 16), ("conv_b", 1024),
          ("q_norm_g", 128), ("k_norm_g", 128), ("rel_bias", 384), ("mlstm_norm_g", 1024), ("conv_w", 4096))
_SMALL_ROWS = 120
_REPL = _SMALL[:-1]
_SMALL_SENT = _SMALL + (("loss", 1),)


def _pack(d, names, rows):
    flat = jnp.concatenate([d[k].reshape(-1) for k, _ in names])
    return jnp.pad(flat, (0, rows * 128 - flat.shape[0])).reshape(rows, 128)


def _unpack(slab, names, shapes):
    flat = slab.reshape(-1)
    out, off = {}, 0
    for k, nel in names:
        out[k] = flat[off:off + nel].reshape(shapes[k])
        off += nel
    return out


def kernel(x, c, w_ada, b_ada, norm1_g, norm2_g, w_in, b_if, conv_w, conv_b, q_norm_g, k_norm_g, rel_bias, mlstm_norm_g, w_att_out, w_ml_out, w_out, w_ff1, w_ff2, loss_target, m_w_ada, m_b_ada, m_norm1_g, m_norm2_g, m_w_in, m_b_if, m_conv_w, m_conv_b, m_q_norm_g, m_k_norm_g, m_rel_bias, m_mlstm_norm_g, m_w_att_out, m_w_ml_out, m_w_out, m_w_ff1, m_w_ff2, v_w_ada, v_b_ada, v_norm1_g, v_norm2_g, v_w_in, v_b_if, v_conv_w, v_conv_b, v_q_norm_g, v_k_norm_g, v_rel_bias, v_mlstm_norm_g, v_w_att_out, v_w_ml_out, v_w_out, v_w_ff1, v_w_ff2):
    P = dict(w_ada=w_ada, b_ada=b_ada, norm1_g=norm1_g, norm2_g=norm2_g, w_in=w_in, b_if=b_if, conv_w=conv_w,
             conv_b=conv_b, q_norm_g=q_norm_g, k_norm_g=k_norm_g, rel_bias=rel_bias, mlstm_norm_g=mlstm_norm_g,
             w_att_out=w_att_out, w_ml_out=w_ml_out, w_out=w_out, w_ff1=w_ff1, w_ff2=w_ff2)
    M = dict(w_ada=m_w_ada, b_ada=m_b_ada, norm1_g=m_norm1_g, norm2_g=m_norm2_g, w_in=m_w_in, b_if=m_b_if,
             conv_w=m_conv_w, conv_b=m_conv_b, q_norm_g=m_q_norm_g, k_norm_g=m_k_norm_g, rel_bias=m_rel_bias,
             mlstm_norm_g=m_mlstm_norm_g, w_att_out=m_w_att_out, w_ml_out=m_w_ml_out, w_out=m_w_out,
             w_ff1=m_w_ff1, w_ff2=m_w_ff2)
    V = dict(w_ada=v_w_ada, b_ada=v_b_ada, norm1_g=v_norm1_g, norm2_g=v_norm2_g, w_in=v_w_in, b_if=v_b_if,
             conv_w=v_conv_w, conv_b=v_conv_b, q_norm_g=v_q_norm_g, k_norm_g=v_k_norm_g, rel_bias=v_rel_bias,
             mlstm_norm_g=v_mlstm_norm_g, w_att_out=v_w_att_out, w_ml_out=v_w_ml_out, w_out=v_w_out,
             w_ff1=v_w_ff1, w_ff2=v_w_ff2)
    names = list(P)
    shapes = {k: P[k].shape for k in names}
    B = x.shape[0]
    me = _index(_mesh_pos())

    big_names = ("w_in", "w_att_out", "w_ml_out", "w_out", "w_ff1", "w_ff2")
    shards = cast_bf16([P[k][0] for k in big_names], "cast_w")
    (rows8,) = all_to_all([shards[0].reshape(N_DEV, D_MODEL // N_DEV, W_IN_SHARD)], "w_in_rows_exchange")
    slab = split_w_in(rows8.transpose(1, 0, 2).reshape(D_MODEL // N_DEV, D_IN))
    a8, m8, g8, c8, conv_w8 = all_gather_two_level(list(slab) + [c, conv_w[0]], "gather_w_in")
    c_all = c8.reshape(N_DEV * B, D_MODEL)
    conv_w_full = conv_w8.transpose(1, 0, 2).reshape(4, 1024)
    w = dict(w_att=a8.reshape(D_MODEL, ATT_COLS), w_ml=m8.reshape(D_MODEL, ML_COLS),
             w_gate=g8.reshape(D_MODEL, GATE_COLS))

    (silu_c,) = small_call(lambda a: (_silu(a),), [c_all], [jax.ShapeDtypeStruct(c_all.shape, F32)], "silu_c")
    b_ada_cols = lax.dynamic_slice(b_ada, (0, me * 768), (1, 768))
    ada_cols = matmul(silu_c, w_ada[0], mode="nn", name="ada", extras=(jnp.broadcast_to(b_ada_cols, (N_DEV * B, 768)),),
                      epi=lambda acc, bb: (acc + bb,))
    (ada_t,) = all_to_all([ada_cols.reshape(N_DEV, B, 768)], "ada_exchange")
    ada = ada_t.transpose(1, 0, 2).reshape(B, 6 * D_MODEL)
    mods = tuple(ada[:, i * D_MODEL:(i + 1) * D_MODEL].reshape(B, 1, D_MODEL) for i in range(6))

    late_handle, late_order = exchange_start(shards[1:], "gather_late_start", False, after=(ada_t,))

    def late_w(after):
        lands = exchange_wait(late_handle, after, "gather_late_wait", False)
        gw = dict(zip(big_names[1:], [_own_slot(l, s) for l, s in zip(lands, shards[1:])]))
        return dict(w_att_out=gw["w_att_out"].transpose(1, 0, 2).reshape(512, D_MODEL),
                    w_ml_out=gw["w_ml_out"].reshape(D_MODEL, D_MODEL), w_out=gw["w_out"].reshape(D_MODEL, D_MODEL),
                    w_ff1=gw["w_ff1"].transpose(1, 0, 2).reshape(D_MODEL, D_FF),
                    w_ff2=gw["w_ff2"].reshape(D_FF, D_MODEL))

    pending = {}

    def send_grads(key, blocks, name):
        handle, order = exchange_start(blocks, name, True)
        pending[key] = (handle, [lax.dynamic_index_in_dim(b, me, 0, keepdims=False) for b in blocks])
        return order

    def early_g(g):
        return send_grads("late", [g["w_att_out"].reshape(512, N_DEV, 128).transpose(1, 0, 2),
                                   g["w_ml_out"].reshape(N_DEV, 128, D_MODEL), g["w_out"].reshape(N_DEV, 128, D_MODEL),
                                   g["w_ff1"].reshape(D_MODEL, N_DEV, 512).transpose(1, 0, 2),
                                   g["w_ff2"].reshape(N_DEV, 512, D_MODEL)], "grad_late_start")

    def w_in_g(parts):
        return send_grads("w_in", [g.reshape(N_DEV, D_MODEL // N_DEV, g.shape[1]) for g in parts], "grad_w_in_start")

    def recv_grads(key, after, name):
        handle, own = pending[key]
        return [_own_slot(l, o) for l, o in zip(exchange_wait(handle, after, name, True), own)]

    small = dict(norm1_g=norm1_g + late_order[0, 0], norm2_g=norm2_g, b_if=b_if[0], conv_w=conv_w_full, conv_b=conv_b,
                 q_norm_g=q_norm_g, k_norm_g=k_norm_g, rel_bias=rel_bias, mlstm_norm_g=mlstm_norm_g)
    loss, grad_x, d_mods, _, small_g = local_step(x, loss_target, mods, w, small, late_w, early_g, w_in_g)

    d_ada = jnp.concatenate([d.reshape(B, D_MODEL) for d in d_mods], axis=1)
    (small_g["b_ada"],) = small_call(lambda a: (jnp.sum(a, axis=0, keepdims=True),), [d_ada],
                                     [jax.ShapeDtypeStruct((1, 6144), F32)], "g_b_ada_local")
    small_g["loss"] = loss
    slabs = sum_slabs(recv_grads("w_in", grad_x, "grad_w_in_wait"), "w_in_slab_sum")
    w_in_cols = merge_w_in(*slabs).reshape(D_MODEL // N_DEV, N_DEV, W_IN_SHARD).transpose(1, 0, 2)
    d_ada_t, small_parts = _exchange(
        [d_ada.reshape(B, N_DEV, 768).transpose(1, 0, 2), _pack(small_g, _SMALL_SENT, _SMALL_ROWS)],
        "small_exchange", [True, False])
    cols_handle, _ = exchange_start([w_in_cols], "grad_w_in_cols_start", True, after=(small_parts,))
    d_ada_cols = d_ada_t.reshape(N_DEV * B, 768)
    g_w_ada = matmul(silu_c, d_ada_cols, mode="tn", name="g_w_ada")

    recv = recv_grads("late", grad_x, "grad_late_wait")
    small_sum = sum_parts(small_parts, "small_grad_sum")
    sg = _unpack(small_sum, _SMALL_SENT, {**{k: shapes[k] for k, _ in _REPL}, "conv_w": (4, 1024), "loss": ()})
    loss = sg["loss"]

    G, Dl, NM, NV = {}, {}, {}, {}
    for k, parts in zip(big_names[1:], recv):
        g, d, nm, nv = adamw(P[k][0], parts, M[k][0], V[k][0], f"adamw_{k}", parts=True)
        G[k], Dl[k], NM[k], NV[k] = g[None], d[None], nm[None], nv[None]
    g, d, nm, nv = adamw(w_ada[0], g_w_ada, m_w_ada[0], v_w_ada[0], "adamw_w_ada")
    G["w_ada"], Dl["w_ada"], NM["w_ada"], NV["w_ada"] = g[None], d[None], nm[None], nv[None]
    g_conv = lax.dynamic_slice(sg["conv_w"], (0, me * 128), (4, 128))
    g, d, nm, nv = adamw(conv_w[0], g_conv, m_conv_w[0], v_conv_w[0], "adamw_conv_w")
    G["conv_w"], Dl["conv_w"], NM["conv_w"], NV["conv_w"] = g[None], d[None], nm[None], nv[None]
    flat2 = lambda a: a.reshape(-1, a.shape[-1])
    keys = [k for k, _ in _REPL]
    upd = adamw_many([flat2(P[k]) for k in keys], [flat2(sg[k]) for k in keys], [flat2(M[k]) for k in keys],
                     [flat2(V[k]) for k in keys], "adamw_small")
    for j, k in enumerate(keys):
        G[k] = sg[k]
        Dl[k], NM[k], NV[k] = [upd[3 * j + t].reshape(shapes[k]) for t in range(3)]

    others = [NV[k] for k in big_names[1:]] + [NV["w_ada"], NV["conv_w"], upd[-1]]
    (g_w_in,) = exchange_wait(cols_handle, others, "grad_w_in_cols_wait", True)
    g_w_in = _own_slot(g_w_in, lax.dynamic_index_in_dim(w_in_cols, me, 0, keepdims=False))
    g, d, nm, nv = adamw(w_in[0], g_w_in.reshape(D_MODEL, W_IN_SHARD), m_w_in[0], v_w_in[0], "adamw_w_in")
    G["w_in"], Dl["w_in"], NM["w_in"], NV["w_in"] = g[None], d[None], nm[None], nv[None]

    return (loss, grad_x, *[G[k] for k in names], *[Dl[k] for k in names], *[NM[k] for k in names],
            *[NV[k] for k in names])
```
